```python
import math
import jax, jax.numpy as jnp
from jax import lax
import numpy as np

D_MODEL = 1024
BATCH = 8
SEQ = 8192
DEPTH = 2

N_MIXERS = 2
N_A_LAYERS = (DEPTH + 1) // 2
N_B_LAYERS = DEPTH // 2

GDN_HEADS = 8
GDN_DK = 128
GDN_DV = 128
GDN_CONV = 4
GDN_CHUNK = 64
GDN_HK = GDN_HEADS * GDN_DK
GDN_HV = GDN_HEADS * GDN_DV
GDN_IN = 2 * GDN_HK + 2 * GDN_HV + 2 * GDN_HEADS
GDN_CONV_CH = 2 * GDN_HK + GDN_HV

DSW_GROUPS = ((128, 1), (512, 4), (2048, 16))
DSW_N_GROUPS = len(DSW_GROUPS)
DSW_HEADS = 8
DSW_DH = 64
DSW_HG = DSW_HEADS * DSW_DH
DSW_IN = 3 * DSW_N_GROUPS * DSW_HG

REL_BUCKETS = 32
REL_MAX_DIST = 2048
REL_HEADS = DSW_N_GROUPS * DSW_HEADS

FFN_HIDDEN = -(-8 * D_MODEL // (3 * 256)) * 256

RMS_EPS = 1e-6

kernel_name = "hybrid_gdn_dilated_swa_adaln"


def rms_norm(x, gain):
    xf = x.astype(jnp.float32)
    y = xf * lax.rsqrt(jnp.mean(xf * xf, axis=-1, keepdims=True) + RMS_EPS)
    return y * gain.astype(jnp.float32)


def l2_norm(x):
    xf = x.astype(jnp.float32)
    return xf * lax.rsqrt(jnp.sum(xf * xf, axis=-1, keepdims=True) + RMS_EPS)


def causal_depthwise_conv(x, w):
    K, C = w.shape
    return lax.conv_general_dilated(
        x, w[:, None, :].astype(x.dtype), window_strides=(1,),
        padding=((K - 1, 0),), dimension_numbers=("NWC", "WIO", "NWC"),
        feature_group_count=C)


def chunk_gated_delta_rule(q, k, v, g, beta):
    Bsz, S, H, dk = q.shape
    dv = v.shape[-1]
    C = GDN_CHUNK
    N = S // C
    to_chunks = lambda t: jnp.transpose(t.reshape(Bsz, N, C, H, t.shape[-1]), (0, 3, 1, 2, 4))
    q = to_chunks(q * (dk ** -0.5))
    k = to_chunks(k)
    v = to_chunks(v.astype(jnp.float32))
    beta = jnp.transpose(beta.reshape(Bsz, N, C, H), (0, 3, 1, 2))
    g = jnp.cumsum(jnp.transpose(g.reshape(Bsz, N, C, H), (0, 3, 1, 2)), axis=-1)

    causal = jnp.tril(jnp.ones((C, C), dtype=bool))
    strict = jnp.tril(jnp.ones((C, C), dtype=bool), -1)
    decay = jnp.exp(jnp.where(causal, g[..., :, None] - g[..., None, :], -jnp.inf))
    kb = k * beta[..., None]
    vb = v * beta[..., None]
    Lmat = jnp.where(strict, jnp.einsum('bhncd,bhnmd->bhncm', kb, k) * decay, 0.0)
    rhs = jnp.concatenate([vb, kb * jnp.exp(g)[..., None]], axis=-1)
    sol = lax.linalg.triangular_solve(Lmat, rhs, left_side=True, lower=True, unit_diagonal=True)
    u = sol[..., :dv]
    w = sol[..., dv:]
    qk = jnp.where(causal, jnp.einsum('bhncd,bhnmd->bhncm', q, k) * decay, 0.0)

    g_last = g[..., -1]
    q_dec = q * jnp.exp(g)[..., None]
    k_dec = k * jnp.exp(g_last[..., None] - g)[..., None]

    def step(state, inp):
        qd, a, uu, ww, kd, gl = inp
        v_new = uu - jnp.einsum('bhck,bhkv->bhcv', ww, state)
        o = jnp.einsum('bhck,bhkv->bhcv', qd, state) + jnp.einsum('bhcm,bhmv->bhcv', a, v_new)
        state = state * jnp.exp(gl)[..., None, None] + jnp.einsum('bhck,bhcv->bhkv', kd, v_new)
        return state, o

    xs = tuple(jnp.moveaxis(t, 2, 0) for t in (q_dec, qk, u, w, k_dec, g_last))
    state0 = jnp.zeros((Bsz, H, dk, dv), jnp.float32)
    _, o = lax.scan(step, state0, xs)
    return jnp.transpose(o, (1, 0, 3, 2, 4)).reshape(Bsz, S, H, dv)


def gated_deltanet_mixer(h, w_in, conv_w, a_log, dt_bias, out_gain, w_out):
    Bsz, S, _ = h.shape
    proj = h @ w_in
    qkv, z, a, b = jnp.split(proj, [GDN_CONV_CH, GDN_CONV_CH + GDN_HV,
                                    GDN_CONV_CH + GDN_HV + GDN_HEADS], axis=-1)
    qkv = jax.nn.silu(causal_depthwise_conv(qkv, conv_w))
    q, k, v = jnp.split(qkv, [GDN_HK, 2 * GDN_HK], axis=-1)
    q = l2_norm(q.reshape(Bsz, S, GDN_HEADS, GDN_DK))
    k = l2_norm(k.reshape(Bsz, S, GDN_HEADS, GDN_DK))
    v = v.reshape(Bsz, S, GDN_HEADS, GDN_DV)
    beta = jax.nn.sigmoid(b.astype(jnp.float32))
    g = -jnp.exp(a_log.astype(jnp.float32)) * jax.nn.softplus(a.astype(jnp.float32) + dt_bias.astype(jnp.float32))
    o = chunk_gated_delta_rule(q, k, v, g, beta)
    zf = z.reshape(Bsz, S, GDN_HEADS, GDN_DV).astype(jnp.float32)
    o = rms_norm(o, out_gain) * jax.nn.silu(zf)
    return o.reshape(Bsz, S, GDN_HV).astype(h.dtype) @ w_out


def t5_causal_bucket(dist):
    max_exact = REL_BUCKETS // 2
    scaled = jnp.log(jnp.maximum(dist, 1).astype(jnp.float32) / max_exact) / math.log(REL_MAX_DIST / max_exact)
    large = max_exact + (scaled * (REL_BUCKETS - max_exact)).astype(jnp.int32)
    large = jnp.minimum(large, REL_BUCKETS - 1)
    return jnp.where(dist < max_exact, dist, large)


def dilated_window_group(q, k, v, bias_table, window, dilation):
    Bsz, S, H, dh = q.shape
    span = window // dilation
    blk = span
    unit = dilation * blk
    S_pad = -(-S // unit) * unit
    nb = S_pad // unit

    def sub(t):
        t = jnp.pad(t, ((0, 0), (0, S_pad - S), (0, 0), (0, 0)))
        return jnp.transpose(t.reshape(Bsz, nb, blk, dilation, H, dh), (0, 3, 1, 2, 4, 5))

    qs, ks, vs = sub(q), sub(k), sub(v)
    shift = lambda t: jnp.concatenate([jnp.zeros_like(t[:, :, :1]), t[:, :, :-1]], axis=2)
    kb = jnp.concatenate([shift(ks), ks], axis=3)
    vb = jnp.concatenate([shift(vs), vs], axis=3)

    qi = jnp.arange(blk)[:, None] + blk
    ki = jnp.arange(2 * blk)[None, :]
    dist = qi - ki
    band = (dist >= 0) & (dist <= span)
    valid = band[None] & ((jnp.arange(nb) > 0)[:, None, None] | (ki >= blk)[None])
    bias = jnp.transpose(bias_table.astype(jnp.float32)[t5_causal_bucket(jnp.maximum(dist, 0) * dilation)], (2, 0, 1))

    logits = jnp.einsum('brnqhd,brnkhd->brnhqk', qs, kb) + bias[None, None, None]
    logits = jnp.where(valid[None, None, :, None], logits, -jnp.inf)
    m = jnp.max(logits, axis=-1, keepdims=True)
    p = jnp.exp(logits - m)
    l = jnp.sum(p, axis=-1, keepdims=True)
    o = jnp.einsum('brnhqk,brnkhd->brnqhd', p / l, vb)
    lse = (m + jnp.log(l))[..., 0]
    o = jnp.transpose(o, (0, 2, 3, 1, 4, 5)).reshape(Bsz, S_pad, H, dh)[:, :S]
    lse = jnp.transpose(lse, (0, 2, 4, 1, 3)).reshape(Bsz, S_pad, H)[:, :S]
    return o, lse


def dilated_attention_mixer(h, w_in, q_gain, k_gain, rel_bias, w_out):
    Bsz, S, _ = h.shape
    proj = (h @ w_in).reshape(Bsz, S, 3, DSW_N_GROUPS, DSW_HEADS, DSW_DH)
    q = rms_norm(proj[:, :, 0], q_gain) * (DSW_DH ** -0.5)
    k = rms_norm(proj[:, :, 1], k_gain)
    v = proj[:, :, 2].astype(jnp.float32)
    outs, lses = [], []
    for gi, (window, dilation) in enumerate(DSW_GROUPS):
        o, lse = dilated_window_group(q[:, :, gi], k[:, :, gi], v[:, :, gi],
                                      rel_bias[:, gi * DSW_HEADS:(gi + 1) * DSW_HEADS], window, dilation)
        outs.append(o)
        lses.append(lse)
    wts = jax.nn.softmax(jnp.stack(lses), axis=0)
    o = jnp.sum(wts[..., None] * jnp.stack(outs), axis=0)
    return o.reshape(Bsz, S, DSW_HG).astype(h.dtype) @ w_out


def swiglu(h, w_in, w_out):
    gate, up = jnp.split(h @ w_in, 2, axis=-1)
    return (jax.nn.silu(gate) * up) @ w_out


def _fwd_setup_inputs(seed: int = 0) -> dict:
    key = jax.random.key(seed)
    ks = jax.random.split(key, 20)
    nrm = lambda k, shape, s: jax.random.normal(k, shape, jnp.float32) * s
    D = D_MODEL
    a_init = jax.random.uniform(ks[10], (N_A_LAYERS, GDN_HEADS), jnp.float32, 1.0, 16.0)
    dt = jnp.exp(jax.random.uniform(ks[11], (N_A_LAYERS, GDN_HEADS), jnp.float32,
                                    math.log(1e-3), math.log(1e-1)))
    return {
        "x": nrm(ks[0], (BATCH, SEQ, D), 1.0),
        "c": nrm(ks[1], (BATCH, D), 1.0),
        "w_ada": nrm(ks[2], (DEPTH, D, 6 * D), D ** -0.5),
        "b_ada": nrm(ks[3], (DEPTH, 6 * D), 0.02),
        "norm_mix": 1.0 + nrm(ks[4], (DEPTH, D), 0.02),
        "norm_ffn": 1.0 + nrm(ks[5], (DEPTH, D), 0.02),
        "w_ffn_in": nrm(ks[6], (DEPTH, D, 2 * FFN_HIDDEN), D ** -0.5),
        "w_ffn_out": nrm(ks[7], (DEPTH, FFN_HIDDEN, D), FFN_HIDDEN ** -0.5),
        "gdn_w_in": nrm(ks[8], (N_A_LAYERS, D, GDN_IN), D ** -0.5),
        "gdn_conv": nrm(ks[9], (N_A_LAYERS, GDN_CONV, GDN_CONV_CH), GDN_CONV ** -0.5),
        "gdn_a_log": jnp.log(a_init),
        "gdn_dt_bias": dt + jnp.log(-jnp.expm1(-dt)),
        "gdn_out_norm": 1.0 + nrm(ks[12], (N_A_LAYERS, GDN_DV), 0.02),
        "gdn_w_out": nrm(ks[13], (N_A_LAYERS, GDN_HV, D), GDN_HV ** -0.5),
        "dsw_w_in": nrm(ks[14], (N_B_LAYERS, D, DSW_IN), D ** -0.5),
        "dsw_q_norm": 1.0 + nrm(ks[15], (N_B_LAYERS, DSW_DH), 0.02),
        "dsw_k_norm": 1.0 + nrm(ks[16], (N_B_LAYERS, DSW_DH), 0.02),
        "dsw_w_out": nrm(ks[17], (N_B_LAYERS, DSW_HG, D), DSW_HG ** -0.5),
        "rel_bias": nrm(ks[18], (REL_BUCKETS, REL_HEADS), 0.5),
    }


def _fwd_reference(x, c, w_ada, b_ada, norm_mix, norm_ffn, w_ffn_in, w_ffn_out,
              gdn_w_in, gdn_conv, gdn_a_log, gdn_dt_bias, gdn_out_norm, gdn_w_out,
              dsw_w_in, dsw_q_norm, dsw_k_norm, dsw_w_out, rel_bias):
    cond = jax.nn.silu(c.astype(jnp.float32))
    for layer in range(DEPTH):
        mod = (cond @ w_ada[layer].astype(jnp.float32) + b_ada[layer].astype(jnp.float32)).astype(x.dtype)
        sh1, sc1, g1, sh2, sc2, g2 = jnp.split(mod[:, None, :], 6, axis=-1)

        h = (rms_norm(x, norm_mix[layer]) * (1.0 + sc1) + sh1).astype(x.dtype)
        j = layer // N_MIXERS
        if layer % N_MIXERS == 0:
            y = gated_deltanet_mixer(h, gdn_w_in[j], gdn_conv[j], gdn_a_log[j], gdn_dt_bias[j],
                                     gdn_out_norm[j], gdn_w_out[j])
        else:
            y = dilated_attention_mixer(h, dsw_w_in[j], dsw_q_norm[j], dsw_k_norm[j],
                                        rel_bias, dsw_w_out[j])
        x = x + g1 * y

        h = (rms_norm(x, norm_ffn[layer]) * (1.0 + sc2) + sh2).astype(x.dtype)
        x = x + g2 * swiglu(h, w_ffn_in[layer], w_ffn_out[layer])
    return x


import jax as _jax
import jax.numpy as _jnp

TWIN_FORMAT = 'train_step'
FWD_PARAMS = ['x', 'c', 'w_ada', 'b_ada', 'norm_mix', 'norm_ffn', 'w_ffn_in', 'w_ffn_out', 'gdn_w_in', 'gdn_conv', 'gdn_a_log', 'gdn_dt_bias', 'gdn_out_norm', 'gdn_w_out', 'dsw_w_in', 'dsw_q_norm', 'dsw_k_norm', 'dsw_w_out', 'rel_bias']
TWIN_WEIGHTS = ['w_ada', 'b_ada', 'norm_mix', 'norm_ffn', 'w_ffn_in', 'w_ffn_out', 'gdn_w_in', 'gdn_conv', 'gdn_a_log', 'gdn_dt_bias', 'gdn_out_norm', 'gdn_w_out', 'dsw_w_in', 'dsw_q_norm', 'dsw_k_norm', 'dsw_w_out', 'rel_bias']
TWIN_DIFF_INPUT = 'x'
TWIN_INPUTS = ['x', 'c', 'w_ada', 'b_ada', 'norm_mix', 'norm_ffn', 'w_ffn_in', 'w_ffn_out', 'gdn_w_in', 'gdn_conv', 'gdn_a_log', 'gdn_dt_bias', 'gdn_out_norm', 'gdn_w_out', 'dsw_w_in', 'dsw_q_norm', 'dsw_k_norm', 'dsw_w_out', 'rel_bias', 'loss_target', 'm_w_ada', 'm_b_ada', 'm_norm_mix', 'm_norm_ffn', 'm_w_ffn_in', 'm_w_ffn_out', 'm_gdn_w_in', 'm_gdn_conv', 'm_gdn_a_log', 'm_gdn_dt_bias', 'm_gdn_out_norm', 'm_gdn_w_out', 'm_dsw_w_in', 'm_dsw_q_norm', 'm_dsw_k_norm', 'm_dsw_w_out', 'm_rel_bias', 'v_w_ada', 'v_b_ada', 'v_norm_mix', 'v_norm_ffn', 'v_w_ffn_in', 'v_w_ffn_out', 'v_gdn_w_in', 'v_gdn_conv', 'v_gdn_a_log', 'v_gdn_dt_bias', 'v_gdn_out_norm', 'v_gdn_w_out', 'v_dsw_w_in', 'v_dsw_q_norm', 'v_dsw_k_norm', 'v_dsw_w_out', 'v_rel_bias']
TWIN_OUTPUTS = ['loss', 'grad_x', 'grad_w_ada', 'grad_b_ada', 'grad_norm_mix', 'grad_norm_ffn', 'grad_w_ffn_in', 'grad_w_ffn_out', 'grad_gdn_w_in', 'grad_gdn_conv', 'grad_gdn_a_log', 'grad_gdn_dt_bias', 'grad_gdn_out_norm', 'grad_gdn_w_out', 'grad_dsw_w_in', 'grad_dsw_q_norm', 'grad_dsw_k_norm', 'grad_dsw_w_out', 'grad_rel_bias', 'delta_w_ada', 'delta_b_ada', 'delta_norm_mix', 'delta_norm_ffn', 'delta_w_ffn_in', 'delta_w_ffn_out', 'delta_gdn_w_in', 'delta_gdn_conv', 'delta_gdn_a_log', 'delta_gdn_dt_bias', 'delta_gdn_out_norm', 'delta_gdn_w_out', 'delta_dsw_w_in', 'delta_dsw_q_norm', 'delta_dsw_k_norm', 'delta_dsw_w_out', 'delta_rel_bias', 'new_m_w_ada', 'new_m_b_ada', 'new_m_norm_mix', 'new_m_norm_ffn', 'new_m_w_ffn_in', 'new_m_w_ffn_out', 'new_m_gdn_w_in', 'new_m_gdn_conv', 'new_m_gdn_a_log', 'new_m_gdn_dt_bias', 'new_m_gdn_out_norm', 'new_m_gdn_w_out', 'new_m_dsw_w_in', 'new_m_dsw_q_norm', 'new_m_dsw_k_norm', 'new_m_dsw_w_out', 'new_m_rel_bias', 'new_v_w_ada', 'new_v_b_ada', 'new_v_norm_mix', 'new_v_norm_ffn', 'new_v_w_ffn_in', 'new_v_w_ffn_out', 'new_v_gdn_w_in', 'new_v_gdn_conv', 'new_v_gdn_a_log', 'new_v_gdn_dt_bias', 'new_v_gdn_out_norm', 'new_v_gdn_w_out', 'new_v_dsw_w_in', 'new_v_dsw_q_norm', 'new_v_dsw_k_norm', 'new_v_dsw_w_out', 'new_v_rel_bias']
TWIN_LEAF_KINDS = {'loss': 'loss', 'grad_x': 'grad_x', 'grad_w_ada': 'grad_w', 'grad_b_ada': 'grad_w', 'grad_norm_mix': 'grad_w', 'grad_norm_ffn': 'grad_w', 'grad_w_ffn_in': 'grad_w', 'grad_w_ffn_out': 'grad_w', 'grad_gdn_w_in': 'grad_w', 'grad_gdn_conv': 'grad_w', 'grad_gdn_a_log': 'grad_w', 'grad_gdn_dt_bias': 'grad_w', 'grad_gdn_out_norm': 'grad_w', 'grad_gdn_w_out': 'grad_w', 'grad_dsw_w_in': 'grad_w', 'grad_dsw_q_norm': 'grad_w', 'grad_dsw_k_norm': 'grad_w', 'grad_dsw_w_out': 'grad_w', 'grad_rel_bias': 'grad_w', 'delta_w_ada': 'delta_w', 'delta_b_ada': 'delta_w', 'delta_norm_mix': 'delta_w', 'delta_norm_ffn': 'delta_w', 'delta_w_ffn_in': 'delta_w', 'delta_w_ffn_out': 'delta_w', 'delta_gdn_w_in': 'delta_w', 'delta_gdn_conv': 'delta_w', 'delta_gdn_a_log': 'delta_w', 'delta_gdn_dt_bias': 'delta_w', 'delta_gdn_out_norm': 'delta_w', 'delta_gdn_w_out': 'delta_w', 'delta_dsw_w_in': 'delta_w', 'delta_dsw_q_norm': 'delta_w', 'delta_dsw_k_norm': 'delta_w', 'delta_dsw_w_out': 'delta_w', 'delta_rel_bias': 'delta_w', 'new_m_w_ada': 'new_m', 'new_m_b_ada': 'new_m', 'new_m_norm_mix': 'new_m', 'new_m_norm_ffn': 'new_m', 'new_m_w_ffn_in': 'new_m', 'new_m_w_ffn_out': 'new_m', 'new_m_gdn_w_in': 'new_m', 'new_m_gdn_conv': 'new_m', 'new_m_gdn_a_log': 'new_m', 'new_m_gdn_dt_bias': 'new_m', 'new_m_gdn_out_norm': 'new_m', 'new_m_gdn_w_out': 'new_m', 'new_m_dsw_w_in': 'new_m', 'new_m_dsw_q_norm': 'new_m', 'new_m_dsw_k_norm': 'new_m', 'new_m_dsw_w_out': 'new_m', 'new_m_rel_bias': 'new_m', 'new_v_w_ada': 'new_v', 'new_v_b_ada': 'new_v', 'new_v_norm_mix': 'new_v', 'new_v_norm_ffn': 'new_v', 'new_v_w_ffn_in': 'new_v', 'new_v_w_ffn_out': 'new_v', 'new_v_gdn_w_in': 'new_v', 'new_v_gdn_conv': 'new_v', 'new_v_gdn_a_log': 'new_v', 'new_v_gdn_dt_bias': 'new_v', 'new_v_gdn_out_norm': 'new_v', 'new_v_gdn_w_out': 'new_v', 'new_v_dsw_w_in': 'new_v', 'new_v_dsw_q_norm': 'new_v', 'new_v_dsw_k_norm': 'new_v', 'new_v_dsw_w_out': 'new_v', 'new_v_rel_bias': 'new_v'}


def _forward(args):
    return _fwd_reference(*[args[k] for k in FWD_PARAMS])


def _output_shape():
    def fwd():
        inp = _fwd_setup_inputs(0)
        return _fwd_reference(*[inp[k] for k in FWD_PARAMS])
    out = _jax.eval_shape(fwd)
    return out.shape, out.dtype

N_MICROBATCH = 1
ADAM_LR = 0.001
ADAM_B1 = 0.9
ADAM_B2 = 0.999
ADAM_EPS = 1e-08
ADAM_WD = 0.01
ADAM_STEP = 10
PER_EXAMPLE_BATCH_AXIS = {'x': 0, 'c': 0, 'loss_target': 0}
SHARED_INPUTS = []
_WEIGHT_DTYPES = {'w_ada': _jnp.float32, 'b_ada': _jnp.float32, 'norm_mix': _jnp.float32, 'norm_ffn': _jnp.float32, 'w_ffn_in': _jnp.float32, 'w_ffn_out': _jnp.float32, 'gdn_w_in': _jnp.float32, 'gdn_conv': _jnp.float32, 'gdn_a_log': _jnp.float32, 'gdn_dt_bias': _jnp.float32, 'gdn_out_norm': _jnp.float32, 'gdn_w_out': _jnp.float32, 'dsw_w_in': _jnp.float32, 'dsw_q_norm': _jnp.float32, 'dsw_k_norm': _jnp.float32, 'dsw_w_out': _jnp.float32, 'rel_bias': _jnp.float32}
MOMENT_SCALE = {'w_ada': 8.186910e+00, 'b_ada': 2.092476e+01, 'norm_mix': 1.190650e+01, 'norm_ffn': 5.285073e+01, 'w_ffn_in': 2.279129e+00, 'w_ffn_out': 2.477375e+00, 'gdn_w_in': 4.269402e+00, 'gdn_conv': 4.910897e+00, 'gdn_a_log': 3.330153e+01, 'gdn_dt_bias': 3.094098e+01, 'gdn_out_norm': 1.401102e+02, 'gdn_w_out': 4.156726e+00, 'dsw_w_in': 1.727620e+00, 'dsw_q_norm': 8.266418e+00, 'dsw_k_norm': 8.266793e+00, 'dsw_w_out': 3.198902e+00, 'rel_bias': 1.729041e+00}


def _to_microbatches(a, axis):
    t = _jnp.moveaxis(a, axis, 0)
    t = t.reshape((N_MICROBATCH, t.shape[0] // N_MICROBATCH) + t.shape[1:])
    return _jnp.moveaxis(t, 1, axis + 1)


def setup_inputs(seed: int = 0) -> dict:
    inp = _fwd_setup_inputs(seed)
    key = _jax.random.fold_in(_jax.random.key(seed), 7919)
    shape, _ = _output_shape()
    out = dict(inp)
    out["loss_target"] = _jax.random.normal(_jax.random.fold_in(key, 0), shape, _jnp.float32)
    for i, name in enumerate(TWIN_WEIGHTS):
        w = inp[name].astype(_jnp.float32)
        if MOMENT_SCALE is None:
            s = _jnp.sqrt(_jnp.mean(_jnp.square(w)) + 1e-30)
        else:
            s = MOMENT_SCALE[name]
        km, kv = _jax.random.split(_jax.random.fold_in(key, i + 1))
        out[name] = w
        out["m_" + name] = s * _jax.random.normal(km, w.shape, _jnp.float32)
        out["v_" + name] = (s * s) * _jax.random.uniform(kv, w.shape, _jnp.float32, 0.5, 1.5)
    if N_MICROBATCH > 1:
        for name, axis in PER_EXAMPLE_BATCH_AXIS.items():
            out[name] = _to_microbatches(out[name], axis)
    return {'x': out['x'], 'c': out['c'], 'w_ada': out['w_ada'], 'b_ada': out['b_ada'], 'norm_mix': out['norm_mix'], 'norm_ffn': out['norm_ffn'], 'w_ffn_in': out['w_ffn_in'], 'w_ffn_out': out['w_ffn_out'], 'gdn_w_in': out['gdn_w_in'], 'gdn_conv': out['gdn_conv'], 'gdn_a_log': out['gdn_a_log'], 'gdn_dt_bias': out['gdn_dt_bias'], 'gdn_out_norm': out['gdn_out_norm'], 'gdn_w_out': out['gdn_w_out'], 'dsw_w_in': out['dsw_w_in'], 'dsw_q_norm': out['dsw_q_norm'], 'dsw_k_norm': out['dsw_k_norm'], 'dsw_w_out': out['dsw_w_out'], 'rel_bias': out['rel_bias'], 'loss_target': out['loss_target'], 'm_w_ada': out['m_w_ada'], 'm_b_ada': out['m_b_ada'], 'm_norm_mix': out['m_norm_mix'], 'm_norm_ffn': out['m_norm_ffn'], 'm_w_ffn_in': out['m_w_ffn_in'], 'm_w_ffn_out': out['m_w_ffn_out'], 'm_gdn_w_in': out['m_gdn_w_in'], 'm_gdn_conv': out['m_gdn_conv'], 'm_gdn_a_log': out['m_gdn_a_log'], 'm_gdn_dt_bias': out['m_gdn_dt_bias'], 'm_gdn_out_norm': out['m_gdn_out_norm'], 'm_gdn_w_out': out['m_gdn_w_out'], 'm_dsw_w_in': out['m_dsw_w_in'], 'm_dsw_q_norm': out['m_dsw_q_norm'], 'm_dsw_k_norm': out['m_dsw_k_norm'], 'm_dsw_w_out': out['m_dsw_w_out'], 'm_rel_bias': out['m_rel_bias'], 'v_w_ada': out['v_w_ada'], 'v_b_ada': out['v_b_ada'], 'v_norm_mix': out['v_norm_mix'], 'v_norm_ffn': out['v_norm_ffn'], 'v_w_ffn_in': out['v_w_ffn_in'], 'v_w_ffn_out': out['v_w_ffn_out'], 'v_gdn_w_in': out['v_gdn_w_in'], 'v_gdn_conv': out['v_gdn_conv'], 'v_gdn_a_log': out['v_gdn_a_log'], 'v_gdn_dt_bias': out['v_gdn_dt_bias'], 'v_gdn_out_norm': out['v_gdn_out_norm'], 'v_gdn_w_out': out['v_gdn_w_out'], 'v_dsw_w_in': out['v_dsw_w_in'], 'v_dsw_q_norm': out['v_dsw_q_norm'], 'v_dsw_k_norm': out['v_dsw_k_norm'], 'v_dsw_w_out': out['v_dsw_w_out'], 'v_rel_bias': out['v_rel_bias']}


def _loss(weights, diff, rest, loss_target):
    with _jax.named_scope("forward"):
        args = {**rest, TWIN_DIFF_INPUT: diff, **{k: w.astype(_WEIGHT_DTYPES[k]) for k, w in weights.items()}}
        y = _forward(args)
    with _jax.named_scope("loss_head"):
        err = _jnp.square(y.astype(_jnp.float32) - loss_target)
        return 0.5 * _jnp.sum(_jnp.mean(err, axis=-1)) if err.ndim else 0.5 * err


def _adamw(w, g, m, v):
    m = ADAM_B1 * m + (1.0 - ADAM_B1) * g
    v = ADAM_B2 * v + (1.0 - ADAM_B2) * _jnp.square(g)
    m_hat = m / (1.0 - ADAM_B1 ** ADAM_STEP)
    v_hat = v / (1.0 - ADAM_B2 ** ADAM_STEP)
    delta = -ADAM_LR * (m_hat / (_jnp.sqrt(v_hat) + ADAM_EPS) + ADAM_WD * w)
    return delta, m, v


def reference(x, c, w_ada, b_ada, norm_mix, norm_ffn, w_ffn_in, w_ffn_out, gdn_w_in, gdn_conv, gdn_a_log, gdn_dt_bias, gdn_out_norm, gdn_w_out, dsw_w_in, dsw_q_norm, dsw_k_norm, dsw_w_out, rel_bias, loss_target, m_w_ada, m_b_ada, m_norm_mix, m_norm_ffn, m_w_ffn_in, m_w_ffn_out, m_gdn_w_in, m_gdn_conv, m_gdn_a_log, m_gdn_dt_bias, m_gdn_out_norm, m_gdn_w_out, m_dsw_w_in, m_dsw_q_norm, m_dsw_k_norm, m_dsw_w_out, m_rel_bias, v_w_ada, v_b_ada, v_norm_mix, v_norm_ffn, v_w_ffn_in, v_w_ffn_out, v_gdn_w_in, v_gdn_conv, v_gdn_a_log, v_gdn_dt_bias, v_gdn_out_norm, v_gdn_w_out, v_dsw_w_in, v_dsw_q_norm, v_dsw_k_norm, v_dsw_w_out, v_rel_bias):
    given = dict(x=x, c=c, w_ada=w_ada, b_ada=b_ada, norm_mix=norm_mix, norm_ffn=norm_ffn, w_ffn_in=w_ffn_in, w_ffn_out=w_ffn_out, gdn_w_in=gdn_w_in, gdn_conv=gdn_conv, gdn_a_log=gdn_a_log, gdn_dt_bias=gdn_dt_bias, gdn_out_norm=gdn_out_norm, gdn_w_out=gdn_w_out, dsw_w_in=dsw_w_in, dsw_q_norm=dsw_q_norm, dsw_k_norm=dsw_k_norm, dsw_w_out=dsw_w_out, rel_bias=rel_bias, loss_target=loss_target, m_w_ada=m_w_ada, m_b_ada=m_b_ada, m_norm_mix=m_norm_mix, m_norm_ffn=m_norm_ffn, m_w_ffn_in=m_w_ffn_in, m_w_ffn_out=m_w_ffn_out, m_gdn_w_in=m_gdn_w_in, m_gdn_conv=m_gdn_conv, m_gdn_a_log=m_gdn_a_log, m_gdn_dt_bias=m_gdn_dt_bias, m_gdn_out_norm=m_gdn_out_norm, m_gdn_w_out=m_gdn_w_out, m_dsw_w_in=m_dsw_w_in, m_dsw_q_norm=m_dsw_q_norm, m_dsw_k_norm=m_dsw_k_norm, m_dsw_w_out=m_dsw_w_out, m_rel_bias=m_rel_bias, v_w_ada=v_w_ada, v_b_ada=v_b_ada, v_norm_mix=v_norm_mix, v_norm_ffn=v_norm_ffn, v_w_ffn_in=v_w_ffn_in, v_w_ffn_out=v_w_ffn_out, v_gdn_w_in=v_gdn_w_in, v_gdn_conv=v_gdn_conv, v_gdn_a_log=v_gdn_a_log, v_gdn_dt_bias=v_gdn_dt_bias, v_gdn_out_norm=v_gdn_out_norm, v_gdn_w_out=v_gdn_w_out, v_dsw_w_in=v_dsw_w_in, v_dsw_q_norm=v_dsw_q_norm, v_dsw_k_norm=v_dsw_k_norm, v_dsw_w_out=v_dsw_w_out, v_rel_bias=v_rel_bias)
    weights = {n: given[n] for n in TWIN_WEIGHTS}
    shared = {n: given[n] for n in SHARED_INPUTS}
    per_example = {n: given[n] for n in ['x', 'c']}
    grad_fn = _jax.value_and_grad(_loss, argnums=(0, 1))

    def one_microbatch(ex, loss_target):
        ex = dict(ex)
        diff = ex.pop(TWIN_DIFF_INPUT)
        return grad_fn(weights, diff, {**shared, **ex}, loss_target)

    if N_MICROBATCH == 1:
        loss, (grad_w, grad_x) = one_microbatch(per_example, given["loss_target"])
    else:
        def body(carry, xs):
            loss_sum, grad_sum = carry
            l_k, (gw_k, gx_k) = one_microbatch(xs[0], xs[1])
            with _jax.named_scope("update"):
                return (loss_sum + l_k, _jax.tree.map(_jnp.add, grad_sum, gw_k)), gx_k

        init = (_jnp.zeros((), _jnp.float32), _jax.tree.map(_jnp.zeros_like, weights))
        (loss, grad_w), grad_x = _jax.lax.scan(body, init, (per_example, given["loss_target"]))
    with _jax.named_scope("update"):
        delta_w, new_m, new_v = {}, {}, {}
        for n in TWIN_WEIGHTS:
            delta_w[n], new_m[n], new_v[n] = _adamw(weights[n], grad_w[n], given["m_" + n], given["v_" + n])
    return (loss, grad_x, *[grad_w[n] for n in TWIN_WEIGHTS], *[delta_w[n] for n in TWIN_WEIGHTS],
            *[new_m[n] for n in TWIN_WEIGHTS], *[new_v[n] for n in TWIN_WEIGHTS])
```

```python
import functools
import math

import numpy as np
import jax
import jax.numpy as jnp
from jax import lax
from jax.experimental import pallas as pl
from jax.experimental.pallas import tpu as pltpu

F32 = jnp.float32
BF16 = jnp.bfloat16
SDS = jax.ShapeDtypeStruct
MESH = pl.DeviceIdType.MESH
ANY = pl.BlockSpec(memory_space=pl.ANY)

D = 1024
EPS = 1e-6
LANES = 128
GDN_H = 8
GDN_DK = 128
GDN_C = 64
DSW_GROUPS = ((128, 1), (512, 4), (2048, 16))
DSW_SPAN = 128
DSW_DH = 64
DSW_HG = 512
REL_BUCKETS = 32
REL_MAX_DIST = 2048
FFN = 2816
N_SHARD = 4
N_DEV = 8
VMEM_LIMIT = 48 * 1024 * 1024
NEG = -1e30

ADAM_LR, ADAM_B1, ADAM_B2, ADAM_EPS, ADAM_WD, ADAM_STEP = 0.001, 0.9, 0.999, 1e-08, 0.01, 10


def _cp(n_axes):
    return pltpu.CompilerParams(dimension_semantics=("arbitrary",) * n_axes, vmem_limit_bytes=VMEM_LIMIT)


def _blk(dim, cap):
    if dim <= cap:
        return dim
    best = None
    for b in range(LANES, cap + 1, LANES):
        if dim % b == 0:
            best = b
    assert best is not None, (dim, cap)
    return best


def matmul(a, b, mode, out_dtype, name, cap_m=512, cap_n=512, cap_k=512):
    if mode == "nn":
        (M, K), (K2, N) = a.shape, b.shape
    elif mode == "nt":
        (M, K), (N, K2) = a.shape, b.shape
    else:
        (K, M), (K2, N) = a.shape, b.shape
    assert K == K2, (a.shape, b.shape, mode)
    bm, bn, bk = _blk(M, cap_m), _blk(N, cap_n), _blk(K, cap_k)
    nk = K // bk
    dims = {"nn": ((1,), (0,)), "nt": ((1,), (1,)), "tn": ((0,), (0,))}[mode]

    def body(a_ref, b_ref, o_ref, acc_ref):
        k = pl.program_id(2)

        @pl.when(k == 0)
        def _():
            acc_ref[...] = jnp.zeros_like(acc_ref)

        acc_ref[...] += lax.dot_general(a_ref[...].astype(BF16), b_ref[...].astype(BF16), (dims, ((), ())),
                                        preferred_element_type=F32)

        @pl.when(k == nk - 1)
        def _():
            o_ref[...] = acc_ref[...].astype(o_ref.dtype)

    a_spec = pl.BlockSpec((bk, bm), lambda i, j, k: (k, i)) if mode == "tn" else pl.BlockSpec((bm, bk), lambda i, j, k: (i, k))
    b_spec = pl.BlockSpec((bn, bk), lambda i, j, k: (j, k)) if mode == "nt" else pl.BlockSpec((bk, bn), lambda i, j, k: (k, j))
    return pl.pallas_call(
        body, name=name, grid=(M // bm, N // bn, nk),
        in_specs=[a_spec, b_spec], out_specs=pl.BlockSpec((bm, bn), lambda i, j, k: (i, j)),
        out_shape=SDS((M, N), out_dtype), scratch_shapes=[pltpu.VMEM((bm, bn), F32)],
        compiler_params=_cp(3),
    )(a, b)


class Row:
    def __init__(self, arr, bshape, imap, splits=None, diff=True, acc=False, gdtype=F32, gshape=None, gbshape=None, gimap=None):
        self.arr, self.bshape, self.imap = arr, tuple(bshape), imap
        self.splits = splits
        self.diff, self.acc, self.gdtype = diff, acc, gdtype
        self.gshape = tuple(arr.shape) if gshape is None else tuple(gshape)
        self.gbshape = self.bshape if gbshape is None else tuple(gbshape)
        self.gimap = imap if gimap is None else gimap

    def gspec(self):
        return pl.BlockSpec(self.gbshape, self.gimap)

    def spec(self):
        return pl.BlockSpec(self.bshape, self.imap)

    def pieces(self, val):
        if self.splits is None:
            return [val]
        out, o = [], 0
        for w in self.splits:
            out.append(val[..., o:o + w])
            o += w
        return out

    def n_pieces(self):
        return 1 if self.splits is None else len(self.splits)


class Out:
    def __init__(self, shape, dtype, bshape, imap, splits=None):
        self.shape, self.dtype, self.bshape, self.imap, self.splits = tuple(shape), dtype, tuple(bshape), imap, splits

    def n_pieces(self):
        return 1 if self.splits is None else len(self.splits)


def _store_pieces(ref, splits, vals):
    if splits is None:
        ref[...] = vals[0].astype(ref.dtype)
        return
    o = 0
    for w, v in zip(splits, vals):
        ref[..., o:o + w] = v.astype(ref.dtype)
        o += w


def rowwise(fn, rows, params, outs, grid, name):
    nr, npar = len(rows), len(params)

    def body(*refs):
        ids = tuple(pl.program_id(a) for a in range(len(grid)))
        vals = []
        for r, ref in zip(rows, refs[:nr]):
            vals += [p.astype(F32) for p in r.pieces(ref[...])]
        pvals = [ref[...].astype(F32) for ref in refs[nr:nr + npar]]
        res = list(fn(ids, *vals, *pvals))
        o = 0
        for spec, ref in zip(outs, refs[nr + npar:]):
            n = spec.n_pieces()
            _store_pieces(ref, spec.splits, res[o:o + n])
            o += n

    nz = len(grid)
    pspecs = [pl.BlockSpec(p.shape, (lambda *ids, _n=p.ndim: (0,) * _n)) for p in params]
    res = pl.pallas_call(
        body, name=name, grid=grid,
        in_specs=[r.spec() for r in rows] + pspecs,
        out_specs=[pl.BlockSpec(o.bshape, o.imap) for o in outs],
        out_shape=[SDS(o.shape, o.dtype) for o in outs],
        compiler_params=_cp(nz),
    )(*[r.arr for r in rows], *params)
    return list(res)


def rowwise_bwd(fn, rows, params, cots, grid, name):
    nr, npar, nc = len(rows), len(params), len(cots)
    drows = [r for r in rows if r.diff]
    nz = len(grid)

    def body(*refs):
        ids = tuple(pl.program_id(a) for a in range(nz))
        row_refs, par_refs = refs[:nr], refs[nr:nr + npar]
        cot_refs = refs[nr + npar:nr + npar + nc]
        drow_refs = refs[nr + npar + nc:nr + npar + nc + len(drows)]
        dpar_refs = refs[nr + npar + nc + len(drows):]
        pieces, is_diff = [], []
        for r, ref in zip(rows, row_refs):
            ps = [p.astype(F32) for p in r.pieces(ref[...])]
            pieces += ps
            is_diff += [r.diff] * len(ps)
        pvals = [ref[...].astype(F32) for ref in par_refs]
        dvals = [p for p, dflag in zip(pieces, is_diff) if dflag]
        nd = len(dvals)

        def f(*args):
            it = iter(args[:nd])
            full = [next(it) if dflag else p for p, dflag in zip(pieces, is_diff)]
            return tuple(fn(ids, *full, *args[nd:]))

        _, vjp = jax.vjp(f, *dvals, *pvals)
        cvals = []
        for c, ref in zip(cots, cot_refs):
            cvals += [p.astype(F32) for p in c.pieces(ref[...])]
        g = vjp(tuple(cvals))
        o = 0
        first_inner = ids[-1] == 0
        for r, ref in zip(drows, drow_refs):
            n = r.n_pieces()
            gs = g[o:o + n]
            o += n
            if r.acc:
                @pl.when(first_inner)
                def _(ref=ref):
                    ref[...] = jnp.zeros_like(ref)
                if r.splits is None:
                    ref[...] += gs[0].astype(ref.dtype)
                else:
                    oo = 0
                    for w, v in zip(r.splits, gs):
                        ref[..., oo:oo + w] += v.astype(ref.dtype)
                        oo += w
            else:
                _store_pieces(ref, r.splits, gs)
        first = functools.reduce(jnp.logical_and, [i == 0 for i in ids])
        for ref, gp in zip(dpar_refs, g[nd:]):
            @pl.when(first)
            def _(ref=ref):
                ref[...] = jnp.zeros_like(ref)
            ref[...] += gp

    pspecs = [pl.BlockSpec(p.shape, (lambda *ids, _n=p.ndim: (0,) * _n)) for p in params]
    res = pl.pallas_call(
        body, name=name, grid=grid,
        in_specs=[r.spec() for r in rows] + pspecs + [c.spec() for c in cots],
        out_specs=[r.gspec() for r in drows] + pspecs,
        out_shape=[SDS(r.gshape, r.gdtype) for r in drows] + [SDS(p.shape, F32) for p in params],
        compiler_params=_cp(nz),
    )(*[r.arr for r in rows], *params, *[c.arr for c in cots])
    res = list(res)
    return res[:len(drows)], res[len(drows):]


def _silu(x):
    return x / (1.0 + jnp.exp(-x))


def _normmod(x, gain, sc, sh):
    inv = lax.rsqrt(jnp.mean(x * x, axis=-1, keepdims=True) + EPS)
    return x * inv * gain * (1.0 + sc) + sh


def f_first(ids, x, gain, sc, sh):
    return x, _normmod(x, gain, sc, sh)


def f_resid_norm(ids, x, y, g, gain, sc, sh):
    xn = x + g * y
    return xn, _normmod(xn, gain, sc, sh)


def f_swiglu(ids, gate, up):
    return (_silu(gate) * up,)


def f_loss(ids, x, y, tgt, g):
    out = x + g * y
    e = out - tgt
    part = 0.5 * jnp.sum(e * e, axis=0, keepdims=True) * (1.0 / D)
    return (part,)


def _softplus(x):
    return jnp.maximum(x, 0.0) + jnp.log(1.0 + jnp.exp(-jnp.abs(x)))


def _chunk_tril(T):
    r = lax.broadcasted_iota(jnp.int32, (T, T), 0)
    c = lax.broadcasted_iota(jnp.int32, (T, T), 1)
    return jnp.where((r // GDN_C == c // GDN_C) & (c <= r), 1.0, 0.0).astype(F32)


def _dot_hi(a, b, dims=((1,), (0,))):
    return lax.dot_general(a, b, (dims, ((), ())), precision=lax.Precision.HIGHEST, preferred_element_type=F32)


def f_gdn_gates(ids, ab, alog, dtb):
    h = ids[1]
    T = ab.shape[0]
    g = -jnp.exp(alog) * _softplus(ab + dtb)
    beta = 1.0 / (1.0 + jnp.exp(-ab))
    row = lax.broadcasted_iota(jnp.int32, (LANES, LANES), 0)
    sel_a = jnp.where(row == h, 1.0, 0.0).astype(F32)
    sel_b = jnp.where(row == h + GDN_H, 1.0, 0.0).astype(F32)
    gh = _dot_hi(g, sel_a)
    bh = _dot_hi(beta, sel_b)
    gc = _dot_hi(_chunk_tril(T), gh)
    return gc, bh


def f_gdn_post(ids, o, z, gain):
    inv = lax.rsqrt(jnp.mean(o * o, axis=-1, keepdims=True) + EPS)
    return (o * inv * gain * _silu(z),)


def f_qknorm(ids, x, gain2):
    lane = lax.broadcasted_iota(jnp.int32, x.shape, 1)
    lo = lane < DSW_DH
    x2 = x * x
    s_all = jnp.sum(x2, axis=-1, keepdims=True)
    s_lo = jnp.sum(jnp.where(lo, x2, 0.0), axis=-1, keepdims=True)
    ms = jnp.where(lo, s_lo, s_all - s_lo) * (1.0 / DSW_DH)
    return (x * lax.rsqrt(ms + EPS) * gain2,)


def f_combine(ids, o0, o1, o2, l0, l1, l2):
    m = jnp.maximum(jnp.maximum(l0, l1), l2)
    e0, e1, e2 = jnp.exp(l0 - m), jnp.exp(l1 - m), jnp.exp(l2 - m)
    den = e0 + e1 + e2
    o = (e0 * o0 + e1 * o1 + e2 * o2) / den
    return o, m + jnp.log(den)


GDN_T = 512
HALO = 8


def _conv_pre(xx, w):
    acc = xx * w[3:4, :]
    for j in range(3):
        acc = acc + pltpu.roll(xx, shift=3 - j, axis=0) * w[j:j + 1, :]
    return acc


def _qkv_act(pre, cidx):
    s = _silu(pre)
    r = lax.rsqrt(jnp.sum(s * s, axis=-1, keepdims=True) + EPS)
    scale = jnp.where(cidx < GDN_H, GDN_DK ** -0.5, 1.0).astype(F32)
    return jnp.where(cidx < 2 * GDN_H, s * r * scale, s)


def gdn_pre(proj, conv_w, S):
    nt = S // GDN_T
    hb = GDN_T // HALO

    def body(prev_ref, cur_ref, w_ref, o_ref):
        c, i = pl.program_id(0), pl.program_id(1)
        prev = jnp.where(i > 0, prev_ref[...], 0.0)
        xx = jnp.concatenate([prev, cur_ref[...]], axis=0)
        pre = _conv_pre(xx, w_ref[...])[HALO:]
        o_ref[...] = _qkv_act(pre, c)

    return pl.pallas_call(
        body, name="gdn_pre", grid=(3 * GDN_H, nt),
        in_specs=[pl.BlockSpec((HALO, LANES), lambda c, i: (jnp.maximum(i * hb - 1, 0), c)),
                  pl.BlockSpec((GDN_T, LANES), lambda c, i: (i, c)),
                  pl.BlockSpec((4, LANES), lambda c, i: (0, c))],
        out_specs=pl.BlockSpec((None, None, GDN_T, LANES), lambda c, i: (c // GDN_H, c % GDN_H, i, 0)),
        out_shape=SDS((3, GDN_H, S, LANES), F32),
        compiler_params=_cp(2),
    )(proj, proj, conv_w)


def gdn_pre_bwd(proj, conv_w, dqkv, S):
    nt = S // GDN_T
    hb = GDN_T // HALO
    last_h = S // HALO - 1

    def body(prev_ref, cur_ref, next_ref, w_ref, d_ref, dnext_ref, dx_ref, dw_ref):
        c, i = pl.program_id(0), pl.program_id(1)
        w = w_ref[...]
        prev = jnp.where(i > 0, prev_ref[...], 0.0)
        xx = jnp.concatenate([prev, cur_ref[...], next_ref[...]], axis=0)
        dnext = jnp.where(i < nt - 1, dnext_ref[...], 0.0)
        dd = jnp.concatenate([jnp.zeros((HALO, LANES), F32), d_ref[...], dnext], axis=0)
        pre = _conv_pre(xx, w)
        _, vjp = jax.vjp(lambda p: _qkv_act(p, c), pre)
        (dpre,) = vjp(dd)
        row = lax.broadcasted_iota(jnp.int32, dpre.shape, 0)
        dpre = jnp.where(row >= HALO, dpre, 0.0)
        dx = dpre * w[3:4, :]
        R = dpre.shape[0]
        for j in range(3):
            dx = dx + pltpu.roll(dpre, shift=R - (3 - j), axis=0) * w[j:j + 1, :]
        dx_ref[...] = dx[HALO:HALO + GDN_T].astype(dx_ref.dtype)
        own = jnp.where(row < HALO + GDN_T, dpre, 0.0)
        rows_w = [jnp.sum(own * pltpu.roll(xx, shift=3 - j, axis=0), axis=0, keepdims=True) for j in range(3)]
        rows_w.append(jnp.sum(own * xx, axis=0, keepdims=True))
        r4 = lax.broadcasted_iota(jnp.int32, (4, LANES), 0)
        dw = jnp.zeros((4, LANES), F32)
        for j in range(4):
            dw = dw + jnp.where(r4 == j, rows_w[j], 0.0)

        @pl.when(i == 0)
        def _():
            dw_ref[...] = jnp.zeros_like(dw_ref)

        dw_ref[...] += dw

    dspec = lambda f: pl.BlockSpec((None, None) + f[0], f[1])
    return pl.pallas_call(
        body, name="gdn_pre_bwd", grid=(3 * GDN_H, nt),
        in_specs=[pl.BlockSpec((HALO, LANES), lambda c, i: (jnp.maximum(i * hb - 1, 0), c)),
                  pl.BlockSpec((GDN_T, LANES), lambda c, i: (i, c)),
                  pl.BlockSpec((HALO, LANES), lambda c, i: (jnp.minimum((i + 1) * hb, last_h), c)),
                  pl.BlockSpec((4, LANES), lambda c, i: (0, c)),
                  dspec(((GDN_T, LANES), lambda c, i: (c // GDN_H, c % GDN_H, i, 0))),
                  dspec(((HALO, LANES), lambda c, i: (c // GDN_H, c % GDN_H, jnp.minimum((i + 1) * hb, last_h), 0)))],
        out_specs=[pl.BlockSpec((GDN_T, LANES), lambda c, i: (i, c)),
                   pl.BlockSpec((4, LANES), lambda c, i: (0, c))],
        out_shape=[SDS((S, 3 * GDN_H * LANES), BF16), SDS((4, 3 * GDN_H * LANES), F32)],
        compiler_params=_cp(2),
    )(proj, proj, proj, conv_w, dqkv, dqkv)


_DIMS = {"nn": ((1,), (0,)), "nt": ((1,), (1,)), "tn": ((0,), (0,))}


def _mm_raw(a, b, mode, hi):
    if hi:
        return _dot_hi(a, b, _DIMS[mode])
    return lax.dot_general(a.astype(BF16), b.astype(BF16), (_DIMS[mode], ((), ())), preferred_element_type=F32)


@functools.partial(jax.custom_vjp, nondiff_argnums=(2, 3))
def mm(a, b, mode, hi):
    return _mm_raw(a, b, mode, hi)


def _mm_fwd(a, b, mode, hi):
    return _mm_raw(a, b, mode, hi), (a, b)


def _mm_bwd(mode, hi, res, dc):
    a, b = res
    if mode == "nn":
        da, db = mm(dc, b, "nt", hi), mm(a, dc, "tn", hi)
    elif mode == "nt":
        da, db = mm(dc, b, "nn", hi), mm(dc, a, "tn", hi)
    else:
        da, db = mm(b, dc, "nt", hi), mm(a, dc, "nn", hi)
    return da, db


mm.defvjp(_mm_fwd, _mm_bwd)


def _unit_lower_inverse(L):
    n = L.shape[0]
    r = lax.broadcasted_iota(jnp.int32, (n, n), 0)
    c = lax.broadcasted_iota(jnp.int32, (n, n), 1)
    eye = jnp.where(r == c, 1.0, 0.0).astype(F32)
    P = -L
    inv = eye + P
    k = 1
    while 2 * k < n:
        P = _dot_hi(P, P)
        inv = inv + _dot_hi(inv, P)
        k *= 2
    return inv


@jax.custom_vjp
def tri_solve2(L, r1, r2):
    inv = _unit_lower_inverse(L)
    return _dot_hi(inv, r1), _dot_hi(inv, r2)


def _tri_fwd(L, r1, r2):
    inv = _unit_lower_inverse(L)
    s1, s2 = _dot_hi(inv, r1), _dot_hi(inv, r2)
    return (s1, s2), (inv, s1, s2)


def _tri_bwd(res, ds):
    inv, s1, s2 = res
    d1 = _dot_hi(inv, ds[0], _DIMS["tn"])
    d2 = _dot_hi(inv, ds[1], _DIMS["tn"])
    dL = -(_dot_hi(d1, s1, _DIMS["nt"]) + _dot_hi(d2, s2, _DIMS["nt"]))
    return dL, d1, d2


tri_solve2.defvjp(_tri_fwd, _tri_bwd)


def _gdn_chunk(q, k, v, gcb, btb, S):
    C = q.shape[0]
    r = lax.broadcasted_iota(jnp.int32, (C, C), 0)
    c = lax.broadcasted_iota(jnp.int32, (C, C), 1)
    causal, strict = c <= r, c < r
    G = gcb[:, :C]
    decay = jnp.exp(jnp.where(causal, G - G.T, NEG))
    kb, vb = k * btb, v * btb
    L = jnp.where(strict, mm(kb, k, "nt", False) * decay, 0.0)
    eg = jnp.exp(gcb)
    u, w = tri_solve2(L, vb, kb * eg)
    qk = jnp.where(causal, mm(q, k, "nt", False) * decay, 0.0)
    rows = lax.broadcasted_iota(jnp.int32, gcb.shape, 0)
    g_last = jnp.sum(jnp.where(rows == C - 1, gcb, 0.0), axis=0, keepdims=True)
    q_dec = q * eg
    k_dec = k * jnp.exp(g_last - gcb)
    v_new = u - mm(w, S, "nn", False)
    o = mm(q_dec, S, "nn", False) + mm(qk, v_new, "nn", False)
    S_new = S * jnp.exp(g_last) + mm(k_dec, v_new, "tn", False)
    return o, S_new


def gdn_core(qkv, gc, bt, S):
    nchunk = S // GDN_C

    def body(qkv_ref, g_ref, b_ref, o_ref, st_ref, s_scr):
        n = pl.program_id(1)

        @pl.when(n == 0)
        def _():
            s_scr[...] = jnp.zeros_like(s_scr)

        S_in = s_scr[...]
        st_ref[...] = S_in
        o, S_new = _gdn_chunk(qkv_ref[0], qkv_ref[1], qkv_ref[2], g_ref[...], b_ref[...], S_in)
        o_ref[...] = o
        s_scr[...] = S_new

    blk3 = pl.BlockSpec((3, None, GDN_C, LANES), lambda h, n: (0, h, n, 0))
    hb = pl.BlockSpec((None, GDN_C, LANES), lambda h, n: (h, n, 0))
    return pl.pallas_call(
        body, name="gdn_core", grid=(GDN_H, nchunk),
        in_specs=[blk3, hb, hb],
        out_specs=[hb, pl.BlockSpec((None, None, GDN_DK, LANES), lambda h, n: (h, n, 0, 0))],
        out_shape=[SDS((GDN_H, S, LANES), F32), SDS((GDN_H, nchunk, GDN_DK, LANES), F32)],
        scratch_shapes=[pltpu.VMEM((GDN_DK, LANES), F32)],
        compiler_params=_cp(2),
    )(qkv, gc, bt)


def gdn_core_bwd(qkv, gc, bt, states, do, S):
    nchunk = S // GDN_C

    def body(qkv_ref, g_ref, b_ref, st_ref, do_ref, dqkv_ref, dg_ref, db_ref, ds_scr):
        n = pl.program_id(1)

        @pl.when(n == 0)
        def _():
            ds_scr[...] = jnp.zeros_like(ds_scr)

        _, vjp = jax.vjp(_gdn_chunk, qkv_ref[0], qkv_ref[1], qkv_ref[2], g_ref[...], b_ref[...], st_ref[...])
        dq, dk, dv, dg, db, dS = vjp((do_ref[...], ds_scr[...]))
        dqkv_ref[0] = dq
        dqkv_ref[1] = dk
        dqkv_ref[2] = dv
        dg_ref[...] = dg
        db_ref[...] = db
        ds_scr[...] = dS

    rev = lambda n: nchunk - 1 - n
    blk3 = pl.BlockSpec((3, None, GDN_C, LANES), lambda h, n: (0, h, rev(n), 0))
    hb = pl.BlockSpec((None, GDN_C, LANES), lambda h, n: (h, rev(n), 0))
    return pl.pallas_call(
        body, name="gdn_core_bwd", grid=(GDN_H, nchunk),
        in_specs=[blk3, hb, hb, pl.BlockSpec((None, None, GDN_DK, LANES), lambda h, n: (h, rev(n), 0, 0)), hb],
        out_specs=[blk3, hb, hb],
        out_shape=[SDS((3, GDN_H, S, LANES), F32), SDS((GDN_H, S, LANES), F32), SDS((GDN_H, S, LANES), F32)],
        scratch_shapes=[pltpu.VMEM((GDN_DK, LANES), F32)],
        compiler_params=_cp(2),
    )(qkv, gc, bt, states, do)


GDN_MAIN = 4 * GDN_H * LANES
GDN_PROJ = GDN_MAIN + LANES
RT = 256


def gdn_forward(h, w_in, conv_w, alog, dtb, out_gain, w_out):
    S = h.shape[0]
    nt = S // RT
    proj = matmul(h, w_in, "nn", F32, "gdn_in")
    qkv = gdn_pre(proj, conv_w, S)
    ab_row = Row(proj, (RT, LANES), lambda i, j: (i, GDN_MAIN // LANES), acc=True, gshape=(S, LANES), gimap=lambda i, j: (i, 0))
    hm = lambda i, j: (j, i, 0)
    gc, bt = rowwise(f_gdn_gates, [ab_row], [alog, dtb],
                     [Out((GDN_H, S, LANES), F32, (None, RT, LANES), hm)] * 2, (nt, GDN_H), "gdn_gates")
    o, states = gdn_core(qkv, gc, bt, S)
    o_row = Row(o, (None, RT, LANES), hm)
    z_row = Row(proj, (RT, LANES), lambda i, j: (i, 3 * GDN_H + j), gdtype=BF16, gshape=(S, GDN_H * LANES), gimap=lambda i, j: (i, j))
    (on,) = rowwise(f_gdn_post, [o_row, z_row], [out_gain],
                    [Out((S, GDN_H * LANES), BF16, (RT, LANES), lambda i, j: (i, j))], (nt, GDN_H), "gdn_post")
    y = matmul(on, w_out, "nn", F32, "gdn_out")
    saved = dict(h=h, proj=proj, qkv=qkv, gc=gc, bt=bt, states=states, o=o, on=on, ab_row=ab_row, o_row=o_row, z_row=z_row)
    return y, saved


def gdn_backward(dy, sv, w_in, conv_w, alog, dtb, out_gain, w_out):
    S = dy.shape[0]
    nt = S // RT
    hm = lambda i, j: (j, i, 0)
    don = matmul(dy, w_out, "nt", F32, "gdn_out_dx")
    d_w_out = matmul(sv["on"], dy, "tn", F32, "gdn_out_dw")
    (do, dz), (d_gain,) = rowwise_bwd(f_gdn_post, [sv["o_row"], sv["z_row"]], [out_gain],
                                      [Row(don, (RT, LANES), lambda i, j: (i, j))], (nt, GDN_H), "gdn_post_bwd")
    dqkv, dgc, dbt = gdn_core_bwd(sv["qkv"], sv["gc"], sv["bt"], sv["states"], do, S)
    (dab,), (d_alog, d_dtb) = rowwise_bwd(f_gdn_gates, [sv["ab_row"]], [alog, dtb],
                                          [Row(dgc, (None, RT, LANES), hm), Row(dbt, (None, RT, LANES), hm)],
                                          (nt, GDN_H), "gdn_gates_bwd")
    dqkv_proj, d_conv = gdn_pre_bwd(sv["proj"], conv_w, dqkv, S)
    dproj = jnp.concatenate([dqkv_proj, dz, dab.astype(BF16)], axis=1)
    d_w_in = matmul(sv["h"], dproj, "tn", F32, "gdn_in_dw")
    dh = matmul(dproj, w_in, "nt", F32, "gdn_in_dx")
    return dh, dict(w_in=d_w_in, conv=d_conv, alog=d_alog, dtb=d_dtb, gain=d_gain, w_out=d_w_out)


QB = DSW_SPAN
N_HP = DSW_HG // LANES
PROJ_BLKS = 3 * 3 * N_HP


def _bucket_maps():
    a = np.arange(QB)[:, None]
    j = np.arange(2 * QB)[None, :]
    dist = QB + a - j
    band = (dist >= 0) & (dist <= DSW_SPAN)
    maps = []
    for _, dil in DSW_GROUPS:
        dd = np.maximum(dist, 0) * dil
        max_exact = REL_BUCKETS // 2
        scaled = np.log(np.maximum(dd, 1).astype(np.float32) / np.float32(max_exact)) / np.float32(math.log(REL_MAX_DIST / max_exact))
        large = max_exact + (scaled * np.float32(REL_BUCKETS - max_exact)).astype(np.int32)
        large = np.minimum(large, REL_BUCKETS - 1)
        maps.append(np.where(dd < max_exact, dd, large).astype(np.int32))
    return np.stack(maps), band


def dsw_bias(rel_bias):
    maps, band = _bucket_maps()
    maps = np.where(band[None], maps, -1).astype(np.int32)

    def body(tab_ref, bk_ref, o_ref):
        gh = pl.program_id(0)
        bk = bk_ref[...]
        acc = jnp.full(bk.shape, NEG, F32)
        for b in range(REL_BUCKETS):
            acc = jnp.where(bk == b, tab_ref[b, gh], acc)
        o_ref[...] = acc

    return pl.pallas_call(
        body, name="dsw_bias", grid=(3 * GDN_H,),
        in_specs=[pl.BlockSpec(memory_space=pltpu.SMEM),
                  pl.BlockSpec((None, QB, 2 * QB), lambda gh: (gh // GDN_H, 0, 0))],
        out_specs=pl.BlockSpec((None, QB, 2 * QB), lambda gh: (gh, 0, 0)),
        out_shape=SDS((3 * GDN_H, QB, 2 * QB), F32),
        compiler_params=_cp(1),
    )(rel_bias, jnp.asarray(maps))


def dsw_bias_grad(dbias):
    maps, band = _bucket_maps()
    maps = np.where(band[None], maps, -1).astype(np.int32)

    def body(d_ref, bk_ref, o_ref):
        bk = bk_ref[...]
        d = d_ref[...]
        rows = lax.broadcasted_iota(jnp.int32, (REL_BUCKETS, LANES), 0)
        acc = jnp.zeros((REL_BUCKETS, LANES), F32)
        for b in range(REL_BUCKETS):
            part = jnp.sum(jnp.where(bk == b, d, 0.0), axis=0, keepdims=True)
            val = jnp.sum(part, axis=1, keepdims=True)
            acc = jnp.where(rows == b, val, acc)
        o_ref[...] = acc

    return pl.pallas_call(
        body, name="dsw_bias_grad", grid=(3 * GDN_H,),
        in_specs=[pl.BlockSpec((None, QB, 2 * QB), lambda gh: (gh, 0, 0)),
                  pl.BlockSpec((None, QB, 2 * QB), lambda gh: (gh // GDN_H, 0, 0))],
        out_specs=pl.BlockSpec((None, REL_BUCKETS, LANES), lambda gh: (gh, 0, 0)),
        out_shape=SDS((3 * GDN_H, REL_BUCKETS, LANES), F32),
        compiler_params=_cp(1),
    )(dbias, jnp.asarray(maps))


def _nt(a, b):
    return lax.dot_general(a, b, (((1,), (1,)), ((), ())), preferred_element_type=F32)


def _tn(a, b):
    return lax.dot_general(a, b, (((0,), (0,)), ((), ())), preferred_element_type=F32)


def dsw_group_fwd(qn, kn, proj, bias, gi, S):
    dil = DSW_GROUPS[gi][1]
    sd = S // dil
    nq = sd // QB
    qv = qn.reshape(sd, dil * 3 * DSW_HG)
    kv = kn.reshape(sd, dil * 3 * DSW_HG)
    pv = proj.reshape(sd, dil * 9 * DSW_HG)
    qk_col = lambda hp, r: r * (3 * N_HP) + gi * N_HP + hp
    v_col = lambda hp, r: r * PROJ_BLKS + 2 * 3 * N_HP + gi * N_HP + hp

    def body(q_ref, kp_ref, kc_ref, vp_ref, vc_ref, b_ref, o_ref, l_ref):
        i = pl.program_id(2)
        q = q_ref[...]
        k2 = jnp.concatenate([kp_ref[...], kc_ref[...]], axis=0)
        v2 = jnp.concatenate([vp_ref[...], vc_ref[...]], axis=0).astype(BF16)
        lane_q = lax.broadcasted_iota(jnp.int32, (QB, LANES), 1) < DSW_DH
        lane_k = lax.broadcasted_iota(jnp.int32, (2 * QB, LANES), 1) < DSW_DH
        col = lax.broadcasted_iota(jnp.int32, (QB, 2 * QB), 1)
        first = jnp.logical_and(i == 0, col < QB)
        o_acc = jnp.zeros((QB, LANES), F32)
        lse_b = jnp.zeros((QB, LANES), F32)
        for hh in range(2):
            mq = lane_q if hh == 0 else jnp.logical_not(lane_q)
            mk = lane_k if hh == 0 else jnp.logical_not(lane_k)
            s = _nt(jnp.where(mq, q, 0).astype(BF16), k2) + b_ref[hh]
            s = jnp.where(first, NEG, s)
            mx = jnp.max(s, axis=1, keepdims=True)
            p = jnp.exp(s - mx)
            l = jnp.sum(p, axis=1, keepdims=True)
            oh = jnp.dot(p.astype(BF16), jnp.where(mk, v2, 0).astype(BF16), preferred_element_type=F32) / l
            o_acc = o_acc + oh
            lse_b = jnp.where(mq, mx + jnp.log(l), lse_b)
        o_ref[...] = o_acc
        l_ref[...] = lse_b

    blk = (QB, LANES)
    out_spec = pl.BlockSpec(blk, lambda hp, r, i: (i, r * N_HP + hp))
    o, lse = pl.pallas_call(
        body, name=f"dsw_fwd_g{gi}", grid=(N_HP, dil, nq),
        in_specs=[pl.BlockSpec(blk, lambda hp, r, i: (i, qk_col(hp, r))),
                  pl.BlockSpec(blk, lambda hp, r, i: (jnp.maximum(i - 1, 0), qk_col(hp, r))),
                  pl.BlockSpec(blk, lambda hp, r, i: (i, qk_col(hp, r))),
                  pl.BlockSpec(blk, lambda hp, r, i: (jnp.maximum(i - 1, 0), v_col(hp, r))),
                  pl.BlockSpec(blk, lambda hp, r, i: (i, v_col(hp, r))),
                  pl.BlockSpec((2, QB, 2 * QB), lambda hp, r, i: (gi * N_HP + hp, 0, 0))],
        out_specs=[out_spec, out_spec],
        out_shape=[SDS((sd, dil * DSW_HG), F32)] * 2,
        compiler_params=_cp(3),
    )(qv, kv, kv, pv, pv, bias)
    return o.reshape(S, DSW_HG), lse.reshape(S, DSW_HG)


def dsw_group_bwd(qn, kn, proj, bias, do, o, lse, gi, S):
    dil = DSW_GROUPS[gi][1]
    sd = S // dil
    nq = sd // QB
    qv = qn.reshape(sd, dil * 3 * DSW_HG)
    kv = kn.reshape(sd, dil * 3 * DSW_HG)
    pv = proj.reshape(sd, dil * 9 * DSW_HG)
    dov = do.reshape(sd, dil * DSW_HG)
    ov = o.reshape(sd, dil * DSW_HG)
    lv = lse.reshape(sd, dil * DSW_HG)
    qk_col = lambda hp, r: r * (3 * N_HP) + gi * N_HP + hp
    v_col = lambda hp, r: r * PROJ_BLKS + 2 * 3 * N_HP + gi * N_HP + hp
    o_col = lambda hp, r: r * N_HP + hp
    cur = lambda i: jnp.minimum(i, nq - 1)
    prev = lambda i: jnp.maximum(jnp.minimum(i, nq - 1) - 1, 0)
    done = lambda i: jnp.maximum(i - 1, 0)

    def body(q_ref, kp_ref, kc_ref, vp_ref, vc_ref, b_ref, do_ref, o_ref, l_ref,
             dq_ref, dk_ref, dv_ref, db_ref, dk_scr, dv_scr):
        r, i = pl.program_id(1), pl.program_id(2)

        @pl.when(jnp.logical_and(r == 0, i == 0))
        def _():
            db_ref[...] = jnp.zeros_like(db_ref)

        @pl.when(i == 0)
        def _():
            dk_scr[...] = jnp.zeros_like(dk_scr)
            dv_scr[...] = jnp.zeros_like(dv_scr)

        @pl.when(i < nq)
        def _():
            q = q_ref[...]
            k2 = jnp.concatenate([kp_ref[...], kc_ref[...]], axis=0)
            v2 = jnp.concatenate([vp_ref[...], vc_ref[...]], axis=0).astype(BF16)
            dout = do_ref[...].astype(F32)
            prod = dout * o_ref[...].astype(F32)
            lse_b = l_ref[...]
            lane_q = lax.broadcasted_iota(jnp.int32, (QB, LANES), 1) < DSW_DH
            col = lax.broadcasted_iota(jnp.int32, (QB, 2 * QB), 1)
            first = jnp.logical_and(i == 0, col < QB)
            dq = jnp.zeros((QB, LANES), F32)
            dk2 = jnp.zeros((2 * QB, LANES), F32)
            dv2 = jnp.zeros((2 * QB, LANES), F32)
            for hh in range(2):
                mq = lane_q if hh == 0 else jnp.logical_not(lane_q)
                qm = jnp.where(mq, q, 0).astype(BF16)
                dom = jnp.where(mq, dout, 0.0).astype(BF16)
                s = _nt(qm, k2) + b_ref[hh]
                s = jnp.where(first, NEG, s)
                lse_h = jnp.max(jnp.where(mq, lse_b, NEG), axis=1, keepdims=True)
                p = jnp.exp(s - lse_h)
                delta = jnp.sum(jnp.where(mq, prod, 0.0), axis=1, keepdims=True)
                dp = _nt(dom, v2)
                ds = p * (dp - delta)
                dsb = ds.astype(BF16)
                dq = dq + jnp.where(mq, jnp.dot(dsb, k2, preferred_element_type=F32), 0.0)
                dk2 = dk2 + _tn(dsb, qm)
                dv2 = dv2 + _tn(p.astype(BF16), dom)
                db_ref[hh] += ds
            dq_ref[...] = dq
            dk_ref[...] = dk_scr[...] + dk2[:QB]
            dv_ref[...] = (dv_scr[...] + dv2[:QB]).astype(dv_ref.dtype)
            dk_scr[...] = dk2[QB:]
            dv_scr[...] = dv2[QB:]

        @pl.when(i == nq)
        def _():
            dk_ref[...] = dk_scr[...]
            dv_ref[...] = dv_scr[...].astype(dv_ref.dtype)

    blk = (QB, LANES)
    dq, dk, dv, dbias = pl.pallas_call(
        body, name=f"dsw_bwd_g{gi}", grid=(N_HP, dil, nq + 1),
        in_specs=[pl.BlockSpec(blk, lambda hp, r, i: (cur(i), qk_col(hp, r))),
                  pl.BlockSpec(blk, lambda hp, r, i: (prev(i), qk_col(hp, r))),
                  pl.BlockSpec(blk, lambda hp, r, i: (cur(i), qk_col(hp, r))),
                  pl.BlockSpec(blk, lambda hp, r, i: (prev(i), v_col(hp, r))),
                  pl.BlockSpec(blk, lambda hp, r, i: (cur(i), v_col(hp, r))),
                  pl.BlockSpec((2, QB, 2 * QB), lambda hp, r, i: (gi * N_HP + hp, 0, 0)),
                  pl.BlockSpec(blk, lambda hp, r, i: (cur(i), o_col(hp, r))),
                  pl.BlockSpec(blk, lambda hp, r, i: (cur(i), o_col(hp, r))),
                  pl.BlockSpec(blk, lambda hp, r, i: (cur(i), o_col(hp, r)))],
        out_specs=[pl.BlockSpec(blk, lambda hp, r, i: (cur(i), o_col(hp, r))),
                   pl.BlockSpec(blk, lambda hp, r, i: (done(i), o_col(hp, r))),
                   pl.BlockSpec(blk, lambda hp, r, i: (done(i), o_col(hp, r))),
                   pl.BlockSpec((2, QB, 2 * QB), lambda hp, r, i: (hp, 0, 0))],
        out_shape=[SDS((sd, dil * DSW_HG), F32), SDS((sd, dil * DSW_HG), F32), SDS((sd, dil * DSW_HG), BF16),
                   SDS((GDN_H, QB, 2 * QB), F32)],
        scratch_shapes=[pltpu.VMEM(blk, F32), pltpu.VMEM(blk, F32)],
        compiler_params=_cp(3),
    )(qv, kv, kv, pv, pv, bias, dov, ov, lv)
    return dq.reshape(S, DSW_HG), dk.reshape(S, DSW_HG), dv.reshape(S, DSW_HG), dbias


def f_qnorm(ids, x, gain2):
    return (f_qknorm(ids, x, gain2)[0] * (DSW_DH ** -0.5),)


def dsw_forward(h, w_in, q_gain2, k_gain2, rel_bias, w_out):
    S = h.shape[0]
    nt = S // RT
    nb = 3 * N_HP
    proj = matmul(h, w_in, "nn", F32, "dsw_in")
    width = nb * LANES
    (qn,) = rowwise(f_qnorm, [Row(proj, (RT, LANES), lambda i, j: (i, j))], [q_gain2],
                    [Out((S, width), BF16, (RT, LANES), lambda i, j: (i, j))], (nt, nb), "dsw_qnorm")
    (kn,) = rowwise(f_qknorm, [Row(proj, (RT, LANES), lambda i, j: (i, nb + j))], [k_gain2],
                    [Out((S, width), BF16, (RT, LANES), lambda i, j: (i, j))], (nt, nb), "dsw_knorm")
    bias = dsw_bias(rel_bias)
    os_, ls_ = [], []
    for gi in range(3):
        o, l = dsw_group_fwd(qn, kn, proj, bias, gi, S)
        os_.append(o)
        ls_.append(l)
    full = lambda a: Row(a, (RT, DSW_HG), lambda i: (i, 0))
    o, lse = rowwise(f_combine, [full(a) for a in os_ + ls_], [],
                     [Out((S, DSW_HG), BF16, (RT, DSW_HG), lambda i: (i, 0)), Out((S, DSW_HG), F32, (RT, DSW_HG), lambda i: (i, 0))],
                     (nt,), "dsw_combine")
    y = matmul(o, w_out, "nn", F32, "dsw_out")
    return y, dict(h=h, proj=proj, qn=qn, kn=kn, bias=bias, o=o, lse=lse)


def dsw_backward(dy, sv, w_in, q_gain2, k_gain2, w_out):
    S = dy.shape[0]
    nt = S // RT
    nb = 3 * N_HP
    do = matmul(dy, w_out, "nt", BF16, "dsw_out_dx")
    d_w_out = matmul(sv["o"], dy, "tn", F32, "dsw_out_dw")
    pieces_q, pieces_k, pieces_v, dbs = [], [], [], []
    d_qg = jnp.zeros((1, LANES), F32)
    d_kg = jnp.zeros((1, LANES), F32)
    for gi in range(3):
        dq, dk, dv, db = dsw_group_bwd(sv["qn"], sv["kn"], sv["proj"], sv["bias"], do, sv["o"], sv["lse"], gi, S)
        dbs.append(db)
        pieces_v.append(dv)
        for which, dd in ((0, dq), (1, dk)):
            row = Row(sv["proj"], (RT, LANES), lambda i, j, _o=which * nb + gi * N_HP: (i, _o + j),
                      gdtype=BF16, gshape=(S, DSW_HG), gimap=lambda i, j: (i, j))
            fn, gain = (f_qnorm, q_gain2) if which == 0 else (f_qknorm, k_gain2)
            (dx,), (dg,) = rowwise_bwd(fn, [row], [gain], [Row(dd, (RT, LANES), lambda i, j: (i, j))],
                                       (nt, N_HP), f"dsw_norm_bwd_{which}{gi}")
            if which == 0:
                pieces_q.append(dx)
                d_qg = d_qg + dg
            else:
                pieces_k.append(dx)
                d_kg = d_kg + dg
    dproj = jnp.concatenate(pieces_q + pieces_k + pieces_v, axis=1)
    d_w_in = matmul(sv["h"], dproj, "tn", F32, "dsw_in_dw")
    dh = matmul(dproj, w_in, "nt", F32, "dsw_in_dx")
    d_rel = dsw_bias_grad(jnp.concatenate(dbs, axis=0))
    return dh, dict(w_in=d_w_in, q_gain2=d_qg, k_gain2=d_kg, rel=d_rel, w_out=d_w_out)


FT = 128


def ffn_forward(h, w_in, w_out, tag):
    S = h.shape[0]
    gu = matmul(h, w_in, "nn", F32, f"ffn_in_{tag}")
    gu_row = Row(gu, (FT, 2 * FFN), lambda i: (i, 0), splits=[FFN, FFN], gdtype=BF16)
    (a,) = rowwise(f_swiglu, [gu_row], [], [Out((S, FFN), BF16, (FT, FFN), lambda i: (i, 0))], (S // FT,), f"ffn_act_{tag}")
    f = matmul(a, w_out, "nn", F32, f"ffn_out_{tag}")
    return f, dict(h=h, gu_row=gu_row, a=a)


def ffn_backward(df, sv, w_in, w_out, tag):
    S = df.shape[0]
    da = matmul(df, w_out, "nt", BF16, f"ffn_out_dx_{tag}")
    d_w_out = matmul(sv["a"], df, "tn", F32, f"ffn_out_dw_{tag}")
    (dgu,), _ = rowwise_bwd(f_swiglu, [sv["gu_row"]], [], [Row(da, (FT, FFN), lambda i: (i, 0))], (S // FT,), f"ffn_act_bwd_{tag}")
    d_w_in = matmul(sv["h"], dgu, "tn", F32, f"ffn_in_dw_{tag}")
    dh = matmul(dgu, w_in, "nt", F32, f"ffn_in_dx_{tag}")
    return dh, d_w_in, d_w_out


def f_norm_only(ids, x, gain, sc, sh):
    return (_normmod(x, gain, sc, sh),)


def _wide(a, **kw):
    return Row(a, (RT, D), lambda i: (i, 0), **kw)


def _wide_out(S, dtype):
    return Out((S, D), dtype, (RT, D), lambda i: (i, 0))


def adamw(w, g, m, v, name):
    shape = w.shape
    C = shape[-1]
    R = int(np.prod(shape[:-1]))
    w2, g2, m2, v2 = (a.reshape(R, C) for a in (w, g, m, v))
    br = R
    if R > 256:
        br = max(b for b in range(8, 257, 8) if R % b == 0)
    c1 = 1.0 / (1.0 - ADAM_B1 ** ADAM_STEP)
    c2 = 1.0 / (1.0 - ADAM_B2 ** ADAM_STEP)

    def body(w_ref, g_ref, m_ref, v_ref, d_ref, nm_ref, nv_ref):
        gg = g_ref[...]
        mm_ = ADAM_B1 * m_ref[...] + (1.0 - ADAM_B1) * gg
        vv = ADAM_B2 * v_ref[...] + (1.0 - ADAM_B2) * (gg * gg)
        d_ref[...] = -ADAM_LR * ((mm_ * c1) / (jnp.sqrt(vv * c2) + ADAM_EPS) + ADAM_WD * w_ref[...])
        nm_ref[...] = mm_
        nv_ref[...] = vv

    spec = pl.BlockSpec((br, C), lambda i: (i, 0))
    d, nm, nv = pl.pallas_call(
        body, name=name, grid=(R // br,), in_specs=[spec] * 4, out_specs=[spec] * 3,
        out_shape=[SDS((R, C), F32)] * 3, compiler_params=_cp(1),
    )(w2, g2, m2, v2)
    return d.reshape(shape), nm.reshape(shape), nv.reshape(shape)


def _place():
    x, y, c = lax.axis_index("x"), lax.axis_index("y"), lax.axis_index("c")
    chips = [(1 - x, y), (x, 1 - y), (1 - x, 1 - y)]
    return x, y, c, chips


def all_gather_small(blk, name):
    m_per, n = blk.shape

    def body(x_ref, out_ref, send_sems, recv_sems, local_sem):
        x, y, c, chips = _place()
        me, sibling = (x, y, c), (x, y, 1 - c)

        def rows(px, py, pc):
            return out_ref.at[pl.ds((4 * px + 2 * py + pc) * m_per, m_per), :]

        def copy(k, block, to, src=None):
            return pltpu.make_async_remote_copy(
                src_ref=rows(*block) if src is None else src, dst_ref=rows(*block),
                send_sem=send_sems.at[k], recv_sem=recv_sems.at[k], device_id=to, device_id_type=MESH)

        mine = pltpu.make_async_copy(x_ref, rows(*me), local_sem)
        mine.start()
        first = [copy(0, me, sibling, src=x_ref)]
        first += [copy(1 + j, me, (*chip, c), src=x_ref) for j, chip in enumerate(chips)]
        for cp in first:
            cp.start()
        passed = [copy(4 + j, (*chip, c), sibling) for j, chip in enumerate(chips)]
        for j, chip in enumerate(chips):
            copy(1 + j, (*chip, c), me).wait_recv()
            passed[j].start()
        copy(0, sibling, me).wait_recv()
        for j, chip in enumerate(chips):
            copy(4 + j, (*chip, 1 - c), me).wait_recv()
        for cp in first + passed:
            cp.wait_send()
        mine.wait()

    return pl.pallas_call(
        body, name=name, out_shape=SDS((N_DEV * m_per, n), blk.dtype),
        in_specs=[pl.BlockSpec(memory_space=pltpu.VMEM)], out_specs=pl.BlockSpec(memory_space=pltpu.VMEM),
        scratch_shapes=[pltpu.SemaphoreType.DMA((7,)), pltpu.SemaphoreType.DMA((7,)), pltpu.SemaphoreType.DMA],
    )(blk)


def all_gather_shards(wp):
    R, W = wp.shape
    Rh = R // 2

    def body(w_ref, out_ref, send_sems, recv_sems, local_sem):
        x, y, c, chips = _place()
        sibling = (x, y, 1 - c)
        s_me = 2 * x + y

        def half(cc):
            return pl.ds(pl.multiple_of(cc * Rh, 16), Rh)

        def copy(k, src, dst, to):
            return pltpu.make_async_remote_copy(src_ref=src, dst_ref=dst, send_sem=send_sems.at[k], recv_sem=recv_sems.at[k],
                                                device_id=to, device_id_type=MESH)

        mine = pltpu.make_async_copy(w_ref, out_ref.at[s_me], local_sem)
        mine.start()
        sends = [copy(j, w_ref.at[half(c)], out_ref.at[s_me, half(c)], (*chip, c)) for j, chip in enumerate(chips)]
        for cp in sends:
            cp.start()
        passed = []
        for j, (px, py) in enumerate(chips):
            got = out_ref.at[2 * px + py, half(c)]
            copy(j, got, got, (px, py, c)).wait_recv()
            fw = copy(3 + j, got, got, sibling)
            fw.start()
            passed.append(fw)
        for j, (px, py) in enumerate(chips):
            got = out_ref.at[2 * px + py, half(1 - c)]
            copy(3 + j, got, got, sibling).wait_recv()
        for cp in sends + passed:
            cp.wait_send()
        mine.wait()

    return pl.pallas_call(
        body, name="weights_all_gather", out_shape=SDS((N_SHARD, R, W), wp.dtype),
        in_specs=[ANY], out_specs=ANY,
        scratch_shapes=[pltpu.SemaphoreType.DMA((6,)), pltpu.SemaphoreType.DMA((6,)), pltpu.SemaphoreType.DMA],
    )(wp)


def sibling_exchange(send, name):
    def body(s_ref, o_ref, send_sem, recv_sem):
        x, y, c, _ = _place()
        cp = pltpu.make_async_remote_copy(src_ref=s_ref, dst_ref=o_ref, send_sem=send_sem, recv_sem=recv_sem,
                                          device_id=(x, y, 1 - c), device_id_type=MESH)
        cp.start()
        cp.wait()

    return pl.pallas_call(
        body, name=name, out_shape=SDS(send.shape, send.dtype), in_specs=[ANY], out_specs=ANY,
        scratch_shapes=[pltpu.SemaphoreType.DMA, pltpu.SemaphoreType.DMA],
    )(send)


def scatter_to_chips(p):
    _, Rh, W = p.shape

    def body(p_ref, o_ref, send_sems, recv_sems):
        x, y, c, chips = _place()
        cps = []
        for j, (px, py) in enumerate(chips):
            cp = pltpu.make_async_remote_copy(src_ref=p_ref.at[2 * px + py], dst_ref=o_ref.at[j], send_sem=send_sems.at[j],
                                              recv_sem=recv_sems.at[j], device_id=(px, py, c), device_id_type=MESH)
            cp.start()
            cps.append(cp)
        for cp in cps:
            cp.wait()

    return pl.pallas_call(
        body, name="grads_scatter", out_shape=SDS((3, Rh, W), p.dtype), in_specs=[ANY], out_specs=ANY,
        scratch_shapes=[pltpu.SemaphoreType.DMA((3,)), pltpu.SemaphoreType.DMA((3,))],
    )(p)


def add_rows(arrs, out_dtype, name, rt=256):
    Rr, W = arrs[0].shape

    def fn(ids, *vals):
        acc = vals[0]
        for v in vals[1:]:
            acc = acc + v
        return (acc,)

    t = rt if Rr % rt == 0 else max(b for b in range(16, rt + 1, 16) if Rr % b == 0)
    (out,) = rowwise(fn, [Row(a, (t, W), lambda i: (i, 0)) for a in arrs], [],
                     [Out((Rr, W), out_dtype, (t, W), lambda i: (i, 0))], (Rr // t,), name)
    return out


PACK = (("gdn_w_in", 2), ("gdn_w_out", 1), ("w_ffn_in", 2), ("w_ffn_out", 1), ("dsw_w_in", 2), ("dsw_w_out", 2))
PACK_ALIGN = 32


def _pack_rows(sizes):
    total = sum(sizes)
    rows = -(-total // D)
    return -(-rows // PACK_ALIGN) * PACK_ALIGN


def pack_blocks(blocks, dtype):
    flat = [b.astype(dtype).reshape(-1) for b in blocks]
    total = sum(f.shape[0] for f in flat)
    R = _pack_rows([f.shape[0] for f in flat])
    flat.append(jnp.zeros((R * D - total,), dtype))
    return jnp.concatenate(flat).reshape(R, D)


def unpack_blocks(buf, shapes):
    flat = buf.reshape(-1)
    out, off = [], 0
    for shp in shapes:
        n = int(np.prod(shp))
        out.append(flat[off:off + n].reshape(shp))
        off += n
    return out


def _shard_slice(a, axis, s):
    n = a.shape[axis] // N_SHARD
    return lax.slice_in_dim(a, s * n, (s + 1) * n, axis=axis)


def _pad_lanes(v):
    return jnp.concatenate([v.astype(F32), jnp.zeros((LANES - v.shape[0],), F32)])[None]


def kernel(x, c, w_ada, b_ada, norm_mix, norm_ffn, w_ffn_in, w_ffn_out, gdn_w_in, gdn_conv, gdn_a_log, gdn_dt_bias, gdn_out_norm, gdn_w_out, dsw_w_in, dsw_q_norm, dsw_k_norm, dsw_w_out, rel_bias, loss_target, m_w_ada, m_b_ada, m_norm_mix, m_norm_ffn, m_w_ffn_in, m_w_ffn_out, m_gdn_w_in, m_gdn_conv, m_gdn_a_log, m_gdn_dt_bias, m_gdn_out_norm, m_gdn_w_out, m_dsw_w_in, m_dsw_q_norm, m_dsw_k_norm, m_dsw_w_out, m_rel_bias, v_w_ada, v_b_ada, v_norm_mix, v_norm_ffn, v_w_ffn_in, v_w_ffn_out, v_gdn_w_in, v_gdn_conv, v_gdn_a_log, v_gdn_dt_bias, v_gdn_out_norm, v_gdn_w_out, v_dsw_w_in, v_dsw_q_norm, v_dsw_k_norm, v_dsw_w_out, v_rel_bias):
    S = x.shape[1]
    nt = S // RT
    xi, yi, ci = lax.axis_index("x"), lax.axis_index("y"), lax.axis_index("c")
    me = 4 * xi + 2 * yi + ci
    s_me = 2 * xi + yi
    x0, tgt = x[0], loss_target[0]
    shard = dict(w_ffn_in=w_ffn_in, w_ffn_out=w_ffn_out, gdn_w_in=gdn_w_in, gdn_w_out=gdn_w_out, dsw_w_in=dsw_w_in, dsw_w_out=dsw_w_out)

    whole = lambda a: Row(a, a.shape, lambda i: (0,) * a.ndim)
    (cond8,) = rowwise(lambda ids, v: (_silu(v),), [whole(c.reshape(8, LANES))], [], [Out((8, LANES), F32, (8, LANES), lambda i: (0, 0))], (1,), "cond")
    cond_all = all_gather_small(cond8, "gather_cond").reshape(N_DEV, D)
    cond16 = jnp.concatenate([cond_all, jnp.zeros((8, D), F32)], axis=0)
    ada_cols = w_ada.shape[2]
    mods = [matmul(cond16, w_ada[l], "nn", F32, f"ada_{l}")[:N_DEV] for l in range(2)]
    buf = jnp.concatenate([jnp.stack(mods, axis=1).reshape(-1, LANES), gdn_conv.reshape(-1, LANES)], axis=0)
    n_mod_rows = N_DEV * 2 * ada_cols // LANES
    got = all_gather_small(buf, "gather_mod").reshape(N_DEV, buf.shape[0], LANES)
    mod_parts, conv_parts = [], []
    for s in range(N_SHARD):
        from_dev = got[2 * s]
        mod_parts.append(lax.dynamic_index_in_dim(from_dev[:n_mod_rows].reshape(N_DEV, 2, ada_cols), me, 0, keepdims=False))
        conv_parts.append(from_dev[n_mod_rows:].reshape(4, -1))
    mod_nb = jnp.concatenate(mod_parts, axis=1)
    conv_w = jnp.concatenate(conv_parts, axis=1)
    (mod,) = rowwise(lambda ids, a, b: (a + b,), [whole(mod_nb), whole(b_ada)], [], [Out(mod_nb.shape, F32, mod_nb.shape, lambda i: (0, 0))], (1,), "mod_bias")
    mod = mod.reshape(2, 6, 1, D)
    sh1, sc1, g1, sh2, sc2, g2 = ([mod[l, k] for l in range(2)] for k in range(6))
    gmix = [norm_mix[l][None] for l in range(2)]
    gffn = [norm_ffn[l][None] for l in range(2)]

    wp = pack_blocks([shard[n] for n, _ in PACK], BF16)
    wall = all_gather_shards(wp)
    shapes = [shard[n].shape for n, _ in PACK]
    per_shard = [unpack_blocks(wall[s], shapes) for s in range(N_SHARD)]
    W = {n: jnp.concatenate([per_shard[s][k] for s in range(N_SHARD)], axis=ax) for k, (n, ax) in enumerate(PACK)}
    gw = W["gdn_w_in"][0]
    w_gdn = jnp.concatenate([gw, jnp.zeros((D, GDN_PROJ - gw.shape[1]), BF16)], axis=1)
    alog, dtb = _pad_lanes(gdn_a_log[0]), _pad_lanes(gdn_dt_bias[0])
    qg2 = jnp.concatenate([dsw_q_norm, dsw_q_norm], axis=1)
    kg2 = jnp.concatenate([dsw_k_norm, dsw_k_norm], axis=1)
    gdn_args = (w_gdn, conv_w, alog, dtb, gdn_out_norm, W["gdn_w_out"][0])
    dsw_args = (W["dsw_w_in"][0], qg2, kg2)

    (h10,) = rowwise(f_norm_only, [_wide(x0)], [gmix[0], sc1[0], sh1[0]], [_wide_out(S, BF16)], (nt,), "l0_norm")
    y0, sv_g = gdn_forward(h10, *gdn_args)
    x1, h20 = rowwise(f_resid_norm, [_wide(x0), _wide(y0)], [g1[0], gffn[0], sc2[0], sh2[0]], [_wide_out(S, F32), _wide_out(S, BF16)], (nt,), "l0_mid")
    f0, sv_f0 = ffn_forward(h20, W["w_ffn_in"][0], W["w_ffn_out"][0], "0")
    x2, h11 = rowwise(f_resid_norm, [_wide(x1), _wide(f0)], [g2[0], gmix[1], sc1[1], sh1[1]], [_wide_out(S, F32), _wide_out(S, BF16)], (nt,), "l1_in")
    y1, sv_d = dsw_forward(h11, *dsw_args, rel_bias, W["dsw_w_out"][0])
    x3, h21 = rowwise(f_resid_norm, [_wide(x2), _wide(y1)], [g1[1], gffn[1], sc2[1], sh2[1]], [_wide_out(S, F32), _wide_out(S, BF16)], (nt,), "l1_mid")
    f1, sv_f1 = ffn_forward(h21, W["w_ffn_in"][1], W["w_ffn_out"][1], "1")
    part_spec = lambda a: Row(a, (None, 1, D), lambda i: (i, 0, 0))
    (parts,) = rowwise(f_loss, [_wide(x3), _wide(f1), _wide(tgt)], [g2[1]], [Out((nt, 1, D), F32, (None, 1, D), lambda i: (i, 0, 0))], (nt,), "loss")
    loss = lax.psum(jnp.sum(parts), ("x", "y", "c"))

    (dx3, df1), (dg2_1,) = rowwise_bwd(f_loss, [_wide(x3), _wide(f1, gdtype=BF16), _wide(tgt, diff=False)], [g2[1]],
                                       [part_spec(jnp.ones((nt, 1, D), F32))], (nt,), "loss_bwd")
    dh21, d_win1, d_wout1 = ffn_backward(df1, sv_f1, W["w_ffn_in"][1], W["w_ffn_out"][1], "1")
    (dx2, dy1), (dg1_1, dgf1, dsc2_1, dsh2_1) = rowwise_bwd(
        f_resid_norm, [_wide(x2), _wide(y1, gdtype=BF16)], [g1[1], gffn[1], sc2[1], sh2[1]], [_wide(dx3), _wide(dh21)], (nt,), "l1_mid_bwd")
    dh11, g_d = dsw_backward(dy1, sv_d, *dsw_args, W["dsw_w_out"][0])
    (dx1, df0), (dg2_0, dgm1, dsc1_1, dsh1_1) = rowwise_bwd(
        f_resid_norm, [_wide(x1), _wide(f0, gdtype=BF16)], [g2[0], gmix[1], sc1[1], sh1[1]], [_wide(dx2), _wide(dh11)], (nt,), "l1_in_bwd")
    dh20, d_win0, d_wout0 = ffn_backward(df0, sv_f0, W["w_ffn_in"][0], W["w_ffn_out"][0], "0")
    (dx0p, dy0), (dg1_0, dgf0, dsc2_0, dsh2_0) = rowwise_bwd(
        f_resid_norm, [_wide(x0), _wide(y0, gdtype=BF16)], [g1[0], gffn[0], sc2[0], sh2[0]], [_wide(dx1), _wide(dh20)], (nt,), "l0_mid_bwd")
    dh10, g_g = gdn_backward(dy0, sv_g, *gdn_args)
    (grad_x,), (dgm0, dsc1_0, dsh1_0) = rowwise_bwd(f_first, [_wide(x0)], [gmix[0], sc1[0], sh1[0]], [_wide(dx0p), _wide(dh10)], (nt,), "l0_norm_bwd")

    dmod = jnp.concatenate([dsh1_0, dsc1_0, dg1_0, dsh2_0, dsc2_0, dg2_0, dsh1_1, dsc1_1, dg1_1, dsh2_1, dsc2_1, dg2_1], axis=1)
    d_rel = jnp.transpose(g_d["rel"][:, :, 0])
    fold = lambda v: v[:, :DSW_DH] + v[:, DSW_DH:]
    small = [dmod, jnp.concatenate([dgm0, dgm1], axis=1), jnp.concatenate([dgf0, dgf1], axis=1), g_g["conv"].reshape(1, -1),
             g_g["alog"], g_g["dtb"], g_g["gain"], _pad_lanes(fold(g_d["q_gain2"])[0]), _pad_lanes(fold(g_d["k_gain2"])[0]),
             d_rel.reshape(1, -1)]
    sizes = [v.shape[1] // LANES for v in small]
    n_rows = sum(sizes)
    pad_rows = -(-n_rows // 8) * 8
    sbuf = jnp.concatenate([v.reshape(-1, LANES) for v in small] + [jnp.zeros((pad_rows - n_rows, LANES), F32)], axis=0)
    sgot = all_gather_small(sbuf, "gather_small_grads")
    ssum = add_rows([sgot[d * pad_rows:(d + 1) * pad_rows] for d in range(N_DEV)], F32, "sum_small_grads", rt=pad_rows)
    offs = np.cumsum([0] + sizes)
    take = lambda k: ssum[offs[k]:offs[k + 1]].reshape(1, -1)
    grad_b_ada = take(0).reshape(2, 6 * D)
    grad_norm_mix = take(1).reshape(2, D)
    grad_norm_ffn = take(2).reshape(2, D)
    conv_full = take(3).reshape(4, -1)
    ncv = gdn_conv.shape[2]
    grad_gdn_conv = lax.dynamic_slice_in_dim(conv_full, s_me * ncv, ncv, axis=1)[None]
    grad_a_log = take(4)[:, :GDN_H]
    grad_dt_bias = take(5)[:, :GDN_H]
    grad_out_norm = take(6)
    grad_q_norm = take(7)[:, :DSW_DH]
    grad_k_norm = take(8)[:, :DSW_DH]
    grad_rel = take(9).reshape(REL_BUCKETS, 3 * GDN_H)
    dmod_all = sgot.reshape(N_DEV, pad_rows, LANES)[:, :sizes[0]].reshape(N_DEV, 2, 6 * D)
    dmod_mine = lax.dynamic_slice_in_dim(dmod_all, s_me * ada_cols, ada_cols, axis=2)
    dmod16 = jnp.concatenate([dmod_mine, jnp.zeros_like(dmod_mine)], axis=0)
    grad_w_ada = jnp.stack([matmul(cond16, dmod16[:, l], "tn", F32, f"ada_dw_{l}") for l in range(2)])

    full = {"gdn_w_in": g_g["w_in"][:, :gdn_w_in.shape[2] * N_SHARD][None], "gdn_w_out": g_g["w_out"][None],
            "w_ffn_in": jnp.stack([d_win0, d_win1]), "w_ffn_out": jnp.stack([d_wout0, d_wout1]),
            "dsw_w_in": g_d["w_in"][None], "dsw_w_out": g_d["w_out"][None]}
    g32 = jnp.stack([pack_blocks([_shard_slice(full[n], ax, s) for n, ax in PACK], F32) for s in range(N_SHARD)])
    R = g32.shape[1]
    Rh = R // 2
    keep = lax.dynamic_slice_in_dim(g32, ci * Rh, Rh, axis=1)
    give = lax.dynamic_slice_in_dim(g32, (1 - ci) * Rh, Rh, axis=1).astype(BF16)
    from_sib = sibling_exchange(give, "grads_to_sibling")
    part = add_rows([keep.reshape(N_SHARD * Rh, D), from_sib.reshape(N_SHARD * Rh, D)], BF16, "grads_chip_sum").reshape(N_SHARD, Rh, D)
    others = scatter_to_chips(part)
    own = lax.dynamic_index_in_dim(part, s_me, 0, keepdims=False)
    half_sum = add_rows([own, others[0], others[1], others[2]], F32, "grads_sum")
    sib_half = sibling_exchange(half_sum, "grads_from_sibling")
    lo = jnp.where(ci == 0, half_sum, sib_half)
    hi = jnp.where(ci == 0, sib_half, half_sum)
    gsh = dict(zip([n for n, _ in PACK], unpack_blocks(jnp.concatenate([lo, hi], axis=0), shapes)))

    grads = dict(w_ada=grad_w_ada, b_ada=grad_b_ada, norm_mix=grad_norm_mix, norm_ffn=grad_norm_ffn, w_ffn_in=gsh["w_ffn_in"],
                 w_ffn_out=gsh["w_ffn_out"], gdn_w_in=gsh["gdn_w_in"], gdn_conv=grad_gdn_conv, gdn_a_log=grad_a_log,
                 gdn_dt_bias=grad_dt_bias, gdn_out_norm=grad_out_norm, gdn_w_out=gsh["gdn_w_out"], dsw_w_in=gsh["dsw_w_in"],
                 dsw_q_norm=grad_q_norm, dsw_k_norm=grad_k_norm, dsw_w_out=gsh["dsw_w_out"], rel_bias=grad_rel)
    weights = dict(w_ada=w_ada, b_ada=b_ada, norm_mix=norm_mix, norm_ffn=norm_ffn, w_ffn_in=w_ffn_in, w_ffn_out=w_ffn_out,
                   gdn_w_in=gdn_w_in, gdn_conv=gdn_conv, gdn_a_log=gdn_a_log, gdn_dt_bias=gdn_dt_bias, gdn_out_norm=gdn_out_norm,
                   gdn_w_out=gdn_w_out, dsw_w_in=dsw_w_in, dsw_q_norm=dsw_q_norm, dsw_k_norm=dsw_k_norm, dsw_w_out=dsw_w_out,
                   rel_bias=rel_bias)
    ms = dict(w_ada=m_w_ada, b_ada=m_b_ada, norm_mix=m_norm_mix, norm_ffn=m_norm_ffn, w_ffn_in=m_w_ffn_in, w_ffn_out=m_w_ffn_out,
              gdn_w_in=m_gdn_w_in, gdn_conv=m_gdn_conv, gdn_a_log=m_gdn_a_log, gdn_dt_bias=m_gdn_dt_bias, gdn_out_norm=m_gdn_out_norm,
              gdn_w_out=m_gdn_w_out, dsw_w_in=m_dsw_w_in, dsw_q_norm=m_dsw_q_norm, dsw_k_norm=m_dsw_k_norm, dsw_w_out=m_dsw_w_out,
              rel_bias=m_rel_bias)
    vs = dict(w_ada=v_w_ada, b_ada=v_b_ada, norm_mix=v_norm_mix, norm_ffn=v_norm_ffn, w_ffn_in=v_w_ffn_in, w_ffn_out=v_w_ffn_out,
              gdn_w_in=v_gdn_w_in, gdn_conv=v_gdn_conv, gdn_a_log=v_gdn_a_log, gdn_dt_bias=v_gdn_dt_bias, gdn_out_norm=v_gdn_out_norm,
              gdn_w_out=v_gdn_w_out, dsw_w_in=v_dsw_w_in, dsw_q_norm=v_dsw_q_norm, dsw_k_norm=v_dsw_k_norm, dsw_w_out=v_dsw_w_out,
              rel_bias=v_rel_bias)
    names = list(weights)
    deltas, new_m, new_v = [], [], []
    for n in names:
        g = grads[n].reshape(weights[n].shape)
        grads[n] = g
        d, nm, nv = adamw(weights[n], g, ms[n], vs[n], f"adamw_{n}")
        deltas.append(d)
        new_m.append(nm)
        new_v.append(nv)
    return (loss, grad_x[None], *[grads[n] for n in names], *deltas, *new_m, *new_v)
```

```python
import functools
import math

import numpy as np
import jax
import jax.numpy as jnp
from jax import lax
from jax.experimental import pallas as pl
from jax.experimental.pallas import tpu as pltpu

F32 = jnp.float32
BF16 = jnp.bfloat16
SDS = jax.ShapeDtypeStruct
MESH = pl.DeviceIdType.MESH
ANY = pl.BlockSpec(memory_space=pl.ANY)

D = 1024
EPS = 1e-6
LANES = 128
GDN_H = 8
GDN_DK = 128
GDN_C = 64
DSW_GROUPS = ((128, 1), (512, 4), (2048, 16))
DSW_SPAN = 128
DSW_DH = 64
DSW_HG = 512
REL_BUCKETS = 32
REL_MAX_DIST = 2048
FFN = 2816
N_SHARD = 4
N_DEV = 8
VMEM_LIMIT = 48 * 1024 * 1024
NEG = -1e30

ADAM_LR, ADAM_B1, ADAM_B2, ADAM_EPS, ADAM_WD, ADAM_STEP = 0.001, 0.9, 0.999, 1e-08, 0.01, 10


def _cp(n_axes):
    return pltpu.CompilerParams(dimension_semantics=("arbitrary",) * n_axes, vmem_limit_bytes=VMEM_LIMIT)


def _blk(dim, cap):
    if dim <= cap:
        return dim
    best = None
    for b in range(LANES, cap + 1, LANES):
        if dim % b == 0:
            best = b
    assert best is not None, (dim, cap)
    return best


def matmul(a, b, mode, out_dtype, name, cap_m=1024, cap_n=1024, cap_k=2048):
    if mode == "nn":
        (M, K), (K2, N) = a.shape, b.shape
    elif mode == "nt":
        (M, K), (N, K2) = a.shape, b.shape
    else:
        (K, M), (K2, N) = a.shape, b.shape
    assert K == K2, (a.shape, b.shape, mode)
    if K <= 3072:
        cap_k = K
        if K > 2048:
            cap_n = 512
    bm, bn, bk = _blk(M, cap_m), _blk(N, cap_n), _blk(K, cap_k)
    nk = K // bk
    dims = {"nn": ((1,), (0,)), "nt": ((1,), (1,)), "tn": ((0,), (0,))}[mode]

    def dot(a_ref, b_ref):
        return lax.dot_general(a_ref[...].astype(BF16), b_ref[...].astype(BF16), (dims, ((), ())), preferred_element_type=F32)

    def body_one(a_ref, b_ref, o_ref):
        o_ref[...] = dot(a_ref, b_ref).astype(o_ref.dtype)

    def body_acc(a_ref, b_ref, o_ref, acc_ref):
        k = pl.program_id(2)

        @pl.when(k == 0)
        def _():
            acc_ref[...] = jnp.zeros_like(acc_ref)

        acc_ref[...] += dot(a_ref, b_ref)

        @pl.when(k == nk - 1)
        def _():
            o_ref[...] = acc_ref[...].astype(o_ref.dtype)

    a_spec = pl.BlockSpec((bk, bm), lambda i, j, k: (k, i)) if mode == "tn" else pl.BlockSpec((bm, bk), lambda i, j, k: (i, k))
    b_spec = pl.BlockSpec((bn, bk), lambda i, j, k: (j, k)) if mode == "nt" else pl.BlockSpec((bk, bn), lambda i, j, k: (k, j))
    return pl.pallas_call(
        body_one if nk == 1 else body_acc, name=name, grid=(M // bm, N // bn, nk),
        in_specs=[a_spec, b_spec], out_specs=pl.BlockSpec((bm, bn), lambda i, j, k: (i, j)),
        out_shape=SDS((M, N), out_dtype), scratch_shapes=[] if nk == 1 else [pltpu.VMEM((bm, bn), F32)],
        compiler_params=_cp(3),
    )(a, b)


class Row:
    def __init__(self, arr, bshape, imap, splits=None, diff=True, acc=False, gdtype=F32, gshape=None, gbshape=None, gimap=None,
                 lead=0):
        self.arr, self.bshape, self.imap = arr, tuple(bshape), imap
        self.splits, self.lead = splits, lead
        self.diff, self.acc, self.gdtype = diff, acc, gdtype
        self.gshape = tuple(arr.shape) if gshape is None else tuple(gshape)
        self.gbshape = self.bshape if gbshape is None else tuple(gbshape)
        self.gimap = imap if gimap is None else gimap

    def gspec(self):
        return pl.BlockSpec(self.gbshape, self.gimap)

    def spec(self):
        return pl.BlockSpec(self.bshape, self.imap)

    def pieces(self, ref):
        return _load_pieces(ref, self.splits, self.lead)

    def n_pieces(self):
        return _n_pieces(self.splits, self.lead)


class Out:
    def __init__(self, shape, dtype, bshape, imap, splits=None, lead=0):
        self.shape, self.dtype, self.bshape, self.imap = tuple(shape), dtype, tuple(bshape), imap
        self.splits, self.lead = splits, lead

    def n_pieces(self):
        return _n_pieces(self.splits, self.lead)


def _n_pieces(splits, lead):
    return lead if lead else (1 if splits is None else len(splits))


def _load_pieces(ref, splits, lead):
    if lead:
        return [ref[k].astype(F32) for k in range(lead)]
    if splits is None:
        return [ref[...].astype(F32)]
    out, o = [], 0
    for w in splits:
        out.append(ref[..., o:o + w].astype(F32))
        o += w
    return out


def _store_pieces(ref, splits, lead, vals, accumulate=False):
    def put(idx, v):
        if accumulate:
            ref[idx] += v.astype(ref.dtype)
        else:
            ref[idx] = v.astype(ref.dtype)

    if lead:
        for k in range(lead):
            put(k, vals[k])
    elif splits is None:
        put(..., vals[0])
    else:
        o = 0
        for w, v in zip(splits, vals):
            put((..., slice(o, o + w)), v)
            o += w


def rowwise(fn, rows, params, outs, grid, name):
    nr, npar = len(rows), len(params)

    def body(*refs):
        ids = tuple(pl.program_id(a) for a in range(len(grid)))
        vals = []
        for r, ref in zip(rows, refs[:nr]):
            vals += r.pieces(ref)
        pvals = [ref[...].astype(F32) for ref in refs[nr:nr + npar]]
        res = list(fn(ids, *vals, *pvals))
        o = 0
        for spec, ref in zip(outs, refs[nr + npar:]):
            n = spec.n_pieces()
            _store_pieces(ref, spec.splits, spec.lead, res[o:o + n])
            o += n

    nz = len(grid)
    pspecs = [pl.BlockSpec(p.shape, (lambda *ids, _n=p.ndim: (0,) * _n)) for p in params]
    res = pl.pallas_call(
        body, name=name, grid=grid,
        in_specs=[r.spec() for r in rows] + pspecs,
        out_specs=[pl.BlockSpec(o.bshape, o.imap) for o in outs],
        out_shape=[SDS(o.shape, o.dtype) for o in outs],
        compiler_params=_cp(nz),
    )(*[r.arr for r in rows], *params)
    return list(res)


def rowwise_bwd(fn, rows, params, cots, grid, name):
    nr, npar, nc = len(rows), len(params), len(cots)
    drows = [r for r in rows if r.diff]
    nz = len(grid)

    def body(*refs):
        ids = tuple(pl.program_id(a) for a in range(nz))
        row_refs, par_refs = refs[:nr], refs[nr:nr + npar]
        cot_refs = refs[nr + npar:nr + npar + nc]
        drow_refs = refs[nr + npar + nc:nr + npar + nc + len(drows)]
        dpar_refs = refs[nr + npar + nc + len(drows):]
        pieces, is_diff = [], []
        for r, ref in zip(rows, row_refs):
            ps = r.pieces(ref)
            pieces += ps
            is_diff += [r.diff] * len(ps)
        pvals = [ref[...].astype(F32) for ref in par_refs]
        dvals = [p for p, dflag in zip(pieces, is_diff) if dflag]
        nd = len(dvals)

        def f(*args):
            it = iter(args[:nd])
            full = [next(it) if dflag else p for p, dflag in zip(pieces, is_diff)]
            return tuple(fn(ids, *full, *args[nd:]))

        _, vjp = jax.vjp(f, *dvals, *pvals)
        cvals = []
        for c, ref in zip(cots, cot_refs):
            cvals += c.pieces(ref)
        g = vjp(tuple(cvals))
        o = 0
        first_inner = ids[-1] == 0
        for r, ref in zip(drows, drow_refs):
            n = r.n_pieces()
            gs = g[o:o + n]
            o += n
            if r.acc:
                @pl.when(first_inner)
                def _(ref=ref):
                    ref[...] = jnp.zeros_like(ref)
            _store_pieces(ref, r.splits, r.lead, gs, accumulate=r.acc)
        first = functools.reduce(jnp.logical_and, [i == 0 for i in ids])
        for ref, gp in zip(dpar_refs, g[nd:]):
            @pl.when(first)
            def _(ref=ref):
                ref[...] = jnp.zeros_like(ref)
            ref[...] += gp

    pspecs = [pl.BlockSpec(p.shape, (lambda *ids, _n=p.ndim: (0,) * _n)) for p in params]
    res = pl.pallas_call(
        body, name=name, grid=grid,
        in_specs=[r.spec() for r in rows] + pspecs + [c.spec() for c in cots],
        out_specs=[r.gspec() for r in drows] + pspecs,
        out_shape=[SDS(r.gshape, r.gdtype) for r in drows] + [SDS(p.shape, F32) for p in params],
        compiler_params=_cp(nz),
    )(*[r.arr for r in rows], *params, *[c.arr for c in cots])
    res = list(res)
    return res[:len(drows)], res[len(drows):]


def _sigmoid(x):
    return 0.5 * (jnp.tanh(0.5 * x) + 1.0)


def _silu(x):
    return x * _sigmoid(x)


def _normmod(x, gain, sc, sh):
    inv = lax.rsqrt(jnp.mean(x * x, axis=-1, keepdims=True) + EPS)
    return x * inv * gain * (1.0 + sc) + sh


def f_first(ids, x, gain, sc, sh):
    return x, _normmod(x, gain, sc, sh)


def f_resid_norm(ids, x, y, g, gain, sc, sh):
    xn = x + g * y
    return xn, _normmod(xn, gain, sc, sh)


def f_swiglu(ids, gate, up):
    return (_silu(gate) * up,)


def f_loss(ids, x, y, tgt, g):
    out = x + g * y
    e = out - tgt
    part = 0.5 * jnp.sum(e * e, axis=0, keepdims=True) * (1.0 / D)
    return (part,)


def _softplus(x):
    return jnp.maximum(x, 0.0) + jnp.log(1.0 + jnp.exp(-jnp.abs(x)))


def _chunk_tril(T):
    r = lax.broadcasted_iota(jnp.int32, (T, T), 0)
    c = lax.broadcasted_iota(jnp.int32, (T, T), 1)
    return jnp.where((r // GDN_C == c // GDN_C) & (c <= r), 1.0, 0.0).astype(F32)


def _dot_hi(a, b, dims=((1,), (0,))):
    return lax.dot_general(a, b, (dims, ((), ())), precision=lax.Precision.HIGHEST, preferred_element_type=F32)


def _dot_x3(a, b, dims=((1,), (0,))):
    return lax.dot_general(a, b, (dims, ((), ())), precision=lax.Precision.HIGH, preferred_element_type=F32)


def f_gdn_gates(ids, ab, alog, dtb):
    T = ab.shape[0]
    g = -jnp.exp(alog) * _softplus(ab + dtb)
    beta = _sigmoid(ab)
    gcum = _dot_hi(_chunk_tril(T), g)
    row = lax.broadcasted_iota(jnp.int32, (LANES, LANES), 0)
    sel = lambda k: jnp.where(row == k, 1.0, 0.0).astype(F32)
    gcs = [_dot_hi(gcum, sel(h)) for h in range(GDN_H)]
    bts = [_dot_hi(beta, sel(GDN_H + h)) for h in range(GDN_H)]
    return (*gcs, *bts)


def f_gdn_post(ids, *args):
    os_, zs, gain = args[:GDN_H], args[GDN_H:2 * GDN_H], args[2 * GDN_H]
    out = []
    for o, z in zip(os_, zs):
        inv = lax.rsqrt(jnp.mean(o * o, axis=-1, keepdims=True) + EPS)
        out.append(o * inv * gain * _silu(z))
    return tuple(out)


def _qknorm1(x, gain2, scale):
    lane = lax.broadcasted_iota(jnp.int32, x.shape, 1)
    lo = lane < DSW_DH
    x2 = x * x
    s_all = jnp.sum(x2, axis=-1, keepdims=True)
    s_lo = jnp.sum(jnp.where(lo, x2, 0.0), axis=-1, keepdims=True)
    ms = jnp.where(lo, s_lo, s_all - s_lo) * (1.0 / DSW_DH)
    return x * lax.rsqrt(ms + EPS) * (gain2 * scale)


def f_qknorm(ids, *args):
    return tuple(_qknorm1(x, args[-1], 1.0) for x in args[:-1])


def f_qnorm(ids, *args):
    return tuple(_qknorm1(x, args[-1], DSW_DH ** -0.5) for x in args[:-1])


def f_combine(ids, o0, o1, o2, l0, l1, l2):
    m = jnp.maximum(jnp.maximum(l0, l1), l2)
    e0, e1, e2 = jnp.exp(l0 - m), jnp.exp(l1 - m), jnp.exp(l2 - m)
    den = e0 + e1 + e2
    o = (e0 * o0 + e1 * o1 + e2 * o2) / den
    return o, m + jnp.log(den)


GDN_T = 512
HALO = 8


def _conv_pre(xx, w):
    acc = xx * w[3:4, :]
    for j in range(3):
        acc = acc + pltpu.roll(xx, shift=3 - j, axis=0) * w[j:j + 1, :]
    return acc


def _qkv_act(pre, cidx):
    s = _silu(pre)
    r = lax.rsqrt(jnp.sum(s * s, axis=-1, keepdims=True) + EPS)
    scale = jnp.where(cidx < GDN_H, GDN_DK ** -0.5, 1.0).astype(F32)
    return jnp.where(cidx < 2 * GDN_H, s * r * scale, s)


def gdn_pre(proj, conv_w, S):
    nt = S // GDN_T
    hb = GDN_T // HALO

    def body(prev_ref, cur_ref, w_ref, o_ref):
        c, i = pl.program_id(0), pl.program_id(1)
        prev = jnp.where(i > 0, prev_ref[...], 0.0)
        xx = jnp.concatenate([prev, cur_ref[...]], axis=0)
        pre = _conv_pre(xx, w_ref[...])[HALO:]
        o_ref[...] = _qkv_act(pre, c)

    return pl.pallas_call(
        body, name="gdn_pre", grid=(3 * GDN_H, nt),
        in_specs=[pl.BlockSpec((HALO, LANES), lambda c, i: (jnp.maximum(i * hb - 1, 0), c)),
                  pl.BlockSpec((GDN_T, LANES), lambda c, i: (i, c)),
                  pl.BlockSpec((4, LANES), lambda c, i: (0, c))],
        out_specs=pl.BlockSpec((None, None, GDN_T, LANES), lambda c, i: (c // GDN_H, c % GDN_H, i, 0)),
        out_shape=SDS((3, GDN_H, S, LANES), F32),
        compiler_params=_cp(2),
    )(proj, proj, conv_w)


def gdn_pre_bwd(proj, conv_w, dqkv, S):
    nt = S // GDN_T
    hb = GDN_T // HALO
    last_h = S // HALO - 1

    def body(prev_ref, cur_ref, next_ref, w_ref, d_ref, dnext_ref, dx_ref, dw_ref):
        c, i = pl.program_id(0), pl.program_id(1)
        w = w_ref[...]
        prev = jnp.where(i > 0, prev_ref[...], 0.0)
        xx = jnp.concatenate([prev, cur_ref[...], next_ref[...]], axis=0)
        dnext = jnp.where(i < nt - 1, dnext_ref[...], 0.0)
        dd = jnp.concatenate([jnp.zeros((HALO, LANES), F32), d_ref[...], dnext], axis=0)
        pre = _conv_pre(xx, w)
        _, vjp = jax.vjp(lambda p: _qkv_act(p, c), pre)
        (dpre,) = vjp(dd)
        row = lax.broadcasted_iota(jnp.int32, dpre.shape, 0)
        dpre = jnp.where(row >= HALO, dpre, 0.0)
        dx = dpre * w[3:4, :]
        R = dpre.shape[0]
        for j in range(3):
            dx = dx + pltpu.roll(dpre, shift=R - (3 - j), axis=0) * w[j:j + 1, :]
        dx_ref[...] = dx[HALO:HALO + GDN_T].astype(dx_ref.dtype)
        own = jnp.where(row < HALO + GDN_T, dpre, 0.0)
        rows_w = [jnp.sum(own * pltpu.roll(xx, shift=3 - j, axis=0), axis=0, keepdims=True) for j in range(3)]
        rows_w.append(jnp.sum(own * xx, axis=0, keepdims=True))
        r4 = lax.broadcasted_iota(jnp.int32, (4, LANES), 0)
        dw = jnp.zeros((4, LANES), F32)
        for j in range(4):
            dw = dw + jnp.where(r4 == j, rows_w[j], 0.0)

        @pl.when(i == 0)
        def _():
            dw_ref[...] = jnp.zeros_like(dw_ref)

        dw_ref[...] += dw

    dspec = lambda f: pl.BlockSpec((None, None) + f[0], f[1])
    return pl.pallas_call(
        body, name="gdn_pre_bwd", grid=(3 * GDN_H, nt),
        in_specs=[pl.BlockSpec((HALO, LANES), lambda c, i: (jnp.maximum(i * hb - 1, 0), c)),
                  pl.BlockSpec((GDN_T, LANES), lambda c, i: (i, c)),
                  pl.BlockSpec((HALO, LANES), lambda c, i: (jnp.minimum((i + 1) * hb, last_h), c)),
                  pl.BlockSpec((4, LANES), lambda c, i: (0, c)),
                  dspec(((GDN_T, LANES), lambda c, i: (c // GDN_H, c % GDN_H, i, 0))),
                  dspec(((HALO, LANES), lambda c, i: (c // GDN_H, c % GDN_H, jnp.minimum((i + 1) * hb, last_h), 0)))],
        out_specs=[pl.BlockSpec((GDN_T, LANES), lambda c, i: (i, c)),
                   pl.BlockSpec((4, LANES), lambda c, i: (0, c))],
        out_shape=[SDS((S, 3 * GDN_H * LANES), BF16), SDS((4, 3 * GDN_H * LANES), F32)],
        compiler_params=_cp(2),
    )(proj, proj, proj, conv_w, dqkv, dqkv)


_DIMS = {"nn": ((1,), (0,)), "nt": ((1,), (1,)), "tn": ((0,), (0,))}


def _mm_raw(a, b, mode, hi):
    if hi:
        return _dot_hi(a, b, _DIMS[mode])
    return lax.dot_general(a.astype(BF16), b.astype(BF16), (_DIMS[mode], ((), ())), preferred_element_type=F32)


@functools.partial(jax.custom_vjp, nondiff_argnums=(2, 3))
def mm(a, b, mode, hi):
    return _mm_raw(a, b, mode, hi)


def _mm_fwd(a, b, mode, hi):
    return _mm_raw(a, b, mode, hi), (a, b)


def _mm_bwd(mode, hi, res, dc):
    a, b = res
    if mode == "nn":
        da, db = mm(dc, b, "nt", hi), mm(a, dc, "tn", hi)
    elif mode == "nt":
        da, db = mm(dc, b, "nn", hi), mm(dc, a, "tn", hi)
    else:
        da, db = mm(b, dc, "nt", hi), mm(a, dc, "nn", hi)
    return da, db


mm.defvjp(_mm_fwd, _mm_bwd)


TRI_BASE = 8


def _unit_lower_inverse(L):
    n = L.shape[0]
    r = lax.broadcasted_iota(jnp.int32, (n, n), 0)
    c = lax.broadcasted_iota(jnp.int32, (n, n), 1)
    eye = jnp.where(r == c, 1.0, 0.0).astype(F32)
    P = jnp.where(r // TRI_BASE == c // TRI_BASE, -L, 0.0)
    inv = eye + P
    k = 1
    while 2 * k < TRI_BASE:
        P = _dot_x3(P, P)
        inv = inv + _dot_x3(inv, P)
        k *= 2
    b = 2 * TRI_BASE
    while b <= n:
        off = jnp.where((r // b == c // b) & ((r % b) >= b // 2) & ((c % b) < b // 2), L, 0.0)
        inv = inv - _dot_x3(_dot_x3(inv, off), inv)
        b *= 2
    return inv


@jax.custom_vjp
def tri_solve2(L, r1, r2):
    inv = _unit_lower_inverse(L)
    return _dot_x3(inv, r1), _dot_x3(inv, r2)


def _tri_fwd(L, r1, r2):
    inv = _unit_lower_inverse(L)
    s1, s2 = _dot_x3(inv, r1), _dot_x3(inv, r2)
    return (s1, s2), (inv, s1, s2)


def _tri_bwd(res, ds):
    inv, s1, s2 = res
    d1 = _dot_x3(inv, ds[0], _DIMS["tn"])
    d2 = _dot_x3(inv, ds[1], _DIMS["tn"])
    dL = -(_dot_x3(d1, s1, _DIMS["nt"]) + _dot_x3(d2, s2, _DIMS["nt"]))
    return dL, d1, d2


tri_solve2.defvjp(_tri_fwd, _tri_bwd)


def _gdn_chunk(q, k, v, gcb, btb, S):
    C = q.shape[0]
    r = lax.broadcasted_iota(jnp.int32, (C, C), 0)
    c = lax.broadcasted_iota(jnp.int32, (C, C), 1)
    causal, strict = c <= r, c < r
    G = gcb[:, :C]
    decay = jnp.exp(jnp.where(causal, G - G.T, NEG))
    kb, vb = k * btb, v * btb
    L = jnp.where(strict, mm(kb, k, "nt", False) * decay, 0.0)
    eg = jnp.exp(gcb)
    u, w = tri_solve2(L, vb, kb * eg)
    qk = jnp.where(causal, mm(q, k, "nt", False) * decay, 0.0)
    rows = lax.broadcasted_iota(jnp.int32, gcb.shape, 0)
    g_last = jnp.sum(jnp.where(rows == C - 1, gcb, 0.0), axis=0, keepdims=True)
    q_dec = q * eg
    k_dec = k * jnp.exp(g_last - gcb)
    v_new = u - mm(w, S, "nn", False)
    o = mm(q_dec, S, "nn", False) + mm(qk, v_new, "nn", False)
    S_new = S * jnp.exp(g_last) + mm(k_dec, v_new, "tn", False)
    return o, S_new


def gdn_core(qkv, gc, bt, S):
    nchunk = S // GDN_C

    def body(qkv_ref, g_ref, b_ref, o_ref, st_ref, s_scr):
        n = pl.program_id(0)

        @pl.when(n == 0)
        def _():
            s_scr[...] = jnp.zeros_like(s_scr)

        for h in range(GDN_H):
            S_in = s_scr[h]
            st_ref[h] = S_in
            o, S_new = _gdn_chunk(qkv_ref[0, h], qkv_ref[1, h], qkv_ref[2, h], g_ref[h], b_ref[h], S_in)
            o_ref[h] = o
            s_scr[h] = S_new

    blk3 = pl.BlockSpec((3, GDN_H, GDN_C, LANES), lambda n: (0, 0, n, 0))
    hb = pl.BlockSpec((GDN_H, GDN_C, LANES), lambda n: (0, n, 0))
    return pl.pallas_call(
        body, name="gdn_core", grid=(nchunk,),
        in_specs=[blk3, hb, hb],
        out_specs=[hb, pl.BlockSpec((GDN_H, None, GDN_DK, LANES), lambda n: (0, n, 0, 0))],
        out_shape=[SDS((GDN_H, S, LANES), F32), SDS((GDN_H, nchunk, GDN_DK, LANES), F32)],
        scratch_shapes=[pltpu.VMEM((GDN_H, GDN_DK, LANES), F32)],
        compiler_params=_cp(1),
    )(qkv, gc, bt)


def gdn_core_bwd(qkv, gc, bt, states, do, S):
    nchunk = S // GDN_C

    def body(qkv_ref, g_ref, b_ref, st_ref, do_ref, dqkv_ref, dg_ref, db_ref, ds_scr):
        n = pl.program_id(0)

        @pl.when(n == 0)
        def _():
            ds_scr[...] = jnp.zeros_like(ds_scr)

        for h in range(GDN_H):
            _, vjp = jax.vjp(_gdn_chunk, qkv_ref[0, h], qkv_ref[1, h], qkv_ref[2, h], g_ref[h], b_ref[h], st_ref[h])
            dq, dk, dv, dg, db, dS = vjp((do_ref[h], ds_scr[h]))
            dqkv_ref[0, h] = dq
            dqkv_ref[1, h] = dk
            dqkv_ref[2, h] = dv
            dg_ref[h] = dg
            db_ref[h] = db
            ds_scr[h] = dS

    rev = lambda n: nchunk - 1 - n
    blk3 = pl.BlockSpec((3, GDN_H, GDN_C, LANES), lambda n: (0, 0, rev(n), 0))
    hb = pl.BlockSpec((GDN_H, GDN_C, LANES), lambda n: (0, rev(n), 0))
    return pl.pallas_call(
        body, name="gdn_core_bwd", grid=(nchunk,),
        in_specs=[blk3, hb, hb, pl.BlockSpec((GDN_H, None, GDN_DK, LANES), lambda n: (0, rev(n), 0, 0)), hb],
        out_specs=[blk3, hb, hb],
        out_shape=[SDS((3, GDN_H, S, LANES), F32), SDS((GDN_H, S, LANES), F32), SDS((GDN_H, S, LANES), F32)],
        scratch_shapes=[pltpu.VMEM((GDN_H, GDN_DK, LANES), F32)],
        compiler_params=_cp(1),
    )(qkv, gc, bt, states, do)


GDN_MAIN = 4 * GDN_H * LANES
GDN_PROJ = GDN_MAIN + LANES
RT = 256


def gdn_forward(h, w_in, conv_w, alog, dtb, out_gain, w_out):
    S = h.shape[0]
    nt = S // RT
    proj = matmul(h, w_in, "nn", F32, "gdn_in")
    qkv = gdn_pre(proj, conv_w, S)
    ab_row = Row(proj, (RT, LANES), lambda i: (i, GDN_MAIN // LANES), gdtype=BF16, gshape=(S, LANES), gimap=lambda i: (i, 0))
    hm = lambda i: (0, i, 0)
    hv = GDN_H * LANES
    gc, bt = rowwise(f_gdn_gates, [ab_row], [alog, dtb],
                     [Out((GDN_H, S, LANES), F32, (GDN_H, RT, LANES), hm, lead=GDN_H)] * 2, (nt,), "gdn_gates")
    o, states = gdn_core(qkv, gc, bt, S)
    o_row = Row(o, (GDN_H, RT, LANES), hm, lead=GDN_H)
    z_row = Row(proj, (RT, hv), lambda i: (i, 3), splits=[LANES] * GDN_H, gdtype=BF16, gshape=(S, hv), gimap=lambda i: (i, 0))
    (on,) = rowwise(f_gdn_post, [o_row, z_row], [out_gain],
                    [Out((S, hv), BF16, (RT, hv), lambda i: (i, 0), splits=[LANES] * GDN_H)], (nt,), "gdn_post")
    y = matmul(on, w_out, "nn", F32, "gdn_out")
    saved = dict(h=h, proj=proj, qkv=qkv, gc=gc, bt=bt, states=states, o=o, on=on, ab_row=ab_row, o_row=o_row, z_row=z_row)
    return y, saved


def gdn_backward(dy, sv, w_in, conv_w, alog, dtb, out_gain, w_out):
    S = dy.shape[0]
    nt = S // RT
    hm = lambda i: (0, i, 0)
    hv = GDN_H * LANES
    don = matmul(dy, w_out, "nt", F32, "gdn_out_dx")
    d_w_out = matmul(sv["on"], dy, "tn", F32, "gdn_out_dw")
    (do, dz), (d_gain,) = rowwise_bwd(f_gdn_post, [sv["o_row"], sv["z_row"]], [out_gain],
                                      [Row(don, (RT, hv), lambda i: (i, 0), splits=[LANES] * GDN_H)], (nt,), "gdn_post_bwd")
    dqkv, dgc, dbt = gdn_core_bwd(sv["qkv"], sv["gc"], sv["bt"], sv["states"], do, S)
    head_blk = lambda a: Row(a, (GDN_H, RT, LANES), hm, lead=GDN_H)
    (dab,), (d_alog, d_dtb) = rowwise_bwd(f_gdn_gates, [sv["ab_row"]], [alog, dtb], [head_blk(dgc), head_blk(dbt)],
                                          (nt,), "gdn_gates_bwd")
    dqkv_proj, d_conv = gdn_pre_bwd(sv["proj"], conv_w, dqkv, S)
    dproj = jnp.concatenate([dqkv_proj, dz, dab], axis=1)
    d_w_in = matmul(sv["h"], dproj, "tn", F32, "gdn_in_dw")
    dh = matmul(dproj, w_in, "nt", F32, "gdn_in_dx")
    return dh, dict(w_in=d_w_in, conv=d_conv, alog=d_alog, dtb=d_dtb, gain=d_gain, w_out=d_w_out)


QB = DSW_SPAN
N_HP = DSW_HG // LANES
PROJ_BLKS = 3 * 3 * N_HP


def _bucket_maps():
    a = np.arange(QB)[:, None]
    j = np.arange(2 * QB)[None, :]
    dist = QB + a - j
    band = (dist >= 0) & (dist <= DSW_SPAN)
    maps = []
    for _, dil in DSW_GROUPS:
        dd = np.maximum(dist, 0) * dil
        max_exact = REL_BUCKETS // 2
        scaled = np.log(np.maximum(dd, 1).astype(np.float32) / np.float32(max_exact)) / np.float32(math.log(REL_MAX_DIST / max_exact))
        large = max_exact + (scaled * np.float32(REL_BUCKETS - max_exact)).astype(np.int32)
        large = np.minimum(large, REL_BUCKETS - 1)
        maps.append(np.where(dd < max_exact, dd, large).astype(np.int32))
    return np.stack(maps), band


def dsw_bias(rel_bias):
    maps, band = _bucket_maps()
    maps = np.where(band[None], maps, -1).astype(np.int32)

    def body(tab_ref, bk_ref, o_ref):
        gh = pl.program_id(0)
        bk = bk_ref[...]
        acc = jnp.full(bk.shape, NEG, F32)
        for b in range(REL_BUCKETS):
            acc = jnp.where(bk == b, tab_ref[b, gh], acc)
        o_ref[...] = acc

    return pl.pallas_call(
        body, name="dsw_bias", grid=(3 * GDN_H,),
        in_specs=[pl.BlockSpec(memory_space=pltpu.SMEM),
                  pl.BlockSpec((None, QB, 2 * QB), lambda gh: (gh // GDN_H, 0, 0))],
        out_specs=pl.BlockSpec((None, QB, 2 * QB), lambda gh: (gh, 0, 0)),
        out_shape=SDS((3 * GDN_H, QB, 2 * QB), F32),
        compiler_params=_cp(1),
    )(rel_bias, jnp.asarray(maps))


def dsw_bias_grad(dbias):
    maps, band = _bucket_maps()
    maps = np.where(band[None], maps, -1).astype(np.int32)

    def body(d_ref, bk_ref, o_ref):
        bk = bk_ref[...]
        d = d_ref[...]
        rows = lax.broadcasted_iota(jnp.int32, (REL_BUCKETS, LANES), 0)
        acc = jnp.zeros((REL_BUCKETS, LANES), F32)
        for b in range(REL_BUCKETS):
            part = jnp.sum(jnp.where(bk == b, d, 0.0), axis=0, keepdims=True)
            val = jnp.sum(part, axis=1, keepdims=True)
            acc = jnp.where(rows == b, val, acc)
        o_ref[...] = acc

    return pl.pallas_call(
        body, name="dsw_bias_grad", grid=(3 * GDN_H,),
        in_specs=[pl.BlockSpec((None, QB, 2 * QB), lambda gh: (gh, 0, 0)),
                  pl.BlockSpec((None, QB, 2 * QB), lambda gh: (gh // GDN_H, 0, 0))],
        out_specs=pl.BlockSpec((None, REL_BUCKETS, LANES), lambda gh: (gh, 0, 0)),
        out_shape=SDS((3 * GDN_H, REL_BUCKETS, LANES), F32),
        compiler_params=_cp(1),
    )(dbias, jnp.asarray(maps))


def _nt(a, b):
    return lax.dot_general(a, b, (((1,), (1,)), ((), ())), preferred_element_type=F32)


def _tn(a, b):
    return lax.dot_general(a, b, (((0,), (0,)), ((), ())), preferred_element_type=F32)


def dsw_group_fwd(qn, kn, proj, bias, gi, S):
    dil = DSW_GROUPS[gi][1]
    sd = S // dil
    nq = sd // QB
    qv = qn.reshape(sd, dil * 3 * DSW_HG)
    kv = kn.reshape(sd, dil * 3 * DSW_HG)
    pv = proj.reshape(sd, dil * 9 * DSW_HG)
    qk_col = lambda hp, r: r * (3 * N_HP) + gi * N_HP + hp
    v_col = lambda hp, r: r * PROJ_BLKS + 2 * 3 * N_HP + gi * N_HP + hp

    def body(q_ref, kp_ref, kc_ref, vp_ref, vc_ref, b_ref, o_ref, l_ref):
        i = pl.program_id(2)
        q = q_ref[...]
        k2 = jnp.concatenate([kp_ref[...], kc_ref[...]], axis=0)
        v2 = jnp.concatenate([vp_ref[...], vc_ref[...]], axis=0).astype(BF16)
        lane_q = lax.broadcasted_iota(jnp.int32, (QB, LANES), 1) < DSW_DH
        lane_k = lax.broadcasted_iota(jnp.int32, (2 * QB, LANES), 1) < DSW_DH
        col = lax.broadcasted_iota(jnp.int32, (QB, 2 * QB), 1)
        first = jnp.logical_and(i == 0, col < QB)
        o_acc = jnp.zeros((QB, LANES), F32)
        lse_b = jnp.zeros((QB, LANES), F32)
        for hh in range(2):
            mq = lane_q if hh == 0 else jnp.logical_not(lane_q)
            mk = lane_k if hh == 0 else jnp.logical_not(lane_k)
            s = _nt(jnp.where(mq, q, 0).astype(BF16), k2) + b_ref[hh]
            s = jnp.where(first, NEG, s)
            mx = jnp.max(s, axis=1, keepdims=True)
            p = jnp.exp(s - mx)
            l = jnp.sum(p, axis=1, keepdims=True)
            oh = jnp.dot(p.astype(BF16), jnp.where(mk, v2, 0).astype(BF16), preferred_element_type=F32) / l
            o_acc = o_acc + oh
            lse_b = jnp.where(mq, mx + jnp.log(l), lse_b)
        o_ref[...] = o_acc
        l_ref[...] = lse_b

    blk = (QB, LANES)
    out_spec = pl.BlockSpec(blk, lambda hp, r, i: (i, r * N_HP + hp))
    o, lse = pl.pallas_call(
        body, name=f"dsw_fwd_g{gi}", grid=(N_HP, dil, nq),
        in_specs=[pl.BlockSpec(blk, lambda hp, r, i: (i, qk_col(hp, r))),
                  pl.BlockSpec(blk, lambda hp, r, i: (jnp.maximum(i - 1, 0), qk_col(hp, r))),
                  pl.BlockSpec(blk, lambda hp, r, i: (i, qk_col(hp, r))),
                  pl.BlockSpec(blk, lambda hp, r, i: (jnp.maximum(i - 1, 0), v_col(hp, r))),
                  pl.BlockSpec(blk, lambda hp, r, i: (i, v_col(hp, r))),
                  pl.BlockSpec((2, QB, 2 * QB), lambda hp, r, i: (gi * N_HP + hp, 0, 0))],
        out_specs=[out_spec, out_spec],
        out_shape=[SDS((sd, dil * DSW_HG), F32)] * 2,
        compiler_params=_cp(3),
    )(qv, kv, kv, pv, pv, bias)
    return o.reshape(S, DSW_HG), lse.reshape(S, DSW_HG)


def dsw_group_bwd(qn, kn, proj, bias, do, o, lse, gi, S):
    dil = DSW_GROUPS[gi][1]
    sd = S // dil
    nq = sd // QB
    qv = qn.reshape(sd, dil * 3 * DSW_HG)
    kv = kn.reshape(sd, dil * 3 * DSW_HG)
    pv = proj.reshape(sd, dil * 9 * DSW_HG)
    dov = do.reshape(sd, dil * DSW_HG)
    ov = o.reshape(sd, dil * DSW_HG)
    lv = lse.reshape(sd, dil * DSW_HG)
    qk_col = lambda hp, r: r * (3 * N_HP) + gi * N_HP + hp
    v_col = lambda hp, r: r * PROJ_BLKS + 2 * 3 * N_HP + gi * N_HP + hp
    o_col = lambda hp, r: r * N_HP + hp
    cur = lambda i: jnp.minimum(i, nq - 1)
    prev = lambda i: jnp.maximum(jnp.minimum(i, nq - 1) - 1, 0)
    done = lambda i: jnp.maximum(i - 1, 0)

    def body(q_ref, kp_ref, kc_ref, vp_ref, vc_ref, b_ref, do_ref, o_ref, l_ref,
             dq_ref, dk_ref, dv_ref, db_ref, dk_scr, dv_scr):
        r, i = pl.program_id(1), pl.program_id(2)

        @pl.when(jnp.logical_and(r == 0, i == 0))
        def _():
            db_ref[...] = jnp.zeros_like(db_ref)

        @pl.when(i == 0)
        def _():
            dk_scr[...] = jnp.zeros_like(dk_scr)
            dv_scr[...] = jnp.zeros_like(dv_scr)

        @pl.when(i < nq)
        def _():
            q = q_ref[...]
            k2 = jnp.concatenate([kp_ref[...], kc_ref[...]], axis=0)
            v2 = jnp.concatenate([vp_ref[...], vc_ref[...]], axis=0).astype(BF16)
            dout = do_ref[...].astype(F32)
            prod = dout * o_ref[...].astype(F32)
            lse_b = l_ref[...]
            lane_q = lax.broadcasted_iota(jnp.int32, (QB, LANES), 1) < DSW_DH
            col = lax.broadcasted_iota(jnp.int32, (QB, 2 * QB), 1)
            first = jnp.logical_and(i == 0, col < QB)
            dq = jnp.zeros((QB, LANES), F32)
            dk2 = jnp.zeros((2 * QB, LANES), F32)
            dv2 = jnp.zeros((2 * QB, LANES), F32)
            for hh in range(2):
                mq = lane_q if hh == 0 else jnp.logical_not(lane_q)
                qm = jnp.where(mq, q, 0).astype(BF16)
                dom = jnp.where(mq, dout, 0.0).astype(BF16)
                s = _nt(qm, k2) + b_ref[hh]
                s = jnp.where(first, NEG, s)
                lse_h = jnp.max(jnp.where(mq, lse_b, NEG), axis=1, keepdims=True)
                p = jnp.exp(s - lse_h)
                delta = jnp.sum(jnp.where(mq, prod, 0.0), axis=1, keepdims=True)
                dp = _nt(dom, v2)
                ds = p * (dp - delta)
                dsb = ds.astype(BF16)
                dq = dq + jnp.where(mq, jnp.dot(dsb, k2, preferred_element_type=F32), 0.0)
                dk2 = dk2 + _tn(dsb, qm)
                dv2 = dv2 + _tn(p.astype(BF16), dom)
                db_ref[hh] += ds
            dq_ref[...] = dq
            dk_ref[...] = dk_scr[...] + dk2[:QB]
            dv_ref[...] = (dv_scr[...] + dv2[:QB]).astype(dv_ref.dtype)
            dk_scr[...] = dk2[QB:]
            dv_scr[...] = dv2[QB:]

        @pl.when(i == nq)
        def _():
            dk_ref[...] = dk_scr[...]
            dv_ref[...] = dv_scr[...].astype(dv_ref.dtype)

    blk = (QB, LANES)
    dq, dk, dv, dbias = pl.pallas_call(
        body, name=f"dsw_bwd_g{gi}", grid=(N_HP, dil, nq + 1),
        in_specs=[pl.BlockSpec(blk, lambda hp, r, i: (cur(i), qk_col(hp, r))),
                  pl.BlockSpec(blk, lambda hp, r, i: (prev(i), qk_col(hp, r))),
                  pl.BlockSpec(blk, lambda hp, r, i: (cur(i), qk_col(hp, r))),
                  pl.BlockSpec(blk, lambda hp, r, i: (prev(i), v_col(hp, r))),
                  pl.BlockSpec(blk, lambda hp, r, i: (cur(i), v_col(hp, r))),
                  pl.BlockSpec((2, QB, 2 * QB), lambda hp, r, i: (gi * N_HP + hp, 0, 0)),
                  pl.BlockSpec(blk, lambda hp, r, i: (cur(i), o_col(hp, r))),
                  pl.BlockSpec(blk, lambda hp, r, i: (cur(i), o_col(hp, r))),
                  pl.BlockSpec(blk, lambda hp, r, i: (cur(i), o_col(hp, r)))],
        out_specs=[pl.BlockSpec(blk, lambda hp, r, i: (cur(i), o_col(hp, r))),
                   pl.BlockSpec(blk, lambda hp, r, i: (done(i), o_col(hp, r))),
                   pl.BlockSpec(blk, lambda hp, r, i: (done(i), o_col(hp, r))),
                   pl.BlockSpec((2, QB, 2 * QB), lambda hp, r, i: (hp, 0, 0))],
        out_shape=[SDS((sd, dil * DSW_HG), F32), SDS((sd, dil * DSW_HG), F32), SDS((sd, dil * DSW_HG), BF16),
                   SDS((GDN_H, QB, 2 * QB), F32)],
        scratch_shapes=[pltpu.VMEM(blk, F32), pltpu.VMEM(blk, F32)],
        compiler_params=_cp(3),
    )(qv, kv, kv, pv, pv, bias, dov, ov, lv)
    return dq.reshape(S, DSW_HG), dk.reshape(S, DSW_HG), dv.reshape(S, DSW_HG), dbias


def dsw_forward(h, w_in, q_gain2, k_gain2, rel_bias, w_out):
    S = h.shape[0]
    nt = S // RT
    nb = 3 * N_HP
    proj = matmul(h, w_in, "nn", F32, "dsw_in")
    width = nb * LANES
    lanes12 = [LANES] * nb
    (qn,) = rowwise(f_qnorm, [Row(proj, (RT, width), lambda i: (i, 0), splits=lanes12)], [q_gain2],
                    [Out((S, width), BF16, (RT, width), lambda i: (i, 0), splits=lanes12)], (nt,), "dsw_qnorm")
    (kn,) = rowwise(f_qknorm, [Row(proj, (RT, width), lambda i: (i, 1), splits=lanes12)], [k_gain2],
                    [Out((S, width), BF16, (RT, width), lambda i: (i, 0), splits=lanes12)], (nt,), "dsw_knorm")
    bias = dsw_bias(rel_bias)
    os_, ls_ = [], []
    for gi in range(3):
        o, l = dsw_group_fwd(qn, kn, proj, bias, gi, S)
        os_.append(o)
        ls_.append(l)
    full = lambda a: Row(a, (RT, DSW_HG), lambda i: (i, 0))
    o, lse = rowwise(f_combine, [full(a) for a in os_ + ls_], [],
                     [Out((S, DSW_HG), BF16, (RT, DSW_HG), lambda i: (i, 0)), Out((S, DSW_HG), F32, (RT, DSW_HG), lambda i: (i, 0))],
                     (nt,), "dsw_combine")
    y = matmul(o, w_out, "nn", F32, "dsw_out")
    return y, dict(h=h, proj=proj, qn=qn, kn=kn, bias=bias, o=o, lse=lse)


def dsw_backward(dy, sv, w_in, q_gain2, k_gain2, w_out):
    S = dy.shape[0]
    nt = S // RT
    nb = 3 * N_HP
    do = matmul(dy, w_out, "nt", BF16, "dsw_out_dx")
    d_w_out = matmul(sv["o"], dy, "tn", F32, "dsw_out_dw")
    pieces_q, pieces_k, pieces_v, dbs = [], [], [], []
    d_qg = jnp.zeros((1, LANES), F32)
    d_kg = jnp.zeros((1, LANES), F32)
    for gi in range(3):
        dq, dk, dv, db = dsw_group_bwd(sv["qn"], sv["kn"], sv["proj"], sv["bias"], do, sv["o"], sv["lse"], gi, S)
        dbs.append(db)
        pieces_v.append(dv)
        for which, dd in ((0, dq), (1, dk)):
            lanes4 = [LANES] * N_HP
            row = Row(sv["proj"], (RT, DSW_HG), lambda i, _o=which * 3 + gi: (i, _o), splits=lanes4,
                      gdtype=BF16, gshape=(S, DSW_HG), gimap=lambda i: (i, 0))
            fn, gain = (f_qnorm, q_gain2) if which == 0 else (f_qknorm, k_gain2)
            (dx,), (dg,) = rowwise_bwd(fn, [row], [gain], [Row(dd, (RT, DSW_HG), lambda i: (i, 0), splits=lanes4)],
                                       (nt,), f"dsw_norm_bwd_{which}{gi}")
            if which == 0:
                pieces_q.append(dx)
                d_qg = d_qg + dg
            else:
                pieces_k.append(dx)
                d_kg = d_kg + dg
    dproj = jnp.concatenate(pieces_q + pieces_k + pieces_v, axis=1)
    d_w_in = matmul(sv["h"], dproj, "tn", F32, "dsw_in_dw")
    dh = matmul(dproj, w_in, "nt", F32, "dsw_in_dx")
    d_rel = dsw_bias_grad(jnp.concatenate(dbs, axis=0))
    return dh, dict(w_in=d_w_in, q_gain2=d_qg, k_gain2=d_kg, rel=d_rel, w_out=d_w_out)


FT = 128


def ffn_forward(h, w_in, w_out, tag):
    S = h.shape[0]
    gu = matmul(h, w_in, "nn", F32, f"ffn_in_{tag}")
    gu_row = Row(gu, (FT, 2 * FFN), lambda i: (i, 0), splits=[FFN, FFN], gdtype=BF16)
    (a,) = rowwise(f_swiglu, [gu_row], [], [Out((S, FFN), BF16, (FT, FFN), lambda i: (i, 0))], (S // FT,), f"ffn_act_{tag}")
    f = matmul(a, w_out, "nn", F32, f"ffn_out_{tag}")
    return f, dict(h=h, gu_row=gu_row, a=a)


def ffn_backward(df, sv, w_in, w_out, tag):
    S = df.shape[0]
    da = matmul(df, w_out, "nt", BF16, f"ffn_out_dx_{tag}")
    d_w_out = matmul(sv["a"], df, "tn", F32, f"ffn_out_dw_{tag}")
    (dgu,), _ = rowwise_bwd(f_swiglu, [sv["gu_row"]], [], [Row(da, (FT, FFN), lambda i: (i, 0))], (S // FT,), f"ffn_act_bwd_{tag}")
    d_w_in = matmul(sv["h"], dgu, "tn", F32, f"ffn_in_dw_{tag}")
    dh = matmul(dgu, w_in, "nt", F32, f"ffn_in_dx_{tag}")
    return dh, d_w_in, d_w_out


def f_norm_only(ids, x, gain, sc, sh):
    return (_normmod(x, gain, sc, sh),)


def _wide(a, **kw):
    return Row(a, (RT, D), lambda i: (i, 0), **kw)


def _wide_out(S, dtype):
    return Out((S, D), dtype, (RT, D), lambda i: (i, 0))


def adamw(w, g, m, v, name):
    shape = w.shape
    C = shape[-1]
    R = int(np.prod(shape[:-1]))
    w2, g2, m2, v2 = (a.reshape(R, C) for a in (w, g, m, v))
    br = R
    if R > 256:
        br = max(b for b in range(8, 257, 8) if R % b == 0)
    c1 = 1.0 / (1.0 - ADAM_B1 ** ADAM_STEP)
    c2 = 1.0 / (1.0 - ADAM_B2 ** ADAM_STEP)

    def body(w_ref, g_ref, m_ref, v_ref, d_ref, nm_ref, nv_ref):
        gg = g_ref[...]
        mm_ = ADAM_B1 * m_ref[...] + (1.0 - ADAM_B1) * gg
        vv = ADAM_B2 * v_ref[...] + (1.0 - ADAM_B2) * (gg * gg)
        d_ref[...] = -ADAM_LR * ((mm_ * c1) / (jnp.sqrt(vv * c2) + ADAM_EPS) + ADAM_WD * w_ref[...])
        nm_ref[...] = mm_
        nv_ref[...] = vv

    spec = pl.BlockSpec((br, C), lambda i: (i, 0))
    d, nm, nv = pl.pallas_call(
        body, name=name, grid=(R // br,), in_specs=[spec] * 4, out_specs=[spec] * 3,
        out_shape=[SDS((R, C), F32)] * 3, compiler_params=_cp(1),
    )(w2, g2, m2, v2)
    return d.reshape(shape), nm.reshape(shape), nv.reshape(shape)


def _place():
    x, y, c = lax.axis_index("x"), lax.axis_index("y"), lax.axis_index("c")
    chips = [(1 - x, y), (x, 1 - y), (1 - x, 1 - y)]
    return x, y, c, chips


def all_gather_small(blk, name):
    m_per, n = blk.shape

    def body(x_ref, out_ref, send_sems, recv_sems, local_sem):
        x, y, c, chips = _place()
        me, sibling = (x, y, c), (x, y, 1 - c)

        def rows(px, py, pc):
            return out_ref.at[pl.ds((4 * px + 2 * py + pc) * m_per, m_per), :]

        def copy(k, block, to, src=None):
            return pltpu.make_async_remote_copy(
                src_ref=rows(*block) if src is None else src, dst_ref=rows(*block),
                send_sem=send_sems.at[k], recv_sem=recv_sems.at[k], device_id=to, device_id_type=MESH)

        mine = pltpu.make_async_copy(x_ref, rows(*me), local_sem)
        mine.start()
        first = [copy(0, me, sibling, src=x_ref)]
        first += [copy(1 + j, me, (*chip, c), src=x_ref) for j, chip in enumerate(chips)]
        for cp in first:
            cp.start()
        passed = [copy(4 + j, (*chip, c), sibling) for j, chip in enumerate(chips)]
        for j, chip in enumerate(chips):
            copy(1 + j, (*chip, c), me).wait_recv()
            passed[j].start()
        copy(0, sibling, me).wait_recv()
        for j, chip in enumerate(chips):
            copy(4 + j, (*chip, 1 - c), me).wait_recv()
        for cp in first + passed:
            cp.wait_send()
        mine.wait()

    return pl.pallas_call(
        body, name=name, out_shape=SDS((N_DEV * m_per, n), blk.dtype),
        in_specs=[pl.BlockSpec(memory_space=pltpu.VMEM)], out_specs=pl.BlockSpec(memory_space=pltpu.VMEM),
        scratch_shapes=[pltpu.SemaphoreType.DMA((7,)), pltpu.SemaphoreType.DMA((7,)), pltpu.SemaphoreType.DMA],
    )(blk)


def all_gather_shards(wp):
    R, W = wp.shape
    Rh = R // 2

    def body(w_ref, out_ref, send_sems, recv_sems, local_sem):
        x, y, c, chips = _place()
        sibling = (x, y, 1 - c)
        s_me = 2 * x + y

        def half(cc):
            return pl.ds(pl.multiple_of(cc * Rh, 16), Rh)

        def copy(k, src, dst, to):
            return pltpu.make_async_remote_copy(src_ref=src, dst_ref=dst, send_sem=send_sems.at[k], recv_sem=recv_sems.at[k],
                                                device_id=to, device_id_type=MESH)

        mine = pltpu.make_async_copy(w_ref, out_ref.at[s_me], local_sem)
        mine.start()
        sends = [copy(j, w_ref.at[half(c)], out_ref.at[s_me, half(c)], (*chip, c)) for j, chip in enumerate(chips)]
        for cp in sends:
            cp.start()
        passed = []
        for j, (px, py) in enumerate(chips):
            got = out_ref.at[2 * px + py, half(c)]
            copy(j, got, got, (px, py, c)).wait_recv()
            fw = copy(3 + j, got, got, sibling)
            fw.start()
            passed.append(fw)
        for j, (px, py) in enumerate(chips):
            got = out_ref.at[2 * px + py, half(1 - c)]
            copy(3 + j, got, got, sibling).wait_recv()
        for cp in sends + passed:
            cp.wait_send()
        mine.wait()

    return pl.pallas_call(
        body, name="weights_all_gather", out_shape=SDS((N_SHARD, R, W), wp.dtype),
        in_specs=[ANY], out_specs=ANY,
        scratch_shapes=[pltpu.SemaphoreType.DMA((6,)), pltpu.SemaphoreType.DMA((6,)), pltpu.SemaphoreType.DMA],
    )(wp)


def sibling_exchange(send, name):
    def body(s_ref, o_ref, send_sem, recv_sem):
        x, y, c, _ = _place()
        cp = pltpu.make_async_remote_copy(src_ref=s_ref, dst_ref=o_ref, send_sem=send_sem, recv_sem=recv_sem,
                                          device_id=(x, y, 1 - c), device_id_type=MESH)
        cp.start()
        cp.wait()

    return pl.pallas_call(
        body, name=name, out_shape=SDS(send.shape, send.dtype), in_specs=[ANY], out_specs=ANY,
        scratch_shapes=[pltpu.SemaphoreType.DMA, pltpu.SemaphoreType.DMA],
    )(send)


def scatter_to_chips(p):
    _, Rh, W = p.shape

    def body(p_ref, o_ref, send_sems, recv_sems):
        x, y, c, chips = _place()
        cps = []
        for j, (px, py) in enumerate(chips):
            cp = pltpu.make_async_remote_copy(src_ref=p_ref.at[2 * px + py], dst_ref=o_ref.at[j], send_sem=send_sems.at[j],
                                              recv_sem=recv_sems.at[j], device_id=(px, py, c), device_id_type=MESH)
            cp.start()
            cps.append(cp)
        for cp in cps:
            cp.wait()

    return pl.pallas_call(
        body, name="grads_scatter", out_shape=SDS((3, Rh, W), p.dtype), in_specs=[ANY], out_specs=ANY,
        scratch_shapes=[pltpu.SemaphoreType.DMA((3,)), pltpu.SemaphoreType.DMA((3,))],
    )(p)


def add_rows(arrs, out_dtype, name, rt=256):
    Rr, W = arrs[0].shape

    def fn(ids, *vals):
        acc = vals[0]
        for v in vals[1:]:
            acc = acc + v
        return (acc,)

    t = rt if Rr % rt == 0 else max(b for b in range(16, rt + 1, 16) if Rr % b == 0)
    (out,) = rowwise(fn, [Row(a, (t, W), lambda i: (i, 0)) for a in arrs], [],
                     [Out((Rr, W), out_dtype, (t, W), lambda i: (i, 0))], (Rr // t,), name)
    return out


PACK = (("gdn_w_in", 2), ("gdn_w_out", 1), ("w_ffn_in", 2), ("w_ffn_out", 1), ("dsw_w_in", 2), ("dsw_w_out", 2))
PACK_ALIGN = 32


def _pack_rows(sizes):
    total = sum(sizes)
    rows = -(-total // D)
    return -(-rows // PACK_ALIGN) * PACK_ALIGN


def pack_blocks(blocks, dtype):
    flat = [b.astype(dtype).reshape(-1) for b in blocks]
    total = sum(f.shape[0] for f in flat)
    R = _pack_rows([f.shape[0] for f in flat])
    flat.append(jnp.zeros((R * D - total,), dtype))
    return jnp.concatenate(flat).reshape(R, D)


def unpack_blocks(buf, shapes):
    flat = buf.reshape(-1)
    out, off = [], 0
    for shp in shapes:
        n = int(np.prod(shp))
        out.append(flat[off:off + n].reshape(shp))
        off += n
    return out


def _shard_slice(a, axis, s):
    n = a.shape[axis] // N_SHARD
    return lax.slice_in_dim(a, s * n, (s + 1) * n, axis=axis)


def _pad_lanes(v):
    return jnp.concatenate([v.astype(F32), jnp.zeros((LANES - v.shape[0],), F32)])[None]


def kernel(x, c, w_ada, b_ada, norm_mix, norm_ffn, w_ffn_in, w_ffn_out, gdn_w_in, gdn_conv, gdn_a_log, gdn_dt_bias, gdn_out_norm, gdn_w_out, dsw_w_in, dsw_q_norm, dsw_k_norm, dsw_w_out, rel_bias, loss_target, m_w_ada, m_b_ada, m_norm_mix, m_norm_ffn, m_w_ffn_in, m_w_ffn_out, m_gdn_w_in, m_gdn_conv, m_gdn_a_log, m_gdn_dt_bias, m_gdn_out_norm, m_gdn_w_out, m_dsw_w_in, m_dsw_q_norm, m_dsw_k_norm, m_dsw_w_out, m_rel_bias, v_w_ada, v_b_ada, v_norm_mix, v_norm_ffn, v_w_ffn_in, v_w_ffn_out, v_gdn_w_in, v_gdn_conv, v_gdn_a_log, v_gdn_dt_bias, v_gdn_out_norm, v_gdn_w_out, v_dsw_w_in, v_dsw_q_norm, v_dsw_k_norm, v_dsw_w_out, v_rel_bias):
    S = x.shape[1]
    nt = S // RT
    xi, yi, ci = lax.axis_index("x"), lax.axis_index("y"), lax.axis_index("c")
    me = 4 * xi + 2 * yi + ci
    s_me = 2 * xi + yi
    x0, tgt = x[0], loss_target[0]
    shard = dict(w_ffn_in=w_ffn_in, w_ffn_out=w_ffn_out, gdn_w_in=gdn_w_in, gdn_w_out=gdn_w_out, dsw_w_in=dsw_w_in, dsw_w_out=dsw_w_out)

    whole = lambda a: Row(a, a.shape, lambda i: (0,) * a.ndim)
    (cond8,) = rowwise(lambda ids, v: (_silu(v),), [whole(c.reshape(8, LANES))], [], [Out((8, LANES), F32, (8, LANES), lambda i: (0, 0))], (1,), "cond")
    cond_all = all_gather_small(cond8, "gather_cond").reshape(N_DEV, D)
    cond16 = jnp.concatenate([cond_all, jnp.zeros((8, D), F32)], axis=0)
    ada_cols = w_ada.shape[2]
    mods = [matmul(cond16, w_ada[l], "nn", F32, f"ada_{l}")[:N_DEV] for l in range(2)]
    buf = jnp.concatenate([jnp.stack(mods, axis=1).reshape(-1, LANES), gdn_conv.reshape(-1, LANES)], axis=0)
    n_mod_rows = N_DEV * 2 * ada_cols // LANES
    got = all_gather_small(buf, "gather_mod").reshape(N_DEV, buf.shape[0], LANES)
    mod_parts, conv_parts = [], []
    for s in range(N_SHARD):
        from_dev = got[2 * s]
        mod_parts.append(lax.dynamic_index_in_dim(from_dev[:n_mod_rows].reshape(N_DEV, 2, ada_cols), me, 0, keepdims=False))
        conv_parts.append(from_dev[n_mod_rows:].reshape(4, -1))
    mod_nb = jnp.concatenate(mod_parts, axis=1)
    conv_w = jnp.concatenate(conv_parts, axis=1)
    (mod,) = rowwise(lambda ids, a, b: (a + b,), [whole(mod_nb), whole(b_ada)], [], [Out(mod_nb.shape, F32, mod_nb.shape, lambda i: (0, 0))], (1,), "mod_bias")
    mod = mod.reshape(2, 6, 1, D)
    sh1, sc1, g1, sh2, sc2, g2 = ([mod[l, k] for l in range(2)] for k in range(6))
    gmix = [norm_mix[l][None] for l in range(2)]
    gffn = [norm_ffn[l][None] for l in range(2)]

    wp = pack_blocks([shard[n] for n, _ in PACK], BF16)
    wall = all_gather_shards(wp)
    shapes = [shard[n].shape for n, _ in PACK]
    per_shard = [unpack_blocks(wall[s], shapes) for s in range(N_SHARD)]
    W = {n: jnp.concatenate([per_shard[s][k] for s in range(N_SHARD)], axis=ax) for k, (n, ax) in enumerate(PACK)}
    gw = W["gdn_w_in"][0]
    w_gdn = jnp.concatenate([gw, jnp.zeros((D, GDN_PROJ - gw.shape[1]), BF16)], axis=1)
    alog, dtb = _pad_lanes(gdn_a_log[0]), _pad_lanes(gdn_dt_bias[0])
    qg2 = jnp.concatenate([dsw_q_norm, dsw_q_norm], axis=1)
    kg2 = jnp.concatenate([dsw_k_norm, dsw_k_norm], axis=1)
    gdn_args = (w_gdn, conv_w, alog, dtb, gdn_out_norm, W["gdn_w_out"][0])
    dsw_args = (W["dsw_w_in"][0], qg2, kg2)

    (h10,) = rowwise(f_norm_only, [_wide(x0)], [gmix[0], sc1[0], sh1[0]], [_wide_out(S, BF16)], (nt,), "l0_norm")
    y0, sv_g = gdn_forward(h10, *gdn_args)
    x1, h20 = rowwise(f_resid_norm, [_wide(x0), _wide(y0)], [g1[0], gffn[0], sc2[0], sh2[0]], [_wide_out(S, F32), _wide_out(S, BF16)], (nt,), "l0_mid")
    f0, sv_f0 = ffn_forward(h20, W["w_ffn_in"][0], W["w_ffn_out"][0], "0")
    x2, h11 = rowwise(f_resid_norm, [_wide(x1), _wide(f0)], [g2[0], gmix[1], sc1[1], sh1[1]], [_wide_out(S, F32), _wide_out(S, BF16)], (nt,), "l1_in")
    y1, sv_d = dsw_forward(h11, *dsw_args, rel_bias, W["dsw_w_out"][0])
    x3, h21 = rowwise(f_resid_norm, [_wide(x2), _wide(y1)], [g1[1], gffn[1], sc2[1], sh2[1]], [_wide_out(S, F32), _wide_out(S, BF16)], (nt,), "l1_mid")
    f1, sv_f1 = ffn_forward(h21, W["w_ffn_in"][1], W["w_ffn_out"][1], "1")
    part_spec = lambda a: Row(a, (None, 1, D), lambda i: (i, 0, 0))
    (parts,) = rowwise(f_loss, [_wide(x3), _wide(f1), _wide(tgt)], [g2[1]], [Out((nt, 1, D), F32, (None, 1, D), lambda i: (i, 0, 0))], (nt,), "loss")
    loss = lax.psum(jnp.sum(parts), ("x", "y", "c"))

    (dx3, df1), (dg2_1,) = rowwise_bwd(f_loss, [_wide(x3), _wide(f1, gdtype=BF16), _wide(tgt, diff=False)], [g2[1]],
                                       [part_spec(jnp.ones((nt, 1, D), F32))], (nt,), "loss_bwd")
    dh21, d_win1, d_wout1 = ffn_backward(df1, sv_f1, W["w_ffn_in"][1], W["w_ffn_out"][1], "1")
    (dx2, dy1), (dg1_1, dgf1, dsc2_1, dsh2_1) = rowwise_bwd(
        f_resid_norm, [_wide(x2), _wide(y1, gdtype=BF16)], [g1[1], gffn[1], sc2[1], sh2[1]], [_wide(dx3), _wide(dh21)], (nt,), "l1_mid_bwd")
    dh11, g_d = dsw_backward(dy1, sv_d, *dsw_args, W["dsw_w_out"][0])
    (dx1, df0), (dg2_0, dgm1, dsc1_1, dsh1_1) = rowwise_bwd(
        f_resid_norm, [_wide(x1), _wide(f0, gdtype=BF16)], [g2[0], gmix[1], sc1[1], sh1[1]], [_wide(dx2), _wide(dh11)], (nt,), "l1_in_bwd")
    dh20, d_win0, d_wout0 = ffn_backward(df0, sv_f0, W["w_ffn_in"][0], W["w_ffn_out"][0], "0")
    (dx0p, dy0), (dg1_0, dgf0, dsc2_0, dsh2_0) = rowwise_bwd(
        f_resid_norm, [_wide(x0), _wide(y0, gdtype=BF16)], [g1[0], gffn[0], sc2[0], sh2[0]], [_wide(dx1), _wide(dh20)], (nt,), "l0_mid_bwd")
    dh10, g_g = gdn_backward(dy0, sv_g, *gdn_args)
    (grad_x,), (dgm0, dsc1_0, dsh1_0) = rowwise_bwd(f_first, [_wide(x0)], [gmix[0], sc1[0], sh1[0]], [_wide(dx0p), _wide(dh10)], (nt,), "l0_norm_bwd")

    dmod = jnp.concatenate([dsh1_0, dsc1_0, dg1_0, dsh2_0, dsc2_0, dg2_0, dsh1_1, dsc1_1, dg1_1, dsh2_1, dsc2_1, dg2_1], axis=1)
    d_rel = jnp.transpose(g_d["rel"][:, :, 0])
    fold = lambda v: v[:, :DSW_DH] + v[:, DSW_DH:]
    small = [dmod, jnp.concatenate([dgm0, dgm1], axis=1), jnp.concatenate([dgf0, dgf1], axis=1), g_g["conv"].reshape(1, -1),
             g_g["alog"], g_g["dtb"], g_g["gain"], _pad_lanes(fold(g_d["q_gain2"])[0]), _pad_lanes(fold(g_d["k_gain2"])[0]),
             d_rel.reshape(1, -1)]
    sizes = [v.shape[1] // LANES for v in small]
    n_rows = sum(sizes)
    pad_rows = -(-n_rows // 8) * 8
    sbuf = jnp.concatenate([v.reshape(-1, LANES) for v in small] + [jnp.zeros((pad_rows - n_rows, LANES), F32)], axis=0)
    sgot = all_gather_small(sbuf, "gather_small_grads")
    ssum = add_rows([sgot[d * pad_rows:(d + 1) * pad_rows] for d in range(N_DEV)], F32, "sum_small_grads", rt=pad_rows)
    offs = np.cumsum([0] + sizes)
    take = lambda k: ssum[offs[k]:offs[k + 1]].reshape(1, -1)
    grad_b_ada = take(0).reshape(2, 6 * D)
    grad_norm_mix = take(1).reshape(2, D)
    grad_norm_ffn = take(2).reshape(2, D)
    conv_full = take(3).reshape(4, -1)
    ncv = gdn_conv.shape[2]
    grad_gdn_conv = lax.dynamic_slice_in_dim(conv_full, s_me * ncv, ncv, axis=1)[None]
    grad_a_log = take(4)[:, :GDN_H]
    grad_dt_bias = take(5)[:, :GDN_H]
    grad_out_norm = take(6)
    grad_q_norm = take(7)[:, :DSW_DH]
    grad_k_norm = take(8)[:, :DSW_DH]
    grad_rel = take(9).reshape(REL_BUCKETS, 3 * GDN_H)
    dmod_all = sgot.reshape(N_DEV, pad_rows, LANES)[:, :sizes[0]].reshape(N_DEV, 2, 6 * D)
    dmod_mine = lax.dynamic_slice_in_dim(dmod_all, s_me * ada_cols, ada_cols, axis=2)
    dmod16 = jnp.concatenate([dmod_mine, jnp.zeros_like(dmod_mine)], axis=0)
    grad_w_ada = jnp.stack([matmul(cond16, dmod16[:, l], "tn", F32, f"ada_dw_{l}") for l in range(2)])

    full = {"gdn_w_in": g_g["w_in"][:, :gdn_w_in.shape[2] * N_SHARD][None], "gdn_w_out": g_g["w_out"][None],
            "w_ffn_in": jnp.stack([d_win0, d_win1]), "w_ffn_out": jnp.stack([d_wout0, d_wout1]),
            "dsw_w_in": g_d["w_in"][None], "dsw_w_out": g_d["w_out"][None]}
    g32 = jnp.stack([pack_blocks([_shard_slice(full[n], ax, s) for n, ax in PACK], F32) for s in range(N_SHARD)])
    R = g32.shape[1]
    Rh = R // 2
    keep = lax.dynamic_slice_in_dim(g32, ci * Rh, Rh, axis=1)
    give = lax.dynamic_slice_in_dim(g32, (1 - ci) * Rh, Rh, axis=1).astype(BF16)
    from_sib = sibling_exchange(give, "grads_to_sibling")
    part = add_rows([keep.reshape(N_SHARD * Rh, D), from_sib.reshape(N_SHARD * Rh, D)], BF16, "grads_chip_sum").reshape(N_SHARD, Rh, D)
    others = scatter_to_chips(part)
    own = lax.dynamic_index_in_dim(part, s_me, 0, keepdims=False)
    half_sum = add_rows([own, others[0], others[1], others[2]], F32, "grads_sum")
    sib_half = sibling_exchange(half_sum, "grads_from_sibling")
    lo = jnp.where(ci == 0, half_sum, sib_half)
    hi = jnp.where(ci == 0, sib_half, half_sum)
    gsh = dict(zip([n for n, _ in PACK], unpack_blocks(jnp.concatenate([lo, hi], axis=0), shapes)))

    grads = dict(w_ada=grad_w_ada, b_ada=grad_b_ada, norm_mix=grad_norm_mix, norm_ffn=grad_norm_ffn, w_ffn_in=gsh["w_ffn_in"],
                 w_ffn_out=gsh["w_ffn_out"], gdn_w_in=gsh["gdn_w_in"], gdn_conv=grad_gdn_conv, gdn_a_log=grad_a_log,
                 gdn_dt_bias=grad_dt_bias, gdn_out_norm=grad_out_norm, gdn_w_out=gsh["gdn_w_out"], dsw_w_in=gsh["dsw_w_in"],
                 dsw_q_norm=grad_q_norm, dsw_k_norm=grad_k_norm, dsw_w_out=gsh["dsw_w_out"], rel_bias=grad_rel)
    weights = dict(w_ada=w_ada, b_ada=b_ada, norm_mix=norm_mix, norm_ffn=norm_ffn, w_ffn_in=w_ffn_in, w_ffn_out=w_ffn_out,
                   gdn_w_in=gdn_w_in, gdn_conv=gdn_conv, gdn_a_log=gdn_a_log, gdn_dt_bias=gdn_dt_bias, gdn_out_norm=gdn_out_norm,
                   gdn_w_out=gdn_w_out, dsw_w_in=dsw_w_in, dsw_q_norm=dsw_q_norm, dsw_k_norm=dsw_k_norm, dsw_w_out=dsw_w_out,
                   rel_bias=rel_bias)
    ms = dict(w_ada=m_w_ada, b_ada=m_b_ada, norm_mix=m_norm_mix, norm_ffn=m_norm_ffn, w_ffn_in=m_w_ffn_in, w_ffn_out=m_w_ffn_out,
              gdn_w_in=m_gdn_w_in, gdn_conv=m_gdn_conv, gdn_a_log=m_gdn_a_log, gdn_dt_bias=m_gdn_dt_bias, gdn_out_norm=m_gdn_out_norm,
              gdn_w_out=m_gdn_w_out, dsw_w_in=m_dsw_w_in, dsw_q_norm=m_dsw_q_norm, dsw_k_norm=m_dsw_k_norm, dsw_w_out=m_dsw_w_out,
              rel_bias=m_rel_bias)
    vs = dict(w_ada=v_w_ada, b_ada=v_b_ada, norm_mix=v_norm_mix, norm_ffn=v_norm_ffn, w_ffn_in=v_w_ffn_in, w_ffn_out=v_w_ffn_out,
              gdn_w_in=v_gdn_w_in, gdn_conv=v_gdn_conv, gdn_a_log=v_gdn_a_log, gdn_dt_bias=v_gdn_dt_bias, gdn_out_norm=v_gdn_out_norm,
              gdn_w_out=v_gdn_w_out, dsw_w_in=v_dsw_w_in, dsw_q_norm=v_dsw_q_norm, dsw_k_norm=v_dsw_k_norm, dsw_w_out=v_dsw_w_out,
              rel_bias=v_rel_bias)
    names = list(weights)
    deltas, new_m, new_v = [], [], []
    for n in names:
        g = grads[n].reshape(weights[n].shape)
        grads[n] = g
        d, nm, nv = adamw(weights[n], g, ms[n], vs[n], f"adamw_{n}")
        deltas.append(d)
        new_m.append(nm)
        new_v.append(nv)
    return (loss, grad_x[None], *[grads[n] for n in names], *deltas, *new_m, *new_v)
```

```python
import functools
import math

import numpy as np
import jax
import jax.numpy as jnp
from jax import lax
from jax.experimental import pallas as pl
from jax.experimental.pallas import tpu as pltpu

F32 = jnp.float32
BF16 = jnp.bfloat16
SDS = jax.ShapeDtypeStruct
MESH = pl.DeviceIdType.MESH
ANY = pl.BlockSpec(memory_space=pl.ANY)

D = 1024
EPS = 1e-6
LANES = 128
GDN_H = 8
GDN_DK = 128
GDN_C = 64
DSW_GROUPS = ((128, 1), (512, 4), (2048, 16))
DSW_SPAN = 128
DSW_DH = 64
DSW_HG = 512
REL_BUCKETS = 32
REL_MAX_DIST = 2048
FFN = 2816
N_SHARD = 4
N_DEV = 8
VMEM_LIMIT = 48 * 1024 * 1024
NEG = -1e30

ADAM_LR, ADAM_B1, ADAM_B2, ADAM_EPS, ADAM_WD, ADAM_STEP = 0.001, 0.9, 0.999, 1e-08, 0.01, 10


def _cp(n_axes):
    return pltpu.CompilerParams(dimension_semantics=("arbitrary",) * n_axes, vmem_limit_bytes=VMEM_LIMIT)


def _blk(dim, cap):
    if dim <= cap:
        return dim
    best = None
    for b in range(LANES, cap + 1, LANES):
        if dim % b == 0:
            best = b
    assert best is not None, (dim, cap)
    return best


def matmul(a, b, mode, out_dtype, name, cap_m=1024, cap_n=1024, cap_k=2048):
    if mode == "nn":
        (M, K), (K2, N) = a.shape, b.shape
    elif mode == "nt":
        (M, K), (N, K2) = a.shape, b.shape
    else:
        (K, M), (K2, N) = a.shape, b.shape
    assert K == K2, (a.shape, b.shape, mode)
    if K <= 3072:
        cap_k = K
        if K > 2048:
            cap_n = 512
    bm, bn, bk = _blk(M, cap_m), _blk(N, cap_n), _blk(K, cap_k)
    nk = K // bk
    dims = {"nn": ((1,), (0,)), "nt": ((1,), (1,)), "tn": ((0,), (0,))}[mode]

    def dot(a_ref, b_ref):
        return lax.dot_general(a_ref[...].astype(BF16), b_ref[...].astype(BF16), (dims, ((), ())), preferred_element_type=F32)

    def body_one(a_ref, b_ref, o_ref):
        o_ref[...] = dot(a_ref, b_ref).astype(o_ref.dtype)

    def body_acc(a_ref, b_ref, o_ref, acc_ref):
        k = pl.program_id(2)

        @pl.when(k == 0)
        def _():
            acc_ref[...] = jnp.zeros_like(acc_ref)

        acc_ref[...] += dot(a_ref, b_ref)

        @pl.when(k == nk - 1)
        def _():
            o_ref[...] = acc_ref[...].astype(o_ref.dtype)

    a_spec = pl.BlockSpec((bk, bm), lambda i, j, k: (k, i)) if mode == "tn" else pl.BlockSpec((bm, bk), lambda i, j, k: (i, k))
    b_spec = pl.BlockSpec((bn, bk), lambda i, j, k: (j, k)) if mode == "nt" else pl.BlockSpec((bk, bn), lambda i, j, k: (k, j))
    return pl.pallas_call(
        body_one if nk == 1 else body_acc, name=name, grid=(M // bm, N // bn, nk),
        in_specs=[a_spec, b_spec], out_specs=pl.BlockSpec((bm, bn), lambda i, j, k: (i, j)),
        out_shape=SDS((M, N), out_dtype), scratch_shapes=[] if nk == 1 else [pltpu.VMEM((bm, bn), F32)],
        compiler_params=_cp(3),
    )(a, b)


class Row:
    def __init__(self, arr, bshape, imap, splits=None, diff=True, acc=False, gdtype=F32, gshape=None, gbshape=None, gimap=None,
                 lead=0):
        self.arr, self.bshape, self.imap = arr, tuple(bshape), imap
        self.splits, self.lead = splits, lead
        self.diff, self.acc, self.gdtype = diff, acc, gdtype
        self.gshape = tuple(arr.shape) if gshape is None else tuple(gshape)
        self.gbshape = self.bshape if gbshape is None else tuple(gbshape)
        self.gimap = imap if gimap is None else gimap

    def gspec(self):
        return pl.BlockSpec(self.gbshape, self.gimap)

    def spec(self):
        return pl.BlockSpec(self.bshape, self.imap)

    def pieces(self, ref):
        return _load_pieces(ref, self.splits, self.lead)

    def n_pieces(self):
        return _n_pieces(self.splits, self.lead)


class Out:
    def __init__(self, shape, dtype, bshape, imap, splits=None, lead=0):
        self.shape, self.dtype, self.bshape, self.imap = tuple(shape), dtype, tuple(bshape), imap
        self.splits, self.lead = splits, lead

    def n_pieces(self):
        return _n_pieces(self.splits, self.lead)


def _n_pieces(splits, lead):
    return lead if lead else (1 if splits is None else len(splits))


def _load_pieces(ref, splits, lead):
    if lead:
        return [ref[k].astype(F32) for k in range(lead)]
    if splits is None:
        return [ref[...].astype(F32)]
    out, o = [], 0
    for w in splits:
        out.append(ref[..., o:o + w].astype(F32))
        o += w
    return out


def _store_pieces(ref, splits, lead, vals, accumulate=False):
    def put(idx, v):
        if accumulate:
            ref[idx] += v.astype(ref.dtype)
        else:
            ref[idx] = v.astype(ref.dtype)

    if lead:
        for k in range(lead):
            put(k, vals[k])
    elif splits is None:
        put(..., vals[0])
    else:
        o = 0
        for w, v in zip(splits, vals):
            put((..., slice(o, o + w)), v)
            o += w


def rowwise(fn, rows, params, outs, grid, name):
    nr, npar = len(rows), len(params)

    def body(*refs):
        ids = tuple(pl.program_id(a) for a in range(len(grid)))
        vals = []
        for r, ref in zip(rows, refs[:nr]):
            vals += r.pieces(ref)
        pvals = [ref[...].astype(F32) for ref in refs[nr:nr + npar]]
        res = list(fn(ids, *vals, *pvals))
        o = 0
        for spec, ref in zip(outs, refs[nr + npar:]):
            n = spec.n_pieces()
            _store_pieces(ref, spec.splits, spec.lead, res[o:o + n])
            o += n

    nz = len(grid)
    pspecs = [pl.BlockSpec(p.shape, (lambda *ids, _n=p.ndim: (0,) * _n)) for p in params]
    res = pl.pallas_call(
        body, name=name, grid=grid,
        in_specs=[r.spec() for r in rows] + pspecs,
        out_specs=[pl.BlockSpec(o.bshape, o.imap) for o in outs],
        out_shape=[SDS(o.shape, o.dtype) for o in outs],
        compiler_params=_cp(nz),
    )(*[r.arr for r in rows], *params)
    return list(res)


def rowwise_bwd(fn, rows, params, cots, grid, name):
    nr, npar, nc = len(rows), len(params), len(cots)
    drows = [r for r in rows if r.diff]
    nz = len(grid)

    def body(*refs):
        ids = tuple(pl.program_id(a) for a in range(nz))
        row_refs, par_refs = refs[:nr], refs[nr:nr + npar]
        cot_refs = refs[nr + npar:nr + npar + nc]
        drow_refs = refs[nr + npar + nc:nr + npar + nc + len(drows)]
        dpar_refs = refs[nr + npar + nc + len(drows):]
        pieces, is_diff = [], []
        for r, ref in zip(rows, row_refs):
            ps = r.pieces(ref)
            pieces += ps
            is_diff += [r.diff] * len(ps)
        pvals = [ref[...].astype(F32) for ref in par_refs]
        dvals = [p for p, dflag in zip(pieces, is_diff) if dflag]
        nd = len(dvals)

        def f(*args):
            it = iter(args[:nd])
            full = [next(it) if dflag else p for p, dflag in zip(pieces, is_diff)]
            return tuple(fn(ids, *full, *args[nd:]))

        _, vjp = jax.vjp(f, *dvals, *pvals)
        cvals = []
        for c, ref in zip(cots, cot_refs):
            cvals += c.pieces(ref)
        g = vjp(tuple(cvals))
        o = 0
        first_inner = ids[-1] == 0
        for r, ref in zip(drows, drow_refs):
            n = r.n_pieces()
            gs = g[o:o + n]
            o += n
            if r.acc:
                @pl.when(first_inner)
                def _(ref=ref):
                    ref[...] = jnp.zeros_like(ref)
            _store_pieces(ref, r.splits, r.lead, gs, accumulate=r.acc)
        first = functools.reduce(jnp.logical_and, [i == 0 for i in ids])
        for ref, gp in zip(dpar_refs, g[nd:]):
            @pl.when(first)
            def _(ref=ref):
                ref[...] = jnp.zeros_like(ref)
            ref[...] += gp

    pspecs = [pl.BlockSpec(p.shape, (lambda *ids, _n=p.ndim: (0,) * _n)) for p in params]
    res = pl.pallas_call(
        body, name=name, grid=grid,
        in_specs=[r.spec() for r in rows] + pspecs + [c.spec() for c in cots],
        out_specs=[r.gspec() for r in drows] + pspecs,
        out_shape=[SDS(r.gshape, r.gdtype) for r in drows] + [SDS(p.shape, F32) for p in params],
        compiler_params=_cp(nz),
    )(*[r.arr for r in rows], *params, *[c.arr for c in cots])
    res = list(res)
    return res[:len(drows)], res[len(drows):]


def _sigmoid(x):
    return 0.5 * (jnp.tanh(0.5 * x) + 1.0)


def _silu(x):
    return x * _sigmoid(x)


def _normmod(x, gain, sc, sh):
    inv = lax.rsqrt(jnp.mean(x * x, axis=-1, keepdims=True) + EPS)
    return x * inv * gain * (1.0 + sc) + sh


def f_first(ids, x, gain, sc, sh):
    return x, _normmod(x, gain, sc, sh)


def f_resid_norm(ids, x, y, g, gain, sc, sh):
    xn = x + g * y
    return xn, _normmod(xn, gain, sc, sh)


def f_swiglu(ids, gate, up):
    return (_silu(gate) * up,)


def f_loss(ids, x, y, tgt, g):
    out = x + g * y
    e = out - tgt
    part = 0.5 * jnp.sum(e * e, axis=0, keepdims=True) * (1.0 / D)
    return (part,)


def _softplus(x):
    return jnp.maximum(x, 0.0) + jnp.log(1.0 + jnp.exp(-jnp.abs(x)))


def _chunk_tril(T):
    r = lax.broadcasted_iota(jnp.int32, (T, T), 0)
    c = lax.broadcasted_iota(jnp.int32, (T, T), 1)
    return jnp.where((r // GDN_C == c // GDN_C) & (c <= r), 1.0, 0.0).astype(F32)


def _dot_hi(a, b, dims=((1,), (0,))):
    return lax.dot_general(a, b, (dims, ((), ())), precision=lax.Precision.HIGHEST, preferred_element_type=F32)


def _dot_x3(a, b, dims=((1,), (0,))):
    return lax.dot_general(a, b, (dims, ((), ())), precision=lax.Precision.HIGH, preferred_element_type=F32)


def f_gdn_gates(ids, ab, alog, dtb):
    T = ab.shape[0]
    g = -jnp.exp(alog) * _softplus(ab + dtb)
    beta = _sigmoid(ab)
    gcum = _dot_hi(_chunk_tril(T), g)
    row = lax.broadcasted_iota(jnp.int32, (LANES, LANES), 0)
    sel = lambda k: jnp.where(row == k, 1.0, 0.0).astype(F32)
    gcs = [_dot_hi(gcum, sel(h)) for h in range(GDN_H)]
    bts = [_dot_hi(beta, sel(GDN_H + h)) for h in range(GDN_H)]
    return (*gcs, *bts)


def f_gdn_post(ids, *args):
    os_, zs, gain = args[:GDN_H], args[GDN_H:2 * GDN_H], args[2 * GDN_H]
    out = []
    for o, z in zip(os_, zs):
        inv = lax.rsqrt(jnp.mean(o * o, axis=-1, keepdims=True) + EPS)
        out.append(o * inv * gain * _silu(z))
    return tuple(out)


def _qknorm1(x, gain2, scale):
    lane = lax.broadcasted_iota(jnp.int32, x.shape, 1)
    lo = lane < DSW_DH
    x2 = x * x
    s_all = jnp.sum(x2, axis=-1, keepdims=True)
    s_lo = jnp.sum(jnp.where(lo, x2, 0.0), axis=-1, keepdims=True)
    ms = jnp.where(lo, s_lo, s_all - s_lo) * (1.0 / DSW_DH)
    return x * lax.rsqrt(ms + EPS) * (gain2 * scale)


def f_qknorm(ids, *args):
    return tuple(_qknorm1(x, args[-1], 1.0) for x in args[:-1])


def f_qnorm(ids, *args):
    return tuple(_qknorm1(x, args[-1], DSW_DH ** -0.5) for x in args[:-1])


def f_combine(ids, o0, o1, o2, l0, l1, l2):
    m = jnp.maximum(jnp.maximum(l0, l1), l2)
    e0, e1, e2 = jnp.exp(l0 - m), jnp.exp(l1 - m), jnp.exp(l2 - m)
    den = e0 + e1 + e2
    o = (e0 * o0 + e1 * o1 + e2 * o2) / den
    return o, m + jnp.log(den)


GDN_T = 512
HALO = 8


def _conv_pre(xx, w):
    acc = xx * w[3:4, :]
    for j in range(3):
        acc = acc + pltpu.roll(xx, shift=3 - j, axis=0) * w[j:j + 1, :]
    return acc


def _qkv_act(pre, cidx):
    s = _silu(pre)
    r = lax.rsqrt(jnp.sum(s * s, axis=-1, keepdims=True) + EPS)
    scale = jnp.where(cidx < GDN_H, GDN_DK ** -0.5, 1.0).astype(F32)
    return jnp.where(cidx < 2 * GDN_H, s * r * scale, s)


def gdn_pre(proj, conv_w, S):
    nt = S // GDN_T
    hb = GDN_T // HALO

    def body(prev_ref, cur_ref, w_ref, o_ref):
        c, i = pl.program_id(0), pl.program_id(1)
        prev = jnp.where(i > 0, prev_ref[...], 0.0)
        xx = jnp.concatenate([prev, cur_ref[...]], axis=0)
        pre = _conv_pre(xx, w_ref[...])[HALO:]
        o_ref[...] = _qkv_act(pre, c)

    return pl.pallas_call(
        body, name="gdn_pre", grid=(3 * GDN_H, nt),
        in_specs=[pl.BlockSpec((HALO, LANES), lambda c, i: (jnp.maximum(i * hb - 1, 0), c)),
                  pl.BlockSpec((GDN_T, LANES), lambda c, i: (i, c)),
                  pl.BlockSpec((4, LANES), lambda c, i: (0, c))],
        out_specs=pl.BlockSpec((None, None, GDN_T, LANES), lambda c, i: (c // GDN_H, c % GDN_H, i, 0)),
        out_shape=SDS((3, GDN_H, S, LANES), F32),
        compiler_params=_cp(2),
    )(proj, proj, conv_w)


def gdn_pre_bwd(proj, conv_w, dqkv, S):
    nt = S // GDN_T
    hb = GDN_T // HALO
    last_h = S // HALO - 1

    def body(prev_ref, cur_ref, next_ref, w_ref, d_ref, dnext_ref, dx_ref, dw_ref):
        c, i = pl.program_id(0), pl.program_id(1)
        w = w_ref[...]
        prev = jnp.where(i > 0, prev_ref[...], 0.0)
        xx = jnp.concatenate([prev, cur_ref[...], next_ref[...]], axis=0)
        dnext = jnp.where(i < nt - 1, dnext_ref[...], 0.0)
        dd = jnp.concatenate([jnp.zeros((HALO, LANES), F32), d_ref[...], dnext], axis=0)
        pre = _conv_pre(xx, w)
        _, vjp = jax.vjp(lambda p: _qkv_act(p, c), pre)
        (dpre,) = vjp(dd)
        row = lax.broadcasted_iota(jnp.int32, dpre.shape, 0)
        dpre = jnp.where(row >= HALO, dpre, 0.0)
        dx = dpre * w[3:4, :]
        R = dpre.shape[0]
        for j in range(3):
            dx = dx + pltpu.roll(dpre, shift=R - (3 - j), axis=0) * w[j:j + 1, :]
        dx_ref[...] = dx[HALO:HALO + GDN_T].astype(dx_ref.dtype)
        own = jnp.where(row < HALO + GDN_T, dpre, 0.0)
        rows_w = [jnp.sum(own * pltpu.roll(xx, shift=3 - j, axis=0), axis=0, keepdims=True) for j in range(3)]
        rows_w.append(jnp.sum(own * xx, axis=0, keepdims=True))
        r4 = lax.broadcasted_iota(jnp.int32, (4, LANES), 0)
        dw = jnp.zeros((4, LANES), F32)
        for j in range(4):
            dw = dw + jnp.where(r4 == j, rows_w[j], 0.0)

        @pl.when(i == 0)
        def _():
            dw_ref[...] = jnp.zeros_like(dw_ref)

        dw_ref[...] += dw

    dspec = lambda f: pl.BlockSpec((None, None) + f[0], f[1])
    return pl.pallas_call(
        body, name="gdn_pre_bwd", grid=(3 * GDN_H, nt),
        in_specs=[pl.BlockSpec((HALO, LANES), lambda c, i: (jnp.maximum(i * hb - 1, 0), c)),
                  pl.BlockSpec((GDN_T, LANES), lambda c, i: (i, c)),
                  pl.BlockSpec((HALO, LANES), lambda c, i: (jnp.minimum((i + 1) * hb, last_h), c)),
                  pl.BlockSpec((4, LANES), lambda c, i: (0, c)),
                  dspec(((GDN_T, LANES), lambda c, i: (c // GDN_H, c % GDN_H, i, 0))),
                  dspec(((HALO, LANES), lambda c, i: (c // GDN_H, c % GDN_H, jnp.minimum((i + 1) * hb, last_h), 0)))],
        out_specs=[pl.BlockSpec((GDN_T, LANES), lambda c, i: (i, c)),
                   pl.BlockSpec((4, LANES), lambda c, i: (0, c))],
        out_shape=[SDS((S, 3 * GDN_H * LANES), BF16), SDS((4, 3 * GDN_H * LANES), F32)],
        compiler_params=_cp(2),
    )(proj, proj, proj, conv_w, dqkv, dqkv)


_DIMS = {"nn": ((1,), (0,)), "nt": ((1,), (1,)), "tn": ((0,), (0,))}


def _mm_raw(a, b, mode, hi):
    if hi:
        return _dot_hi(a, b, _DIMS[mode])
    return lax.dot_general(a.astype(BF16), b.astype(BF16), (_DIMS[mode], ((), ())), preferred_element_type=F32)


@functools.partial(jax.custom_vjp, nondiff_argnums=(2, 3))
def mm(a, b, mode, hi):
    return _mm_raw(a, b, mode, hi)


def _mm_fwd(a, b, mode, hi):
    return _mm_raw(a, b, mode, hi), (a, b)


def _mm_bwd(mode, hi, res, dc):
    a, b = res
    if mode == "nn":
        da, db = mm(dc, b, "nt", hi), mm(a, dc, "tn", hi)
    elif mode == "nt":
        da, db = mm(dc, b, "nn", hi), mm(dc, a, "tn", hi)
    else:
        da, db = mm(b, dc, "nt", hi), mm(a, dc, "nn", hi)
    return da, db


mm.defvjp(_mm_fwd, _mm_bwd)


TRI_BASE = 8


def _unit_lower_inverses(Ls):
    n = Ls[0].shape[0]
    r = lax.broadcasted_iota(jnp.int32, (n, n), 0)
    c = lax.broadcasted_iota(jnp.int32, (n, n), 1)
    eye = jnp.where(r == c, 1.0, 0.0).astype(F32)
    base = r // TRI_BASE == c // TRI_BASE
    Ps = [jnp.where(base, -L, 0.0) for L in Ls]
    invs = [eye + P for P in Ps]
    k = 1
    while 2 * k < TRI_BASE:
        Ps = [_dot_x3(P, P) for P in Ps]
        invs = [inv + _dot_x3(inv, P) for inv, P in zip(invs, Ps)]
        k *= 2
    b = 2 * TRI_BASE
    while b <= n:
        off_mask = (r // b == c // b) & ((r % b) >= b // 2) & ((c % b) < b // 2)
        ts = [_dot_x3(inv, jnp.where(off_mask, L, 0.0)) for inv, L in zip(invs, Ls)]
        invs = [inv - _dot_x3(t, inv) for inv, t in zip(invs, ts)]
        b *= 2
    return invs


@jax.custom_vjp
def tri_solve2(Ls, r1s, r2s):
    invs = _unit_lower_inverses(Ls)
    return [_dot_x3(i, r) for i, r in zip(invs, r1s)], [_dot_x3(i, r) for i, r in zip(invs, r2s)]


def _tri_fwd(Ls, r1s, r2s):
    invs = _unit_lower_inverses(Ls)
    s1s = [_dot_x3(i, r) for i, r in zip(invs, r1s)]
    s2s = [_dot_x3(i, r) for i, r in zip(invs, r2s)]
    return (s1s, s2s), (invs, s1s, s2s)


def _tri_bwd(res, ds):
    invs, s1s, s2s = res
    d1s = [_dot_x3(i, d, _DIMS["tn"]) for i, d in zip(invs, ds[0])]
    d2s = [_dot_x3(i, d, _DIMS["tn"]) for i, d in zip(invs, ds[1])]
    dLs = [-(_dot_x3(d1, s1, _DIMS["nt"]) + _dot_x3(d2, s2, _DIMS["nt"])) for d1, s1, d2, s2 in zip(d1s, s1s, d2s, s2s)]
    return dLs, d1s, d2s


tri_solve2.defvjp(_tri_fwd, _tri_bwd)


def _gdn_chunk(qs, ks, vs, gcbs, btbs, Ss):
    C = qs[0].shape[0]
    r = lax.broadcasted_iota(jnp.int32, (C, C), 0)
    c = lax.broadcasted_iota(jnp.int32, (C, C), 1)
    causal, strict = c <= r, c < r
    rows = lax.broadcasted_iota(jnp.int32, gcbs[0].shape, 0)
    Gs = [g[:, :C] for g in gcbs]
    decays = [jnp.exp(jnp.where(causal, G - G.T, NEG)) for G in Gs]
    kbs = [k * b for k, b in zip(ks, btbs)]
    vbs = [v * b for v, b in zip(vs, btbs)]
    Ls = [jnp.where(strict, mm(kb, k, "nt", False) * d, 0.0) for kb, k, d in zip(kbs, ks, decays)]
    egs = [jnp.exp(g) for g in gcbs]
    us, ws = tri_solve2(Ls, vbs, [kb * eg for kb, eg in zip(kbs, egs)])
    qks = [jnp.where(causal, mm(q, k, "nt", False) * d, 0.0) for q, k, d in zip(qs, ks, decays)]
    g_lasts = [jnp.sum(jnp.where(rows == C - 1, g, 0.0), axis=0, keepdims=True) for g in gcbs]
    q_decs = [q * eg for q, eg in zip(qs, egs)]
    k_decs = [k * jnp.exp(gl - g) for k, gl, g in zip(ks, g_lasts, gcbs)]
    v_news = [u - mm(w, S, "nn", False) for u, w, S in zip(us, ws, Ss)]
    os_ = [mm(qd, S, "nn", False) + mm(qk, vn, "nn", False) for qd, S, qk, vn in zip(q_decs, Ss, qks, v_news)]
    S_news = [S * jnp.exp(gl) + mm(kd, vn, "tn", False) for S, gl, kd, vn in zip(Ss, g_lasts, k_decs, v_news)]
    return os_, S_news


def gdn_core(qkv, gc, bt, S):
    nchunk = S // GDN_C

    def body(qkv_ref, g_ref, b_ref, o_ref, st_ref, s_scr):
        n = pl.program_id(0)

        @pl.when(n == 0)
        def _():
            s_scr[...] = jnp.zeros_like(s_scr)

        heads = range(GDN_H)
        S_in = [s_scr[h] for h in heads]
        os_, S_new = _gdn_chunk([qkv_ref[0, h] for h in heads], [qkv_ref[1, h] for h in heads], [qkv_ref[2, h] for h in heads],
                                [g_ref[h] for h in heads], [b_ref[h] for h in heads], S_in)
        for h in heads:
            st_ref[h] = S_in[h]
            o_ref[h] = os_[h]
            s_scr[h] = S_new[h]

    blk3 = pl.BlockSpec((3, GDN_H, GDN_C, LANES), lambda n: (0, 0, n, 0))
    hb = pl.BlockSpec((GDN_H, GDN_C, LANES), lambda n: (0, n, 0))
    return pl.pallas_call(
        body, name="gdn_core", grid=(nchunk,),
        in_specs=[blk3, hb, hb],
        out_specs=[hb, pl.BlockSpec((GDN_H, None, GDN_DK, LANES), lambda n: (0, n, 0, 0))],
        out_shape=[SDS((GDN_H, S, LANES), F32), SDS((GDN_H, nchunk, GDN_DK, LANES), F32)],
        scratch_shapes=[pltpu.VMEM((GDN_H, GDN_DK, LANES), F32)],
        compiler_params=_cp(1),
    )(qkv, gc, bt)


def gdn_core_bwd(qkv, gc, bt, states, do, S):
    nchunk = S // GDN_C

    def body(qkv_ref, g_ref, b_ref, st_ref, do_ref, dqkv_ref, dg_ref, db_ref, ds_scr):
        n = pl.program_id(0)

        @pl.when(n == 0)
        def _():
            ds_scr[...] = jnp.zeros_like(ds_scr)

        heads = range(GDN_H)
        _, vjp = jax.vjp(_gdn_chunk, [qkv_ref[0, h] for h in heads], [qkv_ref[1, h] for h in heads], [qkv_ref[2, h] for h in heads],
                         [g_ref[h] for h in heads], [b_ref[h] for h in heads], [st_ref[h] for h in heads])
        dq, dk, dv, dg, db, dS = vjp(([do_ref[h] for h in heads], [ds_scr[h] for h in heads]))
        for h in heads:
            dqkv_ref[0, h] = dq[h]
            dqkv_ref[1, h] = dk[h]
            dqkv_ref[2, h] = dv[h]
            dg_ref[h] = dg[h]
            db_ref[h] = db[h]
            ds_scr[h] = dS[h]

    rev = lambda n: nchunk - 1 - n
    blk3 = pl.BlockSpec((3, GDN_H, GDN_C, LANES), lambda n: (0, 0, rev(n), 0))
    hb = pl.BlockSpec((GDN_H, GDN_C, LANES), lambda n: (0, rev(n), 0))
    return pl.pallas_call(
        body, name="gdn_core_bwd", grid=(nchunk,),
        in_specs=[blk3, hb, hb, pl.BlockSpec((GDN_H, None, GDN_DK, LANES), lambda n: (0, rev(n), 0, 0)), hb],
        out_specs=[blk3, hb, hb],
        out_shape=[SDS((3, GDN_H, S, LANES), F32), SDS((GDN_H, S, LANES), F32), SDS((GDN_H, S, LANES), F32)],
        scratch_shapes=[pltpu.VMEM((GDN_H, GDN_DK, LANES), F32)],
        compiler_params=_cp(1),
    )(qkv, gc, bt, states, do)


GDN_MAIN = 4 * GDN_H * LANES
GDN_PROJ = GDN_MAIN + LANES
RT = 256


def gdn_forward(h, w_in, conv_w, alog, dtb, out_gain, w_out):
    S = h.shape[0]
    nt = S // RT
    proj = matmul(h, w_in, "nn", F32, "gdn_in")
    qkv = gdn_pre(proj, conv_w, S)
    ab_row = Row(proj, (RT, LANES), lambda i: (i, GDN_MAIN // LANES), gdtype=BF16, gshape=(S, LANES), gimap=lambda i: (i, 0))
    hm = lambda i: (0, i, 0)
    hv = GDN_H * LANES
    gc, bt = rowwise(f_gdn_gates, [ab_row], [alog, dtb],
                     [Out((GDN_H, S, LANES), F32, (GDN_H, RT, LANES), hm, lead=GDN_H)] * 2, (nt,), "gdn_gates")
    o, states = gdn_core(qkv, gc, bt, S)
    o_row = Row(o, (GDN_H, RT, LANES), hm, lead=GDN_H)
    z_row = Row(proj, (RT, hv), lambda i: (i, 3), splits=[LANES] * GDN_H, gdtype=BF16, gshape=(S, hv), gimap=lambda i: (i, 0))
    (on,) = rowwise(f_gdn_post, [o_row, z_row], [out_gain],
                    [Out((S, hv), BF16, (RT, hv), lambda i: (i, 0), splits=[LANES] * GDN_H)], (nt,), "gdn_post")
    y = matmul(on, w_out, "nn", F32, "gdn_out")
    saved = dict(h=h, proj=proj, qkv=qkv, gc=gc, bt=bt, states=states, o=o, on=on, ab_row=ab_row, o_row=o_row, z_row=z_row)
    return y, saved


def gdn_backward(dy, sv, w_in, conv_w, alog, dtb, out_gain, w_out):
    S = dy.shape[0]
    nt = S // RT
    hm = lambda i: (0, i, 0)
    hv = GDN_H * LANES
    don = matmul(dy, w_out, "nt", F32, "gdn_out_dx")
    d_w_out = matmul(sv["on"], dy, "tn", F32, "gdn_out_dw")
    (do, dz), (d_gain,) = rowwise_bwd(f_gdn_post, [sv["o_row"], sv["z_row"]], [out_gain],
                                      [Row(don, (RT, hv), lambda i: (i, 0), splits=[LANES] * GDN_H)], (nt,), "gdn_post_bwd")
    dqkv, dgc, dbt = gdn_core_bwd(sv["qkv"], sv["gc"], sv["bt"], sv["states"], do, S)
    head_blk = lambda a: Row(a, (GDN_H, RT, LANES), hm, lead=GDN_H)
    (dab,), (d_alog, d_dtb) = rowwise_bwd(f_gdn_gates, [sv["ab_row"]], [alog, dtb], [head_blk(dgc), head_blk(dbt)],
                                          (nt,), "gdn_gates_bwd")
    dqkv_proj, d_conv = gdn_pre_bwd(sv["proj"], conv_w, dqkv, S)
    dproj = jnp.concatenate([dqkv_proj, dz, dab], axis=1)
    d_w_in = matmul(sv["h"], dproj, "tn", F32, "gdn_in_dw")
    dh = matmul(dproj, w_in, "nt", F32, "gdn_in_dx")
    return dh, dict(w_in=d_w_in, conv=d_conv, alog=d_alog, dtb=d_dtb, gain=d_gain, w_out=d_w_out)


QB = DSW_SPAN
N_HP = DSW_HG // LANES
PROJ_BLKS = 3 * 3 * N_HP


def _bucket_maps():
    a = np.arange(QB)[:, None]
    j = np.arange(2 * QB)[None, :]
    dist = QB + a - j
    band = (dist >= 0) & (dist <= DSW_SPAN)
    maps = []
    for _, dil in DSW_GROUPS:
        dd = np.maximum(dist, 0) * dil
        max_exact = REL_BUCKETS // 2
        scaled = np.log(np.maximum(dd, 1).astype(np.float32) / np.float32(max_exact)) / np.float32(math.log(REL_MAX_DIST / max_exact))
        large = max_exact + (scaled * np.float32(REL_BUCKETS - max_exact)).astype(np.int32)
        large = np.minimum(large, REL_BUCKETS - 1)
        maps.append(np.where(dd < max_exact, dd, large).astype(np.int32))
    return np.stack(maps), band


def dsw_bias(rel_bias):
    maps, band = _bucket_maps()
    maps = np.where(band[None], maps, -1).astype(np.int32)

    def body(tab_ref, bk_ref, o_ref):
        gh = pl.program_id(0)
        bk = bk_ref[...]
        acc = jnp.full(bk.shape, NEG, F32)
        for b in range(REL_BUCKETS):
            acc = jnp.where(bk == b, tab_ref[b, gh], acc)
        o_ref[...] = acc

    return pl.pallas_call(
        body, name="dsw_bias", grid=(3 * GDN_H,),
        in_specs=[pl.BlockSpec(memory_space=pltpu.SMEM),
                  pl.BlockSpec((None, QB, 2 * QB), lambda gh: (gh // GDN_H, 0, 0))],
        out_specs=pl.BlockSpec((None, QB, 2 * QB), lambda gh: (gh, 0, 0)),
        out_shape=SDS((3 * GDN_H, QB, 2 * QB), F32),
        compiler_params=_cp(1),
    )(rel_bias, jnp.asarray(maps))


def dsw_bias_grad(dbias):
    maps, band = _bucket_maps()
    maps = np.where(band[None], maps, -1).astype(np.int32)

    def body(d_ref, bk_ref, o_ref):
        bk = bk_ref[...]
        d = d_ref[...]
        rows = lax.broadcasted_iota(jnp.int32, (REL_BUCKETS, LANES), 0)
        acc = jnp.zeros((REL_BUCKETS, LANES), F32)
        for b in range(REL_BUCKETS):
            part = jnp.sum(jnp.where(bk == b, d, 0.0), axis=0, keepdims=True)
            val = jnp.sum(part, axis=1, keepdims=True)
            acc = jnp.where(rows == b, val, acc)
        o_ref[...] = acc

    return pl.pallas_call(
        body, name="dsw_bias_grad", grid=(3 * GDN_H,),
        in_specs=[pl.BlockSpec((None, QB, 2 * QB), lambda gh: (gh, 0, 0)),
                  pl.BlockSpec((None, QB, 2 * QB), lambda gh: (gh // GDN_H, 0, 0))],
        out_specs=pl.BlockSpec((None, REL_BUCKETS, LANES), lambda gh: (gh, 0, 0)),
        out_shape=SDS((3 * GDN_H, REL_BUCKETS, LANES), F32),
        compiler_params=_cp(1),
    )(dbias, jnp.asarray(maps))


def _nt(a, b):
    return lax.dot_general(a, b, (((1,), (1,)), ((), ())), preferred_element_type=F32)


def _tn(a, b):
    return lax.dot_general(a, b, (((0,), (0,)), ((), ())), preferred_element_type=F32)


def dsw_group_fwd(qn, kn, proj, bias, gi, S):
    dil = DSW_GROUPS[gi][1]
    sd = S // dil
    nq = sd // QB
    qv = qn.reshape(sd, dil * 3 * DSW_HG)
    kv = kn.reshape(sd, dil * 3 * DSW_HG)
    pv = proj.reshape(sd, dil * 9 * DSW_HG)
    qk_col = lambda hp, r: r * (3 * N_HP) + gi * N_HP + hp
    v_col = lambda hp, r: r * PROJ_BLKS + 2 * 3 * N_HP + gi * N_HP + hp

    def body(q_ref, kp_ref, kc_ref, vp_ref, vc_ref, b_ref, o_ref, l_ref):
        i = pl.program_id(2)
        q = q_ref[...]
        k2 = jnp.concatenate([kp_ref[...], kc_ref[...]], axis=0)
        v2 = jnp.concatenate([vp_ref[...], vc_ref[...]], axis=0).astype(BF16)
        lane_q = lax.broadcasted_iota(jnp.int32, (QB, LANES), 1) < DSW_DH
        lane_k = lax.broadcasted_iota(jnp.int32, (2 * QB, LANES), 1) < DSW_DH
        col = lax.broadcasted_iota(jnp.int32, (QB, 2 * QB), 1)
        first = jnp.logical_and(i == 0, col < QB)
        o_acc = jnp.zeros((QB, LANES), F32)
        lse_b = jnp.zeros((QB, LANES), F32)
        for hh in range(2):
            mq = lane_q if hh == 0 else jnp.logical_not(lane_q)
            mk = lane_k if hh == 0 else jnp.logical_not(lane_k)
            s = _nt(jnp.where(mq, q, 0).astype(BF16), k2) + b_ref[hh]
            s = jnp.where(first, NEG, s)
            mx = jnp.max(s, axis=1, keepdims=True)
            p = jnp.exp(s - mx)
            l = jnp.sum(p, axis=1, keepdims=True)
            oh = jnp.dot(p.astype(BF16), jnp.where(mk, v2, 0).astype(BF16), preferred_element_type=F32) / l
            o_acc = o_acc + oh
            lse_b = jnp.where(mq, mx + jnp.log(l), lse_b)
        o_ref[...] = o_acc
        l_ref[...] = lse_b

    blk = (QB, LANES)
    out_spec = pl.BlockSpec(blk, lambda hp, r, i: (i, r * N_HP + hp))
    o, lse = pl.pallas_call(
        body, name=f"dsw_fwd_g{gi}", grid=(N_HP, dil, nq),
        in_specs=[pl.BlockSpec(blk, lambda hp, r, i: (i, qk_col(hp, r))),
                  pl.BlockSpec(blk, lambda hp, r, i: (jnp.maximum(i - 1, 0), qk_col(hp, r))),
                  pl.BlockSpec(blk, lambda hp, r, i: (i, qk_col(hp, r))),
                  pl.BlockSpec(blk, lambda hp, r, i: (jnp.maximum(i - 1, 0), v_col(hp, r))),
                  pl.BlockSpec(blk, lambda hp, r, i: (i, v_col(hp, r))),
                  pl.BlockSpec((2, QB, 2 * QB), lambda hp, r, i: (gi * N_HP + hp, 0, 0))],
        out_specs=[out_spec, out_spec],
        out_shape=[SDS((sd, dil * DSW_HG), F32)] * 2,
        compiler_params=_cp(3),
    )(qv, kv, kv, pv, pv, bias)
    return o.reshape(S, DSW_HG), lse.reshape(S, DSW_HG)


def dsw_group_bwd(qn, kn, proj, bias, do, o, lse, gi, S):
    dil = DSW_GROUPS[gi][1]
    sd = S // dil
    nq = sd // QB
    qv = qn.reshape(sd, dil * 3 * DSW_HG)
    kv = kn.reshape(sd, dil * 3 * DSW_HG)
    pv = proj.reshape(sd, dil * 9 * DSW_HG)
    dov = do.reshape(sd, dil * DSW_HG)
    ov = o.reshape(sd, dil * DSW_HG)
    lv = lse.reshape(sd, dil * DSW_HG)
    qk_col = lambda hp, r: r * (3 * N_HP) + gi * N_HP + hp
    v_col = lambda hp, r: r * PROJ_BLKS + 2 * 3 * N_HP + gi * N_HP + hp
    o_col = lambda hp, r: r * N_HP + hp
    cur = lambda i: jnp.minimum(i, nq - 1)
    prev = lambda i: jnp.maximum(jnp.minimum(i, nq - 1) - 1, 0)
    done = lambda i: jnp.maximum(i - 1, 0)

    def body(q_ref, kp_ref, kc_ref, vp_ref, vc_ref, b_ref, do_ref, o_ref, l_ref,
             dq_ref, dk_ref, dv_ref, db_ref, dk_scr, dv_scr):
        r, i = pl.program_id(1), pl.program_id(2)

        @pl.when(jnp.logical_and(r == 0, i == 0))
        def _():
            db_ref[...] = jnp.zeros_like(db_ref)

        @pl.when(i == 0)
        def _():
            dk_scr[...] = jnp.zeros_like(dk_scr)
            dv_scr[...] = jnp.zeros_like(dv_scr)

        @pl.when(i < nq)
        def _():
            q = q_ref[...]
            k2 = jnp.concatenate([kp_ref[...], kc_ref[...]], axis=0)
            v2 = jnp.concatenate([vp_ref[...], vc_ref[...]], axis=0).astype(BF16)
            dout = do_ref[...].astype(F32)
            prod = dout * o_ref[...].astype(F32)
            lse_b = l_ref[...]
            lane_q = lax.broadcasted_iota(jnp.int32, (QB, LANES), 1) < DSW_DH
            col = lax.broadcasted_iota(jnp.int32, (QB, 2 * QB), 1)
            first = jnp.logical_and(i == 0, col < QB)
            dq = jnp.zeros((QB, LANES), F32)
            dk2 = jnp.zeros((2 * QB, LANES), F32)
            dv2 = jnp.zeros((2 * QB, LANES), F32)
            for hh in range(2):
                mq = lane_q if hh == 0 else jnp.logical_not(lane_q)
                qm = jnp.where(mq, q, 0).astype(BF16)
                dom = jnp.where(mq, dout, 0.0).astype(BF16)
                s = _nt(qm, k2) + b_ref[hh]
                s = jnp.where(first, NEG, s)
                lse_h = jnp.max(jnp.where(mq, lse_b, NEG), axis=1, keepdims=True)
                p = jnp.exp(s - lse_h)
                delta = jnp.sum(jnp.where(mq, prod, 0.0), axis=1, keepdims=True)
                dp = _nt(dom, v2)
                ds = p * (dp - delta)
                dsb = ds.astype(BF16)
                dq = dq + jnp.where(mq, jnp.dot(dsb, k2, preferred_element_type=F32), 0.0)
                dk2 = dk2 + _tn(dsb, qm)
                dv2 = dv2 + _tn(p.astype(BF16), dom)
                db_ref[hh] += ds
            dq_ref[...] = dq
            dk_ref[...] = dk_scr[...] + dk2[:QB]
            dv_ref[...] = (dv_scr[...] + dv2[:QB]).astype(dv_ref.dtype)
            dk_scr[...] = dk2[QB:]
            dv_scr[...] = dv2[QB:]

        @pl.when(i == nq)
        def _():
            dk_ref[...] = dk_scr[...]
            dv_ref[...] = dv_scr[...].astype(dv_ref.dtype)

    blk = (QB, LANES)
    dq, dk, dv, dbias = pl.pallas_call(
        body, name=f"dsw_bwd_g{gi}", grid=(N_HP, dil, nq + 1),
        in_specs=[pl.BlockSpec(blk, lambda hp, r, i: (cur(i), qk_col(hp, r))),
                  pl.BlockSpec(blk, lambda hp, r, i: (prev(i), qk_col(hp, r))),
                  pl.BlockSpec(blk, lambda hp, r, i: (cur(i), qk_col(hp, r))),
                  pl.BlockSpec(blk, lambda hp, r, i: (prev(i), v_col(hp, r))),
                  pl.BlockSpec(blk, lambda hp, r, i: (cur(i), v_col(hp, r))),
                  pl.BlockSpec((2, QB, 2 * QB), lambda hp, r, i: (gi * N_HP + hp, 0, 0)),
                  pl.BlockSpec(blk, lambda hp, r, i: (cur(i), o_col(hp, r))),
                  pl.BlockSpec(blk, lambda hp, r, i: (cur(i), o_col(hp, r))),
                  pl.BlockSpec(blk, lambda hp, r, i: (cur(i), o_col(hp, r)))],
        out_specs=[pl.BlockSpec(blk, lambda hp, r, i: (cur(i), o_col(hp, r))),
                   pl.BlockSpec(blk, lambda hp, r, i: (done(i), o_col(hp, r))),
                   pl.BlockSpec(blk, lambda hp, r, i: (done(i), o_col(hp, r))),
                   pl.BlockSpec((2, QB, 2 * QB), lambda hp, r, i: (hp, 0, 0))],
        out_shape=[SDS((sd, dil * DSW_HG), F32), SDS((sd, dil * DSW_HG), F32), SDS((sd, dil * DSW_HG), BF16),
                   SDS((GDN_H, QB, 2 * QB), F32)],
        scratch_shapes=[pltpu.VMEM(blk, F32), pltpu.VMEM(blk, F32)],
        compiler_params=_cp(3),
    )(qv, kv, kv, pv, pv, bias, dov, ov, lv)
    return dq.reshape(S, DSW_HG), dk.reshape(S, DSW_HG), dv.reshape(S, DSW_HG), dbias


def dsw_forward(h, w_in, q_gain2, k_gain2, rel_bias, w_out):
    S = h.shape[0]
    nt = S // RT
    nb = 3 * N_HP
    proj = matmul(h, w_in, "nn", F32, "dsw_in")
    width = nb * LANES
    lanes12 = [LANES] * nb
    (qn,) = rowwise(f_qnorm, [Row(proj, (RT, width), lambda i: (i, 0), splits=lanes12)], [q_gain2],
                    [Out((S, width), BF16, (RT, width), lambda i: (i, 0), splits=lanes12)], (nt,), "dsw_qnorm")
    (kn,) = rowwise(f_qknorm, [Row(proj, (RT, width), lambda i: (i, 1), splits=lanes12)], [k_gain2],
                    [Out((S, width), BF16, (RT, width), lambda i: (i, 0), splits=lanes12)], (nt,), "dsw_knorm")
    bias = dsw_bias(rel_bias)
    os_, ls_ = [], []
    for gi in range(3):
        o, l = dsw_group_fwd(qn, kn, proj, bias, gi, S)
        os_.append(o)
        ls_.append(l)
    full = lambda a: Row(a, (RT, DSW_HG), lambda i: (i, 0))
    o, lse = rowwise(f_combine, [full(a) for a in os_ + ls_], [],
                     [Out((S, DSW_HG), BF16, (RT, DSW_HG), lambda i: (i, 0)), Out((S, DSW_HG), F32, (RT, DSW_HG), lambda i: (i, 0))],
                     (nt,), "dsw_combine")
    y = matmul(o, w_out, "nn", F32, "dsw_out")
    return y, dict(h=h, proj=proj, qn=qn, kn=kn, bias=bias, o=o, lse=lse)


def dsw_backward(dy, sv, w_in, q_gain2, k_gain2, w_out):
    S = dy.shape[0]
    nt = S // RT
    nb = 3 * N_HP
    do = matmul(dy, w_out, "nt", BF16, "dsw_out_dx")
    d_w_out = matmul(sv["o"], dy, "tn", F32, "dsw_out_dw")
    pieces_q, pieces_k, pieces_v, dbs = [], [], [], []
    d_qg = jnp.zeros((1, LANES), F32)
    d_kg = jnp.zeros((1, LANES), F32)
    for gi in range(3):
        dq, dk, dv, db = dsw_group_bwd(sv["qn"], sv["kn"], sv["proj"], sv["bias"], do, sv["o"], sv["lse"], gi, S)
        dbs.append(db)
        pieces_v.append(dv)
        for which, dd in ((0, dq), (1, dk)):
            lanes4 = [LANES] * N_HP
            row = Row(sv["proj"], (RT, DSW_HG), lambda i, _o=which * 3 + gi: (i, _o), splits=lanes4,
                      gdtype=BF16, gshape=(S, DSW_HG), gimap=lambda i: (i, 0))
            fn, gain = (f_qnorm, q_gain2) if which == 0 else (f_qknorm, k_gain2)
            (dx,), (dg,) = rowwise_bwd(fn, [row], [gain], [Row(dd, (RT, DSW_HG), lambda i: (i, 0), splits=lanes4)],
                                       (nt,), f"dsw_norm_bwd_{which}{gi}")
            if which == 0:
                pieces_q.append(dx)
                d_qg = d_qg + dg
            else:
                pieces_k.append(dx)
                d_kg = d_kg + dg
    dproj = jnp.concatenate(pieces_q + pieces_k + pieces_v, axis=1)
    d_w_in = matmul(sv["h"], dproj, "tn", F32, "dsw_in_dw")
    dh = matmul(dproj, w_in, "nt", F32, "dsw_in_dx")
    d_rel = dsw_bias_grad(jnp.concatenate(dbs, axis=0))
    return dh, dict(w_in=d_w_in, q_gain2=d_qg, k_gain2=d_kg, rel=d_rel, w_out=d_w_out)


FT = 128


def ffn_forward(h, w_in, w_out, tag):
    S = h.shape[0]
    gu = matmul(h, w_in, "nn", F32, f"ffn_in_{tag}")
    gu_row = Row(gu, (FT, 2 * FFN), lambda i: (i, 0), splits=[FFN, FFN], gdtype=BF16)
    (a,) = rowwise(f_swiglu, [gu_row], [], [Out((S, FFN), BF16, (FT, FFN), lambda i: (i, 0))], (S // FT,), f"ffn_act_{tag}")
    f = matmul(a, w_out, "nn", F32, f"ffn_out_{tag}")
    return f, dict(h=h, gu_row=gu_row, a=a)


def ffn_backward(df, sv, w_in, w_out, tag):
    S = df.shape[0]
    da = matmul(df, w_out, "nt", BF16, f"ffn_out_dx_{tag}")
    d_w_out = matmul(sv["a"], df, "tn", F32, f"ffn_out_dw_{tag}")
    (dgu,), _ = rowwise_bwd(f_swiglu, [sv["gu_row"]], [], [Row(da, (FT, FFN), lambda i: (i, 0))], (S // FT,), f"ffn_act_bwd_{tag}")
    d_w_in = matmul(sv["h"], dgu, "tn", F32, f"ffn_in_dw_{tag}")
    dh = matmul(dgu, w_in, "nt", F32, f"ffn_in_dx_{tag}")
    return dh, d_w_in, d_w_out


def f_norm_only(ids, x, gain, sc, sh):
    return (_normmod(x, gain, sc, sh),)


def _wide(a, **kw):
    return Row(a, (RT, D), lambda i: (i, 0), **kw)


def _wide_out(S, dtype):
    return Out((S, D), dtype, (RT, D), lambda i: (i, 0))


def adamw(w, g, m, v, name):
    shape = w.shape
    C = shape[-1]
    R = int(np.prod(shape[:-1]))
    w2, g2, m2, v2 = (a.reshape(R, C) for a in (w, g, m, v))
    br = R
    if R > 256:
        br = max(b for b in range(8, 257, 8) if R % b == 0)
    c1 = 1.0 / (1.0 - ADAM_B1 ** ADAM_STEP)
    c2 = 1.0 / (1.0 - ADAM_B2 ** ADAM_STEP)

    def body(w_ref, g_ref, m_ref, v_ref, d_ref, nm_ref, nv_ref):
        gg = g_ref[...]
        mm_ = ADAM_B1 * m_ref[...] + (1.0 - ADAM_B1) * gg
        vv = ADAM_B2 * v_ref[...] + (1.0 - ADAM_B2) * (gg * gg)
        d_ref[...] = -ADAM_LR * ((mm_ * c1) / (jnp.sqrt(vv * c2) + ADAM_EPS) + ADAM_WD * w_ref[...])
        nm_ref[...] = mm_
        nv_ref[...] = vv

    spec = pl.BlockSpec((br, C), lambda i: (i, 0))
    d, nm, nv = pl.pallas_call(
        body, name=name, grid=(R // br,), in_specs=[spec] * 4, out_specs=[spec] * 3,
        out_shape=[SDS((R, C), F32)] * 3, compiler_params=_cp(1),
    )(w2, g2, m2, v2)
    return d.reshape(shape), nm.reshape(shape), nv.reshape(shape)


def _place():
    x, y, c = lax.axis_index("x"), lax.axis_index("y"), lax.axis_index("c")
    chips = [(1 - x, y), (x, 1 - y), (1 - x, 1 - y)]
    return x, y, c, chips


def all_gather_small(blk, name):
    m_per, n = blk.shape

    def body(x_ref, out_ref, send_sems, recv_sems, local_sem):
        x, y, c, chips = _place()
        me, sibling = (x, y, c), (x, y, 1 - c)

        def rows(px, py, pc):
            return out_ref.at[pl.ds((4 * px + 2 * py + pc) * m_per, m_per), :]

        def copy(k, block, to, src=None):
            return pltpu.make_async_remote_copy(
                src_ref=rows(*block) if src is None else src, dst_ref=rows(*block),
                send_sem=send_sems.at[k], recv_sem=recv_sems.at[k], device_id=to, device_id_type=MESH)

        mine = pltpu.make_async_copy(x_ref, rows(*me), local_sem)
        mine.start()
        first = [copy(0, me, sibling, src=x_ref)]
        first += [copy(1 + j, me, (*chip, c), src=x_ref) for j, chip in enumerate(chips)]
        for cp in first:
            cp.start()
        passed = [copy(4 + j, (*chip, c), sibling) for j, chip in enumerate(chips)]
        for j, chip in enumerate(chips):
            copy(1 + j, (*chip, c), me).wait_recv()
            passed[j].start()
        copy(0, sibling, me).wait_recv()
        for j, chip in enumerate(chips):
            copy(4 + j, (*chip, 1 - c), me).wait_recv()
        for cp in first + passed:
            cp.wait_send()
        mine.wait()

    return pl.pallas_call(
        body, name=name, out_shape=SDS((N_DEV * m_per, n), blk.dtype),
        in_specs=[pl.BlockSpec(memory_space=pltpu.VMEM)], out_specs=pl.BlockSpec(memory_space=pltpu.VMEM),
        scratch_shapes=[pltpu.SemaphoreType.DMA((7,)), pltpu.SemaphoreType.DMA((7,)), pltpu.SemaphoreType.DMA],
    )(blk)


def all_gather_shards(wp):
    R, W = wp.shape
    Rh = R // 2

    def body(w_ref, out_ref, send_sems, recv_sems, local_sem):
        x, y, c, chips = _place()
        sibling = (x, y, 1 - c)
        s_me = 2 * x + y

        def half(cc):
            return pl.ds(pl.multiple_of(cc * Rh, 16), Rh)

        def copy(k, src, dst, to):
            return pltpu.make_async_remote_copy(src_ref=src, dst_ref=dst, send_sem=send_sems.at[k], recv_sem=recv_sems.at[k],
                                                device_id=to, device_id_type=MESH)

        mine = pltpu.make_async_copy(w_ref, out_ref.at[s_me], local_sem)
        mine.start()
        sends = [copy(j, w_ref.at[half(c)], out_ref.at[s_me, half(c)], (*chip, c)) for j, chip in enumerate(chips)]
        for cp in sends:
            cp.start()
        passed = []
        for j, (px, py) in enumerate(chips):
            got = out_ref.at[2 * px + py, half(c)]
            copy(j, got, got, (px, py, c)).wait_recv()
            fw = copy(3 + j, got, got, sibling)
            fw.start()
            passed.append(fw)
        for j, (px, py) in enumerate(chips):
            got = out_ref.at[2 * px + py, half(1 - c)]
            copy(3 + j, got, got, sibling).wait_recv()
        for cp in sends + passed:
            cp.wait_send()
        mine.wait()

    return pl.pallas_call(
        body, name="weights_all_gather", out_shape=SDS((N_SHARD, R, W), wp.dtype),
        in_specs=[ANY], out_specs=ANY,
        scratch_shapes=[pltpu.SemaphoreType.DMA((6,)), pltpu.SemaphoreType.DMA((6,)), pltpu.SemaphoreType.DMA],
    )(wp)


def sibling_exchange(send, name):
    def body(s_ref, o_ref, send_sem, recv_sem):
        x, y, c, _ = _place()
        cp = pltpu.make_async_remote_copy(src_ref=s_ref, dst_ref=o_ref, send_sem=send_sem, recv_sem=recv_sem,
                                          device_id=(x, y, 1 - c), device_id_type=MESH)
        cp.start()
        cp.wait()

    return pl.pallas_call(
        body, name=name, out_shape=SDS(send.shape, send.dtype), in_specs=[ANY], out_specs=ANY,
        scratch_shapes=[pltpu.SemaphoreType.DMA, pltpu.SemaphoreType.DMA],
    )(send)


def scatter_to_chips(p):
    _, Rh, W = p.shape

    def body(p_ref, o_ref, send_sems, recv_sems):
        x, y, c, chips = _place()
        cps = []
        for j, (px, py) in enumerate(chips):
            cp = pltpu.make_async_remote_copy(src_ref=p_ref.at[2 * px + py], dst_ref=o_ref.at[j], send_sem=send_sems.at[j],
                                              recv_sem=recv_sems.at[j], device_id=(px, py, c), device_id_type=MESH)
            cp.start()
            cps.append(cp)
        for cp in cps:
            cp.wait()

    return pl.pallas_call(
        body, name="grads_scatter", out_shape=SDS((3, Rh, W), p.dtype), in_specs=[ANY], out_specs=ANY,
        scratch_shapes=[pltpu.SemaphoreType.DMA((3,)), pltpu.SemaphoreType.DMA((3,))],
    )(p)


def add_rows(arrs, out_dtype, name, rt=256):
    Rr, W = arrs[0].shape

    def fn(ids, *vals):
        acc = vals[0]
        for v in vals[1:]:
            acc = acc + v
        return (acc,)

    t = rt if Rr % rt == 0 else max(b for b in range(16, rt + 1, 16) if Rr % b == 0)
    (out,) = rowwise(fn, [Row(a, (t, W), lambda i: (i, 0)) for a in arrs], [],
                     [Out((Rr, W), out_dtype, (t, W), lambda i: (i, 0))], (Rr // t,), name)
    return out


PACK = (("gdn_w_in", 2), ("gdn_w_out", 1), ("w_ffn_in", 2), ("w_ffn_out", 1), ("dsw_w_in", 2), ("dsw_w_out", 2))
PACK_ALIGN = 32


def _pack_rows(sizes):
    total = sum(sizes)
    rows = -(-total // D)
    return -(-rows // PACK_ALIGN) * PACK_ALIGN


def pack_blocks(blocks, dtype):
    flat = [b.astype(dtype).reshape(-1) for b in blocks]
    total = sum(f.shape[0] for f in flat)
    R = _pack_rows([f.shape[0] for f in flat])
    flat.append(jnp.zeros((R * D - total,), dtype))
    return jnp.concatenate(flat).reshape(R, D)


def unpack_blocks(buf, shapes):
    flat = buf.reshape(-1)
    out, off = [], 0
    for shp in shapes:
        n = int(np.prod(shp))
        out.append(flat[off:off + n].reshape(shp))
        off += n
    return out


def _shard_slice(a, axis, s):
    n = a.shape[axis] // N_SHARD
    return lax.slice_in_dim(a, s * n, (s + 1) * n, axis=axis)


def _pad_lanes(v):
    return jnp.concatenate([v.astype(F32), jnp.zeros((LANES - v.shape[0],), F32)])[None]


def kernel(x, c, w_ada, b_ada, norm_mix, norm_ffn, w_ffn_in, w_ffn_out, gdn_w_in, gdn_conv, gdn_a_log, gdn_dt_bias, gdn_out_norm, gdn_w_out, dsw_w_in, dsw_q_norm, dsw_k_norm, dsw_w_out, rel_bias, loss_target, m_w_ada, m_b_ada, m_norm_mix, m_norm_ffn, m_w_ffn_in, m_w_ffn_out, m_gdn_w_in, m_gdn_conv, m_gdn_a_log, m_gdn_dt_bias, m_gdn_out_norm, m_gdn_w_out, m_dsw_w_in, m_dsw_q_norm, m_dsw_k_norm, m_dsw_w_out, m_rel_bias, v_w_ada, v_b_ada, v_norm_mix, v_norm_ffn, v_w_ffn_in, v_w_ffn_out, v_gdn_w_in, v_gdn_conv, v_gdn_a_log, v_gdn_dt_bias, v_gdn_out_norm, v_gdn_w_out, v_dsw_w_in, v_dsw_q_norm, v_dsw_k_norm, v_dsw_w_out, v_rel_bias):
    S = x.shape[1]
    nt = S // RT
    xi, yi, ci = lax.axis_index("x"), lax.axis_index("y"), lax.axis_index("c")
    me = 4 * xi + 2 * yi + ci
    s_me = 2 * xi + yi
    x0, tgt = x[0], loss_target[0]
    shard = dict(w_ffn_in=w_ffn_in, w_ffn_out=w_ffn_out, gdn_w_in=gdn_w_in, gdn_w_out=gdn_w_out, dsw_w_in=dsw_w_in, dsw_w_out=dsw_w_out)

    whole = lambda a: Row(a, a.shape, lambda i: (0,) * a.ndim)
    (cond8,) = rowwise(lambda ids, v: (_silu(v),), [whole(c.reshape(8, LANES))], [], [Out((8, LANES), F32, (8, LANES), lambda i: (0, 0))], (1,), "cond")
    cond_all = all_gather_small(cond8, "gather_cond").reshape(N_DEV, D)
    cond16 = jnp.concatenate([cond_all, jnp.zeros((8, D), F32)], axis=0)
    ada_cols = w_ada.shape[2]
    mods = [matmul(cond16, w_ada[l], "nn", F32, f"ada_{l}")[:N_DEV] for l in range(2)]
    buf = jnp.concatenate([jnp.stack(mods, axis=1).reshape(-1, LANES), gdn_conv.reshape(-1, LANES)], axis=0)
    n_mod_rows = N_DEV * 2 * ada_cols // LANES
    got = all_gather_small(buf, "gather_mod").reshape(N_DEV, buf.shape[0], LANES)
    mod_parts, conv_parts = [], []
    for s in range(N_SHARD):
        from_dev = got[2 * s]
        mod_parts.append(lax.dynamic_index_in_dim(from_dev[:n_mod_rows].reshape(N_DEV, 2, ada_cols), me, 0, keepdims=False))
        conv_parts.append(from_dev[n_mod_rows:].reshape(4, -1))
    mod_nb = jnp.concatenate(mod_parts, axis=1)
    conv_w = jnp.concatenate(conv_parts, axis=1)
    (mod,) = rowwise(lambda ids, a, b: (a + b,), [whole(mod_nb), whole(b_ada)], [], [Out(mod_nb.shape, F32, mod_nb.shape, lambda i: (0, 0))], (1,), "mod_bias")
    mod = mod.reshape(2, 6, 1, D)
    sh1, sc1, g1, sh2, sc2, g2 = ([mod[l, k] for l in range(2)] for k in range(6))
    gmix = [norm_mix[l][None] for l in range(2)]
    gffn = [norm_ffn[l][None] for l in range(2)]

    wp = pack_blocks([shard[n] for n, _ in PACK], BF16)
    wall = all_gather_shards(wp)
    shapes = [shard[n].shape for n, _ in PACK]
    per_shard = [unpack_blocks(wall[s], shapes) for s in range(N_SHARD)]
    W = {n: jnp.concatenate([per_shard[s][k] for s in range(N_SHARD)], axis=ax) for k, (n, ax) in enumerate(PACK)}
    gw = W["gdn_w_in"][0]
    w_gdn = jnp.concatenate([gw, jnp.zeros((D, GDN_PROJ - gw.shape[1]), BF16)], axis=1)
    alog, dtb = _pad_lanes(gdn_a_log[0]), _pad_lanes(gdn_dt_bias[0])
    qg2 = jnp.concatenate([dsw_q_norm, dsw_q_norm], axis=1)
    kg2 = jnp.concatenate([dsw_k_norm, dsw_k_norm], axis=1)
    gdn_args = (w_gdn, conv_w, alog, dtb, gdn_out_norm, W["gdn_w_out"][0])
    dsw_args = (W["dsw_w_in"][0], qg2, kg2)

    (h10,) = rowwise(f_norm_only, [_wide(x0)], [gmix[0], sc1[0], sh1[0]], [_wide_out(S, BF16)], (nt,), "l0_norm")
    y0, sv_g = gdn_forward(h10, *gdn_args)
    x1, h20 = rowwise(f_resid_norm, [_wide(x0), _wide(y0)], [g1[0], gffn[0], sc2[0], sh2[0]], [_wide_out(S, F32), _wide_out(S, BF16)], (nt,), "l0_mid")
    f0, sv_f0 = ffn_forward(h20, W["w_ffn_in"][0], W["w_ffn_out"][0], "0")
    x2, h11 = rowwise(f_resid_norm, [_wide(x1), _wide(f0)], [g2[0], gmix[1], sc1[1], sh1[1]], [_wide_out(S, F32), _wide_out(S, BF16)], (nt,), "l1_in")
    y1, sv_d = dsw_forward(h11, *dsw_args, rel_bias, W["dsw_w_out"][0])
    x3, h21 = rowwise(f_resid_norm, [_wide(x2), _wide(y1)], [g1[1], gffn[1], sc2[1], sh2[1]], [_wide_out(S, F32), _wide_out(S, BF16)], (nt,), "l1_mid")
    f1, sv_f1 = ffn_forward(h21, W["w_ffn_in"][1], W["w_ffn_out"][1], "1")
    part_spec = lambda a: Row(a, (None, 1, D), lambda i: (i, 0, 0))
    (parts,) = rowwise(f_loss, [_wide(x3), _wide(f1), _wide(tgt)], [g2[1]], [Out((nt, 1, D), F32, (None, 1, D), lambda i: (i, 0, 0))], (nt,), "loss")
    loss = lax.psum(jnp.sum(parts), ("x", "y", "c"))

    (dx3, df1), (dg2_1,) = rowwise_bwd(f_loss, [_wide(x3), _wide(f1, gdtype=BF16), _wide(tgt, diff=False)], [g2[1]],
                                       [part_spec(jnp.ones((nt, 1, D), F32))], (nt,), "loss_bwd")
    dh21, d_win1, d_wout1 = ffn_backward(df1, sv_f1, W["w_ffn_in"][1], W["w_ffn_out"][1], "1")
    (dx2, dy1), (dg1_1, dgf1, dsc2_1, dsh2_1) = rowwise_bwd(
        f_resid_norm, [_wide(x2), _wide(y1, gdtype=BF16)], [g1[1], gffn[1], sc2[1], sh2[1]], [_wide(dx3), _wide(dh21)], (nt,), "l1_mid_bwd")
    dh11, g_d = dsw_backward(dy1, sv_d, *dsw_args, W["dsw_w_out"][0])
    (dx1, df0), (dg2_0, dgm1, dsc1_1, dsh1_1) = rowwise_bwd(
        f_resid_norm, [_wide(x1), _wide(f0, gdtype=BF16)], [g2[0], gmix[1], sc1[1], sh1[1]], [_wide(dx2), _wide(dh11)], (nt,), "l1_in_bwd")
    dh20, d_win0, d_wout0 = ffn_backward(df0, sv_f0, W["w_ffn_in"][0], W["w_ffn_out"][0], "0")
    (dx0p, dy0), (dg1_0, dgf0, dsc2_0, dsh2_0) = rowwise_bwd(
        f_resid_norm, [_wide(x0), _wide(y0, gdtype=BF16)], [g1[0], gffn[0], sc2[0], sh2[0]], [_wide(dx1), _wide(dh20)], (nt,), "l0_mid_bwd")
    dh10, g_g = gdn_backward(dy0, sv_g, *gdn_args)
    (grad_x,), (dgm0, dsc1_0, dsh1_0) = rowwise_bwd(f_first, [_wide(x0)], [gmix[0], sc1[0], sh1[0]], [_wide(dx0p), _wide(dh10)], (nt,), "l0_norm_bwd")

    dmod = jnp.concatenate([dsh1_0, dsc1_0, dg1_0, dsh2_0, dsc2_0, dg2_0, dsh1_1, dsc1_1, dg1_1, dsh2_1, dsc2_1, dg2_1], axis=1)
    d_rel = jnp.transpose(g_d["rel"][:, :, 0])
    fold = lambda v: v[:, :DSW_DH] + v[:, DSW_DH:]
    small = [dmod, jnp.concatenate([dgm0, dgm1], axis=1), jnp.concatenate([dgf0, dgf1], axis=1), g_g["conv"].reshape(1, -1),
             g_g["alog"], g_g["dtb"], g_g["gain"], _pad_lanes(fold(g_d["q_gain2"])[0]), _pad_lanes(fold(g_d["k_gain2"])[0]),
             d_rel.reshape(1, -1)]
    sizes = [v.shape[1] // LANES for v in small]
    n_rows = sum(sizes)
    pad_rows = -(-n_rows // 8) * 8
    sbuf = jnp.concatenate([v.reshape(-1, LANES) for v in small] + [jnp.zeros((pad_rows - n_rows, LANES), F32)], axis=0)
    sgot = all_gather_small(sbuf, "gather_small_grads")
    ssum = add_rows([sgot[d * pad_rows:(d + 1) * pad_rows] for d in range(N_DEV)], F32, "sum_small_grads", rt=pad_rows)
    offs = np.cumsum([0] + sizes)
    take = lambda k: ssum[offs[k]:offs[k + 1]].reshape(1, -1)
    grad_b_ada = take(0).reshape(2, 6 * D)
    grad_norm_mix = take(1).reshape(2, D)
    grad_norm_ffn = take(2).reshape(2, D)
    conv_full = take(3).reshape(4, -1)
    ncv = gdn_conv.shape[2]
    grad_gdn_conv = lax.dynamic_slice_in_dim(conv_full, s_me * ncv, ncv, axis=1)[None]
    grad_a_log = take(4)[:, :GDN_H]
    grad_dt_bias = take(5)[:, :GDN_H]
    grad_out_norm = take(6)
    grad_q_norm = take(7)[:, :DSW_DH]
    grad_k_norm = take(8)[:, :DSW_DH]
    grad_rel = take(9).reshape(REL_BUCKETS, 3 * GDN_H)
    dmod_all = sgot.reshape(N_DEV, pad_rows, LANES)[:, :sizes[0]].reshape(N_DEV, 2, 6 * D)
    dmod_mine = lax.dynamic_slice_in_dim(dmod_all, s_me * ada_cols, ada_cols, axis=2)
    dmod16 = jnp.concatenate([dmod_mine, jnp.zeros_like(dmod_mine)], axis=0)
    grad_w_ada = jnp.stack([matmul(cond16, dmod16[:, l], "tn", F32, f"ada_dw_{l}") for l in range(2)])

    full = {"gdn_w_in": g_g["w_in"][:, :gdn_w_in.shape[2] * N_SHARD][None], "gdn_w_out": g_g["w_out"][None],
            "w_ffn_in": jnp.stack([d_win0, d_win1]), "w_ffn_out": jnp.stack([d_wout0, d_wout1]),
            "dsw_w_in": g_d["w_in"][None], "dsw_w_out": g_d["w_out"][None]}
    g32 = jnp.stack([pack_blocks([_shard_slice(full[n], ax, s) for n, ax in PACK], F32) for s in range(N_SHARD)])
    R = g32.shape[1]
    Rh = R // 2
    keep = lax.dynamic_slice_in_dim(g32, ci * Rh, Rh, axis=1)
    give = lax.dynamic_slice_in_dim(g32, (1 - ci) * Rh, Rh, axis=1).astype(BF16)
    from_sib = sibling_exchange(give, "grads_to_sibling")
    part = add_rows([keep.reshape(N_SHARD * Rh, D), from_sib.reshape(N_SHARD * Rh, D)], BF16, "grads_chip_sum").reshape(N_SHARD, Rh, D)
    others = scatter_to_chips(part)
    own = lax.dynamic_index_in_dim(part, s_me, 0, keepdims=False)
    half_sum = add_rows([own, others[0], others[1], others[2]], F32, "grads_sum")
    sib_half = sibling_exchange(half_sum, "grads_from_sibling")
    lo = jnp.where(ci == 0, half_sum, sib_half)
    hi = jnp.where(ci == 0, sib_half, half_sum)
    gsh = dict(zip([n for n, _ in PACK], unpack_blocks(jnp.concatenate([lo, hi], axis=0), shapes)))

    grads = dict(w_ada=grad_w_ada, b_ada=grad_b_ada, norm_mix=grad_norm_mix, norm_ffn=grad_norm_ffn, w_ffn_in=gsh["w_ffn_in"],
                 w_ffn_out=gsh["w_ffn_out"], gdn_w_in=gsh["gdn_w_in"], gdn_conv=grad_gdn_conv, gdn_a_log=grad_a_log,
                 gdn_dt_bias=grad_dt_bias, gdn_out_norm=grad_out_norm, gdn_w_out=gsh["gdn_w_out"], dsw_w_in=gsh["dsw_w_in"],
                 dsw_q_norm=grad_q_norm, dsw_k_norm=grad_k_norm, dsw_w_out=gsh["dsw_w_out"], rel_bias=grad_rel)
    weights = dict(w_ada=w_ada, b_ada=b_ada, norm_mix=norm_mix, norm_ffn=norm_ffn, w_ffn_in=w_ffn_in, w_ffn_out=w_ffn_out,
                   gdn_w_in=gdn_w_in, gdn_conv=gdn_conv, gdn_a_log=gdn_a_log, gdn_dt_bias=gdn_dt_bias, gdn_out_norm=gdn_out_norm,
                   gdn_w_out=gdn_w_out, dsw_w_in=dsw_w_in, dsw_q_norm=dsw_q_norm, dsw_k_norm=dsw_k_norm, dsw_w_out=dsw_w_out,
                   rel_bias=rel_bias)
    ms = dict(w_ada=m_w_ada, b_ada=m_b_ada, norm_mix=m_norm_mix, norm_ffn=m_norm_ffn, w_ffn_in=m_w_ffn_in, w_ffn_out=m_w_ffn_out,
              gdn_w_in=m_gdn_w_in, gdn_conv=m_gdn_conv, gdn_a_log=m_gdn_a_log, gdn_dt_bias=m_gdn_dt_bias, gdn_out_norm=m_gdn_out_norm,
              gdn_w_out=m_gdn_w_out, dsw_w_in=m_dsw_w_in, dsw_q_norm=m_dsw_q_norm, dsw_k_norm=m_dsw_k_norm, dsw_w_out=m_dsw_w_out,
              rel_bias=m_rel_bias)
    vs = dict(w_ada=v_w_ada, b_ada=v_b_ada, norm_mix=v_norm_mix, norm_ffn=v_norm_ffn, w_ffn_in=v_w_ffn_in, w_ffn_out=v_w_ffn_out,
              gdn_w_in=v_gdn_w_in, gdn_conv=v_gdn_conv, gdn_a_log=v_gdn_a_log, gdn_dt_bias=v_gdn_dt_bias, gdn_out_norm=v_gdn_out_norm,
              gdn_w_out=v_gdn_w_out, dsw_w_in=v_dsw_w_in, dsw_q_norm=v_dsw_q_norm, dsw_k_norm=v_dsw_k_norm, dsw_w_out=v_dsw_w_out,
              rel_bias=v_rel_bias)
    names = list(weights)
    deltas, new_m, new_v = [], [], []
    for n in names:
        g = grads[n].reshape(weights[n].shape)
        grads[n] = g
        d, nm, nv = adamw(weights[n], g, ms[n], vs[n], f"adamw_{n}")
        deltas.append(d)
        new_m.append(nm)
        new_v.append(nv)
    return (loss, grad_x[None], *[grads[n] for n in names], *deltas, *new_m, *new_v)
```

```python
import functools
import math

import numpy as np
import jax
import jax.numpy as jnp
from jax import lax
from jax.experimental import pallas as pl
from jax.experimental.pallas import tpu as pltpu

F32 = jnp.float32
BF16 = jnp.bfloat16
SDS = jax.ShapeDtypeStruct
MESH = pl.DeviceIdType.MESH
ANY = pl.BlockSpec(memory_space=pl.ANY)

D = 1024
EPS = 1e-6
LANES = 128
GDN_H = 8
GDN_DK = 128
GDN_C = 64
DSW_GROUPS = ((128, 1), (512, 4), (2048, 16))
DSW_SPAN = 128
DSW_DH = 64
DSW_HG = 512
REL_BUCKETS = 32
REL_MAX_DIST = 2048
FFN = 2816
N_SHARD = 4
N_DEV = 8
VMEM_LIMIT = 48 * 1024 * 1024
NEG = -1e30

ADAM_LR, ADAM_B1, ADAM_B2, ADAM_EPS, ADAM_WD, ADAM_STEP = 0.001, 0.9, 0.999, 1e-08, 0.01, 10


def _cp(n_axes):
    return pltpu.CompilerParams(dimension_semantics=("arbitrary",) * n_axes, vmem_limit_bytes=VMEM_LIMIT)


def _blk(dim, cap):
    if dim <= cap:
        return dim
    best = None
    for b in range(LANES, cap + 1, LANES):
        if dim % b == 0:
            best = b
    assert best is not None, (dim, cap)
    return best


def matmul(a, b, mode, out_dtype, name, cap_m=1024, cap_n=1024, cap_k=2048):
    if mode == "nn":
        (M, K), (K2, N) = a.shape, b.shape
    elif mode == "nt":
        (M, K), (N, K2) = a.shape, b.shape
    else:
        (K, M), (K2, N) = a.shape, b.shape
    assert K == K2, (a.shape, b.shape, mode)
    if K <= 3072:
        cap_k = K
        if K > 2048:
            cap_n = 512
    bm, bn, bk = _blk(M, cap_m), _blk(N, cap_n), _blk(K, cap_k)
    nk = K // bk
    dims = {"nn": ((1,), (0,)), "nt": ((1,), (1,)), "tn": ((0,), (0,))}[mode]

    def dot(a_ref, b_ref):
        return lax.dot_general(a_ref[...].astype(BF16), b_ref[...].astype(BF16), (dims, ((), ())), preferred_element_type=F32)

    def body_one(a_ref, b_ref, o_ref):
        o_ref[...] = dot(a_ref, b_ref).astype(o_ref.dtype)

    def body_acc(a_ref, b_ref, o_ref, acc_ref):
        k = pl.program_id(2)

        @pl.when(k == 0)
        def _():
            acc_ref[...] = jnp.zeros_like(acc_ref)

        acc_ref[...] += dot(a_ref, b_ref)

        @pl.when(k == nk - 1)
        def _():
            o_ref[...] = acc_ref[...].astype(o_ref.dtype)

    a_spec = pl.BlockSpec((bk, bm), lambda i, j, k: (k, i)) if mode == "tn" else pl.BlockSpec((bm, bk), lambda i, j, k: (i, k))
    b_spec = pl.BlockSpec((bn, bk), lambda i, j, k: (j, k)) if mode == "nt" else pl.BlockSpec((bk, bn), lambda i, j, k: (k, j))
    return pl.pallas_call(
        body_one if nk == 1 else body_acc, name=name, grid=(M // bm, N // bn, nk),
        in_specs=[a_spec, b_spec], out_specs=pl.BlockSpec((bm, bn), lambda i, j, k: (i, j)),
        out_shape=SDS((M, N), out_dtype), scratch_shapes=[] if nk == 1 else [pltpu.VMEM((bm, bn), F32)],
        compiler_params=_cp(3),
    )(a, b)


class Row:
    def __init__(self, arr, bshape, imap, splits=None, diff=True, acc=False, gdtype=F32, gshape=None, gbshape=None, gimap=None,
                 lead=0):
        self.arr, self.bshape, self.imap = arr, tuple(bshape), imap
        self.splits, self.lead = splits, lead
        self.diff, self.acc, self.gdtype = diff, acc, gdtype
        self.gshape = tuple(arr.shape) if gshape is None else tuple(gshape)
        self.gbshape = self.bshape if gbshape is None else tuple(gbshape)
        self.gimap = imap if gimap is None else gimap

    def gspec(self):
        return pl.BlockSpec(self.gbshape, self.gimap)

    def spec(self):
        return pl.BlockSpec(self.bshape, self.imap)

    def pieces(self, ref):
        return _load_pieces(ref, self.splits, self.lead)

    def n_pieces(self):
        return _n_pieces(self.splits, self.lead)


class Out:
    def __init__(self, shape, dtype, bshape, imap, splits=None, lead=0):
        self.shape, self.dtype, self.bshape, self.imap = tuple(shape), dtype, tuple(bshape), imap
        self.splits, self.lead = splits, lead

    def n_pieces(self):
        return _n_pieces(self.splits, self.lead)


def _n_pieces(splits, lead):
    return lead if lead else (1 if splits is None else len(splits))


def _load_pieces(ref, splits, lead):
    if lead:
        return [ref[k].astype(F32) for k in range(lead)]
    if splits is None:
        return [ref[...].astype(F32)]
    out, o = [], 0
    for w in splits:
        out.append(ref[..., o:o + w].astype(F32))
        o += w
    return out


def _store_pieces(ref, splits, lead, vals, accumulate=False):
    def put(idx, v):
        if accumulate:
            ref[idx] += v.astype(ref.dtype)
        else:
            ref[idx] = v.astype(ref.dtype)

    if lead:
        for k in range(lead):
            put(k, vals[k])
    elif splits is None:
        put(..., vals[0])
    else:
        o = 0
        for w, v in zip(splits, vals):
            put((..., slice(o, o + w)), v)
            o += w


def rowwise(fn, rows, params, outs, grid, name):
    nr, npar = len(rows), len(params)

    def body(*refs):
        ids = tuple(pl.program_id(a) for a in range(len(grid)))
        vals = []
        for r, ref in zip(rows, refs[:nr]):
            vals += r.pieces(ref)
        pvals = [ref[...].astype(F32) for ref in refs[nr:nr + npar]]
        res = list(fn(ids, *vals, *pvals))
        o = 0
        for spec, ref in zip(outs, refs[nr + npar:]):
            n = spec.n_pieces()
            _store_pieces(ref, spec.splits, spec.lead, res[o:o + n])
            o += n

    nz = len(grid)
    pspecs = [pl.BlockSpec(p.shape, (lambda *ids, _n=p.ndim: (0,) * _n)) for p in params]
    res = pl.pallas_call(
        body, name=name, grid=grid,
        in_specs=[r.spec() for r in rows] + pspecs,
        out_specs=[pl.BlockSpec(o.bshape, o.imap) for o in outs],
        out_shape=[SDS(o.shape, o.dtype) for o in outs],
        compiler_params=_cp(nz),
    )(*[r.arr for r in rows], *params)
    return list(res)


def rowwise_bwd(fn, rows, params, cots, grid, name):
    nr, npar, nc = len(rows), len(params), len(cots)
    drows = [r for r in rows if r.diff]
    nz = len(grid)

    def body(*refs):
        ids = tuple(pl.program_id(a) for a in range(nz))
        row_refs, par_refs = refs[:nr], refs[nr:nr + npar]
        cot_refs = refs[nr + npar:nr + npar + nc]
        drow_refs = refs[nr + npar + nc:nr + npar + nc + len(drows)]
        dpar_refs = refs[nr + npar + nc + len(drows):]
        pieces, is_diff = [], []
        for r, ref in zip(rows, row_refs):
            ps = r.pieces(ref)
            pieces += ps
            is_diff += [r.diff] * len(ps)
        pvals = [ref[...].astype(F32) for ref in par_refs]
        dvals = [p for p, dflag in zip(pieces, is_diff) if dflag]
        nd = len(dvals)

        def f(*args):
            it = iter(args[:nd])
            full = [next(it) if dflag else p for p, dflag in zip(pieces, is_diff)]
            return tuple(fn(ids, *full, *args[nd:]))

        _, vjp = jax.vjp(f, *dvals, *pvals)
        cvals = []
        for c, ref in zip(cots, cot_refs):
            cvals += c.pieces(ref)
        g = vjp(tuple(cvals))
        o = 0
        first_inner = ids[-1] == 0
        for r, ref in zip(drows, drow_refs):
            n = r.n_pieces()
            gs = g[o:o + n]
            o += n
            if r.acc:
                @pl.when(first_inner)
                def _(ref=ref):
                    ref[...] = jnp.zeros_like(ref)
            _store_pieces(ref, r.splits, r.lead, gs, accumulate=r.acc)
        first = functools.reduce(jnp.logical_and, [i == 0 for i in ids])
        for ref, gp in zip(dpar_refs, g[nd:]):
            @pl.when(first)
            def _(ref=ref):
                ref[...] = jnp.zeros_like(ref)
            ref[...] += gp

    pspecs = [pl.BlockSpec(p.shape, (lambda *ids, _n=p.ndim: (0,) * _n)) for p in params]
    res = pl.pallas_call(
        body, name=name, grid=grid,
        in_specs=[r.spec() for r in rows] + pspecs + [c.spec() for c in cots],
        out_specs=[r.gspec() for r in drows] + pspecs,
        out_shape=[SDS(r.gshape, r.gdtype) for r in drows] + [SDS(p.shape, F32) for p in params],
        compiler_params=_cp(nz),
    )(*[r.arr for r in rows], *params, *[c.arr for c in cots])
    res = list(res)
    return res[:len(drows)], res[len(drows):]


def _sigmoid(x):
    return 0.5 * (jnp.tanh(0.5 * x) + 1.0)


def _silu(x):
    return x * _sigmoid(x)


def _normmod(x, gain, sc, sh):
    inv = lax.rsqrt(jnp.mean(x * x, axis=-1, keepdims=True) + EPS)
    return x * inv * gain * (1.0 + sc) + sh


def f_first(ids, x, gain, sc, sh):
    return x, _normmod(x, gain, sc, sh)


def f_resid_norm(ids, x, y, g, gain, sc, sh):
    xn = x + g * y
    return xn, _normmod(xn, gain, sc, sh)


def f_swiglu(ids, gate, up):
    return (_silu(gate) * up,)


def f_loss(ids, x, y, tgt, g):
    out = x + g * y
    e = out - tgt
    part = 0.5 * jnp.sum(e * e, axis=0, keepdims=True) * (1.0 / D)
    return (part,)


def _softplus(x):
    return jnp.maximum(x, 0.0) + jnp.log(1.0 + jnp.exp(-jnp.abs(x)))


def _chunk_tril(T):
    r = lax.broadcasted_iota(jnp.int32, (T, T), 0)
    c = lax.broadcasted_iota(jnp.int32, (T, T), 1)
    return jnp.where((r // GDN_C == c // GDN_C) & (c <= r), 1.0, 0.0).astype(F32)


def _dot_hi(a, b, dims=((1,), (0,))):
    return lax.dot_general(a, b, (dims, ((), ())), precision=lax.Precision.HIGHEST, preferred_element_type=F32)


def _dot_x3(a, b, dims=((1,), (0,))):
    return lax.dot_general(a, b, (dims, ((), ())), precision=lax.Precision.HIGH, preferred_element_type=F32)


def f_gdn_gates(ids, ab, alog, dtb):
    T = ab.shape[0]
    g = -jnp.exp(alog) * _softplus(ab + dtb)
    beta = _sigmoid(ab)
    gcum = _dot_hi(_chunk_tril(T), g)
    row = lax.broadcasted_iota(jnp.int32, (LANES, LANES), 0)
    sel = lambda k: jnp.where(row == k, 1.0, 0.0).astype(F32)
    gcs = [_dot_hi(gcum, sel(h)) for h in range(GDN_H)]
    bts = [_dot_hi(beta, sel(GDN_H + h)) for h in range(GDN_H)]
    return (*gcs, *bts)


def f_gdn_post(ids, *args):
    os_, zs, gain = args[:GDN_H], args[GDN_H:2 * GDN_H], args[2 * GDN_H]
    out = []
    for o, z in zip(os_, zs):
        inv = lax.rsqrt(jnp.mean(o * o, axis=-1, keepdims=True) + EPS)
        out.append(o * inv * gain * _silu(z))
    return tuple(out)


def _qknorm1(x, gain2, scale):
    lane = lax.broadcasted_iota(jnp.int32, x.shape, 1)
    lo = lane < DSW_DH
    x2 = x * x
    s_all = jnp.sum(x2, axis=-1, keepdims=True)
    s_lo = jnp.sum(jnp.where(lo, x2, 0.0), axis=-1, keepdims=True)
    ms = jnp.where(lo, s_lo, s_all - s_lo) * (1.0 / DSW_DH)
    return x * lax.rsqrt(ms + EPS) * (gain2 * scale)


def f_qknorm(ids, *args):
    return tuple(_qknorm1(x, args[-1], 1.0) for x in args[:-1])


def f_qnorm(ids, *args):
    return tuple(_qknorm1(x, args[-1], DSW_DH ** -0.5) for x in args[:-1])


def f_combine(ids, o0, o1, o2, l0, l1, l2):
    m = jnp.maximum(jnp.maximum(l0, l1), l2)
    e0, e1, e2 = jnp.exp(l0 - m), jnp.exp(l1 - m), jnp.exp(l2 - m)
    den = e0 + e1 + e2
    o = (e0 * o0 + e1 * o1 + e2 * o2) / den
    return o, m + jnp.log(den)


GDN_T = 512
HALO = 8


def _conv_pre(xx, w):
    acc = xx * w[3:4, :]
    for j in range(3):
        acc = acc + pltpu.roll(xx, shift=3 - j, axis=0) * w[j:j + 1, :]
    return acc


def _qkv_act(pre, cidx):
    s = _silu(pre)
    r = lax.rsqrt(jnp.sum(s * s, axis=-1, keepdims=True) + EPS)
    scale = jnp.where(cidx < GDN_H, GDN_DK ** -0.5, 1.0).astype(F32)
    return jnp.where(cidx < 2 * GDN_H, s * r * scale, s)


def gdn_pre(proj, conv_w, S):
    nt = S // GDN_T
    hb = GDN_T // HALO

    def body(prev_ref, cur_ref, w_ref, o_ref):
        p, i = pl.program_id(0), pl.program_id(1)
        for h in range(GDN_H):
            cols = slice(LANES * h, LANES * (h + 1))
            prev = jnp.where(i > 0, prev_ref[:, cols], 0.0)
            xx = jnp.concatenate([prev, cur_ref[:, cols]], axis=0)
            pre = _conv_pre(xx, w_ref[:, cols])[HALO:]
            o_ref[h] = _qkv_act(pre, p * GDN_H + h)

    hv = GDN_H * LANES
    return pl.pallas_call(
        body, name="gdn_pre", grid=(3, nt),
        in_specs=[pl.BlockSpec((HALO, hv), lambda p, i: (jnp.maximum(i * hb - 1, 0), p)),
                  pl.BlockSpec((GDN_T, hv), lambda p, i: (i, p)),
                  pl.BlockSpec((4, hv), lambda p, i: (0, p))],
        out_specs=pl.BlockSpec((None, GDN_H, GDN_T, LANES), lambda p, i: (p, 0, i, 0)),
        out_shape=SDS((3, GDN_H, S, LANES), F32),
        compiler_params=_cp(2),
    )(proj, proj, conv_w)


def gdn_pre_bwd(proj, conv_w, dqkv, S):
    nt = S // GDN_T
    hb = GDN_T // HALO
    last_h = S // HALO - 1

    def body(prev_ref, cur_ref, next_ref, w_ref, d_ref, dnext_ref, dx_ref, dw_ref):
        p, i = pl.program_id(0), pl.program_id(1)

        @pl.when(i == 0)
        def _():
            dw_ref[...] = jnp.zeros_like(dw_ref)

        for h in range(GDN_H):
            cols = slice(LANES * h, LANES * (h + 1))
            w = w_ref[:, cols]
            prev = jnp.where(i > 0, prev_ref[:, cols], 0.0)
            xx = jnp.concatenate([prev, cur_ref[:, cols], next_ref[:, cols]], axis=0)
            dnext = jnp.where(i < nt - 1, dnext_ref[h], 0.0)
            dd = jnp.concatenate([jnp.zeros((HALO, LANES), F32), d_ref[h], dnext], axis=0)
            pre = _conv_pre(xx, w)
            _, vjp = jax.vjp(lambda v, _c=p * GDN_H + h: _qkv_act(v, _c), pre)
            (dpre,) = vjp(dd)
            row = lax.broadcasted_iota(jnp.int32, dpre.shape, 0)
            dpre = jnp.where(row >= HALO, dpre, 0.0)
            dx = dpre * w[3:4, :]
            R = dpre.shape[0]
            for j in range(3):
                dx = dx + pltpu.roll(dpre, shift=R - (3 - j), axis=0) * w[j:j + 1, :]
            dx_ref[:, cols] = dx[HALO:HALO + GDN_T].astype(dx_ref.dtype)
            own = jnp.where(row < HALO + GDN_T, dpre, 0.0)
            rows_w = [jnp.sum(own * pltpu.roll(xx, shift=3 - j, axis=0), axis=0, keepdims=True) for j in range(3)]
            rows_w.append(jnp.sum(own * xx, axis=0, keepdims=True))
            r4 = lax.broadcasted_iota(jnp.int32, (4, LANES), 0)
            dw = jnp.zeros((4, LANES), F32)
            for j in range(4):
                dw = dw + jnp.where(r4 == j, rows_w[j], 0.0)
            dw_ref[:, cols] += dw

    hv = GDN_H * LANES
    return pl.pallas_call(
        body, name="gdn_pre_bwd", grid=(3, nt),
        in_specs=[pl.BlockSpec((HALO, hv), lambda p, i: (jnp.maximum(i * hb - 1, 0), p)),
                  pl.BlockSpec((GDN_T, hv), lambda p, i: (i, p)),
                  pl.BlockSpec((HALO, hv), lambda p, i: (jnp.minimum((i + 1) * hb, last_h), p)),
                  pl.BlockSpec((4, hv), lambda p, i: (0, p)),
                  pl.BlockSpec((None, GDN_H, GDN_T, LANES), lambda p, i: (p, 0, i, 0)),
                  pl.BlockSpec((None, GDN_H, HALO, LANES), lambda p, i: (p, 0, jnp.minimum((i + 1) * hb, last_h), 0))],
        out_specs=[pl.BlockSpec((GDN_T, hv), lambda p, i: (i, p)),
                   pl.BlockSpec((4, hv), lambda p, i: (0, p))],
        out_shape=[SDS((S, 3 * hv), BF16), SDS((4, 3 * hv), F32)],
        compiler_params=_cp(2),
    )(proj, proj, proj, conv_w, dqkv, dqkv)


_DIMS = {"nn": ((1,), (0,)), "nt": ((1,), (1,)), "tn": ((0,), (0,))}


def _mm_raw(a, b, mode, hi):
    if hi:
        return _dot_hi(a, b, _DIMS[mode])
    return lax.dot_general(a.astype(BF16), b.astype(BF16), (_DIMS[mode], ((), ())), preferred_element_type=F32)


@functools.partial(jax.custom_vjp, nondiff_argnums=(2, 3))
def mm(a, b, mode, hi):
    return _mm_raw(a, b, mode, hi)


def _mm_fwd(a, b, mode, hi):
    return _mm_raw(a, b, mode, hi), (a, b)


def _mm_bwd(mode, hi, res, dc):
    a, b = res
    if mode == "nn":
        da, db = mm(dc, b, "nt", hi), mm(a, dc, "tn", hi)
    elif mode == "nt":
        da, db = mm(dc, b, "nn", hi), mm(dc, a, "tn", hi)
    else:
        da, db = mm(b, dc, "nt", hi), mm(a, dc, "nn", hi)
    return da, db


mm.defvjp(_mm_fwd, _mm_bwd)


TRI_BASE = 8


def _unit_lower_inverses(Ls):
    n = Ls[0].shape[0]
    r = lax.broadcasted_iota(jnp.int32, (n, n), 0)
    c = lax.broadcasted_iota(jnp.int32, (n, n), 1)
    eye = jnp.where(r == c, 1.0, 0.0).astype(F32)
    base = r // TRI_BASE == c // TRI_BASE
    Ps = [jnp.where(base, -L, 0.0) for L in Ls]
    invs = [eye + P for P in Ps]
    k = 1
    while 2 * k < TRI_BASE:
        Ps = [_dot_x3(P, P) for P in Ps]
        invs = [inv + _dot_x3(inv, P) for inv, P in zip(invs, Ps)]
        k *= 2
    b = 2 * TRI_BASE
    while b <= n:
        off_mask = (r // b == c // b) & ((r % b) >= b // 2) & ((c % b) < b // 2)
        ts = [_dot_x3(inv, jnp.where(off_mask, L, 0.0)) for inv, L in zip(invs, Ls)]
        invs = [inv - _dot_x3(t, inv) for inv, t in zip(invs, ts)]
        b *= 2
    return invs


@jax.custom_vjp
def tri_solve2(Ls, r1s, r2s):
    invs = _unit_lower_inverses(Ls)
    return [_dot_x3(i, r) for i, r in zip(invs, r1s)], [_dot_x3(i, r) for i, r in zip(invs, r2s)]


def _tri_fwd(Ls, r1s, r2s):
    invs = _unit_lower_inverses(Ls)
    s1s = [_dot_x3(i, r) for i, r in zip(invs, r1s)]
    s2s = [_dot_x3(i, r) for i, r in zip(invs, r2s)]
    return (s1s, s2s), (invs, s1s, s2s)


def _tri_bwd(res, ds):
    invs, s1s, s2s = res
    d1s = [_dot_x3(i, d, _DIMS["tn"]) for i, d in zip(invs, ds[0])]
    d2s = [_dot_x3(i, d, _DIMS["tn"]) for i, d in zip(invs, ds[1])]
    dLs = [-(_dot_x3(d1, s1, _DIMS["nt"]) + _dot_x3(d2, s2, _DIMS["nt"])) for d1, s1, d2, s2 in zip(d1s, s1s, d2s, s2s)]
    return dLs, d1s, d2s


tri_solve2.defvjp(_tri_fwd, _tri_bwd)


def _gdn_chunk(qs, ks, vs, gcbs, btbs, Ss):
    C = qs[0].shape[0]
    r = lax.broadcasted_iota(jnp.int32, (C, C), 0)
    c = lax.broadcasted_iota(jnp.int32, (C, C), 1)
    causal, strict = c <= r, c < r
    rows = lax.broadcasted_iota(jnp.int32, gcbs[0].shape, 0)
    Gs = [g[:, :C] for g in gcbs]
    decays = [jnp.exp(jnp.where(causal, G - G.T, NEG)) for G in Gs]
    kbs = [k * b for k, b in zip(ks, btbs)]
    vbs = [v * b for v, b in zip(vs, btbs)]
    Ls = [jnp.where(strict, mm(kb, k, "nt", False) * d, 0.0) for kb, k, d in zip(kbs, ks, decays)]
    egs = [jnp.exp(g) for g in gcbs]
    us, ws = tri_solve2(Ls, vbs, [kb * eg for kb, eg in zip(kbs, egs)])
    qks = [jnp.where(causal, mm(q, k, "nt", False) * d, 0.0) for q, k, d in zip(qs, ks, decays)]
    g_lasts = [jnp.sum(jnp.where(rows == C - 1, g, 0.0), axis=0, keepdims=True) for g in gcbs]
    q_decs = [q * eg for q, eg in zip(qs, egs)]
    k_decs = [k * jnp.exp(gl - g) for k, gl, g in zip(ks, g_lasts, gcbs)]
    v_news = [u - mm(w, S, "nn", False) for u, w, S in zip(us, ws, Ss)]
    os_ = [mm(qd, S, "nn", False) + mm(qk, vn, "nn", False) for qd, S, qk, vn in zip(q_decs, Ss, qks, v_news)]
    S_news = [S * jnp.exp(gl) + mm(kd, vn, "tn", False) for S, gl, kd, vn in zip(Ss, g_lasts, k_decs, v_news)]
    return os_, S_news


def gdn_core(qkv, gc, bt, S):
    nchunk = S // GDN_C

    def body(qkv_ref, g_ref, b_ref, o_ref, st_ref, s_scr):
        n = pl.program_id(0)

        @pl.when(n == 0)
        def _():
            s_scr[...] = jnp.zeros_like(s_scr)

        heads = range(GDN_H)
        S_in = [s_scr[h] for h in heads]
        os_, S_new = _gdn_chunk([qkv_ref[0, h] for h in heads], [qkv_ref[1, h] for h in heads], [qkv_ref[2, h] for h in heads],
                                [g_ref[h] for h in heads], [b_ref[h] for h in heads], S_in)
        for h in heads:
            st_ref[h] = S_in[h]
            o_ref[h] = os_[h]
            s_scr[h] = S_new[h]

    blk3 = pl.BlockSpec((3, GDN_H, GDN_C, LANES), lambda n: (0, 0, n, 0))
    hb = pl.BlockSpec((GDN_H, GDN_C, LANES), lambda n: (0, n, 0))
    return pl.pallas_call(
        body, name="gdn_core", grid=(nchunk,),
        in_specs=[blk3, hb, hb],
        out_specs=[hb, pl.BlockSpec((GDN_H, None, GDN_DK, LANES), lambda n: (0, n, 0, 0))],
        out_shape=[SDS((GDN_H, S, LANES), F32), SDS((GDN_H, nchunk, GDN_DK, LANES), F32)],
        scratch_shapes=[pltpu.VMEM((GDN_H, GDN_DK, LANES), F32)],
        compiler_params=_cp(1),
    )(qkv, gc, bt)


def gdn_core_bwd(qkv, gc, bt, states, do, S):
    nchunk = S // GDN_C

    def body(qkv_ref, g_ref, b_ref, st_ref, do_ref, dqkv_ref, dg_ref, db_ref, ds_scr):
        n = pl.program_id(0)

        @pl.when(n == 0)
        def _():
            ds_scr[...] = jnp.zeros_like(ds_scr)

        heads = range(GDN_H)
        _, vjp = jax.vjp(_gdn_chunk, [qkv_ref[0, h] for h in heads], [qkv_ref[1, h] for h in heads], [qkv_ref[2, h] for h in heads],
                         [g_ref[h] for h in heads], [b_ref[h] for h in heads], [st_ref[h] for h in heads])
        dq, dk, dv, dg, db, dS = vjp(([do_ref[h] for h in heads], [ds_scr[h] for h in heads]))
        for h in heads:
            dqkv_ref[0, h] = dq[h]
            dqkv_ref[1, h] = dk[h]
            dqkv_ref[2, h] = dv[h]
            dg_ref[h] = dg[h]
            db_ref[h] = db[h]
            ds_scr[h] = dS[h]

    rev = lambda n: nchunk - 1 - n
    blk3 = pl.BlockSpec((3, GDN_H, GDN_C, LANES), lambda n: (0, 0, rev(n), 0))
    hb = pl.BlockSpec((GDN_H, GDN_C, LANES), lambda n: (0, rev(n), 0))
    return pl.pallas_call(
        body, name="gdn_core_bwd", grid=(nchunk,),
        in_specs=[blk3, hb, hb, pl.BlockSpec((GDN_H, None, GDN_DK, LANES), lambda n: (0, rev(n), 0, 0)), hb],
        out_specs=[blk3, hb, hb],
        out_shape=[SDS((3, GDN_H, S, LANES), F32), SDS((GDN_H, S, LANES), F32), SDS((GDN_H, S, LANES), F32)],
        scratch_shapes=[pltpu.VMEM((GDN_H, GDN_DK, LANES), F32)],
        compiler_params=_cp(1),
    )(qkv, gc, bt, states, do)


GDN_MAIN = 4 * GDN_H * LANES
GDN_PROJ = GDN_MAIN + LANES
RT = 256


def gdn_forward(h, w_in, conv_w, alog, dtb, out_gain, w_out):
    S = h.shape[0]
    nt = S // RT
    proj = matmul(h, w_in, "nn", F32, "gdn_in")
    qkv = gdn_pre(proj, conv_w, S)
    ab_row = Row(proj, (RT, LANES), lambda i: (i, GDN_MAIN // LANES), gdtype=BF16, gshape=(S, LANES), gimap=lambda i: (i, 0))
    hm = lambda i: (0, i, 0)
    hv = GDN_H * LANES
    gc, bt = rowwise(f_gdn_gates, [ab_row], [alog, dtb],
                     [Out((GDN_H, S, LANES), F32, (GDN_H, RT, LANES), hm, lead=GDN_H)] * 2, (nt,), "gdn_gates")
    o, states = gdn_core(qkv, gc, bt, S)
    o_row = Row(o, (GDN_H, RT, LANES), hm, lead=GDN_H)
    z_row = Row(proj, (RT, hv), lambda i: (i, 3), splits=[LANES] * GDN_H, gdtype=BF16, gshape=(S, hv), gimap=lambda i: (i, 0))
    (on,) = rowwise(f_gdn_post, [o_row, z_row], [out_gain],
                    [Out((S, hv), BF16, (RT, hv), lambda i: (i, 0), splits=[LANES] * GDN_H)], (nt,), "gdn_post")
    y = matmul(on, w_out, "nn", F32, "gdn_out")
    saved = dict(h=h, proj=proj, qkv=qkv, gc=gc, bt=bt, states=states, o=o, on=on, ab_row=ab_row, o_row=o_row, z_row=z_row)
    return y, saved


def gdn_backward(dy, sv, w_in, conv_w, alog, dtb, out_gain, w_out):
    S = dy.shape[0]
    nt = S // RT
    hm = lambda i: (0, i, 0)
    hv = GDN_H * LANES
    don = matmul(dy, w_out, "nt", F32, "gdn_out_dx")
    d_w_out = matmul(sv["on"], dy, "tn", F32, "gdn_out_dw")
    (do, dz), (d_gain,) = rowwise_bwd(f_gdn_post, [sv["o_row"], sv["z_row"]], [out_gain],
                                      [Row(don, (RT, hv), lambda i: (i, 0), splits=[LANES] * GDN_H)], (nt,), "gdn_post_bwd")
    dqkv, dgc, dbt = gdn_core_bwd(sv["qkv"], sv["gc"], sv["bt"], sv["states"], do, S)
    head_blk = lambda a: Row(a, (GDN_H, RT, LANES), hm, lead=GDN_H)
    (dab,), (d_alog, d_dtb) = rowwise_bwd(f_gdn_gates, [sv["ab_row"]], [alog, dtb], [head_blk(dgc), head_blk(dbt)],
                                          (nt,), "gdn_gates_bwd")
    dqkv_proj, d_conv = gdn_pre_bwd(sv["proj"], conv_w, dqkv, S)
    dproj = jnp.concatenate([dqkv_proj, dz, dab], axis=1)
    d_w_in = matmul(sv["h"], dproj, "tn", F32, "gdn_in_dw")
    dh = matmul(dproj, w_in, "nt", F32, "gdn_in_dx")
    return dh, dict(w_in=d_w_in, conv=d_conv, alog=d_alog, dtb=d_dtb, gain=d_gain, w_out=d_w_out)


QB = DSW_SPAN
N_HP = DSW_HG // LANES
PROJ_BLKS = 3 * 3 * N_HP


def _bucket_maps():
    a = np.arange(QB)[:, None]
    j = np.arange(2 * QB)[None, :]
    dist = QB + a - j
    band = (dist >= 0) & (dist <= DSW_SPAN)
    maps = []
    for _, dil in DSW_GROUPS:
        dd = np.maximum(dist, 0) * dil
        max_exact = REL_BUCKETS // 2
        scaled = np.log(np.maximum(dd, 1).astype(np.float32) / np.float32(max_exact)) / np.float32(math.log(REL_MAX_DIST / max_exact))
        large = max_exact + (scaled * np.float32(REL_BUCKETS - max_exact)).astype(np.int32)
        large = np.minimum(large, REL_BUCKETS - 1)
        maps.append(np.where(dd < max_exact, dd, large).astype(np.int32))
    return np.stack(maps), band


def dsw_bias(rel_bias):
    maps, band = _bucket_maps()
    maps = np.where(band[None], maps, -1).astype(np.int32)

    def body(tab_ref, bk_ref, o_ref):
        gh = pl.program_id(0)
        bk = bk_ref[...]
        acc = jnp.full(bk.shape, NEG, F32)
        for b in range(REL_BUCKETS):
            acc = jnp.where(bk == b, tab_ref[b, gh], acc)
        o_ref[...] = acc

    return pl.pallas_call(
        body, name="dsw_bias", grid=(3 * GDN_H,),
        in_specs=[pl.BlockSpec(memory_space=pltpu.SMEM),
                  pl.BlockSpec((None, QB, 2 * QB), lambda gh: (gh // GDN_H, 0, 0))],
        out_specs=pl.BlockSpec((None, QB, 2 * QB), lambda gh: (gh, 0, 0)),
        out_shape=SDS((3 * GDN_H, QB, 2 * QB), F32),
        compiler_params=_cp(1),
    )(rel_bias, jnp.asarray(maps))


def dsw_bias_grad(dbias):
    maps, band = _bucket_maps()
    maps = np.where(band[None], maps, -1).astype(np.int32)

    def body(d_ref, bk_ref, o_ref):
        bk = bk_ref[...]
        d = d_ref[...]
        rows = lax.broadcasted_iota(jnp.int32, (REL_BUCKETS, LANES), 0)
        acc = jnp.zeros((REL_BUCKETS, LANES), F32)
        for b in range(REL_BUCKETS):
            part = jnp.sum(jnp.where(bk == b, d, 0.0), axis=0, keepdims=True)
            val = jnp.sum(part, axis=1, keepdims=True)
            acc = jnp.where(rows == b, val, acc)
        o_ref[...] = acc

    return pl.pallas_call(
        body, name="dsw_bias_grad", grid=(3 * GDN_H,),
        in_specs=[pl.BlockSpec((None, QB, 2 * QB), lambda gh: (gh, 0, 0)),
                  pl.BlockSpec((None, QB, 2 * QB), lambda gh: (gh // GDN_H, 0, 0))],
        out_specs=pl.BlockSpec((None, REL_BUCKETS, LANES), lambda gh: (gh, 0, 0)),
        out_shape=SDS((3 * GDN_H, REL_BUCKETS, LANES), F32),
        compiler_params=_cp(1),
    )(dbias, jnp.asarray(maps))


def _nt(a, b):
    return lax.dot_general(a, b, (((1,), (1,)), ((), ())), preferred_element_type=F32)


def _tn(a, b):
    return lax.dot_general(a, b, (((0,), (0,)), ((), ())), preferred_element_type=F32)


def dsw_group_fwd(qn, kn, proj, bias, gi, S):
    dil = DSW_GROUPS[gi][1]
    sd = S // dil
    nq = sd // QB
    qv = qn.reshape(sd, dil * 3 * DSW_HG)
    kv = kn.reshape(sd, dil * 3 * DSW_HG)
    pv = proj.reshape(sd, dil * 9 * DSW_HG)
    qk_col = lambda hp, r: r * (3 * N_HP) + gi * N_HP + hp
    v_col = lambda hp, r: r * PROJ_BLKS + 2 * 3 * N_HP + gi * N_HP + hp

    def body(q_ref, kp_ref, kc_ref, vp_ref, vc_ref, b_ref, o_ref, l_ref):
        i = pl.program_id(2)
        q = q_ref[...]
        k2 = jnp.concatenate([kp_ref[...], kc_ref[...]], axis=0)
        v2 = jnp.concatenate([vp_ref[...], vc_ref[...]], axis=0).astype(BF16)
        lane_q = lax.broadcasted_iota(jnp.int32, (QB, LANES), 1) < DSW_DH
        lane_k = lax.broadcasted_iota(jnp.int32, (2 * QB, LANES), 1) < DSW_DH
        col = lax.broadcasted_iota(jnp.int32, (QB, 2 * QB), 1)
        first = jnp.logical_and(i == 0, col < QB)
        o_acc = jnp.zeros((QB, LANES), F32)
        lse_b = jnp.zeros((QB, LANES), F32)
        for hh in range(2):
            mq = lane_q if hh == 0 else jnp.logical_not(lane_q)
            mk = lane_k if hh == 0 else jnp.logical_not(lane_k)
            s = _nt(jnp.where(mq, q, 0).astype(BF16), k2) + b_ref[hh]
            s = jnp.where(first, NEG, s)
            mx = jnp.max(s, axis=1, keepdims=True)
            p = jnp.exp(s - mx)
            l = jnp.sum(p, axis=1, keepdims=True)
            oh = jnp.dot(p.astype(BF16), jnp.where(mk, v2, 0).astype(BF16), preferred_element_type=F32) / l
            o_acc = o_acc + oh
            lse_b = jnp.where(mq, mx + jnp.log(l), lse_b)
        o_ref[...] = o_acc
        l_ref[...] = lse_b

    blk = (QB, LANES)
    out_spec = pl.BlockSpec(blk, lambda hp, r, i: (i, r * N_HP + hp))
    o, lse = pl.pallas_call(
        body, name=f"dsw_fwd_g{gi}", grid=(N_HP, dil, nq),
        in_specs=[pl.BlockSpec(blk, lambda hp, r, i: (i, qk_col(hp, r))),
                  pl.BlockSpec(blk, lambda hp, r, i: (jnp.maximum(i - 1, 0), qk_col(hp, r))),
                  pl.BlockSpec(blk, lambda hp, r, i: (i, qk_col(hp, r))),
                  pl.BlockSpec(blk, lambda hp, r, i: (jnp.maximum(i - 1, 0), v_col(hp, r))),
                  pl.BlockSpec(blk, lambda hp, r, i: (i, v_col(hp, r))),
                  pl.BlockSpec((2, QB, 2 * QB), lambda hp, r, i: (gi * N_HP + hp, 0, 0))],
        out_specs=[out_spec, out_spec],
        out_shape=[SDS((sd, dil * DSW_HG), F32)] * 2,
        compiler_params=_cp(3),
    )(qv, kv, kv, pv, pv, bias)
    return o.reshape(S, DSW_HG), lse.reshape(S, DSW_HG)


def dsw_group_bwd(qn, kn, proj, bias, do, o, lse, gi, S):
    dil = DSW_GROUPS[gi][1]
    sd = S // dil
    nq = sd // QB
    qv = qn.reshape(sd, dil * 3 * DSW_HG)
    kv = kn.reshape(sd, dil * 3 * DSW_HG)
    pv = proj.reshape(sd, dil * 9 * DSW_HG)
    dov = do.reshape(sd, dil * DSW_HG)
    ov = o.reshape(sd, dil * DSW_HG)
    lv = lse.reshape(sd, dil * DSW_HG)
    qk_col = lambda hp, r: r * (3 * N_HP) + gi * N_HP + hp
    v_col = lambda hp, r: r * PROJ_BLKS + 2 * 3 * N_HP + gi * N_HP + hp
    o_col = lambda hp, r: r * N_HP + hp
    cur = lambda i: jnp.minimum(i, nq - 1)
    prev = lambda i: jnp.maximum(jnp.minimum(i, nq - 1) - 1, 0)
    done = lambda i: jnp.maximum(i - 1, 0)

    def body(q_ref, kp_ref, kc_ref, vp_ref, vc_ref, b_ref, do_ref, o_ref, l_ref,
             dq_ref, dk_ref, dv_ref, db_ref, dk_scr, dv_scr):
        r, i = pl.program_id(1), pl.program_id(2)

        @pl.when(jnp.logical_and(r == 0, i == 0))
        def _():
            db_ref[...] = jnp.zeros_like(db_ref)

        @pl.when(i == 0)
        def _():
            dk_scr[...] = jnp.zeros_like(dk_scr)
            dv_scr[...] = jnp.zeros_like(dv_scr)

        @pl.when(i < nq)
        def _():
            q = q_ref[...]
            k2 = jnp.concatenate([kp_ref[...], kc_ref[...]], axis=0)
            v2 = jnp.concatenate([vp_ref[...], vc_ref[...]], axis=0).astype(BF16)
            dout = do_ref[...].astype(F32)
            prod = dout * o_ref[...].astype(F32)
            lse_b = l_ref[...]
            lane_q = lax.broadcasted_iota(jnp.int32, (QB, LANES), 1) < DSW_DH
            col = lax.broadcasted_iota(jnp.int32, (QB, 2 * QB), 1)
            first = jnp.logical_and(i == 0, col < QB)
            dq = jnp.zeros((QB, LANES), F32)
            dk2 = jnp.zeros((2 * QB, LANES), F32)
            dv2 = jnp.zeros((2 * QB, LANES), F32)
            for hh in range(2):
                mq = lane_q if hh == 0 else jnp.logical_not(lane_q)
                qm = jnp.where(mq, q, 0).astype(BF16)
                dom = jnp.where(mq, dout, 0.0).astype(BF16)
                s = _nt(qm, k2) + b_ref[hh]
                s = jnp.where(first, NEG, s)
                lse_h = jnp.max(jnp.where(mq, lse_b, NEG), axis=1, keepdims=True)
                p = jnp.exp(s - lse_h)
                delta = jnp.sum(jnp.where(mq, prod, 0.0), axis=1, keepdims=True)
                dp = _nt(dom, v2)
                ds = p * (dp - delta)
                dsb = ds.astype(BF16)
                dq = dq + jnp.where(mq, jnp.dot(dsb, k2, preferred_element_type=F32), 0.0)
                dk2 = dk2 + _tn(dsb, qm)
                dv2 = dv2 + _tn(p.astype(BF16), dom)
                db_ref[hh] += ds
            dq_ref[...] = dq
            dk_ref[...] = dk_scr[...] + dk2[:QB]
            dv_ref[...] = (dv_scr[...] + dv2[:QB]).astype(dv_ref.dtype)
            dk_scr[...] = dk2[QB:]
            dv_scr[...] = dv2[QB:]

        @pl.when(i == nq)
        def _():
            dk_ref[...] = dk_scr[...]
            dv_ref[...] = dv_scr[...].astype(dv_ref.dtype)

    blk = (QB, LANES)
    dq, dk, dv, dbias = pl.pallas_call(
        body, name=f"dsw_bwd_g{gi}", grid=(N_HP, dil, nq + 1),
        in_specs=[pl.BlockSpec(blk, lambda hp, r, i: (cur(i), qk_col(hp, r))),
                  pl.BlockSpec(blk, lambda hp, r, i: (prev(i), qk_col(hp, r))),
                  pl.BlockSpec(blk, lambda hp, r, i: (cur(i), qk_col(hp, r))),
                  pl.BlockSpec(blk, lambda hp, r, i: (prev(i), v_col(hp, r))),
                  pl.BlockSpec(blk, lambda hp, r, i: (cur(i), v_col(hp, r))),
                  pl.BlockSpec((2, QB, 2 * QB), lambda hp, r, i: (gi * N_HP + hp, 0, 0)),
                  pl.BlockSpec(blk, lambda hp, r, i: (cur(i), o_col(hp, r))),
                  pl.BlockSpec(blk, lambda hp, r, i: (cur(i), o_col(hp, r))),
                  pl.BlockSpec(blk, lambda hp, r, i: (cur(i), o_col(hp, r)))],
        out_specs=[pl.BlockSpec(blk, lambda hp, r, i: (cur(i), o_col(hp, r))),
                   pl.BlockSpec(blk, lambda hp, r, i: (done(i), o_col(hp, r))),
                   pl.BlockSpec(blk, lambda hp, r, i: (done(i), o_col(hp, r))),
                   pl.BlockSpec((2, QB, 2 * QB), lambda hp, r, i: (hp, 0, 0))],
        out_shape=[SDS((sd, dil * DSW_HG), F32), SDS((sd, dil * DSW_HG), F32), SDS((sd, dil * DSW_HG), BF16),
                   SDS((GDN_H, QB, 2 * QB), F32)],
        scratch_shapes=[pltpu.VMEM(blk, F32), pltpu.VMEM(blk, F32)],
        compiler_params=_cp(3),
    )(qv, kv, kv, pv, pv, bias, dov, ov, lv)
    return dq.reshape(S, DSW_HG), dk.reshape(S, DSW_HG), dv.reshape(S, DSW_HG), dbias


def dsw_forward(h, w_in, q_gain2, k_gain2, rel_bias, w_out):
    S = h.shape[0]
    nt = S // RT
    nb = 3 * N_HP
    proj = matmul(h, w_in, "nn", F32, "dsw_in")
    width = nb * LANES
    lanes12 = [LANES] * nb
    (qn,) = rowwise(f_qnorm, [Row(proj, (RT, width), lambda i: (i, 0), splits=lanes12)], [q_gain2],
                    [Out((S, width), BF16, (RT, width), lambda i: (i, 0), splits=lanes12)], (nt,), "dsw_qnorm")
    (kn,) = rowwise(f_qknorm, [Row(proj, (RT, width), lambda i: (i, 1), splits=lanes12)], [k_gain2],
                    [Out((S, width), BF16, (RT, width), lambda i: (i, 0), splits=lanes12)], (nt,), "dsw_knorm")
    bias = dsw_bias(rel_bias)
    os_, ls_ = [], []
    for gi in range(3):
        o, l = dsw_group_fwd(qn, kn, proj, bias, gi, S)
        os_.append(o)
        ls_.append(l)
    full = lambda a: Row(a, (RT, DSW_HG), lambda i: (i, 0))
    o, lse = rowwise(f_combine, [full(a) for a in os_ + ls_], [],
                     [Out((S, DSW_HG), BF16, (RT, DSW_HG), lambda i: (i, 0)), Out((S, DSW_HG), F32, (RT, DSW_HG), lambda i: (i, 0))],
                     (nt,), "dsw_combine")
    y = matmul(o, w_out, "nn", F32, "dsw_out")
    return y, dict(h=h, proj=proj, qn=qn, kn=kn, bias=bias, o=o, lse=lse)


def dsw_backward(dy, sv, w_in, q_gain2, k_gain2, w_out):
    S = dy.shape[0]
    nt = S // RT
    nb = 3 * N_HP
    do = matmul(dy, w_out, "nt", BF16, "dsw_out_dx")
    d_w_out = matmul(sv["o"], dy, "tn", F32, "dsw_out_dw")
    pieces_q, pieces_k, pieces_v, dbs = [], [], [], []
    d_qg = jnp.zeros((1, LANES), F32)
    d_kg = jnp.zeros((1, LANES), F32)
    for gi in range(3):
        dq, dk, dv, db = dsw_group_bwd(sv["qn"], sv["kn"], sv["proj"], sv["bias"], do, sv["o"], sv["lse"], gi, S)
        dbs.append(db)
        pieces_v.append(dv)
        for which, dd in ((0, dq), (1, dk)):
            lanes4 = [LANES] * N_HP
            row = Row(sv["proj"], (RT, DSW_HG), lambda i, _o=which * 3 + gi: (i, _o), splits=lanes4,
                      gdtype=BF16, gshape=(S, DSW_HG), gimap=lambda i: (i, 0))
            fn, gain = (f_qnorm, q_gain2) if which == 0 else (f_qknorm, k_gain2)
            (dx,), (dg,) = rowwise_bwd(fn, [row], [gain], [Row(dd, (RT, DSW_HG), lambda i: (i, 0), splits=lanes4)],
                                       (nt,), f"dsw_norm_bwd_{which}{gi}")
            if which == 0:
                pieces_q.append(dx)
                d_qg = d_qg + dg
            else:
                pieces_k.append(dx)
                d_kg = d_kg + dg
    dproj = jnp.concatenate(pieces_q + pieces_k + pieces_v, axis=1)
    d_w_in = matmul(sv["h"], dproj, "tn", F32, "dsw_in_dw")
    dh = matmul(dproj, w_in, "nt", F32, "dsw_in_dx")
    d_rel = dsw_bias_grad(jnp.concatenate(dbs, axis=0))
    return dh, dict(w_in=d_w_in, q_gain2=d_qg, k_gain2=d_kg, rel=d_rel, w_out=d_w_out)


N_LB = DSW_HG // LANES
HALF = DSW_DH // 2


def _lanes(j):
    return slice(LANES * j, LANES * (j + 1))


def _deinterleave(stage, out_ref, dil, rows, dtype):
    for r in range(dil):
        for j in range(N_LB):
            out_ref[r, :, _lanes(j)] = stage[j, pl.ds(r, rows, stride=dil), :].astype(dtype)


def _interleave(in_ref, stage, dil, rows):
    for r in range(dil):
        for j in range(N_LB):
            stage[j, pl.ds(r, rows, stride=dil), :] = in_ref[r, :, _lanes(j)].astype(F32)


def dsw_prep(proj, q_gain2, k_gain2, gi, S):
    dil = DSW_GROUPS[gi][1]
    nt, rows = S // RT, RT // dil

    def body(q_ref, k_ref, v_ref, qg_ref, kg_ref, qo_ref, ko_ref, vo_ref, stage):
        for src, gain_ref, scale, dst in ((q_ref, qg_ref, DSW_DH ** -0.5, qo_ref), (k_ref, kg_ref, 1.0, ko_ref), (v_ref, None, None, vo_ref)):
            for j in range(N_LB):
                val = src[:, _lanes(j)]
                stage[j] = val if gain_ref is None else _qknorm1(val, gain_ref[...], scale)
            _deinterleave(stage, dst, dil, rows, BF16)

    col = lambda which: pl.BlockSpec((RT, DSW_HG), lambda i, _c=which * 3 + gi: (i, _c))
    gspec = pl.BlockSpec((1, LANES), lambda i: (0, 0))
    ospec = pl.BlockSpec((dil, rows, DSW_HG), lambda i: (0, i, 0))
    return pl.pallas_call(
        body, name=f"dsw_prep_g{gi}", grid=(nt,),
        in_specs=[col(0), col(1), col(2), gspec, gspec], out_specs=[ospec] * 3,
        out_shape=[SDS((dil, S // dil, DSW_HG), BF16)] * 3,
        scratch_shapes=[pltpu.VMEM((N_LB, RT, LANES), F32)], compiler_params=_cp(1),
    )(proj, proj, proj, q_gain2, k_gain2)


def dsw_prep_bwd(proj, q_gain2, k_gain2, dqd, dkd, dvd, gi, S):
    dil = DSW_GROUPS[gi][1]
    nt, rows = S // RT, RT // dil

    def body(q_ref, k_ref, qg_ref, kg_ref, dq_ref, dk_ref, dv_ref, oq_ref, ok_ref, ov_ref, dqg_ref, dkg_ref, stage):
        i = pl.program_id(0)

        @pl.when(i == 0)
        def _():
            dqg_ref[...] = jnp.zeros_like(dqg_ref)
            dkg_ref[...] = jnp.zeros_like(dkg_ref)

        for src, gain_ref, scale, cot_ref, dst, dg_ref in ((q_ref, qg_ref, DSW_DH ** -0.5, dq_ref, oq_ref, dqg_ref),
                                                          (k_ref, kg_ref, 1.0, dk_ref, ok_ref, dkg_ref)):
            _interleave(cot_ref, stage, dil, rows)
            for j in range(N_LB):
                _, vjp = jax.vjp(lambda x, g, _s=scale: _qknorm1(x, g, _s), src[:, _lanes(j)], gain_ref[...])
                dx, dg = vjp(stage[j])
                dst[:, _lanes(j)] = dx.astype(dst.dtype)
                dg_ref[...] += dg
        _interleave(dv_ref, stage, dil, rows)
        for j in range(N_LB):
            ov_ref[:, _lanes(j)] = stage[j].astype(ov_ref.dtype)

    col = lambda which: pl.BlockSpec((RT, DSW_HG), lambda i, _c=which * 3 + gi: (i, _c))
    gspec = pl.BlockSpec((1, LANES), lambda i: (0, 0))
    dspec = pl.BlockSpec((dil, rows, DSW_HG), lambda i: (0, i, 0))
    nspec = pl.BlockSpec((RT, DSW_HG), lambda i: (i, 0))
    return pl.pallas_call(
        body, name=f"dsw_prep_bwd_g{gi}", grid=(nt,),
        in_specs=[col(0), col(1), gspec, gspec, dspec, dspec, dspec], out_specs=[nspec] * 3 + [gspec] * 2,
        out_shape=[SDS((S, DSW_HG), BF16)] * 3 + [SDS((1, LANES), F32)] * 2,
        scratch_shapes=[pltpu.VMEM((N_LB, RT, LANES), F32)], compiler_params=_cp(1),
    )(proj, proj, q_gain2, k_gain2, dqd, dkd, dvd)


def _head_masks(rows):
    lane = lax.broadcasted_iota(jnp.int32, (rows, LANES), 1)
    return lane < DSW_DH, (lane % DSW_DH) < HALF


def dsw_attn_fwd(qd, kd, vd, bias, gi, S):
    dil = DSW_GROUPS[gi][1]
    sd = S // dil
    nq = sd // QB

    def body(q_ref, k_ref, v_ref, b_ref, o_ref, l_ref, kp_scr, vp_scr):
        i = pl.program_id(1)

        @pl.when(i == 0)
        def _():
            kp_scr[...] = jnp.zeros_like(kp_scr)
            vp_scr[...] = jnp.zeros_like(vp_scr)

        lo_q, _ = _head_masks(QB)
        lo_k, _ = _head_masks(2 * QB)
        col = lax.broadcasted_iota(jnp.int32, (QB, 2 * QB), 1)
        first = jnp.logical_and(i == 0, col < QB)
        for hp in range(N_HP):
            q = q_ref[:, _lanes(hp)]
            k2 = jnp.concatenate([kp_scr[:, _lanes(hp)], k_ref[:, _lanes(hp)]], axis=0)
            v2 = jnp.concatenate([vp_scr[:, _lanes(hp)], v_ref[:, _lanes(hp)]], axis=0)
            o_acc = jnp.zeros((QB, LANES), F32)
            lse_b = jnp.zeros((QB, LANES), F32)
            for hh in range(2):
                mq = lo_q if hh == 0 else jnp.logical_not(lo_q)
                mk = lo_k if hh == 0 else jnp.logical_not(lo_k)
                s = _nt(jnp.where(mq, q, 0).astype(BF16), k2) + b_ref[2 * hp + hh]
                s = jnp.where(first, NEG, s)
                mx = jnp.max(s, axis=1, keepdims=True)
                p = jnp.exp(s - mx)
                l = jnp.sum(p, axis=1, keepdims=True)
                oh = jnp.dot(p.astype(BF16), jnp.where(mk, v2, 0).astype(BF16), preferred_element_type=F32) / l
                o_acc = o_acc + oh
                lse_b = jnp.where(mq, mx + jnp.log(l), lse_b)
            o_ref[:, _lanes(hp)] = o_acc
            l_ref[:, _lanes(hp)] = lse_b
        kp_scr[...] = k_ref[...]
        vp_scr[...] = v_ref[...]

    blk = pl.BlockSpec((None, QB, DSW_HG), lambda r, i: (r, i, 0))
    return pl.pallas_call(
        body, name=f"dsw_attn_g{gi}", grid=(dil, nq),
        in_specs=[blk, blk, blk, pl.BlockSpec((GDN_H, QB, 2 * QB), lambda r, i: (gi, 0, 0))],
        out_specs=[blk, blk], out_shape=[SDS((dil, sd, DSW_HG), F32)] * 2,
        scratch_shapes=[pltpu.VMEM((QB, DSW_HG), BF16)] * 2, compiler_params=_cp(2),
    )(qd, kd, vd, bias)


def dsw_attn_bwd(qd, kd, vd, bias, dod, statd, gi, S):
    dil = DSW_GROUPS[gi][1]
    sd = S // dil
    nq = sd // QB
    cur = lambda i: jnp.minimum(i, nq - 1)
    done = lambda i: jnp.maximum(i - 1, 0)

    def body(q_ref, k_ref, v_ref, b_ref, do_ref, st_ref, dq_ref, dk_ref, dv_ref, db_ref, kp_scr, vp_scr, dk_scr, dv_scr):
        r, i = pl.program_id(0), pl.program_id(1)

        @pl.when(jnp.logical_and(r == 0, i == 0))
        def _():
            db_ref[...] = jnp.zeros_like(db_ref)

        @pl.when(i == 0)
        def _():
            for scr in (kp_scr, vp_scr, dk_scr, dv_scr):
                scr[...] = jnp.zeros_like(scr)

        @pl.when(i < nq)
        def _():
            lo_q, first_half = _head_masks(QB)
            col = lax.broadcasted_iota(jnp.int32, (QB, 2 * QB), 1)
            first = jnp.logical_and(i == 0, col < QB)
            for hp in range(N_HP):
                q = q_ref[:, _lanes(hp)]
                k2 = jnp.concatenate([kp_scr[:, _lanes(hp)], k_ref[:, _lanes(hp)]], axis=0)
                v2 = jnp.concatenate([vp_scr[:, _lanes(hp)], v_ref[:, _lanes(hp)]], axis=0)
                dout = do_ref[:, _lanes(hp)]
                stat = st_ref[:, _lanes(hp)]
                dq = jnp.zeros((QB, LANES), F32)
                dk2 = jnp.zeros((2 * QB, LANES), F32)
                dv2 = jnp.zeros((2 * QB, LANES), F32)
                for hh in range(2):
                    mq = lo_q if hh == 0 else jnp.logical_not(lo_q)
                    qm = jnp.where(mq, q, 0).astype(BF16)
                    dom = jnp.where(mq, dout, 0).astype(BF16)
                    s = _nt(qm, k2) + b_ref[2 * hp + hh]
                    s = jnp.where(first, NEG, s)
                    lse_h = jnp.max(jnp.where(jnp.logical_and(mq, first_half), stat, NEG), axis=1, keepdims=True)
                    delta = jnp.max(jnp.where(jnp.logical_and(mq, jnp.logical_not(first_half)), stat, NEG), axis=1, keepdims=True)
                    p = jnp.exp(s - lse_h)
                    ds = p * (_nt(dom, v2) - delta)
                    dsb = ds.astype(BF16)
                    dq = dq + jnp.where(mq, jnp.dot(dsb, k2, preferred_element_type=F32), 0.0)
                    dk2 = dk2 + _tn(dsb, qm)
                    dv2 = dv2 + _tn(p.astype(BF16), dom)
                    db_ref[2 * hp + hh] += ds
                dq_ref[:, _lanes(hp)] = dq
                dk_ref[:, _lanes(hp)] = dk_scr[:, _lanes(hp)] + dk2[:QB]
                dv_ref[:, _lanes(hp)] = (dv_scr[:, _lanes(hp)] + dv2[:QB]).astype(dv_ref.dtype)
                dk_scr[:, _lanes(hp)] = dk2[QB:]
                dv_scr[:, _lanes(hp)] = dv2[QB:]
            kp_scr[...] = k_ref[...]
            vp_scr[...] = v_ref[...]

        @pl.when(i == nq)
        def _():
            dk_ref[...] = dk_scr[...]
            dv_ref[...] = dv_scr[...].astype(dv_ref.dtype)

    blk = pl.BlockSpec((None, QB, DSW_HG), lambda r, i: (r, cur(i), 0))
    oblk = pl.BlockSpec((None, QB, DSW_HG), lambda r, i: (r, done(i), 0))
    return pl.pallas_call(
        body, name=f"dsw_attn_bwd_g{gi}", grid=(dil, nq + 1),
        in_specs=[blk, blk, blk, pl.BlockSpec((GDN_H, QB, 2 * QB), lambda r, i: (gi, 0, 0)), blk, blk],
        out_specs=[blk, oblk, oblk, pl.BlockSpec((GDN_H, QB, 2 * QB), lambda r, i: (0, 0, 0))],
        out_shape=[SDS((dil, sd, DSW_HG), F32), SDS((dil, sd, DSW_HG), F32), SDS((dil, sd, DSW_HG), BF16),
                   SDS((GDN_H, QB, 2 * QB), F32)],
        scratch_shapes=[pltpu.VMEM((QB, DSW_HG), BF16)] * 2 + [pltpu.VMEM((QB, DSW_HG), F32)] * 2,
        compiler_params=_cp(2),
    )(qd, kd, vd, bias, dod, statd)


def dsw_combine(ods, lseds, S):
    nt = S // RT
    dils = [d for _, d in DSW_GROUPS]

    def body(*refs):
        ins, (o_ref, l_ref), stages = refs[:6], refs[6:8], refs[8:]
        for g in range(3):
            _interleave(ins[g], stages[g], dils[g], RT // dils[g])
            _interleave(ins[3 + g], stages[3 + g], dils[g], RT // dils[g])
        for j in range(N_LB):
            o, lse = f_combine(None, *[st[j] for st in stages])
            o_ref[:, _lanes(j)] = o.astype(o_ref.dtype)
            l_ref[:, _lanes(j)] = lse

    dspec = lambda d: pl.BlockSpec((d, RT // d, DSW_HG), lambda i: (0, i, 0))
    nspec = pl.BlockSpec((RT, DSW_HG), lambda i: (i, 0))
    return pl.pallas_call(
        body, name="dsw_combine", grid=(nt,),
        in_specs=[dspec(d) for d in dils] * 2, out_specs=[nspec, nspec],
        out_shape=[SDS((S, DSW_HG), BF16), SDS((S, DSW_HG), F32)],
        scratch_shapes=[pltpu.VMEM((N_LB, RT, LANES), F32)] * 6, compiler_params=_cp(1),
    )(*ods, *lseds)


def dsw_bwd_prep(do, o, lse, S):
    nt = S // RT
    dils = [d for _, d in DSW_GROUPS]

    def body(do_ref, o_ref, l_ref, *rest):
        outs, (st_do, st_stat) = rest[:6], rest[6:]
        lo, first_half = _head_masks(RT)
        for j in range(N_LB):
            dout = do_ref[:, _lanes(j)]
            prod = dout * o_ref[:, _lanes(j)].astype(F32)
            s_all = jnp.sum(prod, axis=1, keepdims=True)
            s_lo = jnp.sum(jnp.where(lo, prod, 0.0), axis=1, keepdims=True)
            delta = jnp.where(lo, s_lo, s_all - s_lo)
            st_do[j] = dout
            st_stat[j] = jnp.where(first_half, l_ref[:, _lanes(j)], delta)
        for g in range(3):
            _deinterleave(st_do, outs[g], dils[g], RT // dils[g], BF16)
            _deinterleave(st_stat, outs[3 + g], dils[g], RT // dils[g], F32)

    nspec = pl.BlockSpec((RT, DSW_HG), lambda i: (i, 0))
    dspec = lambda d: pl.BlockSpec((d, RT // d, DSW_HG), lambda i: (0, i, 0))
    res = pl.pallas_call(
        body, name="dsw_bwd_prep", grid=(nt,),
        in_specs=[nspec] * 3, out_specs=[dspec(d) for d in dils] * 2,
        out_shape=[SDS((d, S // d, DSW_HG), BF16) for d in dils] + [SDS((d, S // d, DSW_HG), F32) for d in dils],
        scratch_shapes=[pltpu.VMEM((N_LB, RT, LANES), F32)] * 2, compiler_params=_cp(1),
    )(do, o, lse)
    return res[:3], res[3:]


def dsw_forward(h, w_in, q_gain2, k_gain2, rel_bias, w_out):
    S = h.shape[0]
    proj = matmul(h, w_in, "nn", F32, "dsw_in")
    bias = dsw_bias(rel_bias)
    qkv, ods, lseds = [], [], []
    for gi in range(3):
        qd, kd, vd = dsw_prep(proj, q_gain2, k_gain2, gi, S)
        od, ld = dsw_attn_fwd(qd, kd, vd, bias, gi, S)
        qkv.append((qd, kd, vd))
        ods.append(od)
        lseds.append(ld)
    o, lse = dsw_combine(ods, lseds, S)
    y = matmul(o, w_out, "nn", F32, "dsw_out")
    return y, dict(h=h, proj=proj, qkv=qkv, bias=bias, o=o, lse=lse)


def dsw_backward(dy, sv, w_in, q_gain2, k_gain2, w_out):
    S = dy.shape[0]
    do = matmul(dy, w_out, "nt", F32, "dsw_out_dx")
    d_w_out = matmul(sv["o"], dy, "tn", F32, "dsw_out_dw")
    dods, statds = dsw_bwd_prep(do, sv["o"], sv["lse"], S)
    pieces_q, pieces_k, pieces_v, dbs = [], [], [], []
    d_qg = jnp.zeros((1, LANES), F32)
    d_kg = jnp.zeros((1, LANES), F32)
    for gi in range(3):
        qd, kd, vd = sv["qkv"][gi]
        dqd, dkd, dvd, db = dsw_attn_bwd(qd, kd, vd, sv["bias"], dods[gi], statds[gi], gi, S)
        dq, dk, dv, dqg, dkg = dsw_prep_bwd(sv["proj"], q_gain2, k_gain2, dqd, dkd, dvd, gi, S)
        dbs.append(db)
        pieces_q.append(dq)
        pieces_k.append(dk)
        pieces_v.append(dv)
        d_qg = d_qg + dqg
        d_kg = d_kg + dkg
    dproj = jnp.concatenate(pieces_q + pieces_k + pieces_v, axis=1)
    d_w_in = matmul(sv["h"], dproj, "tn", F32, "dsw_in_dw")
    dh = matmul(dproj, w_in, "nt", F32, "dsw_in_dx")
    d_rel = dsw_bias_grad(jnp.concatenate(dbs, axis=0))
    return dh, dict(w_in=d_w_in, q_gain2=d_qg, k_gain2=d_kg, rel=d_rel, w_out=d_w_out)


FT = 128


def ffn_forward(h, w_in, w_out, tag):
    S = h.shape[0]
    gu = matmul(h, w_in, "nn", BF16, f"ffn_in_{tag}")
    gu_row = Row(gu, (FT, 2 * FFN), lambda i: (i, 0), splits=[FFN, FFN], gdtype=BF16)
    (a,) = rowwise(f_swiglu, [gu_row], [], [Out((S, FFN), BF16, (FT, FFN), lambda i: (i, 0))], (S // FT,), f"ffn_act_{tag}")
    f = matmul(a, w_out, "nn", F32, f"ffn_out_{tag}")
    return f, dict(h=h, gu_row=gu_row, a=a)


def ffn_backward(df, sv, w_in, w_out, tag):
    S = df.shape[0]
    da = matmul(df, w_out, "nt", BF16, f"ffn_out_dx_{tag}")
    d_w_out = matmul(sv["a"], df, "tn", F32, f"ffn_out_dw_{tag}")
    (dgu,), _ = rowwise_bwd(f_swiglu, [sv["gu_row"]], [], [Row(da, (FT, FFN), lambda i: (i, 0))], (S // FT,), f"ffn_act_bwd_{tag}")
    d_w_in = matmul(sv["h"], dgu, "tn", F32, f"ffn_in_dw_{tag}")
    dh = matmul(dgu, w_in, "nt", F32, f"ffn_in_dx_{tag}")
    return dh, d_w_in, d_w_out


def f_norm_only(ids, x, gain, sc, sh):
    return (_normmod(x, gain, sc, sh),)


def _wide(a, **kw):
    return Row(a, (RT, D), lambda i: (i, 0), **kw)


def _wide_out(S, dtype):
    return Out((S, D), dtype, (RT, D), lambda i: (i, 0))


def adamw(w, g, m, v, name):
    shape = w.shape
    C = shape[-1]
    R = int(np.prod(shape[:-1]))
    w2, g2, m2, v2 = (a.reshape(R, C) for a in (w, g, m, v))
    br = R
    if R > 256:
        br = max(b for b in range(8, 257, 8) if R % b == 0)
    c1 = 1.0 / (1.0 - ADAM_B1 ** ADAM_STEP)
    c2 = 1.0 / (1.0 - ADAM_B2 ** ADAM_STEP)

    def body(w_ref, g_ref, m_ref, v_ref, d_ref, nm_ref, nv_ref):
        gg = g_ref[...]
        mm_ = ADAM_B1 * m_ref[...] + (1.0 - ADAM_B1) * gg
        vv = ADAM_B2 * v_ref[...] + (1.0 - ADAM_B2) * (gg * gg)
        d_ref[...] = -ADAM_LR * ((mm_ * c1) / (jnp.sqrt(vv * c2) + ADAM_EPS) + ADAM_WD * w_ref[...])
        nm_ref[...] = mm_
        nv_ref[...] = vv

    spec = pl.BlockSpec((br, C), lambda i: (i, 0))
    d, nm, nv = pl.pallas_call(
        body, name=name, grid=(R // br,), in_specs=[spec] * 4, out_specs=[spec] * 3,
        out_shape=[SDS((R, C), F32)] * 3, compiler_params=_cp(1),
    )(w2, g2, m2, v2)
    return d.reshape(shape), nm.reshape(shape), nv.reshape(shape)


def _place():
    x, y, c = lax.axis_index("x"), lax.axis_index("y"), lax.axis_index("c")
    chips = [(1 - x, y), (x, 1 - y), (1 - x, 1 - y)]
    return x, y, c, chips


def all_gather_small(blk, name):
    m_per, n = blk.shape

    def body(x_ref, out_ref, send_sems, recv_sems, local_sem):
        x, y, c, chips = _place()
        me, sibling = (x, y, c), (x, y, 1 - c)

        def rows(px, py, pc):
            return out_ref.at[pl.ds((4 * px + 2 * py + pc) * m_per, m_per), :]

        def copy(k, block, to, src=None):
            return pltpu.make_async_remote_copy(
                src_ref=rows(*block) if src is None else src, dst_ref=rows(*block),
                send_sem=send_sems.at[k], recv_sem=recv_sems.at[k], device_id=to, device_id_type=MESH)

        mine = pltpu.make_async_copy(x_ref, rows(*me), local_sem)
        mine.start()
        first = [copy(0, me, sibling, src=x_ref)]
        first += [copy(1 + j, me, (*chip, c), src=x_ref) for j, chip in enumerate(chips)]
        for cp in first:
            cp.start()
        passed = [copy(4 + j, (*chip, c), sibling) for j, chip in enumerate(chips)]
        for j, chip in enumerate(chips):
            copy(1 + j, (*chip, c), me).wait_recv()
            passed[j].start()
        copy(0, sibling, me).wait_recv()
        for j, chip in enumerate(chips):
            copy(4 + j, (*chip, 1 - c), me).wait_recv()
        for cp in first + passed:
            cp.wait_send()
        mine.wait()

    return pl.pallas_call(
        body, name=name, out_shape=SDS((N_DEV * m_per, n), blk.dtype),
        in_specs=[pl.BlockSpec(memory_space=pltpu.VMEM)], out_specs=pl.BlockSpec(memory_space=pltpu.VMEM),
        scratch_shapes=[pltpu.SemaphoreType.DMA((7,)), pltpu.SemaphoreType.DMA((7,)), pltpu.SemaphoreType.DMA],
    )(blk)


def all_gather_shards(wp):
    R, W = wp.shape
    Rh = R // 2

    def body(w_ref, out_ref, send_sems, recv_sems, local_sem):
        x, y, c, chips = _place()
        sibling = (x, y, 1 - c)
        s_me = 2 * x + y

        def half(cc):
            return pl.ds(pl.multiple_of(cc * Rh, 16), Rh)

        def copy(k, src, dst, to):
            return pltpu.make_async_remote_copy(src_ref=src, dst_ref=dst, send_sem=send_sems.at[k], recv_sem=recv_sems.at[k],
                                                device_id=to, device_id_type=MESH)

        mine = pltpu.make_async_copy(w_ref, out_ref.at[s_me], local_sem)
        mine.start()
        sends = [copy(j, w_ref.at[half(c)], out_ref.at[s_me, half(c)], (*chip, c)) for j, chip in enumerate(chips)]
        for cp in sends:
            cp.start()
        passed = []
        for j, (px, py) in enumerate(chips):
            got = out_ref.at[2 * px + py, half(c)]
            copy(j, got, got, (px, py, c)).wait_recv()
            fw = copy(3 + j, got, got, sibling)
            fw.start()
            passed.append(fw)
        for j, (px, py) in enumerate(chips):
            got = out_ref.at[2 * px + py, half(1 - c)]
            copy(3 + j, got, got, sibling).wait_recv()
        for cp in sends + passed:
            cp.wait_send()
        mine.wait()

    return pl.pallas_call(
        body, name="weights_all_gather", out_shape=SDS((N_SHARD, R, W), wp.dtype),
        in_specs=[ANY], out_specs=ANY,
        scratch_shapes=[pltpu.SemaphoreType.DMA((6,)), pltpu.SemaphoreType.DMA((6,)), pltpu.SemaphoreType.DMA],
    )(wp)


def sibling_exchange(send, name):
    def body(s_ref, o_ref, send_sem, recv_sem):
        x, y, c, _ = _place()
        cp = pltpu.make_async_remote_copy(src_ref=s_ref, dst_ref=o_ref, send_sem=send_sem, recv_sem=recv_sem,
                                          device_id=(x, y, 1 - c), device_id_type=MESH)
        cp.start()
        cp.wait()

    return pl.pallas_call(
        body, name=name, out_shape=SDS(send.shape, send.dtype), in_specs=[ANY], out_specs=ANY,
        scratch_shapes=[pltpu.SemaphoreType.DMA, pltpu.SemaphoreType.DMA],
    )(send)


def scatter_to_chips(p):
    _, Rh, W = p.shape

    def body(p_ref, o_ref, send_sems, recv_sems):
        x, y, c, chips = _place()
        cps = []
        for j, (px, py) in enumerate(chips):
            cp = pltpu.make_async_remote_copy(src_ref=p_ref.at[2 * px + py], dst_ref=o_ref.at[j], send_sem=send_sems.at[j],
                                              recv_sem=recv_sems.at[j], device_id=(px, py, c), device_id_type=MESH)
            cp.start()
            cps.append(cp)
        for cp in cps:
            cp.wait()

    return pl.pallas_call(
        body, name="grads_scatter", out_shape=SDS((3, Rh, W), p.dtype), in_specs=[ANY], out_specs=ANY,
        scratch_shapes=[pltpu.SemaphoreType.DMA((3,)), pltpu.SemaphoreType.DMA((3,))],
    )(p)


def add_rows(arrs, out_dtype, name, rt=256):
    Rr, W = arrs[0].shape

    def fn(ids, *vals):
        acc = vals[0]
        for v in vals[1:]:
            acc = acc + v
        return (acc,)

    t = rt if Rr % rt == 0 else max(b for b in range(16, rt + 1, 16) if Rr % b == 0)
    (out,) = rowwise(fn, [Row(a, (t, W), lambda i: (i, 0)) for a in arrs], [],
                     [Out((Rr, W), out_dtype, (t, W), lambda i: (i, 0))], (Rr // t,), name)
    return out


PACK = (("gdn_w_in", 2), ("gdn_w_out", 1), ("w_ffn_in", 2), ("w_ffn_out", 1), ("dsw_w_in", 2), ("dsw_w_out", 2))
PACK_ALIGN = 32


def _pack_rows(sizes):
    total = sum(sizes)
    rows = -(-total // D)
    return -(-rows // PACK_ALIGN) * PACK_ALIGN


def pack_blocks(blocks, dtype):
    flat = [b.astype(dtype).reshape(-1) for b in blocks]
    total = sum(f.shape[0] for f in flat)
    R = _pack_rows([f.shape[0] for f in flat])
    flat.append(jnp.zeros((R * D - total,), dtype))
    return jnp.concatenate(flat).reshape(R, D)


def unpack_blocks(buf, shapes):
    flat = buf.reshape(-1)
    out, off = [], 0
    for shp in shapes:
        n = int(np.prod(shp))
        out.append(flat[off:off + n].reshape(shp))
        off += n
    return out


def _shard_slice(a, axis, s):
    n = a.shape[axis] // N_SHARD
    return lax.slice_in_dim(a, s * n, (s + 1) * n, axis=axis)


def _pad_lanes(v):
    return jnp.concatenate([v.astype(F32), jnp.zeros((LANES - v.shape[0],), F32)])[None]


def kernel(x, c, w_ada, b_ada, norm_mix, norm_ffn, w_ffn_in, w_ffn_out, gdn_w_in, gdn_conv, gdn_a_log, gdn_dt_bias, gdn_out_norm, gdn_w_out, dsw_w_in, dsw_q_norm, dsw_k_norm, dsw_w_out, rel_bias, loss_target, m_w_ada, m_b_ada, m_norm_mix, m_norm_ffn, m_w_ffn_in, m_w_ffn_out, m_gdn_w_in, m_gdn_conv, m_gdn_a_log, m_gdn_dt_bias, m_gdn_out_norm, m_gdn_w_out, m_dsw_w_in, m_dsw_q_norm, m_dsw_k_norm, m_dsw_w_out, m_rel_bias, v_w_ada, v_b_ada, v_norm_mix, v_norm_ffn, v_w_ffn_in, v_w_ffn_out, v_gdn_w_in, v_gdn_conv, v_gdn_a_log, v_gdn_dt_bias, v_gdn_out_norm, v_gdn_w_out, v_dsw_w_in, v_dsw_q_norm, v_dsw_k_norm, v_dsw_w_out, v_rel_bias):
    S = x.shape[1]
    nt = S // RT
    xi, yi, ci = lax.axis_index("x"), lax.axis_index("y"), lax.axis_index("c")
    me = 4 * xi + 2 * yi + ci
    s_me = 2 * xi + yi
    x0, tgt = x[0], loss_target[0]
    shard = dict(w_ffn_in=w_ffn_in, w_ffn_out=w_ffn_out, gdn_w_in=gdn_w_in, gdn_w_out=gdn_w_out, dsw_w_in=dsw_w_in, dsw_w_out=dsw_w_out)

    whole = lambda a: Row(a, a.shape, lambda i: (0,) * a.ndim)
    (cond8,) = rowwise(lambda ids, v: (_silu(v),), [whole(c.reshape(8, LANES))], [], [Out((8, LANES), F32, (8, LANES), lambda i: (0, 0))], (1,), "cond")
    cond_all = all_gather_small(cond8, "gather_cond").reshape(N_DEV, D)
    cond16 = jnp.concatenate([cond_all, jnp.zeros((8, D), F32)], axis=0)
    ada_cols = w_ada.shape[2]
    mods = [matmul(cond16, w_ada[l], "nn", F32, f"ada_{l}")[:N_DEV] for l in range(2)]
    buf = jnp.concatenate([jnp.stack(mods, axis=1).reshape(-1, LANES), gdn_conv.reshape(-1, LANES)], axis=0)
    n_mod_rows = N_DEV * 2 * ada_cols // LANES
    got = all_gather_small(buf, "gather_mod").reshape(N_DEV, buf.shape[0], LANES)
    mod_parts, conv_parts = [], []
    for s in range(N_SHARD):
        from_dev = got[2 * s]
        mod_parts.append(lax.dynamic_index_in_dim(from_dev[:n_mod_rows].reshape(N_DEV, 2, ada_cols), me, 0, keepdims=False))
        conv_parts.append(from_dev[n_mod_rows:].reshape(4, -1))
    mod_nb = jnp.concatenate(mod_parts, axis=1)
    conv_w = jnp.concatenate(conv_parts, axis=1)
    (mod,) = rowwise(lambda ids, a, b: (a + b,), [whole(mod_nb), whole(b_ada)], [], [Out(mod_nb.shape, F32, mod_nb.shape, lambda i: (0, 0))], (1,), "mod_bias")
    mod = mod.reshape(2, 6, 1, D)
    sh1, sc1, g1, sh2, sc2, g2 = ([mod[l, k] for l in range(2)] for k in range(6))
    gmix = [norm_mix[l][None] for l in range(2)]
    gffn = [norm_ffn[l][None] for l in range(2)]

    wp = pack_blocks([shard[n] for n, _ in PACK], BF16)
    wall = all_gather_shards(wp)
    shapes = [shard[n].shape for n, _ in PACK]
    per_shard = [unpack_blocks(wall[s], shapes) for s in range(N_SHARD)]
    W = {n: jnp.concatenate([per_shard[s][k] for s in range(N_SHARD)], axis=ax) for k, (n, ax) in enumerate(PACK)}
    gw = W["gdn_w_in"][0]
    w_gdn = jnp.concatenate([gw, jnp.zeros((D, GDN_PROJ - gw.shape[1]), BF16)], axis=1)
    alog, dtb = _pad_lanes(gdn_a_log[0]), _pad_lanes(gdn_dt_bias[0])
    qg2 = jnp.concatenate([dsw_q_norm, dsw_q_norm], axis=1)
    kg2 = jnp.concatenate([dsw_k_norm, dsw_k_norm], axis=1)
    gdn_args = (w_gdn, conv_w, alog, dtb, gdn_out_norm, W["gdn_w_out"][0])
    dsw_args = (W["dsw_w_in"][0], qg2, kg2)

    (h10,) = rowwise(f_norm_only, [_wide(x0)], [gmix[0], sc1[0], sh1[0]], [_wide_out(S, BF16)], (nt,), "l0_norm")
    y0, sv_g = gdn_forward(h10, *gdn_args)
    x1, h20 = rowwise(f_resid_norm, [_wide(x0), _wide(y0)], [g1[0], gffn[0], sc2[0], sh2[0]], [_wide_out(S, F32), _wide_out(S, BF16)], (nt,), "l0_mid")
    f0, sv_f0 = ffn_forward(h20, W["w_ffn_in"][0], W["w_ffn_out"][0], "0")
    x2, h11 = rowwise(f_resid_norm, [_wide(x1), _wide(f0)], [g2[0], gmix[1], sc1[1], sh1[1]], [_wide_out(S, F32), _wide_out(S, BF16)], (nt,), "l1_in")
    y1, sv_d = dsw_forward(h11, *dsw_args, rel_bias, W["dsw_w_out"][0])
    x3, h21 = rowwise(f_resid_norm, [_wide(x2), _wide(y1)], [g1[1], gffn[1], sc2[1], sh2[1]], [_wide_out(S, F32), _wide_out(S, BF16)], (nt,), "l1_mid")
    f1, sv_f1 = ffn_forward(h21, W["w_ffn_in"][1], W["w_ffn_out"][1], "1")
    part_spec = lambda a: Row(a, (None, 1, D), lambda i: (i, 0, 0))
    (parts,) = rowwise(f_loss, [_wide(x3), _wide(f1), _wide(tgt)], [g2[1]], [Out((nt, 1, D), F32, (None, 1, D), lambda i: (i, 0, 0))], (nt,), "loss")
    loss = lax.psum(jnp.sum(parts), ("x", "y", "c"))

    (dx3, df1), (dg2_1,) = rowwise_bwd(f_loss, [_wide(x3), _wide(f1, gdtype=BF16), _wide(tgt, diff=False)], [g2[1]],
                                       [part_spec(jnp.ones((nt, 1, D), F32))], (nt,), "loss_bwd")
    dh21, d_win1, d_wout1 = ffn_backward(df1, sv_f1, W["w_ffn_in"][1], W["w_ffn_out"][1], "1")
    (dx2, dy1), (dg1_1, dgf1, dsc2_1, dsh2_1) = rowwise_bwd(
        f_resid_norm, [_wide(x2), _wide(y1, gdtype=BF16)], [g1[1], gffn[1], sc2[1], sh2[1]], [_wide(dx3), _wide(dh21)], (nt,), "l1_mid_bwd")
    dh11, g_d = dsw_backward(dy1, sv_d, *dsw_args, W["dsw_w_out"][0])
    (dx1, df0), (dg2_0, dgm1, dsc1_1, dsh1_1) = rowwise_bwd(
        f_resid_norm, [_wide(x1), _wide(f0, gdtype=BF16)], [g2[0], gmix[1], sc1[1], sh1[1]], [_wide(dx2), _wide(dh11)], (nt,), "l1_in_bwd")
    dh20, d_win0, d_wout0 = ffn_backward(df0, sv_f0, W["w_ffn_in"][0], W["w_ffn_out"][0], "0")
    (dx0p, dy0), (dg1_0, dgf0, dsc2_0, dsh2_0) = rowwise_bwd(
        f_resid_norm, [_wide(x0), _wide(y0, gdtype=BF16)], [g1[0], gffn[0], sc2[0], sh2[0]], [_wide(dx1), _wide(dh20)], (nt,), "l0_mid_bwd")
    dh10, g_g = gdn_backward(dy0, sv_g, *gdn_args)
    (grad_x,), (dgm0, dsc1_0, dsh1_0) = rowwise_bwd(f_first, [_wide(x0)], [gmix[0], sc1[0], sh1[0]], [_wide(dx0p), _wide(dh10)], (nt,), "l0_norm_bwd")

    dmod = jnp.concatenate([dsh1_0, dsc1_0, dg1_0, dsh2_0, dsc2_0, dg2_0, dsh1_1, dsc1_1, dg1_1, dsh2_1, dsc2_1, dg2_1], axis=1)
    d_rel = jnp.transpose(g_d["rel"][:, :, 0])
    fold = lambda v: v[:, :DSW_DH] + v[:, DSW_DH:]
    small = [dmod, jnp.concatenate([dgm0, dgm1], axis=1), jnp.concatenate([dgf0, dgf1], axis=1), g_g["conv"].reshape(1, -1),
             g_g["alog"], g_g["dtb"], g_g["gain"], _pad_lanes(fold(g_d["q_gain2"])[0]), _pad_lanes(fold(g_d["k_gain2"])[0]),
             d_rel.reshape(1, -1)]
    sizes = [v.shape[1] // LANES for v in small]
    n_rows = sum(sizes)
    pad_rows = -(-n_rows // 8) * 8
    sbuf = jnp.concatenate([v.reshape(-1, LANES) for v in small] + [jnp.zeros((pad_rows - n_rows, LANES), F32)], axis=0)
    sgot = all_gather_small(sbuf, "gather_small_grads")
    ssum = add_rows([sgot[d * pad_rows:(d + 1) * pad_rows] for d in range(N_DEV)], F32, "sum_small_grads", rt=pad_rows)
    offs = np.cumsum([0] + sizes)
    take = lambda k: ssum[offs[k]:offs[k + 1]].reshape(1, -1)
    grad_b_ada = take(0).reshape(2, 6 * D)
    grad_norm_mix = take(1).reshape(2, D)
    grad_norm_ffn = take(2).reshape(2, D)
    conv_full = take(3).reshape(4, -1)
    ncv = gdn_conv.shape[2]
    grad_gdn_conv = lax.dynamic_slice_in_dim(conv_full, s_me * ncv, ncv, axis=1)[None]
    grad_a_log = take(4)[:, :GDN_H]
    grad_dt_bias = take(5)[:, :GDN_H]
    grad_out_norm = take(6)
    grad_q_norm = take(7)[:, :DSW_DH]
    grad_k_norm = take(8)[:, :DSW_DH]
    grad_rel = take(9).reshape(REL_BUCKETS, 3 * GDN_H)
    dmod_all = sgot.reshape(N_DEV, pad_rows, LANES)[:, :sizes[0]].reshape(N_DEV, 2, 6 * D)
    dmod_mine = lax.dynamic_slice_in_dim(dmod_all, s_me * ada_cols, ada_cols, axis=2)
    dmod16 = jnp.concatenate([dmod_mine, jnp.zeros_like(dmod_mine)], axis=0)
    grad_w_ada = jnp.stack([matmul(cond16, dmod16[:, l], "tn", F32, f"ada_dw_{l}") for l in range(2)])

    full = {"gdn_w_in": g_g["w_in"][:, :gdn_w_in.shape[2] * N_SHARD][None], "gdn_w_out": g_g["w_out"][None],
            "w_ffn_in": jnp.stack([d_win0, d_win1]), "w_ffn_out": jnp.stack([d_wout0, d_wout1]),
            "dsw_w_in": g_d["w_in"][None], "dsw_w_out": g_d["w_out"][None]}
    g32 = jnp.stack([pack_blocks([_shard_slice(full[n], ax, s) for n, ax in PACK], F32) for s in range(N_SHARD)])
    R = g32.shape[1]
    Rh = R // 2
    keep = lax.dynamic_slice_in_dim(g32, ci * Rh, Rh, axis=1)
    give = lax.dynamic_slice_in_dim(g32, (1 - ci) * Rh, Rh, axis=1).astype(BF16)
    from_sib = sibling_exchange(give, "grads_to_sibling")
    part = add_rows([keep.reshape(N_SHARD * Rh, D), from_sib.reshape(N_SHARD * Rh, D)], BF16, "grads_chip_sum").reshape(N_SHARD, Rh, D)
    others = scatter_to_chips(part)
    own = lax.dynamic_index_in_dim(part, s_me, 0, keepdims=False)
    half_sum = add_rows([own, others[0], others[1], others[2]], F32, "grads_sum")
    sib_half = sibling_exchange(half_sum, "grads_from_sibling")
    lo = jnp.where(ci == 0, half_sum, sib_half)
    hi = jnp.where(ci == 0, sib_half, half_sum)
    gsh = dict(zip([n for n, _ in PACK], unpack_blocks(jnp.concatenate([lo, hi], axis=0), shapes)))

    grads = dict(w_ada=grad_w_ada, b_ada=grad_b_ada, norm_mix=grad_norm_mix, norm_ffn=grad_norm_ffn, w_ffn_in=gsh["w_ffn_in"],
                 w_ffn_out=gsh["w_ffn_out"], gdn_w_in=gsh["gdn_w_in"], gdn_conv=grad_gdn_conv, gdn_a_log=grad_a_log,
                 gdn_dt_bias=grad_dt_bias, gdn_out_norm=grad_out_norm, gdn_w_out=gsh["gdn_w_out"], dsw_w_in=gsh["dsw_w_in"],
                 dsw_q_norm=grad_q_norm, dsw_k_norm=grad_k_norm, dsw_w_out=gsh["dsw_w_out"], rel_bias=grad_rel)
    weights = dict(w_ada=w_ada, b_ada=b_ada, norm_mix=norm_mix, norm_ffn=norm_ffn, w_ffn_in=w_ffn_in, w_ffn_out=w_ffn_out,
                   gdn_w_in=gdn_w_in, gdn_conv=gdn_conv, gdn_a_log=gdn_a_log, gdn_dt_bias=gdn_dt_bias, gdn_out_norm=gdn_out_norm,
                   gdn_w_out=gdn_w_out, dsw_w_in=dsw_w_in, dsw_q_norm=dsw_q_norm, dsw_k_norm=dsw_k_norm, dsw_w_out=dsw_w_out,
                   rel_bias=rel_bias)
    ms = dict(w_ada=m_w_ada, b_ada=m_b_ada, norm_mix=m_norm_mix, norm_ffn=m_norm_ffn, w_ffn_in=m_w_ffn_in, w_ffn_out=m_w_ffn_out,
              gdn_w_in=m_gdn_w_in, gdn_conv=m_gdn_conv, gdn_a_log=m_gdn_a_log, gdn_dt_bias=m_gdn_dt_bias, gdn_out_norm=m_gdn_out_norm,
              gdn_w_out=m_gdn_w_out, dsw_w_in=m_dsw_w_in, dsw_q_norm=m_dsw_q_norm, dsw_k_norm=m_dsw_k_norm, dsw_w_out=m_dsw_w_out,
              rel_bias=m_rel_bias)
    vs = dict(w_ada=v_w_ada, b_ada=v_b_ada, norm_mix=v_norm_mix, norm_ffn=v_norm_ffn, w_ffn_in=v_w_ffn_in, w_ffn_out=v_w_ffn_out,
              gdn_w_in=v_gdn_w_in, gdn_conv=v_gdn_conv, gdn_a_log=v_gdn_a_log, gdn_dt_bias=v_gdn_dt_bias, gdn_out_norm=v_gdn_out_norm,
              gdn_w_out=v_gdn_w_out, dsw_w_in=v_dsw_w_in, dsw_q_norm=v_dsw_q_norm, dsw_k_norm=v_dsw_k_norm, dsw_w_out=v_dsw_w_out,
              rel_bias=v_rel_bias)
    names = list(weights)
    deltas, new_m, new_v = [], [], []
    for n in names:
        g = grads[n].reshape(weights[n].shape)
        grads[n] = g
        d, nm, nv = adamw(weights[n], g, ms[n], vs[n], f"adamw_{n}")
        deltas.append(d)
        new_m.append(nm)
        new_v.append(nv)
    return (loss, grad_x[None], *[grads[n] for n in names], *deltas, *new_m, *new_v)
```

```python
import functools
import math

import numpy as np
import jax
import jax.numpy as jnp
from jax import lax
from jax.experimental import pallas as pl
from jax.experimental.pallas import tpu as pltpu

F32 = jnp.float32
BF16 = jnp.bfloat16
SDS = jax.ShapeDtypeStruct
MESH = pl.DeviceIdType.MESH
ANY = pl.BlockSpec(memory_space=pl.ANY)

D = 1024
EPS = 1e-6
LANES = 128
GDN_H = 8
GDN_DK = 128
GDN_C = 64
DSW_GROUPS = ((128, 1), (512, 4), (2048, 16))
DSW_SPAN = 128
DSW_DH = 64
DSW_HG = 512
REL_BUCKETS = 32
REL_MAX_DIST = 2048
FFN = 2816
N_SHARD = 4
N_DEV = 8
VMEM_LIMIT = 48 * 1024 * 1024
NEG = -1e30

ADAM_LR, ADAM_B1, ADAM_B2, ADAM_EPS, ADAM_WD, ADAM_STEP = 0.001, 0.9, 0.999, 1e-08, 0.01, 10


def _cp(n_axes):
    return pltpu.CompilerParams(dimension_semantics=("arbitrary",) * n_axes, vmem_limit_bytes=VMEM_LIMIT)


def _blk(dim, cap):
    if dim <= cap:
        return dim
    best = None
    for b in range(LANES, cap + 1, LANES):
        if dim % b == 0:
            best = b
    assert best is not None, (dim, cap)
    return best


MAX_SHARD_BLOCK = 1408
def matmul(a, b, mode, out_dtype, name, cap_m=1024, cap_n=1024, cap_k=2048, col_shards=0):
    ns = col_shards
    if mode == "nn":
        (M, K) = a.shape
        K2, N = (b.shape[1], ns * b.shape[2]) if ns else b.shape
    elif mode == "nt":
        (M, K) = a.shape
        N, K2 = (b.shape[1], ns * b.shape[2]) if ns else b.shape
    else:
        (K, M), (K2, N) = a.shape, b.shape
    assert K == K2, (a.shape, b.shape, mode)
    if K <= 3072:
        cap_k = K
        if K > 2048:
            cap_n = 512
    n_unit = N // ns if (ns and mode != "nt") else N
    k_unit = K // ns if (ns and mode == "nt") else K
    bm = _blk(M, cap_m)
    bn = _blk(n_unit, MAX_SHARD_BLOCK) if n_unit != N else _blk(N, cap_n)
    if k_unit != K:
        bk = _blk(k_unit, MAX_SHARD_BLOCK)
    else:
        bk = _blk(K, 1024 if (ns and mode == "tn") else cap_k)
    nk = K // bk
    nps, kps = n_unit // bn, k_unit // bk
    dims = {"nn": ((1,), (0,)), "nt": ((1,), (1,)), "tn": ((0,), (0,))}[mode]

    def dot(a_ref, b_ref):
        return lax.dot_general(a_ref[...].astype(BF16), b_ref[...].astype(BF16), (dims, ((), ())), preferred_element_type=F32)

    def body_one(a_ref, b_ref, o_ref):
        o_ref[...] = dot(a_ref, b_ref).astype(o_ref.dtype)

    def body_acc(a_ref, b_ref, o_ref, acc_ref):
        k = pl.program_id(2)

        @pl.when(k == 0)
        def _():
            acc_ref[...] = jnp.zeros_like(acc_ref)

        acc_ref[...] += dot(a_ref, b_ref)

        @pl.when(k == nk - 1)
        def _():
            o_ref[...] = acc_ref[...].astype(o_ref.dtype)

    a_spec = pl.BlockSpec((bk, bm), lambda i, j, k: (k, i)) if mode == "tn" else pl.BlockSpec((bm, bk), lambda i, j, k: (i, k))
    if mode == "nt":
        b_spec = pl.BlockSpec((None, bn, bk), lambda i, j, k: (k // kps, j, k % kps)) if ns else pl.BlockSpec((bn, bk), lambda i, j, k: (j, k))
    elif mode == "nn" and ns:
        b_spec = pl.BlockSpec((None, bk, bn), lambda i, j, k: (j // nps, k, j % nps))
    else:
        b_spec = pl.BlockSpec((bk, bn), lambda i, j, k: (k, j))
    if mode == "tn" and ns:
        o_spec, o_shape = pl.BlockSpec((None, bm, bn), lambda i, j, k: (j // nps, i, j % nps)), (ns, M, n_unit)
    else:
        o_spec, o_shape = pl.BlockSpec((bm, bn), lambda i, j, k: (i, j)), (M, N)
    return pl.pallas_call(
        body_one if nk == 1 else body_acc, name=name, grid=(M // bm, N // bn, nk),
        in_specs=[a_spec, b_spec], out_specs=o_spec,
        out_shape=SDS(o_shape, out_dtype), scratch_shapes=[] if nk == 1 else [pltpu.VMEM((bm, bn), F32)],
        compiler_params=_cp(3),
    )(a, b)


class Row:
    def __init__(self, arr, bshape, imap, splits=None, diff=True, acc=False, gdtype=F32, gshape=None, gbshape=None, gimap=None,
                 lead=0):
        self.arr, self.bshape, self.imap = arr, tuple(bshape), imap
        self.splits, self.lead = splits, lead
        self.diff, self.acc, self.gdtype = diff, acc, gdtype
        self.gshape = tuple(arr.shape) if gshape is None else tuple(gshape)
        self.gbshape = self.bshape if gbshape is None else tuple(gbshape)
        self.gimap = imap if gimap is None else gimap

    def gspec(self):
        return pl.BlockSpec(self.gbshape, self.gimap)

    def spec(self):
        return pl.BlockSpec(self.bshape, self.imap)

    def pieces(self, ref):
        return _load_pieces(ref, self.splits, self.lead)

    def n_pieces(self):
        return _n_pieces(self.splits, self.lead)


class Out:
    def __init__(self, shape, dtype, bshape, imap, splits=None, lead=0):
        self.shape, self.dtype, self.bshape, self.imap = tuple(shape), dtype, tuple(bshape), imap
        self.splits, self.lead = splits, lead

    def n_pieces(self):
        return _n_pieces(self.splits, self.lead)


def _n_pieces(splits, lead):
    return lead if lead else (1 if splits is None else len(splits))


def _load_pieces(ref, splits, lead):
    if lead:
        return [ref[k].astype(F32) for k in range(lead)]
    if splits is None:
        return [ref[...].astype(F32)]
    out, o = [], 0
    for w in splits:
        out.append(ref[..., o:o + w].astype(F32))
        o += w
    return out


def _store_pieces(ref, splits, lead, vals, accumulate=False):
    def put(idx, v):
        if accumulate:
            ref[idx] += v.astype(ref.dtype)
        else:
            ref[idx] = v.astype(ref.dtype)

    if lead:
        for k in range(lead):
            put(k, vals[k])
    elif splits is None:
        put(..., vals[0])
    else:
        o = 0
        for w, v in zip(splits, vals):
            put((..., slice(o, o + w)), v)
            o += w


def rowwise(fn, rows, params, outs, grid, name):
    nr, npar = len(rows), len(params)

    def body(*refs):
        ids = tuple(pl.program_id(a) for a in range(len(grid)))
        vals = []
        for r, ref in zip(rows, refs[:nr]):
            vals += r.pieces(ref)
        pvals = [ref[...].astype(F32) for ref in refs[nr:nr + npar]]
        res = list(fn(ids, *vals, *pvals))
        o = 0
        for spec, ref in zip(outs, refs[nr + npar:]):
            n = spec.n_pieces()
            _store_pieces(ref, spec.splits, spec.lead, res[o:o + n])
            o += n

    nz = len(grid)
    pspecs = [pl.BlockSpec(p.shape, (lambda *ids, _n=p.ndim: (0,) * _n)) for p in params]
    res = pl.pallas_call(
        body, name=name, grid=grid,
        in_specs=[r.spec() for r in rows] + pspecs,
        out_specs=[pl.BlockSpec(o.bshape, o.imap) for o in outs],
        out_shape=[SDS(o.shape, o.dtype) for o in outs],
        compiler_params=_cp(nz),
    )(*[r.arr for r in rows], *params)
    return list(res)


def rowwise_bwd(fn, rows, params, cots, grid, name):
    nr, npar, nc = len(rows), len(params), len(cots)
    drows = [r for r in rows if r.diff]
    nz = len(grid)

    def body(*refs):
        ids = tuple(pl.program_id(a) for a in range(nz))
        row_refs, par_refs = refs[:nr], refs[nr:nr + npar]
        cot_refs = refs[nr + npar:nr + npar + nc]
        drow_refs = refs[nr + npar + nc:nr + npar + nc + len(drows)]
        dpar_refs = refs[nr + npar + nc + len(drows):]
        pieces, is_diff = [], []
        for r, ref in zip(rows, row_refs):
            ps = r.pieces(ref)
            pieces += ps
            is_diff += [r.diff] * len(ps)
        pvals = [ref[...].astype(F32) for ref in par_refs]
        dvals = [p for p, dflag in zip(pieces, is_diff) if dflag]
        nd = len(dvals)

        def f(*args):
            it = iter(args[:nd])
            full = [next(it) if dflag else p for p, dflag in zip(pieces, is_diff)]
            return tuple(fn(ids, *full, *args[nd:]))

        _, vjp = jax.vjp(f, *dvals, *pvals)
        cvals = []
        for c, ref in zip(cots, cot_refs):
            cvals += c.pieces(ref)
        g = vjp(tuple(cvals))
        o = 0
        first_inner = ids[-1] == 0
        for r, ref in zip(drows, drow_refs):
            n = r.n_pieces()
            gs = g[o:o + n]
            o += n
            if r.acc:
                @pl.when(first_inner)
                def _(ref=ref):
                    ref[...] = jnp.zeros_like(ref)
            _store_pieces(ref, r.splits, r.lead, gs, accumulate=r.acc)
        first = functools.reduce(jnp.logical_and, [i == 0 for i in ids])
        for ref, gp in zip(dpar_refs, g[nd:]):
            @pl.when(first)
            def _(ref=ref):
                ref[...] = jnp.zeros_like(ref)
            ref[...] += gp

    pspecs = [pl.BlockSpec(p.shape, (lambda *ids, _n=p.ndim: (0,) * _n)) for p in params]
    res = pl.pallas_call(
        body, name=name, grid=grid,
        in_specs=[r.spec() for r in rows] + pspecs + [c.spec() for c in cots],
        out_specs=[r.gspec() for r in drows] + pspecs,
        out_shape=[SDS(r.gshape, r.gdtype) for r in drows] + [SDS(p.shape, F32) for p in params],
        compiler_params=_cp(nz),
    )(*[r.arr for r in rows], *params, *[c.arr for c in cots])
    res = list(res)
    return res[:len(drows)], res[len(drows):]


def _sigmoid(x):
    return 0.5 * (jnp.tanh(0.5 * x) + 1.0)


def _silu(x):
    return x * _sigmoid(x)


def _normmod(x, gain, sc, sh):
    inv = lax.rsqrt(jnp.mean(x * x, axis=-1, keepdims=True) + EPS)
    return x * inv * gain * (1.0 + sc) + sh


def f_first(ids, x, gain, sc, sh):
    return x, _normmod(x, gain, sc, sh)


def f_resid_norm(ids, x, y, g, gain, sc, sh):
    xn = x + g * y
    return xn, _normmod(xn, gain, sc, sh)


def f_swiglu(ids, gate, up):
    return (_silu(gate) * up,)


def f_loss(ids, x, y, tgt, g):
    out = x + g * y
    e = out - tgt
    part = 0.5 * jnp.sum(e * e, axis=0, keepdims=True) * (1.0 / D)
    return (part,)


def _softplus(x):
    return jnp.maximum(x, 0.0) + jnp.log(1.0 + jnp.exp(-jnp.abs(x)))


def _chunk_tril(T):
    r = lax.broadcasted_iota(jnp.int32, (T, T), 0)
    c = lax.broadcasted_iota(jnp.int32, (T, T), 1)
    return jnp.where((r // GDN_C == c // GDN_C) & (c <= r), 1.0, 0.0).astype(F32)


def _dot_hi(a, b, dims=((1,), (0,))):
    return lax.dot_general(a, b, (dims, ((), ())), precision=lax.Precision.HIGHEST, preferred_element_type=F32)


def _dot_x3(a, b, dims=((1,), (0,))):
    return lax.dot_general(a, b, (dims, ((), ())), precision=lax.Precision.HIGH, preferred_element_type=F32)


def f_gdn_gates(ids, ab, alog, dtb):
    T = ab.shape[0]
    g = -jnp.exp(alog) * _softplus(ab + dtb)
    beta = _sigmoid(ab)
    gcum = _dot_hi(_chunk_tril(T), g)
    row = lax.broadcasted_iota(jnp.int32, (LANES, LANES), 0)
    sel = lambda k: jnp.where(row == k, 1.0, 0.0).astype(F32)
    gcs = [_dot_hi(gcum, sel(h)) for h in range(GDN_H)]
    bts = [_dot_hi(beta, sel(GDN_H + h)) for h in range(GDN_H)]
    return (*gcs, *bts)


def f_gdn_post(ids, *args):
    os_, zs, gain = args[:GDN_H], args[GDN_H:2 * GDN_H], args[2 * GDN_H]
    out = []
    for o, z in zip(os_, zs):
        inv = lax.rsqrt(jnp.mean(o * o, axis=-1, keepdims=True) + EPS)
        out.append(o * inv * gain * _silu(z))
    return tuple(out)


def _qknorm1(x, gain2, scale):
    lane = lax.broadcasted_iota(jnp.int32, x.shape, 1)
    lo = lane < DSW_DH
    x2 = x * x
    s_all = jnp.sum(x2, axis=-1, keepdims=True)
    s_lo = jnp.sum(jnp.where(lo, x2, 0.0), axis=-1, keepdims=True)
    ms = jnp.where(lo, s_lo, s_all - s_lo) * (1.0 / DSW_DH)
    return x * lax.rsqrt(ms + EPS) * (gain2 * scale)


def f_qknorm(ids, *args):
    return tuple(_qknorm1(x, args[-1], 1.0) for x in args[:-1])


def f_qnorm(ids, *args):
    return tuple(_qknorm1(x, args[-1], DSW_DH ** -0.5) for x in args[:-1])


def f_combine(ids, o0, o1, o2, l0, l1, l2):
    m = jnp.maximum(jnp.maximum(l0, l1), l2)
    e0, e1, e2 = jnp.exp(l0 - m), jnp.exp(l1 - m), jnp.exp(l2 - m)
    den = e0 + e1 + e2
    o = (e0 * o0 + e1 * o1 + e2 * o2) / den
    return o, m + jnp.log(den)


GDN_T = 512
HALO = 8


def _conv_pre(xx, w):
    acc = xx * w[3:4, :]
    for j in range(3):
        acc = acc + pltpu.roll(xx, shift=3 - j, axis=0) * w[j:j + 1, :]
    return acc


def _qkv_act(pre, cidx):
    s = _silu(pre)
    r = lax.rsqrt(jnp.sum(s * s, axis=-1, keepdims=True) + EPS)
    scale = jnp.where(cidx < GDN_H, GDN_DK ** -0.5, 1.0).astype(F32)
    return jnp.where(cidx < 2 * GDN_H, s * r * scale, s)


def gdn_pre(proj, conv_w, S):
    nt = S // GDN_T
    hb = GDN_T // HALO

    def body(prev_ref, cur_ref, w_ref, o_ref):
        p, i = pl.program_id(0), pl.program_id(1)
        for h in range(GDN_H):
            cols = slice(LANES * h, LANES * (h + 1))
            prev = jnp.where(i > 0, prev_ref[:, cols], 0.0)
            xx = jnp.concatenate([prev, cur_ref[:, cols]], axis=0)
            pre = _conv_pre(xx, w_ref[:, cols])[HALO:]
            o_ref[h] = _qkv_act(pre, p * GDN_H + h)

    hv = GDN_H * LANES
    return pl.pallas_call(
        body, name="gdn_pre", grid=(3, nt),
        in_specs=[pl.BlockSpec((HALO, hv), lambda p, i: (jnp.maximum(i * hb - 1, 0), p)),
                  pl.BlockSpec((GDN_T, hv), lambda p, i: (i, p)),
                  pl.BlockSpec((4, hv), lambda p, i: (0, p))],
        out_specs=pl.BlockSpec((None, GDN_H, GDN_T, LANES), lambda p, i: (p, 0, i, 0)),
        out_shape=SDS((3, GDN_H, S, LANES), F32),
        compiler_params=_cp(2),
    )(proj, proj, conv_w)


def gdn_pre_bwd(proj, conv_w, dqkv, S):
    nt = S // GDN_T
    hb = GDN_T // HALO
    last_h = S // HALO - 1

    def body(prev_ref, cur_ref, next_ref, w_ref, d_ref, dnext_ref, dx_ref, dw_ref):
        p, i = pl.program_id(0), pl.program_id(1)

        @pl.when(i == 0)
        def _():
            dw_ref[...] = jnp.zeros_like(dw_ref)

        for h in range(GDN_H):
            cols = slice(LANES * h, LANES * (h + 1))
            w = w_ref[:, cols]
            prev = jnp.where(i > 0, prev_ref[:, cols], 0.0)
            xx = jnp.concatenate([prev, cur_ref[:, cols], next_ref[:, cols]], axis=0)
            dnext = jnp.where(i < nt - 1, dnext_ref[h], 0.0)
            dd = jnp.concatenate([jnp.zeros((HALO, LANES), F32), d_ref[h], dnext], axis=0)
            pre = _conv_pre(xx, w)
            _, vjp = jax.vjp(lambda v, _c=p * GDN_H + h: _qkv_act(v, _c), pre)
            (dpre,) = vjp(dd)
            row = lax.broadcasted_iota(jnp.int32, dpre.shape, 0)
            dpre = jnp.where(row >= HALO, dpre, 0.0)
            dx = dpre * w[3:4, :]
            R = dpre.shape[0]
            for j in range(3):
                dx = dx + pltpu.roll(dpre, shift=R - (3 - j), axis=0) * w[j:j + 1, :]
            dx_ref[:, cols] = dx[HALO:HALO + GDN_T].astype(dx_ref.dtype)
            own = jnp.where(row < HALO + GDN_T, dpre, 0.0)
            rows_w = [jnp.sum(own * pltpu.roll(xx, shift=3 - j, axis=0), axis=0, keepdims=True) for j in range(3)]
            rows_w.append(jnp.sum(own * xx, axis=0, keepdims=True))
            r4 = lax.broadcasted_iota(jnp.int32, (4, LANES), 0)
            dw = jnp.zeros((4, LANES), F32)
            for j in range(4):
                dw = dw + jnp.where(r4 == j, rows_w[j], 0.0)
            dw_ref[:, cols] += dw

    hv = GDN_H * LANES
    return pl.pallas_call(
        body, name="gdn_pre_bwd", grid=(3, nt),
        in_specs=[pl.BlockSpec((HALO, hv), lambda p, i: (jnp.maximum(i * hb - 1, 0), p)),
                  pl.BlockSpec((GDN_T, hv), lambda p, i: (i, p)),
                  pl.BlockSpec((HALO, hv), lambda p, i: (jnp.minimum((i + 1) * hb, last_h), p)),
                  pl.BlockSpec((4, hv), lambda p, i: (0, p)),
                  pl.BlockSpec((None, GDN_H, GDN_T, LANES), lambda p, i: (p, 0, i, 0)),
                  pl.BlockSpec((None, GDN_H, HALO, LANES), lambda p, i: (p, 0, jnp.minimum((i + 1) * hb, last_h), 0))],
        out_specs=[pl.BlockSpec((GDN_T, hv), lambda p, i: (i, p)),
                   pl.BlockSpec((4, hv), lambda p, i: (0, p))],
        out_shape=[SDS((S, 3 * hv), BF16), SDS((4, 3 * hv), F32)],
        compiler_params=_cp(2),
    )(proj, proj, proj, conv_w, dqkv, dqkv)


_DIMS = {"nn": ((1,), (0,)), "nt": ((1,), (1,)), "tn": ((0,), (0,))}


def _mm_raw(a, b, mode, hi):
    if hi:
        return _dot_hi(a, b, _DIMS[mode])
    return lax.dot_general(a.astype(BF16), b.astype(BF16), (_DIMS[mode], ((), ())), preferred_element_type=F32)


@functools.partial(jax.custom_vjp, nondiff_argnums=(2, 3))
def mm(a, b, mode, hi):
    return _mm_raw(a, b, mode, hi)


def _mm_fwd(a, b, mode, hi):
    return _mm_raw(a, b, mode, hi), (a, b)


def _mm_bwd(mode, hi, res, dc):
    a, b = res
    if mode == "nn":
        da, db = mm(dc, b, "nt", hi), mm(a, dc, "tn", hi)
    elif mode == "nt":
        da, db = mm(dc, b, "nn", hi), mm(dc, a, "tn", hi)
    else:
        da, db = mm(b, dc, "nt", hi), mm(a, dc, "nn", hi)
    return da, db


mm.defvjp(_mm_fwd, _mm_bwd)


TRI_BASE = 8


def _unit_lower_inverses(Ls):
    n = Ls[0].shape[0]
    r = lax.broadcasted_iota(jnp.int32, (n, n), 0)
    c = lax.broadcasted_iota(jnp.int32, (n, n), 1)
    eye = jnp.where(r == c, 1.0, 0.0).astype(F32)
    base = r // TRI_BASE == c // TRI_BASE
    Ps = [jnp.where(base, -L, 0.0) for L in Ls]
    invs = [eye + P for P in Ps]
    k = 1
    while 2 * k < TRI_BASE:
        Ps = [_dot_x3(P, P) for P in Ps]
        invs = [inv + _dot_x3(inv, P) for inv, P in zip(invs, Ps)]
        k *= 2
    b = 2 * TRI_BASE
    while b <= n:
        off_mask = (r // b == c // b) & ((r % b) >= b // 2) & ((c % b) < b // 2)
        ts = [_dot_x3(inv, jnp.where(off_mask, L, 0.0)) for inv, L in zip(invs, Ls)]
        invs = [inv - _dot_x3(t, inv) for inv, t in zip(invs, ts)]
        b *= 2
    return invs


@jax.custom_vjp
def tri_solve2(Ls, r1s, r2s):
    invs = _unit_lower_inverses(Ls)
    return [_dot_x3(i, r) for i, r in zip(invs, r1s)], [_dot_x3(i, r) for i, r in zip(invs, r2s)]


def _tri_fwd(Ls, r1s, r2s):
    invs = _unit_lower_inverses(Ls)
    s1s = [_dot_x3(i, r) for i, r in zip(invs, r1s)]
    s2s = [_dot_x3(i, r) for i, r in zip(invs, r2s)]
    return (s1s, s2s), (invs, s1s, s2s)


def _tri_bwd(res, ds):
    invs, s1s, s2s = res
    d1s = [_dot_x3(i, d, _DIMS["tn"]) for i, d in zip(invs, ds[0])]
    d2s = [_dot_x3(i, d, _DIMS["tn"]) for i, d in zip(invs, ds[1])]
    dLs = [-(_dot_x3(d1, s1, _DIMS["nt"]) + _dot_x3(d2, s2, _DIMS["nt"])) for d1, s1, d2, s2 in zip(d1s, s1s, d2s, s2s)]
    return dLs, d1s, d2s


tri_solve2.defvjp(_tri_fwd, _tri_bwd)


def _gdn_chunk(qs, ks, vs, gcbs, btbs, Ss):
    C = qs[0].shape[0]
    r = lax.broadcasted_iota(jnp.int32, (C, C), 0)
    c = lax.broadcasted_iota(jnp.int32, (C, C), 1)
    causal, strict = c <= r, c < r
    rows = lax.broadcasted_iota(jnp.int32, gcbs[0].shape, 0)
    Gs = [g[:, :C] for g in gcbs]
    decays = [jnp.exp(jnp.where(causal, G - G.T, NEG)) for G in Gs]
    kbs = [k * b for k, b in zip(ks, btbs)]
    vbs = [v * b for v, b in zip(vs, btbs)]
    Ls = [jnp.where(strict, mm(kb, k, "nt", False) * d, 0.0) for kb, k, d in zip(kbs, ks, decays)]
    egs = [jnp.exp(g) for g in gcbs]
    us, ws = tri_solve2(Ls, vbs, [kb * eg for kb, eg in zip(kbs, egs)])
    qks = [jnp.where(causal, mm(q, k, "nt", False) * d, 0.0) for q, k, d in zip(qs, ks, decays)]
    g_lasts = [jnp.sum(jnp.where(rows == C - 1, g, 0.0), axis=0, keepdims=True) for g in gcbs]
    q_decs = [q * eg for q, eg in zip(qs, egs)]
    k_decs = [k * jnp.exp(gl - g) for k, gl, g in zip(ks, g_lasts, gcbs)]
    v_news = [u - mm(w, S, "nn", False) for u, w, S in zip(us, ws, Ss)]
    os_ = [mm(qd, S, "nn", False) + mm(qk, vn, "nn", False) for qd, S, qk, vn in zip(q_decs, Ss, qks, v_news)]
    S_news = [S * jnp.exp(gl) + mm(kd, vn, "tn", False) for S, gl, kd, vn in zip(Ss, g_lasts, k_decs, v_news)]
    return os_, S_news


def gdn_core(qkv, gc, bt, S):
    nchunk = S // GDN_C

    def body(qkv_ref, g_ref, b_ref, o_ref, st_ref, s_scr):
        n = pl.program_id(0)

        @pl.when(n == 0)
        def _():
            s_scr[...] = jnp.zeros_like(s_scr)

        heads = range(GDN_H)
        S_in = [s_scr[h] for h in heads]
        os_, S_new = _gdn_chunk([qkv_ref[0, h] for h in heads], [qkv_ref[1, h] for h in heads], [qkv_ref[2, h] for h in heads],
                                [g_ref[h] for h in heads], [b_ref[h] for h in heads], S_in)
        for h in heads:
            st_ref[h] = S_in[h]
            o_ref[h] = os_[h]
            s_scr[h] = S_new[h]

    blk3 = pl.BlockSpec((3, GDN_H, GDN_C, LANES), lambda n: (0, 0, n, 0))
    hb = pl.BlockSpec((GDN_H, GDN_C, LANES), lambda n: (0, n, 0))
    return pl.pallas_call(
        body, name="gdn_core", grid=(nchunk,),
        in_specs=[blk3, hb, hb],
        out_specs=[hb, pl.BlockSpec((GDN_H, None, GDN_DK, LANES), lambda n: (0, n, 0, 0))],
        out_shape=[SDS((GDN_H, S, LANES), F32), SDS((GDN_H, nchunk, GDN_DK, LANES), F32)],
        scratch_shapes=[pltpu.VMEM((GDN_H, GDN_DK, LANES), F32)],
        compiler_params=_cp(1),
    )(qkv, gc, bt)


def gdn_core_bwd(qkv, gc, bt, states, do, S):
    nchunk = S // GDN_C

    def body(qkv_ref, g_ref, b_ref, st_ref, do_ref, dqkv_ref, dg_ref, db_ref, ds_scr):
        n = pl.program_id(0)

        @pl.when(n == 0)
        def _():
            ds_scr[...] = jnp.zeros_like(ds_scr)

        heads = range(GDN_H)
        _, vjp = jax.vjp(_gdn_chunk, [qkv_ref[0, h] for h in heads], [qkv_ref[1, h] for h in heads], [qkv_ref[2, h] for h in heads],
                         [g_ref[h] for h in heads], [b_ref[h] for h in heads], [st_ref[h] for h in heads])
        dq, dk, dv, dg, db, dS = vjp(([do_ref[h] for h in heads], [ds_scr[h] for h in heads]))
        for h in heads:
            dqkv_ref[0, h] = dq[h]
            dqkv_ref[1, h] = dk[h]
            dqkv_ref[2, h] = dv[h]
            dg_ref[h] = dg[h]
            db_ref[h] = db[h]
            ds_scr[h] = dS[h]

    rev = lambda n: nchunk - 1 - n
    blk3 = pl.BlockSpec((3, GDN_H, GDN_C, LANES), lambda n: (0, 0, rev(n), 0))
    hb = pl.BlockSpec((GDN_H, GDN_C, LANES), lambda n: (0, rev(n), 0))
    return pl.pallas_call(
        body, name="gdn_core_bwd", grid=(nchunk,),
        in_specs=[blk3, hb, hb, pl.BlockSpec((GDN_H, None, GDN_DK, LANES), lambda n: (0, rev(n), 0, 0)), hb],
        out_specs=[blk3, hb, hb],
        out_shape=[SDS((3, GDN_H, S, LANES), F32), SDS((GDN_H, S, LANES), F32), SDS((GDN_H, S, LANES), F32)],
        scratch_shapes=[pltpu.VMEM((GDN_H, GDN_DK, LANES), F32)],
        compiler_params=_cp(1),
    )(qkv, gc, bt, states, do)


GDN_MAIN = 4 * GDN_H * LANES
GDN_PROJ = GDN_MAIN + LANES
RT = 256


def gdn_forward(h, w_in, conv_w, alog, dtb, out_gain, w_out):
    S = h.shape[0]
    nt = S // RT
    proj = matmul(h, w_in, "nn", F32, "gdn_in")
    qkv = gdn_pre(proj, conv_w, S)
    ab_row = Row(proj, (RT, LANES), lambda i: (i, GDN_MAIN // LANES), gdtype=BF16, gshape=(S, LANES), gimap=lambda i: (i, 0))
    hm = lambda i: (0, i, 0)
    hv = GDN_H * LANES
    gc, bt = rowwise(f_gdn_gates, [ab_row], [alog, dtb],
                     [Out((GDN_H, S, LANES), F32, (GDN_H, RT, LANES), hm, lead=GDN_H)] * 2, (nt,), "gdn_gates")
    o, states = gdn_core(qkv, gc, bt, S)
    o_row = Row(o, (GDN_H, RT, LANES), hm, lead=GDN_H)
    z_row = Row(proj, (RT, hv), lambda i: (i, 3), splits=[LANES] * GDN_H, gdtype=BF16, gshape=(S, hv), gimap=lambda i: (i, 0))
    (on,) = rowwise(f_gdn_post, [o_row, z_row], [out_gain],
                    [Out((S, hv), BF16, (RT, hv), lambda i: (i, 0), splits=[LANES] * GDN_H)], (nt,), "gdn_post")
    y = matmul(on, w_out, "nn", F32, "gdn_out")
    saved = dict(h=h, proj=proj, qkv=qkv, gc=gc, bt=bt, states=states, o=o, on=on, ab_row=ab_row, o_row=o_row, z_row=z_row)
    return y, saved


def gdn_backward(dy, sv, w_in, conv_w, alog, dtb, out_gain, w_out):
    S = dy.shape[0]
    nt = S // RT
    hm = lambda i: (0, i, 0)
    hv = GDN_H * LANES
    don = matmul(dy, w_out, "nt", F32, "gdn_out_dx")
    d_w_out = matmul(sv["on"], dy, "tn", F32, "gdn_out_dw")
    (do, dz), (d_gain,) = rowwise_bwd(f_gdn_post, [sv["o_row"], sv["z_row"]], [out_gain],
                                      [Row(don, (RT, hv), lambda i: (i, 0), splits=[LANES] * GDN_H)], (nt,), "gdn_post_bwd")
    dqkv, dgc, dbt = gdn_core_bwd(sv["qkv"], sv["gc"], sv["bt"], sv["states"], do, S)
    head_blk = lambda a: Row(a, (GDN_H, RT, LANES), hm, lead=GDN_H)
    (dab,), (d_alog, d_dtb) = rowwise_bwd(f_gdn_gates, [sv["ab_row"]], [alog, dtb], [head_blk(dgc), head_blk(dbt)],
                                          (nt,), "gdn_gates_bwd")
    dqkv_proj, d_conv = gdn_pre_bwd(sv["proj"], conv_w, dqkv, S)
    dproj = jnp.concatenate([dqkv_proj, dz, dab], axis=1)
    d_w_in = matmul(sv["h"], dproj, "tn", F32, "gdn_in_dw")
    dh = matmul(dproj, w_in, "nt", F32, "gdn_in_dx")
    return dh, dict(w_in=d_w_in, conv=d_conv, alog=d_alog, dtb=d_dtb, gain=d_gain, w_out=d_w_out)


QB = DSW_SPAN
N_HP = DSW_HG // LANES
PROJ_BLKS = 3 * 3 * N_HP


def _bucket_maps():
    a = np.arange(QB)[:, None]
    j = np.arange(2 * QB)[None, :]
    dist = QB + a - j
    band = (dist >= 0) & (dist <= DSW_SPAN)
    maps = []
    for _, dil in DSW_GROUPS:
        dd = np.maximum(dist, 0) * dil
        max_exact = REL_BUCKETS // 2
        scaled = np.log(np.maximum(dd, 1).astype(np.float32) / np.float32(max_exact)) / np.float32(math.log(REL_MAX_DIST / max_exact))
        large = max_exact + (scaled * np.float32(REL_BUCKETS - max_exact)).astype(np.int32)
        large = np.minimum(large, REL_BUCKETS - 1)
        maps.append(np.where(dd < max_exact, dd, large).astype(np.int32))
    return np.stack(maps), band


def dsw_bias(rel_bias):
    maps, band = _bucket_maps()
    maps = np.where(band[None], maps, -1).astype(np.int32)

    def body(tab_ref, bk_ref, o_ref):
        gh = pl.program_id(0)
        bk = bk_ref[...]
        acc = jnp.full(bk.shape, NEG, F32)
        for b in range(REL_BUCKETS):
            acc = jnp.where(bk == b, tab_ref[b, gh], acc)
        o_ref[...] = acc

    return pl.pallas_call(
        body, name="dsw_bias", grid=(3 * GDN_H,),
        in_specs=[pl.BlockSpec(memory_space=pltpu.SMEM),
                  pl.BlockSpec((None, QB, 2 * QB), lambda gh: (gh // GDN_H, 0, 0))],
        out_specs=pl.BlockSpec((None, QB, 2 * QB), lambda gh: (gh, 0, 0)),
        out_shape=SDS((3 * GDN_H, QB, 2 * QB), F32),
        compiler_params=_cp(1),
    )(rel_bias, jnp.asarray(maps))


def dsw_bias_grad(dbias):
    maps, band = _bucket_maps()
    maps = np.where(band[None], maps, -1).astype(np.int32)

    def body(d_ref, bk_ref, o_ref):
        bk = bk_ref[...]
        d = d_ref[...]
        rows = lax.broadcasted_iota(jnp.int32, (REL_BUCKETS, LANES), 0)
        acc = jnp.zeros((REL_BUCKETS, LANES), F32)
        for b in range(REL_BUCKETS):
            part = jnp.sum(jnp.where(bk == b, d, 0.0), axis=0, keepdims=True)
            val = jnp.sum(part, axis=1, keepdims=True)
            acc = jnp.where(rows == b, val, acc)
        o_ref[...] = acc

    return pl.pallas_call(
        body, name="dsw_bias_grad", grid=(3 * GDN_H,),
        in_specs=[pl.BlockSpec((None, QB, 2 * QB), lambda gh: (gh, 0, 0)),
                  pl.BlockSpec((None, QB, 2 * QB), lambda gh: (gh // GDN_H, 0, 0))],
        out_specs=pl.BlockSpec((None, REL_BUCKETS, LANES), lambda gh: (gh, 0, 0)),
        out_shape=SDS((3 * GDN_H, REL_BUCKETS, LANES), F32),
        compiler_params=_cp(1),
    )(dbias, jnp.asarray(maps))


def _nt(a, b):
    return lax.dot_general(a, b, (((1,), (1,)), ((), ())), preferred_element_type=F32)


def _tn(a, b):
    return lax.dot_general(a, b, (((0,), (0,)), ((), ())), preferred_element_type=F32)


def dsw_group_fwd(qn, kn, proj, bias, gi, S):
    dil = DSW_GROUPS[gi][1]
    sd = S // dil
    nq = sd // QB
    qv = qn.reshape(sd, dil * 3 * DSW_HG)
    kv = kn.reshape(sd, dil * 3 * DSW_HG)
    pv = proj.reshape(sd, dil * 9 * DSW_HG)
    qk_col = lambda hp, r: r * (3 * N_HP) + gi * N_HP + hp
    v_col = lambda hp, r: r * PROJ_BLKS + 2 * 3 * N_HP + gi * N_HP + hp

    def body(q_ref, kp_ref, kc_ref, vp_ref, vc_ref, b_ref, o_ref, l_ref):
        i = pl.program_id(2)
        q = q_ref[...]
        k2 = jnp.concatenate([kp_ref[...], kc_ref[...]], axis=0)
        v2 = jnp.concatenate([vp_ref[...], vc_ref[...]], axis=0).astype(BF16)
        lane_q = lax.broadcasted_iota(jnp.int32, (QB, LANES), 1) < DSW_DH
        lane_k = lax.broadcasted_iota(jnp.int32, (2 * QB, LANES), 1) < DSW_DH
        col = lax.broadcasted_iota(jnp.int32, (QB, 2 * QB), 1)
        first = jnp.logical_and(i == 0, col < QB)
        o_acc = jnp.zeros((QB, LANES), F32)
        lse_b = jnp.zeros((QB, LANES), F32)
        for hh in range(2):
            mq = lane_q if hh == 0 else jnp.logical_not(lane_q)
            mk = lane_k if hh == 0 else jnp.logical_not(lane_k)
            s = _nt(jnp.where(mq, q, 0).astype(BF16), k2) + b_ref[hh]
            s = jnp.where(first, NEG, s)
            mx = jnp.max(s, axis=1, keepdims=True)
            p = jnp.exp(s - mx)
            l = jnp.sum(p, axis=1, keepdims=True)
            oh = jnp.dot(p.astype(BF16), jnp.where(mk, v2, 0).astype(BF16), preferred_element_type=F32) / l
            o_acc = o_acc + oh
            lse_b = jnp.where(mq, mx + jnp.log(l), lse_b)
        o_ref[...] = o_acc
        l_ref[...] = lse_b

    blk = (QB, LANES)
    out_spec = pl.BlockSpec(blk, lambda hp, r, i: (i, r * N_HP + hp))
    o, lse = pl.pallas_call(
        body, name=f"dsw_fwd_g{gi}", grid=(N_HP, dil, nq),
        in_specs=[pl.BlockSpec(blk, lambda hp, r, i: (i, qk_col(hp, r))),
                  pl.BlockSpec(blk, lambda hp, r, i: (jnp.maximum(i - 1, 0), qk_col(hp, r))),
                  pl.BlockSpec(blk, lambda hp, r, i: (i, qk_col(hp, r))),
                  pl.BlockSpec(blk, lambda hp, r, i: (jnp.maximum(i - 1, 0), v_col(hp, r))),
                  pl.BlockSpec(blk, lambda hp, r, i: (i, v_col(hp, r))),
                  pl.BlockSpec((2, QB, 2 * QB), lambda hp, r, i: (gi * N_HP + hp, 0, 0))],
        out_specs=[out_spec, out_spec],
        out_shape=[SDS((sd, dil * DSW_HG), F32)] * 2,
        compiler_params=_cp(3),
    )(qv, kv, kv, pv, pv, bias)
    return o.reshape(S, DSW_HG), lse.reshape(S, DSW_HG)


def dsw_group_bwd(qn, kn, proj, bias, do, o, lse, gi, S):
    dil = DSW_GROUPS[gi][1]
    sd = S // dil
    nq = sd // QB
    qv = qn.reshape(sd, dil * 3 * DSW_HG)
    kv = kn.reshape(sd, dil * 3 * DSW_HG)
    pv = proj.reshape(sd, dil * 9 * DSW_HG)
    dov = do.reshape(sd, dil * DSW_HG)
    ov = o.reshape(sd, dil * DSW_HG)
    lv = lse.reshape(sd, dil * DSW_HG)
    qk_col = lambda hp, r: r * (3 * N_HP) + gi * N_HP + hp
    v_col = lambda hp, r: r * PROJ_BLKS + 2 * 3 * N_HP + gi * N_HP + hp
    o_col = lambda hp, r: r * N_HP + hp
    cur = lambda i: jnp.minimum(i, nq - 1)
    prev = lambda i: jnp.maximum(jnp.minimum(i, nq - 1) - 1, 0)
    done = lambda i: jnp.maximum(i - 1, 0)

    def body(q_ref, kp_ref, kc_ref, vp_ref, vc_ref, b_ref, do_ref, o_ref, l_ref,
             dq_ref, dk_ref, dv_ref, db_ref, dk_scr, dv_scr):
        r, i = pl.program_id(1), pl.program_id(2)

        @pl.when(jnp.logical_and(r == 0, i == 0))
        def _():
            db_ref[...] = jnp.zeros_like(db_ref)

        @pl.when(i == 0)
        def _():
            dk_scr[...] = jnp.zeros_like(dk_scr)
            dv_scr[...] = jnp.zeros_like(dv_scr)

        @pl.when(i < nq)
        def _():
            q = q_ref[...]
            k2 = jnp.concatenate([kp_ref[...], kc_ref[...]], axis=0)
            v2 = jnp.concatenate([vp_ref[...], vc_ref[...]], axis=0).astype(BF16)
            dout = do_ref[...].astype(F32)
            prod = dout * o_ref[...].astype(F32)
            lse_b = l_ref[...]
            lane_q = lax.broadcasted_iota(jnp.int32, (QB, LANES), 1) < DSW_DH
            col = lax.broadcasted_iota(jnp.int32, (QB, 2 * QB), 1)
            first = jnp.logical_and(i == 0, col < QB)
            dq = jnp.zeros((QB, LANES), F32)
            dk2 = jnp.zeros((2 * QB, LANES), F32)
            dv2 = jnp.zeros((2 * QB, LANES), F32)
            for hh in range(2):
                mq = lane_q if hh == 0 else jnp.logical_not(lane_q)
                qm = jnp.where(mq, q, 0).astype(BF16)
                dom = jnp.where(mq, dout, 0.0).astype(BF16)
                s = _nt(qm, k2) + b_ref[hh]
                s = jnp.where(first, NEG, s)
                lse_h = jnp.max(jnp.where(mq, lse_b, NEG), axis=1, keepdims=True)
                p = jnp.exp(s - lse_h)
                delta = jnp.sum(jnp.where(mq, prod, 0.0), axis=1, keepdims=True)
                dp = _nt(dom, v2)
                ds = p * (dp - delta)
                dsb = ds.astype(BF16)
                dq = dq + jnp.where(mq, jnp.dot(dsb, k2, preferred_element_type=F32), 0.0)
                dk2 = dk2 + _tn(dsb, qm)
                dv2 = dv2 + _tn(p.astype(BF16), dom)
                db_ref[hh] += ds
            dq_ref[...] = dq
            dk_ref[...] = dk_scr[...] + dk2[:QB]
            dv_ref[...] = (dv_scr[...] + dv2[:QB]).astype(dv_ref.dtype)
            dk_scr[...] = dk2[QB:]
            dv_scr[...] = dv2[QB:]

        @pl.when(i == nq)
        def _():
            dk_ref[...] = dk_scr[...]
            dv_ref[...] = dv_scr[...].astype(dv_ref.dtype)

    blk = (QB, LANES)
    dq, dk, dv, dbias = pl.pallas_call(
        body, name=f"dsw_bwd_g{gi}", grid=(N_HP, dil, nq + 1),
        in_specs=[pl.BlockSpec(blk, lambda hp, r, i: (cur(i), qk_col(hp, r))),
                  pl.BlockSpec(blk, lambda hp, r, i: (prev(i), qk_col(hp, r))),
                  pl.BlockSpec(blk, lambda hp, r, i: (cur(i), qk_col(hp, r))),
                  pl.BlockSpec(blk, lambda hp, r, i: (prev(i), v_col(hp, r))),
                  pl.BlockSpec(blk, lambda hp, r, i: (cur(i), v_col(hp, r))),
                  pl.BlockSpec((2, QB, 2 * QB), lambda hp, r, i: (gi * N_HP + hp, 0, 0)),
                  pl.BlockSpec(blk, lambda hp, r, i: (cur(i), o_col(hp, r))),
                  pl.BlockSpec(blk, lambda hp, r, i: (cur(i), o_col(hp, r))),
                  pl.BlockSpec(blk, lambda hp, r, i: (cur(i), o_col(hp, r)))],
        out_specs=[pl.BlockSpec(blk, lambda hp, r, i: (cur(i), o_col(hp, r))),
                   pl.BlockSpec(blk, lambda hp, r, i: (done(i), o_col(hp, r))),
                   pl.BlockSpec(blk, lambda hp, r, i: (done(i), o_col(hp, r))),
                   pl.BlockSpec((2, QB, 2 * QB), lambda hp, r, i: (hp, 0, 0))],
        out_shape=[SDS((sd, dil * DSW_HG), F32), SDS((sd, dil * DSW_HG), F32), SDS((sd, dil * DSW_HG), BF16),
                   SDS((GDN_H, QB, 2 * QB), F32)],
        scratch_shapes=[pltpu.VMEM(blk, F32), pltpu.VMEM(blk, F32)],
        compiler_params=_cp(3),
    )(qv, kv, kv, pv, pv, bias, dov, ov, lv)
    return dq.reshape(S, DSW_HG), dk.reshape(S, DSW_HG), dv.reshape(S, DSW_HG), dbias


def dsw_forward(h, w_in, q_gain2, k_gain2, rel_bias, w_out):
    S = h.shape[0]
    nt = S // RT
    nb = 3 * N_HP
    proj = matmul(h, w_in, "nn", F32, "dsw_in")
    width = nb * LANES
    lanes12 = [LANES] * nb
    (qn,) = rowwise(f_qnorm, [Row(proj, (RT, width), lambda i: (i, 0), splits=lanes12)], [q_gain2],
                    [Out((S, width), BF16, (RT, width), lambda i: (i, 0), splits=lanes12)], (nt,), "dsw_qnorm")
    (kn,) = rowwise(f_qknorm, [Row(proj, (RT, width), lambda i: (i, 1), splits=lanes12)], [k_gain2],
                    [Out((S, width), BF16, (RT, width), lambda i: (i, 0), splits=lanes12)], (nt,), "dsw_knorm")
    bias = dsw_bias(rel_bias)
    os_, ls_ = [], []
    for gi in range(3):
        o, l = dsw_group_fwd(qn, kn, proj, bias, gi, S)
        os_.append(o)
        ls_.append(l)
    full = lambda a: Row(a, (RT, DSW_HG), lambda i: (i, 0))
    o, lse = rowwise(f_combine, [full(a) for a in os_ + ls_], [],
                     [Out((S, DSW_HG), BF16, (RT, DSW_HG), lambda i: (i, 0)), Out((S, DSW_HG), F32, (RT, DSW_HG), lambda i: (i, 0))],
                     (nt,), "dsw_combine")
    y = matmul(o, w_out, "nn", F32, "dsw_out")
    return y, dict(h=h, proj=proj, qn=qn, kn=kn, bias=bias, o=o, lse=lse)


def dsw_backward(dy, sv, w_in, q_gain2, k_gain2, w_out):
    S = dy.shape[0]
    nt = S // RT
    nb = 3 * N_HP
    do = matmul(dy, w_out, "nt", BF16, "dsw_out_dx")
    d_w_out = matmul(sv["o"], dy, "tn", F32, "dsw_out_dw")
    pieces_q, pieces_k, pieces_v, dbs = [], [], [], []
    d_qg = jnp.zeros((1, LANES), F32)
    d_kg = jnp.zeros((1, LANES), F32)
    for gi in range(3):
        dq, dk, dv, db = dsw_group_bwd(sv["qn"], sv["kn"], sv["proj"], sv["bias"], do, sv["o"], sv["lse"], gi, S)
        dbs.append(db)
        pieces_v.append(dv)
        for which, dd in ((0, dq), (1, dk)):
            lanes4 = [LANES] * N_HP
            row = Row(sv["proj"], (RT, DSW_HG), lambda i, _o=which * 3 + gi: (i, _o), splits=lanes4,
                      gdtype=BF16, gshape=(S, DSW_HG), gimap=lambda i: (i, 0))
            fn, gain = (f_qnorm, q_gain2) if which == 0 else (f_qknorm, k_gain2)
            (dx,), (dg,) = rowwise_bwd(fn, [row], [gain], [Row(dd, (RT, DSW_HG), lambda i: (i, 0), splits=lanes4)],
                                       (nt,), f"dsw_norm_bwd_{which}{gi}")
            if which == 0:
                pieces_q.append(dx)
                d_qg = d_qg + dg
            else:
                pieces_k.append(dx)
                d_kg = d_kg + dg
    dproj = jnp.concatenate(pieces_q + pieces_k + pieces_v, axis=1)
    d_w_in = matmul(sv["h"], dproj, "tn", F32, "dsw_in_dw")
    dh = matmul(dproj, w_in, "nt", F32, "dsw_in_dx")
    d_rel = dsw_bias_grad(jnp.concatenate(dbs, axis=0))
    return dh, dict(w_in=d_w_in, q_gain2=d_qg, k_gain2=d_kg, rel=d_rel, w_out=d_w_out)


N_LB = DSW_HG // LANES
HALF = DSW_DH // 2


def _lanes(j):
    return slice(LANES * j, LANES * (j + 1))


def _deinterleave(stage, out_ref, dil, rows, dtype):
    for r in range(dil):
        for j in range(N_LB):
            out_ref[r, :, _lanes(j)] = stage[j, pl.ds(r, rows, stride=dil), :].astype(dtype)


def _interleave(in_ref, stage, dil, rows):
    for r in range(dil):
        for j in range(N_LB):
            stage[j, pl.ds(r, rows, stride=dil), :] = in_ref[r, :, _lanes(j)].astype(F32)


def dsw_prep(proj, q_gain2, k_gain2, gi, S):
    dil = DSW_GROUPS[gi][1]
    nt, rows = S // RT, RT // dil

    def body(q_ref, k_ref, v_ref, qg_ref, kg_ref, qo_ref, ko_ref, vo_ref, stage):
        for src, gain_ref, scale, dst in ((q_ref, qg_ref, DSW_DH ** -0.5, qo_ref), (k_ref, kg_ref, 1.0, ko_ref), (v_ref, None, None, vo_ref)):
            for j in range(N_LB):
                val = src[:, _lanes(j)]
                stage[j] = val if gain_ref is None else _qknorm1(val, gain_ref[...], scale)
            _deinterleave(stage, dst, dil, rows, BF16)

    col = lambda which: pl.BlockSpec((RT, DSW_HG), lambda i, _c=which * 3 + gi: (i, _c))
    gspec = pl.BlockSpec((1, LANES), lambda i: (0, 0))
    ospec = pl.BlockSpec((dil, rows, DSW_HG), lambda i: (0, i, 0))
    return pl.pallas_call(
        body, name=f"dsw_prep_g{gi}", grid=(nt,),
        in_specs=[col(0), col(1), col(2), gspec, gspec], out_specs=[ospec] * 3,
        out_shape=[SDS((dil, S // dil, DSW_HG), BF16)] * 3,
        scratch_shapes=[pltpu.VMEM((N_LB, RT, LANES), F32)], compiler_params=_cp(1),
    )(proj, proj, proj, q_gain2, k_gain2)


def dsw_prep_bwd(proj, q_gain2, k_gain2, dqd, dkd, dvd, gi, S):
    dil = DSW_GROUPS[gi][1]
    nt, rows = S // RT, RT // dil

    def body(q_ref, k_ref, qg_ref, kg_ref, dq_ref, dk_ref, dv_ref, oq_ref, ok_ref, ov_ref, dqg_ref, dkg_ref, stage):
        i = pl.program_id(0)

        @pl.when(i == 0)
        def _():
            dqg_ref[...] = jnp.zeros_like(dqg_ref)
            dkg_ref[...] = jnp.zeros_like(dkg_ref)

        for src, gain_ref, scale, cot_ref, dst, dg_ref in ((q_ref, qg_ref, DSW_DH ** -0.5, dq_ref, oq_ref, dqg_ref),
                                                          (k_ref, kg_ref, 1.0, dk_ref, ok_ref, dkg_ref)):
            _interleave(cot_ref, stage, dil, rows)
            for j in range(N_LB):
                _, vjp = jax.vjp(lambda x, g, _s=scale: _qknorm1(x, g, _s), src[:, _lanes(j)], gain_ref[...])
                dx, dg = vjp(stage[j])
                dst[:, _lanes(j)] = dx.astype(dst.dtype)
                dg_ref[...] += dg
        _interleave(dv_ref, stage, dil, rows)
        for j in range(N_LB):
            ov_ref[:, _lanes(j)] = stage[j].astype(ov_ref.dtype)

    col = lambda which: pl.BlockSpec((RT, DSW_HG), lambda i, _c=which * 3 + gi: (i, _c))
    gspec = pl.BlockSpec((1, LANES), lambda i: (0, 0))
    dspec = pl.BlockSpec((dil, rows, DSW_HG), lambda i: (0, i, 0))
    nspec = pl.BlockSpec((RT, DSW_HG), lambda i: (i, 0))
    return pl.pallas_call(
        body, name=f"dsw_prep_bwd_g{gi}", grid=(nt,),
        in_specs=[col(0), col(1), gspec, gspec, dspec, dspec, dspec], out_specs=[nspec] * 3 + [gspec] * 2,
        out_shape=[SDS((S, DSW_HG), BF16)] * 3 + [SDS((1, LANES), F32)] * 2,
        scratch_shapes=[pltpu.VMEM((N_LB, RT, LANES), F32)], compiler_params=_cp(1),
    )(proj, proj, q_gain2, k_gain2, dqd, dkd, dvd)


def _head_masks(rows):
    lane = lax.broadcasted_iota(jnp.int32, (rows, LANES), 1)
    return lane < DSW_DH, (lane % DSW_DH) < HALF


def dsw_attn_fwd(qd, kd, vd, bias, gi, S):
    dil = DSW_GROUPS[gi][1]
    sd = S // dil
    nq = sd // QB

    def body(q_ref, k_ref, v_ref, b_ref, o_ref, l_ref, kp_scr, vp_scr):
        i = pl.program_id(1)

        @pl.when(i == 0)
        def _():
            kp_scr[...] = jnp.zeros_like(kp_scr)
            vp_scr[...] = jnp.zeros_like(vp_scr)

        lo_q, _ = _head_masks(QB)
        lo_k, _ = _head_masks(2 * QB)
        col = lax.broadcasted_iota(jnp.int32, (QB, 2 * QB), 1)
        first = jnp.logical_and(i == 0, col < QB)
        for hp in range(N_HP):
            q = q_ref[:, _lanes(hp)]
            k2 = jnp.concatenate([kp_scr[:, _lanes(hp)], k_ref[:, _lanes(hp)]], axis=0)
            v2 = jnp.concatenate([vp_scr[:, _lanes(hp)], v_ref[:, _lanes(hp)]], axis=0)
            o_acc = jnp.zeros((QB, LANES), F32)
            lse_b = jnp.zeros((QB, LANES), F32)
            for hh in range(2):
                mq = lo_q if hh == 0 else jnp.logical_not(lo_q)
                mk = lo_k if hh == 0 else jnp.logical_not(lo_k)
                s = _nt(jnp.where(mq, q, 0).astype(BF16), k2) + b_ref[2 * hp + hh]
                s = jnp.where(first, NEG, s)
                mx = jnp.max(s, axis=1, keepdims=True)
                p = jnp.exp(s - mx)
                l = jnp.sum(p, axis=1, keepdims=True)
                oh = jnp.dot(p.astype(BF16), jnp.where(mk, v2, 0).astype(BF16), preferred_element_type=F32) / l
                o_acc = o_acc + oh
                lse_b = jnp.where(mq, mx + jnp.log(l), lse_b)
            o_ref[:, _lanes(hp)] = o_acc
            l_ref[:, _lanes(hp)] = lse_b
        kp_scr[...] = k_ref[...]
        vp_scr[...] = v_ref[...]

    blk = pl.BlockSpec((None, QB, DSW_HG), lambda r, i: (r, i, 0))
    return pl.pallas_call(
        body, name=f"dsw_attn_g{gi}", grid=(dil, nq),
        in_specs=[blk, blk, blk, pl.BlockSpec((GDN_H, QB, 2 * QB), lambda r, i: (gi, 0, 0))],
        out_specs=[blk, blk], out_shape=[SDS((dil, sd, DSW_HG), F32)] * 2,
        scratch_shapes=[pltpu.VMEM((QB, DSW_HG), BF16)] * 2, compiler_params=_cp(2),
    )(qd, kd, vd, bias)


def dsw_attn_bwd(qd, kd, vd, bias, dod, statd, gi, S):
    dil = DSW_GROUPS[gi][1]
    sd = S // dil
    nq = sd // QB
    cur = lambda i: jnp.minimum(i, nq - 1)
    done = lambda i: jnp.maximum(i - 1, 0)

    def body(q_ref, k_ref, v_ref, b_ref, do_ref, st_ref, dq_ref, dk_ref, dv_ref, db_ref, kp_scr, vp_scr, dk_scr, dv_scr):
        r, i = pl.program_id(0), pl.program_id(1)

        @pl.when(jnp.logical_and(r == 0, i == 0))
        def _():
            db_ref[...] = jnp.zeros_like(db_ref)

        @pl.when(i == 0)
        def _():
            for scr in (kp_scr, vp_scr, dk_scr, dv_scr):
                scr[...] = jnp.zeros_like(scr)

        @pl.when(i < nq)
        def _():
            lo_q, first_half = _head_masks(QB)
            col = lax.broadcasted_iota(jnp.int32, (QB, 2 * QB), 1)
            first = jnp.logical_and(i == 0, col < QB)
            for hp in range(N_HP):
                q = q_ref[:, _lanes(hp)]
                k2 = jnp.concatenate([kp_scr[:, _lanes(hp)], k_ref[:, _lanes(hp)]], axis=0)
                v2 = jnp.concatenate([vp_scr[:, _lanes(hp)], v_ref[:, _lanes(hp)]], axis=0)
                dout = do_ref[:, _lanes(hp)]
                stat = st_ref[:, _lanes(hp)]
                dq = jnp.zeros((QB, LANES), F32)
                dk2 = jnp.zeros((2 * QB, LANES), F32)
                dv2 = jnp.zeros((2 * QB, LANES), F32)
                for hh in range(2):
                    mq = lo_q if hh == 0 else jnp.logical_not(lo_q)
                    qm = jnp.where(mq, q, 0).astype(BF16)
                    dom = jnp.where(mq, dout, 0).astype(BF16)
                    s = _nt(qm, k2) + b_ref[2 * hp + hh]
                    s = jnp.where(first, NEG, s)
                    lse_h = jnp.max(jnp.where(jnp.logical_and(mq, first_half), stat, NEG), axis=1, keepdims=True)
                    delta = jnp.max(jnp.where(jnp.logical_and(mq, jnp.logical_not(first_half)), stat, NEG), axis=1, keepdims=True)
                    p = jnp.exp(s - lse_h)
                    ds = p * (_nt(dom, v2) - delta)
                    dsb = ds.astype(BF16)
                    dq = dq + jnp.where(mq, jnp.dot(dsb, k2, preferred_element_type=F32), 0.0)
                    dk2 = dk2 + _tn(dsb, qm)
                    dv2 = dv2 + _tn(p.astype(BF16), dom)
                    db_ref[2 * hp + hh] += ds
                dq_ref[:, _lanes(hp)] = dq
                dk_ref[:, _lanes(hp)] = dk_scr[:, _lanes(hp)] + dk2[:QB]
                dv_ref[:, _lanes(hp)] = (dv_scr[:, _lanes(hp)] + dv2[:QB]).astype(dv_ref.dtype)
                dk_scr[:, _lanes(hp)] = dk2[QB:]
                dv_scr[:, _lanes(hp)] = dv2[QB:]
            kp_scr[...] = k_ref[...]
            vp_scr[...] = v_ref[...]

        @pl.when(i == nq)
        def _():
            dk_ref[...] = dk_scr[...]
            dv_ref[...] = dv_scr[...].astype(dv_ref.dtype)

    blk = pl.BlockSpec((None, QB, DSW_HG), lambda r, i: (r, cur(i), 0))
    oblk = pl.BlockSpec((None, QB, DSW_HG), lambda r, i: (r, done(i), 0))
    return pl.pallas_call(
        body, name=f"dsw_attn_bwd_g{gi}", grid=(dil, nq + 1),
        in_specs=[blk, blk, blk, pl.BlockSpec((GDN_H, QB, 2 * QB), lambda r, i: (gi, 0, 0)), blk, blk],
        out_specs=[blk, oblk, oblk, pl.BlockSpec((GDN_H, QB, 2 * QB), lambda r, i: (0, 0, 0))],
        out_shape=[SDS((dil, sd, DSW_HG), F32), SDS((dil, sd, DSW_HG), F32), SDS((dil, sd, DSW_HG), BF16),
                   SDS((GDN_H, QB, 2 * QB), F32)],
        scratch_shapes=[pltpu.VMEM((QB, DSW_HG), BF16)] * 2 + [pltpu.VMEM((QB, DSW_HG), F32)] * 2,
        compiler_params=_cp(2),
    )(qd, kd, vd, bias, dod, statd)


def dsw_combine(ods, lseds, S):
    nt = S // RT
    dils = [d for _, d in DSW_GROUPS]

    def body(*refs):
        ins, (o_ref, l_ref), stages = refs[:6], refs[6:8], refs[8:]
        for g in range(3):
            _interleave(ins[g], stages[g], dils[g], RT // dils[g])
            _interleave(ins[3 + g], stages[3 + g], dils[g], RT // dils[g])
        for j in range(N_LB):
            o, lse = f_combine(None, *[st[j] for st in stages])
            o_ref[:, _lanes(j)] = o.astype(o_ref.dtype)
            l_ref[:, _lanes(j)] = lse

    dspec = lambda d: pl.BlockSpec((d, RT // d, DSW_HG), lambda i: (0, i, 0))
    nspec = pl.BlockSpec((RT, DSW_HG), lambda i: (i, 0))
    return pl.pallas_call(
        body, name="dsw_combine", grid=(nt,),
        in_specs=[dspec(d) for d in dils] * 2, out_specs=[nspec, nspec],
        out_shape=[SDS((S, DSW_HG), BF16), SDS((S, DSW_HG), F32)],
        scratch_shapes=[pltpu.VMEM((N_LB, RT, LANES), F32)] * 6, compiler_params=_cp(1),
    )(*ods, *lseds)


def dsw_bwd_prep(do, o, lse, S):
    nt = S // RT
    dils = [d for _, d in DSW_GROUPS]

    def body(do_ref, o_ref, l_ref, *rest):
        outs, (st_do, st_stat) = rest[:6], rest[6:]
        lo, first_half = _head_masks(RT)
        for j in range(N_LB):
            dout = do_ref[:, _lanes(j)]
            prod = dout * o_ref[:, _lanes(j)].astype(F32)
            s_all = jnp.sum(prod, axis=1, keepdims=True)
            s_lo = jnp.sum(jnp.where(lo, prod, 0.0), axis=1, keepdims=True)
            delta = jnp.where(lo, s_lo, s_all - s_lo)
            st_do[j] = dout
            st_stat[j] = jnp.where(first_half, l_ref[:, _lanes(j)], delta)
        for g in range(3):
            _deinterleave(st_do, outs[g], dils[g], RT // dils[g], BF16)
            _deinterleave(st_stat, outs[3 + g], dils[g], RT // dils[g], F32)

    nspec = pl.BlockSpec((RT, DSW_HG), lambda i: (i, 0))
    dspec = lambda d: pl.BlockSpec((d, RT // d, DSW_HG), lambda i: (0, i, 0))
    res = pl.pallas_call(
        body, name="dsw_bwd_prep", grid=(nt,),
        in_specs=[nspec] * 3, out_specs=[dspec(d) for d in dils] * 2,
        out_shape=[SDS((d, S // d, DSW_HG), BF16) for d in dils] + [SDS((d, S // d, DSW_HG), F32) for d in dils],
        scratch_shapes=[pltpu.VMEM((N_LB, RT, LANES), F32)] * 2, compiler_params=_cp(1),
    )(do, o, lse)
    return res[:3], res[3:]


def dsw_forward(h, w_in, q_gain2, k_gain2, rel_bias, w_out):
    S = h.shape[0]
    proj = matmul(h, w_in, "nn", F32, "dsw_in", col_shards=N_SHARD)
    bias = dsw_bias(rel_bias)
    qkv, ods, lseds = [], [], []
    for gi in range(3):
        qd, kd, vd = dsw_prep(proj, q_gain2, k_gain2, gi, S)
        od, ld = dsw_attn_fwd(qd, kd, vd, bias, gi, S)
        qkv.append((qd, kd, vd))
        ods.append(od)
        lseds.append(ld)
    o, lse = dsw_combine(ods, lseds, S)
    y = matmul(o, w_out, "nn", F32, "dsw_out", col_shards=N_SHARD)
    return y, dict(h=h, proj=proj, qkv=qkv, bias=bias, o=o, lse=lse)


def dsw_backward(dy, sv, w_in, q_gain2, k_gain2, w_out):
    S = dy.shape[0]
    do = matmul(dy, w_out, "nt", F32, "dsw_out_dx", col_shards=N_SHARD)
    d_w_out = matmul(sv["o"], dy, "tn", F32, "dsw_out_dw", col_shards=N_SHARD)
    dods, statds = dsw_bwd_prep(do, sv["o"], sv["lse"], S)
    pieces_q, pieces_k, pieces_v, dbs = [], [], [], []
    d_qg = jnp.zeros((1, LANES), F32)
    d_kg = jnp.zeros((1, LANES), F32)
    for gi in range(3):
        qd, kd, vd = sv["qkv"][gi]
        dqd, dkd, dvd, db = dsw_attn_bwd(qd, kd, vd, sv["bias"], dods[gi], statds[gi], gi, S)
        dq, dk, dv, dqg, dkg = dsw_prep_bwd(sv["proj"], q_gain2, k_gain2, dqd, dkd, dvd, gi, S)
        dbs.append(db)
        pieces_q.append(dq)
        pieces_k.append(dk)
        pieces_v.append(dv)
        d_qg = d_qg + dqg
        d_kg = d_kg + dkg
    dproj = jnp.concatenate(pieces_q + pieces_k + pieces_v, axis=1)
    d_w_in = matmul(sv["h"], dproj, "tn", F32, "dsw_in_dw", col_shards=N_SHARD)
    dh = matmul(dproj, w_in, "nt", F32, "dsw_in_dx", col_shards=N_SHARD)
    d_rel = dsw_bias_grad(jnp.concatenate(dbs, axis=0))
    return dh, dict(w_in=d_w_in, q_gain2=d_qg, k_gain2=d_kg, rel=d_rel, w_out=d_w_out)


FT = 128


def ffn_forward(h, w_in, w_out, tag):
    S = h.shape[0]
    gu = matmul(h, w_in, "nn", BF16, f"ffn_in_{tag}", col_shards=N_SHARD)
    gu_row = Row(gu, (FT, 2 * FFN), lambda i: (i, 0), splits=[FFN, FFN], gdtype=BF16)
    (a,) = rowwise(f_swiglu, [gu_row], [], [Out((S, FFN), BF16, (FT, FFN), lambda i: (i, 0))], (S // FT,), f"ffn_act_{tag}")
    f = matmul(a, w_out, "nn", F32, f"ffn_out_{tag}")
    return f, dict(h=h, gu_row=gu_row, a=a)


def ffn_backward(df, sv, w_in, w_out, tag):
    S = df.shape[0]
    da = matmul(df, w_out, "nt", BF16, f"ffn_out_dx_{tag}")
    d_w_out = matmul(sv["a"], df, "tn", F32, f"ffn_out_dw_{tag}")
    (dgu,), _ = rowwise_bwd(f_swiglu, [sv["gu_row"]], [], [Row(da, (FT, FFN), lambda i: (i, 0))], (S // FT,), f"ffn_act_bwd_{tag}")
    d_w_in = matmul(sv["h"], dgu, "tn", F32, f"ffn_in_dw_{tag}", col_shards=N_SHARD)
    dh = matmul(dgu, w_in, "nt", F32, f"ffn_in_dx_{tag}", col_shards=N_SHARD)
    return dh, d_w_in, d_w_out


def f_norm_only(ids, x, gain, sc, sh):
    return (_normmod(x, gain, sc, sh),)


def _wide(a, **kw):
    return Row(a, (RT, D), lambda i: (i, 0), **kw)


def _wide_out(S, dtype):
    return Out((S, D), dtype, (RT, D), lambda i: (i, 0))


def adamw(w, g, m, v, name):
    shape = w.shape
    C = shape[-1]
    R = int(np.prod(shape[:-1]))
    w2, g2, m2, v2 = (a.reshape(R, C) for a in (w, g, m, v))
    br = R
    if R > 256:
        br = max(b for b in range(8, 257, 8) if R % b == 0)
    c1 = 1.0 / (1.0 - ADAM_B1 ** ADAM_STEP)
    c2 = 1.0 / (1.0 - ADAM_B2 ** ADAM_STEP)

    def body(w_ref, g_ref, m_ref, v_ref, d_ref, nm_ref, nv_ref):
        gg = g_ref[...]
        mm_ = ADAM_B1 * m_ref[...] + (1.0 - ADAM_B1) * gg
        vv = ADAM_B2 * v_ref[...] + (1.0 - ADAM_B2) * (gg * gg)
        d_ref[...] = -ADAM_LR * ((mm_ * c1) / (jnp.sqrt(vv * c2) + ADAM_EPS) + ADAM_WD * w_ref[...])
        nm_ref[...] = mm_
        nv_ref[...] = vv

    spec = pl.BlockSpec((br, C), lambda i: (i, 0))
    d, nm, nv = pl.pallas_call(
        body, name=name, grid=(R // br,), in_specs=[spec] * 4, out_specs=[spec] * 3,
        out_shape=[SDS((R, C), F32)] * 3, compiler_params=_cp(1),
    )(w2, g2, m2, v2)
    return d.reshape(shape), nm.reshape(shape), nv.reshape(shape)


def _place():
    x, y, c = lax.axis_index("x"), lax.axis_index("y"), lax.axis_index("c")
    chips = [(1 - x, y), (x, 1 - y), (1 - x, 1 - y)]
    return x, y, c, chips


def all_gather_small(blk, name):
    m_per, n = blk.shape

    def body(x_ref, out_ref, send_sems, recv_sems, local_sem):
        x, y, c, chips = _place()
        me, sibling = (x, y, c), (x, y, 1 - c)

        def rows(px, py, pc):
            return out_ref.at[pl.ds((4 * px + 2 * py + pc) * m_per, m_per), :]

        def copy(k, block, to, src=None):
            return pltpu.make_async_remote_copy(
                src_ref=rows(*block) if src is None else src, dst_ref=rows(*block),
                send_sem=send_sems.at[k], recv_sem=recv_sems.at[k], device_id=to, device_id_type=MESH)

        mine = pltpu.make_async_copy(x_ref, rows(*me), local_sem)
        mine.start()
        first = [copy(0, me, sibling, src=x_ref)]
        first += [copy(1 + j, me, (*chip, c), src=x_ref) for j, chip in enumerate(chips)]
        for cp in first:
            cp.start()
        passed = [copy(4 + j, (*chip, c), sibling) for j, chip in enumerate(chips)]
        for j, chip in enumerate(chips):
            copy(1 + j, (*chip, c), me).wait_recv()
            passed[j].start()
        copy(0, sibling, me).wait_recv()
        for j, chip in enumerate(chips):
            copy(4 + j, (*chip, 1 - c), me).wait_recv()
        for cp in first + passed:
            cp.wait_send()
        mine.wait()

    return pl.pallas_call(
        body, name=name, out_shape=SDS((N_DEV * m_per, n), blk.dtype),
        in_specs=[pl.BlockSpec(memory_space=pltpu.VMEM)], out_specs=pl.BlockSpec(memory_space=pltpu.VMEM),
        scratch_shapes=[pltpu.SemaphoreType.DMA((7,)), pltpu.SemaphoreType.DMA((7,)), pltpu.SemaphoreType.DMA],
    )(blk)


def _half(cc, rh):
    return pl.ds(pl.multiple_of(cc * rh, 16), rh)


def all_gather_shards(ws):
    n = len(ws)

    def body(*refs):
        w_refs, out_refs = refs[:n], refs[n:2 * n]
        send_sems, recv_sems, local_sems = refs[2 * n:]
        x, y, c, chips = _place()
        sibling = (x, y, 1 - c)
        s_me = 2 * x + y

        def copy(k, src, dst, to):
            return pltpu.make_async_remote_copy(src_ref=src, dst_ref=dst, send_sem=send_sems.at[k], recv_sem=recv_sems.at[k],
                                                device_id=to, device_id_type=MESH)

        local, sends, passed = [], [], []
        for k in range(n):
            rh = ws[k].shape[0] // 2
            cp = pltpu.make_async_copy(w_refs[k], out_refs[k].at[s_me], local_sems.at[k])
            cp.start()
            local.append(cp)
            for j, chip in enumerate(chips):
                sd = copy(6 * k + j, w_refs[k].at[_half(c, rh)], out_refs[k].at[s_me, _half(c, rh)], (*chip, c))
                sd.start()
                sends.append(sd)
        for k in range(n):
            rh = ws[k].shape[0] // 2
            for j, (px, py) in enumerate(chips):
                got = out_refs[k].at[2 * px + py, _half(c, rh)]
                copy(6 * k + j, got, got, (px, py, c)).wait_recv()
                fw = copy(6 * k + 3 + j, got, got, sibling)
                fw.start()
                passed.append(fw)
        for k in range(n):
            rh = ws[k].shape[0] // 2
            for j, (px, py) in enumerate(chips):
                got = out_refs[k].at[2 * px + py, _half(1 - c, rh)]
                copy(6 * k + 3 + j, got, got, sibling).wait_recv()
        for cp in sends + passed:
            cp.wait_send()
        for cp in local:
            cp.wait()

    return pl.pallas_call(
        body, name="weights_all_gather", out_shape=[SDS((N_SHARD,) + w.shape, w.dtype) for w in ws],
        in_specs=[ANY] * n, out_specs=[ANY] * n,
        scratch_shapes=[pltpu.SemaphoreType.DMA((6 * n,)), pltpu.SemaphoreType.DMA((6 * n,)), pltpu.SemaphoreType.DMA((n,))],
    )(*ws)


def sibling_exchange(sends, name):
    n = len(sends)

    def body(*refs):
        s_refs, o_refs, send_sems, recv_sems = refs[:n], refs[n:2 * n], refs[2 * n], refs[2 * n + 1]
        x, y, c, _ = _place()
        cps = [pltpu.make_async_remote_copy(src_ref=s_refs[k], dst_ref=o_refs[k], send_sem=send_sems.at[k], recv_sem=recv_sems.at[k],
                                            device_id=(x, y, 1 - c), device_id_type=MESH) for k in range(n)]
        for cp in cps:
            cp.start()
        for cp in cps:
            cp.wait()

    return pl.pallas_call(
        body, name=name, out_shape=[SDS(s.shape, s.dtype) for s in sends], in_specs=[ANY] * n, out_specs=[ANY] * n,
        scratch_shapes=[pltpu.SemaphoreType.DMA((n,)), pltpu.SemaphoreType.DMA((n,))],
    )(*sends)


def scatter_to_chips(parts):
    n = len(parts)

    def body(*refs):
        p_refs, o_refs, send_sems, recv_sems = refs[:n], refs[n:2 * n], refs[2 * n], refs[2 * n + 1]
        x, y, c, chips = _place()
        cps = []
        for k in range(n):
            for j, (px, py) in enumerate(chips):
                cp = pltpu.make_async_remote_copy(src_ref=p_refs[k].at[2 * px + py], dst_ref=o_refs[k].at[j],
                                                  send_sem=send_sems.at[3 * k + j], recv_sem=recv_sems.at[3 * k + j],
                                                  device_id=(px, py, c), device_id_type=MESH)
                cp.start()
                cps.append(cp)
        for cp in cps:
            cp.wait()

    return pl.pallas_call(
        body, name="grads_scatter", out_shape=[SDS((3,) + p.shape[1:], p.dtype) for p in parts], in_specs=[ANY] * n, out_specs=[ANY] * n,
        scratch_shapes=[pltpu.SemaphoreType.DMA((3 * n,)), pltpu.SemaphoreType.DMA((3 * n,))],
    )(*parts)


def merge_halves(halves):
    n = len(halves)

    def body(*refs):
        h_refs, o_refs = refs[:n], refs[n:2 * n]
        send_sems, recv_sems, local_sems = refs[2 * n:]
        x, y, c, _ = _place()
        local, cps = [], []
        for k in range(n):
            rh = halves[k].shape[0]
            lc = pltpu.make_async_copy(h_refs[k], o_refs[k].at[_half(c, rh)], local_sems.at[k])
            lc.start()
            local.append(lc)
            cp = pltpu.make_async_remote_copy(src_ref=h_refs[k], dst_ref=o_refs[k].at[_half(c, rh)], send_sem=send_sems.at[k],
                                              recv_sem=recv_sems.at[k], device_id=(x, y, 1 - c), device_id_type=MESH)
            cp.start()
            cps.append(cp)
        for k in range(n):
            rh = halves[k].shape[0]
            got = o_refs[k].at[_half(1 - c, rh)]
            pltpu.make_async_remote_copy(src_ref=got, dst_ref=got, send_sem=send_sems.at[k], recv_sem=recv_sems.at[k],
                                         device_id=(x, y, 1 - c), device_id_type=MESH).wait_recv()
        for cp in cps:
            cp.wait_send()
        for lc in local:
            lc.wait()

    return pl.pallas_call(
        body, name="grads_merge_halves", out_shape=[SDS((2 * h.shape[0], h.shape[1]), h.dtype) for h in halves],
        in_specs=[ANY] * n, out_specs=[ANY] * n,
        scratch_shapes=[pltpu.SemaphoreType.DMA((n,)), pltpu.SemaphoreType.DMA((n,)), pltpu.SemaphoreType.DMA((n,))],
    )(*halves)


def add_rows(arrs, out_dtype, name, rt=256):
    Rr, W = arrs[0].shape

    def fn(ids, *vals):
        acc = vals[0]
        for v in vals[1:]:
            acc = acc + v
        return (acc,)

    t = rt if Rr % rt == 0 else max(b for b in range(16, rt + 1, 16) if Rr % b == 0)
    (out,) = rowwise(fn, [Row(a, (t, W), lambda i: (i, 0)) for a in arrs], [],
                     [Out((Rr, W), out_dtype, (t, W), lambda i: (i, 0))], (Rr // t,), name)
    return out


PACK = (("gdn_w_in", 2), ("gdn_w_out", 1), ("w_ffn_in", 2), ("w_ffn_out", 1), ("dsw_w_in", 2), ("dsw_w_out", 2))
PACK_ALIGN = 32


def _pack_rows(sizes):
    total = sum(sizes)
    rows = -(-total // D)
    return -(-rows // PACK_ALIGN) * PACK_ALIGN


def pack_blocks(blocks, dtype):
    flat = [b.astype(dtype).reshape(-1) for b in blocks]
    total = sum(f.shape[0] for f in flat)
    R = _pack_rows([f.shape[0] for f in flat])
    flat.append(jnp.zeros((R * D - total,), dtype))
    return jnp.concatenate(flat).reshape(R, D)


def unpack_blocks(buf, shapes):
    flat = buf.reshape(-1)
    out, off = [], 0
    for shp in shapes:
        n = int(np.prod(shp))
        out.append(flat[off:off + n].reshape(shp))
        off += n
    return out


def _shard_slice(a, axis, s):
    n = a.shape[axis] // N_SHARD
    return lax.slice_in_dim(a, s * n, (s + 1) * n, axis=axis)


def _pad_lanes(v):
    return jnp.concatenate([v.astype(F32), jnp.zeros((LANES - v.shape[0],), F32)])[None]


def kernel(x, c, w_ada, b_ada, norm_mix, norm_ffn, w_ffn_in, w_ffn_out, gdn_w_in, gdn_conv, gdn_a_log, gdn_dt_bias, gdn_out_norm, gdn_w_out, dsw_w_in, dsw_q_norm, dsw_k_norm, dsw_w_out, rel_bias, loss_target, m_w_ada, m_b_ada, m_norm_mix, m_norm_ffn, m_w_ffn_in, m_w_ffn_out, m_gdn_w_in, m_gdn_conv, m_gdn_a_log, m_gdn_dt_bias, m_gdn_out_norm, m_gdn_w_out, m_dsw_w_in, m_dsw_q_norm, m_dsw_k_norm, m_dsw_w_out, m_rel_bias, v_w_ada, v_b_ada, v_norm_mix, v_norm_ffn, v_w_ffn_in, v_w_ffn_out, v_gdn_w_in, v_gdn_conv, v_gdn_a_log, v_gdn_dt_bias, v_gdn_out_norm, v_gdn_w_out, v_dsw_w_in, v_dsw_q_norm, v_dsw_k_norm, v_dsw_w_out, v_rel_bias):
    S = x.shape[1]
    nt = S // RT
    xi, yi, ci = lax.axis_index("x"), lax.axis_index("y"), lax.axis_index("c")
    me = 4 * xi + 2 * yi + ci
    s_me = 2 * xi + yi
    x0, tgt = x[0], loss_target[0]
    shard = dict(w_ffn_in=w_ffn_in, w_ffn_out=w_ffn_out, gdn_w_in=gdn_w_in, gdn_w_out=gdn_w_out, dsw_w_in=dsw_w_in, dsw_w_out=dsw_w_out)

    whole = lambda a: Row(a, a.shape, lambda i: (0,) * a.ndim)
    (cond8,) = rowwise(lambda ids, v: (_silu(v),), [whole(c.reshape(8, LANES))], [], [Out((8, LANES), F32, (8, LANES), lambda i: (0, 0))], (1,), "cond")
    cond_all = all_gather_small(cond8, "gather_cond").reshape(N_DEV, D)
    cond16 = jnp.concatenate([cond_all, jnp.zeros((8, D), F32)], axis=0)
    ada_cols = w_ada.shape[2]
    mods = [matmul(cond16, w_ada[l], "nn", F32, f"ada_{l}")[:N_DEV] for l in range(2)]
    buf = jnp.concatenate([jnp.stack(mods, axis=1).reshape(-1, LANES), gdn_conv.reshape(-1, LANES)], axis=0)
    n_mod_rows = N_DEV * 2 * ada_cols // LANES
    got = all_gather_small(buf, "gather_mod").reshape(N_DEV, buf.shape[0], LANES)
    mod_parts, conv_parts = [], []
    for s in range(N_SHARD):
        from_dev = got[2 * s]
        mod_parts.append(lax.dynamic_index_in_dim(from_dev[:n_mod_rows].reshape(N_DEV, 2, ada_cols), me, 0, keepdims=False))
        conv_parts.append(from_dev[n_mod_rows:].reshape(4, -1))
    mod_nb = jnp.concatenate(mod_parts, axis=1)
    conv_w = jnp.concatenate(conv_parts, axis=1)
    (mod,) = rowwise(lambda ids, a, b: (a + b,), [whole(mod_nb), whole(b_ada)], [], [Out(mod_nb.shape, F32, mod_nb.shape, lambda i: (0, 0))], (1,), "mod_bias")
    mod = mod.reshape(2, 6, 1, D)
    sh1, sc1, g1, sh2, sc2, g2 = ([mod[l, k] for l in range(2)] for k in range(6))
    gmix = [norm_mix[l][None] for l in range(2)]
    gffn = [norm_ffn[l][None] for l in range(2)]

    gcols = gdn_w_in.shape[2]
    gflat_rows = -(-(D * gcols) // (D * PACK_ALIGN)) * PACK_ALIGN
    gflat = jnp.concatenate([gdn_w_in.astype(BF16).reshape(-1), jnp.zeros((gflat_rows * D - D * gcols,), BF16)]).reshape(gflat_rows, D)
    send = [gflat, gdn_w_out[0].astype(BF16), w_ffn_in[0].astype(BF16), w_ffn_out[0].astype(BF16),
            dsw_w_in[0].astype(BF16), dsw_w_out[0].astype(BF16), w_ffn_in[1].astype(BF16), w_ffn_out[1].astype(BF16)]
    g_gdn_in, g_gdn_out, g_in0, g_out0, g_dsw_in, g_dsw_out, g_in1, g_out1 = all_gather_shards(send)
    gw = jnp.transpose(g_gdn_in.reshape(N_SHARD, -1)[:, :D * gcols].reshape(N_SHARD, D, gcols), (1, 0, 2)).reshape(D, N_SHARD * gcols)
    w_gdn = jnp.concatenate([gw, jnp.zeros((D, GDN_PROJ - gw.shape[1]), BF16)], axis=1)
    w_ffn = [(g_in0, g_out0.reshape(FFN, D)), (g_in1, g_out1.reshape(FFN, D))]
    alog, dtb = _pad_lanes(gdn_a_log[0]), _pad_lanes(gdn_dt_bias[0])
    qg2 = jnp.concatenate([dsw_q_norm, dsw_q_norm], axis=1)
    kg2 = jnp.concatenate([dsw_k_norm, dsw_k_norm], axis=1)
    gdn_args = (w_gdn, conv_w, alog, dtb, gdn_out_norm, g_gdn_out.reshape(GDN_H * LANES, D))
    dsw_args = (g_dsw_in, qg2, kg2)

    (h10,) = rowwise(f_norm_only, [_wide(x0)], [gmix[0], sc1[0], sh1[0]], [_wide_out(S, BF16)], (nt,), "l0_norm")
    y0, sv_g = gdn_forward(h10, *gdn_args)
    x1, h20 = rowwise(f_resid_norm, [_wide(x0), _wide(y0)], [g1[0], gffn[0], sc2[0], sh2[0]], [_wide_out(S, F32), _wide_out(S, BF16)], (nt,), "l0_mid")
    f0, sv_f0 = ffn_forward(h20, *w_ffn[0], "0")
    x2, h11 = rowwise(f_resid_norm, [_wide(x1), _wide(f0)], [g2[0], gmix[1], sc1[1], sh1[1]], [_wide_out(S, F32), _wide_out(S, BF16)], (nt,), "l1_in")
    y1, sv_d = dsw_forward(h11, *dsw_args, rel_bias, g_dsw_out)
    x3, h21 = rowwise(f_resid_norm, [_wide(x2), _wide(y1)], [g1[1], gffn[1], sc2[1], sh2[1]], [_wide_out(S, F32), _wide_out(S, BF16)], (nt,), "l1_mid")
    f1, sv_f1 = ffn_forward(h21, *w_ffn[1], "1")
    part_spec = lambda a: Row(a, (None, 1, D), lambda i: (i, 0, 0))
    (parts,) = rowwise(f_loss, [_wide(x3), _wide(f1), _wide(tgt)], [g2[1]], [Out((nt, 1, D), F32, (None, 1, D), lambda i: (i, 0, 0))], (nt,), "loss")
    loss = lax.psum(jnp.sum(parts), ("x", "y", "c"))

    (dx3, df1), (dg2_1,) = rowwise_bwd(f_loss, [_wide(x3), _wide(f1, gdtype=BF16), _wide(tgt, diff=False)], [g2[1]],
                                       [part_spec(jnp.ones((nt, 1, D), F32))], (nt,), "loss_bwd")
    dh21, d_win1, d_wout1 = ffn_backward(df1, sv_f1, *w_ffn[1], "1")
    (dx2, dy1), (dg1_1, dgf1, dsc2_1, dsh2_1) = rowwise_bwd(
        f_resid_norm, [_wide(x2), _wide(y1, gdtype=BF16)], [g1[1], gffn[1], sc2[1], sh2[1]], [_wide(dx3), _wide(dh21)], (nt,), "l1_mid_bwd")
    dh11, g_d = dsw_backward(dy1, sv_d, *dsw_args, g_dsw_out)
    (dx1, df0), (dg2_0, dgm1, dsc1_1, dsh1_1) = rowwise_bwd(
        f_resid_norm, [_wide(x1), _wide(f0, gdtype=BF16)], [g2[0], gmix[1], sc1[1], sh1[1]], [_wide(dx2), _wide(dh11)], (nt,), "l1_in_bwd")
    dh20, d_win0, d_wout0 = ffn_backward(df0, sv_f0, *w_ffn[0], "0")
    (dx0p, dy0), (dg1_0, dgf0, dsc2_0, dsh2_0) = rowwise_bwd(
        f_resid_norm, [_wide(x0), _wide(y0, gdtype=BF16)], [g1[0], gffn[0], sc2[0], sh2[0]], [_wide(dx1), _wide(dh20)], (nt,), "l0_mid_bwd")
    dh10, g_g = gdn_backward(dy0, sv_g, *gdn_args)
    (grad_x,), (dgm0, dsc1_0, dsh1_0) = rowwise_bwd(f_first, [_wide(x0)], [gmix[0], sc1[0], sh1[0]], [_wide(dx0p), _wide(dh10)], (nt,), "l0_norm_bwd")

    dmod = jnp.concatenate([dsh1_0, dsc1_0, dg1_0, dsh2_0, dsc2_0, dg2_0, dsh1_1, dsc1_1, dg1_1, dsh2_1, dsc2_1, dg2_1], axis=1)
    d_rel = jnp.transpose(g_d["rel"][:, :, 0])
    fold = lambda v: v[:, :DSW_DH] + v[:, DSW_DH:]
    small = [dmod, jnp.concatenate([dgm0, dgm1], axis=1), jnp.concatenate([dgf0, dgf1], axis=1), g_g["conv"].reshape(1, -1),
             g_g["alog"], g_g["dtb"], g_g["gain"], _pad_lanes(fold(g_d["q_gain2"])[0]), _pad_lanes(fold(g_d["k_gain2"])[0]),
             d_rel.reshape(1, -1)]
    used = [v.shape[1] // LANES for v in small]
    sizes = [-(-u // 8) * 8 for u in used]
    pad8 = lambda v, u, s: jnp.concatenate([v.reshape(u, LANES), jnp.zeros((s - u, LANES), F32)], axis=0) if s > u else v.reshape(u, LANES)
    pad_rows = sum(sizes)
    sbuf = jnp.concatenate([pad8(v, u, s) for v, u, s in zip(small, used, sizes)], axis=0)
    sgot = all_gather_small(sbuf, "gather_small_grads")
    ssum = add_rows([sgot[d * pad_rows:(d + 1) * pad_rows] for d in range(N_DEV)], F32, "sum_small_grads", rt=pad_rows)
    offs = np.cumsum([0] + sizes)
    take = lambda k: ssum[offs[k]:offs[k] + used[k]].reshape(1, -1)
    grad_b_ada = take(0).reshape(2, 6 * D)
    grad_norm_mix = take(1).reshape(2, D)
    grad_norm_ffn = take(2).reshape(2, D)
    conv_full = take(3).reshape(4, -1)
    ncv = gdn_conv.shape[2]
    grad_gdn_conv = lax.dynamic_slice_in_dim(conv_full, s_me * ncv, ncv, axis=1)[None]
    grad_a_log = take(4)[:, :GDN_H]
    grad_dt_bias = take(5)[:, :GDN_H]
    grad_out_norm = take(6)
    grad_q_norm = take(7)[:, :DSW_DH]
    grad_k_norm = take(8)[:, :DSW_DH]
    grad_rel = take(9).reshape(REL_BUCKETS, 3 * GDN_H)
    dmod_all = sgot.reshape(N_DEV, pad_rows, LANES)[:, :used[0]].reshape(N_DEV, 2, 6 * D)
    dmod_mine = lax.dynamic_slice_in_dim(dmod_all, s_me * ada_cols, ada_cols, axis=2)
    dmod16 = jnp.concatenate([dmod_mine, jnp.zeros_like(dmod_mine)], axis=0)
    grad_w_ada = jnp.stack([matmul(cond16, dmod16[:, l], "tn", F32, f"ada_dw_{l}") for l in range(2)])

    dg_in = jnp.transpose(g_g["w_in"][:, :N_SHARD * gcols].reshape(D, N_SHARD, gcols), (1, 0, 2)).reshape(N_SHARD, D * gcols)
    dg_in = jnp.concatenate([dg_in, jnp.zeros((N_SHARD, gflat_rows * D - D * gcols), F32)], axis=1).reshape(N_SHARD, gflat_rows, D)
    by_shard = lambda a: a.reshape(N_SHARD, a.shape[0] // N_SHARD, a.shape[1])
    dws = [dg_in, by_shard(g_g["w_out"]), d_win0, by_shard(d_wout0), g_d["w_in"], g_d["w_out"], d_win1, by_shard(d_wout1)]
    keeps, gives = [], []
    for a in dws:
        rh = a.shape[1] // 2
        keeps.append(lax.dynamic_slice_in_dim(a, ci * rh, rh, axis=1))
        gives.append(lax.dynamic_slice_in_dim(a, (1 - ci) * rh, rh, axis=1).astype(BF16))
    from_sib = sibling_exchange(gives, "grads_to_sibling")
    flat2 = lambda a: a.reshape(-1, a.shape[-1])
    parts = [add_rows([flat2(k_), flat2(f_)], BF16, f"grads_chip_sum_{i}").reshape(k_.shape) for i, (k_, f_) in enumerate(zip(keeps, from_sib))]
    others = scatter_to_chips(parts)
    halves = []
    for i, (p_, o_) in enumerate(zip(parts, others)):
        own = lax.dynamic_index_in_dim(p_, s_me, 0, keepdims=False)
        halves.append(add_rows([own, o_[0], o_[1], o_[2]], F32, f"grads_sum_{i}"))
    s_gdn_in, s_gdn_out, s_in0, s_out0, s_dsw_in, s_dsw_out, s_in1, s_out1 = merge_halves(halves)
    gsh = dict(gdn_w_in=s_gdn_in.reshape(-1)[:D * gcols].reshape(1, D, gcols), gdn_w_out=s_gdn_out[None],
               w_ffn_in=jnp.stack([s_in0, s_in1]), w_ffn_out=jnp.stack([s_out0, s_out1]),
               dsw_w_in=s_dsw_in[None], dsw_w_out=s_dsw_out[None])

    grads = dict(w_ada=grad_w_ada, b_ada=grad_b_ada, norm_mix=grad_norm_mix, norm_ffn=grad_norm_ffn, w_ffn_in=gsh["w_ffn_in"],
                 w_ffn_out=gsh["w_ffn_out"], gdn_w_in=gsh["gdn_w_in"], gdn_conv=grad_gdn_conv, gdn_a_log=grad_a_log,
                 gdn_dt_bias=grad_dt_bias, gdn_out_norm=grad_out_norm, gdn_w_out=gsh["gdn_w_out"], dsw_w_in=gsh["dsw_w_in"],
                 dsw_q_norm=grad_q_norm, dsw_k_norm=grad_k_norm, dsw_w_out=gsh["dsw_w_out"], rel_bias=grad_rel)
    weights = dict(w_ada=w_ada, b_ada=b_ada, norm_mix=norm_mix, norm_ffn=norm_ffn, w_ffn_in=w_ffn_in, w_ffn_out=w_ffn_out,
                   gdn_w_in=gdn_w_in, gdn_conv=gdn_conv, gdn_a_log=gdn_a_log, gdn_dt_bias=gdn_dt_bias, gdn_out_norm=gdn_out_norm,
                   gdn_w_out=gdn_w_out, dsw_w_in=dsw_w_in, dsw_q_norm=dsw_q_norm, dsw_k_norm=dsw_k_norm, dsw_w_out=dsw_w_out,
                   rel_bias=rel_bias)
    ms = dict(w_ada=m_w_ada, b_ada=m_b_ada, norm_mix=m_norm_mix, norm_ffn=m_norm_ffn, w_ffn_in=m_w_ffn_in, w_ffn_out=m_w_ffn_out,
              gdn_w_in=m_gdn_w_in, gdn_conv=m_gdn_conv, gdn_a_log=m_gdn_a_log, gdn_dt_bias=m_gdn_dt_bias, gdn_out_norm=m_gdn_out_norm,
              gdn_w_out=m_gdn_w_out, dsw_w_in=m_dsw_w_in, dsw_q_norm=m_dsw_q_norm, dsw_k_norm=m_dsw_k_norm, dsw_w_out=m_dsw_w_out,
              rel_bias=m_rel_bias)
    vs = dict(w_ada=v_w_ada, b_ada=v_b_ada, norm_mix=v_norm_mix, norm_ffn=v_norm_ffn, w_ffn_in=v_w_ffn_in, w_ffn_out=v_w_ffn_out,
              gdn_w_in=v_gdn_w_in, gdn_conv=v_gdn_conv, gdn_a_log=v_gdn_a_log, gdn_dt_bias=v_gdn_dt_bias, gdn_out_norm=v_gdn_out_norm,
              gdn_w_out=v_gdn_w_out, dsw_w_in=v_dsw_w_in, dsw_q_norm=v_dsw_q_norm, dsw_k_norm=v_dsw_k_norm, dsw_w_out=v_dsw_w_out,
              rel_bias=v_rel_bias)
    names = list(weights)
    deltas, new_m, new_v = [], [], []
    for n in names:
        g = grads[n].reshape(weights[n].shape)
        grads[n] = g
        d, nm, nv = adamw(weights[n], g, ms[n], vs[n], f"adamw_{n}")
        deltas.append(d)
        new_m.append(nm)
        new_v.append(nv)
    return (loss, grad_x[None], *[grads[n] for n in names], *deltas, *new_m, *new_v)
```

```python
import functools
import math

import numpy as np
import jax
import jax.numpy as jnp
from jax import lax
from jax.experimental import pallas as pl
from jax.experimental.pallas import tpu as pltpu

F32 = jnp.float32
BF16 = jnp.bfloat16
SDS = jax.ShapeDtypeStruct
MESH = pl.DeviceIdType.MESH
ANY = pl.BlockSpec(memory_space=pl.ANY)

D = 1024
EPS = 1e-6
LANES = 128
GDN_H = 8
GDN_DK = 128
GDN_C = 64
DSW_GROUPS = ((128, 1), (512, 4), (2048, 16))
DSW_SPAN = 128
DSW_DH = 64
DSW_HG = 512
REL_BUCKETS = 32
REL_MAX_DIST = 2048
FFN = 2816
N_SHARD = 4
N_DEV = 8
VMEM_LIMIT = 48 * 1024 * 1024
NEG = -1e30

ADAM_LR, ADAM_B1, ADAM_B2, ADAM_EPS, ADAM_WD, ADAM_STEP = 0.001, 0.9, 0.999, 1e-08, 0.01, 10


def _cp(n_axes):
    return pltpu.CompilerParams(dimension_semantics=("arbitrary",) * n_axes, vmem_limit_bytes=VMEM_LIMIT)


def _blk(dim, cap):
    if dim <= cap:
        return dim
    best = None
    for b in range(LANES, cap + 1, LANES):
        if dim % b == 0:
            best = b
    assert best is not None, (dim, cap)
    return best


MAX_SHARD_BLOCK = 1408
def matmul(a, b, mode, out_dtype, name, cap_m=1024, cap_n=1024, cap_k=2048, col_shards=0):
    ns = col_shards
    if mode == "nn":
        (M, K) = a.shape
        K2, N = (b.shape[1], ns * b.shape[2]) if ns else b.shape
    elif mode == "nt":
        (M, K) = a.shape
        N, K2 = (b.shape[1], ns * b.shape[2]) if ns else b.shape
    else:
        (K, M), (K2, N) = a.shape, b.shape
    assert K == K2, (a.shape, b.shape, mode)
    if K <= 3072:
        cap_k = K
        if K > 2048:
            cap_n = 512
    n_unit = N // ns if (ns and mode != "nt") else N
    k_unit = K // ns if (ns and mode == "nt") else K
    bm = _blk(M, cap_m)
    bn = _blk(n_unit, MAX_SHARD_BLOCK) if n_unit != N else _blk(N, cap_n)
    if k_unit != K:
        bk = _blk(k_unit, MAX_SHARD_BLOCK)
    else:
        bk = _blk(K, 1024 if (ns and mode == "tn") else cap_k)
    nk = K // bk
    nps, kps = n_unit // bn, k_unit // bk
    dims = {"nn": ((1,), (0,)), "nt": ((1,), (1,)), "tn": ((0,), (0,))}[mode]

    def dot(a_ref, b_ref):
        return lax.dot_general(a_ref[...].astype(BF16), b_ref[...].astype(BF16), (dims, ((), ())), preferred_element_type=F32)

    def body_one(a_ref, b_ref, o_ref):
        o_ref[...] = dot(a_ref, b_ref).astype(o_ref.dtype)

    def body_acc(a_ref, b_ref, o_ref, acc_ref):
        k = pl.program_id(2)

        @pl.when(k == 0)
        def _():
            acc_ref[...] = jnp.zeros_like(acc_ref)

        acc_ref[...] += dot(a_ref, b_ref)

        @pl.when(k == nk - 1)
        def _():
            o_ref[...] = acc_ref[...].astype(o_ref.dtype)

    a_spec = pl.BlockSpec((bk, bm), lambda i, j, k: (k, i)) if mode == "tn" else pl.BlockSpec((bm, bk), lambda i, j, k: (i, k))
    if mode == "nt":
        b_spec = pl.BlockSpec((None, bn, bk), lambda i, j, k: (k // kps, j, k % kps)) if ns else pl.BlockSpec((bn, bk), lambda i, j, k: (j, k))
    elif mode == "nn" and ns:
        b_spec = pl.BlockSpec((None, bk, bn), lambda i, j, k: (j // nps, k, j % nps))
    else:
        b_spec = pl.BlockSpec((bk, bn), lambda i, j, k: (k, j))
    if mode == "tn" and ns:
        o_spec, o_shape = pl.BlockSpec((None, bm, bn), lambda i, j, k: (j // nps, i, j % nps)), (ns, M, n_unit)
    else:
        o_spec, o_shape = pl.BlockSpec((bm, bn), lambda i, j, k: (i, j)), (M, N)
    return pl.pallas_call(
        body_one if nk == 1 else body_acc, name=name, grid=(M // bm, N // bn, nk),
        in_specs=[a_spec, b_spec], out_specs=o_spec,
        out_shape=SDS(o_shape, out_dtype), scratch_shapes=[] if nk == 1 else [pltpu.VMEM((bm, bn), F32)],
        compiler_params=_cp(3),
    )(a, b)


class Row:
    def __init__(self, arr, bshape, imap, splits=None, diff=True, acc=False, gdtype=F32, gshape=None, gbshape=None, gimap=None,
                 lead=0):
        self.arr, self.bshape, self.imap = arr, tuple(bshape), imap
        self.splits, self.lead = splits, lead
        self.diff, self.acc, self.gdtype = diff, acc, gdtype
        self.gshape = tuple(arr.shape) if gshape is None else tuple(gshape)
        self.gbshape = self.bshape if gbshape is None else tuple(gbshape)
        self.gimap = imap if gimap is None else gimap

    def gspec(self):
        return pl.BlockSpec(self.gbshape, self.gimap)

    def spec(self):
        return pl.BlockSpec(self.bshape, self.imap)

    def pieces(self, ref):
        return _load_pieces(ref, self.splits, self.lead)

    def n_pieces(self):
        return _n_pieces(self.splits, self.lead)


class Out:
    def __init__(self, shape, dtype, bshape, imap, splits=None, lead=0):
        self.shape, self.dtype, self.bshape, self.imap = tuple(shape), dtype, tuple(bshape), imap
        self.splits, self.lead = splits, lead

    def n_pieces(self):
        return _n_pieces(self.splits, self.lead)


def _n_pieces(splits, lead):
    return lead if lead else (1 if splits is None else len(splits))


def _load_pieces(ref, splits, lead):
    if lead:
        return [ref[k].astype(F32) for k in range(lead)]
    if splits is None:
        return [ref[...].astype(F32)]
    out, o = [], 0
    for w in splits:
        out.append(ref[..., o:o + w].astype(F32))
        o += w
    return out


def _store_pieces(ref, splits, lead, vals, accumulate=False):
    def put(idx, v):
        if accumulate:
            ref[idx] += v.astype(ref.dtype)
        else:
            ref[idx] = v.astype(ref.dtype)

    if lead:
        for k in range(lead):
            put(k, vals[k])
    elif splits is None:
        put(..., vals[0])
    else:
        o = 0
        for w, v in zip(splits, vals):
            put((..., slice(o, o + w)), v)
            o += w


def rowwise(fn, rows, params, outs, grid, name):
    nr, npar = len(rows), len(params)

    def body(*refs):
        ids = tuple(pl.program_id(a) for a in range(len(grid)))
        vals = []
        for r, ref in zip(rows, refs[:nr]):
            vals += r.pieces(ref)
        pvals = [ref[...].astype(F32) for ref in refs[nr:nr + npar]]
        res = list(fn(ids, *vals, *pvals))
        o = 0
        for spec, ref in zip(outs, refs[nr + npar:]):
            n = spec.n_pieces()
            _store_pieces(ref, spec.splits, spec.lead, res[o:o + n])
            o += n

    nz = len(grid)
    pspecs = [pl.BlockSpec(p.shape, (lambda *ids, _n=p.ndim: (0,) * _n)) for p in params]
    res = pl.pallas_call(
        body, name=name, grid=grid,
        in_specs=[r.spec() for r in rows] + pspecs,
        out_specs=[pl.BlockSpec(o.bshape, o.imap) for o in outs],
        out_shape=[SDS(o.shape, o.dtype) for o in outs],
        compiler_params=_cp(nz),
    )(*[r.arr for r in rows], *params)
    return list(res)


def rowwise_bwd(fn, rows, params, cots, grid, name):
    nr, npar, nc = len(rows), len(params), len(cots)
    drows = [r for r in rows if r.diff]
    nz = len(grid)

    def body(*refs):
        ids = tuple(pl.program_id(a) for a in range(nz))
        row_refs, par_refs = refs[:nr], refs[nr:nr + npar]
        cot_refs = refs[nr + npar:nr + npar + nc]
        drow_refs = refs[nr + npar + nc:nr + npar + nc + len(drows)]
        dpar_refs = refs[nr + npar + nc + len(drows):]
        pieces, is_diff = [], []
        for r, ref in zip(rows, row_refs):
            ps = r.pieces(ref)
            pieces += ps
            is_diff += [r.diff] * len(ps)
        pvals = [ref[...].astype(F32) for ref in par_refs]
        dvals = [p for p, dflag in zip(pieces, is_diff) if dflag]
        nd = len(dvals)

        def f(*args):
            it = iter(args[:nd])
            full = [next(it) if dflag else p for p, dflag in zip(pieces, is_diff)]
            return tuple(fn(ids, *full, *args[nd:]))

        _, vjp = jax.vjp(f, *dvals, *pvals)
        cvals = []
        for c, ref in zip(cots, cot_refs):
            cvals += c.pieces(ref)
        g = vjp(tuple(cvals))
        o = 0
        first_inner = ids[-1] == 0
        for r, ref in zip(drows, drow_refs):
            n = r.n_pieces()
            gs = g[o:o + n]
            o += n
            if r.acc:
                @pl.when(first_inner)
                def _(ref=ref):
                    ref[...] = jnp.zeros_like(ref)
            _store_pieces(ref, r.splits, r.lead, gs, accumulate=r.acc)
        first = functools.reduce(jnp.logical_and, [i == 0 for i in ids])
        for ref, gp in zip(dpar_refs, g[nd:]):
            @pl.when(first)
            def _(ref=ref):
                ref[...] = jnp.zeros_like(ref)
            ref[...] += gp

    pspecs = [pl.BlockSpec(p.shape, (lambda *ids, _n=p.ndim: (0,) * _n)) for p in params]
    res = pl.pallas_call(
        body, name=name, grid=grid,
        in_specs=[r.spec() for r in rows] + pspecs + [c.spec() for c in cots],
        out_specs=[r.gspec() for r in drows] + pspecs,
        out_shape=[SDS(r.gshape, r.gdtype) for r in drows] + [SDS(p.shape, F32) for p in params],
        compiler_params=_cp(nz),
    )(*[r.arr for r in rows], *params, *[c.arr for c in cots])
    res = list(res)
    return res[:len(drows)], res[len(drows):]


def _sigmoid(x):
    return 0.5 * (jnp.tanh(0.5 * x) + 1.0)


def _silu(x):
    return x * _sigmoid(x)


def _normmod(x, gain, sc, sh):
    inv = lax.rsqrt(jnp.mean(x * x, axis=-1, keepdims=True) + EPS)
    return x * inv * gain * (1.0 + sc) + sh


def f_first(ids, x, gain, sc, sh):
    return x, _normmod(x, gain, sc, sh)


def f_resid_norm(ids, x, y, g, gain, sc, sh):
    xn = x + g * y
    return xn, _normmod(xn, gain, sc, sh)


def f_swiglu(ids, gate, up):
    return (_silu(gate) * up,)


def f_loss(ids, x, y, tgt, g):
    out = x + g * y
    e = out - tgt
    part = 0.5 * jnp.sum(e * e, axis=0, keepdims=True) * (1.0 / D)
    return (part,)


def _softplus(x):
    return jnp.maximum(x, 0.0) + jnp.log(1.0 + jnp.exp(-jnp.abs(x)))


def _chunk_tril(T):
    r = lax.broadcasted_iota(jnp.int32, (T, T), 0)
    c = lax.broadcasted_iota(jnp.int32, (T, T), 1)
    return jnp.where((r // GDN_C == c // GDN_C) & (c <= r), 1.0, 0.0).astype(F32)


def _dot_hi(a, b, dims=((1,), (0,))):
    return lax.dot_general(a, b, (dims, ((), ())), precision=lax.Precision.HIGHEST, preferred_element_type=F32)


def _dot_x3(a, b, dims=((1,), (0,))):
    return lax.dot_general(a, b, (dims, ((), ())), precision=lax.Precision.HIGH, preferred_element_type=F32)


def f_gdn_gates(ids, ab, alog, dtb):
    T = ab.shape[0]
    g = -jnp.exp(alog) * _softplus(ab + dtb)
    beta = _sigmoid(ab)
    gcum = _dot_hi(_chunk_tril(T), g)
    row = lax.broadcasted_iota(jnp.int32, (LANES, LANES), 0)
    sel = lambda k: jnp.where(row == k, 1.0, 0.0).astype(F32)
    gcs = [_dot_hi(gcum, sel(h)) for h in range(GDN_H)]
    bts = [_dot_hi(beta, sel(GDN_H + h)) for h in range(GDN_H)]
    return (*gcs, *bts)


def f_gdn_post(ids, *args):
    os_, zs, gain = args[:GDN_H], args[GDN_H:2 * GDN_H], args[2 * GDN_H]
    out = []
    for o, z in zip(os_, zs):
        inv = lax.rsqrt(jnp.mean(o * o, axis=-1, keepdims=True) + EPS)
        out.append(o * inv * gain * _silu(z))
    return tuple(out)


def _qknorm1(x, gain2, scale):
    lane = lax.broadcasted_iota(jnp.int32, x.shape, 1)
    lo = lane < DSW_DH
    x2 = x * x
    s_all = jnp.sum(x2, axis=-1, keepdims=True)
    s_lo = jnp.sum(jnp.where(lo, x2, 0.0), axis=-1, keepdims=True)
    ms = jnp.where(lo, s_lo, s_all - s_lo) * (1.0 / DSW_DH)
    return x * lax.rsqrt(ms + EPS) * (gain2 * scale)


def f_qknorm(ids, *args):
    return tuple(_qknorm1(x, args[-1], 1.0) for x in args[:-1])


def f_qnorm(ids, *args):
    return tuple(_qknorm1(x, args[-1], DSW_DH ** -0.5) for x in args[:-1])


def f_combine(ids, o0, o1, o2, l0, l1, l2):
    m = jnp.maximum(jnp.maximum(l0, l1), l2)
    e0, e1, e2 = jnp.exp(l0 - m), jnp.exp(l1 - m), jnp.exp(l2 - m)
    den = e0 + e1 + e2
    o = (e0 * o0 + e1 * o1 + e2 * o2) / den
    return o, m + jnp.log(den)


GDN_T = 512
HALO = 8


def _conv_pre(xx, w):
    acc = xx * w[3:4, :]
    for j in range(3):
        acc = acc + pltpu.roll(xx, shift=3 - j, axis=0) * w[j:j + 1, :]
    return acc


def _qkv_act(pre, cidx):
    s = _silu(pre)
    r = lax.rsqrt(jnp.sum(s * s, axis=-1, keepdims=True) + EPS)
    scale = jnp.where(cidx < GDN_H, GDN_DK ** -0.5, 1.0).astype(F32)
    return jnp.where(cidx < 2 * GDN_H, s * r * scale, s)


def gdn_pre(proj, conv_w, S):
    nt = S // GDN_T
    hb = GDN_T // HALO

    def body(prev_ref, cur_ref, w_ref, o_ref):
        p, i = pl.program_id(0), pl.program_id(1)
        for h in range(GDN_H):
            cols = slice(LANES * h, LANES * (h + 1))
            prev = jnp.where(i > 0, prev_ref[:, cols], 0.0)
            xx = jnp.concatenate([prev, cur_ref[:, cols]], axis=0)
            pre = _conv_pre(xx, w_ref[:, cols])[HALO:]
            o_ref[h] = _qkv_act(pre, p * GDN_H + h)

    hv = GDN_H * LANES
    return pl.pallas_call(
        body, name="gdn_pre", grid=(3, nt),
        in_specs=[pl.BlockSpec((HALO, hv), lambda p, i: (jnp.maximum(i * hb - 1, 0), p)),
                  pl.BlockSpec((GDN_T, hv), lambda p, i: (i, p)),
                  pl.BlockSpec((4, hv), lambda p, i: (0, p))],
        out_specs=pl.BlockSpec((None, GDN_H, GDN_T, LANES), lambda p, i: (p, 0, i, 0)),
        out_shape=SDS((3, GDN_H, S, LANES), F32),
        compiler_params=_cp(2),
    )(proj, proj, conv_w)


def gdn_pre_bwd(proj, conv_w, dqkv, S):
    nt = S // GDN_T
    hb = GDN_T // HALO
    last_h = S // HALO - 1

    def body(prev_ref, cur_ref, next_ref, w_ref, d_ref, dnext_ref, dx_ref, dw_ref):
        p, i = pl.program_id(0), pl.program_id(1)

        @pl.when(i == 0)
        def _():
            dw_ref[...] = jnp.zeros_like(dw_ref)

        for h in range(GDN_H):
            cols = slice(LANES * h, LANES * (h + 1))
            w = w_ref[:, cols]
            prev = jnp.where(i > 0, prev_ref[:, cols], 0.0)
            xx = jnp.concatenate([prev, cur_ref[:, cols], next_ref[:, cols]], axis=0)
            dnext = jnp.where(i < nt - 1, dnext_ref[h], 0.0)
            dd = jnp.concatenate([jnp.zeros((HALO, LANES), F32), d_ref[h], dnext], axis=0)
            pre = _conv_pre(xx, w)
            _, vjp = jax.vjp(lambda v, _c=p * GDN_H + h: _qkv_act(v, _c), pre)
            (dpre,) = vjp(dd)
            row = lax.broadcasted_iota(jnp.int32, dpre.shape, 0)
            dpre = jnp.where(row >= HALO, dpre, 0.0)
            dx = dpre * w[3:4, :]
            R = dpre.shape[0]
            for j in range(3):
                dx = dx + pltpu.roll(dpre, shift=R - (3 - j), axis=0) * w[j:j + 1, :]
            dx_ref[:, cols] = dx[HALO:HALO + GDN_T].astype(dx_ref.dtype)
            own = jnp.where(row < HALO + GDN_T, dpre, 0.0)
            rows_w = [jnp.sum(own * pltpu.roll(xx, shift=3 - j, axis=0), axis=0, keepdims=True) for j in range(3)]
            rows_w.append(jnp.sum(own * xx, axis=0, keepdims=True))
            r4 = lax.broadcasted_iota(jnp.int32, (4, LANES), 0)
            dw = jnp.zeros((4, LANES), F32)
            for j in range(4):
                dw = dw + jnp.where(r4 == j, rows_w[j], 0.0)
            dw_ref[:, cols] += dw

    hv = GDN_H * LANES
    return pl.pallas_call(
        body, name="gdn_pre_bwd", grid=(3, nt),
        in_specs=[pl.BlockSpec((HALO, hv), lambda p, i: (jnp.maximum(i * hb - 1, 0), p)),
                  pl.BlockSpec((GDN_T, hv), lambda p, i: (i, p)),
                  pl.BlockSpec((HALO, hv), lambda p, i: (jnp.minimum((i + 1) * hb, last_h), p)),
                  pl.BlockSpec((4, hv), lambda p, i: (0, p)),
                  pl.BlockSpec((None, GDN_H, GDN_T, LANES), lambda p, i: (p, 0, i, 0)),
                  pl.BlockSpec((None, GDN_H, HALO, LANES), lambda p, i: (p, 0, jnp.minimum((i + 1) * hb, last_h), 0))],
        out_specs=[pl.BlockSpec((GDN_T, hv), lambda p, i: (i, p)),
                   pl.BlockSpec((4, hv), lambda p, i: (0, p))],
        out_shape=[SDS((S, 3 * hv), BF16), SDS((4, 3 * hv), F32)],
        compiler_params=_cp(2),
    )(proj, proj, proj, conv_w, dqkv, dqkv)


_DIMS = {"nn": ((1,), (0,)), "nt": ((1,), (1,)), "tn": ((0,), (0,))}


def _mm_raw(a, b, mode, hi):
    if hi:
        return _dot_hi(a, b, _DIMS[mode])
    return lax.dot_general(a.astype(BF16), b.astype(BF16), (_DIMS[mode], ((), ())), preferred_element_type=F32)


@functools.partial(jax.custom_vjp, nondiff_argnums=(2, 3))
def mm(a, b, mode, hi):
    return _mm_raw(a, b, mode, hi)


def _mm_fwd(a, b, mode, hi):
    return _mm_raw(a, b, mode, hi), (a, b)


def _mm_bwd(mode, hi, res, dc):
    a, b = res
    if mode == "nn":
        da, db = mm(dc, b, "nt", hi), mm(a, dc, "tn", hi)
    elif mode == "nt":
        da, db = mm(dc, b, "nn", hi), mm(dc, a, "tn", hi)
    else:
        da, db = mm(b, dc, "nt", hi), mm(a, dc, "nn", hi)
    return da, db


mm.defvjp(_mm_fwd, _mm_bwd)


TRI_BASE = 8


def _unit_lower_inverses(Ls):
    n = Ls[0].shape[0]
    r = lax.broadcasted_iota(jnp.int32, (n, n), 0)
    c = lax.broadcasted_iota(jnp.int32, (n, n), 1)
    eye = jnp.where(r == c, 1.0, 0.0).astype(F32)
    base = r // TRI_BASE == c // TRI_BASE
    Ps = [jnp.where(base, -L, 0.0) for L in Ls]
    invs = [eye + P for P in Ps]
    k = 1
    while 2 * k < TRI_BASE:
        Ps = [_dot_x3(P, P) for P in Ps]
        invs = [inv + _dot_x3(inv, P) for inv, P in zip(invs, Ps)]
        k *= 2
    b = 2 * TRI_BASE
    while b <= n:
        off_mask = (r // b == c // b) & ((r % b) >= b // 2) & ((c % b) < b // 2)
        ts = [_dot_x3(inv, jnp.where(off_mask, L, 0.0)) for inv, L in zip(invs, Ls)]
        invs = [inv - _dot_x3(t, inv) for inv, t in zip(invs, ts)]
        b *= 2
    return invs


@jax.custom_vjp
def tri_solve2(Ls, r1s, r2s):
    invs = _unit_lower_inverses(Ls)
    return [_dot_x3(i, r) for i, r in zip(invs, r1s)], [_dot_x3(i, r) for i, r in zip(invs, r2s)]


def _tri_fwd(Ls, r1s, r2s):
    invs = _unit_lower_inverses(Ls)
    s1s = [_dot_x3(i, r) for i, r in zip(invs, r1s)]
    s2s = [_dot_x3(i, r) for i, r in zip(invs, r2s)]
    return (s1s, s2s), (invs, s1s, s2s)


def _tri_bwd(res, ds):
    invs, s1s, s2s = res
    d1s = [_dot_x3(i, d, _DIMS["tn"]) for i, d in zip(invs, ds[0])]
    d2s = [_dot_x3(i, d, _DIMS["tn"]) for i, d in zip(invs, ds[1])]
    dLs = [-(_dot_x3(d1, s1, _DIMS["nt"]) + _dot_x3(d2, s2, _DIMS["nt"])) for d1, s1, d2, s2 in zip(d1s, s1s, d2s, s2s)]
    return dLs, d1s, d2s


tri_solve2.defvjp(_tri_fwd, _tri_bwd)


def _gdn_chunk(qs, ks, vs, gcbs, btbs, Ss):
    C = qs[0].shape[0]
    r = lax.broadcasted_iota(jnp.int32, (C, C), 0)
    c = lax.broadcasted_iota(jnp.int32, (C, C), 1)
    causal, strict = c <= r, c < r
    rows = lax.broadcasted_iota(jnp.int32, gcbs[0].shape, 0)
    Gs = [g[:, :C] for g in gcbs]
    decays = [jnp.exp(jnp.where(causal, G - G.T, NEG)) for G in Gs]
    kbs = [k * b for k, b in zip(ks, btbs)]
    vbs = [v * b for v, b in zip(vs, btbs)]
    Ls = [jnp.where(strict, mm(kb, k, "nt", False) * d, 0.0) for kb, k, d in zip(kbs, ks, decays)]
    egs = [jnp.exp(g) for g in gcbs]
    us, ws = tri_solve2(Ls, vbs, [kb * eg for kb, eg in zip(kbs, egs)])
    qks = [jnp.where(causal, mm(q, k, "nt", False) * d, 0.0) for q, k, d in zip(qs, ks, decays)]
    g_lasts = [jnp.sum(jnp.where(rows == C - 1, g, 0.0), axis=0, keepdims=True) for g in gcbs]
    q_decs = [q * eg for q, eg in zip(qs, egs)]
    k_decs = [k * jnp.exp(gl - g) for k, gl, g in zip(ks, g_lasts, gcbs)]
    v_news = [u - mm(w, S, "nn", False) for u, w, S in zip(us, ws, Ss)]
    os_ = [mm(qd, S, "nn", False) + mm(qk, vn, "nn", False) for qd, S, qk, vn in zip(q_decs, Ss, qks, v_news)]
    S_news = [S * jnp.exp(gl) + mm(kd, vn, "tn", False) for S, gl, kd, vn in zip(Ss, g_lasts, k_decs, v_news)]
    return os_, S_news


def gdn_core(qkv, gc, bt, S):
    nchunk = S // GDN_C

    def body(qkv_ref, g_ref, b_ref, o_ref, st_ref, s_scr):
        n = pl.program_id(0)

        @pl.when(n == 0)
        def _():
            s_scr[...] = jnp.zeros_like(s_scr)

        heads = range(GDN_H)
        S_in = [s_scr[h] for h in heads]
        os_, S_new = _gdn_chunk([qkv_ref[0, h] for h in heads], [qkv_ref[1, h] for h in heads], [qkv_ref[2, h] for h in heads],
                                [g_ref[h] for h in heads], [b_ref[h] for h in heads], S_in)
        for h in heads:
            st_ref[h] = S_in[h]
            o_ref[h] = os_[h]
            s_scr[h] = S_new[h]

    blk3 = pl.BlockSpec((3, GDN_H, GDN_C, LANES), lambda n: (0, 0, n, 0))
    hb = pl.BlockSpec((GDN_H, GDN_C, LANES), lambda n: (0, n, 0))
    return pl.pallas_call(
        body, name="gdn_core", grid=(nchunk,),
        in_specs=[blk3, hb, hb],
        out_specs=[hb, pl.BlockSpec((GDN_H, None, GDN_DK, LANES), lambda n: (0, n, 0, 0))],
        out_shape=[SDS((GDN_H, S, LANES), F32), SDS((GDN_H, nchunk, GDN_DK, LANES), F32)],
        scratch_shapes=[pltpu.VMEM((GDN_H, GDN_DK, LANES), F32)],
        compiler_params=_cp(1),
    )(qkv, gc, bt)


def gdn_core_bwd(qkv, gc, bt, states, do, S):
    nchunk = S // GDN_C

    def body(qkv_ref, g_ref, b_ref, st_ref, do_ref, dqkv_ref, dg_ref, db_ref, ds_scr):
        n = pl.program_id(0)

        @pl.when(n == 0)
        def _():
            ds_scr[...] = jnp.zeros_like(ds_scr)

        heads = range(GDN_H)
        _, vjp = jax.vjp(_gdn_chunk, [qkv_ref[0, h] for h in heads], [qkv_ref[1, h] for h in heads], [qkv_ref[2, h] for h in heads],
                         [g_ref[h] for h in heads], [b_ref[h] for h in heads], [st_ref[h] for h in heads])
        dq, dk, dv, dg, db, dS = vjp(([do_ref[h] for h in heads], [ds_scr[h] for h in heads]))
        for h in heads:
            dqkv_ref[0, h] = dq[h]
            dqkv_ref[1, h] = dk[h]
            dqkv_ref[2, h] = dv[h]
            dg_ref[h] = dg[h]
            db_ref[h] = db[h]
            ds_scr[h] = dS[h]

    rev = lambda n: nchunk - 1 - n
    blk3 = pl.BlockSpec((3, GDN_H, GDN_C, LANES), lambda n: (0, 0, rev(n), 0))
    hb = pl.BlockSpec((GDN_H, GDN_C, LANES), lambda n: (0, rev(n), 0))
    return pl.pallas_call(
        body, name="gdn_core_bwd", grid=(nchunk,),
        in_specs=[blk3, hb, hb, pl.BlockSpec((GDN_H, None, GDN_DK, LANES), lambda n: (0, rev(n), 0, 0)), hb],
        out_specs=[blk3, hb, hb],
        out_shape=[SDS((3, GDN_H, S, LANES), F32), SDS((GDN_H, S, LANES), F32), SDS((GDN_H, S, LANES), F32)],
        scratch_shapes=[pltpu.VMEM((GDN_H, GDN_DK, LANES), F32)],
        compiler_params=_cp(1),
    )(qkv, gc, bt, states, do)


GDN_MAIN = 4 * GDN_H * LANES
GDN_PROJ = GDN_MAIN + LANES
RT = 256


def gdn_forward(h, w_in, conv_w, alog, dtb, out_gain, w_out):
    S = h.shape[0]
    nt = S // RT
    proj = matmul(h, w_in, "nn", F32, "gdn_in")
    qkv = gdn_pre(proj, conv_w, S)
    ab_row = Row(proj, (RT, LANES), lambda i: (i, GDN_MAIN // LANES), gdtype=BF16, gshape=(S, LANES), gimap=lambda i: (i, 0))
    hm = lambda i: (0, i, 0)
    hv = GDN_H * LANES
    gc, bt = rowwise(f_gdn_gates, [ab_row], [alog, dtb],
                     [Out((GDN_H, S, LANES), F32, (GDN_H, RT, LANES), hm, lead=GDN_H)] * 2, (nt,), "gdn_gates")
    o, states = gdn_core(qkv, gc, bt, S)
    o_row = Row(o, (GDN_H, RT, LANES), hm, lead=GDN_H)
    z_row = Row(proj, (RT, hv), lambda i: (i, 3), splits=[LANES] * GDN_H, gdtype=BF16, gshape=(S, hv), gimap=lambda i: (i, 0))
    (on,) = rowwise(f_gdn_post, [o_row, z_row], [out_gain],
                    [Out((S, hv), BF16, (RT, hv), lambda i: (i, 0), splits=[LANES] * GDN_H)], (nt,), "gdn_post")
    y = matmul(on, w_out, "nn", F32, "gdn_out")
    saved = dict(h=h, proj=proj, qkv=qkv, gc=gc, bt=bt, states=states, o=o, on=on, ab_row=ab_row, o_row=o_row, z_row=z_row)
    return y, saved


def gdn_backward(dy, sv, w_in, conv_w, alog, dtb, out_gain, w_out):
    S = dy.shape[0]
    nt = S // RT
    hm = lambda i: (0, i, 0)
    hv = GDN_H * LANES
    don = matmul(dy, w_out, "nt", F32, "gdn_out_dx")
    d_w_out = matmul(sv["on"], dy, "tn", F32, "gdn_out_dw")
    (do, dz), (d_gain,) = rowwise_bwd(f_gdn_post, [sv["o_row"], sv["z_row"]], [out_gain],
                                      [Row(don, (RT, hv), lambda i: (i, 0), splits=[LANES] * GDN_H)], (nt,), "gdn_post_bwd")
    dqkv, dgc, dbt = gdn_core_bwd(sv["qkv"], sv["gc"], sv["bt"], sv["states"], do, S)
    head_blk = lambda a: Row(a, (GDN_H, RT, LANES), hm, lead=GDN_H)
    (dab,), (d_alog, d_dtb) = rowwise_bwd(f_gdn_gates, [sv["ab_row"]], [alog, dtb], [head_blk(dgc), head_blk(dbt)],
                                          (nt,), "gdn_gates_bwd")
    dqkv_proj, d_conv = gdn_pre_bwd(sv["proj"], conv_w, dqkv, S)
    dproj = jnp.concatenate([dqkv_proj, dz, dab], axis=1)
    d_w_in = matmul(sv["h"], dproj, "tn", F32, "gdn_in_dw")
    dh = matmul(dproj, w_in, "nt", F32, "gdn_in_dx")
    return dh, dict(w_in=d_w_in, conv=d_conv, alog=d_alog, dtb=d_dtb, gain=d_gain, w_out=d_w_out)


QB = DSW_SPAN
N_HP = DSW_HG // LANES
PROJ_BLKS = 3 * 3 * N_HP


def _bucket_maps():
    a = np.arange(QB)[:, None]
    j = np.arange(2 * QB)[None, :]
    dist = QB + a - j
    band = (dist >= 0) & (dist <= DSW_SPAN)
    maps = []
    for _, dil in DSW_GROUPS:
        dd = np.maximum(dist, 0) * dil
        max_exact = REL_BUCKETS // 2
        scaled = np.log(np.maximum(dd, 1).astype(np.float32) / np.float32(max_exact)) / np.float32(math.log(REL_MAX_DIST / max_exact))
        large = max_exact + (scaled * np.float32(REL_BUCKETS - max_exact)).astype(np.int32)
        large = np.minimum(large, REL_BUCKETS - 1)
        maps.append(np.where(dd < max_exact, dd, large).astype(np.int32))
    return np.stack(maps), band


def dsw_bias(rel_bias):
    maps, band = _bucket_maps()
    maps = np.where(band[None], maps, -1).astype(np.int32)

    def body(tab_ref, bk_ref, o_ref):
        gh = pl.program_id(0)
        bk = bk_ref[...]
        acc = jnp.full(bk.shape, NEG, F32)
        for b in range(REL_BUCKETS):
            acc = jnp.where(bk == b, tab_ref[b, gh], acc)
        o_ref[...] = acc

    return pl.pallas_call(
        body, name="dsw_bias", grid=(3 * GDN_H,),
        in_specs=[pl.BlockSpec(memory_space=pltpu.SMEM),
                  pl.BlockSpec((None, QB, 2 * QB), lambda gh: (gh // GDN_H, 0, 0))],
        out_specs=pl.BlockSpec((None, QB, 2 * QB), lambda gh: (gh, 0, 0)),
        out_shape=SDS((3 * GDN_H, QB, 2 * QB), F32),
        compiler_params=_cp(1),
    )(rel_bias, jnp.asarray(maps))


def dsw_bias_grad(dbias):
    maps, band = _bucket_maps()
    maps = np.where(band[None], maps, -1).astype(np.int32)

    def body(d_ref, bk_ref, o_ref):
        bk = bk_ref[...]
        d = d_ref[...]
        rows = lax.broadcasted_iota(jnp.int32, (REL_BUCKETS, LANES), 0)
        acc = jnp.zeros((REL_BUCKETS, LANES), F32)
        for b in range(REL_BUCKETS):
            part = jnp.sum(jnp.where(bk == b, d, 0.0), axis=0, keepdims=True)
            val = jnp.sum(part, axis=1, keepdims=True)
            acc = jnp.where(rows == b, val, acc)
        o_ref[...] = acc

    return pl.pallas_call(
        body, name="dsw_bias_grad", grid=(3 * GDN_H,),
        in_specs=[pl.BlockSpec((None, QB, 2 * QB), lambda gh: (gh, 0, 0)),
                  pl.BlockSpec((None, QB, 2 * QB), lambda gh: (gh // GDN_H, 0, 0))],
        out_specs=pl.BlockSpec((None, REL_BUCKETS, LANES), lambda gh: (gh, 0, 0)),
        out_shape=SDS((3 * GDN_H, REL_BUCKETS, LANES), F32),
        compiler_params=_cp(1),
    )(dbias, jnp.asarray(maps))


def _nt(a, b):
    return lax.dot_general(a, b, (((1,), (1,)), ((), ())), preferred_element_type=F32)


def _tn(a, b):
    return lax.dot_general(a, b, (((0,), (0,)), ((), ())), preferred_element_type=F32)


def dsw_group_fwd(qn, kn, proj, bias, gi, S):
    dil = DSW_GROUPS[gi][1]
    sd = S // dil
    nq = sd // QB
    qv = qn.reshape(sd, dil * 3 * DSW_HG)
    kv = kn.reshape(sd, dil * 3 * DSW_HG)
    pv = proj.reshape(sd, dil * 9 * DSW_HG)
    qk_col = lambda hp, r: r * (3 * N_HP) + gi * N_HP + hp
    v_col = lambda hp, r: r * PROJ_BLKS + 2 * 3 * N_HP + gi * N_HP + hp

    def body(q_ref, kp_ref, kc_ref, vp_ref, vc_ref, b_ref, o_ref, l_ref):
        i = pl.program_id(2)
        q = q_ref[...]
        k2 = jnp.concatenate([kp_ref[...], kc_ref[...]], axis=0)
        v2 = jnp.concatenate([vp_ref[...], vc_ref[...]], axis=0).astype(BF16)
        lane_q = lax.broadcasted_iota(jnp.int32, (QB, LANES), 1) < DSW_DH
        lane_k = lax.broadcasted_iota(jnp.int32, (2 * QB, LANES), 1) < DSW_DH
        col = lax.broadcasted_iota(jnp.int32, (QB, 2 * QB), 1)
        first = jnp.logical_and(i == 0, col < QB)
        o_acc = jnp.zeros((QB, LANES), F32)
        lse_b = jnp.zeros((QB, LANES), F32)
        for hh in range(2):
            mq = lane_q if hh == 0 else jnp.logical_not(lane_q)
            mk = lane_k if hh == 0 else jnp.logical_not(lane_k)
            s = _nt(jnp.where(mq, q, 0).astype(BF16), k2) + b_ref[hh]
            s = jnp.where(first, NEG, s)
            mx = jnp.max(s, axis=1, keepdims=True)
            p = jnp.exp(s - mx)
            l = jnp.sum(p, axis=1, keepdims=True)
            oh = jnp.dot(p.astype(BF16), jnp.where(mk, v2, 0).astype(BF16), preferred_element_type=F32) / l
            o_acc = o_acc + oh
            lse_b = jnp.where(mq, mx + jnp.log(l), lse_b)
        o_ref[...] = o_acc
        l_ref[...] = lse_b

    blk = (QB, LANES)
    out_spec = pl.BlockSpec(blk, lambda hp, r, i: (i, r * N_HP + hp))
    o, lse = pl.pallas_call(
        body, name=f"dsw_fwd_g{gi}", grid=(N_HP, dil, nq),
        in_specs=[pl.BlockSpec(blk, lambda hp, r, i: (i, qk_col(hp, r))),
                  pl.BlockSpec(blk, lambda hp, r, i: (jnp.maximum(i - 1, 0), qk_col(hp, r))),
                  pl.BlockSpec(blk, lambda hp, r, i: (i, qk_col(hp, r))),
                  pl.BlockSpec(blk, lambda hp, r, i: (jnp.maximum(i - 1, 0), v_col(hp, r))),
                  pl.BlockSpec(blk, lambda hp, r, i: (i, v_col(hp, r))),
                  pl.BlockSpec((2, QB, 2 * QB), lambda hp, r, i: (gi * N_HP + hp, 0, 0))],
        out_specs=[out_spec, out_spec],
        out_shape=[SDS((sd, dil * DSW_HG), F32)] * 2,
        compiler_params=_cp(3),
    )(qv, kv, kv, pv, pv, bias)
    return o.reshape(S, DSW_HG), lse.reshape(S, DSW_HG)


def dsw_group_bwd(qn, kn, proj, bias, do, o, lse, gi, S):
    dil = DSW_GROUPS[gi][1]
    sd = S // dil
    nq = sd // QB
    qv = qn.reshape(sd, dil * 3 * DSW_HG)
    kv = kn.reshape(sd, dil * 3 * DSW_HG)
    pv = proj.reshape(sd, dil * 9 * DSW_HG)
    dov = do.reshape(sd, dil * DSW_HG)
    ov = o.reshape(sd, dil * DSW_HG)
    lv = lse.reshape(sd, dil * DSW_HG)
    qk_col = lambda hp, r: r * (3 * N_HP) + gi * N_HP + hp
    v_col = lambda hp, r: r * PROJ_BLKS + 2 * 3 * N_HP + gi * N_HP + hp
    o_col = lambda hp, r: r * N_HP + hp
    cur = lambda i: jnp.minimum(i, nq - 1)
    prev = lambda i: jnp.maximum(jnp.minimum(i, nq - 1) - 1, 0)
    done = lambda i: jnp.maximum(i - 1, 0)

    def body(q_ref, kp_ref, kc_ref, vp_ref, vc_ref, b_ref, do_ref, o_ref, l_ref,
             dq_ref, dk_ref, dv_ref, db_ref, dk_scr, dv_scr):
        r, i = pl.program_id(1), pl.program_id(2)

        @pl.when(jnp.logical_and(r == 0, i == 0))
        def _():
            db_ref[...] = jnp.zeros_like(db_ref)

        @pl.when(i == 0)
        def _():
            dk_scr[...] = jnp.zeros_like(dk_scr)
            dv_scr[...] = jnp.zeros_like(dv_scr)

        @pl.when(i < nq)
        def _():
            q = q_ref[...]
            k2 = jnp.concatenate([kp_ref[...], kc_ref[...]], axis=0)
            v2 = jnp.concatenate([vp_ref[...], vc_ref[...]], axis=0).astype(BF16)
            dout = do_ref[...].astype(F32)
            prod = dout * o_ref[...].astype(F32)
            lse_b = l_ref[...]
            lane_q = lax.broadcasted_iota(jnp.int32, (QB, LANES), 1) < DSW_DH
            col = lax.broadcasted_iota(jnp.int32, (QB, 2 * QB), 1)
            first = jnp.logical_and(i == 0, col < QB)
            dq = jnp.zeros((QB, LANES), F32)
            dk2 = jnp.zeros((2 * QB, LANES), F32)
            dv2 = jnp.zeros((2 * QB, LANES), F32)
            for hh in range(2):
                mq = lane_q if hh == 0 else jnp.logical_not(lane_q)
                qm = jnp.where(mq, q, 0).astype(BF16)
                dom = jnp.where(mq, dout, 0.0).astype(BF16)
                s = _nt(qm, k2) + b_ref[hh]
                s = jnp.where(first, NEG, s)
                lse_h = jnp.max(jnp.where(mq, lse_b, NEG), axis=1, keepdims=True)
                p = jnp.exp(s - lse_h)
                delta = jnp.sum(jnp.where(mq, prod, 0.0), axis=1, keepdims=True)
                dp = _nt(dom, v2)
                ds = p * (dp - delta)
                dsb = ds.astype(BF16)
                dq = dq + jnp.where(mq, jnp.dot(dsb, k2, preferred_element_type=F32), 0.0)
                dk2 = dk2 + _tn(dsb, qm)
                dv2 = dv2 + _tn(p.astype(BF16), dom)
                db_ref[hh] += ds
            dq_ref[...] = dq
            dk_ref[...] = dk_scr[...] + dk2[:QB]
            dv_ref[...] = (dv_scr[...] + dv2[:QB]).astype(dv_ref.dtype)
            dk_scr[...] = dk2[QB:]
            dv_scr[...] = dv2[QB:]

        @pl.when(i == nq)
        def _():
            dk_ref[...] = dk_scr[...]
            dv_ref[...] = dv_scr[...].astype(dv_ref.dtype)

    blk = (QB, LANES)
    dq, dk, dv, dbias = pl.pallas_call(
        body, name=f"dsw_bwd_g{gi}", grid=(N_HP, dil, nq + 1),
        in_specs=[pl.BlockSpec(blk, lambda hp, r, i: (cur(i), qk_col(hp, r))),
                  pl.BlockSpec(blk, lambda hp, r, i: (prev(i), qk_col(hp, r))),
                  pl.BlockSpec(blk, lambda hp, r, i: (cur(i), qk_col(hp, r))),
                  pl.BlockSpec(blk, lambda hp, r, i: (prev(i), v_col(hp, r))),
                  pl.BlockSpec(blk, lambda hp, r, i: (cur(i), v_col(hp, r))),
                  pl.BlockSpec((2, QB, 2 * QB), lambda hp, r, i: (gi * N_HP + hp, 0, 0)),
                  pl.BlockSpec(blk, lambda hp, r, i: (cur(i), o_col(hp, r))),
                  pl.BlockSpec(blk, lambda hp, r, i: (cur(i), o_col(hp, r))),
                  pl.BlockSpec(blk, lambda hp, r, i: (cur(i), o_col(hp, r)))],
        out_specs=[pl.BlockSpec(blk, lambda hp, r, i: (cur(i), o_col(hp, r))),
                   pl.BlockSpec(blk, lambda hp, r, i: (done(i), o_col(hp, r))),
                   pl.BlockSpec(blk, lambda hp, r, i: (done(i), o_col(hp, r))),
                   pl.BlockSpec((2, QB, 2 * QB), lambda hp, r, i: (hp, 0, 0))],
        out_shape=[SDS((sd, dil * DSW_HG), F32), SDS((sd, dil * DSW_HG), F32), SDS((sd, dil * DSW_HG), BF16),
                   SDS((GDN_H, QB, 2 * QB), F32)],
        scratch_shapes=[pltpu.VMEM(blk, F32), pltpu.VMEM(blk, F32)],
        compiler_params=_cp(3),
    )(qv, kv, kv, pv, pv, bias, dov, ov, lv)
    return dq.reshape(S, DSW_HG), dk.reshape(S, DSW_HG), dv.reshape(S, DSW_HG), dbias


def dsw_forward(h, w_in, q_gain2, k_gain2, rel_bias, w_out):
    S = h.shape[0]
    nt = S // RT
    nb = 3 * N_HP
    proj = matmul(h, w_in, "nn", F32, "dsw_in")
    width = nb * LANES
    lanes12 = [LANES] * nb
    (qn,) = rowwise(f_qnorm, [Row(proj, (RT, width), lambda i: (i, 0), splits=lanes12)], [q_gain2],
                    [Out((S, width), BF16, (RT, width), lambda i: (i, 0), splits=lanes12)], (nt,), "dsw_qnorm")
    (kn,) = rowwise(f_qknorm, [Row(proj, (RT, width), lambda i: (i, 1), splits=lanes12)], [k_gain2],
                    [Out((S, width), BF16, (RT, width), lambda i: (i, 0), splits=lanes12)], (nt,), "dsw_knorm")
    bias = dsw_bias(rel_bias)
    os_, ls_ = [], []
    for gi in range(3):
        o, l = dsw_group_fwd(qn, kn, proj, bias, gi, S)
        os_.append(o)
        ls_.append(l)
    full = lambda a: Row(a, (RT, DSW_HG), lambda i: (i, 0))
    o, lse = rowwise(f_combine, [full(a) for a in os_ + ls_], [],
                     [Out((S, DSW_HG), BF16, (RT, DSW_HG), lambda i: (i, 0)), Out((S, DSW_HG), F32, (RT, DSW_HG), lambda i: (i, 0))],
                     (nt,), "dsw_combine")
    y = matmul(o, w_out, "nn", F32, "dsw_out")
    return y, dict(h=h, proj=proj, qn=qn, kn=kn, bias=bias, o=o, lse=lse)


def dsw_backward(dy, sv, w_in, q_gain2, k_gain2, w_out):
    S = dy.shape[0]
    nt = S // RT
    nb = 3 * N_HP
    do = matmul(dy, w_out, "nt", BF16, "dsw_out_dx")
    d_w_out = matmul(sv["o"], dy, "tn", F32, "dsw_out_dw")
    pieces_q, pieces_k, pieces_v, dbs = [], [], [], []
    d_qg = jnp.zeros((1, LANES), F32)
    d_kg = jnp.zeros((1, LANES), F32)
    for gi in range(3):
        dq, dk, dv, db = dsw_group_bwd(sv["qn"], sv["kn"], sv["proj"], sv["bias"], do, sv["o"], sv["lse"], gi, S)
        dbs.append(db)
        pieces_v.append(dv)
        for which, dd in ((0, dq), (1, dk)):
            lanes4 = [LANES] * N_HP
            row = Row(sv["proj"], (RT, DSW_HG), lambda i, _o=which * 3 + gi: (i, _o), splits=lanes4,
                      gdtype=BF16, gshape=(S, DSW_HG), gimap=lambda i: (i, 0))
            fn, gain = (f_qnorm, q_gain2) if which == 0 else (f_qknorm, k_gain2)
            (dx,), (dg,) = rowwise_bwd(fn, [row], [gain], [Row(dd, (RT, DSW_HG), lambda i: (i, 0), splits=lanes4)],
                                       (nt,), f"dsw_norm_bwd_{which}{gi}")
            if which == 0:
                pieces_q.append(dx)
                d_qg = d_qg + dg
            else:
                pieces_k.append(dx)
                d_kg = d_kg + dg
    dproj = jnp.concatenate(pieces_q + pieces_k + pieces_v, axis=1)
    d_w_in = matmul(sv["h"], dproj, "tn", F32, "dsw_in_dw")
    dh = matmul(dproj, w_in, "nt", F32, "dsw_in_dx")
    d_rel = dsw_bias_grad(jnp.concatenate(dbs, axis=0))
    return dh, dict(w_in=d_w_in, q_gain2=d_qg, k_gain2=d_kg, rel=d_rel, w_out=d_w_out)


N_LB = DSW_HG // LANES
HALF = DSW_DH // 2


def _lanes(j):
    return slice(LANES * j, LANES * (j + 1))


def _deinterleave(stage, out_ref, dil, rows, dtype):
    for r in range(dil):
        for j in range(N_LB):
            out_ref[r, :, _lanes(j)] = stage[j, pl.ds(r, rows, stride=dil), :].astype(dtype)


def _interleave(in_ref, stage, dil, rows):
    for r in range(dil):
        for j in range(N_LB):
            stage[j, pl.ds(r, rows, stride=dil), :] = in_ref[r, :, _lanes(j)].astype(F32)


def dsw_prep(proj, q_gain2, k_gain2, gi, S):
    dil = DSW_GROUPS[gi][1]
    nt, rows = S // RT, RT // dil

    def body(q_ref, k_ref, v_ref, qg_ref, kg_ref, qo_ref, ko_ref, vo_ref, stage):
        for src, gain_ref, scale, dst in ((q_ref, qg_ref, DSW_DH ** -0.5, qo_ref), (k_ref, kg_ref, 1.0, ko_ref), (v_ref, None, None, vo_ref)):
            for j in range(N_LB):
                val = src[:, _lanes(j)]
                stage[j] = val if gain_ref is None else _qknorm1(val, gain_ref[...], scale)
            _deinterleave(stage, dst, dil, rows, BF16)

    col = lambda which: pl.BlockSpec((RT, DSW_HG), lambda i, _c=which * 3 + gi: (i, _c))
    gspec = pl.BlockSpec((1, LANES), lambda i: (0, 0))
    ospec = pl.BlockSpec((dil, rows, DSW_HG), lambda i: (0, i, 0))
    return pl.pallas_call(
        body, name=f"dsw_prep_g{gi}", grid=(nt,),
        in_specs=[col(0), col(1), col(2), gspec, gspec], out_specs=[ospec] * 3,
        out_shape=[SDS((dil, S // dil, DSW_HG), BF16)] * 3,
        scratch_shapes=[pltpu.VMEM((N_LB, RT, LANES), F32)], compiler_params=_cp(1),
    )(proj, proj, proj, q_gain2, k_gain2)


def dsw_prep_bwd(proj, q_gain2, k_gain2, dqd, dkd, dvd, gi, S):
    dil = DSW_GROUPS[gi][1]
    nt, rows = S // RT, RT // dil

    def body(q_ref, k_ref, qg_ref, kg_ref, dq_ref, dk_ref, dv_ref, oq_ref, ok_ref, ov_ref, dqg_ref, dkg_ref, stage):
        i = pl.program_id(0)

        @pl.when(i == 0)
        def _():
            dqg_ref[...] = jnp.zeros_like(dqg_ref)
            dkg_ref[...] = jnp.zeros_like(dkg_ref)

        for src, gain_ref, scale, cot_ref, dst, dg_ref in ((q_ref, qg_ref, DSW_DH ** -0.5, dq_ref, oq_ref, dqg_ref),
                                                          (k_ref, kg_ref, 1.0, dk_ref, ok_ref, dkg_ref)):
            _interleave(cot_ref, stage, dil, rows)
            for j in range(N_LB):
                _, vjp = jax.vjp(lambda x, g, _s=scale: _qknorm1(x, g, _s), src[:, _lanes(j)], gain_ref[...])
                dx, dg = vjp(stage[j])
                dst[:, _lanes(j)] = dx.astype(dst.dtype)
                dg_ref[...] += dg
        _interleave(dv_ref, stage, dil, rows)
        for j in range(N_LB):
            ov_ref[:, _lanes(j)] = stage[j].astype(ov_ref.dtype)

    col = lambda which: pl.BlockSpec((RT, DSW_HG), lambda i, _c=which * 3 + gi: (i, _c))
    gspec = pl.BlockSpec((1, LANES), lambda i: (0, 0))
    dspec = pl.BlockSpec((dil, rows, DSW_HG), lambda i: (0, i, 0))
    nspec = pl.BlockSpec((RT, DSW_HG), lambda i: (i, 0))
    return pl.pallas_call(
        body, name=f"dsw_prep_bwd_g{gi}", grid=(nt,),
        in_specs=[col(0), col(1), gspec, gspec, dspec, dspec, dspec], out_specs=[nspec] * 3 + [gspec] * 2,
        out_shape=[SDS((S, DSW_HG), BF16)] * 3 + [SDS((1, LANES), F32)] * 2,
        scratch_shapes=[pltpu.VMEM((N_LB, RT, LANES), F32)], compiler_params=_cp(1),
    )(proj, proj, q_gain2, k_gain2, dqd, dkd, dvd)


def _head_masks(rows):
    lane = lax.broadcasted_iota(jnp.int32, (rows, LANES), 1)
    return lane < DSW_DH, (lane % DSW_DH) < HALF


def dsw_attn_fwd(qd, kd, vd, bias, gi, S):
    dil = DSW_GROUPS[gi][1]
    sd = S // dil
    nq = sd // QB

    def body(q_ref, k_ref, v_ref, b_ref, o_ref, l_ref, kp_scr, vp_scr):
        i = pl.program_id(1)

        @pl.when(i == 0)
        def _():
            kp_scr[...] = jnp.zeros_like(kp_scr)
            vp_scr[...] = jnp.zeros_like(vp_scr)

        lo_q, _ = _head_masks(QB)
        lo_k, _ = _head_masks(2 * QB)
        col = lax.broadcasted_iota(jnp.int32, (QB, 2 * QB), 1)
        first = jnp.logical_and(i == 0, col < QB)
        for hp in range(N_HP):
            q = q_ref[:, _lanes(hp)]
            k2 = jnp.concatenate([kp_scr[:, _lanes(hp)], k_ref[:, _lanes(hp)]], axis=0)
            v2 = jnp.concatenate([vp_scr[:, _lanes(hp)], v_ref[:, _lanes(hp)]], axis=0)
            o_acc = jnp.zeros((QB, LANES), F32)
            lse_b = jnp.zeros((QB, LANES), F32)
            for hh in range(2):
                mq = lo_q if hh == 0 else jnp.logical_not(lo_q)
                mk = lo_k if hh == 0 else jnp.logical_not(lo_k)
                s = _nt(jnp.where(mq, q, 0).astype(BF16), k2) + b_ref[2 * hp + hh]
                s = jnp.where(first, NEG, s)
                mx = jnp.max(s, axis=1, keepdims=True)
                p = jnp.exp(s - mx)
                l = jnp.sum(p, axis=1, keepdims=True)
                oh = jnp.dot(p.astype(BF16), jnp.where(mk, v2, 0).astype(BF16), preferred_element_type=F32) / l
                o_acc = o_acc + oh
                lse_b = jnp.where(mq, mx + jnp.log(l), lse_b)
            o_ref[:, _lanes(hp)] = o_acc
            l_ref[:, _lanes(hp)] = lse_b
        kp_scr[...] = k_ref[...]
        vp_scr[...] = v_ref[...]

    blk = pl.BlockSpec((None, QB, DSW_HG), lambda r, i: (r, i, 0))
    return pl.pallas_call(
        body, name=f"dsw_attn_g{gi}", grid=(dil, nq),
        in_specs=[blk, blk, blk, pl.BlockSpec((GDN_H, QB, 2 * QB), lambda r, i: (gi, 0, 0))],
        out_specs=[blk, blk], out_shape=[SDS((dil, sd, DSW_HG), F32)] * 2,
        scratch_shapes=[pltpu.VMEM((QB, DSW_HG), BF16)] * 2, compiler_params=_cp(2),
    )(qd, kd, vd, bias)


def dsw_attn_bwd(qd, kd, vd, bias, dod, statd, gi, S):
    dil = DSW_GROUPS[gi][1]
    sd = S // dil
    nq = sd // QB
    cur = lambda i: jnp.minimum(i, nq - 1)
    done = lambda i: jnp.maximum(i - 1, 0)

    def body(q_ref, k_ref, v_ref, b_ref, do_ref, st_ref, dq_ref, dk_ref, dv_ref, db_ref, kp_scr, vp_scr, dk_scr, dv_scr):
        r, i = pl.program_id(0), pl.program_id(1)

        @pl.when(jnp.logical_and(r == 0, i == 0))
        def _():
            db_ref[...] = jnp.zeros_like(db_ref)

        @pl.when(i == 0)
        def _():
            for scr in (kp_scr, vp_scr, dk_scr, dv_scr):
                scr[...] = jnp.zeros_like(scr)

        @pl.when(i < nq)
        def _():
            lo_q, first_half = _head_masks(QB)
            col = lax.broadcasted_iota(jnp.int32, (QB, 2 * QB), 1)
            first = jnp.logical_and(i == 0, col < QB)
            for hp in range(N_HP):
                q = q_ref[:, _lanes(hp)]
                k2 = jnp.concatenate([kp_scr[:, _lanes(hp)], k_ref[:, _lanes(hp)]], axis=0)
                v2 = jnp.concatenate([vp_scr[:, _lanes(hp)], v_ref[:, _lanes(hp)]], axis=0)
                dout = do_ref[:, _lanes(hp)]
                stat = st_ref[:, _lanes(hp)]
                dq = jnp.zeros((QB, LANES), F32)
                dk2 = jnp.zeros((2 * QB, LANES), F32)
                dv2 = jnp.zeros((2 * QB, LANES), F32)
                for hh in range(2):
                    mq = lo_q if hh == 0 else jnp.logical_not(lo_q)
                    qm = jnp.where(mq, q, 0).astype(BF16)
                    dom = jnp.where(mq, dout, 0).astype(BF16)
                    s = _nt(qm, k2) + b_ref[2 * hp + hh]
                    s = jnp.where(first, NEG, s)
                    lse_h = jnp.max(jnp.where(jnp.logical_and(mq, first_half), stat, NEG), axis=1, keepdims=True)
                    delta = jnp.max(jnp.where(jnp.logical_and(mq, jnp.logical_not(first_half)), stat, NEG), axis=1, keepdims=True)
                    p = jnp.exp(s - lse_h)
                    ds = p * (_nt(dom, v2) - delta)
                    dsb = ds.astype(BF16)
                    dq = dq + jnp.where(mq, jnp.dot(dsb, k2, preferred_element_type=F32), 0.0)
                    dk2 = dk2 + _tn(dsb, qm)
                    dv2 = dv2 + _tn(p.astype(BF16), dom)
                    db_ref[2 * hp + hh] += ds
                dq_ref[:, _lanes(hp)] = dq
                dk_ref[:, _lanes(hp)] = dk_scr[:, _lanes(hp)] + dk2[:QB]
                dv_ref[:, _lanes(hp)] = (dv_scr[:, _lanes(hp)] + dv2[:QB]).astype(dv_ref.dtype)
                dk_scr[:, _lanes(hp)] = dk2[QB:]
                dv_scr[:, _lanes(hp)] = dv2[QB:]
            kp_scr[...] = k_ref[...]
            vp_scr[...] = v_ref[...]

        @pl.when(i == nq)
        def _():
            dk_ref[...] = dk_scr[...]
            dv_ref[...] = dv_scr[...].astype(dv_ref.dtype)

    blk = pl.BlockSpec((None, QB, DSW_HG), lambda r, i: (r, cur(i), 0))
    oblk = pl.BlockSpec((None, QB, DSW_HG), lambda r, i: (r, done(i), 0))
    return pl.pallas_call(
        body, name=f"dsw_attn_bwd_g{gi}", grid=(dil, nq + 1),
        in_specs=[blk, blk, blk, pl.BlockSpec((GDN_H, QB, 2 * QB), lambda r, i: (gi, 0, 0)), blk, blk],
        out_specs=[blk, oblk, oblk, pl.BlockSpec((GDN_H, QB, 2 * QB), lambda r, i: (0, 0, 0))],
        out_shape=[SDS((dil, sd, DSW_HG), F32), SDS((dil, sd, DSW_HG), F32), SDS((dil, sd, DSW_HG), BF16),
                   SDS((GDN_H, QB, 2 * QB), F32)],
        scratch_shapes=[pltpu.VMEM((QB, DSW_HG), BF16)] * 2 + [pltpu.VMEM((QB, DSW_HG), F32)] * 2,
        compiler_params=_cp(2),
    )(qd, kd, vd, bias, dod, statd)


def dsw_combine(ods, lseds, S):
    nt = S // RT
    dils = [d for _, d in DSW_GROUPS]

    def body(*refs):
        ins, (o_ref, l_ref), stages = refs[:6], refs[6:8], refs[8:]
        for g in range(3):
            _interleave(ins[g], stages[g], dils[g], RT // dils[g])
            _interleave(ins[3 + g], stages[3 + g], dils[g], RT // dils[g])
        for j in range(N_LB):
            o, lse = f_combine(None, *[st[j] for st in stages])
            o_ref[:, _lanes(j)] = o.astype(o_ref.dtype)
            l_ref[:, _lanes(j)] = lse

    dspec = lambda d: pl.BlockSpec((d, RT // d, DSW_HG), lambda i: (0, i, 0))
    nspec = pl.BlockSpec((RT, DSW_HG), lambda i: (i, 0))
    return pl.pallas_call(
        body, name="dsw_combine", grid=(nt,),
        in_specs=[dspec(d) for d in dils] * 2, out_specs=[nspec, nspec],
        out_shape=[SDS((S, DSW_HG), BF16), SDS((S, DSW_HG), F32)],
        scratch_shapes=[pltpu.VMEM((N_LB, RT, LANES), F32)] * 6, compiler_params=_cp(1),
    )(*ods, *lseds)


def dsw_bwd_prep(do, o, lse, S):
    nt = S // RT
    dils = [d for _, d in DSW_GROUPS]

    def body(do_ref, o_ref, l_ref, *rest):
        outs, (st_do, st_stat) = rest[:6], rest[6:]
        lo, first_half = _head_masks(RT)
        for j in range(N_LB):
            dout = do_ref[:, _lanes(j)]
            prod = dout * o_ref[:, _lanes(j)].astype(F32)
            s_all = jnp.sum(prod, axis=1, keepdims=True)
            s_lo = jnp.sum(jnp.where(lo, prod, 0.0), axis=1, keepdims=True)
            delta = jnp.where(lo, s_lo, s_all - s_lo)
            st_do[j] = dout
            st_stat[j] = jnp.where(first_half, l_ref[:, _lanes(j)], delta)
        for g in range(3):
            _deinterleave(st_do, outs[g], dils[g], RT // dils[g], BF16)
            _deinterleave(st_stat, outs[3 + g], dils[g], RT // dils[g], F32)

    nspec = pl.BlockSpec((RT, DSW_HG), lambda i: (i, 0))
    dspec = lambda d: pl.BlockSpec((d, RT // d, DSW_HG), lambda i: (0, i, 0))
    res = pl.pallas_call(
        body, name="dsw_bwd_prep", grid=(nt,),
        in_specs=[nspec] * 3, out_specs=[dspec(d) for d in dils] * 2,
        out_shape=[SDS((d, S // d, DSW_HG), BF16) for d in dils] + [SDS((d, S // d, DSW_HG), F32) for d in dils],
        scratch_shapes=[pltpu.VMEM((N_LB, RT, LANES), F32)] * 2, compiler_params=_cp(1),
    )(do, o, lse)
    return res[:3], res[3:]


def dsw_forward(h, w_in, q_gain2, k_gain2, rel_bias, w_out):
    S = h.shape[0]
    proj = matmul(h, w_in, "nn", F32, "dsw_in", col_shards=N_SHARD)
    bias = dsw_bias(rel_bias)
    qkv, ods, lseds = [], [], []
    for gi in range(3):
        qd, kd, vd = dsw_prep(proj, q_gain2, k_gain2, gi, S)
        od, ld = dsw_attn_fwd(qd, kd, vd, bias, gi, S)
        qkv.append((qd, kd, vd))
        ods.append(od)
        lseds.append(ld)
    o, lse = dsw_combine(ods, lseds, S)
    y = matmul(o, w_out, "nn", F32, "dsw_out", col_shards=N_SHARD)
    return y, dict(h=h, proj=proj, qkv=qkv, bias=bias, o=o, lse=lse)


def dsw_backward(dy, sv, w_in, q_gain2, k_gain2, w_out):
    S = dy.shape[0]
    do = matmul(dy, w_out, "nt", F32, "dsw_out_dx", col_shards=N_SHARD)
    d_w_out = matmul(sv["o"], dy, "tn", F32, "dsw_out_dw", col_shards=N_SHARD)
    dods, statds = dsw_bwd_prep(do, sv["o"], sv["lse"], S)
    pieces_q, pieces_k, pieces_v, dbs = [], [], [], []
    d_qg = jnp.zeros((1, LANES), F32)
    d_kg = jnp.zeros((1, LANES), F32)
    for gi in range(3):
        qd, kd, vd = sv["qkv"][gi]
        dqd, dkd, dvd, db = dsw_attn_bwd(qd, kd, vd, sv["bias"], dods[gi], statds[gi], gi, S)
        dq, dk, dv, dqg, dkg = dsw_prep_bwd(sv["proj"], q_gain2, k_gain2, dqd, dkd, dvd, gi, S)
        dbs.append(db)
        pieces_q.append(dq)
        pieces_k.append(dk)
        pieces_v.append(dv)
        d_qg = d_qg + dqg
        d_kg = d_kg + dkg
    dproj = jnp.concatenate(pieces_q + pieces_k + pieces_v, axis=1)
    d_w_in = matmul(sv["h"], dproj, "tn", F32, "dsw_in_dw", col_shards=N_SHARD)
    dh = matmul(dproj, w_in, "nt", F32, "dsw_in_dx", col_shards=N_SHARD)
    d_rel = dsw_bias_grad(jnp.concatenate(dbs, axis=0))
    return dh, dict(w_in=d_w_in, q_gain2=d_qg, k_gain2=d_kg, rel=d_rel, w_out=d_w_out)


FT = 128


def ffn_forward(h, w_in, w_out, tag):
    S = h.shape[0]
    gu = matmul(h, w_in, "nn", BF16, f"ffn_in_{tag}", col_shards=N_SHARD)
    gu_row = Row(gu, (FT, 2 * FFN), lambda i: (i, 0), splits=[FFN, FFN], gdtype=BF16)
    (a,) = rowwise(f_swiglu, [gu_row], [], [Out((S, FFN), BF16, (FT, FFN), lambda i: (i, 0))], (S // FT,), f"ffn_act_{tag}")
    f = matmul(a, w_out, "nn", F32, f"ffn_out_{tag}")
    return f, dict(h=h, gu_row=gu_row, a=a)


def ffn_backward(df, sv, w_in, w_out, tag):
    S = df.shape[0]
    da = matmul(df, w_out, "nt", BF16, f"ffn_out_dx_{tag}")
    d_w_out = matmul(sv["a"], df, "tn", F32, f"ffn_out_dw_{tag}")
    (dgu,), _ = rowwise_bwd(f_swiglu, [sv["gu_row"]], [], [Row(da, (FT, FFN), lambda i: (i, 0))], (S // FT,), f"ffn_act_bwd_{tag}")
    d_w_in = matmul(sv["h"], dgu, "tn", F32, f"ffn_in_dw_{tag}", col_shards=N_SHARD)
    dh = matmul(dgu, w_in, "nt", F32, f"ffn_in_dx_{tag}", col_shards=N_SHARD)
    return dh, d_w_in, d_w_out


def f_norm_only(ids, x, gain, sc, sh):
    return (_normmod(x, gain, sc, sh),)


def _wide(a, **kw):
    return Row(a, (RT, D), lambda i: (i, 0), **kw)


def _wide_out(S, dtype):
    return Out((S, D), dtype, (RT, D), lambda i: (i, 0))


def adamw(w, g, m, v, name):
    shape = w.shape
    C = shape[-1]
    R = int(np.prod(shape[:-1]))
    w2, g2, m2, v2 = (a.reshape(R, C) for a in (w, g, m, v))
    br = R
    if R > 256:
        br = max(b for b in range(8, 257, 8) if R % b == 0)
    c1 = 1.0 / (1.0 - ADAM_B1 ** ADAM_STEP)
    c2 = 1.0 / (1.0 - ADAM_B2 ** ADAM_STEP)

    def body(w_ref, g_ref, m_ref, v_ref, d_ref, nm_ref, nv_ref):
        gg = g_ref[...]
        mm_ = ADAM_B1 * m_ref[...] + (1.0 - ADAM_B1) * gg
        vv = ADAM_B2 * v_ref[...] + (1.0 - ADAM_B2) * (gg * gg)
        d_ref[...] = -ADAM_LR * ((mm_ * c1) / (jnp.sqrt(vv * c2) + ADAM_EPS) + ADAM_WD * w_ref[...])
        nm_ref[...] = mm_
        nv_ref[...] = vv

    spec = pl.BlockSpec((br, C), lambda i: (i, 0))
    d, nm, nv = pl.pallas_call(
        body, name=name, grid=(R // br,), in_specs=[spec] * 4, out_specs=[spec] * 3,
        out_shape=[SDS((R, C), F32)] * 3, compiler_params=_cp(1),
    )(w2, g2, m2, v2)
    return d.reshape(shape), nm.reshape(shape), nv.reshape(shape)


def _place():
    x, y, c = lax.axis_index("x"), lax.axis_index("y"), lax.axis_index("c")
    chips = [(1 - x, y), (x, 1 - y), (1 - x, 1 - y)]
    return x, y, c, chips


def all_gather_small(blk, name):
    m_per, n = blk.shape

    def body(x_ref, out_ref, send_sems, recv_sems, local_sem):
        x, y, c, chips = _place()
        me, sibling = (x, y, c), (x, y, 1 - c)

        def rows(px, py, pc):
            return out_ref.at[pl.ds((4 * px + 2 * py + pc) * m_per, m_per), :]

        def copy(k, block, to, src=None):
            return pltpu.make_async_remote_copy(
                src_ref=rows(*block) if src is None else src, dst_ref=rows(*block),
                send_sem=send_sems.at[k], recv_sem=recv_sems.at[k], device_id=to, device_id_type=MESH)

        mine = pltpu.make_async_copy(x_ref, rows(*me), local_sem)
        mine.start()
        first = [copy(0, me, sibling, src=x_ref)]
        first += [copy(1 + j, me, (*chip, c), src=x_ref) for j, chip in enumerate(chips)]
        for cp in first:
            cp.start()
        passed = [copy(4 + j, (*chip, c), sibling) for j, chip in enumerate(chips)]
        for j, chip in enumerate(chips):
            copy(1 + j, (*chip, c), me).wait_recv()
            passed[j].start()
        copy(0, sibling, me).wait_recv()
        for j, chip in enumerate(chips):
            copy(4 + j, (*chip, 1 - c), me).wait_recv()
        for cp in first + passed:
            cp.wait_send()
        mine.wait()

    return pl.pallas_call(
        body, name=name, out_shape=SDS((N_DEV * m_per, n), blk.dtype),
        in_specs=[pl.BlockSpec(memory_space=pltpu.VMEM)], out_specs=pl.BlockSpec(memory_space=pltpu.VMEM),
        scratch_shapes=[pltpu.SemaphoreType.DMA((7,)), pltpu.SemaphoreType.DMA((7,)), pltpu.SemaphoreType.DMA],
    )(blk)


def _half(cc, rh):
    return pl.ds(pl.multiple_of(cc * rh, 16), rh)


def all_gather_shards(ws):
    n = len(ws)

    def body(*refs):
        w_refs, out_refs = refs[:n], refs[n:2 * n]
        send_sems, recv_sems, local_sems, own_sems = refs[2 * n:]
        x, y, c, chips = _place()
        sibling = (x, y, 1 - c)
        s_me = 2 * x + y

        def copy(k, src, dst, to):
            return pltpu.make_async_remote_copy(src_ref=src, dst_ref=dst, send_sem=send_sems.at[k], recv_sem=recv_sems.at[k],
                                                device_id=to, device_id_type=MESH)

        local, sends, passed = [], [], []
        for k in range(n):
            rh = ws[k].shape[0] // 2
            cp = pltpu.make_async_remote_copy(src_ref=w_refs[k], dst_ref=out_refs[k].at[s_me], send_sem=local_sems.at[k],
                                              recv_sem=own_sems.at[k], device_id=sibling, device_id_type=MESH)
            cp.start()
            local.append(cp)
            for j, chip in enumerate(chips):
                sd = copy(6 * k + j, w_refs[k].at[_half(c, rh)], out_refs[k].at[s_me, _half(c, rh)], (*chip, c))
                sd.start()
                sends.append(sd)
        for k in range(n):
            rh = ws[k].shape[0] // 2
            for j, (px, py) in enumerate(chips):
                got = out_refs[k].at[2 * px + py, _half(c, rh)]
                copy(6 * k + j, got, got, (px, py, c)).wait_recv()
                fw = copy(6 * k + 3 + j, got, got, sibling)
                fw.start()
                passed.append(fw)
        for k in range(n):
            rh = ws[k].shape[0] // 2
            for j, (px, py) in enumerate(chips):
                got = out_refs[k].at[2 * px + py, _half(1 - c, rh)]
                copy(6 * k + 3 + j, got, got, sibling).wait_recv()
        for cp in sends + passed:
            cp.wait_send()
        for cp in local:
            cp.wait()

    return pl.pallas_call(
        body, name="weights_all_gather", out_shape=[SDS((N_SHARD,) + w.shape, w.dtype) for w in ws],
        in_specs=[ANY] * n, out_specs=[ANY] * n,
        scratch_shapes=[pltpu.SemaphoreType.DMA((6 * n,)), pltpu.SemaphoreType.DMA((6 * n,)), pltpu.SemaphoreType.DMA((n,)),
                        pltpu.SemaphoreType.DMA((n,))],
    )(*ws)


def sibling_exchange(sends, name):
    n = len(sends)

    def body(*refs):
        s_refs, o_refs, send_sems, recv_sems = refs[:n], refs[n:2 * n], refs[2 * n], refs[2 * n + 1]
        x, y, c, _ = _place()
        cps = [pltpu.make_async_remote_copy(src_ref=s_refs[k], dst_ref=o_refs[k], send_sem=send_sems.at[k], recv_sem=recv_sems.at[k],
                                            device_id=(x, y, 1 - c), device_id_type=MESH) for k in range(n)]
        for cp in cps:
            cp.start()
        for cp in cps:
            cp.wait()

    return pl.pallas_call(
        body, name=name, out_shape=[SDS(s.shape, s.dtype) for s in sends], in_specs=[ANY] * n, out_specs=[ANY] * n,
        scratch_shapes=[pltpu.SemaphoreType.DMA((n,)), pltpu.SemaphoreType.DMA((n,))],
    )(*sends)


def scatter_to_chips(parts):
    n = len(parts)

    def body(*refs):
        p_refs, o_refs, send_sems, recv_sems = refs[:n], refs[n:2 * n], refs[2 * n], refs[2 * n + 1]
        x, y, c, chips = _place()
        cps = []
        for k in range(n):
            for j, (px, py) in enumerate(chips):
                cp = pltpu.make_async_remote_copy(src_ref=p_refs[k].at[2 * px + py], dst_ref=o_refs[k].at[j],
                                                  send_sem=send_sems.at[3 * k + j], recv_sem=recv_sems.at[3 * k + j],
                                                  device_id=(px, py, c), device_id_type=MESH)
                cp.start()
                cps.append(cp)
        for cp in cps:
            cp.wait()

    return pl.pallas_call(
        body, name="grads_scatter", out_shape=[SDS((3,) + p.shape[1:], p.dtype) for p in parts], in_specs=[ANY] * n, out_specs=[ANY] * n,
        scratch_shapes=[pltpu.SemaphoreType.DMA((3 * n,)), pltpu.SemaphoreType.DMA((3 * n,))],
    )(*parts)


def merge_halves(halves):
    n = len(halves)

    def body(*refs):
        h_refs, o_refs = refs[:n], refs[n:2 * n]
        send_sems, recv_sems, local_sems = refs[2 * n:]
        x, y, c, _ = _place()
        local, cps = [], []
        for k in range(n):
            rh = halves[k].shape[0]
            lc = pltpu.make_async_copy(h_refs[k], o_refs[k].at[_half(c, rh)], local_sems.at[k])
            lc.start()
            local.append(lc)
            cp = pltpu.make_async_remote_copy(src_ref=h_refs[k], dst_ref=o_refs[k].at[_half(c, rh)], send_sem=send_sems.at[k],
                                              recv_sem=recv_sems.at[k], device_id=(x, y, 1 - c), device_id_type=MESH)
            cp.start()
            cps.append(cp)
        for k in range(n):
            rh = halves[k].shape[0]
            got = o_refs[k].at[_half(1 - c, rh)]
            pltpu.make_async_remote_copy(src_ref=got, dst_ref=got, send_sem=send_sems.at[k], recv_sem=recv_sems.at[k],
                                         device_id=(x, y, 1 - c), device_id_type=MESH).wait_recv()
        for cp in cps:
            cp.wait_send()
        for lc in local:
            lc.wait()

    return pl.pallas_call(
        body, name="grads_merge_halves", out_shape=[SDS((2 * h.shape[0], h.shape[1]), h.dtype) for h in halves],
        in_specs=[ANY] * n, out_specs=[ANY] * n,
        scratch_shapes=[pltpu.SemaphoreType.DMA((n,)), pltpu.SemaphoreType.DMA((n,)), pltpu.SemaphoreType.DMA((n,))],
    )(*halves)


def add_rows(arrs, out_dtype, name, rt=256):
    Rr, W = arrs[0].shape

    def fn(ids, *vals):
        acc = vals[0]
        for v in vals[1:]:
            acc = acc + v
        return (acc,)

    t = rt if Rr % rt == 0 else max(b for b in range(16, rt + 1, 16) if Rr % b == 0)
    (out,) = rowwise(fn, [Row(a, (t, W), lambda i: (i, 0)) for a in arrs], [],
                     [Out((Rr, W), out_dtype, (t, W), lambda i: (i, 0))], (Rr // t,), name)
    return out


PACK = (("gdn_w_in", 2), ("gdn_w_out", 1), ("w_ffn_in", 2), ("w_ffn_out", 1), ("dsw_w_in", 2), ("dsw_w_out", 2))
PACK_ALIGN = 32


def _pack_rows(sizes):
    total = sum(sizes)
    rows = -(-total // D)
    return -(-rows // PACK_ALIGN) * PACK_ALIGN


def pack_blocks(blocks, dtype):
    flat = [b.astype(dtype).reshape(-1) for b in blocks]
    total = sum(f.shape[0] for f in flat)
    R = _pack_rows([f.shape[0] for f in flat])
    flat.append(jnp.zeros((R * D - total,), dtype))
    return jnp.concatenate(flat).reshape(R, D)


def unpack_blocks(buf, shapes):
    flat = buf.reshape(-1)
    out, off = [], 0
    for shp in shapes:
        n = int(np.prod(shp))
        out.append(flat[off:off + n].reshape(shp))
        off += n
    return out


def _shard_slice(a, axis, s):
    n = a.shape[axis] // N_SHARD
    return lax.slice_in_dim(a, s * n, (s + 1) * n, axis=axis)


def _pad_lanes(v):
    return jnp.concatenate([v.astype(F32), jnp.zeros((LANES - v.shape[0],), F32)])[None]


def kernel(x, c, w_ada, b_ada, norm_mix, norm_ffn, w_ffn_in, w_ffn_out, gdn_w_in, gdn_conv, gdn_a_log, gdn_dt_bias, gdn_out_norm, gdn_w_out, dsw_w_in, dsw_q_norm, dsw_k_norm, dsw_w_out, rel_bias, loss_target, m_w_ada, m_b_ada, m_norm_mix, m_norm_ffn, m_w_ffn_in, m_w_ffn_out, m_gdn_w_in, m_gdn_conv, m_gdn_a_log, m_gdn_dt_bias, m_gdn_out_norm, m_gdn_w_out, m_dsw_w_in, m_dsw_q_norm, m_dsw_k_norm, m_dsw_w_out, m_rel_bias, v_w_ada, v_b_ada, v_norm_mix, v_norm_ffn, v_w_ffn_in, v_w_ffn_out, v_gdn_w_in, v_gdn_conv, v_gdn_a_log, v_gdn_dt_bias, v_gdn_out_norm, v_gdn_w_out, v_dsw_w_in, v_dsw_q_norm, v_dsw_k_norm, v_dsw_w_out, v_rel_bias):
    S = x.shape[1]
    nt = S // RT
    xi, yi, ci = lax.axis_index("x"), lax.axis_index("y"), lax.axis_index("c")
    me = 4 * xi + 2 * yi + ci
    s_me = 2 * xi + yi
    x0, tgt = x[0], loss_target[0]
    shard = dict(w_ffn_in=w_ffn_in, w_ffn_out=w_ffn_out, gdn_w_in=gdn_w_in, gdn_w_out=gdn_w_out, dsw_w_in=dsw_w_in, dsw_w_out=dsw_w_out)

    whole = lambda a: Row(a, a.shape, lambda i: (0,) * a.ndim)
    (cond8,) = rowwise(lambda ids, v: (_silu(v),), [whole(c.reshape(8, LANES))], [], [Out((8, LANES), F32, (8, LANES), lambda i: (0, 0))], (1,), "cond")
    cond_all = all_gather_small(cond8, "gather_cond").reshape(N_DEV, D)
    cond16 = jnp.concatenate([cond_all, jnp.zeros((8, D), F32)], axis=0)
    ada_cols = w_ada.shape[2]
    mods = [matmul(cond16, w_ada[l], "nn", F32, f"ada_{l}")[:N_DEV] for l in range(2)]
    buf = jnp.concatenate([jnp.stack(mods, axis=1).reshape(-1, LANES), gdn_conv.reshape(-1, LANES)], axis=0)
    n_mod_rows = N_DEV * 2 * ada_cols // LANES
    got = all_gather_small(buf, "gather_mod").reshape(N_DEV, buf.shape[0], LANES)
    mod_parts, conv_parts = [], []
    for s in range(N_SHARD):
        from_dev = got[2 * s]
        mod_parts.append(lax.dynamic_index_in_dim(from_dev[:n_mod_rows].reshape(N_DEV, 2, ada_cols), me, 0, keepdims=False))
        conv_parts.append(from_dev[n_mod_rows:].reshape(4, -1))
    mod_nb = jnp.concatenate(mod_parts, axis=1)
    conv_w = jnp.concatenate(conv_parts, axis=1)
    (mod,) = rowwise(lambda ids, a, b: (a + b,), [whole(mod_nb), whole(b_ada)], [], [Out(mod_nb.shape, F32, mod_nb.shape, lambda i: (0, 0))], (1,), "mod_bias")
    mod = mod.reshape(2, 6, 1, D)
    sh1, sc1, g1, sh2, sc2, g2 = ([mod[l, k] for l in range(2)] for k in range(6))
    gmix = [norm_mix[l][None] for l in range(2)]
    gffn = [norm_ffn[l][None] for l in range(2)]

    gcols = gdn_w_in.shape[2]
    send = [gdn_w_in[0].astype(BF16), gdn_w_out[0].astype(BF16), w_ffn_in[0].astype(BF16), w_ffn_out[0].astype(BF16),
            dsw_w_in[0].astype(BF16), dsw_w_out[0].astype(BF16), w_ffn_in[1].astype(BF16), w_ffn_out[1].astype(BF16)]
    g_gdn_in, g_gdn_out, g_in0, g_out0, g_dsw_in, g_dsw_out, g_in1, g_out1 = all_gather_shards(send)
    w_gdn = jnp.concatenate([g_gdn_in[s] for s in range(N_SHARD)] + [jnp.zeros((D, GDN_PROJ - N_SHARD * gcols), BF16)], axis=1)
    w_ffn = [(g_in0, g_out0.reshape(FFN, D)), (g_in1, g_out1.reshape(FFN, D))]
    alog, dtb = _pad_lanes(gdn_a_log[0]), _pad_lanes(gdn_dt_bias[0])
    qg2 = jnp.concatenate([dsw_q_norm, dsw_q_norm], axis=1)
    kg2 = jnp.concatenate([dsw_k_norm, dsw_k_norm], axis=1)
    gdn_args = (w_gdn, conv_w, alog, dtb, gdn_out_norm, g_gdn_out.reshape(GDN_H * LANES, D))
    dsw_args = (g_dsw_in, qg2, kg2)

    (h10,) = rowwise(f_norm_only, [_wide(x0)], [gmix[0], sc1[0], sh1[0]], [_wide_out(S, BF16)], (nt,), "l0_norm")
    y0, sv_g = gdn_forward(h10, *gdn_args)
    x1, h20 = rowwise(f_resid_norm, [_wide(x0), _wide(y0)], [g1[0], gffn[0], sc2[0], sh2[0]], [_wide_out(S, F32), _wide_out(S, BF16)], (nt,), "l0_mid")
    f0, sv_f0 = ffn_forward(h20, *w_ffn[0], "0")
    x2, h11 = rowwise(f_resid_norm, [_wide(x1), _wide(f0)], [g2[0], gmix[1], sc1[1], sh1[1]], [_wide_out(S, F32), _wide_out(S, BF16)], (nt,), "l1_in")
    y1, sv_d = dsw_forward(h11, *dsw_args, rel_bias, g_dsw_out)
    x3, h21 = rowwise(f_resid_norm, [_wide(x2), _wide(y1)], [g1[1], gffn[1], sc2[1], sh2[1]], [_wide_out(S, F32), _wide_out(S, BF16)], (nt,), "l1_mid")
    f1, sv_f1 = ffn_forward(h21, *w_ffn[1], "1")
    part_spec = lambda a: Row(a, (None, 1, D), lambda i: (i, 0, 0))
    (parts,) = rowwise(f_loss, [_wide(x3), _wide(f1), _wide(tgt)], [g2[1]], [Out((nt, 1, D), F32, (None, 1, D), lambda i: (i, 0, 0))], (nt,), "loss")
    loss = lax.psum(jnp.sum(parts), ("x", "y", "c"))

    (dx3, df1), (dg2_1,) = rowwise_bwd(f_loss, [_wide(x3), _wide(f1, gdtype=BF16), _wide(tgt, diff=False)], [g2[1]],
                                       [part_spec(jnp.ones((nt, 1, D), F32))], (nt,), "loss_bwd")
    dh21, d_win1, d_wout1 = ffn_backward(df1, sv_f1, *w_ffn[1], "1")
    (dx2, dy1), (dg1_1, dgf1, dsc2_1, dsh2_1) = rowwise_bwd(
        f_resid_norm, [_wide(x2), _wide(y1, gdtype=BF16)], [g1[1], gffn[1], sc2[1], sh2[1]], [_wide(dx3), _wide(dh21)], (nt,), "l1_mid_bwd")
    dh11, g_d = dsw_backward(dy1, sv_d, *dsw_args, g_dsw_out)
    (dx1, df0), (dg2_0, dgm1, dsc1_1, dsh1_1) = rowwise_bwd(
        f_resid_norm, [_wide(x1), _wide(f0, gdtype=BF16)], [g2[0], gmix[1], sc1[1], sh1[1]], [_wide(dx2), _wide(dh11)], (nt,), "l1_in_bwd")
    dh20, d_win0, d_wout0 = ffn_backward(df0, sv_f0, *w_ffn[0], "0")
    (dx0p, dy0), (dg1_0, dgf0, dsc2_0, dsh2_0) = rowwise_bwd(
        f_resid_norm, [_wide(x0), _wide(y0, gdtype=BF16)], [g1[0], gffn[0], sc2[0], sh2[0]], [_wide(dx1), _wide(dh20)], (nt,), "l0_mid_bwd")
    dh10, g_g = gdn_backward(dy0, sv_g, *gdn_args)
    (grad_x,), (dgm0, dsc1_0, dsh1_0) = rowwise_bwd(f_first, [_wide(x0)], [gmix[0], sc1[0], sh1[0]], [_wide(dx0p), _wide(dh10)], (nt,), "l0_norm_bwd")

    dmod = jnp.concatenate([dsh1_0, dsc1_0, dg1_0, dsh2_0, dsc2_0, dg2_0, dsh1_1, dsc1_1, dg1_1, dsh2_1, dsc2_1, dg2_1], axis=1)
    d_rel = jnp.transpose(g_d["rel"][:, :, 0])
    fold = lambda v: v[:, :DSW_DH] + v[:, DSW_DH:]
    small = [dmod, jnp.concatenate([dgm0, dgm1], axis=1), jnp.concatenate([dgf0, dgf1], axis=1), g_g["conv"].reshape(1, -1),
             g_g["alog"], g_g["dtb"], g_g["gain"], _pad_lanes(fold(g_d["q_gain2"])[0]), _pad_lanes(fold(g_d["k_gain2"])[0]),
             d_rel.reshape(1, -1)]
    used = [v.shape[1] // LANES for v in small]
    sizes = [-(-u // 8) * 8 for u in used]
    pad8 = lambda v, u, s: jnp.concatenate([v.reshape(u, LANES), jnp.zeros((s - u, LANES), F32)], axis=0) if s > u else v.reshape(u, LANES)
    pad_rows = sum(sizes)
    sbuf = jnp.concatenate([pad8(v, u, s) for v, u, s in zip(small, used, sizes)], axis=0)
    sgot = all_gather_small(sbuf, "gather_small_grads")
    ssum = add_rows([sgot[d * pad_rows:(d + 1) * pad_rows] for d in range(N_DEV)], F32, "sum_small_grads", rt=pad_rows)
    offs = np.cumsum([0] + sizes)
    take = lambda k: ssum[offs[k]:offs[k] + used[k]].reshape(1, -1)
    grad_b_ada = take(0).reshape(2, 6 * D)
    grad_norm_mix = take(1).reshape(2, D)
    grad_norm_ffn = take(2).reshape(2, D)
    conv_full = take(3).reshape(4, -1)
    ncv = gdn_conv.shape[2]
    grad_gdn_conv = lax.dynamic_slice_in_dim(conv_full, s_me * ncv, ncv, axis=1)[None]
    grad_a_log = take(4)[:, :GDN_H]
    grad_dt_bias = take(5)[:, :GDN_H]
    grad_out_norm = take(6)
    grad_q_norm = take(7)[:, :DSW_DH]
    grad_k_norm = take(8)[:, :DSW_DH]
    grad_rel = take(9).reshape(REL_BUCKETS, 3 * GDN_H)
    dmod_all = sgot.reshape(N_DEV, pad_rows, LANES)[:, :used[0]].reshape(N_DEV, 2, 6 * D)
    dmod_mine = lax.dynamic_slice_in_dim(dmod_all, s_me * ada_cols, ada_cols, axis=2)
    dmod16 = jnp.concatenate([dmod_mine, jnp.zeros_like(dmod_mine)], axis=0)
    grad_w_ada = jnp.stack([matmul(cond16, dmod16[:, l], "tn", F32, f"ada_dw_{l}") for l in range(2)])

    dg_in = jnp.stack([g_g["w_in"][:, s * gcols:(s + 1) * gcols] for s in range(N_SHARD)])
    by_shard = lambda a: a.reshape(N_SHARD, a.shape[0] // N_SHARD, a.shape[1])
    dws = [dg_in, by_shard(g_g["w_out"]), d_win0, by_shard(d_wout0), g_d["w_in"], g_d["w_out"], d_win1, by_shard(d_wout1)]
    keeps, gives = [], []
    for a in dws:
        rh = a.shape[1] // 2
        keeps.append(lax.dynamic_slice_in_dim(a, ci * rh, rh, axis=1))
        gives.append(lax.dynamic_slice_in_dim(a, (1 - ci) * rh, rh, axis=1).astype(BF16))
    from_sib = sibling_exchange(gives, "grads_to_sibling")
    flat2 = lambda a: a.reshape(-1, a.shape[-1])
    parts = [add_rows([flat2(k_), flat2(f_)], BF16, f"grads_chip_sum_{i}").reshape(k_.shape) for i, (k_, f_) in enumerate(zip(keeps, from_sib))]
    others = scatter_to_chips(parts)
    halves = []
    for i, (p_, o_) in enumerate(zip(parts, others)):
        own = lax.dynamic_index_in_dim(p_, s_me, 0, keepdims=False)
        halves.append(add_rows([own, o_[0], o_[1], o_[2]], F32, f"grads_sum_{i}"))
    sib_halves = sibling_exchange(halves, "grads_from_sibling")

    def whole_shard(mine, theirs):
        both = jnp.stack([mine, theirs])
        return jnp.concatenate([lax.dynamic_index_in_dim(both, ci, 0, keepdims=False),
                                lax.dynamic_index_in_dim(both, 1 - ci, 0, keepdims=False)], axis=0)

    s_gdn_in, s_gdn_out, s_in0, s_out0, s_dsw_in, s_dsw_out, s_in1, s_out1 = [whole_shard(a, b) for a, b in zip(halves, sib_halves)]
    gsh = dict(gdn_w_in=s_gdn_in[None], gdn_w_out=s_gdn_out[None],
               w_ffn_in=jnp.stack([s_in0, s_in1]), w_ffn_out=jnp.stack([s_out0, s_out1]),
               dsw_w_in=s_dsw_in[None], dsw_w_out=s_dsw_out[None])

    grads = dict(w_ada=grad_w_ada, b_ada=grad_b_ada, norm_mix=grad_norm_mix, norm_ffn=grad_norm_ffn, w_ffn_in=gsh["w_ffn_in"],
                 w_ffn_out=gsh["w_ffn_out"], gdn_w_in=gsh["gdn_w_in"], gdn_conv=grad_gdn_conv, gdn_a_log=grad_a_log,
                 gdn_dt_bias=grad_dt_bias, gdn_out_norm=grad_out_norm, gdn_w_out=gsh["gdn_w_out"], dsw_w_in=gsh["dsw_w_in"],
                 dsw_q_norm=grad_q_norm, dsw_k_norm=grad_k_norm, dsw_w_out=gsh["dsw_w_out"], rel_bias=grad_rel)
    weights = dict(w_ada=w_ada, b_ada=b_ada, norm_mix=norm_mix, norm_ffn=norm_ffn, w_ffn_in=w_ffn_in, w_ffn_out=w_ffn_out,
                   gdn_w_in=gdn_w_in, gdn_conv=gdn_conv, gdn_a_log=gdn_a_log, gdn_dt_bias=gdn_dt_bias, gdn_out_norm=gdn_out_norm,
                   gdn_w_out=gdn_w_out, dsw_w_in=dsw_w_in, dsw_q_norm=dsw_q_norm, dsw_k_norm=dsw_k_norm, dsw_w_out=dsw_w_out,
                   rel_bias=rel_bias)
    ms = dict(w_ada=m_w_ada, b_ada=m_b_ada, norm_mix=m_norm_mix, norm_ffn=m_norm_ffn, w_ffn_in=m_w_ffn_in, w_ffn_out=m_w_ffn_out,
              gdn_w_in=m_gdn_w_in, gdn_conv=m_gdn_conv, gdn_a_log=m_gdn_a_log, gdn_dt_bias=m_gdn_dt_bias, gdn_out_norm=m_gdn_out_norm,
              gdn_w_out=m_gdn_w_out, dsw_w_in=m_dsw_w_in, dsw_q_norm=m_dsw_q_norm, dsw_k_norm=m_dsw_k_norm, dsw_w_out=m_dsw_w_out,
              rel_bias=m_rel_bias)
    vs = dict(w_ada=v_w_ada, b_ada=v_b_ada, norm_mix=v_norm_mix, norm_ffn=v_norm_ffn, w_ffn_in=v_w_ffn_in, w_ffn_out=v_w_ffn_out,
              gdn_w_in=v_gdn_w_in, gdn_conv=v_gdn_conv, gdn_a_log=v_gdn_a_log, gdn_dt_bias=v_gdn_dt_bias, gdn_out_norm=v_gdn_out_norm,
              gdn_w_out=v_gdn_w_out, dsw_w_in=v_dsw_w_in, dsw_q_norm=v_dsw_q_norm, dsw_k_norm=v_dsw_k_norm, dsw_w_out=v_dsw_w_out,
              rel_bias=v_rel_bias)
    names = list(weights)
    deltas, new_m, new_v = [], [], []
    for n in names:
        g = grads[n].reshape(weights[n].shape)
        grads[n] = g
        d, nm, nv = adamw(weights[n], g, ms[n], vs[n], f"adamw_{n}")
        deltas.append(d)
        new_m.append(nm)
        new_v.append(nv)
    return (loss, grad_x[None], *[grads[n] for n in names], *deltas, *new_m, *new_v)
```

```python
import functools
import math

import numpy as np
import jax
import jax.numpy as jnp
from jax import lax
from jax.experimental import pallas as pl
from jax.experimental.pallas import tpu as pltpu

F32 = jnp.float32
BF16 = jnp.bfloat16
SDS = jax.ShapeDtypeStruct
MESH = pl.DeviceIdType.MESH
ANY = pl.BlockSpec(memory_space=pl.ANY)

D = 1024
EPS = 1e-6
LANES = 128
GDN_H = 8
GDN_DK = 128
GDN_C = 64
DSW_GROUPS = ((128, 1), (512, 4), (2048, 16))
DSW_SPAN = 128
DSW_DH = 64
DSW_HG = 512
REL_BUCKETS = 32
REL_MAX_DIST = 2048
FFN = 2816
N_SHARD = 4
N_DEV = 8
VMEM_LIMIT = 48 * 1024 * 1024
NEG = -1e30

ADAM_LR, ADAM_B1, ADAM_B2, ADAM_EPS, ADAM_WD, ADAM_STEP = 0.001, 0.9, 0.999, 1e-08, 0.01, 10


def _cp(n_axes):
    return pltpu.CompilerParams(dimension_semantics=("arbitrary",) * n_axes, vmem_limit_bytes=VMEM_LIMIT)


def _blk(dim, cap):
    if dim <= cap:
        return dim
    best = None
    for b in range(LANES, cap + 1, LANES):
        if dim % b == 0:
            best = b
    assert best is not None, (dim, cap)
    return best


MAX_SHARD_BLOCK = 1408
def matmul(a, b, mode, out_dtype, name, cap_m=1024, cap_n=1024, cap_k=2048, col_shards=0):
    ns = col_shards
    if mode == "nn":
        (M, K) = a.shape
        K2, N = (b.shape[1], ns * b.shape[2]) if ns else b.shape
    elif mode == "nt":
        (M, K) = a.shape
        N, K2 = (b.shape[1], ns * b.shape[2]) if ns else b.shape
    else:
        (K, M), (K2, N) = a.shape, b.shape
    assert K == K2, (a.shape, b.shape, mode)
    if K <= 3072:
        cap_k = K
        if K > 2048:
            cap_n = 512
    n_unit = N // ns if (ns and mode != "nt") else N
    k_unit = K // ns if (ns and mode == "nt") else K
    bm = _blk(M, cap_m)
    bn = _blk(n_unit, MAX_SHARD_BLOCK) if n_unit != N else _blk(N, cap_n)
    if k_unit != K:
        bk = _blk(k_unit, MAX_SHARD_BLOCK)
    else:
        bk = _blk(K, 1024 if (ns and mode == "tn") else cap_k)
    nk = K // bk
    nps, kps = n_unit // bn, k_unit // bk
    dims = {"nn": ((1,), (0,)), "nt": ((1,), (1,)), "tn": ((0,), (0,))}[mode]

    def dot(a_ref, b_ref):
        return lax.dot_general(a_ref[...].astype(BF16), b_ref[...].astype(BF16), (dims, ((), ())), preferred_element_type=F32)

    def body_one(a_ref, b_ref, o_ref):
        o_ref[...] = dot(a_ref, b_ref).astype(o_ref.dtype)

    def body_acc(a_ref, b_ref, o_ref, acc_ref):
        k = pl.program_id(2)

        @pl.when(k == 0)
        def _():
            acc_ref[...] = jnp.zeros_like(acc_ref)

        acc_ref[...] += dot(a_ref, b_ref)

        @pl.when(k == nk - 1)
        def _():
            o_ref[...] = acc_ref[...].astype(o_ref.dtype)

    a_spec = pl.BlockSpec((bk, bm), lambda i, j, k: (k, i)) if mode == "tn" else pl.BlockSpec((bm, bk), lambda i, j, k: (i, k))
    if mode == "nt":
        b_spec = pl.BlockSpec((None, bn, bk), lambda i, j, k: (k // kps, j, k % kps)) if ns else pl.BlockSpec((bn, bk), lambda i, j, k: (j, k))
    elif mode == "nn" and ns:
        b_spec = pl.BlockSpec((None, bk, bn), lambda i, j, k: (j // nps, k, j % nps))
    else:
        b_spec = pl.BlockSpec((bk, bn), lambda i, j, k: (k, j))
    if mode == "tn" and ns:
        o_spec, o_shape = pl.BlockSpec((None, bm, bn), lambda i, j, k: (j // nps, i, j % nps)), (ns, M, n_unit)
    else:
        o_spec, o_shape = pl.BlockSpec((bm, bn), lambda i, j, k: (i, j)), (M, N)
    return pl.pallas_call(
        body_one if nk == 1 else body_acc, name=name, grid=(M // bm, N // bn, nk),
        in_specs=[a_spec, b_spec], out_specs=o_spec,
        out_shape=SDS(o_shape, out_dtype), scratch_shapes=[] if nk == 1 else [pltpu.VMEM((bm, bn), F32)],
        compiler_params=_cp(3),
    )(a, b)


class Row:
    def __init__(self, arr, bshape, imap, splits=None, diff=True, acc=False, gdtype=F32, gshape=None, gbshape=None, gimap=None,
                 lead=0):
        self.arr, self.bshape, self.imap = arr, tuple(bshape), imap
        self.splits, self.lead = splits, lead
        self.diff, self.acc, self.gdtype = diff, acc, gdtype
        self.gshape = tuple(arr.shape) if gshape is None else tuple(gshape)
        self.gbshape = self.bshape if gbshape is None else tuple(gbshape)
        self.gimap = imap if gimap is None else gimap

    def gspec(self):
        return pl.BlockSpec(self.gbshape, self.gimap)

    def spec(self):
        return pl.BlockSpec(self.bshape, self.imap)

    def pieces(self, ref):
        return _load_pieces(ref, self.splits, self.lead)

    def n_pieces(self):
        return _n_pieces(self.splits, self.lead)


class Out:
    def __init__(self, shape, dtype, bshape, imap, splits=None, lead=0):
        self.shape, self.dtype, self.bshape, self.imap = tuple(shape), dtype, tuple(bshape), imap
        self.splits, self.lead = splits, lead

    def n_pieces(self):
        return _n_pieces(self.splits, self.lead)


def _n_pieces(splits, lead):
    return lead if lead else (1 if splits is None else len(splits))


def _load_pieces(ref, splits, lead):
    if lead:
        return [ref[k].astype(F32) for k in range(lead)]
    if splits is None:
        return [ref[...].astype(F32)]
    out, o = [], 0
    for w in splits:
        out.append(ref[..., o:o + w].astype(F32))
        o += w
    return out


def _store_pieces(ref, splits, lead, vals, accumulate=False):
    def put(idx, v):
        if accumulate:
            ref[idx] += v.astype(ref.dtype)
        else:
            ref[idx] = v.astype(ref.dtype)

    if lead:
        for k in range(lead):
            put(k, vals[k])
    elif splits is None:
        put(..., vals[0])
    else:
        o = 0
        for w, v in zip(splits, vals):
            put((..., slice(o, o + w)), v)
            o += w


def rowwise(fn, rows, params, outs, grid, name):
    nr, npar = len(rows), len(params)

    def body(*refs):
        ids = tuple(pl.program_id(a) for a in range(len(grid)))
        vals = []
        for r, ref in zip(rows, refs[:nr]):
            vals += r.pieces(ref)
        pvals = [ref[...].astype(F32) for ref in refs[nr:nr + npar]]
        res = list(fn(ids, *vals, *pvals))
        o = 0
        for spec, ref in zip(outs, refs[nr + npar:]):
            n = spec.n_pieces()
            _store_pieces(ref, spec.splits, spec.lead, res[o:o + n])
            o += n

    nz = len(grid)
    pspecs = [pl.BlockSpec(p.shape, (lambda *ids, _n=p.ndim: (0,) * _n)) for p in params]
    res = pl.pallas_call(
        body, name=name, grid=grid,
        in_specs=[r.spec() for r in rows] + pspecs,
        out_specs=[pl.BlockSpec(o.bshape, o.imap) for o in outs],
        out_shape=[SDS(o.shape, o.dtype) for o in outs],
        compiler_params=_cp(nz),
    )(*[r.arr for r in rows], *params)
    return list(res)


def rowwise_bwd(fn, rows, params, cots, grid, name):
    nr, npar, nc = len(rows), len(params), len(cots)
    drows = [r for r in rows if r.diff]
    nz = len(grid)

    def body(*refs):
        ids = tuple(pl.program_id(a) for a in range(nz))
        row_refs, par_refs = refs[:nr], refs[nr:nr + npar]
        cot_refs = refs[nr + npar:nr + npar + nc]
        drow_refs = refs[nr + npar + nc:nr + npar + nc + len(drows)]
        dpar_refs = refs[nr + npar + nc + len(drows):]
        pieces, is_diff = [], []
        for r, ref in zip(rows, row_refs):
            ps = r.pieces(ref)
            pieces += ps
            is_diff += [r.diff] * len(ps)
        pvals = [ref[...].astype(F32) for ref in par_refs]
        dvals = [p for p, dflag in zip(pieces, is_diff) if dflag]
        nd = len(dvals)

        def f(*args):
            it = iter(args[:nd])
            full = [next(it) if dflag else p for p, dflag in zip(pieces, is_diff)]
            return tuple(fn(ids, *full, *args[nd:]))

        _, vjp = jax.vjp(f, *dvals, *pvals)
        cvals = []
        for c, ref in zip(cots, cot_refs):
            cvals += c.pieces(ref)
        g = vjp(tuple(cvals))
        o = 0
        first_inner = ids[-1] == 0
        for r, ref in zip(drows, drow_refs):
            n = r.n_pieces()
            gs = g[o:o + n]
            o += n
            if r.acc:
                @pl.when(first_inner)
                def _(ref=ref):
                    ref[...] = jnp.zeros_like(ref)
            _store_pieces(ref, r.splits, r.lead, gs, accumulate=r.acc)
        first = functools.reduce(jnp.logical_and, [i == 0 for i in ids])
        for ref, gp in zip(dpar_refs, g[nd:]):
            @pl.when(first)
            def _(ref=ref):
                ref[...] = jnp.zeros_like(ref)
            ref[...] += gp

    pspecs = [pl.BlockSpec(p.shape, (lambda *ids, _n=p.ndim: (0,) * _n)) for p in params]
    res = pl.pallas_call(
        body, name=name, grid=grid,
        in_specs=[r.spec() for r in rows] + pspecs + [c.spec() for c in cots],
        out_specs=[r.gspec() for r in drows] + pspecs,
        out_shape=[SDS(r.gshape, r.gdtype) for r in drows] + [SDS(p.shape, F32) for p in params],
        compiler_params=_cp(nz),
    )(*[r.arr for r in rows], *params, *[c.arr for c in cots])
    res = list(res)
    return res[:len(drows)], res[len(drows):]


def _sigmoid(x):
    return 0.5 * (jnp.tanh(0.5 * x) + 1.0)


def _silu(x):
    return x * _sigmoid(x)


def _normmod(x, gain, sc, sh):
    inv = lax.rsqrt(jnp.mean(x * x, axis=-1, keepdims=True) + EPS)
    return x * inv * gain * (1.0 + sc) + sh


def f_first(ids, x, gain, sc, sh):
    return x, _normmod(x, gain, sc, sh)


def f_resid_norm(ids, x, y, g, gain, sc, sh):
    xn = x + g * y
    return xn, _normmod(xn, gain, sc, sh)


def f_swiglu(ids, gate, up):
    return (_silu(gate) * up,)


def f_loss(ids, x, y, tgt, g):
    out = x + g * y
    e = out - tgt
    part = 0.5 * jnp.sum(e * e, axis=0, keepdims=True) * (1.0 / D)
    return (part,)


def _softplus(x):
    return jnp.maximum(x, 0.0) + jnp.log(1.0 + jnp.exp(-jnp.abs(x)))


def _chunk_tril(T):
    r = lax.broadcasted_iota(jnp.int32, (T, T), 0)
    c = lax.broadcasted_iota(jnp.int32, (T, T), 1)
    return jnp.where((r // GDN_C == c // GDN_C) & (c <= r), 1.0, 0.0).astype(F32)


def _dot_hi(a, b, dims=((1,), (0,))):
    return lax.dot_general(a, b, (dims, ((), ())), precision=lax.Precision.HIGHEST, preferred_element_type=F32)


def _dot_x3(a, b, dims=((1,), (0,))):
    return lax.dot_general(a, b, (dims, ((), ())), precision=lax.Precision.HIGH, preferred_element_type=F32)


def f_gdn_gates(ids, ab, alog, dtb):
    T = ab.shape[0]
    g = -jnp.exp(alog) * _softplus(ab + dtb)
    beta = _sigmoid(ab)
    gcum = _dot_hi(_chunk_tril(T), g)
    row = lax.broadcasted_iota(jnp.int32, (LANES, LANES), 0)
    sel = lambda k: jnp.where(row == k, 1.0, 0.0).astype(F32)
    gcs = [_dot_hi(gcum, sel(h)) for h in range(GDN_H)]
    bts = [_dot_hi(beta, sel(GDN_H + h)) for h in range(GDN_H)]
    return (*gcs, *bts)


def f_gdn_post(ids, *args):
    os_, zs, gain = args[:GDN_H], args[GDN_H:2 * GDN_H], args[2 * GDN_H]
    out = []
    for o, z in zip(os_, zs):
        inv = lax.rsqrt(jnp.mean(o * o, axis=-1, keepdims=True) + EPS)
        out.append(o * inv * gain * _silu(z))
    return tuple(out)


def _qknorm1(x, gain2, scale):
    lane = lax.broadcasted_iota(jnp.int32, x.shape, 1)
    lo = lane < DSW_DH
    x2 = x * x
    s_all = jnp.sum(x2, axis=-1, keepdims=True)
    s_lo = jnp.sum(jnp.where(lo, x2, 0.0), axis=-1, keepdims=True)
    ms = jnp.where(lo, s_lo, s_all - s_lo) * (1.0 / DSW_DH)
    return x * lax.rsqrt(ms + EPS) * (gain2 * scale)


def f_qknorm(ids, *args):
    return tuple(_qknorm1(x, args[-1], 1.0) for x in args[:-1])


def f_qnorm(ids, *args):
    return tuple(_qknorm1(x, args[-1], DSW_DH ** -0.5) for x in args[:-1])


def f_combine(ids, o0, o1, o2, l0, l1, l2):
    m = jnp.maximum(jnp.maximum(l0, l1), l2)
    e0, e1, e2 = jnp.exp(l0 - m), jnp.exp(l1 - m), jnp.exp(l2 - m)
    den = e0 + e1 + e2
    o = (e0 * o0 + e1 * o1 + e2 * o2) / den
    return o, m + jnp.log(den)


GDN_T = 512
HALO = 8


def _conv_pre(xx, w):
    acc = xx * w[3:4, :]
    for j in range(3):
        acc = acc + pltpu.roll(xx, shift=3 - j, axis=0) * w[j:j + 1, :]
    return acc


def _qkv_act(pre, cidx):
    s = _silu(pre)
    r = lax.rsqrt(jnp.sum(s * s, axis=-1, keepdims=True) + EPS)
    scale = jnp.where(cidx < GDN_H, GDN_DK ** -0.5, 1.0).astype(F32)
    return jnp.where(cidx < 2 * GDN_H, s * r * scale, s)


def gdn_pre(proj, conv_w, S):
    nt = S // GDN_T
    hb = GDN_T // HALO

    def body(prev_ref, cur_ref, w_ref, o_ref):
        p, i = pl.program_id(0), pl.program_id(1)
        for h in range(GDN_H):
            cols = slice(LANES * h, LANES * (h + 1))
            prev = jnp.where(i > 0, prev_ref[:, cols], 0.0)
            xx = jnp.concatenate([prev, cur_ref[:, cols]], axis=0)
            pre = _conv_pre(xx, w_ref[:, cols])[HALO:]
            o_ref[h] = _qkv_act(pre, p * GDN_H + h)

    hv = GDN_H * LANES
    return pl.pallas_call(
        body, name="gdn_pre", grid=(3, nt),
        in_specs=[pl.BlockSpec((HALO, hv), lambda p, i: (jnp.maximum(i * hb - 1, 0), p)),
                  pl.BlockSpec((GDN_T, hv), lambda p, i: (i, p)),
                  pl.BlockSpec((4, hv), lambda p, i: (0, p))],
        out_specs=pl.BlockSpec((None, GDN_H, GDN_T, LANES), lambda p, i: (p, 0, i, 0)),
        out_shape=SDS((3, GDN_H, S, LANES), F32),
        compiler_params=_cp(2),
    )(proj, proj, conv_w)


def gdn_pre_bwd(proj, conv_w, dqkv, S):
    nt = S // GDN_T
    hb = GDN_T // HALO
    last_h = S // HALO - 1

    def body(prev_ref, cur_ref, next_ref, w_ref, d_ref, dnext_ref, dx_ref, dw_ref):
        p, i = pl.program_id(0), pl.program_id(1)

        @pl.when(i == 0)
        def _():
            dw_ref[...] = jnp.zeros_like(dw_ref)

        for h in range(GDN_H):
            cols = slice(LANES * h, LANES * (h + 1))
            w = w_ref[:, cols]
            prev = jnp.where(i > 0, prev_ref[:, cols], 0.0)
            xx = jnp.concatenate([prev, cur_ref[:, cols], next_ref[:, cols]], axis=0)
            dnext = jnp.where(i < nt - 1, dnext_ref[h], 0.0)
            dd = jnp.concatenate([jnp.zeros((HALO, LANES), F32), d_ref[h], dnext], axis=0)
            pre = _conv_pre(xx, w)
            _, vjp = jax.vjp(lambda v, _c=p * GDN_H + h: _qkv_act(v, _c), pre)
            (dpre,) = vjp(dd)
            row = lax.broadcasted_iota(jnp.int32, dpre.shape, 0)
            dpre = jnp.where(row >= HALO, dpre, 0.0)
            dx = dpre * w[3:4, :]
            R = dpre.shape[0]
            for j in range(3):
                dx = dx + pltpu.roll(dpre, shift=R - (3 - j), axis=0) * w[j:j + 1, :]
            dx_ref[:, cols] = dx[HALO:HALO + GDN_T].astype(dx_ref.dtype)
            own = jnp.where(row < HALO + GDN_T, dpre, 0.0)
            rows_w = [jnp.sum(own * pltpu.roll(xx, shift=3 - j, axis=0), axis=0, keepdims=True) for j in range(3)]
            rows_w.append(jnp.sum(own * xx, axis=0, keepdims=True))
            r4 = lax.broadcasted_iota(jnp.int32, (4, LANES), 0)
            dw = jnp.zeros((4, LANES), F32)
            for j in range(4):
                dw = dw + jnp.where(r4 == j, rows_w[j], 0.0)
            dw_ref[:, cols] += dw

    hv = GDN_H * LANES
    return pl.pallas_call(
        body, name="gdn_pre_bwd", grid=(3, nt),
        in_specs=[pl.BlockSpec((HALO, hv), lambda p, i: (jnp.maximum(i * hb - 1, 0), p)),
                  pl.BlockSpec((GDN_T, hv), lambda p, i: (i, p)),
                  pl.BlockSpec((HALO, hv), lambda p, i: (jnp.minimum((i + 1) * hb, last_h), p)),
                  pl.BlockSpec((4, hv), lambda p, i: (0, p)),
                  pl.BlockSpec((None, GDN_H, GDN_T, LANES), lambda p, i: (p, 0, i, 0)),
                  pl.BlockSpec((None, GDN_H, HALO, LANES), lambda p, i: (p, 0, jnp.minimum((i + 1) * hb, last_h), 0))],
        out_specs=[pl.BlockSpec((GDN_T, hv), lambda p, i: (i, p)),
                   pl.BlockSpec((4, hv), lambda p, i: (0, p))],
        out_shape=[SDS((S, 3 * hv), BF16), SDS((4, 3 * hv), F32)],
        compiler_params=_cp(2),
    )(proj, proj, proj, conv_w, dqkv, dqkv)


_DIMS = {"nn": ((1,), (0,)), "nt": ((1,), (1,)), "tn": ((0,), (0,))}


def _mm_raw(a, b, mode, hi):
    if hi:
        return _dot_hi(a, b, _DIMS[mode])
    return lax.dot_general(a.astype(BF16), b.astype(BF16), (_DIMS[mode], ((), ())), preferred_element_type=F32)


@functools.partial(jax.custom_vjp, nondiff_argnums=(2, 3))
def mm(a, b, mode, hi):
    return _mm_raw(a, b, mode, hi)


def _mm_fwd(a, b, mode, hi):
    return _mm_raw(a, b, mode, hi), (a, b)


def _mm_bwd(mode, hi, res, dc):
    a, b = res
    if mode == "nn":
        da, db = mm(dc, b, "nt", hi), mm(a, dc, "tn", hi)
    elif mode == "nt":
        da, db = mm(dc, b, "nn", hi), mm(dc, a, "tn", hi)
    else:
        da, db = mm(b, dc, "nt", hi), mm(a, dc, "nn", hi)
    return da, db


mm.defvjp(_mm_fwd, _mm_bwd)


TRI_BASE = 8


def _unit_lower_inverses(Ls):
    n = Ls[0].shape[0]
    r = lax.broadcasted_iota(jnp.int32, (n, n), 0)
    c = lax.broadcasted_iota(jnp.int32, (n, n), 1)
    eye = jnp.where(r == c, 1.0, 0.0).astype(F32)
    base = r // TRI_BASE == c // TRI_BASE
    Ps = [jnp.where(base, -L, 0.0) for L in Ls]
    invs = [eye + P for P in Ps]
    k = 1
    while 2 * k < TRI_BASE:
        Ps = [_dot_x3(P, P) for P in Ps]
        invs = [inv + _dot_x3(inv, P) for inv, P in zip(invs, Ps)]
        k *= 2
    b = 2 * TRI_BASE
    while b <= n:
        off_mask = (r // b == c // b) & ((r % b) >= b // 2) & ((c % b) < b // 2)
        ts = [_dot_x3(inv, jnp.where(off_mask, L, 0.0)) for inv, L in zip(invs, Ls)]
        invs = [inv - _dot_x3(t, inv) for inv, t in zip(invs, ts)]
        b *= 2
    return invs


@jax.custom_vjp
def tri_solve2(Ls, r1s, r2s):
    invs = _unit_lower_inverses(Ls)
    return [_dot_x3(i, r) for i, r in zip(invs, r1s)], [_dot_x3(i, r) for i, r in zip(invs, r2s)]


def _tri_fwd(Ls, r1s, r2s):
    invs = _unit_lower_inverses(Ls)
    s1s = [_dot_x3(i, r) for i, r in zip(invs, r1s)]
    s2s = [_dot_x3(i, r) for i, r in zip(invs, r2s)]
    return (s1s, s2s), (invs, s1s, s2s)


def _tri_bwd(res, ds):
    invs, s1s, s2s = res
    d1s = [_dot_x3(i, d, _DIMS["tn"]) for i, d in zip(invs, ds[0])]
    d2s = [_dot_x3(i, d, _DIMS["tn"]) for i, d in zip(invs, ds[1])]
    dLs = [-(_dot_x3(d1, s1, _DIMS["nt"]) + _dot_x3(d2, s2, _DIMS["nt"])) for d1, s1, d2, s2 in zip(d1s, s1s, d2s, s2s)]
    return dLs, d1s, d2s


tri_solve2.defvjp(_tri_fwd, _tri_bwd)


def _gdn_chunk(qs, ks, vs, gcbs, btbs, Ss):
    C = qs[0].shape[0]
    r = lax.broadcasted_iota(jnp.int32, (C, C), 0)
    c = lax.broadcasted_iota(jnp.int32, (C, C), 1)
    causal, strict = c <= r, c < r
    rows = lax.broadcasted_iota(jnp.int32, gcbs[0].shape, 0)
    Gs = [g[:, :C] for g in gcbs]
    decays = [jnp.exp(jnp.where(causal, G - G.T, NEG)) for G in Gs]
    kbs = [k * b for k, b in zip(ks, btbs)]
    vbs = [v * b for v, b in zip(vs, btbs)]
    Ls = [jnp.where(strict, mm(kb, k, "nt", False) * d, 0.0) for kb, k, d in zip(kbs, ks, decays)]
    egs = [jnp.exp(g) for g in gcbs]
    us, ws = tri_solve2(Ls, vbs, [kb * eg for kb, eg in zip(kbs, egs)])
    qks = [jnp.where(causal, mm(q, k, "nt", False) * d, 0.0) for q, k, d in zip(qs, ks, decays)]
    g_lasts = [jnp.sum(jnp.where(rows == C - 1, g, 0.0), axis=0, keepdims=True) for g in gcbs]
    q_decs = [q * eg for q, eg in zip(qs, egs)]
    k_decs = [k * jnp.exp(gl - g) for k, gl, g in zip(ks, g_lasts, gcbs)]
    v_news = [u - mm(w, S, "nn", False) for u, w, S in zip(us, ws, Ss)]
    os_ = [mm(qd, S, "nn", False) + mm(qk, vn, "nn", False) for qd, S, qk, vn in zip(q_decs, Ss, qks, v_news)]
    S_news = [S * jnp.exp(gl) + mm(kd, vn, "tn", False) for S, gl, kd, vn in zip(Ss, g_lasts, k_decs, v_news)]
    return os_, S_news


def gdn_core(qkv, gc, bt, S):
    nchunk = S // GDN_C

    def body(qkv_ref, g_ref, b_ref, o_ref, st_ref, s_scr):
        n = pl.program_id(0)

        @pl.when(n == 0)
        def _():
            s_scr[...] = jnp.zeros_like(s_scr)

        heads = range(GDN_H)
        S_in = [s_scr[h] for h in heads]
        os_, S_new = _gdn_chunk([qkv_ref[0, h] for h in heads], [qkv_ref[1, h] for h in heads], [qkv_ref[2, h] for h in heads],
                                [g_ref[h] for h in heads], [b_ref[h] for h in heads], S_in)
        for h in heads:
            st_ref[h] = S_in[h]
            o_ref[h] = os_[h]
            s_scr[h] = S_new[h]

    blk3 = pl.BlockSpec((3, GDN_H, GDN_C, LANES), lambda n: (0, 0, n, 0))
    hb = pl.BlockSpec((GDN_H, GDN_C, LANES), lambda n: (0, n, 0))
    return pl.pallas_call(
        body, name="gdn_core", grid=(nchunk,),
        in_specs=[blk3, hb, hb],
        out_specs=[hb, pl.BlockSpec((GDN_H, None, GDN_DK, LANES), lambda n: (0, n, 0, 0))],
        out_shape=[SDS((GDN_H, S, LANES), F32), SDS((GDN_H, nchunk, GDN_DK, LANES), F32)],
        scratch_shapes=[pltpu.VMEM((GDN_H, GDN_DK, LANES), F32)],
        compiler_params=_cp(1),
    )(qkv, gc, bt)


def gdn_core_bwd(qkv, gc, bt, states, do, S):
    nchunk = S // GDN_C

    def body(qkv_ref, g_ref, b_ref, st_ref, do_ref, dqkv_ref, dg_ref, db_ref, ds_scr):
        n = pl.program_id(0)

        @pl.when(n == 0)
        def _():
            ds_scr[...] = jnp.zeros_like(ds_scr)

        heads = range(GDN_H)
        _, vjp = jax.vjp(_gdn_chunk, [qkv_ref[0, h] for h in heads], [qkv_ref[1, h] for h in heads], [qkv_ref[2, h] for h in heads],
                         [g_ref[h] for h in heads], [b_ref[h] for h in heads], [st_ref[h] for h in heads])
        dq, dk, dv, dg, db, dS = vjp(([do_ref[h] for h in heads], [ds_scr[h] for h in heads]))
        for h in heads:
            dqkv_ref[0, h] = dq[h]
            dqkv_ref[1, h] = dk[h]
            dqkv_ref[2, h] = dv[h]
            dg_ref[h] = dg[h]
            db_ref[h] = db[h]
            ds_scr[h] = dS[h]

    rev = lambda n: nchunk - 1 - n
    blk3 = pl.BlockSpec((3, GDN_H, GDN_C, LANES), lambda n: (0, 0, rev(n), 0))
    hb = pl.BlockSpec((GDN_H, GDN_C, LANES), lambda n: (0, rev(n), 0))
    return pl.pallas_call(
        body, name="gdn_core_bwd", grid=(nchunk,),
        in_specs=[blk3, hb, hb, pl.BlockSpec((GDN_H, None, GDN_DK, LANES), lambda n: (0, rev(n), 0, 0)), hb],
        out_specs=[blk3, hb, hb],
        out_shape=[SDS((3, GDN_H, S, LANES), F32), SDS((GDN_H, S, LANES), F32), SDS((GDN_H, S, LANES), F32)],
        scratch_shapes=[pltpu.VMEM((GDN_H, GDN_DK, LANES), F32)],
        compiler_params=_cp(1),
    )(qkv, gc, bt, states, do)


GDN_MAIN = 4 * GDN_H * LANES
GDN_PROJ = GDN_MAIN + LANES
RT = 256


def gdn_forward(h, w_in, conv_w, alog, dtb, out_gain, w_out):
    S = h.shape[0]
    nt = S // RT
    proj = matmul(h, w_in, "nn", F32, "gdn_in")
    qkv = gdn_pre(proj, conv_w, S)
    ab_row = Row(proj, (RT, LANES), lambda i: (i, GDN_MAIN // LANES), gdtype=BF16, gshape=(S, LANES), gimap=lambda i: (i, 0))
    hm = lambda i: (0, i, 0)
    hv = GDN_H * LANES
    gc, bt = rowwise(f_gdn_gates, [ab_row], [alog, dtb],
                     [Out((GDN_H, S, LANES), F32, (GDN_H, RT, LANES), hm, lead=GDN_H)] * 2, (nt,), "gdn_gates")
    o, states = gdn_core(qkv, gc, bt, S)
    o_row = Row(o, (GDN_H, RT, LANES), hm, lead=GDN_H)
    z_row = Row(proj, (RT, hv), lambda i: (i, 3), splits=[LANES] * GDN_H, gdtype=BF16, gshape=(S, hv), gimap=lambda i: (i, 0))
    (on,) = rowwise(f_gdn_post, [o_row, z_row], [out_gain],
                    [Out((S, hv), BF16, (RT, hv), lambda i: (i, 0), splits=[LANES] * GDN_H)], (nt,), "gdn_post")
    y = matmul(on, w_out, "nn", F32, "gdn_out")
    saved = dict(h=h, proj=proj, qkv=qkv, gc=gc, bt=bt, states=states, o=o, on=on, ab_row=ab_row, o_row=o_row, z_row=z_row)
    return y, saved


def gdn_backward(dy, sv, w_in, conv_w, alog, dtb, out_gain, w_out):
    S = dy.shape[0]
    nt = S // RT
    hm = lambda i: (0, i, 0)
    hv = GDN_H * LANES
    don = matmul(dy, w_out, "nt", F32, "gdn_out_dx")
    d_w_out = matmul(sv["on"], dy, "tn", F32, "gdn_out_dw")
    (do, dz), (d_gain,) = rowwise_bwd(f_gdn_post, [sv["o_row"], sv["z_row"]], [out_gain],
                                      [Row(don, (RT, hv), lambda i: (i, 0), splits=[LANES] * GDN_H)], (nt,), "gdn_post_bwd")
    dqkv, dgc, dbt = gdn_core_bwd(sv["qkv"], sv["gc"], sv["bt"], sv["states"], do, S)
    head_blk = lambda a: Row(a, (GDN_H, RT, LANES), hm, lead=GDN_H)
    (dab,), (d_alog, d_dtb) = rowwise_bwd(f_gdn_gates, [sv["ab_row"]], [alog, dtb], [head_blk(dgc), head_blk(dbt)],
                                          (nt,), "gdn_gates_bwd")
    dqkv_proj, d_conv = gdn_pre_bwd(sv["proj"], conv_w, dqkv, S)
    dproj = jnp.concatenate([dqkv_proj, dz, dab], axis=1)
    d_w_in = matmul(sv["h"], dproj, "tn", F32, "gdn_in_dw")
    dh = matmul(dproj, w_in, "nt", F32, "gdn_in_dx")
    return dh, dict(w_in=d_w_in, conv=d_conv, alog=d_alog, dtb=d_dtb, gain=d_gain, w_out=d_w_out)


QB = DSW_SPAN
N_HP = DSW_HG // LANES
PROJ_BLKS = 3 * 3 * N_HP


def _bucket_maps():
    a = np.arange(QB)[:, None]
    j = np.arange(2 * QB)[None, :]
    dist = QB + a - j
    band = (dist >= 0) & (dist <= DSW_SPAN)
    maps = []
    for _, dil in DSW_GROUPS:
        dd = np.maximum(dist, 0) * dil
        max_exact = REL_BUCKETS // 2
        scaled = np.log(np.maximum(dd, 1).astype(np.float32) / np.float32(max_exact)) / np.float32(math.log(REL_MAX_DIST / max_exact))
        large = max_exact + (scaled * np.float32(REL_BUCKETS - max_exact)).astype(np.int32)
        large = np.minimum(large, REL_BUCKETS - 1)
        maps.append(np.where(dd < max_exact, dd, large).astype(np.int32))
    return np.stack(maps), band


def dsw_bias(rel_bias):
    maps, band = _bucket_maps()
    maps = np.where(band[None], maps, -1).astype(np.int32)

    def body(tab_ref, bk_ref, o_ref):
        gh = pl.program_id(0)
        bk = bk_ref[...]
        acc = jnp.full(bk.shape, NEG, F32)
        for b in range(REL_BUCKETS):
            acc = jnp.where(bk == b, tab_ref[b, gh], acc)
        o_ref[...] = acc

    return pl.pallas_call(
        body, name="dsw_bias", grid=(3 * GDN_H,),
        in_specs=[pl.BlockSpec(memory_space=pltpu.SMEM),
                  pl.BlockSpec((None, QB, 2 * QB), lambda gh: (gh // GDN_H, 0, 0))],
        out_specs=pl.BlockSpec((None, QB, 2 * QB), lambda gh: (gh, 0, 0)),
        out_shape=SDS((3 * GDN_H, QB, 2 * QB), F32),
        compiler_params=_cp(1),
    )(rel_bias, jnp.asarray(maps))


def dsw_bias_grad(dbias):
    maps, band = _bucket_maps()
    maps = np.where(band[None], maps, -1).astype(np.int32)

    def body(d_ref, bk_ref, o_ref):
        bk = bk_ref[...]
        d = d_ref[...]
        rows = lax.broadcasted_iota(jnp.int32, (REL_BUCKETS, LANES), 0)
        acc = jnp.zeros((REL_BUCKETS, LANES), F32)
        for b in range(REL_BUCKETS):
            part = jnp.sum(jnp.where(bk == b, d, 0.0), axis=0, keepdims=True)
            val = jnp.sum(part, axis=1, keepdims=True)
            acc = jnp.where(rows == b, val, acc)
        o_ref[...] = acc

    return pl.pallas_call(
        body, name="dsw_bias_grad", grid=(3 * GDN_H,),
        in_specs=[pl.BlockSpec((None, QB, 2 * QB), lambda gh: (gh, 0, 0)),
                  pl.BlockSpec((None, QB, 2 * QB), lambda gh: (gh // GDN_H, 0, 0))],
        out_specs=pl.BlockSpec((None, REL_BUCKETS, LANES), lambda gh: (gh, 0, 0)),
        out_shape=SDS((3 * GDN_H, REL_BUCKETS, LANES), F32),
        compiler_params=_cp(1),
    )(dbias, jnp.asarray(maps))


def _nt(a, b):
    return lax.dot_general(a, b, (((1,), (1,)), ((), ())), preferred_element_type=F32)


def _tn(a, b):
    return lax.dot_general(a, b, (((0,), (0,)), ((), ())), preferred_element_type=F32)


def dsw_group_fwd(qn, kn, proj, bias, gi, S):
    dil = DSW_GROUPS[gi][1]
    sd = S // dil
    nq = sd // QB
    qv = qn.reshape(sd, dil * 3 * DSW_HG)
    kv = kn.reshape(sd, dil * 3 * DSW_HG)
    pv = proj.reshape(sd, dil * 9 * DSW_HG)
    qk_col = lambda hp, r: r * (3 * N_HP) + gi * N_HP + hp
    v_col = lambda hp, r: r * PROJ_BLKS + 2 * 3 * N_HP + gi * N_HP + hp

    def body(q_ref, kp_ref, kc_ref, vp_ref, vc_ref, b_ref, o_ref, l_ref):
        i = pl.program_id(2)
        q = q_ref[...]
        k2 = jnp.concatenate([kp_ref[...], kc_ref[...]], axis=0)
        v2 = jnp.concatenate([vp_ref[...], vc_ref[...]], axis=0).astype(BF16)
        lane_q = lax.broadcasted_iota(jnp.int32, (QB, LANES), 1) < DSW_DH
        lane_k = lax.broadcasted_iota(jnp.int32, (2 * QB, LANES), 1) < DSW_DH
        col = lax.broadcasted_iota(jnp.int32, (QB, 2 * QB), 1)
        first = jnp.logical_and(i == 0, col < QB)
        o_acc = jnp.zeros((QB, LANES), F32)
        lse_b = jnp.zeros((QB, LANES), F32)
        for hh in range(2):
            mq = lane_q if hh == 0 else jnp.logical_not(lane_q)
            mk = lane_k if hh == 0 else jnp.logical_not(lane_k)
            s = _nt(jnp.where(mq, q, 0).astype(BF16), k2) + b_ref[hh]
            s = jnp.where(first, NEG, s)
            mx = jnp.max(s, axis=1, keepdims=True)
            p = jnp.exp(s - mx)
            l = jnp.sum(p, axis=1, keepdims=True)
            oh = jnp.dot(p.astype(BF16), jnp.where(mk, v2, 0).astype(BF16), preferred_element_type=F32) / l
            o_acc = o_acc + oh
            lse_b = jnp.where(mq, mx + jnp.log(l), lse_b)
        o_ref[...] = o_acc
        l_ref[...] = lse_b

    blk = (QB, LANES)
    out_spec = pl.BlockSpec(blk, lambda hp, r, i: (i, r * N_HP + hp))
    o, lse = pl.pallas_call(
        body, name=f"dsw_fwd_g{gi}", grid=(N_HP, dil, nq),
        in_specs=[pl.BlockSpec(blk, lambda hp, r, i: (i, qk_col(hp, r))),
                  pl.BlockSpec(blk, lambda hp, r, i: (jnp.maximum(i - 1, 0), qk_col(hp, r))),
                  pl.BlockSpec(blk, lambda hp, r, i: (i, qk_col(hp, r))),
                  pl.BlockSpec(blk, lambda hp, r, i: (jnp.maximum(i - 1, 0), v_col(hp, r))),
                  pl.BlockSpec(blk, lambda hp, r, i: (i, v_col(hp, r))),
                  pl.BlockSpec((2, QB, 2 * QB), lambda hp, r, i: (gi * N_HP + hp, 0, 0))],
        out_specs=[out_spec, out_spec],
        out_shape=[SDS((sd, dil * DSW_HG), F32)] * 2,
        compiler_params=_cp(3),
    )(qv, kv, kv, pv, pv, bias)
    return o.reshape(S, DSW_HG), lse.reshape(S, DSW_HG)


def dsw_group_bwd(qn, kn, proj, bias, do, o, lse, gi, S):
    dil = DSW_GROUPS[gi][1]
    sd = S // dil
    nq = sd // QB
    qv = qn.reshape(sd, dil * 3 * DSW_HG)
    kv = kn.reshape(sd, dil * 3 * DSW_HG)
    pv = proj.reshape(sd, dil * 9 * DSW_HG)
    dov = do.reshape(sd, dil * DSW_HG)
    ov = o.reshape(sd, dil * DSW_HG)
    lv = lse.reshape(sd, dil * DSW_HG)
    qk_col = lambda hp, r: r * (3 * N_HP) + gi * N_HP + hp
    v_col = lambda hp, r: r * PROJ_BLKS + 2 * 3 * N_HP + gi * N_HP + hp
    o_col = lambda hp, r: r * N_HP + hp
    cur = lambda i: jnp.minimum(i, nq - 1)
    prev = lambda i: jnp.maximum(jnp.minimum(i, nq - 1) - 1, 0)
    done = lambda i: jnp.maximum(i - 1, 0)

    def body(q_ref, kp_ref, kc_ref, vp_ref, vc_ref, b_ref, do_ref, o_ref, l_ref,
             dq_ref, dk_ref, dv_ref, db_ref, dk_scr, dv_scr):
        r, i = pl.program_id(1), pl.program_id(2)

        @pl.when(jnp.logical_and(r == 0, i == 0))
        def _():
            db_ref[...] = jnp.zeros_like(db_ref)

        @pl.when(i == 0)
        def _():
            dk_scr[...] = jnp.zeros_like(dk_scr)
            dv_scr[...] = jnp.zeros_like(dv_scr)

        @pl.when(i < nq)
        def _():
            q = q_ref[...]
            k2 = jnp.concatenate([kp_ref[...], kc_ref[...]], axis=0)
            v2 = jnp.concatenate([vp_ref[...], vc_ref[...]], axis=0).astype(BF16)
            dout = do_ref[...].astype(F32)
            prod = dout * o_ref[...].astype(F32)
            lse_b = l_ref[...]
            lane_q = lax.broadcasted_iota(jnp.int32, (QB, LANES), 1) < DSW_DH
            col = lax.broadcasted_iota(jnp.int32, (QB, 2 * QB), 1)
            first = jnp.logical_and(i == 0, col < QB)
            dq = jnp.zeros((QB, LANES), F32)
            dk2 = jnp.zeros((2 * QB, LANES), F32)
            dv2 = jnp.zeros((2 * QB, LANES), F32)
            for hh in range(2):
                mq = lane_q if hh == 0 else jnp.logical_not(lane_q)
                qm = jnp.where(mq, q, 0).astype(BF16)
                dom = jnp.where(mq, dout, 0.0).astype(BF16)
                s = _nt(qm, k2) + b_ref[hh]
                s = jnp.where(first, NEG, s)
                lse_h = jnp.max(jnp.where(mq, lse_b, NEG), axis=1, keepdims=True)
                p = jnp.exp(s - lse_h)
                delta = jnp.sum(jnp.where(mq, prod, 0.0), axis=1, keepdims=True)
                dp = _nt(dom, v2)
                ds = p * (dp - delta)
                dsb = ds.astype(BF16)
                dq = dq + jnp.where(mq, jnp.dot(dsb, k2, preferred_element_type=F32), 0.0)
                dk2 = dk2 + _tn(dsb, qm)
                dv2 = dv2 + _tn(p.astype(BF16), dom)
                db_ref[hh] += ds
            dq_ref[...] = dq
            dk_ref[...] = dk_scr[...] + dk2[:QB]
            dv_ref[...] = (dv_scr[...] + dv2[:QB]).astype(dv_ref.dtype)
            dk_scr[...] = dk2[QB:]
            dv_scr[...] = dv2[QB:]

        @pl.when(i == nq)
        def _():
            dk_ref[...] = dk_scr[...]
            dv_ref[...] = dv_scr[...].astype(dv_ref.dtype)

    blk = (QB, LANES)
    dq, dk, dv, dbias = pl.pallas_call(
        body, name=f"dsw_bwd_g{gi}", grid=(N_HP, dil, nq + 1),
        in_specs=[pl.BlockSpec(blk, lambda hp, r, i: (cur(i), qk_col(hp, r))),
                  pl.BlockSpec(blk, lambda hp, r, i: (prev(i), qk_col(hp, r))),
                  pl.BlockSpec(blk, lambda hp, r, i: (cur(i), qk_col(hp, r))),
                  pl.BlockSpec(blk, lambda hp, r, i: (prev(i), v_col(hp, r))),
                  pl.BlockSpec(blk, lambda hp, r, i: (cur(i), v_col(hp, r))),
                  pl.BlockSpec((2, QB, 2 * QB), lambda hp, r, i: (gi * N_HP + hp, 0, 0)),
                  pl.BlockSpec(blk, lambda hp, r, i: (cur(i), o_col(hp, r))),
                  pl.BlockSpec(blk, lambda hp, r, i: (cur(i), o_col(hp, r))),
                  pl.BlockSpec(blk, lambda hp, r, i: (cur(i), o_col(hp, r)))],
        out_specs=[pl.BlockSpec(blk, lambda hp, r, i: (cur(i), o_col(hp, r))),
                   pl.BlockSpec(blk, lambda hp, r, i: (done(i), o_col(hp, r))),
                   pl.BlockSpec(blk, lambda hp, r, i: (done(i), o_col(hp, r))),
                   pl.BlockSpec((2, QB, 2 * QB), lambda hp, r, i: (hp, 0, 0))],
        out_shape=[SDS((sd, dil * DSW_HG), F32), SDS((sd, dil * DSW_HG), F32), SDS((sd, dil * DSW_HG), BF16),
                   SDS((GDN_H, QB, 2 * QB), F32)],
        scratch_shapes=[pltpu.VMEM(blk, F32), pltpu.VMEM(blk, F32)],
        compiler_params=_cp(3),
    )(qv, kv, kv, pv, pv, bias, dov, ov, lv)
    return dq.reshape(S, DSW_HG), dk.reshape(S, DSW_HG), dv.reshape(S, DSW_HG), dbias


def dsw_forward(h, w_in, q_gain2, k_gain2, rel_bias, w_out):
    S = h.shape[0]
    nt = S // RT
    nb = 3 * N_HP
    proj = matmul(h, w_in, "nn", F32, "dsw_in")
    width = nb * LANES
    lanes12 = [LANES] * nb
    (qn,) = rowwise(f_qnorm, [Row(proj, (RT, width), lambda i: (i, 0), splits=lanes12)], [q_gain2],
                    [Out((S, width), BF16, (RT, width), lambda i: (i, 0), splits=lanes12)], (nt,), "dsw_qnorm")
    (kn,) = rowwise(f_qknorm, [Row(proj, (RT, width), lambda i: (i, 1), splits=lanes12)], [k_gain2],
                    [Out((S, width), BF16, (RT, width), lambda i: (i, 0), splits=lanes12)], (nt,), "dsw_knorm")
    bias = dsw_bias(rel_bias)
    os_, ls_ = [], []
    for gi in range(3):
        o, l = dsw_group_fwd(qn, kn, proj, bias, gi, S)
        os_.append(o)
        ls_.append(l)
    full = lambda a: Row(a, (RT, DSW_HG), lambda i: (i, 0))
    o, lse = rowwise(f_combine, [full(a) for a in os_ + ls_], [],
                     [Out((S, DSW_HG), BF16, (RT, DSW_HG), lambda i: (i, 0)), Out((S, DSW_HG), F32, (RT, DSW_HG), lambda i: (i, 0))],
                     (nt,), "dsw_combine")
    y = matmul(o, w_out, "nn", F32, "dsw_out")
    return y, dict(h=h, proj=proj, qn=qn, kn=kn, bias=bias, o=o, lse=lse)


def dsw_backward(dy, sv, w_in, q_gain2, k_gain2, w_out):
    S = dy.shape[0]
    nt = S // RT
    nb = 3 * N_HP
    do = matmul(dy, w_out, "nt", BF16, "dsw_out_dx")
    d_w_out = matmul(sv["o"], dy, "tn", F32, "dsw_out_dw")
    pieces_q, pieces_k, pieces_v, dbs = [], [], [], []
    d_qg = jnp.zeros((1, LANES), F32)
    d_kg = jnp.zeros((1, LANES), F32)
    for gi in range(3):
        dq, dk, dv, db = dsw_group_bwd(sv["qn"], sv["kn"], sv["proj"], sv["bias"], do, sv["o"], sv["lse"], gi, S)
        dbs.append(db)
        pieces_v.append(dv)
        for which, dd in ((0, dq), (1, dk)):
            lanes4 = [LANES] * N_HP
            row = Row(sv["proj"], (RT, DSW_HG), lambda i, _o=which * 3 + gi: (i, _o), splits=lanes4,
                      gdtype=BF16, gshape=(S, DSW_HG), gimap=lambda i: (i, 0))
            fn, gain = (f_qnorm, q_gain2) if which == 0 else (f_qknorm, k_gain2)
            (dx,), (dg,) = rowwise_bwd(fn, [row], [gain], [Row(dd, (RT, DSW_HG), lambda i: (i, 0), splits=lanes4)],
                                       (nt,), f"dsw_norm_bwd_{which}{gi}")
            if which == 0:
                pieces_q.append(dx)
                d_qg = d_qg + dg
            else:
                pieces_k.append(dx)
                d_kg = d_kg + dg
    dproj = jnp.concatenate(pieces_q + pieces_k + pieces_v, axis=1)
    d_w_in = matmul(sv["h"], dproj, "tn", F32, "dsw_in_dw")
    dh = matmul(dproj, w_in, "nt", F32, "dsw_in_dx")
    d_rel = dsw_bias_grad(jnp.concatenate(dbs, axis=0))
    return dh, dict(w_in=d_w_in, q_gain2=d_qg, k_gain2=d_kg, rel=d_rel, w_out=d_w_out)


N_LB = DSW_HG // LANES
HALF = DSW_DH // 2


def _lanes(j):
    return slice(LANES * j, LANES * (j + 1))


def _deinterleave(stage, out_ref, dil, rows, dtype):
    for r in range(dil):
        for j in range(N_LB):
            out_ref[r, :, _lanes(j)] = stage[j, pl.ds(r, rows, stride=dil), :].astype(dtype)


def _interleave(in_ref, stage, dil, rows):
    for r in range(dil):
        for j in range(N_LB):
            stage[j, pl.ds(r, rows, stride=dil), :] = in_ref[r, :, _lanes(j)].astype(F32)


def dsw_prep(proj, q_gain2, k_gain2, gi, S):
    dil = DSW_GROUPS[gi][1]
    nt, rows = S // RT, RT // dil

    def body(q_ref, k_ref, v_ref, qg_ref, kg_ref, qo_ref, ko_ref, vo_ref, stage):
        for src, gain_ref, scale, dst in ((q_ref, qg_ref, DSW_DH ** -0.5, qo_ref), (k_ref, kg_ref, 1.0, ko_ref), (v_ref, None, None, vo_ref)):
            for j in range(N_LB):
                val = src[:, _lanes(j)]
                stage[j] = val if gain_ref is None else _qknorm1(val, gain_ref[...], scale)
            _deinterleave(stage, dst, dil, rows, BF16)

    col = lambda which: pl.BlockSpec((RT, DSW_HG), lambda i, _c=which * 3 + gi: (i, _c))
    gspec = pl.BlockSpec((1, LANES), lambda i: (0, 0))
    ospec = pl.BlockSpec((dil, rows, DSW_HG), lambda i: (0, i, 0))
    return pl.pallas_call(
        body, name=f"dsw_prep_g{gi}", grid=(nt,),
        in_specs=[col(0), col(1), col(2), gspec, gspec], out_specs=[ospec] * 3,
        out_shape=[SDS((dil, S // dil, DSW_HG), BF16)] * 3,
        scratch_shapes=[pltpu.VMEM((N_LB, RT, LANES), F32)], compiler_params=_cp(1),
    )(proj, proj, proj, q_gain2, k_gain2)


def dsw_prep_bwd(proj, q_gain2, k_gain2, dqd, dkd, dvd, gi, S):
    dil = DSW_GROUPS[gi][1]
    nt, rows = S // RT, RT // dil

    def body(q_ref, k_ref, qg_ref, kg_ref, dq_ref, dk_ref, dv_ref, oq_ref, ok_ref, ov_ref, dqg_ref, dkg_ref, stage):
        i = pl.program_id(0)

        @pl.when(i == 0)
        def _():
            dqg_ref[...] = jnp.zeros_like(dqg_ref)
            dkg_ref[...] = jnp.zeros_like(dkg_ref)

        for src, gain_ref, scale, cot_ref, dst, dg_ref in ((q_ref, qg_ref, DSW_DH ** -0.5, dq_ref, oq_ref, dqg_ref),
                                                          (k_ref, kg_ref, 1.0, dk_ref, ok_ref, dkg_ref)):
            _interleave(cot_ref, stage, dil, rows)
            for j in range(N_LB):
                _, vjp = jax.vjp(lambda x, g, _s=scale: _qknorm1(x, g, _s), src[:, _lanes(j)], gain_ref[...])
                dx, dg = vjp(stage[j])
                dst[:, _lanes(j)] = dx.astype(dst.dtype)
                dg_ref[...] += dg
        _interleave(dv_ref, stage, dil, rows)
        for j in range(N_LB):
            ov_ref[:, _lanes(j)] = stage[j].astype(ov_ref.dtype)

    col = lambda which: pl.BlockSpec((RT, DSW_HG), lambda i, _c=which * 3 + gi: (i, _c))
    gspec = pl.BlockSpec((1, LANES), lambda i: (0, 0))
    dspec = pl.BlockSpec((dil, rows, DSW_HG), lambda i: (0, i, 0))
    nspec = pl.BlockSpec((RT, DSW_HG), lambda i: (i, 0))
    return pl.pallas_call(
        body, name=f"dsw_prep_bwd_g{gi}", grid=(nt,),
        in_specs=[col(0), col(1), gspec, gspec, dspec, dspec, dspec], out_specs=[nspec] * 3 + [gspec] * 2,
        out_shape=[SDS((S, DSW_HG), BF16)] * 3 + [SDS((1, LANES), F32)] * 2,
        scratch_shapes=[pltpu.VMEM((N_LB, RT, LANES), F32)], compiler_params=_cp(1),
    )(proj, proj, q_gain2, k_gain2, dqd, dkd, dvd)


def _head_masks(rows):
    lane = lax.broadcasted_iota(jnp.int32, (rows, LANES), 1)
    return lane < DSW_DH, (lane % DSW_DH) < HALF


def dsw_attn_fwd(qd, kd, vd, bias, gi, S):
    dil = DSW_GROUPS[gi][1]
    sd = S // dil
    nq = sd // QB

    def body(q_ref, k_ref, v_ref, b_ref, o_ref, l_ref, kp_scr, vp_scr):
        i = pl.program_id(1)

        @pl.when(i == 0)
        def _():
            kp_scr[...] = jnp.zeros_like(kp_scr)
            vp_scr[...] = jnp.zeros_like(vp_scr)

        lo_q, _ = _head_masks(QB)
        lo_k, _ = _head_masks(2 * QB)
        col = lax.broadcasted_iota(jnp.int32, (QB, 2 * QB), 1)
        first = jnp.logical_and(i == 0, col < QB)
        for hp in range(N_HP):
            q = q_ref[:, _lanes(hp)]
            k2 = jnp.concatenate([kp_scr[:, _lanes(hp)], k_ref[:, _lanes(hp)]], axis=0)
            v2 = jnp.concatenate([vp_scr[:, _lanes(hp)], v_ref[:, _lanes(hp)]], axis=0)
            o_acc = jnp.zeros((QB, LANES), F32)
            lse_b = jnp.zeros((QB, LANES), F32)
            for hh in range(2):
                mq = lo_q if hh == 0 else jnp.logical_not(lo_q)
                mk = lo_k if hh == 0 else jnp.logical_not(lo_k)
                s = _nt(jnp.where(mq, q, 0).astype(BF16), k2) + b_ref[2 * hp + hh]
                s = jnp.where(first, NEG, s)
                mx = jnp.max(s, axis=1, keepdims=True)
                p = jnp.exp(s - mx)
                l = jnp.sum(p, axis=1, keepdims=True)
                oh = jnp.dot(p.astype(BF16), jnp.where(mk, v2, 0).astype(BF16), preferred_element_type=F32) / l
                o_acc = o_acc + oh
                lse_b = jnp.where(mq, mx + jnp.log(l), lse_b)
            o_ref[:, _lanes(hp)] = o_acc
            l_ref[:, _lanes(hp)] = lse_b
        kp_scr[...] = k_ref[...]
        vp_scr[...] = v_ref[...]

    blk = pl.BlockSpec((None, QB, DSW_HG), lambda r, i: (r, i, 0))
    return pl.pallas_call(
        body, name=f"dsw_attn_g{gi}", grid=(dil, nq),
        in_specs=[blk, blk, blk, pl.BlockSpec((GDN_H, QB, 2 * QB), lambda r, i: (gi, 0, 0))],
        out_specs=[blk, blk], out_shape=[SDS((dil, sd, DSW_HG), F32)] * 2,
        scratch_shapes=[pltpu.VMEM((QB, DSW_HG), BF16)] * 2, compiler_params=_cp(2),
    )(qd, kd, vd, bias)


def dsw_attn_bwd(qd, kd, vd, bias, dod, statd, gi, S):
    dil = DSW_GROUPS[gi][1]
    sd = S // dil
    nq = sd // QB
    cur = lambda i: jnp.minimum(i, nq - 1)
    done = lambda i: jnp.maximum(i - 1, 0)

    def body(q_ref, k_ref, v_ref, b_ref, do_ref, st_ref, dq_ref, dk_ref, dv_ref, db_ref, kp_scr, vp_scr, dk_scr, dv_scr):
        r, i = pl.program_id(0), pl.program_id(1)

        @pl.when(jnp.logical_and(r == 0, i == 0))
        def _():
            db_ref[...] = jnp.zeros_like(db_ref)

        @pl.when(i == 0)
        def _():
            for scr in (kp_scr, vp_scr, dk_scr, dv_scr):
                scr[...] = jnp.zeros_like(scr)

        @pl.when(i < nq)
        def _():
            lo_q, first_half = _head_masks(QB)
            col = lax.broadcasted_iota(jnp.int32, (QB, 2 * QB), 1)
            first = jnp.logical_and(i == 0, col < QB)
            for hp in range(N_HP):
                q = q_ref[:, _lanes(hp)]
                k2 = jnp.concatenate([kp_scr[:, _lanes(hp)], k_ref[:, _lanes(hp)]], axis=0)
                v2 = jnp.concatenate([vp_scr[:, _lanes(hp)], v_ref[:, _lanes(hp)]], axis=0)
                dout = do_ref[:, _lanes(hp)]
                stat = st_ref[:, _lanes(hp)]
                dq = jnp.zeros((QB, LANES), F32)
                dk2 = jnp.zeros((2 * QB, LANES), F32)
                dv2 = jnp.zeros((2 * QB, LANES), F32)
                for hh in range(2):
                    mq = lo_q if hh == 0 else jnp.logical_not(lo_q)
                    qm = jnp.where(mq, q, 0).astype(BF16)
                    dom = jnp.where(mq, dout, 0).astype(BF16)
                    s = _nt(qm, k2) + b_ref[2 * hp + hh]
                    s = jnp.where(first, NEG, s)
                    lse_h = jnp.max(jnp.where(jnp.logical_and(mq, first_half), stat, NEG), axis=1, keepdims=True)
                    delta = jnp.max(jnp.where(jnp.logical_and(mq, jnp.logical_not(first_half)), stat, NEG), axis=1, keepdims=True)
                    p = jnp.exp(s - lse_h)
                    ds = p * (_nt(dom, v2) - delta)
                    dsb = ds.astype(BF16)
                    dq = dq + jnp.where(mq, jnp.dot(dsb, k2, preferred_element_type=F32), 0.0)
                    dk2 = dk2 + _tn(dsb, qm)
                    dv2 = dv2 + _tn(p.astype(BF16), dom)
                    db_ref[2 * hp + hh] += ds
                dq_ref[:, _lanes(hp)] = dq
                dk_ref[:, _lanes(hp)] = dk_scr[:, _lanes(hp)] + dk2[:QB]
                dv_ref[:, _lanes(hp)] = (dv_scr[:, _lanes(hp)] + dv2[:QB]).astype(dv_ref.dtype)
                dk_scr[:, _lanes(hp)] = dk2[QB:]
                dv_scr[:, _lanes(hp)] = dv2[QB:]
            kp_scr[...] = k_ref[...]
            vp_scr[...] = v_ref[...]

        @pl.when(i == nq)
        def _():
            dk_ref[...] = dk_scr[...]
            dv_ref[...] = dv_scr[...].astype(dv_ref.dtype)

    blk = pl.BlockSpec((None, QB, DSW_HG), lambda r, i: (r, cur(i), 0))
    oblk = pl.BlockSpec((None, QB, DSW_HG), lambda r, i: (r, done(i), 0))
    return pl.pallas_call(
        body, name=f"dsw_attn_bwd_g{gi}", grid=(dil, nq + 1),
        in_specs=[blk, blk, blk, pl.BlockSpec((GDN_H, QB, 2 * QB), lambda r, i: (gi, 0, 0)), blk, blk],
        out_specs=[blk, oblk, oblk, pl.BlockSpec((GDN_H, QB, 2 * QB), lambda r, i: (0, 0, 0))],
        out_shape=[SDS((dil, sd, DSW_HG), F32), SDS((dil, sd, DSW_HG), F32), SDS((dil, sd, DSW_HG), BF16),
                   SDS((GDN_H, QB, 2 * QB), F32)],
        scratch_shapes=[pltpu.VMEM((QB, DSW_HG), BF16)] * 2 + [pltpu.VMEM((QB, DSW_HG), F32)] * 2,
        compiler_params=_cp(2),
    )(qd, kd, vd, bias, dod, statd)


def dsw_combine(ods, lseds, S):
    nt = S // RT
    dils = [d for _, d in DSW_GROUPS]

    def body(*refs):
        ins, (o_ref, l_ref), stages = refs[:6], refs[6:8], refs[8:]
        for g in range(3):
            _interleave(ins[g], stages[g], dils[g], RT // dils[g])
            _interleave(ins[3 + g], stages[3 + g], dils[g], RT // dils[g])
        for j in range(N_LB):
            o, lse = f_combine(None, *[st[j] for st in stages])
            o_ref[:, _lanes(j)] = o.astype(o_ref.dtype)
            l_ref[:, _lanes(j)] = lse

    dspec = lambda d: pl.BlockSpec((d, RT // d, DSW_HG), lambda i: (0, i, 0))
    nspec = pl.BlockSpec((RT, DSW_HG), lambda i: (i, 0))
    return pl.pallas_call(
        body, name="dsw_combine", grid=(nt,),
        in_specs=[dspec(d) for d in dils] * 2, out_specs=[nspec, nspec],
        out_shape=[SDS((S, DSW_HG), BF16), SDS((S, DSW_HG), F32)],
        scratch_shapes=[pltpu.VMEM((N_LB, RT, LANES), F32)] * 6, compiler_params=_cp(1),
    )(*ods, *lseds)


def dsw_bwd_prep(do, o, lse, S):
    nt = S // RT
    dils = [d for _, d in DSW_GROUPS]

    def body(do_ref, o_ref, l_ref, *rest):
        outs, (st_do, st_stat) = rest[:6], rest[6:]
        lo, first_half = _head_masks(RT)
        for j in range(N_LB):
            dout = do_ref[:, _lanes(j)]
            prod = dout * o_ref[:, _lanes(j)].astype(F32)
            s_all = jnp.sum(prod, axis=1, keepdims=True)
            s_lo = jnp.sum(jnp.where(lo, prod, 0.0), axis=1, keepdims=True)
            delta = jnp.where(lo, s_lo, s_all - s_lo)
            st_do[j] = dout
            st_stat[j] = jnp.where(first_half, l_ref[:, _lanes(j)], delta)
        for g in range(3):
            _deinterleave(st_do, outs[g], dils[g], RT // dils[g], BF16)
            _deinterleave(st_stat, outs[3 + g], dils[g], RT // dils[g], F32)

    nspec = pl.BlockSpec((RT, DSW_HG), lambda i: (i, 0))
    dspec = lambda d: pl.BlockSpec((d, RT // d, DSW_HG), lambda i: (0, i, 0))
    res = pl.pallas_call(
        body, name="dsw_bwd_prep", grid=(nt,),
        in_specs=[nspec] * 3, out_specs=[dspec(d) for d in dils] * 2,
        out_shape=[SDS((d, S // d, DSW_HG), BF16) for d in dils] + [SDS((d, S // d, DSW_HG), F32) for d in dils],
        scratch_shapes=[pltpu.VMEM((N_LB, RT, LANES), F32)] * 2, compiler_params=_cp(1),
    )(do, o, lse)
    return res[:3], res[3:]


def dsw_forward(h, w_in, q_gain2, k_gain2, rel_bias, w_out):
    S = h.shape[0]
    proj = matmul(h, w_in, "nn", F32, "dsw_in", col_shards=N_SHARD)
    bias = dsw_bias(rel_bias)
    qkv, ods, lseds = [], [], []
    for gi in range(3):
        qd, kd, vd = dsw_prep(proj, q_gain2, k_gain2, gi, S)
        od, ld = dsw_attn_fwd(qd, kd, vd, bias, gi, S)
        qkv.append((qd, kd, vd))
        ods.append(od)
        lseds.append(ld)
    o, lse = dsw_combine(ods, lseds, S)
    y = matmul(o, w_out, "nn", F32, "dsw_out", col_shards=N_SHARD)
    return y, dict(h=h, proj=proj, qkv=qkv, bias=bias, o=o, lse=lse)


def dsw_backward(dy, sv, w_in, q_gain2, k_gain2, w_out):
    S = dy.shape[0]
    do = matmul(dy, w_out, "nt", F32, "dsw_out_dx", col_shards=N_SHARD)
    d_w_out = matmul(sv["o"], dy, "tn", F32, "dsw_out_dw", col_shards=N_SHARD)
    dods, statds = dsw_bwd_prep(do, sv["o"], sv["lse"], S)
    pieces_q, pieces_k, pieces_v, dbs = [], [], [], []
    d_qg = jnp.zeros((1, LANES), F32)
    d_kg = jnp.zeros((1, LANES), F32)
    for gi in range(3):
        qd, kd, vd = sv["qkv"][gi]
        dqd, dkd, dvd, db = dsw_attn_bwd(qd, kd, vd, sv["bias"], dods[gi], statds[gi], gi, S)
        dq, dk, dv, dqg, dkg = dsw_prep_bwd(sv["proj"], q_gain2, k_gain2, dqd, dkd, dvd, gi, S)
        dbs.append(db)
        pieces_q.append(dq)
        pieces_k.append(dk)
        pieces_v.append(dv)
        d_qg = d_qg + dqg
        d_kg = d_kg + dkg
    dproj = jnp.concatenate(pieces_q + pieces_k + pieces_v, axis=1)
    d_w_in = matmul(sv["h"], dproj, "tn", F32, "dsw_in_dw", col_shards=N_SHARD)
    dh = matmul(dproj, w_in, "nt", F32, "dsw_in_dx", col_shards=N_SHARD)
    d_rel = dsw_bias_grad(jnp.concatenate(dbs, axis=0))
    return dh, dict(w_in=d_w_in, q_gain2=d_qg, k_gain2=d_kg, rel=d_rel, w_out=d_w_out)


FT = 128


def ffn_forward(h, w_in, w_out, tag):
    S = h.shape[0]
    gu = matmul(h, w_in, "nn", BF16, f"ffn_in_{tag}", col_shards=N_SHARD)
    gu_row = Row(gu, (FT, 2 * FFN), lambda i: (i, 0), splits=[FFN, FFN], gdtype=BF16)
    (a,) = rowwise(f_swiglu, [gu_row], [], [Out((S, FFN), BF16, (FT, FFN), lambda i: (i, 0))], (S // FT,), f"ffn_act_{tag}")
    f = matmul(a, w_out, "nn", F32, f"ffn_out_{tag}")
    return f, dict(h=h, gu_row=gu_row, a=a)


def ffn_backward(df, sv, w_in, w_out, tag):
    S = df.shape[0]
    da = matmul(df, w_out, "nt", BF16, f"ffn_out_dx_{tag}")
    d_w_out = matmul(sv["a"], df, "tn", F32, f"ffn_out_dw_{tag}")
    (dgu,), _ = rowwise_bwd(f_swiglu, [sv["gu_row"]], [], [Row(da, (FT, FFN), lambda i: (i, 0))], (S // FT,), f"ffn_act_bwd_{tag}")
    d_w_in = matmul(sv["h"], dgu, "tn", F32, f"ffn_in_dw_{tag}", col_shards=N_SHARD)
    dh = matmul(dgu, w_in, "nt", F32, f"ffn_in_dx_{tag}", col_shards=N_SHARD)
    return dh, d_w_in, d_w_out


def f_norm_only(ids, x, gain, sc, sh):
    return (_normmod(x, gain, sc, sh),)


def _wide(a, **kw):
    return Row(a, (RT, D), lambda i: (i, 0), **kw)


def _wide_out(S, dtype):
    return Out((S, D), dtype, (RT, D), lambda i: (i, 0))


def adamw(w, g, m, v, name):
    shape = w.shape
    C = shape[-1]
    R = int(np.prod(shape[:-1]))
    w2, g2, m2, v2 = (a.reshape(R, C) for a in (w, g, m, v))
    br = R
    if R > 256:
        br = max(b for b in range(8, 257, 8) if R % b == 0)
    c1 = 1.0 / (1.0 - ADAM_B1 ** ADAM_STEP)
    c2 = 1.0 / (1.0 - ADAM_B2 ** ADAM_STEP)

    def body(w_ref, g_ref, m_ref, v_ref, d_ref, nm_ref, nv_ref):
        gg = g_ref[...]
        mm_ = ADAM_B1 * m_ref[...] + (1.0 - ADAM_B1) * gg
        vv = ADAM_B2 * v_ref[...] + (1.0 - ADAM_B2) * (gg * gg)
        d_ref[...] = -ADAM_LR * ((mm_ * c1) / (jnp.sqrt(vv * c2) + ADAM_EPS) + ADAM_WD * w_ref[...])
        nm_ref[...] = mm_
        nv_ref[...] = vv

    spec = pl.BlockSpec((br, C), lambda i: (i, 0))
    d, nm, nv = pl.pallas_call(
        body, name=name, grid=(R // br,), in_specs=[spec] * 4, out_specs=[spec] * 3,
        out_shape=[SDS((R, C), F32)] * 3, compiler_params=_cp(1),
    )(w2, g2, m2, v2)
    return d.reshape(shape), nm.reshape(shape), nv.reshape(shape)


def _place():
    x, y, c = lax.axis_index("x"), lax.axis_index("y"), lax.axis_index("c")
    chips = [(1 - x, y), (x, 1 - y), (1 - x, 1 - y)]
    return x, y, c, chips


def all_gather_small(blk, name):
    m_per, n = blk.shape

    def body(x_ref, out_ref, send_sems, recv_sems, local_sem):
        x, y, c, chips = _place()
        me, sibling = (x, y, c), (x, y, 1 - c)

        def rows(px, py, pc):
            return out_ref.at[pl.ds((4 * px + 2 * py + pc) * m_per, m_per), :]

        def copy(k, block, to, src=None):
            return pltpu.make_async_remote_copy(
                src_ref=rows(*block) if src is None else src, dst_ref=rows(*block),
                send_sem=send_sems.at[k], recv_sem=recv_sems.at[k], device_id=to, device_id_type=MESH)

        mine = pltpu.make_async_copy(x_ref, rows(*me), local_sem)
        mine.start()
        first = [copy(0, me, sibling, src=x_ref)]
        first += [copy(1 + j, me, (*chip, c), src=x_ref) for j, chip in enumerate(chips)]
        for cp in first:
            cp.start()
        passed = [copy(4 + j, (*chip, c), sibling) for j, chip in enumerate(chips)]
        for j, chip in enumerate(chips):
            copy(1 + j, (*chip, c), me).wait_recv()
            passed[j].start()
        copy(0, sibling, me).wait_recv()
        for j, chip in enumerate(chips):
            copy(4 + j, (*chip, 1 - c), me).wait_recv()
        for cp in first + passed:
            cp.wait_send()
        mine.wait()

    return pl.pallas_call(
        body, name=name, out_shape=SDS((N_DEV * m_per, n), blk.dtype),
        in_specs=[pl.BlockSpec(memory_space=pltpu.VMEM)], out_specs=pl.BlockSpec(memory_space=pltpu.VMEM),
        scratch_shapes=[pltpu.SemaphoreType.DMA((7,)), pltpu.SemaphoreType.DMA((7,)), pltpu.SemaphoreType.DMA],
    )(blk)


def _half(cc, rh):
    return pl.ds(pl.multiple_of(cc * rh, 16), rh)


def all_gather_shards(ws):
    n = len(ws)

    def body(*refs):
        w_refs, out_refs = refs[:n], refs[n:2 * n]
        send_sems, recv_sems, local_sems, own_sems = refs[2 * n:]
        x, y, c, chips = _place()
        sibling = (x, y, 1 - c)
        s_me = 2 * x + y

        def copy(k, src, dst, to):
            return pltpu.make_async_remote_copy(src_ref=src, dst_ref=dst, send_sem=send_sems.at[k], recv_sem=recv_sems.at[k],
                                                device_id=to, device_id_type=MESH)

        local, sends, passed = [], [], []
        for k in range(n):
            rh = ws[k].shape[0] // 2
            cp = pltpu.make_async_remote_copy(src_ref=w_refs[k], dst_ref=out_refs[k].at[s_me], send_sem=local_sems.at[k],
                                              recv_sem=own_sems.at[k], device_id=sibling, device_id_type=MESH)
            cp.start()
            local.append(cp)
            for j, chip in enumerate(chips):
                sd = copy(6 * k + j, w_refs[k].at[_half(c, rh)], out_refs[k].at[s_me, _half(c, rh)], (*chip, c))
                sd.start()
                sends.append(sd)
        for k in range(n):
            rh = ws[k].shape[0] // 2
            for j, (px, py) in enumerate(chips):
                got = out_refs[k].at[2 * px + py, _half(c, rh)]
                copy(6 * k + j, got, got, (px, py, c)).wait_recv()
                fw = copy(6 * k + 3 + j, got, got, sibling)
                fw.start()
                passed.append(fw)
        for k in range(n):
            rh = ws[k].shape[0] // 2
            for j, (px, py) in enumerate(chips):
                got = out_refs[k].at[2 * px + py, _half(1 - c, rh)]
                copy(6 * k + 3 + j, got, got, sibling).wait_recv()
        for cp in sends + passed:
            cp.wait_send()
        for cp in local:
            cp.wait()

    return pl.pallas_call(
        body, name="weights_all_gather", out_shape=[SDS((N_SHARD,) + w.shape, w.dtype) for w in ws],
        in_specs=[ANY] * n, out_specs=[ANY] * n,
        scratch_shapes=[pltpu.SemaphoreType.DMA((6 * n,)), pltpu.SemaphoreType.DMA((6 * n,)), pltpu.SemaphoreType.DMA((n,)),
                        pltpu.SemaphoreType.DMA((n,))],
    )(*ws)


def sibling_exchange(sends, name):
    n = len(sends)

    def body(*refs):
        s_refs, o_refs, send_sems, recv_sems = refs[:n], refs[n:2 * n], refs[2 * n], refs[2 * n + 1]
        x, y, c, _ = _place()
        cps = [pltpu.make_async_remote_copy(src_ref=s_refs[k], dst_ref=o_refs[k], send_sem=send_sems.at[k], recv_sem=recv_sems.at[k],
                                            device_id=(x, y, 1 - c), device_id_type=MESH) for k in range(n)]
        for cp in cps:
            cp.start()
        for cp in cps:
            cp.wait()

    return pl.pallas_call(
        body, name=name, out_shape=[SDS(s.shape, s.dtype) for s in sends], in_specs=[ANY] * n, out_specs=[ANY] * n,
        scratch_shapes=[pltpu.SemaphoreType.DMA((n,)), pltpu.SemaphoreType.DMA((n,))],
    )(*sends)


def scatter_to_chips(parts):
    n = len(parts)

    def body(*refs):
        p_refs, o_refs, send_sems, recv_sems = refs[:n], refs[n:2 * n], refs[2 * n], refs[2 * n + 1]
        x, y, c, chips = _place()
        cps = []
        for k in range(n):
            for j, (px, py) in enumerate(chips):
                cp = pltpu.make_async_remote_copy(src_ref=p_refs[k].at[2 * px + py], dst_ref=o_refs[k].at[j],
                                                  send_sem=send_sems.at[3 * k + j], recv_sem=recv_sems.at[3 * k + j],
                                                  device_id=(px, py, c), device_id_type=MESH)
                cp.start()
                cps.append(cp)
        for cp in cps:
            cp.wait()

    return pl.pallas_call(
        body, name="grads_scatter", out_shape=[SDS((3,) + p.shape[1:], p.dtype) for p in parts], in_specs=[ANY] * n, out_specs=[ANY] * n,
        scratch_shapes=[pltpu.SemaphoreType.DMA((3 * n,)), pltpu.SemaphoreType.DMA((3 * n,))],
    )(*parts)


def merge_halves(halves):
    n = len(halves)

    def body(*refs):
        h_refs, o_refs = refs[:n], refs[n:2 * n]
        send_sems, recv_sems, local_sems = refs[2 * n:]
        x, y, c, _ = _place()
        local, cps = [], []
        for k in range(n):
            rh = halves[k].shape[0]
            lc = pltpu.make_async_copy(h_refs[k], o_refs[k].at[_half(c, rh)], local_sems.at[k])
            lc.start()
            local.append(lc)
            cp = pltpu.make_async_remote_copy(src_ref=h_refs[k], dst_ref=o_refs[k].at[_half(c, rh)], send_sem=send_sems.at[k],
                                              recv_sem=recv_sems.at[k], device_id=(x, y, 1 - c), device_id_type=MESH)
            cp.start()
            cps.append(cp)
        for k in range(n):
            rh = halves[k].shape[0]
            got = o_refs[k].at[_half(1 - c, rh)]
            pltpu.make_async_remote_copy(src_ref=got, dst_ref=got, send_sem=send_sems.at[k], recv_sem=recv_sems.at[k],
                                         device_id=(x, y, 1 - c), device_id_type=MESH).wait_recv()
        for cp in cps:
            cp.wait_send()
        for lc in local:
            lc.wait()

    return pl.pallas_call(
        body, name="grads_merge_halves", out_shape=[SDS((2 * h.shape[0], h.shape[1]), h.dtype) for h in halves],
        in_specs=[ANY] * n, out_specs=[ANY] * n,
        scratch_shapes=[pltpu.SemaphoreType.DMA((n,)), pltpu.SemaphoreType.DMA((n,)), pltpu.SemaphoreType.DMA((n,))],
    )(*halves)


def add_rows(arrs, out_dtype, name, rt=256):
    Rr, W = arrs[0].shape

    def fn(ids, *vals):
        acc = vals[0]
        for v in vals[1:]:
            acc = acc + v
        return (acc,)

    t = rt if Rr % rt == 0 else max(b for b in range(16, rt + 1, 16) if Rr % b == 0)
    (out,) = rowwise(fn, [Row(a, (t, W), lambda i: (i, 0)) for a in arrs], [],
                     [Out((Rr, W), out_dtype, (t, W), lambda i: (i, 0))], (Rr // t,), name)
    return out


HBM_SPEC = pl.BlockSpec(memory_space=pltpu.HBM)
SEM_SPEC = pl.BlockSpec(memory_space=pltpu.SEMAPHORE)
DATAFLOW = pltpu.SideEffectType.DATAFLOW_SIDE_EFFECTING


def _in_hbm(a):
    return pltpu.with_memory_space_constraint(a, pltpu.HBM)


def _gather_copies(w_refs, land_refs, send_sems, recv_sems):
    x, y, c, chips = _place()
    targets = [(x, y, 1 - c)] + [(*chip, c) for chip in chips]
    cps = []
    for k, (w_ref, land_ref) in enumerate(zip(w_refs, land_refs)):
        for j, to in enumerate(targets):
            cps.append(pltpu.make_async_remote_copy(src_ref=w_ref, dst_ref=land_ref.at[2 * x + y], send_sem=send_sems.at[4 * k + j],
                                                    recv_sem=recv_sems.at[4 * k + j], device_id=to, device_id_type=MESH))
    return cps


def _scatter_copies(p_refs, land_refs, send_sems, recv_sems):
    x, y, c, chips = _place()
    cps = []
    for k, (p_ref, land_ref) in enumerate(zip(p_refs, land_refs)):
        for j, (px, py) in enumerate(chips):
            cps.append(pltpu.make_async_remote_copy(src_ref=p_ref.at[2 * px + py], dst_ref=land_ref.at[j], send_sem=send_sems.at[3 * k + j],
                                                    recv_sem=recv_sems.at[3 * k + j], device_id=(px, py, c), device_id_type=MESH))
    return cps


def copies_start(srcs, land_shapes, make_copies, per_src, name):
    n = len(srcs)
    m = per_src * n

    def body(*refs):
        src_refs, land_refs = refs[:n], refs[n:2 * n]
        send_sems, recv_sems, token = refs[2 * n], refs[2 * n + 1], refs[-1]
        for cp in make_copies(src_refs, land_refs, send_sems, recv_sems):
            cp.start()
        token[...] = jnp.zeros_like(token)

    lands = [lax.empty(shp, s.dtype) for shp, s in zip(land_shapes, srcs)]
    res = pl.pallas_call(
        body, name=name,
        out_shape=(pltpu.SemaphoreType.DMA((m,)), pltpu.SemaphoreType.DMA((m,)), *[pltpu.HBM(s.shape, s.dtype) for s in srcs],
                   *[pltpu.HBM(shp, s.dtype) for shp, s in zip(land_shapes, srcs)], SDS((8, LANES), F32)),
        in_specs=[HBM_SPEC] * (2 * n),
        out_specs=(SEM_SPEC, SEM_SPEC, *[HBM_SPEC] * (2 * n), pl.BlockSpec(memory_space=pltpu.VMEM)),
        input_output_aliases={i: 2 + i for i in range(2 * n)},
        compiler_params=pltpu.CompilerParams(has_side_effects=DATAFLOW),
    )(*[_in_hbm(s) for s in srcs], *[_in_hbm(l) for l in lands])
    return res[0], res[1], list(res[2:2 + n]), list(res[2 + n:2 + 2 * n]), res[-1]


def copies_wait(send_sems, recv_sems, srcs, lands, make_copies, after, name):
    n = len(srcs)

    def body(*refs):
        src_refs, land_refs = refs[:n], refs[n:2 * n]
        for cp in make_copies(src_refs, land_refs, refs[2 * n], refs[2 * n + 1]):
            cp.wait_send()
            cp.wait_recv()

    res = pl.pallas_call(
        body, name=name,
        out_shape=(*[pltpu.HBM(s.shape, s.dtype) for s in srcs], *[pltpu.HBM(l.shape, l.dtype) for l in lands]),
        in_specs=[HBM_SPEC] * (2 * n) + [SEM_SPEC, SEM_SPEC, ANY],
        out_specs=tuple([HBM_SPEC] * (2 * n)),
        input_output_aliases={i: i for i in range(2 * n)},
        compiler_params=pltpu.CompilerParams(has_side_effects=DATAFLOW),
    )(*srcs, *lands, send_sems, recv_sems, after)
    return list(res[n:])


PACK = (("gdn_w_in", 2), ("gdn_w_out", 1), ("w_ffn_in", 2), ("w_ffn_out", 1), ("dsw_w_in", 2), ("dsw_w_out", 2))
PACK_ALIGN = 32


def _pack_rows(sizes):
    total = sum(sizes)
    rows = -(-total // D)
    return -(-rows // PACK_ALIGN) * PACK_ALIGN


def pack_blocks(blocks, dtype):
    flat = [b.astype(dtype).reshape(-1) for b in blocks]
    total = sum(f.shape[0] for f in flat)
    R = _pack_rows([f.shape[0] for f in flat])
    flat.append(jnp.zeros((R * D - total,), dtype))
    return jnp.concatenate(flat).reshape(R, D)


def unpack_blocks(buf, shapes):
    flat = buf.reshape(-1)
    out, off = [], 0
    for shp in shapes:
        n = int(np.prod(shp))
        out.append(flat[off:off + n].reshape(shp))
        off += n
    return out


def _shard_slice(a, axis, s):
    n = a.shape[axis] // N_SHARD
    return lax.slice_in_dim(a, s * n, (s + 1) * n, axis=axis)


def _pad_lanes(v):
    return jnp.concatenate([v.astype(F32), jnp.zeros((LANES - v.shape[0],), F32)])[None]


def kernel(x, c, w_ada, b_ada, norm_mix, norm_ffn, w_ffn_in, w_ffn_out, gdn_w_in, gdn_conv, gdn_a_log, gdn_dt_bias, gdn_out_norm, gdn_w_out, dsw_w_in, dsw_q_norm, dsw_k_norm, dsw_w_out, rel_bias, loss_target, m_w_ada, m_b_ada, m_norm_mix, m_norm_ffn, m_w_ffn_in, m_w_ffn_out, m_gdn_w_in, m_gdn_conv, m_gdn_a_log, m_gdn_dt_bias, m_gdn_out_norm, m_gdn_w_out, m_dsw_w_in, m_dsw_q_norm, m_dsw_k_norm, m_dsw_w_out, m_rel_bias, v_w_ada, v_b_ada, v_norm_mix, v_norm_ffn, v_w_ffn_in, v_w_ffn_out, v_gdn_w_in, v_gdn_conv, v_gdn_a_log, v_gdn_dt_bias, v_gdn_out_norm, v_gdn_w_out, v_dsw_w_in, v_dsw_q_norm, v_dsw_k_norm, v_dsw_w_out, v_rel_bias):
    S = x.shape[1]
    nt = S // RT
    xi, yi, ci = lax.axis_index("x"), lax.axis_index("y"), lax.axis_index("c")
    me = 4 * xi + 2 * yi + ci
    s_me = 2 * xi + yi
    x0, tgt = x[0], loss_target[0]
    shard = dict(w_ffn_in=w_ffn_in, w_ffn_out=w_ffn_out, gdn_w_in=gdn_w_in, gdn_w_out=gdn_w_out, dsw_w_in=dsw_w_in, dsw_w_out=dsw_w_out)

    whole = lambda a: Row(a, a.shape, lambda i: (0,) * a.ndim)
    (cond8,) = rowwise(lambda ids, v: (_silu(v),), [whole(c.reshape(8, LANES))], [], [Out((8, LANES), F32, (8, LANES), lambda i: (0, 0))], (1,), "cond")
    cond_all = all_gather_small(cond8, "gather_cond").reshape(N_DEV, D)
    cond16 = jnp.concatenate([cond_all, jnp.zeros((8, D), F32)], axis=0)
    ada_cols = w_ada.shape[2]
    mods = [matmul(cond16, w_ada[l], "nn", F32, f"ada_{l}")[:N_DEV] for l in range(2)]
    buf = jnp.concatenate([jnp.stack(mods, axis=1).reshape(-1, LANES), gdn_conv.reshape(-1, LANES)], axis=0)
    n_mod_rows = N_DEV * 2 * ada_cols // LANES
    got = all_gather_small(buf, "gather_mod").reshape(N_DEV, buf.shape[0], LANES)
    mod_parts, conv_parts = [], []
    for s in range(N_SHARD):
        from_dev = got[2 * s]
        mod_parts.append(lax.dynamic_index_in_dim(from_dev[:n_mod_rows].reshape(N_DEV, 2, ada_cols), me, 0, keepdims=False))
        conv_parts.append(from_dev[n_mod_rows:].reshape(4, -1))
    mod_nb = jnp.concatenate(mod_parts, axis=1)
    conv_w = jnp.concatenate(conv_parts, axis=1)
    (mod,) = rowwise(lambda ids, a, b: (a + b,), [whole(mod_nb), whole(b_ada)], [], [Out(mod_nb.shape, F32, mod_nb.shape, lambda i: (0, 0))], (1,), "mod_bias")
    mod = mod.reshape(2, 6, 1, D)
    sh1, sc1, g1, sh2, sc2, g2 = ([mod[l, k] for l in range(2)] for k in range(6))
    gmix = [norm_mix[l][None] for l in range(2)]
    gffn = [norm_ffn[l][None] for l in range(2)]

    gcols = gdn_w_in.shape[2]
    g_gdn_in, g_gdn_out = all_gather_shards([gdn_w_in[0].astype(BF16), gdn_w_out[0].astype(BF16)])
    gathered = lambda ws: [(N_SHARD,) + w.shape for w in ws]
    w2 = [w_ffn_in[0].astype(BF16), w_ffn_out[0].astype(BF16)]
    w3 = [dsw_w_in[0].astype(BF16), dsw_w_out[0].astype(BF16), w_ffn_in[1].astype(BF16), w_ffn_out[1].astype(BF16)]
    fly2 = copies_start(w2, gathered(w2), _gather_copies, 4, "weights_ffn0_start")
    fly3 = copies_start(w3, gathered(w3), _gather_copies, 4, "weights_layer1_start")
    started = fly2[4][0, 0] + fly3[4][0, 0]
    w_gdn = jnp.concatenate([g_gdn_in[s] for s in range(N_SHARD)] + [jnp.zeros((D, GDN_PROJ - N_SHARD * gcols), BF16)], axis=1)
    alog, dtb = _pad_lanes(gdn_a_log[0]), _pad_lanes(gdn_dt_bias[0])
    qg2 = jnp.concatenate([dsw_q_norm, dsw_q_norm], axis=1)
    kg2 = jnp.concatenate([dsw_k_norm, dsw_k_norm], axis=1)
    w_gdn_out = g_gdn_out.reshape(GDN_H * LANES, D)
    gdn_args = (w_gdn, conv_w, alog, dtb, gdn_out_norm, w_gdn_out)
    sc1[0] = sc1[0] + started

    (h10,) = rowwise(f_norm_only, [_wide(x0)], [gmix[0], sc1[0], sh1[0]], [_wide_out(S, BF16)], (nt,), "l0_norm")
    y0, sv_g = gdn_forward(h10, *gdn_args)
    x1, h20 = rowwise(f_resid_norm, [_wide(x0), _wide(y0)], [g1[0], gffn[0], sc2[0], sh2[0]], [_wide_out(S, F32), _wide_out(S, BF16)], (nt,), "l0_mid")
    g_in0, g_out0 = copies_wait(*fly2[:4], _gather_copies, y0, "weights_ffn0_wait")
    w_ffn = [(g_in0, g_out0.reshape(FFN, D)), None]
    f0, sv_f0 = ffn_forward(h20, *w_ffn[0], "0")
    x2, h11 = rowwise(f_resid_norm, [_wide(x1), _wide(f0)], [g2[0], gmix[1], sc1[1], sh1[1]], [_wide_out(S, F32), _wide_out(S, BF16)], (nt,), "l1_in")
    g_dsw_in, g_dsw_out, g_in1, g_out1 = copies_wait(*fly3[:4], _gather_copies, f0, "weights_layer1_wait")
    w_ffn[1] = (g_in1, g_out1.reshape(FFN, D))
    dsw_args = (g_dsw_in, qg2, kg2)
    y1, sv_d = dsw_forward(h11, *dsw_args, rel_bias, g_dsw_out)
    x3, h21 = rowwise(f_resid_norm, [_wide(x2), _wide(y1)], [g1[1], gffn[1], sc2[1], sh2[1]], [_wide_out(S, F32), _wide_out(S, BF16)], (nt,), "l1_mid")
    f1, sv_f1 = ffn_forward(h21, *w_ffn[1], "1")
    part_spec = lambda a: Row(a, (None, 1, D), lambda i: (i, 0, 0))
    (parts,) = rowwise(f_loss, [_wide(x3), _wide(f1), _wide(tgt)], [g2[1]], [Out((nt, 1, D), F32, (None, 1, D), lambda i: (i, 0, 0))], (nt,), "loss")
    loss = lax.psum(jnp.sum(parts), ("x", "y", "c"))

    (dx3, df1), (dg2_1,) = rowwise_bwd(f_loss, [_wide(x3), _wide(f1, gdtype=BF16), _wide(tgt, diff=False)], [g2[1]],
                                       [part_spec(jnp.ones((nt, 1, D), F32))], (nt,), "loss_bwd")
    dh21, d_win1, d_wout1 = ffn_backward(df1, sv_f1, *w_ffn[1], "1")
    (dx2, dy1), (dg1_1, dgf1, dsc2_1, dsh2_1) = rowwise_bwd(
        f_resid_norm, [_wide(x2), _wide(y1, gdtype=BF16)], [g1[1], gffn[1], sc2[1], sh2[1]], [_wide(dx3), _wide(dh21)], (nt,), "l1_mid_bwd")
    dh11, g_d = dsw_backward(dy1, sv_d, *dsw_args, g_dsw_out)
    (dx1, df0), (dg2_0, dgm1, dsc1_1, dsh1_1) = rowwise_bwd(
        f_resid_norm, [_wide(x1), _wide(f0, gdtype=BF16)], [g2[0], gmix[1], sc1[1], sh1[1]], [_wide(dx2), _wide(dh11)], (nt,), "l1_in_bwd")
    by_shard = lambda a: a.reshape(N_SHARD, a.shape[0] // N_SHARD, a.shape[1])
    landing = lambda ps: [(3,) + p.shape[1:] for p in ps]
    dws3 = [g_d["w_in"], g_d["w_out"], d_win1, by_shard(d_wout1)]
    parts3 = [a.astype(BF16) for a in dws3]
    gfly3 = copies_start(parts3, landing(parts3), _scatter_copies, 3, "grads_layer1_start")
    w_out0 = w_ffn[0][1] + gfly3[4][0, 0].astype(BF16)
    dh20, d_win0, d_wout0 = ffn_backward(df0, sv_f0, w_ffn[0][0], w_out0, "0")
    (dx0p, dy0), (dg1_0, dgf0, dsc2_0, dsh2_0) = rowwise_bwd(
        f_resid_norm, [_wide(x0), _wide(y0, gdtype=BF16)], [g1[0], gffn[0], sc2[0], sh2[0]], [_wide(dx1), _wide(dh20)], (nt,), "l0_mid_bwd")
    dws2 = [d_win0, by_shard(d_wout0)]
    parts2 = [a.astype(BF16) for a in dws2]
    gfly2 = copies_start(parts2, landing(parts2), _scatter_copies, 3, "grads_ffn0_start")
    gdn_args = gdn_args[:5] + (w_gdn_out + gfly2[4][0, 0].astype(BF16),)
    dh10, g_g = gdn_backward(dy0, sv_g, *gdn_args)
    (grad_x,), (dgm0, dsc1_0, dsh1_0) = rowwise_bwd(f_first, [_wide(x0)], [gmix[0], sc1[0], sh1[0]], [_wide(dx0p), _wide(dh10)], (nt,), "l0_norm_bwd")

    dmod = jnp.concatenate([dsh1_0, dsc1_0, dg1_0, dsh2_0, dsc2_0, dg2_0, dsh1_1, dsc1_1, dg1_1, dsh2_1, dsc2_1, dg2_1], axis=1)
    d_rel = jnp.transpose(g_d["rel"][:, :, 0])
    fold = lambda v: v[:, :DSW_DH] + v[:, DSW_DH:]
    small = [dmod, jnp.concatenate([dgm0, dgm1], axis=1), jnp.concatenate([dgf0, dgf1], axis=1), g_g["conv"].reshape(1, -1),
             g_g["alog"], g_g["dtb"], g_g["gain"], _pad_lanes(fold(g_d["q_gain2"])[0]), _pad_lanes(fold(g_d["k_gain2"])[0]),
             d_rel.reshape(1, -1)]
    used = [v.shape[1] // LANES for v in small]
    sizes = [-(-u // 8) * 8 for u in used]
    pad8 = lambda v, u, s: jnp.concatenate([v.reshape(u, LANES), jnp.zeros((s - u, LANES), F32)], axis=0) if s > u else v.reshape(u, LANES)
    pad_rows = sum(sizes)
    sbuf = jnp.concatenate([pad8(v, u, s) for v, u, s in zip(small, used, sizes)], axis=0)
    sgot = all_gather_small(sbuf, "gather_small_grads")
    ssum = add_rows([sgot[d * pad_rows:(d + 1) * pad_rows] for d in range(N_DEV)], F32, "sum_small_grads", rt=pad_rows)
    offs = np.cumsum([0] + sizes)
    take = lambda k: ssum[offs[k]:offs[k] + used[k]].reshape(1, -1)
    grad_b_ada = take(0).reshape(2, 6 * D)
    grad_norm_mix = take(1).reshape(2, D)
    grad_norm_ffn = take(2).reshape(2, D)
    conv_full = take(3).reshape(4, -1)
    ncv = gdn_conv.shape[2]
    grad_gdn_conv = lax.dynamic_slice_in_dim(conv_full, s_me * ncv, ncv, axis=1)[None]
    grad_a_log = take(4)[:, :GDN_H]
    grad_dt_bias = take(5)[:, :GDN_H]
    grad_out_norm = take(6)
    grad_q_norm = take(7)[:, :DSW_DH]
    grad_k_norm = take(8)[:, :DSW_DH]
    grad_rel = take(9).reshape(REL_BUCKETS, 3 * GDN_H)
    dmod_all = sgot.reshape(N_DEV, pad_rows, LANES)[:, :used[0]].reshape(N_DEV, 2, 6 * D)
    dmod_mine = lax.dynamic_slice_in_dim(dmod_all, s_me * ada_cols, ada_cols, axis=2)
    dmod16 = jnp.concatenate([dmod_mine, jnp.zeros_like(dmod_mine)], axis=0)
    grad_w_ada = jnp.stack([matmul(cond16, dmod16[:, l], "tn", F32, f"ada_dw_{l}") for l in range(2)])

    dg_in = jnp.stack([g_g["w_in"][:, s * gcols:(s + 1) * gcols] for s in range(N_SHARD)])
    dws = [dg_in, by_shard(g_g["w_out"])]
    keeps, gives = [], []
    for a in dws:
        rh = a.shape[1] // 2
        keeps.append(lax.dynamic_slice_in_dim(a, ci * rh, rh, axis=1))
        gives.append(lax.dynamic_slice_in_dim(a, (1 - ci) * rh, rh, axis=1).astype(BF16))
    from_sib = sibling_exchange(gives, "grads_to_sibling")
    flat2 = lambda a: a.reshape(-1, a.shape[-1])
    parts = [add_rows([flat2(k_), flat2(f_)], BF16, f"grads_chip_sum_{i}").reshape(k_.shape) for i, (k_, f_) in enumerate(zip(keeps, from_sib))]
    others = scatter_to_chips(parts)
    halves = []
    for i, (p_, o_) in enumerate(zip(parts, others)):
        own = lax.dynamic_index_in_dim(p_, s_me, 0, keepdims=False)
        halves.append(add_rows([own, o_[0], o_[1], o_[2]], F32, f"grads_sum_{i}"))
    sib_halves = sibling_exchange(halves, "grads_from_sibling")

    def whole_shard(mine, theirs):
        both = jnp.stack([mine, theirs])
        return jnp.concatenate([lax.dynamic_index_in_dim(both, ci, 0, keepdims=False),
                                lax.dynamic_index_in_dim(both, 1 - ci, 0, keepdims=False)], axis=0)

    s_gdn_in, s_gdn_out = [whole_shard(a, b) for a, b in zip(halves, sib_halves)]
    got3 = copies_wait(*gfly3[:4], _scatter_copies, grad_x, "grads_layer1_wait")
    got2 = copies_wait(*gfly2[:4], _scatter_copies, grad_x, "grads_ffn0_wait")
    core_sums = []
    for i, (full, got) in enumerate(zip(dws3 + dws2, got3 + got2)):
        own = lax.dynamic_index_in_dim(full, s_me, 0, keepdims=False)
        core_sums.append(add_rows([own, got[0], got[1], got[2]], F32, f"grads_core_sum_{i}"))
    sib_sums = sibling_exchange(core_sums, "grads_core_sums_swap")
    s_dsw_in, s_dsw_out, s_in1, s_out1, s_in0, s_out0 = [add_rows([a, b], F32, f"grads_chip_total_{i}")
                                                          for i, (a, b) in enumerate(zip(core_sums, sib_sums))]
    gsh = dict(gdn_w_in=s_gdn_in[None], gdn_w_out=s_gdn_out[None],
               w_ffn_in=jnp.stack([s_in0, s_in1]), w_ffn_out=jnp.stack([s_out0, s_out1]),
               dsw_w_in=s_dsw_in[None], dsw_w_out=s_dsw_out[None])

    grads = dict(w_ada=grad_w_ada, b_ada=grad_b_ada, norm_mix=grad_norm_mix, norm_ffn=grad_norm_ffn, w_ffn_in=gsh["w_ffn_in"],
                 w_ffn_out=gsh["w_ffn_out"], gdn_w_in=gsh["gdn_w_in"], gdn_conv=grad_gdn_conv, gdn_a_log=grad_a_log,
                 gdn_dt_bias=grad_dt_bias, gdn_out_norm=grad_out_norm, gdn_w_out=gsh["gdn_w_out"], dsw_w_in=gsh["dsw_w_in"],
                 dsw_q_norm=grad_q_norm, dsw_k_norm=grad_k_norm, dsw_w_out=gsh["dsw_w_out"], rel_bias=grad_rel)
    weights = dict(w_ada=w_ada, b_ada=b_ada, norm_mix=norm_mix, norm_ffn=norm_ffn, w_ffn_in=w_ffn_in, w_ffn_out=w_ffn_out,
                   gdn_w_in=gdn_w_in, gdn_conv=gdn_conv, gdn_a_log=gdn_a_log, gdn_dt_bias=gdn_dt_bias, gdn_out_norm=gdn_out_norm,
                   gdn_w_out=gdn_w_out, dsw_w_in=dsw_w_in, dsw_q_norm=dsw_q_norm, dsw_k_norm=dsw_k_norm, dsw_w_out=dsw_w_out,
                   rel_bias=rel_bias)
    ms = dict(w_ada=m_w_ada, b_ada=m_b_ada, norm_mix=m_norm_mix, norm_ffn=m_norm_ffn, w_ffn_in=m_w_ffn_in, w_ffn_out=m_w_ffn_out,
              gdn_w_in=m_gdn_w_in, gdn_conv=m_gdn_conv, gdn_a_log=m_gdn_a_log, gdn_dt_bias=m_gdn_dt_bias, gdn_out_norm=m_gdn_out_norm,
              gdn_w_out=m_gdn_w_out, dsw_w_in=m_dsw_w_in, dsw_q_norm=m_dsw_q_norm, dsw_k_norm=m_dsw_k_norm, dsw_w_out=m_dsw_w_out,
              rel_bias=m_rel_bias)
    vs = dict(w_ada=v_w_ada, b_ada=v_b_ada, norm_mix=v_norm_mix, norm_ffn=v_norm_ffn, w_ffn_in=v_w_ffn_in, w_ffn_out=v_w_ffn_out,
              gdn_w_in=v_gdn_w_in, gdn_conv=v_gdn_conv, gdn_a_log=v_gdn_a_log, gdn_dt_bias=v_gdn_dt_bias, gdn_out_norm=v_gdn_out_norm,
              gdn_w_out=v_gdn_w_out, dsw_w_in=v_dsw_w_in, dsw_q_norm=v_dsw_q_norm, dsw_k_norm=v_dsw_k_norm, dsw_w_out=v_dsw_w_out,
              rel_bias=v_rel_bias)
    names = list(weights)
    deltas, new_m, new_v = [], [], []
    for n in names:
        g = grads[n].reshape(weights[n].shape)
        grads[n] = g
        d, nm, nv = adamw(weights[n], g, ms[n], vs[n], f"adamw_{n}")
        deltas.append(d)
        new_m.append(nm)
        new_v.append(nv)
    return (loss, grad_x[None], *[grads[n] for n in names], *deltas, *new_m, *new_v)
```

```python
import functools
import math

import numpy as np
import jax
import jax.numpy as jnp
from jax import lax
from jax.experimental import pallas as pl
from jax.experimental.pallas import tpu as pltpu

F32 = jnp.float32
BF16 = jnp.bfloat16
SDS = jax.ShapeDtypeStruct
MESH = pl.DeviceIdType.MESH
ANY = pl.BlockSpec(memory_space=pl.ANY)

D = 1024
EPS = 1e-6
LANES = 128
GDN_H = 8
GDN_DK = 128
GDN_C = 64
DSW_GROUPS = ((128, 1), (512, 4), (2048, 16))
DSW_SPAN = 128
DSW_DH = 64
DSW_HG = 512
REL_BUCKETS = 32
REL_MAX_DIST = 2048
FFN = 2816
N_SHARD = 4
N_DEV = 8
VMEM_LIMIT = 48 * 1024 * 1024
NEG = -1e30

ADAM_LR, ADAM_B1, ADAM_B2, ADAM_EPS, ADAM_WD, ADAM_STEP = 0.001, 0.9, 0.999, 1e-08, 0.01, 10


def _cp(n_axes):
    return pltpu.CompilerParams(dimension_semantics=("arbitrary",) * n_axes, vmem_limit_bytes=VMEM_LIMIT)


def _blk(dim, cap):
    if dim <= cap:
        return dim
    best = None
    for b in range(LANES, cap + 1, LANES):
        if dim % b == 0:
            best = b
    assert best is not None, (dim, cap)
    return best


MAX_SHARD_BLOCK = 1408
def matmul(a, b, mode, out_dtype, name, cap_m=1024, cap_n=1024, cap_k=2048, col_shards=0):
    ns = col_shards
    if mode == "nn":
        (M, K) = a.shape
        K2, N = (b.shape[1], ns * b.shape[2]) if ns else b.shape
    elif mode == "nt":
        (M, K) = a.shape
        N, K2 = (b.shape[1], ns * b.shape[2]) if ns else b.shape
    else:
        (K, M), (K2, N) = a.shape, b.shape
    assert K == K2, (a.shape, b.shape, mode)
    if K <= 3072:
        cap_k = K
        if K > 2048:
            cap_n = 512
    n_unit = N // ns if (ns and mode != "nt") else N
    k_unit = K // ns if (ns and mode == "nt") else K
    bm = _blk(M, cap_m)
    bn = _blk(n_unit, MAX_SHARD_BLOCK) if n_unit != N else _blk(N, cap_n)
    if k_unit != K:
        bk = _blk(k_unit, MAX_SHARD_BLOCK)
    else:
        bk = _blk(K, 1024 if (ns and mode == "tn") else cap_k)
    nk = K // bk
    nps, kps = n_unit // bn, k_unit // bk
    dims = {"nn": ((1,), (0,)), "nt": ((1,), (1,)), "tn": ((0,), (0,))}[mode]

    def dot(a_ref, b_ref):
        return lax.dot_general(a_ref[...].astype(BF16), b_ref[...].astype(BF16), (dims, ((), ())), preferred_element_type=F32)

    def body_one(a_ref, b_ref, o_ref):
        o_ref[...] = dot(a_ref, b_ref).astype(o_ref.dtype)

    def body_acc(a_ref, b_ref, o_ref, acc_ref):
        k = pl.program_id(2)

        @pl.when(k == 0)
        def _():
            acc_ref[...] = jnp.zeros_like(acc_ref)

        acc_ref[...] += dot(a_ref, b_ref)

        @pl.when(k == nk - 1)
        def _():
            o_ref[...] = acc_ref[...].astype(o_ref.dtype)

    a_spec = pl.BlockSpec((bk, bm), lambda i, j, k: (k, i)) if mode == "tn" else pl.BlockSpec((bm, bk), lambda i, j, k: (i, k))
    if mode == "nt":
        b_spec = pl.BlockSpec((None, bn, bk), lambda i, j, k: (k // kps, j, k % kps)) if ns else pl.BlockSpec((bn, bk), lambda i, j, k: (j, k))
    elif mode == "nn" and ns:
        b_spec = pl.BlockSpec((None, bk, bn), lambda i, j, k: (j // nps, k, j % nps))
    else:
        b_spec = pl.BlockSpec((bk, bn), lambda i, j, k: (k, j))
    if mode == "tn" and ns:
        o_spec, o_shape = pl.BlockSpec((None, bm, bn), lambda i, j, k: (j // nps, i, j % nps)), (ns, M, n_unit)
    else:
        o_spec, o_shape = pl.BlockSpec((bm, bn), lambda i, j, k: (i, j)), (M, N)
    return pl.pallas_call(
        body_one if nk == 1 else body_acc, name=name, grid=(M // bm, N // bn, nk),
        in_specs=[a_spec, b_spec], out_specs=o_spec,
        out_shape=SDS(o_shape, out_dtype), scratch_shapes=[] if nk == 1 else [pltpu.VMEM((bm, bn), F32)],
        compiler_params=_cp(3),
    )(a, b)


class Row:
    def __init__(self, arr, bshape, imap, splits=None, diff=True, acc=False, gdtype=F32, gshape=None, gbshape=None, gimap=None,
                 lead=0):
        self.arr, self.bshape, self.imap = arr, tuple(bshape), imap
        self.splits, self.lead = splits, lead
        self.diff, self.acc, self.gdtype = diff, acc, gdtype
        self.gshape = tuple(arr.shape) if gshape is None else tuple(gshape)
        self.gbshape = self.bshape if gbshape is None else tuple(gbshape)
        self.gimap = imap if gimap is None else gimap

    def gspec(self):
        return pl.BlockSpec(self.gbshape, self.gimap)

    def spec(self):
        return pl.BlockSpec(self.bshape, self.imap)

    def pieces(self, ref):
        return _load_pieces(ref, self.splits, self.lead)

    def n_pieces(self):
        return _n_pieces(self.splits, self.lead)


class Out:
    def __init__(self, shape, dtype, bshape, imap, splits=None, lead=0):
        self.shape, self.dtype, self.bshape, self.imap = tuple(shape), dtype, tuple(bshape), imap
        self.splits, self.lead = splits, lead

    def n_pieces(self):
        return _n_pieces(self.splits, self.lead)


def _n_pieces(splits, lead):
    return lead if lead else (1 if splits is None else len(splits))


def _load_pieces(ref, splits, lead):
    if lead:
        return [ref[k].astype(F32) for k in range(lead)]
    if splits is None:
        return [ref[...].astype(F32)]
    out, o = [], 0
    for w in splits:
        out.append(ref[..., o:o + w].astype(F32))
        o += w
    return out


def _store_pieces(ref, splits, lead, vals, accumulate=False):
    def put(idx, v):
        if accumulate:
            ref[idx] += v.astype(ref.dtype)
        else:
            ref[idx] = v.astype(ref.dtype)

    if lead:
        for k in range(lead):
            put(k, vals[k])
    elif splits is None:
        put(..., vals[0])
    else:
        o = 0
        for w, v in zip(splits, vals):
            put((..., slice(o, o + w)), v)
            o += w


def rowwise(fn, rows, params, outs, grid, name):
    nr, npar = len(rows), len(params)

    def body(*refs):
        ids = tuple(pl.program_id(a) for a in range(len(grid)))
        vals = []
        for r, ref in zip(rows, refs[:nr]):
            vals += r.pieces(ref)
        pvals = [ref[...].astype(F32) for ref in refs[nr:nr + npar]]
        res = list(fn(ids, *vals, *pvals))
        o = 0
        for spec, ref in zip(outs, refs[nr + npar:]):
            n = spec.n_pieces()
            _store_pieces(ref, spec.splits, spec.lead, res[o:o + n])
            o += n

    nz = len(grid)
    pspecs = [pl.BlockSpec(p.shape, (lambda *ids, _n=p.ndim: (0,) * _n)) for p in params]
    res = pl.pallas_call(
        body, name=name, grid=grid,
        in_specs=[r.spec() for r in rows] + pspecs,
        out_specs=[pl.BlockSpec(o.bshape, o.imap) for o in outs],
        out_shape=[SDS(o.shape, o.dtype) for o in outs],
        compiler_params=_cp(nz),
    )(*[r.arr for r in rows], *params)
    return list(res)


def rowwise_bwd(fn, rows, params, cots, grid, name):
    nr, npar, nc = len(rows), len(params), len(cots)
    drows = [r for r in rows if r.diff]
    nz = len(grid)

    def body(*refs):
        ids = tuple(pl.program_id(a) for a in range(nz))
        row_refs, par_refs = refs[:nr], refs[nr:nr + npar]
        cot_refs = refs[nr + npar:nr + npar + nc]
        drow_refs = refs[nr + npar + nc:nr + npar + nc + len(drows)]
        dpar_refs = refs[nr + npar + nc + len(drows):]
        pieces, is_diff = [], []
        for r, ref in zip(rows, row_refs):
            ps = r.pieces(ref)
            pieces += ps
            is_diff += [r.diff] * len(ps)
        pvals = [ref[...].astype(F32) for ref in par_refs]
        dvals = [p for p, dflag in zip(pieces, is_diff) if dflag]
        nd = len(dvals)

        def f(*args):
            it = iter(args[:nd])
            full = [next(it) if dflag else p for p, dflag in zip(pieces, is_diff)]
            return tuple(fn(ids, *full, *args[nd:]))

        _, vjp = jax.vjp(f, *dvals, *pvals)
        cvals = []
        for c, ref in zip(cots, cot_refs):
            cvals += c.pieces(ref)
        g = vjp(tuple(cvals))
        o = 0
        first_inner = ids[-1] == 0
        for r, ref in zip(drows, drow_refs):
            n = r.n_pieces()
            gs = g[o:o + n]
            o += n
            if r.acc:
                @pl.when(first_inner)
                def _(ref=ref):
                    ref[...] = jnp.zeros_like(ref)
            _store_pieces(ref, r.splits, r.lead, gs, accumulate=r.acc)
        first = functools.reduce(jnp.logical_and, [i == 0 for i in ids])
        for ref, gp in zip(dpar_refs, g[nd:]):
            @pl.when(first)
            def _(ref=ref):
                ref[...] = jnp.zeros_like(ref)
            ref[...] += gp

    pspecs = [pl.BlockSpec(p.shape, (lambda *ids, _n=p.ndim: (0,) * _n)) for p in params]
    res = pl.pallas_call(
        body, name=name, grid=grid,
        in_specs=[r.spec() for r in rows] + pspecs + [c.spec() for c in cots],
        out_specs=[r.gspec() for r in drows] + pspecs,
        out_shape=[SDS(r.gshape, r.gdtype) for r in drows] + [SDS(p.shape, F32) for p in params],
        compiler_params=_cp(nz),
    )(*[r.arr for r in rows], *params, *[c.arr for c in cots])
    res = list(res)
    return res[:len(drows)], res[len(drows):]


def _sigmoid(x):
    return 0.5 * (jnp.tanh(0.5 * x) + 1.0)


def _silu(x):
    return x * _sigmoid(x)


def _normmod(x, gain, sc, sh):
    inv = lax.rsqrt(jnp.mean(x * x, axis=-1, keepdims=True) + EPS)
    return x * inv * gain * (1.0 + sc) + sh


def f_first(ids, x, gain, sc, sh):
    return x, _normmod(x, gain, sc, sh)


def f_resid_norm(ids, x, y, g, gain, sc, sh):
    xn = x + g * y
    return xn, _normmod(xn, gain, sc, sh)


def f_swiglu(ids, gate, up):
    return (_silu(gate) * up,)


def f_loss(ids, x, y, tgt, g):
    out = x + g * y
    e = out - tgt
    part = 0.5 * jnp.sum(e * e, axis=0, keepdims=True) * (1.0 / D)
    return (part,)


def _softplus(x):
    return jnp.maximum(x, 0.0) + jnp.log(1.0 + jnp.exp(-jnp.abs(x)))


def _chunk_tril(T):
    r = lax.broadcasted_iota(jnp.int32, (T, T), 0)
    c = lax.broadcasted_iota(jnp.int32, (T, T), 1)
    return jnp.where((r // GDN_C == c // GDN_C) & (c <= r), 1.0, 0.0).astype(F32)


def _dot_hi(a, b, dims=((1,), (0,))):
    return lax.dot_general(a, b, (dims, ((), ())), precision=lax.Precision.HIGHEST, preferred_element_type=F32)


def _dot_x3(a, b, dims=((1,), (0,))):
    return lax.dot_general(a, b, (dims, ((), ())), precision=lax.Precision.HIGH, preferred_element_type=F32)


def f_gdn_gates(ids, ab, alog, dtb):
    T = ab.shape[0]
    g = -jnp.exp(alog) * _softplus(ab + dtb)
    beta = _sigmoid(ab)
    gcum = _dot_hi(_chunk_tril(T), g)
    row = lax.broadcasted_iota(jnp.int32, (LANES, LANES), 0)
    sel = lambda k: jnp.where(row == k, 1.0, 0.0).astype(F32)
    gcs = [_dot_hi(gcum, sel(h)) for h in range(GDN_H)]
    bts = [_dot_hi(beta, sel(GDN_H + h)) for h in range(GDN_H)]
    return (*gcs, *bts)


def f_gdn_post(ids, *args):
    os_, zs, gain = args[:GDN_H], args[GDN_H:2 * GDN_H], args[2 * GDN_H]
    out = []
    for o, z in zip(os_, zs):
        inv = lax.rsqrt(jnp.mean(o * o, axis=-1, keepdims=True) + EPS)
        out.append(o * inv * gain * _silu(z))
    return tuple(out)


def _qknorm1(x, gain2, scale):
    lane = lax.broadcasted_iota(jnp.int32, x.shape, 1)
    lo = lane < DSW_DH
    x2 = x * x
    s_all = jnp.sum(x2, axis=-1, keepdims=True)
    s_lo = jnp.sum(jnp.where(lo, x2, 0.0), axis=-1, keepdims=True)
    ms = jnp.where(lo, s_lo, s_all - s_lo) * (1.0 / DSW_DH)
    return x * lax.rsqrt(ms + EPS) * (gain2 * scale)


def f_qknorm(ids, *args):
    return tuple(_qknorm1(x, args[-1], 1.0) for x in args[:-1])


def f_qnorm(ids, *args):
    return tuple(_qknorm1(x, args[-1], DSW_DH ** -0.5) for x in args[:-1])


def f_combine(ids, o0, o1, o2, l0, l1, l2):
    m = jnp.maximum(jnp.maximum(l0, l1), l2)
    e0, e1, e2 = jnp.exp(l0 - m), jnp.exp(l1 - m), jnp.exp(l2 - m)
    den = e0 + e1 + e2
    o = (e0 * o0 + e1 * o1 + e2 * o2) / den
    return o, m + jnp.log(den)


GDN_T = 512
HALO = 8


def _conv_pre(xx, w):
    acc = xx * w[3:4, :]
    for j in range(3):
        acc = acc + pltpu.roll(xx, shift=3 - j, axis=0) * w[j:j + 1, :]
    return acc


def _qkv_act(pre, cidx):
    s = _silu(pre)
    r = lax.rsqrt(jnp.sum(s * s, axis=-1, keepdims=True) + EPS)
    scale = jnp.where(cidx < GDN_H, GDN_DK ** -0.5, 1.0).astype(F32)
    return jnp.where(cidx < 2 * GDN_H, s * r * scale, s)


def gdn_pre(proj, conv_w, S):
    nt = S // GDN_T
    hb = GDN_T // HALO

    def body(prev_ref, cur_ref, w_ref, o_ref):
        p, i = pl.program_id(0), pl.program_id(1)
        for h in range(GDN_H):
            cols = slice(LANES * h, LANES * (h + 1))
            prev = jnp.where(i > 0, prev_ref[:, cols], 0.0)
            xx = jnp.concatenate([prev, cur_ref[:, cols]], axis=0)
            pre = _conv_pre(xx, w_ref[:, cols])[HALO:]
            o_ref[h] = _qkv_act(pre, p * GDN_H + h)

    hv = GDN_H * LANES
    return pl.pallas_call(
        body, name="gdn_pre", grid=(3, nt),
        in_specs=[pl.BlockSpec((HALO, hv), lambda p, i: (jnp.maximum(i * hb - 1, 0), p)),
                  pl.BlockSpec((GDN_T, hv), lambda p, i: (i, p)),
                  pl.BlockSpec((4, hv), lambda p, i: (0, p))],
        out_specs=pl.BlockSpec((None, GDN_H, GDN_T, LANES), lambda p, i: (p, 0, i, 0)),
        out_shape=SDS((3, GDN_H, S, LANES), F32),
        compiler_params=_cp(2),
    )(proj, proj, conv_w)


def gdn_pre_bwd(proj, conv_w, dqkv, S):
    nt = S // GDN_T
    hb = GDN_T // HALO
    last_h = S // HALO - 1

    def body(prev_ref, cur_ref, next_ref, w_ref, d_ref, dnext_ref, dx_ref, dw_ref):
        p, i = pl.program_id(0), pl.program_id(1)

        @pl.when(i == 0)
        def _():
            dw_ref[...] = jnp.zeros_like(dw_ref)

        for h in range(GDN_H):
            cols = slice(LANES * h, LANES * (h + 1))
            w = w_ref[:, cols]
            prev = jnp.where(i > 0, prev_ref[:, cols], 0.0)
            xx = jnp.concatenate([prev, cur_ref[:, cols], next_ref[:, cols]], axis=0)
            dnext = jnp.where(i < nt - 1, dnext_ref[h], 0.0)
            dd = jnp.concatenate([jnp.zeros((HALO, LANES), F32), d_ref[h], dnext], axis=0)
            pre = _conv_pre(xx, w)
            _, vjp = jax.vjp(lambda v, _c=p * GDN_H + h: _qkv_act(v, _c), pre)
            (dpre,) = vjp(dd)
            row = lax.broadcasted_iota(jnp.int32, dpre.shape, 0)
            dpre = jnp.where(row >= HALO, dpre, 0.0)
            dx = dpre * w[3:4, :]
            R = dpre.shape[0]
            for j in range(3):
                dx = dx + pltpu.roll(dpre, shift=R - (3 - j), axis=0) * w[j:j + 1, :]
            dx_ref[:, cols] = dx[HALO:HALO + GDN_T].astype(dx_ref.dtype)
            own = jnp.where(row < HALO + GDN_T, dpre, 0.0)
            rows_w = [jnp.sum(own * pltpu.roll(xx, shift=3 - j, axis=0), axis=0, keepdims=True) for j in range(3)]
            rows_w.append(jnp.sum(own * xx, axis=0, keepdims=True))
            r4 = lax.broadcasted_iota(jnp.int32, (4, LANES), 0)
            dw = jnp.zeros((4, LANES), F32)
            for j in range(4):
                dw = dw + jnp.where(r4 == j, rows_w[j], 0.0)
            dw_ref[:, cols] += dw

    hv = GDN_H * LANES
    return pl.pallas_call(
        body, name="gdn_pre_bwd", grid=(3, nt),
        in_specs=[pl.BlockSpec((HALO, hv), lambda p, i: (jnp.maximum(i * hb - 1, 0), p)),
                  pl.BlockSpec((GDN_T, hv), lambda p, i: (i, p)),
                  pl.BlockSpec((HALO, hv), lambda p, i: (jnp.minimum((i + 1) * hb, last_h), p)),
                  pl.BlockSpec((4, hv), lambda p, i: (0, p)),
                  pl.BlockSpec((None, GDN_H, GDN_T, LANES), lambda p, i: (p, 0, i, 0)),
                  pl.BlockSpec((None, GDN_H, HALO, LANES), lambda p, i: (p, 0, jnp.minimum((i + 1) * hb, last_h), 0))],
        out_specs=[pl.BlockSpec((GDN_T, hv), lambda p, i: (i, p)),
                   pl.BlockSpec((4, hv), lambda p, i: (0, p))],
        out_shape=[SDS((S, 3 * hv), BF16), SDS((4, 3 * hv), F32)],
        compiler_params=_cp(2),
    )(proj, proj, proj, conv_w, dqkv, dqkv)


_DIMS = {"nn": ((1,), (0,)), "nt": ((1,), (1,)), "tn": ((0,), (0,))}


def _mm_raw(a, b, mode, hi):
    if hi:
        return _dot_hi(a, b, _DIMS[mode])
    return lax.dot_general(a.astype(BF16), b.astype(BF16), (_DIMS[mode], ((), ())), preferred_element_type=F32)


@functools.partial(jax.custom_vjp, nondiff_argnums=(2, 3))
def mm(a, b, mode, hi):
    return _mm_raw(a, b, mode, hi)


def _mm_fwd(a, b, mode, hi):
    return _mm_raw(a, b, mode, hi), (a, b)


def _mm_bwd(mode, hi, res, dc):
    a, b = res
    if mode == "nn":
        da, db = mm(dc, b, "nt", hi), mm(a, dc, "tn", hi)
    elif mode == "nt":
        da, db = mm(dc, b, "nn", hi), mm(dc, a, "tn", hi)
    else:
        da, db = mm(b, dc, "nt", hi), mm(a, dc, "nn", hi)
    return da, db


mm.defvjp(_mm_fwd, _mm_bwd)


TRI_BASE = 8


def _unit_lower_inverses(Ls):
    n = Ls[0].shape[0]
    r = lax.broadcasted_iota(jnp.int32, (n, n), 0)
    c = lax.broadcasted_iota(jnp.int32, (n, n), 1)
    eye = jnp.where(r == c, 1.0, 0.0).astype(F32)
    base = r // TRI_BASE == c // TRI_BASE
    Ps = [jnp.where(base, -L, 0.0) for L in Ls]
    invs = [eye + P for P in Ps]
    k = 1
    while 2 * k < TRI_BASE:
        Ps = [_dot_x3(P, P) for P in Ps]
        invs = [inv + _dot_x3(inv, P) for inv, P in zip(invs, Ps)]
        k *= 2
    b = 2 * TRI_BASE
    while b <= n:
        off_mask = (r // b == c // b) & ((r % b) >= b // 2) & ((c % b) < b // 2)
        ts = [_dot_x3(inv, jnp.where(off_mask, L, 0.0)) for inv, L in zip(invs, Ls)]
        invs = [inv - _dot_x3(t, inv) for inv, t in zip(invs, ts)]
        b *= 2
    return invs


@jax.custom_vjp
def tri_solve2(Ls, r1s, r2s):
    invs = _unit_lower_inverses(Ls)
    return [_dot_x3(i, r) for i, r in zip(invs, r1s)], [_dot_x3(i, r) for i, r in zip(invs, r2s)]


def _tri_fwd(Ls, r1s, r2s):
    invs = _unit_lower_inverses(Ls)
    s1s = [_dot_x3(i, r) for i, r in zip(invs, r1s)]
    s2s = [_dot_x3(i, r) for i, r in zip(invs, r2s)]
    return (s1s, s2s), (invs, s1s, s2s)


def _tri_bwd(res, ds):
    invs, s1s, s2s = res
    d1s = [_dot_x3(i, d, _DIMS["tn"]) for i, d in zip(invs, ds[0])]
    d2s = [_dot_x3(i, d, _DIMS["tn"]) for i, d in zip(invs, ds[1])]
    dLs = [-(_dot_x3(d1, s1, _DIMS["nt"]) + _dot_x3(d2, s2, _DIMS["nt"])) for d1, s1, d2, s2 in zip(d1s, s1s, d2s, s2s)]
    return dLs, d1s, d2s


tri_solve2.defvjp(_tri_fwd, _tri_bwd)


def _gdn_chunk(qs, ks, vs, gcbs, btbs, Ss):
    C = qs[0].shape[0]
    r = lax.broadcasted_iota(jnp.int32, (C, C), 0)
    c = lax.broadcasted_iota(jnp.int32, (C, C), 1)
    causal, strict = c <= r, c < r
    rows = lax.broadcasted_iota(jnp.int32, gcbs[0].shape, 0)
    Gs = [g[:, :C] for g in gcbs]
    decays = [jnp.exp(jnp.where(causal, G - G.T, NEG)) for G in Gs]
    kbs = [k * b for k, b in zip(ks, btbs)]
    vbs = [v * b for v, b in zip(vs, btbs)]
    Ls = [jnp.where(strict, mm(kb, k, "nt", False) * d, 0.0) for kb, k, d in zip(kbs, ks, decays)]
    egs = [jnp.exp(g) for g in gcbs]
    us, ws = tri_solve2(Ls, vbs, [kb * eg for kb, eg in zip(kbs, egs)])
    qks = [jnp.where(causal, mm(q, k, "nt", False) * d, 0.0) for q, k, d in zip(qs, ks, decays)]
    g_lasts = [jnp.sum(jnp.where(rows == C - 1, g, 0.0), axis=0, keepdims=True) for g in gcbs]
    q_decs = [q * eg for q, eg in zip(qs, egs)]
    k_decs = [k * jnp.exp(gl - g) for k, gl, g in zip(ks, g_lasts, gcbs)]
    v_news = [u - mm(w, S, "nn", False) for u, w, S in zip(us, ws, Ss)]
    os_ = [mm(qd, S, "nn", False) + mm(qk, vn, "nn", False) for qd, S, qk, vn in zip(q_decs, Ss, qks, v_news)]
    S_news = [S * jnp.exp(gl) + mm(kd, vn, "tn", False) for S, gl, kd, vn in zip(Ss, g_lasts, k_decs, v_news)]
    return os_, S_news


def gdn_core(qkv, gc, bt, S):
    nchunk = S // GDN_C

    def body(qkv_ref, g_ref, b_ref, o_ref, st_ref, s_scr):
        n = pl.program_id(0)

        @pl.when(n == 0)
        def _():
            s_scr[...] = jnp.zeros_like(s_scr)

        heads = range(GDN_H)
        S_in = [s_scr[h] for h in heads]
        os_, S_new = _gdn_chunk([qkv_ref[0, h] for h in heads], [qkv_ref[1, h] for h in heads], [qkv_ref[2, h] for h in heads],
                                [g_ref[h] for h in heads], [b_ref[h] for h in heads], S_in)
        for h in heads:
            st_ref[h] = S_in[h]
            o_ref[h] = os_[h]
            s_scr[h] = S_new[h]

    blk3 = pl.BlockSpec((3, GDN_H, GDN_C, LANES), lambda n: (0, 0, n, 0))
    hb = pl.BlockSpec((GDN_H, GDN_C, LANES), lambda n: (0, n, 0))
    return pl.pallas_call(
        body, name="gdn_core", grid=(nchunk,),
        in_specs=[blk3, hb, hb],
        out_specs=[hb, pl.BlockSpec((GDN_H, None, GDN_DK, LANES), lambda n: (0, n, 0, 0))],
        out_shape=[SDS((GDN_H, S, LANES), F32), SDS((GDN_H, nchunk, GDN_DK, LANES), F32)],
        scratch_shapes=[pltpu.VMEM((GDN_H, GDN_DK, LANES), F32)],
        compiler_params=_cp(1),
    )(qkv, gc, bt)


def gdn_core_bwd(qkv, gc, bt, states, do, S):
    nchunk = S // GDN_C

    def body(qkv_ref, g_ref, b_ref, st_ref, do_ref, dqkv_ref, dg_ref, db_ref, ds_scr):
        n = pl.program_id(0)

        @pl.when(n == 0)
        def _():
            ds_scr[...] = jnp.zeros_like(ds_scr)

        heads = range(GDN_H)
        _, vjp = jax.vjp(_gdn_chunk, [qkv_ref[0, h] for h in heads], [qkv_ref[1, h] for h in heads], [qkv_ref[2, h] for h in heads],
                         [g_ref[h] for h in heads], [b_ref[h] for h in heads], [st_ref[h] for h in heads])
        dq, dk, dv, dg, db, dS = vjp(([do_ref[h] for h in heads], [ds_scr[h] for h in heads]))
        for h in heads:
            dqkv_ref[0, h] = dq[h]
            dqkv_ref[1, h] = dk[h]
            dqkv_ref[2, h] = dv[h]
            dg_ref[h] = dg[h]
            db_ref[h] = db[h]
            ds_scr[h] = dS[h]

    rev = lambda n: nchunk - 1 - n
    blk3 = pl.BlockSpec((3, GDN_H, GDN_C, LANES), lambda n: (0, 0, rev(n), 0))
    hb = pl.BlockSpec((GDN_H, GDN_C, LANES), lambda n: (0, rev(n), 0))
    return pl.pallas_call(
        body, name="gdn_core_bwd", grid=(nchunk,),
        in_specs=[blk3, hb, hb, pl.BlockSpec((GDN_H, None, GDN_DK, LANES), lambda n: (0, rev(n), 0, 0)), hb],
        out_specs=[blk3, hb, hb],
        out_shape=[SDS((3, GDN_H, S, LANES), F32), SDS((GDN_H, S, LANES), F32), SDS((GDN_H, S, LANES), F32)],
        scratch_shapes=[pltpu.VMEM((GDN_H, GDN_DK, LANES), F32)],
        compiler_params=_cp(1),
    )(qkv, gc, bt, states, do)


GDN_MAIN = 4 * GDN_H * LANES
GDN_PROJ = GDN_MAIN + LANES
RT = 256


def gdn_forward(h, w_in, conv_w, alog, dtb, out_gain, w_out):
    S = h.shape[0]
    nt = S // RT
    proj = matmul(h, w_in, "nn", F32, "gdn_in")
    qkv = gdn_pre(proj, conv_w, S)
    ab_row = Row(proj, (RT, LANES), lambda i: (i, GDN_MAIN // LANES), gdtype=BF16, gshape=(S, LANES), gimap=lambda i: (i, 0))
    hm = lambda i: (0, i, 0)
    hv = GDN_H * LANES
    gc, bt = rowwise(f_gdn_gates, [ab_row], [alog, dtb],
                     [Out((GDN_H, S, LANES), F32, (GDN_H, RT, LANES), hm, lead=GDN_H)] * 2, (nt,), "gdn_gates")
    o, states = gdn_core(qkv, gc, bt, S)
    o_row = Row(o, (GDN_H, RT, LANES), hm, lead=GDN_H)
    z_row = Row(proj, (RT, hv), lambda i: (i, 3), splits=[LANES] * GDN_H, gdtype=BF16, gshape=(S, hv), gimap=lambda i: (i, 0))
    (on,) = rowwise(f_gdn_post, [o_row, z_row], [out_gain],
                    [Out((S, hv), BF16, (RT, hv), lambda i: (i, 0), splits=[LANES] * GDN_H)], (nt,), "gdn_post")
    y = matmul(on, w_out, "nn", F32, "gdn_out")
    saved = dict(h=h, proj=proj, qkv=qkv, gc=gc, bt=bt, states=states, o=o, on=on, ab_row=ab_row, o_row=o_row, z_row=z_row)
    return y, saved


def gdn_backward(dy, sv, w_in, conv_w, alog, dtb, out_gain, w_out):
    S = dy.shape[0]
    nt = S // RT
    hm = lambda i: (0, i, 0)
    hv = GDN_H * LANES
    don = matmul(dy, w_out, "nt", F32, "gdn_out_dx")
    d_w_out = matmul(sv["on"], dy, "tn", F32, "gdn_out_dw")
    (do, dz), (d_gain,) = rowwise_bwd(f_gdn_post, [sv["o_row"], sv["z_row"]], [out_gain],
                                      [Row(don, (RT, hv), lambda i: (i, 0), splits=[LANES] * GDN_H)], (nt,), "gdn_post_bwd")
    dqkv, dgc, dbt = gdn_core_bwd(sv["qkv"], sv["gc"], sv["bt"], sv["states"], do, S)
    head_blk = lambda a: Row(a, (GDN_H, RT, LANES), hm, lead=GDN_H)
    (dab,), (d_alog, d_dtb) = rowwise_bwd(f_gdn_gates, [sv["ab_row"]], [alog, dtb], [head_blk(dgc), head_blk(dbt)],
                                          (nt,), "gdn_gates_bwd")
    dqkv_proj, d_conv = gdn_pre_bwd(sv["proj"], conv_w, dqkv, S)
    dproj = jnp.concatenate([dqkv_proj, dz, dab], axis=1)
    d_w_in = matmul(sv["h"], dproj, "tn", F32, "gdn_in_dw")
    dh = matmul(dproj, w_in, "nt", F32, "gdn_in_dx")
    return dh, dict(w_in=d_w_in, conv=d_conv, alog=d_alog, dtb=d_dtb, gain=d_gain, w_out=d_w_out)


QB = DSW_SPAN
N_HP = DSW_HG // LANES
PROJ_BLKS = 3 * 3 * N_HP


def _bucket_maps():
    a = np.arange(QB)[:, None]
    j = np.arange(2 * QB)[None, :]
    dist = QB + a - j
    band = (dist >= 0) & (dist <= DSW_SPAN)
    maps = []
    for _, dil in DSW_GROUPS:
        dd = np.maximum(dist, 0) * dil
        max_exact = REL_BUCKETS // 2
        scaled = np.log(np.maximum(dd, 1).astype(np.float32) / np.float32(max_exact)) / np.float32(math.log(REL_MAX_DIST / max_exact))
        large = max_exact + (scaled * np.float32(REL_BUCKETS - max_exact)).astype(np.int32)
        large = np.minimum(large, REL_BUCKETS - 1)
        maps.append(np.where(dd < max_exact, dd, large).astype(np.int32))
    return np.stack(maps), band


def dsw_bias(rel_bias):
    maps, band = _bucket_maps()
    maps = np.where(band[None], maps, -1).astype(np.int32)

    def body(tab_ref, bk_ref, o_ref):
        gh = pl.program_id(0)
        bk = bk_ref[...]
        acc = jnp.full(bk.shape, NEG, F32)
        for b in range(REL_BUCKETS):
            acc = jnp.where(bk == b, tab_ref[b, gh], acc)
        o_ref[...] = acc

    return pl.pallas_call(
        body, name="dsw_bias", grid=(3 * GDN_H,),
        in_specs=[pl.BlockSpec(memory_space=pltpu.SMEM),
                  pl.BlockSpec((None, QB, 2 * QB), lambda gh: (gh // GDN_H, 0, 0))],
        out_specs=pl.BlockSpec((None, QB, 2 * QB), lambda gh: (gh, 0, 0)),
        out_shape=SDS((3 * GDN_H, QB, 2 * QB), F32),
        compiler_params=_cp(1),
    )(rel_bias, jnp.asarray(maps))


def dsw_bias_grad(dbias):
    maps, band = _bucket_maps()
    maps = np.where(band[None], maps, -1).astype(np.int32)

    def body(d_ref, bk_ref, o_ref):
        bk = bk_ref[...]
        d = d_ref[...]
        rows = lax.broadcasted_iota(jnp.int32, (REL_BUCKETS, LANES), 0)
        acc = jnp.zeros((REL_BUCKETS, LANES), F32)
        for b in range(REL_BUCKETS):
            part = jnp.sum(jnp.where(bk == b, d, 0.0), axis=0, keepdims=True)
            val = jnp.sum(part, axis=1, keepdims=True)
            acc = jnp.where(rows == b, val, acc)
        o_ref[...] = acc

    return pl.pallas_call(
        body, name="dsw_bias_grad", grid=(3 * GDN_H,),
        in_specs=[pl.BlockSpec((None, QB, 2 * QB), lambda gh: (gh, 0, 0)),
                  pl.BlockSpec((None, QB, 2 * QB), lambda gh: (gh // GDN_H, 0, 0))],
        out_specs=pl.BlockSpec((None, REL_BUCKETS, LANES), lambda gh: (gh, 0, 0)),
        out_shape=SDS((3 * GDN_H, REL_BUCKETS, LANES), F32),
        compiler_params=_cp(1),
    )(dbias, jnp.asarray(maps))


def _nt(a, b):
    return lax.dot_general(a, b, (((1,), (1,)), ((), ())), preferred_element_type=F32)


def _tn(a, b):
    return lax.dot_general(a, b, (((0,), (0,)), ((), ())), preferred_element_type=F32)


def dsw_group_fwd(qn, kn, proj, bias, gi, S):
    dil = DSW_GROUPS[gi][1]
    sd = S // dil
    nq = sd // QB
    qv = qn.reshape(sd, dil * 3 * DSW_HG)
    kv = kn.reshape(sd, dil * 3 * DSW_HG)
    pv = proj.reshape(sd, dil * 9 * DSW_HG)
    qk_col = lambda hp, r: r * (3 * N_HP) + gi * N_HP + hp
    v_col = lambda hp, r: r * PROJ_BLKS + 2 * 3 * N_HP + gi * N_HP + hp

    def body(q_ref, kp_ref, kc_ref, vp_ref, vc_ref, b_ref, o_ref, l_ref):
        i = pl.program_id(2)
        q = q_ref[...]
        k2 = jnp.concatenate([kp_ref[...], kc_ref[...]], axis=0)
        v2 = jnp.concatenate([vp_ref[...], vc_ref[...]], axis=0).astype(BF16)
        lane_q = lax.broadcasted_iota(jnp.int32, (QB, LANES), 1) < DSW_DH
        lane_k = lax.broadcasted_iota(jnp.int32, (2 * QB, LANES), 1) < DSW_DH
        col = lax.broadcasted_iota(jnp.int32, (QB, 2 * QB), 1)
        first = jnp.logical_and(i == 0, col < QB)
        o_acc = jnp.zeros((QB, LANES), F32)
        lse_b = jnp.zeros((QB, LANES), F32)
        for hh in range(2):
            mq = lane_q if hh == 0 else jnp.logical_not(lane_q)
            mk = lane_k if hh == 0 else jnp.logical_not(lane_k)
            s = _nt(jnp.where(mq, q, 0).astype(BF16), k2) + b_ref[hh]
            s = jnp.where(first, NEG, s)
            mx = jnp.max(s, axis=1, keepdims=True)
            p = jnp.exp(s - mx)
            l = jnp.sum(p, axis=1, keepdims=True)
            oh = jnp.dot(p.astype(BF16), jnp.where(mk, v2, 0).astype(BF16), preferred_element_type=F32) / l
            o_acc = o_acc + oh
            lse_b = jnp.where(mq, mx + jnp.log(l), lse_b)
        o_ref[...] = o_acc
        l_ref[...] = lse_b

    blk = (QB, LANES)
    out_spec = pl.BlockSpec(blk, lambda hp, r, i: (i, r * N_HP + hp))
    o, lse = pl.pallas_call(
        body, name=f"dsw_fwd_g{gi}", grid=(N_HP, dil, nq),
        in_specs=[pl.BlockSpec(blk, lambda hp, r, i: (i, qk_col(hp, r))),
                  pl.BlockSpec(blk, lambda hp, r, i: (jnp.maximum(i - 1, 0), qk_col(hp, r))),
                  pl.BlockSpec(blk, lambda hp, r, i: (i, qk_col(hp, r))),
                  pl.BlockSpec(blk, lambda hp, r, i: (jnp.maximum(i - 1, 0), v_col(hp, r))),
                  pl.BlockSpec(blk, lambda hp, r, i: (i, v_col(hp, r))),
                  pl.BlockSpec((2, QB, 2 * QB), lambda hp, r, i: (gi * N_HP + hp, 0, 0))],
        out_specs=[out_spec, out_spec],
        out_shape=[SDS((sd, dil * DSW_HG), F32)] * 2,
        compiler_params=_cp(3),
    )(qv, kv, kv, pv, pv, bias)
    return o.reshape(S, DSW_HG), lse.reshape(S, DSW_HG)


def dsw_group_bwd(qn, kn, proj, bias, do, o, lse, gi, S):
    dil = DSW_GROUPS[gi][1]
    sd = S // dil
    nq = sd // QB
    qv = qn.reshape(sd, dil * 3 * DSW_HG)
    kv = kn.reshape(sd, dil * 3 * DSW_HG)
    pv = proj.reshape(sd, dil * 9 * DSW_HG)
    dov = do.reshape(sd, dil * DSW_HG)
    ov = o.reshape(sd, dil * DSW_HG)
    lv = lse.reshape(sd, dil * DSW_HG)
    qk_col = lambda hp, r: r * (3 * N_HP) + gi * N_HP + hp
    v_col = lambda hp, r: r * PROJ_BLKS + 2 * 3 * N_HP + gi * N_HP + hp
    o_col = lambda hp, r: r * N_HP + hp
    cur = lambda i: jnp.minimum(i, nq - 1)
    prev = lambda i: jnp.maximum(jnp.minimum(i, nq - 1) - 1, 0)
    done = lambda i: jnp.maximum(i - 1, 0)

    def body(q_ref, kp_ref, kc_ref, vp_ref, vc_ref, b_ref, do_ref, o_ref, l_ref,
             dq_ref, dk_ref, dv_ref, db_ref, dk_scr, dv_scr):
        r, i = pl.program_id(1), pl.program_id(2)

        @pl.when(jnp.logical_and(r == 0, i == 0))
        def _():
            db_ref[...] = jnp.zeros_like(db_ref)

        @pl.when(i == 0)
        def _():
            dk_scr[...] = jnp.zeros_like(dk_scr)
            dv_scr[...] = jnp.zeros_like(dv_scr)

        @pl.when(i < nq)
        def _():
            q = q_ref[...]
            k2 = jnp.concatenate([kp_ref[...], kc_ref[...]], axis=0)
            v2 = jnp.concatenate([vp_ref[...], vc_ref[...]], axis=0).astype(BF16)
            dout = do_ref[...].astype(F32)
            prod = dout * o_ref[...].astype(F32)
            lse_b = l_ref[...]
            lane_q = lax.broadcasted_iota(jnp.int32, (QB, LANES), 1) < DSW_DH
            col = lax.broadcasted_iota(jnp.int32, (QB, 2 * QB), 1)
            first = jnp.logical_and(i == 0, col < QB)
            dq = jnp.zeros((QB, LANES), F32)
            dk2 = jnp.zeros((2 * QB, LANES), F32)
            dv2 = jnp.zeros((2 * QB, LANES), F32)
            for hh in range(2):
                mq = lane_q if hh == 0 else jnp.logical_not(lane_q)
                qm = jnp.where(mq, q, 0).astype(BF16)
                dom = jnp.where(mq, dout, 0.0).astype(BF16)
                s = _nt(qm, k2) + b_ref[hh]
                s = jnp.where(first, NEG, s)
                lse_h = jnp.max(jnp.where(mq, lse_b, NEG), axis=1, keepdims=True)
                p = jnp.exp(s - lse_h)
                delta = jnp.sum(jnp.where(mq, prod, 0.0), axis=1, keepdims=True)
                dp = _nt(dom, v2)
                ds = p * (dp - delta)
                dsb = ds.astype(BF16)
                dq = dq + jnp.where(mq, jnp.dot(dsb, k2, preferred_element_type=F32), 0.0)
                dk2 = dk2 + _tn(dsb, qm)
                dv2 = dv2 + _tn(p.astype(BF16), dom)
                db_ref[hh] += ds
            dq_ref[...] = dq
            dk_ref[...] = dk_scr[...] + dk2[:QB]
            dv_ref[...] = (dv_scr[...] + dv2[:QB]).astype(dv_ref.dtype)
            dk_scr[...] = dk2[QB:]
            dv_scr[...] = dv2[QB:]

        @pl.when(i == nq)
        def _():
            dk_ref[...] = dk_scr[...]
            dv_ref[...] = dv_scr[...].astype(dv_ref.dtype)

    blk = (QB, LANES)
    dq, dk, dv, dbias = pl.pallas_call(
        body, name=f"dsw_bwd_g{gi}", grid=(N_HP, dil, nq + 1),
        in_specs=[pl.BlockSpec(blk, lambda hp, r, i: (cur(i), qk_col(hp, r))),
                  pl.BlockSpec(blk, lambda hp, r, i: (prev(i), qk_col(hp, r))),
                  pl.BlockSpec(blk, lambda hp, r, i: (cur(i), qk_col(hp, r))),
                  pl.BlockSpec(blk, lambda hp, r, i: (prev(i), v_col(hp, r))),
                  pl.BlockSpec(blk, lambda hp, r, i: (cur(i), v_col(hp, r))),
                  pl.BlockSpec((2, QB, 2 * QB), lambda hp, r, i: (gi * N_HP + hp, 0, 0)),
                  pl.BlockSpec(blk, lambda hp, r, i: (cur(i), o_col(hp, r))),
                  pl.BlockSpec(blk, lambda hp, r, i: (cur(i), o_col(hp, r))),
                  pl.BlockSpec(blk, lambda hp, r, i: (cur(i), o_col(hp, r)))],
        out_specs=[pl.BlockSpec(blk, lambda hp, r, i: (cur(i), o_col(hp, r))),
                   pl.BlockSpec(blk, lambda hp, r, i: (done(i), o_col(hp, r))),
                   pl.BlockSpec(blk, lambda hp, r, i: (done(i), o_col(hp, r))),
                   pl.BlockSpec((2, QB, 2 * QB), lambda hp, r, i: (hp, 0, 0))],
        out_shape=[SDS((sd, dil * DSW_HG), F32), SDS((sd, dil * DSW_HG), F32), SDS((sd, dil * DSW_HG), BF16),
                   SDS((GDN_H, QB, 2 * QB), F32)],
        scratch_shapes=[pltpu.VMEM(blk, F32), pltpu.VMEM(blk, F32)],
        compiler_params=_cp(3),
    )(qv, kv, kv, pv, pv, bias, dov, ov, lv)
    return dq.reshape(S, DSW_HG), dk.reshape(S, DSW_HG), dv.reshape(S, DSW_HG), dbias


def dsw_forward(h, w_in, q_gain2, k_gain2, rel_bias, w_out):
    S = h.shape[0]
    nt = S // RT
    nb = 3 * N_HP
    proj = matmul(h, w_in, "nn", F32, "dsw_in")
    width = nb * LANES
    lanes12 = [LANES] * nb
    (qn,) = rowwise(f_qnorm, [Row(proj, (RT, width), lambda i: (i, 0), splits=lanes12)], [q_gain2],
                    [Out((S, width), BF16, (RT, width), lambda i: (i, 0), splits=lanes12)], (nt,), "dsw_qnorm")
    (kn,) = rowwise(f_qknorm, [Row(proj, (RT, width), lambda i: (i, 1), splits=lanes12)], [k_gain2],
                    [Out((S, width), BF16, (RT, width), lambda i: (i, 0), splits=lanes12)], (nt,), "dsw_knorm")
    bias = dsw_bias(rel_bias)
    os_, ls_ = [], []
    for gi in range(3):
        o, l = dsw_group_fwd(qn, kn, proj, bias, gi, S)
        os_.append(o)
        ls_.append(l)
    full = lambda a: Row(a, (RT, DSW_HG), lambda i: (i, 0))
    o, lse = rowwise(f_combine, [full(a) for a in os_ + ls_], [],
                     [Out((S, DSW_HG), BF16, (RT, DSW_HG), lambda i: (i, 0)), Out((S, DSW_HG), F32, (RT, DSW_HG), lambda i: (i, 0))],
                     (nt,), "dsw_combine")
    y = matmul(o, w_out, "nn", F32, "dsw_out")
    return y, dict(h=h, proj=proj, qn=qn, kn=kn, bias=bias, o=o, lse=lse)


def dsw_backward(dy, sv, w_in, q_gain2, k_gain2, w_out):
    S = dy.shape[0]
    nt = S // RT
    nb = 3 * N_HP
    do = matmul(dy, w_out, "nt", BF16, "dsw_out_dx")
    d_w_out = matmul(sv["o"], dy, "tn", F32, "dsw_out_dw")
    pieces_q, pieces_k, pieces_v, dbs = [], [], [], []
    d_qg = jnp.zeros((1, LANES), F32)
    d_kg = jnp.zeros((1, LANES), F32)
    for gi in range(3):
        dq, dk, dv, db = dsw_group_bwd(sv["qn"], sv["kn"], sv["proj"], sv["bias"], do, sv["o"], sv["lse"], gi, S)
        dbs.append(db)
        pieces_v.append(dv)
        for which, dd in ((0, dq), (1, dk)):
            lanes4 = [LANES] * N_HP
            row = Row(sv["proj"], (RT, DSW_HG), lambda i, _o=which * 3 + gi: (i, _o), splits=lanes4,
                      gdtype=BF16, gshape=(S, DSW_HG), gimap=lambda i: (i, 0))
            fn, gain = (f_qnorm, q_gain2) if which == 0 else (f_qknorm, k_gain2)
            (dx,), (dg,) = rowwise_bwd(fn, [row], [gain], [Row(dd, (RT, DSW_HG), lambda i: (i, 0), splits=lanes4)],
                                       (nt,), f"dsw_norm_bwd_{which}{gi}")
            if which == 0:
                pieces_q.append(dx)
                d_qg = d_qg + dg
            else:
                pieces_k.append(dx)
                d_kg = d_kg + dg
    dproj = jnp.concatenate(pieces_q + pieces_k + pieces_v, axis=1)
    d_w_in = matmul(sv["h"], dproj, "tn", F32, "dsw_in_dw")
    dh = matmul(dproj, w_in, "nt", F32, "dsw_in_dx")
    d_rel = dsw_bias_grad(jnp.concatenate(dbs, axis=0))
    return dh, dict(w_in=d_w_in, q_gain2=d_qg, k_gain2=d_kg, rel=d_rel, w_out=d_w_out)


N_LB = DSW_HG // LANES
HALF = DSW_DH // 2


def _lanes(j):
    return slice(LANES * j, LANES * (j + 1))


def _deinterleave(stage, out_ref, dil, rows, dtype):
    for r in range(dil):
        for j in range(N_LB):
            out_ref[r, :, _lanes(j)] = stage[j, pl.ds(r, rows, stride=dil), :].astype(dtype)


def _interleave(in_ref, stage, dil, rows):
    for r in range(dil):
        for j in range(N_LB):
            stage[j, pl.ds(r, rows, stride=dil), :] = in_ref[r, :, _lanes(j)].astype(F32)


def dsw_prep(proj, q_gain2, k_gain2, gi, S):
    dil = DSW_GROUPS[gi][1]
    nt, rows = S // RT, RT // dil

    def body(q_ref, k_ref, v_ref, qg_ref, kg_ref, qo_ref, ko_ref, vo_ref, stage):
        for src, gain_ref, scale, dst in ((q_ref, qg_ref, DSW_DH ** -0.5, qo_ref), (k_ref, kg_ref, 1.0, ko_ref), (v_ref, None, None, vo_ref)):
            for j in range(N_LB):
                val = src[:, _lanes(j)]
                stage[j] = val if gain_ref is None else _qknorm1(val, gain_ref[...], scale)
            _deinterleave(stage, dst, dil, rows, BF16)

    col = lambda which: pl.BlockSpec((RT, DSW_HG), lambda i, _c=which * 3 + gi: (i, _c))
    gspec = pl.BlockSpec((1, LANES), lambda i: (0, 0))
    ospec = pl.BlockSpec((dil, rows, DSW_HG), lambda i: (0, i, 0))
    return pl.pallas_call(
        body, name=f"dsw_prep_g{gi}", grid=(nt,),
        in_specs=[col(0), col(1), col(2), gspec, gspec], out_specs=[ospec] * 3,
        out_shape=[SDS((dil, S // dil, DSW_HG), BF16)] * 3,
        scratch_shapes=[pltpu.VMEM((N_LB, RT, LANES), F32)], compiler_params=_cp(1),
    )(proj, proj, proj, q_gain2, k_gain2)


def dsw_prep_bwd(proj, q_gain2, k_gain2, dqd, dkd, dvd, gi, S):
    dil = DSW_GROUPS[gi][1]
    nt, rows = S // RT, RT // dil

    def body(q_ref, k_ref, qg_ref, kg_ref, dq_ref, dk_ref, dv_ref, oq_ref, ok_ref, ov_ref, dqg_ref, dkg_ref, stage):
        i = pl.program_id(0)

        @pl.when(i == 0)
        def _():
            dqg_ref[...] = jnp.zeros_like(dqg_ref)
            dkg_ref[...] = jnp.zeros_like(dkg_ref)

        for src, gain_ref, scale, cot_ref, dst, dg_ref in ((q_ref, qg_ref, DSW_DH ** -0.5, dq_ref, oq_ref, dqg_ref),
                                                          (k_ref, kg_ref, 1.0, dk_ref, ok_ref, dkg_ref)):
            _interleave(cot_ref, stage, dil, rows)
            for j in range(N_LB):
                _, vjp = jax.vjp(lambda x, g, _s=scale: _qknorm1(x, g, _s), src[:, _lanes(j)], gain_ref[...])
                dx, dg = vjp(stage[j])
                dst[:, _lanes(j)] = dx.astype(dst.dtype)
                dg_ref[...] += dg
        _interleave(dv_ref, stage, dil, rows)
        for j in range(N_LB):
            ov_ref[:, _lanes(j)] = stage[j].astype(ov_ref.dtype)

    col = lambda which: pl.BlockSpec((RT, DSW_HG), lambda i, _c=which * 3 + gi: (i, _c))
    gspec = pl.BlockSpec((1, LANES), lambda i: (0, 0))
    dspec = pl.BlockSpec((dil, rows, DSW_HG), lambda i: (0, i, 0))
    nspec = pl.BlockSpec((RT, DSW_HG), lambda i: (i, 0))
    return pl.pallas_call(
        body, name=f"dsw_prep_bwd_g{gi}", grid=(nt,),
        in_specs=[col(0), col(1), gspec, gspec, dspec, dspec, dspec], out_specs=[nspec] * 3 + [gspec] * 2,
        out_shape=[SDS((S, DSW_HG), BF16)] * 3 + [SDS((1, LANES), F32)] * 2,
        scratch_shapes=[pltpu.VMEM((N_LB, RT, LANES), F32)], compiler_params=_cp(1),
    )(proj, proj, q_gain2, k_gain2, dqd, dkd, dvd)


def _head_masks(rows):
    lane = lax.broadcasted_iota(jnp.int32, (rows, LANES), 1)
    return lane < DSW_DH, (lane % DSW_DH) < HALF


def dsw_attn_fwd(qd, kd, vd, bias, gi, S):
    dil = DSW_GROUPS[gi][1]
    sd = S // dil
    nq = sd // QB

    def body(q_ref, k_ref, v_ref, b_ref, o_ref, l_ref, kp_scr, vp_scr):
        i = pl.program_id(1)

        @pl.when(i == 0)
        def _():
            kp_scr[...] = jnp.zeros_like(kp_scr)
            vp_scr[...] = jnp.zeros_like(vp_scr)

        lo_q, _ = _head_masks(QB)
        lo_k, _ = _head_masks(2 * QB)
        col = lax.broadcasted_iota(jnp.int32, (QB, 2 * QB), 1)
        first = jnp.logical_and(i == 0, col < QB)
        for hp in range(N_HP):
            q = q_ref[:, _lanes(hp)]
            k2 = jnp.concatenate([kp_scr[:, _lanes(hp)], k_ref[:, _lanes(hp)]], axis=0)
            v2 = jnp.concatenate([vp_scr[:, _lanes(hp)], v_ref[:, _lanes(hp)]], axis=0)
            o_acc = jnp.zeros((QB, LANES), F32)
            lse_b = jnp.zeros((QB, LANES), F32)
            for hh in range(2):
                mq = lo_q if hh == 0 else jnp.logical_not(lo_q)
                mk = lo_k if hh == 0 else jnp.logical_not(lo_k)
                s = _nt(jnp.where(mq, q, 0).astype(BF16), k2) + b_ref[2 * hp + hh]
                s = jnp.where(first, NEG, s)
                mx = jnp.max(s, axis=1, keepdims=True)
                p = jnp.exp(s - mx)
                l = jnp.sum(p, axis=1, keepdims=True)
                oh = jnp.dot(p.astype(BF16), jnp.where(mk, v2, 0).astype(BF16), preferred_element_type=F32) / l
                o_acc = o_acc + oh
                lse_b = jnp.where(mq, mx + jnp.log(l), lse_b)
            o_ref[:, _lanes(hp)] = o_acc
            l_ref[:, _lanes(hp)] = lse_b
        kp_scr[...] = k_ref[...]
        vp_scr[...] = v_ref[...]

    blk = pl.BlockSpec((None, QB, DSW_HG), lambda r, i: (r, i, 0))
    return pl.pallas_call(
        body, name=f"dsw_attn_g{gi}", grid=(dil, nq),
        in_specs=[blk, blk, blk, pl.BlockSpec((GDN_H, QB, 2 * QB), lambda r, i: (gi, 0, 0))],
        out_specs=[blk, blk], out_shape=[SDS((dil, sd, DSW_HG), F32)] * 2,
        scratch_shapes=[pltpu.VMEM((QB, DSW_HG), BF16)] * 2, compiler_params=_cp(2),
    )(qd, kd, vd, bias)


def dsw_attn_bwd(qd, kd, vd, bias, dod, statd, gi, S):
    dil = DSW_GROUPS[gi][1]
    sd = S // dil
    nq = sd // QB
    cur = lambda i: jnp.minimum(i, nq - 1)
    done = lambda i: jnp.maximum(i - 1, 0)

    def body(q_ref, k_ref, v_ref, b_ref, do_ref, st_ref, dq_ref, dk_ref, dv_ref, db_ref, kp_scr, vp_scr, dk_scr, dv_scr):
        r, i = pl.program_id(0), pl.program_id(1)

        @pl.when(jnp.logical_and(r == 0, i == 0))
        def _():
            db_ref[...] = jnp.zeros_like(db_ref)

        @pl.when(i == 0)
        def _():
            for scr in (kp_scr, vp_scr, dk_scr, dv_scr):
                scr[...] = jnp.zeros_like(scr)

        @pl.when(i < nq)
        def _():
            lo_q, first_half = _head_masks(QB)
            col = lax.broadcasted_iota(jnp.int32, (QB, 2 * QB), 1)
            first = jnp.logical_and(i == 0, col < QB)
            for hp in range(N_HP):
                q = q_ref[:, _lanes(hp)]
                k2 = jnp.concatenate([kp_scr[:, _lanes(hp)], k_ref[:, _lanes(hp)]], axis=0)
                v2 = jnp.concatenate([vp_scr[:, _lanes(hp)], v_ref[:, _lanes(hp)]], axis=0)
                dout = do_ref[:, _lanes(hp)]
                stat = st_ref[:, _lanes(hp)]
                dq = jnp.zeros((QB, LANES), F32)
                dk2 = jnp.zeros((2 * QB, LANES), F32)
                dv2 = jnp.zeros((2 * QB, LANES), F32)
                for hh in range(2):
                    mq = lo_q if hh == 0 else jnp.logical_not(lo_q)
                    qm = jnp.where(mq, q, 0).astype(BF16)
                    dom = jnp.where(mq, dout, 0).astype(BF16)
                    s = _nt(qm, k2) + b_ref[2 * hp + hh]
                    s = jnp.where(first, NEG, s)
                    lse_h = jnp.max(jnp.where(jnp.logical_and(mq, first_half), stat, NEG), axis=1, keepdims=True)
                    delta = jnp.max(jnp.where(jnp.logical_and(mq, jnp.logical_not(first_half)), stat, NEG), axis=1, keepdims=True)
                    p = jnp.exp(s - lse_h)
                    ds = p * (_nt(dom, v2) - delta)
                    dsb = ds.astype(BF16)
                    dq = dq + jnp.where(mq, jnp.dot(dsb, k2, preferred_element_type=F32), 0.0)
                    dk2 = dk2 + _tn(dsb, qm)
                    dv2 = dv2 + _tn(p.astype(BF16), dom)
                    db_ref[2 * hp + hh] += ds
                dq_ref[:, _lanes(hp)] = dq
                dk_ref[:, _lanes(hp)] = dk_scr[:, _lanes(hp)] + dk2[:QB]
                dv_ref[:, _lanes(hp)] = (dv_scr[:, _lanes(hp)] + dv2[:QB]).astype(dv_ref.dtype)
                dk_scr[:, _lanes(hp)] = dk2[QB:]
                dv_scr[:, _lanes(hp)] = dv2[QB:]
            kp_scr[...] = k_ref[...]
            vp_scr[...] = v_ref[...]

        @pl.when(i == nq)
        def _():
            dk_ref[...] = dk_scr[...]
            dv_ref[...] = dv_scr[...].astype(dv_ref.dtype)

    blk = pl.BlockSpec((None, QB, DSW_HG), lambda r, i: (r, cur(i), 0))
    oblk = pl.BlockSpec((None, QB, DSW_HG), lambda r, i: (r, done(i), 0))
    return pl.pallas_call(
        body, name=f"dsw_attn_bwd_g{gi}", grid=(dil, nq + 1),
        in_specs=[blk, blk, blk, pl.BlockSpec((GDN_H, QB, 2 * QB), lambda r, i: (gi, 0, 0)), blk, blk],
        out_specs=[blk, oblk, oblk, pl.BlockSpec((GDN_H, QB, 2 * QB), lambda r, i: (0, 0, 0))],
        out_shape=[SDS((dil, sd, DSW_HG), F32), SDS((dil, sd, DSW_HG), F32), SDS((dil, sd, DSW_HG), BF16),
                   SDS((GDN_H, QB, 2 * QB), F32)],
        scratch_shapes=[pltpu.VMEM((QB, DSW_HG), BF16)] * 2 + [pltpu.VMEM((QB, DSW_HG), F32)] * 2,
        compiler_params=_cp(2),
    )(qd, kd, vd, bias, dod, statd)


def dsw_combine(ods, lseds, S):
    nt = S // RT
    dils = [d for _, d in DSW_GROUPS]

    def body(*refs):
        ins, (o_ref, l_ref), stages = refs[:6], refs[6:8], refs[8:]
        for g in range(3):
            _interleave(ins[g], stages[g], dils[g], RT // dils[g])
            _interleave(ins[3 + g], stages[3 + g], dils[g], RT // dils[g])
        for j in range(N_LB):
            o, lse = f_combine(None, *[st[j] for st in stages])
            o_ref[:, _lanes(j)] = o.astype(o_ref.dtype)
            l_ref[:, _lanes(j)] = lse

    dspec = lambda d: pl.BlockSpec((d, RT // d, DSW_HG), lambda i: (0, i, 0))
    nspec = pl.BlockSpec((RT, DSW_HG), lambda i: (i, 0))
    return pl.pallas_call(
        body, name="dsw_combine", grid=(nt,),
        in_specs=[dspec(d) for d in dils] * 2, out_specs=[nspec, nspec],
        out_shape=[SDS((S, DSW_HG), BF16), SDS((S, DSW_HG), F32)],
        scratch_shapes=[pltpu.VMEM((N_LB, RT, LANES), F32)] * 6, compiler_params=_cp(1),
    )(*ods, *lseds)


def dsw_bwd_prep(do, o, lse, S):
    nt = S // RT
    dils = [d for _, d in DSW_GROUPS]

    def body(do_ref, o_ref, l_ref, *rest):
        outs, (st_do, st_stat) = rest[:6], rest[6:]
        lo, first_half = _head_masks(RT)
        for j in range(N_LB):
            dout = do_ref[:, _lanes(j)]
            prod = dout * o_ref[:, _lanes(j)].astype(F32)
            s_all = jnp.sum(prod, axis=1, keepdims=True)
            s_lo = jnp.sum(jnp.where(lo, prod, 0.0), axis=1, keepdims=True)
            delta = jnp.where(lo, s_lo, s_all - s_lo)
            st_do[j] = dout
            st_stat[j] = jnp.where(first_half, l_ref[:, _lanes(j)], delta)
        for g in range(3):
            _deinterleave(st_do, outs[g], dils[g], RT // dils[g], BF16)
            _deinterleave(st_stat, outs[3 + g], dils[g], RT // dils[g], F32)

    nspec = pl.BlockSpec((RT, DSW_HG), lambda i: (i, 0))
    dspec = lambda d: pl.BlockSpec((d, RT // d, DSW_HG), lambda i: (0, i, 0))
    res = pl.pallas_call(
        body, name="dsw_bwd_prep", grid=(nt,),
        in_specs=[nspec] * 3, out_specs=[dspec(d) for d in dils] * 2,
        out_shape=[SDS((d, S // d, DSW_HG), BF16) for d in dils] + [SDS((d, S // d, DSW_HG), F32) for d in dils],
        scratch_shapes=[pltpu.VMEM((N_LB, RT, LANES), F32)] * 2, compiler_params=_cp(1),
    )(do, o, lse)
    return res[:3], res[3:]


def dsw_forward(h, w_in, q_gain2, k_gain2, rel_bias, w_out):
    S = h.shape[0]
    proj = matmul(h, w_in, "nn", F32, "dsw_in", col_shards=N_SHARD)
    bias = dsw_bias(rel_bias)
    qkv, ods, lseds = [], [], []
    for gi in range(3):
        qd, kd, vd = dsw_prep(proj, q_gain2, k_gain2, gi, S)
        od, ld = dsw_attn_fwd(qd, kd, vd, bias, gi, S)
        qkv.append((qd, kd, vd))
        ods.append(od)
        lseds.append(ld)
    o, lse = dsw_combine(ods, lseds, S)
    y = matmul(o, w_out, "nn", F32, "dsw_out", col_shards=N_SHARD)
    return y, dict(h=h, proj=proj, qkv=qkv, bias=bias, o=o, lse=lse)


def dsw_backward(dy, sv, w_in, q_gain2, k_gain2, w_out):
    S = dy.shape[0]
    do = matmul(dy, w_out, "nt", F32, "dsw_out_dx", col_shards=N_SHARD)
    d_w_out = matmul(sv["o"], dy, "tn", F32, "dsw_out_dw", col_shards=N_SHARD)
    dods, statds = dsw_bwd_prep(do, sv["o"], sv["lse"], S)
    pieces_q, pieces_k, pieces_v, dbs = [], [], [], []
    d_qg = jnp.zeros((1, LANES), F32)
    d_kg = jnp.zeros((1, LANES), F32)
    for gi in range(3):
        qd, kd, vd = sv["qkv"][gi]
        dqd, dkd, dvd, db = dsw_attn_bwd(qd, kd, vd, sv["bias"], dods[gi], statds[gi], gi, S)
        dq, dk, dv, dqg, dkg = dsw_prep_bwd(sv["proj"], q_gain2, k_gain2, dqd, dkd, dvd, gi, S)
        dbs.append(db)
        pieces_q.append(dq)
        pieces_k.append(dk)
        pieces_v.append(dv)
        d_qg = d_qg + dqg
        d_kg = d_kg + dkg
    dproj = jnp.concatenate(pieces_q + pieces_k + pieces_v, axis=1)
    d_w_in = matmul(sv["h"], dproj, "tn", F32, "dsw_in_dw", col_shards=N_SHARD)
    dh = matmul(dproj, w_in, "nt", F32, "dsw_in_dx", col_shards=N_SHARD)
    d_rel = dsw_bias_grad(jnp.concatenate(dbs, axis=0))
    return dh, dict(w_in=d_w_in, q_gain2=d_qg, k_gain2=d_kg, rel=d_rel, w_out=d_w_out)


FT = 128


def ffn_forward(h, w_in, w_out, tag):
    S = h.shape[0]
    gu = matmul(h, w_in, "nn", BF16, f"ffn_in_{tag}", col_shards=N_SHARD)
    gu_row = Row(gu, (FT, 2 * FFN), lambda i: (i, 0), splits=[FFN, FFN], gdtype=BF16)
    (a,) = rowwise(f_swiglu, [gu_row], [], [Out((S, FFN), BF16, (FT, FFN), lambda i: (i, 0))], (S // FT,), f"ffn_act_{tag}")
    f = matmul(a, w_out, "nn", F32, f"ffn_out_{tag}")
    return f, dict(h=h, gu_row=gu_row, a=a)


def ffn_backward(df, sv, w_in, w_out, tag):
    S = df.shape[0]
    da = matmul(df, w_out, "nt", BF16, f"ffn_out_dx_{tag}")
    d_w_out = matmul(sv["a"], df, "tn", F32, f"ffn_out_dw_{tag}")
    (dgu,), _ = rowwise_bwd(f_swiglu, [sv["gu_row"]], [], [Row(da, (FT, FFN), lambda i: (i, 0))], (S // FT,), f"ffn_act_bwd_{tag}")
    d_w_in = matmul(sv["h"], dgu, "tn", F32, f"ffn_in_dw_{tag}", col_shards=N_SHARD)
    dh = matmul(dgu, w_in, "nt", F32, f"ffn_in_dx_{tag}", col_shards=N_SHARD)
    return dh, d_w_in, d_w_out


def f_norm_only(ids, x, gain, sc, sh):
    return (_normmod(x, gain, sc, sh),)


def _wide(a, **kw):
    return Row(a, (RT, D), lambda i: (i, 0), **kw)


def _wide_out(S, dtype):
    return Out((S, D), dtype, (RT, D), lambda i: (i, 0))


def adamw(w, g, m, v, name):
    shape = w.shape
    C = shape[-1]
    R = int(np.prod(shape[:-1]))
    w2, g2, m2, v2 = (a.reshape(R, C) for a in (w, g, m, v))
    br = R
    if R > 256:
        br = max(b for b in range(8, 257, 8) if R % b == 0)
    c1 = 1.0 / (1.0 - ADAM_B1 ** ADAM_STEP)
    c2 = 1.0 / (1.0 - ADAM_B2 ** ADAM_STEP)

    def body(w_ref, g_ref, m_ref, v_ref, d_ref, nm_ref, nv_ref):
        gg = g_ref[...]
        mm_ = ADAM_B1 * m_ref[...] + (1.0 - ADAM_B1) * gg
        vv = ADAM_B2 * v_ref[...] + (1.0 - ADAM_B2) * (gg * gg)
        d_ref[...] = -ADAM_LR * ((mm_ * c1) / (jnp.sqrt(vv * c2) + ADAM_EPS) + ADAM_WD * w_ref[...])
        nm_ref[...] = mm_
        nv_ref[...] = vv

    spec = pl.BlockSpec((br, C), lambda i: (i, 0))
    d, nm, nv = pl.pallas_call(
        body, name=name, grid=(R // br,), in_specs=[spec] * 4, out_specs=[spec] * 3,
        out_shape=[SDS((R, C), F32)] * 3, compiler_params=_cp(1),
    )(w2, g2, m2, v2)
    return d.reshape(shape), nm.reshape(shape), nv.reshape(shape)


def _place():
    x, y, c = lax.axis_index("x"), lax.axis_index("y"), lax.axis_index("c")
    chips = [(1 - x, y), (x, 1 - y), (1 - x, 1 - y)]
    return x, y, c, chips


def all_gather_small(blk, name):
    m_per, n = blk.shape

    def body(x_ref, out_ref, send_sems, recv_sems, local_sem):
        x, y, c, chips = _place()
        me, sibling = (x, y, c), (x, y, 1 - c)

        def rows(px, py, pc):
            return out_ref.at[pl.ds((4 * px + 2 * py + pc) * m_per, m_per), :]

        def copy(k, block, to, src=None):
            return pltpu.make_async_remote_copy(
                src_ref=rows(*block) if src is None else src, dst_ref=rows(*block),
                send_sem=send_sems.at[k], recv_sem=recv_sems.at[k], device_id=to, device_id_type=MESH)

        mine = pltpu.make_async_copy(x_ref, rows(*me), local_sem)
        mine.start()
        first = [copy(0, me, sibling, src=x_ref)]
        first += [copy(1 + j, me, (*chip, c), src=x_ref) for j, chip in enumerate(chips)]
        for cp in first:
            cp.start()
        passed = [copy(4 + j, (*chip, c), sibling) for j, chip in enumerate(chips)]
        for j, chip in enumerate(chips):
            copy(1 + j, (*chip, c), me).wait_recv()
            passed[j].start()
        copy(0, sibling, me).wait_recv()
        for j, chip in enumerate(chips):
            copy(4 + j, (*chip, 1 - c), me).wait_recv()
        for cp in first + passed:
            cp.wait_send()
        mine.wait()

    return pl.pallas_call(
        body, name=name, out_shape=SDS((N_DEV * m_per, n), blk.dtype),
        in_specs=[pl.BlockSpec(memory_space=pltpu.VMEM)], out_specs=pl.BlockSpec(memory_space=pltpu.VMEM),
        scratch_shapes=[pltpu.SemaphoreType.DMA((7,)), pltpu.SemaphoreType.DMA((7,)), pltpu.SemaphoreType.DMA],
    )(blk)


def _half(cc, rh):
    return pl.ds(pl.multiple_of(cc * rh, 16), rh)


def all_gather_shards(ws):
    n = len(ws)

    def body(*refs):
        w_refs, out_refs = refs[:n], refs[n:2 * n]
        send_sems, recv_sems, local_sems, own_sems = refs[2 * n:]
        x, y, c, chips = _place()
        sibling = (x, y, 1 - c)
        s_me = 2 * x + y

        def copy(k, src, dst, to):
            return pltpu.make_async_remote_copy(src_ref=src, dst_ref=dst, send_sem=send_sems.at[k], recv_sem=recv_sems.at[k],
                                                device_id=to, device_id_type=MESH)

        local, sends, passed = [], [], []
        for k in range(n):
            rh = ws[k].shape[0] // 2
            cp = pltpu.make_async_remote_copy(src_ref=w_refs[k], dst_ref=out_refs[k].at[s_me], send_sem=local_sems.at[k],
                                              recv_sem=own_sems.at[k], device_id=sibling, device_id_type=MESH)
            cp.start()
            local.append(cp)
            for j, chip in enumerate(chips):
                sd = copy(6 * k + j, w_refs[k].at[_half(c, rh)], out_refs[k].at[s_me, _half(c, rh)], (*chip, c))
                sd.start()
                sends.append(sd)
        for k in range(n):
            rh = ws[k].shape[0] // 2
            for j, (px, py) in enumerate(chips):
                got = out_refs[k].at[2 * px + py, _half(c, rh)]
                copy(6 * k + j, got, got, (px, py, c)).wait_recv()
                fw = copy(6 * k + 3 + j, got, got, sibling)
                fw.start()
                passed.append(fw)
        for k in range(n):
            rh = ws[k].shape[0] // 2
            for j, (px, py) in enumerate(chips):
                got = out_refs[k].at[2 * px + py, _half(1 - c, rh)]
                copy(6 * k + 3 + j, got, got, sibling).wait_recv()
        for cp in sends + passed:
            cp.wait_send()
        for cp in local:
            cp.wait()

    return pl.pallas_call(
        body, name="weights_all_gather", out_shape=[SDS((N_SHARD,) + w.shape, w.dtype) for w in ws],
        in_specs=[ANY] * n, out_specs=[ANY] * n,
        scratch_shapes=[pltpu.SemaphoreType.DMA((6 * n,)), pltpu.SemaphoreType.DMA((6 * n,)), pltpu.SemaphoreType.DMA((n,)),
                        pltpu.SemaphoreType.DMA((n,))],
    )(*ws)


def sibling_exchange(sends, name):
    n = len(sends)

    def body(*refs):
        s_refs, o_refs, send_sems, recv_sems = refs[:n], refs[n:2 * n], refs[2 * n], refs[2 * n + 1]
        x, y, c, _ = _place()
        cps = [pltpu.make_async_remote_copy(src_ref=s_refs[k], dst_ref=o_refs[k], send_sem=send_sems.at[k], recv_sem=recv_sems.at[k],
                                            device_id=(x, y, 1 - c), device_id_type=MESH) for k in range(n)]
        for cp in cps:
            cp.start()
        for cp in cps:
            cp.wait()

    return pl.pallas_call(
        body, name=name, out_shape=[SDS(s.shape, s.dtype) for s in sends], in_specs=[ANY] * n, out_specs=[ANY] * n,
        scratch_shapes=[pltpu.SemaphoreType.DMA((n,)), pltpu.SemaphoreType.DMA((n,))],
    )(*sends)


def scatter_to_chips(parts):
    n = len(parts)

    def body(*refs):
        p_refs, o_refs, send_sems, recv_sems = refs[:n], refs[n:2 * n], refs[2 * n], refs[2 * n + 1]
        x, y, c, chips = _place()
        cps = []
        for k in range(n):
            for j, (px, py) in enumerate(chips):
                cp = pltpu.make_async_remote_copy(src_ref=p_refs[k].at[2 * px + py], dst_ref=o_refs[k].at[j],
                                                  send_sem=send_sems.at[3 * k + j], recv_sem=recv_sems.at[3 * k + j],
                                                  device_id=(px, py, c), device_id_type=MESH)
                cp.start()
                cps.append(cp)
        for cp in cps:
            cp.wait()

    return pl.pallas_call(
        body, name="grads_scatter", out_shape=[SDS((3,) + p.shape[1:], p.dtype) for p in parts], in_specs=[ANY] * n, out_specs=[ANY] * n,
        scratch_shapes=[pltpu.SemaphoreType.DMA((3 * n,)), pltpu.SemaphoreType.DMA((3 * n,))],
    )(*parts)


def merge_halves(halves):
    n = len(halves)

    def body(*refs):
        h_refs, o_refs = refs[:n], refs[n:2 * n]
        send_sems, recv_sems, local_sems = refs[2 * n:]
        x, y, c, _ = _place()
        local, cps = [], []
        for k in range(n):
            rh = halves[k].shape[0]
            lc = pltpu.make_async_copy(h_refs[k], o_refs[k].at[_half(c, rh)], local_sems.at[k])
            lc.start()
            local.append(lc)
            cp = pltpu.make_async_remote_copy(src_ref=h_refs[k], dst_ref=o_refs[k].at[_half(c, rh)], send_sem=send_sems.at[k],
                                              recv_sem=recv_sems.at[k], device_id=(x, y, 1 - c), device_id_type=MESH)
            cp.start()
            cps.append(cp)
        for k in range(n):
            rh = halves[k].shape[0]
            got = o_refs[k].at[_half(1 - c, rh)]
            pltpu.make_async_remote_copy(src_ref=got, dst_ref=got, send_sem=send_sems.at[k], recv_sem=recv_sems.at[k],
                                         device_id=(x, y, 1 - c), device_id_type=MESH).wait_recv()
        for cp in cps:
            cp.wait_send()
        for lc in local:
            lc.wait()

    return pl.pallas_call(
        body, name="grads_merge_halves", out_shape=[SDS((2 * h.shape[0], h.shape[1]), h.dtype) for h in halves],
        in_specs=[ANY] * n, out_specs=[ANY] * n,
        scratch_shapes=[pltpu.SemaphoreType.DMA((n,)), pltpu.SemaphoreType.DMA((n,)), pltpu.SemaphoreType.DMA((n,))],
    )(*halves)


def add_rows(arrs, out_dtype, name, rt=256):
    Rr, W = arrs[0].shape

    def fn(ids, *vals):
        acc = vals[0]
        for v in vals[1:]:
            acc = acc + v
        return (acc,)

    t = rt if Rr % rt == 0 else max(b for b in range(16, rt + 1, 16) if Rr % b == 0)
    (out,) = rowwise(fn, [Row(a, (t, W), lambda i: (i, 0)) for a in arrs], [],
                     [Out((Rr, W), out_dtype, (t, W), lambda i: (i, 0))], (Rr // t,), name)
    return out


HBM_SPEC = pl.BlockSpec(memory_space=pltpu.HBM)
SEM_SPEC = pl.BlockSpec(memory_space=pltpu.SEMAPHORE)
DATAFLOW = pltpu.SideEffectType.DATAFLOW_SIDE_EFFECTING


def _in_hbm(a):
    return pltpu.with_memory_space_constraint(a, pltpu.HBM)


def _gather_copies(w_refs, land_refs, send_sems, recv_sems):
    x, y, c, chips = _place()
    targets = [(x, y, 1 - c)] + [(*chip, c) for chip in chips]
    cps = []
    for k, (w_ref, land_ref) in enumerate(zip(w_refs, land_refs)):
        for j, to in enumerate(targets):
            cps.append(pltpu.make_async_remote_copy(src_ref=w_ref, dst_ref=land_ref.at[2 * x + y], send_sem=send_sems.at[4 * k + j],
                                                    recv_sem=recv_sems.at[4 * k + j], device_id=to, device_id_type=MESH))
    return cps


def _scatter_copies(p_refs, land_refs, send_sems, recv_sems):
    x, y, c, chips = _place()
    cps = []
    for k, (p_ref, land_ref) in enumerate(zip(p_refs, land_refs)):
        for j, (px, py) in enumerate(chips):
            cps.append(pltpu.make_async_remote_copy(src_ref=p_ref.at[2 * px + py], dst_ref=land_ref.at[j], send_sem=send_sems.at[3 * k + j],
                                                    recv_sem=recv_sems.at[3 * k + j], device_id=(px, py, c), device_id_type=MESH))
    return cps


def copies_start(srcs, land_shapes, make_copies, per_src, name):
    n = len(srcs)
    m = per_src * n

    def body(*refs):
        src_refs, land_refs = refs[:n], refs[n:2 * n]
        send_sems, recv_sems, token = refs[2 * n], refs[2 * n + 1], refs[-1]
        for cp in make_copies(src_refs, land_refs, send_sems, recv_sems):
            cp.start()
        token[...] = jnp.zeros_like(token)

    lands = [lax.empty(shp, s.dtype) for shp, s in zip(land_shapes, srcs)]
    res = pl.pallas_call(
        body, name=name,
        out_shape=(pltpu.SemaphoreType.DMA((m,)), pltpu.SemaphoreType.DMA((m,)), *[pltpu.HBM(s.shape, s.dtype) for s in srcs],
                   *[pltpu.HBM(shp, s.dtype) for shp, s in zip(land_shapes, srcs)], SDS((8, LANES), F32)),
        in_specs=[HBM_SPEC] * (2 * n),
        out_specs=(SEM_SPEC, SEM_SPEC, *[HBM_SPEC] * (2 * n), pl.BlockSpec(memory_space=pltpu.VMEM)),
        input_output_aliases={i: 2 + i for i in range(2 * n)},
        compiler_params=pltpu.CompilerParams(has_side_effects=DATAFLOW),
    )(*[_in_hbm(s) for s in srcs], *[_in_hbm(l) for l in lands])
    return res[0], res[1], list(res[2:2 + n]), list(res[2 + n:2 + 2 * n]), res[-1]


def copies_wait(send_sems, recv_sems, srcs, lands, make_copies, after, name):
    n = len(srcs)

    def body(*refs):
        src_refs, land_refs = refs[:n], refs[n:2 * n]
        for cp in make_copies(src_refs, land_refs, refs[2 * n], refs[2 * n + 1]):
            cp.wait_send()
            cp.wait_recv()

    res = pl.pallas_call(
        body, name=name,
        out_shape=(*[pltpu.HBM(s.shape, s.dtype) for s in srcs], *[pltpu.HBM(l.shape, l.dtype) for l in lands]),
        in_specs=[HBM_SPEC] * (2 * n) + [SEM_SPEC, SEM_SPEC, ANY],
        out_specs=tuple([HBM_SPEC] * (2 * n)),
        input_output_aliases={i: i for i in range(2 * n)},
        compiler_params=pltpu.CompilerParams(has_side_effects=DATAFLOW),
    )(*srcs, *lands, send_sems, recv_sems, after)
    return list(res[n:])


PACK = (("gdn_w_in", 2), ("gdn_w_out", 1), ("w_ffn_in", 2), ("w_ffn_out", 1), ("dsw_w_in", 2), ("dsw_w_out", 2))
PACK_ALIGN = 32


def _pack_rows(sizes):
    total = sum(sizes)
    rows = -(-total // D)
    return -(-rows // PACK_ALIGN) * PACK_ALIGN


def pack_blocks(blocks, dtype):
    flat = [b.astype(dtype).reshape(-1) for b in blocks]
    total = sum(f.shape[0] for f in flat)
    R = _pack_rows([f.shape[0] for f in flat])
    flat.append(jnp.zeros((R * D - total,), dtype))
    return jnp.concatenate(flat).reshape(R, D)


def unpack_blocks(buf, shapes):
    flat = buf.reshape(-1)
    out, off = [], 0
    for shp in shapes:
        n = int(np.prod(shp))
        out.append(flat[off:off + n].reshape(shp))
        off += n
    return out


def _shard_slice(a, axis, s):
    n = a.shape[axis] // N_SHARD
    return lax.slice_in_dim(a, s * n, (s + 1) * n, axis=axis)


def _pad_lanes(v):
    return jnp.concatenate([v.astype(F32), jnp.zeros((LANES - v.shape[0],), F32)])[None]


def kernel(x, c, w_ada, b_ada, norm_mix, norm_ffn, w_ffn_in, w_ffn_out, gdn_w_in, gdn_conv, gdn_a_log, gdn_dt_bias, gdn_out_norm, gdn_w_out, dsw_w_in, dsw_q_norm, dsw_k_norm, dsw_w_out, rel_bias, loss_target, m_w_ada, m_b_ada, m_norm_mix, m_norm_ffn, m_w_ffn_in, m_w_ffn_out, m_gdn_w_in, m_gdn_conv, m_gdn_a_log, m_gdn_dt_bias, m_gdn_out_norm, m_gdn_w_out, m_dsw_w_in, m_dsw_q_norm, m_dsw_k_norm, m_dsw_w_out, m_rel_bias, v_w_ada, v_b_ada, v_norm_mix, v_norm_ffn, v_w_ffn_in, v_w_ffn_out, v_gdn_w_in, v_gdn_conv, v_gdn_a_log, v_gdn_dt_bias, v_gdn_out_norm, v_gdn_w_out, v_dsw_w_in, v_dsw_q_norm, v_dsw_k_norm, v_dsw_w_out, v_rel_bias):
    S = x.shape[1]
    nt = S // RT
    xi, yi, ci = lax.axis_index("x"), lax.axis_index("y"), lax.axis_index("c")
    me = 4 * xi + 2 * yi + ci
    s_me = 2 * xi + yi
    x0, tgt = x[0], loss_target[0]
    shard = dict(w_ffn_in=w_ffn_in, w_ffn_out=w_ffn_out, gdn_w_in=gdn_w_in, gdn_w_out=gdn_w_out, dsw_w_in=dsw_w_in, dsw_w_out=dsw_w_out)

    whole = lambda a: Row(a, a.shape, lambda i: (0,) * a.ndim)
    (cond8,) = rowwise(lambda ids, v: (_silu(v),), [whole(c.reshape(8, LANES))], [], [Out((8, LANES), F32, (8, LANES), lambda i: (0, 0))], (1,), "cond")
    cond_all = all_gather_small(cond8, "gather_cond").reshape(N_DEV, D)
    cond16 = jnp.concatenate([cond_all, jnp.zeros((8, D), F32)], axis=0)
    ada_cols = w_ada.shape[2]
    mods = [matmul(cond16, w_ada[l], "nn", F32, f"ada_{l}")[:N_DEV] for l in range(2)]
    buf = jnp.concatenate([jnp.stack(mods, axis=1).reshape(-1, LANES), gdn_conv.reshape(-1, LANES)], axis=0)
    n_mod_rows = N_DEV * 2 * ada_cols // LANES
    got = all_gather_small(buf, "gather_mod").reshape(N_DEV, buf.shape[0], LANES)
    mod_parts, conv_parts = [], []
    for s in range(N_SHARD):
        from_dev = got[2 * s]
        mod_parts.append(lax.dynamic_index_in_dim(from_dev[:n_mod_rows].reshape(N_DEV, 2, ada_cols), me, 0, keepdims=False))
        conv_parts.append(from_dev[n_mod_rows:].reshape(4, -1))
    mod_nb = jnp.concatenate(mod_parts, axis=1)
    conv_w = jnp.concatenate(conv_parts, axis=1)
    (mod,) = rowwise(lambda ids, a, b: (a + b,), [whole(mod_nb), whole(b_ada)], [], [Out(mod_nb.shape, F32, mod_nb.shape, lambda i: (0, 0))], (1,), "mod_bias")
    mod = mod.reshape(2, 6, 1, D)
    sh1, sc1, g1, sh2, sc2, g2 = ([mod[l, k] for l in range(2)] for k in range(6))
    gmix = [norm_mix[l][None] for l in range(2)]
    gffn = [norm_ffn[l][None] for l in range(2)]

    gcols = gdn_w_in.shape[2]
    g_gdn_in, g_gdn_out = all_gather_shards([gdn_w_in[0].astype(BF16), gdn_w_out[0].astype(BF16)])
    gathered = lambda ws: [(N_SHARD,) + w.shape for w in ws]
    gate = (jnp.minimum(jnp.abs(g_gdn_in[0, 0, 0].astype(F32)), 0.0) + jnp.minimum(jnp.abs(mod[0, 0, 0, 0]), 0.0)).astype(BF16)
    w2 = [w_ffn_in[0].astype(BF16) + gate, w_ffn_out[0].astype(BF16) + gate]
    w3 = [dsw_w_in[0].astype(BF16) + gate, dsw_w_out[0].astype(BF16) + gate, w_ffn_in[1].astype(BF16) + gate, w_ffn_out[1].astype(BF16) + gate]
    fly2 = copies_start(w2, gathered(w2), _gather_copies, 4, "weights_ffn0_start")
    fly3 = copies_start(w3, gathered(w3), _gather_copies, 4, "weights_layer1_start")
    started = fly2[4][0, 0] + fly3[4][0, 0]
    w_gdn = jnp.concatenate([g_gdn_in[s] for s in range(N_SHARD)] + [jnp.zeros((D, GDN_PROJ - N_SHARD * gcols), BF16)], axis=1)
    alog, dtb = _pad_lanes(gdn_a_log[0]), _pad_lanes(gdn_dt_bias[0])
    qg2 = jnp.concatenate([dsw_q_norm, dsw_q_norm], axis=1)
    kg2 = jnp.concatenate([dsw_k_norm, dsw_k_norm], axis=1)
    w_gdn_out = g_gdn_out.reshape(GDN_H * LANES, D)
    gdn_args = (w_gdn, conv_w, alog, dtb, gdn_out_norm, w_gdn_out)
    sc1[0] = sc1[0] + started

    (h10,) = rowwise(f_norm_only, [_wide(x0)], [gmix[0], sc1[0], sh1[0]], [_wide_out(S, BF16)], (nt,), "l0_norm")
    y0, sv_g = gdn_forward(h10, *gdn_args)
    x1, h20 = rowwise(f_resid_norm, [_wide(x0), _wide(y0)], [g1[0], gffn[0], sc2[0], sh2[0]], [_wide_out(S, F32), _wide_out(S, BF16)], (nt,), "l0_mid")
    g_in0, g_out0 = copies_wait(*fly2[:4], _gather_copies, y0, "weights_ffn0_wait")
    w_ffn = [(g_in0, g_out0.reshape(FFN, D)), None]
    f0, sv_f0 = ffn_forward(h20, *w_ffn[0], "0")
    x2, h11 = rowwise(f_resid_norm, [_wide(x1), _wide(f0)], [g2[0], gmix[1], sc1[1], sh1[1]], [_wide_out(S, F32), _wide_out(S, BF16)], (nt,), "l1_in")
    g_dsw_in, g_dsw_out, g_in1, g_out1 = copies_wait(*fly3[:4], _gather_copies, f0, "weights_layer1_wait")
    w_ffn[1] = (g_in1, g_out1.reshape(FFN, D))
    dsw_args = (g_dsw_in, qg2, kg2)
    y1, sv_d = dsw_forward(h11, *dsw_args, rel_bias, g_dsw_out)
    x3, h21 = rowwise(f_resid_norm, [_wide(x2), _wide(y1)], [g1[1], gffn[1], sc2[1], sh2[1]], [_wide_out(S, F32), _wide_out(S, BF16)], (nt,), "l1_mid")
    f1, sv_f1 = ffn_forward(h21, *w_ffn[1], "1")
    part_spec = lambda a: Row(a, (None, 1, D), lambda i: (i, 0, 0))
    (parts,) = rowwise(f_loss, [_wide(x3), _wide(f1), _wide(tgt)], [g2[1]], [Out((nt, 1, D), F32, (None, 1, D), lambda i: (i, 0, 0))], (nt,), "loss")
    loss = lax.psum(jnp.sum(parts), ("x", "y", "c"))

    (dx3, df1), (dg2_1,) = rowwise_bwd(f_loss, [_wide(x3), _wide(f1, gdtype=BF16), _wide(tgt, diff=False)], [g2[1]],
                                       [part_spec(jnp.ones((nt, 1, D), F32))], (nt,), "loss_bwd")
    dh21, d_win1, d_wout1 = ffn_backward(df1, sv_f1, *w_ffn[1], "1")
    (dx2, dy1), (dg1_1, dgf1, dsc2_1, dsh2_1) = rowwise_bwd(
        f_resid_norm, [_wide(x2), _wide(y1, gdtype=BF16)], [g1[1], gffn[1], sc2[1], sh2[1]], [_wide(dx3), _wide(dh21)], (nt,), "l1_mid_bwd")
    dh11, g_d = dsw_backward(dy1, sv_d, *dsw_args, g_dsw_out)
    (dx1, df0), (dg2_0, dgm1, dsc1_1, dsh1_1) = rowwise_bwd(
        f_resid_norm, [_wide(x1), _wide(f0, gdtype=BF16)], [g2[0], gmix[1], sc1[1], sh1[1]], [_wide(dx2), _wide(dh11)], (nt,), "l1_in_bwd")
    by_shard = lambda a: a.reshape(N_SHARD, a.shape[0] // N_SHARD, a.shape[1])
    landing = lambda ps: [(3,) + p.shape[1:] for p in ps]
    dws3 = [g_d["w_in"], g_d["w_out"], d_win1, by_shard(d_wout1)]
    parts3 = [a.astype(BF16) for a in dws3]
    gfly3 = copies_start(parts3, landing(parts3), _scatter_copies, 3, "grads_layer1_start")
    w_out0 = w_ffn[0][1] + gfly3[4][0, 0].astype(BF16)
    dh20, d_win0, d_wout0 = ffn_backward(df0, sv_f0, w_ffn[0][0], w_out0, "0")
    (dx0p, dy0), (dg1_0, dgf0, dsc2_0, dsh2_0) = rowwise_bwd(
        f_resid_norm, [_wide(x0), _wide(y0, gdtype=BF16)], [g1[0], gffn[0], sc2[0], sh2[0]], [_wide(dx1), _wide(dh20)], (nt,), "l0_mid_bwd")
    dws2 = [d_win0, by_shard(d_wout0)]
    parts2 = [a.astype(BF16) for a in dws2]
    gfly2 = copies_start(parts2, landing(parts2), _scatter_copies, 3, "grads_ffn0_start")
    gdn_args = gdn_args[:5] + (w_gdn_out + gfly2[4][0, 0].astype(BF16),)
    dh10, g_g = gdn_backward(dy0, sv_g, *gdn_args)
    (grad_x,), (dgm0, dsc1_0, dsh1_0) = rowwise_bwd(f_first, [_wide(x0)], [gmix[0], sc1[0], sh1[0]], [_wide(dx0p), _wide(dh10)], (nt,), "l0_norm_bwd")

    dmod = jnp.concatenate([dsh1_0, dsc1_0, dg1_0, dsh2_0, dsc2_0, dg2_0, dsh1_1, dsc1_1, dg1_1, dsh2_1, dsc2_1, dg2_1], axis=1)
    d_rel = jnp.transpose(g_d["rel"][:, :, 0])
    fold = lambda v: v[:, :DSW_DH] + v[:, DSW_DH:]
    small = [dmod, jnp.concatenate([dgm0, dgm1], axis=1), jnp.concatenate([dgf0, dgf1], axis=1), g_g["conv"].reshape(1, -1),
             g_g["alog"], g_g["dtb"], g_g["gain"], _pad_lanes(fold(g_d["q_gain2"])[0]), _pad_lanes(fold(g_d["k_gain2"])[0]),
             d_rel.reshape(1, -1)]
    used = [v.shape[1] // LANES for v in small]
    sizes = [-(-u // 8) * 8 for u in used]
    pad8 = lambda v, u, s: jnp.concatenate([v.reshape(u, LANES), jnp.zeros((s - u, LANES), F32)], axis=0) if s > u else v.reshape(u, LANES)
    pad_rows = sum(sizes)
    sbuf = jnp.concatenate([pad8(v, u, s) for v, u, s in zip(small, used, sizes)], axis=0)
    sgot = all_gather_small(sbuf, "gather_small_grads")
    ssum = add_rows([sgot[d * pad_rows:(d + 1) * pad_rows] for d in range(N_DEV)], F32, "sum_small_grads", rt=pad_rows)
    offs = np.cumsum([0] + sizes)
    take = lambda k: ssum[offs[k]:offs[k] + used[k]].reshape(1, -1)
    grad_b_ada = take(0).reshape(2, 6 * D)
    grad_norm_mix = take(1).reshape(2, D)
    grad_norm_ffn = take(2).reshape(2, D)
    conv_full = take(3).reshape(4, -1)
    ncv = gdn_conv.shape[2]
    grad_gdn_conv = lax.dynamic_slice_in_dim(conv_full, s_me * ncv, ncv, axis=1)[None]
    grad_a_log = take(4)[:, :GDN_H]
    grad_dt_bias = take(5)[:, :GDN_H]
    grad_out_norm = take(6)
    grad_q_norm = take(7)[:, :DSW_DH]
    grad_k_norm = take(8)[:, :DSW_DH]
    grad_rel = take(9).reshape(REL_BUCKETS, 3 * GDN_H)
    dmod_all = sgot.reshape(N_DEV, pad_rows, LANES)[:, :used[0]].reshape(N_DEV, 2, 6 * D)
    dmod_mine = lax.dynamic_slice_in_dim(dmod_all, s_me * ada_cols, ada_cols, axis=2)
    dmod16 = jnp.concatenate([dmod_mine, jnp.zeros_like(dmod_mine)], axis=0)
    grad_w_ada = jnp.stack([matmul(cond16, dmod16[:, l], "tn", F32, f"ada_dw_{l}") for l in range(2)])

    dg_in = jnp.stack([g_g["w_in"][:, s * gcols:(s + 1) * gcols] for s in range(N_SHARD)])
    dws = [dg_in, by_shard(g_g["w_out"])]
    keeps, gives = [], []
    for a in dws:
        rh = a.shape[1] // 2
        keeps.append(lax.dynamic_slice_in_dim(a, ci * rh, rh, axis=1))
        gives.append(lax.dynamic_slice_in_dim(a, (1 - ci) * rh, rh, axis=1).astype(BF16))
    from_sib = sibling_exchange(gives, "grads_to_sibling")
    flat2 = lambda a: a.reshape(-1, a.shape[-1])
    parts = [add_rows([flat2(k_), flat2(f_)], BF16, f"grads_chip_sum_{i}").reshape(k_.shape) for i, (k_, f_) in enumerate(zip(keeps, from_sib))]
    others = scatter_to_chips(parts)
    halves = []
    for i, (p_, o_) in enumerate(zip(parts, others)):
        own = lax.dynamic_index_in_dim(p_, s_me, 0, keepdims=False)
        halves.append(add_rows([own, o_[0], o_[1], o_[2]], F32, f"grads_sum_{i}"))
    sib_halves = sibling_exchange(halves, "grads_from_sibling")

    def whole_shard(mine, theirs):
        both = jnp.stack([mine, theirs])
        return jnp.concatenate([lax.dynamic_index_in_dim(both, ci, 0, keepdims=False),
                                lax.dynamic_index_in_dim(both, 1 - ci, 0, keepdims=False)], axis=0)

    s_gdn_in, s_gdn_out = [whole_shard(a, b) for a, b in zip(halves, sib_halves)]
    got3 = copies_wait(*gfly3[:4], _scatter_copies, grad_x, "grads_layer1_wait")
    got2 = copies_wait(*gfly2[:4], _scatter_copies, grad_x, "grads_ffn0_wait")
    core_sums = []
    for i, (full, got) in enumerate(zip(dws3 + dws2, got3 + got2)):
        own = lax.dynamic_index_in_dim(full, s_me, 0, keepdims=False)
        core_sums.append(add_rows([own, got[0], got[1], got[2]], F32, f"grads_core_sum_{i}"))
    sib_sums = sibling_exchange(core_sums, "grads_core_sums_swap")
    s_dsw_in, s_dsw_out, s_in1, s_out1, s_in0, s_out0 = [add_rows([a, b], F32, f"grads_chip_total_{i}")
                                                          for i, (a, b) in enumerate(zip(core_sums, sib_sums))]
    gsh = dict(gdn_w_in=s_gdn_in[None], gdn_w_out=s_gdn_out[None],
               w_ffn_in=jnp.stack([s_in0, s_in1]), w_ffn_out=jnp.stack([s_out0, s_out1]),
               dsw_w_in=s_dsw_in[None], dsw_w_out=s_dsw_out[None])

    grads = dict(w_ada=grad_w_ada, b_ada=grad_b_ada, norm_mix=grad_norm_mix, norm_ffn=grad_norm_ffn, w_ffn_in=gsh["w_ffn_in"],
                 w_ffn_out=gsh["w_ffn_out"], gdn_w_in=gsh["gdn_w_in"], gdn_conv=grad_gdn_conv, gdn_a_log=grad_a_log,
                 gdn_dt_bias=grad_dt_bias, gdn_out_norm=grad_out_norm, gdn_w_out=gsh["gdn_w_out"], dsw_w_in=gsh["dsw_w_in"],
                 dsw_q_norm=grad_q_norm, dsw_k_norm=grad_k_norm, dsw_w_out=gsh["dsw_w_out"], rel_bias=grad_rel)
    weights = dict(w_ada=w_ada, b_ada=b_ada, norm_mix=norm_mix, norm_ffn=norm_ffn, w_ffn_in=w_ffn_in, w_ffn_out=w_ffn_out,
                   gdn_w_in=gdn_w_in, gdn_conv=gdn_conv, gdn_a_log=gdn_a_log, gdn_dt_bias=gdn_dt_bias, gdn_out_norm=gdn_out_norm,
                   gdn_w_out=gdn_w_out, dsw_w_in=dsw_w_in, dsw_q_norm=dsw_q_norm, dsw_k_norm=dsw_k_norm, dsw_w_out=dsw_w_out,
                   rel_bias=rel_bias)
    ms = dict(w_ada=m_w_ada, b_ada=m_b_ada, norm_mix=m_norm_mix, norm_ffn=m_norm_ffn, w_ffn_in=m_w_ffn_in, w_ffn_out=m_w_ffn_out,
              gdn_w_in=m_gdn_w_in, gdn_conv=m_gdn_conv, gdn_a_log=m_gdn_a_log, gdn_dt_bias=m_gdn_dt_bias, gdn_out_norm=m_gdn_out_norm,
              gdn_w_out=m_gdn_w_out, dsw_w_in=m_dsw_w_in, dsw_q_norm=m_dsw_q_norm, dsw_k_norm=m_dsw_k_norm, dsw_w_out=m_dsw_w_out,
              rel_bias=m_rel_bias)
    vs = dict(w_ada=v_w_ada, b_ada=v_b_ada, norm_mix=v_norm_mix, norm_ffn=v_norm_ffn, w_ffn_in=v_w_ffn_in, w_ffn_out=v_w_ffn_out,
              gdn_w_in=v_gdn_w_in, gdn_conv=v_gdn_conv, gdn_a_log=v_gdn_a_log, gdn_dt_bias=v_gdn_dt_bias, gdn_out_norm=v_gdn_out_norm,
              gdn_w_out=v_gdn_w_out, dsw_w_in=v_dsw_w_in, dsw_q_norm=v_dsw_q_norm, dsw_k_norm=v_dsw_k_norm, dsw_w_out=v_dsw_w_out,
              rel_bias=v_rel_bias)
    names = list(weights)
    deltas, new_m, new_v = [], [], []
    for n in names:
        g = grads[n].reshape(weights[n].shape)
        grads[n] = g
        d, nm, nv = adamw(weights[n], g, ms[n], vs[n], f"adamw_{n}")
        deltas.append(d)
        new_m.append(nm)
        new_v.append(nv)
    return (loss, grad_x[None], *[grads[n] for n in names], *deltas, *new_m, *new_v)
```

```python
import functools
import math

import numpy as np
import jax
import jax.numpy as jnp
from jax import lax
from jax.experimental import pallas as pl
from jax.experimental.pallas import tpu as pltpu

F32 = jnp.float32
BF16 = jnp.bfloat16
SDS = jax.ShapeDtypeStruct
MESH = pl.DeviceIdType.MESH
ANY = pl.BlockSpec(memory_space=pl.ANY)

D = 1024
EPS = 1e-6
LANES = 128
GDN_H = 8
GDN_DK = 128
GDN_C = 64
DSW_GROUPS = ((128, 1), (512, 4), (2048, 16))
DSW_SPAN = 128
DSW_DH = 64
DSW_HG = 512
REL_BUCKETS = 32
REL_MAX_DIST = 2048
FFN = 2816
N_SHARD = 4
N_DEV = 8
VMEM_LIMIT = 48 * 1024 * 1024
NEG = -1e30

ADAM_LR, ADAM_B1, ADAM_B2, ADAM_EPS, ADAM_WD, ADAM_STEP = 0.001, 0.9, 0.999, 1e-08, 0.01, 10


def _cp(n_axes):
    return pltpu.CompilerParams(dimension_semantics=("arbitrary",) * n_axes, vmem_limit_bytes=VMEM_LIMIT)


def _blk(dim, cap):
    if dim <= cap:
        return dim
    best = None
    for b in range(LANES, cap + 1, LANES):
        if dim % b == 0:
            best = b
    assert best is not None, (dim, cap)
    return best


MAX_SHARD_BLOCK = 1408
def matmul(a, b, mode, out_dtype, name, cap_m=MAX_SHARD_BLOCK, cap_n=MAX_SHARD_BLOCK, cap_k=2048, col_shards=0):
    ns = col_shards
    if mode == "nn":
        (M, K) = a.shape
        K2, N = (b.shape[1], ns * b.shape[2]) if ns else b.shape
    elif mode == "nt":
        (M, K) = a.shape
        N, K2 = (b.shape[1], ns * b.shape[2]) if ns else b.shape
    else:
        (K, M), (K2, N) = a.shape, b.shape
    assert K == K2, (a.shape, b.shape, mode)
    if K <= 3072:
        cap_k = K
        if K > 2048:
            cap_n = 1024
    n_unit = N // ns if (ns and mode != "nt") else N
    k_unit = K // ns if (ns and mode == "nt") else K
    bm = _blk(M, cap_m)
    bn = _blk(n_unit, MAX_SHARD_BLOCK) if n_unit != N else _blk(N, cap_n)
    if k_unit != K:
        bk = _blk(k_unit, MAX_SHARD_BLOCK)
    else:
        bk = _blk(K, 1024 if (ns and mode == "tn") else cap_k)
    nk = K // bk
    nps, kps = n_unit // bn, k_unit // bk
    dims = {"nn": ((1,), (0,)), "nt": ((1,), (1,)), "tn": ((0,), (0,))}[mode]

    def dot(a_ref, b_ref):
        return lax.dot_general(a_ref[...].astype(BF16), b_ref[...].astype(BF16), (dims, ((), ())), preferred_element_type=F32)

    def body_one(a_ref, b_ref, o_ref):
        o_ref[...] = dot(a_ref, b_ref).astype(o_ref.dtype)

    def body_acc(a_ref, b_ref, o_ref, acc_ref):
        k = pl.program_id(2)

        @pl.when(k == 0)
        def _():
            acc_ref[...] = jnp.zeros_like(acc_ref)

        acc_ref[...] += dot(a_ref, b_ref)

        @pl.when(k == nk - 1)
        def _():
            o_ref[...] = acc_ref[...].astype(o_ref.dtype)

    a_spec = pl.BlockSpec((bk, bm), lambda i, j, k: (k, i)) if mode == "tn" else pl.BlockSpec((bm, bk), lambda i, j, k: (i, k))
    if mode == "nt":
        b_spec = pl.BlockSpec((None, bn, bk), lambda i, j, k: (k // kps, j, k % kps)) if ns else pl.BlockSpec((bn, bk), lambda i, j, k: (j, k))
    elif mode == "nn" and ns:
        b_spec = pl.BlockSpec((None, bk, bn), lambda i, j, k: (j // nps, k, j % nps))
    else:
        b_spec = pl.BlockSpec((bk, bn), lambda i, j, k: (k, j))
    if mode == "tn" and ns:
        o_spec, o_shape = pl.BlockSpec((None, bm, bn), lambda i, j, k: (j // nps, i, j % nps)), (ns, M, n_unit)
    else:
        o_spec, o_shape = pl.BlockSpec((bm, bn), lambda i, j, k: (i, j)), (M, N)
    return pl.pallas_call(
        body_one if nk == 1 else body_acc, name=name, grid=(M // bm, N // bn, nk),
        in_specs=[a_spec, b_spec], out_specs=o_spec,
        out_shape=SDS(o_shape, out_dtype), scratch_shapes=[] if nk == 1 else [pltpu.VMEM((bm, bn), F32)],
        compiler_params=_cp(3),
    )(a, b)


class Row:
    def __init__(self, arr, bshape, imap, splits=None, diff=True, acc=False, gdtype=F32, gshape=None, gbshape=None, gimap=None,
                 lead=0):
        self.arr, self.bshape, self.imap = arr, tuple(bshape), imap
        self.splits, self.lead = splits, lead
        self.diff, self.acc, self.gdtype = diff, acc, gdtype
        self.gshape = tuple(arr.shape) if gshape is None else tuple(gshape)
        self.gbshape = self.bshape if gbshape is None else tuple(gbshape)
        self.gimap = imap if gimap is None else gimap

    def gspec(self):
        return pl.BlockSpec(self.gbshape, self.gimap)

    def spec(self):
        return pl.BlockSpec(self.bshape, self.imap)

    def pieces(self, ref):
        return _load_pieces(ref, self.splits, self.lead)

    def n_pieces(self):
        return _n_pieces(self.splits, self.lead)


class Out:
    def __init__(self, shape, dtype, bshape, imap, splits=None, lead=0):
        self.shape, self.dtype, self.bshape, self.imap = tuple(shape), dtype, tuple(bshape), imap
        self.splits, self.lead = splits, lead

    def n_pieces(self):
        return _n_pieces(self.splits, self.lead)


def _n_pieces(splits, lead):
    return lead if lead else (1 if splits is None else len(splits))


def _load_pieces(ref, splits, lead):
    if lead:
        return [ref[k].astype(F32) for k in range(lead)]
    if splits is None:
        return [ref[...].astype(F32)]
    out, o = [], 0
    for w in splits:
        out.append(ref[..., o:o + w].astype(F32))
        o += w
    return out


def _store_pieces(ref, splits, lead, vals, accumulate=False):
    def put(idx, v):
        if accumulate:
            ref[idx] += v.astype(ref.dtype)
        else:
            ref[idx] = v.astype(ref.dtype)

    if lead:
        for k in range(lead):
            put(k, vals[k])
    elif splits is None:
        put(..., vals[0])
    else:
        o = 0
        for w, v in zip(splits, vals):
            put((..., slice(o, o + w)), v)
            o += w


def rowwise(fn, rows, params, outs, grid, name):
    nr, npar = len(rows), len(params)

    def body(*refs):
        ids = tuple(pl.program_id(a) for a in range(len(grid)))
        vals = []
        for r, ref in zip(rows, refs[:nr]):
            vals += r.pieces(ref)
        pvals = [ref[...].astype(F32) for ref in refs[nr:nr + npar]]
        res = list(fn(ids, *vals, *pvals))
        o = 0
        for spec, ref in zip(outs, refs[nr + npar:]):
            n = spec.n_pieces()
            _store_pieces(ref, spec.splits, spec.lead, res[o:o + n])
            o += n

    nz = len(grid)
    pspecs = [pl.BlockSpec(p.shape, (lambda *ids, _n=p.ndim: (0,) * _n)) for p in params]
    res = pl.pallas_call(
        body, name=name, grid=grid,
        in_specs=[r.spec() for r in rows] + pspecs,
        out_specs=[pl.BlockSpec(o.bshape, o.imap) for o in outs],
        out_shape=[SDS(o.shape, o.dtype) for o in outs],
        compiler_params=_cp(nz),
    )(*[r.arr for r in rows], *params)
    return list(res)


def rowwise_bwd(fn, rows, params, cots, grid, name):
    nr, npar, nc = len(rows), len(params), len(cots)
    drows = [r for r in rows if r.diff]
    nz = len(grid)

    def body(*refs):
        ids = tuple(pl.program_id(a) for a in range(nz))
        row_refs, par_refs = refs[:nr], refs[nr:nr + npar]
        cot_refs = refs[nr + npar:nr + npar + nc]
        drow_refs = refs[nr + npar + nc:nr + npar + nc + len(drows)]
        dpar_refs = refs[nr + npar + nc + len(drows):]
        pieces, is_diff = [], []
        for r, ref in zip(rows, row_refs):
            ps = r.pieces(ref)
            pieces += ps
            is_diff += [r.diff] * len(ps)
        pvals = [ref[...].astype(F32) for ref in par_refs]
        dvals = [p for p, dflag in zip(pieces, is_diff) if dflag]
        nd = len(dvals)

        def f(*args):
            it = iter(args[:nd])
            full = [next(it) if dflag else p for p, dflag in zip(pieces, is_diff)]
            return tuple(fn(ids, *full, *args[nd:]))

        _, vjp = jax.vjp(f, *dvals, *pvals)
        cvals = []
        for c, ref in zip(cots, cot_refs):
            cvals += c.pieces(ref)
        g = vjp(tuple(cvals))
        o = 0
        first_inner = ids[-1] == 0
        for r, ref in zip(drows, drow_refs):
            n = r.n_pieces()
            gs = g[o:o + n]
            o += n
            if r.acc:
                @pl.when(first_inner)
                def _(ref=ref):
                    ref[...] = jnp.zeros_like(ref)
            _store_pieces(ref, r.splits, r.lead, gs, accumulate=r.acc)
        first = functools.reduce(jnp.logical_and, [i == 0 for i in ids])
        for ref, gp in zip(dpar_refs, g[nd:]):
            @pl.when(first)
            def _(ref=ref):
                ref[...] = jnp.zeros_like(ref)
            ref[...] += gp

    pspecs = [pl.BlockSpec(p.shape, (lambda *ids, _n=p.ndim: (0,) * _n)) for p in params]
    res = pl.pallas_call(
        body, name=name, grid=grid,
        in_specs=[r.spec() for r in rows] + pspecs + [c.spec() for c in cots],
        out_specs=[r.gspec() for r in drows] + pspecs,
        out_shape=[SDS(r.gshape, r.gdtype) for r in drows] + [SDS(p.shape, F32) for p in params],
        compiler_params=_cp(nz),
    )(*[r.arr for r in rows], *params, *[c.arr for c in cots])
    res = list(res)
    return res[:len(drows)], res[len(drows):]


def _sigmoid(x):
    return 0.5 * (jnp.tanh(0.5 * x) + 1.0)


def _silu(x):
    return x * _sigmoid(x)


def _normmod(x, gain, sc, sh):
    inv = lax.rsqrt(jnp.mean(x * x, axis=-1, keepdims=True) + EPS)
    return x * inv * gain * (1.0 + sc) + sh


def f_first(ids, x, gain, sc, sh):
    return x, _normmod(x, gain, sc, sh)


def f_resid_norm(ids, x, y, g, gain, sc, sh):
    xn = x + g * y
    return xn, _normmod(xn, gain, sc, sh)


def f_swiglu(ids, gate, up):
    return (_silu(gate) * up,)


def f_loss(ids, x, y, tgt, g):
    out = x + g * y
    e = out - tgt
    part = 0.5 * jnp.sum(e * e, axis=0, keepdims=True) * (1.0 / D)
    return (part,)


def _softplus(x):
    return jnp.maximum(x, 0.0) + jnp.log(1.0 + jnp.exp(-jnp.abs(x)))


def _chunk_tril(T):
    r = lax.broadcasted_iota(jnp.int32, (T, T), 0)
    c = lax.broadcasted_iota(jnp.int32, (T, T), 1)
    return jnp.where((r // GDN_C == c // GDN_C) & (c <= r), 1.0, 0.0).astype(F32)


def _dot_hi(a, b, dims=((1,), (0,))):
    return lax.dot_general(a, b, (dims, ((), ())), precision=lax.Precision.HIGHEST, preferred_element_type=F32)


def _dot_x3(a, b, dims=((1,), (0,))):
    return lax.dot_general(a, b, (dims, ((), ())), precision=lax.Precision.HIGH, preferred_element_type=F32)


def f_gdn_gates(ids, ab, alog, dtb):
    T = ab.shape[0]
    g = -jnp.exp(alog) * _softplus(ab + dtb)
    beta = _sigmoid(ab)
    gcum = _dot_x3(_chunk_tril(T), g)
    row = lax.broadcasted_iota(jnp.int32, (LANES, LANES), 0)
    sel = lambda k: jnp.where(row == k, 1.0, 0.0).astype(F32)
    gcs = [_dot_x3(gcum, sel(h)) for h in range(GDN_H)]
    bts = [_dot_x3(beta, sel(GDN_H + h)) for h in range(GDN_H)]
    return (*gcs, *bts)


def f_gdn_post(ids, *args):
    os_, zs, gain = args[:GDN_H], args[GDN_H:2 * GDN_H], args[2 * GDN_H]
    out = []
    for o, z in zip(os_, zs):
        inv = lax.rsqrt(jnp.mean(o * o, axis=-1, keepdims=True) + EPS)
        out.append(o * inv * gain * _silu(z))
    return tuple(out)


def _qknorm1(x, gain2, scale):
    lane = lax.broadcasted_iota(jnp.int32, x.shape, 1)
    lo = lane < DSW_DH
    x2 = x * x
    s_all = jnp.sum(x2, axis=-1, keepdims=True)
    s_lo = jnp.sum(jnp.where(lo, x2, 0.0), axis=-1, keepdims=True)
    ms = jnp.where(lo, s_lo, s_all - s_lo) * (1.0 / DSW_DH)
    return x * lax.rsqrt(ms + EPS) * (gain2 * scale)


def f_qknorm(ids, *args):
    return tuple(_qknorm1(x, args[-1], 1.0) for x in args[:-1])


def f_qnorm(ids, *args):
    return tuple(_qknorm1(x, args[-1], DSW_DH ** -0.5) for x in args[:-1])


def f_combine(ids, o0, o1, o2, l0, l1, l2):
    m = jnp.maximum(jnp.maximum(l0, l1), l2)
    e0, e1, e2 = jnp.exp(l0 - m), jnp.exp(l1 - m), jnp.exp(l2 - m)
    den = e0 + e1 + e2
    o = (e0 * o0 + e1 * o1 + e2 * o2) / den
    return o, m + jnp.log(den)


GDN_T = 512
HALO = 8


def _conv_pre(xx, w):
    acc = xx * w[3:4, :]
    for j in range(3):
        acc = acc + pltpu.roll(xx, shift=3 - j, axis=0) * w[j:j + 1, :]
    return acc


def _qkv_act(pre, cidx):
    s = _silu(pre)
    r = lax.rsqrt(jnp.sum(s * s, axis=-1, keepdims=True) + EPS)
    scale = jnp.where(cidx < GDN_H, GDN_DK ** -0.5, 1.0).astype(F32)
    return jnp.where(cidx < 2 * GDN_H, s * r * scale, s)


def gdn_pre(proj, conv_w, S):
    nt = S // GDN_T
    hb = GDN_T // HALO

    def body(prev_ref, cur_ref, w_ref, o_ref):
        p, i = pl.program_id(0), pl.program_id(1)
        for h in range(GDN_H):
            cols = slice(LANES * h, LANES * (h + 1))
            prev = jnp.where(i > 0, prev_ref[:, cols], 0.0)
            xx = jnp.concatenate([prev, cur_ref[:, cols]], axis=0)
            pre = _conv_pre(xx, w_ref[:, cols])[HALO:]
            o_ref[h] = _qkv_act(pre, p * GDN_H + h)

    hv = GDN_H * LANES
    return pl.pallas_call(
        body, name="gdn_pre", grid=(3, nt),
        in_specs=[pl.BlockSpec((HALO, hv), lambda p, i: (jnp.maximum(i * hb - 1, 0), p)),
                  pl.BlockSpec((GDN_T, hv), lambda p, i: (i, p)),
                  pl.BlockSpec((4, hv), lambda p, i: (0, p))],
        out_specs=pl.BlockSpec((None, GDN_H, GDN_T, LANES), lambda p, i: (p, 0, i, 0)),
        out_shape=SDS((3, GDN_H, S, LANES), F32),
        compiler_params=_cp(2),
    )(proj, proj, conv_w)


def gdn_pre_bwd(proj, conv_w, dqkv, S):
    nt = S // GDN_T
    hb = GDN_T // HALO
    last_h = S // HALO - 1

    def body(prev_ref, cur_ref, next_ref, w_ref, d_ref, dnext_ref, dx_ref, dw_ref):
        p, i = pl.program_id(0), pl.program_id(1)

        @pl.when(i == 0)
        def _():
            dw_ref[...] = jnp.zeros_like(dw_ref)

        for h in range(GDN_H):
            cols = slice(LANES * h, LANES * (h + 1))
            w = w_ref[:, cols]
            prev = jnp.where(i > 0, prev_ref[:, cols], 0.0)
            xx = jnp.concatenate([prev, cur_ref[:, cols], next_ref[:, cols]], axis=0)
            dnext = jnp.where(i < nt - 1, dnext_ref[h], 0.0)
            dd = jnp.concatenate([jnp.zeros((HALO, LANES), F32), d_ref[h], dnext], axis=0)
            pre = _conv_pre(xx, w)
            _, vjp = jax.vjp(lambda v, _c=p * GDN_H + h: _qkv_act(v, _c), pre)
            (dpre,) = vjp(dd)
            row = lax.broadcasted_iota(jnp.int32, dpre.shape, 0)
            dpre = jnp.where(row >= HALO, dpre, 0.0)
            dx = dpre * w[3:4, :]
            R = dpre.shape[0]
            for j in range(3):
                dx = dx + pltpu.roll(dpre, shift=R - (3 - j), axis=0) * w[j:j + 1, :]
            dx_ref[:, cols] = dx[HALO:HALO + GDN_T].astype(dx_ref.dtype)
            own = jnp.where(row < HALO + GDN_T, dpre, 0.0)
            rows_w = [jnp.sum(own * pltpu.roll(xx, shift=3 - j, axis=0), axis=0, keepdims=True) for j in range(3)]
            rows_w.append(jnp.sum(own * xx, axis=0, keepdims=True))
            r4 = lax.broadcasted_iota(jnp.int32, (4, LANES), 0)
            dw = jnp.zeros((4, LANES), F32)
            for j in range(4):
                dw = dw + jnp.where(r4 == j, rows_w[j], 0.0)
            dw_ref[:, cols] += dw

    hv = GDN_H * LANES
    return pl.pallas_call(
        body, name="gdn_pre_bwd", grid=(3, nt),
        in_specs=[pl.BlockSpec((HALO, hv), lambda p, i: (jnp.maximum(i * hb - 1, 0), p)),
                  pl.BlockSpec((GDN_T, hv), lambda p, i: (i, p)),
                  pl.BlockSpec((HALO, hv), lambda p, i: (jnp.minimum((i + 1) * hb, last_h), p)),
                  pl.BlockSpec((4, hv), lambda p, i: (0, p)),
                  pl.BlockSpec((None, GDN_H, GDN_T, LANES), lambda p, i: (p, 0, i, 0)),
                  pl.BlockSpec((None, GDN_H, HALO, LANES), lambda p, i: (p, 0, jnp.minimum((i + 1) * hb, last_h), 0))],
        out_specs=[pl.BlockSpec((GDN_T, hv), lambda p, i: (i, p)),
                   pl.BlockSpec((4, hv), lambda p, i: (0, p))],
        out_shape=[SDS((S, 3 * hv), BF16), SDS((4, 3 * hv), F32)],
        compiler_params=_cp(2),
    )(proj, proj, proj, conv_w, dqkv, dqkv)


_DIMS = {"nn": ((1,), (0,)), "nt": ((1,), (1,)), "tn": ((0,), (0,))}


def _mm_raw(a, b, mode, hi):
    if hi:
        return _dot_hi(a, b, _DIMS[mode])
    return lax.dot_general(a.astype(BF16), b.astype(BF16), (_DIMS[mode], ((), ())), preferred_element_type=F32)


@functools.partial(jax.custom_vjp, nondiff_argnums=(2, 3))
def mm(a, b, mode, hi):
    return _mm_raw(a, b, mode, hi)


def _mm_fwd(a, b, mode, hi):
    return _mm_raw(a, b, mode, hi), (a, b)


def _mm_bwd(mode, hi, res, dc):
    a, b = res
    if mode == "nn":
        da, db = mm(dc, b, "nt", hi), mm(a, dc, "tn", hi)
    elif mode == "nt":
        da, db = mm(dc, b, "nn", hi), mm(dc, a, "tn", hi)
    else:
        da, db = mm(b, dc, "nt", hi), mm(a, dc, "nn", hi)
    return da, db


mm.defvjp(_mm_fwd, _mm_bwd)


TRI_BASE = 8


def _unit_lower_inverses(Ls):
    n = Ls[0].shape[0]
    r = lax.broadcasted_iota(jnp.int32, (n, n), 0)
    c = lax.broadcasted_iota(jnp.int32, (n, n), 1)
    eye = jnp.where(r == c, 1.0, 0.0).astype(F32)
    base = r // TRI_BASE == c // TRI_BASE
    Ps = [jnp.where(base, -L, 0.0) for L in Ls]
    invs = [eye + P for P in Ps]
    k = 1
    while 2 * k < TRI_BASE:
        Ps = [_dot_x3(P, P) for P in Ps]
        invs = [inv + _dot_x3(inv, P) for inv, P in zip(invs, Ps)]
        k *= 2
    b = 2 * TRI_BASE
    while b <= n:
        off_mask = (r // b == c // b) & ((r % b) >= b // 2) & ((c % b) < b // 2)
        ts = [_dot_x3(inv, jnp.where(off_mask, L, 0.0)) for inv, L in zip(invs, Ls)]
        invs = [inv - _dot_x3(t, inv) for inv, t in zip(invs, ts)]
        b *= 2
    return invs


@jax.custom_vjp
def tri_apply(invs, Ls, r1s, r2s):
    return [_dot_x3(i, r) for i, r in zip(invs, r1s)], [_dot_x3(i, r) for i, r in zip(invs, r2s)]


def _tri_fwd(invs, Ls, r1s, r2s):
    s1s = [_dot_x3(i, r) for i, r in zip(invs, r1s)]
    s2s = [_dot_x3(i, r) for i, r in zip(invs, r2s)]
    return (s1s, s2s), (invs, s1s, s2s)


def _tri_bwd(res, ds):
    invs, s1s, s2s = res
    d1s = [_dot_x3(i, d, _DIMS["tn"]) for i, d in zip(invs, ds[0])]
    d2s = [_dot_x3(i, d, _DIMS["tn"]) for i, d in zip(invs, ds[1])]
    dLs = [-(_dot_x3(d1, s1, _DIMS["nt"]) + _dot_x3(d2, s2, _DIMS["nt"])) for d1, s1, d2, s2 in zip(d1s, s1s, d2s, s2s)]
    return [jnp.zeros_like(i) for i in invs], dLs, d1s, d2s


tri_apply.defvjp(_tri_fwd, _tri_bwd)


def _gdn_chunk(qs, ks, vs, gcbs, btbs, Ss, invs=None):
    C = qs[0].shape[0]
    r = lax.broadcasted_iota(jnp.int32, (C, C), 0)
    c = lax.broadcasted_iota(jnp.int32, (C, C), 1)
    causal, strict = c <= r, c < r
    rows = lax.broadcasted_iota(jnp.int32, gcbs[0].shape, 0)
    Gs = [g[:, :C] for g in gcbs]
    decays = [jnp.exp(jnp.where(causal, G - G.T, NEG)) for G in Gs]
    kbs = [k * b for k, b in zip(ks, btbs)]
    vbs = [v * b for v, b in zip(vs, btbs)]
    Ls = [jnp.where(strict, mm(kb, k, "nt", False) * d, 0.0) for kb, k, d in zip(kbs, ks, decays)]
    egs = [jnp.exp(g) for g in gcbs]
    if invs is None:
        invs = _unit_lower_inverses(Ls)
    us, ws = tri_apply(invs, Ls, vbs, [kb * eg for kb, eg in zip(kbs, egs)])
    qks = [jnp.where(causal, mm(q, k, "nt", False) * d, 0.0) for q, k, d in zip(qs, ks, decays)]
    g_lasts = [jnp.sum(jnp.where(rows == C - 1, g, 0.0), axis=0, keepdims=True) for g in gcbs]
    q_decs = [q * eg for q, eg in zip(qs, egs)]
    k_decs = [k * jnp.exp(gl - g) for k, gl, g in zip(ks, g_lasts, gcbs)]
    v_news = [u - mm(w, S, "nn", False) for u, w, S in zip(us, ws, Ss)]
    os_ = [mm(qd, S, "nn", False) + mm(qk, vn, "nn", False) for qd, S, qk, vn in zip(q_decs, Ss, qks, v_news)]
    S_news = [S * jnp.exp(gl) + mm(kd, vn, "tn", False) for S, gl, kd, vn in zip(Ss, g_lasts, k_decs, v_news)]
    return os_, S_news, invs


def gdn_core(qkv, gc, bt, S):
    nchunk = S // GDN_C

    def body(qkv_ref, g_ref, b_ref, o_ref, st_ref, inv_ref, s_scr):
        n = pl.program_id(0)

        @pl.when(n == 0)
        def _():
            s_scr[...] = jnp.zeros_like(s_scr)

        heads = range(GDN_H)
        S_in = [s_scr[h] for h in heads]
        os_, S_new, invs = _gdn_chunk([qkv_ref[0, h] for h in heads], [qkv_ref[1, h] for h in heads], [qkv_ref[2, h] for h in heads],
                                      [g_ref[h] for h in heads], [b_ref[h] for h in heads], S_in)
        for h in heads:
            st_ref[h] = S_in[h]
            inv_ref[h] = invs[h]
            o_ref[h] = os_[h]
            s_scr[h] = S_new[h]

    blk3 = pl.BlockSpec((3, GDN_H, GDN_C, LANES), lambda n: (0, 0, n, 0))
    hb = pl.BlockSpec((GDN_H, GDN_C, LANES), lambda n: (0, n, 0))
    return pl.pallas_call(
        body, name="gdn_core", grid=(nchunk,),
        in_specs=[blk3, hb, hb],
        out_specs=[hb, pl.BlockSpec((GDN_H, None, GDN_DK, LANES), lambda n: (0, n, 0, 0)),
                   pl.BlockSpec((GDN_H, None, GDN_C, GDN_C), lambda n: (0, n, 0, 0))],
        out_shape=[SDS((GDN_H, S, LANES), F32), SDS((GDN_H, nchunk, GDN_DK, LANES), F32), SDS((GDN_H, nchunk, GDN_C, GDN_C), F32)],
        scratch_shapes=[pltpu.VMEM((GDN_H, GDN_DK, LANES), F32)],
        compiler_params=_cp(1),
    )(qkv, gc, bt)


def gdn_core_bwd(qkv, gc, bt, states, invs, do, S):
    nchunk = S // GDN_C

    def body(qkv_ref, g_ref, b_ref, st_ref, inv_ref, do_ref, dqkv_ref, dg_ref, db_ref, ds_scr):
        n = pl.program_id(0)

        @pl.when(n == 0)
        def _():
            ds_scr[...] = jnp.zeros_like(ds_scr)

        heads = range(GDN_H)
        saved = [inv_ref[h] for h in heads]
        _, vjp = jax.vjp(lambda *a: _gdn_chunk(*a, invs=saved)[:2],
                         [qkv_ref[0, h] for h in heads], [qkv_ref[1, h] for h in heads], [qkv_ref[2, h] for h in heads],
                         [g_ref[h] for h in heads], [b_ref[h] for h in heads], [st_ref[h] for h in heads])
        dq, dk, dv, dg, db, dS = vjp(([do_ref[h] for h in heads], [ds_scr[h] for h in heads]))
        for h in heads:
            dqkv_ref[0, h] = dq[h]
            dqkv_ref[1, h] = dk[h]
            dqkv_ref[2, h] = dv[h]
            dg_ref[h] = dg[h]
            db_ref[h] = db[h]
            ds_scr[h] = dS[h]

    rev = lambda n: nchunk - 1 - n
    blk3 = pl.BlockSpec((3, GDN_H, GDN_C, LANES), lambda n: (0, 0, rev(n), 0))
    hb = pl.BlockSpec((GDN_H, GDN_C, LANES), lambda n: (0, rev(n), 0))
    return pl.pallas_call(
        body, name="gdn_core_bwd", grid=(nchunk,),
        in_specs=[blk3, hb, hb, pl.BlockSpec((GDN_H, None, GDN_DK, LANES), lambda n: (0, rev(n), 0, 0)),
                  pl.BlockSpec((GDN_H, None, GDN_C, GDN_C), lambda n: (0, rev(n), 0, 0)), hb],
        out_specs=[blk3, hb, hb],
        out_shape=[SDS((3, GDN_H, S, LANES), F32), SDS((GDN_H, S, LANES), F32), SDS((GDN_H, S, LANES), F32)],
        scratch_shapes=[pltpu.VMEM((GDN_H, GDN_DK, LANES), F32)],
        compiler_params=_cp(1),
    )(qkv, gc, bt, states, invs, do)


GDN_MAIN = 4 * GDN_H * LANES
GDN_PROJ = GDN_MAIN + LANES
RT = 256


def gdn_forward(h, w_in, conv_w, alog, dtb, out_gain, w_out):
    S = h.shape[0]
    nt = S // RT
    proj = matmul(h, w_in, "nn", F32, "gdn_in")
    qkv = gdn_pre(proj, conv_w, S)
    ab_row = Row(proj, (RT, LANES), lambda i: (i, GDN_MAIN // LANES), gdtype=BF16, gshape=(S, LANES), gimap=lambda i: (i, 0))
    hm = lambda i: (0, i, 0)
    hv = GDN_H * LANES
    gc, bt = rowwise(f_gdn_gates, [ab_row], [alog, dtb],
                     [Out((GDN_H, S, LANES), F32, (GDN_H, RT, LANES), hm, lead=GDN_H)] * 2, (nt,), "gdn_gates")
    o, states, invs = gdn_core(qkv, gc, bt, S)
    o_row = Row(o, (GDN_H, RT, LANES), hm, lead=GDN_H)
    z_row = Row(proj, (RT, hv), lambda i: (i, 3), splits=[LANES] * GDN_H, gdtype=BF16, gshape=(S, hv), gimap=lambda i: (i, 0))
    (on,) = rowwise(f_gdn_post, [o_row, z_row], [out_gain],
                    [Out((S, hv), BF16, (RT, hv), lambda i: (i, 0), splits=[LANES] * GDN_H)], (nt,), "gdn_post")
    y = matmul(on, w_out, "nn", F32, "gdn_out")
    saved = dict(h=h, proj=proj, qkv=qkv, gc=gc, bt=bt, states=states, invs=invs, o=o, on=on, ab_row=ab_row, o_row=o_row, z_row=z_row)
    return y, saved


def gdn_backward(dy, sv, w_in, conv_w, alog, dtb, out_gain, w_out):
    S = dy.shape[0]
    nt = S // RT
    hm = lambda i: (0, i, 0)
    hv = GDN_H * LANES
    don = matmul(dy, w_out, "nt", F32, "gdn_out_dx")
    d_w_out = matmul(sv["on"], dy, "tn", F32, "gdn_out_dw")
    (do, dz), (d_gain,) = rowwise_bwd(f_gdn_post, [sv["o_row"], sv["z_row"]], [out_gain],
                                      [Row(don, (RT, hv), lambda i: (i, 0), splits=[LANES] * GDN_H)], (nt,), "gdn_post_bwd")
    dqkv, dgc, dbt = gdn_core_bwd(sv["qkv"], sv["gc"], sv["bt"], sv["states"], sv["invs"], do, S)
    head_blk = lambda a: Row(a, (GDN_H, RT, LANES), hm, lead=GDN_H)
    (dab,), (d_alog, d_dtb) = rowwise_bwd(f_gdn_gates, [sv["ab_row"]], [alog, dtb], [head_blk(dgc), head_blk(dbt)],
                                          (nt,), "gdn_gates_bwd")
    dqkv_proj, d_conv = gdn_pre_bwd(sv["proj"], conv_w, dqkv, S)
    dproj = jnp.concatenate([dqkv_proj, dz, dab], axis=1)
    d_w_in = matmul(sv["h"], dproj, "tn", F32, "gdn_in_dw")
    dh = matmul(dproj, w_in, "nt", F32, "gdn_in_dx")
    return dh, dict(w_in=d_w_in, conv=d_conv, alog=d_alog, dtb=d_dtb, gain=d_gain, w_out=d_w_out)


QB = DSW_SPAN
N_HP = DSW_HG // LANES
PROJ_BLKS = 3 * 3 * N_HP


def _bucket_maps():
    a = np.arange(QB)[:, None]
    j = np.arange(2 * QB)[None, :]
    dist = QB + a - j
    band = (dist >= 0) & (dist <= DSW_SPAN)
    maps = []
    for _, dil in DSW_GROUPS:
        dd = np.maximum(dist, 0) * dil
        max_exact = REL_BUCKETS // 2
        scaled = np.log(np.maximum(dd, 1).astype(np.float32) / np.float32(max_exact)) / np.float32(math.log(REL_MAX_DIST / max_exact))
        large = max_exact + (scaled * np.float32(REL_BUCKETS - max_exact)).astype(np.int32)
        large = np.minimum(large, REL_BUCKETS - 1)
        maps.append(np.where(dd < max_exact, dd, large).astype(np.int32))
    return np.stack(maps), band


def dsw_bias(rel_bias):
    maps, band = _bucket_maps()
    maps = np.where(band[None], maps, -1).astype(np.int32)

    def body(tab_ref, bk_ref, o_ref):
        gh = pl.program_id(0)
        bk = bk_ref[...]
        acc = jnp.full(bk.shape, NEG, F32)
        for b in range(REL_BUCKETS):
            acc = jnp.where(bk == b, tab_ref[b, gh], acc)
        o_ref[...] = acc

    return pl.pallas_call(
        body, name="dsw_bias", grid=(3 * GDN_H,),
        in_specs=[pl.BlockSpec(memory_space=pltpu.SMEM),
                  pl.BlockSpec((None, QB, 2 * QB), lambda gh: (gh // GDN_H, 0, 0))],
        out_specs=pl.BlockSpec((None, QB, 2 * QB), lambda gh: (gh, 0, 0)),
        out_shape=SDS((3 * GDN_H, QB, 2 * QB), F32),
        compiler_params=_cp(1),
    )(rel_bias, jnp.asarray(maps))


def dsw_bias_grad(dbias):
    maps, band = _bucket_maps()
    maps = np.where(band[None], maps, -1).astype(np.int32)

    def body(d_ref, bk_ref, o_ref):
        bk = bk_ref[...]
        d = d_ref[...]
        rows = lax.broadcasted_iota(jnp.int32, (REL_BUCKETS, LANES), 0)
        acc = jnp.zeros((REL_BUCKETS, LANES), F32)
        for b in range(REL_BUCKETS):
            part = jnp.sum(jnp.where(bk == b, d, 0.0), axis=0, keepdims=True)
            val = jnp.sum(part, axis=1, keepdims=True)
            acc = jnp.where(rows == b, val, acc)
        o_ref[...] = acc

    return pl.pallas_call(
        body, name="dsw_bias_grad", grid=(3 * GDN_H,),
        in_specs=[pl.BlockSpec((None, QB, 2 * QB), lambda gh: (gh, 0, 0)),
                  pl.BlockSpec((None, QB, 2 * QB), lambda gh: (gh // GDN_H, 0, 0))],
        out_specs=pl.BlockSpec((None, REL_BUCKETS, LANES), lambda gh: (gh, 0, 0)),
        out_shape=SDS((3 * GDN_H, REL_BUCKETS, LANES), F32),
        compiler_params=_cp(1),
    )(dbias, jnp.asarray(maps))


def _nt(a, b):
    return lax.dot_general(a, b, (((1,), (1,)), ((), ())), preferred_element_type=F32)


def _tn(a, b):
    return lax.dot_general(a, b, (((0,), (0,)), ((), ())), preferred_element_type=F32)


def dsw_group_fwd(qn, kn, proj, bias, gi, S):
    dil = DSW_GROUPS[gi][1]
    sd = S // dil
    nq = sd // QB
    qv = qn.reshape(sd, dil * 3 * DSW_HG)
    kv = kn.reshape(sd, dil * 3 * DSW_HG)
    pv = proj.reshape(sd, dil * 9 * DSW_HG)
    qk_col = lambda hp, r: r * (3 * N_HP) + gi * N_HP + hp
    v_col = lambda hp, r: r * PROJ_BLKS + 2 * 3 * N_HP + gi * N_HP + hp

    def body(q_ref, kp_ref, kc_ref, vp_ref, vc_ref, b_ref, o_ref, l_ref):
        i = pl.program_id(2)
        q = q_ref[...]
        k2 = jnp.concatenate([kp_ref[...], kc_ref[...]], axis=0)
        v2 = jnp.concatenate([vp_ref[...], vc_ref[...]], axis=0).astype(BF16)
        lane_q = lax.broadcasted_iota(jnp.int32, (QB, LANES), 1) < DSW_DH
        lane_k = lax.broadcasted_iota(jnp.int32, (2 * QB, LANES), 1) < DSW_DH
        col = lax.broadcasted_iota(jnp.int32, (QB, 2 * QB), 1)
        first = jnp.logical_and(i == 0, col < QB)
        o_acc = jnp.zeros((QB, LANES), F32)
        lse_b = jnp.zeros((QB, LANES), F32)
        for hh in range(2):
            mq = lane_q if hh == 0 else jnp.logical_not(lane_q)
            mk = lane_k if hh == 0 else jnp.logical_not(lane_k)
            s = _nt(jnp.where(mq, q, 0).astype(BF16), k2) + b_ref[hh]
            s = jnp.where(first, NEG, s)
            mx = jnp.max(s, axis=1, keepdims=True)
            p = jnp.exp(s - mx)
            l = jnp.sum(p, axis=1, keepdims=True)
            oh = jnp.dot(p.astype(BF16), jnp.where(mk, v2, 0).astype(BF16), preferred_element_type=F32) / l
            o_acc = o_acc + oh
            lse_b = jnp.where(mq, mx + jnp.log(l), lse_b)
        o_ref[...] = o_acc
        l_ref[...] = lse_b

    blk = (QB, LANES)
    out_spec = pl.BlockSpec(blk, lambda hp, r, i: (i, r * N_HP + hp))
    o, lse = pl.pallas_call(
        body, name=f"dsw_fwd_g{gi}", grid=(N_HP, dil, nq),
        in_specs=[pl.BlockSpec(blk, lambda hp, r, i: (i, qk_col(hp, r))),
                  pl.BlockSpec(blk, lambda hp, r, i: (jnp.maximum(i - 1, 0), qk_col(hp, r))),
                  pl.BlockSpec(blk, lambda hp, r, i: (i, qk_col(hp, r))),
                  pl.BlockSpec(blk, lambda hp, r, i: (jnp.maximum(i - 1, 0), v_col(hp, r))),
                  pl.BlockSpec(blk, lambda hp, r, i: (i, v_col(hp, r))),
                  pl.BlockSpec((2, QB, 2 * QB), lambda hp, r, i: (gi * N_HP + hp, 0, 0))],
        out_specs=[out_spec, out_spec],
        out_shape=[SDS((sd, dil * DSW_HG), F32)] * 2,
        compiler_params=_cp(3),
    )(qv, kv, kv, pv, pv, bias)
    return o.reshape(S, DSW_HG), lse.reshape(S, DSW_HG)


def dsw_group_bwd(qn, kn, proj, bias, do, o, lse, gi, S):
    dil = DSW_GROUPS[gi][1]
    sd = S // dil
    nq = sd // QB
    qv = qn.reshape(sd, dil * 3 * DSW_HG)
    kv = kn.reshape(sd, dil * 3 * DSW_HG)
    pv = proj.reshape(sd, dil * 9 * DSW_HG)
    dov = do.reshape(sd, dil * DSW_HG)
    ov = o.reshape(sd, dil * DSW_HG)
    lv = lse.reshape(sd, dil * DSW_HG)
    qk_col = lambda hp, r: r * (3 * N_HP) + gi * N_HP + hp
    v_col = lambda hp, r: r * PROJ_BLKS + 2 * 3 * N_HP + gi * N_HP + hp
    o_col = lambda hp, r: r * N_HP + hp
    cur = lambda i: jnp.minimum(i, nq - 1)
    prev = lambda i: jnp.maximum(jnp.minimum(i, nq - 1) - 1, 0)
    done = lambda i: jnp.maximum(i - 1, 0)

    def body(q_ref, kp_ref, kc_ref, vp_ref, vc_ref, b_ref, do_ref, o_ref, l_ref,
             dq_ref, dk_ref, dv_ref, db_ref, dk_scr, dv_scr):
        r, i = pl.program_id(1), pl.program_id(2)

        @pl.when(jnp.logical_and(r == 0, i == 0))
        def _():
            db_ref[...] = jnp.zeros_like(db_ref)

        @pl.when(i == 0)
        def _():
            dk_scr[...] = jnp.zeros_like(dk_scr)
            dv_scr[...] = jnp.zeros_like(dv_scr)

        @pl.when(i < nq)
        def _():
            q = q_ref[...]
            k2 = jnp.concatenate([kp_ref[...], kc_ref[...]], axis=0)
            v2 = jnp.concatenate([vp_ref[...], vc_ref[...]], axis=0).astype(BF16)
            dout = do_ref[...].astype(F32)
            prod = dout * o_ref[...].astype(F32)
            lse_b = l_ref[...]
            lane_q = lax.broadcasted_iota(jnp.int32, (QB, LANES), 1) < DSW_DH
            col = lax.broadcasted_iota(jnp.int32, (QB, 2 * QB), 1)
            first = jnp.logical_and(i == 0, col < QB)
            dq = jnp.zeros((QB, LANES), F32)
            dk2 = jnp.zeros((2 * QB, LANES), F32)
            dv2 = jnp.zeros((2 * QB, LANES), F32)
            for hh in range(2):
                mq = lane_q if hh == 0 else jnp.logical_not(lane_q)
                qm = jnp.where(mq, q, 0).astype(BF16)
                dom = jnp.where(mq, dout, 0.0).astype(BF16)
                s = _nt(qm, k2) + b_ref[hh]
                s = jnp.where(first, NEG, s)
                lse_h = jnp.max(jnp.where(mq, lse_b, NEG), axis=1, keepdims=True)
                p = jnp.exp(s - lse_h)
                delta = jnp.sum(jnp.where(mq, prod, 0.0), axis=1, keepdims=True)
                dp = _nt(dom, v2)
                ds = p * (dp - delta)
                dsb = ds.astype(BF16)
                dq = dq + jnp.where(mq, jnp.dot(dsb, k2, preferred_element_type=F32), 0.0)
                dk2 = dk2 + _tn(dsb, qm)
                dv2 = dv2 + _tn(p.astype(BF16), dom)
                db_ref[hh] += ds
            dq_ref[...] = dq
            dk_ref[...] = dk_scr[...] + dk2[:QB]
            dv_ref[...] = (dv_scr[...] + dv2[:QB]).astype(dv_ref.dtype)
            dk_scr[...] = dk2[QB:]
            dv_scr[...] = dv2[QB:]

        @pl.when(i == nq)
        def _():
            dk_ref[...] = dk_scr[...]
            dv_ref[...] = dv_scr[...].astype(dv_ref.dtype)

    blk = (QB, LANES)
    dq, dk, dv, dbias = pl.pallas_call(
        body, name=f"dsw_bwd_g{gi}", grid=(N_HP, dil, nq + 1),
        in_specs=[pl.BlockSpec(blk, lambda hp, r, i: (cur(i), qk_col(hp, r))),
                  pl.BlockSpec(blk, lambda hp, r, i: (prev(i), qk_col(hp, r))),
                  pl.BlockSpec(blk, lambda hp, r, i: (cur(i), qk_col(hp, r))),
                  pl.BlockSpec(blk, lambda hp, r, i: (prev(i), v_col(hp, r))),
                  pl.BlockSpec(blk, lambda hp, r, i: (cur(i), v_col(hp, r))),
                  pl.BlockSpec((2, QB, 2 * QB), lambda hp, r, i: (gi * N_HP + hp, 0, 0)),
                  pl.BlockSpec(blk, lambda hp, r, i: (cur(i), o_col(hp, r))),
                  pl.BlockSpec(blk, lambda hp, r, i: (cur(i), o_col(hp, r))),
                  pl.BlockSpec(blk, lambda hp, r, i: (cur(i), o_col(hp, r)))],
        out_specs=[pl.BlockSpec(blk, lambda hp, r, i: (cur(i), o_col(hp, r))),
                   pl.BlockSpec(blk, lambda hp, r, i: (done(i), o_col(hp, r))),
                   pl.BlockSpec(blk, lambda hp, r, i: (done(i), o_col(hp, r))),
                   pl.BlockSpec((2, QB, 2 * QB), lambda hp, r, i: (hp, 0, 0))],
        out_shape=[SDS((sd, dil * DSW_HG), F32), SDS((sd, dil * DSW_HG), F32), SDS((sd, dil * DSW_HG), BF16),
                   SDS((GDN_H, QB, 2 * QB), F32)],
        scratch_shapes=[pltpu.VMEM(blk, F32), pltpu.VMEM(blk, F32)],
        compiler_params=_cp(3),
    )(qv, kv, kv, pv, pv, bias, dov, ov, lv)
    return dq.reshape(S, DSW_HG), dk.reshape(S, DSW_HG), dv.reshape(S, DSW_HG), dbias


def dsw_forward(h, w_in, q_gain2, k_gain2, rel_bias, w_out):
    S = h.shape[0]
    nt = S // RT
    nb = 3 * N_HP
    proj = matmul(h, w_in, "nn", F32, "dsw_in")
    width = nb * LANES
    lanes12 = [LANES] * nb
    (qn,) = rowwise(f_qnorm, [Row(proj, (RT, width), lambda i: (i, 0), splits=lanes12)], [q_gain2],
                    [Out((S, width), BF16, (RT, width), lambda i: (i, 0), splits=lanes12)], (nt,), "dsw_qnorm")
    (kn,) = rowwise(f_qknorm, [Row(proj, (RT, width), lambda i: (i, 1), splits=lanes12)], [k_gain2],
                    [Out((S, width), BF16, (RT, width), lambda i: (i, 0), splits=lanes12)], (nt,), "dsw_knorm")
    bias = dsw_bias(rel_bias)
    os_, ls_ = [], []
    for gi in range(3):
        o, l = dsw_group_fwd(qn, kn, proj, bias, gi, S)
        os_.append(o)
        ls_.append(l)
    full = lambda a: Row(a, (RT, DSW_HG), lambda i: (i, 0))
    o, lse = rowwise(f_combine, [full(a) for a in os_ + ls_], [],
                     [Out((S, DSW_HG), BF16, (RT, DSW_HG), lambda i: (i, 0)), Out((S, DSW_HG), F32, (RT, DSW_HG), lambda i: (i, 0))],
                     (nt,), "dsw_combine")
    y = matmul(o, w_out, "nn", F32, "dsw_out")
    return y, dict(h=h, proj=proj, qn=qn, kn=kn, bias=bias, o=o, lse=lse)


def dsw_backward(dy, sv, w_in, q_gain2, k_gain2, w_out):
    S = dy.shape[0]
    nt = S // RT
    nb = 3 * N_HP
    do = matmul(dy, w_out, "nt", BF16, "dsw_out_dx")
    d_w_out = matmul(sv["o"], dy, "tn", F32, "dsw_out_dw")
    pieces_q, pieces_k, pieces_v, dbs = [], [], [], []
    d_qg = jnp.zeros((1, LANES), F32)
    d_kg = jnp.zeros((1, LANES), F32)
    for gi in range(3):
        dq, dk, dv, db = dsw_group_bwd(sv["qn"], sv["kn"], sv["proj"], sv["bias"], do, sv["o"], sv["lse"], gi, S)
        dbs.append(db)
        pieces_v.append(dv)
        for which, dd in ((0, dq), (1, dk)):
            lanes4 = [LANES] * N_HP
            row = Row(sv["proj"], (RT, DSW_HG), lambda i, _o=which * 3 + gi: (i, _o), splits=lanes4,
                      gdtype=BF16, gshape=(S, DSW_HG), gimap=lambda i: (i, 0))
            fn, gain = (f_qnorm, q_gain2) if which == 0 else (f_qknorm, k_gain2)
            (dx,), (dg,) = rowwise_bwd(fn, [row], [gain], [Row(dd, (RT, DSW_HG), lambda i: (i, 0), splits=lanes4)],
                                       (nt,), f"dsw_norm_bwd_{which}{gi}")
            if which == 0:
                pieces_q.append(dx)
                d_qg = d_qg + dg
            else:
                pieces_k.append(dx)
                d_kg = d_kg + dg
    dproj = jnp.concatenate(pieces_q + pieces_k + pieces_v, axis=1)
    d_w_in = matmul(sv["h"], dproj, "tn", F32, "dsw_in_dw")
    dh = matmul(dproj, w_in, "nt", F32, "dsw_in_dx")
    d_rel = dsw_bias_grad(jnp.concatenate(dbs, axis=0))
    return dh, dict(w_in=d_w_in, q_gain2=d_qg, k_gain2=d_kg, rel=d_rel, w_out=d_w_out)


N_LB = DSW_HG // LANES
HALF = DSW_DH // 2


def _lanes(j):
    return slice(LANES * j, LANES * (j + 1))


def _deinterleave(stage, out_ref, dil, rows, dtype):
    for r in range(dil):
        for j in range(N_LB):
            out_ref[r, :, _lanes(j)] = stage[j, pl.ds(r, rows, stride=dil), :].astype(dtype)


def _interleave(in_ref, stage, dil, rows):
    for r in range(dil):
        for j in range(N_LB):
            stage[j, pl.ds(r, rows, stride=dil), :] = in_ref[r, :, _lanes(j)].astype(F32)


def dsw_prep(proj, q_gain2, k_gain2, gi, S):
    dil = DSW_GROUPS[gi][1]
    nt, rows = S // RT, RT // dil

    def body(q_ref, k_ref, v_ref, qg_ref, kg_ref, qo_ref, ko_ref, vo_ref, stage):
        for src, gain_ref, scale, dst in ((q_ref, qg_ref, DSW_DH ** -0.5, qo_ref), (k_ref, kg_ref, 1.0, ko_ref), (v_ref, None, None, vo_ref)):
            for j in range(N_LB):
                val = src[:, _lanes(j)]
                stage[j] = val if gain_ref is None else _qknorm1(val, gain_ref[...], scale)
            _deinterleave(stage, dst, dil, rows, BF16)

    col = lambda which: pl.BlockSpec((RT, DSW_HG), lambda i, _c=which * 3 + gi: (i, _c))
    gspec = pl.BlockSpec((1, LANES), lambda i: (0, 0))
    ospec = pl.BlockSpec((dil, rows, DSW_HG), lambda i: (0, i, 0))
    return pl.pallas_call(
        body, name=f"dsw_prep_g{gi}", grid=(nt,),
        in_specs=[col(0), col(1), col(2), gspec, gspec], out_specs=[ospec] * 3,
        out_shape=[SDS((dil, S // dil, DSW_HG), BF16)] * 3,
        scratch_shapes=[pltpu.VMEM((N_LB, RT, LANES), F32)], compiler_params=_cp(1),
    )(proj, proj, proj, q_gain2, k_gain2)


def dsw_prep_bwd(proj, q_gain2, k_gain2, dqd, dkd, dvd, gi, S):
    dil = DSW_GROUPS[gi][1]
    nt, rows = S // RT, RT // dil

    def body(q_ref, k_ref, qg_ref, kg_ref, dq_ref, dk_ref, dv_ref, oq_ref, ok_ref, ov_ref, dqg_ref, dkg_ref, stage):
        i = pl.program_id(0)

        @pl.when(i == 0)
        def _():
            dqg_ref[...] = jnp.zeros_like(dqg_ref)
            dkg_ref[...] = jnp.zeros_like(dkg_ref)

        for src, gain_ref, scale, cot_ref, dst, dg_ref in ((q_ref, qg_ref, DSW_DH ** -0.5, dq_ref, oq_ref, dqg_ref),
                                                          (k_ref, kg_ref, 1.0, dk_ref, ok_ref, dkg_ref)):
            _interleave(cot_ref, stage, dil, rows)
            for j in range(N_LB):
                _, vjp = jax.vjp(lambda x, g, _s=scale: _qknorm1(x, g, _s), src[:, _lanes(j)], gain_ref[...])
                dx, dg = vjp(stage[j])
                dst[:, _lanes(j)] = dx.astype(dst.dtype)
                dg_ref[...] += dg
        _interleave(dv_ref, stage, dil, rows)
        for j in range(N_LB):
            ov_ref[:, _lanes(j)] = stage[j].astype(ov_ref.dtype)

    col = lambda which: pl.BlockSpec((RT, DSW_HG), lambda i, _c=which * 3 + gi: (i, _c))
    gspec = pl.BlockSpec((1, LANES), lambda i: (0, 0))
    dspec = pl.BlockSpec((dil, rows, DSW_HG), lambda i: (0, i, 0))
    nspec = pl.BlockSpec((RT, DSW_HG), lambda i: (i, 0))
    return pl.pallas_call(
        body, name=f"dsw_prep_bwd_g{gi}", grid=(nt,),
        in_specs=[col(0), col(1), gspec, gspec, dspec, dspec, dspec], out_specs=[nspec] * 3 + [gspec] * 2,
        out_shape=[SDS((S, DSW_HG), BF16)] * 3 + [SDS((1, LANES), F32)] * 2,
        scratch_shapes=[pltpu.VMEM((N_LB, RT, LANES), F32)], compiler_params=_cp(1),
    )(proj, proj, q_gain2, k_gain2, dqd, dkd, dvd)


def _head_masks(rows):
    lane = lax.broadcasted_iota(jnp.int32, (rows, LANES), 1)
    return lane < DSW_DH, (lane % DSW_DH) < HALF


def dsw_attn_fwd(qd, kd, vd, bias, gi, S):
    dil = DSW_GROUPS[gi][1]
    sd = S // dil
    nq = sd // QB

    def body(q_ref, k_ref, v_ref, b_ref, o_ref, l_ref, kp_scr, vp_scr):
        i = pl.program_id(1)

        @pl.when(i == 0)
        def _():
            kp_scr[...] = jnp.zeros_like(kp_scr)
            vp_scr[...] = jnp.zeros_like(vp_scr)

        lo_q, _ = _head_masks(QB)
        lo_k, _ = _head_masks(2 * QB)
        col = lax.broadcasted_iota(jnp.int32, (QB, 2 * QB), 1)
        first = jnp.logical_and(i == 0, col < QB)
        for hp in range(N_HP):
            q = q_ref[:, _lanes(hp)]
            k2 = jnp.concatenate([kp_scr[:, _lanes(hp)], k_ref[:, _lanes(hp)]], axis=0)
            v2 = jnp.concatenate([vp_scr[:, _lanes(hp)], v_ref[:, _lanes(hp)]], axis=0)
            o_acc = jnp.zeros((QB, LANES), F32)
            lse_b = jnp.zeros((QB, LANES), F32)
            for hh in range(2):
                mq = lo_q if hh == 0 else jnp.logical_not(lo_q)
                mk = lo_k if hh == 0 else jnp.logical_not(lo_k)
                s = _nt(jnp.where(mq, q, 0).astype(BF16), k2) + b_ref[2 * hp + hh]
                s = jnp.where(first, NEG, s)
                mx = jnp.max(s, axis=1, keepdims=True)
                p = jnp.exp(s - mx)
                l = jnp.sum(p, axis=1, keepdims=True)
                oh = jnp.dot(p.astype(BF16), jnp.where(mk, v2, 0).astype(BF16), preferred_element_type=F32) / l
                o_acc = o_acc + oh
                lse_b = jnp.where(mq, mx + jnp.log(l), lse_b)
            o_ref[:, _lanes(hp)] = o_acc
            l_ref[:, _lanes(hp)] = lse_b
        kp_scr[...] = k_ref[...]
        vp_scr[...] = v_ref[...]

    blk = pl.BlockSpec((None, QB, DSW_HG), lambda r, i: (r, i, 0))
    return pl.pallas_call(
        body, name=f"dsw_attn_g{gi}", grid=(dil, nq),
        in_specs=[blk, blk, blk, pl.BlockSpec((GDN_H, QB, 2 * QB), lambda r, i: (gi, 0, 0))],
        out_specs=[blk, blk], out_shape=[SDS((dil, sd, DSW_HG), F32)] * 2,
        scratch_shapes=[pltpu.VMEM((QB, DSW_HG), BF16)] * 2, compiler_params=_cp(2),
    )(qd, kd, vd, bias)


def dsw_attn_bwd(qd, kd, vd, bias, dod, statd, gi, S):
    dil = DSW_GROUPS[gi][1]
    sd = S // dil
    nq = sd // QB
    cur = lambda i: jnp.minimum(i, nq - 1)
    done = lambda i: jnp.maximum(i - 1, 0)

    def body(q_ref, k_ref, v_ref, b_ref, do_ref, st_ref, dq_ref, dk_ref, dv_ref, db_ref, kp_scr, vp_scr, dk_scr, dv_scr):
        r, i = pl.program_id(0), pl.program_id(1)

        @pl.when(jnp.logical_and(r == 0, i == 0))
        def _():
            db_ref[...] = jnp.zeros_like(db_ref)

        @pl.when(i == 0)
        def _():
            for scr in (kp_scr, vp_scr, dk_scr, dv_scr):
                scr[...] = jnp.zeros_like(scr)

        @pl.when(i < nq)
        def _():
            lo_q, first_half = _head_masks(QB)
            col = lax.broadcasted_iota(jnp.int32, (QB, 2 * QB), 1)
            first = jnp.logical_and(i == 0, col < QB)
            for hp in range(N_HP):
                q = q_ref[:, _lanes(hp)]
                k2 = jnp.concatenate([kp_scr[:, _lanes(hp)], k_ref[:, _lanes(hp)]], axis=0)
                v2 = jnp.concatenate([vp_scr[:, _lanes(hp)], v_ref[:, _lanes(hp)]], axis=0)
                dout = do_ref[:, _lanes(hp)]
                stat = st_ref[:, _lanes(hp)]
                dq = jnp.zeros((QB, LANES), F32)
                dk2 = jnp.zeros((2 * QB, LANES), F32)
                dv2 = jnp.zeros((2 * QB, LANES), F32)
                for hh in range(2):
                    mq = lo_q if hh == 0 else jnp.logical_not(lo_q)
                    qm = jnp.where(mq, q, 0).astype(BF16)
                    dom = jnp.where(mq, dout, 0).astype(BF16)
                    s = _nt(qm, k2) + b_ref[2 * hp + hh]
                    s = jnp.where(first, NEG, s)
                    lse_h = jnp.max(jnp.where(jnp.logical_and(mq, first_half), stat, NEG), axis=1, keepdims=True)
                    delta = jnp.max(jnp.where(jnp.logical_and(mq, jnp.logical_not(first_half)), stat, NEG), axis=1, keepdims=True)
                    p = jnp.exp(s - lse_h)
                    ds = p * (_nt(dom, v2) - delta)
                    dsb = ds.astype(BF16)
                    dq = dq + jnp.where(mq, jnp.dot(dsb, k2, preferred_element_type=F32), 0.0)
                    dk2 = dk2 + _tn(dsb, qm)
                    dv2 = dv2 + _tn(p.astype(BF16), dom)
                    db_ref[2 * hp + hh] += ds
                dq_ref[:, _lanes(hp)] = dq
                dk_ref[:, _lanes(hp)] = dk_scr[:, _lanes(hp)] + dk2[:QB]
                dv_ref[:, _lanes(hp)] = (dv_scr[:, _lanes(hp)] + dv2[:QB]).astype(dv_ref.dtype)
                dk_scr[:, _lanes(hp)] = dk2[QB:]
                dv_scr[:, _lanes(hp)] = dv2[QB:]
            kp_scr[...] = k_ref[...]
            vp_scr[...] = v_ref[...]

        @pl.when(i == nq)
        def _():
            dk_ref[...] = dk_scr[...]
            dv_ref[...] = dv_scr[...].astype(dv_ref.dtype)

    blk = pl.BlockSpec((None, QB, DSW_HG), lambda r, i: (r, cur(i), 0))
    oblk = pl.BlockSpec((None, QB, DSW_HG), lambda r, i: (r, done(i), 0))
    return pl.pallas_call(
        body, name=f"dsw_attn_bwd_g{gi}", grid=(dil, nq + 1),
        in_specs=[blk, blk, blk, pl.BlockSpec((GDN_H, QB, 2 * QB), lambda r, i: (gi, 0, 0)), blk, blk],
        out_specs=[blk, oblk, oblk, pl.BlockSpec((GDN_H, QB, 2 * QB), lambda r, i: (0, 0, 0))],
        out_shape=[SDS((dil, sd, DSW_HG), F32), SDS((dil, sd, DSW_HG), F32), SDS((dil, sd, DSW_HG), BF16),
                   SDS((GDN_H, QB, 2 * QB), F32)],
        scratch_shapes=[pltpu.VMEM((QB, DSW_HG), BF16)] * 2 + [pltpu.VMEM((QB, DSW_HG), F32)] * 2,
        compiler_params=_cp(2),
    )(qd, kd, vd, bias, dod, statd)


def dsw_combine(ods, lseds, S):
    nt = S // RT
    dils = [d for _, d in DSW_GROUPS]

    def body(*refs):
        ins, (o_ref, l_ref), stages = refs[:6], refs[6:8], refs[8:]
        for g in range(3):
            _interleave(ins[g], stages[g], dils[g], RT // dils[g])
            _interleave(ins[3 + g], stages[3 + g], dils[g], RT // dils[g])
        for j in range(N_LB):
            o, lse = f_combine(None, *[st[j] for st in stages])
            o_ref[:, _lanes(j)] = o.astype(o_ref.dtype)
            l_ref[:, _lanes(j)] = lse

    dspec = lambda d: pl.BlockSpec((d, RT // d, DSW_HG), lambda i: (0, i, 0))
    nspec = pl.BlockSpec((RT, DSW_HG), lambda i: (i, 0))
    return pl.pallas_call(
        body, name="dsw_combine", grid=(nt,),
        in_specs=[dspec(d) for d in dils] * 2, out_specs=[nspec, nspec],
        out_shape=[SDS((S, DSW_HG), BF16), SDS((S, DSW_HG), F32)],
        scratch_shapes=[pltpu.VMEM((N_LB, RT, LANES), F32)] * 6, compiler_params=_cp(1),
    )(*ods, *lseds)


def dsw_bwd_prep(do, o, lse, S):
    nt = S // RT
    dils = [d for _, d in DSW_GROUPS]

    def body(do_ref, o_ref, l_ref, *rest):
        outs, (st_do, st_stat) = rest[:6], rest[6:]
        lo, first_half = _head_masks(RT)
        for j in range(N_LB):
            dout = do_ref[:, _lanes(j)]
            prod = dout * o_ref[:, _lanes(j)].astype(F32)
            s_all = jnp.sum(prod, axis=1, keepdims=True)
            s_lo = jnp.sum(jnp.where(lo, prod, 0.0), axis=1, keepdims=True)
            delta = jnp.where(lo, s_lo, s_all - s_lo)
            st_do[j] = dout
            st_stat[j] = jnp.where(first_half, l_ref[:, _lanes(j)], delta)
        for g in range(3):
            _deinterleave(st_do, outs[g], dils[g], RT // dils[g], BF16)
            _deinterleave(st_stat, outs[3 + g], dils[g], RT // dils[g], F32)

    nspec = pl.BlockSpec((RT, DSW_HG), lambda i: (i, 0))
    dspec = lambda d: pl.BlockSpec((d, RT // d, DSW_HG), lambda i: (0, i, 0))
    res = pl.pallas_call(
        body, name="dsw_bwd_prep", grid=(nt,),
        in_specs=[nspec] * 3, out_specs=[dspec(d) for d in dils] * 2,
        out_shape=[SDS((d, S // d, DSW_HG), BF16) for d in dils] + [SDS((d, S // d, DSW_HG), F32) for d in dils],
        scratch_shapes=[pltpu.VMEM((N_LB, RT, LANES), F32)] * 2, compiler_params=_cp(1),
    )(do, o, lse)
    return res[:3], res[3:]


def dsw_forward(h, w_in, q_gain2, k_gain2, rel_bias, w_out):
    S = h.shape[0]
    proj = matmul(h, w_in, "nn", F32, "dsw_in", col_shards=N_SHARD)
    bias = dsw_bias(rel_bias)
    qkv, ods, lseds = [], [], []
    for gi in range(3):
        qd, kd, vd = dsw_prep(proj, q_gain2, k_gain2, gi, S)
        od, ld = dsw_attn_fwd(qd, kd, vd, bias, gi, S)
        qkv.append((qd, kd, vd))
        ods.append(od)
        lseds.append(ld)
    o, lse = dsw_combine(ods, lseds, S)
    y = matmul(o, w_out, "nn", F32, "dsw_out", col_shards=N_SHARD)
    return y, dict(h=h, proj=proj, qkv=qkv, bias=bias, o=o, lse=lse)


def dsw_backward(dy, sv, w_in, q_gain2, k_gain2, w_out):
    S = dy.shape[0]
    do = matmul(dy, w_out, "nt", F32, "dsw_out_dx", col_shards=N_SHARD)
    d_w_out = matmul(sv["o"], dy, "tn", F32, "dsw_out_dw", col_shards=N_SHARD)
    dods, statds = dsw_bwd_prep(do, sv["o"], sv["lse"], S)
    pieces_q, pieces_k, pieces_v, dbs = [], [], [], []
    d_qg = jnp.zeros((1, LANES), F32)
    d_kg = jnp.zeros((1, LANES), F32)
    for gi in range(3):
        qd, kd, vd = sv["qkv"][gi]
        dqd, dkd, dvd, db = dsw_attn_bwd(qd, kd, vd, sv["bias"], dods[gi], statds[gi], gi, S)
        dq, dk, dv, dqg, dkg = dsw_prep_bwd(sv["proj"], q_gain2, k_gain2, dqd, dkd, dvd, gi, S)
        dbs.append(db)
        pieces_q.append(dq)
        pieces_k.append(dk)
        pieces_v.append(dv)
        d_qg = d_qg + dqg
        d_kg = d_kg + dkg
    dproj = jnp.concatenate(pieces_q + pieces_k + pieces_v, axis=1)
    d_w_in = matmul(sv["h"], dproj, "tn", F32, "dsw_in_dw", col_shards=N_SHARD)
    dh = matmul(dproj, w_in, "nt", F32, "dsw_in_dx", col_shards=N_SHARD)
    d_rel = dsw_bias_grad(jnp.concatenate(dbs, axis=0))
    return dh, dict(w_in=d_w_in, q_gain2=d_qg, k_gain2=d_kg, rel=d_rel, w_out=d_w_out)


FT = 128


def ffn_forward(h, w_in, w_out, tag):
    S = h.shape[0]
    gu = matmul(h, w_in, "nn", BF16, f"ffn_in_{tag}", col_shards=N_SHARD)
    gu_row = Row(gu, (FT, 2 * FFN), lambda i: (i, 0), splits=[FFN, FFN], gdtype=BF16)
    (a,) = rowwise(f_swiglu, [gu_row], [], [Out((S, FFN), BF16, (FT, FFN), lambda i: (i, 0))], (S // FT,), f"ffn_act_{tag}")
    f = matmul(a, w_out, "nn", F32, f"ffn_out_{tag}")
    return f, dict(h=h, gu_row=gu_row, a=a)


def ffn_backward(df, sv, w_in, w_out, tag):
    S = df.shape[0]
    da = matmul(df, w_out, "nt", BF16, f"ffn_out_dx_{tag}")
    d_w_out = matmul(sv["a"], df, "tn", F32, f"ffn_out_dw_{tag}")
    (dgu,), _ = rowwise_bwd(f_swiglu, [sv["gu_row"]], [], [Row(da, (FT, FFN), lambda i: (i, 0))], (S // FT,), f"ffn_act_bwd_{tag}")
    d_w_in = matmul(sv["h"], dgu, "tn", F32, f"ffn_in_dw_{tag}", col_shards=N_SHARD)
    dh = matmul(dgu, w_in, "nt", F32, f"ffn_in_dx_{tag}", col_shards=N_SHARD)
    return dh, d_w_in, d_w_out


def f_norm_only(ids, x, gain, sc, sh):
    return (_normmod(x, gain, sc, sh),)


def _wide(a, **kw):
    return Row(a, (RT, D), lambda i: (i, 0), **kw)


def _wide_out(S, dtype):
    return Out((S, D), dtype, (RT, D), lambda i: (i, 0))


def adamw(w, g, m, v, name):
    shape = w.shape
    C = shape[-1]
    R = int(np.prod(shape[:-1]))
    w2, g2, m2, v2 = (a.reshape(R, C) for a in (w, g, m, v))
    br = R
    if R > 256:
        br = max(b for b in range(8, 257, 8) if R % b == 0)
    c1 = 1.0 / (1.0 - ADAM_B1 ** ADAM_STEP)
    c2 = 1.0 / (1.0 - ADAM_B2 ** ADAM_STEP)

    def body(w_ref, g_ref, m_ref, v_ref, d_ref, nm_ref, nv_ref):
        gg = g_ref[...]
        mm_ = ADAM_B1 * m_ref[...] + (1.0 - ADAM_B1) * gg
        vv = ADAM_B2 * v_ref[...] + (1.0 - ADAM_B2) * (gg * gg)
        d_ref[...] = -ADAM_LR * ((mm_ * c1) / (jnp.sqrt(vv * c2) + ADAM_EPS) + ADAM_WD * w_ref[...])
        nm_ref[...] = mm_
        nv_ref[...] = vv

    spec = pl.BlockSpec((br, C), lambda i: (i, 0))
    d, nm, nv = pl.pallas_call(
        body, name=name, grid=(R // br,), in_specs=[spec] * 4, out_specs=[spec] * 3,
        out_shape=[SDS((R, C), F32)] * 3, compiler_params=_cp(1),
    )(w2, g2, m2, v2)
    return d.reshape(shape), nm.reshape(shape), nv.reshape(shape)


def _place():
    x, y, c = lax.axis_index("x"), lax.axis_index("y"), lax.axis_index("c")
    chips = [(1 - x, y), (x, 1 - y), (1 - x, 1 - y)]
    return x, y, c, chips


def all_gather_small(blk, name):
    m_per, n = blk.shape

    def body(x_ref, out_ref, send_sems, recv_sems, local_sem):
        x, y, c, chips = _place()
        me, sibling = (x, y, c), (x, y, 1 - c)

        def rows(px, py, pc):
            return out_ref.at[pl.ds((4 * px + 2 * py + pc) * m_per, m_per), :]

        def copy(k, block, to, src=None):
            return pltpu.make_async_remote_copy(
                src_ref=rows(*block) if src is None else src, dst_ref=rows(*block),
                send_sem=send_sems.at[k], recv_sem=recv_sems.at[k], device_id=to, device_id_type=MESH)

        mine = pltpu.make_async_copy(x_ref, rows(*me), local_sem)
        mine.start()
        first = [copy(0, me, sibling, src=x_ref)]
        first += [copy(1 + j, me, (*chip, c), src=x_ref) for j, chip in enumerate(chips)]
        for cp in first:
            cp.start()
        passed = [copy(4 + j, (*chip, c), sibling) for j, chip in enumerate(chips)]
        for j, chip in enumerate(chips):
            copy(1 + j, (*chip, c), me).wait_recv()
            passed[j].start()
        copy(0, sibling, me).wait_recv()
        for j, chip in enumerate(chips):
            copy(4 + j, (*chip, 1 - c), me).wait_recv()
        for cp in first + passed:
            cp.wait_send()
        mine.wait()

    return pl.pallas_call(
        body, name=name, out_shape=SDS((N_DEV * m_per, n), blk.dtype),
        in_specs=[pl.BlockSpec(memory_space=pltpu.VMEM)], out_specs=pl.BlockSpec(memory_space=pltpu.VMEM),
        scratch_shapes=[pltpu.SemaphoreType.DMA((7,)), pltpu.SemaphoreType.DMA((7,)), pltpu.SemaphoreType.DMA],
    )(blk)


def _half(cc, rh):
    return pl.ds(pl.multiple_of(cc * rh, 16), rh)


def all_gather_shards(ws):
    n = len(ws)

    def body(*refs):
        w_refs, out_refs = refs[:n], refs[n:2 * n]
        send_sems, recv_sems, local_sems, own_sems = refs[2 * n:]
        x, y, c, chips = _place()
        sibling = (x, y, 1 - c)
        s_me = 2 * x + y

        def copy(k, src, dst, to):
            return pltpu.make_async_remote_copy(src_ref=src, dst_ref=dst, send_sem=send_sems.at[k], recv_sem=recv_sems.at[k],
                                                device_id=to, device_id_type=MESH)

        local, sends, passed = [], [], []
        for k in range(n):
            rh = ws[k].shape[0] // 2
            cp = pltpu.make_async_remote_copy(src_ref=w_refs[k], dst_ref=out_refs[k].at[s_me], send_sem=local_sems.at[k],
                                              recv_sem=own_sems.at[k], device_id=sibling, device_id_type=MESH)
            cp.start()
            local.append(cp)
            for j, chip in enumerate(chips):
                sd = copy(6 * k + j, w_refs[k].at[_half(c, rh)], out_refs[k].at[s_me, _half(c, rh)], (*chip, c))
                sd.start()
                sends.append(sd)
        for k in range(n):
            rh = ws[k].shape[0] // 2
            for j, (px, py) in enumerate(chips):
                got = out_refs[k].at[2 * px + py, _half(c, rh)]
                copy(6 * k + j, got, got, (px, py, c)).wait_recv()
                fw = copy(6 * k + 3 + j, got, got, sibling)
                fw.start()
                passed.append(fw)
        for k in range(n):
            rh = ws[k].shape[0] // 2
            for j, (px, py) in enumerate(chips):
                got = out_refs[k].at[2 * px + py, _half(1 - c, rh)]
                copy(6 * k + 3 + j, got, got, sibling).wait_recv()
        for cp in sends + passed:
            cp.wait_send()
        for cp in local:
            cp.wait()

    return pl.pallas_call(
        body, name="weights_all_gather", out_shape=[SDS((N_SHARD,) + w.shape, w.dtype) for w in ws],
        in_specs=[ANY] * n, out_specs=[ANY] * n,
        scratch_shapes=[pltpu.SemaphoreType.DMA((6 * n,)), pltpu.SemaphoreType.DMA((6 * n,)), pltpu.SemaphoreType.DMA((n,)),
                        pltpu.SemaphoreType.DMA((n,))],
    )(*ws)


def sibling_exchange(sends, name):
    n = len(sends)

    def body(*refs):
        s_refs, o_refs, send_sems, recv_sems = refs[:n], refs[n:2 * n], refs[2 * n], refs[2 * n + 1]
        x, y, c, _ = _place()
        cps = [pltpu.make_async_remote_copy(src_ref=s_refs[k], dst_ref=o_refs[k], send_sem=send_sems.at[k], recv_sem=recv_sems.at[k],
                                            device_id=(x, y, 1 - c), device_id_type=MESH) for k in range(n)]
        for cp in cps:
            cp.start()
        for cp in cps:
            cp.wait()

    return pl.pallas_call(
        body, name=name, out_shape=[SDS(s.shape, s.dtype) for s in sends], in_specs=[ANY] * n, out_specs=[ANY] * n,
        scratch_shapes=[pltpu.SemaphoreType.DMA((n,)), pltpu.SemaphoreType.DMA((n,))],
    )(*sends)


def scatter_to_chips(parts):
    n = len(parts)

    def body(*refs):
        p_refs, o_refs, send_sems, recv_sems = refs[:n], refs[n:2 * n], refs[2 * n], refs[2 * n + 1]
        x, y, c, chips = _place()
        cps = []
        for k in range(n):
            for j, (px, py) in enumerate(chips):
                cp = pltpu.make_async_remote_copy(src_ref=p_refs[k].at[2 * px + py], dst_ref=o_refs[k].at[j],
                                                  send_sem=send_sems.at[3 * k + j], recv_sem=recv_sems.at[3 * k + j],
                                                  device_id=(px, py, c), device_id_type=MESH)
                cp.start()
                cps.append(cp)
        for cp in cps:
            cp.wait()

    return pl.pallas_call(
        body, name="grads_scatter", out_shape=[SDS((3,) + p.shape[1:], p.dtype) for p in parts], in_specs=[ANY] * n, out_specs=[ANY] * n,
        scratch_shapes=[pltpu.SemaphoreType.DMA((3 * n,)), pltpu.SemaphoreType.DMA((3 * n,))],
    )(*parts)


def merge_halves(halves):
    n = len(halves)

    def body(*refs):
        h_refs, o_refs = refs[:n], refs[n:2 * n]
        send_sems, recv_sems, local_sems = refs[2 * n:]
        x, y, c, _ = _place()
        local, cps = [], []
        for k in range(n):
            rh = halves[k].shape[0]
            lc = pltpu.make_async_copy(h_refs[k], o_refs[k].at[_half(c, rh)], local_sems.at[k])
            lc.start()
            local.append(lc)
            cp = pltpu.make_async_remote_copy(src_ref=h_refs[k], dst_ref=o_refs[k].at[_half(c, rh)], send_sem=send_sems.at[k],
                                              recv_sem=recv_sems.at[k], device_id=(x, y, 1 - c), device_id_type=MESH)
            cp.start()
            cps.append(cp)
        for k in range(n):
            rh = halves[k].shape[0]
            got = o_refs[k].at[_half(1 - c, rh)]
            pltpu.make_async_remote_copy(src_ref=got, dst_ref=got, send_sem=send_sems.at[k], recv_sem=recv_sems.at[k],
                                         device_id=(x, y, 1 - c), device_id_type=MESH).wait_recv()
        for cp in cps:
            cp.wait_send()
        for lc in local:
            lc.wait()

    return pl.pallas_call(
        body, name="grads_merge_halves", out_shape=[SDS((2 * h.shape[0], h.shape[1]), h.dtype) for h in halves],
        in_specs=[ANY] * n, out_specs=[ANY] * n,
        scratch_shapes=[pltpu.SemaphoreType.DMA((n,)), pltpu.SemaphoreType.DMA((n,)), pltpu.SemaphoreType.DMA((n,))],
    )(*halves)


def add_rows(arrs, out_dtype, name, rt=256):
    Rr, W = arrs[0].shape

    def fn(ids, *vals):
        acc = vals[0]
        for v in vals[1:]:
            acc = acc + v
        return (acc,)

    t = rt if Rr % rt == 0 else max(b for b in range(16, rt + 1, 16) if Rr % b == 0)
    (out,) = rowwise(fn, [Row(a, (t, W), lambda i: (i, 0)) for a in arrs], [],
                     [Out((Rr, W), out_dtype, (t, W), lambda i: (i, 0))], (Rr // t,), name)
    return out


HBM_SPEC = pl.BlockSpec(memory_space=pltpu.HBM)
SEM_SPEC = pl.BlockSpec(memory_space=pltpu.SEMAPHORE)
DATAFLOW = pltpu.SideEffectType.DATAFLOW_SIDE_EFFECTING


def _in_hbm(a):
    return pltpu.with_memory_space_constraint(a, pltpu.HBM)


def _gather_copies(w_refs, land_refs, send_sems, recv_sems):
    x, y, c, chips = _place()
    targets = [(x, y, 1 - c)] + [(*chip, c) for chip in chips]
    cps = []
    for k, (w_ref, land_ref) in enumerate(zip(w_refs, land_refs)):
        for j, to in enumerate(targets):
            cps.append(pltpu.make_async_remote_copy(src_ref=w_ref, dst_ref=land_ref.at[2 * x + y], send_sem=send_sems.at[4 * k + j],
                                                    recv_sem=recv_sems.at[4 * k + j], device_id=to, device_id_type=MESH))
    return cps


def _scatter_copies(p_refs, land_refs, send_sems, recv_sems):
    x, y, c, chips = _place()
    cps = []
    for k, (p_ref, land_ref) in enumerate(zip(p_refs, land_refs)):
        for j, (px, py) in enumerate(chips):
            cps.append(pltpu.make_async_remote_copy(src_ref=p_ref.at[2 * px + py], dst_ref=land_ref.at[j], send_sem=send_sems.at[3 * k + j],
                                                    recv_sem=recv_sems.at[3 * k + j], device_id=(px, py, c), device_id_type=MESH))
    return cps


def copies_start(srcs, land_shapes, make_copies, per_src, name):
    n = len(srcs)
    m = per_src * n

    def body(*refs):
        src_refs, land_refs = refs[:n], refs[n:2 * n]
        send_sems, recv_sems, token = refs[2 * n], refs[2 * n + 1], refs[-1]
        for cp in make_copies(src_refs, land_refs, send_sems, recv_sems):
            cp.start()
        token[...] = jnp.zeros_like(token)

    lands = [lax.empty(shp, s.dtype) for shp, s in zip(land_shapes, srcs)]
    res = pl.pallas_call(
        body, name=name,
        out_shape=(pltpu.SemaphoreType.DMA((m,)), pltpu.SemaphoreType.DMA((m,)), *[pltpu.HBM(s.shape, s.dtype) for s in srcs],
                   *[pltpu.HBM(shp, s.dtype) for shp, s in zip(land_shapes, srcs)], SDS((8, LANES), F32)),
        in_specs=[HBM_SPEC] * (2 * n),
        out_specs=(SEM_SPEC, SEM_SPEC, *[HBM_SPEC] * (2 * n), pl.BlockSpec(memory_space=pltpu.VMEM)),
        input_output_aliases={i: 2 + i for i in range(2 * n)},
        compiler_params=pltpu.CompilerParams(has_side_effects=DATAFLOW),
    )(*[_in_hbm(s) for s in srcs], *[_in_hbm(l) for l in lands])
    return res[0], res[1], list(res[2:2 + n]), list(res[2 + n:2 + 2 * n]), res[-1]


def copies_wait(send_sems, recv_sems, srcs, lands, make_copies, after, name):
    n = len(srcs)

    def body(*refs):
        src_refs, land_refs = refs[:n], refs[n:2 * n]
        for cp in make_copies(src_refs, land_refs, refs[2 * n], refs[2 * n + 1]):
            cp.wait_send()
            cp.wait_recv()

    res = pl.pallas_call(
        body, name=name,
        out_shape=(*[pltpu.HBM(s.shape, s.dtype) for s in srcs], *[pltpu.HBM(l.shape, l.dtype) for l in lands]),
        in_specs=[HBM_SPEC] * (2 * n) + [SEM_SPEC, SEM_SPEC, ANY],
        out_specs=tuple([HBM_SPEC] * (2 * n)),
        input_output_aliases={i: i for i in range(2 * n)},
        compiler_params=pltpu.CompilerParams(has_side_effects=DATAFLOW),
    )(*srcs, *lands, send_sems, recv_sems, after)
    return list(res[n:])


PACK = (("gdn_w_in", 2), ("gdn_w_out", 1), ("w_ffn_in", 2), ("w_ffn_out", 1), ("dsw_w_in", 2), ("dsw_w_out", 2))
PACK_ALIGN = 32


def _pack_rows(sizes):
    total = sum(sizes)
    rows = -(-total // D)
    return -(-rows // PACK_ALIGN) * PACK_ALIGN


def pack_blocks(blocks, dtype):
    flat = [b.astype(dtype).reshape(-1) for b in blocks]
    total = sum(f.shape[0] for f in flat)
    R = _pack_rows([f.shape[0] for f in flat])
    flat.append(jnp.zeros((R * D - total,), dtype))
    return jnp.concatenate(flat).reshape(R, D)


def unpack_blocks(buf, shapes):
    flat = buf.reshape(-1)
    out, off = [], 0
    for shp in shapes:
        n = int(np.prod(shp))
        out.append(flat[off:off + n].reshape(shp))
        off += n
    return out


def _shard_slice(a, axis, s):
    n = a.shape[axis] // N_SHARD
    return lax.slice_in_dim(a, s * n, (s + 1) * n, axis=axis)


def _pad_lanes(v):
    return jnp.concatenate([v.astype(F32), jnp.zeros((LANES - v.shape[0],), F32)])[None]


def kernel(x, c, w_ada, b_ada, norm_mix, norm_ffn, w_ffn_in, w_ffn_out, gdn_w_in, gdn_conv, gdn_a_log, gdn_dt_bias, gdn_out_norm, gdn_w_out, dsw_w_in, dsw_q_norm, dsw_k_norm, dsw_w_out, rel_bias, loss_target, m_w_ada, m_b_ada, m_norm_mix, m_norm_ffn, m_w_ffn_in, m_w_ffn_out, m_gdn_w_in, m_gdn_conv, m_gdn_a_log, m_gdn_dt_bias, m_gdn_out_norm, m_gdn_w_out, m_dsw_w_in, m_dsw_q_norm, m_dsw_k_norm, m_dsw_w_out, m_rel_bias, v_w_ada, v_b_ada, v_norm_mix, v_norm_ffn, v_w_ffn_in, v_w_ffn_out, v_gdn_w_in, v_gdn_conv, v_gdn_a_log, v_gdn_dt_bias, v_gdn_out_norm, v_gdn_w_out, v_dsw_w_in, v_dsw_q_norm, v_dsw_k_norm, v_dsw_w_out, v_rel_bias):
    S = x.shape[1]
    nt = S // RT
    xi, yi, ci = lax.axis_index("x"), lax.axis_index("y"), lax.axis_index("c")
    me = 4 * xi + 2 * yi + ci
    s_me = 2 * xi + yi
    x0, tgt = x[0], loss_target[0]
    shard = dict(w_ffn_in=w_ffn_in, w_ffn_out=w_ffn_out, gdn_w_in=gdn_w_in, gdn_w_out=gdn_w_out, dsw_w_in=dsw_w_in, dsw_w_out=dsw_w_out)

    whole = lambda a: Row(a, a.shape, lambda i: (0,) * a.ndim)
    (cond8,) = rowwise(lambda ids, v: (_silu(v),), [whole(c.reshape(8, LANES))], [], [Out((8, LANES), F32, (8, LANES), lambda i: (0, 0))], (1,), "cond")
    cond_all = all_gather_small(cond8, "gather_cond").reshape(N_DEV, D)
    cond16 = jnp.concatenate([cond_all, jnp.zeros((8, D), F32)], axis=0)
    ada_cols = w_ada.shape[2]
    mods = [matmul(cond16, w_ada[l], "nn", F32, f"ada_{l}")[:N_DEV] for l in range(2)]
    buf = jnp.concatenate([jnp.stack(mods, axis=1).reshape(-1, LANES), gdn_conv.reshape(-1, LANES)], axis=0)
    n_mod_rows = N_DEV * 2 * ada_cols // LANES
    got = all_gather_small(buf, "gather_mod").reshape(N_DEV, buf.shape[0], LANES)
    mod_parts, conv_parts = [], []
    for s in range(N_SHARD):
        from_dev = got[2 * s]
        mod_parts.append(lax.dynamic_index_in_dim(from_dev[:n_mod_rows].reshape(N_DEV, 2, ada_cols), me, 0, keepdims=False))
        conv_parts.append(from_dev[n_mod_rows:].reshape(4, -1))
    mod_nb = jnp.concatenate(mod_parts, axis=1)
    conv_w = jnp.concatenate(conv_parts, axis=1)
    (mod,) = rowwise(lambda ids, a, b: (a + b,), [whole(mod_nb), whole(b_ada)], [], [Out(mod_nb.shape, F32, mod_nb.shape, lambda i: (0, 0))], (1,), "mod_bias")
    mod = mod.reshape(2, 6, 1, D)
    sh1, sc1, g1, sh2, sc2, g2 = ([mod[l, k] for l in range(2)] for k in range(6))
    gmix = [norm_mix[l][None] for l in range(2)]
    gffn = [norm_ffn[l][None] for l in range(2)]

    gcols = gdn_w_in.shape[2]
    g_gdn_in, g_gdn_out = all_gather_shards([gdn_w_in[0].astype(BF16), gdn_w_out[0].astype(BF16)])
    gathered = lambda ws: [(N_SHARD,) + w.shape for w in ws]
    gate = (jnp.minimum(jnp.abs(g_gdn_in[0, 0, 0].astype(F32)), 0.0) + jnp.minimum(jnp.abs(mod[0, 0, 0, 0]), 0.0)).astype(BF16)
    w2 = [w_ffn_in[0].astype(BF16) + gate, w_ffn_out[0].astype(BF16) + gate]
    w3 = [dsw_w_in[0].astype(BF16) + gate, dsw_w_out[0].astype(BF16) + gate, w_ffn_in[1].astype(BF16) + gate, w_ffn_out[1].astype(BF16) + gate]
    fly2 = copies_start(w2, gathered(w2), _gather_copies, 4, "weights_ffn0_start")
    fly3 = copies_start(w3, gathered(w3), _gather_copies, 4, "weights_layer1_start")
    started = fly2[4][0, 0] + fly3[4][0, 0]
    w_gdn = jnp.concatenate([g_gdn_in[s] for s in range(N_SHARD)] + [jnp.zeros((D, GDN_PROJ - N_SHARD * gcols), BF16)], axis=1)
    alog, dtb = _pad_lanes(gdn_a_log[0]), _pad_lanes(gdn_dt_bias[0])
    qg2 = jnp.concatenate([dsw_q_norm, dsw_q_norm], axis=1)
    kg2 = jnp.concatenate([dsw_k_norm, dsw_k_norm], axis=1)
    w_gdn_out = g_gdn_out.reshape(GDN_H * LANES, D)
    gdn_args = (w_gdn, conv_w, alog, dtb, gdn_out_norm, w_gdn_out)
    sc1[0] = sc1[0] + started

    (h10,) = rowwise(f_norm_only, [_wide(x0)], [gmix[0], sc1[0], sh1[0]], [_wide_out(S, BF16)], (nt,), "l0_norm")
    y0, sv_g = gdn_forward(h10, *gdn_args)
    x1, h20 = rowwise(f_resid_norm, [_wide(x0), _wide(y0)], [g1[0], gffn[0], sc2[0], sh2[0]], [_wide_out(S, F32), _wide_out(S, BF16)], (nt,), "l0_mid")
    g_in0, g_out0 = copies_wait(*fly2[:4], _gather_copies, y0, "weights_ffn0_wait")
    w_ffn = [(g_in0, g_out0.reshape(FFN, D)), None]
    f0, sv_f0 = ffn_forward(h20, *w_ffn[0], "0")
    x2, h11 = rowwise(f_resid_norm, [_wide(x1), _wide(f0)], [g2[0], gmix[1], sc1[1], sh1[1]], [_wide_out(S, F32), _wide_out(S, BF16)], (nt,), "l1_in")
    g_dsw_in, g_dsw_out, g_in1, g_out1 = copies_wait(*fly3[:4], _gather_copies, f0, "weights_layer1_wait")
    w_ffn[1] = (g_in1, g_out1.reshape(FFN, D))
    dsw_args = (g_dsw_in, qg2, kg2)
    y1, sv_d = dsw_forward(h11, *dsw_args, rel_bias, g_dsw_out)
    x3, h21 = rowwise(f_resid_norm, [_wide(x2), _wide(y1)], [g1[1], gffn[1], sc2[1], sh2[1]], [_wide_out(S, F32), _wide_out(S, BF16)], (nt,), "l1_mid")
    f1, sv_f1 = ffn_forward(h21, *w_ffn[1], "1")
    part_spec = lambda a: Row(a, (None, 1, D), lambda i: (i, 0, 0))
    (parts,) = rowwise(f_loss, [_wide(x3), _wide(f1), _wide(tgt)], [g2[1]], [Out((nt, 1, D), F32, (None, 1, D), lambda i: (i, 0, 0))], (nt,), "loss")
    loss = lax.psum(jnp.sum(parts), ("x", "y", "c"))

    (dx3, df1), (dg2_1,) = rowwise_bwd(f_loss, [_wide(x3), _wide(f1, gdtype=BF16), _wide(tgt, diff=False)], [g2[1]],
                                       [part_spec(jnp.ones((nt, 1, D), F32))], (nt,), "loss_bwd")
    dh21, d_win1, d_wout1 = ffn_backward(df1, sv_f1, *w_ffn[1], "1")
    (dx2, dy1), (dg1_1, dgf1, dsc2_1, dsh2_1) = rowwise_bwd(
        f_resid_norm, [_wide(x2), _wide(y1, gdtype=BF16)], [g1[1], gffn[1], sc2[1], sh2[1]], [_wide(dx3), _wide(dh21)], (nt,), "l1_mid_bwd")
    dh11, g_d = dsw_backward(dy1, sv_d, *dsw_args, g_dsw_out)
    (dx1, df0), (dg2_0, dgm1, dsc1_1, dsh1_1) = rowwise_bwd(
        f_resid_norm, [_wide(x1), _wide(f0, gdtype=BF16)], [g2[0], gmix[1], sc1[1], sh1[1]], [_wide(dx2), _wide(dh11)], (nt,), "l1_in_bwd")
    by_shard = lambda a: a.reshape(N_SHARD, a.shape[0] // N_SHARD, a.shape[1])
    landing = lambda ps: [(3,) + p.shape[1:] for p in ps]
    dws3 = [g_d["w_in"], g_d["w_out"], d_win1, by_shard(d_wout1)]
    parts3 = [a.astype(BF16) for a in dws3]
    gfly3 = copies_start(parts3, landing(parts3), _scatter_copies, 3, "grads_layer1_start")
    w_out0 = w_ffn[0][1] + gfly3[4][0, 0].astype(BF16)
    dh20, d_win0, d_wout0 = ffn_backward(df0, sv_f0, w_ffn[0][0], w_out0, "0")
    (dx0p, dy0), (dg1_0, dgf0, dsc2_0, dsh2_0) = rowwise_bwd(
        f_resid_norm, [_wide(x0), _wide(y0, gdtype=BF16)], [g1[0], gffn[0], sc2[0], sh2[0]], [_wide(dx1), _wide(dh20)], (nt,), "l0_mid_bwd")
    dws2 = [d_win0, by_shard(d_wout0)]
    parts2 = [a.astype(BF16) for a in dws2]
    gfly2 = copies_start(parts2, landing(parts2), _scatter_copies, 3, "grads_ffn0_start")
    gdn_args = gdn_args[:5] + (w_gdn_out + gfly2[4][0, 0].astype(BF16),)
    dh10, g_g = gdn_backward(dy0, sv_g, *gdn_args)
    (grad_x,), (dgm0, dsc1_0, dsh1_0) = rowwise_bwd(f_first, [_wide(x0)], [gmix[0], sc1[0], sh1[0]], [_wide(dx0p), _wide(dh10)], (nt,), "l0_norm_bwd")

    dmod = jnp.concatenate([dsh1_0, dsc1_0, dg1_0, dsh2_0, dsc2_0, dg2_0, dsh1_1, dsc1_1, dg1_1, dsh2_1, dsc2_1, dg2_1], axis=1)
    d_rel = jnp.transpose(g_d["rel"][:, :, 0])
    fold = lambda v: v[:, :DSW_DH] + v[:, DSW_DH:]
    small = [dmod, jnp.concatenate([dgm0, dgm1], axis=1), jnp.concatenate([dgf0, dgf1], axis=1), g_g["conv"].reshape(1, -1),
             g_g["alog"], g_g["dtb"], g_g["gain"], _pad_lanes(fold(g_d["q_gain2"])[0]), _pad_lanes(fold(g_d["k_gain2"])[0]),
             d_rel.reshape(1, -1)]
    used = [v.shape[1] // LANES for v in small]
    sizes = [-(-u // 8) * 8 for u in used]
    pad8 = lambda v, u, s: jnp.concatenate([v.reshape(u, LANES), jnp.zeros((s - u, LANES), F32)], axis=0) if s > u else v.reshape(u, LANES)
    pad_rows = sum(sizes)
    sbuf = jnp.concatenate([pad8(v, u, s) for v, u, s in zip(small, used, sizes)], axis=0)
    sgot = all_gather_small(sbuf, "gather_small_grads")
    ssum = add_rows([sgot[d * pad_rows:(d + 1) * pad_rows] for d in range(N_DEV)], F32, "sum_small_grads", rt=pad_rows)
    offs = np.cumsum([0] + sizes)
    take = lambda k: ssum[offs[k]:offs[k] + used[k]].reshape(1, -1)
    grad_b_ada = take(0).reshape(2, 6 * D)
    grad_norm_mix = take(1).reshape(2, D)
    grad_norm_ffn = take(2).reshape(2, D)
    conv_full = take(3).reshape(4, -1)
    ncv = gdn_conv.shape[2]
    grad_gdn_conv = lax.dynamic_slice_in_dim(conv_full, s_me * ncv, ncv, axis=1)[None]
    grad_a_log = take(4)[:, :GDN_H]
    grad_dt_bias = take(5)[:, :GDN_H]
    grad_out_norm = take(6)
    grad_q_norm = take(7)[:, :DSW_DH]
    grad_k_norm = take(8)[:, :DSW_DH]
    grad_rel = take(9).reshape(REL_BUCKETS, 3 * GDN_H)
    dmod_all = sgot.reshape(N_DEV, pad_rows, LANES)[:, :used[0]].reshape(N_DEV, 2, 6 * D)
    dmod_mine = lax.dynamic_slice_in_dim(dmod_all, s_me * ada_cols, ada_cols, axis=2)
    dmod16 = jnp.concatenate([dmod_mine, jnp.zeros_like(dmod_mine)], axis=0)
    grad_w_ada = jnp.stack([matmul(cond16, dmod16[:, l], "tn", F32, f"ada_dw_{l}") for l in range(2)])

    dg_in = jnp.stack([g_g["w_in"][:, s * gcols:(s + 1) * gcols] for s in range(N_SHARD)])
    dws = [dg_in, by_shard(g_g["w_out"])]
    keeps, gives = [], []
    for a in dws:
        rh = a.shape[1] // 2
        keeps.append(lax.dynamic_slice_in_dim(a, ci * rh, rh, axis=1))
        gives.append(lax.dynamic_slice_in_dim(a, (1 - ci) * rh, rh, axis=1).astype(BF16))
    from_sib = sibling_exchange(gives, "grads_to_sibling")
    flat2 = lambda a: a.reshape(-1, a.shape[-1])
    parts = [add_rows([flat2(k_), flat2(f_)], BF16, f"grads_chip_sum_{i}").reshape(k_.shape) for i, (k_, f_) in enumerate(zip(keeps, from_sib))]
    others = scatter_to_chips(parts)
    halves = []
    for i, (p_, o_) in enumerate(zip(parts, others)):
        own = lax.dynamic_index_in_dim(p_, s_me, 0, keepdims=False)
        halves.append(add_rows([own, o_[0], o_[1], o_[2]], F32, f"grads_sum_{i}"))
    sib_halves = sibling_exchange(halves, "grads_from_sibling")

    def whole_shard(mine, theirs):
        both = jnp.stack([mine, theirs])
        return jnp.concatenate([lax.dynamic_index_in_dim(both, ci, 0, keepdims=False),
                                lax.dynamic_index_in_dim(both, 1 - ci, 0, keepdims=False)], axis=0)

    s_gdn_in, s_gdn_out = [whole_shard(a, b) for a, b in zip(halves, sib_halves)]
    got3 = copies_wait(*gfly3[:4], _scatter_copies, grad_x, "grads_layer1_wait")
    got2 = copies_wait(*gfly2[:4], _scatter_copies, grad_x, "grads_ffn0_wait")
    core_sums = []
    for i, (full, got) in enumerate(zip(dws3 + dws2, got3 + got2)):
        own = lax.dynamic_index_in_dim(full, s_me, 0, keepdims=False)
        core_sums.append(add_rows([own, got[0], got[1], got[2]], F32, f"grads_core_sum_{i}"))
    sib_sums = sibling_exchange(core_sums, "grads_core_sums_swap")
    s_dsw_in, s_dsw_out, s_in1, s_out1, s_in0, s_out0 = [add_rows([a, b], F32, f"grads_chip_total_{i}")
                                                          for i, (a, b) in enumerate(zip(core_sums, sib_sums))]
    gsh = dict(gdn_w_in=s_gdn_in[None], gdn_w_out=s_gdn_out[None],
               w_ffn_in=jnp.stack([s_in0, s_in1]), w_ffn_out=jnp.stack([s_out0, s_out1]),
               dsw_w_in=s_dsw_in[None], dsw_w_out=s_dsw_out[None])

    grads = dict(w_ada=grad_w_ada, b_ada=grad_b_ada, norm_mix=grad_norm_mix, norm_ffn=grad_norm_ffn, w_ffn_in=gsh["w_ffn_in"],
                 w_ffn_out=gsh["w_ffn_out"], gdn_w_in=gsh["gdn_w_in"], gdn_conv=grad_gdn_conv, gdn_a_log=grad_a_log,
                 gdn_dt_bias=grad_dt_bias, gdn_out_norm=grad_out_norm, gdn_w_out=gsh["gdn_w_out"], dsw_w_in=gsh["dsw_w_in"],
                 dsw_q_norm=grad_q_norm, dsw_k_norm=grad_k_norm, dsw_w_out=gsh["dsw_w_out"], rel_bias=grad_rel)
    weights = dict(w_ada=w_ada, b_ada=b_ada, norm_mix=norm_mix, norm_ffn=norm_ffn, w_ffn_in=w_ffn_in, w_ffn_out=w_ffn_out,
                   gdn_w_in=gdn_w_in, gdn_conv=gdn_conv, gdn_a_log=gdn_a_log, gdn_dt_bias=gdn_dt_bias, gdn_out_norm=gdn_out_norm,
                   gdn_w_out=gdn_w_out, dsw_w_in=dsw_w_in, dsw_q_norm=dsw_q_norm, dsw_k_norm=dsw_k_norm, dsw_w_out=dsw_w_out,
                   rel_bias=rel_bias)
    ms = dict(w_ada=m_w_ada, b_ada=m_b_ada, norm_mix=m_norm_mix, norm_ffn=m_norm_ffn, w_ffn_in=m_w_ffn_in, w_ffn_out=m_w_ffn_out,
              gdn_w_in=m_gdn_w_in, gdn_conv=m_gdn_conv, gdn_a_log=m_gdn_a_log, gdn_dt_bias=m_gdn_dt_bias, gdn_out_norm=m_gdn_out_norm,
              gdn_w_out=m_gdn_w_out, dsw_w_in=m_dsw_w_in, dsw_q_norm=m_dsw_q_norm, dsw_k_norm=m_dsw_k_norm, dsw_w_out=m_dsw_w_out,
              rel_bias=m_rel_bias)
    vs = dict(w_ada=v_w_ada, b_ada=v_b_ada, norm_mix=v_norm_mix, norm_ffn=v_norm_ffn, w_ffn_in=v_w_ffn_in, w_ffn_out=v_w_ffn_out,
              gdn_w_in=v_gdn_w_in, gdn_conv=v_gdn_conv, gdn_a_log=v_gdn_a_log, gdn_dt_bias=v_gdn_dt_bias, gdn_out_norm=v_gdn_out_norm,
              gdn_w_out=v_gdn_w_out, dsw_w_in=v_dsw_w_in, dsw_q_norm=v_dsw_q_norm, dsw_k_norm=v_dsw_k_norm, dsw_w_out=v_dsw_w_out,
              rel_bias=v_rel_bias)
    names = list(weights)
    deltas, new_m, new_v = [], [], []
    for n in names:
        g = grads[n].reshape(weights[n].shape)
        grads[n] = g
        d, nm, nv = adamw(weights[n], g, ms[n], vs[n], f"adamw_{n}")
        deltas.append(d)
        new_m.append(nm)
        new_v.append(nv)
    return (loss, grad_x[None], *[grads[n] for n in names], *deltas, *new_m, *new_v)
```

```python
import functools
import math

import numpy as np
import jax
import jax.numpy as jnp
from jax import lax
from jax.experimental import pallas as pl
from jax.experimental.pallas import tpu as pltpu

F32 = jnp.float32
BF16 = jnp.bfloat16
SDS = jax.ShapeDtypeStruct
MESH = pl.DeviceIdType.MESH
ANY = pl.BlockSpec(memory_space=pl.ANY)

D = 1024
EPS = 1e-6
LANES = 128
GDN_H = 8
GDN_DK = 128
GDN_C = 64
DSW_GROUPS = ((128, 1), (512, 4), (2048, 16))
DSW_SPAN = 128
DSW_DH = 64
DSW_HG = 512
REL_BUCKETS = 32
REL_MAX_DIST = 2048
FFN = 2816
N_SHARD = 4
N_DEV = 8
VMEM_LIMIT = 48 * 1024 * 1024
NEG = -1e30

ADAM_LR, ADAM_B1, ADAM_B2, ADAM_EPS, ADAM_WD, ADAM_STEP = 0.001, 0.9, 0.999, 1e-08, 0.01, 10


def _cp(n_axes):
    return pltpu.CompilerParams(dimension_semantics=("arbitrary",) * n_axes, vmem_limit_bytes=VMEM_LIMIT)


def _blk(dim, cap):
    if dim <= cap:
        return dim
    best = None
    for b in range(LANES, cap + 1, LANES):
        if dim % b == 0:
            best = b
    assert best is not None, (dim, cap)
    return best


MAX_SHARD_BLOCK = 1408
def matmul(a, b, mode, out_dtype, name, cap_m=MAX_SHARD_BLOCK, cap_n=MAX_SHARD_BLOCK, cap_k=2048, col_shards=0):
    ns = col_shards
    if mode == "nn":
        (M, K) = a.shape
        K2, N = (b.shape[1], ns * b.shape[2]) if ns else b.shape
    elif mode == "nt":
        (M, K) = a.shape
        N, K2 = (b.shape[1], ns * b.shape[2]) if ns else b.shape
    else:
        (K, M), (K2, N) = a.shape, b.shape
    assert K == K2, (a.shape, b.shape, mode)
    if K <= 3072:
        cap_k = K
        if K > 2048:
            cap_n = 1024
    n_unit = N // ns if (ns and mode != "nt") else N
    k_unit = K // ns if (ns and mode == "nt") else K
    bm = _blk(M, cap_m)
    bn = _blk(n_unit, MAX_SHARD_BLOCK) if n_unit != N else _blk(N, cap_n)
    if k_unit != K:
        bk = _blk(k_unit, MAX_SHARD_BLOCK)
    else:
        bk = _blk(K, 1024 if (ns and mode == "tn") else cap_k)
    nk = K // bk
    nps, kps = n_unit // bn, k_unit // bk
    dims = {"nn": ((1,), (0,)), "nt": ((1,), (1,)), "tn": ((0,), (0,))}[mode]

    def dot(a_ref, b_ref):
        return lax.dot_general(a_ref[...].astype(BF16), b_ref[...].astype(BF16), (dims, ((), ())), preferred_element_type=F32)

    def body_one(a_ref, b_ref, o_ref):
        o_ref[...] = dot(a_ref, b_ref).astype(o_ref.dtype)

    def body_acc(a_ref, b_ref, o_ref, acc_ref):
        k = pl.program_id(2)

        @pl.when(k == 0)
        def _():
            acc_ref[...] = jnp.zeros_like(acc_ref)

        acc_ref[...] += dot(a_ref, b_ref)

        @pl.when(k == nk - 1)
        def _():
            o_ref[...] = acc_ref[...].astype(o_ref.dtype)

    a_spec = pl.BlockSpec((bk, bm), lambda i, j, k: (k, i)) if mode == "tn" else pl.BlockSpec((bm, bk), lambda i, j, k: (i, k))
    if mode == "nt":
        b_spec = pl.BlockSpec((None, bn, bk), lambda i, j, k: (k // kps, j, k % kps)) if ns else pl.BlockSpec((bn, bk), lambda i, j, k: (j, k))
    elif mode == "nn" and ns:
        b_spec = pl.BlockSpec((None, bk, bn), lambda i, j, k: (j // nps, k, j % nps))
    else:
        b_spec = pl.BlockSpec((bk, bn), lambda i, j, k: (k, j))
    if mode == "tn" and ns:
        o_spec, o_shape = pl.BlockSpec((None, bm, bn), lambda i, j, k: (j // nps, i, j % nps)), (ns, M, n_unit)
    else:
        o_spec, o_shape = pl.BlockSpec((bm, bn), lambda i, j, k: (i, j)), (M, N)
    return pl.pallas_call(
        body_one if nk == 1 else body_acc, name=name, grid=(M // bm, N // bn, nk),
        in_specs=[a_spec, b_spec], out_specs=o_spec,
        out_shape=SDS(o_shape, out_dtype), scratch_shapes=[] if nk == 1 else [pltpu.VMEM((bm, bn), F32)],
        compiler_params=_cp(3),
    )(a, b)


class Row:
    def __init__(self, arr, bshape, imap, splits=None, diff=True, acc=False, gdtype=F32, gshape=None, gbshape=None, gimap=None,
                 lead=0):
        self.arr, self.bshape, self.imap = arr, tuple(bshape), imap
        self.splits, self.lead = splits, lead
        self.diff, self.acc, self.gdtype = diff, acc, gdtype
        self.gshape = tuple(arr.shape) if gshape is None else tuple(gshape)
        self.gbshape = self.bshape if gbshape is None else tuple(gbshape)
        self.gimap = imap if gimap is None else gimap

    def gspec(self):
        return pl.BlockSpec(self.gbshape, self.gimap)

    def spec(self):
        return pl.BlockSpec(self.bshape, self.imap)

    def pieces(self, ref):
        return _load_pieces(ref, self.splits, self.lead)

    def n_pieces(self):
        return _n_pieces(self.splits, self.lead)


class Out:
    def __init__(self, shape, dtype, bshape, imap, splits=None, lead=0):
        self.shape, self.dtype, self.bshape, self.imap = tuple(shape), dtype, tuple(bshape), imap
        self.splits, self.lead = splits, lead

    def n_pieces(self):
        return _n_pieces(self.splits, self.lead)


def _n_pieces(splits, lead):
    return lead if lead else (1 if splits is None else len(splits))


def _load_pieces(ref, splits, lead):
    if lead:
        return [ref[k].astype(F32) for k in range(lead)]
    if splits is None:
        return [ref[...].astype(F32)]
    out, o = [], 0
    for w in splits:
        out.append(ref[..., o:o + w].astype(F32))
        o += w
    return out


def _store_pieces(ref, splits, lead, vals, accumulate=False):
    def put(idx, v):
        if accumulate:
            ref[idx] += v.astype(ref.dtype)
        else:
            ref[idx] = v.astype(ref.dtype)

    if lead:
        for k in range(lead):
            put(k, vals[k])
    elif splits is None:
        put(..., vals[0])
    else:
        o = 0
        for w, v in zip(splits, vals):
            put((..., slice(o, o + w)), v)
            o += w


def rowwise(fn, rows, params, outs, grid, name):
    nr, npar = len(rows), len(params)

    def body(*refs):
        ids = tuple(pl.program_id(a) for a in range(len(grid)))
        vals = []
        for r, ref in zip(rows, refs[:nr]):
            vals += r.pieces(ref)
        pvals = [ref[...].astype(F32) for ref in refs[nr:nr + npar]]
        res = list(fn(ids, *vals, *pvals))
        o = 0
        for spec, ref in zip(outs, refs[nr + npar:]):
            n = spec.n_pieces()
            _store_pieces(ref, spec.splits, spec.lead, res[o:o + n])
            o += n

    nz = len(grid)
    pspecs = [pl.BlockSpec(p.shape, (lambda *ids, _n=p.ndim: (0,) * _n)) for p in params]
    res = pl.pallas_call(
        body, name=name, grid=grid,
        in_specs=[r.spec() for r in rows] + pspecs,
        out_specs=[pl.BlockSpec(o.bshape, o.imap) for o in outs],
        out_shape=[SDS(o.shape, o.dtype) for o in outs],
        compiler_params=_cp(nz),
    )(*[r.arr for r in rows], *params)
    return list(res)


def rowwise_bwd(fn, rows, params, cots, grid, name):
    nr, npar, nc = len(rows), len(params), len(cots)
    drows = [r for r in rows if r.diff]
    nz = len(grid)

    def body(*refs):
        ids = tuple(pl.program_id(a) for a in range(nz))
        row_refs, par_refs = refs[:nr], refs[nr:nr + npar]
        cot_refs = refs[nr + npar:nr + npar + nc]
        drow_refs = refs[nr + npar + nc:nr + npar + nc + len(drows)]
        dpar_refs = refs[nr + npar + nc + len(drows):]
        pieces, is_diff = [], []
        for r, ref in zip(rows, row_refs):
            ps = r.pieces(ref)
            pieces += ps
            is_diff += [r.diff] * len(ps)
        pvals = [ref[...].astype(F32) for ref in par_refs]
        dvals = [p for p, dflag in zip(pieces, is_diff) if dflag]
        nd = len(dvals)

        def f(*args):
            it = iter(args[:nd])
            full = [next(it) if dflag else p for p, dflag in zip(pieces, is_diff)]
            return tuple(fn(ids, *full, *args[nd:]))

        _, vjp = jax.vjp(f, *dvals, *pvals)
        cvals = []
        for c, ref in zip(cots, cot_refs):
            cvals += c.pieces(ref)
        g = vjp(tuple(cvals))
        o = 0
        first_inner = ids[-1] == 0
        for r, ref in zip(drows, drow_refs):
            n = r.n_pieces()
            gs = g[o:o + n]
            o += n
            if r.acc:
                @pl.when(first_inner)
                def _(ref=ref):
                    ref[...] = jnp.zeros_like(ref)
            _store_pieces(ref, r.splits, r.lead, gs, accumulate=r.acc)
        first = functools.reduce(jnp.logical_and, [i == 0 for i in ids])
        for ref, gp in zip(dpar_refs, g[nd:]):
            @pl.when(first)
            def _(ref=ref):
                ref[...] = jnp.zeros_like(ref)
            ref[...] += gp

    pspecs = [pl.BlockSpec(p.shape, (lambda *ids, _n=p.ndim: (0,) * _n)) for p in params]
    res = pl.pallas_call(
        body, name=name, grid=grid,
        in_specs=[r.spec() for r in rows] + pspecs + [c.spec() for c in cots],
        out_specs=[r.gspec() for r in drows] + pspecs,
        out_shape=[SDS(r.gshape, r.gdtype) for r in drows] + [SDS(p.shape, F32) for p in params],
        compiler_params=_cp(nz),
    )(*[r.arr for r in rows], *params, *[c.arr for c in cots])
    res = list(res)
    return res[:len(drows)], res[len(drows):]


def _sigmoid(x):
    return 0.5 * (jnp.tanh(0.5 * x) + 1.0)


def _silu(x):
    return x * _sigmoid(x)


def _normmod(x, gain, sc, sh):
    inv = lax.rsqrt(jnp.mean(x * x, axis=-1, keepdims=True) + EPS)
    return x * inv * gain * (1.0 + sc) + sh


def f_first(ids, x, gain, sc, sh):
    return x, _normmod(x, gain, sc, sh)


def f_resid_norm(ids, x, y, g, gain, sc, sh):
    xn = x + g * y
    return xn, _normmod(xn, gain, sc, sh)


def f_swiglu(ids, gate, up):
    return (_silu(gate) * up,)


def f_loss(ids, x, y, tgt, g):
    out = x + g * y
    e = out - tgt
    part = 0.5 * jnp.sum(e * e, axis=0, keepdims=True) * (1.0 / D)
    return (part,)


def _softplus(x):
    return jnp.maximum(x, 0.0) + jnp.log(1.0 + jnp.exp(-jnp.abs(x)))


def _chunk_tril(T):
    r = lax.broadcasted_iota(jnp.int32, (T, T), 0)
    c = lax.broadcasted_iota(jnp.int32, (T, T), 1)
    return jnp.where((r // GDN_C == c // GDN_C) & (c <= r), 1.0, 0.0).astype(F32)


def _dot_hi(a, b, dims=((1,), (0,))):
    return lax.dot_general(a, b, (dims, ((), ())), precision=lax.Precision.HIGHEST, preferred_element_type=F32)


def _dot_x3(a, b, dims=((1,), (0,))):
    return lax.dot_general(a, b, (dims, ((), ())), precision=lax.Precision.HIGH, preferred_element_type=F32)


def f_gdn_gates(ids, ab, alog, dtb):
    T = ab.shape[0]
    g = -jnp.exp(alog) * _softplus(ab + dtb)
    beta = _sigmoid(ab)
    gcum = _dot_x3(_chunk_tril(T), g)
    row = lax.broadcasted_iota(jnp.int32, (LANES, LANES), 0)
    sel = lambda k: jnp.where(row == k, 1.0, 0.0).astype(F32)
    gcs = [_dot_x3(gcum, sel(h)) for h in range(GDN_H)]
    bts = [_dot_x3(beta, sel(GDN_H + h)) for h in range(GDN_H)]
    return (*gcs, *bts)


def f_gdn_post(ids, *args):
    os_, zs, gain = args[:GDN_H], args[GDN_H:2 * GDN_H], args[2 * GDN_H]
    out = []
    for o, z in zip(os_, zs):
        inv = lax.rsqrt(jnp.mean(o * o, axis=-1, keepdims=True) + EPS)
        out.append(o * inv * gain * _silu(z))
    return tuple(out)


def _qknorm1(x, gain2, scale):
    lane = lax.broadcasted_iota(jnp.int32, x.shape, 1)
    lo = lane < DSW_DH
    x2 = x * x
    s_all = jnp.sum(x2, axis=-1, keepdims=True)
    s_lo = jnp.sum(jnp.where(lo, x2, 0.0), axis=-1, keepdims=True)
    ms = jnp.where(lo, s_lo, s_all - s_lo) * (1.0 / DSW_DH)
    return x * lax.rsqrt(ms + EPS) * (gain2 * scale)


def f_qknorm(ids, *args):
    return tuple(_qknorm1(x, args[-1], 1.0) for x in args[:-1])


def f_qnorm(ids, *args):
    return tuple(_qknorm1(x, args[-1], DSW_DH ** -0.5) for x in args[:-1])


def f_combine(ids, o0, o1, o2, l0, l1, l2):
    m = jnp.maximum(jnp.maximum(l0, l1), l2)
    e0, e1, e2 = jnp.exp(l0 - m), jnp.exp(l1 - m), jnp.exp(l2 - m)
    den = e0 + e1 + e2
    o = (e0 * o0 + e1 * o1 + e2 * o2) / den
    return o, m + jnp.log(den)


GDN_T = 512
HALO = 8


def _conv_pre(xx, w):
    acc = xx * w[3:4, :]
    for j in range(3):
        acc = acc + pltpu.roll(xx, shift=3 - j, axis=0) * w[j:j + 1, :]
    return acc


def _qkv_act(pre, cidx):
    s = _silu(pre)
    r = lax.rsqrt(jnp.sum(s * s, axis=-1, keepdims=True) + EPS)
    scale = jnp.where(cidx < GDN_H, GDN_DK ** -0.5, 1.0).astype(F32)
    return jnp.where(cidx < 2 * GDN_H, s * r * scale, s)


def gdn_pre(proj, conv_w, S):
    nt = S // GDN_T
    hb = GDN_T // HALO

    def body(prev_ref, cur_ref, w_ref, o_ref):
        p, i = pl.program_id(0), pl.program_id(1)
        for h in range(GDN_H):
            cols = slice(LANES * h, LANES * (h + 1))
            prev = jnp.where(i > 0, prev_ref[:, cols], 0.0)
            xx = jnp.concatenate([prev, cur_ref[:, cols]], axis=0)
            pre = _conv_pre(xx, w_ref[:, cols])[HALO:]
            o_ref[h] = _qkv_act(pre, p * GDN_H + h)

    hv = GDN_H * LANES
    return pl.pallas_call(
        body, name="gdn_pre", grid=(3, nt),
        in_specs=[pl.BlockSpec((HALO, hv), lambda p, i: (jnp.maximum(i * hb - 1, 0), p)),
                  pl.BlockSpec((GDN_T, hv), lambda p, i: (i, p)),
                  pl.BlockSpec((4, hv), lambda p, i: (0, p))],
        out_specs=pl.BlockSpec((None, GDN_H, GDN_T, LANES), lambda p, i: (p, 0, i, 0)),
        out_shape=SDS((3, GDN_H, S, LANES), F32),
        compiler_params=_cp(2),
    )(proj, proj, conv_w)


def gdn_pre_bwd(proj, conv_w, dqkv, S):
    nt = S // GDN_T
    hb = GDN_T // HALO
    last_h = S // HALO - 1

    def body(prev_ref, cur_ref, next_ref, w_ref, d_ref, dnext_ref, dx_ref, dw_ref):
        p, i = pl.program_id(0), pl.program_id(1)

        @pl.when(i == 0)
        def _():
            dw_ref[...] = jnp.zeros_like(dw_ref)

        for h in range(GDN_H):
            cols = slice(LANES * h, LANES * (h + 1))
            w = w_ref[:, cols]
            prev = jnp.where(i > 0, prev_ref[:, cols], 0.0)
            xx = jnp.concatenate([prev, cur_ref[:, cols], next_ref[:, cols]], axis=0)
            dnext = jnp.where(i < nt - 1, dnext_ref[h], 0.0)
            dd = jnp.concatenate([jnp.zeros((HALO, LANES), F32), d_ref[h], dnext], axis=0)
            pre = _conv_pre(xx, w)
            _, vjp = jax.vjp(lambda v, _c=p * GDN_H + h: _qkv_act(v, _c), pre)
            (dpre,) = vjp(dd)
            row = lax.broadcasted_iota(jnp.int32, dpre.shape, 0)
            dpre = jnp.where(row >= HALO, dpre, 0.0)
            dx = dpre * w[3:4, :]
            R = dpre.shape[0]
            for j in range(3):
                dx = dx + pltpu.roll(dpre, shift=R - (3 - j), axis=0) * w[j:j + 1, :]
            dx_ref[:, cols] = dx[HALO:HALO + GDN_T].astype(dx_ref.dtype)
            own = jnp.where(row < HALO + GDN_T, dpre, 0.0)
            rows_w = [jnp.sum(own * pltpu.roll(xx, shift=3 - j, axis=0), axis=0, keepdims=True) for j in range(3)]
            rows_w.append(jnp.sum(own * xx, axis=0, keepdims=True))
            r4 = lax.broadcasted_iota(jnp.int32, (4, LANES), 0)
            dw = jnp.zeros((4, LANES), F32)
            for j in range(4):
                dw = dw + jnp.where(r4 == j, rows_w[j], 0.0)
            dw_ref[:, cols] += dw

    hv = GDN_H * LANES
    return pl.pallas_call(
        body, name="gdn_pre_bwd", grid=(3, nt),
        in_specs=[pl.BlockSpec((HALO, hv), lambda p, i: (jnp.maximum(i * hb - 1, 0), p)),
                  pl.BlockSpec((GDN_T, hv), lambda p, i: (i, p)),
                  pl.BlockSpec((HALO, hv), lambda p, i: (jnp.minimum((i + 1) * hb, last_h), p)),
                  pl.BlockSpec((4, hv), lambda p, i: (0, p)),
                  pl.BlockSpec((None, GDN_H, GDN_T, LANES), lambda p, i: (p, 0, i, 0)),
                  pl.BlockSpec((None, GDN_H, HALO, LANES), lambda p, i: (p, 0, jnp.minimum((i + 1) * hb, last_h), 0))],
        out_specs=[pl.BlockSpec((GDN_T, hv), lambda p, i: (i, p)),
                   pl.BlockSpec((4, hv), lambda p, i: (0, p))],
        out_shape=[SDS((S, 3 * hv), BF16), SDS((4, 3 * hv), F32)],
        compiler_params=_cp(2),
    )(proj, proj, proj, conv_w, dqkv, dqkv)


_DIMS = {"nn": ((1,), (0,)), "nt": ((1,), (1,)), "tn": ((0,), (0,))}


def _mm_raw(a, b, mode, hi):
    if hi:
        return _dot_hi(a, b, _DIMS[mode])
    return lax.dot_general(a.astype(BF16), b.astype(BF16), (_DIMS[mode], ((), ())), preferred_element_type=F32)


@functools.partial(jax.custom_vjp, nondiff_argnums=(2, 3))
def mm(a, b, mode, hi):
    return _mm_raw(a, b, mode, hi)


def _mm_fwd(a, b, mode, hi):
    return _mm_raw(a, b, mode, hi), (a, b)


def _mm_bwd(mode, hi, res, dc):
    a, b = res
    if mode == "nn":
        da, db = mm(dc, b, "nt", hi), mm(a, dc, "tn", hi)
    elif mode == "nt":
        da, db = mm(dc, b, "nn", hi), mm(dc, a, "tn", hi)
    else:
        da, db = mm(b, dc, "nt", hi), mm(a, dc, "nn", hi)
    return da, db


mm.defvjp(_mm_fwd, _mm_bwd)


TRI_BASE = 8


def _unit_lower_inverses(Ls):
    n = Ls[0].shape[0]
    r = lax.broadcasted_iota(jnp.int32, (n, n), 0)
    c = lax.broadcasted_iota(jnp.int32, (n, n), 1)
    eye = jnp.where(r == c, 1.0, 0.0).astype(F32)
    base = r // TRI_BASE == c // TRI_BASE
    Ps = [jnp.where(base, -L, 0.0) for L in Ls]
    invs = [eye + P for P in Ps]
    k = 1
    while 2 * k < TRI_BASE:
        Ps = [_dot_x3(P, P) for P in Ps]
        invs = [inv + _dot_x3(inv, P) for inv, P in zip(invs, Ps)]
        k *= 2
    b = 2 * TRI_BASE
    while b <= n:
        off_mask = (r // b == c // b) & ((r % b) >= b // 2) & ((c % b) < b // 2)
        ts = [_dot_x3(inv, jnp.where(off_mask, L, 0.0)) for inv, L in zip(invs, Ls)]
        invs = [inv - _dot_x3(t, inv) for inv, t in zip(invs, ts)]
        b *= 2
    return invs


@jax.custom_vjp
def tri_apply(invs, Ls, r1s, r2s):
    return [_dot_x3(i, r) for i, r in zip(invs, r1s)], [_dot_x3(i, r) for i, r in zip(invs, r2s)]


def _tri_fwd(invs, Ls, r1s, r2s):
    s1s = [_dot_x3(i, r) for i, r in zip(invs, r1s)]
    s2s = [_dot_x3(i, r) for i, r in zip(invs, r2s)]
    return (s1s, s2s), (invs, s1s, s2s)


def _tri_bwd(res, ds):
    invs, s1s, s2s = res
    d1s = [_dot_x3(i, d, _DIMS["tn"]) for i, d in zip(invs, ds[0])]
    d2s = [_dot_x3(i, d, _DIMS["tn"]) for i, d in zip(invs, ds[1])]
    dLs = [-(_dot_x3(d1, s1, _DIMS["nt"]) + _dot_x3(d2, s2, _DIMS["nt"])) for d1, s1, d2, s2 in zip(d1s, s1s, d2s, s2s)]
    return [jnp.zeros_like(i) for i in invs], dLs, d1s, d2s


tri_apply.defvjp(_tri_fwd, _tri_bwd)


def _gdn_chunk(qs, ks, vs, gcbs, btbs, Ss, invs=None):
    C = qs[0].shape[0]
    r = lax.broadcasted_iota(jnp.int32, (C, C), 0)
    c = lax.broadcasted_iota(jnp.int32, (C, C), 1)
    causal, strict = c <= r, c < r
    rows = lax.broadcasted_iota(jnp.int32, gcbs[0].shape, 0)
    Gs = [g[:, :C] for g in gcbs]
    decays = [jnp.exp(jnp.where(causal, G - G.T, NEG)) for G in Gs]
    kbs = [k * b for k, b in zip(ks, btbs)]
    vbs = [v * b for v, b in zip(vs, btbs)]
    Ls = [jnp.where(strict, mm(kb, k, "nt", False) * d, 0.0) for kb, k, d in zip(kbs, ks, decays)]
    egs = [jnp.exp(g) for g in gcbs]
    if invs is None:
        invs = _unit_lower_inverses(Ls)
    us, ws = tri_apply(invs, Ls, vbs, [kb * eg for kb, eg in zip(kbs, egs)])
    qks = [jnp.where(causal, mm(q, k, "nt", False) * d, 0.0) for q, k, d in zip(qs, ks, decays)]
    g_lasts = [jnp.sum(jnp.where(rows == C - 1, g, 0.0), axis=0, keepdims=True) for g in gcbs]
    q_decs = [q * eg for q, eg in zip(qs, egs)]
    k_decs = [k * jnp.exp(gl - g) for k, gl, g in zip(ks, g_lasts, gcbs)]
    v_news = [u - mm(w, S, "nn", False) for u, w, S in zip(us, ws, Ss)]
    os_ = [mm(qd, S, "nn", False) + mm(qk, vn, "nn", False) for qd, S, qk, vn in zip(q_decs, Ss, qks, v_news)]
    S_news = [S * jnp.exp(gl) + mm(kd, vn, "tn", False) for S, gl, kd, vn in zip(Ss, g_lasts, k_decs, v_news)]
    return os_, S_news, invs


def gdn_core(qkv, gc, bt, S):
    nchunk = S // GDN_C

    def body(qkv_ref, g_ref, b_ref, o_ref, st_ref, inv_ref, s_scr):
        n = pl.program_id(0)

        @pl.when(n == 0)
        def _():
            s_scr[...] = jnp.zeros_like(s_scr)

        heads = range(GDN_H)
        S_in = [s_scr[h] for h in heads]
        os_, S_new, invs = _gdn_chunk([qkv_ref[0, h] for h in heads], [qkv_ref[1, h] for h in heads], [qkv_ref[2, h] for h in heads],
                                      [g_ref[h] for h in heads], [b_ref[h] for h in heads], S_in)
        for h in heads:
            st_ref[h] = S_in[h]
            inv_ref[h] = invs[h]
            o_ref[h] = os_[h]
            s_scr[h] = S_new[h]

    blk3 = pl.BlockSpec((3, GDN_H, GDN_C, LANES), lambda n: (0, 0, n, 0))
    hb = pl.BlockSpec((GDN_H, GDN_C, LANES), lambda n: (0, n, 0))
    return pl.pallas_call(
        body, name="gdn_core", grid=(nchunk,),
        in_specs=[blk3, hb, hb],
        out_specs=[hb, pl.BlockSpec((GDN_H, None, GDN_DK, LANES), lambda n: (0, n, 0, 0)),
                   pl.BlockSpec((GDN_H, None, GDN_C, GDN_C), lambda n: (0, n, 0, 0))],
        out_shape=[SDS((GDN_H, S, LANES), F32), SDS((GDN_H, nchunk, GDN_DK, LANES), F32), SDS((GDN_H, nchunk, GDN_C, GDN_C), F32)],
        scratch_shapes=[pltpu.VMEM((GDN_H, GDN_DK, LANES), F32)],
        compiler_params=_cp(1),
    )(qkv, gc, bt)


def gdn_core_bwd(qkv, gc, bt, states, invs, do, S):
    nchunk = S // GDN_C

    def body(qkv_ref, g_ref, b_ref, st_ref, inv_ref, do_ref, dqkv_ref, dg_ref, db_ref, ds_scr):
        n = pl.program_id(0)

        @pl.when(n == 0)
        def _():
            ds_scr[...] = jnp.zeros_like(ds_scr)

        heads = range(GDN_H)
        saved = [inv_ref[h] for h in heads]
        _, vjp = jax.vjp(lambda *a: _gdn_chunk(*a, invs=saved)[:2],
                         [qkv_ref[0, h] for h in heads], [qkv_ref[1, h] for h in heads], [qkv_ref[2, h] for h in heads],
                         [g_ref[h] for h in heads], [b_ref[h] for h in heads], [st_ref[h] for h in heads])
        dq, dk, dv, dg, db, dS = vjp(([do_ref[h] for h in heads], [ds_scr[h] for h in heads]))
        for h in heads:
            dqkv_ref[0, h] = dq[h]
            dqkv_ref[1, h] = dk[h]
            dqkv_ref[2, h] = dv[h]
            dg_ref[h] = dg[h]
            db_ref[h] = db[h]
            ds_scr[h] = dS[h]

    rev = lambda n: nchunk - 1 - n
    blk3 = pl.BlockSpec((3, GDN_H, GDN_C, LANES), lambda n: (0, 0, rev(n), 0))
    hb = pl.BlockSpec((GDN_H, GDN_C, LANES), lambda n: (0, rev(n), 0))
    return pl.pallas_call(
        body, name="gdn_core_bwd", grid=(nchunk,),
        in_specs=[blk3, hb, hb, pl.BlockSpec((GDN_H, None, GDN_DK, LANES), lambda n: (0, rev(n), 0, 0)),
                  pl.BlockSpec((GDN_H, None, GDN_C, GDN_C), lambda n: (0, rev(n), 0, 0)), hb],
        out_specs=[blk3, hb, hb],
        out_shape=[SDS((3, GDN_H, S, LANES), F32), SDS((GDN_H, S, LANES), F32), SDS((GDN_H, S, LANES), F32)],
        scratch_shapes=[pltpu.VMEM((GDN_H, GDN_DK, LANES), F32)],
        compiler_params=_cp(1),
    )(qkv, gc, bt, states, invs, do)


GDN_MAIN = 4 * GDN_H * LANES
GDN_PROJ = GDN_MAIN + LANES
RT = 256


def gdn_forward(h, w_in, conv_w, alog, dtb, out_gain, w_out):
    S = h.shape[0]
    nt = S // RT
    proj = matmul(h, w_in, "nn", F32, "gdn_in")
    qkv = gdn_pre(proj, conv_w, S)
    ab_row = Row(proj, (RT, LANES), lambda i: (i, GDN_MAIN // LANES), gdtype=BF16, gshape=(S, LANES), gimap=lambda i: (i, 0))
    hm = lambda i: (0, i, 0)
    hv = GDN_H * LANES
    gc, bt = rowwise(f_gdn_gates, [ab_row], [alog, dtb],
                     [Out((GDN_H, S, LANES), F32, (GDN_H, RT, LANES), hm, lead=GDN_H)] * 2, (nt,), "gdn_gates")
    o, states, invs = gdn_core(qkv, gc, bt, S)
    o_row = Row(o, (GDN_H, RT, LANES), hm, lead=GDN_H)
    z_row = Row(proj, (RT, hv), lambda i: (i, 3), splits=[LANES] * GDN_H, gdtype=BF16, gshape=(S, hv), gimap=lambda i: (i, 0))
    (on,) = rowwise(f_gdn_post, [o_row, z_row], [out_gain],
                    [Out((S, hv), BF16, (RT, hv), lambda i: (i, 0), splits=[LANES] * GDN_H)], (nt,), "gdn_post")
    y = matmul(on, w_out, "nn", F32, "gdn_out")
    saved = dict(h=h, proj=proj, qkv=qkv, gc=gc, bt=bt, states=states, invs=invs, o=o, on=on, ab_row=ab_row, o_row=o_row, z_row=z_row)
    return y, saved


def gdn_backward(dy, sv, w_in, conv_w, alog, dtb, out_gain, w_out):
    S = dy.shape[0]
    nt = S // RT
    hm = lambda i: (0, i, 0)
    hv = GDN_H * LANES
    don = matmul(dy, w_out, "nt", F32, "gdn_out_dx")
    d_w_out = matmul(sv["on"], dy, "tn", F32, "gdn_out_dw")
    (do, dz), (d_gain,) = rowwise_bwd(f_gdn_post, [sv["o_row"], sv["z_row"]], [out_gain],
                                      [Row(don, (RT, hv), lambda i: (i, 0), splits=[LANES] * GDN_H)], (nt,), "gdn_post_bwd")
    dqkv, dgc, dbt = gdn_core_bwd(sv["qkv"], sv["gc"], sv["bt"], sv["states"], sv["invs"], do, S)
    head_blk = lambda a: Row(a, (GDN_H, RT, LANES), hm, lead=GDN_H)
    (dab,), (d_alog, d_dtb) = rowwise_bwd(f_gdn_gates, [sv["ab_row"]], [alog, dtb], [head_blk(dgc), head_blk(dbt)],
                                          (nt,), "gdn_gates_bwd")
    dqkv_proj, d_conv = gdn_pre_bwd(sv["proj"], conv_w, dqkv, S)
    dproj = jnp.concatenate([dqkv_proj, dz, dab], axis=1)
    d_w_in = matmul(sv["h"], dproj, "tn", F32, "gdn_in_dw")
    dh = matmul(dproj, w_in, "nt", F32, "gdn_in_dx")
    return dh, dict(w_in=d_w_in, conv=d_conv, alog=d_alog, dtb=d_dtb, gain=d_gain, w_out=d_w_out)


QB = DSW_SPAN
N_HP = DSW_HG // LANES
PROJ_BLKS = 3 * 3 * N_HP


def _bucket_maps():
    a = np.arange(QB)[:, None]
    j = np.arange(2 * QB)[None, :]
    dist = QB + a - j
    band = (dist >= 0) & (dist <= DSW_SPAN)
    maps = []
    for _, dil in DSW_GROUPS:
        dd = np.maximum(dist, 0) * dil
        max_exact = REL_BUCKETS // 2
        scaled = np.log(np.maximum(dd, 1).astype(np.float32) / np.float32(max_exact)) / np.float32(math.log(REL_MAX_DIST / max_exact))
        large = max_exact + (scaled * np.float32(REL_BUCKETS - max_exact)).astype(np.int32)
        large = np.minimum(large, REL_BUCKETS - 1)
        maps.append(np.where(dd < max_exact, dd, large).astype(np.int32))
    return np.stack(maps), band


def dsw_bias(rel_bias):
    maps, band = _bucket_maps()
    maps = np.where(band[None], maps, -1).astype(np.int32)

    def body(tab_ref, bk_ref, o_ref):
        gh = pl.program_id(0)
        bk = bk_ref[...]
        acc = jnp.full(bk.shape, NEG, F32)
        for b in range(REL_BUCKETS):
            acc = jnp.where(bk == b, tab_ref[b, gh], acc)
        o_ref[...] = acc

    return pl.pallas_call(
        body, name="dsw_bias", grid=(3 * GDN_H,),
        in_specs=[pl.BlockSpec(memory_space=pltpu.SMEM),
                  pl.BlockSpec((None, QB, 2 * QB), lambda gh: (gh // GDN_H, 0, 0))],
        out_specs=pl.BlockSpec((None, QB, 2 * QB), lambda gh: (gh, 0, 0)),
        out_shape=SDS((3 * GDN_H, QB, 2 * QB), F32),
        compiler_params=_cp(1),
    )(rel_bias, jnp.asarray(maps))


def dsw_bias_grad(dbias):
    maps, band = _bucket_maps()
    maps = np.where(band[None], maps, -1).astype(np.int32)

    def body(d_ref, bk_ref, o_ref):
        bk = bk_ref[...]
        d = d_ref[...]
        rows = lax.broadcasted_iota(jnp.int32, (REL_BUCKETS, LANES), 0)
        acc = jnp.zeros((REL_BUCKETS, LANES), F32)
        for b in range(REL_BUCKETS):
            part = jnp.sum(jnp.where(bk == b, d, 0.0), axis=0, keepdims=True)
            val = jnp.sum(part, axis=1, keepdims=True)
            acc = jnp.where(rows == b, val, acc)
        o_ref[...] = acc

    return pl.pallas_call(
        body, name="dsw_bias_grad", grid=(3 * GDN_H,),
        in_specs=[pl.BlockSpec((None, QB, 2 * QB), lambda gh: (gh, 0, 0)),
                  pl.BlockSpec((None, QB, 2 * QB), lambda gh: (gh // GDN_H, 0, 0))],
        out_specs=pl.BlockSpec((None, REL_BUCKETS, LANES), lambda gh: (gh, 0, 0)),
        out_shape=SDS((3 * GDN_H, REL_BUCKETS, LANES), F32),
        compiler_params=_cp(1),
    )(dbias, jnp.asarray(maps))


def _nt(a, b):
    return lax.dot_general(a, b, (((1,), (1,)), ((), ())), preferred_element_type=F32)


def _tn(a, b):
    return lax.dot_general(a, b, (((0,), (0,)), ((), ())), preferred_element_type=F32)


def dsw_group_fwd(qn, kn, proj, bias, gi, S):
    dil = DSW_GROUPS[gi][1]
    sd = S // dil
    nq = sd // QB
    qv = qn.reshape(sd, dil * 3 * DSW_HG)
    kv = kn.reshape(sd, dil * 3 * DSW_HG)
    pv = proj.reshape(sd, dil * 9 * DSW_HG)
    qk_col = lambda hp, r: r * (3 * N_HP) + gi * N_HP + hp
    v_col = lambda hp, r: r * PROJ_BLKS + 2 * 3 * N_HP + gi * N_HP + hp

    def body(q_ref, kp_ref, kc_ref, vp_ref, vc_ref, b_ref, o_ref, l_ref):
        i = pl.program_id(2)
        q = q_ref[...]
        k2 = jnp.concatenate([kp_ref[...], kc_ref[...]], axis=0)
        v2 = jnp.concatenate([vp_ref[...], vc_ref[...]], axis=0).astype(BF16)
        lane_q = lax.broadcasted_iota(jnp.int32, (QB, LANES), 1) < DSW_DH
        lane_k = lax.broadcasted_iota(jnp.int32, (2 * QB, LANES), 1) < DSW_DH
        col = lax.broadcasted_iota(jnp.int32, (QB, 2 * QB), 1)
        first = jnp.logical_and(i == 0, col < QB)
        o_acc = jnp.zeros((QB, LANES), F32)
        lse_b = jnp.zeros((QB, LANES), F32)
        for hh in range(2):
            mq = lane_q if hh == 0 else jnp.logical_not(lane_q)
            mk = lane_k if hh == 0 else jnp.logical_not(lane_k)
            s = _nt(jnp.where(mq, q, 0).astype(BF16), k2) + b_ref[hh]
            s = jnp.where(first, NEG, s)
            mx = jnp.max(s, axis=1, keepdims=True)
            p = jnp.exp(s - mx)
            l = jnp.sum(p, axis=1, keepdims=True)
            oh = jnp.dot(p.astype(BF16), jnp.where(mk, v2, 0).astype(BF16), preferred_element_type=F32) / l
            o_acc = o_acc + oh
            lse_b = jnp.where(mq, mx + jnp.log(l), lse_b)
        o_ref[...] = o_acc
        l_ref[...] = lse_b

    blk = (QB, LANES)
    out_spec = pl.BlockSpec(blk, lambda hp, r, i: (i, r * N_HP + hp))
    o, lse = pl.pallas_call(
        body, name=f"dsw_fwd_g{gi}", grid=(N_HP, dil, nq),
        in_specs=[pl.BlockSpec(blk, lambda hp, r, i: (i, qk_col(hp, r))),
                  pl.BlockSpec(blk, lambda hp, r, i: (jnp.maximum(i - 1, 0), qk_col(hp, r))),
                  pl.BlockSpec(blk, lambda hp, r, i: (i, qk_col(hp, r))),
                  pl.BlockSpec(blk, lambda hp, r, i: (jnp.maximum(i - 1, 0), v_col(hp, r))),
                  pl.BlockSpec(blk, lambda hp, r, i: (i, v_col(hp, r))),
                  pl.BlockSpec((2, QB, 2 * QB), lambda hp, r, i: (gi * N_HP + hp, 0, 0))],
        out_specs=[out_spec, out_spec],
        out_shape=[SDS((sd, dil * DSW_HG), F32)] * 2,
        compiler_params=_cp(3),
    )(qv, kv, kv, pv, pv, bias)
    return o.reshape(S, DSW_HG), lse.reshape(S, DSW_HG)


def dsw_group_bwd(qn, kn, proj, bias, do, o, lse, gi, S):
    dil = DSW_GROUPS[gi][1]
    sd = S // dil
    nq = sd // QB
    qv = qn.reshape(sd, dil * 3 * DSW_HG)
    kv = kn.reshape(sd, dil * 3 * DSW_HG)
    pv = proj.reshape(sd, dil * 9 * DSW_HG)
    dov = do.reshape(sd, dil * DSW_HG)
    ov = o.reshape(sd, dil * DSW_HG)
    lv = lse.reshape(sd, dil * DSW_HG)
    qk_col = lambda hp, r: r * (3 * N_HP) + gi * N_HP + hp
    v_col = lambda hp, r: r * PROJ_BLKS + 2 * 3 * N_HP + gi * N_HP + hp
    o_col = lambda hp, r: r * N_HP + hp
    cur = lambda i: jnp.minimum(i, nq - 1)
    prev = lambda i: jnp.maximum(jnp.minimum(i, nq - 1) - 1, 0)
    done = lambda i: jnp.maximum(i - 1, 0)

    def body(q_ref, kp_ref, kc_ref, vp_ref, vc_ref, b_ref, do_ref, o_ref, l_ref,
             dq_ref, dk_ref, dv_ref, db_ref, dk_scr, dv_scr):
        r, i = pl.program_id(1), pl.program_id(2)

        @pl.when(jnp.logical_and(r == 0, i == 0))
        def _():
            db_ref[...] = jnp.zeros_like(db_ref)

        @pl.when(i == 0)
        def _():
            dk_scr[...] = jnp.zeros_like(dk_scr)
            dv_scr[...] = jnp.zeros_like(dv_scr)

        @pl.when(i < nq)
        def _():
            q = q_ref[...]
            k2 = jnp.concatenate([kp_ref[...], kc_ref[...]], axis=0)
            v2 = jnp.concatenate([vp_ref[...], vc_ref[...]], axis=0).astype(BF16)
            dout = do_ref[...].astype(F32)
            prod = dout * o_ref[...].astype(F32)
            lse_b = l_ref[...]
            lane_q = lax.broadcasted_iota(jnp.int32, (QB, LANES), 1) < DSW_DH
            col = lax.broadcasted_iota(jnp.int32, (QB, 2 * QB), 1)
            first = jnp.logical_and(i == 0, col < QB)
            dq = jnp.zeros((QB, LANES), F32)
            dk2 = jnp.zeros((2 * QB, LANES), F32)
            dv2 = jnp.zeros((2 * QB, LANES), F32)
            for hh in range(2):
                mq = lane_q if hh == 0 else jnp.logical_not(lane_q)
                qm = jnp.where(mq, q, 0).astype(BF16)
                dom = jnp.where(mq, dout, 0.0).astype(BF16)
                s = _nt(qm, k2) + b_ref[hh]
                s = jnp.where(first, NEG, s)
                lse_h = jnp.max(jnp.where(mq, lse_b, NEG), axis=1, keepdims=True)
                p = jnp.exp(s - lse_h)
                delta = jnp.sum(jnp.where(mq, prod, 0.0), axis=1, keepdims=True)
                dp = _nt(dom, v2)
                ds = p * (dp - delta)
                dsb = ds.astype(BF16)
                dq = dq + jnp.where(mq, jnp.dot(dsb, k2, preferred_element_type=F32), 0.0)
                dk2 = dk2 + _tn(dsb, qm)
                dv2 = dv2 + _tn(p.astype(BF16), dom)
                db_ref[hh] += ds
            dq_ref[...] = dq
            dk_ref[...] = dk_scr[...] + dk2[:QB]
            dv_ref[...] = (dv_scr[...] + dv2[:QB]).astype(dv_ref.dtype)
            dk_scr[...] = dk2[QB:]
            dv_scr[...] = dv2[QB:]

        @pl.when(i == nq)
        def _():
            dk_ref[...] = dk_scr[...]
            dv_ref[...] = dv_scr[...].astype(dv_ref.dtype)

    blk = (QB, LANES)
    dq, dk, dv, dbias = pl.pallas_call(
        body, name=f"dsw_bwd_g{gi}", grid=(N_HP, dil, nq + 1),
        in_specs=[pl.BlockSpec(blk, lambda hp, r, i: (cur(i), qk_col(hp, r))),
                  pl.BlockSpec(blk, lambda hp, r, i: (prev(i), qk_col(hp, r))),
                  pl.BlockSpec(blk, lambda hp, r, i: (cur(i), qk_col(hp, r))),
                  pl.BlockSpec(blk, lambda hp, r, i: (prev(i), v_col(hp, r))),
                  pl.BlockSpec(blk, lambda hp, r, i: (cur(i), v_col(hp, r))),
                  pl.BlockSpec((2, QB, 2 * QB), lambda hp, r, i: (gi * N_HP + hp, 0, 0)),
                  pl.BlockSpec(blk, lambda hp, r, i: (cur(i), o_col(hp, r))),
                  pl.BlockSpec(blk, lambda hp, r, i: (cur(i), o_col(hp, r))),
                  pl.BlockSpec(blk, lambda hp, r, i: (cur(i), o_col(hp, r)))],
        out_specs=[pl.BlockSpec(blk, lambda hp, r, i: (cur(i), o_col(hp, r))),
                   pl.BlockSpec(blk, lambda hp, r, i: (done(i), o_col(hp, r))),
                   pl.BlockSpec(blk, lambda hp, r, i: (done(i), o_col(hp, r))),
                   pl.BlockSpec((2, QB, 2 * QB), lambda hp, r, i: (hp, 0, 0))],
        out_shape=[SDS((sd, dil * DSW_HG), F32), SDS((sd, dil * DSW_HG), F32), SDS((sd, dil * DSW_HG), BF16),
                   SDS((GDN_H, QB, 2 * QB), F32)],
        scratch_shapes=[pltpu.VMEM(blk, F32), pltpu.VMEM(blk, F32)],
        compiler_params=_cp(3),
    )(qv, kv, kv, pv, pv, bias, dov, ov, lv)
    return dq.reshape(S, DSW_HG), dk.reshape(S, DSW_HG), dv.reshape(S, DSW_HG), dbias


def dsw_forward(h, w_in, q_gain2, k_gain2, rel_bias, w_out):
    S = h.shape[0]
    nt = S // RT
    nb = 3 * N_HP
    proj = matmul(h, w_in, "nn", F32, "dsw_in")
    width = nb * LANES
    lanes12 = [LANES] * nb
    (qn,) = rowwise(f_qnorm, [Row(proj, (RT, width), lambda i: (i, 0), splits=lanes12)], [q_gain2],
                    [Out((S, width), BF16, (RT, width), lambda i: (i, 0), splits=lanes12)], (nt,), "dsw_qnorm")
    (kn,) = rowwise(f_qknorm, [Row(proj, (RT, width), lambda i: (i, 1), splits=lanes12)], [k_gain2],
                    [Out((S, width), BF16, (RT, width), lambda i: (i, 0), splits=lanes12)], (nt,), "dsw_knorm")
    bias = dsw_bias(rel_bias)
    os_, ls_ = [], []
    for gi in range(3):
        o, l = dsw_group_fwd(qn, kn, proj, bias, gi, S)
        os_.append(o)
        ls_.append(l)
    full = lambda a: Row(a, (RT, DSW_HG), lambda i: (i, 0))
    o, lse = rowwise(f_combine, [full(a) for a in os_ + ls_], [],
                     [Out((S, DSW_HG), BF16, (RT, DSW_HG), lambda i: (i, 0)), Out((S, DSW_HG), F32, (RT, DSW_HG), lambda i: (i, 0))],
                     (nt,), "dsw_combine")
    y = matmul(o, w_out, "nn", F32, "dsw_out")
    return y, dict(h=h, proj=proj, qn=qn, kn=kn, bias=bias, o=o, lse=lse)


def dsw_backward(dy, sv, w_in, q_gain2, k_gain2, w_out):
    S = dy.shape[0]
    nt = S // RT
    nb = 3 * N_HP
    do = matmul(dy, w_out, "nt", BF16, "dsw_out_dx")
    d_w_out = matmul(sv["o"], dy, "tn", F32, "dsw_out_dw")
    pieces_q, pieces_k, pieces_v, dbs = [], [], [], []
    d_qg = jnp.zeros((1, LANES), F32)
    d_kg = jnp.zeros((1, LANES), F32)
    for gi in range(3):
        dq, dk, dv, db = dsw_group_bwd(sv["qn"], sv["kn"], sv["proj"], sv["bias"], do, sv["o"], sv["lse"], gi, S)
        dbs.append(db)
        pieces_v.append(dv)
        for which, dd in ((0, dq), (1, dk)):
            lanes4 = [LANES] * N_HP
            row = Row(sv["proj"], (RT, DSW_HG), lambda i, _o=which * 3 + gi: (i, _o), splits=lanes4,
                      gdtype=BF16, gshape=(S, DSW_HG), gimap=lambda i: (i, 0))
            fn, gain = (f_qnorm, q_gain2) if which == 0 else (f_qknorm, k_gain2)
            (dx,), (dg,) = rowwise_bwd(fn, [row], [gain], [Row(dd, (RT, DSW_HG), lambda i: (i, 0), splits=lanes4)],
                                       (nt,), f"dsw_norm_bwd_{which}{gi}")
            if which == 0:
                pieces_q.append(dx)
                d_qg = d_qg + dg
            else:
                pieces_k.append(dx)
                d_kg = d_kg + dg
    dproj = jnp.concatenate(pieces_q + pieces_k + pieces_v, axis=1)
    d_w_in = matmul(sv["h"], dproj, "tn", F32, "dsw_in_dw")
    dh = matmul(dproj, w_in, "nt", F32, "dsw_in_dx")
    d_rel = dsw_bias_grad(jnp.concatenate(dbs, axis=0))
    return dh, dict(w_in=d_w_in, q_gain2=d_qg, k_gain2=d_kg, rel=d_rel, w_out=d_w_out)


N_LB = DSW_HG // LANES
HALF = DSW_DH // 2


def _lanes(j):
    return slice(LANES * j, LANES * (j + 1))


def _deinterleave(stage, out_ref, dil, rows, dtype):
    for r in range(dil):
        for j in range(N_LB):
            out_ref[r, :, _lanes(j)] = stage[j, pl.ds(r, rows, stride=dil), :].astype(dtype)


def _interleave(in_ref, stage, dil, rows):
    for r in range(dil):
        for j in range(N_LB):
            stage[j, pl.ds(r, rows, stride=dil), :] = in_ref[r, :, _lanes(j)].astype(F32)


def dsw_prep(proj, q_gain2, k_gain2, gi, S):
    dil = DSW_GROUPS[gi][1]
    nt, rows = S // RT, RT // dil

    def body(q_ref, k_ref, v_ref, qg_ref, kg_ref, qo_ref, ko_ref, vo_ref, stage):
        for src, gain_ref, scale, dst in ((q_ref, qg_ref, DSW_DH ** -0.5, qo_ref), (k_ref, kg_ref, 1.0, ko_ref), (v_ref, None, None, vo_ref)):
            for j in range(N_LB):
                val = src[:, _lanes(j)].astype(F32)
                stage[j] = val if gain_ref is None else _qknorm1(val, gain_ref[...], scale)
            _deinterleave(stage, dst, dil, rows, BF16)

    col = lambda which: pl.BlockSpec((RT, DSW_HG), lambda i, _c=which * 3 + gi: (i, _c))
    gspec = pl.BlockSpec((1, LANES), lambda i: (0, 0))
    ospec = pl.BlockSpec((dil, rows, DSW_HG), lambda i: (0, i, 0))
    return pl.pallas_call(
        body, name=f"dsw_prep_g{gi}", grid=(nt,),
        in_specs=[col(0), col(1), col(2), gspec, gspec], out_specs=[ospec] * 3,
        out_shape=[SDS((dil, S // dil, DSW_HG), BF16)] * 3,
        scratch_shapes=[pltpu.VMEM((N_LB, RT, LANES), F32)], compiler_params=_cp(1),
    )(proj, proj, proj, q_gain2, k_gain2)


def dsw_prep_bwd(proj, q_gain2, k_gain2, dqd, dkd, dvd, gi, S):
    dil = DSW_GROUPS[gi][1]
    nt, rows = S // RT, RT // dil

    def body(q_ref, k_ref, qg_ref, kg_ref, dq_ref, dk_ref, dv_ref, oq_ref, ok_ref, ov_ref, dqg_ref, dkg_ref, stage):
        i = pl.program_id(0)

        @pl.when(i == 0)
        def _():
            dqg_ref[...] = jnp.zeros_like(dqg_ref)
            dkg_ref[...] = jnp.zeros_like(dkg_ref)

        for src, gain_ref, scale, cot_ref, dst, dg_ref in ((q_ref, qg_ref, DSW_DH ** -0.5, dq_ref, oq_ref, dqg_ref),
                                                          (k_ref, kg_ref, 1.0, dk_ref, ok_ref, dkg_ref)):
            _interleave(cot_ref, stage, dil, rows)
            for j in range(N_LB):
                _, vjp = jax.vjp(lambda x, g, _s=scale: _qknorm1(x, g, _s), src[:, _lanes(j)].astype(F32), gain_ref[...])
                dx, dg = vjp(stage[j])
                dst[:, _lanes(j)] = dx.astype(dst.dtype)
                dg_ref[...] += dg
        _interleave(dv_ref, stage, dil, rows)
        for j in range(N_LB):
            ov_ref[:, _lanes(j)] = stage[j].astype(ov_ref.dtype)

    col = lambda which: pl.BlockSpec((RT, DSW_HG), lambda i, _c=which * 3 + gi: (i, _c))
    gspec = pl.BlockSpec((1, LANES), lambda i: (0, 0))
    dspec = pl.BlockSpec((dil, rows, DSW_HG), lambda i: (0, i, 0))
    nspec = pl.BlockSpec((RT, DSW_HG), lambda i: (i, 0))
    return pl.pallas_call(
        body, name=f"dsw_prep_bwd_g{gi}", grid=(nt,),
        in_specs=[col(0), col(1), gspec, gspec, dspec, dspec, dspec], out_specs=[nspec] * 3 + [gspec] * 2,
        out_shape=[SDS((S, DSW_HG), BF16)] * 3 + [SDS((1, LANES), F32)] * 2,
        scratch_shapes=[pltpu.VMEM((N_LB, RT, LANES), F32)], compiler_params=_cp(1),
    )(proj, proj, q_gain2, k_gain2, dqd, dkd, dvd)


def _head_masks(rows):
    lane = lax.broadcasted_iota(jnp.int32, (rows, LANES), 1)
    return lane < DSW_DH, (lane % DSW_DH) < HALF


def dsw_attn_fwd(qd, kd, vd, bias, gi, S):
    dil = DSW_GROUPS[gi][1]
    sd = S // dil
    nq = sd // QB

    def body(q_ref, k_ref, v_ref, b_ref, o_ref, l_ref, kp_scr, vp_scr):
        i = pl.program_id(1)

        @pl.when(i == 0)
        def _():
            kp_scr[...] = jnp.zeros_like(kp_scr)
            vp_scr[...] = jnp.zeros_like(vp_scr)

        lo_q, _ = _head_masks(QB)
        lo_k, _ = _head_masks(2 * QB)
        col = lax.broadcasted_iota(jnp.int32, (QB, 2 * QB), 1)
        first = jnp.logical_and(i == 0, col < QB)
        hps, heads = range(N_HP), range(2 * N_HP)
        k2s = [jnp.concatenate([kp_scr[:, _lanes(hp)], k_ref[:, _lanes(hp)]], axis=0) for hp in hps]
        v2s = [jnp.concatenate([vp_scr[:, _lanes(hp)], v_ref[:, _lanes(hp)]], axis=0) for hp in hps]
        qs = [q_ref[:, _lanes(hp)] for hp in hps]
        k_now, v_now = k_ref[...], v_ref[...]
        mqs = [lo_q if h % 2 == 0 else jnp.logical_not(lo_q) for h in heads]
        mks = [lo_k if h % 2 == 0 else jnp.logical_not(lo_k) for h in heads]
        ss = [jnp.where(first, NEG, _nt(jnp.where(mqs[h], qs[h // 2], 0).astype(BF16), k2s[h // 2]) + b_ref[h]) for h in heads]
        mxs = [jnp.max(s, axis=1, keepdims=True) for s in ss]
        ps = [jnp.exp(s - mx) for s, mx in zip(ss, mxs)]
        ls = [jnp.sum(p, axis=1, keepdims=True) for p in ps]
        ohs = [jnp.dot(ps[h].astype(BF16), jnp.where(mks[h], v2s[h // 2], 0).astype(BF16), preferred_element_type=F32) / ls[h] for h in heads]
        lse_h = [mx + jnp.log(l) for mx, l in zip(mxs, ls)]
        for hp in hps:
            o_ref[:, _lanes(hp)] = ohs[2 * hp] + ohs[2 * hp + 1]
            l_ref[:, _lanes(hp)] = jnp.where(lo_q, lse_h[2 * hp], lse_h[2 * hp + 1])
        kp_scr[...] = k_now
        vp_scr[...] = v_now

    blk = pl.BlockSpec((None, QB, DSW_HG), lambda r, i: (r, i, 0))
    return pl.pallas_call(
        body, name=f"dsw_attn_g{gi}", grid=(dil, nq),
        in_specs=[blk, blk, blk, pl.BlockSpec((GDN_H, QB, 2 * QB), lambda r, i: (gi, 0, 0))],
        out_specs=[blk, blk], out_shape=[SDS((dil, sd, DSW_HG), F32)] * 2,
        scratch_shapes=[pltpu.VMEM((QB, DSW_HG), BF16)] * 2, compiler_params=_cp(2),
    )(qd, kd, vd, bias)


def dsw_attn_bwd(qd, kd, vd, bias, dod, statd, gi, S):
    dil = DSW_GROUPS[gi][1]
    sd = S // dil
    nq = sd // QB
    cur = lambda i: jnp.minimum(i, nq - 1)
    done = lambda i: jnp.maximum(i - 1, 0)

    def body(q_ref, k_ref, v_ref, b_ref, do_ref, st_ref, dq_ref, dk_ref, dv_ref, db_ref, kp_scr, vp_scr, dk_scr, dv_scr):
        r, i = pl.program_id(0), pl.program_id(1)

        @pl.when(jnp.logical_and(r == 0, i == 0))
        def _():
            db_ref[...] = jnp.zeros_like(db_ref)

        @pl.when(i == 0)
        def _():
            for scr in (kp_scr, vp_scr, dk_scr, dv_scr):
                scr[...] = jnp.zeros_like(scr)

        @pl.when(i < nq)
        def _():
            lo_q, first_half = _head_masks(QB)
            col = lax.broadcasted_iota(jnp.int32, (QB, 2 * QB), 1)
            first = jnp.logical_and(i == 0, col < QB)
            hps, heads = range(N_HP), range(2 * N_HP)
            k2s = [jnp.concatenate([kp_scr[:, _lanes(hp)], k_ref[:, _lanes(hp)]], axis=0) for hp in hps]
            v2s = [jnp.concatenate([vp_scr[:, _lanes(hp)], v_ref[:, _lanes(hp)]], axis=0) for hp in hps]
            qs = [q_ref[:, _lanes(hp)] for hp in hps]
            douts = [do_ref[:, _lanes(hp)] for hp in hps]
            stats = [st_ref[:, _lanes(hp)] for hp in hps]
            dkc = [dk_scr[:, _lanes(hp)] for hp in hps]
            dvc = [dv_scr[:, _lanes(hp)] for hp in hps]
            k_now, v_now = k_ref[...], v_ref[...]
            mqs = [lo_q if h % 2 == 0 else jnp.logical_not(lo_q) for h in heads]
            qms = [jnp.where(mqs[h], qs[h // 2], 0).astype(BF16) for h in heads]
            doms = [jnp.where(mqs[h], douts[h // 2], 0).astype(BF16) for h in heads]
            ss = [jnp.where(first, NEG, _nt(qms[h], k2s[h // 2]) + b_ref[h]) for h in heads]
            lses = [jnp.max(jnp.where(jnp.logical_and(mqs[h], first_half), stats[h // 2], NEG), axis=1, keepdims=True) for h in heads]
            deltas = [jnp.max(jnp.where(jnp.logical_and(mqs[h], jnp.logical_not(first_half)), stats[h // 2], NEG), axis=1, keepdims=True)
                      for h in heads]
            ps = [jnp.exp(ss[h] - lses[h]) for h in heads]
            dss = [ps[h] * (_nt(doms[h], v2s[h // 2]) - deltas[h]) for h in heads]
            dsbs = [d.astype(BF16) for d in dss]
            dqh = [jnp.where(mqs[h], jnp.dot(dsbs[h], k2s[h // 2], preferred_element_type=F32), 0.0) for h in heads]
            dkh = [_tn(dsbs[h], qms[h]) for h in heads]
            dvh = [_tn(ps[h].astype(BF16), doms[h]) for h in heads]
            for h in heads:
                db_ref[h] += dss[h]
            for hp in hps:
                dk2 = dkh[2 * hp] + dkh[2 * hp + 1]
                dv2 = dvh[2 * hp] + dvh[2 * hp + 1]
                dq_ref[:, _lanes(hp)] = dqh[2 * hp] + dqh[2 * hp + 1]
                dk_ref[:, _lanes(hp)] = dkc[hp] + dk2[:QB]
                dv_ref[:, _lanes(hp)] = (dvc[hp] + dv2[:QB]).astype(dv_ref.dtype)
                dk_scr[:, _lanes(hp)] = dk2[QB:]
                dv_scr[:, _lanes(hp)] = dv2[QB:]
            kp_scr[...] = k_now
            vp_scr[...] = v_now

        @pl.when(i == nq)
        def _():
            dk_ref[...] = dk_scr[...]
            dv_ref[...] = dv_scr[...].astype(dv_ref.dtype)

    blk = pl.BlockSpec((None, QB, DSW_HG), lambda r, i: (r, cur(i), 0))
    oblk = pl.BlockSpec((None, QB, DSW_HG), lambda r, i: (r, done(i), 0))
    return pl.pallas_call(
        body, name=f"dsw_attn_bwd_g{gi}", grid=(dil, nq + 1),
        in_specs=[blk, blk, blk, pl.BlockSpec((GDN_H, QB, 2 * QB), lambda r, i: (gi, 0, 0)), blk, blk],
        out_specs=[blk, oblk, oblk, pl.BlockSpec((GDN_H, QB, 2 * QB), lambda r, i: (0, 0, 0))],
        out_shape=[SDS((dil, sd, DSW_HG), F32), SDS((dil, sd, DSW_HG), F32), SDS((dil, sd, DSW_HG), BF16),
                   SDS((GDN_H, QB, 2 * QB), F32)],
        scratch_shapes=[pltpu.VMEM((QB, DSW_HG), BF16)] * 2 + [pltpu.VMEM((QB, DSW_HG), F32)] * 2,
        compiler_params=_cp(2),
    )(qd, kd, vd, bias, dod, statd)


def dsw_combine(ods, lseds, S):
    nt = S // RT
    dils = [d for _, d in DSW_GROUPS]

    def body(*refs):
        ins, (o_ref, l_ref), stages = refs[:6], refs[6:8], refs[8:]
        for g in range(3):
            _interleave(ins[g], stages[g], dils[g], RT // dils[g])
            _interleave(ins[3 + g], stages[3 + g], dils[g], RT // dils[g])
        for j in range(N_LB):
            o, lse = f_combine(None, *[st[j] for st in stages])
            o_ref[:, _lanes(j)] = o.astype(o_ref.dtype)
            l_ref[:, _lanes(j)] = lse

    dspec = lambda d: pl.BlockSpec((d, RT // d, DSW_HG), lambda i: (0, i, 0))
    nspec = pl.BlockSpec((RT, DSW_HG), lambda i: (i, 0))
    return pl.pallas_call(
        body, name="dsw_combine", grid=(nt,),
        in_specs=[dspec(d) for d in dils] * 2, out_specs=[nspec, nspec],
        out_shape=[SDS((S, DSW_HG), BF16), SDS((S, DSW_HG), F32)],
        scratch_shapes=[pltpu.VMEM((N_LB, RT, LANES), F32)] * 6, compiler_params=_cp(1),
    )(*ods, *lseds)


def dsw_bwd_prep(do, o, lse, S):
    nt = S // RT
    dils = [d for _, d in DSW_GROUPS]

    def body(do_ref, o_ref, l_ref, *rest):
        outs, (st_do, st_stat) = rest[:6], rest[6:]
        lo, first_half = _head_masks(RT)
        for j in range(N_LB):
            dout = do_ref[:, _lanes(j)]
            prod = dout * o_ref[:, _lanes(j)].astype(F32)
            s_all = jnp.sum(prod, axis=1, keepdims=True)
            s_lo = jnp.sum(jnp.where(lo, prod, 0.0), axis=1, keepdims=True)
            delta = jnp.where(lo, s_lo, s_all - s_lo)
            st_do[j] = dout
            st_stat[j] = jnp.where(first_half, l_ref[:, _lanes(j)], delta)
        for g in range(3):
            _deinterleave(st_do, outs[g], dils[g], RT // dils[g], BF16)
            _deinterleave(st_stat, outs[3 + g], dils[g], RT // dils[g], F32)

    nspec = pl.BlockSpec((RT, DSW_HG), lambda i: (i, 0))
    dspec = lambda d: pl.BlockSpec((d, RT // d, DSW_HG), lambda i: (0, i, 0))
    res = pl.pallas_call(
        body, name="dsw_bwd_prep", grid=(nt,),
        in_specs=[nspec] * 3, out_specs=[dspec(d) for d in dils] * 2,
        out_shape=[SDS((d, S // d, DSW_HG), BF16) for d in dils] + [SDS((d, S // d, DSW_HG), F32) for d in dils],
        scratch_shapes=[pltpu.VMEM((N_LB, RT, LANES), F32)] * 2, compiler_params=_cp(1),
    )(do, o, lse)
    return res[:3], res[3:]


def dsw_forward(h, w_in, q_gain2, k_gain2, rel_bias, w_out):
    S = h.shape[0]
    proj = matmul(h, w_in, "nn", BF16, "dsw_in", col_shards=N_SHARD)
    bias = dsw_bias(rel_bias)
    qkv, ods, lseds = [], [], []
    for gi in range(3):
        qd, kd, vd = dsw_prep(proj, q_gain2, k_gain2, gi, S)
        od, ld = dsw_attn_fwd(qd, kd, vd, bias, gi, S)
        qkv.append((qd, kd, vd))
        ods.append(od)
        lseds.append(ld)
    o, lse = dsw_combine(ods, lseds, S)
    y = matmul(o, w_out, "nn", F32, "dsw_out", col_shards=N_SHARD)
    return y, dict(h=h, proj=proj, qkv=qkv, bias=bias, o=o, lse=lse)


def dsw_backward(dy, sv, w_in, q_gain2, k_gain2, w_out):
    S = dy.shape[0]
    do = matmul(dy, w_out, "nt", F32, "dsw_out_dx", col_shards=N_SHARD)
    d_w_out = matmul(sv["o"], dy, "tn", F32, "dsw_out_dw", col_shards=N_SHARD)
    dods, statds = dsw_bwd_prep(do, sv["o"], sv["lse"], S)
    pieces_q, pieces_k, pieces_v, dbs = [], [], [], []
    d_qg = jnp.zeros((1, LANES), F32)
    d_kg = jnp.zeros((1, LANES), F32)
    for gi in range(3):
        qd, kd, vd = sv["qkv"][gi]
        dqd, dkd, dvd, db = dsw_attn_bwd(qd, kd, vd, sv["bias"], dods[gi], statds[gi], gi, S)
        dq, dk, dv, dqg, dkg = dsw_prep_bwd(sv["proj"], q_gain2, k_gain2, dqd, dkd, dvd, gi, S)
        dbs.append(db)
        pieces_q.append(dq)
        pieces_k.append(dk)
        pieces_v.append(dv)
        d_qg = d_qg + dqg
        d_kg = d_kg + dkg
    dproj = jnp.concatenate(pieces_q + pieces_k + pieces_v, axis=1)
    d_w_in = matmul(sv["h"], dproj, "tn", F32, "dsw_in_dw", col_shards=N_SHARD)
    dh = matmul(dproj, w_in, "nt", F32, "dsw_in_dx", col_shards=N_SHARD)
    d_rel = dsw_bias_grad(jnp.concatenate(dbs, axis=0))
    return dh, dict(w_in=d_w_in, q_gain2=d_qg, k_gain2=d_kg, rel=d_rel, w_out=d_w_out)


FT = 128


def ffn_forward(h, w_in, w_out, tag):
    S = h.shape[0]
    gu = matmul(h, w_in, "nn", BF16, f"ffn_in_{tag}", col_shards=N_SHARD)
    gu_row = Row(gu, (FT, 2 * FFN), lambda i: (i, 0), splits=[FFN, FFN], gdtype=BF16)
    (a,) = rowwise(f_swiglu, [gu_row], [], [Out((S, FFN), BF16, (FT, FFN), lambda i: (i, 0))], (S // FT,), f"ffn_act_{tag}")
    f = matmul(a, w_out, "nn", F32, f"ffn_out_{tag}")
    return f, dict(h=h, gu_row=gu_row, a=a)


def ffn_backward(df, sv, w_in, w_out, tag):
    S = df.shape[0]
    da = matmul(df, w_out, "nt", BF16, f"ffn_out_dx_{tag}")
    d_w_out = matmul(sv["a"], df, "tn", F32, f"ffn_out_dw_{tag}")
    (dgu,), _ = rowwise_bwd(f_swiglu, [sv["gu_row"]], [], [Row(da, (FT, FFN), lambda i: (i, 0))], (S // FT,), f"ffn_act_bwd_{tag}")
    d_w_in = matmul(sv["h"], dgu, "tn", F32, f"ffn_in_dw_{tag}", col_shards=N_SHARD)
    dh = matmul(dgu, w_in, "nt", F32, f"ffn_in_dx_{tag}", col_shards=N_SHARD)
    return dh, d_w_in, d_w_out


def f_norm_only(ids, x, gain, sc, sh):
    return (_normmod(x, gain, sc, sh),)


def _wide(a, **kw):
    return Row(a, (RT, D), lambda i: (i, 0), **kw)


def _wide_out(S, dtype):
    return Out((S, D), dtype, (RT, D), lambda i: (i, 0))


def adamw(w, g, m, v, name):
    shape = w.shape
    C = shape[-1]
    R = int(np.prod(shape[:-1]))
    w2, g2, m2, v2 = (a.reshape(R, C) for a in (w, g, m, v))
    br = R
    if R > 256:
        br = max(b for b in range(8, 257, 8) if R % b == 0)
    c1 = 1.0 / (1.0 - ADAM_B1 ** ADAM_STEP)
    c2 = 1.0 / (1.0 - ADAM_B2 ** ADAM_STEP)

    def body(w_ref, g_ref, m_ref, v_ref, d_ref, nm_ref, nv_ref):
        gg = g_ref[...]
        mm_ = ADAM_B1 * m_ref[...] + (1.0 - ADAM_B1) * gg
        vv = ADAM_B2 * v_ref[...] + (1.0 - ADAM_B2) * (gg * gg)
        d_ref[...] = -ADAM_LR * ((mm_ * c1) / (jnp.sqrt(vv * c2) + ADAM_EPS) + ADAM_WD * w_ref[...])
        nm_ref[...] = mm_
        nv_ref[...] = vv

    spec = pl.BlockSpec((br, C), lambda i: (i, 0))
    d, nm, nv = pl.pallas_call(
        body, name=name, grid=(R // br,), in_specs=[spec] * 4, out_specs=[spec] * 3,
        out_shape=[SDS((R, C), F32)] * 3, compiler_params=_cp(1),
    )(w2, g2, m2, v2)
    return d.reshape(shape), nm.reshape(shape), nv.reshape(shape)


def _place():
    x, y, c = lax.axis_index("x"), lax.axis_index("y"), lax.axis_index("c")
    chips = [(1 - x, y), (x, 1 - y), (1 - x, 1 - y)]
    return x, y, c, chips


def all_gather_small(blk, name):
    m_per, n = blk.shape

    def body(x_ref, out_ref, send_sems, recv_sems, local_sem):
        x, y, c, chips = _place()
        me, sibling = (x, y, c), (x, y, 1 - c)

        def rows(px, py, pc):
            return out_ref.at[pl.ds((4 * px + 2 * py + pc) * m_per, m_per), :]

        def copy(k, block, to, src=None):
            return pltpu.make_async_remote_copy(
                src_ref=rows(*block) if src is None else src, dst_ref=rows(*block),
                send_sem=send_sems.at[k], recv_sem=recv_sems.at[k], device_id=to, device_id_type=MESH)

        mine = pltpu.make_async_copy(x_ref, rows(*me), local_sem)
        mine.start()
        first = [copy(0, me, sibling, src=x_ref)]
        first += [copy(1 + j, me, (*chip, c), src=x_ref) for j, chip in enumerate(chips)]
        for cp in first:
            cp.start()
        passed = [copy(4 + j, (*chip, c), sibling) for j, chip in enumerate(chips)]
        for j, chip in enumerate(chips):
            copy(1 + j, (*chip, c), me).wait_recv()
            passed[j].start()
        copy(0, sibling, me).wait_recv()
        for j, chip in enumerate(chips):
            copy(4 + j, (*chip, 1 - c), me).wait_recv()
        for cp in first + passed:
            cp.wait_send()
        mine.wait()

    return pl.pallas_call(
        body, name=name, out_shape=SDS((N_DEV * m_per, n), blk.dtype),
        in_specs=[pl.BlockSpec(memory_space=pltpu.VMEM)], out_specs=pl.BlockSpec(memory_space=pltpu.VMEM),
        scratch_shapes=[pltpu.SemaphoreType.DMA((7,)), pltpu.SemaphoreType.DMA((7,)), pltpu.SemaphoreType.DMA],
    )(blk)


def _half(cc, rh):
    return pl.ds(pl.multiple_of(cc * rh, 16), rh)


def all_gather_shards(ws):
    n = len(ws)

    def body(*refs):
        w_refs, out_refs = refs[:n], refs[n:2 * n]
        send_sems, recv_sems, local_sems, own_sems = refs[2 * n:]
        x, y, c, chips = _place()
        sibling = (x, y, 1 - c)
        s_me = 2 * x + y

        def copy(k, src, dst, to):
            return pltpu.make_async_remote_copy(src_ref=src, dst_ref=dst, send_sem=send_sems.at[k], recv_sem=recv_sems.at[k],
                                                device_id=to, device_id_type=MESH)

        local, sends, passed = [], [], []
        for k in range(n):
            rh = ws[k].shape[0] // 2
            cp = pltpu.make_async_remote_copy(src_ref=w_refs[k], dst_ref=out_refs[k].at[s_me], send_sem=local_sems.at[k],
                                              recv_sem=own_sems.at[k], device_id=sibling, device_id_type=MESH)
            cp.start()
            local.append(cp)
            for j, chip in enumerate(chips):
                sd = copy(6 * k + j, w_refs[k].at[_half(c, rh)], out_refs[k].at[s_me, _half(c, rh)], (*chip, c))
                sd.start()
                sends.append(sd)
        for k in range(n):
            rh = ws[k].shape[0] // 2
            for j, (px, py) in enumerate(chips):
                got = out_refs[k].at[2 * px + py, _half(c, rh)]
                copy(6 * k + j, got, got, (px, py, c)).wait_recv()
                fw = copy(6 * k + 3 + j, got, got, sibling)
                fw.start()
                passed.append(fw)
        for k in range(n):
            rh = ws[k].shape[0] // 2
            for j, (px, py) in enumerate(chips):
                got = out_refs[k].at[2 * px + py, _half(1 - c, rh)]
                copy(6 * k + 3 + j, got, got, sibling).wait_recv()
        for cp in sends + passed:
            cp.wait_send()
        for cp in local:
            cp.wait()

    return pl.pallas_call(
        body, name="weights_all_gather", out_shape=[SDS((N_SHARD,) + w.shape, w.dtype) for w in ws],
        in_specs=[ANY] * n, out_specs=[ANY] * n,
        scratch_shapes=[pltpu.SemaphoreType.DMA((6 * n,)), pltpu.SemaphoreType.DMA((6 * n,)), pltpu.SemaphoreType.DMA((n,)),
                        pltpu.SemaphoreType.DMA((n,))],
    )(*ws)


def sibling_exchange(sends, name):
    n = len(sends)

    def body(*refs):
        s_refs, o_refs, send_sems, recv_sems = refs[:n], refs[n:2 * n], refs[2 * n], refs[2 * n + 1]
        x, y, c, _ = _place()
        cps = [pltpu.make_async_remote_copy(src_ref=s_refs[k], dst_ref=o_refs[k], send_sem=send_sems.at[k], recv_sem=recv_sems.at[k],
                                            device_id=(x, y, 1 - c), device_id_type=MESH) for k in range(n)]
        for cp in cps:
            cp.start()
        for cp in cps:
            cp.wait()

    return pl.pallas_call(
        body, name=name, out_shape=[SDS(s.shape, s.dtype) for s in sends], in_specs=[ANY] * n, out_specs=[ANY] * n,
        scratch_shapes=[pltpu.SemaphoreType.DMA((n,)), pltpu.SemaphoreType.DMA((n,))],
    )(*sends)


def scatter_to_chips(parts):
    n = len(parts)

    def body(*refs):
        p_refs, o_refs, send_sems, recv_sems = refs[:n], refs[n:2 * n], refs[2 * n], refs[2 * n + 1]
        x, y, c, chips = _place()
        cps = []
        for k in range(n):
            for j, (px, py) in enumerate(chips):
                cp = pltpu.make_async_remote_copy(src_ref=p_refs[k].at[2 * px + py], dst_ref=o_refs[k].at[j],
                                                  send_sem=send_sems.at[3 * k + j], recv_sem=recv_sems.at[3 * k + j],
                                                  device_id=(px, py, c), device_id_type=MESH)
                cp.start()
                cps.append(cp)
        for cp in cps:
            cp.wait()

    return pl.pallas_call(
        body, name="grads_scatter", out_shape=[SDS((3,) + p.shape[1:], p.dtype) for p in parts], in_specs=[ANY] * n, out_specs=[ANY] * n,
        scratch_shapes=[pltpu.SemaphoreType.DMA((3 * n,)), pltpu.SemaphoreType.DMA((3 * n,))],
    )(*parts)


def merge_halves(halves):
    n = len(halves)

    def body(*refs):
        h_refs, o_refs = refs[:n], refs[n:2 * n]
        send_sems, recv_sems, local_sems = refs[2 * n:]
        x, y, c, _ = _place()
        local, cps = [], []
        for k in range(n):
            rh = halves[k].shape[0]
            lc = pltpu.make_async_copy(h_refs[k], o_refs[k].at[_half(c, rh)], local_sems.at[k])
            lc.start()
            local.append(lc)
            cp = pltpu.make_async_remote_copy(src_ref=h_refs[k], dst_ref=o_refs[k].at[_half(c, rh)], send_sem=send_sems.at[k],
                                              recv_sem=recv_sems.at[k], device_id=(x, y, 1 - c), device_id_type=MESH)
            cp.start()
            cps.append(cp)
        for k in range(n):
            rh = halves[k].shape[0]
            got = o_refs[k].at[_half(1 - c, rh)]
            pltpu.make_async_remote_copy(src_ref=got, dst_ref=got, send_sem=send_sems.at[k], recv_sem=recv_sems.at[k],
                                         device_id=(x, y, 1 - c), device_id_type=MESH).wait_recv()
        for cp in cps:
            cp.wait_send()
        for lc in local:
            lc.wait()

    return pl.pallas_call(
        body, name="grads_merge_halves", out_shape=[SDS((2 * h.shape[0], h.shape[1]), h.dtype) for h in halves],
        in_specs=[ANY] * n, out_specs=[ANY] * n,
        scratch_shapes=[pltpu.SemaphoreType.DMA((n,)), pltpu.SemaphoreType.DMA((n,)), pltpu.SemaphoreType.DMA((n,))],
    )(*halves)


def add_rows(arrs, out_dtype, name, rt=256):
    Rr, W = arrs[0].shape

    def fn(ids, *vals):
        acc = vals[0]
        for v in vals[1:]:
            acc = acc + v
        return (acc,)

    t = rt if Rr % rt == 0 else max(b for b in range(16, rt + 1, 16) if Rr % b == 0)
    (out,) = rowwise(fn, [Row(a, (t, W), lambda i: (i, 0)) for a in arrs], [],
                     [Out((Rr, W), out_dtype, (t, W), lambda i: (i, 0))], (Rr // t,), name)
    return out


HBM_SPEC = pl.BlockSpec(memory_space=pltpu.HBM)
SEM_SPEC = pl.BlockSpec(memory_space=pltpu.SEMAPHORE)
DATAFLOW = pltpu.SideEffectType.DATAFLOW_SIDE_EFFECTING


def _in_hbm(a):
    return pltpu.with_memory_space_constraint(a, pltpu.HBM)


def _gather_copies(w_refs, land_refs, send_sems, recv_sems):
    x, y, c, chips = _place()
    targets = [(x, y, 1 - c)] + [(*chip, c) for chip in chips]
    cps = []
    for k, (w_ref, land_ref) in enumerate(zip(w_refs, land_refs)):
        for j, to in enumerate(targets):
            cps.append(pltpu.make_async_remote_copy(src_ref=w_ref, dst_ref=land_ref.at[2 * x + y], send_sem=send_sems.at[4 * k + j],
                                                    recv_sem=recv_sems.at[4 * k + j], device_id=to, device_id_type=MESH))
    return cps


def _scatter_copies(p_refs, land_refs, send_sems, recv_sems):
    x, y, c, chips = _place()
    cps = []
    for k, (p_ref, land_ref) in enumerate(zip(p_refs, land_refs)):
        for j, (px, py) in enumerate(chips):
            cps.append(pltpu.make_async_remote_copy(src_ref=p_ref.at[2 * px + py], dst_ref=land_ref.at[j], send_sem=send_sems.at[3 * k + j],
                                                    recv_sem=recv_sems.at[3 * k + j], device_id=(px, py, c), device_id_type=MESH))
    return cps


def copies_start(srcs, land_shapes, make_copies, per_src, name):
    n = len(srcs)
    m = per_src * n

    def body(*refs):
        src_refs, land_refs = refs[:n], refs[n:2 * n]
        send_sems, recv_sems, token = refs[2 * n], refs[2 * n + 1], refs[-1]
        for cp in make_copies(src_refs, land_refs, send_sems, recv_sems):
            cp.start()
        token[...] = jnp.zeros_like(token)

    lands = [lax.empty(shp, s.dtype) for shp, s in zip(land_shapes, srcs)]
    res = pl.pallas_call(
        body, name=name,
        out_shape=(pltpu.SemaphoreType.DMA((m,)), pltpu.SemaphoreType.DMA((m,)), *[pltpu.HBM(s.shape, s.dtype) for s in srcs],
                   *[pltpu.HBM(shp, s.dtype) for shp, s in zip(land_shapes, srcs)], SDS((8, LANES), F32)),
        in_specs=[HBM_SPEC] * (2 * n),
        out_specs=(SEM_SPEC, SEM_SPEC, *[HBM_SPEC] * (2 * n), pl.BlockSpec(memory_space=pltpu.VMEM)),
        input_output_aliases={i: 2 + i for i in range(2 * n)},
        compiler_params=pltpu.CompilerParams(has_side_effects=DATAFLOW),
    )(*[_in_hbm(s) for s in srcs], *[_in_hbm(l) for l in lands])
    return res[0], res[1], list(res[2:2 + n]), list(res[2 + n:2 + 2 * n]), res[-1]


def copies_wait(send_sems, recv_sems, srcs, lands, make_copies, after, name):
    n = len(srcs)

    def body(*refs):
        src_refs, land_refs = refs[:n], refs[n:2 * n]
        for cp in make_copies(src_refs, land_refs, refs[2 * n], refs[2 * n + 1]):
            cp.wait_send()
            cp.wait_recv()

    res = pl.pallas_call(
        body, name=name,
        out_shape=(*[pltpu.HBM(s.shape, s.dtype) for s in srcs], *[pltpu.HBM(l.shape, l.dtype) for l in lands]),
        in_specs=[HBM_SPEC] * (2 * n) + [SEM_SPEC, SEM_SPEC, ANY],
        out_specs=tuple([HBM_SPEC] * (2 * n)),
        input_output_aliases={i: i for i in range(2 * n)},
        compiler_params=pltpu.CompilerParams(has_side_effects=DATAFLOW),
    )(*srcs, *lands, send_sems, recv_sems, after)
    return list(res[n:])


PACK = (("gdn_w_in", 2), ("gdn_w_out", 1), ("w_ffn_in", 2), ("w_ffn_out", 1), ("dsw_w_in", 2), ("dsw_w_out", 2))
PACK_ALIGN = 32


def _pack_rows(sizes):
    total = sum(sizes)
    rows = -(-total // D)
    return -(-rows // PACK_ALIGN) * PACK_ALIGN


def pack_blocks(blocks, dtype):
    flat = [b.astype(dtype).reshape(-1) for b in blocks]
    total = sum(f.shape[0] for f in flat)
    R = _pack_rows([f.shape[0] for f in flat])
    flat.append(jnp.zeros((R * D - total,), dtype))
    return jnp.concatenate(flat).reshape(R, D)


def unpack_blocks(buf, shapes):
    flat = buf.reshape(-1)
    out, off = [], 0
    for shp in shapes:
        n = int(np.prod(shp))
        out.append(flat[off:off + n].reshape(shp))
        off += n
    return out


def _shard_slice(a, axis, s):
    n = a.shape[axis] // N_SHARD
    return lax.slice_in_dim(a, s * n, (s + 1) * n, axis=axis)


def _pad_lanes(v):
    return jnp.concatenate([v.astype(F32), jnp.zeros((LANES - v.shape[0],), F32)])[None]


def kernel(x, c, w_ada, b_ada, norm_mix, norm_ffn, w_ffn_in, w_ffn_out, gdn_w_in, gdn_conv, gdn_a_log, gdn_dt_bias, gdn_out_norm, gdn_w_out, dsw_w_in, dsw_q_norm, dsw_k_norm, dsw_w_out, rel_bias, loss_target, m_w_ada, m_b_ada, m_norm_mix, m_norm_ffn, m_w_ffn_in, m_w_ffn_out, m_gdn_w_in, m_gdn_conv, m_gdn_a_log, m_gdn_dt_bias, m_gdn_out_norm, m_gdn_w_out, m_dsw_w_in, m_dsw_q_norm, m_dsw_k_norm, m_dsw_w_out, m_rel_bias, v_w_ada, v_b_ada, v_norm_mix, v_norm_ffn, v_w_ffn_in, v_w_ffn_out, v_gdn_w_in, v_gdn_conv, v_gdn_a_log, v_gdn_dt_bias, v_gdn_out_norm, v_gdn_w_out, v_dsw_w_in, v_dsw_q_norm, v_dsw_k_norm, v_dsw_w_out, v_rel_bias):
    S = x.shape[1]
    nt = S // RT
    xi, yi, ci = lax.axis_index("x"), lax.axis_index("y"), lax.axis_index("c")
    me = 4 * xi + 2 * yi + ci
    s_me = 2 * xi + yi
    x0, tgt = x[0], loss_target[0]
    shard = dict(w_ffn_in=w_ffn_in, w_ffn_out=w_ffn_out, gdn_w_in=gdn_w_in, gdn_w_out=gdn_w_out, dsw_w_in=dsw_w_in, dsw_w_out=dsw_w_out)

    whole = lambda a: Row(a, a.shape, lambda i: (0,) * a.ndim)
    (cond8,) = rowwise(lambda ids, v: (_silu(v),), [whole(c.reshape(8, LANES))], [], [Out((8, LANES), F32, (8, LANES), lambda i: (0, 0))], (1,), "cond")
    cond_all = all_gather_small(cond8, "gather_cond").reshape(N_DEV, D)
    cond16 = jnp.concatenate([cond_all, jnp.zeros((8, D), F32)], axis=0)
    ada_cols = w_ada.shape[2]
    mods = [matmul(cond16, w_ada[l], "nn", F32, f"ada_{l}")[:N_DEV] for l in range(2)]
    buf = jnp.concatenate([jnp.stack(mods, axis=1).reshape(-1, LANES), gdn_conv.reshape(-1, LANES)], axis=0)
    n_mod_rows = N_DEV * 2 * ada_cols // LANES
    got = all_gather_small(buf, "gather_mod").reshape(N_DEV, buf.shape[0], LANES)
    mod_parts, conv_parts = [], []
    for s in range(N_SHARD):
        from_dev = got[2 * s]
        mod_parts.append(lax.dynamic_index_in_dim(from_dev[:n_mod_rows].reshape(N_DEV, 2, ada_cols), me, 0, keepdims=False))
        conv_parts.append(from_dev[n_mod_rows:].reshape(4, -1))
    mod_nb = jnp.concatenate(mod_parts, axis=1)
    conv_w = jnp.concatenate(conv_parts, axis=1)
    (mod,) = rowwise(lambda ids, a, b: (a + b,), [whole(mod_nb), whole(b_ada)], [], [Out(mod_nb.shape, F32, mod_nb.shape, lambda i: (0, 0))], (1,), "mod_bias")
    mod = mod.reshape(2, 6, 1, D)
    sh1, sc1, g1, sh2, sc2, g2 = ([mod[l, k] for l in range(2)] for k in range(6))
    gmix = [norm_mix[l][None] for l in range(2)]
    gffn = [norm_ffn[l][None] for l in range(2)]

    gcols = gdn_w_in.shape[2]
    g_gdn_in, g_gdn_out = all_gather_shards([gdn_w_in[0].astype(BF16), gdn_w_out[0].astype(BF16)])
    gathered = lambda ws: [(N_SHARD,) + w.shape for w in ws]
    gate = (jnp.minimum(jnp.abs(g_gdn_in[0, 0, 0].astype(F32)), 0.0) + jnp.minimum(jnp.abs(mod[0, 0, 0, 0]), 0.0)).astype(BF16)
    w2 = [w_ffn_in[0].astype(BF16) + gate, w_ffn_out[0].astype(BF16) + gate]
    w3 = [dsw_w_in[0].astype(BF16) + gate, dsw_w_out[0].astype(BF16) + gate, w_ffn_in[1].astype(BF16) + gate, w_ffn_out[1].astype(BF16) + gate]
    fly2 = copies_start(w2, gathered(w2), _gather_copies, 4, "weights_ffn0_start")
    fly3 = copies_start(w3, gathered(w3), _gather_copies, 4, "weights_layer1_start")
    started = fly2[4][0, 0] + fly3[4][0, 0]
    w_gdn = jnp.concatenate([g_gdn_in[s] for s in range(N_SHARD)] + [jnp.zeros((D, GDN_PROJ - N_SHARD * gcols), BF16)], axis=1)
    alog, dtb = _pad_lanes(gdn_a_log[0]), _pad_lanes(gdn_dt_bias[0])
    qg2 = jnp.concatenate([dsw_q_norm, dsw_q_norm], axis=1)
    kg2 = jnp.concatenate([dsw_k_norm, dsw_k_norm], axis=1)
    w_gdn_out = g_gdn_out.reshape(GDN_H * LANES, D)
    gdn_args = (w_gdn, conv_w, alog, dtb, gdn_out_norm, w_gdn_out)
    sc1[0] = sc1[0] + started

    (h10,) = rowwise(f_norm_only, [_wide(x0)], [gmix[0], sc1[0], sh1[0]], [_wide_out(S, BF16)], (nt,), "l0_norm")
    y0, sv_g = gdn_forward(h10, *gdn_args)
    x1, h20 = rowwise(f_resid_norm, [_wide(x0), _wide(y0)], [g1[0], gffn[0], sc2[0], sh2[0]], [_wide_out(S, F32), _wide_out(S, BF16)], (nt,), "l0_mid")
    g_in0, g_out0 = copies_wait(*fly2[:4], _gather_copies, y0, "weights_ffn0_wait")
    w_ffn = [(g_in0, g_out0.reshape(FFN, D)), None]
    f0, sv_f0 = ffn_forward(h20, *w_ffn[0], "0")
    x2, h11 = rowwise(f_resid_norm, [_wide(x1), _wide(f0)], [g2[0], gmix[1], sc1[1], sh1[1]], [_wide_out(S, F32), _wide_out(S, BF16)], (nt,), "l1_in")
    g_dsw_in, g_dsw_out, g_in1, g_out1 = copies_wait(*fly3[:4], _gather_copies, f0, "weights_layer1_wait")
    w_ffn[1] = (g_in1, g_out1.reshape(FFN, D))
    dsw_args = (g_dsw_in, qg2, kg2)
    y1, sv_d = dsw_forward(h11, *dsw_args, rel_bias, g_dsw_out)
    x3, h21 = rowwise(f_resid_norm, [_wide(x2), _wide(y1)], [g1[1], gffn[1], sc2[1], sh2[1]], [_wide_out(S, F32), _wide_out(S, BF16)], (nt,), "l1_mid")
    f1, sv_f1 = ffn_forward(h21, *w_ffn[1], "1")
    part_spec = lambda a: Row(a, (None, 1, D), lambda i: (i, 0, 0))
    (parts,) = rowwise(f_loss, [_wide(x3), _wide(f1), _wide(tgt)], [g2[1]], [Out((nt, 1, D), F32, (None, 1, D), lambda i: (i, 0, 0))], (nt,), "loss")
    loss = lax.psum(jnp.sum(parts), ("x", "y", "c"))

    (dx3, df1), (dg2_1,) = rowwise_bwd(f_loss, [_wide(x3), _wide(f1, gdtype=BF16), _wide(tgt, diff=False)], [g2[1]],
                                       [part_spec(jnp.ones((nt, 1, D), F32))], (nt,), "loss_bwd")
    dh21, d_win1, d_wout1 = ffn_backward(df1, sv_f1, *w_ffn[1], "1")
    (dx2, dy1), (dg1_1, dgf1, dsc2_1, dsh2_1) = rowwise_bwd(
        f_resid_norm, [_wide(x2), _wide(y1, gdtype=BF16)], [g1[1], gffn[1], sc2[1], sh2[1]], [_wide(dx3), _wide(dh21)], (nt,), "l1_mid_bwd")
    dh11, g_d = dsw_backward(dy1, sv_d, *dsw_args, g_dsw_out)
    (dx1, df0), (dg2_0, dgm1, dsc1_1, dsh1_1) = rowwise_bwd(
        f_resid_norm, [_wide(x1), _wide(f0, gdtype=BF16)], [g2[0], gmix[1], sc1[1], sh1[1]], [_wide(dx2), _wide(dh11)], (nt,), "l1_in_bwd")
    by_shard = lambda a: a.reshape(N_SHARD, a.shape[0] // N_SHARD, a.shape[1])
    landing = lambda ps: [(3,) + p.shape[1:] for p in ps]
    dws3 = [g_d["w_in"], g_d["w_out"], d_win1, by_shard(d_wout1)]
    parts3 = [a.astype(BF16) for a in dws3]
    gfly3 = copies_start(parts3, landing(parts3), _scatter_copies, 3, "grads_layer1_start")
    w_out0 = w_ffn[0][1] + gfly3[4][0, 0].astype(BF16)
    dh20, d_win0, d_wout0 = ffn_backward(df0, sv_f0, w_ffn[0][0], w_out0, "0")
    (dx0p, dy0), (dg1_0, dgf0, dsc2_0, dsh2_0) = rowwise_bwd(
        f_resid_norm, [_wide(x0), _wide(y0, gdtype=BF16)], [g1[0], gffn[0], sc2[0], sh2[0]], [_wide(dx1), _wide(dh20)], (nt,), "l0_mid_bwd")
    dws2 = [d_win0, by_shard(d_wout0)]
    parts2 = [a.astype(BF16) for a in dws2]
    gfly2 = copies_start(parts2, landing(parts2), _scatter_copies, 3, "grads_ffn0_start")
    gdn_args = gdn_args[:5] + (w_gdn_out + gfly2[4][0, 0].astype(BF16),)
    dh10, g_g = gdn_backward(dy0, sv_g, *gdn_args)
    (grad_x,), (dgm0, dsc1_0, dsh1_0) = rowwise_bwd(f_first, [_wide(x0)], [gmix[0], sc1[0], sh1[0]], [_wide(dx0p), _wide(dh10)], (nt,), "l0_norm_bwd")

    dmod = jnp.concatenate([dsh1_0, dsc1_0, dg1_0, dsh2_0, dsc2_0, dg2_0, dsh1_1, dsc1_1, dg1_1, dsh2_1, dsc2_1, dg2_1], axis=1)
    d_rel = jnp.transpose(g_d["rel"][:, :, 0])
    fold = lambda v: v[:, :DSW_DH] + v[:, DSW_DH:]
    small = [dmod, jnp.concatenate([dgm0, dgm1], axis=1), jnp.concatenate([dgf0, dgf1], axis=1), g_g["conv"].reshape(1, -1),
             g_g["alog"], g_g["dtb"], g_g["gain"], _pad_lanes(fold(g_d["q_gain2"])[0]), _pad_lanes(fold(g_d["k_gain2"])[0]),
             d_rel.reshape(1, -1)]
    used = [v.shape[1] // LANES for v in small]
    sizes = [-(-u // 8) * 8 for u in used]
    pad8 = lambda v, u, s: jnp.concatenate([v.reshape(u, LANES), jnp.zeros((s - u, LANES), F32)], axis=0) if s > u else v.reshape(u, LANES)
    pad_rows = sum(sizes)
    sbuf = jnp.concatenate([pad8(v, u, s) for v, u, s in zip(small, used, sizes)], axis=0)
    sgot = all_gather_small(sbuf, "gather_small_grads")
    ssum = add_rows([sgot[d * pad_rows:(d + 1) * pad_rows] for d in range(N_DEV)], F32, "sum_small_grads", rt=pad_rows)
    offs = np.cumsum([0] + sizes)
    take = lambda k: ssum[offs[k]:offs[k] + used[k]].reshape(1, -1)
    grad_b_ada = take(0).reshape(2, 6 * D)
    grad_norm_mix = take(1).reshape(2, D)
    grad_norm_ffn = take(2).reshape(2, D)
    conv_full = take(3).reshape(4, -1)
    ncv = gdn_conv.shape[2]
    grad_gdn_conv = lax.dynamic_slice_in_dim(conv_full, s_me * ncv, ncv, axis=1)[None]
    grad_a_log = take(4)[:, :GDN_H]
    grad_dt_bias = take(5)[:, :GDN_H]
    grad_out_norm = take(6)
    grad_q_norm = take(7)[:, :DSW_DH]
    grad_k_norm = take(8)[:, :DSW_DH]
    grad_rel = take(9).reshape(REL_BUCKETS, 3 * GDN_H)
    dmod_all = sgot.reshape(N_DEV, pad_rows, LANES)[:, :used[0]].reshape(N_DEV, 2, 6 * D)
    dmod_mine = lax.dynamic_slice_in_dim(dmod_all, s_me * ada_cols, ada_cols, axis=2)
    dmod16 = jnp.concatenate([dmod_mine, jnp.zeros_like(dmod_mine)], axis=0)
    grad_w_ada = jnp.stack([matmul(cond16, dmod16[:, l], "tn", F32, f"ada_dw_{l}") for l in range(2)])

    dg_in = jnp.stack([g_g["w_in"][:, s * gcols:(s + 1) * gcols] for s in range(N_SHARD)])
    dws = [dg_in, by_shard(g_g["w_out"])]
    keeps, gives = [], []
    for a in dws:
        rh = a.shape[1] // 2
        keeps.append(lax.dynamic_slice_in_dim(a, ci * rh, rh, axis=1))
        gives.append(lax.dynamic_slice_in_dim(a, (1 - ci) * rh, rh, axis=1).astype(BF16))
    from_sib = sibling_exchange(gives, "grads_to_sibling")
    flat2 = lambda a: a.reshape(-1, a.shape[-1])
    parts = [add_rows([flat2(k_), flat2(f_)], BF16, f"grads_chip_sum_{i}").reshape(k_.shape) for i, (k_, f_) in enumerate(zip(keeps, from_sib))]
    others = scatter_to_chips(parts)
    halves = []
    for i, (p_, o_) in enumerate(zip(parts, others)):
        own = lax.dynamic_index_in_dim(p_, s_me, 0, keepdims=False)
        halves.append(add_rows([own, o_[0], o_[1], o_[2]], F32, f"grads_sum_{i}"))
    sib_halves = sibling_exchange(halves, "grads_from_sibling")

    def whole_shard(mine, theirs):
        both = jnp.stack([mine, theirs])
        return jnp.concatenate([lax.dynamic_index_in_dim(both, ci, 0, keepdims=False),
                                lax.dynamic_index_in_dim(both, 1 - ci, 0, keepdims=False)], axis=0)

    s_gdn_in, s_gdn_out = [whole_shard(a, b) for a, b in zip(halves, sib_halves)]
    got3 = copies_wait(*gfly3[:4], _scatter_copies, grad_x, "grads_layer1_wait")
    got2 = copies_wait(*gfly2[:4], _scatter_copies, grad_x, "grads_ffn0_wait")
    core_sums = []
    for i, (full, got) in enumerate(zip(dws3 + dws2, got3 + got2)):
        own = lax.dynamic_index_in_dim(full, s_me, 0, keepdims=False)
        core_sums.append(add_rows([own, got[0], got[1], got[2]], F32, f"grads_core_sum_{i}"))
    sib_sums = sibling_exchange(core_sums, "grads_core_sums_swap")
    s_dsw_in, s_dsw_out, s_in1, s_out1, s_in0, s_out0 = [add_rows([a, b], F32, f"grads_chip_total_{i}")
                                                          for i, (a, b) in enumerate(zip(core_sums, sib_sums))]
    gsh = dict(gdn_w_in=s_gdn_in[None], gdn_w_out=s_gdn_out[None],
               w_ffn_in=jnp.stack([s_in0, s_in1]), w_ffn_out=jnp.stack([s_out0, s_out1]),
               dsw_w_in=s_dsw_in[None], dsw_w_out=s_dsw_out[None])

    grads = dict(w_ada=grad_w_ada, b_ada=grad_b_ada, norm_mix=grad_norm_mix, norm_ffn=grad_norm_ffn, w_ffn_in=gsh["w_ffn_in"],
                 w_ffn_out=gsh["w_ffn_out"], gdn_w_in=gsh["gdn_w_in"], gdn_conv=grad_gdn_conv, gdn_a_log=grad_a_log,
                 gdn_dt_bias=grad_dt_bias, gdn_out_norm=grad_out_norm, gdn_w_out=gsh["gdn_w_out"], dsw_w_in=gsh["dsw_w_in"],
                 dsw_q_norm=grad_q_norm, dsw_k_norm=grad_k_norm, dsw_w_out=gsh["dsw_w_out"], rel_bias=grad_rel)
    weights = dict(w_ada=w_ada, b_ada=b_ada, norm_mix=norm_mix, norm_ffn=norm_ffn, w_ffn_in=w_ffn_in, w_ffn_out=w_ffn_out,
                   gdn_w_in=gdn_w_in, gdn_conv=gdn_conv, gdn_a_log=gdn_a_log, gdn_dt_bias=gdn_dt_bias, gdn_out_norm=gdn_out_norm,
                   gdn_w_out=gdn_w_out, dsw_w_in=dsw_w_in, dsw_q_norm=dsw_q_norm, dsw_k_norm=dsw_k_norm, dsw_w_out=dsw_w_out,
                   rel_bias=rel_bias)
    ms = dict(w_ada=m_w_ada, b_ada=m_b_ada, norm_mix=m_norm_mix, norm_ffn=m_norm_ffn, w_ffn_in=m_w_ffn_in, w_ffn_out=m_w_ffn_out,
              gdn_w_in=m_gdn_w_in, gdn_conv=m_gdn_conv, gdn_a_log=m_gdn_a_log, gdn_dt_bias=m_gdn_dt_bias, gdn_out_norm=m_gdn_out_norm,
              gdn_w_out=m_gdn_w_out, dsw_w_in=m_dsw_w_in, dsw_q_norm=m_dsw_q_norm, dsw_k_norm=m_dsw_k_norm, dsw_w_out=m_dsw_w_out,
              rel_bias=m_rel_bias)
    vs = dict(w_ada=v_w_ada, b_ada=v_b_ada, norm_mix=v_norm_mix, norm_ffn=v_norm_ffn, w_ffn_in=v_w_ffn_in, w_ffn_out=v_w_ffn_out,
              gdn_w_in=v_gdn_w_in, gdn_conv=v_gdn_conv, gdn_a_log=v_gdn_a_log, gdn_dt_bias=v_gdn_dt_bias, gdn_out_norm=v_gdn_out_norm,
              gdn_w_out=v_gdn_w_out, dsw_w_in=v_dsw_w_in, dsw_q_norm=v_dsw_q_norm, dsw_k_norm=v_dsw_k_norm, dsw_w_out=v_dsw_w_out,
              rel_bias=v_rel_bias)
    names = list(weights)
    deltas, new_m, new_v = [], [], []
    for n in names:
        g = grads[n].reshape(weights[n].shape)
        grads[n] = g
        d, nm, nv = adamw(weights[n], g, ms[n], vs[n], f"adamw_{n}")
        deltas.append(d)
        new_m.append(nm)
        new_v.append(nv)
    return (loss, grad_x[None], *[grads[n] for n in names], *deltas, *new_m, *new_v)
```

```python
import functools
import math

import numpy as np
import jax
import jax.numpy as jnp
from jax import lax
from jax.experimental import pallas as pl
from jax.experimental.pallas import tpu as pltpu

F32 = jnp.float32
BF16 = jnp.bfloat16
SDS = jax.ShapeDtypeStruct
MESH = pl.DeviceIdType.MESH
ANY = pl.BlockSpec(memory_space=pl.ANY)

D = 1024
EPS = 1e-6
LANES = 128
GDN_H = 8
GDN_DK = 128
GDN_C = 64
DSW_GROUPS = ((128, 1), (512, 4), (2048, 16))
DSW_SPAN = 128
DSW_DH = 64
DSW_HG = 512
REL_BUCKETS = 32
REL_MAX_DIST = 2048
FFN = 2816
N_SHARD = 4
N_DEV = 8
VMEM_LIMIT = 48 * 1024 * 1024
NEG = -1e30

ADAM_LR, ADAM_B1, ADAM_B2, ADAM_EPS, ADAM_WD, ADAM_STEP = 0.001, 0.9, 0.999, 1e-08, 0.01, 10


def _cp(n_axes):
    return pltpu.CompilerParams(dimension_semantics=("arbitrary",) * n_axes, vmem_limit_bytes=VMEM_LIMIT)


def _blk(dim, cap):
    if dim <= cap:
        return dim
    best = None
    for b in range(LANES, cap + 1, LANES):
        if dim % b == 0:
            best = b
    assert best is not None, (dim, cap)
    return best


MAX_SHARD_BLOCK = 1408
def matmul(a, b, mode, out_dtype, name, cap_m=MAX_SHARD_BLOCK, cap_n=MAX_SHARD_BLOCK, cap_k=2048, col_shards=0):
    ns = col_shards
    if mode == "nn":
        (M, K) = a.shape
        K2, N = (b.shape[1], ns * b.shape[2]) if ns else b.shape
    elif mode == "nt":
        (M, K) = a.shape
        N, K2 = (b.shape[1], ns * b.shape[2]) if ns else b.shape
    else:
        (K, M), (K2, N) = a.shape, b.shape
    assert K == K2, (a.shape, b.shape, mode)
    if K <= 3072:
        cap_k = K
        if K > 2048:
            cap_n = 1024
    n_unit = N // ns if (ns and mode != "nt") else N
    k_unit = K // ns if (ns and mode == "nt") else K
    bm = _blk(M, cap_m)
    bn = _blk(n_unit, MAX_SHARD_BLOCK) if n_unit != N else _blk(N, cap_n)
    if k_unit != K:
        bk = _blk(k_unit, MAX_SHARD_BLOCK)
    else:
        bk = _blk(K, 1024 if (ns and mode == "tn") else cap_k)
    nk = K // bk
    nps, kps = n_unit // bn, k_unit // bk
    dims = {"nn": ((1,), (0,)), "nt": ((1,), (1,)), "tn": ((0,), (0,))}[mode]

    def dot(a_ref, b_ref):
        return lax.dot_general(a_ref[...].astype(BF16), b_ref[...].astype(BF16), (dims, ((), ())), preferred_element_type=F32)

    def body_one(a_ref, b_ref, o_ref):
        o_ref[...] = dot(a_ref, b_ref).astype(o_ref.dtype)

    def body_acc(a_ref, b_ref, o_ref, acc_ref):
        k = pl.program_id(2)

        @pl.when(k == 0)
        def _():
            acc_ref[...] = jnp.zeros_like(acc_ref)

        acc_ref[...] += dot(a_ref, b_ref)

        @pl.when(k == nk - 1)
        def _():
            o_ref[...] = acc_ref[...].astype(o_ref.dtype)

    a_spec = pl.BlockSpec((bk, bm), lambda i, j, k: (k, i)) if mode == "tn" else pl.BlockSpec((bm, bk), lambda i, j, k: (i, k))
    if mode == "nt":
        b_spec = pl.BlockSpec((None, bn, bk), lambda i, j, k: (k // kps, j, k % kps)) if ns else pl.BlockSpec((bn, bk), lambda i, j, k: (j, k))
    elif mode == "nn" and ns:
        b_spec = pl.BlockSpec((None, bk, bn), lambda i, j, k: (j // nps, k, j % nps))
    else:
        b_spec = pl.BlockSpec((bk, bn), lambda i, j, k: (k, j))
    if mode == "tn" and ns:
        o_spec, o_shape = pl.BlockSpec((None, bm, bn), lambda i, j, k: (j // nps, i, j % nps)), (ns, M, n_unit)
    else:
        o_spec, o_shape = pl.BlockSpec((bm, bn), lambda i, j, k: (i, j)), (M, N)
    return pl.pallas_call(
        body_one if nk == 1 else body_acc, name=name, grid=(M // bm, N // bn, nk),
        in_specs=[a_spec, b_spec], out_specs=o_spec,
        out_shape=SDS(o_shape, out_dtype), scratch_shapes=[] if nk == 1 else [pltpu.VMEM((bm, bn), F32)],
        compiler_params=_cp(3),
    )(a, b)


class Row:
    def __init__(self, arr, bshape, imap, splits=None, diff=True, acc=False, gdtype=F32, gshape=None, gbshape=None, gimap=None,
                 lead=0):
        self.arr, self.bshape, self.imap = arr, tuple(bshape), imap
        self.splits, self.lead = splits, lead
        self.diff, self.acc, self.gdtype = diff, acc, gdtype
        self.gshape = tuple(arr.shape) if gshape is None else tuple(gshape)
        self.gbshape = self.bshape if gbshape is None else tuple(gbshape)
        self.gimap = imap if gimap is None else gimap

    def gspec(self):
        return pl.BlockSpec(self.gbshape, self.gimap)

    def spec(self):
        return pl.BlockSpec(self.bshape, self.imap)

    def pieces(self, ref):
        return _load_pieces(ref, self.splits, self.lead)

    def n_pieces(self):
        return _n_pieces(self.splits, self.lead)


class Out:
    def __init__(self, shape, dtype, bshape, imap, splits=None, lead=0):
        self.shape, self.dtype, self.bshape, self.imap = tuple(shape), dtype, tuple(bshape), imap
        self.splits, self.lead = splits, lead

    def n_pieces(self):
        return _n_pieces(self.splits, self.lead)


def _n_pieces(splits, lead):
    return lead if lead else (1 if splits is None else len(splits))


def _load_pieces(ref, splits, lead):
    if lead:
        return [ref[k].astype(F32) for k in range(lead)]
    if splits is None:
        return [ref[...].astype(F32)]
    out, o = [], 0
    for w in splits:
        out.append(ref[..., o:o + w].astype(F32))
        o += w
    return out


def _store_pieces(ref, splits, lead, vals, accumulate=False):
    def put(idx, v):
        if accumulate:
            ref[idx] += v.astype(ref.dtype)
        else:
            ref[idx] = v.astype(ref.dtype)

    if lead:
        for k in range(lead):
            put(k, vals[k])
    elif splits is None:
        put(..., vals[0])
    else:
        o = 0
        for w, v in zip(splits, vals):
            put((..., slice(o, o + w)), v)
            o += w


def rowwise(fn, rows, params, outs, grid, name):
    nr, npar = len(rows), len(params)

    def body(*refs):
        ids = tuple(pl.program_id(a) for a in range(len(grid)))
        vals = []
        for r, ref in zip(rows, refs[:nr]):
            vals += r.pieces(ref)
        pvals = [ref[...].astype(F32) for ref in refs[nr:nr + npar]]
        res = list(fn(ids, *vals, *pvals))
        o = 0
        for spec, ref in zip(outs, refs[nr + npar:]):
            n = spec.n_pieces()
            _store_pieces(ref, spec.splits, spec.lead, res[o:o + n])
            o += n

    nz = len(grid)
    pspecs = [pl.BlockSpec(p.shape, (lambda *ids, _n=p.ndim: (0,) * _n)) for p in params]
    res = pl.pallas_call(
        body, name=name, grid=grid,
        in_specs=[r.spec() for r in rows] + pspecs,
        out_specs=[pl.BlockSpec(o.bshape, o.imap) for o in outs],
        out_shape=[SDS(o.shape, o.dtype) for o in outs],
        compiler_params=_cp(nz),
    )(*[r.arr for r in rows], *params)
    return list(res)


def rowwise_bwd(fn, rows, params, cots, grid, name):
    nr, npar, nc = len(rows), len(params), len(cots)
    drows = [r for r in rows if r.diff]
    nz = len(grid)

    def body(*refs):
        ids = tuple(pl.program_id(a) for a in range(nz))
        row_refs, par_refs = refs[:nr], refs[nr:nr + npar]
        cot_refs = refs[nr + npar:nr + npar + nc]
        drow_refs = refs[nr + npar + nc:nr + npar + nc + len(drows)]
        dpar_refs = refs[nr + npar + nc + len(drows):]
        pieces, is_diff = [], []
        for r, ref in zip(rows, row_refs):
            ps = r.pieces(ref)
            pieces += ps
            is_diff += [r.diff] * len(ps)
        pvals = [ref[...].astype(F32) for ref in par_refs]
        dvals = [p for p, dflag in zip(pieces, is_diff) if dflag]
        nd = len(dvals)

        def f(*args):
            it = iter(args[:nd])
            full = [next(it) if dflag else p for p, dflag in zip(pieces, is_diff)]
            return tuple(fn(ids, *full, *args[nd:]))

        _, vjp = jax.vjp(f, *dvals, *pvals)
        cvals = []
        for c, ref in zip(cots, cot_refs):
            cvals += c.pieces(ref)
        g = vjp(tuple(cvals))
        o = 0
        first_inner = ids[-1] == 0
        for r, ref in zip(drows, drow_refs):
            n = r.n_pieces()
            gs = g[o:o + n]
            o += n
            if r.acc:
                @pl.when(first_inner)
                def _(ref=ref):
                    ref[...] = jnp.zeros_like(ref)
            _store_pieces(ref, r.splits, r.lead, gs, accumulate=r.acc)
        first = functools.reduce(jnp.logical_and, [i == 0 for i in ids])
        for ref, gp in zip(dpar_refs, g[nd:]):
            @pl.when(first)
            def _(ref=ref):
                ref[...] = jnp.zeros_like(ref)
            ref[...] += gp

    pspecs = [pl.BlockSpec(p.shape, (lambda *ids, _n=p.ndim: (0,) * _n)) for p in params]
    res = pl.pallas_call(
        body, name=name, grid=grid,
        in_specs=[r.spec() for r in rows] + pspecs + [c.spec() for c in cots],
        out_specs=[r.gspec() for r in drows] + pspecs,
        out_shape=[SDS(r.gshape, r.gdtype) for r in drows] + [SDS(p.shape, F32) for p in params],
        compiler_params=_cp(nz),
    )(*[r.arr for r in rows], *params, *[c.arr for c in cots])
    res = list(res)
    return res[:len(drows)], res[len(drows):]


def _sigmoid(x):
    return 0.5 * (jnp.tanh(0.5 * x) + 1.0)


def _silu(x):
    return x * _sigmoid(x)


def _normmod(x, gain, sc, sh):
    inv = lax.rsqrt(jnp.mean(x * x, axis=-1, keepdims=True) + EPS)
    return x * inv * gain * (1.0 + sc) + sh


def f_first(ids, x, gain, sc, sh):
    return x, _normmod(x, gain, sc, sh)


def f_resid_norm(ids, x, y, g, gain, sc, sh):
    xn = x + g * y
    return xn, _normmod(xn, gain, sc, sh)


@jax.custom_vjp
def _swiglu(gate, up):
    return _silu(gate) * up


def _swiglu_fwd(gate, up):
    return _silu(gate) * up, (gate, up)


def _swiglu_bwd(res, da):
    gate, up = res
    s = _sigmoid(gate)
    gs = gate * s
    return da * up * (s + gs * (1.0 - s)), da * gs


_swiglu.defvjp(_swiglu_fwd, _swiglu_bwd)


def f_swiglu(ids, gate, up):
    return (_swiglu(gate, up),)


def f_loss(ids, x, y, tgt, g):
    out = x + g * y
    e = out - tgt
    part = 0.5 * jnp.sum(e * e, axis=0, keepdims=True) * (1.0 / D)
    return (part,)


def _softplus(x):
    return jnp.maximum(x, 0.0) + jnp.log(1.0 + jnp.exp(-jnp.abs(x)))


def _chunk_tril(T):
    r = lax.broadcasted_iota(jnp.int32, (T, T), 0)
    c = lax.broadcasted_iota(jnp.int32, (T, T), 1)
    return jnp.where((r // GDN_C == c // GDN_C) & (c <= r), 1.0, 0.0).astype(F32)


def _dot_hi(a, b, dims=((1,), (0,))):
    return lax.dot_general(a, b, (dims, ((), ())), precision=lax.Precision.HIGHEST, preferred_element_type=F32)


def _dot_x3(a, b, dims=((1,), (0,))):
    return lax.dot_general(a, b, (dims, ((), ())), precision=lax.Precision.HIGH, preferred_element_type=F32)


def f_gdn_gates(ids, ab, alog, dtb):
    T = ab.shape[0]
    g = -jnp.exp(alog) * _softplus(ab + dtb)
    beta = _sigmoid(ab)
    gcum = _dot_x3(_chunk_tril(T), g)
    row = lax.broadcasted_iota(jnp.int32, (LANES, LANES), 0)
    sel = lambda k: jnp.where(row == k, 1.0, 0.0).astype(F32)
    gcs = [_dot_x3(gcum, sel(h)) for h in range(GDN_H)]
    bts = [_dot_x3(beta, sel(GDN_H + h)) for h in range(GDN_H)]
    return (*gcs, *bts)


def f_gdn_post(ids, *args):
    os_, zs, gain = args[:GDN_H], args[GDN_H:2 * GDN_H], args[2 * GDN_H]
    out = []
    for o, z in zip(os_, zs):
        inv = lax.rsqrt(jnp.mean(o * o, axis=-1, keepdims=True) + EPS)
        out.append(o * inv * gain * _silu(z))
    return tuple(out)


def _qknorm1(x, gain2, scale):
    lane = lax.broadcasted_iota(jnp.int32, x.shape, 1)
    lo = lane < DSW_DH
    x2 = x * x
    s_all = jnp.sum(x2, axis=-1, keepdims=True)
    s_lo = jnp.sum(jnp.where(lo, x2, 0.0), axis=-1, keepdims=True)
    ms = jnp.where(lo, s_lo, s_all - s_lo) * (1.0 / DSW_DH)
    return x * lax.rsqrt(ms + EPS) * (gain2 * scale)


def f_qknorm(ids, *args):
    return tuple(_qknorm1(x, args[-1], 1.0) for x in args[:-1])


def f_qnorm(ids, *args):
    return tuple(_qknorm1(x, args[-1], DSW_DH ** -0.5) for x in args[:-1])


def f_combine(ids, o0, o1, o2, l0, l1, l2):
    m = jnp.maximum(jnp.maximum(l0, l1), l2)
    e0, e1, e2 = jnp.exp(l0 - m), jnp.exp(l1 - m), jnp.exp(l2 - m)
    den = e0 + e1 + e2
    o = (e0 * o0 + e1 * o1 + e2 * o2) / den
    return o, m + jnp.log(den)


GDN_T = 512
HALO = 16


def _conv_pre(xx, w):
    acc = xx * w[3:4, :]
    for j in range(3):
        acc = acc + pltpu.roll(xx, shift=3 - j, axis=0) * w[j:j + 1, :]
    return acc


@jax.custom_vjp
def _qkv_act_core(pre, norm_on, scale):
    s = _silu(pre)
    r = lax.rsqrt(jnp.sum(s * s, axis=-1, keepdims=True) + EPS)
    return jnp.where(norm_on > 0.5, s * r * scale, s)


def _qkv_act_fwd(pre, norm_on, scale):
    return _qkv_act_core(pre, norm_on, scale), (pre, norm_on, scale)


def _qkv_act_bwd(res, dout):
    pre, norm_on, scale = res
    sig = _sigmoid(pre)
    s = pre * sig
    r = lax.rsqrt(jnp.sum(s * s, axis=-1, keepdims=True) + EPS)
    unit = s * r
    dn = dout * scale
    ds = jnp.where(norm_on > 0.5, r * (dn - unit * jnp.sum(dn * unit, axis=-1, keepdims=True)), dout)
    return ds * (sig + s * (1.0 - sig)), jnp.zeros_like(norm_on), jnp.zeros_like(scale)


_qkv_act_core.defvjp(_qkv_act_fwd, _qkv_act_bwd)


def _qkv_act(pre, cidx):
    norm_on = jnp.where(cidx < 2 * GDN_H, 1.0, 0.0).astype(F32)
    scale = jnp.where(cidx < GDN_H, GDN_DK ** -0.5, 1.0).astype(F32)
    return _qkv_act_core(pre, norm_on, scale)


def gdn_pre(proj, conv_w, S):
    nt = S // GDN_T
    hb = GDN_T // HALO

    def body(prev_ref, cur_ref, w_ref, o_ref):
        p, i = pl.program_id(0), pl.program_id(1)
        for h in range(GDN_H):
            cols = slice(LANES * h, LANES * (h + 1))
            prev = jnp.where(i > 0, prev_ref[:, cols].astype(F32), 0.0)
            xx = jnp.concatenate([prev, cur_ref[:, cols].astype(F32)], axis=0)
            pre = _conv_pre(xx, w_ref[:, cols])[HALO:]
            o_ref[h] = _qkv_act(pre, p * GDN_H + h)

    hv = GDN_H * LANES
    return pl.pallas_call(
        body, name="gdn_pre", grid=(3, nt),
        in_specs=[pl.BlockSpec((HALO, hv), lambda p, i: (jnp.maximum(i * hb - 1, 0), p)),
                  pl.BlockSpec((GDN_T, hv), lambda p, i: (i, p)),
                  pl.BlockSpec((4, hv), lambda p, i: (0, p))],
        out_specs=pl.BlockSpec((None, GDN_H, GDN_T, LANES), lambda p, i: (p, 0, i, 0)),
        out_shape=SDS((3, GDN_H, S, LANES), F32),
        compiler_params=_cp(2),
    )(proj, proj, conv_w)


def gdn_pre_bwd(proj, conv_w, dqkv, S):
    nt = S // GDN_T
    hb = GDN_T // HALO
    last_h = S // HALO - 1

    def body(prev_ref, cur_ref, next_ref, w_ref, d_ref, dnext_ref, dx_ref, dw_ref):
        p, i = pl.program_id(0), pl.program_id(1)

        @pl.when(i == 0)
        def _():
            dw_ref[...] = jnp.zeros_like(dw_ref)

        for h in range(GDN_H):
            cols = slice(LANES * h, LANES * (h + 1))
            w = w_ref[:, cols]
            prev = jnp.where(i > 0, prev_ref[:, cols].astype(F32), 0.0)
            xx = jnp.concatenate([prev, cur_ref[:, cols].astype(F32), next_ref[:, cols].astype(F32)], axis=0)
            dnext = jnp.where(i < nt - 1, dnext_ref[h], 0.0)
            dd = jnp.concatenate([jnp.zeros((HALO, LANES), F32), d_ref[h], dnext], axis=0)
            pre = _conv_pre(xx, w)
            _, vjp = jax.vjp(lambda v, _c=p * GDN_H + h: _qkv_act(v, _c), pre)
            (dpre,) = vjp(dd)
            row = lax.broadcasted_iota(jnp.int32, dpre.shape, 0)
            dpre = jnp.where(row >= HALO, dpre, 0.0)
            dx = dpre * w[3:4, :]
            R = dpre.shape[0]
            for j in range(3):
                dx = dx + pltpu.roll(dpre, shift=R - (3 - j), axis=0) * w[j:j + 1, :]
            dx_ref[:, cols] = dx[HALO:HALO + GDN_T].astype(dx_ref.dtype)
            own = jnp.where(row < HALO + GDN_T, dpre, 0.0)
            rows_w = [jnp.sum(own * pltpu.roll(xx, shift=3 - j, axis=0), axis=0, keepdims=True) for j in range(3)]
            rows_w.append(jnp.sum(own * xx, axis=0, keepdims=True))
            r4 = lax.broadcasted_iota(jnp.int32, (4, LANES), 0)
            dw = jnp.zeros((4, LANES), F32)
            for j in range(4):
                dw = dw + jnp.where(r4 == j, rows_w[j], 0.0)
            dw_ref[:, cols] += dw

    hv = GDN_H * LANES
    return pl.pallas_call(
        body, name="gdn_pre_bwd", grid=(3, nt),
        in_specs=[pl.BlockSpec((HALO, hv), lambda p, i: (jnp.maximum(i * hb - 1, 0), p)),
                  pl.BlockSpec((GDN_T, hv), lambda p, i: (i, p)),
                  pl.BlockSpec((HALO, hv), lambda p, i: (jnp.minimum((i + 1) * hb, last_h), p)),
                  pl.BlockSpec((4, hv), lambda p, i: (0, p)),
                  pl.BlockSpec((None, GDN_H, GDN_T, LANES), lambda p, i: (p, 0, i, 0)),
                  pl.BlockSpec((None, GDN_H, HALO, LANES), lambda p, i: (p, 0, jnp.minimum((i + 1) * hb, last_h), 0))],
        out_specs=[pl.BlockSpec((GDN_T, hv), lambda p, i: (i, p)),
                   pl.BlockSpec((4, hv), lambda p, i: (0, p))],
        out_shape=[SDS((S, 3 * hv), BF16), SDS((4, 3 * hv), F32)],
        compiler_params=_cp(2),
    )(proj, proj, proj, conv_w, dqkv, dqkv)


_DIMS = {"nn": ((1,), (0,)), "nt": ((1,), (1,)), "tn": ((0,), (0,))}


def _mm_raw(a, b, mode, hi):
    if hi:
        return _dot_hi(a, b, _DIMS[mode])
    return lax.dot_general(a.astype(BF16), b.astype(BF16), (_DIMS[mode], ((), ())), preferred_element_type=F32)


@functools.partial(jax.custom_vjp, nondiff_argnums=(2, 3))
def mm(a, b, mode, hi):
    return _mm_raw(a, b, mode, hi)


def _mm_fwd(a, b, mode, hi):
    return _mm_raw(a, b, mode, hi), (a, b)


def _mm_bwd(mode, hi, res, dc):
    a, b = res
    if mode == "nn":
        da, db = mm(dc, b, "nt", hi), mm(a, dc, "tn", hi)
    elif mode == "nt":
        da, db = mm(dc, b, "nn", hi), mm(dc, a, "tn", hi)
    else:
        da, db = mm(b, dc, "nt", hi), mm(a, dc, "nn", hi)
    return da, db


mm.defvjp(_mm_fwd, _mm_bwd)


TRI_BASE = 8


def _unit_lower_inverses(Ls):
    n = Ls[0].shape[0]
    r = lax.broadcasted_iota(jnp.int32, (n, n), 0)
    c = lax.broadcasted_iota(jnp.int32, (n, n), 1)
    eye = jnp.where(r == c, 1.0, 0.0).astype(F32)
    base = r // TRI_BASE == c // TRI_BASE
    Ps = [jnp.where(base, -L, 0.0) for L in Ls]
    invs = [eye + P for P in Ps]
    k = 1
    while 2 * k < TRI_BASE:
        Ps = [_dot_x3(P, P) for P in Ps]
        invs = [inv + _dot_x3(inv, P) for inv, P in zip(invs, Ps)]
        k *= 2
    b = 2 * TRI_BASE
    while b <= n:
        off_mask = (r // b == c // b) & ((r % b) >= b // 2) & ((c % b) < b // 2)
        ts = [_dot_x3(inv, jnp.where(off_mask, L, 0.0)) for inv, L in zip(invs, Ls)]
        invs = [inv - _dot_x3(t, inv) for inv, t in zip(invs, ts)]
        b *= 2
    return invs


@jax.custom_vjp
def tri_apply(invs, Ls, r1s, r2s):
    return [_dot_x3(i, r) for i, r in zip(invs, r1s)], [_dot_x3(i, r) for i, r in zip(invs, r2s)]


def _tri_fwd(invs, Ls, r1s, r2s):
    s1s = [_dot_x3(i, r) for i, r in zip(invs, r1s)]
    s2s = [_dot_x3(i, r) for i, r in zip(invs, r2s)]
    return (s1s, s2s), (invs, s1s, s2s)


def _tri_bwd(res, ds):
    invs, s1s, s2s = res
    d1s = [_dot_x3(i, d, _DIMS["tn"]) for i, d in zip(invs, ds[0])]
    d2s = [_dot_x3(i, d, _DIMS["tn"]) for i, d in zip(invs, ds[1])]
    dLs = [-(_dot_x3(d1, s1, _DIMS["nt"]) + _dot_x3(d2, s2, _DIMS["nt"])) for d1, s1, d2, s2 in zip(d1s, s1s, d2s, s2s)]
    return [jnp.zeros_like(i) for i in invs], dLs, d1s, d2s


tri_apply.defvjp(_tri_fwd, _tri_bwd)


def _gdn_chunk(qs, ks, vs, gcbs, btbs, Ss, invs=None):
    C = qs[0].shape[0]
    r = lax.broadcasted_iota(jnp.int32, (C, C), 0)
    c = lax.broadcasted_iota(jnp.int32, (C, C), 1)
    causal, strict = c <= r, c < r
    rows = lax.broadcasted_iota(jnp.int32, gcbs[0].shape, 0)
    Gs = [g[:, :C] for g in gcbs]
    decays = [jnp.exp(jnp.where(causal, G - G.T, NEG)) for G in Gs]
    kbs = [k * b for k, b in zip(ks, btbs)]
    vbs = [v * b for v, b in zip(vs, btbs)]
    Ls = [jnp.where(strict, mm(kb, k, "nt", False) * d, 0.0) for kb, k, d in zip(kbs, ks, decays)]
    egs = [jnp.exp(g) for g in gcbs]
    if invs is None:
        invs = _unit_lower_inverses(Ls)
    us, ws = tri_apply(invs, Ls, vbs, [kb * eg for kb, eg in zip(kbs, egs)])
    qks = [jnp.where(causal, mm(q, k, "nt", False) * d, 0.0) for q, k, d in zip(qs, ks, decays)]
    g_lasts = [jnp.sum(jnp.where(rows == C - 1, g, 0.0), axis=0, keepdims=True) for g in gcbs]
    q_decs = [q * eg for q, eg in zip(qs, egs)]
    k_decs = [k * jnp.exp(gl - g) for k, gl, g in zip(ks, g_lasts, gcbs)]
    v_news = [u - mm(w, S, "nn", False) for u, w, S in zip(us, ws, Ss)]
    os_ = [mm(qd, S, "nn", False) + mm(qk, vn, "nn", False) for qd, S, qk, vn in zip(q_decs, Ss, qks, v_news)]
    S_news = [S * jnp.exp(gl) + mm(kd, vn, "tn", False) for S, gl, kd, vn in zip(Ss, g_lasts, k_decs, v_news)]
    return os_, S_news, invs


def gdn_core(qkv, gc, bt, S):
    nchunk = S // GDN_C

    def body(qkv_ref, g_ref, b_ref, o_ref, st_ref, inv_ref, s_scr):
        n = pl.program_id(0)

        @pl.when(n == 0)
        def _():
            s_scr[...] = jnp.zeros_like(s_scr)

        heads = range(GDN_H)
        S_in = [s_scr[h] for h in heads]
        os_, S_new, invs = _gdn_chunk([qkv_ref[0, h] for h in heads], [qkv_ref[1, h] for h in heads], [qkv_ref[2, h] for h in heads],
                                      [g_ref[h] for h in heads], [b_ref[h] for h in heads], S_in)
        for h in heads:
            st_ref[h] = S_in[h]
            inv_ref[h] = invs[h]
            o_ref[h] = os_[h]
            s_scr[h] = S_new[h]

    blk3 = pl.BlockSpec((3, GDN_H, GDN_C, LANES), lambda n: (0, 0, n, 0))
    hb = pl.BlockSpec((GDN_H, GDN_C, LANES), lambda n: (0, n, 0))
    return pl.pallas_call(
        body, name="gdn_core", grid=(nchunk,),
        in_specs=[blk3, hb, hb],
        out_specs=[hb, pl.BlockSpec((GDN_H, None, GDN_DK, LANES), lambda n: (0, n, 0, 0)),
                   pl.BlockSpec((GDN_H, None, GDN_C, GDN_C), lambda n: (0, n, 0, 0))],
        out_shape=[SDS((GDN_H, S, LANES), F32), SDS((GDN_H, nchunk, GDN_DK, LANES), F32), SDS((GDN_H, nchunk, GDN_C, GDN_C), F32)],
        scratch_shapes=[pltpu.VMEM((GDN_H, GDN_DK, LANES), F32)],
        compiler_params=_cp(1),
    )(qkv, gc, bt)


def gdn_core_bwd(qkv, gc, bt, states, invs, do, S):
    nchunk = S // GDN_C

    def body(qkv_ref, g_ref, b_ref, st_ref, inv_ref, do_ref, dqkv_ref, dg_ref, db_ref, ds_scr):
        n = pl.program_id(0)

        @pl.when(n == 0)
        def _():
            ds_scr[...] = jnp.zeros_like(ds_scr)

        heads = range(GDN_H)
        saved = [inv_ref[h] for h in heads]
        _, vjp = jax.vjp(lambda *a: _gdn_chunk(*a, invs=saved)[:2],
                         [qkv_ref[0, h] for h in heads], [qkv_ref[1, h] for h in heads], [qkv_ref[2, h] for h in heads],
                         [g_ref[h] for h in heads], [b_ref[h] for h in heads], [st_ref[h] for h in heads])
        dq, dk, dv, dg, db, dS = vjp(([do_ref[h] for h in heads], [ds_scr[h] for h in heads]))
        for h in heads:
            dqkv_ref[0, h] = dq[h]
            dqkv_ref[1, h] = dk[h]
            dqkv_ref[2, h] = dv[h]
            dg_ref[h] = dg[h]
            db_ref[h] = db[h]
            ds_scr[h] = dS[h]

    rev = lambda n: nchunk - 1 - n
    blk3 = pl.BlockSpec((3, GDN_H, GDN_C, LANES), lambda n: (0, 0, rev(n), 0))
    hb = pl.BlockSpec((GDN_H, GDN_C, LANES), lambda n: (0, rev(n), 0))
    return pl.pallas_call(
        body, name="gdn_core_bwd", grid=(nchunk,),
        in_specs=[blk3, hb, hb, pl.BlockSpec((GDN_H, None, GDN_DK, LANES), lambda n: (0, rev(n), 0, 0)),
                  pl.BlockSpec((GDN_H, None, GDN_C, GDN_C), lambda n: (0, rev(n), 0, 0)), hb],
        out_specs=[blk3, hb, hb],
        out_shape=[SDS((3, GDN_H, S, LANES), F32), SDS((GDN_H, S, LANES), F32), SDS((GDN_H, S, LANES), F32)],
        scratch_shapes=[pltpu.VMEM((GDN_H, GDN_DK, LANES), F32)],
        compiler_params=_cp(1),
    )(qkv, gc, bt, states, invs, do)


GDN_MAIN = 4 * GDN_H * LANES
GDN_PROJ = GDN_MAIN + LANES
RT = 256


def gdn_forward(h, w_in, conv_w, alog, dtb, out_gain, w_out):
    S = h.shape[0]
    nt = S // RT
    proj = matmul(h, w_in, "nn", BF16, "gdn_in")
    qkv = gdn_pre(proj, conv_w, S)
    ab_row = Row(proj, (RT, LANES), lambda i: (i, GDN_MAIN // LANES), gdtype=BF16, gshape=(S, LANES), gimap=lambda i: (i, 0))
    hm = lambda i: (0, i, 0)
    hv = GDN_H * LANES
    gc, bt = rowwise(f_gdn_gates, [ab_row], [alog, dtb],
                     [Out((GDN_H, S, LANES), F32, (GDN_H, RT, LANES), hm, lead=GDN_H)] * 2, (nt,), "gdn_gates")
    o, states, invs = gdn_core(qkv, gc, bt, S)
    o_row = Row(o, (GDN_H, RT, LANES), hm, lead=GDN_H)
    z_row = Row(proj, (RT, hv), lambda i: (i, 3), splits=[LANES] * GDN_H, gdtype=BF16, gshape=(S, hv), gimap=lambda i: (i, 0))
    (on,) = rowwise(f_gdn_post, [o_row, z_row], [out_gain],
                    [Out((S, hv), BF16, (RT, hv), lambda i: (i, 0), splits=[LANES] * GDN_H)], (nt,), "gdn_post")
    y = matmul(on, w_out, "nn", BF16, "gdn_out")
    saved = dict(h=h, proj=proj, qkv=qkv, gc=gc, bt=bt, states=states, invs=invs, o=o, on=on, ab_row=ab_row, o_row=o_row, z_row=z_row)
    return y, saved


def gdn_backward(dy, sv, w_in, conv_w, alog, dtb, out_gain, w_out):
    S = dy.shape[0]
    nt = S // RT
    hm = lambda i: (0, i, 0)
    hv = GDN_H * LANES
    don = matmul(dy, w_out, "nt", BF16, "gdn_out_dx")
    d_w_out = matmul(sv["on"], dy, "tn", F32, "gdn_out_dw")
    (do, dz), (d_gain,) = rowwise_bwd(f_gdn_post, [sv["o_row"], sv["z_row"]], [out_gain],
                                      [Row(don, (RT, hv), lambda i: (i, 0), splits=[LANES] * GDN_H)], (nt,), "gdn_post_bwd")
    dqkv, dgc, dbt = gdn_core_bwd(sv["qkv"], sv["gc"], sv["bt"], sv["states"], sv["invs"], do, S)
    head_blk = lambda a: Row(a, (GDN_H, RT, LANES), hm, lead=GDN_H)
    (dab,), (d_alog, d_dtb) = rowwise_bwd(f_gdn_gates, [sv["ab_row"]], [alog, dtb], [head_blk(dgc), head_blk(dbt)],
                                          (nt,), "gdn_gates_bwd")
    dqkv_proj, d_conv = gdn_pre_bwd(sv["proj"], conv_w, dqkv, S)
    dproj = jnp.concatenate([dqkv_proj, dz, dab], axis=1)
    d_w_in = matmul(sv["h"], dproj, "tn", F32, "gdn_in_dw")
    dh = matmul(dproj, w_in, "nt", BF16, "gdn_in_dx")
    return dh, dict(w_in=d_w_in, conv=d_conv, alog=d_alog, dtb=d_dtb, gain=d_gain, w_out=d_w_out)


QB = DSW_SPAN
N_HP = DSW_HG // LANES
PROJ_BLKS = 3 * 3 * N_HP


def _bucket_maps():
    a = np.arange(QB)[:, None]
    j = np.arange(2 * QB)[None, :]
    dist = QB + a - j
    band = (dist >= 0) & (dist <= DSW_SPAN)
    maps = []
    for _, dil in DSW_GROUPS:
        dd = np.maximum(dist, 0) * dil
        max_exact = REL_BUCKETS // 2
        scaled = np.log(np.maximum(dd, 1).astype(np.float32) / np.float32(max_exact)) / np.float32(math.log(REL_MAX_DIST / max_exact))
        large = max_exact + (scaled * np.float32(REL_BUCKETS - max_exact)).astype(np.int32)
        large = np.minimum(large, REL_BUCKETS - 1)
        maps.append(np.where(dd < max_exact, dd, large).astype(np.int32))
    return np.stack(maps), band


def dsw_bias(rel_bias):
    maps, band = _bucket_maps()
    maps = np.where(band[None], maps, -1).astype(np.int32)

    def body(tab_ref, bk_ref, o_ref):
        gh = pl.program_id(0)
        bk = bk_ref[...]
        acc = jnp.full(bk.shape, NEG, F32)
        for b in range(REL_BUCKETS):
            acc = jnp.where(bk == b, tab_ref[b, gh], acc)
        o_ref[...] = acc

    return pl.pallas_call(
        body, name="dsw_bias", grid=(3 * GDN_H,),
        in_specs=[pl.BlockSpec(memory_space=pltpu.SMEM),
                  pl.BlockSpec((None, QB, 2 * QB), lambda gh: (gh // GDN_H, 0, 0))],
        out_specs=pl.BlockSpec((None, QB, 2 * QB), lambda gh: (gh, 0, 0)),
        out_shape=SDS((3 * GDN_H, QB, 2 * QB), F32),
        compiler_params=_cp(1),
    )(rel_bias, jnp.asarray(maps))


def dsw_bias_grad(dbias):
    maps, band = _bucket_maps()
    maps = np.where(band[None], maps, -1).astype(np.int32)

    def body(d_ref, bk_ref, o_ref):
        bk = bk_ref[...]
        d = d_ref[...]
        rows = lax.broadcasted_iota(jnp.int32, (REL_BUCKETS, LANES), 0)
        acc = jnp.zeros((REL_BUCKETS, LANES), F32)
        for b in range(REL_BUCKETS):
            part = jnp.sum(jnp.where(bk == b, d, 0.0), axis=0, keepdims=True)
            val = jnp.sum(part, axis=1, keepdims=True)
            acc = jnp.where(rows == b, val, acc)
        o_ref[...] = acc

    return pl.pallas_call(
        body, name="dsw_bias_grad", grid=(3 * GDN_H,),
        in_specs=[pl.BlockSpec((None, QB, 2 * QB), lambda gh: (gh, 0, 0)),
                  pl.BlockSpec((None, QB, 2 * QB), lambda gh: (gh // GDN_H, 0, 0))],
        out_specs=pl.BlockSpec((None, REL_BUCKETS, LANES), lambda gh: (gh, 0, 0)),
        out_shape=SDS((3 * GDN_H, REL_BUCKETS, LANES), F32),
        compiler_params=_cp(1),
    )(dbias, jnp.asarray(maps))


def _nt(a, b):
    return lax.dot_general(a, b, (((1,), (1,)), ((), ())), preferred_element_type=F32)


def _tn(a, b):
    return lax.dot_general(a, b, (((0,), (0,)), ((), ())), preferred_element_type=F32)


def dsw_group_fwd(qn, kn, proj, bias, gi, S):
    dil = DSW_GROUPS[gi][1]
    sd = S // dil
    nq = sd // QB
    qv = qn.reshape(sd, dil * 3 * DSW_HG)
    kv = kn.reshape(sd, dil * 3 * DSW_HG)
    pv = proj.reshape(sd, dil * 9 * DSW_HG)
    qk_col = lambda hp, r: r * (3 * N_HP) + gi * N_HP + hp
    v_col = lambda hp, r: r * PROJ_BLKS + 2 * 3 * N_HP + gi * N_HP + hp

    def body(q_ref, kp_ref, kc_ref, vp_ref, vc_ref, b_ref, o_ref, l_ref):
        i = pl.program_id(2)
        q = q_ref[...]
        k2 = jnp.concatenate([kp_ref[...], kc_ref[...]], axis=0)
        v2 = jnp.concatenate([vp_ref[...], vc_ref[...]], axis=0).astype(BF16)
        lane_q = lax.broadcasted_iota(jnp.int32, (QB, LANES), 1) < DSW_DH
        lane_k = lax.broadcasted_iota(jnp.int32, (2 * QB, LANES), 1) < DSW_DH
        col = lax.broadcasted_iota(jnp.int32, (QB, 2 * QB), 1)
        first = jnp.logical_and(i == 0, col < QB)
        o_acc = jnp.zeros((QB, LANES), F32)
        lse_b = jnp.zeros((QB, LANES), F32)
        for hh in range(2):
            mq = lane_q if hh == 0 else jnp.logical_not(lane_q)
            mk = lane_k if hh == 0 else jnp.logical_not(lane_k)
            s = _nt(jnp.where(mq, q, 0).astype(BF16), k2) + b_ref[hh]
            s = jnp.where(first, NEG, s)
            mx = jnp.max(s, axis=1, keepdims=True)
            p = jnp.exp(s - mx)
            l = jnp.sum(p, axis=1, keepdims=True)
            oh = jnp.dot(p.astype(BF16), jnp.where(mk, v2, 0).astype(BF16), preferred_element_type=F32) / l
            o_acc = o_acc + oh
            lse_b = jnp.where(mq, mx + jnp.log(l), lse_b)
        o_ref[...] = o_acc
        l_ref[...] = lse_b

    blk = (QB, LANES)
    out_spec = pl.BlockSpec(blk, lambda hp, r, i: (i, r * N_HP + hp))
    o, lse = pl.pallas_call(
        body, name=f"dsw_fwd_g{gi}", grid=(N_HP, dil, nq),
        in_specs=[pl.BlockSpec(blk, lambda hp, r, i: (i, qk_col(hp, r))),
                  pl.BlockSpec(blk, lambda hp, r, i: (jnp.maximum(i - 1, 0), qk_col(hp, r))),
                  pl.BlockSpec(blk, lambda hp, r, i: (i, qk_col(hp, r))),
                  pl.BlockSpec(blk, lambda hp, r, i: (jnp.maximum(i - 1, 0), v_col(hp, r))),
                  pl.BlockSpec(blk, lambda hp, r, i: (i, v_col(hp, r))),
                  pl.BlockSpec((2, QB, 2 * QB), lambda hp, r, i: (gi * N_HP + hp, 0, 0))],
        out_specs=[out_spec, out_spec],
        out_shape=[SDS((sd, dil * DSW_HG), F32)] * 2,
        compiler_params=_cp(3),
    )(qv, kv, kv, pv, pv, bias)
    return o.reshape(S, DSW_HG), lse.reshape(S, DSW_HG)


def dsw_group_bwd(qn, kn, proj, bias, do, o, lse, gi, S):
    dil = DSW_GROUPS[gi][1]
    sd = S // dil
    nq = sd // QB
    qv = qn.reshape(sd, dil * 3 * DSW_HG)
    kv = kn.reshape(sd, dil * 3 * DSW_HG)
    pv = proj.reshape(sd, dil * 9 * DSW_HG)
    dov = do.reshape(sd, dil * DSW_HG)
    ov = o.reshape(sd, dil * DSW_HG)
    lv = lse.reshape(sd, dil * DSW_HG)
    qk_col = lambda hp, r: r * (3 * N_HP) + gi * N_HP + hp
    v_col = lambda hp, r: r * PROJ_BLKS + 2 * 3 * N_HP + gi * N_HP + hp
    o_col = lambda hp, r: r * N_HP + hp
    cur = lambda i: jnp.minimum(i, nq - 1)
    prev = lambda i: jnp.maximum(jnp.minimum(i, nq - 1) - 1, 0)
    done = lambda i: jnp.maximum(i - 1, 0)

    def body(q_ref, kp_ref, kc_ref, vp_ref, vc_ref, b_ref, do_ref, o_ref, l_ref,
             dq_ref, dk_ref, dv_ref, db_ref, dk_scr, dv_scr):
        r, i = pl.program_id(1), pl.program_id(2)

        @pl.when(jnp.logical_and(r == 0, i == 0))
        def _():
            db_ref[...] = jnp.zeros_like(db_ref)

        @pl.when(i == 0)
        def _():
            dk_scr[...] = jnp.zeros_like(dk_scr)
            dv_scr[...] = jnp.zeros_like(dv_scr)

        @pl.when(i < nq)
        def _():
            q = q_ref[...]
            k2 = jnp.concatenate([kp_ref[...], kc_ref[...]], axis=0)
            v2 = jnp.concatenate([vp_ref[...], vc_ref[...]], axis=0).astype(BF16)
            dout = do_ref[...].astype(F32)
            prod = dout * o_ref[...].astype(F32)
            lse_b = l_ref[...]
            lane_q = lax.broadcasted_iota(jnp.int32, (QB, LANES), 1) < DSW_DH
            col = lax.broadcasted_iota(jnp.int32, (QB, 2 * QB), 1)
            first = jnp.logical_and(i == 0, col < QB)
            dq = jnp.zeros((QB, LANES), F32)
            dk2 = jnp.zeros((2 * QB, LANES), F32)
            dv2 = jnp.zeros((2 * QB, LANES), F32)
            for hh in range(2):
                mq = lane_q if hh == 0 else jnp.logical_not(lane_q)
                qm = jnp.where(mq, q, 0).astype(BF16)
                dom = jnp.where(mq, dout, 0.0).astype(BF16)
                s = _nt(qm, k2) + b_ref[hh]
                s = jnp.where(first, NEG, s)
                lse_h = jnp.max(jnp.where(mq, lse_b, NEG), axis=1, keepdims=True)
                p = jnp.exp(s - lse_h)
                delta = jnp.sum(jnp.where(mq, prod, 0.0), axis=1, keepdims=True)
                dp = _nt(dom, v2)
                ds = p * (dp - delta)
                dsb = ds.astype(BF16)
                dq = dq + jnp.where(mq, jnp.dot(dsb, k2, preferred_element_type=F32), 0.0)
                dk2 = dk2 + _tn(dsb, qm)
                dv2 = dv2 + _tn(p.astype(BF16), dom)
                db_ref[hh] += ds
            dq_ref[...] = dq
            dk_ref[...] = dk_scr[...] + dk2[:QB]
            dv_ref[...] = (dv_scr[...] + dv2[:QB]).astype(dv_ref.dtype)
            dk_scr[...] = dk2[QB:]
            dv_scr[...] = dv2[QB:]

        @pl.when(i == nq)
        def _():
            dk_ref[...] = dk_scr[...]
            dv_ref[...] = dv_scr[...].astype(dv_ref.dtype)

    blk = (QB, LANES)
    dq, dk, dv, dbias = pl.pallas_call(
        body, name=f"dsw_bwd_g{gi}", grid=(N_HP, dil, nq + 1),
        in_specs=[pl.BlockSpec(blk, lambda hp, r, i: (cur(i), qk_col(hp, r))),
                  pl.BlockSpec(blk, lambda hp, r, i: (prev(i), qk_col(hp, r))),
                  pl.BlockSpec(blk, lambda hp, r, i: (cur(i), qk_col(hp, r))),
                  pl.BlockSpec(blk, lambda hp, r, i: (prev(i), v_col(hp, r))),
                  pl.BlockSpec(blk, lambda hp, r, i: (cur(i), v_col(hp, r))),
                  pl.BlockSpec((2, QB, 2 * QB), lambda hp, r, i: (gi * N_HP + hp, 0, 0)),
                  pl.BlockSpec(blk, lambda hp, r, i: (cur(i), o_col(hp, r))),
                  pl.BlockSpec(blk, lambda hp, r, i: (cur(i), o_col(hp, r))),
                  pl.BlockSpec(blk, lambda hp, r, i: (cur(i), o_col(hp, r)))],
        out_specs=[pl.BlockSpec(blk, lambda hp, r, i: (cur(i), o_col(hp, r))),
                   pl.BlockSpec(blk, lambda hp, r, i: (done(i), o_col(hp, r))),
                   pl.BlockSpec(blk, lambda hp, r, i: (done(i), o_col(hp, r))),
                   pl.BlockSpec((2, QB, 2 * QB), lambda hp, r, i: (hp, 0, 0))],
        out_shape=[SDS((sd, dil * DSW_HG), F32), SDS((sd, dil * DSW_HG), F32), SDS((sd, dil * DSW_HG), BF16),
                   SDS((GDN_H, QB, 2 * QB), F32)],
        scratch_shapes=[pltpu.VMEM(blk, F32), pltpu.VMEM(blk, F32)],
        compiler_params=_cp(3),
    )(qv, kv, kv, pv, pv, bias, dov, ov, lv)
    return dq.reshape(S, DSW_HG), dk.reshape(S, DSW_HG), dv.reshape(S, DSW_HG), dbias


def dsw_forward(h, w_in, q_gain2, k_gain2, rel_bias, w_out):
    S = h.shape[0]
    nt = S // RT
    nb = 3 * N_HP
    proj = matmul(h, w_in, "nn", F32, "dsw_in")
    width = nb * LANES
    lanes12 = [LANES] * nb
    (qn,) = rowwise(f_qnorm, [Row(proj, (RT, width), lambda i: (i, 0), splits=lanes12)], [q_gain2],
                    [Out((S, width), BF16, (RT, width), lambda i: (i, 0), splits=lanes12)], (nt,), "dsw_qnorm")
    (kn,) = rowwise(f_qknorm, [Row(proj, (RT, width), lambda i: (i, 1), splits=lanes12)], [k_gain2],
                    [Out((S, width), BF16, (RT, width), lambda i: (i, 0), splits=lanes12)], (nt,), "dsw_knorm")
    bias = dsw_bias(rel_bias)
    os_, ls_ = [], []
    for gi in range(3):
        o, l = dsw_group_fwd(qn, kn, proj, bias, gi, S)
        os_.append(o)
        ls_.append(l)
    full = lambda a: Row(a, (RT, DSW_HG), lambda i: (i, 0))
    o, lse = rowwise(f_combine, [full(a) for a in os_ + ls_], [],
                     [Out((S, DSW_HG), BF16, (RT, DSW_HG), lambda i: (i, 0)), Out((S, DSW_HG), F32, (RT, DSW_HG), lambda i: (i, 0))],
                     (nt,), "dsw_combine")
    y = matmul(o, w_out, "nn", F32, "dsw_out")
    return y, dict(h=h, proj=proj, qn=qn, kn=kn, bias=bias, o=o, lse=lse)


def dsw_backward(dy, sv, w_in, q_gain2, k_gain2, w_out):
    S = dy.shape[0]
    nt = S // RT
    nb = 3 * N_HP
    do = matmul(dy, w_out, "nt", BF16, "dsw_out_dx")
    d_w_out = matmul(sv["o"], dy, "tn", F32, "dsw_out_dw")
    pieces_q, pieces_k, pieces_v, dbs = [], [], [], []
    d_qg = jnp.zeros((1, LANES), F32)
    d_kg = jnp.zeros((1, LANES), F32)
    for gi in range(3):
        dq, dk, dv, db = dsw_group_bwd(sv["qn"], sv["kn"], sv["proj"], sv["bias"], do, sv["o"], sv["lse"], gi, S)
        dbs.append(db)
        pieces_v.append(dv)
        for which, dd in ((0, dq), (1, dk)):
            lanes4 = [LANES] * N_HP
            row = Row(sv["proj"], (RT, DSW_HG), lambda i, _o=which * 3 + gi: (i, _o), splits=lanes4,
                      gdtype=BF16, gshape=(S, DSW_HG), gimap=lambda i: (i, 0))
            fn, gain = (f_qnorm, q_gain2) if which == 0 else (f_qknorm, k_gain2)
            (dx,), (dg,) = rowwise_bwd(fn, [row], [gain], [Row(dd, (RT, DSW_HG), lambda i: (i, 0), splits=lanes4)],
                                       (nt,), f"dsw_norm_bwd_{which}{gi}")
            if which == 0:
                pieces_q.append(dx)
                d_qg = d_qg + dg
            else:
                pieces_k.append(dx)
                d_kg = d_kg + dg
    dproj = jnp.concatenate(pieces_q + pieces_k + pieces_v, axis=1)
    d_w_in = matmul(sv["h"], dproj, "tn", F32, "dsw_in_dw")
    dh = matmul(dproj, w_in, "nt", F32, "dsw_in_dx")
    d_rel = dsw_bias_grad(jnp.concatenate(dbs, axis=0))
    return dh, dict(w_in=d_w_in, q_gain2=d_qg, k_gain2=d_kg, rel=d_rel, w_out=d_w_out)


N_LB = DSW_HG // LANES
HALF = DSW_DH // 2


def _lanes(j):
    return slice(LANES * j, LANES * (j + 1))


def _deinterleave(stage, out_ref, dil, rows, dtype):
    for r in range(dil):
        for j in range(N_LB):
            out_ref[r, :, _lanes(j)] = stage[j, pl.ds(r, rows, stride=dil), :].astype(dtype)


def _interleave(in_ref, stage, dil, rows):
    for r in range(dil):
        for j in range(N_LB):
            stage[j, pl.ds(r, rows, stride=dil), :] = in_ref[r, :, _lanes(j)].astype(F32)


def dsw_prep(proj, q_gain2, k_gain2, gi, S):
    dil = DSW_GROUPS[gi][1]
    nt, rows = S // RT, RT // dil

    def body(q_ref, k_ref, v_ref, qg_ref, kg_ref, qo_ref, ko_ref, vo_ref, stage):
        for src, gain_ref, scale, dst in ((q_ref, qg_ref, DSW_DH ** -0.5, qo_ref), (k_ref, kg_ref, 1.0, ko_ref), (v_ref, None, None, vo_ref)):
            for j in range(N_LB):
                val = src[:, _lanes(j)].astype(F32)
                stage[j] = val if gain_ref is None else _qknorm1(val, gain_ref[...], scale)
            _deinterleave(stage, dst, dil, rows, BF16)

    col = lambda which: pl.BlockSpec((RT, DSW_HG), lambda i, _c=which * 3 + gi: (i, _c))
    gspec = pl.BlockSpec((1, LANES), lambda i: (0, 0))
    ospec = pl.BlockSpec((dil, rows, DSW_HG), lambda i: (0, i, 0))
    return pl.pallas_call(
        body, name=f"dsw_prep_g{gi}", grid=(nt,),
        in_specs=[col(0), col(1), col(2), gspec, gspec], out_specs=[ospec] * 3,
        out_shape=[SDS((dil, S // dil, DSW_HG), BF16)] * 3,
        scratch_shapes=[pltpu.VMEM((N_LB, RT, LANES), F32)], compiler_params=_cp(1),
    )(proj, proj, proj, q_gain2, k_gain2)


def dsw_prep_bwd(proj, q_gain2, k_gain2, dqd, dkd, dvd, gi, S):
    dil = DSW_GROUPS[gi][1]
    nt, rows = S // RT, RT // dil

    def body(q_ref, k_ref, qg_ref, kg_ref, dq_ref, dk_ref, dv_ref, oq_ref, ok_ref, ov_ref, dqg_ref, dkg_ref, stage):
        i = pl.program_id(0)

        @pl.when(i == 0)
        def _():
            dqg_ref[...] = jnp.zeros_like(dqg_ref)
            dkg_ref[...] = jnp.zeros_like(dkg_ref)

        for src, gain_ref, scale, cot_ref, dst, dg_ref in ((q_ref, qg_ref, DSW_DH ** -0.5, dq_ref, oq_ref, dqg_ref),
                                                          (k_ref, kg_ref, 1.0, dk_ref, ok_ref, dkg_ref)):
            _interleave(cot_ref, stage, dil, rows)
            for j in range(N_LB):
                _, vjp = jax.vjp(lambda x, g, _s=scale: _qknorm1(x, g, _s), src[:, _lanes(j)].astype(F32), gain_ref[...])
                dx, dg = vjp(stage[j])
                dst[:, _lanes(j)] = dx.astype(dst.dtype)
                dg_ref[...] += dg
        _interleave(dv_ref, stage, dil, rows)
        for j in range(N_LB):
            ov_ref[:, _lanes(j)] = stage[j].astype(ov_ref.dtype)

    col = lambda which: pl.BlockSpec((RT, DSW_HG), lambda i, _c=which * 3 + gi: (i, _c))
    gspec = pl.BlockSpec((1, LANES), lambda i: (0, 0))
    dspec = pl.BlockSpec((dil, rows, DSW_HG), lambda i: (0, i, 0))
    nspec = pl.BlockSpec((RT, DSW_HG), lambda i: (i, 0))
    return pl.pallas_call(
        body, name=f"dsw_prep_bwd_g{gi}", grid=(nt,),
        in_specs=[col(0), col(1), gspec, gspec, dspec, dspec, dspec], out_specs=[nspec] * 3 + [gspec] * 2,
        out_shape=[SDS((S, DSW_HG), BF16)] * 3 + [SDS((1, LANES), F32)] * 2,
        scratch_shapes=[pltpu.VMEM((N_LB, RT, LANES), F32)], compiler_params=_cp(1),
    )(proj, proj, q_gain2, k_gain2, dqd, dkd, dvd)


def _head_masks(rows):
    lane = lax.broadcasted_iota(jnp.int32, (rows, LANES), 1)
    return lane < DSW_DH, (lane % DSW_DH) < HALF


def dsw_attn_fwd(qd, kd, vd, bias, gi, S):
    dil = DSW_GROUPS[gi][1]
    sd = S // dil
    nq = sd // QB

    def body(q_ref, k_ref, v_ref, b_ref, o_ref, l_ref, kp_scr, vp_scr):
        i = pl.program_id(1)

        @pl.when(i == 0)
        def _():
            kp_scr[...] = jnp.zeros_like(kp_scr)
            vp_scr[...] = jnp.zeros_like(vp_scr)

        lo_q, _ = _head_masks(QB)
        lo_k, _ = _head_masks(2 * QB)
        col = lax.broadcasted_iota(jnp.int32, (QB, 2 * QB), 1)
        first = jnp.logical_and(i == 0, col < QB)
        hps, heads = range(N_HP), range(2 * N_HP)
        k2s = [jnp.concatenate([kp_scr[:, _lanes(hp)], k_ref[:, _lanes(hp)]], axis=0) for hp in hps]
        v2s = [jnp.concatenate([vp_scr[:, _lanes(hp)], v_ref[:, _lanes(hp)]], axis=0) for hp in hps]
        qs = [q_ref[:, _lanes(hp)] for hp in hps]
        k_now, v_now = k_ref[...], v_ref[...]
        mqs = [lo_q if h % 2 == 0 else jnp.logical_not(lo_q) for h in heads]
        mks = [lo_k if h % 2 == 0 else jnp.logical_not(lo_k) for h in heads]
        ss = [jnp.where(first, NEG, _nt(jnp.where(mqs[h], qs[h // 2], 0).astype(BF16), k2s[h // 2]) + b_ref[h]) for h in heads]
        mxs = [jnp.max(s, axis=1, keepdims=True) for s in ss]
        ps = [jnp.exp(s - mx) for s, mx in zip(ss, mxs)]
        ls = [jnp.sum(p, axis=1, keepdims=True) for p in ps]
        ohs = [jnp.dot(ps[h].astype(BF16), jnp.where(mks[h], v2s[h // 2], 0).astype(BF16), preferred_element_type=F32) / ls[h] for h in heads]
        lse_h = [mx + jnp.log(l) for mx, l in zip(mxs, ls)]
        for hp in hps:
            o_ref[:, _lanes(hp)] = ohs[2 * hp] + ohs[2 * hp + 1]
            l_ref[:, _lanes(hp)] = jnp.where(lo_q, lse_h[2 * hp], lse_h[2 * hp + 1])
        kp_scr[...] = k_now
        vp_scr[...] = v_now

    blk = pl.BlockSpec((None, QB, DSW_HG), lambda r, i: (r, i, 0))
    return pl.pallas_call(
        body, name=f"dsw_attn_g{gi}", grid=(dil, nq),
        in_specs=[blk, blk, blk, pl.BlockSpec((GDN_H, QB, 2 * QB), lambda r, i: (gi, 0, 0))],
        out_specs=[blk, blk], out_shape=[SDS((dil, sd, DSW_HG), F32)] * 2,
        scratch_shapes=[pltpu.VMEM((QB, DSW_HG), BF16)] * 2, compiler_params=_cp(2),
    )(qd, kd, vd, bias)


def dsw_attn_bwd(qd, kd, vd, bias, dod, statd, gi, S):
    dil = DSW_GROUPS[gi][1]
    sd = S // dil
    nq = sd // QB
    cur = lambda i: jnp.minimum(i, nq - 1)
    done = lambda i: jnp.maximum(i - 1, 0)

    def body(q_ref, k_ref, v_ref, b_ref, do_ref, st_ref, dq_ref, dk_ref, dv_ref, db_ref, kp_scr, vp_scr, dk_scr, dv_scr):
        r, i = pl.program_id(0), pl.program_id(1)

        @pl.when(jnp.logical_and(r == 0, i == 0))
        def _():
            db_ref[...] = jnp.zeros_like(db_ref)

        @pl.when(i == 0)
        def _():
            for scr in (kp_scr, vp_scr, dk_scr, dv_scr):
                scr[...] = jnp.zeros_like(scr)

        @pl.when(i < nq)
        def _():
            lo_q, first_half = _head_masks(QB)
            col = lax.broadcasted_iota(jnp.int32, (QB, 2 * QB), 1)
            first = jnp.logical_and(i == 0, col < QB)
            hps, heads = range(N_HP), range(2 * N_HP)
            k2s = [jnp.concatenate([kp_scr[:, _lanes(hp)], k_ref[:, _lanes(hp)]], axis=0) for hp in hps]
            v2s = [jnp.concatenate([vp_scr[:, _lanes(hp)], v_ref[:, _lanes(hp)]], axis=0) for hp in hps]
            qs = [q_ref[:, _lanes(hp)] for hp in hps]
            douts = [do_ref[:, _lanes(hp)] for hp in hps]
            stats = [st_ref[:, _lanes(hp)] for hp in hps]
            dkc = [dk_scr[:, _lanes(hp)] for hp in hps]
            dvc = [dv_scr[:, _lanes(hp)] for hp in hps]
            k_now, v_now = k_ref[...], v_ref[...]
            mqs = [lo_q if h % 2 == 0 else jnp.logical_not(lo_q) for h in heads]
            qms = [jnp.where(mqs[h], qs[h // 2], 0).astype(BF16) for h in heads]
            doms = [jnp.where(mqs[h], douts[h // 2], 0).astype(BF16) for h in heads]
            ss = [jnp.where(first, NEG, _nt(qms[h], k2s[h // 2]) + b_ref[h]) for h in heads]
            lses = [jnp.max(jnp.where(jnp.logical_and(mqs[h], first_half), stats[h // 2], NEG), axis=1, keepdims=True) for h in heads]
            deltas = [jnp.max(jnp.where(jnp.logical_and(mqs[h], jnp.logical_not(first_half)), stats[h // 2], NEG), axis=1, keepdims=True)
                      for h in heads]
            ps = [jnp.exp(ss[h] - lses[h]) for h in heads]
            dss = [ps[h] * (_nt(doms[h], v2s[h // 2]) - deltas[h]) for h in heads]
            dsbs = [d.astype(BF16) for d in dss]
            dqh = [jnp.where(mqs[h], jnp.dot(dsbs[h], k2s[h // 2], preferred_element_type=F32), 0.0) for h in heads]
            dkh = [_tn(dsbs[h], qms[h]) for h in heads]
            dvh = [_tn(ps[h].astype(BF16), doms[h]) for h in heads]
            for h in heads:
                db_ref[h] += dss[h]
            for hp in hps:
                dk2 = dkh[2 * hp] + dkh[2 * hp + 1]
                dv2 = dvh[2 * hp] + dvh[2 * hp + 1]
                dq_ref[:, _lanes(hp)] = dqh[2 * hp] + dqh[2 * hp + 1]
                dk_ref[:, _lanes(hp)] = dkc[hp] + dk2[:QB]
                dv_ref[:, _lanes(hp)] = (dvc[hp] + dv2[:QB]).astype(dv_ref.dtype)
                dk_scr[:, _lanes(hp)] = dk2[QB:]
                dv_scr[:, _lanes(hp)] = dv2[QB:]
            kp_scr[...] = k_now
            vp_scr[...] = v_now

        @pl.when(i == nq)
        def _():
            dk_ref[...] = dk_scr[...]
            dv_ref[...] = dv_scr[...].astype(dv_ref.dtype)

    blk = pl.BlockSpec((None, QB, DSW_HG), lambda r, i: (r, cur(i), 0))
    oblk = pl.BlockSpec((None, QB, DSW_HG), lambda r, i: (r, done(i), 0))
    return pl.pallas_call(
        body, name=f"dsw_attn_bwd_g{gi}", grid=(dil, nq + 1),
        in_specs=[blk, blk, blk, pl.BlockSpec((GDN_H, QB, 2 * QB), lambda r, i: (gi, 0, 0)), blk, blk],
        out_specs=[blk, oblk, oblk, pl.BlockSpec((GDN_H, QB, 2 * QB), lambda r, i: (0, 0, 0))],
        out_shape=[SDS((dil, sd, DSW_HG), F32), SDS((dil, sd, DSW_HG), F32), SDS((dil, sd, DSW_HG), BF16),
                   SDS((GDN_H, QB, 2 * QB), F32)],
        scratch_shapes=[pltpu.VMEM((QB, DSW_HG), BF16)] * 2 + [pltpu.VMEM((QB, DSW_HG), F32)] * 2,
        compiler_params=_cp(2),
    )(qd, kd, vd, bias, dod, statd)


def dsw_combine(ods, lseds, S):
    nt = S // RT
    dils = [d for _, d in DSW_GROUPS]

    def body(*refs):
        ins, (o_ref, l_ref), stages = refs[:6], refs[6:8], refs[8:]
        for g in range(3):
            _interleave(ins[g], stages[g], dils[g], RT // dils[g])
            _interleave(ins[3 + g], stages[3 + g], dils[g], RT // dils[g])
        for j in range(N_LB):
            o, lse = f_combine(None, *[st[j] for st in stages])
            o_ref[:, _lanes(j)] = o.astype(o_ref.dtype)
            l_ref[:, _lanes(j)] = lse

    dspec = lambda d: pl.BlockSpec((d, RT // d, DSW_HG), lambda i: (0, i, 0))
    nspec = pl.BlockSpec((RT, DSW_HG), lambda i: (i, 0))
    return pl.pallas_call(
        body, name="dsw_combine", grid=(nt,),
        in_specs=[dspec(d) for d in dils] * 2, out_specs=[nspec, nspec],
        out_shape=[SDS((S, DSW_HG), BF16), SDS((S, DSW_HG), F32)],
        scratch_shapes=[pltpu.VMEM((N_LB, RT, LANES), F32)] * 6, compiler_params=_cp(1),
    )(*ods, *lseds)


def dsw_bwd_prep(do, o, lse, S):
    nt = S // RT
    dils = [d for _, d in DSW_GROUPS]

    def body(do_ref, o_ref, l_ref, *rest):
        outs, (st_do, st_stat) = rest[:6], rest[6:]
        lo, first_half = _head_masks(RT)
        for j in range(N_LB):
            dout = do_ref[:, _lanes(j)]
            prod = dout * o_ref[:, _lanes(j)].astype(F32)
            s_all = jnp.sum(prod, axis=1, keepdims=True)
            s_lo = jnp.sum(jnp.where(lo, prod, 0.0), axis=1, keepdims=True)
            delta = jnp.where(lo, s_lo, s_all - s_lo)
            st_do[j] = dout
            st_stat[j] = jnp.where(first_half, l_ref[:, _lanes(j)], delta)
        for g in range(3):
            _deinterleave(st_do, outs[g], dils[g], RT // dils[g], BF16)
            _deinterleave(st_stat, outs[3 + g], dils[g], RT // dils[g], F32)

    nspec = pl.BlockSpec((RT, DSW_HG), lambda i: (i, 0))
    dspec = lambda d: pl.BlockSpec((d, RT // d, DSW_HG), lambda i: (0, i, 0))
    res = pl.pallas_call(
        body, name="dsw_bwd_prep", grid=(nt,),
        in_specs=[nspec] * 3, out_specs=[dspec(d) for d in dils] * 2,
        out_shape=[SDS((d, S // d, DSW_HG), BF16) for d in dils] + [SDS((d, S // d, DSW_HG), F32) for d in dils],
        scratch_shapes=[pltpu.VMEM((N_LB, RT, LANES), F32)] * 2, compiler_params=_cp(1),
    )(do, o, lse)
    return res[:3], res[3:]


def dsw_forward(h, w_in, q_gain2, k_gain2, rel_bias, w_out):
    S = h.shape[0]
    proj = matmul(h, w_in, "nn", BF16, "dsw_in", col_shards=N_SHARD)
    bias = dsw_bias(rel_bias)
    qkv, ods, lseds = [], [], []
    for gi in range(3):
        qd, kd, vd = dsw_prep(proj, q_gain2, k_gain2, gi, S)
        od, ld = dsw_attn_fwd(qd, kd, vd, bias, gi, S)
        qkv.append((qd, kd, vd))
        ods.append(od)
        lseds.append(ld)
    o, lse = dsw_combine(ods, lseds, S)
    y = matmul(o, w_out, "nn", BF16, "dsw_out", col_shards=N_SHARD)
    return y, dict(h=h, proj=proj, qkv=qkv, bias=bias, o=o, lse=lse)


def dsw_backward(dy, sv, w_in, q_gain2, k_gain2, w_out):
    S = dy.shape[0]
    do = matmul(dy, w_out, "nt", F32, "dsw_out_dx", col_shards=N_SHARD)
    d_w_out = matmul(sv["o"], dy, "tn", F32, "dsw_out_dw", col_shards=N_SHARD)
    dods, statds = dsw_bwd_prep(do, sv["o"], sv["lse"], S)
    pieces_q, pieces_k, pieces_v, dbs = [], [], [], []
    d_qg = jnp.zeros((1, LANES), F32)
    d_kg = jnp.zeros((1, LANES), F32)
    for gi in range(3):
        qd, kd, vd = sv["qkv"][gi]
        dqd, dkd, dvd, db = dsw_attn_bwd(qd, kd, vd, sv["bias"], dods[gi], statds[gi], gi, S)
        dq, dk, dv, dqg, dkg = dsw_prep_bwd(sv["proj"], q_gain2, k_gain2, dqd, dkd, dvd, gi, S)
        dbs.append(db)
        pieces_q.append(dq)
        pieces_k.append(dk)
        pieces_v.append(dv)
        d_qg = d_qg + dqg
        d_kg = d_kg + dkg
    dproj = jnp.concatenate(pieces_q + pieces_k + pieces_v, axis=1)
    d_w_in = matmul(sv["h"], dproj, "tn", F32, "dsw_in_dw", col_shards=N_SHARD)
    dh = matmul(dproj, w_in, "nt", BF16, "dsw_in_dx", col_shards=N_SHARD)
    d_rel = dsw_bias_grad(jnp.concatenate(dbs, axis=0))
    return dh, dict(w_in=d_w_in, q_gain2=d_qg, k_gain2=d_kg, rel=d_rel, w_out=d_w_out)


FT = 128


def ffn_forward(h, w_in, w_out, tag):
    S = h.shape[0]
    gu = matmul(h, w_in, "nn", BF16, f"ffn_in_{tag}", col_shards=N_SHARD)
    gu_row = Row(gu, (FT, 2 * FFN), lambda i: (i, 0), splits=[FFN, FFN], gdtype=BF16)
    (a,) = rowwise(f_swiglu, [gu_row], [], [Out((S, FFN), BF16, (FT, FFN), lambda i: (i, 0))], (S // FT,), f"ffn_act_{tag}")
    f = matmul(a, w_out, "nn", BF16, f"ffn_out_{tag}")
    return f, dict(h=h, gu_row=gu_row, a=a)


def ffn_backward(df, sv, w_in, w_out, tag):
    S = df.shape[0]
    da = matmul(df, w_out, "nt", BF16, f"ffn_out_dx_{tag}")
    d_w_out = matmul(sv["a"], df, "tn", F32, f"ffn_out_dw_{tag}")
    (dgu,), _ = rowwise_bwd(f_swiglu, [sv["gu_row"]], [], [Row(da, (FT, FFN), lambda i: (i, 0))], (S // FT,), f"ffn_act_bwd_{tag}")
    d_w_in = matmul(sv["h"], dgu, "tn", F32, f"ffn_in_dw_{tag}", col_shards=N_SHARD)
    dh = matmul(dgu, w_in, "nt", BF16, f"ffn_in_dx_{tag}", col_shards=N_SHARD)
    return dh, d_w_in, d_w_out


def f_norm_only(ids, x, gain, sc, sh):
    return (_normmod(x, gain, sc, sh),)


def _wide(a, **kw):
    return Row(a, (RT, D), lambda i: (i, 0), **kw)


def _wide_out(S, dtype):
    return Out((S, D), dtype, (RT, D), lambda i: (i, 0))


def adamw(w, g, m, v, name):
    shape = w.shape
    C = shape[-1]
    R = int(np.prod(shape[:-1]))
    w2, g2, m2, v2 = (a.reshape(R, C) for a in (w, g, m, v))
    br = R
    if R > 256:
        br = max(b for b in range(8, 257, 8) if R % b == 0)
    c1 = 1.0 / (1.0 - ADAM_B1 ** ADAM_STEP)
    c2 = 1.0 / (1.0 - ADAM_B2 ** ADAM_STEP)

    def body(w_ref, g_ref, m_ref, v_ref, d_ref, nm_ref, nv_ref):
        gg = g_ref[...]
        mm_ = ADAM_B1 * m_ref[...] + (1.0 - ADAM_B1) * gg
        vv = ADAM_B2 * v_ref[...] + (1.0 - ADAM_B2) * (gg * gg)
        d_ref[...] = -ADAM_LR * ((mm_ * c1) / (jnp.sqrt(vv * c2) + ADAM_EPS) + ADAM_WD * w_ref[...])
        nm_ref[...] = mm_
        nv_ref[...] = vv

    spec = pl.BlockSpec((br, C), lambda i: (i, 0))
    d, nm, nv = pl.pallas_call(
        body, name=name, grid=(R // br,), in_specs=[spec] * 4, out_specs=[spec] * 3,
        out_shape=[SDS((R, C), F32)] * 3, compiler_params=_cp(1),
    )(w2, g2, m2, v2)
    return d.reshape(shape), nm.reshape(shape), nv.reshape(shape)


def _place():
    x, y, c = lax.axis_index("x"), lax.axis_index("y"), lax.axis_index("c")
    chips = [(1 - x, y), (x, 1 - y), (1 - x, 1 - y)]
    return x, y, c, chips


def all_gather_small(blk, name):
    m_per, n = blk.shape

    def body(x_ref, out_ref, send_sems, recv_sems, local_sem):
        x, y, c, chips = _place()
        me, sibling = (x, y, c), (x, y, 1 - c)

        def rows(px, py, pc):
            return out_ref.at[pl.ds((4 * px + 2 * py + pc) * m_per, m_per), :]

        def copy(k, block, to, src=None):
            return pltpu.make_async_remote_copy(
                src_ref=rows(*block) if src is None else src, dst_ref=rows(*block),
                send_sem=send_sems.at[k], recv_sem=recv_sems.at[k], device_id=to, device_id_type=MESH)

        mine = pltpu.make_async_copy(x_ref, rows(*me), local_sem)
        mine.start()
        first = [copy(0, me, sibling, src=x_ref)]
        first += [copy(1 + j, me, (*chip, c), src=x_ref) for j, chip in enumerate(chips)]
        for cp in first:
            cp.start()
        passed = [copy(4 + j, (*chip, c), sibling) for j, chip in enumerate(chips)]
        for j, chip in enumerate(chips):
            copy(1 + j, (*chip, c), me).wait_recv()
            passed[j].start()
        copy(0, sibling, me).wait_recv()
        for j, chip in enumerate(chips):
            copy(4 + j, (*chip, 1 - c), me).wait_recv()
        for cp in first + passed:
            cp.wait_send()
        mine.wait()

    return pl.pallas_call(
        body, name=name, out_shape=SDS((N_DEV * m_per, n), blk.dtype),
        in_specs=[pl.BlockSpec(memory_space=pltpu.VMEM)], out_specs=pl.BlockSpec(memory_space=pltpu.VMEM),
        scratch_shapes=[pltpu.SemaphoreType.DMA((7,)), pltpu.SemaphoreType.DMA((7,)), pltpu.SemaphoreType.DMA],
    )(blk)


def _half(cc, rh):
    return pl.ds(pl.multiple_of(cc * rh, 16), rh)


def all_gather_shards(ws):
    n = len(ws)

    def body(*refs):
        w_refs, out_refs = refs[:n], refs[n:2 * n]
        send_sems, recv_sems, local_sems, own_sems = refs[2 * n:]
        x, y, c, chips = _place()
        sibling = (x, y, 1 - c)
        s_me = 2 * x + y

        def copy(k, src, dst, to):
            return pltpu.make_async_remote_copy(src_ref=src, dst_ref=dst, send_sem=send_sems.at[k], recv_sem=recv_sems.at[k],
                                                device_id=to, device_id_type=MESH)

        local, sends, passed = [], [], []
        for k in range(n):
            rh = ws[k].shape[0] // 2
            cp = pltpu.make_async_remote_copy(src_ref=w_refs[k], dst_ref=out_refs[k].at[s_me], send_sem=local_sems.at[k],
                                              recv_sem=own_sems.at[k], device_id=sibling, device_id_type=MESH)
            cp.start()
            local.append(cp)
            for j, chip in enumerate(chips):
                sd = copy(6 * k + j, w_refs[k].at[_half(c, rh)], out_refs[k].at[s_me, _half(c, rh)], (*chip, c))
                sd.start()
                sends.append(sd)
        for k in range(n):
            rh = ws[k].shape[0] // 2
            for j, (px, py) in enumerate(chips):
                got = out_refs[k].at[2 * px + py, _half(c, rh)]
                copy(6 * k + j, got, got, (px, py, c)).wait_recv()
                fw = copy(6 * k + 3 + j, got, got, sibling)
                fw.start()
                passed.append(fw)
        for k in range(n):
            rh = ws[k].shape[0] // 2
            for j, (px, py) in enumerate(chips):
                got = out_refs[k].at[2 * px + py, _half(1 - c, rh)]
                copy(6 * k + 3 + j, got, got, sibling).wait_recv()
        for cp in sends + passed:
            cp.wait_send()
        for cp in local:
            cp.wait()

    return pl.pallas_call(
        body, name="weights_all_gather", out_shape=[SDS((N_SHARD,) + w.shape, w.dtype) for w in ws],
        in_specs=[ANY] * n, out_specs=[ANY] * n,
        scratch_shapes=[pltpu.SemaphoreType.DMA((6 * n,)), pltpu.SemaphoreType.DMA((6 * n,)), pltpu.SemaphoreType.DMA((n,)),
                        pltpu.SemaphoreType.DMA((n,))],
    )(*ws)


def sibling_exchange(sends, name):
    n = len(sends)

    def body(*refs):
        s_refs, o_refs, send_sems, recv_sems = refs[:n], refs[n:2 * n], refs[2 * n], refs[2 * n + 1]
        x, y, c, _ = _place()
        cps = [pltpu.make_async_remote_copy(src_ref=s_refs[k], dst_ref=o_refs[k], send_sem=send_sems.at[k], recv_sem=recv_sems.at[k],
                                            device_id=(x, y, 1 - c), device_id_type=MESH) for k in range(n)]
        for cp in cps:
            cp.start()
        for cp in cps:
            cp.wait()

    return pl.pallas_call(
        body, name=name, out_shape=[SDS(s.shape, s.dtype) for s in sends], in_specs=[ANY] * n, out_specs=[ANY] * n,
        scratch_shapes=[pltpu.SemaphoreType.DMA((n,)), pltpu.SemaphoreType.DMA((n,))],
    )(*sends)


def scatter_to_chips(parts):
    n = len(parts)

    def body(*refs):
        p_refs, o_refs, send_sems, recv_sems = refs[:n], refs[n:2 * n], refs[2 * n], refs[2 * n + 1]
        x, y, c, chips = _place()
        cps = []
        for k in range(n):
            for j, (px, py) in enumerate(chips):
                cp = pltpu.make_async_remote_copy(src_ref=p_refs[k].at[2 * px + py], dst_ref=o_refs[k].at[j],
                                                  send_sem=send_sems.at[3 * k + j], recv_sem=recv_sems.at[3 * k + j],
                                                  device_id=(px, py, c), device_id_type=MESH)
                cp.start()
                cps.append(cp)
        for cp in cps:
            cp.wait()

    return pl.pallas_call(
        body, name="grads_scatter", out_shape=[SDS((3,) + p.shape[1:], p.dtype) for p in parts], in_specs=[ANY] * n, out_specs=[ANY] * n,
        scratch_shapes=[pltpu.SemaphoreType.DMA((3 * n,)), pltpu.SemaphoreType.DMA((3 * n,))],
    )(*parts)


def merge_halves(halves):
    n = len(halves)

    def body(*refs):
        h_refs, o_refs = refs[:n], refs[n:2 * n]
        send_sems, recv_sems, local_sems = refs[2 * n:]
        x, y, c, _ = _place()
        local, cps = [], []
        for k in range(n):
            rh = halves[k].shape[0]
            lc = pltpu.make_async_copy(h_refs[k], o_refs[k].at[_half(c, rh)], local_sems.at[k])
            lc.start()
            local.append(lc)
            cp = pltpu.make_async_remote_copy(src_ref=h_refs[k], dst_ref=o_refs[k].at[_half(c, rh)], send_sem=send_sems.at[k],
                                              recv_sem=recv_sems.at[k], device_id=(x, y, 1 - c), device_id_type=MESH)
            cp.start()
            cps.append(cp)
        for k in range(n):
            rh = halves[k].shape[0]
            got = o_refs[k].at[_half(1 - c, rh)]
            pltpu.make_async_remote_copy(src_ref=got, dst_ref=got, send_sem=send_sems.at[k], recv_sem=recv_sems.at[k],
                                         device_id=(x, y, 1 - c), device_id_type=MESH).wait_recv()
        for cp in cps:
            cp.wait_send()
        for lc in local:
            lc.wait()

    return pl.pallas_call(
        body, name="grads_merge_halves", out_shape=[SDS((2 * h.shape[0], h.shape[1]), h.dtype) for h in halves],
        in_specs=[ANY] * n, out_specs=[ANY] * n,
        scratch_shapes=[pltpu.SemaphoreType.DMA((n,)), pltpu.SemaphoreType.DMA((n,)), pltpu.SemaphoreType.DMA((n,))],
    )(*halves)


def add_rows(arrs, out_dtype, name, rt=256):
    Rr, W = arrs[0].shape

    def fn(ids, *vals):
        acc = vals[0]
        for v in vals[1:]:
            acc = acc + v
        return (acc,)

    t = rt if Rr % rt == 0 else max(b for b in range(16, rt + 1, 16) if Rr % b == 0)
    (out,) = rowwise(fn, [Row(a, (t, W), lambda i: (i, 0)) for a in arrs], [],
                     [Out((Rr, W), out_dtype, (t, W), lambda i: (i, 0))], (Rr // t,), name)
    return out


HBM_SPEC = pl.BlockSpec(memory_space=pltpu.HBM)
SEM_SPEC = pl.BlockSpec(memory_space=pltpu.SEMAPHORE)
DATAFLOW = pltpu.SideEffectType.DATAFLOW_SIDE_EFFECTING


def _in_hbm(a):
    return pltpu.with_memory_space_constraint(a, pltpu.HBM)


def _gather_copies(w_refs, land_refs, send_sems, recv_sems):
    x, y, c, chips = _place()
    targets = [(x, y, 1 - c)] + [(*chip, c) for chip in chips]
    cps = []
    for k, (w_ref, land_ref) in enumerate(zip(w_refs, land_refs)):
        for j, to in enumerate(targets):
            cps.append(pltpu.make_async_remote_copy(src_ref=w_ref, dst_ref=land_ref.at[2 * x + y], send_sem=send_sems.at[4 * k + j],
                                                    recv_sem=recv_sems.at[4 * k + j], device_id=to, device_id_type=MESH))
    return cps


def _scatter_copies(p_refs, land_refs, send_sems, recv_sems):
    x, y, c, chips = _place()
    cps = []
    for k, (p_ref, land_ref) in enumerate(zip(p_refs, land_refs)):
        for j, (px, py) in enumerate(chips):
            cps.append(pltpu.make_async_remote_copy(src_ref=p_ref.at[2 * px + py], dst_ref=land_ref.at[j], send_sem=send_sems.at[3 * k + j],
                                                    recv_sem=recv_sems.at[3 * k + j], device_id=(px, py, c), device_id_type=MESH))
    return cps


def copies_start(srcs, land_shapes, make_copies, per_src, name):
    n = len(srcs)
    m = per_src * n

    def body(*refs):
        src_refs, land_refs = refs[:n], refs[n:2 * n]
        send_sems, recv_sems, token = refs[2 * n], refs[2 * n + 1], refs[-1]
        for cp in make_copies(src_refs, land_refs, send_sems, recv_sems):
            cp.start()
        token[...] = jnp.zeros_like(token)

    lands = [lax.empty(shp, s.dtype) for shp, s in zip(land_shapes, srcs)]
    res = pl.pallas_call(
        body, name=name,
        out_shape=(pltpu.SemaphoreType.DMA((m,)), pltpu.SemaphoreType.DMA((m,)), *[pltpu.HBM(s.shape, s.dtype) for s in srcs],
                   *[pltpu.HBM(shp, s.dtype) for shp, s in zip(land_shapes, srcs)], SDS((8, LANES), F32)),
        in_specs=[HBM_SPEC] * (2 * n),
        out_specs=(SEM_SPEC, SEM_SPEC, *[HBM_SPEC] * (2 * n), pl.BlockSpec(memory_space=pltpu.VMEM)),
        input_output_aliases={i: 2 + i for i in range(2 * n)},
        compiler_params=pltpu.CompilerParams(has_side_effects=DATAFLOW),
    )(*[_in_hbm(s) for s in srcs], *[_in_hbm(l) for l in lands])
    return res[0], res[1], list(res[2:2 + n]), list(res[2 + n:2 + 2 * n]), res[-1]


def copies_wait(send_sems, recv_sems, srcs, lands, make_copies, after, name):
    n = len(srcs)

    def body(*refs):
        src_refs, land_refs = refs[:n], refs[n:2 * n]
        for cp in make_copies(src_refs, land_refs, refs[2 * n], refs[2 * n + 1]):
            cp.wait_send()
            cp.wait_recv()

    res = pl.pallas_call(
        body, name=name,
        out_shape=(*[pltpu.HBM(s.shape, s.dtype) for s in srcs], *[pltpu.HBM(l.shape, l.dtype) for l in lands]),
        in_specs=[HBM_SPEC] * (2 * n) + [SEM_SPEC, SEM_SPEC, ANY],
        out_specs=tuple([HBM_SPEC] * (2 * n)),
        input_output_aliases={i: i for i in range(2 * n)},
        compiler_params=pltpu.CompilerParams(has_side_effects=DATAFLOW),
    )(*srcs, *lands, send_sems, recv_sems, after)
    return list(res[n:])


PACK = (("gdn_w_in", 2), ("gdn_w_out", 1), ("w_ffn_in", 2), ("w_ffn_out", 1), ("dsw_w_in", 2), ("dsw_w_out", 2))
PACK_ALIGN = 32


def _pack_rows(sizes):
    total = sum(sizes)
    rows = -(-total // D)
    return -(-rows // PACK_ALIGN) * PACK_ALIGN


def pack_blocks(blocks, dtype):
    flat = [b.astype(dtype).reshape(-1) for b in blocks]
    total = sum(f.shape[0] for f in flat)
    R = _pack_rows([f.shape[0] for f in flat])
    flat.append(jnp.zeros((R * D - total,), dtype))
    return jnp.concatenate(flat).reshape(R, D)


def unpack_blocks(buf, shapes):
    flat = buf.reshape(-1)
    out, off = [], 0
    for shp in shapes:
        n = int(np.prod(shp))
        out.append(flat[off:off + n].reshape(shp))
        off += n
    return out


def _shard_slice(a, axis, s):
    n = a.shape[axis] // N_SHARD
    return lax.slice_in_dim(a, s * n, (s + 1) * n, axis=axis)


def _pad_lanes(v):
    return jnp.concatenate([v.astype(F32), jnp.zeros((LANES - v.shape[0],), F32)])[None]


def kernel(x, c, w_ada, b_ada, norm_mix, norm_ffn, w_ffn_in, w_ffn_out, gdn_w_in, gdn_conv, gdn_a_log, gdn_dt_bias, gdn_out_norm, gdn_w_out, dsw_w_in, dsw_q_norm, dsw_k_norm, dsw_w_out, rel_bias, loss_target, m_w_ada, m_b_ada, m_norm_mix, m_norm_ffn, m_w_ffn_in, m_w_ffn_out, m_gdn_w_in, m_gdn_conv, m_gdn_a_log, m_gdn_dt_bias, m_gdn_out_norm, m_gdn_w_out, m_dsw_w_in, m_dsw_q_norm, m_dsw_k_norm, m_dsw_w_out, m_rel_bias, v_w_ada, v_b_ada, v_norm_mix, v_norm_ffn, v_w_ffn_in, v_w_ffn_out, v_gdn_w_in, v_gdn_conv, v_gdn_a_log, v_gdn_dt_bias, v_gdn_out_norm, v_gdn_w_out, v_dsw_w_in, v_dsw_q_norm, v_dsw_k_norm, v_dsw_w_out, v_rel_bias):
    S = x.shape[1]
    nt = S // RT
    xi, yi, ci = lax.axis_index("x"), lax.axis_index("y"), lax.axis_index("c")
    me = 4 * xi + 2 * yi + ci
    s_me = 2 * xi + yi
    x0, tgt = x[0], loss_target[0]
    shard = dict(w_ffn_in=w_ffn_in, w_ffn_out=w_ffn_out, gdn_w_in=gdn_w_in, gdn_w_out=gdn_w_out, dsw_w_in=dsw_w_in, dsw_w_out=dsw_w_out)

    whole = lambda a: Row(a, a.shape, lambda i: (0,) * a.ndim)
    (cond8,) = rowwise(lambda ids, v: (_silu(v),), [whole(c.reshape(8, LANES))], [], [Out((8, LANES), F32, (8, LANES), lambda i: (0, 0))], (1,), "cond")
    cond_all = all_gather_small(cond8, "gather_cond").reshape(N_DEV, D)
    cond16 = jnp.concatenate([cond_all, jnp.zeros((8, D), F32)], axis=0)
    ada_cols = w_ada.shape[2]
    mods = [matmul(cond16, w_ada[l], "nn", F32, f"ada_{l}")[:N_DEV] for l in range(2)]
    buf = jnp.concatenate([jnp.stack(mods, axis=1).reshape(-1, LANES), gdn_conv.reshape(-1, LANES)], axis=0)
    n_mod_rows = N_DEV * 2 * ada_cols // LANES
    got = all_gather_small(buf, "gather_mod").reshape(N_DEV, buf.shape[0], LANES)
    mod_parts, conv_parts = [], []
    for s in range(N_SHARD):
        from_dev = got[2 * s]
        mod_parts.append(lax.dynamic_index_in_dim(from_dev[:n_mod_rows].reshape(N_DEV, 2, ada_cols), me, 0, keepdims=False))
        conv_parts.append(from_dev[n_mod_rows:].reshape(4, -1))
    mod_nb = jnp.concatenate(mod_parts, axis=1)
    conv_w = jnp.concatenate(conv_parts, axis=1)
    (mod,) = rowwise(lambda ids, a, b: (a + b,), [whole(mod_nb), whole(b_ada)], [], [Out(mod_nb.shape, F32, mod_nb.shape, lambda i: (0, 0))], (1,), "mod_bias")
    mod = mod.reshape(2, 6, 1, D)
    sh1, sc1, g1, sh2, sc2, g2 = ([mod[l, k] for l in range(2)] for k in range(6))
    gmix = [norm_mix[l][None] for l in range(2)]
    gffn = [norm_ffn[l][None] for l in range(2)]

    gcols = gdn_w_in.shape[2]
    g_gdn_in, g_gdn_out = all_gather_shards([gdn_w_in[0].astype(BF16), gdn_w_out[0].astype(BF16)])
    gathered = lambda ws: [(N_SHARD,) + w.shape for w in ws]
    gate = (jnp.minimum(jnp.abs(g_gdn_in[0, 0, 0].astype(F32)), 0.0) + jnp.minimum(jnp.abs(mod[0, 0, 0, 0]), 0.0)).astype(BF16)
    w2 = [w_ffn_in[0].astype(BF16) + gate, w_ffn_out[0].astype(BF16) + gate]
    w3 = [dsw_w_in[0].astype(BF16) + gate, dsw_w_out[0].astype(BF16) + gate, w_ffn_in[1].astype(BF16) + gate, w_ffn_out[1].astype(BF16) + gate]
    fly2 = copies_start(w2, gathered(w2), _gather_copies, 4, "weights_ffn0_start")
    fly3 = copies_start(w3, gathered(w3), _gather_copies, 4, "weights_layer1_start")
    started = fly2[4][0, 0] + fly3[4][0, 0]
    w_gdn = jnp.concatenate([g_gdn_in[s] for s in range(N_SHARD)] + [jnp.zeros((D, GDN_PROJ - N_SHARD * gcols), BF16)], axis=1)
    alog, dtb = _pad_lanes(gdn_a_log[0]), _pad_lanes(gdn_dt_bias[0])
    qg2 = jnp.concatenate([dsw_q_norm, dsw_q_norm], axis=1)
    kg2 = jnp.concatenate([dsw_k_norm, dsw_k_norm], axis=1)
    w_gdn_out = g_gdn_out.reshape(GDN_H * LANES, D)
    gdn_args = (w_gdn, conv_w, alog, dtb, gdn_out_norm, w_gdn_out)
    sc1[0] = sc1[0] + started

    (h10,) = rowwise(f_norm_only, [_wide(x0)], [gmix[0], sc1[0], sh1[0]], [_wide_out(S, BF16)], (nt,), "l0_norm")
    y0, sv_g = gdn_forward(h10, *gdn_args)
    x1, h20 = rowwise(f_resid_norm, [_wide(x0), _wide(y0)], [g1[0], gffn[0], sc2[0], sh2[0]], [_wide_out(S, F32), _wide_out(S, BF16)], (nt,), "l0_mid")
    g_in0, g_out0 = copies_wait(*fly2[:4], _gather_copies, y0, "weights_ffn0_wait")
    w_ffn = [(g_in0, g_out0.reshape(FFN, D)), None]
    f0, sv_f0 = ffn_forward(h20, *w_ffn[0], "0")
    x2, h11 = rowwise(f_resid_norm, [_wide(x1), _wide(f0)], [g2[0], gmix[1], sc1[1], sh1[1]], [_wide_out(S, F32), _wide_out(S, BF16)], (nt,), "l1_in")
    g_dsw_in, g_dsw_out, g_in1, g_out1 = copies_wait(*fly3[:4], _gather_copies, f0, "weights_layer1_wait")
    w_ffn[1] = (g_in1, g_out1.reshape(FFN, D))
    dsw_args = (g_dsw_in, qg2, kg2)
    y1, sv_d = dsw_forward(h11, *dsw_args, rel_bias, g_dsw_out)
    x3, h21 = rowwise(f_resid_norm, [_wide(x2), _wide(y1)], [g1[1], gffn[1], sc2[1], sh2[1]], [_wide_out(S, F32), _wide_out(S, BF16)], (nt,), "l1_mid")
    f1, sv_f1 = ffn_forward(h21, *w_ffn[1], "1")
    part_spec = lambda a: Row(a, (None, 1, D), lambda i: (i, 0, 0))
    (parts,) = rowwise(f_loss, [_wide(x3), _wide(f1), _wide(tgt)], [g2[1]], [Out((nt, 1, D), F32, (None, 1, D), lambda i: (i, 0, 0))], (nt,), "loss")
    loss = lax.psum(jnp.sum(parts), ("x", "y", "c"))

    (dx3, df1), (dg2_1,) = rowwise_bwd(f_loss, [_wide(x3), _wide(f1, gdtype=BF16), _wide(tgt, diff=False)], [g2[1]],
                                       [part_spec(jnp.ones((nt, 1, D), F32))], (nt,), "loss_bwd")
    dh21, d_win1, d_wout1 = ffn_backward(df1, sv_f1, *w_ffn[1], "1")
    (dx2, dy1), (dg1_1, dgf1, dsc2_1, dsh2_1) = rowwise_bwd(
        f_resid_norm, [_wide(x2), _wide(y1, gdtype=BF16)], [g1[1], gffn[1], sc2[1], sh2[1]], [_wide(dx3), _wide(dh21)], (nt,), "l1_mid_bwd")
    dh11, g_d = dsw_backward(dy1, sv_d, *dsw_args, g_dsw_out)
    (dx1, df0), (dg2_0, dgm1, dsc1_1, dsh1_1) = rowwise_bwd(
        f_resid_norm, [_wide(x1), _wide(f0, gdtype=BF16)], [g2[0], gmix[1], sc1[1], sh1[1]], [_wide(dx2), _wide(dh11)], (nt,), "l1_in_bwd")
    by_shard = lambda a: a.reshape(N_SHARD, a.shape[0] // N_SHARD, a.shape[1])
    landing = lambda ps: [(3,) + p.shape[1:] for p in ps]
    dws3 = [g_d["w_in"], g_d["w_out"], d_win1, by_shard(d_wout1)]
    parts3 = [a.astype(BF16) for a in dws3]
    gfly3 = copies_start(parts3, landing(parts3), _scatter_copies, 3, "grads_layer1_start")
    w_out0 = w_ffn[0][1] + gfly3[4][0, 0].astype(BF16)
    dh20, d_win0, d_wout0 = ffn_backward(df0, sv_f0, w_ffn[0][0], w_out0, "0")
    (dx0p, dy0), (dg1_0, dgf0, dsc2_0, dsh2_0) = rowwise_bwd(
        f_resid_norm, [_wide(x0), _wide(y0, gdtype=BF16)], [g1[0], gffn[0], sc2[0], sh2[0]], [_wide(dx1), _wide(dh20)], (nt,), "l0_mid_bwd")
    dws2 = [d_win0, by_shard(d_wout0)]
    parts2 = [a.astype(BF16) for a in dws2]
    gfly2 = copies_start(parts2, landing(parts2), _scatter_copies, 3, "grads_ffn0_start")
    gdn_args = gdn_args[:5] + (w_gdn_out + gfly2[4][0, 0].astype(BF16),)
    dh10, g_g = gdn_backward(dy0, sv_g, *gdn_args)
    (grad_x,), (dgm0, dsc1_0, dsh1_0) = rowwise_bwd(f_first, [_wide(x0)], [gmix[0], sc1[0], sh1[0]], [_wide(dx0p), _wide(dh10)], (nt,), "l0_norm_bwd")

    dmod = jnp.concatenate([dsh1_0, dsc1_0, dg1_0, dsh2_0, dsc2_0, dg2_0, dsh1_1, dsc1_1, dg1_1, dsh2_1, dsc2_1, dg2_1], axis=1)
    d_rel = jnp.transpose(g_d["rel"][:, :, 0])
    fold = lambda v: v[:, :DSW_DH] + v[:, DSW_DH:]
    small = [dmod, jnp.concatenate([dgm0, dgm1], axis=1), jnp.concatenate([dgf0, dgf1], axis=1), g_g["conv"].reshape(1, -1),
             g_g["alog"], g_g["dtb"], g_g["gain"], _pad_lanes(fold(g_d["q_gain2"])[0]), _pad_lanes(fold(g_d["k_gain2"])[0]),
             d_rel.reshape(1, -1)]
    used = [v.shape[1] // LANES for v in small]
    sizes = [-(-u // 8) * 8 for u in used]
    pad8 = lambda v, u, s: jnp.concatenate([v.reshape(u, LANES), jnp.zeros((s - u, LANES), F32)], axis=0) if s > u else v.reshape(u, LANES)
    pad_rows = sum(sizes)
    sbuf = jnp.concatenate([pad8(v, u, s) for v, u, s in zip(small, used, sizes)], axis=0)
    sgot = all_gather_small(sbuf, "gather_small_grads")
    ssum = add_rows([sgot[d * pad_rows:(d + 1) * pad_rows] for d in range(N_DEV)], F32, "sum_small_grads", rt=pad_rows)
    offs = np.cumsum([0] + sizes)
    take = lambda k: ssum[offs[k]:offs[k] + used[k]].reshape(1, -1)
    grad_b_ada = take(0).reshape(2, 6 * D)
    grad_norm_mix = take(1).reshape(2, D)
    grad_norm_ffn = take(2).reshape(2, D)
    conv_full = take(3).reshape(4, -1)
    ncv = gdn_conv.shape[2]
    grad_gdn_conv = lax.dynamic_slice_in_dim(conv_full, s_me * ncv, ncv, axis=1)[None]
    grad_a_log = take(4)[:, :GDN_H]
    grad_dt_bias = take(5)[:, :GDN_H]
    grad_out_norm = take(6)
    grad_q_norm = take(7)[:, :DSW_DH]
    grad_k_norm = take(8)[:, :DSW_DH]
    grad_rel = take(9).reshape(REL_BUCKETS, 3 * GDN_H)
    dmod_all = sgot.reshape(N_DEV, pad_rows, LANES)[:, :used[0]].reshape(N_DEV, 2, 6 * D)
    dmod_mine = lax.dynamic_slice_in_dim(dmod_all, s_me * ada_cols, ada_cols, axis=2)
    dmod16 = jnp.concatenate([dmod_mine, jnp.zeros_like(dmod_mine)], axis=0)
    grad_w_ada = jnp.stack([matmul(cond16, dmod16[:, l], "tn", F32, f"ada_dw_{l}") for l in range(2)])

    dg_in = jnp.stack([g_g["w_in"][:, s * gcols:(s + 1) * gcols] for s in range(N_SHARD)])
    dws = [dg_in, by_shard(g_g["w_out"])]
    keeps, gives = [], []
    for a in dws:
        rh = a.shape[1] // 2
        keeps.append(lax.dynamic_slice_in_dim(a, ci * rh, rh, axis=1))
        gives.append(lax.dynamic_slice_in_dim(a, (1 - ci) * rh, rh, axis=1).astype(BF16))
    from_sib = sibling_exchange(gives, "grads_to_sibling")
    flat2 = lambda a: a.reshape(-1, a.shape[-1])
    parts = [add_rows([flat2(k_), flat2(f_)], BF16, f"grads_chip_sum_{i}").reshape(k_.shape) for i, (k_, f_) in enumerate(zip(keeps, from_sib))]
    others = scatter_to_chips(parts)
    halves = []
    for i, (p_, o_) in enumerate(zip(parts, others)):
        own = lax.dynamic_index_in_dim(p_, s_me, 0, keepdims=False)
        halves.append(add_rows([own, o_[0], o_[1], o_[2]], F32, f"grads_sum_{i}"))
    sib_halves = sibling_exchange(halves, "grads_from_sibling")

    def whole_shard(mine, theirs):
        both = jnp.stack([mine, theirs])
        return jnp.concatenate([lax.dynamic_index_in_dim(both, ci, 0, keepdims=False),
                                lax.dynamic_index_in_dim(both, 1 - ci, 0, keepdims=False)], axis=0)

    s_gdn_in, s_gdn_out = [whole_shard(a, b) for a, b in zip(halves, sib_halves)]
    got3 = copies_wait(*gfly3[:4], _scatter_copies, grad_x, "grads_layer1_wait")
    got2 = copies_wait(*gfly2[:4], _scatter_copies, grad_x, "grads_ffn0_wait")
    core_sums = []
    for i, (full, got) in enumerate(zip(dws3 + dws2, got3 + got2)):
        own = lax.dynamic_index_in_dim(full, s_me, 0, keepdims=False)
        core_sums.append(add_rows([own, got[0], got[1], got[2]], F32, f"grads_core_sum_{i}"))
    sib_sums = sibling_exchange(core_sums, "grads_core_sums_swap")
    s_dsw_in, s_dsw_out, s_in1, s_out1, s_in0, s_out0 = [add_rows([a, b], F32, f"grads_chip_total_{i}")
                                                          for i, (a, b) in enumerate(zip(core_sums, sib_sums))]
    gsh = dict(gdn_w_in=s_gdn_in[None], gdn_w_out=s_gdn_out[None],
               w_ffn_in=jnp.stack([s_in0, s_in1]), w_ffn_out=jnp.stack([s_out0, s_out1]),
               dsw_w_in=s_dsw_in[None], dsw_w_out=s_dsw_out[None])

    grads = dict(w_ada=grad_w_ada, b_ada=grad_b_ada, norm_mix=grad_norm_mix, norm_ffn=grad_norm_ffn, w_ffn_in=gsh["w_ffn_in"],
                 w_ffn_out=gsh["w_ffn_out"], gdn_w_in=gsh["gdn_w_in"], gdn_conv=grad_gdn_conv, gdn_a_log=grad_a_log,
                 gdn_dt_bias=grad_dt_bias, gdn_out_norm=grad_out_norm, gdn_w_out=gsh["gdn_w_out"], dsw_w_in=gsh["dsw_w_in"],
                 dsw_q_norm=grad_q_norm, dsw_k_norm=grad_k_norm, dsw_w_out=gsh["dsw_w_out"], rel_bias=grad_rel)
    weights = dict(w_ada=w_ada, b_ada=b_ada, norm_mix=norm_mix, norm_ffn=norm_ffn, w_ffn_in=w_ffn_in, w_ffn_out=w_ffn_out,
                   gdn_w_in=gdn_w_in, gdn_conv=gdn_conv, gdn_a_log=gdn_a_log, gdn_dt_bias=gdn_dt_bias, gdn_out_norm=gdn_out_norm,
                   gdn_w_out=gdn_w_out, dsw_w_in=dsw_w_in, dsw_q_norm=dsw_q_norm, dsw_k_norm=dsw_k_norm, dsw_w_out=dsw_w_out,
                   rel_bias=rel_bias)
    ms = dict(w_ada=m_w_ada, b_ada=m_b_ada, norm_mix=m_norm_mix, norm_ffn=m_norm_ffn, w_ffn_in=m_w_ffn_in, w_ffn_out=m_w_ffn_out,
              gdn_w_in=m_gdn_w_in, gdn_conv=m_gdn_conv, gdn_a_log=m_gdn_a_log, gdn_dt_bias=m_gdn_dt_bias, gdn_out_norm=m_gdn_out_norm,
              gdn_w_out=m_gdn_w_out, dsw_w_in=m_dsw_w_in, dsw_q_norm=m_dsw_q_norm, dsw_k_norm=m_dsw_k_norm, dsw_w_out=m_dsw_w_out,
              rel_bias=m_rel_bias)
    vs = dict(w_ada=v_w_ada, b_ada=v_b_ada, norm_mix=v_norm_mix, norm_ffn=v_norm_ffn, w_ffn_in=v_w_ffn_in, w_ffn_out=v_w_ffn_out,
              gdn_w_in=v_gdn_w_in, gdn_conv=v_gdn_conv, gdn_a_log=v_gdn_a_log, gdn_dt_bias=v_gdn_dt_bias, gdn_out_norm=v_gdn_out_norm,
              gdn_w_out=v_gdn_w_out, dsw_w_in=v_dsw_w_in, dsw_q_norm=v_dsw_q_norm, dsw_k_norm=v_dsw_k_norm, dsw_w_out=v_dsw_w_out,
              rel_bias=v_rel_bias)
    names = list(weights)
    deltas, new_m, new_v = [], [], []
    for n in names:
        g = grads[n].reshape(weights[n].shape)
        grads[n] = g
        d, nm, nv = adamw(weights[n], g, ms[n], vs[n], f"adamw_{n}")
        deltas.append(d)
        new_m.append(nm)
        new_v.append(nv)
    return (loss, grad_x[None], *[grads[n] for n in names], *deltas, *new_m, *new_v)
```

```python
import functools
import math

import numpy as np
import jax
import jax.numpy as jnp
from jax import lax
from jax.experimental import pallas as pl
from jax.experimental.pallas import tpu as pltpu

F32 = jnp.float32
BF16 = jnp.bfloat16
SDS = jax.ShapeDtypeStruct
MESH = pl.DeviceIdType.MESH
ANY = pl.BlockSpec(memory_space=pl.ANY)

D = 1024
EPS = 1e-6
LANES = 128
GDN_H = 8
GDN_DK = 128
GDN_C = 64
DSW_GROUPS = ((128, 1), (512, 4), (2048, 16))
DSW_SPAN = 128
DSW_DH = 64
DSW_HG = 512
REL_BUCKETS = 32
REL_MAX_DIST = 2048
FFN = 2816
N_SHARD = 4
N_DEV = 8
VMEM_LIMIT = 48 * 1024 * 1024
NEG = -1e30

ADAM_LR, ADAM_B1, ADAM_B2, ADAM_EPS, ADAM_WD, ADAM_STEP = 0.001, 0.9, 0.999, 1e-08, 0.01, 10


def _cp(n_axes):
    return pltpu.CompilerParams(dimension_semantics=("arbitrary",) * n_axes, vmem_limit_bytes=VMEM_LIMIT)


def _blk(dim, cap):
    if dim <= cap:
        return dim
    best = None
    for b in range(LANES, cap + 1, LANES):
        if dim % b == 0:
            best = b
    assert best is not None, (dim, cap)
    return best


MAX_SHARD_BLOCK = 1408
def matmul(a, b, mode, out_dtype, name, cap_m=MAX_SHARD_BLOCK, cap_n=MAX_SHARD_BLOCK, cap_k=2048, col_shards=0):
    ns = col_shards
    if mode == "nn":
        (M, K) = a.shape
        K2, N = (b.shape[1], ns * b.shape[2]) if ns else b.shape
    elif mode == "nt":
        (M, K) = a.shape
        N, K2 = (b.shape[1], ns * b.shape[2]) if ns else b.shape
    else:
        (K, M), (K2, N) = a.shape, b.shape
    assert K == K2, (a.shape, b.shape, mode)
    if K <= 3072:
        cap_k = K
        if K > 2048:
            cap_n = 1024
    n_unit = N // ns if (ns and mode != "nt") else N
    k_unit = K // ns if (ns and mode == "nt") else K
    bm = _blk(M, cap_m)
    bn = _blk(n_unit, MAX_SHARD_BLOCK) if n_unit != N else _blk(N, cap_n)
    if k_unit != K:
        bk = _blk(k_unit, MAX_SHARD_BLOCK)
    else:
        bk = _blk(K, 1024 if (ns and mode == "tn") else cap_k)
    nk = K // bk
    nps, kps = n_unit // bn, k_unit // bk
    dims = {"nn": ((1,), (0,)), "nt": ((1,), (1,)), "tn": ((0,), (0,))}[mode]

    def dot(a_ref, b_ref):
        return lax.dot_general(a_ref[...].astype(BF16), b_ref[...].astype(BF16), (dims, ((), ())), preferred_element_type=F32)

    def body_one(a_ref, b_ref, o_ref):
        o_ref[...] = dot(a_ref, b_ref).astype(o_ref.dtype)

    def body_acc(a_ref, b_ref, o_ref, acc_ref):
        k = pl.program_id(2)

        @pl.when(k == 0)
        def _():
            acc_ref[...] = jnp.zeros_like(acc_ref)

        acc_ref[...] += dot(a_ref, b_ref)

        @pl.when(k == nk - 1)
        def _():
            o_ref[...] = acc_ref[...].astype(o_ref.dtype)

    a_spec = pl.BlockSpec((bk, bm), lambda i, j, k: (k, i)) if mode == "tn" else pl.BlockSpec((bm, bk), lambda i, j, k: (i, k))
    if mode == "nt":
        b_spec = pl.BlockSpec((None, bn, bk), lambda i, j, k: (k // kps, j, k % kps)) if ns else pl.BlockSpec((bn, bk), lambda i, j, k: (j, k))
    elif mode == "nn" and ns:
        b_spec = pl.BlockSpec((None, bk, bn), lambda i, j, k: (j // nps, k, j % nps))
    else:
        b_spec = pl.BlockSpec((bk, bn), lambda i, j, k: (k, j))
    if mode == "tn" and ns:
        o_spec, o_shape = pl.BlockSpec((None, bm, bn), lambda i, j, k: (j // nps, i, j % nps)), (ns, M, n_unit)
    else:
        o_spec, o_shape = pl.BlockSpec((bm, bn), lambda i, j, k: (i, j)), (M, N)
    return pl.pallas_call(
        body_one if nk == 1 else body_acc, name=name, grid=(M // bm, N // bn, nk),
        in_specs=[a_spec, b_spec], out_specs=o_spec,
        out_shape=SDS(o_shape, out_dtype), scratch_shapes=[] if nk == 1 else [pltpu.VMEM((bm, bn), F32)],
        compiler_params=_cp(3),
    )(a, b)


class Row:
    def __init__(self, arr, bshape, imap, splits=None, diff=True, acc=False, gdtype=F32, gshape=None, gbshape=None, gimap=None,
                 lead=0):
        self.arr, self.bshape, self.imap = arr, tuple(bshape), imap
        self.splits, self.lead = splits, lead
        self.diff, self.acc, self.gdtype = diff, acc, gdtype
        self.gshape = tuple(arr.shape) if gshape is None else tuple(gshape)
        self.gbshape = self.bshape if gbshape is None else tuple(gbshape)
        self.gimap = imap if gimap is None else gimap

    def gspec(self):
        return pl.BlockSpec(self.gbshape, self.gimap)

    def spec(self):
        return pl.BlockSpec(self.bshape, self.imap)

    def pieces(self, ref):
        return _load_pieces(ref, self.splits, self.lead)

    def n_pieces(self):
        return _n_pieces(self.splits, self.lead)


class Out:
    def __init__(self, shape, dtype, bshape, imap, splits=None, lead=0):
        self.shape, self.dtype, self.bshape, self.imap = tuple(shape), dtype, tuple(bshape), imap
        self.splits, self.lead = splits, lead

    def n_pieces(self):
        return _n_pieces(self.splits, self.lead)


def _n_pieces(splits, lead):
    return lead if lead else (1 if splits is None else len(splits))


def _load_pieces(ref, splits, lead):
    if lead:
        return [ref[k].astype(F32) for k in range(lead)]
    if splits is None:
        return [ref[...].astype(F32)]
    out, o = [], 0
    for w in splits:
        out.append(ref[..., o:o + w].astype(F32))
        o += w
    return out


def _store_pieces(ref, splits, lead, vals, accumulate=False):
    def put(idx, v):
        if accumulate:
            ref[idx] += v.astype(ref.dtype)
        else:
            ref[idx] = v.astype(ref.dtype)

    if lead:
        for k in range(lead):
            put(k, vals[k])
    elif splits is None:
        put(..., vals[0])
    else:
        o = 0
        for w, v in zip(splits, vals):
            put((..., slice(o, o + w)), v)
            o += w


def rowwise(fn, rows, params, outs, grid, name):
    nr, npar = len(rows), len(params)

    def body(*refs):
        ids = tuple(pl.program_id(a) for a in range(len(grid)))
        vals = []
        for r, ref in zip(rows, refs[:nr]):
            vals += r.pieces(ref)
        pvals = [ref[...].astype(F32) for ref in refs[nr:nr + npar]]
        res = list(fn(ids, *vals, *pvals))
        o = 0
        for spec, ref in zip(outs, refs[nr + npar:]):
            n = spec.n_pieces()
            _store_pieces(ref, spec.splits, spec.lead, res[o:o + n])
            o += n

    nz = len(grid)
    pspecs = [pl.BlockSpec(p.shape, (lambda *ids, _n=p.ndim: (0,) * _n)) for p in params]
    res = pl.pallas_call(
        body, name=name, grid=grid,
        in_specs=[r.spec() for r in rows] + pspecs,
        out_specs=[pl.BlockSpec(o.bshape, o.imap) for o in outs],
        out_shape=[SDS(o.shape, o.dtype) for o in outs],
        compiler_params=_cp(nz),
    )(*[r.arr for r in rows], *params)
    return list(res)


def rowwise_bwd(fn, rows, params, cots, grid, name):
    nr, npar, nc = len(rows), len(params), len(cots)
    drows = [r for r in rows if r.diff]
    nz = len(grid)

    def body(*refs):
        ids = tuple(pl.program_id(a) for a in range(nz))
        row_refs, par_refs = refs[:nr], refs[nr:nr + npar]
        cot_refs = refs[nr + npar:nr + npar + nc]
        drow_refs = refs[nr + npar + nc:nr + npar + nc + len(drows)]
        dpar_refs = refs[nr + npar + nc + len(drows):]
        pieces, is_diff = [], []
        for r, ref in zip(rows, row_refs):
            ps = r.pieces(ref)
            pieces += ps
            is_diff += [r.diff] * len(ps)
        pvals = [ref[...].astype(F32) for ref in par_refs]
        dvals = [p for p, dflag in zip(pieces, is_diff) if dflag]
        nd = len(dvals)

        def f(*args):
            it = iter(args[:nd])
            full = [next(it) if dflag else p for p, dflag in zip(pieces, is_diff)]
            return tuple(fn(ids, *full, *args[nd:]))

        _, vjp = jax.vjp(f, *dvals, *pvals)
        cvals = []
        for c, ref in zip(cots, cot_refs):
            cvals += c.pieces(ref)
        g = vjp(tuple(cvals))
        o = 0
        first_inner = ids[-1] == 0
        for r, ref in zip(drows, drow_refs):
            n = r.n_pieces()
            gs = g[o:o + n]
            o += n
            if r.acc:
                @pl.when(first_inner)
                def _(ref=ref):
                    ref[...] = jnp.zeros_like(ref)
            _store_pieces(ref, r.splits, r.lead, gs, accumulate=r.acc)
        first = functools.reduce(jnp.logical_and, [i == 0 for i in ids])
        for ref, gp in zip(dpar_refs, g[nd:]):
            @pl.when(first)
            def _(ref=ref):
                ref[...] = jnp.zeros_like(ref)
            ref[...] += gp

    pspecs = [pl.BlockSpec(p.shape, (lambda *ids, _n=p.ndim: (0,) * _n)) for p in params]
    res = pl.pallas_call(
        body, name=name, grid=grid,
        in_specs=[r.spec() for r in rows] + pspecs + [c.spec() for c in cots],
        out_specs=[r.gspec() for r in drows] + pspecs,
        out_shape=[SDS(r.gshape, r.gdtype) for r in drows] + [SDS(p.shape, F32) for p in params],
        compiler_params=_cp(nz),
    )(*[r.arr for r in rows], *params, *[c.arr for c in cots])
    res = list(res)
    return res[:len(drows)], res[len(drows):]


def _sigmoid(x):
    return 0.5 * (jnp.tanh(0.5 * x) + 1.0)


def _silu(x):
    return x * _sigmoid(x)


def _normmod(x, gain, sc, sh):
    inv = lax.rsqrt(jnp.mean(x * x, axis=-1, keepdims=True) + EPS)
    return x * inv * gain * (1.0 + sc) + sh


def f_first(ids, x, gain, sc, sh):
    return x, _normmod(x, gain, sc, sh)


def f_resid_norm(ids, x, y, g, gain, sc, sh):
    xn = x + g * y
    return xn, _normmod(xn, gain, sc, sh)


@jax.custom_vjp
def _swiglu(gate, up):
    return _silu(gate) * up


def _swiglu_fwd(gate, up):
    return _silu(gate) * up, (gate, up)


def _swiglu_bwd(res, da):
    gate, up = res
    s = _sigmoid(gate)
    gs = gate * s
    return da * up * (s + gs * (1.0 - s)), da * gs


_swiglu.defvjp(_swiglu_fwd, _swiglu_bwd)


def f_swiglu(ids, gate, up):
    return (_swiglu(gate, up),)


def f_loss(ids, x, y, tgt, g):
    out = x + g * y
    e = out - tgt
    part = 0.5 * jnp.sum(e * e, axis=0, keepdims=True) * (1.0 / D)
    return (part,)


def _softplus(x):
    return jnp.maximum(x, 0.0) + jnp.log(1.0 + jnp.exp(-jnp.abs(x)))


def _chunk_tril(T):
    r = lax.broadcasted_iota(jnp.int32, (T, T), 0)
    c = lax.broadcasted_iota(jnp.int32, (T, T), 1)
    return jnp.where((r // GDN_C == c // GDN_C) & (c <= r), 1.0, 0.0).astype(F32)


def _dot_hi(a, b, dims=((1,), (0,))):
    return lax.dot_general(a, b, (dims, ((), ())), precision=lax.Precision.HIGHEST, preferred_element_type=F32)


def _dot_x3(a, b, dims=((1,), (0,))):
    return lax.dot_general(a, b, (dims, ((), ())), precision=lax.Precision.HIGH, preferred_element_type=F32)


def f_gdn_gates(ids, ab, alog, dtb):
    T = ab.shape[0]
    g = -jnp.exp(alog) * _softplus(ab + dtb)
    beta = _sigmoid(ab)
    gcum = _dot_x3(_chunk_tril(T), g)
    row = lax.broadcasted_iota(jnp.int32, (LANES, LANES), 0)
    sel = lambda k: jnp.where(row == k, 1.0, 0.0).astype(F32)
    gcs = [_dot_x3(gcum, sel(h)) for h in range(GDN_H)]
    bts = [_dot_x3(beta, sel(GDN_H + h)) for h in range(GDN_H)]
    return (*gcs, *bts)


def f_gdn_post(ids, *args):
    os_, zs, gain = args[:GDN_H], args[GDN_H:2 * GDN_H], args[2 * GDN_H]
    out = []
    for o, z in zip(os_, zs):
        inv = lax.rsqrt(jnp.mean(o * o, axis=-1, keepdims=True) + EPS)
        out.append(o * inv * gain * _silu(z))
    return tuple(out)


def _qknorm1(x, gain2, scale):
    lane = lax.broadcasted_iota(jnp.int32, x.shape, 1)
    lo = lane < DSW_DH
    x2 = x * x
    s_all = jnp.sum(x2, axis=-1, keepdims=True)
    s_lo = jnp.sum(jnp.where(lo, x2, 0.0), axis=-1, keepdims=True)
    ms = jnp.where(lo, s_lo, s_all - s_lo) * (1.0 / DSW_DH)
    return x * lax.rsqrt(ms + EPS) * (gain2 * scale)


def f_qknorm(ids, *args):
    return tuple(_qknorm1(x, args[-1], 1.0) for x in args[:-1])


def f_qnorm(ids, *args):
    return tuple(_qknorm1(x, args[-1], DSW_DH ** -0.5) for x in args[:-1])


def f_combine(ids, o0, o1, o2, l0, l1, l2):
    m = jnp.maximum(jnp.maximum(l0, l1), l2)
    e0, e1, e2 = jnp.exp(l0 - m), jnp.exp(l1 - m), jnp.exp(l2 - m)
    den = e0 + e1 + e2
    o = (e0 * o0 + e1 * o1 + e2 * o2) / den
    return o, m + jnp.log(den)


GDN_T = 512
HALO = 16


def _conv_pre(xx, w):
    acc = xx * w[3:4, :]
    for j in range(3):
        acc = acc + pltpu.roll(xx, shift=3 - j, axis=0) * w[j:j + 1, :]
    return acc


@jax.custom_vjp
def _qkv_act_core(pre, norm_on, scale):
    s = _silu(pre)
    r = lax.rsqrt(jnp.sum(s * s, axis=-1, keepdims=True) + EPS)
    return jnp.where(norm_on > 0.5, s * r * scale, s)


def _qkv_act_fwd(pre, norm_on, scale):
    return _qkv_act_core(pre, norm_on, scale), (pre, norm_on, scale)


def _qkv_act_bwd(res, dout):
    pre, norm_on, scale = res
    sig = _sigmoid(pre)
    s = pre * sig
    r = lax.rsqrt(jnp.sum(s * s, axis=-1, keepdims=True) + EPS)
    unit = s * r
    dn = dout * scale
    ds = jnp.where(norm_on > 0.5, r * (dn - unit * jnp.sum(dn * unit, axis=-1, keepdims=True)), dout)
    return ds * (sig + s * (1.0 - sig)), jnp.zeros_like(norm_on), jnp.zeros_like(scale)


_qkv_act_core.defvjp(_qkv_act_fwd, _qkv_act_bwd)


def _qkv_act(pre, cidx):
    norm_on = jnp.where(cidx < 2 * GDN_H, 1.0, 0.0).astype(F32)
    scale = jnp.where(cidx < GDN_H, GDN_DK ** -0.5, 1.0).astype(F32)
    return _qkv_act_core(pre, norm_on, scale)


def gdn_pre(proj, conv_w, S):
    nt = S // GDN_T
    hb = GDN_T // HALO

    def body(prev_ref, cur_ref, w_ref, o_ref):
        p, i = pl.program_id(0), pl.program_id(1)
        for h in range(GDN_H):
            cols = slice(LANES * h, LANES * (h + 1))
            prev = jnp.where(i > 0, prev_ref[:, cols].astype(F32), 0.0)
            xx = jnp.concatenate([prev, cur_ref[:, cols].astype(F32)], axis=0)
            pre = _conv_pre(xx, w_ref[:, cols])[HALO:]
            o_ref[h] = _qkv_act(pre, p * GDN_H + h)

    hv = GDN_H * LANES
    return pl.pallas_call(
        body, name="gdn_pre", grid=(3, nt),
        in_specs=[pl.BlockSpec((HALO, hv), lambda p, i: (jnp.maximum(i * hb - 1, 0), p)),
                  pl.BlockSpec((GDN_T, hv), lambda p, i: (i, p)),
                  pl.BlockSpec((4, hv), lambda p, i: (0, p))],
        out_specs=pl.BlockSpec((None, GDN_H, GDN_T, LANES), lambda p, i: (p, 0, i, 0)),
        out_shape=SDS((3, GDN_H, S, LANES), F32),
        compiler_params=_cp(2),
    )(proj, proj, conv_w)


def gdn_pre_bwd(proj, conv_w, dqkv, S):
    nt = S // GDN_T
    hb = GDN_T // HALO
    last_h = S // HALO - 1

    def body(prev_ref, cur_ref, next_ref, w_ref, d_ref, dnext_ref, dx_ref, dw_ref):
        p, i = pl.program_id(0), pl.program_id(1)

        @pl.when(i == 0)
        def _():
            dw_ref[...] = jnp.zeros_like(dw_ref)

        for h in range(GDN_H):
            cols = slice(LANES * h, LANES * (h + 1))
            w = w_ref[:, cols]
            prev = jnp.where(i > 0, prev_ref[:, cols].astype(F32), 0.0)
            xx = jnp.concatenate([prev, cur_ref[:, cols].astype(F32), next_ref[:, cols].astype(F32)], axis=0)
            dnext = jnp.where(i < nt - 1, dnext_ref[h], 0.0)
            dd = jnp.concatenate([jnp.zeros((HALO, LANES), F32), d_ref[h], dnext], axis=0)
            pre = _conv_pre(xx, w)
            _, vjp = jax.vjp(lambda v, _c=p * GDN_H + h: _qkv_act(v, _c), pre)
            (dpre,) = vjp(dd)
            dx = dpre * w[3:4, :]
            R = dpre.shape[0]
            for j in range(3):
                dx = dx + pltpu.roll(dpre, shift=R - (3 - j), axis=0) * w[j:j + 1, :]
            dx_ref[:, cols] = dx[HALO:HALO + GDN_T].astype(dx_ref.dtype)
            own = HALO + GDN_T
            rows_w = [jnp.sum((dpre * pltpu.roll(xx, shift=3 - j, axis=0))[:own], axis=0, keepdims=True) for j in range(3)]
            rows_w.append(jnp.sum((dpre * xx)[:own], axis=0, keepdims=True))
            r4 = lax.broadcasted_iota(jnp.int32, (4, LANES), 0)
            dw = jnp.zeros((4, LANES), F32)
            for j in range(4):
                dw = dw + jnp.where(r4 == j, rows_w[j], 0.0)
            dw_ref[:, cols] += dw

    hv = GDN_H * LANES
    return pl.pallas_call(
        body, name="gdn_pre_bwd", grid=(3, nt),
        in_specs=[pl.BlockSpec((HALO, hv), lambda p, i: (jnp.maximum(i * hb - 1, 0), p)),
                  pl.BlockSpec((GDN_T, hv), lambda p, i: (i, p)),
                  pl.BlockSpec((HALO, hv), lambda p, i: (jnp.minimum((i + 1) * hb, last_h), p)),
                  pl.BlockSpec((4, hv), lambda p, i: (0, p)),
                  pl.BlockSpec((None, GDN_H, GDN_T, LANES), lambda p, i: (p, 0, i, 0)),
                  pl.BlockSpec((None, GDN_H, HALO, LANES), lambda p, i: (p, 0, jnp.minimum((i + 1) * hb, last_h), 0))],
        out_specs=[pl.BlockSpec((GDN_T, hv), lambda p, i: (i, p)),
                   pl.BlockSpec((4, hv), lambda p, i: (0, p))],
        out_shape=[SDS((S, 3 * hv), BF16), SDS((4, 3 * hv), F32)],
        compiler_params=_cp(2),
    )(proj, proj, proj, conv_w, dqkv, dqkv)


_DIMS = {"nn": ((1,), (0,)), "nt": ((1,), (1,)), "tn": ((0,), (0,))}


def _mm_raw(a, b, mode, hi):
    if hi:
        return _dot_hi(a, b, _DIMS[mode])
    return lax.dot_general(a.astype(BF16), b.astype(BF16), (_DIMS[mode], ((), ())), preferred_element_type=F32)


@functools.partial(jax.custom_vjp, nondiff_argnums=(2, 3))
def mm(a, b, mode, hi):
    return _mm_raw(a, b, mode, hi)


def _mm_fwd(a, b, mode, hi):
    return _mm_raw(a, b, mode, hi), (a, b)


def _mm_bwd(mode, hi, res, dc):
    a, b = res
    if mode == "nn":
        da, db = mm(dc, b, "nt", hi), mm(a, dc, "tn", hi)
    elif mode == "nt":
        da, db = mm(dc, b, "nn", hi), mm(dc, a, "tn", hi)
    else:
        da, db = mm(b, dc, "nt", hi), mm(a, dc, "nn", hi)
    return da, db


mm.defvjp(_mm_fwd, _mm_bwd)


TRI_BASE = 8


def _unit_lower_inverses(Ls):
    n = Ls[0].shape[0]
    r = lax.broadcasted_iota(jnp.int32, (n, n), 0)
    c = lax.broadcasted_iota(jnp.int32, (n, n), 1)
    eye = jnp.where(r == c, 1.0, 0.0).astype(F32)
    base = r // TRI_BASE == c // TRI_BASE
    Ps = [jnp.where(base, -L, 0.0) for L in Ls]
    invs = [eye + P for P in Ps]
    k = 1
    while 2 * k < TRI_BASE:
        Ps = [_dot_x3(P, P) for P in Ps]
        invs = [inv + _dot_x3(inv, P) for inv, P in zip(invs, Ps)]
        k *= 2
    b = 2 * TRI_BASE
    while b <= n:
        off_mask = (r // b == c // b) & ((r % b) >= b // 2) & ((c % b) < b // 2)
        ts = [_dot_x3(inv, jnp.where(off_mask, L, 0.0)) for inv, L in zip(invs, Ls)]
        invs = [inv - _dot_x3(t, inv) for inv, t in zip(invs, ts)]
        b *= 2
    return invs


@jax.custom_vjp
def tri_apply(invs, Ls, r1s, r2s):
    return [_dot_x3(i, r) for i, r in zip(invs, r1s)], [_dot_x3(i, r) for i, r in zip(invs, r2s)]


def _tri_fwd(invs, Ls, r1s, r2s):
    s1s = [_dot_x3(i, r) for i, r in zip(invs, r1s)]
    s2s = [_dot_x3(i, r) for i, r in zip(invs, r2s)]
    return (s1s, s2s), (invs, s1s, s2s)


def _tri_bwd(res, ds):
    invs, s1s, s2s = res
    d1s = [_dot_x3(i, d, _DIMS["tn"]) for i, d in zip(invs, ds[0])]
    d2s = [_dot_x3(i, d, _DIMS["tn"]) for i, d in zip(invs, ds[1])]
    dLs = [-(_dot_x3(d1, s1, _DIMS["nt"]) + _dot_x3(d2, s2, _DIMS["nt"])) for d1, s1, d2, s2 in zip(d1s, s1s, d2s, s2s)]
    return [jnp.zeros_like(i) for i in invs], dLs, d1s, d2s


tri_apply.defvjp(_tri_fwd, _tri_bwd)


def _gdn_chunk(qs, ks, vs, gcbs, btbs, Ss, invs=None):
    C = qs[0].shape[0]
    r = lax.broadcasted_iota(jnp.int32, (C, C), 0)
    c = lax.broadcasted_iota(jnp.int32, (C, C), 1)
    causal, strict = c <= r, c < r
    rows = lax.broadcasted_iota(jnp.int32, gcbs[0].shape, 0)
    Gs = [g[:, :C] for g in gcbs]
    decays = [jnp.exp(jnp.where(causal, G - G.T, NEG)) for G in Gs]
    kbs = [k * b for k, b in zip(ks, btbs)]
    vbs = [v * b for v, b in zip(vs, btbs)]
    Ls = [jnp.where(strict, mm(kb, k, "nt", False) * d, 0.0) for kb, k, d in zip(kbs, ks, decays)]
    egs = [jnp.exp(g) for g in gcbs]
    if invs is None:
        invs = _unit_lower_inverses(Ls)
    us, ws = tri_apply(invs, Ls, vbs, [kb * eg for kb, eg in zip(kbs, egs)])
    qks = [jnp.where(causal, mm(q, k, "nt", False) * d, 0.0) for q, k, d in zip(qs, ks, decays)]
    g_lasts = [jnp.sum(jnp.where(rows == C - 1, g, 0.0), axis=0, keepdims=True) for g in gcbs]
    q_decs = [q * eg for q, eg in zip(qs, egs)]
    k_decs = [k * jnp.exp(gl - g) for k, gl, g in zip(ks, g_lasts, gcbs)]
    v_news = [u - mm(w, S, "nn", False) for u, w, S in zip(us, ws, Ss)]
    os_ = [mm(qd, S, "nn", False) + mm(qk, vn, "nn", False) for qd, S, qk, vn in zip(q_decs, Ss, qks, v_news)]
    S_news = [S * jnp.exp(gl) + mm(kd, vn, "tn", False) for S, gl, kd, vn in zip(Ss, g_lasts, k_decs, v_news)]
    return os_, S_news, invs


def gdn_core(qkv, gc, bt, S):
    nchunk = S // GDN_C

    def body(qkv_ref, g_ref, b_ref, o_ref, st_ref, inv_ref, s_scr):
        n = pl.program_id(0)

        @pl.when(n == 0)
        def _():
            s_scr[...] = jnp.zeros_like(s_scr)

        heads = range(GDN_H)
        S_in = [s_scr[h] for h in heads]
        os_, S_new, invs = _gdn_chunk([qkv_ref[0, h] for h in heads], [qkv_ref[1, h] for h in heads], [qkv_ref[2, h] for h in heads],
                                      [g_ref[h] for h in heads], [b_ref[h] for h in heads], S_in)
        for h in heads:
            st_ref[h] = S_in[h]
            inv_ref[h] = invs[h]
            o_ref[h] = os_[h]
            s_scr[h] = S_new[h]

    blk3 = pl.BlockSpec((3, GDN_H, GDN_C, LANES), lambda n: (0, 0, n, 0))
    hb = pl.BlockSpec((GDN_H, GDN_C, LANES), lambda n: (0, n, 0))
    return pl.pallas_call(
        body, name="gdn_core", grid=(nchunk,),
        in_specs=[blk3, hb, hb],
        out_specs=[hb, pl.BlockSpec((GDN_H, None, GDN_DK, LANES), lambda n: (0, n, 0, 0)),
                   pl.BlockSpec((GDN_H, None, GDN_C, GDN_C), lambda n: (0, n, 0, 0))],
        out_shape=[SDS((GDN_H, S, LANES), F32), SDS((GDN_H, nchunk, GDN_DK, LANES), F32), SDS((GDN_H, nchunk, GDN_C, GDN_C), F32)],
        scratch_shapes=[pltpu.VMEM((GDN_H, GDN_DK, LANES), F32)],
        compiler_params=_cp(1),
    )(qkv, gc, bt)


def gdn_core_bwd(qkv, gc, bt, states, invs, do, S):
    nchunk = S // GDN_C

    def body(qkv_ref, g_ref, b_ref, st_ref, inv_ref, do_ref, dqkv_ref, dg_ref, db_ref, ds_scr):
        n = pl.program_id(0)

        @pl.when(n == 0)
        def _():
            ds_scr[...] = jnp.zeros_like(ds_scr)

        heads = range(GDN_H)
        saved = [inv_ref[h] for h in heads]
        _, vjp = jax.vjp(lambda *a: _gdn_chunk(*a, invs=saved)[:2],
                         [qkv_ref[0, h] for h in heads], [qkv_ref[1, h] for h in heads], [qkv_ref[2, h] for h in heads],
                         [g_ref[h] for h in heads], [b_ref[h] for h in heads], [st_ref[h] for h in heads])
        dq, dk, dv, dg, db, dS = vjp(([do_ref[h] for h in heads], [ds_scr[h] for h in heads]))
        for h in heads:
            dqkv_ref[0, h] = dq[h]
            dqkv_ref[1, h] = dk[h]
            dqkv_ref[2, h] = dv[h]
            dg_ref[h] = dg[h]
            db_ref[h] = db[h]
            ds_scr[h] = dS[h]

    rev = lambda n: nchunk - 1 - n
    blk3 = pl.BlockSpec((3, GDN_H, GDN_C, LANES), lambda n: (0, 0, rev(n), 0))
    hb = pl.BlockSpec((GDN_H, GDN_C, LANES), lambda n: (0, rev(n), 0))
    return pl.pallas_call(
        body, name="gdn_core_bwd", grid=(nchunk,),
        in_specs=[blk3, hb, hb, pl.BlockSpec((GDN_H, None, GDN_DK, LANES), lambda n: (0, rev(n), 0, 0)),
                  pl.BlockSpec((GDN_H, None, GDN_C, GDN_C), lambda n: (0, rev(n), 0, 0)), hb],
        out_specs=[blk3, hb, hb],
        out_shape=[SDS((3, GDN_H, S, LANES), F32), SDS((GDN_H, S, LANES), F32), SDS((GDN_H, S, LANES), F32)],
        scratch_shapes=[pltpu.VMEM((GDN_H, GDN_DK, LANES), F32)],
        compiler_params=_cp(1),
    )(qkv, gc, bt, states, invs, do)


GDN_MAIN = 4 * GDN_H * LANES
GDN_PROJ = GDN_MAIN + LANES
RT = 256


def gdn_forward(h, w_in, conv_w, alog, dtb, out_gain, w_out):
    S = h.shape[0]
    nt = S // RT
    proj = matmul(h, w_in, "nn", BF16, "gdn_in")
    qkv = gdn_pre(proj, conv_w, S)
    ab_row = Row(proj, (RT, LANES), lambda i: (i, GDN_MAIN // LANES), gdtype=BF16, gshape=(S, LANES), gimap=lambda i: (i, 0))
    hm = lambda i: (0, i, 0)
    hv = GDN_H * LANES
    gc, bt = rowwise(f_gdn_gates, [ab_row], [alog, dtb],
                     [Out((GDN_H, S, LANES), F32, (GDN_H, RT, LANES), hm, lead=GDN_H)] * 2, (nt,), "gdn_gates")
    o, states, invs = gdn_core(qkv, gc, bt, S)
    o_row = Row(o, (GDN_H, RT, LANES), hm, lead=GDN_H)
    z_row = Row(proj, (RT, hv), lambda i: (i, 3), splits=[LANES] * GDN_H, gdtype=BF16, gshape=(S, hv), gimap=lambda i: (i, 0))
    (on,) = rowwise(f_gdn_post, [o_row, z_row], [out_gain],
                    [Out((S, hv), BF16, (RT, hv), lambda i: (i, 0), splits=[LANES] * GDN_H)], (nt,), "gdn_post")
    y = matmul(on, w_out, "nn", BF16, "gdn_out")
    saved = dict(h=h, proj=proj, qkv=qkv, gc=gc, bt=bt, states=states, invs=invs, o=o, on=on, ab_row=ab_row, o_row=o_row, z_row=z_row)
    return y, saved


def gdn_backward(dy, sv, w_in, conv_w, alog, dtb, out_gain, w_out):
    S = dy.shape[0]
    nt = S // RT
    hm = lambda i: (0, i, 0)
    hv = GDN_H * LANES
    don = matmul(dy, w_out, "nt", BF16, "gdn_out_dx")
    d_w_out = matmul(sv["on"], dy, "tn", F32, "gdn_out_dw")
    (do, dz), (d_gain,) = rowwise_bwd(f_gdn_post, [sv["o_row"], sv["z_row"]], [out_gain],
                                      [Row(don, (RT, hv), lambda i: (i, 0), splits=[LANES] * GDN_H)], (nt,), "gdn_post_bwd")
    dqkv, dgc, dbt = gdn_core_bwd(sv["qkv"], sv["gc"], sv["bt"], sv["states"], sv["invs"], do, S)
    head_blk = lambda a: Row(a, (GDN_H, RT, LANES), hm, lead=GDN_H)
    (dab,), (d_alog, d_dtb) = rowwise_bwd(f_gdn_gates, [sv["ab_row"]], [alog, dtb], [head_blk(dgc), head_blk(dbt)],
                                          (nt,), "gdn_gates_bwd")
    dqkv_proj, d_conv = gdn_pre_bwd(sv["proj"], conv_w, dqkv, S)
    dproj = jnp.concatenate([dqkv_proj, dz, dab], axis=1)
    d_w_in = matmul(sv["h"], dproj, "tn", F32, "gdn_in_dw")
    dh = matmul(dproj, w_in, "nt", BF16, "gdn_in_dx")
    return dh, dict(w_in=d_w_in, conv=d_conv, alog=d_alog, dtb=d_dtb, gain=d_gain, w_out=d_w_out)


QB = DSW_SPAN
N_HP = DSW_HG // LANES
PROJ_BLKS = 3 * 3 * N_HP


def _bucket_maps():
    a = np.arange(QB)[:, None]
    j = np.arange(2 * QB)[None, :]
    dist = QB + a - j
    band = (dist >= 0) & (dist <= DSW_SPAN)
    maps = []
    for _, dil in DSW_GROUPS:
        dd = np.maximum(dist, 0) * dil
        max_exact = REL_BUCKETS // 2
        scaled = np.log(np.maximum(dd, 1).astype(np.float32) / np.float32(max_exact)) / np.float32(math.log(REL_MAX_DIST / max_exact))
        large = max_exact + (scaled * np.float32(REL_BUCKETS - max_exact)).astype(np.int32)
        large = np.minimum(large, REL_BUCKETS - 1)
        maps.append(np.where(dd < max_exact, dd, large).astype(np.int32))
    return np.stack(maps), band


def dsw_bias(rel_bias):
    maps, band = _bucket_maps()
    maps = np.where(band[None], maps, -1).astype(np.int32)

    def body(tab_ref, bk_ref, o_ref):
        gh = pl.program_id(0)
        bk = bk_ref[...]
        acc = jnp.full(bk.shape, NEG, F32)
        for b in range(REL_BUCKETS):
            acc = jnp.where(bk == b, tab_ref[b, gh], acc)
        o_ref[...] = acc

    return pl.pallas_call(
        body, name="dsw_bias", grid=(3 * GDN_H,),
        in_specs=[pl.BlockSpec(memory_space=pltpu.SMEM),
                  pl.BlockSpec((None, QB, 2 * QB), lambda gh: (gh // GDN_H, 0, 0))],
        out_specs=pl.BlockSpec((None, QB, 2 * QB), lambda gh: (gh, 0, 0)),
        out_shape=SDS((3 * GDN_H, QB, 2 * QB), F32),
        compiler_params=_cp(1),
    )(rel_bias, jnp.asarray(maps))


def dsw_bias_grad(dbias):
    maps, band = _bucket_maps()
    maps = np.where(band[None], maps, -1).astype(np.int32)

    def body(d_ref, bk_ref, o_ref):
        bk = bk_ref[...]
        d = d_ref[...]
        rows = lax.broadcasted_iota(jnp.int32, (REL_BUCKETS, LANES), 0)
        acc = jnp.zeros((REL_BUCKETS, LANES), F32)
        for b in range(REL_BUCKETS):
            part = jnp.sum(jnp.where(bk == b, d, 0.0), axis=0, keepdims=True)
            val = jnp.sum(part, axis=1, keepdims=True)
            acc = jnp.where(rows == b, val, acc)
        o_ref[...] = acc

    return pl.pallas_call(
        body, name="dsw_bias_grad", grid=(3 * GDN_H,),
        in_specs=[pl.BlockSpec((None, QB, 2 * QB), lambda gh: (gh, 0, 0)),
                  pl.BlockSpec((None, QB, 2 * QB), lambda gh: (gh // GDN_H, 0, 0))],
        out_specs=pl.BlockSpec((None, REL_BUCKETS, LANES), lambda gh: (gh, 0, 0)),
        out_shape=SDS((3 * GDN_H, REL_BUCKETS, LANES), F32),
        compiler_params=_cp(1),
    )(dbias, jnp.asarray(maps))


def _nt(a, b):
    return lax.dot_general(a, b, (((1,), (1,)), ((), ())), preferred_element_type=F32)


def _tn(a, b):
    return lax.dot_general(a, b, (((0,), (0,)), ((), ())), preferred_element_type=F32)


def dsw_group_fwd(qn, kn, proj, bias, gi, S):
    dil = DSW_GROUPS[gi][1]
    sd = S // dil
    nq = sd // QB
    qv = qn.reshape(sd, dil * 3 * DSW_HG)
    kv = kn.reshape(sd, dil * 3 * DSW_HG)
    pv = proj.reshape(sd, dil * 9 * DSW_HG)
    qk_col = lambda hp, r: r * (3 * N_HP) + gi * N_HP + hp
    v_col = lambda hp, r: r * PROJ_BLKS + 2 * 3 * N_HP + gi * N_HP + hp

    def body(q_ref, kp_ref, kc_ref, vp_ref, vc_ref, b_ref, o_ref, l_ref):
        i = pl.program_id(2)
        q = q_ref[...]
        k2 = jnp.concatenate([kp_ref[...], kc_ref[...]], axis=0)
        v2 = jnp.concatenate([vp_ref[...], vc_ref[...]], axis=0).astype(BF16)
        lane_q = lax.broadcasted_iota(jnp.int32, (QB, LANES), 1) < DSW_DH
        lane_k = lax.broadcasted_iota(jnp.int32, (2 * QB, LANES), 1) < DSW_DH
        col = lax.broadcasted_iota(jnp.int32, (QB, 2 * QB), 1)
        first = jnp.logical_and(i == 0, col < QB)
        o_acc = jnp.zeros((QB, LANES), F32)
        lse_b = jnp.zeros((QB, LANES), F32)
        for hh in range(2):
            mq = lane_q if hh == 0 else jnp.logical_not(lane_q)
            mk = lane_k if hh == 0 else jnp.logical_not(lane_k)
            s = _nt(jnp.where(mq, q, 0).astype(BF16), k2) + b_ref[hh]
            s = jnp.where(first, NEG, s)
            mx = jnp.max(s, axis=1, keepdims=True)
            p = jnp.exp(s - mx)
            l = jnp.sum(p, axis=1, keepdims=True)
            oh = jnp.dot(p.astype(BF16), jnp.where(mk, v2, 0).astype(BF16), preferred_element_type=F32) / l
            o_acc = o_acc + oh
            lse_b = jnp.where(mq, mx + jnp.log(l), lse_b)
        o_ref[...] = o_acc
        l_ref[...] = lse_b

    blk = (QB, LANES)
    out_spec = pl.BlockSpec(blk, lambda hp, r, i: (i, r * N_HP + hp))
    o, lse = pl.pallas_call(
        body, name=f"dsw_fwd_g{gi}", grid=(N_HP, dil, nq),
        in_specs=[pl.BlockSpec(blk, lambda hp, r, i: (i, qk_col(hp, r))),
                  pl.BlockSpec(blk, lambda hp, r, i: (jnp.maximum(i - 1, 0), qk_col(hp, r))),
                  pl.BlockSpec(blk, lambda hp, r, i: (i, qk_col(hp, r))),
                  pl.BlockSpec(blk, lambda hp, r, i: (jnp.maximum(i - 1, 0), v_col(hp, r))),
                  pl.BlockSpec(blk, lambda hp, r, i: (i, v_col(hp, r))),
                  pl.BlockSpec((2, QB, 2 * QB), lambda hp, r, i: (gi * N_HP + hp, 0, 0))],
        out_specs=[out_spec, out_spec],
        out_shape=[SDS((sd, dil * DSW_HG), F32)] * 2,
        compiler_params=_cp(3),
    )(qv, kv, kv, pv, pv, bias)
    return o.reshape(S, DSW_HG), lse.reshape(S, DSW_HG)


def dsw_group_bwd(qn, kn, proj, bias, do, o, lse, gi, S):
    dil = DSW_GROUPS[gi][1]
    sd = S // dil
    nq = sd // QB
    qv = qn.reshape(sd, dil * 3 * DSW_HG)
    kv = kn.reshape(sd, dil * 3 * DSW_HG)
    pv = proj.reshape(sd, dil * 9 * DSW_HG)
    dov = do.reshape(sd, dil * DSW_HG)
    ov = o.reshape(sd, dil * DSW_HG)
    lv = lse.reshape(sd, dil * DSW_HG)
    qk_col = lambda hp, r: r * (3 * N_HP) + gi * N_HP + hp
    v_col = lambda hp, r: r * PROJ_BLKS + 2 * 3 * N_HP + gi * N_HP + hp
    o_col = lambda hp, r: r * N_HP + hp
    cur = lambda i: jnp.minimum(i, nq - 1)
    prev = lambda i: jnp.maximum(jnp.minimum(i, nq - 1) - 1, 0)
    done = lambda i: jnp.maximum(i - 1, 0)

    def body(q_ref, kp_ref, kc_ref, vp_ref, vc_ref, b_ref, do_ref, o_ref, l_ref,
             dq_ref, dk_ref, dv_ref, db_ref, dk_scr, dv_scr):
        r, i = pl.program_id(1), pl.program_id(2)

        @pl.when(jnp.logical_and(r == 0, i == 0))
        def _():
            db_ref[...] = jnp.zeros_like(db_ref)

        @pl.when(i == 0)
        def _():
            dk_scr[...] = jnp.zeros_like(dk_scr)
            dv_scr[...] = jnp.zeros_like(dv_scr)

        @pl.when(i < nq)
        def _():
            q = q_ref[...]
            k2 = jnp.concatenate([kp_ref[...], kc_ref[...]], axis=0)
            v2 = jnp.concatenate([vp_ref[...], vc_ref[...]], axis=0).astype(BF16)
            dout = do_ref[...].astype(F32)
            prod = dout * o_ref[...].astype(F32)
            lse_b = l_ref[...]
            lane_q = lax.broadcasted_iota(jnp.int32, (QB, LANES), 1) < DSW_DH
            col = lax.broadcasted_iota(jnp.int32, (QB, 2 * QB), 1)
            first = jnp.logical_and(i == 0, col < QB)
            dq = jnp.zeros((QB, LANES), F32)
            dk2 = jnp.zeros((2 * QB, LANES), F32)
            dv2 = jnp.zeros((2 * QB, LANES), F32)
            for hh in range(2):
                mq = lane_q if hh == 0 else jnp.logical_not(lane_q)
                qm = jnp.where(mq, q, 0).astype(BF16)
                dom = jnp.where(mq, dout, 0.0).astype(BF16)
                s = _nt(qm, k2) + b_ref[hh]
                s = jnp.where(first, NEG, s)
                lse_h = jnp.max(jnp.where(mq, lse_b, NEG), axis=1, keepdims=True)
                p = jnp.exp(s - lse_h)
                delta = jnp.sum(jnp.where(mq, prod, 0.0), axis=1, keepdims=True)
                dp = _nt(dom, v2)
                ds = p * (dp - delta)
                dsb = ds.astype(BF16)
                dq = dq + jnp.where(mq, jnp.dot(dsb, k2, preferred_element_type=F32), 0.0)
                dk2 = dk2 + _tn(dsb, qm)
                dv2 = dv2 + _tn(p.astype(BF16), dom)
                db_ref[hh] += ds
            dq_ref[...] = dq
            dk_ref[...] = dk_scr[...] + dk2[:QB]
            dv_ref[...] = (dv_scr[...] + dv2[:QB]).astype(dv_ref.dtype)
            dk_scr[...] = dk2[QB:]
            dv_scr[...] = dv2[QB:]

        @pl.when(i == nq)
        def _():
            dk_ref[...] = dk_scr[...]
            dv_ref[...] = dv_scr[...].astype(dv_ref.dtype)

    blk = (QB, LANES)
    dq, dk, dv, dbias = pl.pallas_call(
        body, name=f"dsw_bwd_g{gi}", grid=(N_HP, dil, nq + 1),
        in_specs=[pl.BlockSpec(blk, lambda hp, r, i: (cur(i), qk_col(hp, r))),
                  pl.BlockSpec(blk, lambda hp, r, i: (prev(i), qk_col(hp, r))),
                  pl.BlockSpec(blk, lambda hp, r, i: (cur(i), qk_col(hp, r))),
                  pl.BlockSpec(blk, lambda hp, r, i: (prev(i), v_col(hp, r))),
                  pl.BlockSpec(blk, lambda hp, r, i: (cur(i), v_col(hp, r))),
                  pl.BlockSpec((2, QB, 2 * QB), lambda hp, r, i: (gi * N_HP + hp, 0, 0)),
                  pl.BlockSpec(blk, lambda hp, r, i: (cur(i), o_col(hp, r))),
                  pl.BlockSpec(blk, lambda hp, r, i: (cur(i), o_col(hp, r))),
                  pl.BlockSpec(blk, lambda hp, r, i: (cur(i), o_col(hp, r)))],
        out_specs=[pl.BlockSpec(blk, lambda hp, r, i: (cur(i), o_col(hp, r))),
                   pl.BlockSpec(blk, lambda hp, r, i: (done(i), o_col(hp, r))),
                   pl.BlockSpec(blk, lambda hp, r, i: (done(i), o_col(hp, r))),
                   pl.BlockSpec((2, QB, 2 * QB), lambda hp, r, i: (hp, 0, 0))],
        out_shape=[SDS((sd, dil * DSW_HG), F32), SDS((sd, dil * DSW_HG), F32), SDS((sd, dil * DSW_HG), BF16),
                   SDS((GDN_H, QB, 2 * QB), F32)],
        scratch_shapes=[pltpu.VMEM(blk, F32), pltpu.VMEM(blk, F32)],
        compiler_params=_cp(3),
    )(qv, kv, kv, pv, pv, bias, dov, ov, lv)
    return dq.reshape(S, DSW_HG), dk.reshape(S, DSW_HG), dv.reshape(S, DSW_HG), dbias


def dsw_forward(h, w_in, q_gain2, k_gain2, rel_bias, w_out):
    S = h.shape[0]
    nt = S // RT
    nb = 3 * N_HP
    proj = matmul(h, w_in, "nn", F32, "dsw_in")
    width = nb * LANES
    lanes12 = [LANES] * nb
    (qn,) = rowwise(f_qnorm, [Row(proj, (RT, width), lambda i: (i, 0), splits=lanes12)], [q_gain2],
                    [Out((S, width), BF16, (RT, width), lambda i: (i, 0), splits=lanes12)], (nt,), "dsw_qnorm")
    (kn,) = rowwise(f_qknorm, [Row(proj, (RT, width), lambda i: (i, 1), splits=lanes12)], [k_gain2],
                    [Out((S, width), BF16, (RT, width), lambda i: (i, 0), splits=lanes12)], (nt,), "dsw_knorm")
    bias = dsw_bias(rel_bias)
    os_, ls_ = [], []
    for gi in range(3):
        o, l = dsw_group_fwd(qn, kn, proj, bias, gi, S)
        os_.append(o)
        ls_.append(l)
    full = lambda a: Row(a, (RT, DSW_HG), lambda i: (i, 0))
    o, lse = rowwise(f_combine, [full(a) for a in os_ + ls_], [],
                     [Out((S, DSW_HG), BF16, (RT, DSW_HG), lambda i: (i, 0)), Out((S, DSW_HG), F32, (RT, DSW_HG), lambda i: (i, 0))],
                     (nt,), "dsw_combine")
    y = matmul(o, w_out, "nn", F32, "dsw_out")
    return y, dict(h=h, proj=proj, qn=qn, kn=kn, bias=bias, o=o, lse=lse)


def dsw_backward(dy, sv, w_in, q_gain2, k_gain2, w_out):
    S = dy.shape[0]
    nt = S // RT
    nb = 3 * N_HP
    do = matmul(dy, w_out, "nt", BF16, "dsw_out_dx")
    d_w_out = matmul(sv["o"], dy, "tn", F32, "dsw_out_dw")
    pieces_q, pieces_k, pieces_v, dbs = [], [], [], []
    d_qg = jnp.zeros((1, LANES), F32)
    d_kg = jnp.zeros((1, LANES), F32)
    for gi in range(3):
        dq, dk, dv, db = dsw_group_bwd(sv["qn"], sv["kn"], sv["proj"], sv["bias"], do, sv["o"], sv["lse"], gi, S)
        dbs.append(db)
        pieces_v.append(dv)
        for which, dd in ((0, dq), (1, dk)):
            lanes4 = [LANES] * N_HP
            row = Row(sv["proj"], (RT, DSW_HG), lambda i, _o=which * 3 + gi: (i, _o), splits=lanes4,
                      gdtype=BF16, gshape=(S, DSW_HG), gimap=lambda i: (i, 0))
            fn, gain = (f_qnorm, q_gain2) if which == 0 else (f_qknorm, k_gain2)
            (dx,), (dg,) = rowwise_bwd(fn, [row], [gain], [Row(dd, (RT, DSW_HG), lambda i: (i, 0), splits=lanes4)],
                                       (nt,), f"dsw_norm_bwd_{which}{gi}")
            if which == 0:
                pieces_q.append(dx)
                d_qg = d_qg + dg
            else:
                pieces_k.append(dx)
                d_kg = d_kg + dg
    dproj = jnp.concatenate(pieces_q + pieces_k + pieces_v, axis=1)
    d_w_in = matmul(sv["h"], dproj, "tn", F32, "dsw_in_dw")
    dh = matmul(dproj, w_in, "nt", F32, "dsw_in_dx")
    d_rel = dsw_bias_grad(jnp.concatenate(dbs, axis=0))
    return dh, dict(w_in=d_w_in, q_gain2=d_qg, k_gain2=d_kg, rel=d_rel, w_out=d_w_out)


N_LB = DSW_HG // LANES
HALF = DSW_DH // 2


def _lanes(j):
    return slice(LANES * j, LANES * (j + 1))


def _deinterleave(stage, out_ref, dil, rows, dtype):
    for r in range(dil):
        for j in range(N_LB):
            out_ref[r, :, _lanes(j)] = stage[j, pl.ds(r, rows, stride=dil), :].astype(dtype)


def _interleave(in_ref, stage, dil, rows):
    for r in range(dil):
        for j in range(N_LB):
            stage[j, pl.ds(r, rows, stride=dil), :] = in_ref[r, :, _lanes(j)].astype(F32)


def dsw_prep(proj, q_gain2, k_gain2, gi, S):
    dil = DSW_GROUPS[gi][1]
    nt, rows = S // RT, RT // dil

    def body(q_ref, k_ref, v_ref, qg_ref, kg_ref, qo_ref, ko_ref, vo_ref, stage):
        for src, gain_ref, scale, dst in ((q_ref, qg_ref, DSW_DH ** -0.5, qo_ref), (k_ref, kg_ref, 1.0, ko_ref), (v_ref, None, None, vo_ref)):
            for j in range(N_LB):
                val = src[:, _lanes(j)].astype(F32)
                stage[j] = val if gain_ref is None else _qknorm1(val, gain_ref[...], scale)
            _deinterleave(stage, dst, dil, rows, BF16)

    col = lambda which: pl.BlockSpec((RT, DSW_HG), lambda i, _c=which * 3 + gi: (i, _c))
    gspec = pl.BlockSpec((1, LANES), lambda i: (0, 0))
    ospec = pl.BlockSpec((dil, rows, DSW_HG), lambda i: (0, i, 0))
    return pl.pallas_call(
        body, name=f"dsw_prep_g{gi}", grid=(nt,),
        in_specs=[col(0), col(1), col(2), gspec, gspec], out_specs=[ospec] * 3,
        out_shape=[SDS((dil, S // dil, DSW_HG), BF16)] * 3,
        scratch_shapes=[pltpu.VMEM((N_LB, RT, LANES), F32)], compiler_params=_cp(1),
    )(proj, proj, proj, q_gain2, k_gain2)


def dsw_prep_bwd(proj, q_gain2, k_gain2, dqd, dkd, dvd, gi, S):
    dil = DSW_GROUPS[gi][1]
    nt, rows = S // RT, RT // dil

    def body(q_ref, k_ref, qg_ref, kg_ref, dq_ref, dk_ref, dv_ref, oq_ref, ok_ref, ov_ref, dqg_ref, dkg_ref, stage):
        i = pl.program_id(0)

        @pl.when(i == 0)
        def _():
            dqg_ref[...] = jnp.zeros_like(dqg_ref)
            dkg_ref[...] = jnp.zeros_like(dkg_ref)

        for src, gain_ref, scale, cot_ref, dst, dg_ref in ((q_ref, qg_ref, DSW_DH ** -0.5, dq_ref, oq_ref, dqg_ref),
                                                          (k_ref, kg_ref, 1.0, dk_ref, ok_ref, dkg_ref)):
            _interleave(cot_ref, stage, dil, rows)
            for j in range(N_LB):
                _, vjp = jax.vjp(lambda x, g, _s=scale: _qknorm1(x, g, _s), src[:, _lanes(j)].astype(F32), gain_ref[...])
                dx, dg = vjp(stage[j])
                dst[:, _lanes(j)] = dx.astype(dst.dtype)
                dg_ref[...] += dg
        _interleave(dv_ref, stage, dil, rows)
        for j in range(N_LB):
            ov_ref[:, _lanes(j)] = stage[j].astype(ov_ref.dtype)

    col = lambda which: pl.BlockSpec((RT, DSW_HG), lambda i, _c=which * 3 + gi: (i, _c))
    gspec = pl.BlockSpec((1, LANES), lambda i: (0, 0))
    dspec = pl.BlockSpec((dil, rows, DSW_HG), lambda i: (0, i, 0))
    nspec = pl.BlockSpec((RT, DSW_HG), lambda i: (i, 0))
    return pl.pallas_call(
        body, name=f"dsw_prep_bwd_g{gi}", grid=(nt,),
        in_specs=[col(0), col(1), gspec, gspec, dspec, dspec, dspec], out_specs=[nspec] * 3 + [gspec] * 2,
        out_shape=[SDS((S, DSW_HG), BF16)] * 3 + [SDS((1, LANES), F32)] * 2,
        scratch_shapes=[pltpu.VMEM((N_LB, RT, LANES), F32)], compiler_params=_cp(1),
    )(proj, proj, q_gain2, k_gain2, dqd, dkd, dvd)


def _head_masks(rows):
    lane = lax.broadcasted_iota(jnp.int32, (rows, LANES), 1)
    return lane < DSW_DH, (lane % DSW_DH) < HALF


def dsw_attn_fwd(qd, kd, vd, bias, gi, S):
    dil = DSW_GROUPS[gi][1]
    sd = S // dil
    nq = sd // QB

    def body(q_ref, k_ref, v_ref, b_ref, o_ref, l_ref, kp_scr, vp_scr):
        i = pl.program_id(1)

        @pl.when(i == 0)
        def _():
            kp_scr[...] = jnp.zeros_like(kp_scr)
            vp_scr[...] = jnp.zeros_like(vp_scr)

        lo_q, _ = _head_masks(QB)
        lo_k, _ = _head_masks(2 * QB)
        col = lax.broadcasted_iota(jnp.int32, (QB, 2 * QB), 1)
        first = jnp.logical_and(i == 0, col < QB)
        hps, heads = range(N_HP), range(2 * N_HP)
        k2s = [jnp.concatenate([kp_scr[:, _lanes(hp)], k_ref[:, _lanes(hp)]], axis=0) for hp in hps]
        v2s = [jnp.concatenate([vp_scr[:, _lanes(hp)], v_ref[:, _lanes(hp)]], axis=0) for hp in hps]
        qs = [q_ref[:, _lanes(hp)] for hp in hps]
        k_now, v_now = k_ref[...], v_ref[...]
        mqs = [lo_q if h % 2 == 0 else jnp.logical_not(lo_q) for h in heads]
        mks = [lo_k if h % 2 == 0 else jnp.logical_not(lo_k) for h in heads]
        ss = [jnp.where(first, NEG, _nt(jnp.where(mqs[h], qs[h // 2], 0).astype(BF16), k2s[h // 2]) + b_ref[h]) for h in heads]
        mxs = [jnp.max(s, axis=1, keepdims=True) for s in ss]
        ps = [jnp.exp(s - mx) for s, mx in zip(ss, mxs)]
        ls = [jnp.sum(p, axis=1, keepdims=True) for p in ps]
        ohs = [jnp.dot(ps[h].astype(BF16), jnp.where(mks[h], v2s[h // 2], 0).astype(BF16), preferred_element_type=F32) / ls[h] for h in heads]
        lse_h = [mx + jnp.log(l) for mx, l in zip(mxs, ls)]
        for hp in hps:
            o_ref[:, _lanes(hp)] = ohs[2 * hp] + ohs[2 * hp + 1]
            l_ref[:, _lanes(hp)] = jnp.where(lo_q, lse_h[2 * hp], lse_h[2 * hp + 1])
        kp_scr[...] = k_now
        vp_scr[...] = v_now

    blk = pl.BlockSpec((None, QB, DSW_HG), lambda r, i: (r, i, 0))
    return pl.pallas_call(
        body, name=f"dsw_attn_g{gi}", grid=(dil, nq),
        in_specs=[blk, blk, blk, pl.BlockSpec((GDN_H, QB, 2 * QB), lambda r, i: (gi, 0, 0))],
        out_specs=[blk, blk], out_shape=[SDS((dil, sd, DSW_HG), F32)] * 2,
        scratch_shapes=[pltpu.VMEM((QB, DSW_HG), BF16)] * 2, compiler_params=_cp(2),
    )(qd, kd, vd, bias)


def dsw_attn_bwd(qd, kd, vd, bias, dod, statd, gi, S):
    dil = DSW_GROUPS[gi][1]
    sd = S // dil
    nq = sd // QB
    cur = lambda i: jnp.minimum(i, nq - 1)
    done = lambda i: jnp.maximum(i - 1, 0)

    def body(q_ref, k_ref, v_ref, b_ref, do_ref, st_ref, dq_ref, dk_ref, dv_ref, db_ref, kp_scr, vp_scr, dk_scr, dv_scr):
        r, i = pl.program_id(0), pl.program_id(1)

        @pl.when(jnp.logical_and(r == 0, i == 0))
        def _():
            db_ref[...] = jnp.zeros_like(db_ref)

        @pl.when(i == 0)
        def _():
            for scr in (kp_scr, vp_scr, dk_scr, dv_scr):
                scr[...] = jnp.zeros_like(scr)

        @pl.when(i < nq)
        def _():
            lo_q, first_half = _head_masks(QB)
            col = lax.broadcasted_iota(jnp.int32, (QB, 2 * QB), 1)
            first = jnp.logical_and(i == 0, col < QB)
            hps, heads = range(N_HP), range(2 * N_HP)
            k2s = [jnp.concatenate([kp_scr[:, _lanes(hp)], k_ref[:, _lanes(hp)]], axis=0) for hp in hps]
            v2s = [jnp.concatenate([vp_scr[:, _lanes(hp)], v_ref[:, _lanes(hp)]], axis=0) for hp in hps]
            qs = [q_ref[:, _lanes(hp)] for hp in hps]
            douts = [do_ref[:, _lanes(hp)] for hp in hps]
            stats = [st_ref[:, _lanes(hp)] for hp in hps]
            dkc = [dk_scr[:, _lanes(hp)] for hp in hps]
            dvc = [dv_scr[:, _lanes(hp)] for hp in hps]
            k_now, v_now = k_ref[...], v_ref[...]
            mqs = [lo_q if h % 2 == 0 else jnp.logical_not(lo_q) for h in heads]
            qms = [jnp.where(mqs[h], qs[h // 2], 0).astype(BF16) for h in heads]
            doms = [jnp.where(mqs[h], douts[h // 2], 0).astype(BF16) for h in heads]
            ss = [jnp.where(first, NEG, _nt(qms[h], k2s[h // 2]) + b_ref[h]) for h in heads]
            lses = [jnp.max(jnp.where(jnp.logical_and(mqs[h], first_half), stats[h // 2], NEG), axis=1, keepdims=True) for h in heads]
            deltas = [jnp.max(jnp.where(jnp.logical_and(mqs[h], jnp.logical_not(first_half)), stats[h // 2], NEG), axis=1, keepdims=True)
                      for h in heads]
            ps = [jnp.exp(ss[h] - lses[h]) for h in heads]
            dss = [ps[h] * (_nt(doms[h], v2s[h // 2]) - deltas[h]) for h in heads]
            dsbs = [d.astype(BF16) for d in dss]
            dqh = [jnp.where(mqs[h], jnp.dot(dsbs[h], k2s[h // 2], preferred_element_type=F32), 0.0) for h in heads]
            dkh = [_tn(dsbs[h], qms[h]) for h in heads]
            dvh = [_tn(ps[h].astype(BF16), doms[h]) for h in heads]
            for h in heads:
                db_ref[h] += dss[h]
            for hp in hps:
                dk2 = dkh[2 * hp] + dkh[2 * hp + 1]
                dv2 = dvh[2 * hp] + dvh[2 * hp + 1]
                dq_ref[:, _lanes(hp)] = dqh[2 * hp] + dqh[2 * hp + 1]
                dk_ref[:, _lanes(hp)] = dkc[hp] + dk2[:QB]
                dv_ref[:, _lanes(hp)] = (dvc[hp] + dv2[:QB]).astype(dv_ref.dtype)
                dk_scr[:, _lanes(hp)] = dk2[QB:]
                dv_scr[:, _lanes(hp)] = dv2[QB:]
            kp_scr[...] = k_now
            vp_scr[...] = v_now

        @pl.when(i == nq)
        def _():
            dk_ref[...] = dk_scr[...]
            dv_ref[...] = dv_scr[...].astype(dv_ref.dtype)

    blk = pl.BlockSpec((None, QB, DSW_HG), lambda r, i: (r, cur(i), 0))
    oblk = pl.BlockSpec((None, QB, DSW_HG), lambda r, i: (r, done(i), 0))
    return pl.pallas_call(
        body, name=f"dsw_attn_bwd_g{gi}", grid=(dil, nq + 1),
        in_specs=[blk, blk, blk, pl.BlockSpec((GDN_H, QB, 2 * QB), lambda r, i: (gi, 0, 0)), blk, blk],
        out_specs=[blk, oblk, oblk, pl.BlockSpec((GDN_H, QB, 2 * QB), lambda r, i: (0, 0, 0))],
        out_shape=[SDS((dil, sd, DSW_HG), F32), SDS((dil, sd, DSW_HG), F32), SDS((dil, sd, DSW_HG), BF16),
                   SDS((GDN_H, QB, 2 * QB), F32)],
        scratch_shapes=[pltpu.VMEM((QB, DSW_HG), BF16)] * 2 + [pltpu.VMEM((QB, DSW_HG), F32)] * 2,
        compiler_params=_cp(2),
    )(qd, kd, vd, bias, dod, statd)


def dsw_combine(ods, lseds, S):
    nt = S // RT
    dils = [d for _, d in DSW_GROUPS]

    def body(*refs):
        ins, (o_ref, l_ref), stages = refs[:6], refs[6:8], refs[8:]
        for g in range(3):
            _interleave(ins[g], stages[g], dils[g], RT // dils[g])
            _interleave(ins[3 + g], stages[3 + g], dils[g], RT // dils[g])
        for j in range(N_LB):
            o, lse = f_combine(None, *[st[j] for st in stages])
            o_ref[:, _lanes(j)] = o.astype(o_ref.dtype)
            l_ref[:, _lanes(j)] = lse

    dspec = lambda d: pl.BlockSpec((d, RT // d, DSW_HG), lambda i: (0, i, 0))
    nspec = pl.BlockSpec((RT, DSW_HG), lambda i: (i, 0))
    return pl.pallas_call(
        body, name="dsw_combine", grid=(nt,),
        in_specs=[dspec(d) for d in dils] * 2, out_specs=[nspec, nspec],
        out_shape=[SDS((S, DSW_HG), BF16), SDS((S, DSW_HG), F32)],
        scratch_shapes=[pltpu.VMEM((N_LB, RT, LANES), F32)] * 6, compiler_params=_cp(1),
    )(*ods, *lseds)


def dsw_bwd_prep(do, o, lse, S):
    nt = S // RT
    dils = [d for _, d in DSW_GROUPS]

    def body(do_ref, o_ref, l_ref, *rest):
        outs, (st_do, st_stat) = rest[:6], rest[6:]
        lo, first_half = _head_masks(RT)
        for j in range(N_LB):
            dout = do_ref[:, _lanes(j)]
            prod = dout * o_ref[:, _lanes(j)].astype(F32)
            s_all = jnp.sum(prod, axis=1, keepdims=True)
            s_lo = jnp.sum(jnp.where(lo, prod, 0.0), axis=1, keepdims=True)
            delta = jnp.where(lo, s_lo, s_all - s_lo)
            st_do[j] = dout
            st_stat[j] = jnp.where(first_half, l_ref[:, _lanes(j)], delta)
        for g in range(3):
            _deinterleave(st_do, outs[g], dils[g], RT // dils[g], BF16)
            _deinterleave(st_stat, outs[3 + g], dils[g], RT // dils[g], F32)

    nspec = pl.BlockSpec((RT, DSW_HG), lambda i: (i, 0))
    dspec = lambda d: pl.BlockSpec((d, RT // d, DSW_HG), lambda i: (0, i, 0))
    res = pl.pallas_call(
        body, name="dsw_bwd_prep", grid=(nt,),
        in_specs=[nspec] * 3, out_specs=[dspec(d) for d in dils] * 2,
        out_shape=[SDS((d, S // d, DSW_HG), BF16) for d in dils] + [SDS((d, S // d, DSW_HG), F32) for d in dils],
        scratch_shapes=[pltpu.VMEM((N_LB, RT, LANES), F32)] * 2, compiler_params=_cp(1),
    )(do, o, lse)
    return res[:3], res[3:]


def dsw_forward(h, w_in, q_gain2, k_gain2, rel_bias, w_out):
    S = h.shape[0]
    proj = matmul(h, w_in, "nn", BF16, "dsw_in", col_shards=N_SHARD)
    bias = dsw_bias(rel_bias)
    qkv, ods, lseds = [], [], []
    for gi in range(3):
        qd, kd, vd = dsw_prep(proj, q_gain2, k_gain2, gi, S)
        od, ld = dsw_attn_fwd(qd, kd, vd, bias, gi, S)
        qkv.append((qd, kd, vd))
        ods.append(od)
        lseds.append(ld)
    o, lse = dsw_combine(ods, lseds, S)
    y = matmul(o, w_out, "nn", BF16, "dsw_out", col_shards=N_SHARD)
    return y, dict(h=h, proj=proj, qkv=qkv, bias=bias, o=o, lse=lse)


def dsw_backward(dy, sv, w_in, q_gain2, k_gain2, w_out):
    S = dy.shape[0]
    do = matmul(dy, w_out, "nt", F32, "dsw_out_dx", col_shards=N_SHARD)
    d_w_out = matmul(sv["o"], dy, "tn", F32, "dsw_out_dw", col_shards=N_SHARD)
    dods, statds = dsw_bwd_prep(do, sv["o"], sv["lse"], S)
    pieces_q, pieces_k, pieces_v, dbs = [], [], [], []
    d_qg = jnp.zeros((1, LANES), F32)
    d_kg = jnp.zeros((1, LANES), F32)
    for gi in range(3):
        qd, kd, vd = sv["qkv"][gi]
        dqd, dkd, dvd, db = dsw_attn_bwd(qd, kd, vd, sv["bias"], dods[gi], statds[gi], gi, S)
        dq, dk, dv, dqg, dkg = dsw_prep_bwd(sv["proj"], q_gain2, k_gain2, dqd, dkd, dvd, gi, S)
        dbs.append(db)
        pieces_q.append(dq)
        pieces_k.append(dk)
        pieces_v.append(dv)
        d_qg = d_qg + dqg
        d_kg = d_kg + dkg
    dproj = jnp.concatenate(pieces_q + pieces_k + pieces_v, axis=1)
    d_w_in = matmul(sv["h"], dproj, "tn", F32, "dsw_in_dw", col_shards=N_SHARD)
    dh = matmul(dproj, w_in, "nt", BF16, "dsw_in_dx", col_shards=N_SHARD)
    d_rel = dsw_bias_grad(jnp.concatenate(dbs, axis=0))
    return dh, dict(w_in=d_w_in, q_gain2=d_qg, k_gain2=d_kg, rel=d_rel, w_out=d_w_out)


FT = 256


FUSE_M = 512


def ffn_in_act(h, w_in, name):
    S = h.shape[0]
    half = FFN // 2

    def body(h_ref, wg_ref, wu_ref, gu_ref, a_ref):
        j = pl.program_id(1)
        sub = FUSE_M // 2
        for part in range(2):
            rows = slice(part * sub, (part + 1) * sub)
            hb = h_ref[rows, :]
            g = jnp.dot(hb, wg_ref[...], preferred_element_type=F32)
            u = jnp.dot(hb, wu_ref[...], preferred_element_type=F32)
            a_ref[rows, :] = (_silu(g) * u).astype(a_ref.dtype)
            for jj in range(2):
                @pl.when(j == jj)
                def _(g=g, u=u, jj=jj, rows=rows):
                    gu_ref[rows, jj * half:(jj + 1) * half] = g.astype(gu_ref.dtype)
                    gu_ref[rows, FFN + jj * half:FFN + (jj + 1) * half] = u.astype(gu_ref.dtype)

    return pl.pallas_call(
        body, name=name, grid=(S // FUSE_M, 2),
        in_specs=[pl.BlockSpec((FUSE_M, D), lambda i, j: (i, 0)),
                  pl.BlockSpec((None, D, half), lambda i, j: (j, 0, 0)),
                  pl.BlockSpec((None, D, half), lambda i, j: (j + 2, 0, 0))],
        out_specs=[pl.BlockSpec((FUSE_M, 2 * FFN), lambda i, j: (i, 0)), pl.BlockSpec((FUSE_M, half), lambda i, j: (i, j))],
        out_shape=[SDS((S, 2 * FFN), BF16), SDS((S, FFN), BF16)],
        compiler_params=_cp(2),
    )(h, w_in, w_in)


def ffn_forward(h, w_in, w_out, tag):
    gu, a = ffn_in_act(h, w_in, f"ffn_in_act_{tag}")
    gu_row = Row(gu, (FT, 2 * FFN), lambda i: (i, 0), splits=[FFN, FFN], gdtype=BF16)
    f = matmul(a, w_out, "nn", BF16, f"ffn_out_{tag}")
    return f, dict(h=h, gu_row=gu_row, a=a)


def ffn_backward(df, sv, w_in, w_out, tag):
    S = df.shape[0]
    da = matmul(df, w_out, "nt", BF16, f"ffn_out_dx_{tag}")
    d_w_out = matmul(sv["a"], df, "tn", F32, f"ffn_out_dw_{tag}")
    (dgu,), _ = rowwise_bwd(f_swiglu, [sv["gu_row"]], [], [Row(da, (FT, FFN), lambda i: (i, 0))], (S // FT,), f"ffn_act_bwd_{tag}")
    d_w_in = matmul(sv["h"], dgu, "tn", F32, f"ffn_in_dw_{tag}", col_shards=N_SHARD)
    dh = matmul(dgu, w_in, "nt", BF16, f"ffn_in_dx_{tag}", col_shards=N_SHARD)
    return dh, d_w_in, d_w_out


def f_norm_only(ids, x, gain, sc, sh):
    return (_normmod(x, gain, sc, sh),)


WT = 512


def _wide(a, **kw):
    return Row(a, (WT, D), lambda i: (i, 0), **kw)


def _wide_out(S, dtype):
    return Out((S, D), dtype, (WT, D), lambda i: (i, 0))


def adamw(w, g, m, v, name):
    shape = w.shape
    C = shape[-1]
    R = int(np.prod(shape[:-1]))
    w2, g2, m2, v2 = (a.reshape(R, C) for a in (w, g, m, v))
    br = R
    if R > 256:
        br = max(b for b in range(8, 257, 8) if R % b == 0)
    c1 = 1.0 / (1.0 - ADAM_B1 ** ADAM_STEP)
    c2 = 1.0 / (1.0 - ADAM_B2 ** ADAM_STEP)

    def body(w_ref, g_ref, m_ref, v_ref, d_ref, nm_ref, nv_ref):
        gg = g_ref[...]
        mm_ = ADAM_B1 * m_ref[...] + (1.0 - ADAM_B1) * gg
        vv = ADAM_B2 * v_ref[...] + (1.0 - ADAM_B2) * (gg * gg)
        d_ref[...] = -ADAM_LR * ((mm_ * c1) / (jnp.sqrt(vv * c2) + ADAM_EPS) + ADAM_WD * w_ref[...])
        nm_ref[...] = mm_
        nv_ref[...] = vv

    spec = pl.BlockSpec((br, C), lambda i: (i, 0))
    d, nm, nv = pl.pallas_call(
        body, name=name, grid=(R // br,), in_specs=[spec] * 4, out_specs=[spec] * 3,
        out_shape=[SDS((R, C), F32)] * 3, compiler_params=_cp(1),
    )(w2, g2, m2, v2)
    return d.reshape(shape), nm.reshape(shape), nv.reshape(shape)


def _place():
    x, y, c = lax.axis_index("x"), lax.axis_index("y"), lax.axis_index("c")
    chips = [(1 - x, y), (x, 1 - y), (1 - x, 1 - y)]
    return x, y, c, chips


def all_gather_small(blk, name):
    m_per, n = blk.shape

    def body(x_ref, out_ref, send_sems, recv_sems, local_sem):
        x, y, c, chips = _place()
        me, sibling = (x, y, c), (x, y, 1 - c)

        def rows(px, py, pc):
            return out_ref.at[pl.ds((4 * px + 2 * py + pc) * m_per, m_per), :]

        def copy(k, block, to, src=None):
            return pltpu.make_async_remote_copy(
                src_ref=rows(*block) if src is None else src, dst_ref=rows(*block),
                send_sem=send_sems.at[k], recv_sem=recv_sems.at[k], device_id=to, device_id_type=MESH)

        mine = pltpu.make_async_copy(x_ref, rows(*me), local_sem)
        mine.start()
        first = [copy(0, me, sibling, src=x_ref)]
        first += [copy(1 + j, me, (*chip, c), src=x_ref) for j, chip in enumerate(chips)]
        for cp in first:
            cp.start()
        passed = [copy(4 + j, (*chip, c), sibling) for j, chip in enumerate(chips)]
        for j, chip in enumerate(chips):
            copy(1 + j, (*chip, c), me).wait_recv()
            passed[j].start()
        copy(0, sibling, me).wait_recv()
        for j, chip in enumerate(chips):
            copy(4 + j, (*chip, 1 - c), me).wait_recv()
        for cp in first + passed:
            cp.wait_send()
        mine.wait()

    return pl.pallas_call(
        body, name=name, out_shape=SDS((N_DEV * m_per, n), blk.dtype),
        in_specs=[pl.BlockSpec(memory_space=pltpu.VMEM)], out_specs=pl.BlockSpec(memory_space=pltpu.VMEM),
        scratch_shapes=[pltpu.SemaphoreType.DMA((7,)), pltpu.SemaphoreType.DMA((7,)), pltpu.SemaphoreType.DMA],
    )(blk)


def _half(cc, rh):
    return pl.ds(pl.multiple_of(cc * rh, 16), rh)


def all_gather_shards(ws):
    n = len(ws)

    def body(*refs):
        w_refs, out_refs = refs[:n], refs[n:2 * n]
        send_sems, recv_sems, local_sems, own_sems = refs[2 * n:]
        x, y, c, chips = _place()
        sibling = (x, y, 1 - c)
        s_me = 2 * x + y

        def copy(k, src, dst, to):
            return pltpu.make_async_remote_copy(src_ref=src, dst_ref=dst, send_sem=send_sems.at[k], recv_sem=recv_sems.at[k],
                                                device_id=to, device_id_type=MESH)

        local, sends, passed = [], [], []
        for k in range(n):
            rh = ws[k].shape[0] // 2
            cp = pltpu.make_async_remote_copy(src_ref=w_refs[k], dst_ref=out_refs[k].at[s_me], send_sem=local_sems.at[k],
                                              recv_sem=own_sems.at[k], device_id=sibling, device_id_type=MESH)
            cp.start()
            local.append(cp)
            for j, chip in enumerate(chips):
                sd = copy(6 * k + j, w_refs[k].at[_half(c, rh)], out_refs[k].at[s_me, _half(c, rh)], (*chip, c))
                sd.start()
                sends.append(sd)
        for k in range(n):
            rh = ws[k].shape[0] // 2
            for j, (px, py) in enumerate(chips):
                got = out_refs[k].at[2 * px + py, _half(c, rh)]
                copy(6 * k + j, got, got, (px, py, c)).wait_recv()
                fw = copy(6 * k + 3 + j, got, got, sibling)
                fw.start()
                passed.append(fw)
        for k in range(n):
            rh = ws[k].shape[0] // 2
            for j, (px, py) in enumerate(chips):
                got = out_refs[k].at[2 * px + py, _half(1 - c, rh)]
                copy(6 * k + 3 + j, got, got, sibling).wait_recv()
        for cp in sends + passed:
            cp.wait_send()
        for cp in local:
            cp.wait()

    return pl.pallas_call(
        body, name="weights_all_gather", out_shape=[SDS((N_SHARD,) + w.shape, w.dtype) for w in ws],
        in_specs=[ANY] * n, out_specs=[ANY] * n,
        scratch_shapes=[pltpu.SemaphoreType.DMA((6 * n,)), pltpu.SemaphoreType.DMA((6 * n,)), pltpu.SemaphoreType.DMA((n,)),
                        pltpu.SemaphoreType.DMA((n,))],
    )(*ws)


def sibling_exchange(sends, name):
    n = len(sends)

    def body(*refs):
        s_refs, o_refs, send_sems, recv_sems = refs[:n], refs[n:2 * n], refs[2 * n], refs[2 * n + 1]
        x, y, c, _ = _place()
        cps = [pltpu.make_async_remote_copy(src_ref=s_refs[k], dst_ref=o_refs[k], send_sem=send_sems.at[k], recv_sem=recv_sems.at[k],
                                            device_id=(x, y, 1 - c), device_id_type=MESH) for k in range(n)]
        for cp in cps:
            cp.start()
        for cp in cps:
            cp.wait()

    return pl.pallas_call(
        body, name=name, out_shape=[SDS(s.shape, s.dtype) for s in sends], in_specs=[ANY] * n, out_specs=[ANY] * n,
        scratch_shapes=[pltpu.SemaphoreType.DMA((n,)), pltpu.SemaphoreType.DMA((n,))],
    )(*sends)


def scatter_to_chips(parts):
    n = len(parts)

    def body(*refs):
        p_refs, o_refs, send_sems, recv_sems = refs[:n], refs[n:2 * n], refs[2 * n], refs[2 * n + 1]
        x, y, c, chips = _place()
        cps = []
        for k in range(n):
            for j, (px, py) in enumerate(chips):
                cp = pltpu.make_async_remote_copy(src_ref=p_refs[k].at[2 * px + py], dst_ref=o_refs[k].at[j],
                                                  send_sem=send_sems.at[3 * k + j], recv_sem=recv_sems.at[3 * k + j],
                                                  device_id=(px, py, c), device_id_type=MESH)
                cp.start()
                cps.append(cp)
        for cp in cps:
            cp.wait()

    return pl.pallas_call(
        body, name="grads_scatter", out_shape=[SDS((3,) + p.shape[1:], p.dtype) for p in parts], in_specs=[ANY] * n, out_specs=[ANY] * n,
        scratch_shapes=[pltpu.SemaphoreType.DMA((3 * n,)), pltpu.SemaphoreType.DMA((3 * n,))],
    )(*parts)


def merge_halves(halves):
    n = len(halves)

    def body(*refs):
        h_refs, o_refs = refs[:n], refs[n:2 * n]
        send_sems, recv_sems, local_sems = refs[2 * n:]
        x, y, c, _ = _place()
        local, cps = [], []
        for k in range(n):
            rh = halves[k].shape[0]
            lc = pltpu.make_async_copy(h_refs[k], o_refs[k].at[_half(c, rh)], local_sems.at[k])
            lc.start()
            local.append(lc)
            cp = pltpu.make_async_remote_copy(src_ref=h_refs[k], dst_ref=o_refs[k].at[_half(c, rh)], send_sem=send_sems.at[k],
                                              recv_sem=recv_sems.at[k], device_id=(x, y, 1 - c), device_id_type=MESH)
            cp.start()
            cps.append(cp)
        for k in range(n):
            rh = halves[k].shape[0]
            got = o_refs[k].at[_half(1 - c, rh)]
            pltpu.make_async_remote_copy(src_ref=got, dst_ref=got, send_sem=send_sems.at[k], recv_sem=recv_sems.at[k],
                                         device_id=(x, y, 1 - c), device_id_type=MESH).wait_recv()
        for cp in cps:
            cp.wait_send()
        for lc in local:
            lc.wait()

    return pl.pallas_call(
        body, name="grads_merge_halves", out_shape=[SDS((2 * h.shape[0], h.shape[1]), h.dtype) for h in halves],
        in_specs=[ANY] * n, out_specs=[ANY] * n,
        scratch_shapes=[pltpu.SemaphoreType.DMA((n,)), pltpu.SemaphoreType.DMA((n,)), pltpu.SemaphoreType.DMA((n,))],
    )(*halves)


def add_rows(arrs, out_dtype, name, rt=256):
    Rr, W = arrs[0].shape

    def fn(ids, *vals):
        acc = vals[0]
        for v in vals[1:]:
            acc = acc + v
        return (acc,)

    t = rt if Rr % rt == 0 else max(b for b in range(16, rt + 1, 16) if Rr % b == 0)
    (out,) = rowwise(fn, [Row(a, (t, W), lambda i: (i, 0)) for a in arrs], [],
                     [Out((Rr, W), out_dtype, (t, W), lambda i: (i, 0))], (Rr // t,), name)
    return out


HBM_SPEC = pl.BlockSpec(memory_space=pltpu.HBM)
SEM_SPEC = pl.BlockSpec(memory_space=pltpu.SEMAPHORE)
DATAFLOW = pltpu.SideEffectType.DATAFLOW_SIDE_EFFECTING


def _in_hbm(a):
    return pltpu.with_memory_space_constraint(a, pltpu.HBM)


def _gather_copies(w_refs, land_refs, send_sems, recv_sems):
    x, y, c, chips = _place()
    targets = [(x, y, 1 - c)] + [(*chip, c) for chip in chips]
    cps = []
    for k, (w_ref, land_ref) in enumerate(zip(w_refs, land_refs)):
        for j, to in enumerate(targets):
            cps.append(pltpu.make_async_remote_copy(src_ref=w_ref, dst_ref=land_ref.at[2 * x + y], send_sem=send_sems.at[4 * k + j],
                                                    recv_sem=recv_sems.at[4 * k + j], device_id=to, device_id_type=MESH))
    return cps


def _scatter_copies(p_refs, land_refs, send_sems, recv_sems):
    x, y, c, chips = _place()
    cps = []
    for k, (p_ref, land_ref) in enumerate(zip(p_refs, land_refs)):
        for j, (px, py) in enumerate(chips):
            cps.append(pltpu.make_async_remote_copy(src_ref=p_ref.at[2 * px + py], dst_ref=land_ref.at[j], send_sem=send_sems.at[3 * k + j],
                                                    recv_sem=recv_sems.at[3 * k + j], device_id=(px, py, c), device_id_type=MESH))
    return cps


def copies_start(srcs, land_shapes, make_copies, per_src, name):
    n = len(srcs)
    m = per_src * n

    def body(*refs):
        src_refs, land_refs = refs[:n], refs[n:2 * n]
        send_sems, recv_sems, token = refs[2 * n], refs[2 * n + 1], refs[-1]
        for cp in make_copies(src_refs, land_refs, send_sems, recv_sems):
            cp.start()
        token[...] = jnp.zeros_like(token)

    lands = [lax.empty(shp, s.dtype) for shp, s in zip(land_shapes, srcs)]
    res = pl.pallas_call(
        body, name=name,
        out_shape=(pltpu.SemaphoreType.DMA((m,)), pltpu.SemaphoreType.DMA((m,)), *[pltpu.HBM(s.shape, s.dtype) for s in srcs],
                   *[pltpu.HBM(shp, s.dtype) for shp, s in zip(land_shapes, srcs)], SDS((8, LANES), F32)),
        in_specs=[HBM_SPEC] * (2 * n),
        out_specs=(SEM_SPEC, SEM_SPEC, *[HBM_SPEC] * (2 * n), pl.BlockSpec(memory_space=pltpu.VMEM)),
        input_output_aliases={i: 2 + i for i in range(2 * n)},
        compiler_params=pltpu.CompilerParams(has_side_effects=DATAFLOW),
    )(*[_in_hbm(s) for s in srcs], *[_in_hbm(l) for l in lands])
    return res[0], res[1], list(res[2:2 + n]), list(res[2 + n:2 + 2 * n]), res[-1]


def copies_wait(send_sems, recv_sems, srcs, lands, make_copies, after, name):
    n = len(srcs)

    def body(*refs):
        src_refs, land_refs = refs[:n], refs[n:2 * n]
        for cp in make_copies(src_refs, land_refs, refs[2 * n], refs[2 * n + 1]):
            cp.wait_send()
            cp.wait_recv()

    res = pl.pallas_call(
        body, name=name,
        out_shape=(*[pltpu.HBM(s.shape, s.dtype) for s in srcs], *[pltpu.HBM(l.shape, l.dtype) for l in lands]),
        in_specs=[HBM_SPEC] * (2 * n) + [SEM_SPEC, SEM_SPEC, ANY],
        out_specs=tuple([HBM_SPEC] * (2 * n)),
        input_output_aliases={i: i for i in range(2 * n)},
        compiler_params=pltpu.CompilerParams(has_side_effects=DATAFLOW),
    )(*srcs, *lands, send_sems, recv_sems, after)
    return list(res[n:])


PACK = (("gdn_w_in", 2), ("gdn_w_out", 1), ("w_ffn_in", 2), ("w_ffn_out", 1), ("dsw_w_in", 2), ("dsw_w_out", 2))
PACK_ALIGN = 32


def _pack_rows(sizes):
    total = sum(sizes)
    rows = -(-total // D)
    return -(-rows // PACK_ALIGN) * PACK_ALIGN


def pack_blocks(blocks, dtype):
    flat = [b.astype(dtype).reshape(-1) for b in blocks]
    total = sum(f.shape[0] for f in flat)
    R = _pack_rows([f.shape[0] for f in flat])
    flat.append(jnp.zeros((R * D - total,), dtype))
    return jnp.concatenate(flat).reshape(R, D)


def unpack_blocks(buf, shapes):
    flat = buf.reshape(-1)
    out, off = [], 0
    for shp in shapes:
        n = int(np.prod(shp))
        out.append(flat[off:off + n].reshape(shp))
        off += n
    return out


def _shard_slice(a, axis, s):
    n = a.shape[axis] // N_SHARD
    return lax.slice_in_dim(a, s * n, (s + 1) * n, axis=axis)


def _pad_lanes(v):
    return jnp.concatenate([v.astype(F32), jnp.zeros((LANES - v.shape[0],), F32)])[None]


def kernel(x, c, w_ada, b_ada, norm_mix, norm_ffn, w_ffn_in, w_ffn_out, gdn_w_in, gdn_conv, gdn_a_log, gdn_dt_bias, gdn_out_norm, gdn_w_out, dsw_w_in, dsw_q_norm, dsw_k_norm, dsw_w_out, rel_bias, loss_target, m_w_ada, m_b_ada, m_norm_mix, m_norm_ffn, m_w_ffn_in, m_w_ffn_out, m_gdn_w_in, m_gdn_conv, m_gdn_a_log, m_gdn_dt_bias, m_gdn_out_norm, m_gdn_w_out, m_dsw_w_in, m_dsw_q_norm, m_dsw_k_norm, m_dsw_w_out, m_rel_bias, v_w_ada, v_b_ada, v_norm_mix, v_norm_ffn, v_w_ffn_in, v_w_ffn_out, v_gdn_w_in, v_gdn_conv, v_gdn_a_log, v_gdn_dt_bias, v_gdn_out_norm, v_gdn_w_out, v_dsw_w_in, v_dsw_q_norm, v_dsw_k_norm, v_dsw_w_out, v_rel_bias):
    S = x.shape[1]
    nt = S // WT
    xi, yi, ci = lax.axis_index("x"), lax.axis_index("y"), lax.axis_index("c")
    me = 4 * xi + 2 * yi + ci
    s_me = 2 * xi + yi
    x0, tgt = x[0], loss_target[0]
    shard = dict(w_ffn_in=w_ffn_in, w_ffn_out=w_ffn_out, gdn_w_in=gdn_w_in, gdn_w_out=gdn_w_out, dsw_w_in=dsw_w_in, dsw_w_out=dsw_w_out)

    whole = lambda a: Row(a, a.shape, lambda i: (0,) * a.ndim)
    (cond8,) = rowwise(lambda ids, v: (_silu(v),), [whole(c.reshape(8, LANES))], [], [Out((8, LANES), F32, (8, LANES), lambda i: (0, 0))], (1,), "cond")
    cond_all = all_gather_small(cond8, "gather_cond").reshape(N_DEV, D)
    cond16 = jnp.concatenate([cond_all, jnp.zeros((8, D), F32)], axis=0)
    ada_cols = w_ada.shape[2]
    mods = [matmul(cond16, w_ada[l], "nn", F32, f"ada_{l}")[:N_DEV] for l in range(2)]
    buf = jnp.concatenate([jnp.stack(mods, axis=1).reshape(-1, LANES), gdn_conv.reshape(-1, LANES)], axis=0)
    n_mod_rows = N_DEV * 2 * ada_cols // LANES
    got = all_gather_small(buf, "gather_mod").reshape(N_DEV, buf.shape[0], LANES)
    mod_parts, conv_parts = [], []
    for s in range(N_SHARD):
        from_dev = got[2 * s]
        mod_parts.append(lax.dynamic_index_in_dim(from_dev[:n_mod_rows].reshape(N_DEV, 2, ada_cols), me, 0, keepdims=False))
        conv_parts.append(from_dev[n_mod_rows:].reshape(4, -1))
    mod_nb = jnp.concatenate(mod_parts, axis=1)
    conv_w = jnp.concatenate(conv_parts, axis=1)
    (mod,) = rowwise(lambda ids, a, b: (a + b,), [whole(mod_nb), whole(b_ada)], [], [Out(mod_nb.shape, F32, mod_nb.shape, lambda i: (0, 0))], (1,), "mod_bias")
    mod = mod.reshape(2, 6, 1, D)
    sh1, sc1, g1, sh2, sc2, g2 = ([mod[l, k] for l in range(2)] for k in range(6))
    gmix = [norm_mix[l][None] for l in range(2)]
    gffn = [norm_ffn[l][None] for l in range(2)]

    gcols = gdn_w_in.shape[2]
    g_gdn_in, g_gdn_out = all_gather_shards([gdn_w_in[0].astype(BF16), gdn_w_out[0].astype(BF16)])
    gathered = lambda ws: [(N_SHARD,) + w.shape for w in ws]
    gate = (jnp.minimum(jnp.abs(g_gdn_in[0, 0, 0].astype(F32)), 0.0) + jnp.minimum(jnp.abs(mod[0, 0, 0, 0]), 0.0)).astype(BF16)
    w2 = [w_ffn_in[0].astype(BF16) + gate, w_ffn_out[0].astype(BF16) + gate]
    w3 = [dsw_w_in[0].astype(BF16) + gate, dsw_w_out[0].astype(BF16) + gate, w_ffn_in[1].astype(BF16) + gate, w_ffn_out[1].astype(BF16) + gate]
    fly2 = copies_start(w2, gathered(w2), _gather_copies, 4, "weights_ffn0_start")
    fly3 = copies_start(w3, gathered(w3), _gather_copies, 4, "weights_layer1_start")
    started = fly2[4][0, 0] + fly3[4][0, 0]
    w_gdn = jnp.concatenate([g_gdn_in[s] for s in range(N_SHARD)] + [jnp.zeros((D, GDN_PROJ - N_SHARD * gcols), BF16)], axis=1)
    alog, dtb = _pad_lanes(gdn_a_log[0]), _pad_lanes(gdn_dt_bias[0])
    qg2 = jnp.concatenate([dsw_q_norm, dsw_q_norm], axis=1)
    kg2 = jnp.concatenate([dsw_k_norm, dsw_k_norm], axis=1)
    w_gdn_out = g_gdn_out.reshape(GDN_H * LANES, D)
    gdn_args = (w_gdn, conv_w, alog, dtb, gdn_out_norm, w_gdn_out)
    sc1[0] = sc1[0] + started

    (h10,) = rowwise(f_norm_only, [_wide(x0)], [gmix[0], sc1[0], sh1[0]], [_wide_out(S, BF16)], (nt,), "l0_norm")
    y0, sv_g = gdn_forward(h10, *gdn_args)
    x1, h20 = rowwise(f_resid_norm, [_wide(x0), _wide(y0)], [g1[0], gffn[0], sc2[0], sh2[0]], [_wide_out(S, F32), _wide_out(S, BF16)], (nt,), "l0_mid")
    g_in0, g_out0 = copies_wait(*fly2[:4], _gather_copies, y0, "weights_ffn0_wait")
    w_ffn = [(g_in0, g_out0.reshape(FFN, D)), None]
    f0, sv_f0 = ffn_forward(h20, *w_ffn[0], "0")
    x2, h11 = rowwise(f_resid_norm, [_wide(x1), _wide(f0)], [g2[0], gmix[1], sc1[1], sh1[1]], [_wide_out(S, F32), _wide_out(S, BF16)], (nt,), "l1_in")
    g_dsw_in, g_dsw_out, g_in1, g_out1 = copies_wait(*fly3[:4], _gather_copies, f0, "weights_layer1_wait")
    w_ffn[1] = (g_in1, g_out1.reshape(FFN, D))
    dsw_args = (g_dsw_in, qg2, kg2)
    y1, sv_d = dsw_forward(h11, *dsw_args, rel_bias, g_dsw_out)
    x3, h21 = rowwise(f_resid_norm, [_wide(x2), _wide(y1)], [g1[1], gffn[1], sc2[1], sh2[1]], [_wide_out(S, F32), _wide_out(S, BF16)], (nt,), "l1_mid")
    f1, sv_f1 = ffn_forward(h21, *w_ffn[1], "1")
    part_spec = lambda a: Row(a, (None, 1, D), lambda i: (i, 0, 0))
    (parts,) = rowwise(f_loss, [_wide(x3), _wide(f1), _wide(tgt)], [g2[1]], [Out((nt, 1, D), F32, (None, 1, D), lambda i: (i, 0, 0))], (nt,), "loss")
    loss = lax.psum(jnp.sum(parts), ("x", "y", "c"))

    (dx3, df1), (dg2_1,) = rowwise_bwd(f_loss, [_wide(x3), _wide(f1, gdtype=BF16), _wide(tgt, diff=False)], [g2[1]],
                                       [part_spec(jnp.ones((nt, 1, D), F32))], (nt,), "loss_bwd")
    dh21, d_win1, d_wout1 = ffn_backward(df1, sv_f1, *w_ffn[1], "1")
    (dx2, dy1), (dg1_1, dgf1, dsc2_1, dsh2_1) = rowwise_bwd(
        f_resid_norm, [_wide(x2), _wide(y1, gdtype=BF16)], [g1[1], gffn[1], sc2[1], sh2[1]], [_wide(dx3), _wide(dh21)], (nt,), "l1_mid_bwd")
    dh11, g_d = dsw_backward(dy1, sv_d, *dsw_args, g_dsw_out)
    (dx1, df0), (dg2_0, dgm1, dsc1_1, dsh1_1) = rowwise_bwd(
        f_resid_norm, [_wide(x1), _wide(f0, gdtype=BF16)], [g2[0], gmix[1], sc1[1], sh1[1]], [_wide(dx2), _wide(dh11)], (nt,), "l1_in_bwd")
    by_shard = lambda a: a.reshape(N_SHARD, a.shape[0] // N_SHARD, a.shape[1])
    landing = lambda ps: [(3,) + p.shape[1:] for p in ps]
    dws3 = [g_d["w_in"], g_d["w_out"], d_win1, by_shard(d_wout1)]
    parts3 = [a.astype(BF16) for a in dws3]
    gfly3 = copies_start(parts3, landing(parts3), _scatter_copies, 3, "grads_layer1_start")
    w_out0 = w_ffn[0][1] + gfly3[4][0, 0].astype(BF16)
    dh20, d_win0, d_wout0 = ffn_backward(df0, sv_f0, w_ffn[0][0], w_out0, "0")
    (dx0p, dy0), (dg1_0, dgf0, dsc2_0, dsh2_0) = rowwise_bwd(
        f_resid_norm, [_wide(x0), _wide(y0, gdtype=BF16)], [g1[0], gffn[0], sc2[0], sh2[0]], [_wide(dx1), _wide(dh20)], (nt,), "l0_mid_bwd")
    dws2 = [d_win0, by_shard(d_wout0)]
    parts2 = [a.astype(BF16) for a in dws2]
    gfly2 = copies_start(parts2, landing(parts2), _scatter_copies, 3, "grads_ffn0_start")
    gdn_args = gdn_args[:5] + (w_gdn_out + gfly2[4][0, 0].astype(BF16),)
    dh10, g_g = gdn_backward(dy0, sv_g, *gdn_args)
    (grad_x,), (dgm0, dsc1_0, dsh1_0) = rowwise_bwd(f_first, [_wide(x0)], [gmix[0], sc1[0], sh1[0]], [_wide(dx0p), _wide(dh10)], (nt,), "l0_norm_bwd")

    dmod = jnp.concatenate([dsh1_0, dsc1_0, dg1_0, dsh2_0, dsc2_0, dg2_0, dsh1_1, dsc1_1, dg1_1, dsh2_1, dsc2_1, dg2_1], axis=1)
    d_rel = jnp.transpose(g_d["rel"][:, :, 0])
    fold = lambda v: v[:, :DSW_DH] + v[:, DSW_DH:]
    small = [dmod, jnp.concatenate([dgm0, dgm1], axis=1), jnp.concatenate([dgf0, dgf1], axis=1), g_g["conv"].reshape(1, -1),
             g_g["alog"], g_g["dtb"], g_g["gain"], _pad_lanes(fold(g_d["q_gain2"])[0]), _pad_lanes(fold(g_d["k_gain2"])[0]),
             d_rel.reshape(1, -1)]
    used = [v.shape[1] // LANES for v in small]
    sizes = [-(-u // 8) * 8 for u in used]
    pad8 = lambda v, u, s: jnp.concatenate([v.reshape(u, LANES), jnp.zeros((s - u, LANES), F32)], axis=0) if s > u else v.reshape(u, LANES)
    pad_rows = sum(sizes)
    sbuf = jnp.concatenate([pad8(v, u, s) for v, u, s in zip(small, used, sizes)], axis=0)
    sgot = all_gather_small(sbuf, "gather_small_grads")
    ssum = add_rows([sgot[d * pad_rows:(d + 1) * pad_rows] for d in range(N_DEV)], F32, "sum_small_grads", rt=pad_rows)
    offs = np.cumsum([0] + sizes)
    take = lambda k: ssum[offs[k]:offs[k] + used[k]].reshape(1, -1)
    grad_b_ada = take(0).reshape(2, 6 * D)
    grad_norm_mix = take(1).reshape(2, D)
    grad_norm_ffn = take(2).reshape(2, D)
    conv_full = take(3).reshape(4, -1)
    ncv = gdn_conv.shape[2]
    grad_gdn_conv = lax.dynamic_slice_in_dim(conv_full, s_me * ncv, ncv, axis=1)[None]
    grad_a_log = take(4)[:, :GDN_H]
    grad_dt_bias = take(5)[:, :GDN_H]
    grad_out_norm = take(6)
    grad_q_norm = take(7)[:, :DSW_DH]
    grad_k_norm = take(8)[:, :DSW_DH]
    grad_rel = take(9).reshape(REL_BUCKETS, 3 * GDN_H)
    dmod_all = sgot.reshape(N_DEV, pad_rows, LANES)[:, :used[0]].reshape(N_DEV, 2, 6 * D)
    dmod_mine = lax.dynamic_slice_in_dim(dmod_all, s_me * ada_cols, ada_cols, axis=2)
    dmod16 = jnp.concatenate([dmod_mine, jnp.zeros_like(dmod_mine)], axis=0)
    grad_w_ada = jnp.stack([matmul(cond16, dmod16[:, l], "tn", F32, f"ada_dw_{l}") for l in range(2)])

    dg_in = jnp.stack([g_g["w_in"][:, s * gcols:(s + 1) * gcols] for s in range(N_SHARD)])
    dws = [dg_in, by_shard(g_g["w_out"])]
    keeps, gives = [], []
    for a in dws:
        rh = a.shape[1] // 2
        keeps.append(lax.dynamic_slice_in_dim(a, ci * rh, rh, axis=1))
        gives.append(lax.dynamic_slice_in_dim(a, (1 - ci) * rh, rh, axis=1).astype(BF16))
    from_sib = sibling_exchange(gives, "grads_to_sibling")
    flat2 = lambda a: a.reshape(-1, a.shape[-1])
    parts = [add_rows([flat2(k_), flat2(f_)], BF16, f"grads_chip_sum_{i}").reshape(k_.shape) for i, (k_, f_) in enumerate(zip(keeps, from_sib))]
    others = scatter_to_chips(parts)
    halves = []
    for i, (p_, o_) in enumerate(zip(parts, others)):
        own = lax.dynamic_index_in_dim(p_, s_me, 0, keepdims=False)
        halves.append(add_rows([own, o_[0], o_[1], o_[2]], F32, f"grads_sum_{i}"))
    sib_halves = sibling_exchange(halves, "grads_from_sibling")

    def whole_shard(mine, theirs):
        both = jnp.stack([mine, theirs])
        return jnp.concatenate([lax.dynamic_index_in_dim(both, ci, 0, keepdims=False),
                                lax.dynamic_index_in_dim(both, 1 - ci, 0, keepdims=False)], axis=0)

    s_gdn_in, s_gdn_out = [whole_shard(a, b) for a, b in zip(halves, sib_halves)]
    got3 = copies_wait(*gfly3[:4], _scatter_copies, grad_x, "grads_layer1_wait")
    got2 = copies_wait(*gfly2[:4], _scatter_copies, grad_x, "grads_ffn0_wait")
    core_sums = []
    for i, (full, got) in enumerate(zip(dws3 + dws2, got3 + got2)):
        own = lax.dynamic_index_in_dim(full, s_me, 0, keepdims=False)
        core_sums.append(add_rows([own, got[0], got[1], got[2]], F32, f"grads_core_sum_{i}"))
    sib_sums = sibling_exchange(core_sums, "grads_core_sums_swap")
    s_dsw_in, s_dsw_out, s_in1, s_out1, s_in0, s_out0 = [add_rows([a, b], F32, f"grads_chip_total_{i}")
                                                          for i, (a, b) in enumerate(zip(core_sums, sib_sums))]
    gsh = dict(gdn_w_in=s_gdn_in[None], gdn_w_out=s_gdn_out[None],
               w_ffn_in=jnp.stack([s_in0, s_in1]), w_ffn_out=jnp.stack([s_out0, s_out1]),
               dsw_w_in=s_dsw_in[None], dsw_w_out=s_dsw_out[None])

    grads = dict(w_ada=grad_w_ada, b_ada=grad_b_ada, norm_mix=grad_norm_mix, norm_ffn=grad_norm_ffn, w_ffn_in=gsh["w_ffn_in"],
                 w_ffn_out=gsh["w_ffn_out"], gdn_w_in=gsh["gdn_w_in"], gdn_conv=grad_gdn_conv, gdn_a_log=grad_a_log,
                 gdn_dt_bias=grad_dt_bias, gdn_out_norm=grad_out_norm, gdn_w_out=gsh["gdn_w_out"], dsw_w_in=gsh["dsw_w_in"],
                 dsw_q_norm=grad_q_norm, dsw_k_norm=grad_k_norm, dsw_w_out=gsh["dsw_w_out"], rel_bias=grad_rel)
    weights = dict(w_ada=w_ada, b_ada=b_ada, norm_mix=norm_mix, norm_ffn=norm_ffn, w_ffn_in=w_ffn_in, w_ffn_out=w_ffn_out,
                   gdn_w_in=gdn_w_in, gdn_conv=gdn_conv, gdn_a_log=gdn_a_log, gdn_dt_bias=gdn_dt_bias, gdn_out_norm=gdn_out_norm,
                   gdn_w_out=gdn_w_out, dsw_w_in=dsw_w_in, dsw_q_norm=dsw_q_norm, dsw_k_norm=dsw_k_norm, dsw_w_out=dsw_w_out,
                   rel_bias=rel_bias)
    ms = dict(w_ada=m_w_ada, b_ada=m_b_ada, norm_mix=m_norm_mix, norm_ffn=m_norm_ffn, w_ffn_in=m_w_ffn_in, w_ffn_out=m_w_ffn_out,
              gdn_w_in=m_gdn_w_in, gdn_conv=m_gdn_conv, gdn_a_log=m_gdn_a_log, gdn_dt_bias=m_gdn_dt_bias, gdn_out_norm=m_gdn_out_norm,
              gdn_w_out=m_gdn_w_out, dsw_w_in=m_dsw_w_in, dsw_q_norm=m_dsw_q_norm, dsw_k_norm=m_dsw_k_norm, dsw_w_out=m_dsw_w_out,
              rel_bias=m_rel_bias)
    vs = dict(w_ada=v_w_ada, b_ada=v_b_ada, norm_mix=v_norm_mix, norm_ffn=v_norm_ffn, w_ffn_in=v_w_ffn_in, w_ffn_out=v_w_ffn_out,
              gdn_w_in=v_gdn_w_in, gdn_conv=v_gdn_conv, gdn_a_log=v_gdn_a_log, gdn_dt_bias=v_gdn_dt_bias, gdn_out_norm=v_gdn_out_norm,
              gdn_w_out=v_gdn_w_out, dsw_w_in=v_dsw_w_in, dsw_q_norm=v_dsw_q_norm, dsw_k_norm=v_dsw_k_norm, dsw_w_out=v_dsw_w_out,
              rel_bias=v_rel_bias)
    names = list(weights)
    deltas, new_m, new_v = [], [], []
    for n in names:
        g = grads[n].reshape(weights[n].shape)
        grads[n] = g
        d, nm, nv = adamw(weights[n], g, ms[n], vs[n], f"adamw_{n}")
        deltas.append(d)
        new_m.append(nm)
        new_v.append(nv)
    return (loss, grad_x[None], *[grads[n] for n in names], *deltas, *new_m, *new_v)
```

```python
import functools
import math

import numpy as np
import jax
import jax.numpy as jnp
from jax import lax
from jax.experimental import pallas as pl
from jax.experimental.pallas import tpu as pltpu

F32 = jnp.float32
BF16 = jnp.bfloat16
SDS = jax.ShapeDtypeStruct
MESH = pl.DeviceIdType.MESH
ANY = pl.BlockSpec(memory_space=pl.ANY)

D = 1024
EPS = 1e-6
LANES = 128
GDN_H = 8
GDN_DK = 128
GDN_C = 64
DSW_GROUPS = ((128, 1), (512, 4), (2048, 16))
DSW_SPAN = 128
DSW_DH = 64
DSW_HG = 512
REL_BUCKETS = 32
REL_MAX_DIST = 2048
FFN = 2816
N_SHARD = 4
N_DEV = 8
VMEM_LIMIT = 48 * 1024 * 1024
NEG = -1e30

ADAM_LR, ADAM_B1, ADAM_B2, ADAM_EPS, ADAM_WD, ADAM_STEP = 0.001, 0.9, 0.999, 1e-08, 0.01, 10


def _cp(n_axes):
    return pltpu.CompilerParams(dimension_semantics=("arbitrary",) * n_axes, vmem_limit_bytes=VMEM_LIMIT)


def _blk(dim, cap):
    if dim <= cap:
        return dim
    best = None
    for b in range(LANES, cap + 1, LANES):
        if dim % b == 0:
            best = b
    assert best is not None, (dim, cap)
    return best


MAX_SHARD_BLOCK = 1408
def matmul(a, b, mode, out_dtype, name, cap_m=MAX_SHARD_BLOCK, cap_n=MAX_SHARD_BLOCK, cap_k=2048, col_shards=0):
    ns = col_shards
    if mode == "nn":
        (M, K) = a.shape
        K2, N = (b.shape[1], ns * b.shape[2]) if ns else b.shape
    elif mode == "nt":
        (M, K) = a.shape
        N, K2 = (b.shape[1], ns * b.shape[2]) if ns else b.shape
    else:
        (K, M), (K2, N) = a.shape, b.shape
    assert K == K2, (a.shape, b.shape, mode)
    if K <= 3072:
        cap_k = K
        if K > 2048:
            cap_n = 1024
    n_unit = N // ns if (ns and mode != "nt") else N
    k_unit = K // ns if (ns and mode == "nt") else K
    bm = _blk(M, cap_m)
    bn = _blk(n_unit, MAX_SHARD_BLOCK) if n_unit != N else _blk(N, cap_n)
    if k_unit != K:
        bk = _blk(k_unit, MAX_SHARD_BLOCK)
    else:
        bk = _blk(K, 1024 if (ns and mode == "tn") else cap_k)
    nk = K // bk
    nps, kps = n_unit // bn, k_unit // bk
    dims = {"nn": ((1,), (0,)), "nt": ((1,), (1,)), "tn": ((0,), (0,))}[mode]

    def dot(a_ref, b_ref):
        return lax.dot_general(a_ref[...].astype(BF16), b_ref[...].astype(BF16), (dims, ((), ())), preferred_element_type=F32)

    def body_one(a_ref, b_ref, o_ref):
        o_ref[...] = dot(a_ref, b_ref).astype(o_ref.dtype)

    def body_acc(a_ref, b_ref, o_ref, acc_ref):
        k = pl.program_id(2)

        @pl.when(k == 0)
        def _():
            acc_ref[...] = jnp.zeros_like(acc_ref)

        acc_ref[...] += dot(a_ref, b_ref)

        @pl.when(k == nk - 1)
        def _():
            o_ref[...] = acc_ref[...].astype(o_ref.dtype)

    a_spec = pl.BlockSpec((bk, bm), lambda i, j, k: (k, i)) if mode == "tn" else pl.BlockSpec((bm, bk), lambda i, j, k: (i, k))
    if mode == "nt":
        b_spec = pl.BlockSpec((None, bn, bk), lambda i, j, k: (k // kps, j, k % kps)) if ns else pl.BlockSpec((bn, bk), lambda i, j, k: (j, k))
    elif mode == "nn" and ns:
        b_spec = pl.BlockSpec((None, bk, bn), lambda i, j, k: (j // nps, k, j % nps))
    else:
        b_spec = pl.BlockSpec((bk, bn), lambda i, j, k: (k, j))
    if mode == "tn" and ns:
        o_spec, o_shape = pl.BlockSpec((None, bm, bn), lambda i, j, k: (j // nps, i, j % nps)), (ns, M, n_unit)
    else:
        o_spec, o_shape = pl.BlockSpec((bm, bn), lambda i, j, k: (i, j)), (M, N)
    return pl.pallas_call(
        body_one if nk == 1 else body_acc, name=name, grid=(M // bm, N // bn, nk),
        in_specs=[a_spec, b_spec], out_specs=o_spec,
        out_shape=SDS(o_shape, out_dtype), scratch_shapes=[] if nk == 1 else [pltpu.VMEM((bm, bn), F32)],
        compiler_params=_cp(3),
    )(a, b)


class Row:
    def __init__(self, arr, bshape, imap, splits=None, diff=True, acc=False, gdtype=F32, gshape=None, gbshape=None, gimap=None,
                 lead=0):
        self.arr, self.bshape, self.imap = arr, tuple(bshape), imap
        self.splits, self.lead = splits, lead
        self.diff, self.acc, self.gdtype = diff, acc, gdtype
        self.gshape = tuple(arr.shape) if gshape is None else tuple(gshape)
        self.gbshape = self.bshape if gbshape is None else tuple(gbshape)
        self.gimap = imap if gimap is None else gimap

    def gspec(self):
        return pl.BlockSpec(self.gbshape, self.gimap)

    def spec(self):
        return pl.BlockSpec(self.bshape, self.imap)

    def pieces(self, ref):
        return _load_pieces(ref, self.splits, self.lead)

    def n_pieces(self):
        return _n_pieces(self.splits, self.lead)


class Out:
    def __init__(self, shape, dtype, bshape, imap, splits=None, lead=0):
        self.shape, self.dtype, self.bshape, self.imap = tuple(shape), dtype, tuple(bshape), imap
        self.splits, self.lead = splits, lead

    def n_pieces(self):
        return _n_pieces(self.splits, self.lead)


def _n_pieces(splits, lead):
    return lead if lead else (1 if splits is None else len(splits))


def _load_pieces(ref, splits, lead):
    if lead:
        return [ref[k].astype(F32) for k in range(lead)]
    if splits is None:
        return [ref[...].astype(F32)]
    out, o = [], 0
    for w in splits:
        out.append(ref[..., o:o + w].astype(F32))
        o += w
    return out


def _store_pieces(ref, splits, lead, vals, accumulate=False):
    def put(idx, v):
        if accumulate:
            ref[idx] += v.astype(ref.dtype)
        else:
            ref[idx] = v.astype(ref.dtype)

    if lead:
        for k in range(lead):
            put(k, vals[k])
    elif splits is None:
        put(..., vals[0])
    else:
        o = 0
        for w, v in zip(splits, vals):
            put((..., slice(o, o + w)), v)
            o += w


def rowwise(fn, rows, params, outs, grid, name):
    nr, npar = len(rows), len(params)

    def body(*refs):
        ids = tuple(pl.program_id(a) for a in range(len(grid)))
        vals = []
        for r, ref in zip(rows, refs[:nr]):
            vals += r.pieces(ref)
        pvals = [ref[...].astype(F32) for ref in refs[nr:nr + npar]]
        res = list(fn(ids, *vals, *pvals))
        o = 0
        for spec, ref in zip(outs, refs[nr + npar:]):
            n = spec.n_pieces()
            _store_pieces(ref, spec.splits, spec.lead, res[o:o + n])
            o += n

    nz = len(grid)
    pspecs = [pl.BlockSpec(p.shape, (lambda *ids, _n=p.ndim: (0,) * _n)) for p in params]
    res = pl.pallas_call(
        body, name=name, grid=grid,
        in_specs=[r.spec() for r in rows] + pspecs,
        out_specs=[pl.BlockSpec(o.bshape, o.imap) for o in outs],
        out_shape=[SDS(o.shape, o.dtype) for o in outs],
        compiler_params=_cp(nz),
    )(*[r.arr for r in rows], *params)
    return list(res)


def rowwise_bwd(fn, rows, params, cots, grid, name):
    nr, npar, nc = len(rows), len(params), len(cots)
    drows = [r for r in rows if r.diff]
    nz = len(grid)

    def body(*refs):
        ids = tuple(pl.program_id(a) for a in range(nz))
        row_refs, par_refs = refs[:nr], refs[nr:nr + npar]
        cot_refs = refs[nr + npar:nr + npar + nc]
        drow_refs = refs[nr + npar + nc:nr + npar + nc + len(drows)]
        dpar_refs = refs[nr + npar + nc + len(drows):]
        pieces, is_diff = [], []
        for r, ref in zip(rows, row_refs):
            ps = r.pieces(ref)
            pieces += ps
            is_diff += [r.diff] * len(ps)
        pvals = [ref[...].astype(F32) for ref in par_refs]
        dvals = [p for p, dflag in zip(pieces, is_diff) if dflag]
        nd = len(dvals)

        def f(*args):
            it = iter(args[:nd])
            full = [next(it) if dflag else p for p, dflag in zip(pieces, is_diff)]
            return tuple(fn(ids, *full, *args[nd:]))

        _, vjp = jax.vjp(f, *dvals, *pvals)
        cvals = []
        for c, ref in zip(cots, cot_refs):
            cvals += c.pieces(ref)
        g = vjp(tuple(cvals))
        o = 0
        first_inner = ids[-1] == 0
        for r, ref in zip(drows, drow_refs):
            n = r.n_pieces()
            gs = g[o:o + n]
            o += n
            if r.acc:
                @pl.when(first_inner)
                def _(ref=ref):
                    ref[...] = jnp.zeros_like(ref)
            _store_pieces(ref, r.splits, r.lead, gs, accumulate=r.acc)
        first = functools.reduce(jnp.logical_and, [i == 0 for i in ids])
        for ref, gp in zip(dpar_refs, g[nd:]):
            @pl.when(first)
            def _(ref=ref):
                ref[...] = jnp.zeros_like(ref)
            ref[...] += gp

    pspecs = [pl.BlockSpec(p.shape, (lambda *ids, _n=p.ndim: (0,) * _n)) for p in params]
    res = pl.pallas_call(
        body, name=name, grid=grid,
        in_specs=[r.spec() for r in rows] + pspecs + [c.spec() for c in cots],
        out_specs=[r.gspec() for r in drows] + pspecs,
        out_shape=[SDS(r.gshape, r.gdtype) for r in drows] + [SDS(p.shape, F32) for p in params],
        compiler_params=_cp(nz),
    )(*[r.arr for r in rows], *params, *[c.arr for c in cots])
    res = list(res)
    return res[:len(drows)], res[len(drows):]


def _sigmoid(x):
    return 0.5 * (jnp.tanh(0.5 * x) + 1.0)


def _silu(x):
    return x * _sigmoid(x)


def _normmod(x, gain, sc, sh):
    inv = lax.rsqrt(jnp.mean(x * x, axis=-1, keepdims=True) + EPS)
    return x * inv * gain * (1.0 + sc) + sh


def f_first(ids, x, gain, sc, sh):
    return x, _normmod(x, gain, sc, sh)


def f_resid_norm(ids, x, y, g, gain, sc, sh):
    xn = x + g * y
    return xn, _normmod(xn, gain, sc, sh)


@jax.custom_vjp
def _swiglu(gate, up):
    return _silu(gate) * up


def _swiglu_fwd(gate, up):
    return _silu(gate) * up, (gate, up)


def _swiglu_bwd(res, da):
    gate, up = res
    s = _sigmoid(gate)
    gs = gate * s
    return da * up * (s + gs * (1.0 - s)), da * gs


_swiglu.defvjp(_swiglu_fwd, _swiglu_bwd)


def f_swiglu(ids, gate, up):
    return (_swiglu(gate, up),)


def f_loss(ids, x, y, tgt, g):
    out = x + g * y
    e = out - tgt
    part = 0.5 * jnp.sum(e * e, axis=0, keepdims=True) * (1.0 / D)
    return (part,)


def _softplus(x):
    return jnp.maximum(x, 0.0) + jnp.log(1.0 + jnp.exp(-jnp.abs(x)))


def _chunk_tril(T):
    r = lax.broadcasted_iota(jnp.int32, (T, T), 0)
    c = lax.broadcasted_iota(jnp.int32, (T, T), 1)
    return jnp.where((r // GDN_C == c // GDN_C) & (c <= r), 1.0, 0.0).astype(F32)


def _dot_hi(a, b, dims=((1,), (0,))):
    return lax.dot_general(a, b, (dims, ((), ())), precision=lax.Precision.HIGHEST, preferred_element_type=F32)


def _dot_x3(a, b, dims=((1,), (0,))):
    return lax.dot_general(a, b, (dims, ((), ())), precision=lax.Precision.HIGH, preferred_element_type=F32)


def f_gdn_gates(ids, ab, alog, dtb):
    T = ab.shape[0]
    g = -jnp.exp(alog) * _softplus(ab + dtb)
    beta = _sigmoid(ab)
    gcum = _dot_x3(_chunk_tril(T), g)
    row = lax.broadcasted_iota(jnp.int32, (LANES, LANES), 0)
    sel = lambda k: jnp.where(row == k, 1.0, 0.0).astype(F32)
    gcs = [_dot_x3(gcum, sel(h)) for h in range(GDN_H)]
    bts = [_dot_x3(beta, sel(GDN_H + h)) for h in range(GDN_H)]
    return (*gcs, *bts)


def f_gdn_post(ids, *args):
    os_, zs, gain = args[:GDN_H], args[GDN_H:2 * GDN_H], args[2 * GDN_H]
    out = []
    for o, z in zip(os_, zs):
        inv = lax.rsqrt(jnp.mean(o * o, axis=-1, keepdims=True) + EPS)
        out.append(o * inv * gain * _silu(z))
    return tuple(out)


def _qknorm1(x, gain2, scale):
    lane = lax.broadcasted_iota(jnp.int32, x.shape, 1)
    lo = lane < DSW_DH
    x2 = x * x
    s_all = jnp.sum(x2, axis=-1, keepdims=True)
    s_lo = jnp.sum(jnp.where(lo, x2, 0.0), axis=-1, keepdims=True)
    ms = jnp.where(lo, s_lo, s_all - s_lo) * (1.0 / DSW_DH)
    return x * lax.rsqrt(ms + EPS) * (gain2 * scale)


def f_qknorm(ids, *args):
    return tuple(_qknorm1(x, args[-1], 1.0) for x in args[:-1])


def f_qnorm(ids, *args):
    return tuple(_qknorm1(x, args[-1], DSW_DH ** -0.5) for x in args[:-1])


def f_combine(ids, o0, o1, o2, l0, l1, l2):
    m = jnp.maximum(jnp.maximum(l0, l1), l2)
    e0, e1, e2 = jnp.exp(l0 - m), jnp.exp(l1 - m), jnp.exp(l2 - m)
    den = e0 + e1 + e2
    o = (e0 * o0 + e1 * o1 + e2 * o2) / den
    return o, m + jnp.log(den)


GDN_T = 512
HALO = 16


def _conv_pre(xx, w):
    acc = xx * w[3:4, :]
    for j in range(3):
        acc = acc + pltpu.roll(xx, shift=3 - j, axis=0) * w[j:j + 1, :]
    return acc


@jax.custom_vjp
def _qkv_act_core(pre, norm_on, scale):
    s = _silu(pre)
    r = lax.rsqrt(jnp.sum(s * s, axis=-1, keepdims=True) + EPS)
    return jnp.where(norm_on > 0.5, s * r * scale, s)


def _qkv_act_fwd(pre, norm_on, scale):
    return _qkv_act_core(pre, norm_on, scale), (pre, norm_on, scale)


def _qkv_act_bwd(res, dout):
    pre, norm_on, scale = res
    sig = _sigmoid(pre)
    s = pre * sig
    r = lax.rsqrt(jnp.sum(s * s, axis=-1, keepdims=True) + EPS)
    unit = s * r
    dn = dout * scale
    ds = jnp.where(norm_on > 0.5, r * (dn - unit * jnp.sum(dn * unit, axis=-1, keepdims=True)), dout)
    return ds * (sig + s * (1.0 - sig)), jnp.zeros_like(norm_on), jnp.zeros_like(scale)


_qkv_act_core.defvjp(_qkv_act_fwd, _qkv_act_bwd)


def _qkv_act(pre, cidx):
    norm_on = jnp.where(cidx < 2 * GDN_H, 1.0, 0.0).astype(F32)
    scale = jnp.where(cidx < GDN_H, GDN_DK ** -0.5, 1.0).astype(F32)
    return _qkv_act_core(pre, norm_on, scale)


def gdn_pre(proj, conv_w, S):
    nt = S // GDN_T
    hb = GDN_T // HALO

    def body(prev_ref, cur_ref, w_ref, o_ref):
        p, i = pl.program_id(0), pl.program_id(1)
        for h in range(GDN_H):
            cols = slice(LANES * h, LANES * (h + 1))
            prev = jnp.where(i > 0, prev_ref[:, cols].astype(F32), 0.0)
            xx = jnp.concatenate([prev, cur_ref[:, cols].astype(F32)], axis=0)
            pre = _conv_pre(xx, w_ref[:, cols])[HALO:]
            o_ref[h] = _qkv_act(pre, p * GDN_H + h)

    hv = GDN_H * LANES
    return pl.pallas_call(
        body, name="gdn_pre", grid=(3, nt),
        in_specs=[pl.BlockSpec((HALO, hv), lambda p, i: (jnp.maximum(i * hb - 1, 0), p)),
                  pl.BlockSpec((GDN_T, hv), lambda p, i: (i, p)),
                  pl.BlockSpec((4, hv), lambda p, i: (0, p))],
        out_specs=pl.BlockSpec((None, GDN_H, GDN_T, LANES), lambda p, i: (p, 0, i, 0)),
        out_shape=SDS((3, GDN_H, S, LANES), F32),
        compiler_params=_cp(2),
    )(proj, proj, conv_w)


def gdn_pre_bwd(proj, conv_w, dqkv, S):
    nt = S // GDN_T
    hb = GDN_T // HALO
    last_h = S // HALO - 1

    def body(prev_ref, cur_ref, next_ref, w_ref, d_ref, dnext_ref, dx_ref, dw_ref):
        p, i = pl.program_id(0), pl.program_id(1)

        @pl.when(i == 0)
        def _():
            dw_ref[...] = jnp.zeros_like(dw_ref)

        for h in range(GDN_H):
            cols = slice(LANES * h, LANES * (h + 1))
            w = w_ref[:, cols]
            prev = jnp.where(i > 0, prev_ref[:, cols].astype(F32), 0.0)
            xx = jnp.concatenate([prev, cur_ref[:, cols].astype(F32), next_ref[:, cols].astype(F32)], axis=0)
            dnext = jnp.where(i < nt - 1, dnext_ref[h], 0.0)
            dd = jnp.concatenate([jnp.zeros((HALO, LANES), F32), d_ref[h], dnext], axis=0)
            pre = _conv_pre(xx, w)
            _, vjp = jax.vjp(lambda v, _c=p * GDN_H + h: _qkv_act(v, _c), pre)
            (dpre,) = vjp(dd)
            dx = dpre * w[3:4, :]
            R = dpre.shape[0]
            for j in range(3):
                dx = dx + pltpu.roll(dpre, shift=R - (3 - j), axis=0) * w[j:j + 1, :]
            dx_ref[:, cols] = dx[HALO:HALO + GDN_T].astype(dx_ref.dtype)
            own = HALO + GDN_T
            rows_w = [jnp.sum((dpre * pltpu.roll(xx, shift=3 - j, axis=0))[:own], axis=0, keepdims=True) for j in range(3)]
            rows_w.append(jnp.sum((dpre * xx)[:own], axis=0, keepdims=True))
            r4 = lax.broadcasted_iota(jnp.int32, (4, LANES), 0)
            dw = jnp.zeros((4, LANES), F32)
            for j in range(4):
                dw = dw + jnp.where(r4 == j, rows_w[j], 0.0)
            dw_ref[:, cols] += dw

    hv = GDN_H * LANES
    return pl.pallas_call(
        body, name="gdn_pre_bwd", grid=(3, nt),
        in_specs=[pl.BlockSpec((HALO, hv), lambda p, i: (jnp.maximum(i * hb - 1, 0), p)),
                  pl.BlockSpec((GDN_T, hv), lambda p, i: (i, p)),
                  pl.BlockSpec((HALO, hv), lambda p, i: (jnp.minimum((i + 1) * hb, last_h), p)),
                  pl.BlockSpec((4, hv), lambda p, i: (0, p)),
                  pl.BlockSpec((None, GDN_H, GDN_T, LANES), lambda p, i: (p, 0, i, 0)),
                  pl.BlockSpec((None, GDN_H, HALO, LANES), lambda p, i: (p, 0, jnp.minimum((i + 1) * hb, last_h), 0))],
        out_specs=[pl.BlockSpec((GDN_T, hv), lambda p, i: (i, p)),
                   pl.BlockSpec((4, hv), lambda p, i: (0, p))],
        out_shape=[SDS((S, 3 * hv), BF16), SDS((4, 3 * hv), F32)],
        compiler_params=_cp(2),
    )(proj, proj, proj, conv_w, dqkv, dqkv)


_DIMS = {"nn": ((1,), (0,)), "nt": ((1,), (1,)), "tn": ((0,), (0,))}


def _mm_raw(a, b, mode, hi):
    if hi:
        return _dot_hi(a, b, _DIMS[mode])
    return lax.dot_general(a.astype(BF16), b.astype(BF16), (_DIMS[mode], ((), ())), preferred_element_type=F32)


@functools.partial(jax.custom_vjp, nondiff_argnums=(2, 3))
def mm(a, b, mode, hi):
    return _mm_raw(a, b, mode, hi)


def _mm_fwd(a, b, mode, hi):
    return _mm_raw(a, b, mode, hi), (a, b)


def _mm_bwd(mode, hi, res, dc):
    a, b = res
    if mode == "nn":
        da, db = mm(dc, b, "nt", hi), mm(a, dc, "tn", hi)
    elif mode == "nt":
        da, db = mm(dc, b, "nn", hi), mm(dc, a, "tn", hi)
    else:
        da, db = mm(b, dc, "nt", hi), mm(a, dc, "nn", hi)
    return da, db


mm.defvjp(_mm_fwd, _mm_bwd)


TRI_BASE = 8


def _unit_lower_inverses(Ls):
    n = Ls[0].shape[0]
    r = lax.broadcasted_iota(jnp.int32, (n, n), 0)
    c = lax.broadcasted_iota(jnp.int32, (n, n), 1)
    eye = jnp.where(r == c, 1.0, 0.0).astype(F32)
    base = r // TRI_BASE == c // TRI_BASE
    Ps = [jnp.where(base, -L, 0.0) for L in Ls]
    invs = [eye + P for P in Ps]
    k = 1
    while 2 * k < TRI_BASE:
        Ps = [_dot_x3(P, P) for P in Ps]
        invs = [inv + _dot_x3(inv, P) for inv, P in zip(invs, Ps)]
        k *= 2
    b = 2 * TRI_BASE
    while b <= n:
        off_mask = (r // b == c // b) & ((r % b) >= b // 2) & ((c % b) < b // 2)
        ts = [_dot_x3(inv, jnp.where(off_mask, L, 0.0)) for inv, L in zip(invs, Ls)]
        invs = [inv - _dot_x3(t, inv) for inv, t in zip(invs, ts)]
        b *= 2
    return invs


@jax.custom_vjp
def tri_apply(invs, Ls, r1s, r2s):
    return [_dot_x3(i, r) for i, r in zip(invs, r1s)], [_dot_x3(i, r) for i, r in zip(invs, r2s)]


def _tri_fwd(invs, Ls, r1s, r2s):
    s1s = [_dot_x3(i, r) for i, r in zip(invs, r1s)]
    s2s = [_dot_x3(i, r) for i, r in zip(invs, r2s)]
    return (s1s, s2s), (invs, s1s, s2s)


def _tri_bwd(res, ds):
    invs, s1s, s2s = res
    d1s = [_dot_x3(i, d, _DIMS["tn"]) for i, d in zip(invs, ds[0])]
    d2s = [_dot_x3(i, d, _DIMS["tn"]) for i, d in zip(invs, ds[1])]
    dLs = [-(_dot_x3(d1, s1, _DIMS["nt"]) + _dot_x3(d2, s2, _DIMS["nt"])) for d1, s1, d2, s2 in zip(d1s, s1s, d2s, s2s)]
    return [jnp.zeros_like(i) for i in invs], dLs, d1s, d2s


tri_apply.defvjp(_tri_fwd, _tri_bwd)


def _gdn_chunk(qs, ks, vs, gcbs, btbs, Ss, invs=None):
    C = qs[0].shape[0]
    r = lax.broadcasted_iota(jnp.int32, (C, C), 0)
    c = lax.broadcasted_iota(jnp.int32, (C, C), 1)
    causal, strict = c <= r, c < r
    rows = lax.broadcasted_iota(jnp.int32, gcbs[0].shape, 0)
    Gs = [g[:, :C] for g in gcbs]
    decays = [jnp.exp(jnp.where(causal, G - G.T, NEG)) for G in Gs]
    kbs = [k * b for k, b in zip(ks, btbs)]
    vbs = [v * b for v, b in zip(vs, btbs)]
    Ls = [jnp.where(strict, mm(kb, k, "nt", False) * d, 0.0) for kb, k, d in zip(kbs, ks, decays)]
    egs = [jnp.exp(g) for g in gcbs]
    if invs is None:
        invs = _unit_lower_inverses(Ls)
    us, ws = tri_apply(invs, Ls, vbs, [kb * eg for kb, eg in zip(kbs, egs)])
    qks = [jnp.where(causal, mm(q, k, "nt", False) * d, 0.0) for q, k, d in zip(qs, ks, decays)]
    g_lasts = [jnp.sum(jnp.where(rows == C - 1, g, 0.0), axis=0, keepdims=True) for g in gcbs]
    q_decs = [q * eg for q, eg in zip(qs, egs)]
    k_decs = [k * jnp.exp(gl - g) for k, gl, g in zip(ks, g_lasts, gcbs)]
    v_news = [u - mm(w, S, "nn", False) for u, w, S in zip(us, ws, Ss)]
    os_ = [mm(qd, S, "nn", False) + mm(qk, vn, "nn", False) for qd, S, qk, vn in zip(q_decs, Ss, qks, v_news)]
    S_news = [S * jnp.exp(gl) + mm(kd, vn, "tn", False) for S, gl, kd, vn in zip(Ss, g_lasts, k_decs, v_news)]
    return os_, S_news, invs


def gdn_core(qkv, gc, bt, S):
    nchunk = S // GDN_C

    def body(qkv_ref, g_ref, b_ref, o_ref, st_ref, inv_ref, s_scr):
        n = pl.program_id(0)

        @pl.when(n == 0)
        def _():
            s_scr[...] = jnp.zeros_like(s_scr)

        heads = range(GDN_H)
        S_in = [s_scr[h] for h in heads]
        os_, S_new, invs = _gdn_chunk([qkv_ref[0, h] for h in heads], [qkv_ref[1, h] for h in heads], [qkv_ref[2, h] for h in heads],
                                      [g_ref[h] for h in heads], [b_ref[h] for h in heads], S_in)
        for h in heads:
            st_ref[h] = S_in[h]
            inv_ref[h] = invs[h]
            o_ref[h] = os_[h]
            s_scr[h] = S_new[h]

    blk3 = pl.BlockSpec((3, GDN_H, GDN_C, LANES), lambda n: (0, 0, n, 0))
    hb = pl.BlockSpec((GDN_H, GDN_C, LANES), lambda n: (0, n, 0))
    return pl.pallas_call(
        body, name="gdn_core", grid=(nchunk,),
        in_specs=[blk3, hb, hb],
        out_specs=[hb, pl.BlockSpec((GDN_H, None, GDN_DK, LANES), lambda n: (0, n, 0, 0)),
                   pl.BlockSpec((GDN_H, None, GDN_C, GDN_C), lambda n: (0, n, 0, 0))],
        out_shape=[SDS((GDN_H, S, LANES), F32), SDS((GDN_H, nchunk, GDN_DK, LANES), F32), SDS((GDN_H, nchunk, GDN_C, GDN_C), F32)],
        scratch_shapes=[pltpu.VMEM((GDN_H, GDN_DK, LANES), F32)],
        compiler_params=_cp(1),
    )(qkv, gc, bt)


def gdn_core_bwd(qkv, gc, bt, states, invs, do, S):
    nchunk = S // GDN_C

    def body(qkv_ref, g_ref, b_ref, st_ref, inv_ref, do_ref, dqkv_ref, dg_ref, db_ref, ds_scr):
        n = pl.program_id(0)

        @pl.when(n == 0)
        def _():
            ds_scr[...] = jnp.zeros_like(ds_scr)

        heads = range(GDN_H)
        saved = [inv_ref[h] for h in heads]
        _, vjp = jax.vjp(lambda *a: _gdn_chunk(*a, invs=saved)[:2],
                         [qkv_ref[0, h] for h in heads], [qkv_ref[1, h] for h in heads], [qkv_ref[2, h] for h in heads],
                         [g_ref[h] for h in heads], [b_ref[h] for h in heads], [st_ref[h] for h in heads])
        dq, dk, dv, dg, db, dS = vjp(([do_ref[h] for h in heads], [ds_scr[h] for h in heads]))
        for h in heads:
            dqkv_ref[0, h] = dq[h]
            dqkv_ref[1, h] = dk[h]
            dqkv_ref[2, h] = dv[h]
            dg_ref[h] = dg[h]
            db_ref[h] = db[h]
            ds_scr[h] = dS[h]

    rev = lambda n: nchunk - 1 - n
    blk3 = pl.BlockSpec((3, GDN_H, GDN_C, LANES), lambda n: (0, 0, rev(n), 0))
    hb = pl.BlockSpec((GDN_H, GDN_C, LANES), lambda n: (0, rev(n), 0))
    return pl.pallas_call(
        body, name="gdn_core_bwd", grid=(nchunk,),
        in_specs=[blk3, hb, hb, pl.BlockSpec((GDN_H, None, GDN_DK, LANES), lambda n: (0, rev(n), 0, 0)),
                  pl.BlockSpec((GDN_H, None, GDN_C, GDN_C), lambda n: (0, rev(n), 0, 0)), hb],
        out_specs=[blk3, hb, hb],
        out_shape=[SDS((3, GDN_H, S, LANES), F32), SDS((GDN_H, S, LANES), F32), SDS((GDN_H, S, LANES), F32)],
        scratch_shapes=[pltpu.VMEM((GDN_H, GDN_DK, LANES), F32)],
        compiler_params=_cp(1),
    )(qkv, gc, bt, states, invs, do)


GDN_MAIN = 4 * GDN_H * LANES
GDN_PROJ = GDN_MAIN + LANES
RT = 256


def gdn_forward(h, w_in, conv_w, alog, dtb, out_gain, w_out):
    S = h.shape[0]
    nt = S // RT
    proj = matmul(h, w_in, "nn", BF16, "gdn_in")
    qkv = gdn_pre(proj, conv_w, S)
    ab_row = Row(proj, (RT, LANES), lambda i: (i, GDN_MAIN // LANES), gdtype=BF16, gshape=(S, LANES), gimap=lambda i: (i, 0))
    hm = lambda i: (0, i, 0)
    hv = GDN_H * LANES
    gc, bt = rowwise(f_gdn_gates, [ab_row], [alog, dtb],
                     [Out((GDN_H, S, LANES), F32, (GDN_H, RT, LANES), hm, lead=GDN_H)] * 2, (nt,), "gdn_gates")
    o, states, invs = gdn_core(qkv, gc, bt, S)
    o_row = Row(o, (GDN_H, RT, LANES), hm, lead=GDN_H)
    z_row = Row(proj, (RT, hv), lambda i: (i, 3), splits=[LANES] * GDN_H, gdtype=BF16, gshape=(S, hv), gimap=lambda i: (i, 0))
    (on,) = rowwise(f_gdn_post, [o_row, z_row], [out_gain],
                    [Out((S, hv), BF16, (RT, hv), lambda i: (i, 0), splits=[LANES] * GDN_H)], (nt,), "gdn_post")
    y = matmul(on, w_out, "nn", BF16, "gdn_out")
    saved = dict(h=h, proj=proj, qkv=qkv, gc=gc, bt=bt, states=states, invs=invs, o=o, on=on, ab_row=ab_row, o_row=o_row, z_row=z_row)
    return y, saved


def gdn_backward(dy, sv, w_in, conv_w, alog, dtb, out_gain, w_out):
    S = dy.shape[0]
    nt = S // RT
    hm = lambda i: (0, i, 0)
    hv = GDN_H * LANES
    don = matmul(dy, w_out, "nt", BF16, "gdn_out_dx")
    d_w_out = matmul(sv["on"], dy, "tn", F32, "gdn_out_dw")
    (do, dz), (d_gain,) = rowwise_bwd(f_gdn_post, [sv["o_row"], sv["z_row"]], [out_gain],
                                      [Row(don, (RT, hv), lambda i: (i, 0), splits=[LANES] * GDN_H)], (nt,), "gdn_post_bwd")
    dqkv, dgc, dbt = gdn_core_bwd(sv["qkv"], sv["gc"], sv["bt"], sv["states"], sv["invs"], do, S)
    head_blk = lambda a: Row(a, (GDN_H, RT, LANES), hm, lead=GDN_H)
    (dab,), (d_alog, d_dtb) = rowwise_bwd(f_gdn_gates, [sv["ab_row"]], [alog, dtb], [head_blk(dgc), head_blk(dbt)],
                                          (nt,), "gdn_gates_bwd")
    dqkv_proj, d_conv = gdn_pre_bwd(sv["proj"], conv_w, dqkv, S)
    dproj = jnp.concatenate([dqkv_proj, dz, dab], axis=1)
    d_w_in = matmul(sv["h"], dproj, "tn", F32, "gdn_in_dw")
    dh = matmul(dproj, w_in, "nt", BF16, "gdn_in_dx")
    return dh, dict(w_in=d_w_in, conv=d_conv, alog=d_alog, dtb=d_dtb, gain=d_gain, w_out=d_w_out)


QB = DSW_SPAN
N_HP = DSW_HG // LANES
PROJ_BLKS = 3 * 3 * N_HP


def _bucket_maps():
    a = np.arange(QB)[:, None]
    j = np.arange(2 * QB)[None, :]
    dist = QB + a - j
    band = (dist >= 0) & (dist <= DSW_SPAN)
    maps = []
    for _, dil in DSW_GROUPS:
        dd = np.maximum(dist, 0) * dil
        max_exact = REL_BUCKETS // 2
        scaled = np.log(np.maximum(dd, 1).astype(np.float32) / np.float32(max_exact)) / np.float32(math.log(REL_MAX_DIST / max_exact))
        large = max_exact + (scaled * np.float32(REL_BUCKETS - max_exact)).astype(np.int32)
        large = np.minimum(large, REL_BUCKETS - 1)
        maps.append(np.where(dd < max_exact, dd, large).astype(np.int32))
    return np.stack(maps), band


def dsw_bias(rel_bias):
    maps, band = _bucket_maps()
    maps = np.where(band[None], maps, -1).astype(np.int32)

    def body(tab_ref, bk_ref, o_ref):
        gh = pl.program_id(0)
        bk = bk_ref[...]
        acc = jnp.full(bk.shape, NEG, F32)
        for b in range(REL_BUCKETS):
            acc = jnp.where(bk == b, tab_ref[b, gh], acc)
        o_ref[...] = acc

    return pl.pallas_call(
        body, name="dsw_bias", grid=(3 * GDN_H,),
        in_specs=[pl.BlockSpec(memory_space=pltpu.SMEM),
                  pl.BlockSpec((None, QB, 2 * QB), lambda gh: (gh // GDN_H, 0, 0))],
        out_specs=pl.BlockSpec((None, QB, 2 * QB), lambda gh: (gh, 0, 0)),
        out_shape=SDS((3 * GDN_H, QB, 2 * QB), F32),
        compiler_params=_cp(1),
    )(rel_bias, jnp.asarray(maps))


def dsw_bias_grad(dbias):
    maps, band = _bucket_maps()
    maps = np.where(band[None], maps, -1).astype(np.int32)

    def body(d_ref, bk_ref, o_ref):
        bk = bk_ref[...]
        d = d_ref[...]
        rows = lax.broadcasted_iota(jnp.int32, (REL_BUCKETS, LANES), 0)
        acc = jnp.zeros((REL_BUCKETS, LANES), F32)
        for b in range(REL_BUCKETS):
            part = jnp.sum(jnp.where(bk == b, d, 0.0), axis=0, keepdims=True)
            val = jnp.sum(part, axis=1, keepdims=True)
            acc = jnp.where(rows == b, val, acc)
        o_ref[...] = acc

    return pl.pallas_call(
        body, name="dsw_bias_grad", grid=(3 * GDN_H,),
        in_specs=[pl.BlockSpec((None, QB, 2 * QB), lambda gh: (gh, 0, 0)),
                  pl.BlockSpec((None, QB, 2 * QB), lambda gh: (gh // GDN_H, 0, 0))],
        out_specs=pl.BlockSpec((None, REL_BUCKETS, LANES), lambda gh: (gh, 0, 0)),
        out_shape=SDS((3 * GDN_H, REL_BUCKETS, LANES), F32),
        compiler_params=_cp(1),
    )(dbias, jnp.asarray(maps))


def _nt(a, b):
    return lax.dot_general(a, b, (((1,), (1,)), ((), ())), preferred_element_type=F32)


def _tn(a, b):
    return lax.dot_general(a, b, (((0,), (0,)), ((), ())), preferred_element_type=F32)


def dsw_group_fwd(qn, kn, proj, bias, gi, S):
    dil = DSW_GROUPS[gi][1]
    sd = S // dil
    nq = sd // QB
    qv = qn.reshape(sd, dil * 3 * DSW_HG)
    kv = kn.reshape(sd, dil * 3 * DSW_HG)
    pv = proj.reshape(sd, dil * 9 * DSW_HG)
    qk_col = lambda hp, r: r * (3 * N_HP) + gi * N_HP + hp
    v_col = lambda hp, r: r * PROJ_BLKS + 2 * 3 * N_HP + gi * N_HP + hp

    def body(q_ref, kp_ref, kc_ref, vp_ref, vc_ref, b_ref, o_ref, l_ref):
        i = pl.program_id(2)
        q = q_ref[...]
        k2 = jnp.concatenate([kp_ref[...], kc_ref[...]], axis=0)
        v2 = jnp.concatenate([vp_ref[...], vc_ref[...]], axis=0).astype(BF16)
        lane_q = lax.broadcasted_iota(jnp.int32, (QB, LANES), 1) < DSW_DH
        lane_k = lax.broadcasted_iota(jnp.int32, (2 * QB, LANES), 1) < DSW_DH
        col = lax.broadcasted_iota(jnp.int32, (QB, 2 * QB), 1)
        first = jnp.logical_and(i == 0, col < QB)
        o_acc = jnp.zeros((QB, LANES), F32)
        lse_b = jnp.zeros((QB, LANES), F32)
        for hh in range(2):
            mq = lane_q if hh == 0 else jnp.logical_not(lane_q)
            mk = lane_k if hh == 0 else jnp.logical_not(lane_k)
            s = _nt(jnp.where(mq, q, 0).astype(BF16), k2) + b_ref[hh]
            s = jnp.where(first, NEG, s)
            mx = jnp.max(s, axis=1, keepdims=True)
            p = jnp.exp(s - mx)
            l = jnp.sum(p, axis=1, keepdims=True)
            oh = jnp.dot(p.astype(BF16), jnp.where(mk, v2, 0).astype(BF16), preferred_element_type=F32) / l
            o_acc = o_acc + oh
            lse_b = jnp.where(mq, mx + jnp.log(l), lse_b)
        o_ref[...] = o_acc
        l_ref[...] = lse_b

    blk = (QB, LANES)
    out_spec = pl.BlockSpec(blk, lambda hp, r, i: (i, r * N_HP + hp))
    o, lse = pl.pallas_call(
        body, name=f"dsw_fwd_g{gi}", grid=(N_HP, dil, nq),
        in_specs=[pl.BlockSpec(blk, lambda hp, r, i: (i, qk_col(hp, r))),
                  pl.BlockSpec(blk, lambda hp, r, i: (jnp.maximum(i - 1, 0), qk_col(hp, r))),
                  pl.BlockSpec(blk, lambda hp, r, i: (i, qk_col(hp, r))),
                  pl.BlockSpec(blk, lambda hp, r, i: (jnp.maximum(i - 1, 0), v_col(hp, r))),
                  pl.BlockSpec(blk, lambda hp, r, i: (i, v_col(hp, r))),
                  pl.BlockSpec((2, QB, 2 * QB), lambda hp, r, i: (gi * N_HP + hp, 0, 0))],
        out_specs=[out_spec, out_spec],
        out_shape=[SDS((sd, dil * DSW_HG), F32)] * 2,
        compiler_params=_cp(3),
    )(qv, kv, kv, pv, pv, bias)
    return o.reshape(S, DSW_HG), lse.reshape(S, DSW_HG)


def dsw_group_bwd(qn, kn, proj, bias, do, o, lse, gi, S):
    dil = DSW_GROUPS[gi][1]
    sd = S // dil
    nq = sd // QB
    qv = qn.reshape(sd, dil * 3 * DSW_HG)
    kv = kn.reshape(sd, dil * 3 * DSW_HG)
    pv = proj.reshape(sd, dil * 9 * DSW_HG)
    dov = do.reshape(sd, dil * DSW_HG)
    ov = o.reshape(sd, dil * DSW_HG)
    lv = lse.reshape(sd, dil * DSW_HG)
    qk_col = lambda hp, r: r * (3 * N_HP) + gi * N_HP + hp
    v_col = lambda hp, r: r * PROJ_BLKS + 2 * 3 * N_HP + gi * N_HP + hp
    o_col = lambda hp, r: r * N_HP + hp
    cur = lambda i: jnp.minimum(i, nq - 1)
    prev = lambda i: jnp.maximum(jnp.minimum(i, nq - 1) - 1, 0)
    done = lambda i: jnp.maximum(i - 1, 0)

    def body(q_ref, kp_ref, kc_ref, vp_ref, vc_ref, b_ref, do_ref, o_ref, l_ref,
             dq_ref, dk_ref, dv_ref, db_ref, dk_scr, dv_scr):
        r, i = pl.program_id(1), pl.program_id(2)

        @pl.when(jnp.logical_and(r == 0, i == 0))
        def _():
            db_ref[...] = jnp.zeros_like(db_ref)

        @pl.when(i == 0)
        def _():
            dk_scr[...] = jnp.zeros_like(dk_scr)
            dv_scr[...] = jnp.zeros_like(dv_scr)

        @pl.when(i < nq)
        def _():
            q = q_ref[...]
            k2 = jnp.concatenate([kp_ref[...], kc_ref[...]], axis=0)
            v2 = jnp.concatenate([vp_ref[...], vc_ref[...]], axis=0).astype(BF16)
            dout = do_ref[...].astype(F32)
            prod = dout * o_ref[...].astype(F32)
            lse_b = l_ref[...]
            lane_q = lax.broadcasted_iota(jnp.int32, (QB, LANES), 1) < DSW_DH
            col = lax.broadcasted_iota(jnp.int32, (QB, 2 * QB), 1)
            first = jnp.logical_and(i == 0, col < QB)
            dq = jnp.zeros((QB, LANES), F32)
            dk2 = jnp.zeros((2 * QB, LANES), F32)
            dv2 = jnp.zeros((2 * QB, LANES), F32)
            for hh in range(2):
                mq = lane_q if hh == 0 else jnp.logical_not(lane_q)
                qm = jnp.where(mq, q, 0).astype(BF16)
                dom = jnp.where(mq, dout, 0.0).astype(BF16)
                s = _nt(qm, k2) + b_ref[hh]
                s = jnp.where(first, NEG, s)
                lse_h = jnp.max(jnp.where(mq, lse_b, NEG), axis=1, keepdims=True)
                p = jnp.exp(s - lse_h)
                delta = jnp.sum(jnp.where(mq, prod, 0.0), axis=1, keepdims=True)
                dp = _nt(dom, v2)
                ds = p * (dp - delta)
                dsb = ds.astype(BF16)
                dq = dq + jnp.where(mq, jnp.dot(dsb, k2, preferred_element_type=F32), 0.0)
                dk2 = dk2 + _tn(dsb, qm)
                dv2 = dv2 + _tn(p.astype(BF16), dom)
                db_ref[hh] += ds
            dq_ref[...] = dq
            dk_ref[...] = dk_scr[...] + dk2[:QB]
            dv_ref[...] = (dv_scr[...] + dv2[:QB]).astype(dv_ref.dtype)
            dk_scr[...] = dk2[QB:]
            dv_scr[...] = dv2[QB:]

        @pl.when(i == nq)
        def _():
            dk_ref[...] = dk_scr[...]
            dv_ref[...] = dv_scr[...].astype(dv_ref.dtype)

    blk = (QB, LANES)
    dq, dk, dv, dbias = pl.pallas_call(
        body, name=f"dsw_bwd_g{gi}", grid=(N_HP, dil, nq + 1),
        in_specs=[pl.BlockSpec(blk, lambda hp, r, i: (cur(i), qk_col(hp, r))),
                  pl.BlockSpec(blk, lambda hp, r, i: (prev(i), qk_col(hp, r))),
                  pl.BlockSpec(blk, lambda hp, r, i: (cur(i), qk_col(hp, r))),
                  pl.BlockSpec(blk, lambda hp, r, i: (prev(i), v_col(hp, r))),
                  pl.BlockSpec(blk, lambda hp, r, i: (cur(i), v_col(hp, r))),
                  pl.BlockSpec((2, QB, 2 * QB), lambda hp, r, i: (gi * N_HP + hp, 0, 0)),
                  pl.BlockSpec(blk, lambda hp, r, i: (cur(i), o_col(hp, r))),
                  pl.BlockSpec(blk, lambda hp, r, i: (cur(i), o_col(hp, r))),
                  pl.BlockSpec(blk, lambda hp, r, i: (cur(i), o_col(hp, r)))],
        out_specs=[pl.BlockSpec(blk, lambda hp, r, i: (cur(i), o_col(hp, r))),
                   pl.BlockSpec(blk, lambda hp, r, i: (done(i), o_col(hp, r))),
                   pl.BlockSpec(blk, lambda hp, r, i: (done(i), o_col(hp, r))),
                   pl.BlockSpec((2, QB, 2 * QB), lambda hp, r, i: (hp, 0, 0))],
        out_shape=[SDS((sd, dil * DSW_HG), F32), SDS((sd, dil * DSW_HG), F32), SDS((sd, dil * DSW_HG), BF16),
                   SDS((GDN_H, QB, 2 * QB), F32)],
        scratch_shapes=[pltpu.VMEM(blk, F32), pltpu.VMEM(blk, F32)],
        compiler_params=_cp(3),
    )(qv, kv, kv, pv, pv, bias, dov, ov, lv)
    return dq.reshape(S, DSW_HG), dk.reshape(S, DSW_HG), dv.reshape(S, DSW_HG), dbias


def dsw_forward(h, w_in, q_gain2, k_gain2, rel_bias, w_out):
    S = h.shape[0]
    nt = S // RT
    nb = 3 * N_HP
    proj = matmul(h, w_in, "nn", F32, "dsw_in")
    width = nb * LANES
    lanes12 = [LANES] * nb
    (qn,) = rowwise(f_qnorm, [Row(proj, (RT, width), lambda i: (i, 0), splits=lanes12)], [q_gain2],
                    [Out((S, width), BF16, (RT, width), lambda i: (i, 0), splits=lanes12)], (nt,), "dsw_qnorm")
    (kn,) = rowwise(f_qknorm, [Row(proj, (RT, width), lambda i: (i, 1), splits=lanes12)], [k_gain2],
                    [Out((S, width), BF16, (RT, width), lambda i: (i, 0), splits=lanes12)], (nt,), "dsw_knorm")
    bias = dsw_bias(rel_bias)
    os_, ls_ = [], []
    for gi in range(3):
        o, l = dsw_group_fwd(qn, kn, proj, bias, gi, S)
        os_.append(o)
        ls_.append(l)
    full = lambda a: Row(a, (RT, DSW_HG), lambda i: (i, 0))
    o, lse = rowwise(f_combine, [full(a) for a in os_ + ls_], [],
                     [Out((S, DSW_HG), BF16, (RT, DSW_HG), lambda i: (i, 0)), Out((S, DSW_HG), F32, (RT, DSW_HG), lambda i: (i, 0))],
                     (nt,), "dsw_combine")
    y = matmul(o, w_out, "nn", F32, "dsw_out")
    return y, dict(h=h, proj=proj, qn=qn, kn=kn, bias=bias, o=o, lse=lse)


def dsw_backward(dy, sv, w_in, q_gain2, k_gain2, w_out):
    S = dy.shape[0]
    nt = S // RT
    nb = 3 * N_HP
    do = matmul(dy, w_out, "nt", BF16, "dsw_out_dx")
    d_w_out = matmul(sv["o"], dy, "tn", F32, "dsw_out_dw")
    pieces_q, pieces_k, pieces_v, dbs = [], [], [], []
    d_qg = jnp.zeros((1, LANES), F32)
    d_kg = jnp.zeros((1, LANES), F32)
    for gi in range(3):
        dq, dk, dv, db = dsw_group_bwd(sv["qn"], sv["kn"], sv["proj"], sv["bias"], do, sv["o"], sv["lse"], gi, S)
        dbs.append(db)
        pieces_v.append(dv)
        for which, dd in ((0, dq), (1, dk)):
            lanes4 = [LANES] * N_HP
            row = Row(sv["proj"], (RT, DSW_HG), lambda i, _o=which * 3 + gi: (i, _o), splits=lanes4,
                      gdtype=BF16, gshape=(S, DSW_HG), gimap=lambda i: (i, 0))
            fn, gain = (f_qnorm, q_gain2) if which == 0 else (f_qknorm, k_gain2)
            (dx,), (dg,) = rowwise_bwd(fn, [row], [gain], [Row(dd, (RT, DSW_HG), lambda i: (i, 0), splits=lanes4)],
                                       (nt,), f"dsw_norm_bwd_{which}{gi}")
            if which == 0:
                pieces_q.append(dx)
                d_qg = d_qg + dg
            else:
                pieces_k.append(dx)
                d_kg = d_kg + dg
    dproj = jnp.concatenate(pieces_q + pieces_k + pieces_v, axis=1)
    d_w_in = matmul(sv["h"], dproj, "tn", F32, "dsw_in_dw")
    dh = matmul(dproj, w_in, "nt", F32, "dsw_in_dx")
    d_rel = dsw_bias_grad(jnp.concatenate(dbs, axis=0))
    return dh, dict(w_in=d_w_in, q_gain2=d_qg, k_gain2=d_kg, rel=d_rel, w_out=d_w_out)


N_LB = DSW_HG // LANES
HALF = DSW_DH // 2


def _lanes(j):
    return slice(LANES * j, LANES * (j + 1))


def _deinterleave(stage, out_ref, dil, rows, dtype):
    for r in range(dil):
        for j in range(N_LB):
            out_ref[r, :, _lanes(j)] = stage[j, pl.ds(r, rows, stride=dil), :].astype(dtype)


def _interleave(in_ref, stage, dil, rows):
    for r in range(dil):
        for j in range(N_LB):
            stage[j, pl.ds(r, rows, stride=dil), :] = in_ref[r, :, _lanes(j)].astype(F32)


def dsw_prep(proj, q_gain2, k_gain2, gi, S):
    dil = DSW_GROUPS[gi][1]
    nt, rows = S // RT, RT // dil

    def body(q_ref, k_ref, v_ref, qg_ref, kg_ref, qo_ref, ko_ref, vo_ref, stage):
        for src, gain_ref, scale, dst in ((q_ref, qg_ref, DSW_DH ** -0.5, qo_ref), (k_ref, kg_ref, 1.0, ko_ref), (v_ref, None, None, vo_ref)):
            for j in range(N_LB):
                val = src[:, _lanes(j)].astype(F32)
                stage[j] = val if gain_ref is None else _qknorm1(val, gain_ref[...], scale)
            _deinterleave(stage, dst, dil, rows, BF16)

    col = lambda which: pl.BlockSpec((RT, DSW_HG), lambda i, _c=which * 3 + gi: (i, _c))
    gspec = pl.BlockSpec((1, LANES), lambda i: (0, 0))
    ospec = pl.BlockSpec((dil, rows, DSW_HG), lambda i: (0, i, 0))
    return pl.pallas_call(
        body, name=f"dsw_prep_g{gi}", grid=(nt,),
        in_specs=[col(0), col(1), col(2), gspec, gspec], out_specs=[ospec] * 3,
        out_shape=[SDS((dil, S // dil, DSW_HG), BF16)] * 3,
        scratch_shapes=[pltpu.VMEM((N_LB, RT, LANES), F32)], compiler_params=_cp(1),
    )(proj, proj, proj, q_gain2, k_gain2)


def dsw_prep_bwd(proj, q_gain2, k_gain2, dqd, dkd, dvd, gi, S):
    dil = DSW_GROUPS[gi][1]
    nt, rows = S // RT, RT // dil

    def body(q_ref, k_ref, qg_ref, kg_ref, dq_ref, dk_ref, dv_ref, oq_ref, ok_ref, ov_ref, dqg_ref, dkg_ref, stage):
        i = pl.program_id(0)

        @pl.when(i == 0)
        def _():
            dqg_ref[...] = jnp.zeros_like(dqg_ref)
            dkg_ref[...] = jnp.zeros_like(dkg_ref)

        for src, gain_ref, scale, cot_ref, dst, dg_ref in ((q_ref, qg_ref, DSW_DH ** -0.5, dq_ref, oq_ref, dqg_ref),
                                                          (k_ref, kg_ref, 1.0, dk_ref, ok_ref, dkg_ref)):
            _interleave(cot_ref, stage, dil, rows)
            for j in range(N_LB):
                _, vjp = jax.vjp(lambda x, g, _s=scale: _qknorm1(x, g, _s), src[:, _lanes(j)].astype(F32), gain_ref[...])
                dx, dg = vjp(stage[j])
                dst[:, _lanes(j)] = dx.astype(dst.dtype)
                dg_ref[...] += dg
        _interleave(dv_ref, stage, dil, rows)
        for j in range(N_LB):
            ov_ref[:, _lanes(j)] = stage[j].astype(ov_ref.dtype)

    col = lambda which: pl.BlockSpec((RT, DSW_HG), lambda i, _c=which * 3 + gi: (i, _c))
    gspec = pl.BlockSpec((1, LANES), lambda i: (0, 0))
    dspec = pl.BlockSpec((dil, rows, DSW_HG), lambda i: (0, i, 0))
    nspec = pl.BlockSpec((RT, DSW_HG), lambda i: (i, 0))
    return pl.pallas_call(
        body, name=f"dsw_prep_bwd_g{gi}", grid=(nt,),
        in_specs=[col(0), col(1), gspec, gspec, dspec, dspec, dspec], out_specs=[nspec] * 3 + [gspec] * 2,
        out_shape=[SDS((S, DSW_HG), BF16)] * 3 + [SDS((1, LANES), F32)] * 2,
        scratch_shapes=[pltpu.VMEM((N_LB, RT, LANES), F32)], compiler_params=_cp(1),
    )(proj, proj, q_gain2, k_gain2, dqd, dkd, dvd)


def _head_masks(rows):
    lane = lax.broadcasted_iota(jnp.int32, (rows, LANES), 1)
    return lane < DSW_DH, (lane % DSW_DH) < HALF


def dsw_attn_fwd(qd, kd, vd, bias, gi, S):
    dil = DSW_GROUPS[gi][1]
    sd = S // dil
    nq = sd // QB

    def body(q_ref, k_ref, v_ref, b_ref, o_ref, l_ref, kp_scr, vp_scr):
        i = pl.program_id(1)

        @pl.when(i == 0)
        def _():
            kp_scr[...] = jnp.zeros_like(kp_scr)
            vp_scr[...] = jnp.zeros_like(vp_scr)

        lo_q, _ = _head_masks(QB)
        lo_k, _ = _head_masks(2 * QB)
        col = lax.broadcasted_iota(jnp.int32, (QB, 2 * QB), 1)
        first = jnp.logical_and(i == 0, col < QB)
        hps, heads = range(N_HP), range(2 * N_HP)
        k2s = [jnp.concatenate([kp_scr[:, _lanes(hp)], k_ref[:, _lanes(hp)]], axis=0) for hp in hps]
        v2s = [jnp.concatenate([vp_scr[:, _lanes(hp)], v_ref[:, _lanes(hp)]], axis=0) for hp in hps]
        qs = [q_ref[:, _lanes(hp)] for hp in hps]
        k_now, v_now = k_ref[...], v_ref[...]
        mqs = [lo_q if h % 2 == 0 else jnp.logical_not(lo_q) for h in heads]
        mks = [lo_k if h % 2 == 0 else jnp.logical_not(lo_k) for h in heads]
        ss = [jnp.where(first, NEG, _nt(jnp.where(mqs[h], qs[h // 2], 0).astype(BF16), k2s[h // 2]) + b_ref[h]) for h in heads]
        mxs = [jnp.max(s, axis=1, keepdims=True) for s in ss]
        ps = [jnp.exp(s - mx) for s, mx in zip(ss, mxs)]
        ls = [jnp.sum(p, axis=1, keepdims=True) for p in ps]
        ohs = [jnp.dot(ps[h].astype(BF16), jnp.where(mks[h], v2s[h // 2], 0).astype(BF16), preferred_element_type=F32) / ls[h] for h in heads]
        lse_h = [mx + jnp.log(l) for mx, l in zip(mxs, ls)]
        for hp in hps:
            o_ref[:, _lanes(hp)] = ohs[2 * hp] + ohs[2 * hp + 1]
            l_ref[:, _lanes(hp)] = jnp.where(lo_q, lse_h[2 * hp], lse_h[2 * hp + 1])
        kp_scr[...] = k_now
        vp_scr[...] = v_now

    blk = pl.BlockSpec((None, QB, DSW_HG), lambda r, i: (r, i, 0))
    return pl.pallas_call(
        body, name=f"dsw_attn_g{gi}", grid=(dil, nq),
        in_specs=[blk, blk, blk, pl.BlockSpec((GDN_H, QB, 2 * QB), lambda r, i: (gi, 0, 0))],
        out_specs=[blk, blk], out_shape=[SDS((dil, sd, DSW_HG), F32)] * 2,
        scratch_shapes=[pltpu.VMEM((QB, DSW_HG), BF16)] * 2, compiler_params=_cp(2),
    )(qd, kd, vd, bias)


def dsw_attn_bwd(qd, kd, vd, bias, dod, statd, gi, S):
    dil = DSW_GROUPS[gi][1]
    sd = S // dil
    nq = sd // QB
    cur = lambda i: jnp.minimum(i, nq - 1)
    done = lambda i: jnp.maximum(i - 1, 0)

    def body(q_ref, k_ref, v_ref, b_ref, do_ref, st_ref, dq_ref, dk_ref, dv_ref, db_ref, kp_scr, vp_scr, dk_scr, dv_scr):
        r, i = pl.program_id(0), pl.program_id(1)

        @pl.when(jnp.logical_and(r == 0, i == 0))
        def _():
            db_ref[...] = jnp.zeros_like(db_ref)

        @pl.when(i == 0)
        def _():
            for scr in (kp_scr, vp_scr, dk_scr, dv_scr):
                scr[...] = jnp.zeros_like(scr)

        @pl.when(i < nq)
        def _():
            lo_q, first_half = _head_masks(QB)
            col = lax.broadcasted_iota(jnp.int32, (QB, 2 * QB), 1)
            first = jnp.logical_and(i == 0, col < QB)
            hps, heads = range(N_HP), range(2 * N_HP)
            k2s = [jnp.concatenate([kp_scr[:, _lanes(hp)], k_ref[:, _lanes(hp)]], axis=0) for hp in hps]
            v2s = [jnp.concatenate([vp_scr[:, _lanes(hp)], v_ref[:, _lanes(hp)]], axis=0) for hp in hps]
            qs = [q_ref[:, _lanes(hp)] for hp in hps]
            douts = [do_ref[:, _lanes(hp)] for hp in hps]
            stats = [st_ref[:, _lanes(hp)] for hp in hps]
            dkc = [dk_scr[:, _lanes(hp)] for hp in hps]
            dvc = [dv_scr[:, _lanes(hp)] for hp in hps]
            k_now, v_now = k_ref[...], v_ref[...]
            mqs = [lo_q if h % 2 == 0 else jnp.logical_not(lo_q) for h in heads]
            qms = [jnp.where(mqs[h], qs[h // 2], 0).astype(BF16) for h in heads]
            doms = [jnp.where(mqs[h], douts[h // 2], 0).astype(BF16) for h in heads]
            ss = [jnp.where(first, NEG, _nt(qms[h], k2s[h // 2]) + b_ref[h]) for h in heads]
            lses = [jnp.max(jnp.where(jnp.logical_and(mqs[h], first_half), stats[h // 2], NEG), axis=1, keepdims=True) for h in heads]
            deltas = [jnp.max(jnp.where(jnp.logical_and(mqs[h], jnp.logical_not(first_half)), stats[h // 2], NEG), axis=1, keepdims=True)
                      for h in heads]
            ps = [jnp.exp(ss[h] - lses[h]) for h in heads]
            dss = [ps[h] * (_nt(doms[h], v2s[h // 2]) - deltas[h]) for h in heads]
            dsbs = [d.astype(BF16) for d in dss]
            dqh = [jnp.where(mqs[h], jnp.dot(dsbs[h], k2s[h // 2], preferred_element_type=F32), 0.0) for h in heads]
            dkh = [_tn(dsbs[h], qms[h]) for h in heads]
            dvh = [_tn(ps[h].astype(BF16), doms[h]) for h in heads]
            for h in heads:
                db_ref[h] += dss[h]
            for hp in hps:
                dk2 = dkh[2 * hp] + dkh[2 * hp + 1]
                dv2 = dvh[2 * hp] + dvh[2 * hp + 1]
                dq_ref[:, _lanes(hp)] = dqh[2 * hp] + dqh[2 * hp + 1]
                dk_ref[:, _lanes(hp)] = dkc[hp] + dk2[:QB]
                dv_ref[:, _lanes(hp)] = (dvc[hp] + dv2[:QB]).astype(dv_ref.dtype)
                dk_scr[:, _lanes(hp)] = dk2[QB:]
                dv_scr[:, _lanes(hp)] = dv2[QB:]
            kp_scr[...] = k_now
            vp_scr[...] = v_now

        @pl.when(i == nq)
        def _():
            dk_ref[...] = dk_scr[...]
            dv_ref[...] = dv_scr[...].astype(dv_ref.dtype)

    blk = pl.BlockSpec((None, QB, DSW_HG), lambda r, i: (r, cur(i), 0))
    oblk = pl.BlockSpec((None, QB, DSW_HG), lambda r, i: (r, done(i), 0))
    return pl.pallas_call(
        body, name=f"dsw_attn_bwd_g{gi}", grid=(dil, nq + 1),
        in_specs=[blk, blk, blk, pl.BlockSpec((GDN_H, QB, 2 * QB), lambda r, i: (gi, 0, 0)), blk, blk],
        out_specs=[blk, oblk, oblk, pl.BlockSpec((GDN_H, QB, 2 * QB), lambda r, i: (0, 0, 0))],
        out_shape=[SDS((dil, sd, DSW_HG), F32), SDS((dil, sd, DSW_HG), F32), SDS((dil, sd, DSW_HG), BF16),
                   SDS((GDN_H, QB, 2 * QB), F32)],
        scratch_shapes=[pltpu.VMEM((QB, DSW_HG), BF16)] * 2 + [pltpu.VMEM((QB, DSW_HG), F32)] * 2,
        compiler_params=_cp(2),
    )(qd, kd, vd, bias, dod, statd)


def dsw_combine(ods, lseds, S):
    nt = S // RT
    dils = [d for _, d in DSW_GROUPS]

    def body(*refs):
        ins, (o_ref, l_ref), stages = refs[:6], refs[6:8], refs[8:]
        for g in range(3):
            _interleave(ins[g], stages[g], dils[g], RT // dils[g])
            _interleave(ins[3 + g], stages[3 + g], dils[g], RT // dils[g])
        for j in range(N_LB):
            o, lse = f_combine(None, *[st[j] for st in stages])
            o_ref[:, _lanes(j)] = o.astype(o_ref.dtype)
            l_ref[:, _lanes(j)] = lse

    dspec = lambda d: pl.BlockSpec((d, RT // d, DSW_HG), lambda i: (0, i, 0))
    nspec = pl.BlockSpec((RT, DSW_HG), lambda i: (i, 0))
    return pl.pallas_call(
        body, name="dsw_combine", grid=(nt,),
        in_specs=[dspec(d) for d in dils] * 2, out_specs=[nspec, nspec],
        out_shape=[SDS((S, DSW_HG), BF16), SDS((S, DSW_HG), F32)],
        scratch_shapes=[pltpu.VMEM((N_LB, RT, LANES), F32)] * 6, compiler_params=_cp(1),
    )(*ods, *lseds)


def dsw_bwd_prep(do, o, lse, S):
    nt = S // RT
    dils = [d for _, d in DSW_GROUPS]

    def body(do_ref, o_ref, l_ref, *rest):
        outs, (st_do, st_stat) = rest[:6], rest[6:]
        lo, first_half = _head_masks(RT)
        for j in range(N_LB):
            dout = do_ref[:, _lanes(j)]
            prod = dout * o_ref[:, _lanes(j)].astype(F32)
            s_all = jnp.sum(prod, axis=1, keepdims=True)
            s_lo = jnp.sum(jnp.where(lo, prod, 0.0), axis=1, keepdims=True)
            delta = jnp.where(lo, s_lo, s_all - s_lo)
            st_do[j] = dout
            st_stat[j] = jnp.where(first_half, l_ref[:, _lanes(j)], delta)
        for g in range(3):
            _deinterleave(st_do, outs[g], dils[g], RT // dils[g], BF16)
            _deinterleave(st_stat, outs[3 + g], dils[g], RT // dils[g], F32)

    nspec = pl.BlockSpec((RT, DSW_HG), lambda i: (i, 0))
    dspec = lambda d: pl.BlockSpec((d, RT // d, DSW_HG), lambda i: (0, i, 0))
    res = pl.pallas_call(
        body, name="dsw_bwd_prep", grid=(nt,),
        in_specs=[nspec] * 3, out_specs=[dspec(d) for d in dils] * 2,
        out_shape=[SDS((d, S // d, DSW_HG), BF16) for d in dils] + [SDS((d, S // d, DSW_HG), F32) for d in dils],
        scratch_shapes=[pltpu.VMEM((N_LB, RT, LANES), F32)] * 2, compiler_params=_cp(1),
    )(do, o, lse)
    return res[:3], res[3:]


def dsw_forward(h, w_in, q_gain2, k_gain2, rel_bias, w_out):
    S = h.shape[0]
    proj = matmul(h, w_in, "nn", BF16, "dsw_in", col_shards=N_SHARD)
    bias = dsw_bias(rel_bias)
    qkv, ods, lseds = [], [], []
    for gi in range(3):
        qd, kd, vd = dsw_prep(proj, q_gain2, k_gain2, gi, S)
        od, ld = dsw_attn_fwd(qd, kd, vd, bias, gi, S)
        qkv.append((qd, kd, vd))
        ods.append(od)
        lseds.append(ld)
    o, lse = dsw_combine(ods, lseds, S)
    y = matmul(o, w_out, "nn", BF16, "dsw_out", col_shards=N_SHARD)
    return y, dict(h=h, proj=proj, qkv=qkv, bias=bias, o=o, lse=lse)


def dsw_backward(dy, sv, w_in, q_gain2, k_gain2, w_out):
    S = dy.shape[0]
    do = matmul(dy, w_out, "nt", F32, "dsw_out_dx", col_shards=N_SHARD)
    d_w_out = matmul(sv["o"], dy, "tn", F32, "dsw_out_dw", col_shards=N_SHARD)
    dods, statds = dsw_bwd_prep(do, sv["o"], sv["lse"], S)
    pieces_q, pieces_k, pieces_v, dbs = [], [], [], []
    d_qg = jnp.zeros((1, LANES), F32)
    d_kg = jnp.zeros((1, LANES), F32)
    for gi in range(3):
        qd, kd, vd = sv["qkv"][gi]
        dqd, dkd, dvd, db = dsw_attn_bwd(qd, kd, vd, sv["bias"], dods[gi], statds[gi], gi, S)
        dq, dk, dv, dqg, dkg = dsw_prep_bwd(sv["proj"], q_gain2, k_gain2, dqd, dkd, dvd, gi, S)
        dbs.append(db)
        pieces_q.append(dq)
        pieces_k.append(dk)
        pieces_v.append(dv)
        d_qg = d_qg + dqg
        d_kg = d_kg + dkg
    dproj = jnp.concatenate(pieces_q + pieces_k + pieces_v, axis=1)
    d_w_in = matmul(sv["h"], dproj, "tn", F32, "dsw_in_dw", col_shards=N_SHARD)
    dh = matmul(dproj, w_in, "nt", BF16, "dsw_in_dx", col_shards=N_SHARD)
    d_rel = dsw_bias_grad(jnp.concatenate(dbs, axis=0))
    return dh, dict(w_in=d_w_in, q_gain2=d_qg, k_gain2=d_kg, rel=d_rel, w_out=d_w_out)


FT = 256


FUSE_M = 512


def ffn_in_act(h, w_in, name):
    S = h.shape[0]
    half = FFN // 2

    def body(h_ref, wg_ref, wu_ref, gu_ref, a_ref):
        j = pl.program_id(1)
        sub = FUSE_M // 2
        for part in range(2):
            rows = slice(part * sub, (part + 1) * sub)
            hb = h_ref[rows, :]
            g = jnp.dot(hb, wg_ref[...], preferred_element_type=F32)
            u = jnp.dot(hb, wu_ref[...], preferred_element_type=F32)
            a_ref[rows, :] = (_silu(g) * u).astype(a_ref.dtype)
            for jj in range(2):
                @pl.when(j == jj)
                def _(g=g, u=u, jj=jj, rows=rows):
                    gu_ref[rows, jj * half:(jj + 1) * half] = g.astype(gu_ref.dtype)
                    gu_ref[rows, FFN + jj * half:FFN + (jj + 1) * half] = u.astype(gu_ref.dtype)

    return pl.pallas_call(
        body, name=name, grid=(S // FUSE_M, 2),
        in_specs=[pl.BlockSpec((FUSE_M, D), lambda i, j: (i, 0)),
                  pl.BlockSpec((None, D, half), lambda i, j: (j, 0, 0)),
                  pl.BlockSpec((None, D, half), lambda i, j: (j + 2, 0, 0))],
        out_specs=[pl.BlockSpec((FUSE_M, 2 * FFN), lambda i, j: (i, 0)), pl.BlockSpec((FUSE_M, half), lambda i, j: (i, j))],
        out_shape=[SDS((S, 2 * FFN), BF16), SDS((S, FFN), BF16)],
        compiler_params=_cp(2),
    )(h, w_in, w_in)


def ffn_forward(h, w_in, w_out, tag):
    gu, a = ffn_in_act(h, w_in, f"ffn_in_act_{tag}")
    gu_row = Row(gu, (FT, 2 * FFN), lambda i: (i, 0), splits=[FFN, FFN], gdtype=BF16)
    f = matmul(a, w_out, "nn", BF16, f"ffn_out_{tag}")
    return f, dict(h=h, gu_row=gu_row, a=a)


def ffn_out_dx_act(df, w_out, gu, name):
    S = df.shape[0]
    half = FFN // 2

    def body(df_ref, w_ref, g_ref, u_ref, dgu_ref):
        j = pl.program_id(1)
        sub = FUSE_M // 2
        for part in range(2):
            rows = slice(part * sub, (part + 1) * sub)
            da = _nt(df_ref[rows, :], w_ref[...])
            dg, du = _swiglu_bwd((g_ref[rows, :].astype(F32), u_ref[rows, :].astype(F32)), da)
            for jj in range(2):
                @pl.when(j == jj)
                def _(dg=dg, du=du, jj=jj, rows=rows):
                    dgu_ref[rows, jj * half:(jj + 1) * half] = dg.astype(dgu_ref.dtype)
                    dgu_ref[rows, FFN + jj * half:FFN + (jj + 1) * half] = du.astype(dgu_ref.dtype)

    return pl.pallas_call(
        body, name=name, grid=(S // FUSE_M, 2),
        in_specs=[pl.BlockSpec((FUSE_M, D), lambda i, j: (i, 0)),
                  pl.BlockSpec((half, D), lambda i, j: (j, 0)),
                  pl.BlockSpec((FUSE_M, half), lambda i, j: (i, j)),
                  pl.BlockSpec((FUSE_M, half), lambda i, j: (i, j + 2))],
        out_specs=pl.BlockSpec((FUSE_M, 2 * FFN), lambda i, j: (i, 0)),
        out_shape=SDS((S, 2 * FFN), BF16),
        compiler_params=_cp(2),
    )(df, w_out, gu, gu)


def ffn_backward(df, sv, w_in, w_out, tag):
    d_w_out = matmul(sv["a"], df, "tn", F32, f"ffn_out_dw_{tag}")
    dgu = ffn_out_dx_act(df, w_out, sv["gu_row"].arr, f"ffn_out_dx_act_{tag}")
    d_w_in = matmul(sv["h"], dgu, "tn", F32, f"ffn_in_dw_{tag}", col_shards=N_SHARD)
    dh = matmul(dgu, w_in, "nt", BF16, f"ffn_in_dx_{tag}", col_shards=N_SHARD)
    return dh, d_w_in, d_w_out


def f_norm_only(ids, x, gain, sc, sh):
    return (_normmod(x, gain, sc, sh),)


WT = 512


def _wide(a, **kw):
    return Row(a, (WT, D), lambda i: (i, 0), **kw)


def _wide_out(S, dtype):
    return Out((S, D), dtype, (WT, D), lambda i: (i, 0))


def adamw(w, g, m, v, name):
    shape = w.shape
    C = shape[-1]
    R = int(np.prod(shape[:-1]))
    w2, g2, m2, v2 = (a.reshape(R, C) for a in (w, g, m, v))
    br = R
    if R > 256:
        br = max(b for b in range(8, 257, 8) if R % b == 0)
    c1 = 1.0 / (1.0 - ADAM_B1 ** ADAM_STEP)
    c2 = 1.0 / (1.0 - ADAM_B2 ** ADAM_STEP)

    def body(w_ref, g_ref, m_ref, v_ref, d_ref, nm_ref, nv_ref):
        gg = g_ref[...]
        mm_ = ADAM_B1 * m_ref[...] + (1.0 - ADAM_B1) * gg
        vv = ADAM_B2 * v_ref[...] + (1.0 - ADAM_B2) * (gg * gg)
        d_ref[...] = -ADAM_LR * ((mm_ * c1) / (jnp.sqrt(vv * c2) + ADAM_EPS) + ADAM_WD * w_ref[...])
        nm_ref[...] = mm_
        nv_ref[...] = vv

    spec = pl.BlockSpec((br, C), lambda i: (i, 0))
    d, nm, nv = pl.pallas_call(
        body, name=name, grid=(R // br,), in_specs=[spec] * 4, out_specs=[spec] * 3,
        out_shape=[SDS((R, C), F32)] * 3, compiler_params=_cp(1),
    )(w2, g2, m2, v2)
    return d.reshape(shape), nm.reshape(shape), nv.reshape(shape)


def _place():
    x, y, c = lax.axis_index("x"), lax.axis_index("y"), lax.axis_index("c")
    chips = [(1 - x, y), (x, 1 - y), (1 - x, 1 - y)]
    return x, y, c, chips


def all_gather_small(blk, name):
    m_per, n = blk.shape

    def body(x_ref, out_ref, send_sems, recv_sems, local_sem):
        x, y, c, chips = _place()
        me, sibling = (x, y, c), (x, y, 1 - c)

        def rows(px, py, pc):
            return out_ref.at[pl.ds((4 * px + 2 * py + pc) * m_per, m_per), :]

        def copy(k, block, to, src=None):
            return pltpu.make_async_remote_copy(
                src_ref=rows(*block) if src is None else src, dst_ref=rows(*block),
                send_sem=send_sems.at[k], recv_sem=recv_sems.at[k], device_id=to, device_id_type=MESH)

        mine = pltpu.make_async_copy(x_ref, rows(*me), local_sem)
        mine.start()
        first = [copy(0, me, sibling, src=x_ref)]
        first += [copy(1 + j, me, (*chip, c), src=x_ref) for j, chip in enumerate(chips)]
        for cp in first:
            cp.start()
        passed = [copy(4 + j, (*chip, c), sibling) for j, chip in enumerate(chips)]
        for j, chip in enumerate(chips):
            copy(1 + j, (*chip, c), me).wait_recv()
            passed[j].start()
        copy(0, sibling, me).wait_recv()
        for j, chip in enumerate(chips):
            copy(4 + j, (*chip, 1 - c), me).wait_recv()
        for cp in first + passed:
            cp.wait_send()
        mine.wait()

    return pl.pallas_call(
        body, name=name, out_shape=SDS((N_DEV * m_per, n), blk.dtype),
        in_specs=[pl.BlockSpec(memory_space=pltpu.VMEM)], out_specs=pl.BlockSpec(memory_space=pltpu.VMEM),
        scratch_shapes=[pltpu.SemaphoreType.DMA((7,)), pltpu.SemaphoreType.DMA((7,)), pltpu.SemaphoreType.DMA],
    )(blk)


def _half(cc, rh):
    return pl.ds(pl.multiple_of(cc * rh, 16), rh)


def all_gather_shards(ws):
    n = len(ws)

    def body(*refs):
        w_refs, out_refs = refs[:n], refs[n:2 * n]
        send_sems, recv_sems, local_sems, own_sems = refs[2 * n:]
        x, y, c, chips = _place()
        sibling = (x, y, 1 - c)
        s_me = 2 * x + y

        def copy(k, src, dst, to):
            return pltpu.make_async_remote_copy(src_ref=src, dst_ref=dst, send_sem=send_sems.at[k], recv_sem=recv_sems.at[k],
                                                device_id=to, device_id_type=MESH)

        local, sends, passed = [], [], []
        for k in range(n):
            rh = ws[k].shape[0] // 2
            cp = pltpu.make_async_remote_copy(src_ref=w_refs[k], dst_ref=out_refs[k].at[s_me], send_sem=local_sems.at[k],
                                              recv_sem=own_sems.at[k], device_id=sibling, device_id_type=MESH)
            cp.start()
            local.append(cp)
            for j, chip in enumerate(chips):
                sd = copy(6 * k + j, w_refs[k].at[_half(c, rh)], out_refs[k].at[s_me, _half(c, rh)], (*chip, c))
                sd.start()
                sends.append(sd)
        for k in range(n):
            rh = ws[k].shape[0] // 2
            for j, (px, py) in enumerate(chips):
                got = out_refs[k].at[2 * px + py, _half(c, rh)]
                copy(6 * k + j, got, got, (px, py, c)).wait_recv()
                fw = copy(6 * k + 3 + j, got, got, sibling)
                fw.start()
                passed.append(fw)
        for k in range(n):
            rh = ws[k].shape[0] // 2
            for j, (px, py) in enumerate(chips):
                got = out_refs[k].at[2 * px + py, _half(1 - c, rh)]
                copy(6 * k + 3 + j, got, got, sibling).wait_recv()
        for cp in sends + passed:
            cp.wait_send()
        for cp in local:
            cp.wait()

    return pl.pallas_call(
        body, name="weights_all_gather", out_shape=[SDS((N_SHARD,) + w.shape, w.dtype) for w in ws],
        in_specs=[ANY] * n, out_specs=[ANY] * n,
        scratch_shapes=[pltpu.SemaphoreType.DMA((6 * n,)), pltpu.SemaphoreType.DMA((6 * n,)), pltpu.SemaphoreType.DMA((n,)),
                        pltpu.SemaphoreType.DMA((n,))],
    )(*ws)


def sibling_exchange(sends, name):
    n = len(sends)

    def body(*refs):
        s_refs, o_refs, send_sems, recv_sems = refs[:n], refs[n:2 * n], refs[2 * n], refs[2 * n + 1]
        x, y, c, _ = _place()
        cps = [pltpu.make_async_remote_copy(src_ref=s_refs[k], dst_ref=o_refs[k], send_sem=send_sems.at[k], recv_sem=recv_sems.at[k],
                                            device_id=(x, y, 1 - c), device_id_type=MESH) for k in range(n)]
        for cp in cps:
            cp.start()
        for cp in cps:
            cp.wait()

    return pl.pallas_call(
        body, name=name, out_shape=[SDS(s.shape, s.dtype) for s in sends], in_specs=[ANY] * n, out_specs=[ANY] * n,
        scratch_shapes=[pltpu.SemaphoreType.DMA((n,)), pltpu.SemaphoreType.DMA((n,))],
    )(*sends)


def scatter_to_chips(parts):
    n = len(parts)

    def body(*refs):
        p_refs, o_refs, send_sems, recv_sems = refs[:n], refs[n:2 * n], refs[2 * n], refs[2 * n + 1]
        x, y, c, chips = _place()
        cps = []
        for k in range(n):
            for j, (px, py) in enumerate(chips):
                cp = pltpu.make_async_remote_copy(src_ref=p_refs[k].at[2 * px + py], dst_ref=o_refs[k].at[j],
                                                  send_sem=send_sems.at[3 * k + j], recv_sem=recv_sems.at[3 * k + j],
                                                  device_id=(px, py, c), device_id_type=MESH)
                cp.start()
                cps.append(cp)
        for cp in cps:
            cp.wait()

    return pl.pallas_call(
        body, name="grads_scatter", out_shape=[SDS((3,) + p.shape[1:], p.dtype) for p in parts], in_specs=[ANY] * n, out_specs=[ANY] * n,
        scratch_shapes=[pltpu.SemaphoreType.DMA((3 * n,)), pltpu.SemaphoreType.DMA((3 * n,))],
    )(*parts)


def merge_halves(halves):
    n = len(halves)

    def body(*refs):
        h_refs, o_refs = refs[:n], refs[n:2 * n]
        send_sems, recv_sems, local_sems = refs[2 * n:]
        x, y, c, _ = _place()
        local, cps = [], []
        for k in range(n):
            rh = halves[k].shape[0]
            lc = pltpu.make_async_copy(h_refs[k], o_refs[k].at[_half(c, rh)], local_sems.at[k])
            lc.start()
            local.append(lc)
            cp = pltpu.make_async_remote_copy(src_ref=h_refs[k], dst_ref=o_refs[k].at[_half(c, rh)], send_sem=send_sems.at[k],
                                              recv_sem=recv_sems.at[k], device_id=(x, y, 1 - c), device_id_type=MESH)
            cp.start()
            cps.append(cp)
        for k in range(n):
            rh = halves[k].shape[0]
            got = o_refs[k].at[_half(1 - c, rh)]
            pltpu.make_async_remote_copy(src_ref=got, dst_ref=got, send_sem=send_sems.at[k], recv_sem=recv_sems.at[k],
                                         device_id=(x, y, 1 - c), device_id_type=MESH).wait_recv()
        for cp in cps:
            cp.wait_send()
        for lc in local:
            lc.wait()

    return pl.pallas_call(
        body, name="grads_merge_halves", out_shape=[SDS((2 * h.shape[0], h.shape[1]), h.dtype) for h in halves],
        in_specs=[ANY] * n, out_specs=[ANY] * n,
        scratch_shapes=[pltpu.SemaphoreType.DMA((n,)), pltpu.SemaphoreType.DMA((n,)), pltpu.SemaphoreType.DMA((n,))],
    )(*halves)


def add_rows(arrs, out_dtype, name, rt=256):
    Rr, W = arrs[0].shape

    def fn(ids, *vals):
        acc = vals[0]
        for v in vals[1:]:
            acc = acc + v
        return (acc,)

    t = rt if Rr % rt == 0 else max(b for b in range(16, rt + 1, 16) if Rr % b == 0)
    (out,) = rowwise(fn, [Row(a, (t, W), lambda i: (i, 0)) for a in arrs], [],
                     [Out((Rr, W), out_dtype, (t, W), lambda i: (i, 0))], (Rr // t,), name)
    return out


HBM_SPEC = pl.BlockSpec(memory_space=pltpu.HBM)
SEM_SPEC = pl.BlockSpec(memory_space=pltpu.SEMAPHORE)
DATAFLOW = pltpu.SideEffectType.DATAFLOW_SIDE_EFFECTING


def _in_hbm(a):
    return pltpu.with_memory_space_constraint(a, pltpu.HBM)


def _gather_copies(w_refs, land_refs, send_sems, recv_sems):
    x, y, c, chips = _place()
    targets = [(x, y, 1 - c)] + [(*chip, c) for chip in chips]
    cps = []
    for k, (w_ref, land_ref) in enumerate(zip(w_refs, land_refs)):
        for j, to in enumerate(targets):
            cps.append(pltpu.make_async_remote_copy(src_ref=w_ref, dst_ref=land_ref.at[2 * x + y], send_sem=send_sems.at[4 * k + j],
                                                    recv_sem=recv_sems.at[4 * k + j], device_id=to, device_id_type=MESH))
    return cps


def _scatter_copies(p_refs, land_refs, send_sems, recv_sems):
    x, y, c, chips = _place()
    cps = []
    for k, (p_ref, land_ref) in enumerate(zip(p_refs, land_refs)):
        for j, (px, py) in enumerate(chips):
            cps.append(pltpu.make_async_remote_copy(src_ref=p_ref.at[2 * px + py], dst_ref=land_ref.at[j], send_sem=send_sems.at[3 * k + j],
                                                    recv_sem=recv_sems.at[3 * k + j], device_id=(px, py, c), device_id_type=MESH))
    return cps


def copies_start(srcs, land_shapes, make_copies, per_src, name):
    n = len(srcs)
    m = per_src * n

    def body(*refs):
        src_refs, land_refs = refs[:n], refs[n:2 * n]
        send_sems, recv_sems, token = refs[2 * n], refs[2 * n + 1], refs[-1]
        for cp in make_copies(src_refs, land_refs, send_sems, recv_sems):
            cp.start()
        token[...] = jnp.zeros_like(token)

    lands = [lax.empty(shp, s.dtype) for shp, s in zip(land_shapes, srcs)]
    res = pl.pallas_call(
        body, name=name,
        out_shape=(pltpu.SemaphoreType.DMA((m,)), pltpu.SemaphoreType.DMA((m,)), *[pltpu.HBM(s.shape, s.dtype) for s in srcs],
                   *[pltpu.HBM(shp, s.dtype) for shp, s in zip(land_shapes, srcs)], SDS((8, LANES), F32)),
        in_specs=[HBM_SPEC] * (2 * n),
        out_specs=(SEM_SPEC, SEM_SPEC, *[HBM_SPEC] * (2 * n), pl.BlockSpec(memory_space=pltpu.VMEM)),
        input_output_aliases={i: 2 + i for i in range(2 * n)},
        compiler_params=pltpu.CompilerParams(has_side_effects=DATAFLOW),
    )(*[_in_hbm(s) for s in srcs], *[_in_hbm(l) for l in lands])
    return res[0], res[1], list(res[2:2 + n]), list(res[2 + n:2 + 2 * n]), res[-1]


def copies_wait(send_sems, recv_sems, srcs, lands, make_copies, after, name):
    n = len(srcs)

    def body(*refs):
        src_refs, land_refs = refs[:n], refs[n:2 * n]
        for cp in make_copies(src_refs, land_refs, refs[2 * n], refs[2 * n + 1]):
            cp.wait_send()
            cp.wait_recv()

    res = pl.pallas_call(
        body, name=name,
        out_shape=(*[pltpu.HBM(s.shape, s.dtype) for s in srcs], *[pltpu.HBM(l.shape, l.dtype) for l in lands]),
        in_specs=[HBM_SPEC] * (2 * n) + [SEM_SPEC, SEM_SPEC, ANY],
        out_specs=tuple([HBM_SPEC] * (2 * n)),
        input_output_aliases={i: i for i in range(2 * n)},
        compiler_params=pltpu.CompilerParams(has_side_effects=DATAFLOW),
    )(*srcs, *lands, send_sems, recv_sems, after)
    return list(res[n:])


PACK = (("gdn_w_in", 2), ("gdn_w_out", 1), ("w_ffn_in", 2), ("w_ffn_out", 1), ("dsw_w_in", 2), ("dsw_w_out", 2))
PACK_ALIGN = 32


def _pack_rows(sizes):
    total = sum(sizes)
    rows = -(-total // D)
    return -(-rows // PACK_ALIGN) * PACK_ALIGN


def pack_blocks(blocks, dtype):
    flat = [b.astype(dtype).reshape(-1) for b in blocks]
    total = sum(f.shape[0] for f in flat)
    R = _pack_rows([f.shape[0] for f in flat])
    flat.append(jnp.zeros((R * D - total,), dtype))
    return jnp.concatenate(flat).reshape(R, D)


def unpack_blocks(buf, shapes):
    flat = buf.reshape(-1)
    out, off = [], 0
    for shp in shapes:
        n = int(np.prod(shp))
        out.append(flat[off:off + n].reshape(shp))
        off += n
    return out


def _shard_slice(a, axis, s):
    n = a.shape[axis] // N_SHARD
    return lax.slice_in_dim(a, s * n, (s + 1) * n, axis=axis)


def _pad_lanes(v):
    return jnp.concatenate([v.astype(F32), jnp.zeros((LANES - v.shape[0],), F32)])[None]


def kernel(x, c, w_ada, b_ada, norm_mix, norm_ffn, w_ffn_in, w_ffn_out, gdn_w_in, gdn_conv, gdn_a_log, gdn_dt_bias, gdn_out_norm, gdn_w_out, dsw_w_in, dsw_q_norm, dsw_k_norm, dsw_w_out, rel_bias, loss_target, m_w_ada, m_b_ada, m_norm_mix, m_norm_ffn, m_w_ffn_in, m_w_ffn_out, m_gdn_w_in, m_gdn_conv, m_gdn_a_log, m_gdn_dt_bias, m_gdn_out_norm, m_gdn_w_out, m_dsw_w_in, m_dsw_q_norm, m_dsw_k_norm, m_dsw_w_out, m_rel_bias, v_w_ada, v_b_ada, v_norm_mix, v_norm_ffn, v_w_ffn_in, v_w_ffn_out, v_gdn_w_in, v_gdn_conv, v_gdn_a_log, v_gdn_dt_bias, v_gdn_out_norm, v_gdn_w_out, v_dsw_w_in, v_dsw_q_norm, v_dsw_k_norm, v_dsw_w_out, v_rel_bias):
    S = x.shape[1]
    nt = S // WT
    xi, yi, ci = lax.axis_index("x"), lax.axis_index("y"), lax.axis_index("c")
    me = 4 * xi + 2 * yi + ci
    s_me = 2 * xi + yi
    x0, tgt = x[0], loss_target[0]
    shard = dict(w_ffn_in=w_ffn_in, w_ffn_out=w_ffn_out, gdn_w_in=gdn_w_in, gdn_w_out=gdn_w_out, dsw_w_in=dsw_w_in, dsw_w_out=dsw_w_out)

    whole = lambda a: Row(a, a.shape, lambda i: (0,) * a.ndim)
    (cond8,) = rowwise(lambda ids, v: (_silu(v),), [whole(c.reshape(8, LANES))], [], [Out((8, LANES), F32, (8, LANES), lambda i: (0, 0))], (1,), "cond")
    cond_all = all_gather_small(cond8, "gather_cond").reshape(N_DEV, D)
    cond16 = jnp.concatenate([cond_all, jnp.zeros((8, D), F32)], axis=0)
    ada_cols = w_ada.shape[2]
    mods = [matmul(cond16, w_ada[l], "nn", F32, f"ada_{l}")[:N_DEV] for l in range(2)]
    buf = jnp.concatenate([jnp.stack(mods, axis=1).reshape(-1, LANES), gdn_conv.reshape(-1, LANES)], axis=0)
    n_mod_rows = N_DEV * 2 * ada_cols // LANES
    got = all_gather_small(buf, "gather_mod").reshape(N_DEV, buf.shape[0], LANES)
    mod_parts, conv_parts = [], []
    for s in range(N_SHARD):
        from_dev = got[2 * s]
        mod_parts.append(lax.dynamic_index_in_dim(from_dev[:n_mod_rows].reshape(N_DEV, 2, ada_cols), me, 0, keepdims=False))
        conv_parts.append(from_dev[n_mod_rows:].reshape(4, -1))
    mod_nb = jnp.concatenate(mod_parts, axis=1)
    conv_w = jnp.concatenate(conv_parts, axis=1)
    (mod,) = rowwise(lambda ids, a, b: (a + b,), [whole(mod_nb), whole(b_ada)], [], [Out(mod_nb.shape, F32, mod_nb.shape, lambda i: (0, 0))], (1,), "mod_bias")
    mod = mod.reshape(2, 6, 1, D)
    sh1, sc1, g1, sh2, sc2, g2 = ([mod[l, k] for l in range(2)] for k in range(6))
    gmix = [norm_mix[l][None] for l in range(2)]
    gffn = [norm_ffn[l][None] for l in range(2)]

    gcols = gdn_w_in.shape[2]
    g_gdn_in, g_gdn_out = all_gather_shards([gdn_w_in[0].astype(BF16), gdn_w_out[0].astype(BF16)])
    gathered = lambda ws: [(N_SHARD,) + w.shape for w in ws]
    gate = (jnp.minimum(jnp.abs(g_gdn_in[0, 0, 0].astype(F32)), 0.0) + jnp.minimum(jnp.abs(mod[0, 0, 0, 0]), 0.0)).astype(BF16)
    w2 = [w_ffn_in[0].astype(BF16) + gate, w_ffn_out[0].astype(BF16) + gate]
    w3 = [dsw_w_in[0].astype(BF16) + gate, dsw_w_out[0].astype(BF16) + gate, w_ffn_in[1].astype(BF16) + gate, w_ffn_out[1].astype(BF16) + gate]
    fly2 = copies_start(w2, gathered(w2), _gather_copies, 4, "weights_ffn0_start")
    fly3 = copies_start(w3, gathered(w3), _gather_copies, 4, "weights_layer1_start")
    started = fly2[4][0, 0] + fly3[4][0, 0]
    w_gdn = jnp.concatenate([g_gdn_in[s] for s in range(N_SHARD)] + [jnp.zeros((D, GDN_PROJ - N_SHARD * gcols), BF16)], axis=1)
    alog, dtb = _pad_lanes(gdn_a_log[0]), _pad_lanes(gdn_dt_bias[0])
    qg2 = jnp.concatenate([dsw_q_norm, dsw_q_norm], axis=1)
    kg2 = jnp.concatenate([dsw_k_norm, dsw_k_norm], axis=1)
    w_gdn_out = g_gdn_out.reshape(GDN_H * LANES, D)
    gdn_args = (w_gdn, conv_w, alog, dtb, gdn_out_norm, w_gdn_out)
    sc1[0] = sc1[0] + started

    (h10,) = rowwise(f_norm_only, [_wide(x0)], [gmix[0], sc1[0], sh1[0]], [_wide_out(S, BF16)], (nt,), "l0_norm")
    y0, sv_g = gdn_forward(h10, *gdn_args)
    x1, h20 = rowwise(f_resid_norm, [_wide(x0), _wide(y0)], [g1[0], gffn[0], sc2[0], sh2[0]], [_wide_out(S, F32), _wide_out(S, BF16)], (nt,), "l0_mid")
    g_in0, g_out0 = copies_wait(*fly2[:4], _gather_copies, y0, "weights_ffn0_wait")
    w_ffn = [(g_in0, g_out0.reshape(FFN, D)), None]
    f0, sv_f0 = ffn_forward(h20, *w_ffn[0], "0")
    x2, h11 = rowwise(f_resid_norm, [_wide(x1), _wide(f0)], [g2[0], gmix[1], sc1[1], sh1[1]], [_wide_out(S, F32), _wide_out(S, BF16)], (nt,), "l1_in")
    g_dsw_in, g_dsw_out, g_in1, g_out1 = copies_wait(*fly3[:4], _gather_copies, f0, "weights_layer1_wait")
    w_ffn[1] = (g_in1, g_out1.reshape(FFN, D))
    dsw_args = (g_dsw_in, qg2, kg2)
    y1, sv_d = dsw_forward(h11, *dsw_args, rel_bias, g_dsw_out)
    x3, h21 = rowwise(f_resid_norm, [_wide(x2), _wide(y1)], [g1[1], gffn[1], sc2[1], sh2[1]], [_wide_out(S, F32), _wide_out(S, BF16)], (nt,), "l1_mid")
    f1, sv_f1 = ffn_forward(h21, *w_ffn[1], "1")
    part_spec = lambda a: Row(a, (None, 1, D), lambda i: (i, 0, 0))
    (parts,) = rowwise(f_loss, [_wide(x3), _wide(f1), _wide(tgt)], [g2[1]], [Out((nt, 1, D), F32, (None, 1, D), lambda i: (i, 0, 0))], (nt,), "loss")
    loss = lax.psum(jnp.sum(parts), ("x", "y", "c"))

    (dx3, df1), (dg2_1,) = rowwise_bwd(f_loss, [_wide(x3), _wide(f1, gdtype=BF16), _wide(tgt, diff=False)], [g2[1]],
                                       [part_spec(jnp.ones((nt, 1, D), F32))], (nt,), "loss_bwd")
    dh21, d_win1, d_wout1 = ffn_backward(df1, sv_f1, *w_ffn[1], "1")
    (dx2, dy1), (dg1_1, dgf1, dsc2_1, dsh2_1) = rowwise_bwd(
        f_resid_norm, [_wide(x2), _wide(y1, gdtype=BF16)], [g1[1], gffn[1], sc2[1], sh2[1]], [_wide(dx3), _wide(dh21)], (nt,), "l1_mid_bwd")
    dh11, g_d = dsw_backward(dy1, sv_d, *dsw_args, g_dsw_out)
    (dx1, df0), (dg2_0, dgm1, dsc1_1, dsh1_1) = rowwise_bwd(
        f_resid_norm, [_wide(x1), _wide(f0, gdtype=BF16)], [g2[0], gmix[1], sc1[1], sh1[1]], [_wide(dx2), _wide(dh11)], (nt,), "l1_in_bwd")
    by_shard = lambda a: a.reshape(N_SHARD, a.shape[0] // N_SHARD, a.shape[1])
    landing = lambda ps: [(3,) + p.shape[1:] for p in ps]
    dws3 = [g_d["w_in"], g_d["w_out"], d_win1, by_shard(d_wout1)]
    parts3 = [a.astype(BF16) for a in dws3]
    gfly3 = copies_start(parts3, landing(parts3), _scatter_copies, 3, "grads_layer1_start")
    w_out0 = w_ffn[0][1] + gfly3[4][0, 0].astype(BF16)
    dh20, d_win0, d_wout0 = ffn_backward(df0, sv_f0, w_ffn[0][0], w_out0, "0")
    (dx0p, dy0), (dg1_0, dgf0, dsc2_0, dsh2_0) = rowwise_bwd(
        f_resid_norm, [_wide(x0), _wide(y0, gdtype=BF16)], [g1[0], gffn[0], sc2[0], sh2[0]], [_wide(dx1), _wide(dh20)], (nt,), "l0_mid_bwd")
    dws2 = [d_win0, by_shard(d_wout0)]
    parts2 = [a.astype(BF16) for a in dws2]
    gfly2 = copies_start(parts2, landing(parts2), _scatter_copies, 3, "grads_ffn0_start")
    gdn_args = gdn_args[:5] + (w_gdn_out + gfly2[4][0, 0].astype(BF16),)
    dh10, g_g = gdn_backward(dy0, sv_g, *gdn_args)
    (grad_x,), (dgm0, dsc1_0, dsh1_0) = rowwise_bwd(f_first, [_wide(x0)], [gmix[0], sc1[0], sh1[0]], [_wide(dx0p), _wide(dh10)], (nt,), "l0_norm_bwd")

    dmod = jnp.concatenate([dsh1_0, dsc1_0, dg1_0, dsh2_0, dsc2_0, dg2_0, dsh1_1, dsc1_1, dg1_1, dsh2_1, dsc2_1, dg2_1], axis=1)
    d_rel = jnp.transpose(g_d["rel"][:, :, 0])
    fold = lambda v: v[:, :DSW_DH] + v[:, DSW_DH:]
    small = [dmod, jnp.concatenate([dgm0, dgm1], axis=1), jnp.concatenate([dgf0, dgf1], axis=1), g_g["conv"].reshape(1, -1),
             g_g["alog"], g_g["dtb"], g_g["gain"], _pad_lanes(fold(g_d["q_gain2"])[0]), _pad_lanes(fold(g_d["k_gain2"])[0]),
             d_rel.reshape(1, -1)]
    used = [v.shape[1] // LANES for v in small]
    sizes = [-(-u // 8) * 8 for u in used]
    pad8 = lambda v, u, s: jnp.concatenate([v.reshape(u, LANES), jnp.zeros((s - u, LANES), F32)], axis=0) if s > u else v.reshape(u, LANES)
    pad_rows = sum(sizes)
    sbuf = jnp.concatenate([pad8(v, u, s) for v, u, s in zip(small, used, sizes)], axis=0)
    sgot = all_gather_small(sbuf, "gather_small_grads")
    ssum = add_rows([sgot[d * pad_rows:(d + 1) * pad_rows] for d in range(N_DEV)], F32, "sum_small_grads", rt=pad_rows)
    offs = np.cumsum([0] + sizes)
    take = lambda k: ssum[offs[k]:offs[k] + used[k]].reshape(1, -1)
    grad_b_ada = take(0).reshape(2, 6 * D)
    grad_norm_mix = take(1).reshape(2, D)
    grad_norm_ffn = take(2).reshape(2, D)
    conv_full = take(3).reshape(4, -1)
    ncv = gdn_conv.shape[2]
    grad_gdn_conv = lax.dynamic_slice_in_dim(conv_full, s_me * ncv, ncv, axis=1)[None]
    grad_a_log = take(4)[:, :GDN_H]
    grad_dt_bias = take(5)[:, :GDN_H]
    grad_out_norm = take(6)
    grad_q_norm = take(7)[:, :DSW_DH]
    grad_k_norm = take(8)[:, :DSW_DH]
    grad_rel = take(9).reshape(REL_BUCKETS, 3 * GDN_H)
    dmod_all = sgot.reshape(N_DEV, pad_rows, LANES)[:, :used[0]].reshape(N_DEV, 2, 6 * D)
    dmod_mine = lax.dynamic_slice_in_dim(dmod_all, s_me * ada_cols, ada_cols, axis=2)
    dmod16 = jnp.concatenate([dmod_mine, jnp.zeros_like(dmod_mine)], axis=0)
    grad_w_ada = jnp.stack([matmul(cond16, dmod16[:, l], "tn", F32, f"ada_dw_{l}") for l in range(2)])

    dg_in = jnp.stack([g_g["w_in"][:, s * gcols:(s + 1) * gcols] for s in range(N_SHARD)])
    dws = [dg_in, by_shard(g_g["w_out"])]
    keeps, gives = [], []
    for a in dws:
        rh = a.shape[1] // 2
        keeps.append(lax.dynamic_slice_in_dim(a, ci * rh, rh, axis=1))
        gives.append(lax.dynamic_slice_in_dim(a, (1 - ci) * rh, rh, axis=1).astype(BF16))
    from_sib = sibling_exchange(gives, "grads_to_sibling")
    flat2 = lambda a: a.reshape(-1, a.shape[-1])
    parts = [add_rows([flat2(k_), flat2(f_)], BF16, f"grads_chip_sum_{i}").reshape(k_.shape) for i, (k_, f_) in enumerate(zip(keeps, from_sib))]
    others = scatter_to_chips(parts)
    halves = []
    for i, (p_, o_) in enumerate(zip(parts, others)):
        own = lax.dynamic_index_in_dim(p_, s_me, 0, keepdims=False)
        halves.append(add_rows([own, o_[0], o_[1], o_[2]], F32, f"grads_sum_{i}"))
    sib_halves = sibling_exchange(halves, "grads_from_sibling")

    def whole_shard(mine, theirs):
        both = jnp.stack([mine, theirs])
        return jnp.concatenate([lax.dynamic_index_in_dim(both, ci, 0, keepdims=False),
                                lax.dynamic_index_in_dim(both, 1 - ci, 0, keepdims=False)], axis=0)

    s_gdn_in, s_gdn_out = [whole_shard(a, b) for a, b in zip(halves, sib_halves)]
    got3 = copies_wait(*gfly3[:4], _scatter_copies, grad_x, "grads_layer1_wait")
    got2 = copies_wait(*gfly2[:4], _scatter_copies, grad_x, "grads_ffn0_wait")
    core_sums = []
    for i, (full, got) in enumerate(zip(dws3 + dws2, got3 + got2)):
        own = lax.dynamic_index_in_dim(full, s_me, 0, keepdims=False)
        core_sums.append(add_rows([own, got[0], got[1], got[2]], F32, f"grads_core_sum_{i}"))
    sib_sums = sibling_exchange(core_sums, "grads_core_sums_swap")
    s_dsw_in, s_dsw_out, s_in1, s_out1, s_in0, s_out0 = [add_rows([a, b], F32, f"grads_chip_total_{i}")
                                                          for i, (a, b) in enumerate(zip(core_sums, sib_sums))]
    gsh = dict(gdn_w_in=s_gdn_in[None], gdn_w_out=s_gdn_out[None],
               w_ffn_in=jnp.stack([s_in0, s_in1]), w_ffn_out=jnp.stack([s_out0, s_out1]),
               dsw_w_in=s_dsw_in[None], dsw_w_out=s_dsw_out[None])

    grads = dict(w_ada=grad_w_ada, b_ada=grad_b_ada, norm_mix=grad_norm_mix, norm_ffn=grad_norm_ffn, w_ffn_in=gsh["w_ffn_in"],
                 w_ffn_out=gsh["w_ffn_out"], gdn_w_in=gsh["gdn_w_in"], gdn_conv=grad_gdn_conv, gdn_a_log=grad_a_log,
                 gdn_dt_bias=grad_dt_bias, gdn_out_norm=grad_out_norm, gdn_w_out=gsh["gdn_w_out"], dsw_w_in=gsh["dsw_w_in"],
                 dsw_q_norm=grad_q_norm, dsw_k_norm=grad_k_norm, dsw_w_out=gsh["dsw_w_out"], rel_bias=grad_rel)
    weights = dict(w_ada=w_ada, b_ada=b_ada, norm_mix=norm_mix, norm_ffn=norm_ffn, w_ffn_in=w_ffn_in, w_ffn_out=w_ffn_out,
                   gdn_w_in=gdn_w_in, gdn_conv=gdn_conv, gdn_a_log=gdn_a_log, gdn_dt_bias=gdn_dt_bias, gdn_out_norm=gdn_out_norm,
                   gdn_w_out=gdn_w_out, dsw_w_in=dsw_w_in, dsw_q_norm=dsw_q_norm, dsw_k_norm=dsw_k_norm, dsw_w_out=dsw_w_out,
                   rel_bias=rel_bias)
    ms = dict(w_ada=m_w_ada, b_ada=m_b_ada, norm_mix=m_norm_mix, norm_ffn=m_norm_ffn, w_ffn_in=m_w_ffn_in, w_ffn_out=m_w_ffn_out,
              gdn_w_in=m_gdn_w_in, gdn_conv=m_gdn_conv, gdn_a_log=m_gdn_a_log, gdn_dt_bias=m_gdn_dt_bias, gdn_out_norm=m_gdn_out_norm,
              gdn_w_out=m_gdn_w_out, dsw_w_in=m_dsw_w_in, dsw_q_norm=m_dsw_q_norm, dsw_k_norm=m_dsw_k_norm, dsw_w_out=m_dsw_w_out,
              rel_bias=m_rel_bias)
    vs = dict(w_ada=v_w_ada, b_ada=v_b_ada, norm_mix=v_norm_mix, norm_ffn=v_norm_ffn, w_ffn_in=v_w_ffn_in, w_ffn_out=v_w_ffn_out,
              gdn_w_in=v_gdn_w_in, gdn_conv=v_gdn_conv, gdn_a_log=v_gdn_a_log, gdn_dt_bias=v_gdn_dt_bias, gdn_out_norm=v_gdn_out_norm,
              gdn_w_out=v_gdn_w_out, dsw_w_in=v_dsw_w_in, dsw_q_norm=v_dsw_q_norm, dsw_k_norm=v_dsw_k_norm, dsw_w_out=v_dsw_w_out,
              rel_bias=v_rel_bias)
    names = list(weights)
    deltas, new_m, new_v = [], [], []
    for n in names:
        g = grads[n].reshape(weights[n].shape)
        grads[n] = g
        d, nm, nv = adamw(weights[n], g, ms[n], vs[n], f"adamw_{n}")
        deltas.append(d)
        new_m.append(nm)
        new_v.append(nv)
    return (loss, grad_x[None], *[grads[n] for n in names], *deltas, *new_m, *new_v)
```

```python
import functools
import math

import numpy as np
import jax
import jax.numpy as jnp
from jax import lax
from jax.experimental import pallas as pl
from jax.experimental.pallas import tpu as pltpu

F32 = jnp.float32
BF16 = jnp.bfloat16
SDS = jax.ShapeDtypeStruct
MESH = pl.DeviceIdType.MESH
ANY = pl.BlockSpec(memory_space=pl.ANY)

D = 1024
EPS = 1e-6
LANES = 128
GDN_H = 8
GDN_DK = 128
GDN_C = 64
DSW_GROUPS = ((128, 1), (512, 4), (2048, 16))
DSW_SPAN = 128
DSW_DH = 64
DSW_HG = 512
REL_BUCKETS = 32
REL_MAX_DIST = 2048
FFN = 2816
N_SHARD = 4
N_DEV = 8
VMEM_LIMIT = 48 * 1024 * 1024
NEG = -1e30

ADAM_LR, ADAM_B1, ADAM_B2, ADAM_EPS, ADAM_WD, ADAM_STEP = 0.001, 0.9, 0.999, 1e-08, 0.01, 10


def _cp(n_axes):
    return pltpu.CompilerParams(dimension_semantics=("arbitrary",) * n_axes, vmem_limit_bytes=VMEM_LIMIT)


def _blk(dim, cap):
    if dim <= cap:
        return dim
    best = None
    for b in range(LANES, cap + 1, LANES):
        if dim % b == 0:
            best = b
    assert best is not None, (dim, cap)
    return best


MAX_SHARD_BLOCK = 1408
def matmul(a, b, mode, out_dtype, name, cap_m=MAX_SHARD_BLOCK, cap_n=MAX_SHARD_BLOCK, cap_k=2048, col_shards=0):
    ns = col_shards
    if mode == "nn":
        (M, K) = a.shape
        K2, N = (b.shape[1], ns * b.shape[2]) if ns else b.shape
    elif mode == "nt":
        (M, K) = a.shape
        N, K2 = (b.shape[1], ns * b.shape[2]) if ns else b.shape
    else:
        (K, M), (K2, N) = a.shape, b.shape
    assert K == K2, (a.shape, b.shape, mode)
    if K <= 3072:
        cap_k = K
        if K > 2048:
            cap_n = 1024
    n_unit = N // ns if (ns and mode != "nt") else N
    k_unit = K // ns if (ns and mode == "nt") else K
    bm = _blk(M, cap_m)
    bn = _blk(n_unit, MAX_SHARD_BLOCK) if n_unit != N else _blk(N, cap_n)
    if k_unit != K:
        bk = _blk(k_unit, MAX_SHARD_BLOCK)
    else:
        bk = _blk(K, 1024 if (ns and mode == "tn") else cap_k)
    nk = K // bk
    nps, kps = n_unit // bn, k_unit // bk
    dims = {"nn": ((1,), (0,)), "nt": ((1,), (1,)), "tn": ((0,), (0,))}[mode]

    def dot(a_ref, b_ref):
        return lax.dot_general(a_ref[...].astype(BF16), b_ref[...].astype(BF16), (dims, ((), ())), preferred_element_type=F32)

    def body_one(a_ref, b_ref, o_ref):
        o_ref[...] = dot(a_ref, b_ref).astype(o_ref.dtype)

    def body_acc(a_ref, b_ref, o_ref, acc_ref):
        k = pl.program_id(2)

        @pl.when(k == 0)
        def _():
            acc_ref[...] = jnp.zeros_like(acc_ref)

        acc_ref[...] += dot(a_ref, b_ref)

        @pl.when(k == nk - 1)
        def _():
            o_ref[...] = acc_ref[...].astype(o_ref.dtype)

    a_spec = pl.BlockSpec((bk, bm), lambda i, j, k: (k, i)) if mode == "tn" else pl.BlockSpec((bm, bk), lambda i, j, k: (i, k))
    if mode == "nt":
        b_spec = pl.BlockSpec((None, bn, bk), lambda i, j, k: (k // kps, j, k % kps)) if ns else pl.BlockSpec((bn, bk), lambda i, j, k: (j, k))
    elif mode == "nn" and ns:
        b_spec = pl.BlockSpec((None, bk, bn), lambda i, j, k: (j // nps, k, j % nps))
    else:
        b_spec = pl.BlockSpec((bk, bn), lambda i, j, k: (k, j))
    if mode == "tn" and ns:
        o_spec, o_shape = pl.BlockSpec((None, bm, bn), lambda i, j, k: (j // nps, i, j % nps)), (ns, M, n_unit)
    else:
        o_spec, o_shape = pl.BlockSpec((bm, bn), lambda i, j, k: (i, j)), (M, N)
    return pl.pallas_call(
        body_one if nk == 1 else body_acc, name=name, grid=(M // bm, N // bn, nk),
        in_specs=[a_spec, b_spec], out_specs=o_spec,
        out_shape=SDS(o_shape, out_dtype), scratch_shapes=[] if nk == 1 else [pltpu.VMEM((bm, bn), F32)],
        compiler_params=_cp(3),
    )(a, b)


class Row:
    def __init__(self, arr, bshape, imap, splits=None, diff=True, acc=False, gdtype=F32, gshape=None, gbshape=None, gimap=None,
                 lead=0):
        self.arr, self.bshape, self.imap = arr, tuple(bshape), imap
        self.splits, self.lead = splits, lead
        self.diff, self.acc, self.gdtype = diff, acc, gdtype
        self.gshape = tuple(arr.shape) if gshape is None else tuple(gshape)
        self.gbshape = self.bshape if gbshape is None else tuple(gbshape)
        self.gimap = imap if gimap is None else gimap

    def gspec(self):
        return pl.BlockSpec(self.gbshape, self.gimap)

    def spec(self):
        return pl.BlockSpec(self.bshape, self.imap)

    def pieces(self, ref):
        return _load_pieces(ref, self.splits, self.lead)

    def n_pieces(self):
        return _n_pieces(self.splits, self.lead)


class Out:
    def __init__(self, shape, dtype, bshape, imap, splits=None, lead=0):
        self.shape, self.dtype, self.bshape, self.imap = tuple(shape), dtype, tuple(bshape), imap
        self.splits, self.lead = splits, lead

    def n_pieces(self):
        return _n_pieces(self.splits, self.lead)


def _n_pieces(splits, lead):
    return lead if lead else (1 if splits is None else len(splits))


def _load_pieces(ref, splits, lead):
    if lead:
        return [ref[k].astype(F32) for k in range(lead)]
    if splits is None:
        return [ref[...].astype(F32)]
    out, o = [], 0
    for w in splits:
        out.append(ref[..., o:o + w].astype(F32))
        o += w
    return out


def _store_pieces(ref, splits, lead, vals, accumulate=False):
    def put(idx, v):
        if accumulate:
            ref[idx] += v.astype(ref.dtype)
        else:
            ref[idx] = v.astype(ref.dtype)

    if lead:
        for k in range(lead):
            put(k, vals[k])
    elif splits is None:
        put(..., vals[0])
    else:
        o = 0
        for w, v in zip(splits, vals):
            put((..., slice(o, o + w)), v)
            o += w


def rowwise(fn, rows, params, outs, grid, name):
    nr, npar = len(rows), len(params)

    def body(*refs):
        ids = tuple(pl.program_id(a) for a in range(len(grid)))
        vals = []
        for r, ref in zip(rows, refs[:nr]):
            vals += r.pieces(ref)
        pvals = [ref[...].astype(F32) for ref in refs[nr:nr + npar]]
        res = list(fn(ids, *vals, *pvals))
        o = 0
        for spec, ref in zip(outs, refs[nr + npar:]):
            n = spec.n_pieces()
            _store_pieces(ref, spec.splits, spec.lead, res[o:o + n])
            o += n

    nz = len(grid)
    pspecs = [pl.BlockSpec(p.shape, (lambda *ids, _n=p.ndim: (0,) * _n)) for p in params]
    res = pl.pallas_call(
        body, name=name, grid=grid,
        in_specs=[r.spec() for r in rows] + pspecs,
        out_specs=[pl.BlockSpec(o.bshape, o.imap) for o in outs],
        out_shape=[SDS(o.shape, o.dtype) for o in outs],
        compiler_params=_cp(nz),
    )(*[r.arr for r in rows], *params)
    return list(res)


def rowwise_bwd(fn, rows, params, cots, grid, name):
    nr, npar, nc = len(rows), len(params), len(cots)
    drows = [r for r in rows if r.diff]
    nz = len(grid)

    def body(*refs):
        ids = tuple(pl.program_id(a) for a in range(nz))
        row_refs, par_refs = refs[:nr], refs[nr:nr + npar]
        cot_refs = refs[nr + npar:nr + npar + nc]
        drow_refs = refs[nr + npar + nc:nr + npar + nc + len(drows)]
        dpar_refs = refs[nr + npar + nc + len(drows):]
        pieces, is_diff = [], []
        for r, ref in zip(rows, row_refs):
            ps = r.pieces(ref)
            pieces += ps
            is_diff += [r.diff] * len(ps)
        pvals = [ref[...].astype(F32) for ref in par_refs]
        dvals = [p for p, dflag in zip(pieces, is_diff) if dflag]
        nd = len(dvals)

        def f(*args):
            it = iter(args[:nd])
            full = [next(it) if dflag else p for p, dflag in zip(pieces, is_diff)]
            return tuple(fn(ids, *full, *args[nd:]))

        _, vjp = jax.vjp(f, *dvals, *pvals)
        cvals = []
        for c, ref in zip(cots, cot_refs):
            cvals += c.pieces(ref)
        g = vjp(tuple(cvals))
        o = 0
        first_inner = ids[-1] == 0
        for r, ref in zip(drows, drow_refs):
            n = r.n_pieces()
            gs = g[o:o + n]
            o += n
            if r.acc:
                @pl.when(first_inner)
                def _(ref=ref):
                    ref[...] = jnp.zeros_like(ref)
            _store_pieces(ref, r.splits, r.lead, gs, accumulate=r.acc)
        first = functools.reduce(jnp.logical_and, [i == 0 for i in ids])
        for ref, gp in zip(dpar_refs, g[nd:]):
            @pl.when(first)
            def _(ref=ref):
                ref[...] = jnp.zeros_like(ref)
            ref[...] += gp

    pspecs = [pl.BlockSpec(p.shape, (lambda *ids, _n=p.ndim: (0,) * _n)) for p in params]
    res = pl.pallas_call(
        body, name=name, grid=grid,
        in_specs=[r.spec() for r in rows] + pspecs + [c.spec() for c in cots],
        out_specs=[r.gspec() for r in drows] + pspecs,
        out_shape=[SDS(r.gshape, r.gdtype) for r in drows] + [SDS(p.shape, F32) for p in params],
        compiler_params=_cp(nz),
    )(*[r.arr for r in rows], *params, *[c.arr for c in cots])
    res = list(res)
    return res[:len(drows)], res[len(drows):]


def _sigmoid(x):
    return 0.5 * (jnp.tanh(0.5 * x) + 1.0)


def _silu(x):
    return x * _sigmoid(x)


def _normmod(x, gain, sc, sh):
    inv = lax.rsqrt(jnp.mean(x * x, axis=-1, keepdims=True) + EPS)
    return x * inv * gain * (1.0 + sc) + sh


def f_first(ids, x, gain, sc, sh):
    return x, _normmod(x, gain, sc, sh)


def f_resid_norm(ids, x, y, g, gain, sc, sh):
    xn = x + g * y
    return xn, _normmod(xn, gain, sc, sh)


@jax.custom_vjp
def _swiglu(gate, up):
    return _silu(gate) * up


def _swiglu_fwd(gate, up):
    return _silu(gate) * up, (gate, up)


def _swiglu_bwd(res, da):
    gate, up = res
    s = _sigmoid(gate)
    gs = gate * s
    return da * up * (s + gs * (1.0 - s)), da * gs


_swiglu.defvjp(_swiglu_fwd, _swiglu_bwd)


def f_swiglu(ids, gate, up):
    return (_swiglu(gate, up),)


def f_loss(ids, x, y, tgt, g):
    out = x + g * y
    e = out - tgt
    part = 0.5 * jnp.sum(e * e, axis=0, keepdims=True) * (1.0 / D)
    return (part,)


def _softplus(x):
    return jnp.maximum(x, 0.0) + jnp.log(1.0 + jnp.exp(-jnp.abs(x)))


def _chunk_tril(T):
    r = lax.broadcasted_iota(jnp.int32, (T, T), 0)
    c = lax.broadcasted_iota(jnp.int32, (T, T), 1)
    return jnp.where((r // GDN_C == c // GDN_C) & (c <= r), 1.0, 0.0).astype(F32)


def _dot_hi(a, b, dims=((1,), (0,))):
    return lax.dot_general(a, b, (dims, ((), ())), precision=lax.Precision.HIGHEST, preferred_element_type=F32)


def _dot_x3(a, b, dims=((1,), (0,))):
    return lax.dot_general(a, b, (dims, ((), ())), precision=lax.Precision.HIGH, preferred_element_type=F32)


def f_gdn_gates(ids, ab, alog, dtb):
    T = ab.shape[0]
    g = -jnp.exp(alog) * _softplus(ab + dtb)
    beta = _sigmoid(ab)
    gcum = _dot_x3(_chunk_tril(T), g)
    row = lax.broadcasted_iota(jnp.int32, (LANES, LANES), 0)
    sel = lambda k: jnp.where(row == k, 1.0, 0.0).astype(F32)
    gcs = [_dot_x3(gcum, sel(h)) for h in range(GDN_H)]
    bts = [_dot_x3(beta, sel(GDN_H + h)) for h in range(GDN_H)]
    return (*gcs, *bts)


def f_gdn_post(ids, *args):
    os_, zs, gain = args[:GDN_H], args[GDN_H:2 * GDN_H], args[2 * GDN_H]
    out = []
    for o, z in zip(os_, zs):
        inv = lax.rsqrt(jnp.mean(o * o, axis=-1, keepdims=True) + EPS)
        out.append(o * inv * gain * _silu(z))
    return tuple(out)


def _qknorm1(x, gain2, scale):
    lane = lax.broadcasted_iota(jnp.int32, x.shape, 1)
    lo = lane < DSW_DH
    x2 = x * x
    s_all = jnp.sum(x2, axis=-1, keepdims=True)
    s_lo = jnp.sum(jnp.where(lo, x2, 0.0), axis=-1, keepdims=True)
    ms = jnp.where(lo, s_lo, s_all - s_lo) * (1.0 / DSW_DH)
    return x * lax.rsqrt(ms + EPS) * (gain2 * scale)


def f_qknorm(ids, *args):
    return tuple(_qknorm1(x, args[-1], 1.0) for x in args[:-1])


def f_qnorm(ids, *args):
    return tuple(_qknorm1(x, args[-1], DSW_DH ** -0.5) for x in args[:-1])


def f_combine(ids, o0, o1, o2, l0, l1, l2):
    m = jnp.maximum(jnp.maximum(l0, l1), l2)
    e0, e1, e2 = jnp.exp(l0 - m), jnp.exp(l1 - m), jnp.exp(l2 - m)
    den = e0 + e1 + e2
    o = (e0 * o0 + e1 * o1 + e2 * o2) / den
    return o, m + jnp.log(den)


GDN_T = 512
HALO = 16


def _conv_pre(xx, w):
    acc = xx * w[3:4, :]
    for j in range(3):
        acc = acc + pltpu.roll(xx, shift=3 - j, axis=0) * w[j:j + 1, :]
    return acc


@jax.custom_vjp
def _qkv_act_core(pre, norm_on, scale):
    s = _silu(pre)
    r = lax.rsqrt(jnp.sum(s * s, axis=-1, keepdims=True) + EPS)
    return jnp.where(norm_on > 0.5, s * r * scale, s)


def _qkv_act_fwd(pre, norm_on, scale):
    return _qkv_act_core(pre, norm_on, scale), (pre, norm_on, scale)


def _qkv_act_bwd(res, dout):
    pre, norm_on, scale = res
    sig = _sigmoid(pre)
    s = pre * sig
    r = lax.rsqrt(jnp.sum(s * s, axis=-1, keepdims=True) + EPS)
    unit = s * r
    dn = dout * scale
    ds = jnp.where(norm_on > 0.5, r * (dn - unit * jnp.sum(dn * unit, axis=-1, keepdims=True)), dout)
    return ds * (sig + s * (1.0 - sig)), jnp.zeros_like(norm_on), jnp.zeros_like(scale)


_qkv_act_core.defvjp(_qkv_act_fwd, _qkv_act_bwd)


def _qkv_act(pre, cidx):
    norm_on = jnp.where(cidx < 2 * GDN_H, 1.0, 0.0).astype(F32)
    scale = jnp.where(cidx < GDN_H, GDN_DK ** -0.5, 1.0).astype(F32)
    return _qkv_act_core(pre, norm_on, scale)


def gdn_pre(proj, conv_w, S):
    nt = S // GDN_T
    hb = GDN_T // HALO

    def body(prev_ref, cur_ref, w_ref, o_ref):
        p, i = pl.program_id(0), pl.program_id(1)
        for h in range(GDN_H):
            cols = slice(LANES * h, LANES * (h + 1))
            prev = jnp.where(i > 0, prev_ref[:, cols].astype(F32), 0.0)
            xx = jnp.concatenate([prev, cur_ref[:, cols].astype(F32)], axis=0)
            pre = _conv_pre(xx, w_ref[:, cols])[HALO:]
            o_ref[h] = _qkv_act(pre, p * GDN_H + h)

    hv = GDN_H * LANES
    return pl.pallas_call(
        body, name="gdn_pre", grid=(3, nt),
        in_specs=[pl.BlockSpec((HALO, hv), lambda p, i: (jnp.maximum(i * hb - 1, 0), p)),
                  pl.BlockSpec((GDN_T, hv), lambda p, i: (i, p)),
                  pl.BlockSpec((4, hv), lambda p, i: (0, p))],
        out_specs=pl.BlockSpec((None, GDN_H, GDN_T, LANES), lambda p, i: (p, 0, i, 0)),
        out_shape=SDS((3, GDN_H, S, LANES), F32),
        compiler_params=_cp(2),
    )(proj, proj, conv_w)


def gdn_pre_bwd(proj, conv_w, dqkv, S):
    nt = S // GDN_T
    hb = GDN_T // HALO
    last_h = S // HALO - 1

    def body(prev_ref, cur_ref, next_ref, w_ref, d_ref, dnext_ref, dx_ref, dw_ref):
        p, i = pl.program_id(0), pl.program_id(1)

        @pl.when(i == 0)
        def _():
            dw_ref[...] = jnp.zeros_like(dw_ref)

        for h in range(GDN_H):
            cols = slice(LANES * h, LANES * (h + 1))
            w = w_ref[:, cols]
            prev = jnp.where(i > 0, prev_ref[:, cols].astype(F32), 0.0)
            xx = jnp.concatenate([prev, cur_ref[:, cols].astype(F32), next_ref[:, cols].astype(F32)], axis=0)
            dnext = jnp.where(i < nt - 1, dnext_ref[h], 0.0)
            dd = jnp.concatenate([jnp.zeros((HALO, LANES), F32), d_ref[h], dnext], axis=0)
            pre = _conv_pre(xx, w)
            _, vjp = jax.vjp(lambda v, _c=p * GDN_H + h: _qkv_act(v, _c), pre)
            (dpre,) = vjp(dd)
            dx = dpre * w[3:4, :]
            R = dpre.shape[0]
            for j in range(3):
                dx = dx + pltpu.roll(dpre, shift=R - (3 - j), axis=0) * w[j:j + 1, :]
            dx_ref[:, cols] = dx[HALO:HALO + GDN_T].astype(dx_ref.dtype)
            own = HALO + GDN_T
            rows_w = [jnp.sum((dpre * pltpu.roll(xx, shift=3 - j, axis=0))[:own], axis=0, keepdims=True) for j in range(3)]
            rows_w.append(jnp.sum((dpre * xx)[:own], axis=0, keepdims=True))
            r4 = lax.broadcasted_iota(jnp.int32, (4, LANES), 0)
            dw = jnp.zeros((4, LANES), F32)
            for j in range(4):
                dw = dw + jnp.where(r4 == j, rows_w[j], 0.0)
            dw_ref[:, cols] += dw

    hv = GDN_H * LANES
    return pl.pallas_call(
        body, name="gdn_pre_bwd", grid=(3, nt),
        in_specs=[pl.BlockSpec((HALO, hv), lambda p, i: (jnp.maximum(i * hb - 1, 0), p)),
                  pl.BlockSpec((GDN_T, hv), lambda p, i: (i, p)),
                  pl.BlockSpec((HALO, hv), lambda p, i: (jnp.minimum((i + 1) * hb, last_h), p)),
                  pl.BlockSpec((4, hv), lambda p, i: (0, p)),
                  pl.BlockSpec((None, GDN_H, GDN_T, LANES), lambda p, i: (p, 0, i, 0)),
                  pl.BlockSpec((None, GDN_H, HALO, LANES), lambda p, i: (p, 0, jnp.minimum((i + 1) * hb, last_h), 0))],
        out_specs=[pl.BlockSpec((GDN_T, hv), lambda p, i: (i, p)),
                   pl.BlockSpec((4, hv), lambda p, i: (0, p))],
        out_shape=[SDS((S, 3 * hv), BF16), SDS((4, 3 * hv), F32)],
        compiler_params=_cp(2),
    )(proj, proj, proj, conv_w, dqkv, dqkv)


_DIMS = {"nn": ((1,), (0,)), "nt": ((1,), (1,)), "tn": ((0,), (0,))}


def _mm_raw(a, b, mode, hi):
    if hi:
        return _dot_hi(a, b, _DIMS[mode])
    return lax.dot_general(a.astype(BF16), b.astype(BF16), (_DIMS[mode], ((), ())), preferred_element_type=F32)


@functools.partial(jax.custom_vjp, nondiff_argnums=(2, 3))
def mm(a, b, mode, hi):
    return _mm_raw(a, b, mode, hi)


def _mm_fwd(a, b, mode, hi):
    return _mm_raw(a, b, mode, hi), (a, b)


def _mm_bwd(mode, hi, res, dc):
    a, b = res
    if mode == "nn":
        da, db = mm(dc, b, "nt", hi), mm(a, dc, "tn", hi)
    elif mode == "nt":
        da, db = mm(dc, b, "nn", hi), mm(dc, a, "tn", hi)
    else:
        da, db = mm(b, dc, "nt", hi), mm(a, dc, "nn", hi)
    return da, db


mm.defvjp(_mm_fwd, _mm_bwd)


TRI_BASE = 8


def _unit_lower_inverses(Ls):
    n = Ls[0].shape[0]
    r = lax.broadcasted_iota(jnp.int32, (n, n), 0)
    c = lax.broadcasted_iota(jnp.int32, (n, n), 1)
    eye = jnp.where(r == c, 1.0, 0.0).astype(F32)
    base = r // TRI_BASE == c // TRI_BASE
    Ps = [jnp.where(base, -L, 0.0) for L in Ls]
    invs = [eye + P for P in Ps]
    k = 1
    while 2 * k < TRI_BASE:
        Ps = [_dot_x3(P, P) for P in Ps]
        invs = [inv + _dot_x3(inv, P) for inv, P in zip(invs, Ps)]
        k *= 2
    b = 2 * TRI_BASE
    while b <= n:
        off_mask = (r // b == c // b) & ((r % b) >= b // 2) & ((c % b) < b // 2)
        ts = [_dot_x3(inv, jnp.where(off_mask, L, 0.0)) for inv, L in zip(invs, Ls)]
        invs = [inv - _dot_x3(t, inv) for inv, t in zip(invs, ts)]
        b *= 2
    return invs


@jax.custom_vjp
def tri_apply(invs, Ls, r1s, r2s):
    return [_dot_x3(i, r) for i, r in zip(invs, r1s)], [_dot_x3(i, r) for i, r in zip(invs, r2s)]


def _tri_fwd(invs, Ls, r1s, r2s):
    s1s = [_dot_x3(i, r) for i, r in zip(invs, r1s)]
    s2s = [_dot_x3(i, r) for i, r in zip(invs, r2s)]
    return (s1s, s2s), (invs, s1s, s2s)


def _tri_bwd(res, ds):
    invs, s1s, s2s = res
    d1s = [_dot_x3(i, d, _DIMS["tn"]) for i, d in zip(invs, ds[0])]
    d2s = [_dot_x3(i, d, _DIMS["tn"]) for i, d in zip(invs, ds[1])]
    dLs = [-(_dot_x3(d1, s1, _DIMS["nt"]) + _dot_x3(d2, s2, _DIMS["nt"])) for d1, s1, d2, s2 in zip(d1s, s1s, d2s, s2s)]
    return [jnp.zeros_like(i) for i in invs], dLs, d1s, d2s


tri_apply.defvjp(_tri_fwd, _tri_bwd)


def _gdn_chunk(qs, ks, vs, gcbs, btbs, Ss, invs=None):
    C = qs[0].shape[0]
    r = lax.broadcasted_iota(jnp.int32, (C, C), 0)
    c = lax.broadcasted_iota(jnp.int32, (C, C), 1)
    causal, strict = c <= r, c < r
    rows = lax.broadcasted_iota(jnp.int32, gcbs[0].shape, 0)
    Gs = [g[:, :C] for g in gcbs]
    decays = [jnp.exp(jnp.where(causal, G - G.T, NEG)) for G in Gs]
    kbs = [k * b for k, b in zip(ks, btbs)]
    vbs = [v * b for v, b in zip(vs, btbs)]
    Ls = [jnp.where(strict, mm(kb, k, "nt", False) * d, 0.0) for kb, k, d in zip(kbs, ks, decays)]
    egs = [jnp.exp(g) for g in gcbs]
    if invs is None:
        invs = _unit_lower_inverses(Ls)
    us, ws = tri_apply(invs, Ls, vbs, [kb * eg for kb, eg in zip(kbs, egs)])
    qks = [jnp.where(causal, mm(q, k, "nt", False) * d, 0.0) for q, k, d in zip(qs, ks, decays)]
    g_lasts = [jnp.sum(jnp.where(rows == C - 1, g, 0.0), axis=0, keepdims=True) for g in gcbs]
    q_decs = [q * eg for q, eg in zip(qs, egs)]
    k_decs = [k * jnp.exp(gl - g) for k, gl, g in zip(ks, g_lasts, gcbs)]
    v_news = [u - mm(w, S, "nn", False) for u, w, S in zip(us, ws, Ss)]
    os_ = [mm(qd, S, "nn", False) + mm(qk, vn, "nn", False) for qd, S, qk, vn in zip(q_decs, Ss, qks, v_news)]
    S_news = [S * jnp.exp(gl) + mm(kd, vn, "tn", False) for S, gl, kd, vn in zip(Ss, g_lasts, k_decs, v_news)]
    return os_, S_news, invs


def gdn_core(qkv, gc, bt, S):
    nchunk = S // GDN_C

    def body(qkv_ref, g_ref, b_ref, o_ref, st_ref, inv_ref, s_scr):
        n = pl.program_id(0)

        @pl.when(n == 0)
        def _():
            s_scr[...] = jnp.zeros_like(s_scr)

        heads = range(GDN_H)
        S_in = [s_scr[h] for h in heads]
        os_, S_new, invs = _gdn_chunk([qkv_ref[0, h] for h in heads], [qkv_ref[1, h] for h in heads], [qkv_ref[2, h] for h in heads],
                                      [g_ref[h] for h in heads], [b_ref[h] for h in heads], S_in)
        for h in heads:
            st_ref[h] = S_in[h]
            inv_ref[h] = invs[h]
            o_ref[h] = os_[h]
            s_scr[h] = S_new[h]

    blk3 = pl.BlockSpec((3, GDN_H, GDN_C, LANES), lambda n: (0, 0, n, 0))
    hb = pl.BlockSpec((GDN_H, GDN_C, LANES), lambda n: (0, n, 0))
    return pl.pallas_call(
        body, name="gdn_core", grid=(nchunk,),
        in_specs=[blk3, hb, hb],
        out_specs=[hb, pl.BlockSpec((GDN_H, None, GDN_DK, LANES), lambda n: (0, n, 0, 0)),
                   pl.BlockSpec((GDN_H, None, GDN_C, GDN_C), lambda n: (0, n, 0, 0))],
        out_shape=[SDS((GDN_H, S, LANES), F32), SDS((GDN_H, nchunk, GDN_DK, LANES), F32), SDS((GDN_H, nchunk, GDN_C, GDN_C), F32)],
        scratch_shapes=[pltpu.VMEM((GDN_H, GDN_DK, LANES), F32)],
        compiler_params=_cp(1),
    )(qkv, gc, bt)


def gdn_core_bwd(qkv, gc, bt, states, invs, do, S):
    nchunk = S // GDN_C

    def body(qkv_ref, g_ref, b_ref, st_ref, inv_ref, do_ref, dqkv_ref, dg_ref, db_ref, ds_scr):
        n = pl.program_id(0)

        @pl.when(n == 0)
        def _():
            ds_scr[...] = jnp.zeros_like(ds_scr)

        heads = range(GDN_H)
        saved = [inv_ref[h] for h in heads]
        _, vjp = jax.vjp(lambda *a: _gdn_chunk(*a, invs=saved)[:2],
                         [qkv_ref[0, h] for h in heads], [qkv_ref[1, h] for h in heads], [qkv_ref[2, h] for h in heads],
                         [g_ref[h] for h in heads], [b_ref[h] for h in heads], [st_ref[h] for h in heads])
        dq, dk, dv, dg, db, dS = vjp(([do_ref[h] for h in heads], [ds_scr[h] for h in heads]))
        for h in heads:
            dqkv_ref[0, h] = dq[h]
            dqkv_ref[1, h] = dk[h]
            dqkv_ref[2, h] = dv[h]
            dg_ref[h] = dg[h]
            db_ref[h] = db[h]
            ds_scr[h] = dS[h]

    rev = lambda n: nchunk - 1 - n
    blk3 = pl.BlockSpec((3, GDN_H, GDN_C, LANES), lambda n: (0, 0, rev(n), 0))
    hb = pl.BlockSpec((GDN_H, GDN_C, LANES), lambda n: (0, rev(n), 0))
    return pl.pallas_call(
        body, name="gdn_core_bwd", grid=(nchunk,),
        in_specs=[blk3, hb, hb, pl.BlockSpec((GDN_H, None, GDN_DK, LANES), lambda n: (0, rev(n), 0, 0)),
                  pl.BlockSpec((GDN_H, None, GDN_C, GDN_C), lambda n: (0, rev(n), 0, 0)), hb],
        out_specs=[blk3, hb, hb],
        out_shape=[SDS((3, GDN_H, S, LANES), F32), SDS((GDN_H, S, LANES), F32), SDS((GDN_H, S, LANES), F32)],
        scratch_shapes=[pltpu.VMEM((GDN_H, GDN_DK, LANES), F32)],
        compiler_params=_cp(1),
    )(qkv, gc, bt, states, invs, do)


GDN_MAIN = 4 * GDN_H * LANES
GDN_PROJ = GDN_MAIN + LANES
RT = 256


def gdn_forward(h, w_in, conv_w, alog, dtb, out_gain, w_out):
    S = h.shape[0]
    nt = S // RT
    proj = matmul(h, w_in, "nn", BF16, "gdn_in")
    qkv = gdn_pre(proj, conv_w, S)
    ab_row = Row(proj, (RT, LANES), lambda i: (i, GDN_MAIN // LANES), gdtype=BF16, gshape=(S, LANES), gimap=lambda i: (i, 0))
    hm = lambda i: (0, i, 0)
    hv = GDN_H * LANES
    gc, bt = rowwise(f_gdn_gates, [ab_row], [alog, dtb],
                     [Out((GDN_H, S, LANES), F32, (GDN_H, RT, LANES), hm, lead=GDN_H)] * 2, (nt,), "gdn_gates")
    o, states, invs = gdn_core(qkv, gc, bt, S)
    o_row = Row(o, (GDN_H, RT, LANES), hm, lead=GDN_H)
    z_row = Row(proj, (RT, hv), lambda i: (i, 3), splits=[LANES] * GDN_H, gdtype=BF16, gshape=(S, hv), gimap=lambda i: (i, 0))
    (on,) = rowwise(f_gdn_post, [o_row, z_row], [out_gain],
                    [Out((S, hv), BF16, (RT, hv), lambda i: (i, 0), splits=[LANES] * GDN_H)], (nt,), "gdn_post")
    y = matmul(on, w_out, "nn", BF16, "gdn_out")
    saved = dict(h=h, proj=proj, qkv=qkv, gc=gc, bt=bt, states=states, invs=invs, o=o, on=on, ab_row=ab_row, o_row=o_row, z_row=z_row)
    return y, saved


def gdn_backward(dy, sv, w_in, conv_w, alog, dtb, out_gain, w_out):
    S = dy.shape[0]
    nt = S // RT
    hm = lambda i: (0, i, 0)
    hv = GDN_H * LANES
    don = matmul(dy, w_out, "nt", BF16, "gdn_out_dx")
    d_w_out = matmul(sv["on"], dy, "tn", F32, "gdn_out_dw")
    (do, dz), (d_gain,) = rowwise_bwd(f_gdn_post, [sv["o_row"], sv["z_row"]], [out_gain],
                                      [Row(don, (RT, hv), lambda i: (i, 0), splits=[LANES] * GDN_H)], (nt,), "gdn_post_bwd")
    dqkv, dgc, dbt = gdn_core_bwd(sv["qkv"], sv["gc"], sv["bt"], sv["states"], sv["invs"], do, S)
    head_blk = lambda a: Row(a, (GDN_H, RT, LANES), hm, lead=GDN_H)
    (dab,), (d_alog, d_dtb) = rowwise_bwd(f_gdn_gates, [sv["ab_row"]], [alog, dtb], [head_blk(dgc), head_blk(dbt)],
                                          (nt,), "gdn_gates_bwd")
    dqkv_proj, d_conv = gdn_pre_bwd(sv["proj"], conv_w, dqkv, S)
    dproj = jnp.concatenate([dqkv_proj, dz, dab], axis=1)
    d_w_in = matmul(sv["h"], dproj, "tn", F32, "gdn_in_dw")
    dh = matmul(dproj, w_in, "nt", BF16, "gdn_in_dx")
    return dh, dict(w_in=d_w_in, conv=d_conv, alog=d_alog, dtb=d_dtb, gain=d_gain, w_out=d_w_out)


QB = DSW_SPAN
N_HP = DSW_HG // LANES
PROJ_BLKS = 3 * 3 * N_HP


def _bucket_maps():
    a = np.arange(QB)[:, None]
    j = np.arange(2 * QB)[None, :]
    dist = QB + a - j
    band = (dist >= 0) & (dist <= DSW_SPAN)
    maps = []
    for _, dil in DSW_GROUPS:
        dd = np.maximum(dist, 0) * dil
        max_exact = REL_BUCKETS // 2
        scaled = np.log(np.maximum(dd, 1).astype(np.float32) / np.float32(max_exact)) / np.float32(math.log(REL_MAX_DIST / max_exact))
        large = max_exact + (scaled * np.float32(REL_BUCKETS - max_exact)).astype(np.int32)
        large = np.minimum(large, REL_BUCKETS - 1)
        maps.append(np.where(dd < max_exact, dd, large).astype(np.int32))
    return np.stack(maps), band


def dsw_bias(rel_bias):
    maps, band = _bucket_maps()
    maps = np.where(band[None], maps, -1).astype(np.int32)

    def body(tab_ref, bk_ref, o_ref):
        gh = pl.program_id(0)
        bk = bk_ref[...]
        acc = jnp.full(bk.shape, NEG, F32)
        for b in range(REL_BUCKETS):
            acc = jnp.where(bk == b, tab_ref[b, gh], acc)
        o_ref[...] = acc

    return pl.pallas_call(
        body, name="dsw_bias", grid=(3 * GDN_H,),
        in_specs=[pl.BlockSpec(memory_space=pltpu.SMEM),
                  pl.BlockSpec((None, QB, 2 * QB), lambda gh: (gh // GDN_H, 0, 0))],
        out_specs=pl.BlockSpec((None, QB, 2 * QB), lambda gh: (gh, 0, 0)),
        out_shape=SDS((3 * GDN_H, QB, 2 * QB), F32),
        compiler_params=_cp(1),
    )(rel_bias, jnp.asarray(maps))


def dsw_bias_grad(dbias):
    maps, band = _bucket_maps()
    maps = np.where(band[None], maps, -1).astype(np.int32)

    def body(d_ref, bk_ref, o_ref):
        bk = bk_ref[...]
        d = d_ref[...]
        rows = lax.broadcasted_iota(jnp.int32, (REL_BUCKETS, LANES), 0)
        acc = jnp.zeros((REL_BUCKETS, LANES), F32)
        for b in range(REL_BUCKETS):
            part = jnp.sum(jnp.where(bk == b, d, 0.0), axis=0, keepdims=True)
            val = jnp.sum(part, axis=1, keepdims=True)
            acc = jnp.where(rows == b, val, acc)
        o_ref[...] = acc

    return pl.pallas_call(
        body, name="dsw_bias_grad", grid=(3 * GDN_H,),
        in_specs=[pl.BlockSpec((None, QB, 2 * QB), lambda gh: (gh, 0, 0)),
                  pl.BlockSpec((None, QB, 2 * QB), lambda gh: (gh // GDN_H, 0, 0))],
        out_specs=pl.BlockSpec((None, REL_BUCKETS, LANES), lambda gh: (gh, 0, 0)),
        out_shape=SDS((3 * GDN_H, REL_BUCKETS, LANES), F32),
        compiler_params=_cp(1),
    )(dbias, jnp.asarray(maps))


def _nt(a, b):
    return lax.dot_general(a, b, (((1,), (1,)), ((), ())), preferred_element_type=F32)


def _tn(a, b):
    return lax.dot_general(a, b, (((0,), (0,)), ((), ())), preferred_element_type=F32)


def dsw_group_fwd(qn, kn, proj, bias, gi, S):
    dil = DSW_GROUPS[gi][1]
    sd = S // dil
    nq = sd // QB
    qv = qn.reshape(sd, dil * 3 * DSW_HG)
    kv = kn.reshape(sd, dil * 3 * DSW_HG)
    pv = proj.reshape(sd, dil * 9 * DSW_HG)
    qk_col = lambda hp, r: r * (3 * N_HP) + gi * N_HP + hp
    v_col = lambda hp, r: r * PROJ_BLKS + 2 * 3 * N_HP + gi * N_HP + hp

    def body(q_ref, kp_ref, kc_ref, vp_ref, vc_ref, b_ref, o_ref, l_ref):
        i = pl.program_id(2)
        q = q_ref[...]
        k2 = jnp.concatenate([kp_ref[...], kc_ref[...]], axis=0)
        v2 = jnp.concatenate([vp_ref[...], vc_ref[...]], axis=0).astype(BF16)
        lane_q = lax.broadcasted_iota(jnp.int32, (QB, LANES), 1) < DSW_DH
        lane_k = lax.broadcasted_iota(jnp.int32, (2 * QB, LANES), 1) < DSW_DH
        col = lax.broadcasted_iota(jnp.int32, (QB, 2 * QB), 1)
        first = jnp.logical_and(i == 0, col < QB)
        o_acc = jnp.zeros((QB, LANES), F32)
        lse_b = jnp.zeros((QB, LANES), F32)
        for hh in range(2):
            mq = lane_q if hh == 0 else jnp.logical_not(lane_q)
            mk = lane_k if hh == 0 else jnp.logical_not(lane_k)
            s = _nt(jnp.where(mq, q, 0).astype(BF16), k2) + b_ref[hh]
            s = jnp.where(first, NEG, s)
            mx = jnp.max(s, axis=1, keepdims=True)
            p = jnp.exp(s - mx)
            l = jnp.sum(p, axis=1, keepdims=True)
            oh = jnp.dot(p.astype(BF16), jnp.where(mk, v2, 0).astype(BF16), preferred_element_type=F32) / l
            o_acc = o_acc + oh
            lse_b = jnp.where(mq, mx + jnp.log(l), lse_b)
        o_ref[...] = o_acc
        l_ref[...] = lse_b

    blk = (QB, LANES)
    out_spec = pl.BlockSpec(blk, lambda hp, r, i: (i, r * N_HP + hp))
    o, lse = pl.pallas_call(
        body, name=f"dsw_fwd_g{gi}", grid=(N_HP, dil, nq),
        in_specs=[pl.BlockSpec(blk, lambda hp, r, i: (i, qk_col(hp, r))),
                  pl.BlockSpec(blk, lambda hp, r, i: (jnp.maximum(i - 1, 0), qk_col(hp, r))),
                  pl.BlockSpec(blk, lambda hp, r, i: (i, qk_col(hp, r))),
                  pl.BlockSpec(blk, lambda hp, r, i: (jnp.maximum(i - 1, 0), v_col(hp, r))),
                  pl.BlockSpec(blk, lambda hp, r, i: (i, v_col(hp, r))),
                  pl.BlockSpec((2, QB, 2 * QB), lambda hp, r, i: (gi * N_HP + hp, 0, 0))],
        out_specs=[out_spec, out_spec],
        out_shape=[SDS((sd, dil * DSW_HG), F32)] * 2,
        compiler_params=_cp(3),
    )(qv, kv, kv, pv, pv, bias)
    return o.reshape(S, DSW_HG), lse.reshape(S, DSW_HG)


def dsw_group_bwd(qn, kn, proj, bias, do, o, lse, gi, S):
    dil = DSW_GROUPS[gi][1]
    sd = S // dil
    nq = sd // QB
    qv = qn.reshape(sd, dil * 3 * DSW_HG)
    kv = kn.reshape(sd, dil * 3 * DSW_HG)
    pv = proj.reshape(sd, dil * 9 * DSW_HG)
    dov = do.reshape(sd, dil * DSW_HG)
    ov = o.reshape(sd, dil * DSW_HG)
    lv = lse.reshape(sd, dil * DSW_HG)
    qk_col = lambda hp, r: r * (3 * N_HP) + gi * N_HP + hp
    v_col = lambda hp, r: r * PROJ_BLKS + 2 * 3 * N_HP + gi * N_HP + hp
    o_col = lambda hp, r: r * N_HP + hp
    cur = lambda i: jnp.minimum(i, nq - 1)
    prev = lambda i: jnp.maximum(jnp.minimum(i, nq - 1) - 1, 0)
    done = lambda i: jnp.maximum(i - 1, 0)

    def body(q_ref, kp_ref, kc_ref, vp_ref, vc_ref, b_ref, do_ref, o_ref, l_ref,
             dq_ref, dk_ref, dv_ref, db_ref, dk_scr, dv_scr):
        r, i = pl.program_id(1), pl.program_id(2)

        @pl.when(jnp.logical_and(r == 0, i == 0))
        def _():
            db_ref[...] = jnp.zeros_like(db_ref)

        @pl.when(i == 0)
        def _():
            dk_scr[...] = jnp.zeros_like(dk_scr)
            dv_scr[...] = jnp.zeros_like(dv_scr)

        @pl.when(i < nq)
        def _():
            q = q_ref[...]
            k2 = jnp.concatenate([kp_ref[...], kc_ref[...]], axis=0)
            v2 = jnp.concatenate([vp_ref[...], vc_ref[...]], axis=0).astype(BF16)
            dout = do_ref[...].astype(F32)
            prod = dout * o_ref[...].astype(F32)
            lse_b = l_ref[...]
            lane_q = lax.broadcasted_iota(jnp.int32, (QB, LANES), 1) < DSW_DH
            col = lax.broadcasted_iota(jnp.int32, (QB, 2 * QB), 1)
            first = jnp.logical_and(i == 0, col < QB)
            dq = jnp.zeros((QB, LANES), F32)
            dk2 = jnp.zeros((2 * QB, LANES), F32)
            dv2 = jnp.zeros((2 * QB, LANES), F32)
            for hh in range(2):
                mq = lane_q if hh == 0 else jnp.logical_not(lane_q)
                qm = jnp.where(mq, q, 0).astype(BF16)
                dom = jnp.where(mq, dout, 0.0).astype(BF16)
                s = _nt(qm, k2) + b_ref[hh]
                s = jnp.where(first, NEG, s)
                lse_h = jnp.max(jnp.where(mq, lse_b, NEG), axis=1, keepdims=True)
                p = jnp.exp(s - lse_h)
                delta = jnp.sum(jnp.where(mq, prod, 0.0), axis=1, keepdims=True)
                dp = _nt(dom, v2)
                ds = p * (dp - delta)
                dsb = ds.astype(BF16)
                dq = dq + jnp.where(mq, jnp.dot(dsb, k2, preferred_element_type=F32), 0.0)
                dk2 = dk2 + _tn(dsb, qm)
                dv2 = dv2 + _tn(p.astype(BF16), dom)
                db_ref[hh] += ds
            dq_ref[...] = dq
            dk_ref[...] = dk_scr[...] + dk2[:QB]
            dv_ref[...] = (dv_scr[...] + dv2[:QB]).astype(dv_ref.dtype)
            dk_scr[...] = dk2[QB:]
            dv_scr[...] = dv2[QB:]

        @pl.when(i == nq)
        def _():
            dk_ref[...] = dk_scr[...]
            dv_ref[...] = dv_scr[...].astype(dv_ref.dtype)

    blk = (QB, LANES)
    dq, dk, dv, dbias = pl.pallas_call(
        body, name=f"dsw_bwd_g{gi}", grid=(N_HP, dil, nq + 1),
        in_specs=[pl.BlockSpec(blk, lambda hp, r, i: (cur(i), qk_col(hp, r))),
                  pl.BlockSpec(blk, lambda hp, r, i: (prev(i), qk_col(hp, r))),
                  pl.BlockSpec(blk, lambda hp, r, i: (cur(i), qk_col(hp, r))),
                  pl.BlockSpec(blk, lambda hp, r, i: (prev(i), v_col(hp, r))),
                  pl.BlockSpec(blk, lambda hp, r, i: (cur(i), v_col(hp, r))),
                  pl.BlockSpec((2, QB, 2 * QB), lambda hp, r, i: (gi * N_HP + hp, 0, 0)),
                  pl.BlockSpec(blk, lambda hp, r, i: (cur(i), o_col(hp, r))),
                  pl.BlockSpec(blk, lambda hp, r, i: (cur(i), o_col(hp, r))),
                  pl.BlockSpec(blk, lambda hp, r, i: (cur(i), o_col(hp, r)))],
        out_specs=[pl.BlockSpec(blk, lambda hp, r, i: (cur(i), o_col(hp, r))),
                   pl.BlockSpec(blk, lambda hp, r, i: (done(i), o_col(hp, r))),
                   pl.BlockSpec(blk, lambda hp, r, i: (done(i), o_col(hp, r))),
                   pl.BlockSpec((2, QB, 2 * QB), lambda hp, r, i: (hp, 0, 0))],
        out_shape=[SDS((sd, dil * DSW_HG), F32), SDS((sd, dil * DSW_HG), F32), SDS((sd, dil * DSW_HG), BF16),
                   SDS((GDN_H, QB, 2 * QB), F32)],
        scratch_shapes=[pltpu.VMEM(blk, F32), pltpu.VMEM(blk, F32)],
        compiler_params=_cp(3),
    )(qv, kv, kv, pv, pv, bias, dov, ov, lv)
    return dq.reshape(S, DSW_HG), dk.reshape(S, DSW_HG), dv.reshape(S, DSW_HG), dbias


def dsw_forward(h, w_in, q_gain2, k_gain2, rel_bias, w_out):
    S = h.shape[0]
    nt = S // RT
    nb = 3 * N_HP
    proj = matmul(h, w_in, "nn", F32, "dsw_in")
    width = nb * LANES
    lanes12 = [LANES] * nb
    (qn,) = rowwise(f_qnorm, [Row(proj, (RT, width), lambda i: (i, 0), splits=lanes12)], [q_gain2],
                    [Out((S, width), BF16, (RT, width), lambda i: (i, 0), splits=lanes12)], (nt,), "dsw_qnorm")
    (kn,) = rowwise(f_qknorm, [Row(proj, (RT, width), lambda i: (i, 1), splits=lanes12)], [k_gain2],
                    [Out((S, width), BF16, (RT, width), lambda i: (i, 0), splits=lanes12)], (nt,), "dsw_knorm")
    bias = dsw_bias(rel_bias)
    os_, ls_ = [], []
    for gi in range(3):
        o, l = dsw_group_fwd(qn, kn, proj, bias, gi, S)
        os_.append(o)
        ls_.append(l)
    full = lambda a: Row(a, (RT, DSW_HG), lambda i: (i, 0))
    o, lse = rowwise(f_combine, [full(a) for a in os_ + ls_], [],
                     [Out((S, DSW_HG), BF16, (RT, DSW_HG), lambda i: (i, 0)), Out((S, DSW_HG), F32, (RT, DSW_HG), lambda i: (i, 0))],
                     (nt,), "dsw_combine")
    y = matmul(o, w_out, "nn", F32, "dsw_out")
    return y, dict(h=h, proj=proj, qn=qn, kn=kn, bias=bias, o=o, lse=lse)


def dsw_backward(dy, sv, w_in, q_gain2, k_gain2, w_out):
    S = dy.shape[0]
    nt = S // RT
    nb = 3 * N_HP
    do = matmul(dy, w_out, "nt", BF16, "dsw_out_dx")
    d_w_out = matmul(sv["o"], dy, "tn", F32, "dsw_out_dw")
    pieces_q, pieces_k, pieces_v, dbs = [], [], [], []
    d_qg = jnp.zeros((1, LANES), F32)
    d_kg = jnp.zeros((1, LANES), F32)
    for gi in range(3):
        dq, dk, dv, db = dsw_group_bwd(sv["qn"], sv["kn"], sv["proj"], sv["bias"], do, sv["o"], sv["lse"], gi, S)
        dbs.append(db)
        pieces_v.append(dv)
        for which, dd in ((0, dq), (1, dk)):
            lanes4 = [LANES] * N_HP
            row = Row(sv["proj"], (RT, DSW_HG), lambda i, _o=which * 3 + gi: (i, _o), splits=lanes4,
                      gdtype=BF16, gshape=(S, DSW_HG), gimap=lambda i: (i, 0))
            fn, gain = (f_qnorm, q_gain2) if which == 0 else (f_qknorm, k_gain2)
            (dx,), (dg,) = rowwise_bwd(fn, [row], [gain], [Row(dd, (RT, DSW_HG), lambda i: (i, 0), splits=lanes4)],
                                       (nt,), f"dsw_norm_bwd_{which}{gi}")
            if which == 0:
                pieces_q.append(dx)
                d_qg = d_qg + dg
            else:
                pieces_k.append(dx)
                d_kg = d_kg + dg
    dproj = jnp.concatenate(pieces_q + pieces_k + pieces_v, axis=1)
    d_w_in = matmul(sv["h"], dproj, "tn", F32, "dsw_in_dw")
    dh = matmul(dproj, w_in, "nt", F32, "dsw_in_dx")
    d_rel = dsw_bias_grad(jnp.concatenate(dbs, axis=0))
    return dh, dict(w_in=d_w_in, q_gain2=d_qg, k_gain2=d_kg, rel=d_rel, w_out=d_w_out)


N_LB = DSW_HG // LANES
HALF = DSW_DH // 2


def _lanes(j):
    return slice(LANES * j, LANES * (j + 1))


def _deinterleave(stage, out_ref, dil, rows, dtype):
    for r in range(dil):
        for j in range(N_LB):
            out_ref[r, :, _lanes(j)] = stage[j, pl.ds(r, rows, stride=dil), :].astype(dtype)


def _interleave(in_ref, stage, dil, rows):
    for r in range(dil):
        for j in range(N_LB):
            stage[j, pl.ds(r, rows, stride=dil), :] = in_ref[r, :, _lanes(j)].astype(F32)


def dsw_prep(proj, q_gain2, k_gain2, gi, S):
    dil = DSW_GROUPS[gi][1]
    nt, rows = S // RT, RT // dil

    def body(q_ref, k_ref, v_ref, qg_ref, kg_ref, qo_ref, ko_ref, vo_ref, stage):
        for src, gain_ref, scale, dst in ((q_ref, qg_ref, DSW_DH ** -0.5, qo_ref), (k_ref, kg_ref, 1.0, ko_ref), (v_ref, None, None, vo_ref)):
            for j in range(N_LB):
                val = src[:, _lanes(j)].astype(F32)
                val = val if gain_ref is None else _qknorm1(val, gain_ref[...], scale)
                if dil == 1:
                    dst[0, :, _lanes(j)] = val.astype(BF16)
                else:
                    stage[j] = val
            if dil > 1:
                _deinterleave(stage, dst, dil, rows, BF16)

    col = lambda which: pl.BlockSpec((RT, DSW_HG), lambda i, _c=which * 3 + gi: (i, _c))
    gspec = pl.BlockSpec((1, LANES), lambda i: (0, 0))
    ospec = pl.BlockSpec((dil, rows, DSW_HG), lambda i: (0, i, 0))
    return pl.pallas_call(
        body, name=f"dsw_prep_g{gi}", grid=(nt,),
        in_specs=[col(0), col(1), col(2), gspec, gspec], out_specs=[ospec] * 3,
        out_shape=[SDS((dil, S // dil, DSW_HG), BF16)] * 3,
        scratch_shapes=[pltpu.VMEM((N_LB, RT, LANES), F32)], compiler_params=_cp(1),
    )(proj, proj, proj, q_gain2, k_gain2)


def dsw_prep_bwd(proj, q_gain2, k_gain2, dqd, dkd, dvd, gi, S):
    dil = DSW_GROUPS[gi][1]
    nt, rows = S // RT, RT // dil

    def body(q_ref, k_ref, qg_ref, kg_ref, dq_ref, dk_ref, dv_ref, oq_ref, ok_ref, ov_ref, dqg_ref, dkg_ref, stage):
        i = pl.program_id(0)

        @pl.when(i == 0)
        def _():
            dqg_ref[...] = jnp.zeros_like(dqg_ref)
            dkg_ref[...] = jnp.zeros_like(dkg_ref)

        for src, gain_ref, scale, cot_ref, dst, dg_ref in ((q_ref, qg_ref, DSW_DH ** -0.5, dq_ref, oq_ref, dqg_ref),
                                                          (k_ref, kg_ref, 1.0, dk_ref, ok_ref, dkg_ref)):
            if dil > 1:
                _interleave(cot_ref, stage, dil, rows)
            for j in range(N_LB):
                _, vjp = jax.vjp(lambda x, g, _s=scale: _qknorm1(x, g, _s), src[:, _lanes(j)].astype(F32), gain_ref[...])
                dx, dg = vjp(stage[j] if dil > 1 else cot_ref[0, :, _lanes(j)].astype(F32))
                dst[:, _lanes(j)] = dx.astype(dst.dtype)
                dg_ref[...] += dg
        if dil > 1:
            _interleave(dv_ref, stage, dil, rows)
        for j in range(N_LB):
            ov_ref[:, _lanes(j)] = (stage[j] if dil > 1 else dv_ref[0, :, _lanes(j)]).astype(ov_ref.dtype)

    col = lambda which: pl.BlockSpec((RT, DSW_HG), lambda i, _c=which * 3 + gi: (i, _c))
    gspec = pl.BlockSpec((1, LANES), lambda i: (0, 0))
    dspec = pl.BlockSpec((dil, rows, DSW_HG), lambda i: (0, i, 0))
    nspec = pl.BlockSpec((RT, DSW_HG), lambda i: (i, 0))
    return pl.pallas_call(
        body, name=f"dsw_prep_bwd_g{gi}", grid=(nt,),
        in_specs=[col(0), col(1), gspec, gspec, dspec, dspec, dspec], out_specs=[nspec] * 3 + [gspec] * 2,
        out_shape=[SDS((S, DSW_HG), BF16)] * 3 + [SDS((1, LANES), F32)] * 2,
        scratch_shapes=[pltpu.VMEM((N_LB, RT, LANES), F32)], compiler_params=_cp(1),
    )(proj, proj, q_gain2, k_gain2, dqd, dkd, dvd)


def _head_masks(rows):
    lane = lax.broadcasted_iota(jnp.int32, (rows, LANES), 1)
    return lane < DSW_DH, (lane % DSW_DH) < HALF


def dsw_attn_fwd(qd, kd, vd, bias, gi, S):
    dil = DSW_GROUPS[gi][1]
    sd = S // dil
    nq = sd // QB

    def body(q_ref, k_ref, v_ref, b_ref, o_ref, l_ref, kp_scr, vp_scr):
        i = pl.program_id(1)

        @pl.when(i == 0)
        def _():
            kp_scr[...] = jnp.zeros_like(kp_scr)
            vp_scr[...] = jnp.zeros_like(vp_scr)

        lo_q, _ = _head_masks(QB)
        lo_k, _ = _head_masks(2 * QB)
        col = lax.broadcasted_iota(jnp.int32, (QB, 2 * QB), 1)
        first = jnp.logical_and(i == 0, col < QB)
        hps, heads = range(N_HP), range(2 * N_HP)
        k2s = [jnp.concatenate([kp_scr[:, _lanes(hp)], k_ref[:, _lanes(hp)]], axis=0) for hp in hps]
        v2s = [jnp.concatenate([vp_scr[:, _lanes(hp)], v_ref[:, _lanes(hp)]], axis=0) for hp in hps]
        qs = [q_ref[:, _lanes(hp)] for hp in hps]
        k_now, v_now = k_ref[...], v_ref[...]
        mqs = [lo_q if h % 2 == 0 else jnp.logical_not(lo_q) for h in heads]
        mks = [lo_k if h % 2 == 0 else jnp.logical_not(lo_k) for h in heads]
        ss = [jnp.where(first, NEG, _nt(jnp.where(mqs[h], qs[h // 2], 0).astype(BF16), k2s[h // 2]) + b_ref[h]) for h in heads]
        mxs = [jnp.max(s, axis=1, keepdims=True) for s in ss]
        ps = [jnp.exp(s - mx) for s, mx in zip(ss, mxs)]
        ls = [jnp.sum(p, axis=1, keepdims=True) for p in ps]
        ohs = [jnp.dot(ps[h].astype(BF16), jnp.where(mks[h], v2s[h // 2], 0).astype(BF16), preferred_element_type=F32) / ls[h] for h in heads]
        lse_h = [mx + jnp.log(l) for mx, l in zip(mxs, ls)]
        for hp in hps:
            o_ref[:, _lanes(hp)] = ohs[2 * hp] + ohs[2 * hp + 1]
            l_ref[:, _lanes(hp)] = jnp.where(lo_q, lse_h[2 * hp], lse_h[2 * hp + 1])
        kp_scr[...] = k_now
        vp_scr[...] = v_now

    blk = pl.BlockSpec((None, QB, DSW_HG), lambda r, i: (r, i, 0))
    return pl.pallas_call(
        body, name=f"dsw_attn_g{gi}", grid=(dil, nq),
        in_specs=[blk, blk, blk, pl.BlockSpec((GDN_H, QB, 2 * QB), lambda r, i: (gi, 0, 0))],
        out_specs=[blk, blk], out_shape=[SDS((dil, sd, DSW_HG), F32)] * 2,
        scratch_shapes=[pltpu.VMEM((QB, DSW_HG), BF16)] * 2, compiler_params=_cp(2),
    )(qd, kd, vd, bias)


def dsw_attn_bwd(qd, kd, vd, bias, dod, statd, gi, S):
    dil = DSW_GROUPS[gi][1]
    sd = S // dil
    nq = sd // QB
    cur = lambda i: jnp.minimum(i, nq - 1)
    done = lambda i: jnp.maximum(i - 1, 0)

    def body(q_ref, k_ref, v_ref, b_ref, do_ref, st_ref, dq_ref, dk_ref, dv_ref, db_ref, kp_scr, vp_scr, dk_scr, dv_scr):
        r, i = pl.program_id(0), pl.program_id(1)

        @pl.when(jnp.logical_and(r == 0, i == 0))
        def _():
            db_ref[...] = jnp.zeros_like(db_ref)

        @pl.when(i == 0)
        def _():
            for scr in (kp_scr, vp_scr, dk_scr, dv_scr):
                scr[...] = jnp.zeros_like(scr)

        @pl.when(i < nq)
        def _():
            lo_q, first_half = _head_masks(QB)
            col = lax.broadcasted_iota(jnp.int32, (QB, 2 * QB), 1)
            first = jnp.logical_and(i == 0, col < QB)
            hps, heads = range(N_HP), range(2 * N_HP)
            k2s = [jnp.concatenate([kp_scr[:, _lanes(hp)], k_ref[:, _lanes(hp)]], axis=0) for hp in hps]
            v2s = [jnp.concatenate([vp_scr[:, _lanes(hp)], v_ref[:, _lanes(hp)]], axis=0) for hp in hps]
            qs = [q_ref[:, _lanes(hp)] for hp in hps]
            douts = [do_ref[:, _lanes(hp)] for hp in hps]
            stats = [st_ref[:, _lanes(hp)] for hp in hps]
            dkc = [dk_scr[:, _lanes(hp)] for hp in hps]
            dvc = [dv_scr[:, _lanes(hp)] for hp in hps]
            k_now, v_now = k_ref[...], v_ref[...]
            mqs = [lo_q if h % 2 == 0 else jnp.logical_not(lo_q) for h in heads]
            qms = [jnp.where(mqs[h], qs[h // 2], 0).astype(BF16) for h in heads]
            doms = [jnp.where(mqs[h], douts[h // 2], 0).astype(BF16) for h in heads]
            ss = [jnp.where(first, NEG, _nt(qms[h], k2s[h // 2]) + b_ref[h]) for h in heads]
            lses = [jnp.max(jnp.where(jnp.logical_and(mqs[h], first_half), stats[h // 2], NEG), axis=1, keepdims=True) for h in heads]
            deltas = [jnp.max(jnp.where(jnp.logical_and(mqs[h], jnp.logical_not(first_half)), stats[h // 2], NEG), axis=1, keepdims=True)
                      for h in heads]
            ps = [jnp.exp(ss[h] - lses[h]) for h in heads]
            dss = [ps[h] * (_nt(doms[h], v2s[h // 2]) - deltas[h]) for h in heads]
            dsbs = [d.astype(BF16) for d in dss]
            dqh = [jnp.where(mqs[h], jnp.dot(dsbs[h], k2s[h // 2], preferred_element_type=F32), 0.0) for h in heads]
            dkh = [_tn(dsbs[h], qms[h]) for h in heads]
            dvh = [_tn(ps[h].astype(BF16), doms[h]) for h in heads]
            for h in heads:
                db_ref[h] += dss[h]
            for hp in hps:
                dk2 = dkh[2 * hp] + dkh[2 * hp + 1]
                dv2 = dvh[2 * hp] + dvh[2 * hp + 1]
                dq_ref[:, _lanes(hp)] = dqh[2 * hp] + dqh[2 * hp + 1]
                dk_ref[:, _lanes(hp)] = dkc[hp] + dk2[:QB]
                dv_ref[:, _lanes(hp)] = (dvc[hp] + dv2[:QB]).astype(dv_ref.dtype)
                dk_scr[:, _lanes(hp)] = dk2[QB:]
                dv_scr[:, _lanes(hp)] = dv2[QB:]
            kp_scr[...] = k_now
            vp_scr[...] = v_now

        @pl.when(i == nq)
        def _():
            dk_ref[...] = dk_scr[...]
            dv_ref[...] = dv_scr[...].astype(dv_ref.dtype)

    blk = pl.BlockSpec((None, QB, DSW_HG), lambda r, i: (r, cur(i), 0))
    oblk = pl.BlockSpec((None, QB, DSW_HG), lambda r, i: (r, done(i), 0))
    return pl.pallas_call(
        body, name=f"dsw_attn_bwd_g{gi}", grid=(dil, nq + 1),
        in_specs=[blk, blk, blk, pl.BlockSpec((GDN_H, QB, 2 * QB), lambda r, i: (gi, 0, 0)), blk, blk],
        out_specs=[blk, oblk, oblk, pl.BlockSpec((GDN_H, QB, 2 * QB), lambda r, i: (0, 0, 0))],
        out_shape=[SDS((dil, sd, DSW_HG), F32), SDS((dil, sd, DSW_HG), F32), SDS((dil, sd, DSW_HG), BF16),
                   SDS((GDN_H, QB, 2 * QB), F32)],
        scratch_shapes=[pltpu.VMEM((QB, DSW_HG), BF16)] * 2 + [pltpu.VMEM((QB, DSW_HG), F32)] * 2,
        compiler_params=_cp(2),
    )(qd, kd, vd, bias, dod, statd)


def dsw_combine(ods, lseds, S):
    nt = S // RT
    dils = [d for _, d in DSW_GROUPS]

    def body(*refs):
        ins, (o_ref, l_ref), stages = refs[:6], refs[6:8], refs[8:]
        for g in range(3):
            if dils[g] > 1:
                _interleave(ins[g], stages[g], dils[g], RT // dils[g])
                _interleave(ins[3 + g], stages[3 + g], dils[g], RT // dils[g])
        for j in range(N_LB):
            natural = lambda a: stages[a][j] if dils[a % 3] > 1 else ins[a][0, :, _lanes(j)]
            o, lse = f_combine(None, *[natural(a) for a in range(6)])
            o_ref[:, _lanes(j)] = o.astype(o_ref.dtype)
            l_ref[:, _lanes(j)] = lse

    dspec = lambda d: pl.BlockSpec((d, RT // d, DSW_HG), lambda i: (0, i, 0))
    nspec = pl.BlockSpec((RT, DSW_HG), lambda i: (i, 0))
    return pl.pallas_call(
        body, name="dsw_combine", grid=(nt,),
        in_specs=[dspec(d) for d in dils] * 2, out_specs=[nspec, nspec],
        out_shape=[SDS((S, DSW_HG), BF16), SDS((S, DSW_HG), F32)],
        scratch_shapes=[pltpu.VMEM((N_LB, RT, LANES), F32)] * 6, compiler_params=_cp(1),
    )(*ods, *lseds)


def dsw_bwd_prep(do, o, lse, S):
    nt = S // RT
    dils = [d for _, d in DSW_GROUPS]

    def body(do_ref, o_ref, l_ref, *rest):
        outs, (st_do, st_stat) = rest[:6], rest[6:]
        lo, first_half = _head_masks(RT)
        for j in range(N_LB):
            dout = do_ref[:, _lanes(j)]
            prod = dout * o_ref[:, _lanes(j)].astype(F32)
            s_all = jnp.sum(prod, axis=1, keepdims=True)
            s_lo = jnp.sum(jnp.where(lo, prod, 0.0), axis=1, keepdims=True)
            delta = jnp.where(lo, s_lo, s_all - s_lo)
            stat = jnp.where(first_half, l_ref[:, _lanes(j)], delta)
            st_do[j] = dout
            st_stat[j] = stat
            for g in range(3):
                if dils[g] == 1:
                    outs[g][0, :, _lanes(j)] = dout.astype(BF16)
                    outs[3 + g][0, :, _lanes(j)] = stat
        for g in range(3):
            if dils[g] > 1:
                _deinterleave(st_do, outs[g], dils[g], RT // dils[g], BF16)
                _deinterleave(st_stat, outs[3 + g], dils[g], RT // dils[g], F32)

    nspec = pl.BlockSpec((RT, DSW_HG), lambda i: (i, 0))
    dspec = lambda d: pl.BlockSpec((d, RT // d, DSW_HG), lambda i: (0, i, 0))
    res = pl.pallas_call(
        body, name="dsw_bwd_prep", grid=(nt,),
        in_specs=[nspec] * 3, out_specs=[dspec(d) for d in dils] * 2,
        out_shape=[SDS((d, S // d, DSW_HG), BF16) for d in dils] + [SDS((d, S // d, DSW_HG), F32) for d in dils],
        scratch_shapes=[pltpu.VMEM((N_LB, RT, LANES), F32)] * 2, compiler_params=_cp(1),
    )(do, o, lse)
    return res[:3], res[3:]


def dsw_forward(h, w_in, q_gain2, k_gain2, rel_bias, w_out):
    S = h.shape[0]
    proj = matmul(h, w_in, "nn", BF16, "dsw_in", col_shards=N_SHARD)
    bias = dsw_bias(rel_bias)
    qkv, ods, lseds = [], [], []
    for gi in range(3):
        qd, kd, vd = dsw_prep(proj, q_gain2, k_gain2, gi, S)
        od, ld = dsw_attn_fwd(qd, kd, vd, bias, gi, S)
        qkv.append((qd, kd, vd))
        ods.append(od)
        lseds.append(ld)
    o, lse = dsw_combine(ods, lseds, S)
    y = matmul(o, w_out, "nn", BF16, "dsw_out", col_shards=N_SHARD)
    return y, dict(h=h, proj=proj, qkv=qkv, bias=bias, o=o, lse=lse)


def dsw_backward(dy, sv, w_in, q_gain2, k_gain2, w_out):
    S = dy.shape[0]
    do = matmul(dy, w_out, "nt", F32, "dsw_out_dx", col_shards=N_SHARD)
    d_w_out = matmul(sv["o"], dy, "tn", F32, "dsw_out_dw", col_shards=N_SHARD)
    dods, statds = dsw_bwd_prep(do, sv["o"], sv["lse"], S)
    pieces_q, pieces_k, pieces_v, dbs = [], [], [], []
    d_qg = jnp.zeros((1, LANES), F32)
    d_kg = jnp.zeros((1, LANES), F32)
    for gi in range(3):
        qd, kd, vd = sv["qkv"][gi]
        dqd, dkd, dvd, db = dsw_attn_bwd(qd, kd, vd, sv["bias"], dods[gi], statds[gi], gi, S)
        dq, dk, dv, dqg, dkg = dsw_prep_bwd(sv["proj"], q_gain2, k_gain2, dqd, dkd, dvd, gi, S)
        dbs.append(db)
        pieces_q.append(dq)
        pieces_k.append(dk)
        pieces_v.append(dv)
        d_qg = d_qg + dqg
        d_kg = d_kg + dkg
    dproj = jnp.concatenate(pieces_q + pieces_k + pieces_v, axis=1)
    d_w_in = matmul(sv["h"], dproj, "tn", F32, "dsw_in_dw", col_shards=N_SHARD)
    dh = matmul(dproj, w_in, "nt", BF16, "dsw_in_dx", col_shards=N_SHARD)
    d_rel = dsw_bias_grad(jnp.concatenate(dbs, axis=0))
    return dh, dict(w_in=d_w_in, q_gain2=d_qg, k_gain2=d_kg, rel=d_rel, w_out=d_w_out)


FT = 256


FUSE_M = 512


def ffn_in_act(h, w_in, name):
    S = h.shape[0]
    half = FFN // 2

    def body(h_ref, wg_ref, wu_ref, gu_ref, a_ref):
        j = pl.program_id(1)
        sub = FUSE_M // 2
        for part in range(2):
            rows = slice(part * sub, (part + 1) * sub)
            hb = h_ref[rows, :]
            g = jnp.dot(hb, wg_ref[...], preferred_element_type=F32)
            u = jnp.dot(hb, wu_ref[...], preferred_element_type=F32)
            a_ref[rows, :] = (_silu(g) * u).astype(a_ref.dtype)
            for jj in range(2):
                @pl.when(j == jj)
                def _(g=g, u=u, jj=jj, rows=rows):
                    gu_ref[rows, jj * half:(jj + 1) * half] = g.astype(gu_ref.dtype)
                    gu_ref[rows, FFN + jj * half:FFN + (jj + 1) * half] = u.astype(gu_ref.dtype)

    return pl.pallas_call(
        body, name=name, grid=(S // FUSE_M, 2),
        in_specs=[pl.BlockSpec((FUSE_M, D), lambda i, j: (i, 0)),
                  pl.BlockSpec((None, D, half), lambda i, j: (j, 0, 0)),
                  pl.BlockSpec((None, D, half), lambda i, j: (j + 2, 0, 0))],
        out_specs=[pl.BlockSpec((FUSE_M, 2 * FFN), lambda i, j: (i, 0)), pl.BlockSpec((FUSE_M, half), lambda i, j: (i, j))],
        out_shape=[SDS((S, 2 * FFN), BF16), SDS((S, FFN), BF16)],
        compiler_params=_cp(2),
    )(h, w_in, w_in)


def ffn_forward(h, w_in, w_out, tag):
    gu, a = ffn_in_act(h, w_in, f"ffn_in_act_{tag}")
    gu_row = Row(gu, (FT, 2 * FFN), lambda i: (i, 0), splits=[FFN, FFN], gdtype=BF16)
    f = matmul(a, w_out, "nn", BF16, f"ffn_out_{tag}")
    return f, dict(h=h, gu_row=gu_row, a=a)


def ffn_out_dx_act(df, w_out, gu, name):
    S = df.shape[0]
    half = FFN // 2

    def body(df_ref, w_ref, g_ref, u_ref, dgu_ref):
        j = pl.program_id(1)
        sub = FUSE_M // 2
        for part in range(2):
            rows = slice(part * sub, (part + 1) * sub)
            da = _nt(df_ref[rows, :], w_ref[...])
            dg, du = _swiglu_bwd((g_ref[rows, :].astype(F32), u_ref[rows, :].astype(F32)), da)
            for jj in range(2):
                @pl.when(j == jj)
                def _(dg=dg, du=du, jj=jj, rows=rows):
                    dgu_ref[rows, jj * half:(jj + 1) * half] = dg.astype(dgu_ref.dtype)
                    dgu_ref[rows, FFN + jj * half:FFN + (jj + 1) * half] = du.astype(dgu_ref.dtype)

    return pl.pallas_call(
        body, name=name, grid=(S // FUSE_M, 2),
        in_specs=[pl.BlockSpec((FUSE_M, D), lambda i, j: (i, 0)),
                  pl.BlockSpec((half, D), lambda i, j: (j, 0)),
                  pl.BlockSpec((FUSE_M, half), lambda i, j: (i, j)),
                  pl.BlockSpec((FUSE_M, half), lambda i, j: (i, j + 2))],
        out_specs=pl.BlockSpec((FUSE_M, 2 * FFN), lambda i, j: (i, 0)),
        out_shape=SDS((S, 2 * FFN), BF16),
        compiler_params=_cp(2),
    )(df, w_out, gu, gu)


def ffn_backward(df, sv, w_in, w_out, tag):
    d_w_out = matmul(sv["a"], df, "tn", F32, f"ffn_out_dw_{tag}")
    dgu = ffn_out_dx_act(df, w_out, sv["gu_row"].arr, f"ffn_out_dx_act_{tag}")
    d_w_in = matmul(sv["h"], dgu, "tn", F32, f"ffn_in_dw_{tag}", col_shards=N_SHARD)
    dh = matmul(dgu, w_in, "nt", BF16, f"ffn_in_dx_{tag}", col_shards=N_SHARD)
    return dh, d_w_in, d_w_out


def f_norm_only(ids, x, gain, sc, sh):
    return (_normmod(x, gain, sc, sh),)


WT = 512


def _wide(a, **kw):
    return Row(a, (WT, D), lambda i: (i, 0), **kw)


def _wide_out(S, dtype):
    return Out((S, D), dtype, (WT, D), lambda i: (i, 0))


def adamw(w, g, m, v, name):
    shape = w.shape
    C = shape[-1]
    R = int(np.prod(shape[:-1]))
    w2, g2, m2, v2 = (a.reshape(R, C) for a in (w, g, m, v))
    br = R
    if R > 256:
        br = max(b for b in range(8, 257, 8) if R % b == 0)
    c1 = 1.0 / (1.0 - ADAM_B1 ** ADAM_STEP)
    c2 = 1.0 / (1.0 - ADAM_B2 ** ADAM_STEP)

    def body(w_ref, g_ref, m_ref, v_ref, d_ref, nm_ref, nv_ref):
        gg = g_ref[...]
        mm_ = ADAM_B1 * m_ref[...] + (1.0 - ADAM_B1) * gg
        vv = ADAM_B2 * v_ref[...] + (1.0 - ADAM_B2) * (gg * gg)
        d_ref[...] = -ADAM_LR * ((mm_ * c1) / (jnp.sqrt(vv * c2) + ADAM_EPS) + ADAM_WD * w_ref[...])
        nm_ref[...] = mm_
        nv_ref[...] = vv

    spec = pl.BlockSpec((br, C), lambda i: (i, 0))
    d, nm, nv = pl.pallas_call(
        body, name=name, grid=(R // br,), in_specs=[spec] * 4, out_specs=[spec] * 3,
        out_shape=[SDS((R, C), F32)] * 3, compiler_params=_cp(1),
    )(w2, g2, m2, v2)
    return d.reshape(shape), nm.reshape(shape), nv.reshape(shape)


def _place():
    x, y, c = lax.axis_index("x"), lax.axis_index("y"), lax.axis_index("c")
    chips = [(1 - x, y), (x, 1 - y), (1 - x, 1 - y)]
    return x, y, c, chips


def all_gather_small(blk, name):
    m_per, n = blk.shape

    def body(x_ref, out_ref, send_sems, recv_sems, local_sem):
        x, y, c, chips = _place()
        me, sibling = (x, y, c), (x, y, 1 - c)

        def rows(px, py, pc):
            return out_ref.at[pl.ds((4 * px + 2 * py + pc) * m_per, m_per), :]

        def copy(k, block, to, src=None):
            return pltpu.make_async_remote_copy(
                src_ref=rows(*block) if src is None else src, dst_ref=rows(*block),
                send_sem=send_sems.at[k], recv_sem=recv_sems.at[k], device_id=to, device_id_type=MESH)

        mine = pltpu.make_async_copy(x_ref, rows(*me), local_sem)
        mine.start()
        first = [copy(0, me, sibling, src=x_ref)]
        first += [copy(1 + j, me, (*chip, c), src=x_ref) for j, chip in enumerate(chips)]
        for cp in first:
            cp.start()
        passed = [copy(4 + j, (*chip, c), sibling) for j, chip in enumerate(chips)]
        for j, chip in enumerate(chips):
            copy(1 + j, (*chip, c), me).wait_recv()
            passed[j].start()
        copy(0, sibling, me).wait_recv()
        for j, chip in enumerate(chips):
            copy(4 + j, (*chip, 1 - c), me).wait_recv()
        for cp in first + passed:
            cp.wait_send()
        mine.wait()

    return pl.pallas_call(
        body, name=name, out_shape=SDS((N_DEV * m_per, n), blk.dtype),
        in_specs=[pl.BlockSpec(memory_space=pltpu.VMEM)], out_specs=pl.BlockSpec(memory_space=pltpu.VMEM),
        scratch_shapes=[pltpu.SemaphoreType.DMA((7,)), pltpu.SemaphoreType.DMA((7,)), pltpu.SemaphoreType.DMA],
    )(blk)


def _half(cc, rh):
    return pl.ds(pl.multiple_of(cc * rh, 16), rh)


def all_gather_shards(ws):
    n = len(ws)

    def body(*refs):
        w_refs, out_refs = refs[:n], refs[n:2 * n]
        send_sems, recv_sems, local_sems, own_sems = refs[2 * n:]
        x, y, c, chips = _place()
        sibling = (x, y, 1 - c)
        s_me = 2 * x + y

        def copy(k, src, dst, to):
            return pltpu.make_async_remote_copy(src_ref=src, dst_ref=dst, send_sem=send_sems.at[k], recv_sem=recv_sems.at[k],
                                                device_id=to, device_id_type=MESH)

        local, sends, passed = [], [], []
        for k in range(n):
            rh = ws[k].shape[0] // 2
            cp = pltpu.make_async_remote_copy(src_ref=w_refs[k], dst_ref=out_refs[k].at[s_me], send_sem=local_sems.at[k],
                                              recv_sem=own_sems.at[k], device_id=sibling, device_id_type=MESH)
            cp.start()
            local.append(cp)
            for j, chip in enumerate(chips):
                sd = copy(6 * k + j, w_refs[k].at[_half(c, rh)], out_refs[k].at[s_me, _half(c, rh)], (*chip, c))
                sd.start()
                sends.append(sd)
        for k in range(n):
            rh = ws[k].shape[0] // 2
            for j, (px, py) in enumerate(chips):
                got = out_refs[k].at[2 * px + py, _half(c, rh)]
                copy(6 * k + j, got, got, (px, py, c)).wait_recv()
                fw = copy(6 * k + 3 + j, got, got, sibling)
                fw.start()
                passed.append(fw)
        for k in range(n):
            rh = ws[k].shape[0] // 2
            for j, (px, py) in enumerate(chips):
                got = out_refs[k].at[2 * px + py, _half(1 - c, rh)]
                copy(6 * k + 3 + j, got, got, sibling).wait_recv()
        for cp in sends + passed:
            cp.wait_send()
        for cp in local:
            cp.wait()

    return pl.pallas_call(
        body, name="weights_all_gather", out_shape=[SDS((N_SHARD,) + w.shape, w.dtype) for w in ws],
        in_specs=[ANY] * n, out_specs=[ANY] * n,
        scratch_shapes=[pltpu.SemaphoreType.DMA((6 * n,)), pltpu.SemaphoreType.DMA((6 * n,)), pltpu.SemaphoreType.DMA((n,)),
                        pltpu.SemaphoreType.DMA((n,))],
    )(*ws)


def sibling_exchange(sends, name):
    n = len(sends)

    def body(*refs):
        s_refs, o_refs, send_sems, recv_sems = refs[:n], refs[n:2 * n], refs[2 * n], refs[2 * n + 1]
        x, y, c, _ = _place()
        cps = [pltpu.make_async_remote_copy(src_ref=s_refs[k], dst_ref=o_refs[k], send_sem=send_sems.at[k], recv_sem=recv_sems.at[k],
                                            device_id=(x, y, 1 - c), device_id_type=MESH) for k in range(n)]
        for cp in cps:
            cp.start()
        for cp in cps:
            cp.wait()

    return pl.pallas_call(
        body, name=name, out_shape=[SDS(s.shape, s.dtype) for s in sends], in_specs=[ANY] * n, out_specs=[ANY] * n,
        scratch_shapes=[pltpu.SemaphoreType.DMA((n,)), pltpu.SemaphoreType.DMA((n,))],
    )(*sends)


def scatter_to_chips(parts):
    n = len(parts)

    def body(*refs):
        p_refs, o_refs, send_sems, recv_sems = refs[:n], refs[n:2 * n], refs[2 * n], refs[2 * n + 1]
        x, y, c, chips = _place()
        cps = []
        for k in range(n):
            for j, (px, py) in enumerate(chips):
                cp = pltpu.make_async_remote_copy(src_ref=p_refs[k].at[2 * px + py], dst_ref=o_refs[k].at[j],
                                                  send_sem=send_sems.at[3 * k + j], recv_sem=recv_sems.at[3 * k + j],
                                                  device_id=(px, py, c), device_id_type=MESH)
                cp.start()
                cps.append(cp)
        for cp in cps:
            cp.wait()

    return pl.pallas_call(
        body, name="grads_scatter", out_shape=[SDS((3,) + p.shape[1:], p.dtype) for p in parts], in_specs=[ANY] * n, out_specs=[ANY] * n,
        scratch_shapes=[pltpu.SemaphoreType.DMA((3 * n,)), pltpu.SemaphoreType.DMA((3 * n,))],
    )(*parts)


def merge_halves(halves):
    n = len(halves)

    def body(*refs):
        h_refs, o_refs = refs[:n], refs[n:2 * n]
        send_sems, recv_sems, local_sems = refs[2 * n:]
        x, y, c, _ = _place()
        local, cps = [], []
        for k in range(n):
            rh = halves[k].shape[0]
            lc = pltpu.make_async_copy(h_refs[k], o_refs[k].at[_half(c, rh)], local_sems.at[k])
            lc.start()
            local.append(lc)
            cp = pltpu.make_async_remote_copy(src_ref=h_refs[k], dst_ref=o_refs[k].at[_half(c, rh)], send_sem=send_sems.at[k],
                                              recv_sem=recv_sems.at[k], device_id=(x, y, 1 - c), device_id_type=MESH)
            cp.start()
            cps.append(cp)
        for k in range(n):
            rh = halves[k].shape[0]
            got = o_refs[k].at[_half(1 - c, rh)]
            pltpu.make_async_remote_copy(src_ref=got, dst_ref=got, send_sem=send_sems.at[k], recv_sem=recv_sems.at[k],
                                         device_id=(x, y, 1 - c), device_id_type=MESH).wait_recv()
        for cp in cps:
            cp.wait_send()
        for lc in local:
            lc.wait()

    return pl.pallas_call(
        body, name="grads_merge_halves", out_shape=[SDS((2 * h.shape[0], h.shape[1]), h.dtype) for h in halves],
        in_specs=[ANY] * n, out_specs=[ANY] * n,
        scratch_shapes=[pltpu.SemaphoreType.DMA((n,)), pltpu.SemaphoreType.DMA((n,)), pltpu.SemaphoreType.DMA((n,))],
    )(*halves)


def add_rows(arrs, out_dtype, name, rt=256):
    Rr, W = arrs[0].shape

    def fn(ids, *vals):
        acc = vals[0]
        for v in vals[1:]:
            acc = acc + v
        return (acc,)

    t = rt if Rr % rt == 0 else max(b for b in range(16, rt + 1, 16) if Rr % b == 0)
    (out,) = rowwise(fn, [Row(a, (t, W), lambda i: (i, 0)) for a in arrs], [],
                     [Out((Rr, W), out_dtype, (t, W), lambda i: (i, 0))], (Rr // t,), name)
    return out


HBM_SPEC = pl.BlockSpec(memory_space=pltpu.HBM)
SEM_SPEC = pl.BlockSpec(memory_space=pltpu.SEMAPHORE)
DATAFLOW = pltpu.SideEffectType.DATAFLOW_SIDE_EFFECTING


def _in_hbm(a):
    return pltpu.with_memory_space_constraint(a, pltpu.HBM)


def _gather_copies(w_refs, land_refs, send_sems, recv_sems):
    x, y, c, chips = _place()
    targets = [(x, y, 1 - c)] + [(*chip, c) for chip in chips]
    cps = []
    for k, (w_ref, land_ref) in enumerate(zip(w_refs, land_refs)):
        for j, to in enumerate(targets):
            cps.append(pltpu.make_async_remote_copy(src_ref=w_ref, dst_ref=land_ref.at[2 * x + y], send_sem=send_sems.at[4 * k + j],
                                                    recv_sem=recv_sems.at[4 * k + j], device_id=to, device_id_type=MESH))
    return cps


def _scatter_copies(p_refs, land_refs, send_sems, recv_sems):
    x, y, c, chips = _place()
    cps = []
    for k, (p_ref, land_ref) in enumerate(zip(p_refs, land_refs)):
        for j, (px, py) in enumerate(chips):
            cps.append(pltpu.make_async_remote_copy(src_ref=p_ref.at[2 * px + py], dst_ref=land_ref.at[j], send_sem=send_sems.at[3 * k + j],
                                                    recv_sem=recv_sems.at[3 * k + j], device_id=(px, py, c), device_id_type=MESH))
    return cps


def copies_start(srcs, land_shapes, make_copies, per_src, name):
    n = len(srcs)
    m = per_src * n

    def body(*refs):
        src_refs, land_refs = refs[:n], refs[n:2 * n]
        send_sems, recv_sems, token = refs[2 * n], refs[2 * n + 1], refs[-1]
        for cp in make_copies(src_refs, land_refs, send_sems, recv_sems):
            cp.start()
        token[...] = jnp.zeros_like(token)

    lands = [lax.empty(shp, s.dtype) for shp, s in zip(land_shapes, srcs)]
    res = pl.pallas_call(
        body, name=name,
        out_shape=(pltpu.SemaphoreType.DMA((m,)), pltpu.SemaphoreType.DMA((m,)), *[pltpu.HBM(s.shape, s.dtype) for s in srcs],
                   *[pltpu.HBM(shp, s.dtype) for shp, s in zip(land_shapes, srcs)], SDS((8, LANES), F32)),
        in_specs=[HBM_SPEC] * (2 * n),
        out_specs=(SEM_SPEC, SEM_SPEC, *[HBM_SPEC] * (2 * n), pl.BlockSpec(memory_space=pltpu.VMEM)),
        input_output_aliases={i: 2 + i for i in range(2 * n)},
        compiler_params=pltpu.CompilerParams(has_side_effects=DATAFLOW),
    )(*[_in_hbm(s) for s in srcs], *[_in_hbm(l) for l in lands])
    return res[0], res[1], list(res[2:2 + n]), list(res[2 + n:2 + 2 * n]), res[-1]


def copies_wait(send_sems, recv_sems, srcs, lands, make_copies, after, name):
    n = len(srcs)

    def body(*refs):
        src_refs, land_refs = refs[:n], refs[n:2 * n]
        for cp in make_copies(src_refs, land_refs, refs[2 * n], refs[2 * n + 1]):
            cp.wait_send()
            cp.wait_recv()

    res = pl.pallas_call(
        body, name=name,
        out_shape=(*[pltpu.HBM(s.shape, s.dtype) for s in srcs], *[pltpu.HBM(l.shape, l.dtype) for l in lands]),
        in_specs=[HBM_SPEC] * (2 * n) + [SEM_SPEC, SEM_SPEC, ANY],
        out_specs=tuple([HBM_SPEC] * (2 * n)),
        input_output_aliases={i: i for i in range(2 * n)},
        compiler_params=pltpu.CompilerParams(has_side_effects=DATAFLOW),
    )(*srcs, *lands, send_sems, recv_sems, after)
    return list(res[n:])


PACK = (("gdn_w_in", 2), ("gdn_w_out", 1), ("w_ffn_in", 2), ("w_ffn_out", 1), ("dsw_w_in", 2), ("dsw_w_out", 2))
PACK_ALIGN = 32


def _pack_rows(sizes):
    total = sum(sizes)
    rows = -(-total // D)
    return -(-rows // PACK_ALIGN) * PACK_ALIGN


def pack_blocks(blocks, dtype):
    flat = [b.astype(dtype).reshape(-1) for b in blocks]
    total = sum(f.shape[0] for f in flat)
    R = _pack_rows([f.shape[0] for f in flat])
    flat.append(jnp.zeros((R * D - total,), dtype))
    return jnp.concatenate(flat).reshape(R, D)


def unpack_blocks(buf, shapes):
    flat = buf.reshape(-1)
    out, off = [], 0
    for shp in shapes:
        n = int(np.prod(shp))
        out.append(flat[off:off + n].reshape(shp))
        off += n
    return out


def _shard_slice(a, axis, s):
    n = a.shape[axis] // N_SHARD
    return lax.slice_in_dim(a, s * n, (s + 1) * n, axis=axis)


def _pad_lanes(v):
    return jnp.concatenate([v.astype(F32), jnp.zeros((LANES - v.shape[0],), F32)])[None]


def kernel(x, c, w_ada, b_ada, norm_mix, norm_ffn, w_ffn_in, w_ffn_out, gdn_w_in, gdn_conv, gdn_a_log, gdn_dt_bias, gdn_out_norm, gdn_w_out, dsw_w_in, dsw_q_norm, dsw_k_norm, dsw_w_out, rel_bias, loss_target, m_w_ada, m_b_ada, m_norm_mix, m_norm_ffn, m_w_ffn_in, m_w_ffn_out, m_gdn_w_in, m_gdn_conv, m_gdn_a_log, m_gdn_dt_bias, m_gdn_out_norm, m_gdn_w_out, m_dsw_w_in, m_dsw_q_norm, m_dsw_k_norm, m_dsw_w_out, m_rel_bias, v_w_ada, v_b_ada, v_norm_mix, v_norm_ffn, v_w_ffn_in, v_w_ffn_out, v_gdn_w_in, v_gdn_conv, v_gdn_a_log, v_gdn_dt_bias, v_gdn_out_norm, v_gdn_w_out, v_dsw_w_in, v_dsw_q_norm, v_dsw_k_norm, v_dsw_w_out, v_rel_bias):
    S = x.shape[1]
    nt = S // WT
    xi, yi, ci = lax.axis_index("x"), lax.axis_index("y"), lax.axis_index("c")
    me = 4 * xi + 2 * yi + ci
    s_me = 2 * xi + yi
    x0, tgt = x[0], loss_target[0]
    shard = dict(w_ffn_in=w_ffn_in, w_ffn_out=w_ffn_out, gdn_w_in=gdn_w_in, gdn_w_out=gdn_w_out, dsw_w_in=dsw_w_in, dsw_w_out=dsw_w_out)

    whole = lambda a: Row(a, a.shape, lambda i: (0,) * a.ndim)
    (cond8,) = rowwise(lambda ids, v: (_silu(v),), [whole(c.reshape(8, LANES))], [], [Out((8, LANES), F32, (8, LANES), lambda i: (0, 0))], (1,), "cond")
    cond_all = all_gather_small(cond8, "gather_cond").reshape(N_DEV, D)
    cond16 = jnp.concatenate([cond_all, jnp.zeros((8, D), F32)], axis=0)
    ada_cols = w_ada.shape[2]
    mods = [matmul(cond16, w_ada[l], "nn", F32, f"ada_{l}")[:N_DEV] for l in range(2)]
    buf = jnp.concatenate([jnp.stack(mods, axis=1).reshape(-1, LANES), gdn_conv.reshape(-1, LANES)], axis=0)
    n_mod_rows = N_DEV * 2 * ada_cols // LANES
    got = all_gather_small(buf, "gather_mod").reshape(N_DEV, buf.shape[0], LANES)
    mod_parts, conv_parts = [], []
    for s in range(N_SHARD):
        from_dev = got[2 * s]
        mod_parts.append(lax.dynamic_index_in_dim(from_dev[:n_mod_rows].reshape(N_DEV, 2, ada_cols), me, 0, keepdims=False))
        conv_parts.append(from_dev[n_mod_rows:].reshape(4, -1))
    mod_nb = jnp.concatenate(mod_parts, axis=1)
    conv_w = jnp.concatenate(conv_parts, axis=1)
    (mod,) = rowwise(lambda ids, a, b: (a + b,), [whole(mod_nb), whole(b_ada)], [], [Out(mod_nb.shape, F32, mod_nb.shape, lambda i: (0, 0))], (1,), "mod_bias")
    mod = mod.reshape(2, 6, 1, D)
    sh1, sc1, g1, sh2, sc2, g2 = ([mod[l, k] for l in range(2)] for k in range(6))
    gmix = [norm_mix[l][None] for l in range(2)]
    gffn = [norm_ffn[l][None] for l in range(2)]

    gcols = gdn_w_in.shape[2]
    g_gdn_in, g_gdn_out = all_gather_shards([gdn_w_in[0].astype(BF16), gdn_w_out[0].astype(BF16)])
    gathered = lambda ws: [(N_SHARD,) + w.shape for w in ws]
    gate = (jnp.minimum(jnp.abs(g_gdn_in[0, 0, 0].astype(F32)), 0.0) + jnp.minimum(jnp.abs(mod[0, 0, 0, 0]), 0.0)).astype(BF16)
    w2 = [w_ffn_in[0].astype(BF16) + gate, w_ffn_out[0].astype(BF16) + gate]
    w3 = [dsw_w_in[0].astype(BF16) + gate, dsw_w_out[0].astype(BF16) + gate, w_ffn_in[1].astype(BF16) + gate, w_ffn_out[1].astype(BF16) + gate]
    fly2 = copies_start(w2, gathered(w2), _gather_copies, 4, "weights_ffn0_start")
    fly3 = copies_start(w3, gathered(w3), _gather_copies, 4, "weights_layer1_start")
    started = fly2[4][0, 0] + fly3[4][0, 0]
    w_gdn = jnp.concatenate([g_gdn_in[s] for s in range(N_SHARD)] + [jnp.zeros((D, GDN_PROJ - N_SHARD * gcols), BF16)], axis=1)
    alog, dtb = _pad_lanes(gdn_a_log[0]), _pad_lanes(gdn_dt_bias[0])
    qg2 = jnp.concatenate([dsw_q_norm, dsw_q_norm], axis=1)
    kg2 = jnp.concatenate([dsw_k_norm, dsw_k_norm], axis=1)
    w_gdn_out = g_gdn_out.reshape(GDN_H * LANES, D)
    gdn_args = (w_gdn, conv_w, alog, dtb, gdn_out_norm, w_gdn_out)
    sc1[0] = sc1[0] + started

    (h10,) = rowwise(f_norm_only, [_wide(x0)], [gmix[0], sc1[0], sh1[0]], [_wide_out(S, BF16)], (nt,), "l0_norm")
    y0, sv_g = gdn_forward(h10, *gdn_args)
    x1, h20 = rowwise(f_resid_norm, [_wide(x0), _wide(y0)], [g1[0], gffn[0], sc2[0], sh2[0]], [_wide_out(S, F32), _wide_out(S, BF16)], (nt,), "l0_mid")
    g_in0, g_out0 = copies_wait(*fly2[:4], _gather_copies, y0, "weights_ffn0_wait")
    w_ffn = [(g_in0, g_out0.reshape(FFN, D)), None]
    f0, sv_f0 = ffn_forward(h20, *w_ffn[0], "0")
    x2, h11 = rowwise(f_resid_norm, [_wide(x1), _wide(f0)], [g2[0], gmix[1], sc1[1], sh1[1]], [_wide_out(S, F32), _wide_out(S, BF16)], (nt,), "l1_in")
    g_dsw_in, g_dsw_out, g_in1, g_out1 = copies_wait(*fly3[:4], _gather_copies, f0, "weights_layer1_wait")
    w_ffn[1] = (g_in1, g_out1.reshape(FFN, D))
    dsw_args = (g_dsw_in, qg2, kg2)
    y1, sv_d = dsw_forward(h11, *dsw_args, rel_bias, g_dsw_out)
    x3, h21 = rowwise(f_resid_norm, [_wide(x2), _wide(y1)], [g1[1], gffn[1], sc2[1], sh2[1]], [_wide_out(S, F32), _wide_out(S, BF16)], (nt,), "l1_mid")
    f1, sv_f1 = ffn_forward(h21, *w_ffn[1], "1")
    part_spec = lambda a: Row(a, (None, 1, D), lambda i: (i, 0, 0))
    (parts,) = rowwise(f_loss, [_wide(x3), _wide(f1), _wide(tgt)], [g2[1]], [Out((nt, 1, D), F32, (None, 1, D), lambda i: (i, 0, 0))], (nt,), "loss")
    loss = lax.psum(jnp.sum(parts), ("x", "y", "c"))

    (dx3, df1), (dg2_1,) = rowwise_bwd(f_loss, [_wide(x3), _wide(f1, gdtype=BF16), _wide(tgt, diff=False)], [g2[1]],
                                       [part_spec(jnp.ones((nt, 1, D), F32))], (nt,), "loss_bwd")
    dh21, d_win1, d_wout1 = ffn_backward(df1, sv_f1, *w_ffn[1], "1")
    (dx2, dy1), (dg1_1, dgf1, dsc2_1, dsh2_1) = rowwise_bwd(
        f_resid_norm, [_wide(x2), _wide(y1, gdtype=BF16)], [g1[1], gffn[1], sc2[1], sh2[1]], [_wide(dx3), _wide(dh21)], (nt,), "l1_mid_bwd")
    dh11, g_d = dsw_backward(dy1, sv_d, *dsw_args, g_dsw_out)
    (dx1, df0), (dg2_0, dgm1, dsc1_1, dsh1_1) = rowwise_bwd(
        f_resid_norm, [_wide(x1), _wide(f0, gdtype=BF16)], [g2[0], gmix[1], sc1[1], sh1[1]], [_wide(dx2), _wide(dh11)], (nt,), "l1_in_bwd")
    by_shard = lambda a: a.reshape(N_SHARD, a.shape[0] // N_SHARD, a.shape[1])
    landing = lambda ps: [(3,) + p.shape[1:] for p in ps]
    dws3 = [g_d["w_in"], g_d["w_out"], d_win1, by_shard(d_wout1)]
    parts3 = [a.astype(BF16) for a in dws3]
    gfly3 = copies_start(parts3, landing(parts3), _scatter_copies, 3, "grads_layer1_start")
    w_out0 = w_ffn[0][1] + gfly3[4][0, 0].astype(BF16)
    dh20, d_win0, d_wout0 = ffn_backward(df0, sv_f0, w_ffn[0][0], w_out0, "0")
    (dx0p, dy0), (dg1_0, dgf0, dsc2_0, dsh2_0) = rowwise_bwd(
        f_resid_norm, [_wide(x0), _wide(y0, gdtype=BF16)], [g1[0], gffn[0], sc2[0], sh2[0]], [_wide(dx1), _wide(dh20)], (nt,), "l0_mid_bwd")
    dws2 = [d_win0, by_shard(d_wout0)]
    parts2 = [a.astype(BF16) for a in dws2]
    gfly2 = copies_start(parts2, landing(parts2), _scatter_copies, 3, "grads_ffn0_start")
    gdn_args = gdn_args[:5] + (w_gdn_out + gfly2[4][0, 0].astype(BF16),)
    dh10, g_g = gdn_backward(dy0, sv_g, *gdn_args)
    (grad_x,), (dgm0, dsc1_0, dsh1_0) = rowwise_bwd(f_first, [_wide(x0)], [gmix[0], sc1[0], sh1[0]], [_wide(dx0p), _wide(dh10)], (nt,), "l0_norm_bwd")

    dmod = jnp.concatenate([dsh1_0, dsc1_0, dg1_0, dsh2_0, dsc2_0, dg2_0, dsh1_1, dsc1_1, dg1_1, dsh2_1, dsc2_1, dg2_1], axis=1)
    d_rel = jnp.transpose(g_d["rel"][:, :, 0])
    fold = lambda v: v[:, :DSW_DH] + v[:, DSW_DH:]
    small = [dmod, jnp.concatenate([dgm0, dgm1], axis=1), jnp.concatenate([dgf0, dgf1], axis=1), g_g["conv"].reshape(1, -1),
             g_g["alog"], g_g["dtb"], g_g["gain"], _pad_lanes(fold(g_d["q_gain2"])[0]), _pad_lanes(fold(g_d["k_gain2"])[0]),
             d_rel.reshape(1, -1)]
    used = [v.shape[1] // LANES for v in small]
    sizes = [-(-u // 8) * 8 for u in used]
    pad8 = lambda v, u, s: jnp.concatenate([v.reshape(u, LANES), jnp.zeros((s - u, LANES), F32)], axis=0) if s > u else v.reshape(u, LANES)
    pad_rows = sum(sizes)
    sbuf = jnp.concatenate([pad8(v, u, s) for v, u, s in zip(small, used, sizes)], axis=0)
    sgot = all_gather_small(sbuf, "gather_small_grads")
    ssum = add_rows([sgot[d * pad_rows:(d + 1) * pad_rows] for d in range(N_DEV)], F32, "sum_small_grads", rt=pad_rows)
    offs = np.cumsum([0] + sizes)
    take = lambda k: ssum[offs[k]:offs[k] + used[k]].reshape(1, -1)
    grad_b_ada = take(0).reshape(2, 6 * D)
    grad_norm_mix = take(1).reshape(2, D)
    grad_norm_ffn = take(2).reshape(2, D)
    conv_full = take(3).reshape(4, -1)
    ncv = gdn_conv.shape[2]
    grad_gdn_conv = lax.dynamic_slice_in_dim(conv_full, s_me * ncv, ncv, axis=1)[None]
    grad_a_log = take(4)[:, :GDN_H]
    grad_dt_bias = take(5)[:, :GDN_H]
    grad_out_norm = take(6)
    grad_q_norm = take(7)[:, :DSW_DH]
    grad_k_norm = take(8)[:, :DSW_DH]
    grad_rel = take(9).reshape(REL_BUCKETS, 3 * GDN_H)
    dmod_all = sgot.reshape(N_DEV, pad_rows, LANES)[:, :used[0]].reshape(N_DEV, 2, 6 * D)
    dmod_mine = lax.dynamic_slice_in_dim(dmod_all, s_me * ada_cols, ada_cols, axis=2)
    dmod16 = jnp.concatenate([dmod_mine, jnp.zeros_like(dmod_mine)], axis=0)
    grad_w_ada = jnp.stack([matmul(cond16, dmod16[:, l], "tn", F32, f"ada_dw_{l}") for l in range(2)])

    dg_in = jnp.stack([g_g["w_in"][:, s * gcols:(s + 1) * gcols] for s in range(N_SHARD)])
    dws = [dg_in, by_shard(g_g["w_out"])]
    keeps, gives = [], []
    for a in dws:
        rh = a.shape[1] // 2
        keeps.append(lax.dynamic_slice_in_dim(a, ci * rh, rh, axis=1))
        gives.append(lax.dynamic_slice_in_dim(a, (1 - ci) * rh, rh, axis=1).astype(BF16))
    from_sib = sibling_exchange(gives, "grads_to_sibling")
    flat2 = lambda a: a.reshape(-1, a.shape[-1])
    parts = [add_rows([flat2(k_), flat2(f_)], BF16, f"grads_chip_sum_{i}").reshape(k_.shape) for i, (k_, f_) in enumerate(zip(keeps, from_sib))]
    others = scatter_to_chips(parts)
    halves = []
    for i, (p_, o_) in enumerate(zip(parts, others)):
        own = lax.dynamic_index_in_dim(p_, s_me, 0, keepdims=False)
        halves.append(add_rows([own, o_[0], o_[1], o_[2]], F32, f"grads_sum_{i}"))
    sib_halves = sibling_exchange(halves, "grads_from_sibling")

    def whole_shard(mine, theirs):
        both = jnp.stack([mine, theirs])
        return jnp.concatenate([lax.dynamic_index_in_dim(both, ci, 0, keepdims=False),
                                lax.dynamic_index_in_dim(both, 1 - ci, 0, keepdims=False)], axis=0)

    s_gdn_in, s_gdn_out = [whole_shard(a, b) for a, b in zip(halves, sib_halves)]
    got3 = copies_wait(*gfly3[:4], _scatter_copies, grad_x, "grads_layer1_wait")
    got2 = copies_wait(*gfly2[:4], _scatter_copies, grad_x, "grads_ffn0_wait")
    core_sums = []
    for i, (full, got) in enumerate(zip(dws3 + dws2, got3 + got2)):
        own = lax.dynamic_index_in_dim(full, s_me, 0, keepdims=False)
        core_sums.append(add_rows([own, got[0], got[1], got[2]], F32, f"grads_core_sum_{i}"))
    sib_sums = sibling_exchange(core_sums, "grads_core_sums_swap")
    s_dsw_in, s_dsw_out, s_in1, s_out1, s_in0, s_out0 = [add_rows([a, b], F32, f"grads_chip_total_{i}")
                                                          for i, (a, b) in enumerate(zip(core_sums, sib_sums))]
    gsh = dict(gdn_w_in=s_gdn_in[None], gdn_w_out=s_gdn_out[None],
               w_ffn_in=jnp.stack([s_in0, s_in1]), w_ffn_out=jnp.stack([s_out0, s_out1]),
               dsw_w_in=s_dsw_in[None], dsw_w_out=s_dsw_out[None])

    grads = dict(w_ada=grad_w_ada, b_ada=grad_b_ada, norm_mix=grad_norm_mix, norm_ffn=grad_norm_ffn, w_ffn_in=gsh["w_ffn_in"],
                 w_ffn_out=gsh["w_ffn_out"], gdn_w_in=gsh["gdn_w_in"], gdn_conv=grad_gdn_conv, gdn_a_log=grad_a_log,
                 gdn_dt_bias=grad_dt_bias, gdn_out_norm=grad_out_norm, gdn_w_out=gsh["gdn_w_out"], dsw_w_in=gsh["dsw_w_in"],
                 dsw_q_norm=grad_q_norm, dsw_k_norm=grad_k_norm, dsw_w_out=gsh["dsw_w_out"], rel_bias=grad_rel)
    weights = dict(w_ada=w_ada, b_ada=b_ada, norm_mix=norm_mix, norm_ffn=norm_ffn, w_ffn_in=w_ffn_in, w_ffn_out=w_ffn_out,
                   gdn_w_in=gdn_w_in, gdn_conv=gdn_conv, gdn_a_log=gdn_a_log, gdn_dt_bias=gdn_dt_bias, gdn_out_norm=gdn_out_norm,
                   gdn_w_out=gdn_w_out, dsw_w_in=dsw_w_in, dsw_q_norm=dsw_q_norm, dsw_k_norm=dsw_k_norm, dsw_w_out=dsw_w_out,
                   rel_bias=rel_bias)
    ms = dict(w_ada=m_w_ada, b_ada=m_b_ada, norm_mix=m_norm_mix, norm_ffn=m_norm_ffn, w_ffn_in=m_w_ffn_in, w_ffn_out=m_w_ffn_out,
              gdn_w_in=m_gdn_w_in, gdn_conv=m_gdn_conv, gdn_a_log=m_gdn_a_log, gdn_dt_bias=m_gdn_dt_bias, gdn_out_norm=m_gdn_out_norm,
              gdn_w_out=m_gdn_w_out, dsw_w_in=m_dsw_w_in, dsw_q_norm=m_dsw_q_norm, dsw_k_norm=m_dsw_k_norm, dsw_w_out=m_dsw_w_out,
              rel_bias=m_rel_bias)
    vs = dict(w_ada=v_w_ada, b_ada=v_b_ada, norm_mix=v_norm_mix, norm_ffn=v_norm_ffn, w_ffn_in=v_w_ffn_in, w_ffn_out=v_w_ffn_out,
              gdn_w_in=v_gdn_w_in, gdn_conv=v_gdn_conv, gdn_a_log=v_gdn_a_log, gdn_dt_bias=v_gdn_dt_bias, gdn_out_norm=v_gdn_out_norm,
              gdn_w_out=v_gdn_w_out, dsw_w_in=v_dsw_w_in, dsw_q_norm=v_dsw_q_norm, dsw_k_norm=v_dsw_k_norm, dsw_w_out=v_dsw_w_out,
              rel_bias=v_rel_bias)
    names = list(weights)
    deltas, new_m, new_v = [], [], []
    for n in names:
        g = grads[n].reshape(weights[n].shape)
        grads[n] = g
        d, nm, nv = adamw(weights[n], g, ms[n], vs[n], f"adamw_{n}")
        deltas.append(d)
        new_m.append(nm)
        new_v.append(nv)
    return (loss, grad_x[None], *[grads[n] for n in names], *deltas, *new_m, *new_v)
```

```python
import functools
import math

import numpy as np
import jax
import jax.numpy as jnp
from jax import lax
from jax.experimental import pallas as pl
from jax.experimental.pallas import tpu as pltpu

F32 = jnp.float32
BF16 = jnp.bfloat16
SDS = jax.ShapeDtypeStruct
MESH = pl.DeviceIdType.MESH
ANY = pl.BlockSpec(memory_space=pl.ANY)

D = 1024
EPS = 1e-6
LANES = 128
GDN_H = 8
GDN_DK = 128
GDN_C = 64
DSW_GROUPS = ((128, 1), (512, 4), (2048, 16))
DSW_SPAN = 128
DSW_DH = 64
DSW_HG = 512
REL_BUCKETS = 32
REL_MAX_DIST = 2048
FFN = 2816
N_SHARD = 4
N_DEV = 8
VMEM_LIMIT = 48 * 1024 * 1024
NEG = -1e30

ADAM_LR, ADAM_B1, ADAM_B2, ADAM_EPS, ADAM_WD, ADAM_STEP = 0.001, 0.9, 0.999, 1e-08, 0.01, 10


def _cp(n_axes):
    return pltpu.CompilerParams(dimension_semantics=("arbitrary",) * n_axes, vmem_limit_bytes=VMEM_LIMIT)


def _blk(dim, cap):
    if dim <= cap:
        return dim
    best = None
    for b in range(LANES, cap + 1, LANES):
        if dim % b == 0:
            best = b
    assert best is not None, (dim, cap)
    return best


MAX_SHARD_BLOCK = 1408
def matmul(a, b, mode, out_dtype, name, cap_m=MAX_SHARD_BLOCK, cap_n=MAX_SHARD_BLOCK, cap_k=2048, col_shards=0):
    ns = col_shards
    if mode == "nn":
        (M, K) = a.shape
        K2, N = (b.shape[1], ns * b.shape[2]) if ns else b.shape
    elif mode == "nt":
        (M, K) = a.shape
        N, K2 = (b.shape[1], ns * b.shape[2]) if ns else b.shape
    else:
        (K, M), (K2, N) = a.shape, b.shape
    assert K == K2, (a.shape, b.shape, mode)
    if K <= 3072:
        cap_k = K
        if K > 2048:
            cap_n = 1024
    n_unit = N // ns if (ns and mode != "nt") else N
    k_unit = K // ns if (ns and mode == "nt") else K
    bm = _blk(M, cap_m)
    bn = _blk(n_unit, MAX_SHARD_BLOCK) if n_unit != N else _blk(N, cap_n)
    if k_unit != K:
        bk = _blk(k_unit, MAX_SHARD_BLOCK)
    else:
        bk = _blk(K, 1024 if (ns and mode == "tn") else cap_k)
    nk = K // bk
    nps, kps = n_unit // bn, k_unit // bk
    dims = {"nn": ((1,), (0,)), "nt": ((1,), (1,)), "tn": ((0,), (0,))}[mode]

    def dot(a_ref, b_ref):
        return lax.dot_general(a_ref[...].astype(BF16), b_ref[...].astype(BF16), (dims, ((), ())), preferred_element_type=F32)

    def body_one(a_ref, b_ref, o_ref):
        o_ref[...] = dot(a_ref, b_ref).astype(o_ref.dtype)

    def body_acc(a_ref, b_ref, o_ref, acc_ref):
        k = pl.program_id(2)

        @pl.when(k == 0)
        def _():
            acc_ref[...] = jnp.zeros_like(acc_ref)

        acc_ref[...] += dot(a_ref, b_ref)

        @pl.when(k == nk - 1)
        def _():
            o_ref[...] = acc_ref[...].astype(o_ref.dtype)

    a_spec = pl.BlockSpec((bk, bm), lambda i, j, k: (k, i)) if mode == "tn" else pl.BlockSpec((bm, bk), lambda i, j, k: (i, k))
    if mode == "nt":
        b_spec = pl.BlockSpec((None, bn, bk), lambda i, j, k: (k // kps, j, k % kps)) if ns else pl.BlockSpec((bn, bk), lambda i, j, k: (j, k))
    elif mode == "nn" and ns:
        b_spec = pl.BlockSpec((None, bk, bn), lambda i, j, k: (j // nps, k, j % nps))
    else:
        b_spec = pl.BlockSpec((bk, bn), lambda i, j, k: (k, j))
    if mode == "tn" and ns:
        o_spec, o_shape = pl.BlockSpec((None, bm, bn), lambda i, j, k: (j // nps, i, j % nps)), (ns, M, n_unit)
    else:
        o_spec, o_shape = pl.BlockSpec((bm, bn), lambda i, j, k: (i, j)), (M, N)
    return pl.pallas_call(
        body_one if nk == 1 else body_acc, name=name, grid=(M // bm, N // bn, nk),
        in_specs=[a_spec, b_spec], out_specs=o_spec,
        out_shape=SDS(o_shape, out_dtype), scratch_shapes=[] if nk == 1 else [pltpu.VMEM((bm, bn), F32)],
        compiler_params=_cp(3),
    )(a, b)


class Row:
    def __init__(self, arr, bshape, imap, splits=None, diff=True, acc=False, gdtype=F32, gshape=None, gbshape=None, gimap=None,
                 lead=0):
        self.arr, self.bshape, self.imap = arr, tuple(bshape), imap
        self.splits, self.lead = splits, lead
        self.diff, self.acc, self.gdtype = diff, acc, gdtype
        self.gshape = tuple(arr.shape) if gshape is None else tuple(gshape)
        self.gbshape = self.bshape if gbshape is None else tuple(gbshape)
        self.gimap = imap if gimap is None else gimap

    def gspec(self):
        return pl.BlockSpec(self.gbshape, self.gimap)

    def spec(self):
        return pl.BlockSpec(self.bshape, self.imap)

    def pieces(self, ref):
        return _load_pieces(ref, self.splits, self.lead)

    def n_pieces(self):
        return _n_pieces(self.splits, self.lead)


class Out:
    def __init__(self, shape, dtype, bshape, imap, splits=None, lead=0):
        self.shape, self.dtype, self.bshape, self.imap = tuple(shape), dtype, tuple(bshape), imap
        self.splits, self.lead = splits, lead

    def n_pieces(self):
        return _n_pieces(self.splits, self.lead)


def _n_pieces(splits, lead):
    return lead if lead else (1 if splits is None else len(splits))


def _load_pieces(ref, splits, lead):
    if lead:
        return [ref[k].astype(F32) for k in range(lead)]
    if splits is None:
        return [ref[...].astype(F32)]
    out, o = [], 0
    for w in splits:
        out.append(ref[..., o:o + w].astype(F32))
        o += w
    return out


def _store_pieces(ref, splits, lead, vals, accumulate=False):
    def put(idx, v):
        if accumulate:
            ref[idx] += v.astype(ref.dtype)
        else:
            ref[idx] = v.astype(ref.dtype)

    if lead:
        for k in range(lead):
            put(k, vals[k])
    elif splits is None:
        put(..., vals[0])
    else:
        o = 0
        for w, v in zip(splits, vals):
            put((..., slice(o, o + w)), v)
            o += w


def rowwise(fn, rows, params, outs, grid, name):
    nr, npar = len(rows), len(params)

    def body(*refs):
        ids = tuple(pl.program_id(a) for a in range(len(grid)))
        vals = []
        for r, ref in zip(rows, refs[:nr]):
            vals += r.pieces(ref)
        pvals = [ref[...].astype(F32) for ref in refs[nr:nr + npar]]
        res = list(fn(ids, *vals, *pvals))
        o = 0
        for spec, ref in zip(outs, refs[nr + npar:]):
            n = spec.n_pieces()
            _store_pieces(ref, spec.splits, spec.lead, res[o:o + n])
            o += n

    nz = len(grid)
    pspecs = [pl.BlockSpec(p.shape, (lambda *ids, _n=p.ndim: (0,) * _n)) for p in params]
    res = pl.pallas_call(
        body, name=name, grid=grid,
        in_specs=[r.spec() for r in rows] + pspecs,
        out_specs=[pl.BlockSpec(o.bshape, o.imap) for o in outs],
        out_shape=[SDS(o.shape, o.dtype) for o in outs],
        compiler_params=_cp(nz),
    )(*[r.arr for r in rows], *params)
    return list(res)


def rowwise_bwd(fn, rows, params, cots, grid, name):
    nr, npar, nc = len(rows), len(params), len(cots)
    drows = [r for r in rows if r.diff]
    nz = len(grid)

    def body(*refs):
        ids = tuple(pl.program_id(a) for a in range(nz))
        row_refs, par_refs = refs[:nr], refs[nr:nr + npar]
        cot_refs = refs[nr + npar:nr + npar + nc]
        drow_refs = refs[nr + npar + nc:nr + npar + nc + len(drows)]
        dpar_refs = refs[nr + npar + nc + len(drows):]
        pieces, is_diff = [], []
        for r, ref in zip(rows, row_refs):
            ps = r.pieces(ref)
            pieces += ps
            is_diff += [r.diff] * len(ps)
        pvals = [ref[...].astype(F32) for ref in par_refs]
        dvals = [p for p, dflag in zip(pieces, is_diff) if dflag]
        nd = len(dvals)

        def f(*args):
            it = iter(args[:nd])
            full = [next(it) if dflag else p for p, dflag in zip(pieces, is_diff)]
            return tuple(fn(ids, *full, *args[nd:]))

        _, vjp = jax.vjp(f, *dvals, *pvals)
        cvals = []
        for c, ref in zip(cots, cot_refs):
            cvals += c.pieces(ref)
        g = vjp(tuple(cvals))
        o = 0
        first_inner = ids[-1] == 0
        for r, ref in zip(drows, drow_refs):
            n = r.n_pieces()
            gs = g[o:o + n]
            o += n
            if r.acc:
                @pl.when(first_inner)
                def _(ref=ref):
                    ref[...] = jnp.zeros_like(ref)
            _store_pieces(ref, r.splits, r.lead, gs, accumulate=r.acc)
        first = functools.reduce(jnp.logical_and, [i == 0 for i in ids])
        for ref, gp in zip(dpar_refs, g[nd:]):
            @pl.when(first)
            def _(ref=ref):
                ref[...] = jnp.zeros_like(ref)
            ref[...] += gp

    pspecs = [pl.BlockSpec(p.shape, (lambda *ids, _n=p.ndim: (0,) * _n)) for p in params]
    res = pl.pallas_call(
        body, name=name, grid=grid,
        in_specs=[r.spec() for r in rows] + pspecs + [c.spec() for c in cots],
        out_specs=[r.gspec() for r in drows] + pspecs,
        out_shape=[SDS(r.gshape, r.gdtype) for r in drows] + [SDS(p.shape, F32) for p in params],
        compiler_params=_cp(nz),
    )(*[r.arr for r in rows], *params, *[c.arr for c in cots])
    res = list(res)
    return res[:len(drows)], res[len(drows):]


def _sigmoid(x):
    return 0.5 * (jnp.tanh(0.5 * x) + 1.0)


def _silu(x):
    return x * _sigmoid(x)


def _normmod(x, gain, sc, sh):
    inv = lax.rsqrt(jnp.mean(x * x, axis=-1, keepdims=True) + EPS)
    return x * inv * gain * (1.0 + sc) + sh


def f_first(ids, x, gain, sc, sh):
    return x, _normmod(x, gain, sc, sh)


def f_resid_norm(ids, x, y, g, gain, sc, sh):
    xn = x + g * y
    return xn, _normmod(xn, gain, sc, sh)


@jax.custom_vjp
def _swiglu(gate, up):
    return _silu(gate) * up


def _swiglu_fwd(gate, up):
    return _silu(gate) * up, (gate, up)


def _swiglu_bwd(res, da):
    gate, up = res
    s = _sigmoid(gate)
    gs = gate * s
    return da * up * (s + gs * (1.0 - s)), da * gs


_swiglu.defvjp(_swiglu_fwd, _swiglu_bwd)


def f_swiglu(ids, gate, up):
    return (_swiglu(gate, up),)


def loss_and_grad(x, y, tgt, g, S):
    nt = S // WT

    def body(x_ref, y_ref, t_ref, g_ref, part_ref, dx_ref, dy_ref, dg_ref):
        @pl.when(pl.program_id(0) == 0)
        def _():
            dg_ref[...] = jnp.zeros_like(dg_ref)

        yv = y_ref[...].astype(F32)
        gg = g_ref[...]
        e = x_ref[...] + gg * yv - t_ref[...]
        part_ref[...] = 0.5 * jnp.sum(e * e, axis=0, keepdims=True) * (1.0 / D)
        d = e * (1.0 / D)
        dx_ref[...] = d
        dy_ref[...] = (d * gg).astype(dy_ref.dtype)
        dg_ref[...] += jnp.sum(d * yv, axis=0, keepdims=True)

    row = pl.BlockSpec((WT, D), lambda i: (i, 0))
    vec = pl.BlockSpec((1, D), lambda i: (0, 0))
    return pl.pallas_call(
        body, name="loss_and_grad", grid=(nt,), in_specs=[row, row, row, vec],
        out_specs=[pl.BlockSpec((None, 1, D), lambda i: (i, 0, 0)), row, row, vec],
        out_shape=[SDS((nt, 1, D), F32), SDS((S, D), F32), SDS((S, D), BF16), SDS((1, D), F32)],
        compiler_params=_cp(1),
    )(x, y, tgt, g)


def _softplus(x):
    return jnp.maximum(x, 0.0) + jnp.log(1.0 + jnp.exp(-jnp.abs(x)))


def _chunk_tril(T):
    r = lax.broadcasted_iota(jnp.int32, (T, T), 0)
    c = lax.broadcasted_iota(jnp.int32, (T, T), 1)
    return jnp.where((r // GDN_C == c // GDN_C) & (c <= r), 1.0, 0.0).astype(F32)


def _dot_hi(a, b, dims=((1,), (0,))):
    return lax.dot_general(a, b, (dims, ((), ())), precision=lax.Precision.HIGHEST, preferred_element_type=F32)


def _dot_x3(a, b, dims=((1,), (0,))):
    return lax.dot_general(a, b, (dims, ((), ())), precision=lax.Precision.HIGH, preferred_element_type=F32)


def f_gdn_gates(ids, ab, alog, dtb):
    T = ab.shape[0]
    g = -jnp.exp(alog) * _softplus(ab + dtb)
    beta = _sigmoid(ab)
    gcum = _dot_x3(_chunk_tril(T), g)
    row = lax.broadcasted_iota(jnp.int32, (LANES, LANES), 0)
    sel = lambda k: jnp.where(row == k, 1.0, 0.0).astype(F32)
    gcs = [_dot_x3(gcum, sel(h)) for h in range(GDN_H)]
    bts = [_dot_x3(beta, sel(GDN_H + h)) for h in range(GDN_H)]
    return (*gcs, *bts)


def f_gdn_post(ids, *args):
    os_, zs, gain = args[:GDN_H], args[GDN_H:2 * GDN_H], args[2 * GDN_H]
    out = []
    for o, z in zip(os_, zs):
        inv = lax.rsqrt(jnp.mean(o * o, axis=-1, keepdims=True) + EPS)
        out.append(o * inv * gain * _silu(z))
    return tuple(out)


def _qknorm1(x, gain2, scale):
    lane = lax.broadcasted_iota(jnp.int32, x.shape, 1)
    lo = lane < DSW_DH
    x2 = x * x
    s_all = jnp.sum(x2, axis=-1, keepdims=True)
    s_lo = jnp.sum(jnp.where(lo, x2, 0.0), axis=-1, keepdims=True)
    ms = jnp.where(lo, s_lo, s_all - s_lo) * (1.0 / DSW_DH)
    return x * lax.rsqrt(ms + EPS) * (gain2 * scale)


def f_qknorm(ids, *args):
    return tuple(_qknorm1(x, args[-1], 1.0) for x in args[:-1])


def f_qnorm(ids, *args):
    return tuple(_qknorm1(x, args[-1], DSW_DH ** -0.5) for x in args[:-1])


def f_combine(ids, o0, o1, o2, l0, l1, l2):
    m = jnp.maximum(jnp.maximum(l0, l1), l2)
    e0, e1, e2 = jnp.exp(l0 - m), jnp.exp(l1 - m), jnp.exp(l2 - m)
    den = e0 + e1 + e2
    o = (e0 * o0 + e1 * o1 + e2 * o2) / den
    return o, m + jnp.log(den)


GDN_T = 512
HALO = 16


def _conv_pre(xx, w):
    acc = xx * w[3:4, :]
    for j in range(3):
        acc = acc + pltpu.roll(xx, shift=3 - j, axis=0) * w[j:j + 1, :]
    return acc


@jax.custom_vjp
def _qkv_act_core(pre, norm_on, scale):
    s = _silu(pre)
    r = lax.rsqrt(jnp.sum(s * s, axis=-1, keepdims=True) + EPS)
    return jnp.where(norm_on > 0.5, s * r * scale, s)


def _qkv_act_fwd(pre, norm_on, scale):
    return _qkv_act_core(pre, norm_on, scale), (pre, norm_on, scale)


def _qkv_act_bwd(res, dout):
    pre, norm_on, scale = res
    sig = _sigmoid(pre)
    s = pre * sig
    r = lax.rsqrt(jnp.sum(s * s, axis=-1, keepdims=True) + EPS)
    unit = s * r
    dn = dout * scale
    ds = jnp.where(norm_on > 0.5, r * (dn - unit * jnp.sum(dn * unit, axis=-1, keepdims=True)), dout)
    return ds * (sig + s * (1.0 - sig)), jnp.zeros_like(norm_on), jnp.zeros_like(scale)


_qkv_act_core.defvjp(_qkv_act_fwd, _qkv_act_bwd)


def _qkv_act(pre, cidx):
    norm_on = jnp.where(cidx < 2 * GDN_H, 1.0, 0.0).astype(F32)
    scale = jnp.where(cidx < GDN_H, GDN_DK ** -0.5, 1.0).astype(F32)
    return _qkv_act_core(pre, norm_on, scale)


def gdn_pre(proj, conv_w, S):
    nt = S // GDN_T
    hb = GDN_T // HALO

    def body(prev_ref, cur_ref, w_ref, o_ref):
        p, i = pl.program_id(0), pl.program_id(1)
        for h in range(GDN_H):
            cols = slice(LANES * h, LANES * (h + 1))
            prev = jnp.where(i > 0, prev_ref[:, cols].astype(F32), 0.0)
            xx = jnp.concatenate([prev, cur_ref[:, cols].astype(F32)], axis=0)
            pre = _conv_pre(xx, w_ref[:, cols])[HALO:]
            o_ref[h] = _qkv_act(pre, p * GDN_H + h)

    hv = GDN_H * LANES
    return pl.pallas_call(
        body, name="gdn_pre", grid=(3, nt),
        in_specs=[pl.BlockSpec((HALO, hv), lambda p, i: (jnp.maximum(i * hb - 1, 0), p)),
                  pl.BlockSpec((GDN_T, hv), lambda p, i: (i, p)),
                  pl.BlockSpec((4, hv), lambda p, i: (0, p))],
        out_specs=pl.BlockSpec((None, GDN_H, GDN_T, LANES), lambda p, i: (p, 0, i, 0)),
        out_shape=SDS((3, GDN_H, S, LANES), F32),
        compiler_params=_cp(2),
    )(proj, proj, conv_w)


def gdn_pre_bwd(proj, conv_w, dqkv, S):
    nt = S // GDN_T
    hb = GDN_T // HALO
    last_h = S // HALO - 1

    def body(prev_ref, cur_ref, next_ref, w_ref, d_ref, dnext_ref, dx_ref, dw_ref):
        p, i = pl.program_id(0), pl.program_id(1)

        @pl.when(i == 0)
        def _():
            dw_ref[...] = jnp.zeros_like(dw_ref)

        for h in range(GDN_H):
            cols = slice(LANES * h, LANES * (h + 1))
            w = w_ref[:, cols]
            prev = jnp.where(i > 0, prev_ref[:, cols].astype(F32), 0.0)
            xx = jnp.concatenate([prev, cur_ref[:, cols].astype(F32), next_ref[:, cols].astype(F32)], axis=0)
            dnext = jnp.where(i < nt - 1, dnext_ref[h], 0.0)
            dd = jnp.concatenate([jnp.zeros((HALO, LANES), F32), d_ref[h], dnext], axis=0)
            pre = _conv_pre(xx, w)
            _, vjp = jax.vjp(lambda v, _c=p * GDN_H + h: _qkv_act(v, _c), pre)
            (dpre,) = vjp(dd)
            dx = dpre * w[3:4, :]
            R = dpre.shape[0]
            for j in range(3):
                dx = dx + pltpu.roll(dpre, shift=R - (3 - j), axis=0) * w[j:j + 1, :]
            dx_ref[:, cols] = dx[HALO:HALO + GDN_T].astype(dx_ref.dtype)
            own = HALO + GDN_T
            rows_w = [jnp.sum((dpre * pltpu.roll(xx, shift=3 - j, axis=0))[:own], axis=0, keepdims=True) for j in range(3)]
            rows_w.append(jnp.sum((dpre * xx)[:own], axis=0, keepdims=True))
            r4 = lax.broadcasted_iota(jnp.int32, (4, LANES), 0)
            dw = jnp.zeros((4, LANES), F32)
            for j in range(4):
                dw = dw + jnp.where(r4 == j, rows_w[j], 0.0)
            dw_ref[:, cols] += dw

    hv = GDN_H * LANES
    return pl.pallas_call(
        body, name="gdn_pre_bwd", grid=(3, nt),
        in_specs=[pl.BlockSpec((HALO, hv), lambda p, i: (jnp.maximum(i * hb - 1, 0), p)),
                  pl.BlockSpec((GDN_T, hv), lambda p, i: (i, p)),
                  pl.BlockSpec((HALO, hv), lambda p, i: (jnp.minimum((i + 1) * hb, last_h), p)),
                  pl.BlockSpec((4, hv), lambda p, i: (0, p)),
                  pl.BlockSpec((None, GDN_H, GDN_T, LANES), lambda p, i: (p, 0, i, 0)),
                  pl.BlockSpec((None, GDN_H, HALO, LANES), lambda p, i: (p, 0, jnp.minimum((i + 1) * hb, last_h), 0))],
        out_specs=[pl.BlockSpec((GDN_T, hv), lambda p, i: (i, p)),
                   pl.BlockSpec((4, hv), lambda p, i: (0, p))],
        out_shape=[SDS((S, 3 * hv), BF16), SDS((4, 3 * hv), F32)],
        compiler_params=_cp(2),
    )(proj, proj, proj, conv_w, dqkv, dqkv)


_DIMS = {"nn": ((1,), (0,)), "nt": ((1,), (1,)), "tn": ((0,), (0,))}


def _mm_raw(a, b, mode, hi):
    if hi:
        return _dot_hi(a, b, _DIMS[mode])
    return lax.dot_general(a.astype(BF16), b.astype(BF16), (_DIMS[mode], ((), ())), preferred_element_type=F32)


@functools.partial(jax.custom_vjp, nondiff_argnums=(2, 3))
def mm(a, b, mode, hi):
    return _mm_raw(a, b, mode, hi)


def _mm_fwd(a, b, mode, hi):
    return _mm_raw(a, b, mode, hi), (a, b)


def _mm_bwd(mode, hi, res, dc):
    a, b = res
    if mode == "nn":
        da, db = mm(dc, b, "nt", hi), mm(a, dc, "tn", hi)
    elif mode == "nt":
        da, db = mm(dc, b, "nn", hi), mm(dc, a, "tn", hi)
    else:
        da, db = mm(b, dc, "nt", hi), mm(a, dc, "nn", hi)
    return da, db


mm.defvjp(_mm_fwd, _mm_bwd)


TRI_BASE = 8


def _unit_lower_inverses(Ls):
    n = Ls[0].shape[0]
    r = lax.broadcasted_iota(jnp.int32, (n, n), 0)
    c = lax.broadcasted_iota(jnp.int32, (n, n), 1)
    eye = jnp.where(r == c, 1.0, 0.0).astype(F32)
    base = r // TRI_BASE == c // TRI_BASE
    Ps = [jnp.where(base, -L, 0.0) for L in Ls]
    invs = [eye + P for P in Ps]
    k = 1
    while 2 * k < TRI_BASE:
        Ps = [_dot_x3(P, P) for P in Ps]
        invs = [inv + _dot_x3(inv, P) for inv, P in zip(invs, Ps)]
        k *= 2
    b = 2 * TRI_BASE
    while b <= n:
        off_mask = (r // b == c // b) & ((r % b) >= b // 2) & ((c % b) < b // 2)
        ts = [_dot_x3(inv, jnp.where(off_mask, L, 0.0)) for inv, L in zip(invs, Ls)]
        invs = [inv - _dot_x3(t, inv) for inv, t in zip(invs, ts)]
        b *= 2
    return invs


@jax.custom_vjp
def tri_apply(invs, Ls, r1s, r2s):
    return [_mm_raw(i, r, "nn", False) for i, r in zip(invs, r1s)], [_mm_raw(i, r, "nn", False) for i, r in zip(invs, r2s)]


def _tri_fwd(invs, Ls, r1s, r2s):
    s1s = [_mm_raw(i, r, "nn", False) for i, r in zip(invs, r1s)]
    s2s = [_mm_raw(i, r, "nn", False) for i, r in zip(invs, r2s)]
    return (s1s, s2s), (invs, s1s, s2s)


def _tri_bwd(res, ds):
    invs, s1s, s2s = res
    d1s = [_mm_raw(i, d, "tn", False) for i, d in zip(invs, ds[0])]
    d2s = [_mm_raw(i, d, "tn", False) for i, d in zip(invs, ds[1])]
    dLs = [-(_mm_raw(d1, s1, "nt", False) + _mm_raw(d2, s2, "nt", False)) for d1, s1, d2, s2 in zip(d1s, s1s, d2s, s2s)]
    return [jnp.zeros_like(i) for i in invs], dLs, d1s, d2s


tri_apply.defvjp(_tri_fwd, _tri_bwd)


def _gdn_chunk(qs, ks, vs, gcbs, btbs, Ss, invs=None):
    C = qs[0].shape[0]
    r = lax.broadcasted_iota(jnp.int32, (C, C), 0)
    c = lax.broadcasted_iota(jnp.int32, (C, C), 1)
    causal, strict = c <= r, c < r
    rows = lax.broadcasted_iota(jnp.int32, gcbs[0].shape, 0)
    Gs = [g[:, :C] for g in gcbs]
    decays = [jnp.exp(jnp.where(causal, G - G.T, NEG)) for G in Gs]
    kbs = [k * b for k, b in zip(ks, btbs)]
    vbs = [v * b for v, b in zip(vs, btbs)]
    Ls = [jnp.where(strict, mm(kb, k, "nt", False) * d, 0.0) for kb, k, d in zip(kbs, ks, decays)]
    egs = [jnp.exp(g) for g in gcbs]
    if invs is None:
        invs = _unit_lower_inverses(Ls)
    us, ws = tri_apply(invs, Ls, vbs, [kb * eg for kb, eg in zip(kbs, egs)])
    qks = [jnp.where(causal, mm(q, k, "nt", False) * d, 0.0) for q, k, d in zip(qs, ks, decays)]
    g_lasts = [jnp.sum(jnp.where(rows == C - 1, g, 0.0), axis=0, keepdims=True) for g in gcbs]
    q_decs = [q * eg for q, eg in zip(qs, egs)]
    k_decs = [k * jnp.exp(gl - g) for k, gl, g in zip(ks, g_lasts, gcbs)]
    v_news = [u - mm(w, S, "nn", False) for u, w, S in zip(us, ws, Ss)]
    os_ = [mm(qd, S, "nn", False) + mm(qk, vn, "nn", False) for qd, S, qk, vn in zip(q_decs, Ss, qks, v_news)]
    S_news = [S * jnp.exp(gl) + mm(kd, vn, "tn", False) for S, gl, kd, vn in zip(Ss, g_lasts, k_decs, v_news)]
    return os_, S_news, invs


def gdn_core(qkv, gc, bt, S):
    nchunk = S // GDN_C

    def body(qkv_ref, g_ref, b_ref, o_ref, st_ref, inv_ref, s_scr):
        n = pl.program_id(0)

        @pl.when(n == 0)
        def _():
            s_scr[...] = jnp.zeros_like(s_scr)

        heads = range(GDN_H)
        S_in = [s_scr[h] for h in heads]
        os_, S_new, invs = _gdn_chunk([qkv_ref[0, h] for h in heads], [qkv_ref[1, h] for h in heads], [qkv_ref[2, h] for h in heads],
                                      [g_ref[h] for h in heads], [b_ref[h] for h in heads], S_in)
        for h in heads:
            st_ref[h] = S_in[h]
            inv_ref[h] = invs[h]
            o_ref[h] = os_[h]
            s_scr[h] = S_new[h]

    blk3 = pl.BlockSpec((3, GDN_H, GDN_C, LANES), lambda n: (0, 0, n, 0))
    hb = pl.BlockSpec((GDN_H, GDN_C, LANES), lambda n: (0, n, 0))
    return pl.pallas_call(
        body, name="gdn_core", grid=(nchunk,),
        in_specs=[blk3, hb, hb],
        out_specs=[hb, pl.BlockSpec((GDN_H, None, GDN_DK, LANES), lambda n: (0, n, 0, 0)),
                   pl.BlockSpec((GDN_H, None, GDN_C, GDN_C), lambda n: (0, n, 0, 0))],
        out_shape=[SDS((GDN_H, S, LANES), F32), SDS((GDN_H, nchunk, GDN_DK, LANES), F32), SDS((GDN_H, nchunk, GDN_C, GDN_C), F32)],
        scratch_shapes=[pltpu.VMEM((GDN_H, GDN_DK, LANES), F32)],
        compiler_params=_cp(1),
    )(qkv, gc, bt)


def gdn_core_bwd(qkv, gc, bt, states, invs, do, S):
    nchunk = S // GDN_C

    def body(qkv_ref, g_ref, b_ref, st_ref, inv_ref, do_ref, dqkv_ref, dg_ref, db_ref, ds_scr):
        n = pl.program_id(0)

        @pl.when(n == 0)
        def _():
            ds_scr[...] = jnp.zeros_like(ds_scr)

        heads = range(GDN_H)
        saved = [inv_ref[h] for h in heads]
        _, vjp = jax.vjp(lambda *a: _gdn_chunk(*a, invs=saved)[:2],
                         [qkv_ref[0, h] for h in heads], [qkv_ref[1, h] for h in heads], [qkv_ref[2, h] for h in heads],
                         [g_ref[h] for h in heads], [b_ref[h] for h in heads], [st_ref[h] for h in heads])
        dq, dk, dv, dg, db, dS = vjp(([do_ref[h] for h in heads], [ds_scr[h] for h in heads]))
        for h in heads:
            dqkv_ref[0, h] = dq[h]
            dqkv_ref[1, h] = dk[h]
            dqkv_ref[2, h] = dv[h]
            dg_ref[h] = dg[h]
            db_ref[h] = db[h]
            ds_scr[h] = dS[h]

    rev = lambda n: nchunk - 1 - n
    blk3 = pl.BlockSpec((3, GDN_H, GDN_C, LANES), lambda n: (0, 0, rev(n), 0))
    hb = pl.BlockSpec((GDN_H, GDN_C, LANES), lambda n: (0, rev(n), 0))
    return pl.pallas_call(
        body, name="gdn_core_bwd", grid=(nchunk,),
        in_specs=[blk3, hb, hb, pl.BlockSpec((GDN_H, None, GDN_DK, LANES), lambda n: (0, rev(n), 0, 0)),
                  pl.BlockSpec((GDN_H, None, GDN_C, GDN_C), lambda n: (0, rev(n), 0, 0)), hb],
        out_specs=[blk3, hb, hb],
        out_shape=[SDS((3, GDN_H, S, LANES), F32), SDS((GDN_H, S, LANES), F32), SDS((GDN_H, S, LANES), F32)],
        scratch_shapes=[pltpu.VMEM((GDN_H, GDN_DK, LANES), F32)],
        compiler_params=_cp(1),
    )(qkv, gc, bt, states, invs, do)


GDN_MAIN = 4 * GDN_H * LANES
GDN_PROJ = GDN_MAIN + LANES
RT = 256


def gdn_forward(h, w_in, conv_w, alog, dtb, out_gain, w_out):
    S = h.shape[0]
    nt = S // RT
    proj = matmul(h, w_in, "nn", BF16, "gdn_in")
    qkv = gdn_pre(proj, conv_w, S)
    ab_row = Row(proj, (RT, LANES), lambda i: (i, GDN_MAIN // LANES), gdtype=BF16, gshape=(S, LANES), gimap=lambda i: (i, 0))
    hm = lambda i: (0, i, 0)
    hv = GDN_H * LANES
    gc, bt = rowwise(f_gdn_gates, [ab_row], [alog, dtb],
                     [Out((GDN_H, S, LANES), F32, (GDN_H, RT, LANES), hm, lead=GDN_H)] * 2, (nt,), "gdn_gates")
    o, states, invs = gdn_core(qkv, gc, bt, S)
    o_row = Row(o, (GDN_H, RT, LANES), hm, lead=GDN_H)
    z_row = Row(proj, (RT, hv), lambda i: (i, 3), splits=[LANES] * GDN_H, gdtype=BF16, gshape=(S, hv), gimap=lambda i: (i, 0))
    (on,) = rowwise(f_gdn_post, [o_row, z_row], [out_gain],
                    [Out((S, hv), BF16, (RT, hv), lambda i: (i, 0), splits=[LANES] * GDN_H)], (nt,), "gdn_post")
    y = matmul(on, w_out, "nn", BF16, "gdn_out")
    saved = dict(h=h, proj=proj, qkv=qkv, gc=gc, bt=bt, states=states, invs=invs, o=o, on=on, ab_row=ab_row, o_row=o_row, z_row=z_row)
    return y, saved


def gdn_backward(dy, sv, w_in, conv_w, alog, dtb, out_gain, w_out):
    S = dy.shape[0]
    nt = S // RT
    hm = lambda i: (0, i, 0)
    hv = GDN_H * LANES
    don = matmul(dy, w_out, "nt", BF16, "gdn_out_dx")
    d_w_out = matmul(sv["on"], dy, "tn", F32, "gdn_out_dw")
    (do, dz), (d_gain,) = rowwise_bwd(f_gdn_post, [sv["o_row"], sv["z_row"]], [out_gain],
                                      [Row(don, (RT, hv), lambda i: (i, 0), splits=[LANES] * GDN_H)], (nt,), "gdn_post_bwd")
    dqkv, dgc, dbt = gdn_core_bwd(sv["qkv"], sv["gc"], sv["bt"], sv["states"], sv["invs"], do, S)
    head_blk = lambda a: Row(a, (GDN_H, RT, LANES), hm, lead=GDN_H)
    (dab,), (d_alog, d_dtb) = rowwise_bwd(f_gdn_gates, [sv["ab_row"]], [alog, dtb], [head_blk(dgc), head_blk(dbt)],
                                          (nt,), "gdn_gates_bwd")
    dqkv_proj, d_conv = gdn_pre_bwd(sv["proj"], conv_w, dqkv, S)
    dproj = jnp.concatenate([dqkv_proj, dz, dab], axis=1)
    d_w_in = matmul(sv["h"], dproj, "tn", F32, "gdn_in_dw")
    dh = matmul(dproj, w_in, "nt", BF16, "gdn_in_dx")
    return dh, dict(w_in=d_w_in, conv=d_conv, alog=d_alog, dtb=d_dtb, gain=d_gain, w_out=d_w_out)


QB = DSW_SPAN
N_HP = DSW_HG // LANES
PROJ_BLKS = 3 * 3 * N_HP


def _bucket_maps():
    a = np.arange(QB)[:, None]
    j = np.arange(2 * QB)[None, :]
    dist = QB + a - j
    band = (dist >= 0) & (dist <= DSW_SPAN)
    maps = []
    for _, dil in DSW_GROUPS:
        dd = np.maximum(dist, 0) * dil
        max_exact = REL_BUCKETS // 2
        scaled = np.log(np.maximum(dd, 1).astype(np.float32) / np.float32(max_exact)) / np.float32(math.log(REL_MAX_DIST / max_exact))
        large = max_exact + (scaled * np.float32(REL_BUCKETS - max_exact)).astype(np.int32)
        large = np.minimum(large, REL_BUCKETS - 1)
        maps.append(np.where(dd < max_exact, dd, large).astype(np.int32))
    return np.stack(maps), band


def dsw_bias(rel_bias):
    maps, band = _bucket_maps()
    maps = np.where(band[None], maps, -1).astype(np.int32)

    def body(tab_ref, bk_ref, o_ref):
        gh = pl.program_id(0)
        bk = bk_ref[...]
        acc = jnp.full(bk.shape, NEG, F32)
        for b in range(REL_BUCKETS):
            acc = jnp.where(bk == b, tab_ref[b, gh], acc)
        o_ref[...] = acc

    return pl.pallas_call(
        body, name="dsw_bias", grid=(3 * GDN_H,),
        in_specs=[pl.BlockSpec(memory_space=pltpu.SMEM),
                  pl.BlockSpec((None, QB, 2 * QB), lambda gh: (gh // GDN_H, 0, 0))],
        out_specs=pl.BlockSpec((None, QB, 2 * QB), lambda gh: (gh, 0, 0)),
        out_shape=SDS((3 * GDN_H, QB, 2 * QB), F32),
        compiler_params=_cp(1),
    )(rel_bias, jnp.asarray(maps))


def dsw_bias_grad(dbias):
    maps, band = _bucket_maps()
    maps = np.where(band[None], maps, -1).astype(np.int32)

    def body(d_ref, bk_ref, o_ref):
        bk = bk_ref[...]
        d = d_ref[...]
        rows = lax.broadcasted_iota(jnp.int32, (REL_BUCKETS, LANES), 0)
        acc = jnp.zeros((REL_BUCKETS, LANES), F32)
        for b in range(REL_BUCKETS):
            part = jnp.sum(jnp.where(bk == b, d, 0.0), axis=0, keepdims=True)
            val = jnp.sum(part, axis=1, keepdims=True)
            acc = jnp.where(rows == b, val, acc)
        o_ref[...] = acc

    return pl.pallas_call(
        body, name="dsw_bias_grad", grid=(3 * GDN_H,),
        in_specs=[pl.BlockSpec((None, QB, 2 * QB), lambda gh: (gh, 0, 0)),
                  pl.BlockSpec((None, QB, 2 * QB), lambda gh: (gh // GDN_H, 0, 0))],
        out_specs=pl.BlockSpec((None, REL_BUCKETS, LANES), lambda gh: (gh, 0, 0)),
        out_shape=SDS((3 * GDN_H, REL_BUCKETS, LANES), F32),
        compiler_params=_cp(1),
    )(dbias, jnp.asarray(maps))


def _nt(a, b):
    return lax.dot_general(a, b, (((1,), (1,)), ((), ())), preferred_element_type=F32)


def _tn(a, b):
    return lax.dot_general(a, b, (((0,), (0,)), ((), ())), preferred_element_type=F32)


def dsw_group_fwd(qn, kn, proj, bias, gi, S):
    dil = DSW_GROUPS[gi][1]
    sd = S // dil
    nq = sd // QB
    qv = qn.reshape(sd, dil * 3 * DSW_HG)
    kv = kn.reshape(sd, dil * 3 * DSW_HG)
    pv = proj.reshape(sd, dil * 9 * DSW_HG)
    qk_col = lambda hp, r: r * (3 * N_HP) + gi * N_HP + hp
    v_col = lambda hp, r: r * PROJ_BLKS + 2 * 3 * N_HP + gi * N_HP + hp

    def body(q_ref, kp_ref, kc_ref, vp_ref, vc_ref, b_ref, o_ref, l_ref):
        i = pl.program_id(2)
        q = q_ref[...]
        k2 = jnp.concatenate([kp_ref[...], kc_ref[...]], axis=0)
        v2 = jnp.concatenate([vp_ref[...], vc_ref[...]], axis=0).astype(BF16)
        lane_q = lax.broadcasted_iota(jnp.int32, (QB, LANES), 1) < DSW_DH
        lane_k = lax.broadcasted_iota(jnp.int32, (2 * QB, LANES), 1) < DSW_DH
        col = lax.broadcasted_iota(jnp.int32, (QB, 2 * QB), 1)
        first = jnp.logical_and(i == 0, col < QB)
        o_acc = jnp.zeros((QB, LANES), F32)
        lse_b = jnp.zeros((QB, LANES), F32)
        for hh in range(2):
            mq = lane_q if hh == 0 else jnp.logical_not(lane_q)
            mk = lane_k if hh == 0 else jnp.logical_not(lane_k)
            s = _nt(jnp.where(mq, q, 0).astype(BF16), k2) + b_ref[hh]
            s = jnp.where(first, NEG, s)
            mx = jnp.max(s, axis=1, keepdims=True)
            p = jnp.exp(s - mx)
            l = jnp.sum(p, axis=1, keepdims=True)
            oh = jnp.dot(p.astype(BF16), jnp.where(mk, v2, 0).astype(BF16), preferred_element_type=F32) / l
            o_acc = o_acc + oh
            lse_b = jnp.where(mq, mx + jnp.log(l), lse_b)
        o_ref[...] = o_acc
        l_ref[...] = lse_b

    blk = (QB, LANES)
    out_spec = pl.BlockSpec(blk, lambda hp, r, i: (i, r * N_HP + hp))
    o, lse = pl.pallas_call(
        body, name=f"dsw_fwd_g{gi}", grid=(N_HP, dil, nq),
        in_specs=[pl.BlockSpec(blk, lambda hp, r, i: (i, qk_col(hp, r))),
                  pl.BlockSpec(blk, lambda hp, r, i: (jnp.maximum(i - 1, 0), qk_col(hp, r))),
                  pl.BlockSpec(blk, lambda hp, r, i: (i, qk_col(hp, r))),
                  pl.BlockSpec(blk, lambda hp, r, i: (jnp.maximum(i - 1, 0), v_col(hp, r))),
                  pl.BlockSpec(blk, lambda hp, r, i: (i, v_col(hp, r))),
                  pl.BlockSpec((2, QB, 2 * QB), lambda hp, r, i: (gi * N_HP + hp, 0, 0))],
        out_specs=[out_spec, out_spec],
        out_shape=[SDS((sd, dil * DSW_HG), F32)] * 2,
        compiler_params=_cp(3),
    )(qv, kv, kv, pv, pv, bias)
    return o.reshape(S, DSW_HG), lse.reshape(S, DSW_HG)


def dsw_group_bwd(qn, kn, proj, bias, do, o, lse, gi, S):
    dil = DSW_GROUPS[gi][1]
    sd = S // dil
    nq = sd // QB
    qv = qn.reshape(sd, dil * 3 * DSW_HG)
    kv = kn.reshape(sd, dil * 3 * DSW_HG)
    pv = proj.reshape(sd, dil * 9 * DSW_HG)
    dov = do.reshape(sd, dil * DSW_HG)
    ov = o.reshape(sd, dil * DSW_HG)
    lv = lse.reshape(sd, dil * DSW_HG)
    qk_col = lambda hp, r: r * (3 * N_HP) + gi * N_HP + hp
    v_col = lambda hp, r: r * PROJ_BLKS + 2 * 3 * N_HP + gi * N_HP + hp
    o_col = lambda hp, r: r * N_HP + hp
    cur = lambda i: jnp.minimum(i, nq - 1)
    prev = lambda i: jnp.maximum(jnp.minimum(i, nq - 1) - 1, 0)
    done = lambda i: jnp.maximum(i - 1, 0)

    def body(q_ref, kp_ref, kc_ref, vp_ref, vc_ref, b_ref, do_ref, o_ref, l_ref,
             dq_ref, dk_ref, dv_ref, db_ref, dk_scr, dv_scr):
        r, i = pl.program_id(1), pl.program_id(2)

        @pl.when(jnp.logical_and(r == 0, i == 0))
        def _():
            db_ref[...] = jnp.zeros_like(db_ref)

        @pl.when(i == 0)
        def _():
            dk_scr[...] = jnp.zeros_like(dk_scr)
            dv_scr[...] = jnp.zeros_like(dv_scr)

        @pl.when(i < nq)
        def _():
            q = q_ref[...]
            k2 = jnp.concatenate([kp_ref[...], kc_ref[...]], axis=0)
            v2 = jnp.concatenate([vp_ref[...], vc_ref[...]], axis=0).astype(BF16)
            dout = do_ref[...].astype(F32)
            prod = dout * o_ref[...].astype(F32)
            lse_b = l_ref[...]
            lane_q = lax.broadcasted_iota(jnp.int32, (QB, LANES), 1) < DSW_DH
            col = lax.broadcasted_iota(jnp.int32, (QB, 2 * QB), 1)
            first = jnp.logical_and(i == 0, col < QB)
            dq = jnp.zeros((QB, LANES), F32)
            dk2 = jnp.zeros((2 * QB, LANES), F32)
            dv2 = jnp.zeros((2 * QB, LANES), F32)
            for hh in range(2):
                mq = lane_q if hh == 0 else jnp.logical_not(lane_q)
                qm = jnp.where(mq, q, 0).astype(BF16)
                dom = jnp.where(mq, dout, 0.0).astype(BF16)
                s = _nt(qm, k2) + b_ref[hh]
                s = jnp.where(first, NEG, s)
                lse_h = jnp.max(jnp.where(mq, lse_b, NEG), axis=1, keepdims=True)
                p = jnp.exp(s - lse_h)
                delta = jnp.sum(jnp.where(mq, prod, 0.0), axis=1, keepdims=True)
                dp = _nt(dom, v2)
                ds = p * (dp - delta)
                dsb = ds.astype(BF16)
                dq = dq + jnp.where(mq, jnp.dot(dsb, k2, preferred_element_type=F32), 0.0)
                dk2 = dk2 + _tn(dsb, qm)
                dv2 = dv2 + _tn(p.astype(BF16), dom)
                db_ref[hh] += ds
            dq_ref[...] = dq
            dk_ref[...] = dk_scr[...] + dk2[:QB]
            dv_ref[...] = (dv_scr[...] + dv2[:QB]).astype(dv_ref.dtype)
            dk_scr[...] = dk2[QB:]
            dv_scr[...] = dv2[QB:]

        @pl.when(i == nq)
        def _():
            dk_ref[...] = dk_scr[...]
            dv_ref[...] = dv_scr[...].astype(dv_ref.dtype)

    blk = (QB, LANES)
    dq, dk, dv, dbias = pl.pallas_call(
        body, name=f"dsw_bwd_g{gi}", grid=(N_HP, dil, nq + 1),
        in_specs=[pl.BlockSpec(blk, lambda hp, r, i: (cur(i), qk_col(hp, r))),
                  pl.BlockSpec(blk, lambda hp, r, i: (prev(i), qk_col(hp, r))),
                  pl.BlockSpec(blk, lambda hp, r, i: (cur(i), qk_col(hp, r))),
                  pl.BlockSpec(blk, lambda hp, r, i: (prev(i), v_col(hp, r))),
                  pl.BlockSpec(blk, lambda hp, r, i: (cur(i), v_col(hp, r))),
                  pl.BlockSpec((2, QB, 2 * QB), lambda hp, r, i: (gi * N_HP + hp, 0, 0)),
                  pl.BlockSpec(blk, lambda hp, r, i: (cur(i), o_col(hp, r))),
                  pl.BlockSpec(blk, lambda hp, r, i: (cur(i), o_col(hp, r))),
                  pl.BlockSpec(blk, lambda hp, r, i: (cur(i), o_col(hp, r)))],
        out_specs=[pl.BlockSpec(blk, lambda hp, r, i: (cur(i), o_col(hp, r))),
                   pl.BlockSpec(blk, lambda hp, r, i: (done(i), o_col(hp, r))),
                   pl.BlockSpec(blk, lambda hp, r, i: (done(i), o_col(hp, r))),
                   pl.BlockSpec((2, QB, 2 * QB), lambda hp, r, i: (hp, 0, 0))],
        out_shape=[SDS((sd, dil * DSW_HG), F32), SDS((sd, dil * DSW_HG), F32), SDS((sd, dil * DSW_HG), BF16),
                   SDS((GDN_H, QB, 2 * QB), F32)],
        scratch_shapes=[pltpu.VMEM(blk, F32), pltpu.VMEM(blk, F32)],
        compiler_params=_cp(3),
    )(qv, kv, kv, pv, pv, bias, dov, ov, lv)
    return dq.reshape(S, DSW_HG), dk.reshape(S, DSW_HG), dv.reshape(S, DSW_HG), dbias


def dsw_forward(h, w_in, q_gain2, k_gain2, rel_bias, w_out):
    S = h.shape[0]
    nt = S // RT
    nb = 3 * N_HP
    proj = matmul(h, w_in, "nn", F32, "dsw_in")
    width = nb * LANES
    lanes12 = [LANES] * nb
    (qn,) = rowwise(f_qnorm, [Row(proj, (RT, width), lambda i: (i, 0), splits=lanes12)], [q_gain2],
                    [Out((S, width), BF16, (RT, width), lambda i: (i, 0), splits=lanes12)], (nt,), "dsw_qnorm")
    (kn,) = rowwise(f_qknorm, [Row(proj, (RT, width), lambda i: (i, 1), splits=lanes12)], [k_gain2],
                    [Out((S, width), BF16, (RT, width), lambda i: (i, 0), splits=lanes12)], (nt,), "dsw_knorm")
    bias = dsw_bias(rel_bias)
    os_, ls_ = [], []
    for gi in range(3):
        o, l = dsw_group_fwd(qn, kn, proj, bias, gi, S)
        os_.append(o)
        ls_.append(l)
    full = lambda a: Row(a, (RT, DSW_HG), lambda i: (i, 0))
    o, lse = rowwise(f_combine, [full(a) for a in os_ + ls_], [],
                     [Out((S, DSW_HG), BF16, (RT, DSW_HG), lambda i: (i, 0)), Out((S, DSW_HG), F32, (RT, DSW_HG), lambda i: (i, 0))],
                     (nt,), "dsw_combine")
    y = matmul(o, w_out, "nn", F32, "dsw_out")
    return y, dict(h=h, proj=proj, qn=qn, kn=kn, bias=bias, o=o, lse=lse)


def dsw_backward(dy, sv, w_in, q_gain2, k_gain2, w_out):
    S = dy.shape[0]
    nt = S // RT
    nb = 3 * N_HP
    do = matmul(dy, w_out, "nt", BF16, "dsw_out_dx")
    d_w_out = matmul(sv["o"], dy, "tn", F32, "dsw_out_dw")
    pieces_q, pieces_k, pieces_v, dbs = [], [], [], []
    d_qg = jnp.zeros((1, LANES), F32)
    d_kg = jnp.zeros((1, LANES), F32)
    for gi in range(3):
        dq, dk, dv, db = dsw_group_bwd(sv["qn"], sv["kn"], sv["proj"], sv["bias"], do, sv["o"], sv["lse"], gi, S)
        dbs.append(db)
        pieces_v.append(dv)
        for which, dd in ((0, dq), (1, dk)):
            lanes4 = [LANES] * N_HP
            row = Row(sv["proj"], (RT, DSW_HG), lambda i, _o=which * 3 + gi: (i, _o), splits=lanes4,
                      gdtype=BF16, gshape=(S, DSW_HG), gimap=lambda i: (i, 0))
            fn, gain = (f_qnorm, q_gain2) if which == 0 else (f_qknorm, k_gain2)
            (dx,), (dg,) = rowwise_bwd(fn, [row], [gain], [Row(dd, (RT, DSW_HG), lambda i: (i, 0), splits=lanes4)],
                                       (nt,), f"dsw_norm_bwd_{which}{gi}")
            if which == 0:
                pieces_q.append(dx)
                d_qg = d_qg + dg
            else:
                pieces_k.append(dx)
                d_kg = d_kg + dg
    dproj = jnp.concatenate(pieces_q + pieces_k + pieces_v, axis=1)
    d_w_in = matmul(sv["h"], dproj, "tn", F32, "dsw_in_dw")
    dh = matmul(dproj, w_in, "nt", F32, "dsw_in_dx")
    d_rel = dsw_bias_grad(jnp.concatenate(dbs, axis=0))
    return dh, dict(w_in=d_w_in, q_gain2=d_qg, k_gain2=d_kg, rel=d_rel, w_out=d_w_out)


N_LB = DSW_HG // LANES
HALF = DSW_DH // 2


def _lanes(j):
    return slice(LANES * j, LANES * (j + 1))


def _deinterleave(stage, out_ref, dil, rows, dtype):
    for r in range(dil):
        for j in range(N_LB):
            out_ref[r, :, _lanes(j)] = stage[j, pl.ds(r, rows, stride=dil), :].astype(dtype)


def _interleave(in_ref, stage, dil, rows):
    for r in range(dil):
        for j in range(N_LB):
            stage[j, pl.ds(r, rows, stride=dil), :] = in_ref[r, :, _lanes(j)].astype(F32)


def dsw_prep(proj, q_gain2, k_gain2, gi, S):
    dil = DSW_GROUPS[gi][1]
    nt, rows = S // RT, RT // dil

    def body(q_ref, k_ref, v_ref, qg_ref, kg_ref, qo_ref, ko_ref, vo_ref, stage):
        for src, gain_ref, scale, dst in ((q_ref, qg_ref, DSW_DH ** -0.5, qo_ref), (k_ref, kg_ref, 1.0, ko_ref), (v_ref, None, None, vo_ref)):
            for j in range(N_LB):
                val = src[:, _lanes(j)].astype(F32)
                val = val if gain_ref is None else _qknorm1(val, gain_ref[...], scale)
                if dil == 1:
                    dst[0, :, _lanes(j)] = val.astype(BF16)
                else:
                    stage[j] = val
            if dil > 1:
                _deinterleave(stage, dst, dil, rows, BF16)

    col = lambda which: pl.BlockSpec((RT, DSW_HG), lambda i, _c=which * 3 + gi: (i, _c))
    gspec = pl.BlockSpec((1, LANES), lambda i: (0, 0))
    ospec = pl.BlockSpec((dil, rows, DSW_HG), lambda i: (0, i, 0))
    return pl.pallas_call(
        body, name=f"dsw_prep_g{gi}", grid=(nt,),
        in_specs=[col(0), col(1), col(2), gspec, gspec], out_specs=[ospec] * 3,
        out_shape=[SDS((dil, S // dil, DSW_HG), BF16)] * 3,
        scratch_shapes=[pltpu.VMEM((N_LB, RT, LANES), F32)], compiler_params=_cp(1),
    )(proj, proj, proj, q_gain2, k_gain2)


def dsw_prep_bwd(proj, q_gain2, k_gain2, dqd, dkd, dvd, gi, S):
    dil = DSW_GROUPS[gi][1]
    nt, rows = S // RT, RT // dil

    def body(q_ref, k_ref, qg_ref, kg_ref, dq_ref, dk_ref, dv_ref, oq_ref, ok_ref, ov_ref, dqg_ref, dkg_ref, stage):
        i = pl.program_id(0)

        @pl.when(i == 0)
        def _():
            dqg_ref[...] = jnp.zeros_like(dqg_ref)
            dkg_ref[...] = jnp.zeros_like(dkg_ref)

        for src, gain_ref, scale, cot_ref, dst, dg_ref in ((q_ref, qg_ref, DSW_DH ** -0.5, dq_ref, oq_ref, dqg_ref),
                                                          (k_ref, kg_ref, 1.0, dk_ref, ok_ref, dkg_ref)):
            if dil > 1:
                _interleave(cot_ref, stage, dil, rows)
            for j in range(N_LB):
                _, vjp = jax.vjp(lambda x, g, _s=scale: _qknorm1(x, g, _s), src[:, _lanes(j)].astype(F32), gain_ref[...])
                dx, dg = vjp(stage[j] if dil > 1 else cot_ref[0, :, _lanes(j)].astype(F32))
                dst[:, _lanes(j)] = dx.astype(dst.dtype)
                dg_ref[...] += dg
        if dil > 1:
            _interleave(dv_ref, stage, dil, rows)
        for j in range(N_LB):
            ov_ref[:, _lanes(j)] = (stage[j] if dil > 1 else dv_ref[0, :, _lanes(j)]).astype(ov_ref.dtype)

    col = lambda which: pl.BlockSpec((RT, DSW_HG), lambda i, _c=which * 3 + gi: (i, _c))
    gspec = pl.BlockSpec((1, LANES), lambda i: (0, 0))
    dspec = pl.BlockSpec((dil, rows, DSW_HG), lambda i: (0, i, 0))
    nspec = pl.BlockSpec((RT, DSW_HG), lambda i: (i, 0))
    return pl.pallas_call(
        body, name=f"dsw_prep_bwd_g{gi}", grid=(nt,),
        in_specs=[col(0), col(1), gspec, gspec, dspec, dspec, dspec], out_specs=[nspec] * 3 + [gspec] * 2,
        out_shape=[SDS((S, DSW_HG), BF16)] * 3 + [SDS((1, LANES), F32)] * 2,
        scratch_shapes=[pltpu.VMEM((N_LB, RT, LANES), F32)], compiler_params=_cp(1),
    )(proj, proj, q_gain2, k_gain2, dqd, dkd, dvd)


def _head_masks(rows):
    lane = lax.broadcasted_iota(jnp.int32, (rows, LANES), 1)
    return lane < DSW_DH, (lane % DSW_DH) < HALF


def dsw_attn_fwd(qd, kd, vd, bias, gi, S):
    dil = DSW_GROUPS[gi][1]
    sd = S // dil
    nq = sd // QB

    def body(q_ref, k_ref, v_ref, b_ref, o_ref, l_ref, kp_scr, vp_scr):
        i = pl.program_id(1)

        @pl.when(i == 0)
        def _():
            kp_scr[...] = jnp.zeros_like(kp_scr)
            vp_scr[...] = jnp.zeros_like(vp_scr)

        lo_q, _ = _head_masks(QB)
        lo_k, _ = _head_masks(2 * QB)
        col = lax.broadcasted_iota(jnp.int32, (QB, 2 * QB), 1)
        first = jnp.logical_and(i == 0, col < QB)
        hps, heads = range(N_HP), range(2 * N_HP)
        k2s = [jnp.concatenate([kp_scr[:, _lanes(hp)], k_ref[:, _lanes(hp)]], axis=0) for hp in hps]
        v2s = [jnp.concatenate([vp_scr[:, _lanes(hp)], v_ref[:, _lanes(hp)]], axis=0) for hp in hps]
        qs = [q_ref[:, _lanes(hp)] for hp in hps]
        k_now, v_now = k_ref[...], v_ref[...]
        mqs = [lo_q if h % 2 == 0 else jnp.logical_not(lo_q) for h in heads]
        mks = [lo_k if h % 2 == 0 else jnp.logical_not(lo_k) for h in heads]
        ss = [jnp.where(first, NEG, _nt(jnp.where(mqs[h], qs[h // 2], 0).astype(BF16), k2s[h // 2]) + b_ref[h]) for h in heads]
        mxs = [jnp.max(s, axis=1, keepdims=True) for s in ss]
        ps = [jnp.exp(s - mx) for s, mx in zip(ss, mxs)]
        ls = [jnp.sum(p, axis=1, keepdims=True) for p in ps]
        ohs = [jnp.dot(ps[h].astype(BF16), jnp.where(mks[h], v2s[h // 2], 0).astype(BF16), preferred_element_type=F32) / ls[h] for h in heads]
        lse_h = [mx + jnp.log(l) for mx, l in zip(mxs, ls)]
        for hp in hps:
            o_ref[:, _lanes(hp)] = ohs[2 * hp] + ohs[2 * hp + 1]
            l_ref[:, _lanes(hp)] = jnp.where(lo_q, lse_h[2 * hp], lse_h[2 * hp + 1])
        kp_scr[...] = k_now
        vp_scr[...] = v_now

    blk = pl.BlockSpec((None, QB, DSW_HG), lambda r, i: (r, i, 0))
    return pl.pallas_call(
        body, name=f"dsw_attn_g{gi}", grid=(dil, nq),
        in_specs=[blk, blk, blk, pl.BlockSpec((GDN_H, QB, 2 * QB), lambda r, i: (gi, 0, 0))],
        out_specs=[blk, blk], out_shape=[SDS((dil, sd, DSW_HG), F32)] * 2,
        scratch_shapes=[pltpu.VMEM((QB, DSW_HG), BF16)] * 2, compiler_params=_cp(2),
    )(qd, kd, vd, bias)


def dsw_attn_bwd(qd, kd, vd, bias, dod, statd, gi, S):
    dil = DSW_GROUPS[gi][1]
    sd = S // dil
    nq = sd // QB
    cur = lambda i: jnp.minimum(i, nq - 1)
    done = lambda i: jnp.maximum(i - 1, 0)

    def body(q_ref, k_ref, v_ref, b_ref, do_ref, st_ref, dq_ref, dk_ref, dv_ref, db_ref, kp_scr, vp_scr, dk_scr, dv_scr):
        r, i = pl.program_id(0), pl.program_id(1)

        @pl.when(jnp.logical_and(r == 0, i == 0))
        def _():
            db_ref[...] = jnp.zeros_like(db_ref)

        @pl.when(i == 0)
        def _():
            for scr in (kp_scr, vp_scr, dk_scr, dv_scr):
                scr[...] = jnp.zeros_like(scr)

        @pl.when(i < nq)
        def _():
            lo_q, first_half = _head_masks(QB)
            col = lax.broadcasted_iota(jnp.int32, (QB, 2 * QB), 1)
            first = jnp.logical_and(i == 0, col < QB)
            hps, heads = range(N_HP), range(2 * N_HP)
            k2s = [jnp.concatenate([kp_scr[:, _lanes(hp)], k_ref[:, _lanes(hp)]], axis=0) for hp in hps]
            v2s = [jnp.concatenate([vp_scr[:, _lanes(hp)], v_ref[:, _lanes(hp)]], axis=0) for hp in hps]
            qs = [q_ref[:, _lanes(hp)] for hp in hps]
            douts = [do_ref[:, _lanes(hp)] for hp in hps]
            stats = [st_ref[:, _lanes(hp)] for hp in hps]
            dkc = [dk_scr[:, _lanes(hp)] for hp in hps]
            dvc = [dv_scr[:, _lanes(hp)] for hp in hps]
            k_now, v_now = k_ref[...], v_ref[...]
            mqs = [lo_q if h % 2 == 0 else jnp.logical_not(lo_q) for h in heads]
            qms = [jnp.where(mqs[h], qs[h // 2], 0).astype(BF16) for h in heads]
            doms = [jnp.where(mqs[h], douts[h // 2], 0).astype(BF16) for h in heads]
            ss = [jnp.where(first, NEG, _nt(qms[h], k2s[h // 2]) + b_ref[h]) for h in heads]
            lses = [jnp.max(jnp.where(jnp.logical_and(mqs[h], first_half), stats[h // 2], NEG), axis=1, keepdims=True) for h in heads]
            deltas = [jnp.max(jnp.where(jnp.logical_and(mqs[h], jnp.logical_not(first_half)), stats[h // 2], NEG), axis=1, keepdims=True)
                      for h in heads]
            ps = [jnp.exp(ss[h] - lses[h]) for h in heads]
            dss = [ps[h] * (_nt(doms[h], v2s[h // 2]) - deltas[h]) for h in heads]
            dsbs = [d.astype(BF16) for d in dss]
            dqh = [jnp.where(mqs[h], jnp.dot(dsbs[h], k2s[h // 2], preferred_element_type=F32), 0.0) for h in heads]
            dkh = [_tn(dsbs[h], qms[h]) for h in heads]
            dvh = [_tn(ps[h].astype(BF16), doms[h]) for h in heads]
            for h in heads:
                db_ref[h] += dss[h]
            for hp in hps:
                dk2 = dkh[2 * hp] + dkh[2 * hp + 1]
                dv2 = dvh[2 * hp] + dvh[2 * hp + 1]
                dq_ref[:, _lanes(hp)] = dqh[2 * hp] + dqh[2 * hp + 1]
                dk_ref[:, _lanes(hp)] = dkc[hp] + dk2[:QB]
                dv_ref[:, _lanes(hp)] = (dvc[hp] + dv2[:QB]).astype(dv_ref.dtype)
                dk_scr[:, _lanes(hp)] = dk2[QB:]
                dv_scr[:, _lanes(hp)] = dv2[QB:]
            kp_scr[...] = k_now
            vp_scr[...] = v_now

        @pl.when(i == nq)
        def _():
            dk_ref[...] = dk_scr[...]
            dv_ref[...] = dv_scr[...].astype(dv_ref.dtype)

    blk = pl.BlockSpec((None, QB, DSW_HG), lambda r, i: (r, cur(i), 0))
    oblk = pl.BlockSpec((None, QB, DSW_HG), lambda r, i: (r, done(i), 0))
    return pl.pallas_call(
        body, name=f"dsw_attn_bwd_g{gi}", grid=(dil, nq + 1),
        in_specs=[blk, blk, blk, pl.BlockSpec((GDN_H, QB, 2 * QB), lambda r, i: (gi, 0, 0)), blk, blk],
        out_specs=[blk, oblk, oblk, pl.BlockSpec((GDN_H, QB, 2 * QB), lambda r, i: (0, 0, 0))],
        out_shape=[SDS((dil, sd, DSW_HG), F32), SDS((dil, sd, DSW_HG), F32), SDS((dil, sd, DSW_HG), BF16),
                   SDS((GDN_H, QB, 2 * QB), F32)],
        scratch_shapes=[pltpu.VMEM((QB, DSW_HG), BF16)] * 2 + [pltpu.VMEM((QB, DSW_HG), F32)] * 2,
        compiler_params=_cp(2),
    )(qd, kd, vd, bias, dod, statd)


def dsw_combine(ods, lseds, S):
    nt = S // RT
    dils = [d for _, d in DSW_GROUPS]

    def body(*refs):
        ins, (o_ref, l_ref), stages = refs[:6], refs[6:8], refs[8:]
        for g in range(3):
            if dils[g] > 1:
                _interleave(ins[g], stages[g], dils[g], RT // dils[g])
                _interleave(ins[3 + g], stages[3 + g], dils[g], RT // dils[g])
        for j in range(N_LB):
            natural = lambda a: stages[a][j] if dils[a % 3] > 1 else ins[a][0, :, _lanes(j)]
            o, lse = f_combine(None, *[natural(a) for a in range(6)])
            o_ref[:, _lanes(j)] = o.astype(o_ref.dtype)
            l_ref[:, _lanes(j)] = lse

    dspec = lambda d: pl.BlockSpec((d, RT // d, DSW_HG), lambda i: (0, i, 0))
    nspec = pl.BlockSpec((RT, DSW_HG), lambda i: (i, 0))
    return pl.pallas_call(
        body, name="dsw_combine", grid=(nt,),
        in_specs=[dspec(d) for d in dils] * 2, out_specs=[nspec, nspec],
        out_shape=[SDS((S, DSW_HG), BF16), SDS((S, DSW_HG), F32)],
        scratch_shapes=[pltpu.VMEM((N_LB, RT, LANES), F32)] * 6, compiler_params=_cp(1),
    )(*ods, *lseds)


def dsw_bwd_prep(do, o, lse, S):
    nt = S // RT
    dils = [d for _, d in DSW_GROUPS]

    def body(do_ref, o_ref, l_ref, *rest):
        outs, (st_do, st_stat) = rest[:6], rest[6:]
        lo, first_half = _head_masks(RT)
        for j in range(N_LB):
            dout = do_ref[:, _lanes(j)]
            prod = dout * o_ref[:, _lanes(j)].astype(F32)
            s_all = jnp.sum(prod, axis=1, keepdims=True)
            s_lo = jnp.sum(jnp.where(lo, prod, 0.0), axis=1, keepdims=True)
            delta = jnp.where(lo, s_lo, s_all - s_lo)
            stat = jnp.where(first_half, l_ref[:, _lanes(j)], delta)
            st_do[j] = dout
            st_stat[j] = stat
            for g in range(3):
                if dils[g] == 1:
                    outs[g][0, :, _lanes(j)] = dout.astype(BF16)
                    outs[3 + g][0, :, _lanes(j)] = stat
        for g in range(3):
            if dils[g] > 1:
                _deinterleave(st_do, outs[g], dils[g], RT // dils[g], BF16)
                _deinterleave(st_stat, outs[3 + g], dils[g], RT // dils[g], F32)

    nspec = pl.BlockSpec((RT, DSW_HG), lambda i: (i, 0))
    dspec = lambda d: pl.BlockSpec((d, RT // d, DSW_HG), lambda i: (0, i, 0))
    res = pl.pallas_call(
        body, name="dsw_bwd_prep", grid=(nt,),
        in_specs=[nspec] * 3, out_specs=[dspec(d) for d in dils] * 2,
        out_shape=[SDS((d, S // d, DSW_HG), BF16) for d in dils] + [SDS((d, S // d, DSW_HG), F32) for d in dils],
        scratch_shapes=[pltpu.VMEM((N_LB, RT, LANES), F32)] * 2, compiler_params=_cp(1),
    )(do, o, lse)
    return res[:3], res[3:]


def dsw_forward(h, w_in, q_gain2, k_gain2, rel_bias, w_out):
    S = h.shape[0]
    proj = matmul(h, w_in, "nn", BF16, "dsw_in", col_shards=N_SHARD)
    bias = dsw_bias(rel_bias)
    qkv, ods, lseds = [], [], []
    for gi in range(3):
        qd, kd, vd = dsw_prep(proj, q_gain2, k_gain2, gi, S)
        od, ld = dsw_attn_fwd(qd, kd, vd, bias, gi, S)
        qkv.append((qd, kd, vd))
        ods.append(od)
        lseds.append(ld)
    o, lse = dsw_combine(ods, lseds, S)
    y = matmul(o, w_out, "nn", BF16, "dsw_out", col_shards=N_SHARD)
    return y, dict(h=h, proj=proj, qkv=qkv, bias=bias, o=o, lse=lse)


def dsw_backward(dy, sv, w_in, q_gain2, k_gain2, w_out):
    S = dy.shape[0]
    do = matmul(dy, w_out, "nt", F32, "dsw_out_dx", col_shards=N_SHARD)
    d_w_out = matmul(sv["o"], dy, "tn", F32, "dsw_out_dw", col_shards=N_SHARD)
    dods, statds = dsw_bwd_prep(do, sv["o"], sv["lse"], S)
    pieces_q, pieces_k, pieces_v, dbs = [], [], [], []
    d_qg = jnp.zeros((1, LANES), F32)
    d_kg = jnp.zeros((1, LANES), F32)
    for gi in range(3):
        qd, kd, vd = sv["qkv"][gi]
        dqd, dkd, dvd, db = dsw_attn_bwd(qd, kd, vd, sv["bias"], dods[gi], statds[gi], gi, S)
        dq, dk, dv, dqg, dkg = dsw_prep_bwd(sv["proj"], q_gain2, k_gain2, dqd, dkd, dvd, gi, S)
        dbs.append(db)
        pieces_q.append(dq)
        pieces_k.append(dk)
        pieces_v.append(dv)
        d_qg = d_qg + dqg
        d_kg = d_kg + dkg
    dproj = jnp.concatenate(pieces_q + pieces_k + pieces_v, axis=1)
    d_w_in = matmul(sv["h"], dproj, "tn", F32, "dsw_in_dw", col_shards=N_SHARD)
    dh = matmul(dproj, w_in, "nt", BF16, "dsw_in_dx", col_shards=N_SHARD)
    d_rel = dsw_bias_grad(jnp.concatenate(dbs, axis=0))
    return dh, dict(w_in=d_w_in, q_gain2=d_qg, k_gain2=d_kg, rel=d_rel, w_out=d_w_out)


FT = 256


FUSE_M = 512


def ffn_in_act(h, w_in, name):
    S = h.shape[0]
    half = FFN // 2

    def body(h_ref, wg_ref, wu_ref, gu_ref, a_ref):
        j = pl.program_id(1)
        sub = FUSE_M // 2
        for part in range(2):
            rows = slice(part * sub, (part + 1) * sub)
            hb = h_ref[rows, :]
            g = jnp.dot(hb, wg_ref[...], preferred_element_type=F32)
            u = jnp.dot(hb, wu_ref[...], preferred_element_type=F32)
            a_ref[rows, :] = (_silu(g) * u).astype(a_ref.dtype)
            for jj in range(2):
                @pl.when(j == jj)
                def _(g=g, u=u, jj=jj, rows=rows):
                    gu_ref[rows, jj * half:(jj + 1) * half] = g.astype(gu_ref.dtype)
                    gu_ref[rows, FFN + jj * half:FFN + (jj + 1) * half] = u.astype(gu_ref.dtype)

    return pl.pallas_call(
        body, name=name, grid=(S // FUSE_M, 2),
        in_specs=[pl.BlockSpec((FUSE_M, D), lambda i, j: (i, 0)),
                  pl.BlockSpec((None, D, half), lambda i, j: (j, 0, 0)),
                  pl.BlockSpec((None, D, half), lambda i, j: (j + 2, 0, 0))],
        out_specs=[pl.BlockSpec((FUSE_M, 2 * FFN), lambda i, j: (i, 0)), pl.BlockSpec((FUSE_M, half), lambda i, j: (i, j))],
        out_shape=[SDS((S, 2 * FFN), BF16), SDS((S, FFN), BF16)],
        compiler_params=_cp(2),
    )(h, w_in, w_in)


def ffn_forward(h, w_in, w_out, tag):
    gu, a = ffn_in_act(h, w_in, f"ffn_in_act_{tag}")
    gu_row = Row(gu, (FT, 2 * FFN), lambda i: (i, 0), splits=[FFN, FFN], gdtype=BF16)
    f = matmul(a, w_out, "nn", BF16, f"ffn_out_{tag}")
    return f, dict(h=h, gu_row=gu_row, a=a)


def ffn_out_dx_act(df, w_out, gu, name):
    S = df.shape[0]
    half = FFN // 2

    def body(df_ref, w_ref, g_ref, u_ref, dgu_ref):
        j = pl.program_id(1)
        sub = FUSE_M // 2
        for part in range(2):
            rows = slice(part * sub, (part + 1) * sub)
            da = _nt(df_ref[rows, :], w_ref[...])
            dg, du = _swiglu_bwd((g_ref[rows, :].astype(F32), u_ref[rows, :].astype(F32)), da)
            for jj in range(2):
                @pl.when(j == jj)
                def _(dg=dg, du=du, jj=jj, rows=rows):
                    dgu_ref[rows, jj * half:(jj + 1) * half] = dg.astype(dgu_ref.dtype)
                    dgu_ref[rows, FFN + jj * half:FFN + (jj + 1) * half] = du.astype(dgu_ref.dtype)

    return pl.pallas_call(
        body, name=name, grid=(S // FUSE_M, 2),
        in_specs=[pl.BlockSpec((FUSE_M, D), lambda i, j: (i, 0)),
                  pl.BlockSpec((half, D), lambda i, j: (j, 0)),
                  pl.BlockSpec((FUSE_M, half), lambda i, j: (i, j)),
                  pl.BlockSpec((FUSE_M, half), lambda i, j: (i, j + 2))],
        out_specs=pl.BlockSpec((FUSE_M, 2 * FFN), lambda i, j: (i, 0)),
        out_shape=SDS((S, 2 * FFN), BF16),
        compiler_params=_cp(2),
    )(df, w_out, gu, gu)


def ffn_backward(df, sv, w_in, w_out, tag):
    d_w_out = matmul(sv["a"], df, "tn", F32, f"ffn_out_dw_{tag}")
    dgu = ffn_out_dx_act(df, w_out, sv["gu_row"].arr, f"ffn_out_dx_act_{tag}")
    d_w_in = matmul(sv["h"], dgu, "tn", F32, f"ffn_in_dw_{tag}", col_shards=N_SHARD)
    dh = matmul(dgu, w_in, "nt", BF16, f"ffn_in_dx_{tag}", col_shards=N_SHARD)
    return dh, d_w_in, d_w_out


def f_norm_only(ids, x, gain, sc, sh):
    return (_normmod(x, gain, sc, sh),)


WT = 512


def _wide(a, **kw):
    return Row(a, (WT, D), lambda i: (i, 0), **kw)


def _wide_out(S, dtype):
    return Out((S, D), dtype, (WT, D), lambda i: (i, 0))


def adamw(w, g, m, v, name):
    shape = w.shape
    C = shape[-1]
    R = int(np.prod(shape[:-1]))
    w2, g2, m2, v2 = (a.reshape(R, C) for a in (w, g, m, v))
    br = R
    if R > 256:
        br = max(b for b in range(8, 257, 8) if R % b == 0)
    c1 = 1.0 / (1.0 - ADAM_B1 ** ADAM_STEP)
    c2 = 1.0 / (1.0 - ADAM_B2 ** ADAM_STEP)

    def body(w_ref, g_ref, m_ref, v_ref, d_ref, nm_ref, nv_ref):
        gg = g_ref[...]
        mm_ = ADAM_B1 * m_ref[...] + (1.0 - ADAM_B1) * gg
        vv = ADAM_B2 * v_ref[...] + (1.0 - ADAM_B2) * (gg * gg)
        d_ref[...] = -ADAM_LR * ((mm_ * c1) / (jnp.sqrt(vv * c2) + ADAM_EPS) + ADAM_WD * w_ref[...])
        nm_ref[...] = mm_
        nv_ref[...] = vv

    spec = pl.BlockSpec((br, C), lambda i: (i, 0))
    d, nm, nv = pl.pallas_call(
        body, name=name, grid=(R // br,), in_specs=[spec] * 4, out_specs=[spec] * 3,
        out_shape=[SDS((R, C), F32)] * 3, compiler_params=_cp(1),
    )(w2, g2, m2, v2)
    return d.reshape(shape), nm.reshape(shape), nv.reshape(shape)


def _place():
    x, y, c = lax.axis_index("x"), lax.axis_index("y"), lax.axis_index("c")
    chips = [(1 - x, y), (x, 1 - y), (1 - x, 1 - y)]
    return x, y, c, chips


def all_gather_small(blk, name):
    m_per, n = blk.shape

    def body(x_ref, out_ref, send_sems, recv_sems, local_sem):
        x, y, c, chips = _place()
        me, sibling = (x, y, c), (x, y, 1 - c)

        def rows(px, py, pc):
            return out_ref.at[pl.ds((4 * px + 2 * py + pc) * m_per, m_per), :]

        def copy(k, block, to, src=None):
            return pltpu.make_async_remote_copy(
                src_ref=rows(*block) if src is None else src, dst_ref=rows(*block),
                send_sem=send_sems.at[k], recv_sem=recv_sems.at[k], device_id=to, device_id_type=MESH)

        mine = pltpu.make_async_copy(x_ref, rows(*me), local_sem)
        mine.start()
        first = [copy(0, me, sibling, src=x_ref)]
        first += [copy(1 + j, me, (*chip, c), src=x_ref) for j, chip in enumerate(chips)]
        for cp in first:
            cp.start()
        passed = [copy(4 + j, (*chip, c), sibling) for j, chip in enumerate(chips)]
        for j, chip in enumerate(chips):
            copy(1 + j, (*chip, c), me).wait_recv()
            passed[j].start()
        copy(0, sibling, me).wait_recv()
        for j, chip in enumerate(chips):
            copy(4 + j, (*chip, 1 - c), me).wait_recv()
        for cp in first + passed:
            cp.wait_send()
        mine.wait()

    return pl.pallas_call(
        body, name=name, out_shape=SDS((N_DEV * m_per, n), blk.dtype),
        in_specs=[pl.BlockSpec(memory_space=pltpu.VMEM)], out_specs=pl.BlockSpec(memory_space=pltpu.VMEM),
        scratch_shapes=[pltpu.SemaphoreType.DMA((7,)), pltpu.SemaphoreType.DMA((7,)), pltpu.SemaphoreType.DMA],
    )(blk)


def _half(cc, rh):
    return pl.ds(pl.multiple_of(cc * rh, 16), rh)


def all_gather_shards(ws):
    n = len(ws)

    def body(*refs):
        w_refs, out_refs = refs[:n], refs[n:2 * n]
        send_sems, recv_sems, local_sems, own_sems = refs[2 * n:]
        x, y, c, chips = _place()
        sibling = (x, y, 1 - c)
        s_me = 2 * x + y

        def copy(k, src, dst, to):
            return pltpu.make_async_remote_copy(src_ref=src, dst_ref=dst, send_sem=send_sems.at[k], recv_sem=recv_sems.at[k],
                                                device_id=to, device_id_type=MESH)

        local, sends, passed = [], [], []
        for k in range(n):
            rh = ws[k].shape[0] // 2
            cp = pltpu.make_async_remote_copy(src_ref=w_refs[k], dst_ref=out_refs[k].at[s_me], send_sem=local_sems.at[k],
                                              recv_sem=own_sems.at[k], device_id=sibling, device_id_type=MESH)
            cp.start()
            local.append(cp)
            for j, chip in enumerate(chips):
                sd = copy(6 * k + j, w_refs[k].at[_half(c, rh)], out_refs[k].at[s_me, _half(c, rh)], (*chip, c))
                sd.start()
                sends.append(sd)
        for k in range(n):
            rh = ws[k].shape[0] // 2
            for j, (px, py) in enumerate(chips):
                got = out_refs[k].at[2 * px + py, _half(c, rh)]
                copy(6 * k + j, got, got, (px, py, c)).wait_recv()
                fw = copy(6 * k + 3 + j, got, got, sibling)
                fw.start()
                passed.append(fw)
        for k in range(n):
            rh = ws[k].shape[0] // 2
            for j, (px, py) in enumerate(chips):
                got = out_refs[k].at[2 * px + py, _half(1 - c, rh)]
                copy(6 * k + 3 + j, got, got, sibling).wait_recv()
        for cp in sends + passed:
            cp.wait_send()
        for cp in local:
            cp.wait()

    return pl.pallas_call(
        body, name="weights_all_gather", out_shape=[SDS((N_SHARD,) + w.shape, w.dtype) for w in ws],
        in_specs=[ANY] * n, out_specs=[ANY] * n,
        scratch_shapes=[pltpu.SemaphoreType.DMA((6 * n,)), pltpu.SemaphoreType.DMA((6 * n,)), pltpu.SemaphoreType.DMA((n,)),
                        pltpu.SemaphoreType.DMA((n,))],
    )(*ws)


def sibling_exchange(sends, name):
    n = len(sends)

    def body(*refs):
        s_refs, o_refs, send_sems, recv_sems = refs[:n], refs[n:2 * n], refs[2 * n], refs[2 * n + 1]
        x, y, c, _ = _place()
        cps = [pltpu.make_async_remote_copy(src_ref=s_refs[k], dst_ref=o_refs[k], send_sem=send_sems.at[k], recv_sem=recv_sems.at[k],
                                            device_id=(x, y, 1 - c), device_id_type=MESH) for k in range(n)]
        for cp in cps:
            cp.start()
        for cp in cps:
            cp.wait()

    return pl.pallas_call(
        body, name=name, out_shape=[SDS(s.shape, s.dtype) for s in sends], in_specs=[ANY] * n, out_specs=[ANY] * n,
        scratch_shapes=[pltpu.SemaphoreType.DMA((n,)), pltpu.SemaphoreType.DMA((n,))],
    )(*sends)


def scatter_to_chips(parts):
    n = len(parts)

    def body(*refs):
        p_refs, o_refs, send_sems, recv_sems = refs[:n], refs[n:2 * n], refs[2 * n], refs[2 * n + 1]
        x, y, c, chips = _place()
        cps = []
        for k in range(n):
            for j, (px, py) in enumerate(chips):
                cp = pltpu.make_async_remote_copy(src_ref=p_refs[k].at[2 * px + py], dst_ref=o_refs[k].at[j],
                                                  send_sem=send_sems.at[3 * k + j], recv_sem=recv_sems.at[3 * k + j],
                                                  device_id=(px, py, c), device_id_type=MESH)
                cp.start()
                cps.append(cp)
        for cp in cps:
            cp.wait()

    return pl.pallas_call(
        body, name="grads_scatter", out_shape=[SDS((3,) + p.shape[1:], p.dtype) for p in parts], in_specs=[ANY] * n, out_specs=[ANY] * n,
        scratch_shapes=[pltpu.SemaphoreType.DMA((3 * n,)), pltpu.SemaphoreType.DMA((3 * n,))],
    )(*parts)


def merge_halves(halves):
    n = len(halves)

    def body(*refs):
        h_refs, o_refs = refs[:n], refs[n:2 * n]
        send_sems, recv_sems, local_sems = refs[2 * n:]
        x, y, c, _ = _place()
        local, cps = [], []
        for k in range(n):
            rh = halves[k].shape[0]
            lc = pltpu.make_async_copy(h_refs[k], o_refs[k].at[_half(c, rh)], local_sems.at[k])
            lc.start()
            local.append(lc)
            cp = pltpu.make_async_remote_copy(src_ref=h_refs[k], dst_ref=o_refs[k].at[_half(c, rh)], send_sem=send_sems.at[k],
                                              recv_sem=recv_sems.at[k], device_id=(x, y, 1 - c), device_id_type=MESH)
            cp.start()
            cps.append(cp)
        for k in range(n):
            rh = halves[k].shape[0]
            got = o_refs[k].at[_half(1 - c, rh)]
            pltpu.make_async_remote_copy(src_ref=got, dst_ref=got, send_sem=send_sems.at[k], recv_sem=recv_sems.at[k],
                                         device_id=(x, y, 1 - c), device_id_type=MESH).wait_recv()
        for cp in cps:
            cp.wait_send()
        for lc in local:
            lc.wait()

    return pl.pallas_call(
        body, name="grads_merge_halves", out_shape=[SDS((2 * h.shape[0], h.shape[1]), h.dtype) for h in halves],
        in_specs=[ANY] * n, out_specs=[ANY] * n,
        scratch_shapes=[pltpu.SemaphoreType.DMA((n,)), pltpu.SemaphoreType.DMA((n,)), pltpu.SemaphoreType.DMA((n,))],
    )(*halves)


def add_rows(arrs, out_dtype, name, rt=256):
    Rr, W = arrs[0].shape

    def fn(ids, *vals):
        acc = vals[0]
        for v in vals[1:]:
            acc = acc + v
        return (acc,)

    t = rt if Rr % rt == 0 else max(b for b in range(16, rt + 1, 16) if Rr % b == 0)
    (out,) = rowwise(fn, [Row(a, (t, W), lambda i: (i, 0)) for a in arrs], [],
                     [Out((Rr, W), out_dtype, (t, W), lambda i: (i, 0))], (Rr // t,), name)
    return out


HBM_SPEC = pl.BlockSpec(memory_space=pltpu.HBM)
SEM_SPEC = pl.BlockSpec(memory_space=pltpu.SEMAPHORE)
DATAFLOW = pltpu.SideEffectType.DATAFLOW_SIDE_EFFECTING


def _in_hbm(a):
    return pltpu.with_memory_space_constraint(a, pltpu.HBM)


def _gather_copies(w_refs, land_refs, send_sems, recv_sems):
    x, y, c, chips = _place()
    targets = [(x, y, 1 - c)] + [(*chip, c) for chip in chips]
    cps = []
    for k, (w_ref, land_ref) in enumerate(zip(w_refs, land_refs)):
        for j, to in enumerate(targets):
            cps.append(pltpu.make_async_remote_copy(src_ref=w_ref, dst_ref=land_ref.at[2 * x + y], send_sem=send_sems.at[4 * k + j],
                                                    recv_sem=recv_sems.at[4 * k + j], device_id=to, device_id_type=MESH))
    return cps


def _scatter_copies(p_refs, land_refs, send_sems, recv_sems):
    x, y, c, chips = _place()
    cps = []
    for k, (p_ref, land_ref) in enumerate(zip(p_refs, land_refs)):
        for j, (px, py) in enumerate(chips):
            cps.append(pltpu.make_async_remote_copy(src_ref=p_ref.at[2 * px + py], dst_ref=land_ref.at[j], send_sem=send_sems.at[3 * k + j],
                                                    recv_sem=recv_sems.at[3 * k + j], device_id=(px, py, c), device_id_type=MESH))
    return cps


def copies_start(srcs, land_shapes, make_copies, per_src, name):
    n = len(srcs)
    m = per_src * n

    def body(*refs):
        src_refs, land_refs = refs[:n], refs[n:2 * n]
        send_sems, recv_sems, token = refs[2 * n], refs[2 * n + 1], refs[-1]
        for cp in make_copies(src_refs, land_refs, send_sems, recv_sems):
            cp.start()
        token[...] = jnp.zeros_like(token)

    lands = [lax.empty(shp, s.dtype) for shp, s in zip(land_shapes, srcs)]
    res = pl.pallas_call(
        body, name=name,
        out_shape=(pltpu.SemaphoreType.DMA((m,)), pltpu.SemaphoreType.DMA((m,)), *[pltpu.HBM(s.shape, s.dtype) for s in srcs],
                   *[pltpu.HBM(shp, s.dtype) for shp, s in zip(land_shapes, srcs)], SDS((8, LANES), F32)),
        in_specs=[HBM_SPEC] * (2 * n),
        out_specs=(SEM_SPEC, SEM_SPEC, *[HBM_SPEC] * (2 * n), pl.BlockSpec(memory_space=pltpu.VMEM)),
        input_output_aliases={i: 2 + i for i in range(2 * n)},
        compiler_params=pltpu.CompilerParams(has_side_effects=DATAFLOW),
    )(*[_in_hbm(s) for s in srcs], *[_in_hbm(l) for l in lands])
    return res[0], res[1], list(res[2:2 + n]), list(res[2 + n:2 + 2 * n]), res[-1]


def copies_wait(send_sems, recv_sems, srcs, lands, make_copies, after, name):
    n = len(srcs)

    def body(*refs):
        src_refs, land_refs = refs[:n], refs[n:2 * n]
        for cp in make_copies(src_refs, land_refs, refs[2 * n], refs[2 * n + 1]):
            cp.wait_send()
            cp.wait_recv()

    res = pl.pallas_call(
        body, name=name,
        out_shape=(*[pltpu.HBM(s.shape, s.dtype) for s in srcs], *[pltpu.HBM(l.shape, l.dtype) for l in lands]),
        in_specs=[HBM_SPEC] * (2 * n) + [SEM_SPEC, SEM_SPEC, ANY],
        out_specs=tuple([HBM_SPEC] * (2 * n)),
        input_output_aliases={i: i for i in range(2 * n)},
        compiler_params=pltpu.CompilerParams(has_side_effects=DATAFLOW),
    )(*srcs, *lands, send_sems, recv_sems, after)
    return list(res[n:])


PACK = (("gdn_w_in", 2), ("gdn_w_out", 1), ("w_ffn_in", 2), ("w_ffn_out", 1), ("dsw_w_in", 2), ("dsw_w_out", 2))
PACK_ALIGN = 32


def _pack_rows(sizes):
    total = sum(sizes)
    rows = -(-total // D)
    return -(-rows // PACK_ALIGN) * PACK_ALIGN


def pack_blocks(blocks, dtype):
    flat = [b.astype(dtype).reshape(-1) for b in blocks]
    total = sum(f.shape[0] for f in flat)
    R = _pack_rows([f.shape[0] for f in flat])
    flat.append(jnp.zeros((R * D - total,), dtype))
    return jnp.concatenate(flat).reshape(R, D)


def unpack_blocks(buf, shapes):
    flat = buf.reshape(-1)
    out, off = [], 0
    for shp in shapes:
        n = int(np.prod(shp))
        out.append(flat[off:off + n].reshape(shp))
        off += n
    return out


def _shard_slice(a, axis, s):
    n = a.shape[axis] // N_SHARD
    return lax.slice_in_dim(a, s * n, (s + 1) * n, axis=axis)


def _pad_lanes(v):
    return jnp.concatenate([v.astype(F32), jnp.zeros((LANES - v.shape[0],), F32)])[None]


def kernel(x, c, w_ada, b_ada, norm_mix, norm_ffn, w_ffn_in, w_ffn_out, gdn_w_in, gdn_conv, gdn_a_log, gdn_dt_bias, gdn_out_norm, gdn_w_out, dsw_w_in, dsw_q_norm, dsw_k_norm, dsw_w_out, rel_bias, loss_target, m_w_ada, m_b_ada, m_norm_mix, m_norm_ffn, m_w_ffn_in, m_w_ffn_out, m_gdn_w_in, m_gdn_conv, m_gdn_a_log, m_gdn_dt_bias, m_gdn_out_norm, m_gdn_w_out, m_dsw_w_in, m_dsw_q_norm, m_dsw_k_norm, m_dsw_w_out, m_rel_bias, v_w_ada, v_b_ada, v_norm_mix, v_norm_ffn, v_w_ffn_in, v_w_ffn_out, v_gdn_w_in, v_gdn_conv, v_gdn_a_log, v_gdn_dt_bias, v_gdn_out_norm, v_gdn_w_out, v_dsw_w_in, v_dsw_q_norm, v_dsw_k_norm, v_dsw_w_out, v_rel_bias):
    S = x.shape[1]
    nt = S // WT
    xi, yi, ci = lax.axis_index("x"), lax.axis_index("y"), lax.axis_index("c")
    me = 4 * xi + 2 * yi + ci
    s_me = 2 * xi + yi
    x0, tgt = x[0], loss_target[0]
    shard = dict(w_ffn_in=w_ffn_in, w_ffn_out=w_ffn_out, gdn_w_in=gdn_w_in, gdn_w_out=gdn_w_out, dsw_w_in=dsw_w_in, dsw_w_out=dsw_w_out)

    whole = lambda a: Row(a, a.shape, lambda i: (0,) * a.ndim)
    (cond8,) = rowwise(lambda ids, v: (_silu(v),), [whole(c.reshape(8, LANES))], [], [Out((8, LANES), F32, (8, LANES), lambda i: (0, 0))], (1,), "cond")
    cond_all = all_gather_small(cond8, "gather_cond").reshape(N_DEV, D)
    cond16 = jnp.concatenate([cond_all, jnp.zeros((8, D), F32)], axis=0)
    ada_cols = w_ada.shape[2]
    mods = [matmul(cond16, w_ada[l], "nn", F32, f"ada_{l}")[:N_DEV] for l in range(2)]
    buf = jnp.concatenate([jnp.stack(mods, axis=1).reshape(-1, LANES), gdn_conv.reshape(-1, LANES)], axis=0)
    n_mod_rows = N_DEV * 2 * ada_cols // LANES
    got = all_gather_small(buf, "gather_mod").reshape(N_DEV, buf.shape[0], LANES)
    mod_parts, conv_parts = [], []
    for s in range(N_SHARD):
        from_dev = got[2 * s]
        mod_parts.append(lax.dynamic_index_in_dim(from_dev[:n_mod_rows].reshape(N_DEV, 2, ada_cols), me, 0, keepdims=False))
        conv_parts.append(from_dev[n_mod_rows:].reshape(4, -1))
    mod_nb = jnp.concatenate(mod_parts, axis=1)
    conv_w = jnp.concatenate(conv_parts, axis=1)
    (mod,) = rowwise(lambda ids, a, b: (a + b,), [whole(mod_nb), whole(b_ada)], [], [Out(mod_nb.shape, F32, mod_nb.shape, lambda i: (0, 0))], (1,), "mod_bias")
    mod = mod.reshape(2, 6, 1, D)
    sh1, sc1, g1, sh2, sc2, g2 = ([mod[l, k] for l in range(2)] for k in range(6))
    gmix = [norm_mix[l][None] for l in range(2)]
    gffn = [norm_ffn[l][None] for l in range(2)]

    gcols = gdn_w_in.shape[2]
    g_gdn_in, g_gdn_out = all_gather_shards([gdn_w_in[0].astype(BF16), gdn_w_out[0].astype(BF16)])
    gathered = lambda ws: [(N_SHARD,) + w.shape for w in ws]
    gate = (jnp.minimum(jnp.abs(g_gdn_in[0, 0, 0].astype(F32)), 0.0) + jnp.minimum(jnp.abs(mod[0, 0, 0, 0]), 0.0)).astype(BF16)
    w2 = [w_ffn_in[0].astype(BF16) + gate, w_ffn_out[0].astype(BF16) + gate]
    w3 = [dsw_w_in[0].astype(BF16) + gate, dsw_w_out[0].astype(BF16) + gate, w_ffn_in[1].astype(BF16) + gate, w_ffn_out[1].astype(BF16) + gate]
    fly2 = copies_start(w2, gathered(w2), _gather_copies, 4, "weights_ffn0_start")
    fly3 = copies_start(w3, gathered(w3), _gather_copies, 4, "weights_layer1_start")
    started = fly2[4][0, 0] + fly3[4][0, 0]
    w_gdn = jnp.concatenate([g_gdn_in[s] for s in range(N_SHARD)] + [jnp.zeros((D, GDN_PROJ - N_SHARD * gcols), BF16)], axis=1)
    alog, dtb = _pad_lanes(gdn_a_log[0]), _pad_lanes(gdn_dt_bias[0])
    qg2 = jnp.concatenate([dsw_q_norm, dsw_q_norm], axis=1)
    kg2 = jnp.concatenate([dsw_k_norm, dsw_k_norm], axis=1)
    w_gdn_out = g_gdn_out.reshape(GDN_H * LANES, D)
    gdn_args = (w_gdn, conv_w, alog, dtb, gdn_out_norm, w_gdn_out)
    sc1[0] = sc1[0] + started

    (h10,) = rowwise(f_norm_only, [_wide(x0)], [gmix[0], sc1[0], sh1[0]], [_wide_out(S, BF16)], (nt,), "l0_norm")
    y0, sv_g = gdn_forward(h10, *gdn_args)
    x1, h20 = rowwise(f_resid_norm, [_wide(x0), _wide(y0)], [g1[0], gffn[0], sc2[0], sh2[0]], [_wide_out(S, F32), _wide_out(S, BF16)], (nt,), "l0_mid")
    g_in0, g_out0 = copies_wait(*fly2[:4], _gather_copies, y0, "weights_ffn0_wait")
    w_ffn = [(g_in0, g_out0.reshape(FFN, D)), None]
    f0, sv_f0 = ffn_forward(h20, *w_ffn[0], "0")
    x2, h11 = rowwise(f_resid_norm, [_wide(x1), _wide(f0)], [g2[0], gmix[1], sc1[1], sh1[1]], [_wide_out(S, F32), _wide_out(S, BF16)], (nt,), "l1_in")
    g_dsw_in, g_dsw_out, g_in1, g_out1 = copies_wait(*fly3[:4], _gather_copies, f0, "weights_layer1_wait")
    w_ffn[1] = (g_in1, g_out1.reshape(FFN, D))
    dsw_args = (g_dsw_in, qg2, kg2)
    y1, sv_d = dsw_forward(h11, *dsw_args, rel_bias, g_dsw_out)
    x3, h21 = rowwise(f_resid_norm, [_wide(x2), _wide(y1)], [g1[1], gffn[1], sc2[1], sh2[1]], [_wide_out(S, F32), _wide_out(S, BF16)], (nt,), "l1_mid")
    f1, sv_f1 = ffn_forward(h21, *w_ffn[1], "1")
    parts, dx3, df1, dg2_1 = loss_and_grad(x3, f1, tgt, g2[1], S)
    loss = lax.psum(jnp.sum(parts), ("x", "y", "c"))

    dh21, d_win1, d_wout1 = ffn_backward(df1, sv_f1, *w_ffn[1], "1")
    (dx2, dy1), (dg1_1, dgf1, dsc2_1, dsh2_1) = rowwise_bwd(
        f_resid_norm, [_wide(x2), _wide(y1, gdtype=BF16)], [g1[1], gffn[1], sc2[1], sh2[1]], [_wide(dx3), _wide(dh21)], (nt,), "l1_mid_bwd")
    dh11, g_d = dsw_backward(dy1, sv_d, *dsw_args, g_dsw_out)
    (dx1, df0), (dg2_0, dgm1, dsc1_1, dsh1_1) = rowwise_bwd(
        f_resid_norm, [_wide(x1), _wide(f0, gdtype=BF16)], [g2[0], gmix[1], sc1[1], sh1[1]], [_wide(dx2), _wide(dh11)], (nt,), "l1_in_bwd")
    by_shard = lambda a: a.reshape(N_SHARD, a.shape[0] // N_SHARD, a.shape[1])
    landing = lambda ps: [(3,) + p.shape[1:] for p in ps]
    dws3 = [g_d["w_in"], g_d["w_out"], d_win1, by_shard(d_wout1)]
    parts3 = [a.astype(BF16) for a in dws3]
    gfly3 = copies_start(parts3, landing(parts3), _scatter_copies, 3, "grads_layer1_start")
    w_out0 = w_ffn[0][1] + gfly3[4][0, 0].astype(BF16)
    dh20, d_win0, d_wout0 = ffn_backward(df0, sv_f0, w_ffn[0][0], w_out0, "0")
    (dx0p, dy0), (dg1_0, dgf0, dsc2_0, dsh2_0) = rowwise_bwd(
        f_resid_norm, [_wide(x0), _wide(y0, gdtype=BF16)], [g1[0], gffn[0], sc2[0], sh2[0]], [_wide(dx1), _wide(dh20)], (nt,), "l0_mid_bwd")
    dws2 = [d_win0, by_shard(d_wout0)]
    parts2 = [a.astype(BF16) for a in dws2]
    gfly2 = copies_start(parts2, landing(parts2), _scatter_copies, 3, "grads_ffn0_start")
    gdn_args = gdn_args[:5] + (w_gdn_out + gfly2[4][0, 0].astype(BF16),)
    dh10, g_g = gdn_backward(dy0, sv_g, *gdn_args)
    (grad_x,), (dgm0, dsc1_0, dsh1_0) = rowwise_bwd(f_first, [_wide(x0)], [gmix[0], sc1[0], sh1[0]], [_wide(dx0p), _wide(dh10)], (nt,), "l0_norm_bwd")

    dmod = jnp.concatenate([dsh1_0, dsc1_0, dg1_0, dsh2_0, dsc2_0, dg2_0, dsh1_1, dsc1_1, dg1_1, dsh2_1, dsc2_1, dg2_1], axis=1)
    d_rel = jnp.transpose(g_d["rel"][:, :, 0])
    fold = lambda v: v[:, :DSW_DH] + v[:, DSW_DH:]
    small = [dmod, jnp.concatenate([dgm0, dgm1], axis=1), jnp.concatenate([dgf0, dgf1], axis=1), g_g["conv"].reshape(1, -1),
             g_g["alog"], g_g["dtb"], g_g["gain"], _pad_lanes(fold(g_d["q_gain2"])[0]), _pad_lanes(fold(g_d["k_gain2"])[0]),
             d_rel.reshape(1, -1)]
    used = [v.shape[1] // LANES for v in small]
    sizes = [-(-u // 8) * 8 for u in used]
    pad8 = lambda v, u, s: jnp.concatenate([v.reshape(u, LANES), jnp.zeros((s - u, LANES), F32)], axis=0) if s > u else v.reshape(u, LANES)
    pad_rows = sum(sizes)
    sbuf = jnp.concatenate([pad8(v, u, s) for v, u, s in zip(small, used, sizes)], axis=0)
    sgot = all_gather_small(sbuf, "gather_small_grads")
    ssum = add_rows([sgot[d * pad_rows:(d + 1) * pad_rows] for d in range(N_DEV)], F32, "sum_small_grads", rt=pad_rows)
    offs = np.cumsum([0] + sizes)
    take = lambda k: ssum[offs[k]:offs[k] + used[k]].reshape(1, -1)
    grad_b_ada = take(0).reshape(2, 6 * D)
    grad_norm_mix = take(1).reshape(2, D)
    grad_norm_ffn = take(2).reshape(2, D)
    conv_full = take(3).reshape(4, -1)
    ncv = gdn_conv.shape[2]
    grad_gdn_conv = lax.dynamic_slice_in_dim(conv_full, s_me * ncv, ncv, axis=1)[None]
    grad_a_log = take(4)[:, :GDN_H]
    grad_dt_bias = take(5)[:, :GDN_H]
    grad_out_norm = take(6)
    grad_q_norm = take(7)[:, :DSW_DH]
    grad_k_norm = take(8)[:, :DSW_DH]
    grad_rel = take(9).reshape(REL_BUCKETS, 3 * GDN_H)
    dmod_all = sgot.reshape(N_DEV, pad_rows, LANES)[:, :used[0]].reshape(N_DEV, 2, 6 * D)
    dmod_mine = lax.dynamic_slice_in_dim(dmod_all, s_me * ada_cols, ada_cols, axis=2)
    dmod16 = jnp.concatenate([dmod_mine, jnp.zeros_like(dmod_mine)], axis=0)
    grad_w_ada = jnp.stack([matmul(cond16, dmod16[:, l], "tn", F32, f"ada_dw_{l}") for l in range(2)])

    dg_in = jnp.stack([g_g["w_in"][:, s * gcols:(s + 1) * gcols] for s in range(N_SHARD)])
    dws = [dg_in, by_shard(g_g["w_out"])]
    keeps, gives = [], []
    for a in dws:
        rh = a.shape[1] // 2
        keeps.append(lax.dynamic_slice_in_dim(a, ci * rh, rh, axis=1))
        gives.append(lax.dynamic_slice_in_dim(a, (1 - ci) * rh, rh, axis=1).astype(BF16))
    from_sib = sibling_exchange(gives, "grads_to_sibling")
    flat2 = lambda a: a.reshape(-1, a.shape[-1])
    parts = [add_rows([flat2(k_), flat2(f_)], BF16, f"grads_chip_sum_{i}").reshape(k_.shape) for i, (k_, f_) in enumerate(zip(keeps, from_sib))]
    others = scatter_to_chips(parts)
    halves = []
    for i, (p_, o_) in enumerate(zip(parts, others)):
        own = lax.dynamic_index_in_dim(p_, s_me, 0, keepdims=False)
        halves.append(add_rows([own, o_[0], o_[1], o_[2]], F32, f"grads_sum_{i}"))
    sib_halves = sibling_exchange(halves, "grads_from_sibling")

    def whole_shard(mine, theirs):
        both = jnp.stack([mine, theirs])
        return jnp.concatenate([lax.dynamic_index_in_dim(both, ci, 0, keepdims=False),
                                lax.dynamic_index_in_dim(both, 1 - ci, 0, keepdims=False)], axis=0)

    s_gdn_in, s_gdn_out = [whole_shard(a, b) for a, b in zip(halves, sib_halves)]
    got3 = copies_wait(*gfly3[:4], _scatter_copies, grad_x, "grads_layer1_wait")
    got2 = copies_wait(*gfly2[:4], _scatter_copies, grad_x, "grads_ffn0_wait")
    core_sums = []
    for i, (full, got) in enumerate(zip(dws3 + dws2, got3 + got2)):
        own = lax.dynamic_index_in_dim(full, s_me, 0, keepdims=False)
        core_sums.append(add_rows([own, got[0], got[1], got[2]], F32, f"grads_core_sum_{i}"))
    sib_sums = sibling_exchange(core_sums, "grads_core_sums_swap")
    s_dsw_in, s_dsw_out, s_in1, s_out1, s_in0, s_out0 = [add_rows([a, b], F32, f"grads_chip_total_{i}")
                                                          for i, (a, b) in enumerate(zip(core_sums, sib_sums))]
    gsh = dict(gdn_w_in=s_gdn_in[None], gdn_w_out=s_gdn_out[None],
               w_ffn_in=jnp.stack([s_in0, s_in1]), w_ffn_out=jnp.stack([s_out0, s_out1]),
               dsw_w_in=s_dsw_in[None], dsw_w_out=s_dsw_out[None])

    grads = dict(w_ada=grad_w_ada, b_ada=grad_b_ada, norm_mix=grad_norm_mix, norm_ffn=grad_norm_ffn, w_ffn_in=gsh["w_ffn_in"],
                 w_ffn_out=gsh["w_ffn_out"], gdn_w_in=gsh["gdn_w_in"], gdn_conv=grad_gdn_conv, gdn_a_log=grad_a_log,
                 gdn_dt_bias=grad_dt_bias, gdn_out_norm=grad_out_norm, gdn_w_out=gsh["gdn_w_out"], dsw_w_in=gsh["dsw_w_in"],
                 dsw_q_norm=grad_q_norm, dsw_k_norm=grad_k_norm, dsw_w_out=gsh["dsw_w_out"], rel_bias=grad_rel)
    weights = dict(w_ada=w_ada, b_ada=b_ada, norm_mix=norm_mix, norm_ffn=norm_ffn, w_ffn_in=w_ffn_in, w_ffn_out=w_ffn_out,
                   gdn_w_in=gdn_w_in, gdn_conv=gdn_conv, gdn_a_log=gdn_a_log, gdn_dt_bias=gdn_dt_bias, gdn_out_norm=gdn_out_norm,
                   gdn_w_out=gdn_w_out, dsw_w_in=dsw_w_in, dsw_q_norm=dsw_q_norm, dsw_k_norm=dsw_k_norm, dsw_w_out=dsw_w_out,
                   rel_bias=rel_bias)
    ms = dict(w_ada=m_w_ada, b_ada=m_b_ada, norm_mix=m_norm_mix, norm_ffn=m_norm_ffn, w_ffn_in=m_w_ffn_in, w_ffn_out=m_w_ffn_out,
              gdn_w_in=m_gdn_w_in, gdn_conv=m_gdn_conv, gdn_a_log=m_gdn_a_log, gdn_dt_bias=m_gdn_dt_bias, gdn_out_norm=m_gdn_out_norm,
              gdn_w_out=m_gdn_w_out, dsw_w_in=m_dsw_w_in, dsw_q_norm=m_dsw_q_norm, dsw_k_norm=m_dsw_k_norm, dsw_w_out=m_dsw_w_out,
              rel_bias=m_rel_bias)
    vs = dict(w_ada=v_w_ada, b_ada=v_b_ada, norm_mix=v_norm_mix, norm_ffn=v_norm_ffn, w_ffn_in=v_w_ffn_in, w_ffn_out=v_w_ffn_out,
              gdn_w_in=v_gdn_w_in, gdn_conv=v_gdn_conv, gdn_a_log=v_gdn_a_log, gdn_dt_bias=v_gdn_dt_bias, gdn_out_norm=v_gdn_out_norm,
              gdn_w_out=v_gdn_w_out, dsw_w_in=v_dsw_w_in, dsw_q_norm=v_dsw_q_norm, dsw_k_norm=v_dsw_k_norm, dsw_w_out=v_dsw_w_out,
              rel_bias=v_rel_bias)
    names = list(weights)
    deltas, new_m, new_v = [], [], []
    for n in names:
        g = grads[n].reshape(weights[n].shape)
        grads[n] = g
        d, nm, nv = adamw(weights[n], g, ms[n], vs[n], f"adamw_{n}")
        deltas.append(d)
        new_m.append(nm)
        new_v.append(nv)
    return (loss, grad_x[None], *[grads[n] for n in names], *deltas, *new_m, *new_v)
```

```python
import functools
import math

import numpy as np
import jax
import jax.numpy as jnp
from jax import lax
from jax.experimental import pallas as pl
from jax.experimental.pallas import tpu as pltpu

F32 = jnp.float32
BF16 = jnp.bfloat16
SDS = jax.ShapeDtypeStruct
MESH = pl.DeviceIdType.MESH
ANY = pl.BlockSpec(memory_space=pl.ANY)

D = 1024
EPS = 1e-6
LANES = 128
GDN_H = 8
GDN_DK = 128
GDN_C = 64
DSW_GROUPS = ((128, 1), (512, 4), (2048, 16))
DSW_SPAN = 128
DSW_DH = 64
DSW_HG = 512
REL_BUCKETS = 32
REL_MAX_DIST = 2048
FFN = 2816
N_SHARD = 4
N_DEV = 8
VMEM_LIMIT = 48 * 1024 * 1024
NEG = -1e30

ADAM_LR, ADAM_B1, ADAM_B2, ADAM_EPS, ADAM_WD, ADAM_STEP = 0.001, 0.9, 0.999, 1e-08, 0.01, 10


def _cp(n_axes):
    return pltpu.CompilerParams(dimension_semantics=("arbitrary",) * n_axes, vmem_limit_bytes=VMEM_LIMIT)


def _blk(dim, cap):
    if dim <= cap:
        return dim
    best = None
    for b in range(LANES, cap + 1, LANES):
        if dim % b == 0:
            best = b
    assert best is not None, (dim, cap)
    return best


MAX_SHARD_BLOCK = 1408
def matmul(a, b, mode, out_dtype, name, cap_m=MAX_SHARD_BLOCK, cap_n=MAX_SHARD_BLOCK, cap_k=2048, col_shards=0):
    ns = col_shards
    if mode == "nn":
        (M, K) = a.shape
        K2, N = (b.shape[1], ns * b.shape[2]) if ns else b.shape
    elif mode == "nt":
        (M, K) = a.shape
        N, K2 = (b.shape[1], ns * b.shape[2]) if ns else b.shape
    else:
        (K, M), (K2, N) = a.shape, b.shape
    assert K == K2, (a.shape, b.shape, mode)
    if K <= 3072:
        cap_k = K
        if K > 2048:
            cap_n = 1024
    n_unit = N // ns if (ns and mode != "nt") else N
    k_unit = K // ns if (ns and mode == "nt") else K
    bm = _blk(M, cap_m)
    bn = _blk(n_unit, MAX_SHARD_BLOCK) if n_unit != N else _blk(N, cap_n)
    if k_unit != K:
        bk = _blk(k_unit, MAX_SHARD_BLOCK)
    else:
        bk = _blk(K, 1024 if (ns and mode == "tn") else cap_k)
    nk = K // bk
    nps, kps = n_unit // bn, k_unit // bk
    dims = {"nn": ((1,), (0,)), "nt": ((1,), (1,)), "tn": ((0,), (0,))}[mode]

    def dot(a_ref, b_ref):
        return lax.dot_general(a_ref[...].astype(BF16), b_ref[...].astype(BF16), (dims, ((), ())), preferred_element_type=F32)

    def body_one(a_ref, b_ref, o_ref):
        o_ref[...] = dot(a_ref, b_ref).astype(o_ref.dtype)

    def body_acc(a_ref, b_ref, o_ref, acc_ref):
        k = pl.program_id(2)

        @pl.when(k == 0)
        def _():
            acc_ref[...] = jnp.zeros_like(acc_ref)

        acc_ref[...] += dot(a_ref, b_ref)

        @pl.when(k == nk - 1)
        def _():
            o_ref[...] = acc_ref[...].astype(o_ref.dtype)

    a_spec = pl.BlockSpec((bk, bm), lambda i, j, k: (k, i)) if mode == "tn" else pl.BlockSpec((bm, bk), lambda i, j, k: (i, k))
    if mode == "nt":
        b_spec = pl.BlockSpec((None, bn, bk), lambda i, j, k: (k // kps, j, k % kps)) if ns else pl.BlockSpec((bn, bk), lambda i, j, k: (j, k))
    elif mode == "nn" and ns:
        b_spec = pl.BlockSpec((None, bk, bn), lambda i, j, k: (j // nps, k, j % nps))
    else:
        b_spec = pl.BlockSpec((bk, bn), lambda i, j, k: (k, j))
    if mode == "tn" and ns:
        o_spec, o_shape = pl.BlockSpec((None, bm, bn), lambda i, j, k: (j // nps, i, j % nps)), (ns, M, n_unit)
    else:
        o_spec, o_shape = pl.BlockSpec((bm, bn), lambda i, j, k: (i, j)), (M, N)
    return pl.pallas_call(
        body_one if nk == 1 else body_acc, name=name, grid=(M // bm, N // bn, nk),
        in_specs=[a_spec, b_spec], out_specs=o_spec,
        out_shape=SDS(o_shape, out_dtype), scratch_shapes=[] if nk == 1 else [pltpu.VMEM((bm, bn), F32)],
        compiler_params=_cp(3),
    )(a, b)


class Row:
    def __init__(self, arr, bshape, imap, splits=None, diff=True, acc=False, gdtype=F32, gshape=None, gbshape=None, gimap=None,
                 lead=0):
        self.arr, self.bshape, self.imap = arr, tuple(bshape), imap
        self.splits, self.lead = splits, lead
        self.diff, self.acc, self.gdtype = diff, acc, gdtype
        self.gshape = tuple(arr.shape) if gshape is None else tuple(gshape)
        self.gbshape = self.bshape if gbshape is None else tuple(gbshape)
        self.gimap = imap if gimap is None else gimap

    def gspec(self):
        return pl.BlockSpec(self.gbshape, self.gimap)

    def spec(self):
        return pl.BlockSpec(self.bshape, self.imap)

    def pieces(self, ref):
        return _load_pieces(ref, self.splits, self.lead)

    def n_pieces(self):
        return _n_pieces(self.splits, self.lead)


class Out:
    def __init__(self, shape, dtype, bshape, imap, splits=None, lead=0):
        self.shape, self.dtype, self.bshape, self.imap = tuple(shape), dtype, tuple(bshape), imap
        self.splits, self.lead = splits, lead

    def n_pieces(self):
        return _n_pieces(self.splits, self.lead)


def _n_pieces(splits, lead):
    return lead if lead else (1 if splits is None else len(splits))


def _load_pieces(ref, splits, lead):
    if lead:
        return [ref[k].astype(F32) for k in range(lead)]
    if splits is None:
        return [ref[...].astype(F32)]
    out, o = [], 0
    for w in splits:
        out.append(ref[..., o:o + w].astype(F32))
        o += w
    return out


def _store_pieces(ref, splits, lead, vals, accumulate=False):
    def put(idx, v):
        if accumulate:
            ref[idx] += v.astype(ref.dtype)
        else:
            ref[idx] = v.astype(ref.dtype)

    if lead:
        for k in range(lead):
            put(k, vals[k])
    elif splits is None:
        put(..., vals[0])
    else:
        o = 0
        for w, v in zip(splits, vals):
            put((..., slice(o, o + w)), v)
            o += w


def rowwise(fn, rows, params, outs, grid, name):
    nr, npar = len(rows), len(params)

    def body(*refs):
        ids = tuple(pl.program_id(a) for a in range(len(grid)))
        vals = []
        for r, ref in zip(rows, refs[:nr]):
            vals += r.pieces(ref)
        pvals = [ref[...].astype(F32) for ref in refs[nr:nr + npar]]
        res = list(fn(ids, *vals, *pvals))
        o = 0
        for spec, ref in zip(outs, refs[nr + npar:]):
            n = spec.n_pieces()
            _store_pieces(ref, spec.splits, spec.lead, res[o:o + n])
            o += n

    nz = len(grid)
    pspecs = [pl.BlockSpec(p.shape, (lambda *ids, _n=p.ndim: (0,) * _n)) for p in params]
    res = pl.pallas_call(
        body, name=name, grid=grid,
        in_specs=[r.spec() for r in rows] + pspecs,
        out_specs=[pl.BlockSpec(o.bshape, o.imap) for o in outs],
        out_shape=[SDS(o.shape, o.dtype) for o in outs],
        compiler_params=_cp(nz),
    )(*[r.arr for r in rows], *params)
    return list(res)


def rowwise_bwd(fn, rows, params, cots, grid, name):
    nr, npar, nc = len(rows), len(params), len(cots)
    drows = [r for r in rows if r.diff]
    nz = len(grid)

    def body(*refs):
        ids = tuple(pl.program_id(a) for a in range(nz))
        row_refs, par_refs = refs[:nr], refs[nr:nr + npar]
        cot_refs = refs[nr + npar:nr + npar + nc]
        drow_refs = refs[nr + npar + nc:nr + npar + nc + len(drows)]
        dpar_refs = refs[nr + npar + nc + len(drows):]
        pieces, is_diff = [], []
        for r, ref in zip(rows, row_refs):
            ps = r.pieces(ref)
            pieces += ps
            is_diff += [r.diff] * len(ps)
        pvals = [ref[...].astype(F32) for ref in par_refs]
        dvals = [p for p, dflag in zip(pieces, is_diff) if dflag]
        nd = len(dvals)

        def f(*args):
            it = iter(args[:nd])
            full = [next(it) if dflag else p for p, dflag in zip(pieces, is_diff)]
            return tuple(fn(ids, *full, *args[nd:]))

        _, vjp = jax.vjp(f, *dvals, *pvals)
        cvals = []
        for c, ref in zip(cots, cot_refs):
            cvals += c.pieces(ref)
        g = vjp(tuple(cvals))
        o = 0
        first_inner = ids[-1] == 0
        for r, ref in zip(drows, drow_refs):
            n = r.n_pieces()
            gs = g[o:o + n]
            o += n
            if r.acc:
                @pl.when(first_inner)
                def _(ref=ref):
                    ref[...] = jnp.zeros_like(ref)
            _store_pieces(ref, r.splits, r.lead, gs, accumulate=r.acc)
        first = functools.reduce(jnp.logical_and, [i == 0 for i in ids])
        for ref, gp in zip(dpar_refs, g[nd:]):
            @pl.when(first)
            def _(ref=ref):
                ref[...] = jnp.zeros_like(ref)
            ref[...] += gp

    pspecs = [pl.BlockSpec(p.shape, (lambda *ids, _n=p.ndim: (0,) * _n)) for p in params]
    res = pl.pallas_call(
        body, name=name, grid=grid,
        in_specs=[r.spec() for r in rows] + pspecs + [c.spec() for c in cots],
        out_specs=[r.gspec() for r in drows] + pspecs,
        out_shape=[SDS(r.gshape, r.gdtype) for r in drows] + [SDS(p.shape, F32) for p in params],
        compiler_params=_cp(nz),
    )(*[r.arr for r in rows], *params, *[c.arr for c in cots])
    res = list(res)
    return res[:len(drows)], res[len(drows):]


def _sigmoid(x):
    return 0.5 * (jnp.tanh(0.5 * x) + 1.0)


def _silu(x):
    return x * _sigmoid(x)


def _normmod(x, gain, sc, sh):
    inv = lax.rsqrt(jnp.mean(x * x, axis=-1, keepdims=True) + EPS)
    return x * inv * gain * (1.0 + sc) + sh


def f_first(ids, x, gain, sc, sh):
    return x, _normmod(x, gain, sc, sh)


def f_resid_norm(ids, x, y, g, gain, sc, sh):
    xn = x + g * y
    return xn, _normmod(xn, gain, sc, sh)


@jax.custom_vjp
def _swiglu(gate, up):
    return _silu(gate) * up


def _swiglu_fwd(gate, up):
    return _silu(gate) * up, (gate, up)


def _swiglu_bwd(res, da):
    gate, up = res
    s = _sigmoid(gate)
    gs = gate * s
    return da * up * (s + gs * (1.0 - s)), da * gs


_swiglu.defvjp(_swiglu_fwd, _swiglu_bwd)


def f_swiglu(ids, gate, up):
    return (_swiglu(gate, up),)


def loss_and_grad(x, y, tgt, g, S):
    nt = S // WT

    def body(x_ref, y_ref, t_ref, g_ref, part_ref, dx_ref, dy_ref, dg_ref):
        @pl.when(pl.program_id(0) == 0)
        def _():
            dg_ref[...] = jnp.zeros_like(dg_ref)

        yv = y_ref[...].astype(F32)
        gg = g_ref[...]
        e = x_ref[...] + gg * yv - t_ref[...]
        part_ref[...] = 0.5 * jnp.sum(e * e, axis=0, keepdims=True) * (1.0 / D)
        d = e * (1.0 / D)
        dx_ref[...] = d
        dy_ref[...] = (d * gg).astype(dy_ref.dtype)
        dg_ref[...] += jnp.sum(d * yv, axis=0, keepdims=True)

    row = pl.BlockSpec((WT, D), lambda i: (i, 0))
    vec = pl.BlockSpec((1, D), lambda i: (0, 0))
    return pl.pallas_call(
        body, name="loss_and_grad", grid=(nt,), in_specs=[row, row, row, vec],
        out_specs=[pl.BlockSpec((None, 1, D), lambda i: (i, 0, 0)), row, row, vec],
        out_shape=[SDS((nt, 1, D), F32), SDS((S, D), F32), SDS((S, D), BF16), SDS((1, D), F32)],
        compiler_params=_cp(1),
    )(x, y, tgt, g)


def _softplus(x):
    return jnp.maximum(x, 0.0) + jnp.log(1.0 + jnp.exp(-jnp.abs(x)))


def _chunk_tril(T):
    r = lax.broadcasted_iota(jnp.int32, (T, T), 0)
    c = lax.broadcasted_iota(jnp.int32, (T, T), 1)
    return jnp.where((r // GDN_C == c // GDN_C) & (c <= r), 1.0, 0.0).astype(F32)


def _dot_hi(a, b, dims=((1,), (0,))):
    return lax.dot_general(a, b, (dims, ((), ())), precision=lax.Precision.HIGHEST, preferred_element_type=F32)


def _dot_x3(a, b, dims=((1,), (0,))):
    return lax.dot_general(a, b, (dims, ((), ())), precision=lax.Precision.HIGH, preferred_element_type=F32)


def f_gdn_gates(ids, ab, alog, dtb):
    T = ab.shape[0]
    g = -jnp.exp(alog) * _softplus(ab + dtb)
    beta = _sigmoid(ab)
    gcum = _dot_x3(_chunk_tril(T), g)
    row = lax.broadcasted_iota(jnp.int32, (LANES, LANES), 0)
    sel = lambda k: jnp.where(row == k, 1.0, 0.0).astype(F32)
    gcs = [_dot_x3(gcum, sel(h)) for h in range(GDN_H)]
    bts = [_dot_x3(beta, sel(GDN_H + h)) for h in range(GDN_H)]
    return (*gcs, *bts)


def f_gdn_post(ids, *args):
    os_, zs, gain = args[:GDN_H], args[GDN_H:2 * GDN_H], args[2 * GDN_H]
    out = []
    for o, z in zip(os_, zs):
        inv = lax.rsqrt(jnp.mean(o * o, axis=-1, keepdims=True) + EPS)
        out.append(o * inv * gain * _silu(z))
    return tuple(out)


def _qknorm1(x, gain2, scale):
    lane = lax.broadcasted_iota(jnp.int32, x.shape, 1)
    lo = lane < DSW_DH
    x2 = x * x
    s_all = jnp.sum(x2, axis=-1, keepdims=True)
    s_lo = jnp.sum(jnp.where(lo, x2, 0.0), axis=-1, keepdims=True)
    ms = jnp.where(lo, s_lo, s_all - s_lo) * (1.0 / DSW_DH)
    return x * lax.rsqrt(ms + EPS) * (gain2 * scale)


def f_qknorm(ids, *args):
    return tuple(_qknorm1(x, args[-1], 1.0) for x in args[:-1])


def f_qnorm(ids, *args):
    return tuple(_qknorm1(x, args[-1], DSW_DH ** -0.5) for x in args[:-1])


def f_combine(ids, o0, o1, o2, l0, l1, l2):
    m = jnp.maximum(jnp.maximum(l0, l1), l2)
    e0, e1, e2 = jnp.exp(l0 - m), jnp.exp(l1 - m), jnp.exp(l2 - m)
    den = e0 + e1 + e2
    o = (e0 * o0 + e1 * o1 + e2 * o2) / den
    return o, m + jnp.log(den)


GDN_T = 512
HALO = 16


def _conv_pre(xx, w):
    acc = xx * w[3:4, :]
    for j in range(3):
        acc = acc + pltpu.roll(xx, shift=3 - j, axis=0) * w[j:j + 1, :]
    return acc


@jax.custom_vjp
def _qkv_act_core(pre, norm_on, scale):
    s = _silu(pre)
    r = lax.rsqrt(jnp.sum(s * s, axis=-1, keepdims=True) + EPS)
    return jnp.where(norm_on > 0.5, s * r * scale, s)


def _qkv_act_fwd(pre, norm_on, scale):
    return _qkv_act_core(pre, norm_on, scale), (pre, norm_on, scale)


def _qkv_act_bwd(res, dout):
    pre, norm_on, scale = res
    sig = _sigmoid(pre)
    s = pre * sig
    r = lax.rsqrt(jnp.sum(s * s, axis=-1, keepdims=True) + EPS)
    unit = s * r
    dn = dout * scale
    ds = jnp.where(norm_on > 0.5, r * (dn - unit * jnp.sum(dn * unit, axis=-1, keepdims=True)), dout)
    return ds * (sig + s * (1.0 - sig)), jnp.zeros_like(norm_on), jnp.zeros_like(scale)


_qkv_act_core.defvjp(_qkv_act_fwd, _qkv_act_bwd)


def _qkv_act(pre, cidx):
    norm_on = jnp.where(cidx < 2 * GDN_H, 1.0, 0.0).astype(F32)
    scale = jnp.where(cidx < GDN_H, GDN_DK ** -0.5, 1.0).astype(F32)
    return _qkv_act_core(pre, norm_on, scale)


def gdn_pre(proj, conv_w, S):
    nt = S // GDN_T
    hb = GDN_T // HALO

    def body(prev_ref, cur_ref, w_ref, o_ref):
        p, i = pl.program_id(0), pl.program_id(1)
        for h in range(GDN_H):
            cols = slice(LANES * h, LANES * (h + 1))
            prev = jnp.where(i > 0, prev_ref[:, cols].astype(F32), 0.0)
            xx = jnp.concatenate([prev, cur_ref[:, cols].astype(F32)], axis=0)
            pre = _conv_pre(xx, w_ref[:, cols])[HALO:]
            o_ref[h] = _qkv_act(pre, p * GDN_H + h)

    hv = GDN_H * LANES
    return pl.pallas_call(
        body, name="gdn_pre", grid=(3, nt),
        in_specs=[pl.BlockSpec((HALO, hv), lambda p, i: (jnp.maximum(i * hb - 1, 0), p)),
                  pl.BlockSpec((GDN_T, hv), lambda p, i: (i, p)),
                  pl.BlockSpec((4, hv), lambda p, i: (0, p))],
        out_specs=pl.BlockSpec((None, GDN_H, GDN_T, LANES), lambda p, i: (p, 0, i, 0)),
        out_shape=SDS((3, GDN_H, S, LANES), F32),
        compiler_params=_cp(2),
    )(proj, proj, conv_w)


def gdn_pre_bwd(proj, conv_w, dqkv, S):
    nt = S // GDN_T
    hb = GDN_T // HALO
    last_h = S // HALO - 1

    def body(prev_ref, cur_ref, next_ref, w_ref, d_ref, dnext_ref, dx_ref, dw_ref):
        p, i = pl.program_id(0), pl.program_id(1)

        @pl.when(i == 0)
        def _():
            dw_ref[...] = jnp.zeros_like(dw_ref)

        for h in range(GDN_H):
            cols = slice(LANES * h, LANES * (h + 1))
            w = w_ref[:, cols]
            prev = jnp.where(i > 0, prev_ref[:, cols].astype(F32), 0.0)
            xx = jnp.concatenate([prev, cur_ref[:, cols].astype(F32), next_ref[:, cols].astype(F32)], axis=0)
            dnext = jnp.where(i < nt - 1, dnext_ref[h], 0.0)
            dd = jnp.concatenate([jnp.zeros((HALO, LANES), F32), d_ref[h], dnext], axis=0)
            pre = _conv_pre(xx, w)
            _, vjp = jax.vjp(lambda v, _c=p * GDN_H + h: _qkv_act(v, _c), pre)
            (dpre,) = vjp(dd)
            dx = dpre * w[3:4, :]
            R = dpre.shape[0]
            for j in range(3):
                dx = dx + pltpu.roll(dpre, shift=R - (3 - j), axis=0) * w[j:j + 1, :]
            dx_ref[:, cols] = dx[HALO:HALO + GDN_T].astype(dx_ref.dtype)
            own = HALO + GDN_T
            rows_w = [jnp.sum((dpre * pltpu.roll(xx, shift=3 - j, axis=0))[:own], axis=0, keepdims=True) for j in range(3)]
            rows_w.append(jnp.sum((dpre * xx)[:own], axis=0, keepdims=True))
            r4 = lax.broadcasted_iota(jnp.int32, (4, LANES), 0)
            dw = jnp.zeros((4, LANES), F32)
            for j in range(4):
                dw = dw + jnp.where(r4 == j, rows_w[j], 0.0)
            dw_ref[:, cols] += dw

    hv = GDN_H * LANES
    return pl.pallas_call(
        body, name="gdn_pre_bwd", grid=(3, nt),
        in_specs=[pl.BlockSpec((HALO, hv), lambda p, i: (jnp.maximum(i * hb - 1, 0), p)),
                  pl.BlockSpec((GDN_T, hv), lambda p, i: (i, p)),
                  pl.BlockSpec((HALO, hv), lambda p, i: (jnp.minimum((i + 1) * hb, last_h), p)),
                  pl.BlockSpec((4, hv), lambda p, i: (0, p)),
                  pl.BlockSpec((None, GDN_H, GDN_T, LANES), lambda p, i: (p, 0, i, 0)),
                  pl.BlockSpec((None, GDN_H, HALO, LANES), lambda p, i: (p, 0, jnp.minimum((i + 1) * hb, last_h), 0))],
        out_specs=[pl.BlockSpec((GDN_T, hv), lambda p, i: (i, p)),
                   pl.BlockSpec((4, hv), lambda p, i: (0, p))],
        out_shape=[SDS((S, 3 * hv), BF16), SDS((4, 3 * hv), F32)],
        compiler_params=_cp(2),
    )(proj, proj, proj, conv_w, dqkv, dqkv)


_DIMS = {"nn": ((1,), (0,)), "nt": ((1,), (1,)), "tn": ((0,), (0,))}


def _mm_raw(a, b, mode, hi):
    if hi:
        return _dot_hi(a, b, _DIMS[mode])
    return lax.dot_general(a.astype(BF16), b.astype(BF16), (_DIMS[mode], ((), ())), preferred_element_type=F32)


@functools.partial(jax.custom_vjp, nondiff_argnums=(2, 3))
def mm(a, b, mode, hi):
    return _mm_raw(a, b, mode, hi)


def _mm_fwd(a, b, mode, hi):
    return _mm_raw(a, b, mode, hi), (a, b)


def _mm_bwd(mode, hi, res, dc):
    a, b = res
    if mode == "nn":
        da, db = mm(dc, b, "nt", hi), mm(a, dc, "tn", hi)
    elif mode == "nt":
        da, db = mm(dc, b, "nn", hi), mm(dc, a, "tn", hi)
    else:
        da, db = mm(b, dc, "nt", hi), mm(a, dc, "nn", hi)
    return da, db


mm.defvjp(_mm_fwd, _mm_bwd)


TRI_BASE = 8


def _unit_lower_inverses(Ls):
    n = Ls[0].shape[0]
    r = lax.broadcasted_iota(jnp.int32, (n, n), 0)
    c = lax.broadcasted_iota(jnp.int32, (n, n), 1)
    eye = jnp.where(r == c, 1.0, 0.0).astype(F32)
    base = r // TRI_BASE == c // TRI_BASE
    one = lambda a, b_: _mm_raw(a, b_, "nn", False)
    Ps = [jnp.where(base, -L, 0.0) for L in Ls]
    invs = [eye + P for P in Ps]
    k = 1
    while 2 * k < TRI_BASE:
        Ps = [one(P, P) for P in Ps]
        invs = [inv + one(inv, P) for inv, P in zip(invs, Ps)]
        k *= 2
    b = 2 * TRI_BASE
    while b <= n:
        off_mask = (r // b == c // b) & ((r % b) >= b // 2) & ((c % b) < b // 2)
        ts = [one(inv, jnp.where(off_mask, L, 0.0)) for inv, L in zip(invs, Ls)]
        invs = [inv - one(t, inv) for inv, t in zip(invs, ts)]
        b *= 2
    resid = [eye - inv - _dot_x3(L, inv) for inv, L in zip(invs, Ls)]
    return [inv + _dot_x3(inv, rs) for inv, rs in zip(invs, resid)]


@jax.custom_vjp
def tri_apply(invs, Ls, r1s, r2s):
    return [_mm_raw(i, r, "nn", False) for i, r in zip(invs, r1s)], [_mm_raw(i, r, "nn", False) for i, r in zip(invs, r2s)]


def _tri_fwd(invs, Ls, r1s, r2s):
    s1s = [_mm_raw(i, r, "nn", False) for i, r in zip(invs, r1s)]
    s2s = [_mm_raw(i, r, "nn", False) for i, r in zip(invs, r2s)]
    return (s1s, s2s), (invs, s1s, s2s)


def _tri_bwd(res, ds):
    invs, s1s, s2s = res
    d1s = [_mm_raw(i, d, "tn", False) for i, d in zip(invs, ds[0])]
    d2s = [_mm_raw(i, d, "tn", False) for i, d in zip(invs, ds[1])]
    dLs = [-(_mm_raw(d1, s1, "nt", False) + _mm_raw(d2, s2, "nt", False)) for d1, s1, d2, s2 in zip(d1s, s1s, d2s, s2s)]
    return [jnp.zeros_like(i) for i in invs], dLs, d1s, d2s


tri_apply.defvjp(_tri_fwd, _tri_bwd)


def _gdn_chunk(qs, ks, vs, gcbs, btbs, Ss, invs=None):
    C = qs[0].shape[0]
    r = lax.broadcasted_iota(jnp.int32, (C, C), 0)
    c = lax.broadcasted_iota(jnp.int32, (C, C), 1)
    causal, strict = c <= r, c < r
    rows = lax.broadcasted_iota(jnp.int32, gcbs[0].shape, 0)
    Gs = [g[:, :C] for g in gcbs]
    decays = [jnp.exp(jnp.where(causal, G - G.T, NEG)) for G in Gs]
    kbs = [k * b for k, b in zip(ks, btbs)]
    vbs = [v * b for v, b in zip(vs, btbs)]
    Ls = [jnp.where(strict, mm(kb, k, "nt", False) * d, 0.0) for kb, k, d in zip(kbs, ks, decays)]
    egs = [jnp.exp(g) for g in gcbs]
    if invs is None:
        invs = _unit_lower_inverses(Ls)
    us, ws = tri_apply(invs, Ls, vbs, [kb * eg for kb, eg in zip(kbs, egs)])
    qks = [jnp.where(causal, mm(q, k, "nt", False) * d, 0.0) for q, k, d in zip(qs, ks, decays)]
    g_lasts = [jnp.sum(jnp.where(rows == C - 1, g, 0.0), axis=0, keepdims=True) for g in gcbs]
    q_decs = [q * eg for q, eg in zip(qs, egs)]
    k_decs = [k * jnp.exp(gl - g) for k, gl, g in zip(ks, g_lasts, gcbs)]
    v_news = [u - mm(w, S, "nn", False) for u, w, S in zip(us, ws, Ss)]
    os_ = [mm(qd, S, "nn", False) + mm(qk, vn, "nn", False) for qd, S, qk, vn in zip(q_decs, Ss, qks, v_news)]
    S_news = [S * jnp.exp(gl) + mm(kd, vn, "tn", False) for S, gl, kd, vn in zip(Ss, g_lasts, k_decs, v_news)]
    return os_, S_news, invs


INV_CHUNKS = 4


def gdn_inverses(qkv, gc, bt, S):
    nchunk = S // GDN_C
    rows = INV_CHUNKS * GDN_C

    def body(k_ref, g_ref, b_ref, inv_ref):
        items = [(h, m) for m in range(INV_CHUNKS) for h in range(GDN_H)]
        r = lax.broadcasted_iota(jnp.int32, (GDN_C, GDN_C), 0)
        c = lax.broadcasted_iota(jnp.int32, (GDN_C, GDN_C), 1)
        sl = lambda m: slice(m * GDN_C, (m + 1) * GDN_C)
        ks = [k_ref[h, sl(m), :] for h, m in items]
        Gs = [g_ref[h, sl(m), :GDN_C] for h, m in items]
        kbs = [k * b_ref[h, sl(m), :] for k, (h, m) in zip(ks, items)]
        decays = [jnp.exp(jnp.where(c <= r, G - G.T, NEG)) for G in Gs]
        Ls = [jnp.where(c < r, _mm_raw(kb, k, "nt", False) * d, 0.0) for kb, k, d in zip(kbs, ks, decays)]
        for (h, m), inv in zip(items, _unit_lower_inverses(Ls)):
            inv_ref[h, m] = inv

    hb = pl.BlockSpec((GDN_H, rows, LANES), lambda n: (0, n, 0))
    return pl.pallas_call(
        body, name="gdn_inverses", grid=(nchunk // INV_CHUNKS,),
        in_specs=[pl.BlockSpec((None, GDN_H, rows, LANES), lambda n: (1, 0, n, 0)), hb, hb],
        out_specs=pl.BlockSpec((GDN_H, INV_CHUNKS, GDN_C, GDN_C), lambda n: (0, n, 0, 0)),
        out_shape=SDS((GDN_H, nchunk, GDN_C, GDN_C), F32),
        compiler_params=_cp(1),
    )(qkv, gc, bt)


def gdn_core(qkv, gc, bt, invs, S):
    nchunk = S // GDN_C

    def body(qkv_ref, g_ref, b_ref, inv_ref, o_ref, st_ref, s_scr):
        n = pl.program_id(0)

        @pl.when(n == 0)
        def _():
            s_scr[...] = jnp.zeros_like(s_scr)

        heads = range(GDN_H)
        S_in = [s_scr[h] for h in heads]
        os_, S_new, _ = _gdn_chunk([qkv_ref[0, h] for h in heads], [qkv_ref[1, h] for h in heads], [qkv_ref[2, h] for h in heads],
                                   [g_ref[h] for h in heads], [b_ref[h] for h in heads], S_in, invs=[inv_ref[h] for h in heads])
        for h in heads:
            st_ref[h] = S_in[h]
            o_ref[h] = os_[h]
            s_scr[h] = S_new[h]

    blk3 = pl.BlockSpec((3, GDN_H, GDN_C, LANES), lambda n: (0, 0, n, 0))
    hb = pl.BlockSpec((GDN_H, GDN_C, LANES), lambda n: (0, n, 0))
    return pl.pallas_call(
        body, name="gdn_core", grid=(nchunk,),
        in_specs=[blk3, hb, hb, pl.BlockSpec((GDN_H, None, GDN_C, GDN_C), lambda n: (0, n, 0, 0))],
        out_specs=[hb, pl.BlockSpec((GDN_H, None, GDN_DK, LANES), lambda n: (0, n, 0, 0))],
        out_shape=[SDS((GDN_H, S, LANES), F32), SDS((GDN_H, nchunk, GDN_DK, LANES), F32)],
        scratch_shapes=[pltpu.VMEM((GDN_H, GDN_DK, LANES), F32)],
        compiler_params=_cp(1),
    )(qkv, gc, bt, invs)


def gdn_core_bwd(qkv, gc, bt, states, invs, do, S):
    nchunk = S // GDN_C

    def body(qkv_ref, g_ref, b_ref, st_ref, inv_ref, do_ref, dqkv_ref, dg_ref, db_ref, ds_scr):
        n = pl.program_id(0)

        @pl.when(n == 0)
        def _():
            ds_scr[...] = jnp.zeros_like(ds_scr)

        heads = range(GDN_H)
        saved = [inv_ref[h] for h in heads]
        _, vjp = jax.vjp(lambda *a: _gdn_chunk(*a, invs=saved)[:2],
                         [qkv_ref[0, h] for h in heads], [qkv_ref[1, h] for h in heads], [qkv_ref[2, h] for h in heads],
                         [g_ref[h] for h in heads], [b_ref[h] for h in heads], [st_ref[h] for h in heads])
        dq, dk, dv, dg, db, dS = vjp(([do_ref[h] for h in heads], [ds_scr[h] for h in heads]))
        for h in heads:
            dqkv_ref[0, h] = dq[h]
            dqkv_ref[1, h] = dk[h]
            dqkv_ref[2, h] = dv[h]
            dg_ref[h] = dg[h]
            db_ref[h] = db[h]
            ds_scr[h] = dS[h]

    rev = lambda n: nchunk - 1 - n
    blk3 = pl.BlockSpec((3, GDN_H, GDN_C, LANES), lambda n: (0, 0, rev(n), 0))
    hb = pl.BlockSpec((GDN_H, GDN_C, LANES), lambda n: (0, rev(n), 0))
    return pl.pallas_call(
        body, name="gdn_core_bwd", grid=(nchunk,),
        in_specs=[blk3, hb, hb, pl.BlockSpec((GDN_H, None, GDN_DK, LANES), lambda n: (0, rev(n), 0, 0)),
                  pl.BlockSpec((GDN_H, None, GDN_C, GDN_C), lambda n: (0, rev(n), 0, 0)), hb],
        out_specs=[blk3, hb, hb],
        out_shape=[SDS((3, GDN_H, S, LANES), F32), SDS((GDN_H, S, LANES), F32), SDS((GDN_H, S, LANES), F32)],
        scratch_shapes=[pltpu.VMEM((GDN_H, GDN_DK, LANES), F32)],
        compiler_params=_cp(1),
    )(qkv, gc, bt, states, invs, do)


GDN_MAIN = 4 * GDN_H * LANES
GDN_PROJ = GDN_MAIN + LANES
RT = 256


def gdn_forward(h, w_in, conv_w, alog, dtb, out_gain, w_out):
    S = h.shape[0]
    nt = S // RT
    proj = matmul(h, w_in, "nn", BF16, "gdn_in")
    qkv = gdn_pre(proj, conv_w, S)
    ab_row = Row(proj, (RT, LANES), lambda i: (i, GDN_MAIN // LANES), gdtype=BF16, gshape=(S, LANES), gimap=lambda i: (i, 0))
    hm = lambda i: (0, i, 0)
    hv = GDN_H * LANES
    gc, bt = rowwise(f_gdn_gates, [ab_row], [alog, dtb],
                     [Out((GDN_H, S, LANES), F32, (GDN_H, RT, LANES), hm, lead=GDN_H)] * 2, (nt,), "gdn_gates")
    invs = gdn_inverses(qkv, gc, bt, S)
    o, states = gdn_core(qkv, gc, bt, invs, S)
    o_row = Row(o, (GDN_H, RT, LANES), hm, lead=GDN_H)
    z_row = Row(proj, (RT, hv), lambda i: (i, 3), splits=[LANES] * GDN_H, gdtype=BF16, gshape=(S, hv), gimap=lambda i: (i, 0))
    (on,) = rowwise(f_gdn_post, [o_row, z_row], [out_gain],
                    [Out((S, hv), BF16, (RT, hv), lambda i: (i, 0), splits=[LANES] * GDN_H)], (nt,), "gdn_post")
    y = matmul(on, w_out, "nn", BF16, "gdn_out")
    saved = dict(h=h, proj=proj, qkv=qkv, gc=gc, bt=bt, states=states, invs=invs, o=o, on=on, ab_row=ab_row, o_row=o_row, z_row=z_row)
    return y, saved


def gdn_backward(dy, sv, w_in, conv_w, alog, dtb, out_gain, w_out, on_weight_grads=None):
    S = dy.shape[0]
    nt = S // RT
    hm = lambda i: (0, i, 0)
    hv = GDN_H * LANES
    don = matmul(dy, w_out, "nt", BF16, "gdn_out_dx")
    d_w_out = matmul(sv["on"], dy, "tn", F32, "gdn_out_dw")
    (do, dz), (d_gain,) = rowwise_bwd(f_gdn_post, [sv["o_row"], sv["z_row"]], [out_gain],
                                      [Row(don, (RT, hv), lambda i: (i, 0), splits=[LANES] * GDN_H)], (nt,), "gdn_post_bwd")
    dqkv, dgc, dbt = gdn_core_bwd(sv["qkv"], sv["gc"], sv["bt"], sv["states"], sv["invs"], do, S)
    head_blk = lambda a: Row(a, (GDN_H, RT, LANES), hm, lead=GDN_H)
    (dab,), (d_alog, d_dtb) = rowwise_bwd(f_gdn_gates, [sv["ab_row"]], [alog, dtb], [head_blk(dgc), head_blk(dbt)],
                                          (nt,), "gdn_gates_bwd")
    dqkv_proj, d_conv = gdn_pre_bwd(sv["proj"], conv_w, dqkv, S)
    dproj = jnp.concatenate([dqkv_proj, dz, dab], axis=1)
    d_w_in = matmul(sv["h"], dproj, "tn", F32, "gdn_in_dw")
    if on_weight_grads is not None:
        w_in = w_in + on_weight_grads(d_w_in, d_w_out).astype(w_in.dtype)
    dh = matmul(dproj, w_in, "nt", BF16, "gdn_in_dx")
    return dh, dict(w_in=d_w_in, conv=d_conv, alog=d_alog, dtb=d_dtb, gain=d_gain, w_out=d_w_out)


QB = DSW_SPAN
N_HP = DSW_HG // LANES
PROJ_BLKS = 3 * 3 * N_HP


def _bucket_maps():
    a = np.arange(QB)[:, None]
    j = np.arange(2 * QB)[None, :]
    dist = QB + a - j
    band = (dist >= 0) & (dist <= DSW_SPAN)
    maps = []
    for _, dil in DSW_GROUPS:
        dd = np.maximum(dist, 0) * dil
        max_exact = REL_BUCKETS // 2
        scaled = np.log(np.maximum(dd, 1).astype(np.float32) / np.float32(max_exact)) / np.float32(math.log(REL_MAX_DIST / max_exact))
        large = max_exact + (scaled * np.float32(REL_BUCKETS - max_exact)).astype(np.int32)
        large = np.minimum(large, REL_BUCKETS - 1)
        maps.append(np.where(dd < max_exact, dd, large).astype(np.int32))
    return np.stack(maps), band


def dsw_bias(rel_bias):
    maps, band = _bucket_maps()
    maps = np.where(band[None], maps, -1).astype(np.int32)

    def body(tab_ref, bk_ref, o_ref):
        gh = pl.program_id(0)
        bk = bk_ref[...]
        acc = jnp.full(bk.shape, NEG, F32)
        for b in range(REL_BUCKETS):
            acc = jnp.where(bk == b, tab_ref[b, gh], acc)
        o_ref[...] = acc

    return pl.pallas_call(
        body, name="dsw_bias", grid=(3 * GDN_H,),
        in_specs=[pl.BlockSpec(memory_space=pltpu.SMEM),
                  pl.BlockSpec((None, QB, 2 * QB), lambda gh: (gh // GDN_H, 0, 0))],
        out_specs=pl.BlockSpec((None, QB, 2 * QB), lambda gh: (gh, 0, 0)),
        out_shape=SDS((3 * GDN_H, QB, 2 * QB), F32),
        compiler_params=_cp(1),
    )(rel_bias, jnp.asarray(maps))


def dsw_bias_grad(dbias):
    maps, band = _bucket_maps()
    maps = np.where(band[None], maps, -1).astype(np.int32)

    def body(d_ref, bk_ref, o_ref):
        bk = bk_ref[...]
        d = d_ref[...]
        rows = lax.broadcasted_iota(jnp.int32, (REL_BUCKETS, LANES), 0)
        acc = jnp.zeros((REL_BUCKETS, LANES), F32)
        for b in range(REL_BUCKETS):
            part = jnp.sum(jnp.where(bk == b, d, 0.0), axis=0, keepdims=True)
            val = jnp.sum(part, axis=1, keepdims=True)
            acc = jnp.where(rows == b, val, acc)
        o_ref[...] = acc

    return pl.pallas_call(
        body, name="dsw_bias_grad", grid=(3 * GDN_H,),
        in_specs=[pl.BlockSpec((None, QB, 2 * QB), lambda gh: (gh, 0, 0)),
                  pl.BlockSpec((None, QB, 2 * QB), lambda gh: (gh // GDN_H, 0, 0))],
        out_specs=pl.BlockSpec((None, REL_BUCKETS, LANES), lambda gh: (gh, 0, 0)),
        out_shape=SDS((3 * GDN_H, REL_BUCKETS, LANES), F32),
        compiler_params=_cp(1),
    )(dbias, jnp.asarray(maps))


def _nt(a, b):
    return lax.dot_general(a, b, (((1,), (1,)), ((), ())), preferred_element_type=F32)


def _tn(a, b):
    return lax.dot_general(a, b, (((0,), (0,)), ((), ())), preferred_element_type=F32)


def dsw_group_fwd(qn, kn, proj, bias, gi, S):
    dil = DSW_GROUPS[gi][1]
    sd = S // dil
    nq = sd // QB
    qv = qn.reshape(sd, dil * 3 * DSW_HG)
    kv = kn.reshape(sd, dil * 3 * DSW_HG)
    pv = proj.reshape(sd, dil * 9 * DSW_HG)
    qk_col = lambda hp, r: r * (3 * N_HP) + gi * N_HP + hp
    v_col = lambda hp, r: r * PROJ_BLKS + 2 * 3 * N_HP + gi * N_HP + hp

    def body(q_ref, kp_ref, kc_ref, vp_ref, vc_ref, b_ref, o_ref, l_ref):
        i = pl.program_id(2)
        q = q_ref[...]
        k2 = jnp.concatenate([kp_ref[...], kc_ref[...]], axis=0)
        v2 = jnp.concatenate([vp_ref[...], vc_ref[...]], axis=0).astype(BF16)
        lane_q = lax.broadcasted_iota(jnp.int32, (QB, LANES), 1) < DSW_DH
        lane_k = lax.broadcasted_iota(jnp.int32, (2 * QB, LANES), 1) < DSW_DH
        col = lax.broadcasted_iota(jnp.int32, (QB, 2 * QB), 1)
        first = jnp.logical_and(i == 0, col < QB)
        o_acc = jnp.zeros((QB, LANES), F32)
        lse_b = jnp.zeros((QB, LANES), F32)
        for hh in range(2):
            mq = lane_q if hh == 0 else jnp.logical_not(lane_q)
            mk = lane_k if hh == 0 else jnp.logical_not(lane_k)
            s = _nt(jnp.where(mq, q, 0).astype(BF16), k2) + b_ref[hh]
            s = jnp.where(first, NEG, s)
            mx = jnp.max(s, axis=1, keepdims=True)
            p = jnp.exp(s - mx)
            l = jnp.sum(p, axis=1, keepdims=True)
            oh = jnp.dot(p.astype(BF16), jnp.where(mk, v2, 0).astype(BF16), preferred_element_type=F32) / l
            o_acc = o_acc + oh
            lse_b = jnp.where(mq, mx + jnp.log(l), lse_b)
        o_ref[...] = o_acc
        l_ref[...] = lse_b

    blk = (QB, LANES)
    out_spec = pl.BlockSpec(blk, lambda hp, r, i: (i, r * N_HP + hp))
    o, lse = pl.pallas_call(
        body, name=f"dsw_fwd_g{gi}", grid=(N_HP, dil, nq),
        in_specs=[pl.BlockSpec(blk, lambda hp, r, i: (i, qk_col(hp, r))),
                  pl.BlockSpec(blk, lambda hp, r, i: (jnp.maximum(i - 1, 0), qk_col(hp, r))),
                  pl.BlockSpec(blk, lambda hp, r, i: (i, qk_col(hp, r))),
                  pl.BlockSpec(blk, lambda hp, r, i: (jnp.maximum(i - 1, 0), v_col(hp, r))),
                  pl.BlockSpec(blk, lambda hp, r, i: (i, v_col(hp, r))),
                  pl.BlockSpec((2, QB, 2 * QB), lambda hp, r, i: (gi * N_HP + hp, 0, 0))],
        out_specs=[out_spec, out_spec],
        out_shape=[SDS((sd, dil * DSW_HG), F32)] * 2,
        compiler_params=_cp(3),
    )(qv, kv, kv, pv, pv, bias)
    return o.reshape(S, DSW_HG), lse.reshape(S, DSW_HG)


def dsw_group_bwd(qn, kn, proj, bias, do, o, lse, gi, S):
    dil = DSW_GROUPS[gi][1]
    sd = S // dil
    nq = sd // QB
    qv = qn.reshape(sd, dil * 3 * DSW_HG)
    kv = kn.reshape(sd, dil * 3 * DSW_HG)
    pv = proj.reshape(sd, dil * 9 * DSW_HG)
    dov = do.reshape(sd, dil * DSW_HG)
    ov = o.reshape(sd, dil * DSW_HG)
    lv = lse.reshape(sd, dil * DSW_HG)
    qk_col = lambda hp, r: r * (3 * N_HP) + gi * N_HP + hp
    v_col = lambda hp, r: r * PROJ_BLKS + 2 * 3 * N_HP + gi * N_HP + hp
    o_col = lambda hp, r: r * N_HP + hp
    cur = lambda i: jnp.minimum(i, nq - 1)
    prev = lambda i: jnp.maximum(jnp.minimum(i, nq - 1) - 1, 0)
    done = lambda i: jnp.maximum(i - 1, 0)

    def body(q_ref, kp_ref, kc_ref, vp_ref, vc_ref, b_ref, do_ref, o_ref, l_ref,
             dq_ref, dk_ref, dv_ref, db_ref, dk_scr, dv_scr):
        r, i = pl.program_id(1), pl.program_id(2)

        @pl.when(jnp.logical_and(r == 0, i == 0))
        def _():
            db_ref[...] = jnp.zeros_like(db_ref)

        @pl.when(i == 0)
        def _():
            dk_scr[...] = jnp.zeros_like(dk_scr)
            dv_scr[...] = jnp.zeros_like(dv_scr)

        @pl.when(i < nq)
        def _():
            q = q_ref[...]
            k2 = jnp.concatenate([kp_ref[...], kc_ref[...]], axis=0)
            v2 = jnp.concatenate([vp_ref[...], vc_ref[...]], axis=0).astype(BF16)
            dout = do_ref[...].astype(F32)
            prod = dout * o_ref[...].astype(F32)
            lse_b = l_ref[...]
            lane_q = lax.broadcasted_iota(jnp.int32, (QB, LANES), 1) < DSW_DH
            col = lax.broadcasted_iota(jnp.int32, (QB, 2 * QB), 1)
            first = jnp.logical_and(i == 0, col < QB)
            dq = jnp.zeros((QB, LANES), F32)
            dk2 = jnp.zeros((2 * QB, LANES), F32)
            dv2 = jnp.zeros((2 * QB, LANES), F32)
            for hh in range(2):
                mq = lane_q if hh == 0 else jnp.logical_not(lane_q)
                qm = jnp.where(mq, q, 0).astype(BF16)
                dom = jnp.where(mq, dout, 0.0).astype(BF16)
                s = _nt(qm, k2) + b_ref[hh]
                s = jnp.where(first, NEG, s)
                lse_h = jnp.max(jnp.where(mq, lse_b, NEG), axis=1, keepdims=True)
                p = jnp.exp(s - lse_h)
                delta = jnp.sum(jnp.where(mq, prod, 0.0), axis=1, keepdims=True)
                dp = _nt(dom, v2)
                ds = p * (dp - delta)
                dsb = ds.astype(BF16)
                dq = dq + jnp.where(mq, jnp.dot(dsb, k2, preferred_element_type=F32), 0.0)
                dk2 = dk2 + _tn(dsb, qm)
                dv2 = dv2 + _tn(p.astype(BF16), dom)
                db_ref[hh] += ds
            dq_ref[...] = dq
            dk_ref[...] = dk_scr[...] + dk2[:QB]
            dv_ref[...] = (dv_scr[...] + dv2[:QB]).astype(dv_ref.dtype)
            dk_scr[...] = dk2[QB:]
            dv_scr[...] = dv2[QB:]

        @pl.when(i == nq)
        def _():
            dk_ref[...] = dk_scr[...]
            dv_ref[...] = dv_scr[...].astype(dv_ref.dtype)

    blk = (QB, LANES)
    dq, dk, dv, dbias = pl.pallas_call(
        body, name=f"dsw_bwd_g{gi}", grid=(N_HP, dil, nq + 1),
        in_specs=[pl.BlockSpec(blk, lambda hp, r, i: (cur(i), qk_col(hp, r))),
                  pl.BlockSpec(blk, lambda hp, r, i: (prev(i), qk_col(hp, r))),
                  pl.BlockSpec(blk, lambda hp, r, i: (cur(i), qk_col(hp, r))),
                  pl.BlockSpec(blk, lambda hp, r, i: (prev(i), v_col(hp, r))),
                  pl.BlockSpec(blk, lambda hp, r, i: (cur(i), v_col(hp, r))),
                  pl.BlockSpec((2, QB, 2 * QB), lambda hp, r, i: (gi * N_HP + hp, 0, 0)),
                  pl.BlockSpec(blk, lambda hp, r, i: (cur(i), o_col(hp, r))),
                  pl.BlockSpec(blk, lambda hp, r, i: (cur(i), o_col(hp, r))),
                  pl.BlockSpec(blk, lambda hp, r, i: (cur(i), o_col(hp, r)))],
        out_specs=[pl.BlockSpec(blk, lambda hp, r, i: (cur(i), o_col(hp, r))),
                   pl.BlockSpec(blk, lambda hp, r, i: (done(i), o_col(hp, r))),
                   pl.BlockSpec(blk, lambda hp, r, i: (done(i), o_col(hp, r))),
                   pl.BlockSpec((2, QB, 2 * QB), lambda hp, r, i: (hp, 0, 0))],
        out_shape=[SDS((sd, dil * DSW_HG), F32), SDS((sd, dil * DSW_HG), F32), SDS((sd, dil * DSW_HG), BF16),
                   SDS((GDN_H, QB, 2 * QB), F32)],
        scratch_shapes=[pltpu.VMEM(blk, F32), pltpu.VMEM(blk, F32)],
        compiler_params=_cp(3),
    )(qv, kv, kv, pv, pv, bias, dov, ov, lv)
    return dq.reshape(S, DSW_HG), dk.reshape(S, DSW_HG), dv.reshape(S, DSW_HG), dbias


def dsw_forward(h, w_in, q_gain2, k_gain2, rel_bias, w_out):
    S = h.shape[0]
    nt = S // RT
    nb = 3 * N_HP
    proj = matmul(h, w_in, "nn", F32, "dsw_in")
    width = nb * LANES
    lanes12 = [LANES] * nb
    (qn,) = rowwise(f_qnorm, [Row(proj, (RT, width), lambda i: (i, 0), splits=lanes12)], [q_gain2],
                    [Out((S, width), BF16, (RT, width), lambda i: (i, 0), splits=lanes12)], (nt,), "dsw_qnorm")
    (kn,) = rowwise(f_qknorm, [Row(proj, (RT, width), lambda i: (i, 1), splits=lanes12)], [k_gain2],
                    [Out((S, width), BF16, (RT, width), lambda i: (i, 0), splits=lanes12)], (nt,), "dsw_knorm")
    bias = dsw_bias(rel_bias)
    os_, ls_ = [], []
    for gi in range(3):
        o, l = dsw_group_fwd(qn, kn, proj, bias, gi, S)
        os_.append(o)
        ls_.append(l)
    full = lambda a: Row(a, (RT, DSW_HG), lambda i: (i, 0))
    o, lse = rowwise(f_combine, [full(a) for a in os_ + ls_], [],
                     [Out((S, DSW_HG), BF16, (RT, DSW_HG), lambda i: (i, 0)), Out((S, DSW_HG), F32, (RT, DSW_HG), lambda i: (i, 0))],
                     (nt,), "dsw_combine")
    y = matmul(o, w_out, "nn", F32, "dsw_out")
    return y, dict(h=h, proj=proj, qn=qn, kn=kn, bias=bias, o=o, lse=lse)


def dsw_backward(dy, sv, w_in, q_gain2, k_gain2, w_out):
    S = dy.shape[0]
    nt = S // RT
    nb = 3 * N_HP
    do = matmul(dy, w_out, "nt", BF16, "dsw_out_dx")
    d_w_out = matmul(sv["o"], dy, "tn", F32, "dsw_out_dw")
    pieces_q, pieces_k, pieces_v, dbs = [], [], [], []
    d_qg = jnp.zeros((1, LANES), F32)
    d_kg = jnp.zeros((1, LANES), F32)
    for gi in range(3):
        dq, dk, dv, db = dsw_group_bwd(sv["qn"], sv["kn"], sv["proj"], sv["bias"], do, sv["o"], sv["lse"], gi, S)
        dbs.append(db)
        pieces_v.append(dv)
        for which, dd in ((0, dq), (1, dk)):
            lanes4 = [LANES] * N_HP
            row = Row(sv["proj"], (RT, DSW_HG), lambda i, _o=which * 3 + gi: (i, _o), splits=lanes4,
                      gdtype=BF16, gshape=(S, DSW_HG), gimap=lambda i: (i, 0))
            fn, gain = (f_qnorm, q_gain2) if which == 0 else (f_qknorm, k_gain2)
            (dx,), (dg,) = rowwise_bwd(fn, [row], [gain], [Row(dd, (RT, DSW_HG), lambda i: (i, 0), splits=lanes4)],
                                       (nt,), f"dsw_norm_bwd_{which}{gi}")
            if which == 0:
                pieces_q.append(dx)
                d_qg = d_qg + dg
            else:
                pieces_k.append(dx)
                d_kg = d_kg + dg
    dproj = jnp.concatenate(pieces_q + pieces_k + pieces_v, axis=1)
    d_w_in = matmul(sv["h"], dproj, "tn", F32, "dsw_in_dw")
    dh = matmul(dproj, w_in, "nt", F32, "dsw_in_dx")
    d_rel = dsw_bias_grad(jnp.concatenate(dbs, axis=0))
    return dh, dict(w_in=d_w_in, q_gain2=d_qg, k_gain2=d_kg, rel=d_rel, w_out=d_w_out)


N_LB = DSW_HG // LANES
HALF = DSW_DH // 2


def _lanes(j):
    return slice(LANES * j, LANES * (j + 1))


def _deinterleave(stage, out_ref, dil, rows, dtype):
    for r in range(dil):
        for j in range(N_LB):
            out_ref[r, :, _lanes(j)] = stage[j, pl.ds(r, rows, stride=dil), :].astype(dtype)


def _interleave(in_ref, stage, dil, rows):
    for r in range(dil):
        for j in range(N_LB):
            stage[j, pl.ds(r, rows, stride=dil), :] = in_ref[r, :, _lanes(j)].astype(F32)


def dsw_prep(proj, q_gain2, k_gain2, gi, S):
    dil = DSW_GROUPS[gi][1]
    nt, rows = S // RT, RT // dil

    def body(q_ref, k_ref, v_ref, qg_ref, kg_ref, qo_ref, ko_ref, vo_ref, stage):
        for src, gain_ref, scale, dst in ((q_ref, qg_ref, DSW_DH ** -0.5, qo_ref), (k_ref, kg_ref, 1.0, ko_ref), (v_ref, None, None, vo_ref)):
            for j in range(N_LB):
                val = src[:, _lanes(j)].astype(F32)
                val = val if gain_ref is None else _qknorm1(val, gain_ref[...], scale)
                if dil == 1:
                    dst[0, :, _lanes(j)] = val.astype(BF16)
                else:
                    stage[j] = val
            if dil > 1:
                _deinterleave(stage, dst, dil, rows, BF16)

    col = lambda which: pl.BlockSpec((RT, DSW_HG), lambda i, _c=which * 3 + gi: (i, _c))
    gspec = pl.BlockSpec((1, LANES), lambda i: (0, 0))
    ospec = pl.BlockSpec((dil, rows, DSW_HG), lambda i: (0, i, 0))
    return pl.pallas_call(
        body, name=f"dsw_prep_g{gi}", grid=(nt,),
        in_specs=[col(0), col(1), col(2), gspec, gspec], out_specs=[ospec] * 3,
        out_shape=[SDS((dil, S // dil, DSW_HG), BF16)] * 3,
        scratch_shapes=[pltpu.VMEM((N_LB, RT, LANES), F32)], compiler_params=_cp(1),
    )(proj, proj, proj, q_gain2, k_gain2)


def dsw_prep_bwd(proj, q_gain2, k_gain2, dqd, dkd, dvd, gi, S):
    dil = DSW_GROUPS[gi][1]
    nt, rows = S // RT, RT // dil

    def body(q_ref, k_ref, qg_ref, kg_ref, dq_ref, dk_ref, dv_ref, oq_ref, ok_ref, ov_ref, dqg_ref, dkg_ref, stage):
        i = pl.program_id(0)

        @pl.when(i == 0)
        def _():
            dqg_ref[...] = jnp.zeros_like(dqg_ref)
            dkg_ref[...] = jnp.zeros_like(dkg_ref)

        for src, gain_ref, scale, cot_ref, dst, dg_ref in ((q_ref, qg_ref, DSW_DH ** -0.5, dq_ref, oq_ref, dqg_ref),
                                                          (k_ref, kg_ref, 1.0, dk_ref, ok_ref, dkg_ref)):
            if dil > 1:
                _interleave(cot_ref, stage, dil, rows)
            for j in range(N_LB):
                _, vjp = jax.vjp(lambda x, g, _s=scale: _qknorm1(x, g, _s), src[:, _lanes(j)].astype(F32), gain_ref[...])
                dx, dg = vjp(stage[j] if dil > 1 else cot_ref[0, :, _lanes(j)].astype(F32))
                dst[:, _lanes(j)] = dx.astype(dst.dtype)
                dg_ref[...] += dg
        if dil > 1:
            _interleave(dv_ref, stage, dil, rows)
        for j in range(N_LB):
            ov_ref[:, _lanes(j)] = (stage[j] if dil > 1 else dv_ref[0, :, _lanes(j)]).astype(ov_ref.dtype)

    col = lambda which: pl.BlockSpec((RT, DSW_HG), lambda i, _c=which * 3 + gi: (i, _c))
    gspec = pl.BlockSpec((1, LANES), lambda i: (0, 0))
    dspec = pl.BlockSpec((dil, rows, DSW_HG), lambda i: (0, i, 0))
    nspec = pl.BlockSpec((RT, DSW_HG), lambda i: (i, 0))
    return pl.pallas_call(
        body, name=f"dsw_prep_bwd_g{gi}", grid=(nt,),
        in_specs=[col(0), col(1), gspec, gspec, dspec, dspec, dspec], out_specs=[nspec] * 3 + [gspec] * 2,
        out_shape=[SDS((S, DSW_HG), BF16)] * 3 + [SDS((1, LANES), F32)] * 2,
        scratch_shapes=[pltpu.VMEM((N_LB, RT, LANES), F32)], compiler_params=_cp(1),
    )(proj, proj, q_gain2, k_gain2, dqd, dkd, dvd)


def _head_masks(rows):
    lane = lax.broadcasted_iota(jnp.int32, (rows, LANES), 1)
    return lane < DSW_DH, (lane % DSW_DH) < HALF


def dsw_attn_fwd(qd, kd, vd, bias, gi, S):
    dil = DSW_GROUPS[gi][1]
    sd = S // dil
    nq = sd // QB

    def body(q_ref, k_ref, v_ref, b_ref, o_ref, l_ref, kp_scr, vp_scr):
        i = pl.program_id(1)

        @pl.when(i == 0)
        def _():
            kp_scr[...] = jnp.zeros_like(kp_scr)
            vp_scr[...] = jnp.zeros_like(vp_scr)

        lo_q, _ = _head_masks(QB)
        lo_k, _ = _head_masks(2 * QB)
        col = lax.broadcasted_iota(jnp.int32, (QB, 2 * QB), 1)
        first = jnp.logical_and(i == 0, col < QB)
        hps, heads = range(N_HP), range(2 * N_HP)
        k2s = [jnp.concatenate([kp_scr[:, _lanes(hp)], k_ref[:, _lanes(hp)]], axis=0) for hp in hps]
        v2s = [jnp.concatenate([vp_scr[:, _lanes(hp)], v_ref[:, _lanes(hp)]], axis=0) for hp in hps]
        qs = [q_ref[:, _lanes(hp)] for hp in hps]
        k_now, v_now = k_ref[...], v_ref[...]
        mqs = [lo_q if h % 2 == 0 else jnp.logical_not(lo_q) for h in heads]
        mks = [lo_k if h % 2 == 0 else jnp.logical_not(lo_k) for h in heads]
        ss = [jnp.where(first, NEG, _nt(jnp.where(mqs[h], qs[h // 2], 0).astype(BF16), k2s[h // 2]) + b_ref[h]) for h in heads]
        mxs = [jnp.max(s, axis=1, keepdims=True) for s in ss]
        ps = [jnp.exp(s - mx) for s, mx in zip(ss, mxs)]
        ls = [jnp.sum(p, axis=1, keepdims=True) for p in ps]
        ohs = [jnp.dot(ps[h].astype(BF16), jnp.where(mks[h], v2s[h // 2], 0).astype(BF16), preferred_element_type=F32) / ls[h] for h in heads]
        lse_h = [mx + jnp.log(l) for mx, l in zip(mxs, ls)]
        for hp in hps:
            o_ref[:, _lanes(hp)] = ohs[2 * hp] + ohs[2 * hp + 1]
            l_ref[:, _lanes(hp)] = jnp.where(lo_q, lse_h[2 * hp], lse_h[2 * hp + 1])
        kp_scr[...] = k_now
        vp_scr[...] = v_now

    blk = pl.BlockSpec((None, QB, DSW_HG), lambda r, i: (r, i, 0))
    return pl.pallas_call(
        body, name=f"dsw_attn_g{gi}", grid=(dil, nq),
        in_specs=[blk, blk, blk, pl.BlockSpec((GDN_H, QB, 2 * QB), lambda r, i: (gi, 0, 0))],
        out_specs=[blk, blk], out_shape=[SDS((dil, sd, DSW_HG), F32)] * 2,
        scratch_shapes=[pltpu.VMEM((QB, DSW_HG), BF16)] * 2, compiler_params=_cp(2),
    )(qd, kd, vd, bias)


def dsw_attn_bwd(qd, kd, vd, bias, dod, statd, gi, S):
    dil = DSW_GROUPS[gi][1]
    sd = S // dil
    nq = sd // QB
    cur = lambda i: jnp.minimum(i, nq - 1)
    done = lambda i: jnp.maximum(i - 1, 0)

    def body(q_ref, k_ref, v_ref, b_ref, do_ref, st_ref, dq_ref, dk_ref, dv_ref, db_ref, kp_scr, vp_scr, dk_scr, dv_scr):
        r, i = pl.program_id(0), pl.program_id(1)

        @pl.when(jnp.logical_and(r == 0, i == 0))
        def _():
            db_ref[...] = jnp.zeros_like(db_ref)

        @pl.when(i == 0)
        def _():
            for scr in (kp_scr, vp_scr, dk_scr, dv_scr):
                scr[...] = jnp.zeros_like(scr)

        @pl.when(i < nq)
        def _():
            lo_q, first_half = _head_masks(QB)
            col = lax.broadcasted_iota(jnp.int32, (QB, 2 * QB), 1)
            first = jnp.logical_and(i == 0, col < QB)
            hps, heads = range(N_HP), range(2 * N_HP)
            k2s = [jnp.concatenate([kp_scr[:, _lanes(hp)], k_ref[:, _lanes(hp)]], axis=0) for hp in hps]
            v2s = [jnp.concatenate([vp_scr[:, _lanes(hp)], v_ref[:, _lanes(hp)]], axis=0) for hp in hps]
            qs = [q_ref[:, _lanes(hp)] for hp in hps]
            douts = [do_ref[:, _lanes(hp)] for hp in hps]
            stats = [st_ref[:, _lanes(hp)] for hp in hps]
            dkc = [dk_scr[:, _lanes(hp)] for hp in hps]
            dvc = [dv_scr[:, _lanes(hp)] for hp in hps]
            k_now, v_now = k_ref[...], v_ref[...]
            mqs = [lo_q if h % 2 == 0 else jnp.logical_not(lo_q) for h in heads]
            qms = [jnp.where(mqs[h], qs[h // 2], 0).astype(BF16) for h in heads]
            doms = [jnp.where(mqs[h], douts[h // 2], 0).astype(BF16) for h in heads]
            ss = [jnp.where(first, NEG, _nt(qms[h], k2s[h // 2]) + b_ref[h]) for h in heads]
            lses = [jnp.max(jnp.where(jnp.logical_and(mqs[h], first_half), stats[h // 2], NEG), axis=1, keepdims=True) for h in heads]
            deltas = [jnp.max(jnp.where(jnp.logical_and(mqs[h], jnp.logical_not(first_half)), stats[h // 2], NEG), axis=1, keepdims=True)
                      for h in heads]
            ps = [jnp.exp(ss[h] - lses[h]) for h in heads]
            dss = [ps[h] * (_nt(doms[h], v2s[h // 2]) - deltas[h]) for h in heads]
            dsbs = [d.astype(BF16) for d in dss]
            dqh = [jnp.where(mqs[h], jnp.dot(dsbs[h], k2s[h // 2], preferred_element_type=F32), 0.0) for h in heads]
            dkh = [_tn(dsbs[h], qms[h]) for h in heads]
            dvh = [_tn(ps[h].astype(BF16), doms[h]) for h in heads]
            for h in heads:
                db_ref[h] += dss[h]
            for hp in hps:
                dk2 = dkh[2 * hp] + dkh[2 * hp + 1]
                dv2 = dvh[2 * hp] + dvh[2 * hp + 1]
                dq_ref[:, _lanes(hp)] = dqh[2 * hp] + dqh[2 * hp + 1]
                dk_ref[:, _lanes(hp)] = dkc[hp] + dk2[:QB]
                dv_ref[:, _lanes(hp)] = (dvc[hp] + dv2[:QB]).astype(dv_ref.dtype)
                dk_scr[:, _lanes(hp)] = dk2[QB:]
                dv_scr[:, _lanes(hp)] = dv2[QB:]
            kp_scr[...] = k_now
            vp_scr[...] = v_now

        @pl.when(i == nq)
        def _():
            dk_ref[...] = dk_scr[...]
            dv_ref[...] = dv_scr[...].astype(dv_ref.dtype)

    blk = pl.BlockSpec((None, QB, DSW_HG), lambda r, i: (r, cur(i), 0))
    oblk = pl.BlockSpec((None, QB, DSW_HG), lambda r, i: (r, done(i), 0))
    return pl.pallas_call(
        body, name=f"dsw_attn_bwd_g{gi}", grid=(dil, nq + 1),
        in_specs=[blk, blk, blk, pl.BlockSpec((GDN_H, QB, 2 * QB), lambda r, i: (gi, 0, 0)), blk, blk],
        out_specs=[blk, oblk, oblk, pl.BlockSpec((GDN_H, QB, 2 * QB), lambda r, i: (0, 0, 0))],
        out_shape=[SDS((dil, sd, DSW_HG), F32), SDS((dil, sd, DSW_HG), F32), SDS((dil, sd, DSW_HG), BF16),
                   SDS((GDN_H, QB, 2 * QB), F32)],
        scratch_shapes=[pltpu.VMEM((QB, DSW_HG), BF16)] * 2 + [pltpu.VMEM((QB, DSW_HG), F32)] * 2,
        compiler_params=_cp(2),
    )(qd, kd, vd, bias, dod, statd)


def dsw_combine(ods, lseds, S):
    nt = S // RT
    dils = [d for _, d in DSW_GROUPS]

    def body(*refs):
        ins, (o_ref, l_ref), stages = refs[:6], refs[6:8], refs[8:]
        for g in range(3):
            if dils[g] > 1:
                _interleave(ins[g], stages[g], dils[g], RT // dils[g])
                _interleave(ins[3 + g], stages[3 + g], dils[g], RT // dils[g])
        for j in range(N_LB):
            natural = lambda a: stages[a][j] if dils[a % 3] > 1 else ins[a][0, :, _lanes(j)]
            o, lse = f_combine(None, *[natural(a) for a in range(6)])
            o_ref[:, _lanes(j)] = o.astype(o_ref.dtype)
            l_ref[:, _lanes(j)] = lse

    dspec = lambda d: pl.BlockSpec((d, RT // d, DSW_HG), lambda i: (0, i, 0))
    nspec = pl.BlockSpec((RT, DSW_HG), lambda i: (i, 0))
    return pl.pallas_call(
        body, name="dsw_combine", grid=(nt,),
        in_specs=[dspec(d) for d in dils] * 2, out_specs=[nspec, nspec],
        out_shape=[SDS((S, DSW_HG), BF16), SDS((S, DSW_HG), F32)],
        scratch_shapes=[pltpu.VMEM((N_LB, RT, LANES), F32)] * 6, compiler_params=_cp(1),
    )(*ods, *lseds)


def dsw_bwd_prep(do, o, lse, S):
    nt = S // RT
    dils = [d for _, d in DSW_GROUPS]

    def body(do_ref, o_ref, l_ref, *rest):
        outs, (st_do, st_stat) = rest[:6], rest[6:]
        lo, first_half = _head_masks(RT)
        for j in range(N_LB):
            dout = do_ref[:, _lanes(j)]
            prod = dout * o_ref[:, _lanes(j)].astype(F32)
            s_all = jnp.sum(prod, axis=1, keepdims=True)
            s_lo = jnp.sum(jnp.where(lo, prod, 0.0), axis=1, keepdims=True)
            delta = jnp.where(lo, s_lo, s_all - s_lo)
            stat = jnp.where(first_half, l_ref[:, _lanes(j)], delta)
            st_do[j] = dout
            st_stat[j] = stat
            for g in range(3):
                if dils[g] == 1:
                    outs[g][0, :, _lanes(j)] = dout.astype(BF16)
                    outs[3 + g][0, :, _lanes(j)] = stat
        for g in range(3):
            if dils[g] > 1:
                _deinterleave(st_do, outs[g], dils[g], RT // dils[g], BF16)
                _deinterleave(st_stat, outs[3 + g], dils[g], RT // dils[g], F32)

    nspec = pl.BlockSpec((RT, DSW_HG), lambda i: (i, 0))
    dspec = lambda d: pl.BlockSpec((d, RT // d, DSW_HG), lambda i: (0, i, 0))
    res = pl.pallas_call(
        body, name="dsw_bwd_prep", grid=(nt,),
        in_specs=[nspec] * 3, out_specs=[dspec(d) for d in dils] * 2,
        out_shape=[SDS((d, S // d, DSW_HG), BF16) for d in dils] + [SDS((d, S // d, DSW_HG), F32) for d in dils],
        scratch_shapes=[pltpu.VMEM((N_LB, RT, LANES), F32)] * 2, compiler_params=_cp(1),
    )(do, o, lse)
    return res[:3], res[3:]


def dsw_forward(h, w_in, q_gain2, k_gain2, rel_bias, w_out):
    S = h.shape[0]
    proj = matmul(h, w_in, "nn", BF16, "dsw_in", col_shards=N_SHARD)
    bias = dsw_bias(rel_bias)
    qkv, ods, lseds = [], [], []
    for gi in range(3):
        qd, kd, vd = dsw_prep(proj, q_gain2, k_gain2, gi, S)
        od, ld = dsw_attn_fwd(qd, kd, vd, bias, gi, S)
        qkv.append((qd, kd, vd))
        ods.append(od)
        lseds.append(ld)
    o, lse = dsw_combine(ods, lseds, S)
    y = matmul(o, w_out, "nn", BF16, "dsw_out", col_shards=N_SHARD)
    return y, dict(h=h, proj=proj, qkv=qkv, bias=bias, o=o, lse=lse)


def dsw_backward(dy, sv, w_in, q_gain2, k_gain2, w_out):
    S = dy.shape[0]
    do = matmul(dy, w_out, "nt", F32, "dsw_out_dx", col_shards=N_SHARD)
    d_w_out = matmul(sv["o"], dy, "tn", F32, "dsw_out_dw", col_shards=N_SHARD)
    dods, statds = dsw_bwd_prep(do, sv["o"], sv["lse"], S)
    pieces_q, pieces_k, pieces_v, dbs = [], [], [], []
    d_qg = jnp.zeros((1, LANES), F32)
    d_kg = jnp.zeros((1, LANES), F32)
    for gi in range(3):
        qd, kd, vd = sv["qkv"][gi]
        dqd, dkd, dvd, db = dsw_attn_bwd(qd, kd, vd, sv["bias"], dods[gi], statds[gi], gi, S)
        dq, dk, dv, dqg, dkg = dsw_prep_bwd(sv["proj"], q_gain2, k_gain2, dqd, dkd, dvd, gi, S)
        dbs.append(db)
        pieces_q.append(dq)
        pieces_k.append(dk)
        pieces_v.append(dv)
        d_qg = d_qg + dqg
        d_kg = d_kg + dkg
    dproj = jnp.concatenate(pieces_q + pieces_k + pieces_v, axis=1)
    d_w_in = matmul(sv["h"], dproj, "tn", F32, "dsw_in_dw", col_shards=N_SHARD)
    dh = matmul(dproj, w_in, "nt", BF16, "dsw_in_dx", col_shards=N_SHARD)
    d_rel = dsw_bias_grad(jnp.concatenate(dbs, axis=0))
    return dh, dict(w_in=d_w_in, q_gain2=d_qg, k_gain2=d_kg, rel=d_rel, w_out=d_w_out)


FT = 256


FUSE_M = 512


def ffn_in_act(h, w_in, name):
    S = h.shape[0]
    half = FFN // 2

    def body(h_ref, wg_ref, wu_ref, gu_ref, a_ref):
        j = pl.program_id(1)
        sub = FUSE_M // 2
        for part in range(2):
            rows = slice(part * sub, (part + 1) * sub)
            hb = h_ref[rows, :]
            g = jnp.dot(hb, wg_ref[...], preferred_element_type=F32)
            u = jnp.dot(hb, wu_ref[...], preferred_element_type=F32)
            a_ref[rows, :] = (_silu(g) * u).astype(a_ref.dtype)
            for jj in range(2):
                @pl.when(j == jj)
                def _(g=g, u=u, jj=jj, rows=rows):
                    gu_ref[rows, jj * half:(jj + 1) * half] = g.astype(gu_ref.dtype)
                    gu_ref[rows, FFN + jj * half:FFN + (jj + 1) * half] = u.astype(gu_ref.dtype)

    return pl.pallas_call(
        body, name=name, grid=(S // FUSE_M, 2),
        in_specs=[pl.BlockSpec((FUSE_M, D), lambda i, j: (i, 0)),
                  pl.BlockSpec((None, D, half), lambda i, j: (j, 0, 0)),
                  pl.BlockSpec((None, D, half), lambda i, j: (j + 2, 0, 0))],
        out_specs=[pl.BlockSpec((FUSE_M, 2 * FFN), lambda i, j: (i, 0)), pl.BlockSpec((FUSE_M, half), lambda i, j: (i, j))],
        out_shape=[SDS((S, 2 * FFN), BF16), SDS((S, FFN), BF16)],
        compiler_params=_cp(2),
    )(h, w_in, w_in)


def ffn_forward(h, w_in, w_out, tag):
    gu, a = ffn_in_act(h, w_in, f"ffn_in_act_{tag}")
    gu_row = Row(gu, (FT, 2 * FFN), lambda i: (i, 0), splits=[FFN, FFN], gdtype=BF16)
    f = matmul(a, w_out, "nn", BF16, f"ffn_out_{tag}")
    return f, dict(h=h, gu_row=gu_row, a=a)


def ffn_out_dx_act(df, w_out, gu, name):
    S = df.shape[0]
    half = FFN // 2

    def body(df_ref, w_ref, g_ref, u_ref, dgu_ref):
        j = pl.program_id(1)
        sub = FUSE_M // 2
        for part in range(2):
            rows = slice(part * sub, (part + 1) * sub)
            da = _nt(df_ref[rows, :], w_ref[...])
            dg, du = _swiglu_bwd((g_ref[rows, :].astype(F32), u_ref[rows, :].astype(F32)), da)
            for jj in range(2):
                @pl.when(j == jj)
                def _(dg=dg, du=du, jj=jj, rows=rows):
                    dgu_ref[rows, jj * half:(jj + 1) * half] = dg.astype(dgu_ref.dtype)
                    dgu_ref[rows, FFN + jj * half:FFN + (jj + 1) * half] = du.astype(dgu_ref.dtype)

    return pl.pallas_call(
        body, name=name, grid=(S // FUSE_M, 2),
        in_specs=[pl.BlockSpec((FUSE_M, D), lambda i, j: (i, 0)),
                  pl.BlockSpec((half, D), lambda i, j: (j, 0)),
                  pl.BlockSpec((FUSE_M, half), lambda i, j: (i, j)),
                  pl.BlockSpec((FUSE_M, half), lambda i, j: (i, j + 2))],
        out_specs=pl.BlockSpec((FUSE_M, 2 * FFN), lambda i, j: (i, 0)),
        out_shape=SDS((S, 2 * FFN), BF16),
        compiler_params=_cp(2),
    )(df, w_out, gu, gu)


def ffn_backward(df, sv, w_in, w_out, tag):
    d_w_out = matmul(sv["a"], df, "tn", F32, f"ffn_out_dw_{tag}")
    dgu = ffn_out_dx_act(df, w_out, sv["gu_row"].arr, f"ffn_out_dx_act_{tag}")
    d_w_in = matmul(sv["h"], dgu, "tn", F32, f"ffn_in_dw_{tag}", col_shards=N_SHARD)
    dh = matmul(dgu, w_in, "nt", BF16, f"ffn_in_dx_{tag}", col_shards=N_SHARD)
    return dh, d_w_in, d_w_out


def f_norm_only(ids, x, gain, sc, sh):
    return (_normmod(x, gain, sc, sh),)


WT = 512


def _wide(a, **kw):
    return Row(a, (WT, D), lambda i: (i, 0), **kw)


def _wide_out(S, dtype):
    return Out((S, D), dtype, (WT, D), lambda i: (i, 0))


def adamw(w, g, m, v, name):
    shape = w.shape
    C = shape[-1]
    R = int(np.prod(shape[:-1]))
    w2, g2, m2, v2 = (a.reshape(R, C) for a in (w, g, m, v))
    br = R
    if R > 256:
        br = max(b for b in range(8, 257, 8) if R % b == 0)
    c1 = 1.0 / (1.0 - ADAM_B1 ** ADAM_STEP)
    c2 = 1.0 / (1.0 - ADAM_B2 ** ADAM_STEP)

    def body(w_ref, g_ref, m_ref, v_ref, d_ref, nm_ref, nv_ref):
        gg = g_ref[...]
        mm_ = ADAM_B1 * m_ref[...] + (1.0 - ADAM_B1) * gg
        vv = ADAM_B2 * v_ref[...] + (1.0 - ADAM_B2) * (gg * gg)
        d_ref[...] = -ADAM_LR * ((mm_ * c1) / (jnp.sqrt(vv * c2) + ADAM_EPS) + ADAM_WD * w_ref[...])
        nm_ref[...] = mm_
        nv_ref[...] = vv

    spec = pl.BlockSpec((br, C), lambda i: (i, 0))
    d, nm, nv = pl.pallas_call(
        body, name=name, grid=(R // br,), in_specs=[spec] * 4, out_specs=[spec] * 3,
        out_shape=[SDS((R, C), F32)] * 3, compiler_params=_cp(1),
    )(w2, g2, m2, v2)
    return d.reshape(shape), nm.reshape(shape), nv.reshape(shape)


def _place():
    x, y, c = lax.axis_index("x"), lax.axis_index("y"), lax.axis_index("c")
    chips = [(1 - x, y), (x, 1 - y), (1 - x, 1 - y)]
    return x, y, c, chips


def all_gather_small(blk, name):
    m_per, n = blk.shape

    def body(x_ref, out_ref, send_sems, recv_sems, local_sem):
        x, y, c, chips = _place()
        me, sibling = (x, y, c), (x, y, 1 - c)

        def rows(px, py, pc):
            return out_ref.at[pl.ds((4 * px + 2 * py + pc) * m_per, m_per), :]

        def copy(k, block, to, src=None):
            return pltpu.make_async_remote_copy(
                src_ref=rows(*block) if src is None else src, dst_ref=rows(*block),
                send_sem=send_sems.at[k], recv_sem=recv_sems.at[k], device_id=to, device_id_type=MESH)

        mine = pltpu.make_async_copy(x_ref, rows(*me), local_sem)
        mine.start()
        first = [copy(0, me, sibling, src=x_ref)]
        first += [copy(1 + j, me, (*chip, c), src=x_ref) for j, chip in enumerate(chips)]
        for cp in first:
            cp.start()
        passed = [copy(4 + j, (*chip, c), sibling) for j, chip in enumerate(chips)]
        for j, chip in enumerate(chips):
            copy(1 + j, (*chip, c), me).wait_recv()
            passed[j].start()
        copy(0, sibling, me).wait_recv()
        for j, chip in enumerate(chips):
            copy(4 + j, (*chip, 1 - c), me).wait_recv()
        for cp in first + passed:
            cp.wait_send()
        mine.wait()

    return pl.pallas_call(
        body, name=name, out_shape=SDS((N_DEV * m_per, n), blk.dtype),
        in_specs=[pl.BlockSpec(memory_space=pltpu.VMEM)], out_specs=pl.BlockSpec(memory_space=pltpu.VMEM),
        scratch_shapes=[pltpu.SemaphoreType.DMA((7,)), pltpu.SemaphoreType.DMA((7,)), pltpu.SemaphoreType.DMA],
    )(blk)


def _half(cc, rh):
    return pl.ds(pl.multiple_of(cc * rh, 16), rh)


def all_gather_shards(ws):
    n = len(ws)

    def body(*refs):
        w_refs, out_refs = refs[:n], refs[n:2 * n]
        send_sems, recv_sems, local_sems, own_sems = refs[2 * n:]
        x, y, c, chips = _place()
        sibling = (x, y, 1 - c)
        s_me = 2 * x + y

        def copy(k, src, dst, to):
            return pltpu.make_async_remote_copy(src_ref=src, dst_ref=dst, send_sem=send_sems.at[k], recv_sem=recv_sems.at[k],
                                                device_id=to, device_id_type=MESH)

        local, sends, passed = [], [], []
        for k in range(n):
            rh = ws[k].shape[0] // 2
            cp = pltpu.make_async_remote_copy(src_ref=w_refs[k], dst_ref=out_refs[k].at[s_me], send_sem=local_sems.at[k],
                                              recv_sem=own_sems.at[k], device_id=sibling, device_id_type=MESH)
            cp.start()
            local.append(cp)
            for j, chip in enumerate(chips):
                sd = copy(6 * k + j, w_refs[k].at[_half(c, rh)], out_refs[k].at[s_me, _half(c, rh)], (*chip, c))
                sd.start()
                sends.append(sd)
        for k in range(n):
            rh = ws[k].shape[0] // 2
            for j, (px, py) in enumerate(chips):
                got = out_refs[k].at[2 * px + py, _half(c, rh)]
                copy(6 * k + j, got, got, (px, py, c)).wait_recv()
                fw = copy(6 * k + 3 + j, got, got, sibling)
                fw.start()
                passed.append(fw)
        for k in range(n):
            rh = ws[k].shape[0] // 2
            for j, (px, py) in enumerate(chips):
                got = out_refs[k].at[2 * px + py, _half(1 - c, rh)]
                copy(6 * k + 3 + j, got, got, sibling).wait_recv()
        for cp in sends + passed:
            cp.wait_send()
        for cp in local:
            cp.wait()

    return pl.pallas_call(
        body, name="weights_all_gather", out_shape=[SDS((N_SHARD,) + w.shape, w.dtype) for w in ws],
        in_specs=[ANY] * n, out_specs=[ANY] * n,
        scratch_shapes=[pltpu.SemaphoreType.DMA((6 * n,)), pltpu.SemaphoreType.DMA((6 * n,)), pltpu.SemaphoreType.DMA((n,)),
                        pltpu.SemaphoreType.DMA((n,))],
    )(*ws)


def sibling_exchange(sends, name):
    n = len(sends)

    def body(*refs):
        s_refs, o_refs, send_sems, recv_sems = refs[:n], refs[n:2 * n], refs[2 * n], refs[2 * n + 1]
        x, y, c, _ = _place()
        cps = [pltpu.make_async_remote_copy(src_ref=s_refs[k], dst_ref=o_refs[k], send_sem=send_sems.at[k], recv_sem=recv_sems.at[k],
                                            device_id=(x, y, 1 - c), device_id_type=MESH) for k in range(n)]
        for cp in cps:
            cp.start()
        for cp in cps:
            cp.wait()

    return pl.pallas_call(
        body, name=name, out_shape=[SDS(s.shape, s.dtype) for s in sends], in_specs=[ANY] * n, out_specs=[ANY] * n,
        scratch_shapes=[pltpu.SemaphoreType.DMA((n,)), pltpu.SemaphoreType.DMA((n,))],
    )(*sends)


def scatter_to_chips(parts):
    n = len(parts)

    def body(*refs):
        p_refs, o_refs, send_sems, recv_sems = refs[:n], refs[n:2 * n], refs[2 * n], refs[2 * n + 1]
        x, y, c, chips = _place()
        cps = []
        for k in range(n):
            for j, (px, py) in enumerate(chips):
                cp = pltpu.make_async_remote_copy(src_ref=p_refs[k].at[2 * px + py], dst_ref=o_refs[k].at[j],
                                                  send_sem=send_sems.at[3 * k + j], recv_sem=recv_sems.at[3 * k + j],
                                                  device_id=(px, py, c), device_id_type=MESH)
                cp.start()
                cps.append(cp)
        for cp in cps:
            cp.wait()

    return pl.pallas_call(
        body, name="grads_scatter", out_shape=[SDS((3,) + p.shape[1:], p.dtype) for p in parts], in_specs=[ANY] * n, out_specs=[ANY] * n,
        scratch_shapes=[pltpu.SemaphoreType.DMA((3 * n,)), pltpu.SemaphoreType.DMA((3 * n,))],
    )(*parts)


def merge_halves(halves):
    n = len(halves)

    def body(*refs):
        h_refs, o_refs = refs[:n], refs[n:2 * n]
        send_sems, recv_sems, local_sems = refs[2 * n:]
        x, y, c, _ = _place()
        local, cps = [], []
        for k in range(n):
            rh = halves[k].shape[0]
            lc = pltpu.make_async_copy(h_refs[k], o_refs[k].at[_half(c, rh)], local_sems.at[k])
            lc.start()
            local.append(lc)
            cp = pltpu.make_async_remote_copy(src_ref=h_refs[k], dst_ref=o_refs[k].at[_half(c, rh)], send_sem=send_sems.at[k],
                                              recv_sem=recv_sems.at[k], device_id=(x, y, 1 - c), device_id_type=MESH)
            cp.start()
            cps.append(cp)
        for k in range(n):
            rh = halves[k].shape[0]
            got = o_refs[k].at[_half(1 - c, rh)]
            pltpu.make_async_remote_copy(src_ref=got, dst_ref=got, send_sem=send_sems.at[k], recv_sem=recv_sems.at[k],
                                         device_id=(x, y, 1 - c), device_id_type=MESH).wait_recv()
        for cp in cps:
            cp.wait_send()
        for lc in local:
            lc.wait()

    return pl.pallas_call(
        body, name="grads_merge_halves", out_shape=[SDS((2 * h.shape[0], h.shape[1]), h.dtype) for h in halves],
        in_specs=[ANY] * n, out_specs=[ANY] * n,
        scratch_shapes=[pltpu.SemaphoreType.DMA((n,)), pltpu.SemaphoreType.DMA((n,)), pltpu.SemaphoreType.DMA((n,))],
    )(*halves)


def add_rows(arrs, out_dtype, name, rt=256):
    Rr, W = arrs[0].shape

    def fn(ids, *vals):
        acc = vals[0]
        for v in vals[1:]:
            acc = acc + v
        return (acc,)

    t = rt if Rr % rt == 0 else max(b for b in range(16, rt + 1, 16) if Rr % b == 0)
    (out,) = rowwise(fn, [Row(a, (t, W), lambda i: (i, 0)) for a in arrs], [],
                     [Out((Rr, W), out_dtype, (t, W), lambda i: (i, 0))], (Rr // t,), name)
    return out


HBM_SPEC = pl.BlockSpec(memory_space=pltpu.HBM)
SEM_SPEC = pl.BlockSpec(memory_space=pltpu.SEMAPHORE)
DATAFLOW = pltpu.SideEffectType.DATAFLOW_SIDE_EFFECTING


def _in_hbm(a):
    return pltpu.with_memory_space_constraint(a, pltpu.HBM)


def _gather_copies(w_refs, land_refs, send_sems, recv_sems):
    x, y, c, chips = _place()
    targets = [(x, y, 1 - c)] + [(*chip, c) for chip in chips]
    cps = []
    for k, (w_ref, land_ref) in enumerate(zip(w_refs, land_refs)):
        for j, to in enumerate(targets):
            cps.append(pltpu.make_async_remote_copy(src_ref=w_ref, dst_ref=land_ref.at[2 * x + y], send_sem=send_sems.at[4 * k + j],
                                                    recv_sem=recv_sems.at[4 * k + j], device_id=to, device_id_type=MESH))
    return cps


def _scatter_copies(p_refs, land_refs, send_sems, recv_sems):
    x, y, c, chips = _place()
    cps = []
    for k, (p_ref, land_ref) in enumerate(zip(p_refs, land_refs)):
        for j, (px, py) in enumerate(chips):
            cps.append(pltpu.make_async_remote_copy(src_ref=p_ref.at[2 * px + py], dst_ref=land_ref.at[j], send_sem=send_sems.at[3 * k + j],
                                                    recv_sem=recv_sems.at[3 * k + j], device_id=(px, py, c), device_id_type=MESH))
    return cps


def copies_start(srcs, land_shapes, make_copies, per_src, name):
    n = len(srcs)
    m = per_src * n

    def body(*refs):
        src_refs, land_refs = refs[:n], refs[n:2 * n]
        send_sems, recv_sems, token = refs[2 * n], refs[2 * n + 1], refs[-1]
        for cp in make_copies(src_refs, land_refs, send_sems, recv_sems):
            cp.start()
        token[...] = jnp.zeros_like(token)

    lands = [lax.empty(shp, s.dtype) for shp, s in zip(land_shapes, srcs)]
    res = pl.pallas_call(
        body, name=name,
        out_shape=(pltpu.SemaphoreType.DMA((m,)), pltpu.SemaphoreType.DMA((m,)), *[pltpu.HBM(s.shape, s.dtype) for s in srcs],
                   *[pltpu.HBM(shp, s.dtype) for shp, s in zip(land_shapes, srcs)], SDS((8, LANES), F32)),
        in_specs=[HBM_SPEC] * (2 * n),
        out_specs=(SEM_SPEC, SEM_SPEC, *[HBM_SPEC] * (2 * n), pl.BlockSpec(memory_space=pltpu.VMEM)),
        input_output_aliases={i: 2 + i for i in range(2 * n)},
        compiler_params=pltpu.CompilerParams(has_side_effects=DATAFLOW),
    )(*[_in_hbm(s) for s in srcs], *[_in_hbm(l) for l in lands])
    return res[0], res[1], list(res[2:2 + n]), list(res[2 + n:2 + 2 * n]), res[-1]


def copies_wait(send_sems, recv_sems, srcs, lands, make_copies, after, name):
    n = len(srcs)

    def body(*refs):
        src_refs, land_refs = refs[:n], refs[n:2 * n]
        for cp in make_copies(src_refs, land_refs, refs[2 * n], refs[2 * n + 1]):
            cp.wait_send()
            cp.wait_recv()

    res = pl.pallas_call(
        body, name=name,
        out_shape=(*[pltpu.HBM(s.shape, s.dtype) for s in srcs], *[pltpu.HBM(l.shape, l.dtype) for l in lands]),
        in_specs=[HBM_SPEC] * (2 * n) + [SEM_SPEC, SEM_SPEC, ANY],
        out_specs=tuple([HBM_SPEC] * (2 * n)),
        input_output_aliases={i: i for i in range(2 * n)},
        compiler_params=pltpu.CompilerParams(has_side_effects=DATAFLOW),
    )(*srcs, *lands, send_sems, recv_sems, after)
    return list(res[n:])


PACK = (("gdn_w_in", 2), ("gdn_w_out", 1), ("w_ffn_in", 2), ("w_ffn_out", 1), ("dsw_w_in", 2), ("dsw_w_out", 2))
PACK_ALIGN = 32


def _pack_rows(sizes):
    total = sum(sizes)
    rows = -(-total // D)
    return -(-rows // PACK_ALIGN) * PACK_ALIGN


def pack_blocks(blocks, dtype):
    flat = [b.astype(dtype).reshape(-1) for b in blocks]
    total = sum(f.shape[0] for f in flat)
    R = _pack_rows([f.shape[0] for f in flat])
    flat.append(jnp.zeros((R * D - total,), dtype))
    return jnp.concatenate(flat).reshape(R, D)


def unpack_blocks(buf, shapes):
    flat = buf.reshape(-1)
    out, off = [], 0
    for shp in shapes:
        n = int(np.prod(shp))
        out.append(flat[off:off + n].reshape(shp))
        off += n
    return out


def _shard_slice(a, axis, s):
    n = a.shape[axis] // N_SHARD
    return lax.slice_in_dim(a, s * n, (s + 1) * n, axis=axis)


def _pad_lanes(v):
    return jnp.concatenate([v.astype(F32), jnp.zeros((LANES - v.shape[0],), F32)])[None]


def kernel(x, c, w_ada, b_ada, norm_mix, norm_ffn, w_ffn_in, w_ffn_out, gdn_w_in, gdn_conv, gdn_a_log, gdn_dt_bias, gdn_out_norm, gdn_w_out, dsw_w_in, dsw_q_norm, dsw_k_norm, dsw_w_out, rel_bias, loss_target, m_w_ada, m_b_ada, m_norm_mix, m_norm_ffn, m_w_ffn_in, m_w_ffn_out, m_gdn_w_in, m_gdn_conv, m_gdn_a_log, m_gdn_dt_bias, m_gdn_out_norm, m_gdn_w_out, m_dsw_w_in, m_dsw_q_norm, m_dsw_k_norm, m_dsw_w_out, m_rel_bias, v_w_ada, v_b_ada, v_norm_mix, v_norm_ffn, v_w_ffn_in, v_w_ffn_out, v_gdn_w_in, v_gdn_conv, v_gdn_a_log, v_gdn_dt_bias, v_gdn_out_norm, v_gdn_w_out, v_dsw_w_in, v_dsw_q_norm, v_dsw_k_norm, v_dsw_w_out, v_rel_bias):
    S = x.shape[1]
    nt = S // WT
    xi, yi, ci = lax.axis_index("x"), lax.axis_index("y"), lax.axis_index("c")
    me = 4 * xi + 2 * yi + ci
    s_me = 2 * xi + yi
    x0, tgt = x[0], loss_target[0]
    shard = dict(w_ffn_in=w_ffn_in, w_ffn_out=w_ffn_out, gdn_w_in=gdn_w_in, gdn_w_out=gdn_w_out, dsw_w_in=dsw_w_in, dsw_w_out=dsw_w_out)

    whole = lambda a: Row(a, a.shape, lambda i: (0,) * a.ndim)
    (cond8,) = rowwise(lambda ids, v: (_silu(v),), [whole(c.reshape(8, LANES))], [], [Out((8, LANES), F32, (8, LANES), lambda i: (0, 0))], (1,), "cond")
    cond_all = all_gather_small(cond8, "gather_cond").reshape(N_DEV, D)
    cond16 = jnp.concatenate([cond_all, jnp.zeros((8, D), F32)], axis=0)
    ada_cols = w_ada.shape[2]
    mods = [matmul(cond16, w_ada[l], "nn", F32, f"ada_{l}")[:N_DEV] for l in range(2)]
    buf = jnp.concatenate([jnp.stack(mods, axis=1).reshape(-1, LANES), gdn_conv.reshape(-1, LANES)], axis=0)
    n_mod_rows = N_DEV * 2 * ada_cols // LANES
    got = all_gather_small(buf, "gather_mod").reshape(N_DEV, buf.shape[0], LANES)
    mod_parts, conv_parts = [], []
    for s in range(N_SHARD):
        from_dev = got[2 * s]
        mod_parts.append(lax.dynamic_index_in_dim(from_dev[:n_mod_rows].reshape(N_DEV, 2, ada_cols), me, 0, keepdims=False))
        conv_parts.append(from_dev[n_mod_rows:].reshape(4, -1))
    mod_nb = jnp.concatenate(mod_parts, axis=1)
    conv_w = jnp.concatenate(conv_parts, axis=1)
    (mod,) = rowwise(lambda ids, a, b: (a + b,), [whole(mod_nb), whole(b_ada)], [], [Out(mod_nb.shape, F32, mod_nb.shape, lambda i: (0, 0))], (1,), "mod_bias")
    mod = mod.reshape(2, 6, 1, D)
    sh1, sc1, g1, sh2, sc2, g2 = ([mod[l, k] for l in range(2)] for k in range(6))
    gmix = [norm_mix[l][None] for l in range(2)]
    gffn = [norm_ffn[l][None] for l in range(2)]

    gcols = gdn_w_in.shape[2]
    g_gdn_in, g_gdn_out = all_gather_shards([gdn_w_in[0].astype(BF16), gdn_w_out[0].astype(BF16)])
    gathered = lambda ws: [(N_SHARD,) + w.shape for w in ws]
    gate = (jnp.minimum(jnp.abs(g_gdn_in[0, 0, 0].astype(F32)), 0.0) + jnp.minimum(jnp.abs(mod[0, 0, 0, 0]), 0.0)).astype(BF16)
    w2 = [w_ffn_in[0].astype(BF16) + gate, w_ffn_out[0].astype(BF16) + gate]
    w3 = [dsw_w_in[0].astype(BF16) + gate, dsw_w_out[0].astype(BF16) + gate, w_ffn_in[1].astype(BF16) + gate, w_ffn_out[1].astype(BF16) + gate]
    fly2 = copies_start(w2, gathered(w2), _gather_copies, 4, "weights_ffn0_start")
    fly3 = copies_start(w3, gathered(w3), _gather_copies, 4, "weights_layer1_start")
    started = fly2[4][0, 0] + fly3[4][0, 0]
    w_gdn = jnp.concatenate([g_gdn_in[s] for s in range(N_SHARD)] + [jnp.zeros((D, GDN_PROJ - N_SHARD * gcols), BF16)], axis=1)
    alog, dtb = _pad_lanes(gdn_a_log[0]), _pad_lanes(gdn_dt_bias[0])
    qg2 = jnp.concatenate([dsw_q_norm, dsw_q_norm], axis=1)
    kg2 = jnp.concatenate([dsw_k_norm, dsw_k_norm], axis=1)
    w_gdn_out = g_gdn_out.reshape(GDN_H * LANES, D)
    gdn_args = (w_gdn, conv_w, alog, dtb, gdn_out_norm, w_gdn_out)
    sc1[0] = sc1[0] + started

    (h10,) = rowwise(f_norm_only, [_wide(x0)], [gmix[0], sc1[0], sh1[0]], [_wide_out(S, BF16)], (nt,), "l0_norm")
    y0, sv_g = gdn_forward(h10, *gdn_args)
    x1, h20 = rowwise(f_resid_norm, [_wide(x0), _wide(y0)], [g1[0], gffn[0], sc2[0], sh2[0]], [_wide_out(S, F32), _wide_out(S, BF16)], (nt,), "l0_mid")
    g_in0, g_out0 = copies_wait(*fly2[:4], _gather_copies, y0, "weights_ffn0_wait")
    w_ffn = [(g_in0, g_out0.reshape(FFN, D)), None]
    f0, sv_f0 = ffn_forward(h20, *w_ffn[0], "0")
    x2, h11 = rowwise(f_resid_norm, [_wide(x1), _wide(f0)], [g2[0], gmix[1], sc1[1], sh1[1]], [_wide_out(S, F32), _wide_out(S, BF16)], (nt,), "l1_in")
    g_dsw_in, g_dsw_out, g_in1, g_out1 = copies_wait(*fly3[:4], _gather_copies, f0, "weights_layer1_wait")
    w_ffn[1] = (g_in1, g_out1.reshape(FFN, D))
    dsw_args = (g_dsw_in, qg2, kg2)
    y1, sv_d = dsw_forward(h11, *dsw_args, rel_bias, g_dsw_out)
    x3, h21 = rowwise(f_resid_norm, [_wide(x2), _wide(y1)], [g1[1], gffn[1], sc2[1], sh2[1]], [_wide_out(S, F32), _wide_out(S, BF16)], (nt,), "l1_mid")
    f1, sv_f1 = ffn_forward(h21, *w_ffn[1], "1")
    parts, dx3, df1, dg2_1 = loss_and_grad(x3, f1, tgt, g2[1], S)
    loss = lax.psum(jnp.sum(parts), ("x", "y", "c"))

    dh21, d_win1, d_wout1 = ffn_backward(df1, sv_f1, *w_ffn[1], "1")
    (dx2, dy1), (dg1_1, dgf1, dsc2_1, dsh2_1) = rowwise_bwd(
        f_resid_norm, [_wide(x2), _wide(y1, gdtype=BF16)], [g1[1], gffn[1], sc2[1], sh2[1]], [_wide(dx3), _wide(dh21)], (nt,), "l1_mid_bwd")
    dh11, g_d = dsw_backward(dy1, sv_d, *dsw_args, g_dsw_out)
    (dx1, df0), (dg2_0, dgm1, dsc1_1, dsh1_1) = rowwise_bwd(
        f_resid_norm, [_wide(x1), _wide(f0, gdtype=BF16)], [g2[0], gmix[1], sc1[1], sh1[1]], [_wide(dx2), _wide(dh11)], (nt,), "l1_in_bwd")
    by_shard = lambda a: a.reshape(N_SHARD, a.shape[0] // N_SHARD, a.shape[1])
    landing = lambda ps: [(3,) + p.shape[1:] for p in ps]
    dws3 = [g_d["w_in"], g_d["w_out"], d_win1, by_shard(d_wout1)]
    parts3 = [a.astype(BF16) for a in dws3]
    gfly3 = copies_start(parts3, landing(parts3), _scatter_copies, 3, "grads_layer1_start")
    w_out0 = w_ffn[0][1] + gfly3[4][0, 0].astype(BF16)
    dh20, d_win0, d_wout0 = ffn_backward(df0, sv_f0, w_ffn[0][0], w_out0, "0")
    (dx0p, dy0), (dg1_0, dgf0, dsc2_0, dsh2_0) = rowwise_bwd(
        f_resid_norm, [_wide(x0), _wide(y0, gdtype=BF16)], [g1[0], gffn[0], sc2[0], sh2[0]], [_wide(dx1), _wide(dh20)], (nt,), "l0_mid_bwd")
    dws2 = [d_win0, by_shard(d_wout0)]
    parts2 = [a.astype(BF16) for a in dws2]
    gfly2 = copies_start(parts2, landing(parts2), _scatter_copies, 3, "grads_ffn0_start")
    gdn_args = gdn_args[:5] + (w_gdn_out + gfly2[4][0, 0].astype(BF16),)
    gdn_flight = []

    def start_gdn_grads(d_w_in, d_w_out):
        dws1 = [jnp.stack([d_w_in[:, s * gcols:(s + 1) * gcols] for s in range(N_SHARD)]), by_shard(d_w_out)]
        parts1 = [a.astype(BF16) for a in dws1]
        fly = copies_start(parts1, landing(parts1), _scatter_copies, 3, "grads_gdn_start")
        gdn_flight.extend([dws1, fly])
        return fly[4][0, 0]

    dh10, g_g = gdn_backward(dy0, sv_g, *gdn_args, on_weight_grads=start_gdn_grads)
    dws1, gfly1 = gdn_flight
    (grad_x,), (dgm0, dsc1_0, dsh1_0) = rowwise_bwd(f_first, [_wide(x0)], [gmix[0], sc1[0], sh1[0]], [_wide(dx0p), _wide(dh10)], (nt,), "l0_norm_bwd")

    dmod = jnp.concatenate([dsh1_0, dsc1_0, dg1_0, dsh2_0, dsc2_0, dg2_0, dsh1_1, dsc1_1, dg1_1, dsh2_1, dsc2_1, dg2_1], axis=1)
    d_rel = jnp.transpose(g_d["rel"][:, :, 0])
    fold = lambda v: v[:, :DSW_DH] + v[:, DSW_DH:]
    small = [dmod, jnp.concatenate([dgm0, dgm1], axis=1), jnp.concatenate([dgf0, dgf1], axis=1), g_g["conv"].reshape(1, -1),
             g_g["alog"], g_g["dtb"], g_g["gain"], _pad_lanes(fold(g_d["q_gain2"])[0]), _pad_lanes(fold(g_d["k_gain2"])[0]),
             d_rel.reshape(1, -1)]
    used = [v.shape[1] // LANES for v in small]
    sizes = [-(-u // 8) * 8 for u in used]
    pad8 = lambda v, u, s: jnp.concatenate([v.reshape(u, LANES), jnp.zeros((s - u, LANES), F32)], axis=0) if s > u else v.reshape(u, LANES)
    pad_rows = sum(sizes)
    sbuf = jnp.concatenate([pad8(v, u, s) for v, u, s in zip(small, used, sizes)], axis=0)
    sgot = all_gather_small(sbuf, "gather_small_grads")
    ssum = add_rows([sgot[d * pad_rows:(d + 1) * pad_rows] for d in range(N_DEV)], F32, "sum_small_grads", rt=pad_rows)
    offs = np.cumsum([0] + sizes)
    take = lambda k: ssum[offs[k]:offs[k] + used[k]].reshape(1, -1)
    grad_b_ada = take(0).reshape(2, 6 * D)
    grad_norm_mix = take(1).reshape(2, D)
    grad_norm_ffn = take(2).reshape(2, D)
    conv_full = take(3).reshape(4, -1)
    ncv = gdn_conv.shape[2]
    grad_gdn_conv = lax.dynamic_slice_in_dim(conv_full, s_me * ncv, ncv, axis=1)[None]
    grad_a_log = take(4)[:, :GDN_H]
    grad_dt_bias = take(5)[:, :GDN_H]
    grad_out_norm = take(6)
    grad_q_norm = take(7)[:, :DSW_DH]
    grad_k_norm = take(8)[:, :DSW_DH]
    grad_rel = take(9).reshape(REL_BUCKETS, 3 * GDN_H)
    dmod_all = sgot.reshape(N_DEV, pad_rows, LANES)[:, :used[0]].reshape(N_DEV, 2, 6 * D)
    dmod_mine = lax.dynamic_slice_in_dim(dmod_all, s_me * ada_cols, ada_cols, axis=2)
    dmod16 = jnp.concatenate([dmod_mine, jnp.zeros_like(dmod_mine)], axis=0)
    grad_w_ada = jnp.stack([matmul(cond16, dmod16[:, l], "tn", F32, f"ada_dw_{l}") for l in range(2)])

    got3 = copies_wait(*gfly3[:4], _scatter_copies, grad_x, "grads_layer1_wait")
    got2 = copies_wait(*gfly2[:4], _scatter_copies, grad_x, "grads_ffn0_wait")
    got1 = copies_wait(*gfly1[:4], _scatter_copies, grad_x, "grads_gdn_wait")
    core_sums = []
    for i, (full, got) in enumerate(zip(dws3 + dws2 + dws1, got3 + got2 + got1)):
        own = lax.dynamic_index_in_dim(full, s_me, 0, keepdims=False)
        core_sums.append(add_rows([own, got[0], got[1], got[2]], F32, f"grads_core_sum_{i}"))
    sib_sums = sibling_exchange(core_sums, "grads_core_sums_swap")
    s_dsw_in, s_dsw_out, s_in1, s_out1, s_in0, s_out0, s_gdn_in, s_gdn_out = [
        add_rows([a, b], F32, f"grads_chip_total_{i}") for i, (a, b) in enumerate(zip(core_sums, sib_sums))]
    gsh = dict(gdn_w_in=s_gdn_in[None], gdn_w_out=s_gdn_out[None],
               w_ffn_in=jnp.stack([s_in0, s_in1]), w_ffn_out=jnp.stack([s_out0, s_out1]),
               dsw_w_in=s_dsw_in[None], dsw_w_out=s_dsw_out[None])

    grads = dict(w_ada=grad_w_ada, b_ada=grad_b_ada, norm_mix=grad_norm_mix, norm_ffn=grad_norm_ffn, w_ffn_in=gsh["w_ffn_in"],
                 w_ffn_out=gsh["w_ffn_out"], gdn_w_in=gsh["gdn_w_in"], gdn_conv=grad_gdn_conv, gdn_a_log=grad_a_log,
                 gdn_dt_bias=grad_dt_bias, gdn_out_norm=grad_out_norm, gdn_w_out=gsh["gdn_w_out"], dsw_w_in=gsh["dsw_w_in"],
                 dsw_q_norm=grad_q_norm, dsw_k_norm=grad_k_norm, dsw_w_out=gsh["dsw_w_out"], rel_bias=grad_rel)
    weights = dict(w_ada=w_ada, b_ada=b_ada, norm_mix=norm_mix, norm_ffn=norm_ffn, w_ffn_in=w_ffn_in, w_ffn_out=w_ffn_out,
                   gdn_w_in=gdn_w_in, gdn_conv=gdn_conv, gdn_a_log=gdn_a_log, gdn_dt_bias=gdn_dt_bias, gdn_out_norm=gdn_out_norm,
                   gdn_w_out=gdn_w_out, dsw_w_in=dsw_w_in, dsw_q_norm=dsw_q_norm, dsw_k_norm=dsw_k_norm, dsw_w_out=dsw_w_out,
                   rel_bias=rel_bias)
    ms = dict(w_ada=m_w_ada, b_ada=m_b_ada, norm_mix=m_norm_mix, norm_ffn=m_norm_ffn, w_ffn_in=m_w_ffn_in, w_ffn_out=m_w_ffn_out,
              gdn_w_in=m_gdn_w_in, gdn_conv=m_gdn_conv, gdn_a_log=m_gdn_a_log, gdn_dt_bias=m_gdn_dt_bias, gdn_out_norm=m_gdn_out_norm,
              gdn_w_out=m_gdn_w_out, dsw_w_in=m_dsw_w_in, dsw_q_norm=m_dsw_q_norm, dsw_k_norm=m_dsw_k_norm, dsw_w_out=m_dsw_w_out,
              rel_bias=m_rel_bias)
    vs = dict(w_ada=v_w_ada, b_ada=v_b_ada, norm_mix=v_norm_mix, norm_ffn=v_norm_ffn, w_ffn_in=v_w_ffn_in, w_ffn_out=v_w_ffn_out,
              gdn_w_in=v_gdn_w_in, gdn_conv=v_gdn_conv, gdn_a_log=v_gdn_a_log, gdn_dt_bias=v_gdn_dt_bias, gdn_out_norm=v_gdn_out_norm,
              gdn_w_out=v_gdn_w_out, dsw_w_in=v_dsw_w_in, dsw_q_norm=v_dsw_q_norm, dsw_k_norm=v_dsw_k_norm, dsw_w_out=v_dsw_w_out,
              rel_bias=v_rel_bias)
    names = list(weights)
    deltas, new_m, new_v = [], [], []
    for n in names:
        g = grads[n].reshape(weights[n].shape)
        grads[n] = g
        d, nm, nv = adamw(weights[n], g, ms[n], vs[n], f"adamw_{n}")
        deltas.append(d)
        new_m.append(nm)
        new_v.append(nv)
    return (loss, grad_x[None], *[grads[n] for n in names], *deltas, *new_m, *new_v)
```

```python
import functools
import math

import numpy as np
import jax
import jax.numpy as jnp
from jax import lax
from jax.experimental import pallas as pl
from jax.experimental.pallas import tpu as pltpu

F32 = jnp.float32
BF16 = jnp.bfloat16
SDS = jax.ShapeDtypeStruct
MESH = pl.DeviceIdType.MESH
ANY = pl.BlockSpec(memory_space=pl.ANY)

D = 1024
EPS = 1e-6
LANES = 128
GDN_H = 8
GDN_DK = 128
GDN_C = 64
DSW_GROUPS = ((128, 1), (512, 4), (2048, 16))
DSW_SPAN = 128
DSW_DH = 64
DSW_HG = 512
REL_BUCKETS = 32
REL_MAX_DIST = 2048
FFN = 2816
N_SHARD = 4
N_DEV = 8
VMEM_LIMIT = 48 * 1024 * 1024
NEG = -1e30

ADAM_LR, ADAM_B1, ADAM_B2, ADAM_EPS, ADAM_WD, ADAM_STEP = 0.001, 0.9, 0.999, 1e-08, 0.01, 10


def _cp(n_axes):
    return pltpu.CompilerParams(dimension_semantics=("arbitrary",) * n_axes, vmem_limit_bytes=VMEM_LIMIT)


def _blk(dim, cap):
    if dim <= cap:
        return dim
    best = None
    for b in range(LANES, cap + 1, LANES):
        if dim % b == 0:
            best = b
    assert best is not None, (dim, cap)
    return best


MAX_SHARD_BLOCK = 1408
def matmul(a, b, mode, out_dtype, name, cap_m=MAX_SHARD_BLOCK, cap_n=MAX_SHARD_BLOCK, cap_k=2048, col_shards=0):
    ns = col_shards
    if mode == "nn":
        (M, K) = a.shape
        K2, N = (b.shape[1], ns * b.shape[2]) if ns else b.shape
    elif mode == "nt":
        (M, K) = a.shape
        N, K2 = (b.shape[1], ns * b.shape[2]) if ns else b.shape
    else:
        (K, M), (K2, N) = a.shape, b.shape
    assert K == K2, (a.shape, b.shape, mode)
    if K <= 3072:
        cap_k = K
        if K > 2048:
            cap_n = 1024
    n_unit = N // ns if (ns and mode != "nt") else N
    k_unit = K // ns if (ns and mode == "nt") else K
    bm = _blk(M, cap_m)
    bn = _blk(n_unit, MAX_SHARD_BLOCK) if n_unit != N else _blk(N, cap_n)
    if k_unit != K:
        bk = _blk(k_unit, MAX_SHARD_BLOCK)
    else:
        bk = _blk(K, 1024 if (ns and mode == "tn") else cap_k)
    nk = K // bk
    nps, kps = n_unit // bn, k_unit // bk
    dims = {"nn": ((1,), (0,)), "nt": ((1,), (1,)), "tn": ((0,), (0,))}[mode]

    def dot(a_ref, b_ref):
        return lax.dot_general(a_ref[...].astype(BF16), b_ref[...].astype(BF16), (dims, ((), ())), preferred_element_type=F32)

    def body_one(a_ref, b_ref, o_ref):
        o_ref[...] = dot(a_ref, b_ref).astype(o_ref.dtype)

    def body_acc(a_ref, b_ref, o_ref, acc_ref):
        k = pl.program_id(2)

        @pl.when(k == 0)
        def _():
            acc_ref[...] = jnp.zeros_like(acc_ref)

        acc_ref[...] += dot(a_ref, b_ref)

        @pl.when(k == nk - 1)
        def _():
            o_ref[...] = acc_ref[...].astype(o_ref.dtype)

    a_spec = pl.BlockSpec((bk, bm), lambda i, j, k: (k, i)) if mode == "tn" else pl.BlockSpec((bm, bk), lambda i, j, k: (i, k))
    if mode == "nt":
        b_spec = pl.BlockSpec((None, bn, bk), lambda i, j, k: (k // kps, j, k % kps)) if ns else pl.BlockSpec((bn, bk), lambda i, j, k: (j, k))
    elif mode == "nn" and ns:
        b_spec = pl.BlockSpec((None, bk, bn), lambda i, j, k: (j // nps, k, j % nps))
    else:
        b_spec = pl.BlockSpec((bk, bn), lambda i, j, k: (k, j))
    if mode == "tn" and ns:
        o_spec, o_shape = pl.BlockSpec((None, bm, bn), lambda i, j, k: (j // nps, i, j % nps)), (ns, M, n_unit)
    else:
        o_spec, o_shape = pl.BlockSpec((bm, bn), lambda i, j, k: (i, j)), (M, N)
    return pl.pallas_call(
        body_one if nk == 1 else body_acc, name=name, grid=(M // bm, N // bn, nk),
        in_specs=[a_spec, b_spec], out_specs=o_spec,
        out_shape=SDS(o_shape, out_dtype), scratch_shapes=[] if nk == 1 else [pltpu.VMEM((bm, bn), F32)],
        compiler_params=_cp(3),
    )(a, b)


class Row:
    def __init__(self, arr, bshape, imap, splits=None, diff=True, acc=False, gdtype=F32, gshape=None, gbshape=None, gimap=None,
                 lead=0):
        self.arr, self.bshape, self.imap = arr, tuple(bshape), imap
        self.splits, self.lead = splits, lead
        self.diff, self.acc, self.gdtype = diff, acc, gdtype
        self.gshape = tuple(arr.shape) if gshape is None else tuple(gshape)
        self.gbshape = self.bshape if gbshape is None else tuple(gbshape)
        self.gimap = imap if gimap is None else gimap

    def gspec(self):
        return pl.BlockSpec(self.gbshape, self.gimap)

    def spec(self):
        return pl.BlockSpec(self.bshape, self.imap)

    def pieces(self, ref):
        return _load_pieces(ref, self.splits, self.lead)

    def n_pieces(self):
        return _n_pieces(self.splits, self.lead)


class Out:
    def __init__(self, shape, dtype, bshape, imap, splits=None, lead=0):
        self.shape, self.dtype, self.bshape, self.imap = tuple(shape), dtype, tuple(bshape), imap
        self.splits, self.lead = splits, lead

    def n_pieces(self):
        return _n_pieces(self.splits, self.lead)


def _n_pieces(splits, lead):
    return lead if lead else (1 if splits is None else len(splits))


def _load_pieces(ref, splits, lead):
    if lead:
        return [ref[k].astype(F32) for k in range(lead)]
    if splits is None:
        return [ref[...].astype(F32)]
    out, o = [], 0
    for w in splits:
        out.append(ref[..., o:o + w].astype(F32))
        o += w
    return out


def _store_pieces(ref, splits, lead, vals, accumulate=False):
    def put(idx, v):
        if accumulate:
            ref[idx] += v.astype(ref.dtype)
        else:
            ref[idx] = v.astype(ref.dtype)

    if lead:
        for k in range(lead):
            put(k, vals[k])
    elif splits is None:
        put(..., vals[0])
    else:
        o = 0
        for w, v in zip(splits, vals):
            put((..., slice(o, o + w)), v)
            o += w


def rowwise(fn, rows, params, outs, grid, name):
    nr, npar = len(rows), len(params)

    def body(*refs):
        ids = tuple(pl.program_id(a) for a in range(len(grid)))
        vals = []
        for r, ref in zip(rows, refs[:nr]):
            vals += r.pieces(ref)
        pvals = [ref[...].astype(F32) for ref in refs[nr:nr + npar]]
        res = list(fn(ids, *vals, *pvals))
        o = 0
        for spec, ref in zip(outs, refs[nr + npar:]):
            n = spec.n_pieces()
            _store_pieces(ref, spec.splits, spec.lead, res[o:o + n])
            o += n

    nz = len(grid)
    pspecs = [pl.BlockSpec(p.shape, (lambda *ids, _n=p.ndim: (0,) * _n)) for p in params]
    res = pl.pallas_call(
        body, name=name, grid=grid,
        in_specs=[r.spec() for r in rows] + pspecs,
        out_specs=[pl.BlockSpec(o.bshape, o.imap) for o in outs],
        out_shape=[SDS(o.shape, o.dtype) for o in outs],
        compiler_params=_cp(nz),
    )(*[r.arr for r in rows], *params)
    return list(res)


def rowwise_bwd(fn, rows, params, cots, grid, name):
    nr, npar, nc = len(rows), len(params), len(cots)
    drows = [r for r in rows if r.diff]
    nz = len(grid)

    def body(*refs):
        ids = tuple(pl.program_id(a) for a in range(nz))
        row_refs, par_refs = refs[:nr], refs[nr:nr + npar]
        cot_refs = refs[nr + npar:nr + npar + nc]
        drow_refs = refs[nr + npar + nc:nr + npar + nc + len(drows)]
        dpar_refs = refs[nr + npar + nc + len(drows):]
        pieces, is_diff = [], []
        for r, ref in zip(rows, row_refs):
            ps = r.pieces(ref)
            pieces += ps
            is_diff += [r.diff] * len(ps)
        pvals = [ref[...].astype(F32) for ref in par_refs]
        dvals = [p for p, dflag in zip(pieces, is_diff) if dflag]
        nd = len(dvals)

        def f(*args):
            it = iter(args[:nd])
            full = [next(it) if dflag else p for p, dflag in zip(pieces, is_diff)]
            return tuple(fn(ids, *full, *args[nd:]))

        _, vjp = jax.vjp(f, *dvals, *pvals)
        cvals = []
        for c, ref in zip(cots, cot_refs):
            cvals += c.pieces(ref)
        g = vjp(tuple(cvals))
        o = 0
        first_inner = ids[-1] == 0
        for r, ref in zip(drows, drow_refs):
            n = r.n_pieces()
            gs = g[o:o + n]
            o += n
            if r.acc:
                @pl.when(first_inner)
                def _(ref=ref):
                    ref[...] = jnp.zeros_like(ref)
            _store_pieces(ref, r.splits, r.lead, gs, accumulate=r.acc)
        first = functools.reduce(jnp.logical_and, [i == 0 for i in ids])
        for ref, gp in zip(dpar_refs, g[nd:]):
            @pl.when(first)
            def _(ref=ref):
                ref[...] = jnp.zeros_like(ref)
            ref[...] += gp

    pspecs = [pl.BlockSpec(p.shape, (lambda *ids, _n=p.ndim: (0,) * _n)) for p in params]
    res = pl.pallas_call(
        body, name=name, grid=grid,
        in_specs=[r.spec() for r in rows] + pspecs + [c.spec() for c in cots],
        out_specs=[r.gspec() for r in drows] + pspecs,
        out_shape=[SDS(r.gshape, r.gdtype) for r in drows] + [SDS(p.shape, F32) for p in params],
        compiler_params=_cp(nz),
    )(*[r.arr for r in rows], *params, *[c.arr for c in cots])
    res = list(res)
    return res[:len(drows)], res[len(drows):]


def _sigmoid(x):
    return 0.5 * (jnp.tanh(0.5 * x) + 1.0)


def _silu(x):
    return x * _sigmoid(x)


def _normmod(x, gain, sc, sh):
    inv = lax.rsqrt(jnp.mean(x * x, axis=-1, keepdims=True) + EPS)
    return x * inv * gain * (1.0 + sc) + sh


def f_first(ids, x, gain, sc, sh):
    return x, _normmod(x, gain, sc, sh)


def f_resid_norm(ids, x, y, g, gain, sc, sh):
    xn = x + g * y
    return xn, _normmod(xn, gain, sc, sh)


@jax.custom_vjp
def _swiglu(gate, up):
    return _silu(gate) * up


def _swiglu_fwd(gate, up):
    return _silu(gate) * up, (gate, up)


def _swiglu_bwd(res, da):
    gate, up = res
    s = _sigmoid(gate)
    gs = gate * s
    return da * up * (s + gs * (1.0 - s)), da * gs


_swiglu.defvjp(_swiglu_fwd, _swiglu_bwd)


def f_swiglu(ids, gate, up):
    return (_swiglu(gate, up),)


def loss_and_grad(x, y, tgt, g, S):
    nt = S // WT

    def body(x_ref, y_ref, t_ref, g_ref, part_ref, dx_ref, dy_ref, dg_ref):
        @pl.when(pl.program_id(0) == 0)
        def _():
            dg_ref[...] = jnp.zeros_like(dg_ref)

        yv = y_ref[...].astype(F32)
        gg = g_ref[...]
        e = x_ref[...] + gg * yv - t_ref[...]
        part_ref[...] = 0.5 * jnp.sum(e * e, axis=0, keepdims=True) * (1.0 / D)
        d = e * (1.0 / D)
        dx_ref[...] = d
        dy_ref[...] = (d * gg).astype(dy_ref.dtype)
        dg_ref[...] += jnp.sum(d * yv, axis=0, keepdims=True)

    row = pl.BlockSpec((WT, D), lambda i: (i, 0))
    vec = pl.BlockSpec((1, D), lambda i: (0, 0))
    return pl.pallas_call(
        body, name="loss_and_grad", grid=(nt,), in_specs=[row, row, row, vec],
        out_specs=[pl.BlockSpec((None, 1, D), lambda i: (i, 0, 0)), row, row, vec],
        out_shape=[SDS((nt, 1, D), F32), SDS((S, D), F32), SDS((S, D), BF16), SDS((1, D), F32)],
        compiler_params=_cp(1),
    )(x, y, tgt, g)


def _softplus(x):
    return jnp.maximum(x, 0.0) + jnp.log(1.0 + jnp.exp(-jnp.abs(x)))


def _chunk_tril(T):
    r = lax.broadcasted_iota(jnp.int32, (T, T), 0)
    c = lax.broadcasted_iota(jnp.int32, (T, T), 1)
    return jnp.where((r // GDN_C == c // GDN_C) & (c <= r), 1.0, 0.0).astype(F32)


def _dot_hi(a, b, dims=((1,), (0,))):
    return lax.dot_general(a, b, (dims, ((), ())), precision=lax.Precision.HIGHEST, preferred_element_type=F32)


def _dot_x3(a, b, dims=((1,), (0,))):
    return lax.dot_general(a, b, (dims, ((), ())), precision=lax.Precision.HIGH, preferred_element_type=F32)


def f_gdn_gates(ids, ab, alog, dtb):
    T = ab.shape[0]
    g = -jnp.exp(alog) * _softplus(ab + dtb)
    beta = _sigmoid(ab)
    gcum = _dot_x3(_chunk_tril(T), g)
    row = lax.broadcasted_iota(jnp.int32, (LANES, LANES), 0)
    sel = lambda k: jnp.where(row == k, 1.0, 0.0).astype(F32)
    gcs = [_dot_x3(gcum, sel(h)) for h in range(GDN_H)]
    bts = [_dot_x3(beta, sel(GDN_H + h)) for h in range(GDN_H)]
    return (*gcs, *bts)


def f_gdn_post(ids, *args):
    os_, zs, gain = args[:GDN_H], args[GDN_H:2 * GDN_H], args[2 * GDN_H]
    out = []
    for o, z in zip(os_, zs):
        inv = lax.rsqrt(jnp.mean(o * o, axis=-1, keepdims=True) + EPS)
        out.append(o * inv * gain * _silu(z))
    return tuple(out)


def _qknorm1(x, gain2, scale):
    lane = lax.broadcasted_iota(jnp.int32, x.shape, 1)
    lo = lane < DSW_DH
    x2 = x * x
    s_all = jnp.sum(x2, axis=-1, keepdims=True)
    s_lo = jnp.sum(jnp.where(lo, x2, 0.0), axis=-1, keepdims=True)
    ms = jnp.where(lo, s_lo, s_all - s_lo) * (1.0 / DSW_DH)
    return x * lax.rsqrt(ms + EPS) * (gain2 * scale)


def f_qknorm(ids, *args):
    return tuple(_qknorm1(x, args[-1], 1.0) for x in args[:-1])


def f_qnorm(ids, *args):
    return tuple(_qknorm1(x, args[-1], DSW_DH ** -0.5) for x in args[:-1])


def f_combine(ids, o0, o1, o2, l0, l1, l2):
    m = jnp.maximum(jnp.maximum(l0, l1), l2)
    e0, e1, e2 = jnp.exp(l0 - m), jnp.exp(l1 - m), jnp.exp(l2 - m)
    den = e0 + e1 + e2
    o = (e0 * o0 + e1 * o1 + e2 * o2) / den
    return o, m + jnp.log(den)


GDN_T = 512
HALO = 16


def _conv_pre(xx, w):
    acc = xx * w[3:4, :]
    for j in range(3):
        acc = acc + pltpu.roll(xx, shift=3 - j, axis=0) * w[j:j + 1, :]
    return acc


@jax.custom_vjp
def _qkv_act_core(pre, norm_on, scale):
    s = _silu(pre)
    r = lax.rsqrt(jnp.sum(s * s, axis=-1, keepdims=True) + EPS)
    return jnp.where(norm_on > 0.5, s * r * scale, s)


def _qkv_act_fwd(pre, norm_on, scale):
    return _qkv_act_core(pre, norm_on, scale), (pre, norm_on, scale)


def _qkv_act_bwd(res, dout):
    pre, norm_on, scale = res
    sig = _sigmoid(pre)
    s = pre * sig
    r = lax.rsqrt(jnp.sum(s * s, axis=-1, keepdims=True) + EPS)
    unit = s * r
    dn = dout * scale
    ds = jnp.where(norm_on > 0.5, r * (dn - unit * jnp.sum(dn * unit, axis=-1, keepdims=True)), dout)
    return ds * (sig + s * (1.0 - sig)), jnp.zeros_like(norm_on), jnp.zeros_like(scale)


_qkv_act_core.defvjp(_qkv_act_fwd, _qkv_act_bwd)


def _qkv_act(pre, cidx):
    norm_on = jnp.where(cidx < 2 * GDN_H, 1.0, 0.0).astype(F32)
    scale = jnp.where(cidx < GDN_H, GDN_DK ** -0.5, 1.0).astype(F32)
    return _qkv_act_core(pre, norm_on, scale)


def gdn_pre(proj, conv_w, S):
    nt = S // GDN_T
    hb = GDN_T // HALO

    def body(prev_ref, cur_ref, w_ref, o_ref):
        p, i = pl.program_id(0), pl.program_id(1)
        for h in range(GDN_H):
            cols = slice(LANES * h, LANES * (h + 1))
            prev = jnp.where(i > 0, prev_ref[:, cols].astype(F32), 0.0)
            xx = jnp.concatenate([prev, cur_ref[:, cols].astype(F32)], axis=0)
            pre = _conv_pre(xx, w_ref[:, cols])[HALO:]
            o_ref[h] = _qkv_act(pre, p * GDN_H + h).astype(o_ref.dtype)

    hv = GDN_H * LANES
    return pl.pallas_call(
        body, name="gdn_pre", grid=(3, nt),
        in_specs=[pl.BlockSpec((HALO, hv), lambda p, i: (jnp.maximum(i * hb - 1, 0), p)),
                  pl.BlockSpec((GDN_T, hv), lambda p, i: (i, p)),
                  pl.BlockSpec((4, hv), lambda p, i: (0, p))],
        out_specs=pl.BlockSpec((None, GDN_H, GDN_T, LANES), lambda p, i: (p, 0, i, 0)),
        out_shape=SDS((3, GDN_H, S, LANES), BF16),
        compiler_params=_cp(2),
    )(proj, proj, conv_w)


def gdn_pre_bwd(proj, conv_w, dqkv, S):
    nt = S // GDN_T
    hb = GDN_T // HALO
    last_h = S // HALO - 1

    def body(prev_ref, cur_ref, next_ref, w_ref, d_ref, dnext_ref, dx_ref, dw_ref):
        p, i = pl.program_id(0), pl.program_id(1)

        @pl.when(i == 0)
        def _():
            dw_ref[...] = jnp.zeros_like(dw_ref)

        for h in range(GDN_H):
            cols = slice(LANES * h, LANES * (h + 1))
            w = w_ref[:, cols]
            prev = jnp.where(i > 0, prev_ref[:, cols].astype(F32), 0.0)
            xx = jnp.concatenate([prev, cur_ref[:, cols].astype(F32), next_ref[:, cols].astype(F32)], axis=0)
            dnext = jnp.where(i < nt - 1, dnext_ref[h], 0.0)
            dd = jnp.concatenate([jnp.zeros((HALO, LANES), F32), d_ref[h], dnext], axis=0)
            pre = _conv_pre(xx, w)
            _, vjp = jax.vjp(lambda v, _c=p * GDN_H + h: _qkv_act(v, _c), pre)
            (dpre,) = vjp(dd)
            dx = dpre * w[3:4, :]
            R = dpre.shape[0]
            for j in range(3):
                dx = dx + pltpu.roll(dpre, shift=R - (3 - j), axis=0) * w[j:j + 1, :]
            dx_ref[:, cols] = dx[HALO:HALO + GDN_T].astype(dx_ref.dtype)
            own = HALO + GDN_T
            rows_w = [jnp.sum((dpre * pltpu.roll(xx, shift=3 - j, axis=0))[:own], axis=0, keepdims=True) for j in range(3)]
            rows_w.append(jnp.sum((dpre * xx)[:own], axis=0, keepdims=True))
            r4 = lax.broadcasted_iota(jnp.int32, (4, LANES), 0)
            dw = jnp.zeros((4, LANES), F32)
            for j in range(4):
                dw = dw + jnp.where(r4 == j, rows_w[j], 0.0)
            dw_ref[:, cols] += dw

    hv = GDN_H * LANES
    return pl.pallas_call(
        body, name="gdn_pre_bwd", grid=(3, nt),
        in_specs=[pl.BlockSpec((HALO, hv), lambda p, i: (jnp.maximum(i * hb - 1, 0), p)),
                  pl.BlockSpec((GDN_T, hv), lambda p, i: (i, p)),
                  pl.BlockSpec((HALO, hv), lambda p, i: (jnp.minimum((i + 1) * hb, last_h), p)),
                  pl.BlockSpec((4, hv), lambda p, i: (0, p)),
                  pl.BlockSpec((None, GDN_H, GDN_T, LANES), lambda p, i: (p, 0, i, 0)),
                  pl.BlockSpec((None, GDN_H, HALO, LANES), lambda p, i: (p, 0, jnp.minimum((i + 1) * hb, last_h), 0))],
        out_specs=[pl.BlockSpec((GDN_T, hv), lambda p, i: (i, p)),
                   pl.BlockSpec((4, hv), lambda p, i: (0, p))],
        out_shape=[SDS((S, 3 * hv), BF16), SDS((4, 3 * hv), F32)],
        compiler_params=_cp(2),
    )(proj, proj, proj, conv_w, dqkv, dqkv)


_DIMS = {"nn": ((1,), (0,)), "nt": ((1,), (1,)), "tn": ((0,), (0,))}


def _mm_raw(a, b, mode, hi):
    if hi:
        return _dot_hi(a, b, _DIMS[mode])
    return lax.dot_general(a.astype(BF16), b.astype(BF16), (_DIMS[mode], ((), ())), preferred_element_type=F32)


@functools.partial(jax.custom_vjp, nondiff_argnums=(2, 3))
def mm(a, b, mode, hi):
    return _mm_raw(a, b, mode, hi)


def _mm_fwd(a, b, mode, hi):
    return _mm_raw(a, b, mode, hi), (a, b)


def _mm_bwd(mode, hi, res, dc):
    a, b = res
    if mode == "nn":
        da, db = mm(dc, b, "nt", hi), mm(a, dc, "tn", hi)
    elif mode == "nt":
        da, db = mm(dc, b, "nn", hi), mm(dc, a, "tn", hi)
    else:
        da, db = mm(b, dc, "nt", hi), mm(a, dc, "nn", hi)
    return da, db


mm.defvjp(_mm_fwd, _mm_bwd)


TRI_BASE = 8


def _unit_lower_inverses(Ls):
    n = Ls[0].shape[0]
    r = lax.broadcasted_iota(jnp.int32, (n, n), 0)
    c = lax.broadcasted_iota(jnp.int32, (n, n), 1)
    eye = jnp.where(r == c, 1.0, 0.0).astype(F32)
    base = r // TRI_BASE == c // TRI_BASE
    one = lambda a, b_: _mm_raw(a, b_, "nn", False)
    Ps = [jnp.where(base, -L, 0.0) for L in Ls]
    invs = [eye + P for P in Ps]
    k = 1
    while 2 * k < TRI_BASE:
        Ps = [one(P, P) for P in Ps]
        invs = [inv + one(inv, P) for inv, P in zip(invs, Ps)]
        k *= 2
    b = 2 * TRI_BASE
    while b <= n:
        off_mask = (r // b == c // b) & ((r % b) >= b // 2) & ((c % b) < b // 2)
        ts = [one(inv, jnp.where(off_mask, L, 0.0)) for inv, L in zip(invs, Ls)]
        invs = [inv - one(t, inv) for inv, t in zip(invs, ts)]
        b *= 2
    resid = [eye - inv - _dot_x3(L, inv) for inv, L in zip(invs, Ls)]
    return [inv + _dot_x3(inv, rs) for inv, rs in zip(invs, resid)]


@jax.custom_vjp
def tri_apply(invs, Ls, r1s, r2s):
    return [_mm_raw(i, r, "nn", False) for i, r in zip(invs, r1s)], [_mm_raw(i, r, "nn", False) for i, r in zip(invs, r2s)]


def _tri_fwd(invs, Ls, r1s, r2s):
    s1s = [_mm_raw(i, r, "nn", False) for i, r in zip(invs, r1s)]
    s2s = [_mm_raw(i, r, "nn", False) for i, r in zip(invs, r2s)]
    return (s1s, s2s), (invs, s1s, s2s)


def _tri_bwd(res, ds):
    invs, s1s, s2s = res
    d1s = [_mm_raw(i, d, "tn", False) for i, d in zip(invs, ds[0])]
    d2s = [_mm_raw(i, d, "tn", False) for i, d in zip(invs, ds[1])]
    dLs = [-(_mm_raw(d1, s1, "nt", False) + _mm_raw(d2, s2, "nt", False)) for d1, s1, d2, s2 in zip(d1s, s1s, d2s, s2s)]
    return [jnp.zeros_like(i) for i in invs], dLs, d1s, d2s


tri_apply.defvjp(_tri_fwd, _tri_bwd)


def _gdn_chunk(qs, ks, vs, gcbs, btbs, Ss, invs=None):
    C = qs[0].shape[0]
    r = lax.broadcasted_iota(jnp.int32, (C, C), 0)
    c = lax.broadcasted_iota(jnp.int32, (C, C), 1)
    causal, strict = c <= r, c < r
    rows = lax.broadcasted_iota(jnp.int32, gcbs[0].shape, 0)
    Gs = [g[:, :C] for g in gcbs]
    decays = [jnp.exp(jnp.where(causal, G - G.T, NEG)) for G in Gs]
    kbs = [k * b for k, b in zip(ks, btbs)]
    vbs = [v * b for v, b in zip(vs, btbs)]
    Ls = [jnp.where(strict, mm(kb, k, "nt", False) * d, 0.0) for kb, k, d in zip(kbs, ks, decays)]
    egs = [jnp.exp(g) for g in gcbs]
    if invs is None:
        invs = _unit_lower_inverses(Ls)
    us, ws = tri_apply(invs, Ls, vbs, [kb * eg for kb, eg in zip(kbs, egs)])
    qks = [jnp.where(causal, mm(q, k, "nt", False) * d, 0.0) for q, k, d in zip(qs, ks, decays)]
    g_lasts = [jnp.sum(jnp.where(rows == C - 1, g, 0.0), axis=0, keepdims=True) for g in gcbs]
    q_decs = [q * eg for q, eg in zip(qs, egs)]
    k_decs = [k * jnp.exp(gl - g) for k, gl, g in zip(ks, g_lasts, gcbs)]
    v_news = [u - mm(w, S, "nn", False) for u, w, S in zip(us, ws, Ss)]
    os_ = [mm(qd, S, "nn", False) + mm(qk, vn, "nn", False) for qd, S, qk, vn in zip(q_decs, Ss, qks, v_news)]
    S_news = [S * jnp.exp(gl) + mm(kd, vn, "tn", False) for S, gl, kd, vn in zip(Ss, g_lasts, k_decs, v_news)]
    return os_, S_news, invs


INV_CHUNKS = 4


def gdn_inverses(qkv, gc, bt, S):
    nchunk = S // GDN_C
    rows = INV_CHUNKS * GDN_C

    def body(k_ref, g_ref, b_ref, inv_ref):
        items = [(h, m) for m in range(INV_CHUNKS) for h in range(GDN_H)]
        r = lax.broadcasted_iota(jnp.int32, (GDN_C, GDN_C), 0)
        c = lax.broadcasted_iota(jnp.int32, (GDN_C, GDN_C), 1)
        sl = lambda m: slice(m * GDN_C, (m + 1) * GDN_C)
        ks = [k_ref[h, sl(m), :].astype(F32) for h, m in items]
        Gs = [g_ref[h, sl(m), :GDN_C] for h, m in items]
        kbs = [k * b_ref[h, sl(m), :] for k, (h, m) in zip(ks, items)]
        decays = [jnp.exp(jnp.where(c <= r, G - G.T, NEG)) for G in Gs]
        Ls = [jnp.where(c < r, _mm_raw(kb, k, "nt", False) * d, 0.0) for kb, k, d in zip(kbs, ks, decays)]
        for (h, m), inv in zip(items, _unit_lower_inverses(Ls)):
            inv_ref[h, m] = inv

    hb = pl.BlockSpec((GDN_H, rows, LANES), lambda n: (0, n, 0))
    return pl.pallas_call(
        body, name="gdn_inverses", grid=(nchunk // INV_CHUNKS,),
        in_specs=[pl.BlockSpec((None, GDN_H, rows, LANES), lambda n: (1, 0, n, 0)), hb, hb],
        out_specs=pl.BlockSpec((GDN_H, INV_CHUNKS, GDN_C, GDN_C), lambda n: (0, n, 0, 0)),
        out_shape=SDS((GDN_H, nchunk, GDN_C, GDN_C), F32),
        compiler_params=_cp(1),
    )(qkv, gc, bt)


def gdn_core(qkv, gc, bt, invs, S):
    nchunk = S // GDN_C

    def body(qkv_ref, g_ref, b_ref, inv_ref, o_ref, st_ref, s_scr):
        n = pl.program_id(0)

        @pl.when(n == 0)
        def _():
            s_scr[...] = jnp.zeros_like(s_scr)

        heads = range(GDN_H)
        S_in = [s_scr[h] for h in heads]
        os_, S_new, _ = _gdn_chunk(*[[qkv_ref[p, h].astype(F32) for h in heads] for p in range(3)],
                                   [g_ref[h] for h in heads], [b_ref[h] for h in heads], S_in, invs=[inv_ref[h] for h in heads])
        for h in heads:
            st_ref[h] = S_in[h].astype(st_ref.dtype)
            o_ref[h] = os_[h]
            s_scr[h] = S_new[h]

    blk3 = pl.BlockSpec((3, GDN_H, GDN_C, LANES), lambda n: (0, 0, n, 0))
    hb = pl.BlockSpec((GDN_H, GDN_C, LANES), lambda n: (0, n, 0))
    return pl.pallas_call(
        body, name="gdn_core", grid=(nchunk,),
        in_specs=[blk3, hb, hb, pl.BlockSpec((GDN_H, None, GDN_C, GDN_C), lambda n: (0, n, 0, 0))],
        out_specs=[hb, pl.BlockSpec((GDN_H, None, GDN_DK, LANES), lambda n: (0, n, 0, 0))],
        out_shape=[SDS((GDN_H, S, LANES), F32), SDS((GDN_H, nchunk, GDN_DK, LANES), BF16)],
        scratch_shapes=[pltpu.VMEM((GDN_H, GDN_DK, LANES), F32)],
        compiler_params=_cp(1),
    )(qkv, gc, bt, invs)


def gdn_core_bwd(qkv, gc, bt, states, invs, do, S):
    nchunk = S // GDN_C

    def body(qkv_ref, g_ref, b_ref, st_ref, inv_ref, do_ref, dqkv_ref, dg_ref, db_ref, ds_scr):
        n = pl.program_id(0)

        @pl.when(n == 0)
        def _():
            ds_scr[...] = jnp.zeros_like(ds_scr)

        heads = range(GDN_H)
        saved = [inv_ref[h] for h in heads]
        _, vjp = jax.vjp(lambda *a: _gdn_chunk(*a, invs=saved)[:2],
                         *[[qkv_ref[p, h].astype(F32) for h in heads] for p in range(3)],
                         [g_ref[h] for h in heads], [b_ref[h] for h in heads], [st_ref[h].astype(F32) for h in heads])
        dq, dk, dv, dg, db, dS = vjp(([do_ref[h] for h in heads], [ds_scr[h] for h in heads]))
        for h in heads:
            dqkv_ref[0, h] = dq[h]
            dqkv_ref[1, h] = dk[h]
            dqkv_ref[2, h] = dv[h]
            dg_ref[h] = dg[h]
            db_ref[h] = db[h]
            ds_scr[h] = dS[h]

    rev = lambda n: nchunk - 1 - n
    blk3 = pl.BlockSpec((3, GDN_H, GDN_C, LANES), lambda n: (0, 0, rev(n), 0))
    hb = pl.BlockSpec((GDN_H, GDN_C, LANES), lambda n: (0, rev(n), 0))
    return pl.pallas_call(
        body, name="gdn_core_bwd", grid=(nchunk,),
        in_specs=[blk3, hb, hb, pl.BlockSpec((GDN_H, None, GDN_DK, LANES), lambda n: (0, rev(n), 0, 0)),
                  pl.BlockSpec((GDN_H, None, GDN_C, GDN_C), lambda n: (0, rev(n), 0, 0)), hb],
        out_specs=[blk3, hb, hb],
        out_shape=[SDS((3, GDN_H, S, LANES), F32), SDS((GDN_H, S, LANES), F32), SDS((GDN_H, S, LANES), F32)],
        scratch_shapes=[pltpu.VMEM((GDN_H, GDN_DK, LANES), F32)],
        compiler_params=_cp(1),
    )(qkv, gc, bt, states, invs, do)


GDN_MAIN = 4 * GDN_H * LANES
GDN_PROJ = GDN_MAIN + LANES
RT = 256


def gdn_forward(h, w_in, conv_w, alog, dtb, out_gain, w_out):
    S = h.shape[0]
    nt = S // RT
    proj = matmul(h, w_in, "nn", BF16, "gdn_in")
    qkv = gdn_pre(proj, conv_w, S)
    ab_row = Row(proj, (RT, LANES), lambda i: (i, GDN_MAIN // LANES), gdtype=BF16, gshape=(S, LANES), gimap=lambda i: (i, 0))
    hm = lambda i: (0, i, 0)
    hv = GDN_H * LANES
    gc, bt = rowwise(f_gdn_gates, [ab_row], [alog, dtb],
                     [Out((GDN_H, S, LANES), F32, (GDN_H, RT, LANES), hm, lead=GDN_H)] * 2, (nt,), "gdn_gates")
    invs = gdn_inverses(qkv, gc, bt, S)
    o, states = gdn_core(qkv, gc, bt, invs, S)
    o_row = Row(o, (GDN_H, RT, LANES), hm, lead=GDN_H)
    z_row = Row(proj, (RT, hv), lambda i: (i, 3), splits=[LANES] * GDN_H, gdtype=BF16, gshape=(S, hv), gimap=lambda i: (i, 0))
    (on,) = rowwise(f_gdn_post, [o_row, z_row], [out_gain],
                    [Out((S, hv), BF16, (RT, hv), lambda i: (i, 0), splits=[LANES] * GDN_H)], (nt,), "gdn_post")
    y = matmul(on, w_out, "nn", BF16, "gdn_out")
    saved = dict(h=h, proj=proj, qkv=qkv, gc=gc, bt=bt, states=states, invs=invs, o=o, on=on, ab_row=ab_row, o_row=o_row, z_row=z_row)
    return y, saved


def gdn_backward(dy, sv, w_in, conv_w, alog, dtb, out_gain, w_out, on_weight_grads=None):
    S = dy.shape[0]
    nt = S // RT
    hm = lambda i: (0, i, 0)
    hv = GDN_H * LANES
    don = matmul(dy, w_out, "nt", BF16, "gdn_out_dx")
    d_w_out = matmul(sv["on"], dy, "tn", F32, "gdn_out_dw")
    (do, dz), (d_gain,) = rowwise_bwd(f_gdn_post, [sv["o_row"], sv["z_row"]], [out_gain],
                                      [Row(don, (RT, hv), lambda i: (i, 0), splits=[LANES] * GDN_H)], (nt,), "gdn_post_bwd")
    dqkv, dgc, dbt = gdn_core_bwd(sv["qkv"], sv["gc"], sv["bt"], sv["states"], sv["invs"], do, S)
    head_blk = lambda a: Row(a, (GDN_H, RT, LANES), hm, lead=GDN_H)
    (dab,), (d_alog, d_dtb) = rowwise_bwd(f_gdn_gates, [sv["ab_row"]], [alog, dtb], [head_blk(dgc), head_blk(dbt)],
                                          (nt,), "gdn_gates_bwd")
    dqkv_proj, d_conv = gdn_pre_bwd(sv["proj"], conv_w, dqkv, S)
    dproj = jnp.concatenate([dqkv_proj, dz, dab], axis=1)
    d_w_in = matmul(sv["h"], dproj, "tn", F32, "gdn_in_dw")
    if on_weight_grads is not None:
        w_in = w_in + on_weight_grads(d_w_in, d_w_out).astype(w_in.dtype)
    dh = matmul(dproj, w_in, "nt", BF16, "gdn_in_dx")
    return dh, dict(w_in=d_w_in, conv=d_conv, alog=d_alog, dtb=d_dtb, gain=d_gain, w_out=d_w_out)


QB = DSW_SPAN
N_HP = DSW_HG // LANES
PROJ_BLKS = 3 * 3 * N_HP


def _bucket_maps():
    a = np.arange(QB)[:, None]
    j = np.arange(2 * QB)[None, :]
    dist = QB + a - j
    band = (dist >= 0) & (dist <= DSW_SPAN)
    maps = []
    for _, dil in DSW_GROUPS:
        dd = np.maximum(dist, 0) * dil
        max_exact = REL_BUCKETS // 2
        scaled = np.log(np.maximum(dd, 1).astype(np.float32) / np.float32(max_exact)) / np.float32(math.log(REL_MAX_DIST / max_exact))
        large = max_exact + (scaled * np.float32(REL_BUCKETS - max_exact)).astype(np.int32)
        large = np.minimum(large, REL_BUCKETS - 1)
        maps.append(np.where(dd < max_exact, dd, large).astype(np.int32))
    return np.stack(maps), band


def dsw_bias(rel_bias):
    maps, band = _bucket_maps()
    maps = np.where(band[None], maps, -1).astype(np.int32)

    def body(tab_ref, bk_ref, o_ref):
        gh = pl.program_id(0)
        bk = bk_ref[...]
        acc = jnp.full(bk.shape, NEG, F32)
        for b in range(REL_BUCKETS):
            acc = jnp.where(bk == b, tab_ref[b, gh], acc)
        o_ref[...] = acc

    return pl.pallas_call(
        body, name="dsw_bias", grid=(3 * GDN_H,),
        in_specs=[pl.BlockSpec(memory_space=pltpu.SMEM),
                  pl.BlockSpec((None, QB, 2 * QB), lambda gh: (gh // GDN_H, 0, 0))],
        out_specs=pl.BlockSpec((None, QB, 2 * QB), lambda gh: (gh, 0, 0)),
        out_shape=SDS((3 * GDN_H, QB, 2 * QB), F32),
        compiler_params=_cp(1),
    )(rel_bias, jnp.asarray(maps))


def dsw_bias_grad(dbias):
    maps, band = _bucket_maps()
    maps = np.where(band[None], maps, -1).astype(np.int32)

    def body(d_ref, bk_ref, o_ref):
        bk = bk_ref[...]
        d = d_ref[...]
        rows = lax.broadcasted_iota(jnp.int32, (REL_BUCKETS, LANES), 0)
        acc = jnp.zeros((REL_BUCKETS, LANES), F32)
        for b in range(REL_BUCKETS):
            part = jnp.sum(jnp.where(bk == b, d, 0.0), axis=0, keepdims=True)
            val = jnp.sum(part, axis=1, keepdims=True)
            acc = jnp.where(rows == b, val, acc)
        o_ref[...] = acc

    return pl.pallas_call(
        body, name="dsw_bias_grad", grid=(3 * GDN_H,),
        in_specs=[pl.BlockSpec((None, QB, 2 * QB), lambda gh: (gh, 0, 0)),
                  pl.BlockSpec((None, QB, 2 * QB), lambda gh: (gh // GDN_H, 0, 0))],
        out_specs=pl.BlockSpec((None, REL_BUCKETS, LANES), lambda gh: (gh, 0, 0)),
        out_shape=SDS((3 * GDN_H, REL_BUCKETS, LANES), F32),
        compiler_params=_cp(1),
    )(dbias, jnp.asarray(maps))


def _nt(a, b):
    return lax.dot_general(a, b, (((1,), (1,)), ((), ())), preferred_element_type=F32)


def _tn(a, b):
    return lax.dot_general(a, b, (((0,), (0,)), ((), ())), preferred_element_type=F32)


def dsw_group_fwd(qn, kn, proj, bias, gi, S):
    dil = DSW_GROUPS[gi][1]
    sd = S // dil
    nq = sd // QB
    qv = qn.reshape(sd, dil * 3 * DSW_HG)
    kv = kn.reshape(sd, dil * 3 * DSW_HG)
    pv = proj.reshape(sd, dil * 9 * DSW_HG)
    qk_col = lambda hp, r: r * (3 * N_HP) + gi * N_HP + hp
    v_col = lambda hp, r: r * PROJ_BLKS + 2 * 3 * N_HP + gi * N_HP + hp

    def body(q_ref, kp_ref, kc_ref, vp_ref, vc_ref, b_ref, o_ref, l_ref):
        i = pl.program_id(2)
        q = q_ref[...]
        k2 = jnp.concatenate([kp_ref[...], kc_ref[...]], axis=0)
        v2 = jnp.concatenate([vp_ref[...], vc_ref[...]], axis=0).astype(BF16)
        lane_q = lax.broadcasted_iota(jnp.int32, (QB, LANES), 1) < DSW_DH
        lane_k = lax.broadcasted_iota(jnp.int32, (2 * QB, LANES), 1) < DSW_DH
        col = lax.broadcasted_iota(jnp.int32, (QB, 2 * QB), 1)
        first = jnp.logical_and(i == 0, col < QB)
        o_acc = jnp.zeros((QB, LANES), F32)
        lse_b = jnp.zeros((QB, LANES), F32)
        for hh in range(2):
            mq = lane_q if hh == 0 else jnp.logical_not(lane_q)
            mk = lane_k if hh == 0 else jnp.logical_not(lane_k)
            s = _nt(jnp.where(mq, q, 0).astype(BF16), k2) + b_ref[hh]
            s = jnp.where(first, NEG, s)
            mx = jnp.max(s, axis=1, keepdims=True)
            p = jnp.exp(s - mx)
            l = jnp.sum(p, axis=1, keepdims=True)
            oh = jnp.dot(p.astype(BF16), jnp.where(mk, v2, 0).astype(BF16), preferred_element_type=F32) / l
            o_acc = o_acc + oh
            lse_b = jnp.where(mq, mx + jnp.log(l), lse_b)
        o_ref[...] = o_acc
        l_ref[...] = lse_b

    blk = (QB, LANES)
    out_spec = pl.BlockSpec(blk, lambda hp, r, i: (i, r * N_HP + hp))
    o, lse = pl.pallas_call(
        body, name=f"dsw_fwd_g{gi}", grid=(N_HP, dil, nq),
        in_specs=[pl.BlockSpec(blk, lambda hp, r, i: (i, qk_col(hp, r))),
                  pl.BlockSpec(blk, lambda hp, r, i: (jnp.maximum(i - 1, 0), qk_col(hp, r))),
                  pl.BlockSpec(blk, lambda hp, r, i: (i, qk_col(hp, r))),
                  pl.BlockSpec(blk, lambda hp, r, i: (jnp.maximum(i - 1, 0), v_col(hp, r))),
                  pl.BlockSpec(blk, lambda hp, r, i: (i, v_col(hp, r))),
                  pl.BlockSpec((2, QB, 2 * QB), lambda hp, r, i: (gi * N_HP + hp, 0, 0))],
        out_specs=[out_spec, out_spec],
        out_shape=[SDS((sd, dil * DSW_HG), F32)] * 2,
        compiler_params=_cp(3),
    )(qv, kv, kv, pv, pv, bias)
    return o.reshape(S, DSW_HG), lse.reshape(S, DSW_HG)


def dsw_group_bwd(qn, kn, proj, bias, do, o, lse, gi, S):
    dil = DSW_GROUPS[gi][1]
    sd = S // dil
    nq = sd // QB
    qv = qn.reshape(sd, dil * 3 * DSW_HG)
    kv = kn.reshape(sd, dil * 3 * DSW_HG)
    pv = proj.reshape(sd, dil * 9 * DSW_HG)
    dov = do.reshape(sd, dil * DSW_HG)
    ov = o.reshape(sd, dil * DSW_HG)
    lv = lse.reshape(sd, dil * DSW_HG)
    qk_col = lambda hp, r: r * (3 * N_HP) + gi * N_HP + hp
    v_col = lambda hp, r: r * PROJ_BLKS + 2 * 3 * N_HP + gi * N_HP + hp
    o_col = lambda hp, r: r * N_HP + hp
    cur = lambda i: jnp.minimum(i, nq - 1)
    prev = lambda i: jnp.maximum(jnp.minimum(i, nq - 1) - 1, 0)
    done = lambda i: jnp.maximum(i - 1, 0)

    def body(q_ref, kp_ref, kc_ref, vp_ref, vc_ref, b_ref, do_ref, o_ref, l_ref,
             dq_ref, dk_ref, dv_ref, db_ref, dk_scr, dv_scr):
        r, i = pl.program_id(1), pl.program_id(2)

        @pl.when(jnp.logical_and(r == 0, i == 0))
        def _():
            db_ref[...] = jnp.zeros_like(db_ref)

        @pl.when(i == 0)
        def _():
            dk_scr[...] = jnp.zeros_like(dk_scr)
            dv_scr[...] = jnp.zeros_like(dv_scr)

        @pl.when(i < nq)
        def _():
            q = q_ref[...]
            k2 = jnp.concatenate([kp_ref[...], kc_ref[...]], axis=0)
            v2 = jnp.concatenate([vp_ref[...], vc_ref[...]], axis=0).astype(BF16)
            dout = do_ref[...].astype(F32)
            prod = dout * o_ref[...].astype(F32)
            lse_b = l_ref[...]
            lane_q = lax.broadcasted_iota(jnp.int32, (QB, LANES), 1) < DSW_DH
            col = lax.broadcasted_iota(jnp.int32, (QB, 2 * QB), 1)
            first = jnp.logical_and(i == 0, col < QB)
            dq = jnp.zeros((QB, LANES), F32)
            dk2 = jnp.zeros((2 * QB, LANES), F32)
            dv2 = jnp.zeros((2 * QB, LANES), F32)
            for hh in range(2):
                mq = lane_q if hh == 0 else jnp.logical_not(lane_q)
                qm = jnp.where(mq, q, 0).astype(BF16)
                dom = jnp.where(mq, dout, 0.0).astype(BF16)
                s = _nt(qm, k2) + b_ref[hh]
                s = jnp.where(first, NEG, s)
                lse_h = jnp.max(jnp.where(mq, lse_b, NEG), axis=1, keepdims=True)
                p = jnp.exp(s - lse_h)
                delta = jnp.sum(jnp.where(mq, prod, 0.0), axis=1, keepdims=True)
                dp = _nt(dom, v2)
                ds = p * (dp - delta)
                dsb = ds.astype(BF16)
                dq = dq + jnp.where(mq, jnp.dot(dsb, k2, preferred_element_type=F32), 0.0)
                dk2 = dk2 + _tn(dsb, qm)
                dv2 = dv2 + _tn(p.astype(BF16), dom)
                db_ref[hh] += ds
            dq_ref[...] = dq
            dk_ref[...] = dk_scr[...] + dk2[:QB]
            dv_ref[...] = (dv_scr[...] + dv2[:QB]).astype(dv_ref.dtype)
            dk_scr[...] = dk2[QB:]
            dv_scr[...] = dv2[QB:]

        @pl.when(i == nq)
        def _():
            dk_ref[...] = dk_scr[...]
            dv_ref[...] = dv_scr[...].astype(dv_ref.dtype)

    blk = (QB, LANES)
    dq, dk, dv, dbias = pl.pallas_call(
        body, name=f"dsw_bwd_g{gi}", grid=(N_HP, dil, nq + 1),
        in_specs=[pl.BlockSpec(blk, lambda hp, r, i: (cur(i), qk_col(hp, r))),
                  pl.BlockSpec(blk, lambda hp, r, i: (prev(i), qk_col(hp, r))),
                  pl.BlockSpec(blk, lambda hp, r, i: (cur(i), qk_col(hp, r))),
                  pl.BlockSpec(blk, lambda hp, r, i: (prev(i), v_col(hp, r))),
                  pl.BlockSpec(blk, lambda hp, r, i: (cur(i), v_col(hp, r))),
                  pl.BlockSpec((2, QB, 2 * QB), lambda hp, r, i: (gi * N_HP + hp, 0, 0)),
                  pl.BlockSpec(blk, lambda hp, r, i: (cur(i), o_col(hp, r))),
                  pl.BlockSpec(blk, lambda hp, r, i: (cur(i), o_col(hp, r))),
                  pl.BlockSpec(blk, lambda hp, r, i: (cur(i), o_col(hp, r)))],
        out_specs=[pl.BlockSpec(blk, lambda hp, r, i: (cur(i), o_col(hp, r))),
                   pl.BlockSpec(blk, lambda hp, r, i: (done(i), o_col(hp, r))),
                   pl.BlockSpec(blk, lambda hp, r, i: (done(i), o_col(hp, r))),
                   pl.BlockSpec((2, QB, 2 * QB), lambda hp, r, i: (hp, 0, 0))],
        out_shape=[SDS((sd, dil * DSW_HG), F32), SDS((sd, dil * DSW_HG), F32), SDS((sd, dil * DSW_HG), BF16),
                   SDS((GDN_H, QB, 2 * QB), F32)],
        scratch_shapes=[pltpu.VMEM(blk, F32), pltpu.VMEM(blk, F32)],
        compiler_params=_cp(3),
    )(qv, kv, kv, pv, pv, bias, dov, ov, lv)
    return dq.reshape(S, DSW_HG), dk.reshape(S, DSW_HG), dv.reshape(S, DSW_HG), dbias


def dsw_forward(h, w_in, q_gain2, k_gain2, rel_bias, w_out):
    S = h.shape[0]
    nt = S // RT
    nb = 3 * N_HP
    proj = matmul(h, w_in, "nn", F32, "dsw_in")
    width = nb * LANES
    lanes12 = [LANES] * nb
    (qn,) = rowwise(f_qnorm, [Row(proj, (RT, width), lambda i: (i, 0), splits=lanes12)], [q_gain2],
                    [Out((S, width), BF16, (RT, width), lambda i: (i, 0), splits=lanes12)], (nt,), "dsw_qnorm")
    (kn,) = rowwise(f_qknorm, [Row(proj, (RT, width), lambda i: (i, 1), splits=lanes12)], [k_gain2],
                    [Out((S, width), BF16, (RT, width), lambda i: (i, 0), splits=lanes12)], (nt,), "dsw_knorm")
    bias = dsw_bias(rel_bias)
    os_, ls_ = [], []
    for gi in range(3):
        o, l = dsw_group_fwd(qn, kn, proj, bias, gi, S)
        os_.append(o)
        ls_.append(l)
    full = lambda a: Row(a, (RT, DSW_HG), lambda i: (i, 0))
    o, lse = rowwise(f_combine, [full(a) for a in os_ + ls_], [],
                     [Out((S, DSW_HG), BF16, (RT, DSW_HG), lambda i: (i, 0)), Out((S, DSW_HG), F32, (RT, DSW_HG), lambda i: (i, 0))],
                     (nt,), "dsw_combine")
    y = matmul(o, w_out, "nn", F32, "dsw_out")
    return y, dict(h=h, proj=proj, qn=qn, kn=kn, bias=bias, o=o, lse=lse)


def dsw_backward(dy, sv, w_in, q_gain2, k_gain2, w_out):
    S = dy.shape[0]
    nt = S // RT
    nb = 3 * N_HP
    do = matmul(dy, w_out, "nt", BF16, "dsw_out_dx")
    d_w_out = matmul(sv["o"], dy, "tn", F32, "dsw_out_dw")
    pieces_q, pieces_k, pieces_v, dbs = [], [], [], []
    d_qg = jnp.zeros((1, LANES), F32)
    d_kg = jnp.zeros((1, LANES), F32)
    for gi in range(3):
        dq, dk, dv, db = dsw_group_bwd(sv["qn"], sv["kn"], sv["proj"], sv["bias"], do, sv["o"], sv["lse"], gi, S)
        dbs.append(db)
        pieces_v.append(dv)
        for which, dd in ((0, dq), (1, dk)):
            lanes4 = [LANES] * N_HP
            row = Row(sv["proj"], (RT, DSW_HG), lambda i, _o=which * 3 + gi: (i, _o), splits=lanes4,
                      gdtype=BF16, gshape=(S, DSW_HG), gimap=lambda i: (i, 0))
            fn, gain = (f_qnorm, q_gain2) if which == 0 else (f_qknorm, k_gain2)
            (dx,), (dg,) = rowwise_bwd(fn, [row], [gain], [Row(dd, (RT, DSW_HG), lambda i: (i, 0), splits=lanes4)],
                                       (nt,), f"dsw_norm_bwd_{which}{gi}")
            if which == 0:
                pieces_q.append(dx)
                d_qg = d_qg + dg
            else:
                pieces_k.append(dx)
                d_kg = d_kg + dg
    dproj = jnp.concatenate(pieces_q + pieces_k + pieces_v, axis=1)
    d_w_in = matmul(sv["h"], dproj, "tn", F32, "dsw_in_dw")
    dh = matmul(dproj, w_in, "nt", F32, "dsw_in_dx")
    d_rel = dsw_bias_grad(jnp.concatenate(dbs, axis=0))
    return dh, dict(w_in=d_w_in, q_gain2=d_qg, k_gain2=d_kg, rel=d_rel, w_out=d_w_out)


N_LB = DSW_HG // LANES
HALF = DSW_DH // 2


def _lanes(j):
    return slice(LANES * j, LANES * (j + 1))


def _deinterleave(stage, out_ref, dil, rows, dtype):
    for r in range(dil):
        for j in range(N_LB):
            out_ref[r, :, _lanes(j)] = stage[j, pl.ds(r, rows, stride=dil), :].astype(dtype)


def _interleave(in_ref, stage, dil, rows):
    for r in range(dil):
        for j in range(N_LB):
            stage[j, pl.ds(r, rows, stride=dil), :] = in_ref[r, :, _lanes(j)].astype(F32)


def dsw_prep(proj, q_gain2, k_gain2, gi, S):
    dil = DSW_GROUPS[gi][1]
    nt, rows = S // RT, RT // dil

    def body(q_ref, k_ref, v_ref, qg_ref, kg_ref, qo_ref, ko_ref, vo_ref, stage):
        for src, gain_ref, scale, dst in ((q_ref, qg_ref, DSW_DH ** -0.5, qo_ref), (k_ref, kg_ref, 1.0, ko_ref), (v_ref, None, None, vo_ref)):
            for j in range(N_LB):
                val = src[:, _lanes(j)].astype(F32)
                val = val if gain_ref is None else _qknorm1(val, gain_ref[...], scale)
                if dil == 1:
                    dst[0, :, _lanes(j)] = val.astype(BF16)
                else:
                    stage[j] = val
            if dil > 1:
                _deinterleave(stage, dst, dil, rows, BF16)

    col = lambda which: pl.BlockSpec((RT, DSW_HG), lambda i, _c=which * 3 + gi: (i, _c))
    gspec = pl.BlockSpec((1, LANES), lambda i: (0, 0))
    ospec = pl.BlockSpec((dil, rows, DSW_HG), lambda i: (0, i, 0))
    return pl.pallas_call(
        body, name=f"dsw_prep_g{gi}", grid=(nt,),
        in_specs=[col(0), col(1), col(2), gspec, gspec], out_specs=[ospec] * 3,
        out_shape=[SDS((dil, S // dil, DSW_HG), BF16)] * 3,
        scratch_shapes=[pltpu.VMEM((N_LB, RT, LANES), F32)], compiler_params=_cp(1),
    )(proj, proj, proj, q_gain2, k_gain2)


def dsw_prep_bwd(proj, q_gain2, k_gain2, dqd, dkd, dvd, gi, S):
    dil = DSW_GROUPS[gi][1]
    nt, rows = S // RT, RT // dil

    def body(q_ref, k_ref, qg_ref, kg_ref, dq_ref, dk_ref, dv_ref, oq_ref, ok_ref, ov_ref, dqg_ref, dkg_ref, stage):
        i = pl.program_id(0)

        @pl.when(i == 0)
        def _():
            dqg_ref[...] = jnp.zeros_like(dqg_ref)
            dkg_ref[...] = jnp.zeros_like(dkg_ref)

        for src, gain_ref, scale, cot_ref, dst, dg_ref in ((q_ref, qg_ref, DSW_DH ** -0.5, dq_ref, oq_ref, dqg_ref),
                                                          (k_ref, kg_ref, 1.0, dk_ref, ok_ref, dkg_ref)):
            if dil > 1:
                _interleave(cot_ref, stage, dil, rows)
            for j in range(N_LB):
                _, vjp = jax.vjp(lambda x, g, _s=scale: _qknorm1(x, g, _s), src[:, _lanes(j)].astype(F32), gain_ref[...])
                dx, dg = vjp(stage[j] if dil > 1 else cot_ref[0, :, _lanes(j)].astype(F32))
                dst[:, _lanes(j)] = dx.astype(dst.dtype)
                dg_ref[...] += dg
        if dil > 1:
            _interleave(dv_ref, stage, dil, rows)
        for j in range(N_LB):
            ov_ref[:, _lanes(j)] = (stage[j] if dil > 1 else dv_ref[0, :, _lanes(j)]).astype(ov_ref.dtype)

    col = lambda which: pl.BlockSpec((RT, DSW_HG), lambda i, _c=which * 3 + gi: (i, _c))
    gspec = pl.BlockSpec((1, LANES), lambda i: (0, 0))
    dspec = pl.BlockSpec((dil, rows, DSW_HG), lambda i: (0, i, 0))
    nspec = pl.BlockSpec((RT, DSW_HG), lambda i: (i, 0))
    return pl.pallas_call(
        body, name=f"dsw_prep_bwd_g{gi}", grid=(nt,),
        in_specs=[col(0), col(1), gspec, gspec, dspec, dspec, dspec], out_specs=[nspec] * 3 + [gspec] * 2,
        out_shape=[SDS((S, DSW_HG), BF16)] * 3 + [SDS((1, LANES), F32)] * 2,
        scratch_shapes=[pltpu.VMEM((N_LB, RT, LANES), F32)], compiler_params=_cp(1),
    )(proj, proj, q_gain2, k_gain2, dqd, dkd, dvd)


def _head_masks(rows):
    lane = lax.broadcasted_iota(jnp.int32, (rows, LANES), 1)
    return lane < DSW_DH, (lane % DSW_DH) < HALF


def dsw_attn_fwd(qd, kd, vd, bias, gi, S):
    dil = DSW_GROUPS[gi][1]
    sd = S // dil
    nq = sd // QB

    def body(q_ref, k_ref, v_ref, b_ref, o_ref, l_ref, kp_scr, vp_scr):
        i = pl.program_id(1)

        @pl.when(i == 0)
        def _():
            kp_scr[...] = jnp.zeros_like(kp_scr)
            vp_scr[...] = jnp.zeros_like(vp_scr)

        lo_q, _ = _head_masks(QB)
        lo_k, _ = _head_masks(2 * QB)
        col = lax.broadcasted_iota(jnp.int32, (QB, 2 * QB), 1)
        first = jnp.logical_and(i == 0, col < QB)
        hps, heads = range(N_HP), range(2 * N_HP)
        k2s = [jnp.concatenate([kp_scr[:, _lanes(hp)], k_ref[:, _lanes(hp)]], axis=0) for hp in hps]
        v2s = [jnp.concatenate([vp_scr[:, _lanes(hp)], v_ref[:, _lanes(hp)]], axis=0) for hp in hps]
        qs = [q_ref[:, _lanes(hp)] for hp in hps]
        k_now, v_now = k_ref[...], v_ref[...]
        mqs = [lo_q if h % 2 == 0 else jnp.logical_not(lo_q) for h in heads]
        mks = [lo_k if h % 2 == 0 else jnp.logical_not(lo_k) for h in heads]
        ss = [jnp.where(first, NEG, _nt(jnp.where(mqs[h], qs[h // 2], 0).astype(BF16), k2s[h // 2]) + b_ref[h]) for h in heads]
        mxs = [jnp.max(s, axis=1, keepdims=True) for s in ss]
        ps = [jnp.exp(s - mx) for s, mx in zip(ss, mxs)]
        ls = [jnp.sum(p, axis=1, keepdims=True) for p in ps]
        ohs = [jnp.dot(ps[h].astype(BF16), jnp.where(mks[h], v2s[h // 2], 0).astype(BF16), preferred_element_type=F32) / ls[h] for h in heads]
        lse_h = [mx + jnp.log(l) for mx, l in zip(mxs, ls)]
        for hp in hps:
            o_ref[:, _lanes(hp)] = ohs[2 * hp] + ohs[2 * hp + 1]
            l_ref[:, _lanes(hp)] = jnp.where(lo_q, lse_h[2 * hp], lse_h[2 * hp + 1])
        kp_scr[...] = k_now
        vp_scr[...] = v_now

    blk = pl.BlockSpec((None, QB, DSW_HG), lambda r, i: (r, i, 0))
    return pl.pallas_call(
        body, name=f"dsw_attn_g{gi}", grid=(dil, nq),
        in_specs=[blk, blk, blk, pl.BlockSpec((GDN_H, QB, 2 * QB), lambda r, i: (gi, 0, 0))],
        out_specs=[blk, blk], out_shape=[SDS((dil, sd, DSW_HG), F32)] * 2,
        scratch_shapes=[pltpu.VMEM((QB, DSW_HG), BF16)] * 2, compiler_params=_cp(2),
    )(qd, kd, vd, bias)


def dsw_attn_bwd(qd, kd, vd, bias, dod, statd, gi, S):
    dil = DSW_GROUPS[gi][1]
    sd = S // dil
    nq = sd // QB
    cur = lambda i: jnp.minimum(i, nq - 1)
    done = lambda i: jnp.maximum(i - 1, 0)

    def body(q_ref, k_ref, v_ref, b_ref, do_ref, st_ref, dq_ref, dk_ref, dv_ref, db_ref, kp_scr, vp_scr, dk_scr, dv_scr):
        r, i = pl.program_id(0), pl.program_id(1)

        @pl.when(jnp.logical_and(r == 0, i == 0))
        def _():
            db_ref[...] = jnp.zeros_like(db_ref)

        @pl.when(i == 0)
        def _():
            for scr in (kp_scr, vp_scr, dk_scr, dv_scr):
                scr[...] = jnp.zeros_like(scr)

        @pl.when(i < nq)
        def _():
            lo_q, first_half = _head_masks(QB)
            col = lax.broadcasted_iota(jnp.int32, (QB, 2 * QB), 1)
            first = jnp.logical_and(i == 0, col < QB)
            hps, heads = range(N_HP), range(2 * N_HP)
            k2s = [jnp.concatenate([kp_scr[:, _lanes(hp)], k_ref[:, _lanes(hp)]], axis=0) for hp in hps]
            v2s = [jnp.concatenate([vp_scr[:, _lanes(hp)], v_ref[:, _lanes(hp)]], axis=0) for hp in hps]
            qs = [q_ref[:, _lanes(hp)] for hp in hps]
            douts = [do_ref[:, _lanes(hp)] for hp in hps]
            stats = [st_ref[:, _lanes(hp)] for hp in hps]
            dkc = [dk_scr[:, _lanes(hp)] for hp in hps]
            dvc = [dv_scr[:, _lanes(hp)] for hp in hps]
            k_now, v_now = k_ref[...], v_ref[...]
            mqs = [lo_q if h % 2 == 0 else jnp.logical_not(lo_q) for h in heads]
            qms = [jnp.where(mqs[h], qs[h // 2], 0).astype(BF16) for h in heads]
            doms = [jnp.where(mqs[h], douts[h // 2], 0).astype(BF16) for h in heads]
            ss = [jnp.where(first, NEG, _nt(qms[h], k2s[h // 2]) + b_ref[h]) for h in heads]
            lses = [jnp.max(jnp.where(jnp.logical_and(mqs[h], first_half), stats[h // 2], NEG), axis=1, keepdims=True) for h in heads]
            deltas = [jnp.max(jnp.where(jnp.logical_and(mqs[h], jnp.logical_not(first_half)), stats[h // 2], NEG), axis=1, keepdims=True)
                      for h in heads]
            ps = [jnp.exp(ss[h] - lses[h]) for h in heads]
            dss = [ps[h] * (_nt(doms[h], v2s[h // 2]) - deltas[h]) for h in heads]
            dsbs = [d.astype(BF16) for d in dss]
            dqh = [jnp.where(mqs[h], jnp.dot(dsbs[h], k2s[h // 2], preferred_element_type=F32), 0.0) for h in heads]
            dkh = [_tn(dsbs[h], qms[h]) for h in heads]
            dvh = [_tn(ps[h].astype(BF16), doms[h]) for h in heads]
            for h in heads:
                db_ref[h] += dss[h]
            for hp in hps:
                dk2 = dkh[2 * hp] + dkh[2 * hp + 1]
                dv2 = dvh[2 * hp] + dvh[2 * hp + 1]
                dq_ref[:, _lanes(hp)] = dqh[2 * hp] + dqh[2 * hp + 1]
                dk_ref[:, _lanes(hp)] = dkc[hp] + dk2[:QB]
                dv_ref[:, _lanes(hp)] = (dvc[hp] + dv2[:QB]).astype(dv_ref.dtype)
                dk_scr[:, _lanes(hp)] = dk2[QB:]
                dv_scr[:, _lanes(hp)] = dv2[QB:]
            kp_scr[...] = k_now
            vp_scr[...] = v_now

        @pl.when(i == nq)
        def _():
            dk_ref[...] = dk_scr[...]
            dv_ref[...] = dv_scr[...].astype(dv_ref.dtype)

    blk = pl.BlockSpec((None, QB, DSW_HG), lambda r, i: (r, cur(i), 0))
    oblk = pl.BlockSpec((None, QB, DSW_HG), lambda r, i: (r, done(i), 0))
    return pl.pallas_call(
        body, name=f"dsw_attn_bwd_g{gi}", grid=(dil, nq + 1),
        in_specs=[blk, blk, blk, pl.BlockSpec((GDN_H, QB, 2 * QB), lambda r, i: (gi, 0, 0)), blk, blk],
        out_specs=[blk, oblk, oblk, pl.BlockSpec((GDN_H, QB, 2 * QB), lambda r, i: (0, 0, 0))],
        out_shape=[SDS((dil, sd, DSW_HG), F32), SDS((dil, sd, DSW_HG), F32), SDS((dil, sd, DSW_HG), BF16),
                   SDS((GDN_H, QB, 2 * QB), F32)],
        scratch_shapes=[pltpu.VMEM((QB, DSW_HG), BF16)] * 2 + [pltpu.VMEM((QB, DSW_HG), F32)] * 2,
        compiler_params=_cp(2),
    )(qd, kd, vd, bias, dod, statd)


def dsw_combine(ods, lseds, S):
    nt = S // RT
    dils = [d for _, d in DSW_GROUPS]

    def body(*refs):
        ins, (o_ref, l_ref), stages = refs[:6], refs[6:8], refs[8:]
        for g in range(3):
            if dils[g] > 1:
                _interleave(ins[g], stages[g], dils[g], RT // dils[g])
                _interleave(ins[3 + g], stages[3 + g], dils[g], RT // dils[g])
        for j in range(N_LB):
            natural = lambda a: stages[a][j] if dils[a % 3] > 1 else ins[a][0, :, _lanes(j)]
            o, lse = f_combine(None, *[natural(a) for a in range(6)])
            o_ref[:, _lanes(j)] = o.astype(o_ref.dtype)
            l_ref[:, _lanes(j)] = lse

    dspec = lambda d: pl.BlockSpec((d, RT // d, DSW_HG), lambda i: (0, i, 0))
    nspec = pl.BlockSpec((RT, DSW_HG), lambda i: (i, 0))
    return pl.pallas_call(
        body, name="dsw_combine", grid=(nt,),
        in_specs=[dspec(d) for d in dils] * 2, out_specs=[nspec, nspec],
        out_shape=[SDS((S, DSW_HG), BF16), SDS((S, DSW_HG), F32)],
        scratch_shapes=[pltpu.VMEM((N_LB, RT, LANES), F32)] * 6, compiler_params=_cp(1),
    )(*ods, *lseds)


def dsw_bwd_prep(do, o, lse, S):
    nt = S // RT
    dils = [d for _, d in DSW_GROUPS]

    def body(do_ref, o_ref, l_ref, *rest):
        outs, (st_do, st_stat) = rest[:6], rest[6:]
        lo, first_half = _head_masks(RT)
        for j in range(N_LB):
            dout = do_ref[:, _lanes(j)]
            prod = dout * o_ref[:, _lanes(j)].astype(F32)
            s_all = jnp.sum(prod, axis=1, keepdims=True)
            s_lo = jnp.sum(jnp.where(lo, prod, 0.0), axis=1, keepdims=True)
            delta = jnp.where(lo, s_lo, s_all - s_lo)
            stat = jnp.where(first_half, l_ref[:, _lanes(j)], delta)
            st_do[j] = dout
            st_stat[j] = stat
            for g in range(3):
                if dils[g] == 1:
                    outs[g][0, :, _lanes(j)] = dout.astype(BF16)
                    outs[3 + g][0, :, _lanes(j)] = stat
        for g in range(3):
            if dils[g] > 1:
                _deinterleave(st_do, outs[g], dils[g], RT // dils[g], BF16)
                _deinterleave(st_stat, outs[3 + g], dils[g], RT // dils[g], F32)

    nspec = pl.BlockSpec((RT, DSW_HG), lambda i: (i, 0))
    dspec = lambda d: pl.BlockSpec((d, RT // d, DSW_HG), lambda i: (0, i, 0))
    res = pl.pallas_call(
        body, name="dsw_bwd_prep", grid=(nt,),
        in_specs=[nspec] * 3, out_specs=[dspec(d) for d in dils] * 2,
        out_shape=[SDS((d, S // d, DSW_HG), BF16) for d in dils] + [SDS((d, S // d, DSW_HG), F32) for d in dils],
        scratch_shapes=[pltpu.VMEM((N_LB, RT, LANES), F32)] * 2, compiler_params=_cp(1),
    )(do, o, lse)
    return res[:3], res[3:]


def dsw_forward(h, w_in, q_gain2, k_gain2, rel_bias, w_out):
    S = h.shape[0]
    proj = matmul(h, w_in, "nn", BF16, "dsw_in", col_shards=N_SHARD)
    bias = dsw_bias(rel_bias)
    qkv, ods, lseds = [], [], []
    for gi in range(3):
        qd, kd, vd = dsw_prep(proj, q_gain2, k_gain2, gi, S)
        od, ld = dsw_attn_fwd(qd, kd, vd, bias, gi, S)
        qkv.append((qd, kd, vd))
        ods.append(od)
        lseds.append(ld)
    o, lse = dsw_combine(ods, lseds, S)
    y = matmul(o, w_out, "nn", BF16, "dsw_out", col_shards=N_SHARD)
    return y, dict(h=h, proj=proj, qkv=qkv, bias=bias, o=o, lse=lse)


def dsw_backward(dy, sv, w_in, q_gain2, k_gain2, w_out):
    S = dy.shape[0]
    do = matmul(dy, w_out, "nt", F32, "dsw_out_dx", col_shards=N_SHARD)
    d_w_out = matmul(sv["o"], dy, "tn", F32, "dsw_out_dw", col_shards=N_SHARD)
    dods, statds = dsw_bwd_prep(do, sv["o"], sv["lse"], S)
    pieces_q, pieces_k, pieces_v, dbs = [], [], [], []
    d_qg = jnp.zeros((1, LANES), F32)
    d_kg = jnp.zeros((1, LANES), F32)
    for gi in range(3):
        qd, kd, vd = sv["qkv"][gi]
        dqd, dkd, dvd, db = dsw_attn_bwd(qd, kd, vd, sv["bias"], dods[gi], statds[gi], gi, S)
        dq, dk, dv, dqg, dkg = dsw_prep_bwd(sv["proj"], q_gain2, k_gain2, dqd, dkd, dvd, gi, S)
        dbs.append(db)
        pieces_q.append(dq)
        pieces_k.append(dk)
        pieces_v.append(dv)
        d_qg = d_qg + dqg
        d_kg = d_kg + dkg
    dproj = jnp.concatenate(pieces_q + pieces_k + pieces_v, axis=1)
    d_w_in = matmul(sv["h"], dproj, "tn", F32, "dsw_in_dw", col_shards=N_SHARD)
    dh = matmul(dproj, w_in, "nt", BF16, "dsw_in_dx", col_shards=N_SHARD)
    d_rel = dsw_bias_grad(jnp.concatenate(dbs, axis=0))
    return dh, dict(w_in=d_w_in, q_gain2=d_qg, k_gain2=d_kg, rel=d_rel, w_out=d_w_out)


FT = 256


FUSE_M = 512


def ffn_in_act(h, w_in, name):
    S = h.shape[0]
    half = FFN // 2

    def body(h_ref, wg_ref, wu_ref, gu_ref, a_ref):
        j = pl.program_id(1)
        sub = FUSE_M // 2
        for part in range(2):
            rows = slice(part * sub, (part + 1) * sub)
            hb = h_ref[rows, :]
            g = jnp.dot(hb, wg_ref[...], preferred_element_type=F32)
            u = jnp.dot(hb, wu_ref[...], preferred_element_type=F32)
            a_ref[rows, :] = (_silu(g) * u).astype(a_ref.dtype)
            for jj in range(2):
                @pl.when(j == jj)
                def _(g=g, u=u, jj=jj, rows=rows):
                    gu_ref[rows, jj * half:(jj + 1) * half] = g.astype(gu_ref.dtype)
                    gu_ref[rows, FFN + jj * half:FFN + (jj + 1) * half] = u.astype(gu_ref.dtype)

    return pl.pallas_call(
        body, name=name, grid=(S // FUSE_M, 2),
        in_specs=[pl.BlockSpec((FUSE_M, D), lambda i, j: (i, 0)),
                  pl.BlockSpec((None, D, half), lambda i, j: (j, 0, 0)),
                  pl.BlockSpec((None, D, half), lambda i, j: (j + 2, 0, 0))],
        out_specs=[pl.BlockSpec((FUSE_M, 2 * FFN), lambda i, j: (i, 0)), pl.BlockSpec((FUSE_M, half), lambda i, j: (i, j))],
        out_shape=[SDS((S, 2 * FFN), BF16), SDS((S, FFN), BF16)],
        compiler_params=_cp(2),
    )(h, w_in, w_in)


def ffn_forward(h, w_in, w_out, tag):
    gu, a = ffn_in_act(h, w_in, f"ffn_in_act_{tag}")
    gu_row = Row(gu, (FT, 2 * FFN), lambda i: (i, 0), splits=[FFN, FFN], gdtype=BF16)
    f = matmul(a, w_out, "nn", BF16, f"ffn_out_{tag}")
    return f, dict(h=h, gu_row=gu_row, a=a)


def ffn_out_dx_act(df, w_out, gu, name):
    S = df.shape[0]
    half = FFN // 2

    def body(df_ref, w_ref, g_ref, u_ref, dgu_ref):
        j = pl.program_id(1)
        sub = FUSE_M // 2
        for part in range(2):
            rows = slice(part * sub, (part + 1) * sub)
            da = _nt(df_ref[rows, :], w_ref[...])
            dg, du = _swiglu_bwd((g_ref[rows, :].astype(F32), u_ref[rows, :].astype(F32)), da)
            for jj in range(2):
                @pl.when(j == jj)
                def _(dg=dg, du=du, jj=jj, rows=rows):
                    dgu_ref[rows, jj * half:(jj + 1) * half] = dg.astype(dgu_ref.dtype)
                    dgu_ref[rows, FFN + jj * half:FFN + (jj + 1) * half] = du.astype(dgu_ref.dtype)

    return pl.pallas_call(
        body, name=name, grid=(S // FUSE_M, 2),
        in_specs=[pl.BlockSpec((FUSE_M, D), lambda i, j: (i, 0)),
                  pl.BlockSpec((half, D), lambda i, j: (j, 0)),
                  pl.BlockSpec((FUSE_M, half), lambda i, j: (i, j)),
                  pl.BlockSpec((FUSE_M, half), lambda i, j: (i, j + 2))],
        out_specs=pl.BlockSpec((FUSE_M, 2 * FFN), lambda i, j: (i, 0)),
        out_shape=SDS((S, 2 * FFN), BF16),
        compiler_params=_cp(2),
    )(df, w_out, gu, gu)


def ffn_backward(df, sv, w_in, w_out, tag):
    d_w_out = matmul(sv["a"], df, "tn", F32, f"ffn_out_dw_{tag}")
    dgu = ffn_out_dx_act(df, w_out, sv["gu_row"].arr, f"ffn_out_dx_act_{tag}")
    d_w_in = matmul(sv["h"], dgu, "tn", F32, f"ffn_in_dw_{tag}", col_shards=N_SHARD)
    dh = matmul(dgu, w_in, "nt", BF16, f"ffn_in_dx_{tag}", col_shards=N_SHARD)
    return dh, d_w_in, d_w_out


def f_norm_only(ids, x, gain, sc, sh):
    return (_normmod(x, gain, sc, sh),)


WT = 512


def _wide(a, **kw):
    return Row(a, (WT, D), lambda i: (i, 0), **kw)


def _wide_out(S, dtype):
    return Out((S, D), dtype, (WT, D), lambda i: (i, 0))


def adamw(w, g, m, v, name):
    shape = w.shape
    C = shape[-1]
    R = int(np.prod(shape[:-1]))
    w2, g2, m2, v2 = (a.reshape(R, C) for a in (w, g, m, v))
    br = R
    if R > 256:
        br = max(b for b in range(8, 257, 8) if R % b == 0)
    c1 = 1.0 / (1.0 - ADAM_B1 ** ADAM_STEP)
    c2 = 1.0 / (1.0 - ADAM_B2 ** ADAM_STEP)

    def body(w_ref, g_ref, m_ref, v_ref, d_ref, nm_ref, nv_ref):
        gg = g_ref[...]
        mm_ = ADAM_B1 * m_ref[...] + (1.0 - ADAM_B1) * gg
        vv = ADAM_B2 * v_ref[...] + (1.0 - ADAM_B2) * (gg * gg)
        d_ref[...] = -ADAM_LR * ((mm_ * c1) / (jnp.sqrt(vv * c2) + ADAM_EPS) + ADAM_WD * w_ref[...])
        nm_ref[...] = mm_
        nv_ref[...] = vv

    spec = pl.BlockSpec((br, C), lambda i: (i, 0))
    d, nm, nv = pl.pallas_call(
        body, name=name, grid=(R // br,), in_specs=[spec] * 4, out_specs=[spec] * 3,
        out_shape=[SDS((R, C), F32)] * 3, compiler_params=_cp(1),
    )(w2, g2, m2, v2)
    return d.reshape(shape), nm.reshape(shape), nv.reshape(shape)


def _place():
    x, y, c = lax.axis_index("x"), lax.axis_index("y"), lax.axis_index("c")
    chips = [(1 - x, y), (x, 1 - y), (1 - x, 1 - y)]
    return x, y, c, chips


def all_gather_small(blk, name):
    m_per, n = blk.shape

    def body(x_ref, out_ref, send_sems, recv_sems, local_sem):
        x, y, c, chips = _place()
        me, sibling = (x, y, c), (x, y, 1 - c)

        def rows(px, py, pc):
            return out_ref.at[pl.ds((4 * px + 2 * py + pc) * m_per, m_per), :]

        def copy(k, block, to, src=None):
            return pltpu.make_async_remote_copy(
                src_ref=rows(*block) if src is None else src, dst_ref=rows(*block),
                send_sem=send_sems.at[k], recv_sem=recv_sems.at[k], device_id=to, device_id_type=MESH)

        mine = pltpu.make_async_copy(x_ref, rows(*me), local_sem)
        mine.start()
        first = [copy(0, me, sibling, src=x_ref)]
        first += [copy(1 + j, me, (*chip, c), src=x_ref) for j, chip in enumerate(chips)]
        for cp in first:
            cp.start()
        passed = [copy(4 + j, (*chip, c), sibling) for j, chip in enumerate(chips)]
        for j, chip in enumerate(chips):
            copy(1 + j, (*chip, c), me).wait_recv()
            passed[j].start()
        copy(0, sibling, me).wait_recv()
        for j, chip in enumerate(chips):
            copy(4 + j, (*chip, 1 - c), me).wait_recv()
        for cp in first + passed:
            cp.wait_send()
        mine.wait()

    return pl.pallas_call(
        body, name=name, out_shape=SDS((N_DEV * m_per, n), blk.dtype),
        in_specs=[pl.BlockSpec(memory_space=pltpu.VMEM)], out_specs=pl.BlockSpec(memory_space=pltpu.VMEM),
        scratch_shapes=[pltpu.SemaphoreType.DMA((7,)), pltpu.SemaphoreType.DMA((7,)), pltpu.SemaphoreType.DMA],
    )(blk)


def _half(cc, rh):
    return pl.ds(pl.multiple_of(cc * rh, 16), rh)


def all_gather_shards(ws):
    n = len(ws)

    def body(*refs):
        w_refs, out_refs = refs[:n], refs[n:2 * n]
        send_sems, recv_sems, local_sems, own_sems = refs[2 * n:]
        x, y, c, chips = _place()
        sibling = (x, y, 1 - c)
        s_me = 2 * x + y

        def copy(k, src, dst, to):
            return pltpu.make_async_remote_copy(src_ref=src, dst_ref=dst, send_sem=send_sems.at[k], recv_sem=recv_sems.at[k],
                                                device_id=to, device_id_type=MESH)

        local, sends, passed = [], [], []
        for k in range(n):
            rh = ws[k].shape[0] // 2
            cp = pltpu.make_async_remote_copy(src_ref=w_refs[k], dst_ref=out_refs[k].at[s_me], send_sem=local_sems.at[k],
                                              recv_sem=own_sems.at[k], device_id=sibling, device_id_type=MESH)
            cp.start()
            local.append(cp)
            for j, chip in enumerate(chips):
                sd = copy(6 * k + j, w_refs[k].at[_half(c, rh)], out_refs[k].at[s_me, _half(c, rh)], (*chip, c))
                sd.start()
                sends.append(sd)
        for k in range(n):
            rh = ws[k].shape[0] // 2
            for j, (px, py) in enumerate(chips):
                got = out_refs[k].at[2 * px + py, _half(c, rh)]
                copy(6 * k + j, got, got, (px, py, c)).wait_recv()
                fw = copy(6 * k + 3 + j, got, got, sibling)
                fw.start()
                passed.append(fw)
        for k in range(n):
            rh = ws[k].shape[0] // 2
            for j, (px, py) in enumerate(chips):
                got = out_refs[k].at[2 * px + py, _half(1 - c, rh)]
                copy(6 * k + 3 + j, got, got, sibling).wait_recv()
        for cp in sends + passed:
            cp.wait_send()
        for cp in local:
            cp.wait()

    return pl.pallas_call(
        body, name="weights_all_gather", out_shape=[SDS((N_SHARD,) + w.shape, w.dtype) for w in ws],
        in_specs=[ANY] * n, out_specs=[ANY] * n,
        scratch_shapes=[pltpu.SemaphoreType.DMA((6 * n,)), pltpu.SemaphoreType.DMA((6 * n,)), pltpu.SemaphoreType.DMA((n,)),
                        pltpu.SemaphoreType.DMA((n,))],
    )(*ws)


def sibling_exchange(sends, name):
    n = len(sends)

    def body(*refs):
        s_refs, o_refs, send_sems, recv_sems = refs[:n], refs[n:2 * n], refs[2 * n], refs[2 * n + 1]
        x, y, c, _ = _place()
        cps = [pltpu.make_async_remote_copy(src_ref=s_refs[k], dst_ref=o_refs[k], send_sem=send_sems.at[k], recv_sem=recv_sems.at[k],
                                            device_id=(x, y, 1 - c), device_id_type=MESH) for k in range(n)]
        for cp in cps:
            cp.start()
        for cp in cps:
            cp.wait()

    return pl.pallas_call(
        body, name=name, out_shape=[SDS(s.shape, s.dtype) for s in sends], in_specs=[ANY] * n, out_specs=[ANY] * n,
        scratch_shapes=[pltpu.SemaphoreType.DMA((n,)), pltpu.SemaphoreType.DMA((n,))],
    )(*sends)


def scatter_to_chips(parts):
    n = len(parts)

    def body(*refs):
        p_refs, o_refs, send_sems, recv_sems = refs[:n], refs[n:2 * n], refs[2 * n], refs[2 * n + 1]
        x, y, c, chips = _place()
        cps = []
        for k in range(n):
            for j, (px, py) in enumerate(chips):
                cp = pltpu.make_async_remote_copy(src_ref=p_refs[k].at[2 * px + py], dst_ref=o_refs[k].at[j],
                                                  send_sem=send_sems.at[3 * k + j], recv_sem=recv_sems.at[3 * k + j],
                                                  device_id=(px, py, c), device_id_type=MESH)
                cp.start()
                cps.append(cp)
        for cp in cps:
            cp.wait()

    return pl.pallas_call(
        body, name="grads_scatter", out_shape=[SDS((3,) + p.shape[1:], p.dtype) for p in parts], in_specs=[ANY] * n, out_specs=[ANY] * n,
        scratch_shapes=[pltpu.SemaphoreType.DMA((3 * n,)), pltpu.SemaphoreType.DMA((3 * n,))],
    )(*parts)


def merge_halves(halves):
    n = len(halves)

    def body(*refs):
        h_refs, o_refs = refs[:n], refs[n:2 * n]
        send_sems, recv_sems, local_sems = refs[2 * n:]
        x, y, c, _ = _place()
        local, cps = [], []
        for k in range(n):
            rh = halves[k].shape[0]
            lc = pltpu.make_async_copy(h_refs[k], o_refs[k].at[_half(c, rh)], local_sems.at[k])
            lc.start()
            local.append(lc)
            cp = pltpu.make_async_remote_copy(src_ref=h_refs[k], dst_ref=o_refs[k].at[_half(c, rh)], send_sem=send_sems.at[k],
                                              recv_sem=recv_sems.at[k], device_id=(x, y, 1 - c), device_id_type=MESH)
            cp.start()
            cps.append(cp)
        for k in range(n):
            rh = halves[k].shape[0]
            got = o_refs[k].at[_half(1 - c, rh)]
            pltpu.make_async_remote_copy(src_ref=got, dst_ref=got, send_sem=send_sems.at[k], recv_sem=recv_sems.at[k],
                                         device_id=(x, y, 1 - c), device_id_type=MESH).wait_recv()
        for cp in cps:
            cp.wait_send()
        for lc in local:
            lc.wait()

    return pl.pallas_call(
        body, name="grads_merge_halves", out_shape=[SDS((2 * h.shape[0], h.shape[1]), h.dtype) for h in halves],
        in_specs=[ANY] * n, out_specs=[ANY] * n,
        scratch_shapes=[pltpu.SemaphoreType.DMA((n,)), pltpu.SemaphoreType.DMA((n,)), pltpu.SemaphoreType.DMA((n,))],
    )(*halves)


def add_rows(arrs, out_dtype, name, rt=256):
    Rr, W = arrs[0].shape

    def fn(ids, *vals):
        acc = vals[0]
        for v in vals[1:]:
            acc = acc + v
        return (acc,)

    t = rt if Rr % rt == 0 else max(b for b in range(16, rt + 1, 16) if Rr % b == 0)
    (out,) = rowwise(fn, [Row(a, (t, W), lambda i: (i, 0)) for a in arrs], [],
                     [Out((Rr, W), out_dtype, (t, W), lambda i: (i, 0))], (Rr // t,), name)
    return out


HBM_SPEC = pl.BlockSpec(memory_space=pltpu.HBM)
SEM_SPEC = pl.BlockSpec(memory_space=pltpu.SEMAPHORE)
DATAFLOW = pltpu.SideEffectType.DATAFLOW_SIDE_EFFECTING


def _in_hbm(a):
    return pltpu.with_memory_space_constraint(a, pltpu.HBM)


def _gather_copies(w_refs, land_refs, send_sems, recv_sems):
    x, y, c, chips = _place()
    targets = [(x, y, 1 - c)] + [(*chip, c) for chip in chips]
    cps = []
    for k, (w_ref, land_ref) in enumerate(zip(w_refs, land_refs)):
        for j, to in enumerate(targets):
            cps.append(pltpu.make_async_remote_copy(src_ref=w_ref, dst_ref=land_ref.at[2 * x + y], send_sem=send_sems.at[4 * k + j],
                                                    recv_sem=recv_sems.at[4 * k + j], device_id=to, device_id_type=MESH))
    return cps


def _scatter_copies(p_refs, land_refs, send_sems, recv_sems):
    x, y, c, chips = _place()
    cps = []
    for k, (p_ref, land_ref) in enumerate(zip(p_refs, land_refs)):
        for j, (px, py) in enumerate(chips):
            cps.append(pltpu.make_async_remote_copy(src_ref=p_ref.at[2 * px + py], dst_ref=land_ref.at[j], send_sem=send_sems.at[3 * k + j],
                                                    recv_sem=recv_sems.at[3 * k + j], device_id=(px, py, c), device_id_type=MESH))
    return cps


def copies_start(srcs, land_shapes, make_copies, per_src, name):
    n = len(srcs)
    m = per_src * n

    def body(*refs):
        src_refs, land_refs = refs[:n], refs[n:2 * n]
        send_sems, recv_sems, token = refs[2 * n], refs[2 * n + 1], refs[-1]
        for cp in make_copies(src_refs, land_refs, send_sems, recv_sems):
            cp.start()
        token[...] = jnp.zeros_like(token)

    lands = [lax.empty(shp, s.dtype) for shp, s in zip(land_shapes, srcs)]
    res = pl.pallas_call(
        body, name=name,
        out_shape=(pltpu.SemaphoreType.DMA((m,)), pltpu.SemaphoreType.DMA((m,)), *[pltpu.HBM(s.shape, s.dtype) for s in srcs],
                   *[pltpu.HBM(shp, s.dtype) for shp, s in zip(land_shapes, srcs)], SDS((8, LANES), F32)),
        in_specs=[HBM_SPEC] * (2 * n),
        out_specs=(SEM_SPEC, SEM_SPEC, *[HBM_SPEC] * (2 * n), pl.BlockSpec(memory_space=pltpu.VMEM)),
        input_output_aliases={i: 2 + i for i in range(2 * n)},
        compiler_params=pltpu.CompilerParams(has_side_effects=DATAFLOW),
    )(*[_in_hbm(s) for s in srcs], *[_in_hbm(l) for l in lands])
    return res[0], res[1], list(res[2:2 + n]), list(res[2 + n:2 + 2 * n]), res[-1]


def copies_wait(send_sems, recv_sems, srcs, lands, make_copies, after, name):
    n = len(srcs)

    def body(*refs):
        src_refs, land_refs = refs[:n], refs[n:2 * n]
        for cp in make_copies(src_refs, land_refs, refs[2 * n], refs[2 * n + 1]):
            cp.wait_send()
            cp.wait_recv()

    res = pl.pallas_call(
        body, name=name,
        out_shape=(*[pltpu.HBM(s.shape, s.dtype) for s in srcs], *[pltpu.HBM(l.shape, l.dtype) for l in lands]),
        in_specs=[HBM_SPEC] * (2 * n) + [SEM_SPEC, SEM_SPEC, ANY],
        out_specs=tuple([HBM_SPEC] * (2 * n)),
        input_output_aliases={i: i for i in range(2 * n)},
        compiler_params=pltpu.CompilerParams(has_side_effects=DATAFLOW),
    )(*srcs, *lands, send_sems, recv_sems, after)
    return list(res[n:])


PACK = (("gdn_w_in", 2), ("gdn_w_out", 1), ("w_ffn_in", 2), ("w_ffn_out", 1), ("dsw_w_in", 2), ("dsw_w_out", 2))
PACK_ALIGN = 32


def _pack_rows(sizes):
    total = sum(sizes)
    rows = -(-total // D)
    return -(-rows // PACK_ALIGN) * PACK_ALIGN


def pack_blocks(blocks, dtype):
    flat = [b.astype(dtype).reshape(-1) for b in blocks]
    total = sum(f.shape[0] for f in flat)
    R = _pack_rows([f.shape[0] for f in flat])
    flat.append(jnp.zeros((R * D - total,), dtype))
    return jnp.concatenate(flat).reshape(R, D)


def unpack_blocks(buf, shapes):
    flat = buf.reshape(-1)
    out, off = [], 0
    for shp in shapes:
        n = int(np.prod(shp))
        out.append(flat[off:off + n].reshape(shp))
        off += n
    return out


def _shard_slice(a, axis, s):
    n = a.shape[axis] // N_SHARD
    return lax.slice_in_dim(a, s * n, (s + 1) * n, axis=axis)


def _pad_lanes(v):
    return jnp.concatenate([v.astype(F32), jnp.zeros((LANES - v.shape[0],), F32)])[None]


def kernel(x, c, w_ada, b_ada, norm_mix, norm_ffn, w_ffn_in, w_ffn_out, gdn_w_in, gdn_conv, gdn_a_log, gdn_dt_bias, gdn_out_norm, gdn_w_out, dsw_w_in, dsw_q_norm, dsw_k_norm, dsw_w_out, rel_bias, loss_target, m_w_ada, m_b_ada, m_norm_mix, m_norm_ffn, m_w_ffn_in, m_w_ffn_out, m_gdn_w_in, m_gdn_conv, m_gdn_a_log, m_gdn_dt_bias, m_gdn_out_norm, m_gdn_w_out, m_dsw_w_in, m_dsw_q_norm, m_dsw_k_norm, m_dsw_w_out, m_rel_bias, v_w_ada, v_b_ada, v_norm_mix, v_norm_ffn, v_w_ffn_in, v_w_ffn_out, v_gdn_w_in, v_gdn_conv, v_gdn_a_log, v_gdn_dt_bias, v_gdn_out_norm, v_gdn_w_out, v_dsw_w_in, v_dsw_q_norm, v_dsw_k_norm, v_dsw_w_out, v_rel_bias):
    S = x.shape[1]
    nt = S // WT
    xi, yi, ci = lax.axis_index("x"), lax.axis_index("y"), lax.axis_index("c")
    me = 4 * xi + 2 * yi + ci
    s_me = 2 * xi + yi
    x0, tgt = x[0], loss_target[0]
    shard = dict(w_ffn_in=w_ffn_in, w_ffn_out=w_ffn_out, gdn_w_in=gdn_w_in, gdn_w_out=gdn_w_out, dsw_w_in=dsw_w_in, dsw_w_out=dsw_w_out)

    whole = lambda a: Row(a, a.shape, lambda i: (0,) * a.ndim)
    (cond8,) = rowwise(lambda ids, v: (_silu(v),), [whole(c.reshape(8, LANES))], [], [Out((8, LANES), F32, (8, LANES), lambda i: (0, 0))], (1,), "cond")
    cond_all = all_gather_small(cond8, "gather_cond").reshape(N_DEV, D)
    cond16 = jnp.concatenate([cond_all, jnp.zeros((8, D), F32)], axis=0)
    ada_cols = w_ada.shape[2]
    mods = [matmul(cond16, w_ada[l], "nn", F32, f"ada_{l}")[:N_DEV] for l in range(2)]
    buf = jnp.concatenate([jnp.stack(mods, axis=1).reshape(-1, LANES), gdn_conv.reshape(-1, LANES)], axis=0)
    n_mod_rows = N_DEV * 2 * ada_cols // LANES
    got = all_gather_small(buf, "gather_mod").reshape(N_DEV, buf.shape[0], LANES)
    mod_parts, conv_parts = [], []
    for s in range(N_SHARD):
        from_dev = got[2 * s]
        mod_parts.append(lax.dynamic_index_in_dim(from_dev[:n_mod_rows].reshape(N_DEV, 2, ada_cols), me, 0, keepdims=False))
        conv_parts.append(from_dev[n_mod_rows:].reshape(4, -1))
    mod_nb = jnp.concatenate(mod_parts, axis=1)
    conv_w = jnp.concatenate(conv_parts, axis=1)
    (mod,) = rowwise(lambda ids, a, b: (a + b,), [whole(mod_nb), whole(b_ada)], [], [Out(mod_nb.shape, F32, mod_nb.shape, lambda i: (0, 0))], (1,), "mod_bias")
    mod = mod.reshape(2, 6, 1, D)
    sh1, sc1, g1, sh2, sc2, g2 = ([mod[l, k] for l in range(2)] for k in range(6))
    gmix = [norm_mix[l][None] for l in range(2)]
    gffn = [norm_ffn[l][None] for l in range(2)]

    gcols = gdn_w_in.shape[2]
    g_gdn_in, g_gdn_out = all_gather_shards([gdn_w_in[0].astype(BF16), gdn_w_out[0].astype(BF16)])
    gathered = lambda ws: [(N_SHARD,) + w.shape for w in ws]
    gate = (jnp.minimum(jnp.abs(g_gdn_in[0, 0, 0].astype(F32)), 0.0) + jnp.minimum(jnp.abs(mod[0, 0, 0, 0]), 0.0)).astype(BF16)
    w2 = [w_ffn_in[0].astype(BF16) + gate, w_ffn_out[0].astype(BF16) + gate]
    w3 = [dsw_w_in[0].astype(BF16) + gate, dsw_w_out[0].astype(BF16) + gate, w_ffn_in[1].astype(BF16) + gate, w_ffn_out[1].astype(BF16) + gate]
    fly2 = copies_start(w2, gathered(w2), _gather_copies, 4, "weights_ffn0_start")
    fly3 = copies_start(w3, gathered(w3), _gather_copies, 4, "weights_layer1_start")
    started = fly2[4][0, 0] + fly3[4][0, 0]
    w_gdn = jnp.concatenate([g_gdn_in[s] for s in range(N_SHARD)] + [jnp.zeros((D, GDN_PROJ - N_SHARD * gcols), BF16)], axis=1)
    alog, dtb = _pad_lanes(gdn_a_log[0]), _pad_lanes(gdn_dt_bias[0])
    qg2 = jnp.concatenate([dsw_q_norm, dsw_q_norm], axis=1)
    kg2 = jnp.concatenate([dsw_k_norm, dsw_k_norm], axis=1)
    w_gdn_out = g_gdn_out.reshape(GDN_H * LANES, D)
    gdn_args = (w_gdn, conv_w, alog, dtb, gdn_out_norm, w_gdn_out)
    sc1[0] = sc1[0] + started

    (h10,) = rowwise(f_norm_only, [_wide(x0)], [gmix[0], sc1[0], sh1[0]], [_wide_out(S, BF16)], (nt,), "l0_norm")
    y0, sv_g = gdn_forward(h10, *gdn_args)
    x1, h20 = rowwise(f_resid_norm, [_wide(x0), _wide(y0)], [g1[0], gffn[0], sc2[0], sh2[0]], [_wide_out(S, F32), _wide_out(S, BF16)], (nt,), "l0_mid")
    g_in0, g_out0 = copies_wait(*fly2[:4], _gather_copies, y0, "weights_ffn0_wait")
    w_ffn = [(g_in0, g_out0.reshape(FFN, D)), None]
    f0, sv_f0 = ffn_forward(h20, *w_ffn[0], "0")
    x2, h11 = rowwise(f_resid_norm, [_wide(x1), _wide(f0)], [g2[0], gmix[1], sc1[1], sh1[1]], [_wide_out(S, F32), _wide_out(S, BF16)], (nt,), "l1_in")
    g_dsw_in, g_dsw_out, g_in1, g_out1 = copies_wait(*fly3[:4], _gather_copies, f0, "weights_layer1_wait")
    w_ffn[1] = (g_in1, g_out1.reshape(FFN, D))
    dsw_args = (g_dsw_in, qg2, kg2)
    y1, sv_d = dsw_forward(h11, *dsw_args, rel_bias, g_dsw_out)
    x3, h21 = rowwise(f_resid_norm, [_wide(x2), _wide(y1)], [g1[1], gffn[1], sc2[1], sh2[1]], [_wide_out(S, F32), _wide_out(S, BF16)], (nt,), "l1_mid")
    f1, sv_f1 = ffn_forward(h21, *w_ffn[1], "1")
    parts, dx3, df1, dg2_1 = loss_and_grad(x3, f1, tgt, g2[1], S)
    loss = lax.psum(jnp.sum(parts), ("x", "y", "c"))

    dh21, d_win1, d_wout1 = ffn_backward(df1, sv_f1, *w_ffn[1], "1")
    (dx2, dy1), (dg1_1, dgf1, dsc2_1, dsh2_1) = rowwise_bwd(
        f_resid_norm, [_wide(x2), _wide(y1, gdtype=BF16)], [g1[1], gffn[1], sc2[1], sh2[1]], [_wide(dx3), _wide(dh21)], (nt,), "l1_mid_bwd")
    dh11, g_d = dsw_backward(dy1, sv_d, *dsw_args, g_dsw_out)
    (dx1, df0), (dg2_0, dgm1, dsc1_1, dsh1_1) = rowwise_bwd(
        f_resid_norm, [_wide(x1), _wide(f0, gdtype=BF16)], [g2[0], gmix[1], sc1[1], sh1[1]], [_wide(dx2), _wide(dh11)], (nt,), "l1_in_bwd")
    by_shard = lambda a: a.reshape(N_SHARD, a.shape[0] // N_SHARD, a.shape[1])
    landing = lambda ps: [(3,) + p.shape[1:] for p in ps]
    dws3 = [g_d["w_in"], g_d["w_out"], d_win1, by_shard(d_wout1)]
    parts3 = [a.astype(BF16) for a in dws3]
    gfly3 = copies_start(parts3, landing(parts3), _scatter_copies, 3, "grads_layer1_start")
    w_out0 = w_ffn[0][1] + gfly3[4][0, 0].astype(BF16)
    dh20, d_win0, d_wout0 = ffn_backward(df0, sv_f0, w_ffn[0][0], w_out0, "0")
    (dx0p, dy0), (dg1_0, dgf0, dsc2_0, dsh2_0) = rowwise_bwd(
        f_resid_norm, [_wide(x0), _wide(y0, gdtype=BF16)], [g1[0], gffn[0], sc2[0], sh2[0]], [_wide(dx1), _wide(dh20)], (nt,), "l0_mid_bwd")
    dws2 = [d_win0, by_shard(d_wout0)]
    parts2 = [a.astype(BF16) for a in dws2]
    gfly2 = copies_start(parts2, landing(parts2), _scatter_copies, 3, "grads_ffn0_start")
    gdn_args = gdn_args[:5] + (w_gdn_out + gfly2[4][0, 0].astype(BF16),)
    gdn_flight = []

    def start_gdn_grads(d_w_in, d_w_out):
        dws1 = [jnp.stack([d_w_in[:, s * gcols:(s + 1) * gcols] for s in range(N_SHARD)]), by_shard(d_w_out)]
        parts1 = [a.astype(BF16) for a in dws1]
        fly = copies_start(parts1, landing(parts1), _scatter_copies, 3, "grads_gdn_start")
        gdn_flight.extend([dws1, fly])
        return fly[4][0, 0]

    dh10, g_g = gdn_backward(dy0, sv_g, *gdn_args, on_weight_grads=start_gdn_grads)
    dws1, gfly1 = gdn_flight
    (grad_x,), (dgm0, dsc1_0, dsh1_0) = rowwise_bwd(f_first, [_wide(x0)], [gmix[0], sc1[0], sh1[0]], [_wide(dx0p), _wide(dh10)], (nt,), "l0_norm_bwd")

    dmod = jnp.concatenate([dsh1_0, dsc1_0, dg1_0, dsh2_0, dsc2_0, dg2_0, dsh1_1, dsc1_1, dg1_1, dsh2_1, dsc2_1, dg2_1], axis=1)
    d_rel = jnp.transpose(g_d["rel"][:, :, 0])
    fold = lambda v: v[:, :DSW_DH] + v[:, DSW_DH:]
    small = [dmod, jnp.concatenate([dgm0, dgm1], axis=1), jnp.concatenate([dgf0, dgf1], axis=1), g_g["conv"].reshape(1, -1),
             g_g["alog"], g_g["dtb"], g_g["gain"], _pad_lanes(fold(g_d["q_gain2"])[0]), _pad_lanes(fold(g_d["k_gain2"])[0]),
             d_rel.reshape(1, -1)]
    used = [v.shape[1] // LANES for v in small]
    sizes = [-(-u // 8) * 8 for u in used]
    pad8 = lambda v, u, s: jnp.concatenate([v.reshape(u, LANES), jnp.zeros((s - u, LANES), F32)], axis=0) if s > u else v.reshape(u, LANES)
    pad_rows = sum(sizes)
    sbuf = jnp.concatenate([pad8(v, u, s) for v, u, s in zip(small, used, sizes)], axis=0)
    sgot = all_gather_small(sbuf, "gather_small_grads")
    ssum = add_rows([sgot[d * pad_rows:(d + 1) * pad_rows] for d in range(N_DEV)], F32, "sum_small_grads", rt=pad_rows)
    offs = np.cumsum([0] + sizes)
    take = lambda k: ssum[offs[k]:offs[k] + used[k]].reshape(1, -1)
    grad_b_ada = take(0).reshape(2, 6 * D)
    grad_norm_mix = take(1).reshape(2, D)
    grad_norm_ffn = take(2).reshape(2, D)
    conv_full = take(3).reshape(4, -1)
    ncv = gdn_conv.shape[2]
    grad_gdn_conv = lax.dynamic_slice_in_dim(conv_full, s_me * ncv, ncv, axis=1)[None]
    grad_a_log = take(4)[:, :GDN_H]
    grad_dt_bias = take(5)[:, :GDN_H]
    grad_out_norm = take(6)
    grad_q_norm = take(7)[:, :DSW_DH]
    grad_k_norm = take(8)[:, :DSW_DH]
    grad_rel = take(9).reshape(REL_BUCKETS, 3 * GDN_H)
    dmod_all = sgot.reshape(N_DEV, pad_rows, LANES)[:, :used[0]].reshape(N_DEV, 2, 6 * D)
    dmod_mine = lax.dynamic_slice_in_dim(dmod_all, s_me * ada_cols, ada_cols, axis=2)
    dmod16 = jnp.concatenate([dmod_mine, jnp.zeros_like(dmod_mine)], axis=0)
    grad_w_ada = jnp.stack([matmul(cond16, dmod16[:, l], "tn", F32, f"ada_dw_{l}") for l in range(2)])

    got3 = copies_wait(*gfly3[:4], _scatter_copies, grad_x, "grads_layer1_wait")
    got2 = copies_wait(*gfly2[:4], _scatter_copies, grad_x, "grads_ffn0_wait")
    got1 = copies_wait(*gfly1[:4], _scatter_copies, grad_x, "grads_gdn_wait")
    core_sums = []
    for i, (full, got) in enumerate(zip(dws3 + dws2 + dws1, got3 + got2 + got1)):
        own = lax.dynamic_index_in_dim(full, s_me, 0, keepdims=False)
        core_sums.append(add_rows([own, got[0], got[1], got[2]], F32, f"grads_core_sum_{i}"))
    sib_sums = sibling_exchange(core_sums, "grads_core_sums_swap")
    s_dsw_in, s_dsw_out, s_in1, s_out1, s_in0, s_out0, s_gdn_in, s_gdn_out = [
        add_rows([a, b], F32, f"grads_chip_total_{i}") for i, (a, b) in enumerate(zip(core_sums, sib_sums))]
    gsh = dict(gdn_w_in=s_gdn_in[None], gdn_w_out=s_gdn_out[None],
               w_ffn_in=jnp.stack([s_in0, s_in1]), w_ffn_out=jnp.stack([s_out0, s_out1]),
               dsw_w_in=s_dsw_in[None], dsw_w_out=s_dsw_out[None])

    grads = dict(w_ada=grad_w_ada, b_ada=grad_b_ada, norm_mix=grad_norm_mix, norm_ffn=grad_norm_ffn, w_ffn_in=gsh["w_ffn_in"],
                 w_ffn_out=gsh["w_ffn_out"], gdn_w_in=gsh["gdn_w_in"], gdn_conv=grad_gdn_conv, gdn_a_log=grad_a_log,
                 gdn_dt_bias=grad_dt_bias, gdn_out_norm=grad_out_norm, gdn_w_out=gsh["gdn_w_out"], dsw_w_in=gsh["dsw_w_in"],
                 dsw_q_norm=grad_q_norm, dsw_k_norm=grad_k_norm, dsw_w_out=gsh["dsw_w_out"], rel_bias=grad_rel)
    weights = dict(w_ada=w_ada, b_ada=b_ada, norm_mix=norm_mix, norm_ffn=norm_ffn, w_ffn_in=w_ffn_in, w_ffn_out=w_ffn_out,
                   gdn_w_in=gdn_w_in, gdn_conv=gdn_conv, gdn_a_log=gdn_a_log, gdn_dt_bias=gdn_dt_bias, gdn_out_norm=gdn_out_norm,
                   gdn_w_out=gdn_w_out, dsw_w_in=dsw_w_in, dsw_q_norm=dsw_q_norm, dsw_k_norm=dsw_k_norm, dsw_w_out=dsw_w_out,
                   rel_bias=rel_bias)
    ms = dict(w_ada=m_w_ada, b_ada=m_b_ada, norm_mix=m_norm_mix, norm_ffn=m_norm_ffn, w_ffn_in=m_w_ffn_in, w_ffn_out=m_w_ffn_out,
              gdn_w_in=m_gdn_w_in, gdn_conv=m_gdn_conv, gdn_a_log=m_gdn_a_log, gdn_dt_bias=m_gdn_dt_bias, gdn_out_norm=m_gdn_out_norm,
              gdn_w_out=m_gdn_w_out, dsw_w_in=m_dsw_w_in, dsw_q_norm=m_dsw_q_norm, dsw_k_norm=m_dsw_k_norm, dsw_w_out=m_dsw_w_out,
              rel_bias=m_rel_bias)
    vs = dict(w_ada=v_w_ada, b_ada=v_b_ada, norm_mix=v_norm_mix, norm_ffn=v_norm_ffn, w_ffn_in=v_w_ffn_in, w_ffn_out=v_w_ffn_out,
              gdn_w_in=v_gdn_w_in, gdn_conv=v_gdn_conv, gdn_a_log=v_gdn_a_log, gdn_dt_bias=v_gdn_dt_bias, gdn_out_norm=v_gdn_out_norm,
              gdn_w_out=v_gdn_w_out, dsw_w_in=v_dsw_w_in, dsw_q_norm=v_dsw_q_norm, dsw_k_norm=v_dsw_k_norm, dsw_w_out=v_dsw_w_out,
              rel_bias=v_rel_bias)
    names = list(weights)
    deltas, new_m, new_v = [], [], []
    for n in names:
        g = grads[n].reshape(weights[n].shape)
        grads[n] = g
        d, nm, nv = adamw(weights[n], g, ms[n], vs[n], f"adamw_{n}")
        deltas.append(d)
        new_m.append(nm)
        new_v.append(nv)
    return (loss, grad_x[None], *[grads[n] for n in names], *deltas, *new_m, *new_v)
```

```python
import functools
import math

import numpy as np
import jax
import jax.numpy as jnp
from jax import lax
from jax.experimental import pallas as pl
from jax.experimental.pallas import tpu as pltpu

F32 = jnp.float32
BF16 = jnp.bfloat16
SDS = jax.ShapeDtypeStruct
MESH = pl.DeviceIdType.MESH
ANY = pl.BlockSpec(memory_space=pl.ANY)

D = 1024
EPS = 1e-6
LANES = 128
GDN_H = 8
GDN_DK = 128
GDN_C = 64
DSW_GROUPS = ((128, 1), (512, 4), (2048, 16))
DSW_SPAN = 128
DSW_DH = 64
DSW_HG = 512
REL_BUCKETS = 32
REL_MAX_DIST = 2048
FFN = 2816
N_SHARD = 4
N_DEV = 8
VMEM_LIMIT = 48 * 1024 * 1024
NEG = -1e30

ADAM_LR, ADAM_B1, ADAM_B2, ADAM_EPS, ADAM_WD, ADAM_STEP = 0.001, 0.9, 0.999, 1e-08, 0.01, 10


def _cp(n_axes):
    return pltpu.CompilerParams(dimension_semantics=("arbitrary",) * n_axes, vmem_limit_bytes=VMEM_LIMIT)


def _blk(dim, cap):
    if dim <= cap:
        return dim
    best = None
    for b in range(LANES, cap + 1, LANES):
        if dim % b == 0:
            best = b
    assert best is not None, (dim, cap)
    return best


MAX_SHARD_BLOCK = 1408
def matmul(a, b, mode, out_dtype, name, cap_m=MAX_SHARD_BLOCK, cap_n=MAX_SHARD_BLOCK, cap_k=2048, col_shards=0):
    ns = col_shards
    if mode == "nn":
        (M, K) = a.shape
        K2, N = (b.shape[1], ns * b.shape[2]) if ns else b.shape
    elif mode == "nt":
        (M, K) = a.shape
        N, K2 = (b.shape[1], ns * b.shape[2]) if ns else b.shape
    else:
        (K, M), (K2, N) = a.shape, b.shape
    assert K == K2, (a.shape, b.shape, mode)
    if K <= 3072:
        cap_k = K
        if K > 2048:
            cap_n = 1024
    n_unit = N // ns if (ns and mode != "nt") else N
    k_unit = K // ns if (ns and mode == "nt") else K
    bm = _blk(M, cap_m)
    bn = _blk(n_unit, MAX_SHARD_BLOCK) if n_unit != N else _blk(N, cap_n)
    if k_unit != K:
        bk = _blk(k_unit, MAX_SHARD_BLOCK)
    else:
        bk = _blk(K, 1024 if (ns and mode == "tn") else cap_k)
    nk = K // bk
    nps, kps = n_unit // bn, k_unit // bk
    dims = {"nn": ((1,), (0,)), "nt": ((1,), (1,)), "tn": ((0,), (0,))}[mode]

    def dot(a_ref, b_ref):
        return lax.dot_general(a_ref[...].astype(BF16), b_ref[...].astype(BF16), (dims, ((), ())), preferred_element_type=F32)

    def body_one(a_ref, b_ref, o_ref):
        o_ref[...] = dot(a_ref, b_ref).astype(o_ref.dtype)

    def body_acc(a_ref, b_ref, o_ref, acc_ref):
        k = pl.program_id(2)

        @pl.when(k == 0)
        def _():
            acc_ref[...] = jnp.zeros_like(acc_ref)

        acc_ref[...] += dot(a_ref, b_ref)

        @pl.when(k == nk - 1)
        def _():
            o_ref[...] = acc_ref[...].astype(o_ref.dtype)

    a_spec = pl.BlockSpec((bk, bm), lambda i, j, k: (k, i)) if mode == "tn" else pl.BlockSpec((bm, bk), lambda i, j, k: (i, k))
    if mode == "nt":
        b_spec = pl.BlockSpec((None, bn, bk), lambda i, j, k: (k // kps, j, k % kps)) if ns else pl.BlockSpec((bn, bk), lambda i, j, k: (j, k))
    elif mode == "nn" and ns:
        b_spec = pl.BlockSpec((None, bk, bn), lambda i, j, k: (j // nps, k, j % nps))
    else:
        b_spec = pl.BlockSpec((bk, bn), lambda i, j, k: (k, j))
    if mode == "tn" and ns:
        o_spec, o_shape = pl.BlockSpec((None, bm, bn), lambda i, j, k: (j // nps, i, j % nps)), (ns, M, n_unit)
    else:
        o_spec, o_shape = pl.BlockSpec((bm, bn), lambda i, j, k: (i, j)), (M, N)
    return pl.pallas_call(
        body_one if nk == 1 else body_acc, name=name, grid=(M // bm, N // bn, nk),
        in_specs=[a_spec, b_spec], out_specs=o_spec,
        out_shape=SDS(o_shape, out_dtype), scratch_shapes=[] if nk == 1 else [pltpu.VMEM((bm, bn), F32)],
        compiler_params=_cp(3),
    )(a, b)


class Row:
    def __init__(self, arr, bshape, imap, splits=None, diff=True, acc=False, gdtype=F32, gshape=None, gbshape=None, gimap=None,
                 lead=0):
        self.arr, self.bshape, self.imap = arr, tuple(bshape), imap
        self.splits, self.lead = splits, lead
        self.diff, self.acc, self.gdtype = diff, acc, gdtype
        self.gshape = tuple(arr.shape) if gshape is None else tuple(gshape)
        self.gbshape = self.bshape if gbshape is None else tuple(gbshape)
        self.gimap = imap if gimap is None else gimap

    def gspec(self):
        return pl.BlockSpec(self.gbshape, self.gimap)

    def spec(self):
        return pl.BlockSpec(self.bshape, self.imap)

    def pieces(self, ref):
        return _load_pieces(ref, self.splits, self.lead)

    def n_pieces(self):
        return _n_pieces(self.splits, self.lead)


class Out:
    def __init__(self, shape, dtype, bshape, imap, splits=None, lead=0):
        self.shape, self.dtype, self.bshape, self.imap = tuple(shape), dtype, tuple(bshape), imap
        self.splits, self.lead = splits, lead

    def n_pieces(self):
        return _n_pieces(self.splits, self.lead)


def _n_pieces(splits, lead):
    return lead if lead else (1 if splits is None else len(splits))


def _load_pieces(ref, splits, lead):
    if lead:
        return [ref[k].astype(F32) for k in range(lead)]
    if splits is None:
        return [ref[...].astype(F32)]
    out, o = [], 0
    for w in splits:
        out.append(ref[..., o:o + w].astype(F32))
        o += w
    return out


def _store_pieces(ref, splits, lead, vals, accumulate=False):
    def put(idx, v):
        if accumulate:
            ref[idx] += v.astype(ref.dtype)
        else:
            ref[idx] = v.astype(ref.dtype)

    if lead:
        for k in range(lead):
            put(k, vals[k])
    elif splits is None:
        put(..., vals[0])
    else:
        o = 0
        for w, v in zip(splits, vals):
            put((..., slice(o, o + w)), v)
            o += w


def rowwise(fn, rows, params, outs, grid, name):
    nr, npar = len(rows), len(params)

    def body(*refs):
        ids = tuple(pl.program_id(a) for a in range(len(grid)))
        vals = []
        for r, ref in zip(rows, refs[:nr]):
            vals += r.pieces(ref)
        pvals = [ref[...].astype(F32) for ref in refs[nr:nr + npar]]
        res = list(fn(ids, *vals, *pvals))
        o = 0
        for spec, ref in zip(outs, refs[nr + npar:]):
            n = spec.n_pieces()
            _store_pieces(ref, spec.splits, spec.lead, res[o:o + n])
            o += n

    nz = len(grid)
    pspecs = [pl.BlockSpec(p.shape, (lambda *ids, _n=p.ndim: (0,) * _n)) for p in params]
    res = pl.pallas_call(
        body, name=name, grid=grid,
        in_specs=[r.spec() for r in rows] + pspecs,
        out_specs=[pl.BlockSpec(o.bshape, o.imap) for o in outs],
        out_shape=[SDS(o.shape, o.dtype) for o in outs],
        compiler_params=_cp(nz),
    )(*[r.arr for r in rows], *params)
    return list(res)


def rowwise_bwd(fn, rows, params, cots, grid, name):
    nr, npar, nc = len(rows), len(params), len(cots)
    drows = [r for r in rows if r.diff]
    nz = len(grid)

    def body(*refs):
        ids = tuple(pl.program_id(a) for a in range(nz))
        row_refs, par_refs = refs[:nr], refs[nr:nr + npar]
        cot_refs = refs[nr + npar:nr + npar + nc]
        drow_refs = refs[nr + npar + nc:nr + npar + nc + len(drows)]
        dpar_refs = refs[nr + npar + nc + len(drows):]
        pieces, is_diff = [], []
        for r, ref in zip(rows, row_refs):
            ps = r.pieces(ref)
            pieces += ps
            is_diff += [r.diff] * len(ps)
        pvals = [ref[...].astype(F32) for ref in par_refs]
        dvals = [p for p, dflag in zip(pieces, is_diff) if dflag]
        nd = len(dvals)

        def f(*args):
            it = iter(args[:nd])
            full = [next(it) if dflag else p for p, dflag in zip(pieces, is_diff)]
            return tuple(fn(ids, *full, *args[nd:]))

        _, vjp = jax.vjp(f, *dvals, *pvals)
        cvals = []
        for c, ref in zip(cots, cot_refs):
            cvals += c.pieces(ref)
        g = vjp(tuple(cvals))
        o = 0
        first_inner = ids[-1] == 0
        for r, ref in zip(drows, drow_refs):
            n = r.n_pieces()
            gs = g[o:o + n]
            o += n
            if r.acc:
                @pl.when(first_inner)
                def _(ref=ref):
                    ref[...] = jnp.zeros_like(ref)
            _store_pieces(ref, r.splits, r.lead, gs, accumulate=r.acc)
        first = functools.reduce(jnp.logical_and, [i == 0 for i in ids])
        for ref, gp in zip(dpar_refs, g[nd:]):
            @pl.when(first)
            def _(ref=ref):
                ref[...] = jnp.zeros_like(ref)
            ref[...] += gp

    pspecs = [pl.BlockSpec(p.shape, (lambda *ids, _n=p.ndim: (0,) * _n)) for p in params]
    res = pl.pallas_call(
        body, name=name, grid=grid,
        in_specs=[r.spec() for r in rows] + pspecs + [c.spec() for c in cots],
        out_specs=[r.gspec() for r in drows] + pspecs,
        out_shape=[SDS(r.gshape, r.gdtype) for r in drows] + [SDS(p.shape, F32) for p in params],
        compiler_params=_cp(nz),
    )(*[r.arr for r in rows], *params, *[c.arr for c in cots])
    res = list(res)
    return res[:len(drows)], res[len(drows):]


def _sigmoid(x):
    return 0.5 * (jnp.tanh(0.5 * x) + 1.0)


def _silu(x):
    return x * _sigmoid(x)


def _normmod(x, gain, sc, sh):
    inv = lax.rsqrt(jnp.mean(x * x, axis=-1, keepdims=True) + EPS)
    return x * inv * gain * (1.0 + sc) + sh


def f_first(ids, x, gain, sc, sh):
    return x, _normmod(x, gain, sc, sh)


def f_resid_norm(ids, x, y, g, gain, sc, sh):
    xn = x + g * y
    return xn, _normmod(xn, gain, sc, sh)


@jax.custom_vjp
def _swiglu(gate, up):
    return _silu(gate) * up


def _swiglu_fwd(gate, up):
    return _silu(gate) * up, (gate, up)


def _swiglu_bwd(res, da):
    gate, up = res
    s = _sigmoid(gate)
    gs = gate * s
    return da * up * (s + gs * (1.0 - s)), da * gs


_swiglu.defvjp(_swiglu_fwd, _swiglu_bwd)


def loss_and_grad(x, y, tgt, g, S):
    nt = S // WT

    def body(x_ref, y_ref, t_ref, g_ref, part_ref, dx_ref, dy_ref, dg_ref):
        @pl.when(pl.program_id(0) == 0)
        def _():
            dg_ref[...] = jnp.zeros_like(dg_ref)

        yv = y_ref[...].astype(F32)
        gg = g_ref[...]
        e = x_ref[...] + gg * yv - t_ref[...]
        part_ref[...] = 0.5 * jnp.sum(e * e, axis=0, keepdims=True) * (1.0 / D)
        d = e * (1.0 / D)
        dx_ref[...] = d
        dy_ref[...] = (d * gg).astype(dy_ref.dtype)
        dg_ref[...] += jnp.sum(d * yv, axis=0, keepdims=True)

    row = pl.BlockSpec((WT, D), lambda i: (i, 0))
    vec = pl.BlockSpec((1, D), lambda i: (0, 0))
    return pl.pallas_call(
        body, name="loss_and_grad", grid=(nt,), in_specs=[row, row, row, vec],
        out_specs=[pl.BlockSpec((None, 1, D), lambda i: (i, 0, 0)), row, row, vec],
        out_shape=[SDS((nt, 1, D), F32), SDS((S, D), F32), SDS((S, D), BF16), SDS((1, D), F32)],
        compiler_params=_cp(1),
    )(x, y, tgt, g)


def _softplus(x):
    return jnp.maximum(x, 0.0) + jnp.log(1.0 + jnp.exp(-jnp.abs(x)))


def _chunk_tril(T):
    r = lax.broadcasted_iota(jnp.int32, (T, T), 0)
    c = lax.broadcasted_iota(jnp.int32, (T, T), 1)
    return jnp.where((r // GDN_C == c // GDN_C) & (c <= r), 1.0, 0.0).astype(F32)


def _dot_hi(a, b, dims=((1,), (0,))):
    return lax.dot_general(a, b, (dims, ((), ())), precision=lax.Precision.HIGHEST, preferred_element_type=F32)


def _dot_x3(a, b, dims=((1,), (0,))):
    return lax.dot_general(a, b, (dims, ((), ())), precision=lax.Precision.HIGH, preferred_element_type=F32)


def f_gdn_gates(ids, ab, alog, dtb):
    T = ab.shape[0]
    g = -jnp.exp(alog) * _softplus(ab + dtb)
    beta = _sigmoid(ab)
    gcum = _dot_x3(_chunk_tril(T), g)
    row = lax.broadcasted_iota(jnp.int32, (LANES, LANES), 0)
    sel = lambda k: jnp.where(row == k, 1.0, 0.0).astype(F32)
    gcs = [_dot_x3(gcum, sel(h)) for h in range(GDN_H)]
    bts = [_dot_x3(beta, sel(GDN_H + h)) for h in range(GDN_H)]
    return (*gcs, *bts)


def f_gdn_post(ids, *args):
    os_, zs, gain = args[:GDN_H], args[GDN_H:2 * GDN_H], args[2 * GDN_H]
    out = []
    for o, z in zip(os_, zs):
        inv = lax.rsqrt(jnp.mean(o * o, axis=-1, keepdims=True) + EPS)
        out.append(o * inv * gain * _silu(z))
    return tuple(out)


def _qknorm1(x, gain2, scale):
    lane = lax.broadcasted_iota(jnp.int32, x.shape, 1)
    lo = lane < DSW_DH
    x2 = x * x
    s_all = jnp.sum(x2, axis=-1, keepdims=True)
    s_lo = jnp.sum(jnp.where(lo, x2, 0.0), axis=-1, keepdims=True)
    ms = jnp.where(lo, s_lo, s_all - s_lo) * (1.0 / DSW_DH)
    return x * lax.rsqrt(ms + EPS) * (gain2 * scale)


def f_combine(ids, o0, o1, o2, l0, l1, l2):
    m = jnp.maximum(jnp.maximum(l0, l1), l2)
    e0, e1, e2 = jnp.exp(l0 - m), jnp.exp(l1 - m), jnp.exp(l2 - m)
    den = e0 + e1 + e2
    o = (e0 * o0 + e1 * o1 + e2 * o2) / den
    return o, m + jnp.log(den)


GDN_T = 512
HALO = 16


def _conv_pre(xx, w):
    acc = xx * w[3:4, :]
    for j in range(3):
        acc = acc + pltpu.roll(xx, shift=3 - j, axis=0) * w[j:j + 1, :]
    return acc


@jax.custom_vjp
def _qkv_act_core(pre, norm_on, scale):
    s = _silu(pre)
    r = lax.rsqrt(jnp.sum(s * s, axis=-1, keepdims=True) + EPS)
    return jnp.where(norm_on > 0.5, s * r * scale, s)


def _qkv_act_fwd(pre, norm_on, scale):
    return _qkv_act_core(pre, norm_on, scale), (pre, norm_on, scale)


def _qkv_act_bwd(res, dout):
    pre, norm_on, scale = res
    sig = _sigmoid(pre)
    s = pre * sig
    r = lax.rsqrt(jnp.sum(s * s, axis=-1, keepdims=True) + EPS)
    unit = s * r
    dn = dout * scale
    ds = jnp.where(norm_on > 0.5, r * (dn - unit * jnp.sum(dn * unit, axis=-1, keepdims=True)), dout)
    return ds * (sig + s * (1.0 - sig)), jnp.zeros_like(norm_on), jnp.zeros_like(scale)


_qkv_act_core.defvjp(_qkv_act_fwd, _qkv_act_bwd)


def _qkv_act(pre, cidx):
    norm_on = jnp.where(cidx < 2 * GDN_H, 1.0, 0.0).astype(F32)
    scale = jnp.where(cidx < GDN_H, GDN_DK ** -0.5, 1.0).astype(F32)
    return _qkv_act_core(pre, norm_on, scale)


def gdn_pre(proj, conv_w, S):
    nt = S // GDN_T
    hb = GDN_T // HALO

    def body(prev_ref, cur_ref, w_ref, o_ref):
        p, i = pl.program_id(0), pl.program_id(1)
        for h in range(GDN_H):
            cols = slice(LANES * h, LANES * (h + 1))
            prev = jnp.where(i > 0, prev_ref[:, cols].astype(F32), 0.0)
            xx = jnp.concatenate([prev, cur_ref[:, cols].astype(F32)], axis=0)
            pre = _conv_pre(xx, w_ref[:, cols])[HALO:]
            o_ref[h] = _qkv_act(pre, p * GDN_H + h).astype(o_ref.dtype)

    hv = GDN_H * LANES
    return pl.pallas_call(
        body, name="gdn_pre", grid=(3, nt),
        in_specs=[pl.BlockSpec((HALO, hv), lambda p, i: (jnp.maximum(i * hb - 1, 0), p)),
                  pl.BlockSpec((GDN_T, hv), lambda p, i: (i, p)),
                  pl.BlockSpec((4, hv), lambda p, i: (0, p))],
        out_specs=pl.BlockSpec((None, GDN_H, GDN_T, LANES), lambda p, i: (p, 0, i, 0)),
        out_shape=SDS((3, GDN_H, S, LANES), BF16),
        compiler_params=_cp(2),
    )(proj, proj, conv_w)


def gdn_pre_bwd(proj, conv_w, dqkv, S):
    nt = S // GDN_T
    hb = GDN_T // HALO
    last_h = S // HALO - 1

    def body(prev_ref, cur_ref, next_ref, w_ref, d_ref, dnext_ref, dx_ref, dw_ref):
        p, i = pl.program_id(0), pl.program_id(1)

        @pl.when(i == 0)
        def _():
            dw_ref[...] = jnp.zeros_like(dw_ref)

        for h in range(GDN_H):
            cols = slice(LANES * h, LANES * (h + 1))
            w = w_ref[:, cols]
            prev = jnp.where(i > 0, prev_ref[:, cols].astype(F32), 0.0)
            xx = jnp.concatenate([prev, cur_ref[:, cols].astype(F32), next_ref[:, cols].astype(F32)], axis=0)
            dnext = jnp.where(i < nt - 1, dnext_ref[h], 0.0)
            dd = jnp.concatenate([jnp.zeros((HALO, LANES), F32), d_ref[h], dnext], axis=0)
            pre = _conv_pre(xx, w)
            _, vjp = jax.vjp(lambda v, _c=p * GDN_H + h: _qkv_act(v, _c), pre)
            (dpre,) = vjp(dd)
            dx = dpre * w[3:4, :]
            R = dpre.shape[0]
            for j in range(3):
                dx = dx + pltpu.roll(dpre, shift=R - (3 - j), axis=0) * w[j:j + 1, :]
            dx_ref[:, cols] = dx[HALO:HALO + GDN_T].astype(dx_ref.dtype)
            own = HALO + GDN_T
            rows_w = [jnp.sum((dpre * pltpu.roll(xx, shift=3 - j, axis=0))[:own], axis=0, keepdims=True) for j in range(3)]
            rows_w.append(jnp.sum((dpre * xx)[:own], axis=0, keepdims=True))
            r4 = lax.broadcasted_iota(jnp.int32, (4, LANES), 0)
            dw = jnp.zeros((4, LANES), F32)
            for j in range(4):
                dw = dw + jnp.where(r4 == j, rows_w[j], 0.0)
            dw_ref[:, cols] += dw

    hv = GDN_H * LANES
    return pl.pallas_call(
        body, name="gdn_pre_bwd", grid=(3, nt),
        in_specs=[pl.BlockSpec((HALO, hv), lambda p, i: (jnp.maximum(i * hb - 1, 0), p)),
                  pl.BlockSpec((GDN_T, hv), lambda p, i: (i, p)),
                  pl.BlockSpec((HALO, hv), lambda p, i: (jnp.minimum((i + 1) * hb, last_h), p)),
                  pl.BlockSpec((4, hv), lambda p, i: (0, p)),
                  pl.BlockSpec((None, GDN_H, GDN_T, LANES), lambda p, i: (p, 0, i, 0)),
                  pl.BlockSpec((None, GDN_H, HALO, LANES), lambda p, i: (p, 0, jnp.minimum((i + 1) * hb, last_h), 0))],
        out_specs=[pl.BlockSpec((GDN_T, hv), lambda p, i: (i, p)),
                   pl.BlockSpec((4, hv), lambda p, i: (0, p))],
        out_shape=[SDS((S, 3 * hv), BF16), SDS((4, 3 * hv), F32)],
        compiler_params=_cp(2),
    )(proj, proj, proj, conv_w, dqkv, dqkv)


_DIMS = {"nn": ((1,), (0,)), "nt": ((1,), (1,)), "tn": ((0,), (0,))}


def _mm_raw(a, b, mode, hi):
    if hi:
        return _dot_hi(a, b, _DIMS[mode])
    return lax.dot_general(a.astype(BF16), b.astype(BF16), (_DIMS[mode], ((), ())), preferred_element_type=F32)


@functools.partial(jax.custom_vjp, nondiff_argnums=(2, 3))
def mm(a, b, mode, hi):
    return _mm_raw(a, b, mode, hi)


def _mm_fwd(a, b, mode, hi):
    return _mm_raw(a, b, mode, hi), (a, b)


def _mm_bwd(mode, hi, res, dc):
    a, b = res
    if mode == "nn":
        da, db = mm(dc, b, "nt", hi), mm(a, dc, "tn", hi)
    elif mode == "nt":
        da, db = mm(dc, b, "nn", hi), mm(dc, a, "tn", hi)
    else:
        da, db = mm(b, dc, "nt", hi), mm(a, dc, "nn", hi)
    return da, db


mm.defvjp(_mm_fwd, _mm_bwd)


TRI_BASE = 8


def _unit_lower_inverses(Ls):
    n = Ls[0].shape[0]
    r = lax.broadcasted_iota(jnp.int32, (n, n), 0)
    c = lax.broadcasted_iota(jnp.int32, (n, n), 1)
    eye = jnp.where(r == c, 1.0, 0.0).astype(F32)
    base = r // TRI_BASE == c // TRI_BASE
    one = lambda a, b_: _mm_raw(a, b_, "nn", False)
    Ps = [jnp.where(base, -L, 0.0) for L in Ls]
    invs = [eye + P for P in Ps]
    k = 1
    while 2 * k < TRI_BASE:
        Ps = [one(P, P) for P in Ps]
        invs = [inv + one(inv, P) for inv, P in zip(invs, Ps)]
        k *= 2
    b = 2 * TRI_BASE
    while b <= n:
        off_mask = (r // b == c // b) & ((r % b) >= b // 2) & ((c % b) < b // 2)
        ts = [one(inv, jnp.where(off_mask, L, 0.0)) for inv, L in zip(invs, Ls)]
        invs = [inv - one(t, inv) for inv, t in zip(invs, ts)]
        b *= 2
    resid = [eye - inv - _dot_x3(L, inv) for inv, L in zip(invs, Ls)]
    return [inv + _dot_x3(inv, rs) for inv, rs in zip(invs, resid)]


@jax.custom_vjp
def tri_apply(invs, Ls, r1s, r2s):
    return [_mm_raw(i, r, "nn", False) for i, r in zip(invs, r1s)], [_mm_raw(i, r, "nn", False) for i, r in zip(invs, r2s)]


def _tri_fwd(invs, Ls, r1s, r2s):
    s1s = [_mm_raw(i, r, "nn", False) for i, r in zip(invs, r1s)]
    s2s = [_mm_raw(i, r, "nn", False) for i, r in zip(invs, r2s)]
    return (s1s, s2s), (invs, s1s, s2s)


def _tri_bwd(res, ds):
    invs, s1s, s2s = res
    d1s = [_mm_raw(i, d, "tn", False) for i, d in zip(invs, ds[0])]
    d2s = [_mm_raw(i, d, "tn", False) for i, d in zip(invs, ds[1])]
    dLs = [-(_mm_raw(d1, s1, "nt", False) + _mm_raw(d2, s2, "nt", False)) for d1, s1, d2, s2 in zip(d1s, s1s, d2s, s2s)]
    return [jnp.zeros_like(i) for i in invs], dLs, d1s, d2s


tri_apply.defvjp(_tri_fwd, _tri_bwd)


def _gdn_chunk(qs, ks, vs, gcbs, btbs, Ss, invs=None):
    C = qs[0].shape[0]
    r = lax.broadcasted_iota(jnp.int32, (C, C), 0)
    c = lax.broadcasted_iota(jnp.int32, (C, C), 1)
    causal, strict = c <= r, c < r
    rows = lax.broadcasted_iota(jnp.int32, gcbs[0].shape, 0)
    Gs = [g[:, :C] for g in gcbs]
    decays = [jnp.exp(jnp.where(causal, G - G.T, NEG)) for G in Gs]
    kbs = [k * b for k, b in zip(ks, btbs)]
    vbs = [v * b for v, b in zip(vs, btbs)]
    Ls = [jnp.where(strict, mm(kb, k, "nt", False) * d, 0.0) for kb, k, d in zip(kbs, ks, decays)]
    egs = [jnp.exp(g) for g in gcbs]
    if invs is None:
        invs = _unit_lower_inverses(Ls)
    us, ws = tri_apply(invs, Ls, vbs, [kb * eg for kb, eg in zip(kbs, egs)])
    qks = [jnp.where(causal, mm(q, k, "nt", False) * d, 0.0) for q, k, d in zip(qs, ks, decays)]
    g_lasts = [jnp.sum(jnp.where(rows == C - 1, g, 0.0), axis=0, keepdims=True) for g in gcbs]
    q_decs = [q * eg for q, eg in zip(qs, egs)]
    k_decs = [k * jnp.exp(gl - g) for k, gl, g in zip(ks, g_lasts, gcbs)]
    v_news = [u - mm(w, S, "nn", False) for u, w, S in zip(us, ws, Ss)]
    os_ = [mm(qd, S, "nn", False) + mm(qk, vn, "nn", False) for qd, S, qk, vn in zip(q_decs, Ss, qks, v_news)]
    S_news = [S * jnp.exp(gl) + mm(kd, vn, "tn", False) for S, gl, kd, vn in zip(Ss, g_lasts, k_decs, v_news)]
    return os_, S_news, invs


INV_CHUNKS = 4
SCAN_CHUNKS = 2


def gdn_inverses(qkv, gc, bt, S):
    nchunk = S // GDN_C
    rows = INV_CHUNKS * GDN_C

    def body(k_ref, g_ref, b_ref, inv_ref):
        items = [(h, m) for m in range(INV_CHUNKS) for h in range(GDN_H)]
        r = lax.broadcasted_iota(jnp.int32, (GDN_C, GDN_C), 0)
        c = lax.broadcasted_iota(jnp.int32, (GDN_C, GDN_C), 1)
        sl = lambda m: slice(m * GDN_C, (m + 1) * GDN_C)
        ks = [k_ref[h, sl(m), :].astype(F32) for h, m in items]
        Gs = [g_ref[h, sl(m), :GDN_C] for h, m in items]
        kbs = [k * b_ref[h, sl(m), :] for k, (h, m) in zip(ks, items)]
        decays = [jnp.exp(jnp.where(c <= r, G - G.T, NEG)) for G in Gs]
        Ls = [jnp.where(c < r, _mm_raw(kb, k, "nt", False) * d, 0.0) for kb, k, d in zip(kbs, ks, decays)]
        for (h, m), inv in zip(items, _unit_lower_inverses(Ls)):
            inv_ref[h, m] = inv

    hb = pl.BlockSpec((GDN_H, rows, LANES), lambda n: (0, n, 0))
    return pl.pallas_call(
        body, name="gdn_inverses", grid=(nchunk // INV_CHUNKS,),
        in_specs=[pl.BlockSpec((None, GDN_H, rows, LANES), lambda n: (1, 0, n, 0)), hb, hb],
        out_specs=pl.BlockSpec((GDN_H, INV_CHUNKS, GDN_C, GDN_C), lambda n: (0, n, 0, 0)),
        out_shape=SDS((GDN_H, nchunk, GDN_C, GDN_C), F32),
        compiler_params=_cp(1),
    )(qkv, gc, bt)


def gdn_core(qkv, gc, bt, invs, S):
    nchunk = S // GDN_C

    def body(qkv_ref, g_ref, b_ref, inv_ref, o_ref, st_ref, s_scr):
        n = pl.program_id(0)

        @pl.when(n == 0)
        def _():
            s_scr[...] = jnp.zeros_like(s_scr)

        heads = range(GDN_H)
        S_cur = [s_scr[h] for h in heads]
        for m in range(SCAN_CHUNKS):
            sl = slice(m * GDN_C, (m + 1) * GDN_C)
            os_, S_new, _ = _gdn_chunk(*[[qkv_ref[p, h, sl, :].astype(F32) for h in heads] for p in range(3)],
                                       [g_ref[h, sl, :] for h in heads], [b_ref[h, sl, :] for h in heads], S_cur,
                                       invs=[inv_ref[h, m] for h in heads])
            for h in heads:
                st_ref[h, m] = S_cur[h].astype(st_ref.dtype)
                o_ref[h, sl, :] = os_[h]
            S_cur = S_new
        for h in heads:
            s_scr[h] = S_cur[h]

    rows = SCAN_CHUNKS * GDN_C
    blk3 = pl.BlockSpec((3, GDN_H, rows, LANES), lambda n: (0, 0, n, 0))
    hb = pl.BlockSpec((GDN_H, rows, LANES), lambda n: (0, n, 0))
    return pl.pallas_call(
        body, name="gdn_core", grid=(nchunk // SCAN_CHUNKS,),
        in_specs=[blk3, hb, hb, pl.BlockSpec((GDN_H, SCAN_CHUNKS, GDN_C, GDN_C), lambda n: (0, n, 0, 0))],
        out_specs=[hb, pl.BlockSpec((GDN_H, SCAN_CHUNKS, GDN_DK, LANES), lambda n: (0, n, 0, 0))],
        out_shape=[SDS((GDN_H, S, LANES), F32), SDS((GDN_H, nchunk, GDN_DK, LANES), BF16)],
        scratch_shapes=[pltpu.VMEM((GDN_H, GDN_DK, LANES), F32)],
        compiler_params=_cp(1),
    )(qkv, gc, bt, invs)


def gdn_core_bwd(qkv, gc, bt, states, invs, do, S):
    nchunk = S // GDN_C

    def body(qkv_ref, g_ref, b_ref, st_ref, inv_ref, do_ref, dqkv_ref, dg_ref, db_ref, ds_scr):
        n = pl.program_id(0)

        @pl.when(n == 0)
        def _():
            ds_scr[...] = jnp.zeros_like(ds_scr)

        heads = range(GDN_H)
        dS_cur = [ds_scr[h] for h in heads]
        for m in reversed(range(SCAN_CHUNKS)):
            sl = slice(m * GDN_C, (m + 1) * GDN_C)
            saved = [inv_ref[h, m] for h in heads]
            _, vjp = jax.vjp(lambda *a, _s=saved: _gdn_chunk(*a, invs=_s)[:2],
                             *[[qkv_ref[p, h, sl, :].astype(F32) for h in heads] for p in range(3)],
                             [g_ref[h, sl, :] for h in heads], [b_ref[h, sl, :] for h in heads],
                             [st_ref[h, m].astype(F32) for h in heads])
            dq, dk, dv, dg, db, dS_cur = vjp(([do_ref[h, sl, :] for h in heads], dS_cur))
            for h in heads:
                dqkv_ref[0, h, sl, :] = dq[h]
                dqkv_ref[1, h, sl, :] = dk[h]
                dqkv_ref[2, h, sl, :] = dv[h]
                dg_ref[h, sl, :] = dg[h]
                db_ref[h, sl, :] = db[h]
        for h in heads:
            ds_scr[h] = dS_cur[h]

    nblk = nchunk // SCAN_CHUNKS
    rows = SCAN_CHUNKS * GDN_C
    rev = lambda n: nblk - 1 - n
    blk3 = pl.BlockSpec((3, GDN_H, rows, LANES), lambda n: (0, 0, rev(n), 0))
    hb = pl.BlockSpec((GDN_H, rows, LANES), lambda n: (0, rev(n), 0))
    return pl.pallas_call(
        body, name="gdn_core_bwd", grid=(nblk,),
        in_specs=[blk3, hb, hb, pl.BlockSpec((GDN_H, SCAN_CHUNKS, GDN_DK, LANES), lambda n: (0, rev(n), 0, 0)),
                  pl.BlockSpec((GDN_H, SCAN_CHUNKS, GDN_C, GDN_C), lambda n: (0, rev(n), 0, 0)), hb],
        out_specs=[blk3, hb, hb],
        out_shape=[SDS((3, GDN_H, S, LANES), F32), SDS((GDN_H, S, LANES), F32), SDS((GDN_H, S, LANES), F32)],
        scratch_shapes=[pltpu.VMEM((GDN_H, GDN_DK, LANES), F32)],
        compiler_params=_cp(1),
    )(qkv, gc, bt, states, invs, do)


GDN_MAIN = 4 * GDN_H * LANES
GDN_PROJ = GDN_MAIN + LANES
RT = 256


def gdn_forward(h, w_in, conv_w, alog, dtb, out_gain, w_out):
    S = h.shape[0]
    nt = S // RT
    proj = matmul(h, w_in, "nn", BF16, "gdn_in")
    qkv = gdn_pre(proj, conv_w, S)
    ab_row = Row(proj, (RT, LANES), lambda i: (i, GDN_MAIN // LANES), gdtype=BF16, gshape=(S, LANES), gimap=lambda i: (i, 0))
    hm = lambda i: (0, i, 0)
    hv = GDN_H * LANES
    gc, bt = rowwise(f_gdn_gates, [ab_row], [alog, dtb],
                     [Out((GDN_H, S, LANES), F32, (GDN_H, RT, LANES), hm, lead=GDN_H)] * 2, (nt,), "gdn_gates")
    invs = gdn_inverses(qkv, gc, bt, S)
    o, states = gdn_core(qkv, gc, bt, invs, S)
    o_row = Row(o, (GDN_H, RT, LANES), hm, lead=GDN_H)
    z_row = Row(proj, (RT, hv), lambda i: (i, 3), splits=[LANES] * GDN_H, gdtype=BF16, gshape=(S, hv), gimap=lambda i: (i, 0))
    (on,) = rowwise(f_gdn_post, [o_row, z_row], [out_gain],
                    [Out((S, hv), BF16, (RT, hv), lambda i: (i, 0), splits=[LANES] * GDN_H)], (nt,), "gdn_post")
    y = matmul(on, w_out, "nn", BF16, "gdn_out")
    saved = dict(h=h, proj=proj, qkv=qkv, gc=gc, bt=bt, states=states, invs=invs, o=o, on=on, ab_row=ab_row, o_row=o_row, z_row=z_row)
    return y, saved


def gdn_backward(dy, sv, w_in, conv_w, alog, dtb, out_gain, w_out, on_weight_grads=None):
    S = dy.shape[0]
    nt = S // RT
    hm = lambda i: (0, i, 0)
    hv = GDN_H * LANES
    don = matmul(dy, w_out, "nt", BF16, "gdn_out_dx")
    d_w_out = matmul(sv["on"], dy, "tn", F32, "gdn_out_dw")
    (do, dz), (d_gain,) = rowwise_bwd(f_gdn_post, [sv["o_row"], sv["z_row"]], [out_gain],
                                      [Row(don, (RT, hv), lambda i: (i, 0), splits=[LANES] * GDN_H)], (nt,), "gdn_post_bwd")
    dqkv, dgc, dbt = gdn_core_bwd(sv["qkv"], sv["gc"], sv["bt"], sv["states"], sv["invs"], do, S)
    head_blk = lambda a: Row(a, (GDN_H, RT, LANES), hm, lead=GDN_H)
    (dab,), (d_alog, d_dtb) = rowwise_bwd(f_gdn_gates, [sv["ab_row"]], [alog, dtb], [head_blk(dgc), head_blk(dbt)],
                                          (nt,), "gdn_gates_bwd")
    dqkv_proj, d_conv = gdn_pre_bwd(sv["proj"], conv_w, dqkv, S)
    dproj = jnp.concatenate([dqkv_proj, dz, dab], axis=1)
    d_w_in = matmul(sv["h"], dproj, "tn", F32, "gdn_in_dw")
    if on_weight_grads is not None:
        w_in = w_in + on_weight_grads(d_w_in, d_w_out).astype(w_in.dtype)
    dh = matmul(dproj, w_in, "nt", BF16, "gdn_in_dx")
    return dh, dict(w_in=d_w_in, conv=d_conv, alog=d_alog, dtb=d_dtb, gain=d_gain, w_out=d_w_out)


QB = DSW_SPAN
N_HP = DSW_HG // LANES


def _bucket_maps():
    a = np.arange(QB)[:, None]
    j = np.arange(2 * QB)[None, :]
    dist = QB + a - j
    band = (dist >= 0) & (dist <= DSW_SPAN)
    maps = []
    for _, dil in DSW_GROUPS:
        dd = np.maximum(dist, 0) * dil
        max_exact = REL_BUCKETS // 2
        scaled = np.log(np.maximum(dd, 1).astype(np.float32) / np.float32(max_exact)) / np.float32(math.log(REL_MAX_DIST / max_exact))
        large = max_exact + (scaled * np.float32(REL_BUCKETS - max_exact)).astype(np.int32)
        large = np.minimum(large, REL_BUCKETS - 1)
        maps.append(np.where(dd < max_exact, dd, large).astype(np.int32))
    return np.stack(maps), band


def dsw_bias(rel_bias):
    maps, band = _bucket_maps()
    maps = np.where(band[None], maps, -1).astype(np.int32)

    def body(tab_ref, bk_ref, o_ref):
        gh = pl.program_id(0)
        bk = bk_ref[...]
        acc = jnp.full(bk.shape, NEG, F32)
        for b in range(REL_BUCKETS):
            acc = jnp.where(bk == b, tab_ref[b, gh], acc)
        o_ref[...] = acc

    return pl.pallas_call(
        body, name="dsw_bias", grid=(3 * GDN_H,),
        in_specs=[pl.BlockSpec(memory_space=pltpu.SMEM),
                  pl.BlockSpec((None, QB, 2 * QB), lambda gh: (gh // GDN_H, 0, 0))],
        out_specs=pl.BlockSpec((None, QB, 2 * QB), lambda gh: (gh, 0, 0)),
        out_shape=SDS((3 * GDN_H, QB, 2 * QB), F32),
        compiler_params=_cp(1),
    )(rel_bias, jnp.asarray(maps))


def dsw_bias_grad(dbias):
    maps, band = _bucket_maps()
    maps = np.where(band[None], maps, -1).astype(np.int32)

    def body(d_ref, bk_ref, o_ref):
        bk = bk_ref[...]
        d = d_ref[...]
        rows = lax.broadcasted_iota(jnp.int32, (REL_BUCKETS, LANES), 0)
        acc = jnp.zeros((REL_BUCKETS, LANES), F32)
        for b in range(REL_BUCKETS):
            part = jnp.sum(jnp.where(bk == b, d, 0.0), axis=0, keepdims=True)
            val = jnp.sum(part, axis=1, keepdims=True)
            acc = jnp.where(rows == b, val, acc)
        o_ref[...] = acc

    return pl.pallas_call(
        body, name="dsw_bias_grad", grid=(3 * GDN_H,),
        in_specs=[pl.BlockSpec((None, QB, 2 * QB), lambda gh: (gh, 0, 0)),
                  pl.BlockSpec((None, QB, 2 * QB), lambda gh: (gh // GDN_H, 0, 0))],
        out_specs=pl.BlockSpec((None, REL_BUCKETS, LANES), lambda gh: (gh, 0, 0)),
        out_shape=SDS((3 * GDN_H, REL_BUCKETS, LANES), F32),
        compiler_params=_cp(1),
    )(dbias, jnp.asarray(maps))


def _nt(a, b):
    return lax.dot_general(a, b, (((1,), (1,)), ((), ())), preferred_element_type=F32)


def _tn(a, b):
    return lax.dot_general(a, b, (((0,), (0,)), ((), ())), preferred_element_type=F32)


N_LB = DSW_HG // LANES
HALF = DSW_DH // 2


def _lanes(j):
    return slice(LANES * j, LANES * (j + 1))


def _deinterleave(stage, out_ref, dil, rows, dtype):
    for r in range(dil):
        for j in range(N_LB):
            out_ref[r, :, _lanes(j)] = stage[j, pl.ds(r, rows, stride=dil), :].astype(dtype)


def _interleave(in_ref, stage, dil, rows):
    for r in range(dil):
        for j in range(N_LB):
            stage[j, pl.ds(r, rows, stride=dil), :] = in_ref[r, :, _lanes(j)].astype(F32)


def dsw_prep(proj, q_gain2, k_gain2, gi, S):
    dil = DSW_GROUPS[gi][1]
    nt, rows = S // RT, RT // dil

    def body(q_ref, k_ref, v_ref, qg_ref, kg_ref, qo_ref, ko_ref, vo_ref, stage):
        for src, gain_ref, scale, dst in ((q_ref, qg_ref, DSW_DH ** -0.5, qo_ref), (k_ref, kg_ref, 1.0, ko_ref), (v_ref, None, None, vo_ref)):
            for j in range(N_LB):
                val = src[:, _lanes(j)].astype(F32)
                val = val if gain_ref is None else _qknorm1(val, gain_ref[...], scale)
                if dil == 1:
                    dst[0, :, _lanes(j)] = val.astype(BF16)
                else:
                    stage[j] = val
            if dil > 1:
                _deinterleave(stage, dst, dil, rows, BF16)

    col = lambda which: pl.BlockSpec((RT, DSW_HG), lambda i, _c=which * 3 + gi: (i, _c))
    gspec = pl.BlockSpec((1, LANES), lambda i: (0, 0))
    ospec = pl.BlockSpec((dil, rows, DSW_HG), lambda i: (0, i, 0))
    return pl.pallas_call(
        body, name=f"dsw_prep_g{gi}", grid=(nt,),
        in_specs=[col(0), col(1), col(2), gspec, gspec], out_specs=[ospec] * 3,
        out_shape=[SDS((dil, S // dil, DSW_HG), BF16)] * 3,
        scratch_shapes=[pltpu.VMEM((N_LB, RT, LANES), F32)], compiler_params=_cp(1),
    )(proj, proj, proj, q_gain2, k_gain2)


def dsw_prep_bwd(proj, q_gain2, k_gain2, dqd, dkd, dvd, gi, S):
    dil = DSW_GROUPS[gi][1]
    nt, rows = S // RT, RT // dil

    def body(q_ref, k_ref, qg_ref, kg_ref, dq_ref, dk_ref, dv_ref, oq_ref, ok_ref, ov_ref, dqg_ref, dkg_ref, stage):
        i = pl.program_id(0)

        @pl.when(i == 0)
        def _():
            dqg_ref[...] = jnp.zeros_like(dqg_ref)
            dkg_ref[...] = jnp.zeros_like(dkg_ref)

        for src, gain_ref, scale, cot_ref, dst, dg_ref in ((q_ref, qg_ref, DSW_DH ** -0.5, dq_ref, oq_ref, dqg_ref),
                                                          (k_ref, kg_ref, 1.0, dk_ref, ok_ref, dkg_ref)):
            if dil > 1:
                _interleave(cot_ref, stage, dil, rows)
            for j in range(N_LB):
                _, vjp = jax.vjp(lambda x, g, _s=scale: _qknorm1(x, g, _s), src[:, _lanes(j)].astype(F32), gain_ref[...])
                dx, dg = vjp(stage[j] if dil > 1 else cot_ref[0, :, _lanes(j)].astype(F32))
                dst[:, _lanes(j)] = dx.astype(dst.dtype)
                dg_ref[...] += dg
        if dil > 1:
            _interleave(dv_ref, stage, dil, rows)
        for j in range(N_LB):
            ov_ref[:, _lanes(j)] = (stage[j] if dil > 1 else dv_ref[0, :, _lanes(j)]).astype(ov_ref.dtype)

    col = lambda which: pl.BlockSpec((RT, DSW_HG), lambda i, _c=which * 3 + gi: (i, _c))
    gspec = pl.BlockSpec((1, LANES), lambda i: (0, 0))
    dspec = pl.BlockSpec((dil, rows, DSW_HG), lambda i: (0, i, 0))
    nspec = pl.BlockSpec((RT, DSW_HG), lambda i: (i, 0))
    return pl.pallas_call(
        body, name=f"dsw_prep_bwd_g{gi}", grid=(nt,),
        in_specs=[col(0), col(1), gspec, gspec, dspec, dspec, dspec], out_specs=[nspec] * 3 + [gspec] * 2,
        out_shape=[SDS((S, DSW_HG), BF16)] * 3 + [SDS((1, LANES), F32)] * 2,
        scratch_shapes=[pltpu.VMEM((N_LB, RT, LANES), F32)], compiler_params=_cp(1),
    )(proj, proj, q_gain2, k_gain2, dqd, dkd, dvd)


def _head_masks(rows):
    lane = lax.broadcasted_iota(jnp.int32, (rows, LANES), 1)
    return lane < DSW_DH, (lane % DSW_DH) < HALF


def dsw_attn_fwd(qd, kd, vd, bias, gi, S):
    dil = DSW_GROUPS[gi][1]
    sd = S // dil
    nq = sd // QB

    def body(q_ref, k_ref, v_ref, b_ref, o_ref, l_ref, kp_scr, vp_scr):
        i = pl.program_id(1)

        @pl.when(i == 0)
        def _():
            kp_scr[...] = jnp.zeros_like(kp_scr)
            vp_scr[...] = jnp.zeros_like(vp_scr)

        lo_q, _ = _head_masks(QB)
        lo_k, _ = _head_masks(2 * QB)
        col = lax.broadcasted_iota(jnp.int32, (QB, 2 * QB), 1)
        first = jnp.logical_and(i == 0, col < QB)
        hps, heads = range(N_HP), range(2 * N_HP)
        k2s = [jnp.concatenate([kp_scr[:, _lanes(hp)], k_ref[:, _lanes(hp)]], axis=0) for hp in hps]
        v2s = [jnp.concatenate([vp_scr[:, _lanes(hp)], v_ref[:, _lanes(hp)]], axis=0) for hp in hps]
        qs = [q_ref[:, _lanes(hp)] for hp in hps]
        k_now, v_now = k_ref[...], v_ref[...]
        mqs = [lo_q if h % 2 == 0 else jnp.logical_not(lo_q) for h in heads]
        mks = [lo_k if h % 2 == 0 else jnp.logical_not(lo_k) for h in heads]
        ss = [jnp.where(first, NEG, _nt(jnp.where(mqs[h], qs[h // 2], 0).astype(BF16), k2s[h // 2]) + b_ref[h]) for h in heads]
        mxs = [jnp.max(s, axis=1, keepdims=True) for s in ss]
        ps = [jnp.exp(s - mx) for s, mx in zip(ss, mxs)]
        ls = [jnp.sum(p, axis=1, keepdims=True) for p in ps]
        ohs = [jnp.dot(ps[h].astype(BF16), jnp.where(mks[h], v2s[h // 2], 0).astype(BF16), preferred_element_type=F32) / ls[h] for h in heads]
        lse_h = [mx + jnp.log(l) for mx, l in zip(mxs, ls)]
        for hp in hps:
            o_ref[:, _lanes(hp)] = ohs[2 * hp] + ohs[2 * hp + 1]
            l_ref[:, _lanes(hp)] = jnp.where(lo_q, lse_h[2 * hp], lse_h[2 * hp + 1])
        kp_scr[...] = k_now
        vp_scr[...] = v_now

    blk = pl.BlockSpec((None, QB, DSW_HG), lambda r, i: (r, i, 0))
    return pl.pallas_call(
        body, name=f"dsw_attn_g{gi}", grid=(dil, nq),
        in_specs=[blk, blk, blk, pl.BlockSpec((GDN_H, QB, 2 * QB), lambda r, i: (gi, 0, 0))],
        out_specs=[blk, blk], out_shape=[SDS((dil, sd, DSW_HG), F32)] * 2,
        scratch_shapes=[pltpu.VMEM((QB, DSW_HG), BF16)] * 2, compiler_params=_cp(2),
    )(qd, kd, vd, bias)


def dsw_attn_bwd(qd, kd, vd, bias, dod, statd, gi, S):
    dil = DSW_GROUPS[gi][1]
    sd = S // dil
    nq = sd // QB
    cur = lambda i: jnp.minimum(i, nq - 1)
    done = lambda i: jnp.maximum(i - 1, 0)

    def body(q_ref, k_ref, v_ref, b_ref, do_ref, st_ref, dq_ref, dk_ref, dv_ref, db_ref, kp_scr, vp_scr, dk_scr, dv_scr):
        r, i = pl.program_id(0), pl.program_id(1)

        @pl.when(jnp.logical_and(r == 0, i == 0))
        def _():
            db_ref[...] = jnp.zeros_like(db_ref)

        @pl.when(i == 0)
        def _():
            for scr in (kp_scr, vp_scr, dk_scr, dv_scr):
                scr[...] = jnp.zeros_like(scr)

        @pl.when(i < nq)
        def _():
            lo_q, first_half = _head_masks(QB)
            col = lax.broadcasted_iota(jnp.int32, (QB, 2 * QB), 1)
            first = jnp.logical_and(i == 0, col < QB)
            hps, heads = range(N_HP), range(2 * N_HP)
            k2s = [jnp.concatenate([kp_scr[:, _lanes(hp)], k_ref[:, _lanes(hp)]], axis=0) for hp in hps]
            v2s = [jnp.concatenate([vp_scr[:, _lanes(hp)], v_ref[:, _lanes(hp)]], axis=0) for hp in hps]
            qs = [q_ref[:, _lanes(hp)] for hp in hps]
            douts = [do_ref[:, _lanes(hp)] for hp in hps]
            stats = [st_ref[:, _lanes(hp)] for hp in hps]
            dkc = [dk_scr[:, _lanes(hp)] for hp in hps]
            dvc = [dv_scr[:, _lanes(hp)] for hp in hps]
            k_now, v_now = k_ref[...], v_ref[...]
            mqs = [lo_q if h % 2 == 0 else jnp.logical_not(lo_q) for h in heads]
            qms = [jnp.where(mqs[h], qs[h // 2], 0).astype(BF16) for h in heads]
            doms = [jnp.where(mqs[h], douts[h // 2], 0).astype(BF16) for h in heads]
            ss = [jnp.where(first, NEG, _nt(qms[h], k2s[h // 2]) + b_ref[h]) for h in heads]
            lses = [jnp.max(jnp.where(jnp.logical_and(mqs[h], first_half), stats[h // 2], NEG), axis=1, keepdims=True) for h in heads]
            deltas = [jnp.max(jnp.where(jnp.logical_and(mqs[h], jnp.logical_not(first_half)), stats[h // 2], NEG), axis=1, keepdims=True)
                      for h in heads]
            ps = [jnp.exp(ss[h] - lses[h]) for h in heads]
            dss = [ps[h] * (_nt(doms[h], v2s[h // 2]) - deltas[h]) for h in heads]
            dsbs = [d.astype(BF16) for d in dss]
            dqh = [jnp.where(mqs[h], jnp.dot(dsbs[h], k2s[h // 2], preferred_element_type=F32), 0.0) for h in heads]
            dkh = [_tn(dsbs[h], qms[h]) for h in heads]
            dvh = [_tn(ps[h].astype(BF16), doms[h]) for h in heads]
            for h in heads:
                db_ref[h] += dss[h]
            for hp in hps:
                dk2 = dkh[2 * hp] + dkh[2 * hp + 1]
                dv2 = dvh[2 * hp] + dvh[2 * hp + 1]
                dq_ref[:, _lanes(hp)] = dqh[2 * hp] + dqh[2 * hp + 1]
                dk_ref[:, _lanes(hp)] = dkc[hp] + dk2[:QB]
                dv_ref[:, _lanes(hp)] = (dvc[hp] + dv2[:QB]).astype(dv_ref.dtype)
                dk_scr[:, _lanes(hp)] = dk2[QB:]
                dv_scr[:, _lanes(hp)] = dv2[QB:]
            kp_scr[...] = k_now
            vp_scr[...] = v_now

        @pl.when(i == nq)
        def _():
            dk_ref[...] = dk_scr[...]
            dv_ref[...] = dv_scr[...].astype(dv_ref.dtype)

    blk = pl.BlockSpec((None, QB, DSW_HG), lambda r, i: (r, cur(i), 0))
    oblk = pl.BlockSpec((None, QB, DSW_HG), lambda r, i: (r, done(i), 0))
    return pl.pallas_call(
        body, name=f"dsw_attn_bwd_g{gi}", grid=(dil, nq + 1),
        in_specs=[blk, blk, blk, pl.BlockSpec((GDN_H, QB, 2 * QB), lambda r, i: (gi, 0, 0)), blk, blk],
        out_specs=[blk, oblk, oblk, pl.BlockSpec((GDN_H, QB, 2 * QB), lambda r, i: (0, 0, 0))],
        out_shape=[SDS((dil, sd, DSW_HG), F32), SDS((dil, sd, DSW_HG), F32), SDS((dil, sd, DSW_HG), BF16),
                   SDS((GDN_H, QB, 2 * QB), F32)],
        scratch_shapes=[pltpu.VMEM((QB, DSW_HG), BF16)] * 2 + [pltpu.VMEM((QB, DSW_HG), F32)] * 2,
        compiler_params=_cp(2),
    )(qd, kd, vd, bias, dod, statd)


def dsw_combine(ods, lseds, S):
    nt = S // RT
    dils = [d for _, d in DSW_GROUPS]

    def body(*refs):
        ins, (o_ref, l_ref), stages = refs[:6], refs[6:8], refs[8:]
        for g in range(3):
            if dils[g] > 1:
                _interleave(ins[g], stages[g], dils[g], RT // dils[g])
                _interleave(ins[3 + g], stages[3 + g], dils[g], RT // dils[g])
        for j in range(N_LB):
            natural = lambda a: stages[a][j] if dils[a % 3] > 1 else ins[a][0, :, _lanes(j)]
            o, lse = f_combine(None, *[natural(a) for a in range(6)])
            o_ref[:, _lanes(j)] = o.astype(o_ref.dtype)
            l_ref[:, _lanes(j)] = lse

    dspec = lambda d: pl.BlockSpec((d, RT // d, DSW_HG), lambda i: (0, i, 0))
    nspec = pl.BlockSpec((RT, DSW_HG), lambda i: (i, 0))
    return pl.pallas_call(
        body, name="dsw_combine", grid=(nt,),
        in_specs=[dspec(d) for d in dils] * 2, out_specs=[nspec, nspec],
        out_shape=[SDS((S, DSW_HG), BF16), SDS((S, DSW_HG), F32)],
        scratch_shapes=[pltpu.VMEM((N_LB, RT, LANES), F32)] * 6, compiler_params=_cp(1),
    )(*ods, *lseds)


def dsw_bwd_prep(do, o, lse, S):
    nt = S // RT
    dils = [d for _, d in DSW_GROUPS]

    def body(do_ref, o_ref, l_ref, *rest):
        outs, (st_do, st_stat) = rest[:6], rest[6:]
        lo, first_half = _head_masks(RT)
        for j in range(N_LB):
            dout = do_ref[:, _lanes(j)]
            prod = dout * o_ref[:, _lanes(j)].astype(F32)
            s_all = jnp.sum(prod, axis=1, keepdims=True)
            s_lo = jnp.sum(jnp.where(lo, prod, 0.0), axis=1, keepdims=True)
            delta = jnp.where(lo, s_lo, s_all - s_lo)
            stat = jnp.where(first_half, l_ref[:, _lanes(j)], delta)
            st_do[j] = dout
            st_stat[j] = stat
            for g in range(3):
                if dils[g] == 1:
                    outs[g][0, :, _lanes(j)] = dout.astype(BF16)
                    outs[3 + g][0, :, _lanes(j)] = stat
        for g in range(3):
            if dils[g] > 1:
                _deinterleave(st_do, outs[g], dils[g], RT // dils[g], BF16)
                _deinterleave(st_stat, outs[3 + g], dils[g], RT // dils[g], F32)

    nspec = pl.BlockSpec((RT, DSW_HG), lambda i: (i, 0))
    dspec = lambda d: pl.BlockSpec((d, RT // d, DSW_HG), lambda i: (0, i, 0))
    res = pl.pallas_call(
        body, name="dsw_bwd_prep", grid=(nt,),
        in_specs=[nspec] * 3, out_specs=[dspec(d) for d in dils] * 2,
        out_shape=[SDS((d, S // d, DSW_HG), BF16) for d in dils] + [SDS((d, S // d, DSW_HG), F32) for d in dils],
        scratch_shapes=[pltpu.VMEM((N_LB, RT, LANES), F32)] * 2, compiler_params=_cp(1),
    )(do, o, lse)
    return res[:3], res[3:]


def dsw_forward(h, w_in, q_gain2, k_gain2, rel_bias, w_out):
    S = h.shape[0]
    proj = matmul(h, w_in, "nn", BF16, "dsw_in", col_shards=N_SHARD)
    bias = dsw_bias(rel_bias)
    qkv, ods, lseds = [], [], []
    for gi in range(3):
        qd, kd, vd = dsw_prep(proj, q_gain2, k_gain2, gi, S)
        od, ld = dsw_attn_fwd(qd, kd, vd, bias, gi, S)
        qkv.append((qd, kd, vd))
        ods.append(od)
        lseds.append(ld)
    o, lse = dsw_combine(ods, lseds, S)
    y = matmul(o, w_out, "nn", BF16, "dsw_out", col_shards=N_SHARD)
    return y, dict(h=h, proj=proj, qkv=qkv, bias=bias, o=o, lse=lse)


def dsw_backward(dy, sv, w_in, q_gain2, k_gain2, w_out):
    S = dy.shape[0]
    do = matmul(dy, w_out, "nt", F32, "dsw_out_dx", col_shards=N_SHARD)
    d_w_out = matmul(sv["o"], dy, "tn", F32, "dsw_out_dw", col_shards=N_SHARD)
    dods, statds = dsw_bwd_prep(do, sv["o"], sv["lse"], S)
    pieces_q, pieces_k, pieces_v, dbs = [], [], [], []
    d_qg = jnp.zeros((1, LANES), F32)
    d_kg = jnp.zeros((1, LANES), F32)
    for gi in range(3):
        qd, kd, vd = sv["qkv"][gi]
        dqd, dkd, dvd, db = dsw_attn_bwd(qd, kd, vd, sv["bias"], dods[gi], statds[gi], gi, S)
        dq, dk, dv, dqg, dkg = dsw_prep_bwd(sv["proj"], q_gain2, k_gain2, dqd, dkd, dvd, gi, S)
        dbs.append(db)
        pieces_q.append(dq)
        pieces_k.append(dk)
        pieces_v.append(dv)
        d_qg = d_qg + dqg
        d_kg = d_kg + dkg
    dproj = jnp.concatenate(pieces_q + pieces_k + pieces_v, axis=1)
    d_w_in = matmul(sv["h"], dproj, "tn", F32, "dsw_in_dw", col_shards=N_SHARD)
    dh = matmul(dproj, w_in, "nt", BF16, "dsw_in_dx", col_shards=N_SHARD)
    d_rel = dsw_bias_grad(jnp.concatenate(dbs, axis=0))
    return dh, dict(w_in=d_w_in, q_gain2=d_qg, k_gain2=d_kg, rel=d_rel, w_out=d_w_out)


FUSE_M = 512


def ffn_in_act(h, w_in, name):
    S = h.shape[0]
    half = FFN // 2

    def body(h_ref, wg_ref, wu_ref, gu_ref, a_ref):
        j = pl.program_id(1)
        sub = FUSE_M // 2
        for part in range(2):
            rows = slice(part * sub, (part + 1) * sub)
            hb = h_ref[rows, :]
            g = jnp.dot(hb, wg_ref[...], preferred_element_type=F32)
            u = jnp.dot(hb, wu_ref[...], preferred_element_type=F32)
            a_ref[rows, :] = (_silu(g) * u).astype(a_ref.dtype)
            for jj in range(2):
                @pl.when(j == jj)
                def _(g=g, u=u, jj=jj, rows=rows):
                    gu_ref[rows, jj * half:(jj + 1) * half] = g.astype(gu_ref.dtype)
                    gu_ref[rows, FFN + jj * half:FFN + (jj + 1) * half] = u.astype(gu_ref.dtype)

    return pl.pallas_call(
        body, name=name, grid=(S // FUSE_M, 2),
        in_specs=[pl.BlockSpec((FUSE_M, D), lambda i, j: (i, 0)),
                  pl.BlockSpec((None, D, half), lambda i, j: (j, 0, 0)),
                  pl.BlockSpec((None, D, half), lambda i, j: (j + 2, 0, 0))],
        out_specs=[pl.BlockSpec((FUSE_M, 2 * FFN), lambda i, j: (i, 0)), pl.BlockSpec((FUSE_M, half), lambda i, j: (i, j))],
        out_shape=[SDS((S, 2 * FFN), BF16), SDS((S, FFN), BF16)],
        compiler_params=_cp(2),
    )(h, w_in, w_in)


def ffn_forward(h, w_in, w_out, tag):
    gu, a = ffn_in_act(h, w_in, f"ffn_in_act_{tag}")
    f = matmul(a, w_out, "nn", BF16, f"ffn_out_{tag}")
    return f, dict(h=h, gu=gu, a=a)


def ffn_out_dx_act(df, w_out, gu, name):
    S = df.shape[0]
    half = FFN // 2

    def body(df_ref, w_ref, g_ref, u_ref, dgu_ref):
        j = pl.program_id(1)
        sub = FUSE_M // 2
        for part in range(2):
            rows = slice(part * sub, (part + 1) * sub)
            da = _nt(df_ref[rows, :], w_ref[...])
            dg, du = _swiglu_bwd((g_ref[rows, :].astype(F32), u_ref[rows, :].astype(F32)), da)
            for jj in range(2):
                @pl.when(j == jj)
                def _(dg=dg, du=du, jj=jj, rows=rows):
                    dgu_ref[rows, jj * half:(jj + 1) * half] = dg.astype(dgu_ref.dtype)
                    dgu_ref[rows, FFN + jj * half:FFN + (jj + 1) * half] = du.astype(dgu_ref.dtype)

    return pl.pallas_call(
        body, name=name, grid=(S // FUSE_M, 2),
        in_specs=[pl.BlockSpec((FUSE_M, D), lambda i, j: (i, 0)),
                  pl.BlockSpec((half, D), lambda i, j: (j, 0)),
                  pl.BlockSpec((FUSE_M, half), lambda i, j: (i, j)),
                  pl.BlockSpec((FUSE_M, half), lambda i, j: (i, j + 2))],
        out_specs=pl.BlockSpec((FUSE_M, 2 * FFN), lambda i, j: (i, 0)),
        out_shape=SDS((S, 2 * FFN), BF16),
        compiler_params=_cp(2),
    )(df, w_out, gu, gu)


def ffn_backward(df, sv, w_in, w_out, tag):
    d_w_out = matmul(sv["a"], df, "tn", F32, f"ffn_out_dw_{tag}")
    dgu = ffn_out_dx_act(df, w_out, sv["gu"], f"ffn_out_dx_act_{tag}")
    d_w_in = matmul(sv["h"], dgu, "tn", F32, f"ffn_in_dw_{tag}", col_shards=N_SHARD)
    dh = matmul(dgu, w_in, "nt", BF16, f"ffn_in_dx_{tag}", col_shards=N_SHARD)
    return dh, d_w_in, d_w_out


def f_norm_only(ids, x, gain, sc, sh):
    return (_normmod(x, gain, sc, sh),)


WT = 512


def _wide(a, **kw):
    return Row(a, (WT, D), lambda i: (i, 0), **kw)


def _wide_out(S, dtype):
    return Out((S, D), dtype, (WT, D), lambda i: (i, 0))


def adamw(w, g, m, v, name):
    shape = w.shape
    C = shape[-1]
    R = int(np.prod(shape[:-1]))
    w2, g2, m2, v2 = (a.reshape(R, C) for a in (w, g, m, v))
    br = R
    if R > 256:
        br = max(b for b in range(8, 257, 8) if R % b == 0)
    c1 = 1.0 / (1.0 - ADAM_B1 ** ADAM_STEP)
    c2 = 1.0 / (1.0 - ADAM_B2 ** ADAM_STEP)

    def body(w_ref, g_ref, m_ref, v_ref, d_ref, nm_ref, nv_ref):
        gg = g_ref[...]
        mm_ = ADAM_B1 * m_ref[...] + (1.0 - ADAM_B1) * gg
        vv = ADAM_B2 * v_ref[...] + (1.0 - ADAM_B2) * (gg * gg)
        d_ref[...] = -ADAM_LR * ((mm_ * c1) / (jnp.sqrt(vv * c2) + ADAM_EPS) + ADAM_WD * w_ref[...])
        nm_ref[...] = mm_
        nv_ref[...] = vv

    spec = pl.BlockSpec((br, C), lambda i: (i, 0))
    d, nm, nv = pl.pallas_call(
        body, name=name, grid=(R // br,), in_specs=[spec] * 4, out_specs=[spec] * 3,
        out_shape=[SDS((R, C), F32)] * 3, compiler_params=_cp(1),
    )(w2, g2, m2, v2)
    return d.reshape(shape), nm.reshape(shape), nv.reshape(shape)


def _place():
    x, y, c = lax.axis_index("x"), lax.axis_index("y"), lax.axis_index("c")
    chips = [(1 - x, y), (x, 1 - y), (1 - x, 1 - y)]
    return x, y, c, chips


def all_gather_small(blk, name):
    m_per, n = blk.shape

    def body(x_ref, out_ref, send_sems, recv_sems, local_sem):
        x, y, c, chips = _place()
        me, sibling = (x, y, c), (x, y, 1 - c)

        def rows(px, py, pc):
            return out_ref.at[pl.ds((4 * px + 2 * py + pc) * m_per, m_per), :]

        def copy(k, block, to, src=None):
            return pltpu.make_async_remote_copy(
                src_ref=rows(*block) if src is None else src, dst_ref=rows(*block),
                send_sem=send_sems.at[k], recv_sem=recv_sems.at[k], device_id=to, device_id_type=MESH)

        mine = pltpu.make_async_copy(x_ref, rows(*me), local_sem)
        mine.start()
        first = [copy(0, me, sibling, src=x_ref)]
        first += [copy(1 + j, me, (*chip, c), src=x_ref) for j, chip in enumerate(chips)]
        for cp in first:
            cp.start()
        passed = [copy(4 + j, (*chip, c), sibling) for j, chip in enumerate(chips)]
        for j, chip in enumerate(chips):
            copy(1 + j, (*chip, c), me).wait_recv()
            passed[j].start()
        copy(0, sibling, me).wait_recv()
        for j, chip in enumerate(chips):
            copy(4 + j, (*chip, 1 - c), me).wait_recv()
        for cp in first + passed:
            cp.wait_send()
        mine.wait()

    return pl.pallas_call(
        body, name=name, out_shape=SDS((N_DEV * m_per, n), blk.dtype),
        in_specs=[pl.BlockSpec(memory_space=pltpu.VMEM)], out_specs=pl.BlockSpec(memory_space=pltpu.VMEM),
        scratch_shapes=[pltpu.SemaphoreType.DMA((7,)), pltpu.SemaphoreType.DMA((7,)), pltpu.SemaphoreType.DMA],
    )(blk)


def _half(cc, rh):
    return pl.ds(pl.multiple_of(cc * rh, 16), rh)


def all_gather_shards(ws):
    n = len(ws)

    def body(*refs):
        w_refs, out_refs = refs[:n], refs[n:2 * n]
        send_sems, recv_sems, local_sems, own_sems = refs[2 * n:]
        x, y, c, chips = _place()
        sibling = (x, y, 1 - c)
        s_me = 2 * x + y

        def copy(k, src, dst, to):
            return pltpu.make_async_remote_copy(src_ref=src, dst_ref=dst, send_sem=send_sems.at[k], recv_sem=recv_sems.at[k],
                                                device_id=to, device_id_type=MESH)

        local, sends, passed = [], [], []
        for k in range(n):
            rh = ws[k].shape[0] // 2
            cp = pltpu.make_async_remote_copy(src_ref=w_refs[k], dst_ref=out_refs[k].at[s_me], send_sem=local_sems.at[k],
                                              recv_sem=own_sems.at[k], device_id=sibling, device_id_type=MESH)
            cp.start()
            local.append(cp)
            for j, chip in enumerate(chips):
                sd = copy(6 * k + j, w_refs[k].at[_half(c, rh)], out_refs[k].at[s_me, _half(c, rh)], (*chip, c))
                sd.start()
                sends.append(sd)
        for k in range(n):
            rh = ws[k].shape[0] // 2
            for j, (px, py) in enumerate(chips):
                got = out_refs[k].at[2 * px + py, _half(c, rh)]
                copy(6 * k + j, got, got, (px, py, c)).wait_recv()
                fw = copy(6 * k + 3 + j, got, got, sibling)
                fw.start()
                passed.append(fw)
        for k in range(n):
            rh = ws[k].shape[0] // 2
            for j, (px, py) in enumerate(chips):
                got = out_refs[k].at[2 * px + py, _half(1 - c, rh)]
                copy(6 * k + 3 + j, got, got, sibling).wait_recv()
        for cp in sends + passed:
            cp.wait_send()
        for cp in local:
            cp.wait()

    return pl.pallas_call(
        body, name="weights_all_gather", out_shape=[SDS((N_SHARD,) + w.shape, w.dtype) for w in ws],
        in_specs=[ANY] * n, out_specs=[ANY] * n,
        scratch_shapes=[pltpu.SemaphoreType.DMA((6 * n,)), pltpu.SemaphoreType.DMA((6 * n,)), pltpu.SemaphoreType.DMA((n,)),
                        pltpu.SemaphoreType.DMA((n,))],
    )(*ws)


def sibling_exchange(sends, name):
    n = len(sends)

    def body(*refs):
        s_refs, o_refs, send_sems, recv_sems = refs[:n], refs[n:2 * n], refs[2 * n], refs[2 * n + 1]
        x, y, c, _ = _place()
        cps = [pltpu.make_async_remote_copy(src_ref=s_refs[k], dst_ref=o_refs[k], send_sem=send_sems.at[k], recv_sem=recv_sems.at[k],
                                            device_id=(x, y, 1 - c), device_id_type=MESH) for k in range(n)]
        for cp in cps:
            cp.start()
        for cp in cps:
            cp.wait()

    return pl.pallas_call(
        body, name=name, out_shape=[SDS(s.shape, s.dtype) for s in sends], in_specs=[ANY] * n, out_specs=[ANY] * n,
        scratch_shapes=[pltpu.SemaphoreType.DMA((n,)), pltpu.SemaphoreType.DMA((n,))],
    )(*sends)


def add_rows(arrs, out_dtype, name, rt=256):
    Rr, W = arrs[0].shape

    def fn(ids, *vals):
        acc = vals[0]
        for v in vals[1:]:
            acc = acc + v
        return (acc,)

    t = rt if Rr % rt == 0 else max(b for b in range(16, rt + 1, 16) if Rr % b == 0)
    (out,) = rowwise(fn, [Row(a, (t, W), lambda i: (i, 0)) for a in arrs], [],
                     [Out((Rr, W), out_dtype, (t, W), lambda i: (i, 0))], (Rr // t,), name)
    return out


HBM_SPEC = pl.BlockSpec(memory_space=pltpu.HBM)
SEM_SPEC = pl.BlockSpec(memory_space=pltpu.SEMAPHORE)
DATAFLOW = pltpu.SideEffectType.DATAFLOW_SIDE_EFFECTING


def _in_hbm(a):
    return pltpu.with_memory_space_constraint(a, pltpu.HBM)


def _gather_copies(w_refs, land_refs, send_sems, recv_sems):
    x, y, c, chips = _place()
    targets = [(x, y, 1 - c)] + [(*chip, c) for chip in chips]
    cps = []
    for k, (w_ref, land_ref) in enumerate(zip(w_refs, land_refs)):
        for j, to in enumerate(targets):
            cps.append(pltpu.make_async_remote_copy(src_ref=w_ref, dst_ref=land_ref.at[2 * x + y], send_sem=send_sems.at[4 * k + j],
                                                    recv_sem=recv_sems.at[4 * k + j], device_id=to, device_id_type=MESH))
    return cps


def _scatter_copies(p_refs, land_refs, send_sems, recv_sems):
    x, y, c, chips = _place()
    cps = []
    for k, (p_ref, land_ref) in enumerate(zip(p_refs, land_refs)):
        for j, (px, py) in enumerate(chips):
            cps.append(pltpu.make_async_remote_copy(src_ref=p_ref.at[2 * px + py], dst_ref=land_ref.at[j], send_sem=send_sems.at[3 * k + j],
                                                    recv_sem=recv_sems.at[3 * k + j], device_id=(px, py, c), device_id_type=MESH))
    return cps


def copies_start(srcs, land_shapes, make_copies, per_src, name):
    n = len(srcs)
    m = per_src * n

    def body(*refs):
        src_refs, land_refs = refs[:n], refs[n:2 * n]
        send_sems, recv_sems, token = refs[2 * n], refs[2 * n + 1], refs[-1]
        for cp in make_copies(src_refs, land_refs, send_sems, recv_sems):
            cp.start()
        token[...] = jnp.zeros_like(token)

    lands = [lax.empty(shp, s.dtype) for shp, s in zip(land_shapes, srcs)]
    res = pl.pallas_call(
        body, name=name,
        out_shape=(pltpu.SemaphoreType.DMA((m,)), pltpu.SemaphoreType.DMA((m,)), *[pltpu.HBM(s.shape, s.dtype) for s in srcs],
                   *[pltpu.HBM(shp, s.dtype) for shp, s in zip(land_shapes, srcs)], SDS((8, LANES), F32)),
        in_specs=[HBM_SPEC] * (2 * n),
        out_specs=(SEM_SPEC, SEM_SPEC, *[HBM_SPEC] * (2 * n), pl.BlockSpec(memory_space=pltpu.VMEM)),
        input_output_aliases={i: 2 + i for i in range(2 * n)},
        compiler_params=pltpu.CompilerParams(has_side_effects=DATAFLOW),
    )(*[_in_hbm(s) for s in srcs], *[_in_hbm(l) for l in lands])
    return res[0], res[1], list(res[2:2 + n]), list(res[2 + n:2 + 2 * n]), res[-1]


def copies_wait(send_sems, recv_sems, srcs, lands, make_copies, after, name):
    n = len(srcs)

    def body(*refs):
        src_refs, land_refs = refs[:n], refs[n:2 * n]
        for cp in make_copies(src_refs, land_refs, refs[2 * n], refs[2 * n + 1]):
            cp.wait_send()
            cp.wait_recv()

    res = pl.pallas_call(
        body, name=name,
        out_shape=(*[pltpu.HBM(s.shape, s.dtype) for s in srcs], *[pltpu.HBM(l.shape, l.dtype) for l in lands]),
        in_specs=[HBM_SPEC] * (2 * n) + [SEM_SPEC, SEM_SPEC, ANY],
        out_specs=tuple([HBM_SPEC] * (2 * n)),
        input_output_aliases={i: i for i in range(2 * n)},
        compiler_params=pltpu.CompilerParams(has_side_effects=DATAFLOW),
    )(*srcs, *lands, send_sems, recv_sems, after)
    return list(res[n:])


def _pad_lanes(v):
    return jnp.concatenate([v.astype(F32), jnp.zeros((LANES - v.shape[0],), F32)])[None]


def kernel(x, c, w_ada, b_ada, norm_mix, norm_ffn, w_ffn_in, w_ffn_out, gdn_w_in, gdn_conv, gdn_a_log, gdn_dt_bias, gdn_out_norm, gdn_w_out, dsw_w_in, dsw_q_norm, dsw_k_norm, dsw_w_out, rel_bias, loss_target, m_w_ada, m_b_ada, m_norm_mix, m_norm_ffn, m_w_ffn_in, m_w_ffn_out, m_gdn_w_in, m_gdn_conv, m_gdn_a_log, m_gdn_dt_bias, m_gdn_out_norm, m_gdn_w_out, m_dsw_w_in, m_dsw_q_norm, m_dsw_k_norm, m_dsw_w_out, m_rel_bias, v_w_ada, v_b_ada, v_norm_mix, v_norm_ffn, v_w_ffn_in, v_w_ffn_out, v_gdn_w_in, v_gdn_conv, v_gdn_a_log, v_gdn_dt_bias, v_gdn_out_norm, v_gdn_w_out, v_dsw_w_in, v_dsw_q_norm, v_dsw_k_norm, v_dsw_w_out, v_rel_bias):
    S = x.shape[1]
    nt = S // WT
    xi, yi, ci = lax.axis_index("x"), lax.axis_index("y"), lax.axis_index("c")
    me = 4 * xi + 2 * yi + ci
    s_me = 2 * xi + yi
    x0, tgt = x[0], loss_target[0]

    whole = lambda a: Row(a, a.shape, lambda i: (0,) * a.ndim)
    (cond8,) = rowwise(lambda ids, v: (_silu(v),), [whole(c.reshape(8, LANES))], [], [Out((8, LANES), F32, (8, LANES), lambda i: (0, 0))], (1,), "cond")
    cond_all = all_gather_small(cond8, "gather_cond").reshape(N_DEV, D)
    cond16 = jnp.concatenate([cond_all, jnp.zeros((8, D), F32)], axis=0)
    ada_cols = w_ada.shape[2]
    mods = [matmul(cond16, w_ada[l], "nn", F32, f"ada_{l}")[:N_DEV] for l in range(2)]
    buf = jnp.concatenate([jnp.stack(mods, axis=1).reshape(-1, LANES), gdn_conv.reshape(-1, LANES)], axis=0)
    n_mod_rows = N_DEV * 2 * ada_cols // LANES
    got = all_gather_small(buf, "gather_mod").reshape(N_DEV, buf.shape[0], LANES)
    mod_parts, conv_parts = [], []
    for s in range(N_SHARD):
        from_dev = got[2 * s]
        mod_parts.append(lax.dynamic_index_in_dim(from_dev[:n_mod_rows].reshape(N_DEV, 2, ada_cols), me, 0, keepdims=False))
        conv_parts.append(from_dev[n_mod_rows:].reshape(4, -1))
    mod_nb = jnp.concatenate(mod_parts, axis=1)
    conv_w = jnp.concatenate(conv_parts, axis=1)
    (mod,) = rowwise(lambda ids, a, b: (a + b,), [whole(mod_nb), whole(b_ada)], [], [Out(mod_nb.shape, F32, mod_nb.shape, lambda i: (0, 0))], (1,), "mod_bias")
    mod = mod.reshape(2, 6, 1, D)
    sh1, sc1, g1, sh2, sc2, g2 = ([mod[l, k] for l in range(2)] for k in range(6))
    gmix = [norm_mix[l][None] for l in range(2)]
    gffn = [norm_ffn[l][None] for l in range(2)]

    gcols = gdn_w_in.shape[2]
    g_gdn_in, g_gdn_out = all_gather_shards([gdn_w_in[0].astype(BF16), gdn_w_out[0].astype(BF16)])
    gathered = lambda ws: [(N_SHARD,) + w.shape for w in ws]
    gate = (jnp.minimum(jnp.abs(g_gdn_in[0, 0, 0].astype(F32)), 0.0) + jnp.minimum(jnp.abs(mod[0, 0, 0, 0]), 0.0)).astype(BF16)
    w2 = [w_ffn_in[0].astype(BF16) + gate, w_ffn_out[0].astype(BF16) + gate]
    w3 = [dsw_w_in[0].astype(BF16) + gate, dsw_w_out[0].astype(BF16) + gate, w_ffn_in[1].astype(BF16) + gate, w_ffn_out[1].astype(BF16) + gate]
    fly2 = copies_start(w2, gathered(w2), _gather_copies, 4, "weights_ffn0_start")
    fly3 = copies_start(w3, gathered(w3), _gather_copies, 4, "weights_layer1_start")
    started = fly2[4][0, 0] + fly3[4][0, 0]
    w_gdn = jnp.concatenate([g_gdn_in[s] for s in range(N_SHARD)] + [jnp.zeros((D, GDN_PROJ - N_SHARD * gcols), BF16)], axis=1)
    alog, dtb = _pad_lanes(gdn_a_log[0]), _pad_lanes(gdn_dt_bias[0])
    qg2 = jnp.concatenate([dsw_q_norm, dsw_q_norm], axis=1)
    kg2 = jnp.concatenate([dsw_k_norm, dsw_k_norm], axis=1)
    w_gdn_out = g_gdn_out.reshape(GDN_H * LANES, D)
    gdn_args = (w_gdn, conv_w, alog, dtb, gdn_out_norm, w_gdn_out)
    sc1[0] = sc1[0] + started

    (h10,) = rowwise(f_norm_only, [_wide(x0)], [gmix[0], sc1[0], sh1[0]], [_wide_out(S, BF16)], (nt,), "l0_norm")
    y0, sv_g = gdn_forward(h10, *gdn_args)
    x1, h20 = rowwise(f_resid_norm, [_wide(x0), _wide(y0)], [g1[0], gffn[0], sc2[0], sh2[0]], [_wide_out(S, F32), _wide_out(S, BF16)], (nt,), "l0_mid")
    g_in0, g_out0 = copies_wait(*fly2[:4], _gather_copies, y0, "weights_ffn0_wait")
    w_ffn = [(g_in0, g_out0.reshape(FFN, D)), None]
    f0, sv_f0 = ffn_forward(h20, *w_ffn[0], "0")
    x2, h11 = rowwise(f_resid_norm, [_wide(x1), _wide(f0)], [g2[0], gmix[1], sc1[1], sh1[1]], [_wide_out(S, F32), _wide_out(S, BF16)], (nt,), "l1_in")
    g_dsw_in, g_dsw_out, g_in1, g_out1 = copies_wait(*fly3[:4], _gather_copies, f0, "weights_layer1_wait")
    w_ffn[1] = (g_in1, g_out1.reshape(FFN, D))
    dsw_args = (g_dsw_in, qg2, kg2)
    y1, sv_d = dsw_forward(h11, *dsw_args, rel_bias, g_dsw_out)
    x3, h21 = rowwise(f_resid_norm, [_wide(x2), _wide(y1)], [g1[1], gffn[1], sc2[1], sh2[1]], [_wide_out(S, F32), _wide_out(S, BF16)], (nt,), "l1_mid")
    f1, sv_f1 = ffn_forward(h21, *w_ffn[1], "1")
    parts, dx3, df1, dg2_1 = loss_and_grad(x3, f1, tgt, g2[1], S)
    loss = lax.psum(jnp.sum(parts), ("x", "y", "c"))

    dh21, d_win1, d_wout1 = ffn_backward(df1, sv_f1, *w_ffn[1], "1")
    (dx2, dy1), (dg1_1, dgf1, dsc2_1, dsh2_1) = rowwise_bwd(
        f_resid_norm, [_wide(x2), _wide(y1, gdtype=BF16)], [g1[1], gffn[1], sc2[1], sh2[1]], [_wide(dx3), _wide(dh21)], (nt,), "l1_mid_bwd")
    dh11, g_d = dsw_backward(dy1, sv_d, *dsw_args, g_dsw_out)
    (dx1, df0), (dg2_0, dgm1, dsc1_1, dsh1_1) = rowwise_bwd(
        f_resid_norm, [_wide(x1), _wide(f0, gdtype=BF16)], [g2[0], gmix[1], sc1[1], sh1[1]], [_wide(dx2), _wide(dh11)], (nt,), "l1_in_bwd")
    by_shard = lambda a: a.reshape(N_SHARD, a.shape[0] // N_SHARD, a.shape[1])
    landing = lambda ps: [(3,) + p.shape[1:] for p in ps]
    dws3 = [g_d["w_in"], g_d["w_out"], d_win1, by_shard(d_wout1)]
    parts3 = [a.astype(BF16) for a in dws3]
    gfly3 = copies_start(parts3, landing(parts3), _scatter_copies, 3, "grads_layer1_start")
    w_out0 = w_ffn[0][1] + gfly3[4][0, 0].astype(BF16)
    dh20, d_win0, d_wout0 = ffn_backward(df0, sv_f0, w_ffn[0][0], w_out0, "0")
    (dx0p, dy0), (dg1_0, dgf0, dsc2_0, dsh2_0) = rowwise_bwd(
        f_resid_norm, [_wide(x0), _wide(y0, gdtype=BF16)], [g1[0], gffn[0], sc2[0], sh2[0]], [_wide(dx1), _wide(dh20)], (nt,), "l0_mid_bwd")
    dws2 = [d_win0, by_shard(d_wout0)]
    parts2 = [a.astype(BF16) for a in dws2]
    gfly2 = copies_start(parts2, landing(parts2), _scatter_copies, 3, "grads_ffn0_start")
    gdn_args = gdn_args[:5] + (w_gdn_out + gfly2[4][0, 0].astype(BF16),)
    gdn_flight = []

    def start_gdn_grads(d_w_in, d_w_out):
        dws1 = [jnp.stack([d_w_in[:, s * gcols:(s + 1) * gcols] for s in range(N_SHARD)]), by_shard(d_w_out)]
        parts1 = [a.astype(BF16) for a in dws1]
        fly = copies_start(parts1, landing(parts1), _scatter_copies, 3, "grads_gdn_start")
        gdn_flight.extend([dws1, fly])
        return fly[4][0, 0]

    dh10, g_g = gdn_backward(dy0, sv_g, *gdn_args, on_weight_grads=start_gdn_grads)
    dws1, gfly1 = gdn_flight
    (grad_x,), (dgm0, dsc1_0, dsh1_0) = rowwise_bwd(f_first, [_wide(x0)], [gmix[0], sc1[0], sh1[0]], [_wide(dx0p), _wide(dh10)], (nt,), "l0_norm_bwd")

    dmod = jnp.concatenate([dsh1_0, dsc1_0, dg1_0, dsh2_0, dsc2_0, dg2_0, dsh1_1, dsc1_1, dg1_1, dsh2_1, dsc2_1, dg2_1], axis=1)
    d_rel = jnp.transpose(g_d["rel"][:, :, 0])
    fold = lambda v: v[:, :DSW_DH] + v[:, DSW_DH:]
    small = [dmod, jnp.concatenate([dgm0, dgm1], axis=1), jnp.concatenate([dgf0, dgf1], axis=1), g_g["conv"].reshape(1, -1),
             g_g["alog"], g_g["dtb"], g_g["gain"], _pad_lanes(fold(g_d["q_gain2"])[0]), _pad_lanes(fold(g_d["k_gain2"])[0]),
             d_rel.reshape(1, -1)]
    used = [v.shape[1] // LANES for v in small]
    sizes = [-(-u // 8) * 8 for u in used]
    pad8 = lambda v, u, s: jnp.concatenate([v.reshape(u, LANES), jnp.zeros((s - u, LANES), F32)], axis=0) if s > u else v.reshape(u, LANES)
    pad_rows = sum(sizes)
    sbuf = jnp.concatenate([pad8(v, u, s) for v, u, s in zip(small, used, sizes)], axis=0)
    sgot = all_gather_small(sbuf, "gather_small_grads")
    ssum = add_rows([sgot[d * pad_rows:(d + 1) * pad_rows] for d in range(N_DEV)], F32, "sum_small_grads", rt=pad_rows)
    offs = np.cumsum([0] + sizes)
    take = lambda k: ssum[offs[k]:offs[k] + used[k]].reshape(1, -1)
    grad_b_ada = take(0).reshape(2, 6 * D)
    grad_norm_mix = take(1).reshape(2, D)
    grad_norm_ffn = take(2).reshape(2, D)
    conv_full = take(3).reshape(4, -1)
    ncv = gdn_conv.shape[2]
    grad_gdn_conv = lax.dynamic_slice_in_dim(conv_full, s_me * ncv, ncv, axis=1)[None]
    grad_a_log = take(4)[:, :GDN_H]
    grad_dt_bias = take(5)[:, :GDN_H]
    grad_out_norm = take(6)
    grad_q_norm = take(7)[:, :DSW_DH]
    grad_k_norm = take(8)[:, :DSW_DH]
    grad_rel = take(9).reshape(REL_BUCKETS, 3 * GDN_H)
    dmod_all = sgot.reshape(N_DEV, pad_rows, LANES)[:, :used[0]].reshape(N_DEV, 2, 6 * D)
    dmod_mine = lax.dynamic_slice_in_dim(dmod_all, s_me * ada_cols, ada_cols, axis=2)
    dmod16 = jnp.concatenate([dmod_mine, jnp.zeros_like(dmod_mine)], axis=0)
    grad_w_ada = jnp.stack([matmul(cond16, dmod16[:, l], "tn", F32, f"ada_dw_{l}") for l in range(2)])

    got3 = copies_wait(*gfly3[:4], _scatter_copies, grad_x, "grads_layer1_wait")
    got2 = copies_wait(*gfly2[:4], _scatter_copies, grad_x, "grads_ffn0_wait")
    got1 = copies_wait(*gfly1[:4], _scatter_copies, grad_x, "grads_gdn_wait")
    core_sums = []
    for i, (full, got) in enumerate(zip(dws3 + dws2 + dws1, got3 + got2 + got1)):
        own = lax.dynamic_index_in_dim(full, s_me, 0, keepdims=False)
        core_sums.append(add_rows([own, got[0], got[1], got[2]], F32, f"grads_core_sum_{i}"))
    sib_sums = sibling_exchange(core_sums, "grads_core_sums_swap")
    s_dsw_in, s_dsw_out, s_in1, s_out1, s_in0, s_out0, s_gdn_in, s_gdn_out = [
        add_rows([a, b], F32, f"grads_chip_total_{i}") for i, (a, b) in enumerate(zip(core_sums, sib_sums))]
    gsh = dict(gdn_w_in=s_gdn_in[None], gdn_w_out=s_gdn_out[None],
               w_ffn_in=jnp.stack([s_in0, s_in1]), w_ffn_out=jnp.stack([s_out0, s_out1]),
               dsw_w_in=s_dsw_in[None], dsw_w_out=s_dsw_out[None])

    grads = dict(w_ada=grad_w_ada, b_ada=grad_b_ada, norm_mix=grad_norm_mix, norm_ffn=grad_norm_ffn, w_ffn_in=gsh["w_ffn_in"],
                 w_ffn_out=gsh["w_ffn_out"], gdn_w_in=gsh["gdn_w_in"], gdn_conv=grad_gdn_conv, gdn_a_log=grad_a_log,
                 gdn_dt_bias=grad_dt_bias, gdn_out_norm=grad_out_norm, gdn_w_out=gsh["gdn_w_out"], dsw_w_in=gsh["dsw_w_in"],
                 dsw_q_norm=grad_q_norm, dsw_k_norm=grad_k_norm, dsw_w_out=gsh["dsw_w_out"], rel_bias=grad_rel)
    weights = dict(w_ada=w_ada, b_ada=b_ada, norm_mix=norm_mix, norm_ffn=norm_ffn, w_ffn_in=w_ffn_in, w_ffn_out=w_ffn_out,
                   gdn_w_in=gdn_w_in, gdn_conv=gdn_conv, gdn_a_log=gdn_a_log, gdn_dt_bias=gdn_dt_bias, gdn_out_norm=gdn_out_norm,
                   gdn_w_out=gdn_w_out, dsw_w_in=dsw_w_in, dsw_q_norm=dsw_q_norm, dsw_k_norm=dsw_k_norm, dsw_w_out=dsw_w_out,
                   rel_bias=rel_bias)
    ms = dict(w_ada=m_w_ada, b_ada=m_b_ada, norm_mix=m_norm_mix, norm_ffn=m_norm_ffn, w_ffn_in=m_w_ffn_in, w_ffn_out=m_w_ffn_out,
              gdn_w_in=m_gdn_w_in, gdn_conv=m_gdn_conv, gdn_a_log=m_gdn_a_log, gdn_dt_bias=m_gdn_dt_bias, gdn_out_norm=m_gdn_out_norm,
              gdn_w_out=m_gdn_w_out, dsw_w_in=m_dsw_w_in, dsw_q_norm=m_dsw_q_norm, dsw_k_norm=m_dsw_k_norm, dsw_w_out=m_dsw_w_out,
              rel_bias=m_rel_bias)
    vs = dict(w_ada=v_w_ada, b_ada=v_b_ada, norm_mix=v_norm_mix, norm_ffn=v_norm_ffn, w_ffn_in=v_w_ffn_in, w_ffn_out=v_w_ffn_out,
              gdn_w_in=v_gdn_w_in, gdn_conv=v_gdn_conv, gdn_a_log=v_gdn_a_log, gdn_dt_bias=v_gdn_dt_bias, gdn_out_norm=v_gdn_out_norm,
              gdn_w_out=v_gdn_w_out, dsw_w_in=v_dsw_w_in, dsw_q_norm=v_dsw_q_norm, dsw_k_norm=v_dsw_k_norm, dsw_w_out=v_dsw_w_out,
              rel_bias=v_rel_bias)
    names = list(weights)
    deltas, new_m, new_v = [], [], []
    for n in names:
        g = grads[n].reshape(weights[n].shape)
        grads[n] = g
        d, nm, nv = adamw(weights[n], g, ms[n], vs[n], f"adamw_{n}")
        deltas.append(d)
        new_m.append(nm)
        new_v.append(nv)
    return (loss, grad_x[None], *[grads[n] for n in names], *deltas, *new_m, *new_v)
```

```python
import functools
import math

import numpy as np
import jax
import jax.numpy as jnp
from jax import lax
from jax.experimental import pallas as pl
from jax.experimental.pallas import tpu as pltpu

F32 = jnp.float32
BF16 = jnp.bfloat16
SDS = jax.ShapeDtypeStruct
MESH = pl.DeviceIdType.MESH
ANY = pl.BlockSpec(memory_space=pl.ANY)

D = 1024
EPS = 1e-6
LANES = 128
GDN_H = 8
GDN_DK = 128
GDN_C = 64
DSW_GROUPS = ((128, 1), (512, 4), (2048, 16))
DSW_SPAN = 128
DSW_DH = 64
DSW_HG = 512
REL_BUCKETS = 32
REL_MAX_DIST = 2048
FFN = 2816
N_SHARD = 4
N_DEV = 8
VMEM_LIMIT = 48 * 1024 * 1024
NEG = -1e30

ADAM_LR, ADAM_B1, ADAM_B2, ADAM_EPS, ADAM_WD, ADAM_STEP = 0.001, 0.9, 0.999, 1e-08, 0.01, 10


def _cp(n_axes):
    return pltpu.CompilerParams(dimension_semantics=("arbitrary",) * n_axes, vmem_limit_bytes=VMEM_LIMIT)


def _blk(dim, cap):
    if dim <= cap:
        return dim
    best = None
    for b in range(LANES, cap + 1, LANES):
        if dim % b == 0:
            best = b
    assert best is not None, (dim, cap)
    return best


MAX_SHARD_BLOCK = 1408
def matmul(a, b, mode, out_dtype, name, cap_m=MAX_SHARD_BLOCK, cap_n=MAX_SHARD_BLOCK, cap_k=2048, col_shards=0):
    ns = col_shards
    if mode == "nn":
        (M, K) = a.shape
        K2, N = (b.shape[1], ns * b.shape[2]) if ns else b.shape
    elif mode == "nt":
        (M, K) = a.shape
        N, K2 = (b.shape[1], ns * b.shape[2]) if ns else b.shape
    else:
        (K, M), (K2, N) = a.shape, b.shape
    assert K == K2, (a.shape, b.shape, mode)
    if K <= 3072:
        cap_k = K
        if K > 2048:
            cap_n = 1024
    n_unit = N // ns if (ns and mode != "nt") else N
    k_unit = K // ns if (ns and mode == "nt") else K
    bm = _blk(M, cap_m)
    bn = _blk(n_unit, MAX_SHARD_BLOCK) if n_unit != N else _blk(N, cap_n)
    if k_unit != K:
        bk = _blk(k_unit, MAX_SHARD_BLOCK)
    else:
        bk = _blk(K, 1024 if (ns and mode == "tn") else cap_k)
    nk = K // bk
    nps, kps = n_unit // bn, k_unit // bk
    dims = {"nn": ((1,), (0,)), "nt": ((1,), (1,)), "tn": ((0,), (0,))}[mode]

    def dot(a_ref, b_ref):
        return lax.dot_general(a_ref[...].astype(BF16), b_ref[...].astype(BF16), (dims, ((), ())), preferred_element_type=F32)

    def body_one(a_ref, b_ref, o_ref):
        o_ref[...] = dot(a_ref, b_ref).astype(o_ref.dtype)

    def body_acc(a_ref, b_ref, o_ref, acc_ref):
        k = pl.program_id(2)

        @pl.when(k == 0)
        def _():
            acc_ref[...] = jnp.zeros_like(acc_ref)

        acc_ref[...] += dot(a_ref, b_ref)

        @pl.when(k == nk - 1)
        def _():
            o_ref[...] = acc_ref[...].astype(o_ref.dtype)

    a_spec = pl.BlockSpec((bk, bm), lambda i, j, k: (k, i)) if mode == "tn" else pl.BlockSpec((bm, bk), lambda i, j, k: (i, k))
    if mode == "nt":
        b_spec = pl.BlockSpec((None, bn, bk), lambda i, j, k: (k // kps, j, k % kps)) if ns else pl.BlockSpec((bn, bk), lambda i, j, k: (j, k))
    elif mode == "nn" and ns:
        b_spec = pl.BlockSpec((None, bk, bn), lambda i, j, k: (j // nps, k, j % nps))
    else:
        b_spec = pl.BlockSpec((bk, bn), lambda i, j, k: (k, j))
    if mode == "tn" and ns:
        o_spec, o_shape = pl.BlockSpec((None, bm, bn), lambda i, j, k: (j // nps, i, j % nps)), (ns, M, n_unit)
    else:
        o_spec, o_shape = pl.BlockSpec((bm, bn), lambda i, j, k: (i, j)), (M, N)
    return pl.pallas_call(
        body_one if nk == 1 else body_acc, name=name, grid=(M // bm, N // bn, nk),
        in_specs=[a_spec, b_spec], out_specs=o_spec,
        out_shape=SDS(o_shape, out_dtype), scratch_shapes=[] if nk == 1 else [pltpu.VMEM((bm, bn), F32)],
        compiler_params=_cp(3),
    )(a, b)


class Row:
    def __init__(self, arr, bshape, imap, splits=None, diff=True, acc=False, gdtype=F32, gshape=None, gbshape=None, gimap=None,
                 lead=0):
        self.arr, self.bshape, self.imap = arr, tuple(bshape), imap
        self.splits, self.lead = splits, lead
        self.diff, self.acc, self.gdtype = diff, acc, gdtype
        self.gshape = tuple(arr.shape) if gshape is None else tuple(gshape)
        self.gbshape = self.bshape if gbshape is None else tuple(gbshape)
        self.gimap = imap if gimap is None else gimap

    def gspec(self):
        return pl.BlockSpec(self.gbshape, self.gimap)

    def spec(self):
        return pl.BlockSpec(self.bshape, self.imap)

    def pieces(self, ref):
        return _load_pieces(ref, self.splits, self.lead)

    def n_pieces(self):
        return _n_pieces(self.splits, self.lead)


class Out:
    def __init__(self, shape, dtype, bshape, imap, splits=None, lead=0):
        self.shape, self.dtype, self.bshape, self.imap = tuple(shape), dtype, tuple(bshape), imap
        self.splits, self.lead = splits, lead

    def n_pieces(self):
        return _n_pieces(self.splits, self.lead)


def _n_pieces(splits, lead):
    return lead if lead else (1 if splits is None else len(splits))


def _load_pieces(ref, splits, lead):
    if lead:
        return [ref[k].astype(F32) for k in range(lead)]
    if splits is None:
        return [ref[...].astype(F32)]
    out, o = [], 0
    for w in splits:
        out.append(ref[..., o:o + w].astype(F32))
        o += w
    return out


def _store_pieces(ref, splits, lead, vals, accumulate=False):
    def put(idx, v):
        if accumulate:
            ref[idx] += v.astype(ref.dtype)
        else:
            ref[idx] = v.astype(ref.dtype)

    if lead:
        for k in range(lead):
            put(k, vals[k])
    elif splits is None:
        put(..., vals[0])
    else:
        o = 0
        for w, v in zip(splits, vals):
            put((..., slice(o, o + w)), v)
            o += w


def rowwise(fn, rows, params, outs, grid, name):
    nr, npar = len(rows), len(params)

    def body(*refs):
        ids = tuple(pl.program_id(a) for a in range(len(grid)))
        vals = []
        for r, ref in zip(rows, refs[:nr]):
            vals += r.pieces(ref)
        pvals = [ref[...].astype(F32) for ref in refs[nr:nr + npar]]
        res = list(fn(ids, *vals, *pvals))
        o = 0
        for spec, ref in zip(outs, refs[nr + npar:]):
            n = spec.n_pieces()
            _store_pieces(ref, spec.splits, spec.lead, res[o:o + n])
            o += n

    nz = len(grid)
    pspecs = [pl.BlockSpec(p.shape, (lambda *ids, _n=p.ndim: (0,) * _n)) for p in params]
    res = pl.pallas_call(
        body, name=name, grid=grid,
        in_specs=[r.spec() for r in rows] + pspecs,
        out_specs=[pl.BlockSpec(o.bshape, o.imap) for o in outs],
        out_shape=[SDS(o.shape, o.dtype) for o in outs],
        compiler_params=_cp(nz),
    )(*[r.arr for r in rows], *params)
    return list(res)


def rowwise_bwd(fn, rows, params, cots, grid, name):
    nr, npar, nc = len(rows), len(params), len(cots)
    drows = [r for r in rows if r.diff]
    nz = len(grid)

    def body(*refs):
        ids = tuple(pl.program_id(a) for a in range(nz))
        row_refs, par_refs = refs[:nr], refs[nr:nr + npar]
        cot_refs = refs[nr + npar:nr + npar + nc]
        drow_refs = refs[nr + npar + nc:nr + npar + nc + len(drows)]
        dpar_refs = refs[nr + npar + nc + len(drows):]
        pieces, is_diff = [], []
        for r, ref in zip(rows, row_refs):
            ps = r.pieces(ref)
            pieces += ps
            is_diff += [r.diff] * len(ps)
        pvals = [ref[...].astype(F32) for ref in par_refs]
        dvals = [p for p, dflag in zip(pieces, is_diff) if dflag]
        nd = len(dvals)

        def f(*args):
            it = iter(args[:nd])
            full = [next(it) if dflag else p for p, dflag in zip(pieces, is_diff)]
            return tuple(fn(ids, *full, *args[nd:]))

        _, vjp = jax.vjp(f, *dvals, *pvals)
        cvals = []
        for c, ref in zip(cots, cot_refs):
            cvals += c.pieces(ref)
        g = vjp(tuple(cvals))
        o = 0
        first_inner = ids[-1] == 0
        for r, ref in zip(drows, drow_refs):
            n = r.n_pieces()
            gs = g[o:o + n]
            o += n
            if r.acc:
                @pl.when(first_inner)
                def _(ref=ref):
                    ref[...] = jnp.zeros_like(ref)
            _store_pieces(ref, r.splits, r.lead, gs, accumulate=r.acc)
        first = functools.reduce(jnp.logical_and, [i == 0 for i in ids])
        for ref, gp in zip(dpar_refs, g[nd:]):
            @pl.when(first)
            def _(ref=ref):
                ref[...] = jnp.zeros_like(ref)
            ref[...] += gp

    pspecs = [pl.BlockSpec(p.shape, (lambda *ids, _n=p.ndim: (0,) * _n)) for p in params]
    res = pl.pallas_call(
        body, name=name, grid=grid,
        in_specs=[r.spec() for r in rows] + pspecs + [c.spec() for c in cots],
        out_specs=[r.gspec() for r in drows] + pspecs,
        out_shape=[SDS(r.gshape, r.gdtype) for r in drows] + [SDS(p.shape, F32) for p in params],
        compiler_params=_cp(nz),
    )(*[r.arr for r in rows], *params, *[c.arr for c in cots])
    res = list(res)
    return res[:len(drows)], res[len(drows):]


def _sigmoid(x):
    return 0.5 * (jnp.tanh(0.5 * x) + 1.0)


def _silu(x):
    return x * _sigmoid(x)


def _normmod(x, gain, sc, sh):
    inv = lax.rsqrt(jnp.mean(x * x, axis=-1, keepdims=True) + EPS)
    return x * inv * gain * (1.0 + sc) + sh


def f_first(ids, x, gain, sc, sh):
    return x, _normmod(x, gain, sc, sh)


def f_resid_norm(ids, x, y, g, gain, sc, sh):
    xn = x + g * y
    return xn, _normmod(xn, gain, sc, sh)


@jax.custom_vjp
def _swiglu(gate, up):
    return _silu(gate) * up


def _swiglu_fwd(gate, up):
    return _silu(gate) * up, (gate, up)


def _swiglu_bwd(res, da):
    gate, up = res
    s = _sigmoid(gate)
    gs = gate * s
    return da * up * (s + gs * (1.0 - s)), da * gs


_swiglu.defvjp(_swiglu_fwd, _swiglu_bwd)


def loss_and_grad(x, y, tgt, g, S):
    nt = S // WT

    def body(x_ref, y_ref, t_ref, g_ref, part_ref, dx_ref, dy_ref, dg_ref):
        @pl.when(pl.program_id(0) == 0)
        def _():
            dg_ref[...] = jnp.zeros_like(dg_ref)

        yv = y_ref[...].astype(F32)
        gg = g_ref[...]
        e = x_ref[...] + gg * yv - t_ref[...]
        part_ref[...] = 0.5 * jnp.sum(e * e, axis=0, keepdims=True) * (1.0 / D)
        d = e * (1.0 / D)
        dx_ref[...] = d
        dy_ref[...] = (d * gg).astype(dy_ref.dtype)
        dg_ref[...] += jnp.sum(d * yv, axis=0, keepdims=True)

    row = pl.BlockSpec((WT, D), lambda i: (i, 0))
    vec = pl.BlockSpec((1, D), lambda i: (0, 0))
    return pl.pallas_call(
        body, name="loss_and_grad", grid=(nt,), in_specs=[row, row, row, vec],
        out_specs=[pl.BlockSpec((None, 1, D), lambda i: (i, 0, 0)), row, row, vec],
        out_shape=[SDS((nt, 1, D), F32), SDS((S, D), F32), SDS((S, D), BF16), SDS((1, D), F32)],
        compiler_params=_cp(1),
    )(x, y, tgt, g)


def _softplus(x):
    return jnp.maximum(x, 0.0) + jnp.log(1.0 + jnp.exp(-jnp.abs(x)))


def _chunk_tril(T):
    r = lax.broadcasted_iota(jnp.int32, (T, T), 0)
    c = lax.broadcasted_iota(jnp.int32, (T, T), 1)
    return jnp.where((r // GDN_C == c // GDN_C) & (c <= r), 1.0, 0.0).astype(F32)


def _dot_hi(a, b, dims=((1,), (0,))):
    return lax.dot_general(a, b, (dims, ((), ())), precision=lax.Precision.HIGHEST, preferred_element_type=F32)


def _dot_x3(a, b, dims=((1,), (0,))):
    return lax.dot_general(a, b, (dims, ((), ())), precision=lax.Precision.HIGH, preferred_element_type=F32)


def f_gdn_gates(ids, ab, alog, dtb):
    T = ab.shape[0]
    g = -jnp.exp(alog) * _softplus(ab + dtb)
    beta = _sigmoid(ab)
    gcum = _dot_x3(_chunk_tril(T), g)
    row = lax.broadcasted_iota(jnp.int32, (LANES, LANES), 0)
    sel = lambda k: jnp.where(row == k, 1.0, 0.0).astype(F32)
    gcs = [_dot_x3(gcum, sel(h)) for h in range(GDN_H)]
    bts = [_dot_x3(beta, sel(GDN_H + h)) for h in range(GDN_H)]
    return (*gcs, *bts)


def f_gdn_post(ids, *args):
    os_, zs, gain = args[:GDN_H], args[GDN_H:2 * GDN_H], args[2 * GDN_H]
    out = []
    for o, z in zip(os_, zs):
        inv = lax.rsqrt(jnp.mean(o * o, axis=-1, keepdims=True) + EPS)
        out.append(o * inv * gain * _silu(z))
    return tuple(out)


def _qknorm1(x, gain2, scale):
    lane = lax.broadcasted_iota(jnp.int32, x.shape, 1)
    lo = lane < DSW_DH
    x2 = x * x
    s_all = jnp.sum(x2, axis=-1, keepdims=True)
    s_lo = jnp.sum(jnp.where(lo, x2, 0.0), axis=-1, keepdims=True)
    ms = jnp.where(lo, s_lo, s_all - s_lo) * (1.0 / DSW_DH)
    return x * lax.rsqrt(ms + EPS) * (gain2 * scale)


def f_combine(ids, o0, o1, o2, l0, l1, l2):
    m = jnp.maximum(jnp.maximum(l0, l1), l2)
    e0, e1, e2 = jnp.exp(l0 - m), jnp.exp(l1 - m), jnp.exp(l2 - m)
    den = e0 + e1 + e2
    o = (e0 * o0 + e1 * o1 + e2 * o2) / den
    return o, m + jnp.log(den)


GDN_T = 512
HALO = 16


def _conv_pre(xx, w):
    acc = xx * w[3:4, :]
    for j in range(3):
        acc = acc + pltpu.roll(xx, shift=3 - j, axis=0) * w[j:j + 1, :]
    return acc


@jax.custom_vjp
def _qkv_act_core(pre, norm_on, scale):
    s = _silu(pre)
    r = lax.rsqrt(jnp.sum(s * s, axis=-1, keepdims=True) + EPS)
    return jnp.where(norm_on > 0.5, s * r * scale, s)


def _qkv_act_fwd(pre, norm_on, scale):
    return _qkv_act_core(pre, norm_on, scale), (pre, norm_on, scale)


def _qkv_act_bwd(res, dout):
    pre, norm_on, scale = res
    sig = _sigmoid(pre)
    s = pre * sig
    r = lax.rsqrt(jnp.sum(s * s, axis=-1, keepdims=True) + EPS)
    unit = s * r
    dn = dout * scale
    ds = jnp.where(norm_on > 0.5, r * (dn - unit * jnp.sum(dn * unit, axis=-1, keepdims=True)), dout)
    return ds * (sig + s * (1.0 - sig)), jnp.zeros_like(norm_on), jnp.zeros_like(scale)


_qkv_act_core.defvjp(_qkv_act_fwd, _qkv_act_bwd)


def _qkv_act(pre, cidx):
    norm_on = jnp.where(cidx < 2 * GDN_H, 1.0, 0.0).astype(F32)
    scale = jnp.where(cidx < GDN_H, GDN_DK ** -0.5, 1.0).astype(F32)
    return _qkv_act_core(pre, norm_on, scale)


def gdn_pre(proj, conv_w, S):
    nt = S // GDN_T
    hb = GDN_T // HALO

    def body(prev_ref, cur_ref, w_ref, o_ref):
        p, i = pl.program_id(0), pl.program_id(1)
        for h in range(GDN_H):
            cols = slice(LANES * h, LANES * (h + 1))
            prev = jnp.where(i > 0, prev_ref[:, cols].astype(F32), 0.0)
            xx = jnp.concatenate([prev, cur_ref[:, cols].astype(F32)], axis=0)
            pre = _conv_pre(xx, w_ref[:, cols])[HALO:]
            o_ref[h] = _qkv_act(pre, p * GDN_H + h).astype(o_ref.dtype)

    hv = GDN_H * LANES
    return pl.pallas_call(
        body, name="gdn_pre", grid=(3, nt),
        in_specs=[pl.BlockSpec((HALO, hv), lambda p, i: (jnp.maximum(i * hb - 1, 0), p)),
                  pl.BlockSpec((GDN_T, hv), lambda p, i: (i, p)),
                  pl.BlockSpec((4, hv), lambda p, i: (0, p))],
        out_specs=pl.BlockSpec((None, GDN_H, GDN_T, LANES), lambda p, i: (p, 0, i, 0)),
        out_shape=SDS((3, GDN_H, S, LANES), BF16),
        compiler_params=_cp(2),
    )(proj, proj, conv_w)


def gdn_pre_bwd(proj, conv_w, dqkv, S):
    nt = S // GDN_T
    hb = GDN_T // HALO
    last_h = S // HALO - 1

    def body(prev_ref, cur_ref, next_ref, w_ref, d_ref, dnext_ref, dx_ref, dw_ref):
        p, i = pl.program_id(0), pl.program_id(1)

        @pl.when(i == 0)
        def _():
            dw_ref[...] = jnp.zeros_like(dw_ref)

        for h in range(GDN_H):
            cols = slice(LANES * h, LANES * (h + 1))
            w = w_ref[:, cols]
            prev = jnp.where(i > 0, prev_ref[:, cols].astype(F32), 0.0)
            xx = jnp.concatenate([prev, cur_ref[:, cols].astype(F32), next_ref[:, cols].astype(F32)], axis=0)
            dnext = jnp.where(i < nt - 1, dnext_ref[h], 0.0)
            dd = jnp.concatenate([jnp.zeros((HALO, LANES), F32), d_ref[h], dnext], axis=0)
            pre = _conv_pre(xx, w)
            _, vjp = jax.vjp(lambda v, _c=p * GDN_H + h: _qkv_act(v, _c), pre)
            (dpre,) = vjp(dd)
            dx = dpre * w[3:4, :]
            R = dpre.shape[0]
            for j in range(3):
                dx = dx + pltpu.roll(dpre, shift=R - (3 - j), axis=0) * w[j:j + 1, :]
            dx_ref[:, cols] = dx[HALO:HALO + GDN_T].astype(dx_ref.dtype)
            own = HALO + GDN_T
            rows_w = [jnp.sum((dpre * pltpu.roll(xx, shift=3 - j, axis=0))[:own], axis=0, keepdims=True) for j in range(3)]
            rows_w.append(jnp.sum((dpre * xx)[:own], axis=0, keepdims=True))
            r4 = lax.broadcasted_iota(jnp.int32, (4, LANES), 0)
            dw = jnp.zeros((4, LANES), F32)
            for j in range(4):
                dw = dw + jnp.where(r4 == j, rows_w[j], 0.0)
            dw_ref[:, cols] += dw

    hv = GDN_H * LANES
    return pl.pallas_call(
        body, name="gdn_pre_bwd", grid=(3, nt),
        in_specs=[pl.BlockSpec((HALO, hv), lambda p, i: (jnp.maximum(i * hb - 1, 0), p)),
                  pl.BlockSpec((GDN_T, hv), lambda p, i: (i, p)),
                  pl.BlockSpec((HALO, hv), lambda p, i: (jnp.minimum((i + 1) * hb, last_h), p)),
                  pl.BlockSpec((4, hv), lambda p, i: (0, p)),
                  pl.BlockSpec((None, GDN_H, GDN_T, LANES), lambda p, i: (p, 0, i, 0)),
                  pl.BlockSpec((None, GDN_H, HALO, LANES), lambda p, i: (p, 0, jnp.minimum((i + 1) * hb, last_h), 0))],
        out_specs=[pl.BlockSpec((GDN_T, hv), lambda p, i: (i, p)),
                   pl.BlockSpec((4, hv), lambda p, i: (0, p))],
        out_shape=[SDS((S, 3 * hv), BF16), SDS((4, 3 * hv), F32)],
        compiler_params=_cp(2),
    )(proj, proj, proj, conv_w, dqkv, dqkv)


_DIMS = {"nn": ((1,), (0,)), "nt": ((1,), (1,)), "tn": ((0,), (0,))}


def _mm_raw(a, b, mode, hi):
    if hi:
        return _dot_hi(a, b, _DIMS[mode])
    return lax.dot_general(a.astype(BF16), b.astype(BF16), (_DIMS[mode], ((), ())), preferred_element_type=F32)


@functools.partial(jax.custom_vjp, nondiff_argnums=(2, 3))
def mm(a, b, mode, hi):
    return _mm_raw(a, b, mode, hi)


def _mm_fwd(a, b, mode, hi):
    return _mm_raw(a, b, mode, hi), (a, b)


def _mm_bwd(mode, hi, res, dc):
    a, b = res
    if mode == "nn":
        da, db = mm(dc, b, "nt", hi), mm(a, dc, "tn", hi)
    elif mode == "nt":
        da, db = mm(dc, b, "nn", hi), mm(dc, a, "tn", hi)
    else:
        da, db = mm(b, dc, "nt", hi), mm(a, dc, "nn", hi)
    return da, db


mm.defvjp(_mm_fwd, _mm_bwd)


TRI_BASE = 8


def _unit_lower_inverses(Ls):
    n = Ls[0].shape[0]
    r = lax.broadcasted_iota(jnp.int32, (n, n), 0)
    c = lax.broadcasted_iota(jnp.int32, (n, n), 1)
    eye = jnp.where(r == c, 1.0, 0.0).astype(F32)
    base = r // TRI_BASE == c // TRI_BASE
    one = lambda a, b_: _mm_raw(a, b_, "nn", False)
    Ps = [jnp.where(base, -L, 0.0) for L in Ls]
    invs = [eye + P for P in Ps]
    k = 1
    while 2 * k < TRI_BASE:
        Ps = [one(P, P) for P in Ps]
        invs = [inv + one(inv, P) for inv, P in zip(invs, Ps)]
        k *= 2
    b = 2 * TRI_BASE
    while b <= n:
        off_mask = (r // b == c // b) & ((r % b) >= b // 2) & ((c % b) < b // 2)
        ts = [one(inv, jnp.where(off_mask, L, 0.0)) for inv, L in zip(invs, Ls)]
        invs = [inv - one(t, inv) for inv, t in zip(invs, ts)]
        b *= 2
    resid = [eye - inv - _dot_x3(L, inv) for inv, L in zip(invs, Ls)]
    return [inv + _dot_x3(inv, rs) for inv, rs in zip(invs, resid)]


@jax.custom_vjp
def tri_apply(invs, Ls, r1s, r2s):
    return [_mm_raw(i, r, "nn", False) for i, r in zip(invs, r1s)], [_mm_raw(i, r, "nn", False) for i, r in zip(invs, r2s)]


def _tri_fwd(invs, Ls, r1s, r2s):
    s1s = [_mm_raw(i, r, "nn", False) for i, r in zip(invs, r1s)]
    s2s = [_mm_raw(i, r, "nn", False) for i, r in zip(invs, r2s)]
    return (s1s, s2s), (invs, s1s, s2s)


def _tri_bwd(res, ds):
    invs, s1s, s2s = res
    d1s = [_mm_raw(i, d, "tn", False) for i, d in zip(invs, ds[0])]
    d2s = [_mm_raw(i, d, "tn", False) for i, d in zip(invs, ds[1])]
    dLs = [-(_mm_raw(d1, s1, "nt", False) + _mm_raw(d2, s2, "nt", False)) for d1, s1, d2, s2 in zip(d1s, s1s, d2s, s2s)]
    return [jnp.zeros_like(i) for i in invs], dLs, d1s, d2s


tri_apply.defvjp(_tri_fwd, _tri_bwd)


def _gdn_chunk(qs, ks, vs, gcbs, btbs, Ss, invs=None):
    C = qs[0].shape[0]
    r = lax.broadcasted_iota(jnp.int32, (C, C), 0)
    c = lax.broadcasted_iota(jnp.int32, (C, C), 1)
    causal, strict = c <= r, c < r
    rows = lax.broadcasted_iota(jnp.int32, gcbs[0].shape, 0)
    Gs = [g[:, :C] for g in gcbs]
    decays = [jnp.exp(jnp.where(causal, G - G.T, NEG)) for G in Gs]
    kbs = [k * b for k, b in zip(ks, btbs)]
    vbs = [v * b for v, b in zip(vs, btbs)]
    Ls = [jnp.where(strict, mm(kb, k, "nt", False) * d, 0.0) for kb, k, d in zip(kbs, ks, decays)]
    egs = [jnp.exp(g) for g in gcbs]
    if invs is None:
        invs = _unit_lower_inverses(Ls)
    us, ws = tri_apply(invs, Ls, vbs, [kb * eg for kb, eg in zip(kbs, egs)])
    qks = [jnp.where(causal, mm(q, k, "nt", False) * d, 0.0) for q, k, d in zip(qs, ks, decays)]
    g_lasts = [jnp.sum(jnp.where(rows == C - 1, g, 0.0), axis=0, keepdims=True) for g in gcbs]
    q_decs = [q * eg for q, eg in zip(qs, egs)]
    k_decs = [k * jnp.exp(gl - g) for k, gl, g in zip(ks, g_lasts, gcbs)]
    v_news = [u - mm(w, S, "nn", False) for u, w, S in zip(us, ws, Ss)]
    os_ = [mm(qd, S, "nn", False) + mm(qk, vn, "nn", False) for qd, S, qk, vn in zip(q_decs, Ss, qks, v_news)]
    S_news = [S * jnp.exp(gl) + mm(kd, vn, "tn", False) for S, gl, kd, vn in zip(Ss, g_lasts, k_decs, v_news)]
    return os_, S_news, invs


INV_CHUNKS = 4
SCAN_CHUNKS = 4


def gdn_inverses(qkv, gc, bt, S):
    nchunk = S // GDN_C
    rows = INV_CHUNKS * GDN_C

    def body(k_ref, g_ref, b_ref, inv_ref):
        items = [(h, m) for m in range(INV_CHUNKS) for h in range(GDN_H)]
        r = lax.broadcasted_iota(jnp.int32, (GDN_C, GDN_C), 0)
        c = lax.broadcasted_iota(jnp.int32, (GDN_C, GDN_C), 1)
        sl = lambda m: slice(m * GDN_C, (m + 1) * GDN_C)
        ks = [k_ref[h, sl(m), :].astype(F32) for h, m in items]
        Gs = [g_ref[h, sl(m), :GDN_C] for h, m in items]
        kbs = [k * b_ref[h, sl(m), :] for k, (h, m) in zip(ks, items)]
        decays = [jnp.exp(jnp.where(c <= r, G - G.T, NEG)) for G in Gs]
        Ls = [jnp.where(c < r, _mm_raw(kb, k, "nt", False) * d, 0.0) for kb, k, d in zip(kbs, ks, decays)]
        for (h, m), inv in zip(items, _unit_lower_inverses(Ls)):
            inv_ref[h, m] = inv

    hb = pl.BlockSpec((GDN_H, rows, LANES), lambda n: (0, n, 0))
    return pl.pallas_call(
        body, name="gdn_inverses", grid=(nchunk // INV_CHUNKS,),
        in_specs=[pl.BlockSpec((None, GDN_H, rows, LANES), lambda n: (1, 0, n, 0)), hb, hb],
        out_specs=pl.BlockSpec((GDN_H, INV_CHUNKS, GDN_C, GDN_C), lambda n: (0, n, 0, 0)),
        out_shape=SDS((GDN_H, nchunk, GDN_C, GDN_C), F32),
        compiler_params=_cp(1),
    )(qkv, gc, bt)


def gdn_core(qkv, gc, bt, invs, S):
    nchunk = S // GDN_C

    def body(qkv_ref, g_ref, b_ref, inv_ref, o_ref, st_ref, s_scr):
        n = pl.program_id(0)

        @pl.when(n == 0)
        def _():
            s_scr[...] = jnp.zeros_like(s_scr)

        heads = range(GDN_H)
        S_cur = [s_scr[h] for h in heads]
        for m in range(SCAN_CHUNKS):
            sl = slice(m * GDN_C, (m + 1) * GDN_C)
            os_, S_new, _ = _gdn_chunk(*[[qkv_ref[p, h, sl, :].astype(F32) for h in heads] for p in range(3)],
                                       [g_ref[h, sl, :] for h in heads], [b_ref[h, sl, :] for h in heads], S_cur,
                                       invs=[inv_ref[h, m] for h in heads])
            for h in heads:
                st_ref[h, m] = S_cur[h].astype(st_ref.dtype)
                o_ref[h, sl, :] = os_[h]
            S_cur = S_new
        for h in heads:
            s_scr[h] = S_cur[h]

    rows = SCAN_CHUNKS * GDN_C
    blk3 = pl.BlockSpec((3, GDN_H, rows, LANES), lambda n: (0, 0, n, 0))
    hb = pl.BlockSpec((GDN_H, rows, LANES), lambda n: (0, n, 0))
    return pl.pallas_call(
        body, name="gdn_core", grid=(nchunk // SCAN_CHUNKS,),
        in_specs=[blk3, hb, hb, pl.BlockSpec((GDN_H, SCAN_CHUNKS, GDN_C, GDN_C), lambda n: (0, n, 0, 0))],
        out_specs=[hb, pl.BlockSpec((GDN_H, SCAN_CHUNKS, GDN_DK, LANES), lambda n: (0, n, 0, 0))],
        out_shape=[SDS((GDN_H, S, LANES), F32), SDS((GDN_H, nchunk, GDN_DK, LANES), BF16)],
        scratch_shapes=[pltpu.VMEM((GDN_H, GDN_DK, LANES), F32)],
        compiler_params=_cp(1),
    )(qkv, gc, bt, invs)


def gdn_core_bwd(qkv, gc, bt, states, invs, do, S):
    nchunk = S // GDN_C

    def body(qkv_ref, g_ref, b_ref, st_ref, inv_ref, do_ref, dqkv_ref, dg_ref, db_ref, ds_scr):
        n = pl.program_id(0)

        @pl.when(n == 0)
        def _():
            ds_scr[...] = jnp.zeros_like(ds_scr)

        heads = range(GDN_H)
        dS_cur = [ds_scr[h] for h in heads]
        for m in reversed(range(SCAN_CHUNKS)):
            sl = slice(m * GDN_C, (m + 1) * GDN_C)
            saved = [inv_ref[h, m] for h in heads]
            _, vjp = jax.vjp(lambda *a, _s=saved: _gdn_chunk(*a, invs=_s)[:2],
                             *[[qkv_ref[p, h, sl, :].astype(F32) for h in heads] for p in range(3)],
                             [g_ref[h, sl, :] for h in heads], [b_ref[h, sl, :] for h in heads],
                             [st_ref[h, m].astype(F32) for h in heads])
            dq, dk, dv, dg, db, dS_cur = vjp(([do_ref[h, sl, :] for h in heads], dS_cur))
            for h in heads:
                dqkv_ref[0, h, sl, :] = dq[h]
                dqkv_ref[1, h, sl, :] = dk[h]
                dqkv_ref[2, h, sl, :] = dv[h]
                dg_ref[h, sl, :] = dg[h]
                db_ref[h, sl, :] = db[h]
        for h in heads:
            ds_scr[h] = dS_cur[h]

    nblk = nchunk // SCAN_CHUNKS
    rows = SCAN_CHUNKS * GDN_C
    rev = lambda n: nblk - 1 - n
    blk3 = pl.BlockSpec((3, GDN_H, rows, LANES), lambda n: (0, 0, rev(n), 0))
    hb = pl.BlockSpec((GDN_H, rows, LANES), lambda n: (0, rev(n), 0))
    return pl.pallas_call(
        body, name="gdn_core_bwd", grid=(nblk,),
        in_specs=[blk3, hb, hb, pl.BlockSpec((GDN_H, SCAN_CHUNKS, GDN_DK, LANES), lambda n: (0, rev(n), 0, 0)),
                  pl.BlockSpec((GDN_H, SCAN_CHUNKS, GDN_C, GDN_C), lambda n: (0, rev(n), 0, 0)), hb],
        out_specs=[blk3, hb, hb],
        out_shape=[SDS((3, GDN_H, S, LANES), F32), SDS((GDN_H, S, LANES), F32), SDS((GDN_H, S, LANES), F32)],
        scratch_shapes=[pltpu.VMEM((GDN_H, GDN_DK, LANES), F32)],
        compiler_params=_cp(1),
    )(qkv, gc, bt, states, invs, do)


GDN_MAIN = 4 * GDN_H * LANES
GDN_PROJ = GDN_MAIN + LANES
RT = 256


def gdn_forward(h, w_in, conv_w, alog, dtb, out_gain, w_out):
    S = h.shape[0]
    nt = S // RT
    proj = matmul(h, w_in, "nn", BF16, "gdn_in")
    qkv = gdn_pre(proj, conv_w, S)
    ab_row = Row(proj, (RT, LANES), lambda i: (i, GDN_MAIN // LANES), gdtype=BF16, gshape=(S, LANES), gimap=lambda i: (i, 0))
    hm = lambda i: (0, i, 0)
    hv = GDN_H * LANES
    gc, bt = rowwise(f_gdn_gates, [ab_row], [alog, dtb],
                     [Out((GDN_H, S, LANES), F32, (GDN_H, RT, LANES), hm, lead=GDN_H)] * 2, (nt,), "gdn_gates")
    invs = gdn_inverses(qkv, gc, bt, S)
    o, states = gdn_core(qkv, gc, bt, invs, S)
    o_row = Row(o, (GDN_H, RT, LANES), hm, lead=GDN_H)
    z_row = Row(proj, (RT, hv), lambda i: (i, 3), splits=[LANES] * GDN_H, gdtype=BF16, gshape=(S, hv), gimap=lambda i: (i, 0))
    (on,) = rowwise(f_gdn_post, [o_row, z_row], [out_gain],
                    [Out((S, hv), BF16, (RT, hv), lambda i: (i, 0), splits=[LANES] * GDN_H)], (nt,), "gdn_post")
    y = matmul(on, w_out, "nn", BF16, "gdn_out")
    saved = dict(h=h, proj=proj, qkv=qkv, gc=gc, bt=bt, states=states, invs=invs, o=o, on=on, ab_row=ab_row, o_row=o_row, z_row=z_row)
    return y, saved


def gdn_backward(dy, sv, w_in, conv_w, alog, dtb, out_gain, w_out, on_weight_grads=None):
    S = dy.shape[0]
    nt = S // RT
    hm = lambda i: (0, i, 0)
    hv = GDN_H * LANES
    don = matmul(dy, w_out, "nt", BF16, "gdn_out_dx")
    d_w_out = matmul(sv["on"], dy, "tn", F32, "gdn_out_dw")
    (do, dz), (d_gain,) = rowwise_bwd(f_gdn_post, [sv["o_row"], sv["z_row"]], [out_gain],
                                      [Row(don, (RT, hv), lambda i: (i, 0), splits=[LANES] * GDN_H)], (nt,), "gdn_post_bwd")
    dqkv, dgc, dbt = gdn_core_bwd(sv["qkv"], sv["gc"], sv["bt"], sv["states"], sv["invs"], do, S)
    head_blk = lambda a: Row(a, (GDN_H, RT, LANES), hm, lead=GDN_H)
    (dab,), (d_alog, d_dtb) = rowwise_bwd(f_gdn_gates, [sv["ab_row"]], [alog, dtb], [head_blk(dgc), head_blk(dbt)],
                                          (nt,), "gdn_gates_bwd")
    dqkv_proj, d_conv = gdn_pre_bwd(sv["proj"], conv_w, dqkv, S)
    dproj = jnp.concatenate([dqkv_proj, dz, dab], axis=1)
    d_w_in = matmul(sv["h"], dproj, "tn", F32, "gdn_in_dw")
    if on_weight_grads is not None:
        w_in = w_in + on_weight_grads(d_w_in, d_w_out).astype(w_in.dtype)
    dh = matmul(dproj, w_in, "nt", BF16, "gdn_in_dx")
    return dh, dict(w_in=d_w_in, conv=d_conv, alog=d_alog, dtb=d_dtb, gain=d_gain, w_out=d_w_out)


QB = DSW_SPAN
N_HP = DSW_HG // LANES


def _bucket_maps():
    a = np.arange(QB)[:, None]
    j = np.arange(2 * QB)[None, :]
    dist = QB + a - j
    band = (dist >= 0) & (dist <= DSW_SPAN)
    maps = []
    for _, dil in DSW_GROUPS:
        dd = np.maximum(dist, 0) * dil
        max_exact = REL_BUCKETS // 2
        scaled = np.log(np.maximum(dd, 1).astype(np.float32) / np.float32(max_exact)) / np.float32(math.log(REL_MAX_DIST / max_exact))
        large = max_exact + (scaled * np.float32(REL_BUCKETS - max_exact)).astype(np.int32)
        large = np.minimum(large, REL_BUCKETS - 1)
        maps.append(np.where(dd < max_exact, dd, large).astype(np.int32))
    return np.stack(maps), band


def dsw_bias(rel_bias):
    maps, band = _bucket_maps()
    maps = np.where(band[None], maps, -1).astype(np.int32)

    def body(tab_ref, bk_ref, o_ref):
        gh = pl.program_id(0)
        bk = bk_ref[...]
        acc = jnp.full(bk.shape, NEG, F32)
        for b in range(REL_BUCKETS):
            acc = jnp.where(bk == b, tab_ref[b, gh], acc)
        o_ref[...] = acc

    return pl.pallas_call(
        body, name="dsw_bias", grid=(3 * GDN_H,),
        in_specs=[pl.BlockSpec(memory_space=pltpu.SMEM),
                  pl.BlockSpec((None, QB, 2 * QB), lambda gh: (gh // GDN_H, 0, 0))],
        out_specs=pl.BlockSpec((None, QB, 2 * QB), lambda gh: (gh, 0, 0)),
        out_shape=SDS((3 * GDN_H, QB, 2 * QB), F32),
        compiler_params=_cp(1),
    )(rel_bias, jnp.asarray(maps))


def dsw_bias_grad(dbias):
    maps, band = _bucket_maps()
    maps = np.where(band[None], maps, -1).astype(np.int32)

    def body(d_ref, bk_ref, o_ref):
        bk = bk_ref[...]
        d = d_ref[...]
        rows = lax.broadcasted_iota(jnp.int32, (REL_BUCKETS, LANES), 0)
        acc = jnp.zeros((REL_BUCKETS, LANES), F32)
        for b in range(REL_BUCKETS):
            part = jnp.sum(jnp.where(bk == b, d, 0.0), axis=0, keepdims=True)
            val = jnp.sum(part, axis=1, keepdims=True)
            acc = jnp.where(rows == b, val, acc)
        o_ref[...] = acc

    return pl.pallas_call(
        body, name="dsw_bias_grad", grid=(3 * GDN_H,),
        in_specs=[pl.BlockSpec((None, QB, 2 * QB), lambda gh: (gh, 0, 0)),
                  pl.BlockSpec((None, QB, 2 * QB), lambda gh: (gh // GDN_H, 0, 0))],
        out_specs=pl.BlockSpec((None, REL_BUCKETS, LANES), lambda gh: (gh, 0, 0)),
        out_shape=SDS((3 * GDN_H, REL_BUCKETS, LANES), F32),
        compiler_params=_cp(1),
    )(dbias, jnp.asarray(maps))


def _nt(a, b):
    return lax.dot_general(a, b, (((1,), (1,)), ((), ())), preferred_element_type=F32)


def _tn(a, b):
    return lax.dot_general(a, b, (((0,), (0,)), ((), ())), preferred_element_type=F32)


N_LB = DSW_HG // LANES
HALF = DSW_DH // 2


def _lanes(j):
    return slice(LANES * j, LANES * (j + 1))


def _deinterleave(stage, out_ref, dil, rows, dtype):
    for r in range(dil):
        for j in range(N_LB):
            out_ref[r, :, _lanes(j)] = stage[j, pl.ds(r, rows, stride=dil), :].astype(dtype)


def _interleave(in_ref, stage, dil, rows):
    for r in range(dil):
        for j in range(N_LB):
            stage[j, pl.ds(r, rows, stride=dil), :] = in_ref[r, :, _lanes(j)].astype(F32)


def dsw_prep(proj, q_gain2, k_gain2, gi, S):
    dil = DSW_GROUPS[gi][1]
    nt, rows = S // RT, RT // dil

    def body(q_ref, k_ref, v_ref, qg_ref, kg_ref, qo_ref, ko_ref, vo_ref, stage):
        for src, gain_ref, scale, dst in ((q_ref, qg_ref, DSW_DH ** -0.5, qo_ref), (k_ref, kg_ref, 1.0, ko_ref), (v_ref, None, None, vo_ref)):
            for j in range(N_LB):
                val = src[:, _lanes(j)].astype(F32)
                val = val if gain_ref is None else _qknorm1(val, gain_ref[...], scale)
                if dil == 1:
                    dst[0, :, _lanes(j)] = val.astype(BF16)
                else:
                    stage[j] = val
            if dil > 1:
                _deinterleave(stage, dst, dil, rows, BF16)

    col = lambda which: pl.BlockSpec((RT, DSW_HG), lambda i, _c=which * 3 + gi: (i, _c))
    gspec = pl.BlockSpec((1, LANES), lambda i: (0, 0))
    ospec = pl.BlockSpec((dil, rows, DSW_HG), lambda i: (0, i, 0))
    return pl.pallas_call(
        body, name=f"dsw_prep_g{gi}", grid=(nt,),
        in_specs=[col(0), col(1), col(2), gspec, gspec], out_specs=[ospec] * 3,
        out_shape=[SDS((dil, S // dil, DSW_HG), BF16)] * 3,
        scratch_shapes=[pltpu.VMEM((N_LB, RT, LANES), F32)], compiler_params=_cp(1),
    )(proj, proj, proj, q_gain2, k_gain2)


def dsw_prep_bwd(proj, q_gain2, k_gain2, dqd, dkd, dvd, gi, S):
    dil = DSW_GROUPS[gi][1]
    nt, rows = S // RT, RT // dil

    def body(q_ref, k_ref, qg_ref, kg_ref, dq_ref, dk_ref, dv_ref, oq_ref, ok_ref, ov_ref, dqg_ref, dkg_ref, stage):
        i = pl.program_id(0)

        @pl.when(i == 0)
        def _():
            dqg_ref[...] = jnp.zeros_like(dqg_ref)
            dkg_ref[...] = jnp.zeros_like(dkg_ref)

        for src, gain_ref, scale, cot_ref, dst, dg_ref in ((q_ref, qg_ref, DSW_DH ** -0.5, dq_ref, oq_ref, dqg_ref),
                                                          (k_ref, kg_ref, 1.0, dk_ref, ok_ref, dkg_ref)):
            if dil > 1:
                _interleave(cot_ref, stage, dil, rows)
            for j in range(N_LB):
                _, vjp = jax.vjp(lambda x, g, _s=scale: _qknorm1(x, g, _s), src[:, _lanes(j)].astype(F32), gain_ref[...])
                dx, dg = vjp(stage[j] if dil > 1 else cot_ref[0, :, _lanes(j)].astype(F32))
                dst[:, _lanes(j)] = dx.astype(dst.dtype)
                dg_ref[...] += dg
        if dil > 1:
            _interleave(dv_ref, stage, dil, rows)
        for j in range(N_LB):
            ov_ref[:, _lanes(j)] = (stage[j] if dil > 1 else dv_ref[0, :, _lanes(j)]).astype(ov_ref.dtype)

    col = lambda which: pl.BlockSpec((RT, DSW_HG), lambda i, _c=which * 3 + gi: (i, _c))
    gspec = pl.BlockSpec((1, LANES), lambda i: (0, 0))
    dspec = pl.BlockSpec((dil, rows, DSW_HG), lambda i: (0, i, 0))
    nspec = pl.BlockSpec((RT, DSW_HG), lambda i: (i, 0))
    return pl.pallas_call(
        body, name=f"dsw_prep_bwd_g{gi}", grid=(nt,),
        in_specs=[col(0), col(1), gspec, gspec, dspec, dspec, dspec], out_specs=[nspec] * 3 + [gspec] * 2,
        out_shape=[SDS((S, DSW_HG), BF16)] * 3 + [SDS((1, LANES), F32)] * 2,
        scratch_shapes=[pltpu.VMEM((N_LB, RT, LANES), F32)], compiler_params=_cp(1),
    )(proj, proj, q_gain2, k_gain2, dqd, dkd, dvd)


def _head_masks(rows):
    lane = lax.broadcasted_iota(jnp.int32, (rows, LANES), 1)
    return lane < DSW_DH, (lane % DSW_DH) < HALF


def dsw_attn_fwd(qd, kd, vd, bias, gi, S):
    dil = DSW_GROUPS[gi][1]
    sd = S // dil
    nq = sd // QB

    def body(q_ref, k_ref, v_ref, b_ref, o_ref, l_ref, kp_scr, vp_scr):
        i = pl.program_id(1)

        @pl.when(i == 0)
        def _():
            kp_scr[...] = jnp.zeros_like(kp_scr)
            vp_scr[...] = jnp.zeros_like(vp_scr)

        lo_q, _ = _head_masks(QB)
        lo_k, _ = _head_masks(2 * QB)
        col = lax.broadcasted_iota(jnp.int32, (QB, 2 * QB), 1)
        first = jnp.logical_and(i == 0, col < QB)
        hps, heads = range(N_HP), range(2 * N_HP)
        k2s = [jnp.concatenate([kp_scr[:, _lanes(hp)], k_ref[:, _lanes(hp)]], axis=0) for hp in hps]
        v2s = [jnp.concatenate([vp_scr[:, _lanes(hp)], v_ref[:, _lanes(hp)]], axis=0) for hp in hps]
        qs = [q_ref[:, _lanes(hp)] for hp in hps]
        k_now, v_now = k_ref[...], v_ref[...]
        mqs = [lo_q if h % 2 == 0 else jnp.logical_not(lo_q) for h in heads]
        mks = [lo_k if h % 2 == 0 else jnp.logical_not(lo_k) for h in heads]
        ss = [jnp.where(first, NEG, _nt(jnp.where(mqs[h], qs[h // 2], 0).astype(BF16), k2s[h // 2]) + b_ref[h]) for h in heads]
        mxs = [jnp.max(s, axis=1, keepdims=True) for s in ss]
        ps = [jnp.exp(s - mx) for s, mx in zip(ss, mxs)]
        ls = [jnp.sum(p, axis=1, keepdims=True) for p in ps]
        ohs = [jnp.dot(ps[h].astype(BF16), jnp.where(mks[h], v2s[h // 2], 0).astype(BF16), preferred_element_type=F32) / ls[h] for h in heads]
        lse_h = [mx + jnp.log(l) for mx, l in zip(mxs, ls)]
        for hp in hps:
            o_ref[:, _lanes(hp)] = ohs[2 * hp] + ohs[2 * hp + 1]
            l_ref[:, _lanes(hp)] = jnp.where(lo_q, lse_h[2 * hp], lse_h[2 * hp + 1])
        kp_scr[...] = k_now
        vp_scr[...] = v_now

    blk = pl.BlockSpec((None, QB, DSW_HG), lambda r, i: (r, i, 0))
    return pl.pallas_call(
        body, name=f"dsw_attn_g{gi}", grid=(dil, nq),
        in_specs=[blk, blk, blk, pl.BlockSpec((GDN_H, QB, 2 * QB), lambda r, i: (gi, 0, 0))],
        out_specs=[blk, blk], out_shape=[SDS((dil, sd, DSW_HG), F32)] * 2,
        scratch_shapes=[pltpu.VMEM((QB, DSW_HG), BF16)] * 2, compiler_params=_cp(2),
    )(qd, kd, vd, bias)


def dsw_attn_bwd(qd, kd, vd, bias, dod, statd, gi, S):
    dil = DSW_GROUPS[gi][1]
    sd = S // dil
    nq = sd // QB
    cur = lambda i: jnp.minimum(i, nq - 1)
    done = lambda i: jnp.maximum(i - 1, 0)

    def body(q_ref, k_ref, v_ref, b_ref, do_ref, st_ref, dq_ref, dk_ref, dv_ref, db_ref, kp_scr, vp_scr, dk_scr, dv_scr):
        r, i = pl.program_id(0), pl.program_id(1)

        @pl.when(jnp.logical_and(r == 0, i == 0))
        def _():
            db_ref[...] = jnp.zeros_like(db_ref)

        @pl.when(i == 0)
        def _():
            for scr in (kp_scr, vp_scr, dk_scr, dv_scr):
                scr[...] = jnp.zeros_like(scr)

        @pl.when(i < nq)
        def _():
            lo_q, first_half = _head_masks(QB)
            col = lax.broadcasted_iota(jnp.int32, (QB, 2 * QB), 1)
            first = jnp.logical_and(i == 0, col < QB)
            hps, heads = range(N_HP), range(2 * N_HP)
            k2s = [jnp.concatenate([kp_scr[:, _lanes(hp)], k_ref[:, _lanes(hp)]], axis=0) for hp in hps]
            v2s = [jnp.concatenate([vp_scr[:, _lanes(hp)], v_ref[:, _lanes(hp)]], axis=0) for hp in hps]
            qs = [q_ref[:, _lanes(hp)] for hp in hps]
            douts = [do_ref[:, _lanes(hp)] for hp in hps]
            stats = [st_ref[:, _lanes(hp)] for hp in hps]
            dkc = [dk_scr[:, _lanes(hp)] for hp in hps]
            dvc = [dv_scr[:, _lanes(hp)] for hp in hps]
            k_now, v_now = k_ref[...], v_ref[...]
            mqs = [lo_q if h % 2 == 0 else jnp.logical_not(lo_q) for h in heads]
            qms = [jnp.where(mqs[h], qs[h // 2], 0).astype(BF16) for h in heads]
            doms = [jnp.where(mqs[h], douts[h // 2], 0).astype(BF16) for h in heads]
            ss = [jnp.where(first, NEG, _nt(qms[h], k2s[h // 2]) + b_ref[h]) for h in heads]
            lses = [jnp.max(jnp.where(jnp.logical_and(mqs[h], first_half), stats[h // 2], NEG), axis=1, keepdims=True) for h in heads]
            deltas = [jnp.max(jnp.where(jnp.logical_and(mqs[h], jnp.logical_not(first_half)), stats[h // 2], NEG), axis=1, keepdims=True)
                      for h in heads]
            ps = [jnp.exp(ss[h] - lses[h]) for h in heads]
            dss = [ps[h] * (_nt(doms[h], v2s[h // 2]) - deltas[h]) for h in heads]
            dsbs = [d.astype(BF16) for d in dss]
            dqh = [jnp.where(mqs[h], jnp.dot(dsbs[h], k2s[h // 2], preferred_element_type=F32), 0.0) for h in heads]
            dkh = [_tn(dsbs[h], qms[h]) for h in heads]
            dvh = [_tn(ps[h].astype(BF16), doms[h]) for h in heads]
            for h in heads:
                db_ref[h] += dss[h]
            for hp in hps:
                dk2 = dkh[2 * hp] + dkh[2 * hp + 1]
                dv2 = dvh[2 * hp] + dvh[2 * hp + 1]
                dq_ref[:, _lanes(hp)] = dqh[2 * hp] + dqh[2 * hp + 1]
                dk_ref[:, _lanes(hp)] = dkc[hp] + dk2[:QB]
                dv_ref[:, _lanes(hp)] = (dvc[hp] + dv2[:QB]).astype(dv_ref.dtype)
                dk_scr[:, _lanes(hp)] = dk2[QB:]
                dv_scr[:, _lanes(hp)] = dv2[QB:]
            kp_scr[...] = k_now
            vp_scr[...] = v_now

        @pl.when(i == nq)
        def _():
            dk_ref[...] = dk_scr[...]
            dv_ref[...] = dv_scr[...].astype(dv_ref.dtype)

    blk = pl.BlockSpec((None, QB, DSW_HG), lambda r, i: (r, cur(i), 0))
    oblk = pl.BlockSpec((None, QB, DSW_HG), lambda r, i: (r, done(i), 0))
    return pl.pallas_call(
        body, name=f"dsw_attn_bwd_g{gi}", grid=(dil, nq + 1),
        in_specs=[blk, blk, blk, pl.BlockSpec((GDN_H, QB, 2 * QB), lambda r, i: (gi, 0, 0)), blk, blk],
        out_specs=[blk, oblk, oblk, pl.BlockSpec((GDN_H, QB, 2 * QB), lambda r, i: (0, 0, 0))],
        out_shape=[SDS((dil, sd, DSW_HG), F32), SDS((dil, sd, DSW_HG), F32), SDS((dil, sd, DSW_HG), BF16),
                   SDS((GDN_H, QB, 2 * QB), F32)],
        scratch_shapes=[pltpu.VMEM((QB, DSW_HG), BF16)] * 2 + [pltpu.VMEM((QB, DSW_HG), F32)] * 2,
        compiler_params=_cp(2),
    )(qd, kd, vd, bias, dod, statd)


def dsw_combine(ods, lseds, S):
    nt = S // RT
    dils = [d for _, d in DSW_GROUPS]

    def body(*refs):
        ins, (o_ref, l_ref), stages = refs[:6], refs[6:8], refs[8:]
        for g in range(3):
            if dils[g] > 1:
                _interleave(ins[g], stages[g], dils[g], RT // dils[g])
                _interleave(ins[3 + g], stages[3 + g], dils[g], RT // dils[g])
        for j in range(N_LB):
            natural = lambda a: stages[a][j] if dils[a % 3] > 1 else ins[a][0, :, _lanes(j)]
            o, lse = f_combine(None, *[natural(a) for a in range(6)])
            o_ref[:, _lanes(j)] = o.astype(o_ref.dtype)
            l_ref[:, _lanes(j)] = lse

    dspec = lambda d: pl.BlockSpec((d, RT // d, DSW_HG), lambda i: (0, i, 0))
    nspec = pl.BlockSpec((RT, DSW_HG), lambda i: (i, 0))
    return pl.pallas_call(
        body, name="dsw_combine", grid=(nt,),
        in_specs=[dspec(d) for d in dils] * 2, out_specs=[nspec, nspec],
        out_shape=[SDS((S, DSW_HG), BF16), SDS((S, DSW_HG), F32)],
        scratch_shapes=[pltpu.VMEM((N_LB, RT, LANES), F32)] * 6, compiler_params=_cp(1),
    )(*ods, *lseds)


def dsw_bwd_prep(do, o, lse, S):
    nt = S // RT
    dils = [d for _, d in DSW_GROUPS]

    def body(do_ref, o_ref, l_ref, *rest):
        outs, (st_do, st_stat) = rest[:6], rest[6:]
        lo, first_half = _head_masks(RT)
        for j in range(N_LB):
            dout = do_ref[:, _lanes(j)]
            prod = dout * o_ref[:, _lanes(j)].astype(F32)
            s_all = jnp.sum(prod, axis=1, keepdims=True)
            s_lo = jnp.sum(jnp.where(lo, prod, 0.0), axis=1, keepdims=True)
            delta = jnp.where(lo, s_lo, s_all - s_lo)
            stat = jnp.where(first_half, l_ref[:, _lanes(j)], delta)
            st_do[j] = dout
            st_stat[j] = stat
            for g in range(3):
                if dils[g] == 1:
                    outs[g][0, :, _lanes(j)] = dout.astype(BF16)
                    outs[3 + g][0, :, _lanes(j)] = stat
        for g in range(3):
            if dils[g] > 1:
                _deinterleave(st_do, outs[g], dils[g], RT // dils[g], BF16)
                _deinterleave(st_stat, outs[3 + g], dils[g], RT // dils[g], F32)

    nspec = pl.BlockSpec((RT, DSW_HG), lambda i: (i, 0))
    dspec = lambda d: pl.BlockSpec((d, RT // d, DSW_HG), lambda i: (0, i, 0))
    res = pl.pallas_call(
        body, name="dsw_bwd_prep", grid=(nt,),
        in_specs=[nspec] * 3, out_specs=[dspec(d) for d in dils] * 2,
        out_shape=[SDS((d, S // d, DSW_HG), BF16) for d in dils] + [SDS((d, S // d, DSW_HG), F32) for d in dils],
        scratch_shapes=[pltpu.VMEM((N_LB, RT, LANES), F32)] * 2, compiler_params=_cp(1),
    )(do, o, lse)
    return res[:3], res[3:]


def dsw_forward(h, w_in, q_gain2, k_gain2, rel_bias, w_out):
    S = h.shape[0]
    proj = matmul(h, w_in, "nn", BF16, "dsw_in", col_shards=N_SHARD)
    bias = dsw_bias(rel_bias)
    qkv, ods, lseds = [], [], []
    for gi in range(3):
        qd, kd, vd = dsw_prep(proj, q_gain2, k_gain2, gi, S)
        od, ld = dsw_attn_fwd(qd, kd, vd, bias, gi, S)
        qkv.append((qd, kd, vd))
        ods.append(od)
        lseds.append(ld)
    o, lse = dsw_combine(ods, lseds, S)
    y = matmul(o, w_out, "nn", BF16, "dsw_out", col_shards=N_SHARD)
    return y, dict(h=h, proj=proj, qkv=qkv, bias=bias, o=o, lse=lse)


def dsw_backward(dy, sv, w_in, q_gain2, k_gain2, w_out):
    S = dy.shape[0]
    do = matmul(dy, w_out, "nt", F32, "dsw_out_dx", col_shards=N_SHARD)
    d_w_out = matmul(sv["o"], dy, "tn", F32, "dsw_out_dw", col_shards=N_SHARD)
    dods, statds = dsw_bwd_prep(do, sv["o"], sv["lse"], S)
    pieces_q, pieces_k, pieces_v, dbs = [], [], [], []
    d_qg = jnp.zeros((1, LANES), F32)
    d_kg = jnp.zeros((1, LANES), F32)
    for gi in range(3):
        qd, kd, vd = sv["qkv"][gi]
        dqd, dkd, dvd, db = dsw_attn_bwd(qd, kd, vd, sv["bias"], dods[gi], statds[gi], gi, S)
        dq, dk, dv, dqg, dkg = dsw_prep_bwd(sv["proj"], q_gain2, k_gain2, dqd, dkd, dvd, gi, S)
        dbs.append(db)
        pieces_q.append(dq)
        pieces_k.append(dk)
        pieces_v.append(dv)
        d_qg = d_qg + dqg
        d_kg = d_kg + dkg
    dproj = jnp.concatenate(pieces_q + pieces_k + pieces_v, axis=1)
    d_w_in = matmul(sv["h"], dproj, "tn", F32, "dsw_in_dw", col_shards=N_SHARD)
    dh = matmul(dproj, w_in, "nt", BF16, "dsw_in_dx", col_shards=N_SHARD)
    d_rel = dsw_bias_grad(jnp.concatenate(dbs, axis=0))
    return dh, dict(w_in=d_w_in, q_gain2=d_qg, k_gain2=d_kg, rel=d_rel, w_out=d_w_out)


FUSE_M = 512


def ffn_in_act(h, w_in, name):
    S = h.shape[0]
    half = FFN // 2

    def body(h_ref, wg_ref, wu_ref, gu_ref, a_ref):
        j = pl.program_id(1)
        sub = FUSE_M // 2
        for part in range(2):
            rows = slice(part * sub, (part + 1) * sub)
            hb = h_ref[rows, :]
            g = jnp.dot(hb, wg_ref[...], preferred_element_type=F32)
            u = jnp.dot(hb, wu_ref[...], preferred_element_type=F32)
            a_ref[rows, :] = (_silu(g) * u).astype(a_ref.dtype)
            for jj in range(2):
                @pl.when(j == jj)
                def _(g=g, u=u, jj=jj, rows=rows):
                    gu_ref[rows, jj * half:(jj + 1) * half] = g.astype(gu_ref.dtype)
                    gu_ref[rows, FFN + jj * half:FFN + (jj + 1) * half] = u.astype(gu_ref.dtype)

    return pl.pallas_call(
        body, name=name, grid=(S // FUSE_M, 2),
        in_specs=[pl.BlockSpec((FUSE_M, D), lambda i, j: (i, 0)),
                  pl.BlockSpec((None, D, half), lambda i, j: (j, 0, 0)),
                  pl.BlockSpec((None, D, half), lambda i, j: (j + 2, 0, 0))],
        out_specs=[pl.BlockSpec((FUSE_M, 2 * FFN), lambda i, j: (i, 0)), pl.BlockSpec((FUSE_M, half), lambda i, j: (i, j))],
        out_shape=[SDS((S, 2 * FFN), BF16), SDS((S, FFN), BF16)],
        compiler_params=_cp(2),
    )(h, w_in, w_in)


def ffn_forward(h, w_in, w_out, tag):
    gu, a = ffn_in_act(h, w_in, f"ffn_in_act_{tag}")
    f = matmul(a, w_out, "nn", BF16, f"ffn_out_{tag}")
    return f, dict(h=h, gu=gu, a=a)


def ffn_out_dx_act(df, w_out, gu, name):
    S = df.shape[0]
    half = FFN // 2

    def body(df_ref, w_ref, g_ref, u_ref, dgu_ref):
        j = pl.program_id(1)
        sub = FUSE_M // 2
        for part in range(2):
            rows = slice(part * sub, (part + 1) * sub)
            da = _nt(df_ref[rows, :], w_ref[...])
            dg, du = _swiglu_bwd((g_ref[rows, :].astype(F32), u_ref[rows, :].astype(F32)), da)
            for jj in range(2):
                @pl.when(j == jj)
                def _(dg=dg, du=du, jj=jj, rows=rows):
                    dgu_ref[rows, jj * half:(jj + 1) * half] = dg.astype(dgu_ref.dtype)
                    dgu_ref[rows, FFN + jj * half:FFN + (jj + 1) * half] = du.astype(dgu_ref.dtype)

    return pl.pallas_call(
        body, name=name, grid=(S // FUSE_M, 2),
        in_specs=[pl.BlockSpec((FUSE_M, D), lambda i, j: (i, 0)),
                  pl.BlockSpec((half, D), lambda i, j: (j, 0)),
                  pl.BlockSpec((FUSE_M, half), lambda i, j: (i, j)),
                  pl.BlockSpec((FUSE_M, half), lambda i, j: (i, j + 2))],
        out_specs=pl.BlockSpec((FUSE_M, 2 * FFN), lambda i, j: (i, 0)),
        out_shape=SDS((S, 2 * FFN), BF16),
        compiler_params=_cp(2),
    )(df, w_out, gu, gu)


def ffn_backward(df, sv, w_in, w_out, tag):
    d_w_out = matmul(sv["a"], df, "tn", F32, f"ffn_out_dw_{tag}")
    dgu = ffn_out_dx_act(df, w_out, sv["gu"], f"ffn_out_dx_act_{tag}")
    d_w_in = matmul(sv["h"], dgu, "tn", F32, f"ffn_in_dw_{tag}", col_shards=N_SHARD)
    dh = matmul(dgu, w_in, "nt", BF16, f"ffn_in_dx_{tag}", col_shards=N_SHARD)
    return dh, d_w_in, d_w_out


def f_norm_only(ids, x, gain, sc, sh):
    return (_normmod(x, gain, sc, sh),)


WT = 512


def _wide(a, **kw):
    return Row(a, (WT, D), lambda i: (i, 0), **kw)


def _wide_out(S, dtype):
    return Out((S, D), dtype, (WT, D), lambda i: (i, 0))


def adamw(w, g, m, v, name):
    shape = w.shape
    C = shape[-1]
    R = int(np.prod(shape[:-1]))
    w2, g2, m2, v2 = (a.reshape(R, C) for a in (w, g, m, v))
    br = R
    if R > 256:
        br = max(b for b in range(8, 257, 8) if R % b == 0)
    c1 = 1.0 / (1.0 - ADAM_B1 ** ADAM_STEP)
    c2 = 1.0 / (1.0 - ADAM_B2 ** ADAM_STEP)

    def body(w_ref, g_ref, m_ref, v_ref, d_ref, nm_ref, nv_ref):
        gg = g_ref[...]
        mm_ = ADAM_B1 * m_ref[...] + (1.0 - ADAM_B1) * gg
        vv = ADAM_B2 * v_ref[...] + (1.0 - ADAM_B2) * (gg * gg)
        d_ref[...] = -ADAM_LR * ((mm_ * c1) / (jnp.sqrt(vv * c2) + ADAM_EPS) + ADAM_WD * w_ref[...])
        nm_ref[...] = mm_
        nv_ref[...] = vv

    spec = pl.BlockSpec((br, C), lambda i: (i, 0))
    d, nm, nv = pl.pallas_call(
        body, name=name, grid=(R // br,), in_specs=[spec] * 4, out_specs=[spec] * 3,
        out_shape=[SDS((R, C), F32)] * 3, compiler_params=_cp(1),
    )(w2, g2, m2, v2)
    return d.reshape(shape), nm.reshape(shape), nv.reshape(shape)


def _place():
    x, y, c = lax.axis_index("x"), lax.axis_index("y"), lax.axis_index("c")
    chips = [(1 - x, y), (x, 1 - y), (1 - x, 1 - y)]
    return x, y, c, chips


def all_gather_small(blk, name):
    m_per, n = blk.shape

    def body(x_ref, out_ref, send_sems, recv_sems, local_sem):
        x, y, c, chips = _place()
        me, sibling = (x, y, c), (x, y, 1 - c)

        def rows(px, py, pc):
            return out_ref.at[pl.ds((4 * px + 2 * py + pc) * m_per, m_per), :]

        def copy(k, block, to, src=None):
            return pltpu.make_async_remote_copy(
                src_ref=rows(*block) if src is None else src, dst_ref=rows(*block),
                send_sem=send_sems.at[k], recv_sem=recv_sems.at[k], device_id=to, device_id_type=MESH)

        mine = pltpu.make_async_copy(x_ref, rows(*me), local_sem)
        mine.start()
        first = [copy(0, me, sibling, src=x_ref)]
        first += [copy(1 + j, me, (*chip, c), src=x_ref) for j, chip in enumerate(chips)]
        for cp in first:
            cp.start()
        passed = [copy(4 + j, (*chip, c), sibling) for j, chip in enumerate(chips)]
        for j, chip in enumerate(chips):
            copy(1 + j, (*chip, c), me).wait_recv()
            passed[j].start()
        copy(0, sibling, me).wait_recv()
        for j, chip in enumerate(chips):
            copy(4 + j, (*chip, 1 - c), me).wait_recv()
        for cp in first + passed:
            cp.wait_send()
        mine.wait()

    return pl.pallas_call(
        body, name=name, out_shape=SDS((N_DEV * m_per, n), blk.dtype),
        in_specs=[pl.BlockSpec(memory_space=pltpu.VMEM)], out_specs=pl.BlockSpec(memory_space=pltpu.VMEM),
        scratch_shapes=[pltpu.SemaphoreType.DMA((7,)), pltpu.SemaphoreType.DMA((7,)), pltpu.SemaphoreType.DMA],
    )(blk)


def _half(cc, rh):
    return pl.ds(pl.multiple_of(cc * rh, 16), rh)


def all_gather_shards(ws):
    n = len(ws)

    def body(*refs):
        w_refs, out_refs = refs[:n], refs[n:2 * n]
        send_sems, recv_sems, local_sems, own_sems = refs[2 * n:]
        x, y, c, chips = _place()
        sibling = (x, y, 1 - c)
        s_me = 2 * x + y

        def copy(k, src, dst, to):
            return pltpu.make_async_remote_copy(src_ref=src, dst_ref=dst, send_sem=send_sems.at[k], recv_sem=recv_sems.at[k],
                                                device_id=to, device_id_type=MESH)

        local, sends, passed = [], [], []
        for k in range(n):
            rh = ws[k].shape[0] // 2
            cp = pltpu.make_async_remote_copy(src_ref=w_refs[k], dst_ref=out_refs[k].at[s_me], send_sem=local_sems.at[k],
                                              recv_sem=own_sems.at[k], device_id=sibling, device_id_type=MESH)
            cp.start()
            local.append(cp)
            for j, chip in enumerate(chips):
                sd = copy(6 * k + j, w_refs[k].at[_half(c, rh)], out_refs[k].at[s_me, _half(c, rh)], (*chip, c))
                sd.start()
                sends.append(sd)
        for k in range(n):
            rh = ws[k].shape[0] // 2
            for j, (px, py) in enumerate(chips):
                got = out_refs[k].at[2 * px + py, _half(c, rh)]
                copy(6 * k + j, got, got, (px, py, c)).wait_recv()
                fw = copy(6 * k + 3 + j, got, got, sibling)
                fw.start()
                passed.append(fw)
        for k in range(n):
            rh = ws[k].shape[0] // 2
            for j, (px, py) in enumerate(chips):
                got = out_refs[k].at[2 * px + py, _half(1 - c, rh)]
                copy(6 * k + 3 + j, got, got, sibling).wait_recv()
        for cp in sends + passed:
            cp.wait_send()
        for cp in local:
            cp.wait()

    return pl.pallas_call(
        body, name="weights_all_gather", out_shape=[SDS((N_SHARD,) + w.shape, w.dtype) for w in ws],
        in_specs=[ANY] * n, out_specs=[ANY] * n,
        scratch_shapes=[pltpu.SemaphoreType.DMA((6 * n,)), pltpu.SemaphoreType.DMA((6 * n,)), pltpu.SemaphoreType.DMA((n,)),
                        pltpu.SemaphoreType.DMA((n,))],
    )(*ws)


def sibling_exchange(sends, name):
    n = len(sends)

    def body(*refs):
        s_refs, o_refs, send_sems, recv_sems = refs[:n], refs[n:2 * n], refs[2 * n], refs[2 * n + 1]
        x, y, c, _ = _place()
        cps = [pltpu.make_async_remote_copy(src_ref=s_refs[k], dst_ref=o_refs[k], send_sem=send_sems.at[k], recv_sem=recv_sems.at[k],
                                            device_id=(x, y, 1 - c), device_id_type=MESH) for k in range(n)]
        for cp in cps:
            cp.start()
        for cp in cps:
            cp.wait()

    return pl.pallas_call(
        body, name=name, out_shape=[SDS(s.shape, s.dtype) for s in sends], in_specs=[ANY] * n, out_specs=[ANY] * n,
        scratch_shapes=[pltpu.SemaphoreType.DMA((n,)), pltpu.SemaphoreType.DMA((n,))],
    )(*sends)


def add_rows(arrs, out_dtype, name, rt=256):
    Rr, W = arrs[0].shape

    def fn(ids, *vals):
        acc = vals[0]
        for v in vals[1:]:
            acc = acc + v
        return (acc,)

    t = rt if Rr % rt == 0 else max(b for b in range(16, rt + 1, 16) if Rr % b == 0)
    (out,) = rowwise(fn, [Row(a, (t, W), lambda i: (i, 0)) for a in arrs], [],
                     [Out((Rr, W), out_dtype, (t, W), lambda i: (i, 0))], (Rr // t,), name)
    return out


HBM_SPEC = pl.BlockSpec(memory_space=pltpu.HBM)
SEM_SPEC = pl.BlockSpec(memory_space=pltpu.SEMAPHORE)
DATAFLOW = pltpu.SideEffectType.DATAFLOW_SIDE_EFFECTING


def _in_hbm(a):
    return pltpu.with_memory_space_constraint(a, pltpu.HBM)


def _gather_copies(w_refs, land_refs, send_sems, recv_sems):
    x, y, c, chips = _place()
    targets = [(x, y, 1 - c)] + [(*chip, c) for chip in chips]
    cps = []
    for k, (w_ref, land_ref) in enumerate(zip(w_refs, land_refs)):
        for j, to in enumerate(targets):
            cps.append(pltpu.make_async_remote_copy(src_ref=w_ref, dst_ref=land_ref.at[2 * x + y], send_sem=send_sems.at[4 * k + j],
                                                    recv_sem=recv_sems.at[4 * k + j], device_id=to, device_id_type=MESH))
    return cps


def _scatter_copies(p_refs, land_refs, send_sems, recv_sems):
    x, y, c, chips = _place()
    cps = []
    for k, (p_ref, land_ref) in enumerate(zip(p_refs, land_refs)):
        for j, (px, py) in enumerate(chips):
            cps.append(pltpu.make_async_remote_copy(src_ref=p_ref.at[2 * px + py], dst_ref=land_ref.at[j], send_sem=send_sems.at[3 * k + j],
                                                    recv_sem=recv_sems.at[3 * k + j], device_id=(px, py, c), device_id_type=MESH))
    return cps


def copies_start(srcs, land_shapes, make_copies, per_src, name):
    n = len(srcs)
    m = per_src * n

    def body(*refs):
        src_refs, land_refs = refs[:n], refs[n:2 * n]
        send_sems, recv_sems, token = refs[2 * n], refs[2 * n + 1], refs[-1]
        for cp in make_copies(src_refs, land_refs, send_sems, recv_sems):
            cp.start()
        token[...] = jnp.zeros_like(token)

    lands = [lax.empty(shp, s.dtype) for shp, s in zip(land_shapes, srcs)]
    res = pl.pallas_call(
        body, name=name,
        out_shape=(pltpu.SemaphoreType.DMA((m,)), pltpu.SemaphoreType.DMA((m,)), *[pltpu.HBM(s.shape, s.dtype) for s in srcs],
                   *[pltpu.HBM(shp, s.dtype) for shp, s in zip(land_shapes, srcs)], SDS((8, LANES), F32)),
        in_specs=[HBM_SPEC] * (2 * n),
        out_specs=(SEM_SPEC, SEM_SPEC, *[HBM_SPEC] * (2 * n), pl.BlockSpec(memory_space=pltpu.VMEM)),
        input_output_aliases={i: 2 + i for i in range(2 * n)},
        compiler_params=pltpu.CompilerParams(has_side_effects=DATAFLOW),
    )(*[_in_hbm(s) for s in srcs], *[_in_hbm(l) for l in lands])
    return res[0], res[1], list(res[2:2 + n]), list(res[2 + n:2 + 2 * n]), res[-1]


def copies_wait(send_sems, recv_sems, srcs, lands, make_copies, after, name):
    n = len(srcs)

    def body(*refs):
        src_refs, land_refs = refs[:n], refs[n:2 * n]
        for cp in make_copies(src_refs, land_refs, refs[2 * n], refs[2 * n + 1]):
            cp.wait_send()
            cp.wait_recv()

    res = pl.pallas_call(
        body, name=name,
        out_shape=(*[pltpu.HBM(s.shape, s.dtype) for s in srcs], *[pltpu.HBM(l.shape, l.dtype) for l in lands]),
        in_specs=[HBM_SPEC] * (2 * n) + [SEM_SPEC, SEM_SPEC, ANY],
        out_specs=tuple([HBM_SPEC] * (2 * n)),
        input_output_aliases={i: i for i in range(2 * n)},
        compiler_params=pltpu.CompilerParams(has_side_effects=DATAFLOW),
    )(*srcs, *lands, send_sems, recv_sems, after)
    return list(res[n:])


def _pad_lanes(v):
    return jnp.concatenate([v.astype(F32), jnp.zeros((LANES - v.shape[0],), F32)])[None]


def kernel(x, c, w_ada, b_ada, norm_mix, norm_ffn, w_ffn_in, w_ffn_out, gdn_w_in, gdn_conv, gdn_a_log, gdn_dt_bias, gdn_out_norm, gdn_w_out, dsw_w_in, dsw_q_norm, dsw_k_norm, dsw_w_out, rel_bias, loss_target, m_w_ada, m_b_ada, m_norm_mix, m_norm_ffn, m_w_ffn_in, m_w_ffn_out, m_gdn_w_in, m_gdn_conv, m_gdn_a_log, m_gdn_dt_bias, m_gdn_out_norm, m_gdn_w_out, m_dsw_w_in, m_dsw_q_norm, m_dsw_k_norm, m_dsw_w_out, m_rel_bias, v_w_ada, v_b_ada, v_norm_mix, v_norm_ffn, v_w_ffn_in, v_w_ffn_out, v_gdn_w_in, v_gdn_conv, v_gdn_a_log, v_gdn_dt_bias, v_gdn_out_norm, v_gdn_w_out, v_dsw_w_in, v_dsw_q_norm, v_dsw_k_norm, v_dsw_w_out, v_rel_bias):
    S = x.shape[1]
    nt = S // WT
    xi, yi, ci = lax.axis_index("x"), lax.axis_index("y"), lax.axis_index("c")
    me = 4 * xi + 2 * yi + ci
    s_me = 2 * xi + yi
    x0, tgt = x[0], loss_target[0]

    whole = lambda a: Row(a, a.shape, lambda i: (0,) * a.ndim)
    (cond8,) = rowwise(lambda ids, v: (_silu(v),), [whole(c.reshape(8, LANES))], [], [Out((8, LANES), F32, (8, LANES), lambda i: (0, 0))], (1,), "cond")
    cond_all = all_gather_small(cond8, "gather_cond").reshape(N_DEV, D)
    cond16 = jnp.concatenate([cond_all, jnp.zeros((8, D), F32)], axis=0)
    ada_cols = w_ada.shape[2]
    mods = [matmul(cond16, w_ada[l], "nn", F32, f"ada_{l}")[:N_DEV] for l in range(2)]
    buf = jnp.concatenate([jnp.stack(mods, axis=1).reshape(-1, LANES), gdn_conv.reshape(-1, LANES)], axis=0)
    n_mod_rows = N_DEV * 2 * ada_cols // LANES
    got = all_gather_small(buf, "gather_mod").reshape(N_DEV, buf.shape[0], LANES)
    mod_parts, conv_parts = [], []
    for s in range(N_SHARD):
        from_dev = got[2 * s]
        mod_parts.append(lax.dynamic_index_in_dim(from_dev[:n_mod_rows].reshape(N_DEV, 2, ada_cols), me, 0, keepdims=False))
        conv_parts.append(from_dev[n_mod_rows:].reshape(4, -1))
    mod_nb = jnp.concatenate(mod_parts, axis=1)
    conv_w = jnp.concatenate(conv_parts, axis=1)
    (mod,) = rowwise(lambda ids, a, b: (a + b,), [whole(mod_nb), whole(b_ada)], [], [Out(mod_nb.shape, F32, mod_nb.shape, lambda i: (0, 0))], (1,), "mod_bias")
    mod = mod.reshape(2, 6, 1, D)
    sh1, sc1, g1, sh2, sc2, g2 = ([mod[l, k] for l in range(2)] for k in range(6))
    gmix = [norm_mix[l][None] for l in range(2)]
    gffn = [norm_ffn[l][None] for l in range(2)]

    gcols = gdn_w_in.shape[2]
    g_gdn_in, g_gdn_out = all_gather_shards([gdn_w_in[0].astype(BF16), gdn_w_out[0].astype(BF16)])
    gathered = lambda ws: [(N_SHARD,) + w.shape for w in ws]
    gate = (jnp.minimum(jnp.abs(g_gdn_in[0, 0, 0].astype(F32)), 0.0) + jnp.minimum(jnp.abs(mod[0, 0, 0, 0]), 0.0)).astype(BF16)
    w2 = [w_ffn_in[0].astype(BF16) + gate, w_ffn_out[0].astype(BF16) + gate]
    w3 = [dsw_w_in[0].astype(BF16) + gate, dsw_w_out[0].astype(BF16) + gate, w_ffn_in[1].astype(BF16) + gate, w_ffn_out[1].astype(BF16) + gate]
    fly2 = copies_start(w2, gathered(w2), _gather_copies, 4, "weights_ffn0_start")
    fly3 = copies_start(w3, gathered(w3), _gather_copies, 4, "weights_layer1_start")
    started = fly2[4][0, 0] + fly3[4][0, 0]
    w_gdn = jnp.concatenate([g_gdn_in[s] for s in range(N_SHARD)] + [jnp.zeros((D, GDN_PROJ - N_SHARD * gcols), BF16)], axis=1)
    alog, dtb = _pad_lanes(gdn_a_log[0]), _pad_lanes(gdn_dt_bias[0])
    qg2 = jnp.concatenate([dsw_q_norm, dsw_q_norm], axis=1)
    kg2 = jnp.concatenate([dsw_k_norm, dsw_k_norm], axis=1)
    w_gdn_out = g_gdn_out.reshape(GDN_H * LANES, D)
    gdn_args = (w_gdn, conv_w, alog, dtb, gdn_out_norm, w_gdn_out)
    sc1[0] = sc1[0] + started

    (h10,) = rowwise(f_norm_only, [_wide(x0)], [gmix[0], sc1[0], sh1[0]], [_wide_out(S, BF16)], (nt,), "l0_norm")
    y0, sv_g = gdn_forward(h10, *gdn_args)
    x1, h20 = rowwise(f_resid_norm, [_wide(x0), _wide(y0)], [g1[0], gffn[0], sc2[0], sh2[0]], [_wide_out(S, F32), _wide_out(S, BF16)], (nt,), "l0_mid")
    g_in0, g_out0 = copies_wait(*fly2[:4], _gather_copies, y0, "weights_ffn0_wait")
    w_ffn = [(g_in0, g_out0.reshape(FFN, D)), None]
    f0, sv_f0 = ffn_forward(h20, *w_ffn[0], "0")
    x2, h11 = rowwise(f_resid_norm, [_wide(x1), _wide(f0)], [g2[0], gmix[1], sc1[1], sh1[1]], [_wide_out(S, F32), _wide_out(S, BF16)], (nt,), "l1_in")
    g_dsw_in, g_dsw_out, g_in1, g_out1 = copies_wait(*fly3[:4], _gather_copies, f0, "weights_layer1_wait")
    w_ffn[1] = (g_in1, g_out1.reshape(FFN, D))
    dsw_args = (g_dsw_in, qg2, kg2)
    y1, sv_d = dsw_forward(h11, *dsw_args, rel_bias, g_dsw_out)
    x3, h21 = rowwise(f_resid_norm, [_wide(x2), _wide(y1)], [g1[1], gffn[1], sc2[1], sh2[1]], [_wide_out(S, F32), _wide_out(S, BF16)], (nt,), "l1_mid")
    f1, sv_f1 = ffn_forward(h21, *w_ffn[1], "1")
    parts, dx3, df1, dg2_1 = loss_and_grad(x3, f1, tgt, g2[1], S)
    loss = lax.psum(jnp.sum(parts), ("x", "y", "c"))

    dh21, d_win1, d_wout1 = ffn_backward(df1, sv_f1, *w_ffn[1], "1")
    (dx2, dy1), (dg1_1, dgf1, dsc2_1, dsh2_1) = rowwise_bwd(
        f_resid_norm, [_wide(x2), _wide(y1, gdtype=BF16)], [g1[1], gffn[1], sc2[1], sh2[1]], [_wide(dx3), _wide(dh21)], (nt,), "l1_mid_bwd")
    dh11, g_d = dsw_backward(dy1, sv_d, *dsw_args, g_dsw_out)
    (dx1, df0), (dg2_0, dgm1, dsc1_1, dsh1_1) = rowwise_bwd(
        f_resid_norm, [_wide(x1), _wide(f0, gdtype=BF16)], [g2[0], gmix[1], sc1[1], sh1[1]], [_wide(dx2), _wide(dh11)], (nt,), "l1_in_bwd")
    by_shard = lambda a: a.reshape(N_SHARD, a.shape[0] // N_SHARD, a.shape[1])
    landing = lambda ps: [(3,) + p.shape[1:] for p in ps]
    dws3 = [g_d["w_in"], g_d["w_out"], d_win1, by_shard(d_wout1)]
    parts3 = [a.astype(BF16) for a in dws3]
    gfly3 = copies_start(parts3, landing(parts3), _scatter_copies, 3, "grads_layer1_start")
    w_out0 = w_ffn[0][1] + gfly3[4][0, 0].astype(BF16)
    dh20, d_win0, d_wout0 = ffn_backward(df0, sv_f0, w_ffn[0][0], w_out0, "0")
    (dx0p, dy0), (dg1_0, dgf0, dsc2_0, dsh2_0) = rowwise_bwd(
        f_resid_norm, [_wide(x0), _wide(y0, gdtype=BF16)], [g1[0], gffn[0], sc2[0], sh2[0]], [_wide(dx1), _wide(dh20)], (nt,), "l0_mid_bwd")
    dws2 = [d_win0, by_shard(d_wout0)]
    parts2 = [a.astype(BF16) for a in dws2]
    gfly2 = copies_start(parts2, landing(parts2), _scatter_copies, 3, "grads_ffn0_start")
    gdn_args = gdn_args[:5] + (w_gdn_out + gfly2[4][0, 0].astype(BF16),)
    gdn_flight = []

    def start_gdn_grads(d_w_in, d_w_out):
        dws1 = [jnp.stack([d_w_in[:, s * gcols:(s + 1) * gcols] for s in range(N_SHARD)]), by_shard(d_w_out)]
        parts1 = [a.astype(BF16) for a in dws1]
        fly = copies_start(parts1, landing(parts1), _scatter_copies, 3, "grads_gdn_start")
        gdn_flight.extend([dws1, fly])
        return fly[4][0, 0]

    dh10, g_g = gdn_backward(dy0, sv_g, *gdn_args, on_weight_grads=start_gdn_grads)
    dws1, gfly1 = gdn_flight
    (grad_x,), (dgm0, dsc1_0, dsh1_0) = rowwise_bwd(f_first, [_wide(x0)], [gmix[0], sc1[0], sh1[0]], [_wide(dx0p), _wide(dh10)], (nt,), "l0_norm_bwd")

    dmod = jnp.concatenate([dsh1_0, dsc1_0, dg1_0, dsh2_0, dsc2_0, dg2_0, dsh1_1, dsc1_1, dg1_1, dsh2_1, dsc2_1, dg2_1], axis=1)
    d_rel = jnp.transpose(g_d["rel"][:, :, 0])
    fold = lambda v: v[:, :DSW_DH] + v[:, DSW_DH:]
    small = [dmod, jnp.concatenate([dgm0, dgm1], axis=1), jnp.concatenate([dgf0, dgf1], axis=1), g_g["conv"].reshape(1, -1),
             g_g["alog"], g_g["dtb"], g_g["gain"], _pad_lanes(fold(g_d["q_gain2"])[0]), _pad_lanes(fold(g_d["k_gain2"])[0]),
             d_rel.reshape(1, -1)]
    used = [v.shape[1] // LANES for v in small]
    sizes = [-(-u // 8) * 8 for u in used]
    pad8 = lambda v, u, s: jnp.concatenate([v.reshape(u, LANES), jnp.zeros((s - u, LANES), F32)], axis=0) if s > u else v.reshape(u, LANES)
    pad_rows = sum(sizes)
    sbuf = jnp.concatenate([pad8(v, u, s) for v, u, s in zip(small, used, sizes)], axis=0)
    sgot = all_gather_small(sbuf, "gather_small_grads")
    ssum = add_rows([sgot[d * pad_rows:(d + 1) * pad_rows] for d in range(N_DEV)], F32, "sum_small_grads", rt=pad_rows)
    offs = np.cumsum([0] + sizes)
    take = lambda k: ssum[offs[k]:offs[k] + used[k]].reshape(1, -1)
    grad_b_ada = take(0).reshape(2, 6 * D)
    grad_norm_mix = take(1).reshape(2, D)
    grad_norm_ffn = take(2).reshape(2, D)
    conv_full = take(3).reshape(4, -1)
    ncv = gdn_conv.shape[2]
    grad_gdn_conv = lax.dynamic_slice_in_dim(conv_full, s_me * ncv, ncv, axis=1)[None]
    grad_a_log = take(4)[:, :GDN_H]
    grad_dt_bias = take(5)[:, :GDN_H]
    grad_out_norm = take(6)
    grad_q_norm = take(7)[:, :DSW_DH]
    grad_k_norm = take(8)[:, :DSW_DH]
    grad_rel = take(9).reshape(REL_BUCKETS, 3 * GDN_H)
    dmod_all = sgot.reshape(N_DEV, pad_rows, LANES)[:, :used[0]].reshape(N_DEV, 2, 6 * D)
    dmod_mine = lax.dynamic_slice_in_dim(dmod_all, s_me * ada_cols, ada_cols, axis=2)
    dmod16 = jnp.concatenate([dmod_mine, jnp.zeros_like(dmod_mine)], axis=0)
    grad_w_ada = jnp.stack([matmul(cond16, dmod16[:, l], "tn", F32, f"ada_dw_{l}") for l in range(2)])

    got3 = copies_wait(*gfly3[:4], _scatter_copies, grad_x, "grads_layer1_wait")
    got2 = copies_wait(*gfly2[:4], _scatter_copies, grad_x, "grads_ffn0_wait")
    got1 = copies_wait(*gfly1[:4], _scatter_copies, grad_x, "grads_gdn_wait")
    core_sums = []
    for i, (full, got) in enumerate(zip(dws3 + dws2 + dws1, got3 + got2 + got1)):
        own = lax.dynamic_index_in_dim(full, s_me, 0, keepdims=False)
        core_sums.append(add_rows([own, got[0], got[1], got[2]], F32, f"grads_core_sum_{i}"))
    sib_sums = sibling_exchange(core_sums, "grads_core_sums_swap")
    s_dsw_in, s_dsw_out, s_in1, s_out1, s_in0, s_out0, s_gdn_in, s_gdn_out = [
        add_rows([a, b], F32, f"grads_chip_total_{i}") for i, (a, b) in enumerate(zip(core_sums, sib_sums))]
    gsh = dict(gdn_w_in=s_gdn_in[None], gdn_w_out=s_gdn_out[None],
               w_ffn_in=jnp.stack([s_in0, s_in1]), w_ffn_out=jnp.stack([s_out0, s_out1]),
               dsw_w_in=s_dsw_in[None], dsw_w_out=s_dsw_out[None])

    grads = dict(w_ada=grad_w_ada, b_ada=grad_b_ada, norm_mix=grad_norm_mix, norm_ffn=grad_norm_ffn, w_ffn_in=gsh["w_ffn_in"],
                 w_ffn_out=gsh["w_ffn_out"], gdn_w_in=gsh["gdn_w_in"], gdn_conv=grad_gdn_conv, gdn_a_log=grad_a_log,
                 gdn_dt_bias=grad_dt_bias, gdn_out_norm=grad_out_norm, gdn_w_out=gsh["gdn_w_out"], dsw_w_in=gsh["dsw_w_in"],
                 dsw_q_norm=grad_q_norm, dsw_k_norm=grad_k_norm, dsw_w_out=gsh["dsw_w_out"], rel_bias=grad_rel)
    weights = dict(w_ada=w_ada, b_ada=b_ada, norm_mix=norm_mix, norm_ffn=norm_ffn, w_ffn_in=w_ffn_in, w_ffn_out=w_ffn_out,
                   gdn_w_in=gdn_w_in, gdn_conv=gdn_conv, gdn_a_log=gdn_a_log, gdn_dt_bias=gdn_dt_bias, gdn_out_norm=gdn_out_norm,
                   gdn_w_out=gdn_w_out, dsw_w_in=dsw_w_in, dsw_q_norm=dsw_q_norm, dsw_k_norm=dsw_k_norm, dsw_w_out=dsw_w_out,
                   rel_bias=rel_bias)
    ms = dict(w_ada=m_w_ada, b_ada=m_b_ada, norm_mix=m_norm_mix, norm_ffn=m_norm_ffn, w_ffn_in=m_w_ffn_in, w_ffn_out=m_w_ffn_out,
              gdn_w_in=m_gdn_w_in, gdn_conv=m_gdn_conv, gdn_a_log=m_gdn_a_log, gdn_dt_bias=m_gdn_dt_bias, gdn_out_norm=m_gdn_out_norm,
              gdn_w_out=m_gdn_w_out, dsw_w_in=m_dsw_w_in, dsw_q_norm=m_dsw_q_norm, dsw_k_norm=m_dsw_k_norm, dsw_w_out=m_dsw_w_out,
              rel_bias=m_rel_bias)
    vs = dict(w_ada=v_w_ada, b_ada=v_b_ada, norm_mix=v_norm_mix, norm_ffn=v_norm_ffn, w_ffn_in=v_w_ffn_in, w_ffn_out=v_w_ffn_out,
              gdn_w_in=v_gdn_w_in, gdn_conv=v_gdn_conv, gdn_a_log=v_gdn_a_log, gdn_dt_bias=v_gdn_dt_bias, gdn_out_norm=v_gdn_out_norm,
              gdn_w_out=v_gdn_w_out, dsw_w_in=v_dsw_w_in, dsw_q_norm=v_dsw_q_norm, dsw_k_norm=v_dsw_k_norm, dsw_w_out=v_dsw_w_out,
              rel_bias=v_rel_bias)
    names = list(weights)
    deltas, new_m, new_v = [], [], []
    for n in names:
        g = grads[n].reshape(weights[n].shape)
        grads[n] = g
        d, nm, nv = adamw(weights[n], g, ms[n], vs[n], f"adamw_{n}")
        deltas.append(d)
        new_m.append(nm)
        new_v.append(nv)
    return (loss, grad_x[None], *[grads[n] for n in names], *deltas, *new_m, *new_v)
```

```python
import functools
import math

import numpy as np
import jax
import jax.numpy as jnp
from jax import lax
from jax.experimental import pallas as pl
from jax.experimental.pallas import tpu as pltpu

F32 = jnp.float32
BF16 = jnp.bfloat16
SDS = jax.ShapeDtypeStruct
MESH = pl.DeviceIdType.MESH
ANY = pl.BlockSpec(memory_space=pl.ANY)

D = 1024
EPS = 1e-6
LANES = 128
GDN_H = 8
GDN_DK = 128
GDN_C = 64
DSW_GROUPS = ((128, 1), (512, 4), (2048, 16))
DSW_SPAN = 128
DSW_DH = 64
DSW_HG = 512
REL_BUCKETS = 32
REL_MAX_DIST = 2048
FFN = 2816
N_SHARD = 4
N_DEV = 8
VMEM_LIMIT = 48 * 1024 * 1024
NEG = -1e30

ADAM_LR, ADAM_B1, ADAM_B2, ADAM_EPS, ADAM_WD, ADAM_STEP = 0.001, 0.9, 0.999, 1e-08, 0.01, 10


def _cp(n_axes):
    return pltpu.CompilerParams(dimension_semantics=("arbitrary",) * n_axes, vmem_limit_bytes=VMEM_LIMIT)


def _blk(dim, cap):
    if dim <= cap:
        return dim
    best = None
    for b in range(LANES, cap + 1, LANES):
        if dim % b == 0:
            best = b
    assert best is not None, (dim, cap)
    return best


MAX_SHARD_BLOCK = 1408
def matmul(a, b, mode, out_dtype, name, cap_m=MAX_SHARD_BLOCK, cap_n=MAX_SHARD_BLOCK, cap_k=2048, col_shards=0):
    ns = col_shards
    if mode == "nn":
        (M, K) = a.shape
        K2, N = (b.shape[1], ns * b.shape[2]) if ns else b.shape
    elif mode == "nt":
        (M, K) = a.shape
        N, K2 = (b.shape[1], ns * b.shape[2]) if ns else b.shape
    else:
        (K, M), (K2, N) = a.shape, b.shape
    assert K == K2, (a.shape, b.shape, mode)
    if K <= 3072:
        cap_k = K
        if K > 2048:
            cap_n = 1024
    n_unit = N // ns if (ns and mode != "nt") else N
    k_unit = K // ns if (ns and mode == "nt") else K
    bm = _blk(M, cap_m)
    bn = _blk(n_unit, MAX_SHARD_BLOCK) if n_unit != N else _blk(N, cap_n)
    if k_unit != K:
        bk = _blk(k_unit, MAX_SHARD_BLOCK)
    else:
        bk = _blk(K, 1024 if (ns and mode == "tn") else cap_k)
    nk = K // bk
    nps, kps = n_unit // bn, k_unit // bk
    dims = {"nn": ((1,), (0,)), "nt": ((1,), (1,)), "tn": ((0,), (0,))}[mode]

    def dot(a_ref, b_ref):
        return lax.dot_general(a_ref[...].astype(BF16), b_ref[...].astype(BF16), (dims, ((), ())), preferred_element_type=F32)

    def body_one(a_ref, b_ref, o_ref):
        o_ref[...] = dot(a_ref, b_ref).astype(o_ref.dtype)

    def body_acc(a_ref, b_ref, o_ref, acc_ref):
        k = pl.program_id(2)

        @pl.when(k == 0)
        def _():
            acc_ref[...] = jnp.zeros_like(acc_ref)

        acc_ref[...] += dot(a_ref, b_ref)

        @pl.when(k == nk - 1)
        def _():
            o_ref[...] = acc_ref[...].astype(o_ref.dtype)

    a_spec = pl.BlockSpec((bk, bm), lambda i, j, k: (k, i)) if mode == "tn" else pl.BlockSpec((bm, bk), lambda i, j, k: (i, k))
    if mode == "nt":
        b_spec = pl.BlockSpec((None, bn, bk), lambda i, j, k: (k // kps, j, k % kps)) if ns else pl.BlockSpec((bn, bk), lambda i, j, k: (j, k))
    elif mode == "nn" and ns:
        b_spec = pl.BlockSpec((None, bk, bn), lambda i, j, k: (j // nps, k, j % nps))
    else:
        b_spec = pl.BlockSpec((bk, bn), lambda i, j, k: (k, j))
    if mode == "tn" and ns:
        o_spec, o_shape = pl.BlockSpec((None, bm, bn), lambda i, j, k: (j // nps, i, j % nps)), (ns, M, n_unit)
    else:
        o_spec, o_shape = pl.BlockSpec((bm, bn), lambda i, j, k: (i, j)), (M, N)
    return pl.pallas_call(
        body_one if nk == 1 else body_acc, name=name, grid=(M // bm, N // bn, nk),
        in_specs=[a_spec, b_spec], out_specs=o_spec,
        out_shape=SDS(o_shape, out_dtype), scratch_shapes=[] if nk == 1 else [pltpu.VMEM((bm, bn), F32)],
        compiler_params=_cp(3),
    )(a, b)


class Row:
    def __init__(self, arr, bshape, imap, splits=None, diff=True, acc=False, gdtype=F32, gshape=None, gbshape=None, gimap=None,
                 lead=0):
        self.arr, self.bshape, self.imap = arr, tuple(bshape), imap
        self.splits, self.lead = splits, lead
        self.diff, self.acc, self.gdtype = diff, acc, gdtype
        self.gshape = tuple(arr.shape) if gshape is None else tuple(gshape)
        self.gbshape = self.bshape if gbshape is None else tuple(gbshape)
        self.gimap = imap if gimap is None else gimap

    def gspec(self):
        return pl.BlockSpec(self.gbshape, self.gimap)

    def spec(self):
        return pl.BlockSpec(self.bshape, self.imap)

    def pieces(self, ref):
        return _load_pieces(ref, self.splits, self.lead)

    def n_pieces(self):
        return _n_pieces(self.splits, self.lead)


class Out:
    def __init__(self, shape, dtype, bshape, imap, splits=None, lead=0):
        self.shape, self.dtype, self.bshape, self.imap = tuple(shape), dtype, tuple(bshape), imap
        self.splits, self.lead = splits, lead

    def n_pieces(self):
        return _n_pieces(self.splits, self.lead)


def _n_pieces(splits, lead):
    return lead if lead else (1 if splits is None else len(splits))


def _load_pieces(ref, splits, lead):
    if lead:
        return [ref[k].astype(F32) for k in range(lead)]
    if splits is None:
        return [ref[...].astype(F32)]
    out, o = [], 0
    for w in splits:
        out.append(ref[..., o:o + w].astype(F32))
        o += w
    return out


def _store_pieces(ref, splits, lead, vals, accumulate=False):
    def put(idx, v):
        if accumulate:
            ref[idx] += v.astype(ref.dtype)
        else:
            ref[idx] = v.astype(ref.dtype)

    if lead:
        for k in range(lead):
            put(k, vals[k])
    elif splits is None:
        put(..., vals[0])
    else:
        o = 0
        for w, v in zip(splits, vals):
            put((..., slice(o, o + w)), v)
            o += w


def rowwise(fn, rows, params, outs, grid, name):
    nr, npar = len(rows), len(params)

    def body(*refs):
        ids = tuple(pl.program_id(a) for a in range(len(grid)))
        vals = []
        for r, ref in zip(rows, refs[:nr]):
            vals += r.pieces(ref)
        pvals = [ref[...].astype(F32) for ref in refs[nr:nr + npar]]
        res = list(fn(ids, *vals, *pvals))
        o = 0
        for spec, ref in zip(outs, refs[nr + npar:]):
            n = spec.n_pieces()
            _store_pieces(ref, spec.splits, spec.lead, res[o:o + n])
            o += n

    nz = len(grid)
    pspecs = [pl.BlockSpec(p.shape, (lambda *ids, _n=p.ndim: (0,) * _n)) for p in params]
    res = pl.pallas_call(
        body, name=name, grid=grid,
        in_specs=[r.spec() for r in rows] + pspecs,
        out_specs=[pl.BlockSpec(o.bshape, o.imap) for o in outs],
        out_shape=[SDS(o.shape, o.dtype) for o in outs],
        compiler_params=_cp(nz),
    )(*[r.arr for r in rows], *params)
    return list(res)


def rowwise_bwd(fn, rows, params, cots, grid, name):
    nr, npar, nc = len(rows), len(params), len(cots)
    drows = [r for r in rows if r.diff]
    nz = len(grid)

    def body(*refs):
        ids = tuple(pl.program_id(a) for a in range(nz))
        row_refs, par_refs = refs[:nr], refs[nr:nr + npar]
        cot_refs = refs[nr + npar:nr + npar + nc]
        drow_refs = refs[nr + npar + nc:nr + npar + nc + len(drows)]
        dpar_refs = refs[nr + npar + nc + len(drows):]
        pieces, is_diff = [], []
        for r, ref in zip(rows, row_refs):
            ps = r.pieces(ref)
            pieces += ps
            is_diff += [r.diff] * len(ps)
        pvals = [ref[...].astype(F32) for ref in par_refs]
        dvals = [p for p, dflag in zip(pieces, is_diff) if dflag]
        nd = len(dvals)

        def f(*args):
            it = iter(args[:nd])
            full = [next(it) if dflag else p for p, dflag in zip(pieces, is_diff)]
            return tuple(fn(ids, *full, *args[nd:]))

        _, vjp = jax.vjp(f, *dvals, *pvals)
        cvals = []
        for c, ref in zip(cots, cot_refs):
            cvals += c.pieces(ref)
        g = vjp(tuple(cvals))
        o = 0
        first_inner = ids[-1] == 0
        for r, ref in zip(drows, drow_refs):
            n = r.n_pieces()
            gs = g[o:o + n]
            o += n
            if r.acc:
                @pl.when(first_inner)
                def _(ref=ref):
                    ref[...] = jnp.zeros_like(ref)
            _store_pieces(ref, r.splits, r.lead, gs, accumulate=r.acc)
        first = functools.reduce(jnp.logical_and, [i == 0 for i in ids])
        for ref, gp in zip(dpar_refs, g[nd:]):
            @pl.when(first)
            def _(ref=ref):
                ref[...] = jnp.zeros_like(ref)
            ref[...] += gp

    pspecs = [pl.BlockSpec(p.shape, (lambda *ids, _n=p.ndim: (0,) * _n)) for p in params]
    res = pl.pallas_call(
        body, name=name, grid=grid,
        in_specs=[r.spec() for r in rows] + pspecs + [c.spec() for c in cots],
        out_specs=[r.gspec() for r in drows] + pspecs,
        out_shape=[SDS(r.gshape, r.gdtype) for r in drows] + [SDS(p.shape, F32) for p in params],
        compiler_params=_cp(nz),
    )(*[r.arr for r in rows], *params, *[c.arr for c in cots])
    res = list(res)
    return res[:len(drows)], res[len(drows):]


def _sigmoid(x):
    return 0.5 * (jnp.tanh(0.5 * x) + 1.0)


def _silu(x):
    return x * _sigmoid(x)


def _normmod(x, gain, sc, sh):
    inv = lax.rsqrt(jnp.mean(x * x, axis=-1, keepdims=True) + EPS)
    return x * inv * gain * (1.0 + sc) + sh


def f_first(ids, x, gain, sc, sh):
    return x, _normmod(x, gain, sc, sh)


def f_resid_norm(ids, x, y, g, gain, sc, sh):
    xn = x + g * y
    return xn, _normmod(xn, gain, sc, sh)


@jax.custom_vjp
def _swiglu(gate, up):
    return _silu(gate) * up


def _swiglu_fwd(gate, up):
    return _silu(gate) * up, (gate, up)


def _swiglu_bwd(res, da):
    gate, up = res
    s = _sigmoid(gate)
    gs = gate * s
    return da * up * (s + gs * (1.0 - s)), da * gs


_swiglu.defvjp(_swiglu_fwd, _swiglu_bwd)


def loss_and_grad(x, y, tgt, g, S):
    nt = S // WT

    def body(x_ref, y_ref, t_ref, g_ref, part_ref, dx_ref, dy_ref, dg_ref):
        @pl.when(pl.program_id(0) == 0)
        def _():
            dg_ref[...] = jnp.zeros_like(dg_ref)

        yv = y_ref[...].astype(F32)
        gg = g_ref[...]
        e = x_ref[...] + gg * yv - t_ref[...]
        part_ref[...] = 0.5 * jnp.sum(e * e, axis=0, keepdims=True) * (1.0 / D)
        d = e * (1.0 / D)
        dx_ref[...] = d
        dy_ref[...] = (d * gg).astype(dy_ref.dtype)
        dg_ref[...] += jnp.sum(d * yv, axis=0, keepdims=True)

    row = pl.BlockSpec((WT, D), lambda i: (i, 0))
    vec = pl.BlockSpec((1, D), lambda i: (0, 0))
    return pl.pallas_call(
        body, name="loss_and_grad", grid=(nt,), in_specs=[row, row, row, vec],
        out_specs=[pl.BlockSpec((None, 1, D), lambda i: (i, 0, 0)), row, row, vec],
        out_shape=[SDS((nt, 1, D), F32), SDS((S, D), F32), SDS((S, D), BF16), SDS((1, D), F32)],
        compiler_params=_cp(1),
    )(x, y, tgt, g)


def _softplus(x):
    return jnp.maximum(x, 0.0) + jnp.log(1.0 + jnp.exp(-jnp.abs(x)))


def _chunk_tril(T):
    r = lax.broadcasted_iota(jnp.int32, (T, T), 0)
    c = lax.broadcasted_iota(jnp.int32, (T, T), 1)
    return jnp.where((r // GDN_C == c // GDN_C) & (c <= r), 1.0, 0.0).astype(F32)


def _dot_hi(a, b, dims=((1,), (0,))):
    return lax.dot_general(a, b, (dims, ((), ())), precision=lax.Precision.HIGHEST, preferred_element_type=F32)


def _dot_x3(a, b, dims=((1,), (0,))):
    return lax.dot_general(a, b, (dims, ((), ())), precision=lax.Precision.HIGH, preferred_element_type=F32)


def f_gdn_gates(ids, ab, alog, dtb):
    T = ab.shape[0]
    g = -jnp.exp(alog) * _softplus(ab + dtb)
    beta = _sigmoid(ab)
    gcum = _dot_x3(_chunk_tril(T), g)
    row = lax.broadcasted_iota(jnp.int32, (LANES, LANES), 0)
    sel = lambda k: jnp.where(row == k, 1.0, 0.0).astype(F32)
    gcs = [_dot_x3(gcum, sel(h)) for h in range(GDN_H)]
    bts = [_dot_x3(beta, sel(GDN_H + h)) for h in range(GDN_H)]
    return (*gcs, *bts)


def f_gdn_post(ids, *args):
    os_, zs, gain = args[:GDN_H], args[GDN_H:2 * GDN_H], args[2 * GDN_H]
    out = []
    for o, z in zip(os_, zs):
        inv = lax.rsqrt(jnp.mean(o * o, axis=-1, keepdims=True) + EPS)
        out.append(o * inv * gain * _silu(z))
    return tuple(out)


def _qknorm1(x, gain2, scale):
    lane = lax.broadcasted_iota(jnp.int32, x.shape, 1)
    lo = lane < DSW_DH
    x2 = x * x
    s_all = jnp.sum(x2, axis=-1, keepdims=True)
    s_lo = jnp.sum(jnp.where(lo, x2, 0.0), axis=-1, keepdims=True)
    ms = jnp.where(lo, s_lo, s_all - s_lo) * (1.0 / DSW_DH)
    return x * lax.rsqrt(ms + EPS) * (gain2 * scale)


def f_combine(ids, o0, o1, o2, l0, l1, l2):
    m = jnp.maximum(jnp.maximum(l0, l1), l2)
    e0, e1, e2 = jnp.exp(l0 - m), jnp.exp(l1 - m), jnp.exp(l2 - m)
    den = e0 + e1 + e2
    o = (e0 * o0 + e1 * o1 + e2 * o2) / den
    return o, m + jnp.log(den)


GDN_T = 512
HALO = 16


def _conv_pre(xx, w):
    acc = xx * w[3:4, :]
    for j in range(3):
        acc = acc + pltpu.roll(xx, shift=3 - j, axis=0) * w[j:j + 1, :]
    return acc


@jax.custom_vjp
def _qkv_act_core(pre, norm_on, scale):
    s = _silu(pre)
    r = lax.rsqrt(jnp.sum(s * s, axis=-1, keepdims=True) + EPS)
    return jnp.where(norm_on > 0.5, s * r * scale, s)


def _qkv_act_fwd(pre, norm_on, scale):
    return _qkv_act_core(pre, norm_on, scale), (pre, norm_on, scale)


def _qkv_act_bwd(res, dout):
    pre, norm_on, scale = res
    sig = _sigmoid(pre)
    s = pre * sig
    r = lax.rsqrt(jnp.sum(s * s, axis=-1, keepdims=True) + EPS)
    unit = s * r
    dn = dout * scale
    ds = jnp.where(norm_on > 0.5, r * (dn - unit * jnp.sum(dn * unit, axis=-1, keepdims=True)), dout)
    return ds * (sig + s * (1.0 - sig)), jnp.zeros_like(norm_on), jnp.zeros_like(scale)


_qkv_act_core.defvjp(_qkv_act_fwd, _qkv_act_bwd)


def _qkv_act(pre, cidx):
    norm_on = jnp.where(cidx < 2 * GDN_H, 1.0, 0.0).astype(F32)
    scale = jnp.where(cidx < GDN_H, GDN_DK ** -0.5, 1.0).astype(F32)
    return _qkv_act_core(pre, norm_on, scale)


def gdn_pre(proj, conv_w, S):
    nt = S // GDN_T
    hb = GDN_T // HALO

    def body(prev_ref, cur_ref, w_ref, o_ref):
        p, i = pl.program_id(0), pl.program_id(1)
        for h in range(GDN_H):
            cols = slice(LANES * h, LANES * (h + 1))
            prev = jnp.where(i > 0, prev_ref[:, cols].astype(F32), 0.0)
            xx = jnp.concatenate([prev, cur_ref[:, cols].astype(F32)], axis=0)
            pre = _conv_pre(xx, w_ref[:, cols])[HALO:]
            o_ref[h] = _qkv_act(pre, p * GDN_H + h).astype(o_ref.dtype)

    hv = GDN_H * LANES
    return pl.pallas_call(
        body, name="gdn_pre", grid=(3, nt),
        in_specs=[pl.BlockSpec((HALO, hv), lambda p, i: (jnp.maximum(i * hb - 1, 0), p)),
                  pl.BlockSpec((GDN_T, hv), lambda p, i: (i, p)),
                  pl.BlockSpec((4, hv), lambda p, i: (0, p))],
        out_specs=pl.BlockSpec((None, GDN_H, GDN_T, LANES), lambda p, i: (p, 0, i, 0)),
        out_shape=SDS((3, GDN_H, S, LANES), BF16),
        compiler_params=_cp(2),
    )(proj, proj, conv_w)


def gdn_pre_bwd(proj, conv_w, dqkv, S):
    nt = S // GDN_T
    hb = GDN_T // HALO
    last_h = S // HALO - 1

    def body(prev_ref, cur_ref, next_ref, w_ref, d_ref, dnext_ref, dx_ref, dw_ref):
        p, i = pl.program_id(0), pl.program_id(1)

        @pl.when(i == 0)
        def _():
            dw_ref[...] = jnp.zeros_like(dw_ref)

        for h in range(GDN_H):
            cols = slice(LANES * h, LANES * (h + 1))
            w = w_ref[:, cols]
            prev = jnp.where(i > 0, prev_ref[:, cols].astype(F32), 0.0)
            xx = jnp.concatenate([prev, cur_ref[:, cols].astype(F32), next_ref[:, cols].astype(F32)], axis=0)
            dnext = jnp.where(i < nt - 1, dnext_ref[h], 0.0)
            dd = jnp.concatenate([jnp.zeros((HALO, LANES), F32), d_ref[h], dnext], axis=0)
            pre = _conv_pre(xx, w)
            _, vjp = jax.vjp(lambda v, _c=p * GDN_H + h: _qkv_act(v, _c), pre)
            (dpre,) = vjp(dd)
            dx = dpre * w[3:4, :]
            R = dpre.shape[0]
            for j in range(3):
                dx = dx + pltpu.roll(dpre, shift=R - (3 - j), axis=0) * w[j:j + 1, :]
            dx_ref[:, cols] = dx[HALO:HALO + GDN_T].astype(dx_ref.dtype)
            own = HALO + GDN_T
            rows_w = [jnp.sum((dpre * pltpu.roll(xx, shift=3 - j, axis=0))[:own], axis=0, keepdims=True) for j in range(3)]
            rows_w.append(jnp.sum((dpre * xx)[:own], axis=0, keepdims=True))
            r4 = lax.broadcasted_iota(jnp.int32, (4, LANES), 0)
            dw = jnp.zeros((4, LANES), F32)
            for j in range(4):
                dw = dw + jnp.where(r4 == j, rows_w[j], 0.0)
            dw_ref[:, cols] += dw

    hv = GDN_H * LANES
    return pl.pallas_call(
        body, name="gdn_pre_bwd", grid=(3, nt),
        in_specs=[pl.BlockSpec((HALO, hv), lambda p, i: (jnp.maximum(i * hb - 1, 0), p)),
                  pl.BlockSpec((GDN_T, hv), lambda p, i: (i, p)),
                  pl.BlockSpec((HALO, hv), lambda p, i: (jnp.minimum((i + 1) * hb, last_h), p)),
                  pl.BlockSpec((4, hv), lambda p, i: (0, p)),
                  pl.BlockSpec((None, GDN_H, GDN_T, LANES), lambda p, i: (p, 0, i, 0)),
                  pl.BlockSpec((None, GDN_H, HALO, LANES), lambda p, i: (p, 0, jnp.minimum((i + 1) * hb, last_h), 0))],
        out_specs=[pl.BlockSpec((GDN_T, hv), lambda p, i: (i, p)),
                   pl.BlockSpec((4, hv), lambda p, i: (0, p))],
        out_shape=[SDS((S, 3 * hv), BF16), SDS((4, 3 * hv), F32)],
        compiler_params=_cp(2),
    )(proj, proj, proj, conv_w, dqkv, dqkv)


_DIMS = {"nn": ((1,), (0,)), "nt": ((1,), (1,)), "tn": ((0,), (0,))}


def _mm_raw(a, b, mode, hi):
    if hi:
        return _dot_hi(a, b, _DIMS[mode])
    return lax.dot_general(a.astype(BF16), b.astype(BF16), (_DIMS[mode], ((), ())), preferred_element_type=F32)


@functools.partial(jax.custom_vjp, nondiff_argnums=(2, 3))
def mm(a, b, mode, hi):
    return _mm_raw(a, b, mode, hi)


def _mm_fwd(a, b, mode, hi):
    return _mm_raw(a, b, mode, hi), (a, b)


def _mm_bwd(mode, hi, res, dc):
    a, b = res
    if mode == "nn":
        da, db = mm(dc, b, "nt", hi), mm(a, dc, "tn", hi)
    elif mode == "nt":
        da, db = mm(dc, b, "nn", hi), mm(dc, a, "tn", hi)
    else:
        da, db = mm(b, dc, "nt", hi), mm(a, dc, "nn", hi)
    return da, db


mm.defvjp(_mm_fwd, _mm_bwd)


TRI_BASE = 8


def _unit_lower_inverses(Ls):
    n = Ls[0].shape[0]
    r = lax.broadcasted_iota(jnp.int32, (n, n), 0)
    c = lax.broadcasted_iota(jnp.int32, (n, n), 1)
    eye = jnp.where(r == c, 1.0, 0.0).astype(F32)
    base = r // TRI_BASE == c // TRI_BASE
    one = lambda a, b_: _mm_raw(a, b_, "nn", False)
    Ps = [jnp.where(base, -L, 0.0) for L in Ls]
    invs = [eye + P for P in Ps]
    k = 1
    while 2 * k < TRI_BASE:
        Ps = [one(P, P) for P in Ps]
        invs = [inv + one(inv, P) for inv, P in zip(invs, Ps)]
        k *= 2
    b = 2 * TRI_BASE
    while b <= n:
        off_mask = (r // b == c // b) & ((r % b) >= b // 2) & ((c % b) < b // 2)
        ts = [one(inv, jnp.where(off_mask, L, 0.0)) for inv, L in zip(invs, Ls)]
        invs = [inv - one(t, inv) for inv, t in zip(invs, ts)]
        b *= 2
    resid = [eye - inv - _dot_x3(L, inv) for inv, L in zip(invs, Ls)]
    return [inv + _dot_x3(inv, rs) for inv, rs in zip(invs, resid)]


@jax.custom_vjp
def tri_apply(invs, Ls, r1s, r2s):
    return [_mm_raw(i, r, "nn", False) for i, r in zip(invs, r1s)], [_mm_raw(i, r, "nn", False) for i, r in zip(invs, r2s)]


def _tri_fwd(invs, Ls, r1s, r2s):
    s1s = [_mm_raw(i, r, "nn", False) for i, r in zip(invs, r1s)]
    s2s = [_mm_raw(i, r, "nn", False) for i, r in zip(invs, r2s)]
    return (s1s, s2s), (invs, s1s, s2s)


def _tri_bwd(res, ds):
    invs, s1s, s2s = res
    d1s = [_mm_raw(i, d, "tn", False) for i, d in zip(invs, ds[0])]
    d2s = [_mm_raw(i, d, "tn", False) for i, d in zip(invs, ds[1])]
    dLs = [-(_mm_raw(d1, s1, "nt", False) + _mm_raw(d2, s2, "nt", False)) for d1, s1, d2, s2 in zip(d1s, s1s, d2s, s2s)]
    return [jnp.zeros_like(i) for i in invs], dLs, d1s, d2s


tri_apply.defvjp(_tri_fwd, _tri_bwd)


def _gdn_chunk(qs, ks, vs, gcbs, btbs, Ss, invs=None):
    C = qs[0].shape[0]
    r = lax.broadcasted_iota(jnp.int32, (C, C), 0)
    c = lax.broadcasted_iota(jnp.int32, (C, C), 1)
    causal, strict = c <= r, c < r
    rows = lax.broadcasted_iota(jnp.int32, gcbs[0].shape, 0)
    Gs = [g[:, :C] for g in gcbs]
    decays = [jnp.exp(jnp.where(causal, G - G.T, NEG)) for G in Gs]
    kbs = [k * b for k, b in zip(ks, btbs)]
    vbs = [v * b for v, b in zip(vs, btbs)]
    Ls = [jnp.where(strict, mm(kb, k, "nt", False) * d, 0.0) for kb, k, d in zip(kbs, ks, decays)]
    egs = [jnp.exp(g) for g in gcbs]
    if invs is None:
        invs = _unit_lower_inverses(Ls)
    us, ws = tri_apply(invs, Ls, vbs, [kb * eg for kb, eg in zip(kbs, egs)])
    qks = [jnp.where(causal, mm(q, k, "nt", False) * d, 0.0) for q, k, d in zip(qs, ks, decays)]
    g_lasts = [jnp.sum(jnp.where(rows == C - 1, g, 0.0), axis=0, keepdims=True) for g in gcbs]
    q_decs = [q * eg for q, eg in zip(qs, egs)]
    k_decs = [k * jnp.exp(gl - g) for k, gl, g in zip(ks, g_lasts, gcbs)]
    v_news = [u - mm(w, S, "nn", False) for u, w, S in zip(us, ws, Ss)]
    os_ = [mm(qd, S, "nn", False) + mm(qk, vn, "nn", False) for qd, S, qk, vn in zip(q_decs, Ss, qks, v_news)]
    S_news = [S * jnp.exp(gl) + mm(kd, vn, "tn", False) for S, gl, kd, vn in zip(Ss, g_lasts, k_decs, v_news)]
    return os_, S_news, invs


INV_CHUNKS = 4
SCAN_CHUNKS = 4


def gdn_inverses(qkv, gc, bt, S):
    nchunk = S // GDN_C
    rows = INV_CHUNKS * GDN_C

    def body(k_ref, g_ref, b_ref, inv_ref):
        items = [(h, m) for m in range(INV_CHUNKS) for h in range(GDN_H)]
        r = lax.broadcasted_iota(jnp.int32, (GDN_C, GDN_C), 0)
        c = lax.broadcasted_iota(jnp.int32, (GDN_C, GDN_C), 1)
        sl = lambda m: slice(m * GDN_C, (m + 1) * GDN_C)
        ks = [k_ref[h, sl(m), :].astype(F32) for h, m in items]
        Gs = [g_ref[h, sl(m), :GDN_C] for h, m in items]
        kbs = [k * b_ref[h, sl(m), :] for k, (h, m) in zip(ks, items)]
        decays = [jnp.exp(jnp.where(c <= r, G - G.T, NEG)) for G in Gs]
        Ls = [jnp.where(c < r, _mm_raw(kb, k, "nt", False) * d, 0.0) for kb, k, d in zip(kbs, ks, decays)]
        for (h, m), inv in zip(items, _unit_lower_inverses(Ls)):
            inv_ref[h, m] = inv

    hb = pl.BlockSpec((GDN_H, rows, LANES), lambda n: (0, n, 0))
    return pl.pallas_call(
        body, name="gdn_inverses", grid=(nchunk // INV_CHUNKS,),
        in_specs=[pl.BlockSpec((None, GDN_H, rows, LANES), lambda n: (1, 0, n, 0)), hb, hb],
        out_specs=pl.BlockSpec((GDN_H, INV_CHUNKS, GDN_C, GDN_C), lambda n: (0, n, 0, 0)),
        out_shape=SDS((GDN_H, nchunk, GDN_C, GDN_C), F32),
        compiler_params=_cp(1),
    )(qkv, gc, bt)


def gdn_core(qkv, gc, bt, invs, S):
    nchunk = S // GDN_C

    def body(qkv_ref, g_ref, b_ref, inv_ref, o_ref, st_ref, s_scr):
        n = pl.program_id(0)

        @pl.when(n == 0)
        def _():
            s_scr[...] = jnp.zeros_like(s_scr)

        heads = range(GDN_H)
        S_cur = [s_scr[h] for h in heads]
        for m in range(SCAN_CHUNKS):
            sl = slice(m * GDN_C, (m + 1) * GDN_C)
            os_, S_new, _ = _gdn_chunk(*[[qkv_ref[p, h, sl, :].astype(F32) for h in heads] for p in range(3)],
                                       [g_ref[h, sl, :] for h in heads], [b_ref[h, sl, :] for h in heads], S_cur,
                                       invs=[inv_ref[h, m] for h in heads])
            for h in heads:
                st_ref[h, m] = S_cur[h].astype(st_ref.dtype)
                o_ref[h, sl, :] = os_[h]
            S_cur = S_new
        for h in heads:
            s_scr[h] = S_cur[h]

    rows = SCAN_CHUNKS * GDN_C
    blk3 = pl.BlockSpec((3, GDN_H, rows, LANES), lambda n: (0, 0, n, 0))
    hb = pl.BlockSpec((GDN_H, rows, LANES), lambda n: (0, n, 0))
    return pl.pallas_call(
        body, name="gdn_core", grid=(nchunk // SCAN_CHUNKS,),
        in_specs=[blk3, hb, hb, pl.BlockSpec((GDN_H, SCAN_CHUNKS, GDN_C, GDN_C), lambda n: (0, n, 0, 0))],
        out_specs=[hb, pl.BlockSpec((GDN_H, SCAN_CHUNKS, GDN_DK, LANES), lambda n: (0, n, 0, 0))],
        out_shape=[SDS((GDN_H, S, LANES), F32), SDS((GDN_H, nchunk, GDN_DK, LANES), BF16)],
        scratch_shapes=[pltpu.VMEM((GDN_H, GDN_DK, LANES), F32)],
        compiler_params=_cp(1),
    )(qkv, gc, bt, invs)


def gdn_core_bwd(qkv, gc, bt, states, invs, do, S):
    nchunk = S // GDN_C

    def body(qkv_ref, g_ref, b_ref, st_ref, inv_ref, do_ref, dqkv_ref, dg_ref, db_ref, ds_scr):
        n = pl.program_id(0)

        @pl.when(n == 0)
        def _():
            ds_scr[...] = jnp.zeros_like(ds_scr)

        heads = range(GDN_H)
        dS_cur = [ds_scr[h] for h in heads]
        for m in reversed(range(SCAN_CHUNKS)):
            sl = slice(m * GDN_C, (m + 1) * GDN_C)
            saved = [inv_ref[h, m] for h in heads]
            _, vjp = jax.vjp(lambda *a, _s=saved: _gdn_chunk(*a, invs=_s)[:2],
                             *[[qkv_ref[p, h, sl, :].astype(F32) for h in heads] for p in range(3)],
                             [g_ref[h, sl, :] for h in heads], [b_ref[h, sl, :] for h in heads],
                             [st_ref[h, m].astype(F32) for h in heads])
            dq, dk, dv, dg, db, dS_cur = vjp(([do_ref[h, sl, :] for h in heads], dS_cur))
            for h in heads:
                dqkv_ref[0, h, sl, :] = dq[h]
                dqkv_ref[1, h, sl, :] = dk[h]
                dqkv_ref[2, h, sl, :] = dv[h]
                dg_ref[h, sl, :] = dg[h]
                db_ref[h, sl, :] = db[h]
        for h in heads:
            ds_scr[h] = dS_cur[h]

    nblk = nchunk // SCAN_CHUNKS
    rows = SCAN_CHUNKS * GDN_C
    rev = lambda n: nblk - 1 - n
    blk3 = pl.BlockSpec((3, GDN_H, rows, LANES), lambda n: (0, 0, rev(n), 0))
    hb = pl.BlockSpec((GDN_H, rows, LANES), lambda n: (0, rev(n), 0))
    return pl.pallas_call(
        body, name="gdn_core_bwd", grid=(nblk,),
        in_specs=[blk3, hb, hb, pl.BlockSpec((GDN_H, SCAN_CHUNKS, GDN_DK, LANES), lambda n: (0, rev(n), 0, 0)),
                  pl.BlockSpec((GDN_H, SCAN_CHUNKS, GDN_C, GDN_C), lambda n: (0, rev(n), 0, 0)), hb],
        out_specs=[blk3, hb, hb],
        out_shape=[SDS((3, GDN_H, S, LANES), F32), SDS((GDN_H, S, LANES), F32), SDS((GDN_H, S, LANES), F32)],
        scratch_shapes=[pltpu.VMEM((GDN_H, GDN_DK, LANES), F32)],
        compiler_params=_cp(1),
    )(qkv, gc, bt, states, invs, do)


GDN_MAIN = 4 * GDN_H * LANES
GDN_PROJ = GDN_MAIN + LANES
RT = 512


def gdn_forward(h, w_in, conv_w, alog, dtb, out_gain, w_out):
    S = h.shape[0]
    nt = S // RT
    proj = matmul(h, w_in, "nn", BF16, "gdn_in")
    qkv = gdn_pre(proj, conv_w, S)
    ab_row = Row(proj, (RT, LANES), lambda i: (i, GDN_MAIN // LANES), gdtype=BF16, gshape=(S, LANES), gimap=lambda i: (i, 0))
    hm = lambda i: (0, i, 0)
    hv = GDN_H * LANES
    gc, bt = rowwise(f_gdn_gates, [ab_row], [alog, dtb],
                     [Out((GDN_H, S, LANES), F32, (GDN_H, RT, LANES), hm, lead=GDN_H)] * 2, (nt,), "gdn_gates")
    invs = gdn_inverses(qkv, gc, bt, S)
    o, states = gdn_core(qkv, gc, bt, invs, S)
    o_row = Row(o, (GDN_H, RT, LANES), hm, lead=GDN_H)
    z_row = Row(proj, (RT, hv), lambda i: (i, 3), splits=[LANES] * GDN_H, gdtype=BF16, gshape=(S, hv), gimap=lambda i: (i, 0))
    (on,) = rowwise(f_gdn_post, [o_row, z_row], [out_gain],
                    [Out((S, hv), BF16, (RT, hv), lambda i: (i, 0), splits=[LANES] * GDN_H)], (nt,), "gdn_post")
    y = matmul(on, w_out, "nn", BF16, "gdn_out")
    saved = dict(h=h, proj=proj, qkv=qkv, gc=gc, bt=bt, states=states, invs=invs, o=o, on=on, ab_row=ab_row, o_row=o_row, z_row=z_row)
    return y, saved


def gdn_backward(dy, sv, w_in, conv_w, alog, dtb, out_gain, w_out, on_weight_grads=None):
    S = dy.shape[0]
    nt = S // RT
    hm = lambda i: (0, i, 0)
    hv = GDN_H * LANES
    don = matmul(dy, w_out, "nt", BF16, "gdn_out_dx")
    d_w_out = matmul(sv["on"], dy, "tn", F32, "gdn_out_dw")
    (do, dz), (d_gain,) = rowwise_bwd(f_gdn_post, [sv["o_row"], sv["z_row"]], [out_gain],
                                      [Row(don, (RT, hv), lambda i: (i, 0), splits=[LANES] * GDN_H)], (nt,), "gdn_post_bwd")
    dqkv, dgc, dbt = gdn_core_bwd(sv["qkv"], sv["gc"], sv["bt"], sv["states"], sv["invs"], do, S)
    head_blk = lambda a: Row(a, (GDN_H, RT, LANES), hm, lead=GDN_H)
    (dab,), (d_alog, d_dtb) = rowwise_bwd(f_gdn_gates, [sv["ab_row"]], [alog, dtb], [head_blk(dgc), head_blk(dbt)],
                                          (nt,), "gdn_gates_bwd")
    dqkv_proj, d_conv = gdn_pre_bwd(sv["proj"], conv_w, dqkv, S)
    dproj = jnp.concatenate([dqkv_proj, dz, dab], axis=1)
    d_w_in = matmul(sv["h"], dproj, "tn", F32, "gdn_in_dw")
    if on_weight_grads is not None:
        w_in = w_in + on_weight_grads(d_w_in, d_w_out).astype(w_in.dtype)
    dh = matmul(dproj, w_in, "nt", BF16, "gdn_in_dx")
    return dh, dict(w_in=d_w_in, conv=d_conv, alog=d_alog, dtb=d_dtb, gain=d_gain, w_out=d_w_out)


QB = DSW_SPAN
N_HP = DSW_HG // LANES


def _bucket_maps():
    a = np.arange(QB)[:, None]
    j = np.arange(2 * QB)[None, :]
    dist = QB + a - j
    band = (dist >= 0) & (dist <= DSW_SPAN)
    maps = []
    for _, dil in DSW_GROUPS:
        dd = np.maximum(dist, 0) * dil
        max_exact = REL_BUCKETS // 2
        scaled = np.log(np.maximum(dd, 1).astype(np.float32) / np.float32(max_exact)) / np.float32(math.log(REL_MAX_DIST / max_exact))
        large = max_exact + (scaled * np.float32(REL_BUCKETS - max_exact)).astype(np.int32)
        large = np.minimum(large, REL_BUCKETS - 1)
        maps.append(np.where(dd < max_exact, dd, large).astype(np.int32))
    return np.stack(maps), band


def dsw_bias(rel_bias):
    maps, band = _bucket_maps()
    maps = np.where(band[None], maps, -1).astype(np.int32)

    def body(tab_ref, bk_ref, o_ref):
        gh = pl.program_id(0)
        bk = bk_ref[...]
        acc = jnp.full(bk.shape, NEG, F32)
        for b in range(REL_BUCKETS):
            acc = jnp.where(bk == b, tab_ref[b, gh], acc)
        o_ref[...] = acc

    return pl.pallas_call(
        body, name="dsw_bias", grid=(3 * GDN_H,),
        in_specs=[pl.BlockSpec(memory_space=pltpu.SMEM),
                  pl.BlockSpec((None, QB, 2 * QB), lambda gh: (gh // GDN_H, 0, 0))],
        out_specs=pl.BlockSpec((None, QB, 2 * QB), lambda gh: (gh, 0, 0)),
        out_shape=SDS((3 * GDN_H, QB, 2 * QB), F32),
        compiler_params=_cp(1),
    )(rel_bias, jnp.asarray(maps))


def dsw_bias_grad(dbias):
    maps, band = _bucket_maps()
    maps = np.where(band[None], maps, -1).astype(np.int32)

    def body(d_ref, bk_ref, o_ref):
        bk = bk_ref[...]
        d = d_ref[...]
        rows = lax.broadcasted_iota(jnp.int32, (REL_BUCKETS, LANES), 0)
        acc = jnp.zeros((REL_BUCKETS, LANES), F32)
        for b in range(REL_BUCKETS):
            part = jnp.sum(jnp.where(bk == b, d, 0.0), axis=0, keepdims=True)
            val = jnp.sum(part, axis=1, keepdims=True)
            acc = jnp.where(rows == b, val, acc)
        o_ref[...] = acc

    return pl.pallas_call(
        body, name="dsw_bias_grad", grid=(3 * GDN_H,),
        in_specs=[pl.BlockSpec((None, QB, 2 * QB), lambda gh: (gh, 0, 0)),
                  pl.BlockSpec((None, QB, 2 * QB), lambda gh: (gh // GDN_H, 0, 0))],
        out_specs=pl.BlockSpec((None, REL_BUCKETS, LANES), lambda gh: (gh, 0, 0)),
        out_shape=SDS((3 * GDN_H, REL_BUCKETS, LANES), F32),
        compiler_params=_cp(1),
    )(dbias, jnp.asarray(maps))


def _nt(a, b):
    return lax.dot_general(a, b, (((1,), (1,)), ((), ())), preferred_element_type=F32)


def _tn(a, b):
    return lax.dot_general(a, b, (((0,), (0,)), ((), ())), preferred_element_type=F32)


N_LB = DSW_HG // LANES
HALF = DSW_DH // 2


def _lanes(j):
    return slice(LANES * j, LANES * (j + 1))


def _deinterleave(stage, out_ref, dil, rows, dtype):
    for r in range(dil):
        for j in range(N_LB):
            out_ref[r, :, _lanes(j)] = stage[j, pl.ds(r, rows, stride=dil), :].astype(dtype)


def _interleave(in_ref, stage, dil, rows):
    for r in range(dil):
        for j in range(N_LB):
            stage[j, pl.ds(r, rows, stride=dil), :] = in_ref[r, :, _lanes(j)].astype(F32)


def dsw_prep(proj, q_gain2, k_gain2, gi, S):
    dil = DSW_GROUPS[gi][1]
    nt, rows = S // RT, RT // dil

    def body(q_ref, k_ref, v_ref, qg_ref, kg_ref, qo_ref, ko_ref, vo_ref, stage):
        for src, gain_ref, scale, dst in ((q_ref, qg_ref, DSW_DH ** -0.5, qo_ref), (k_ref, kg_ref, 1.0, ko_ref), (v_ref, None, None, vo_ref)):
            for j in range(N_LB):
                val = src[:, _lanes(j)].astype(F32)
                val = val if gain_ref is None else _qknorm1(val, gain_ref[...], scale)
                if dil == 1:
                    dst[0, :, _lanes(j)] = val.astype(BF16)
                else:
                    stage[j] = val
            if dil > 1:
                _deinterleave(stage, dst, dil, rows, BF16)

    col = lambda which: pl.BlockSpec((RT, DSW_HG), lambda i, _c=which * 3 + gi: (i, _c))
    gspec = pl.BlockSpec((1, LANES), lambda i: (0, 0))
    ospec = pl.BlockSpec((dil, rows, DSW_HG), lambda i: (0, i, 0))
    return pl.pallas_call(
        body, name=f"dsw_prep_g{gi}", grid=(nt,),
        in_specs=[col(0), col(1), col(2), gspec, gspec], out_specs=[ospec] * 3,
        out_shape=[SDS((dil, S // dil, DSW_HG), BF16)] * 3,
        scratch_shapes=[pltpu.VMEM((N_LB, RT, LANES), F32)], compiler_params=_cp(1),
    )(proj, proj, proj, q_gain2, k_gain2)


def dsw_prep_bwd(proj, q_gain2, k_gain2, dqd, dkd, dvd, gi, S):
    dil = DSW_GROUPS[gi][1]
    nt, rows = S // RT, RT // dil

    def body(q_ref, k_ref, qg_ref, kg_ref, dq_ref, dk_ref, dv_ref, oq_ref, ok_ref, ov_ref, dqg_ref, dkg_ref, stage):
        i = pl.program_id(0)

        @pl.when(i == 0)
        def _():
            dqg_ref[...] = jnp.zeros_like(dqg_ref)
            dkg_ref[...] = jnp.zeros_like(dkg_ref)

        for src, gain_ref, scale, cot_ref, dst, dg_ref in ((q_ref, qg_ref, DSW_DH ** -0.5, dq_ref, oq_ref, dqg_ref),
                                                          (k_ref, kg_ref, 1.0, dk_ref, ok_ref, dkg_ref)):
            if dil > 1:
                _interleave(cot_ref, stage, dil, rows)
            for j in range(N_LB):
                _, vjp = jax.vjp(lambda x, g, _s=scale: _qknorm1(x, g, _s), src[:, _lanes(j)].astype(F32), gain_ref[...])
                dx, dg = vjp(stage[j] if dil > 1 else cot_ref[0, :, _lanes(j)].astype(F32))
                dst[:, _lanes(j)] = dx.astype(dst.dtype)
                dg_ref[...] += dg
        if dil > 1:
            _interleave(dv_ref, stage, dil, rows)
        for j in range(N_LB):
            ov_ref[:, _lanes(j)] = (stage[j] if dil > 1 else dv_ref[0, :, _lanes(j)]).astype(ov_ref.dtype)

    col = lambda which: pl.BlockSpec((RT, DSW_HG), lambda i, _c=which * 3 + gi: (i, _c))
    gspec = pl.BlockSpec((1, LANES), lambda i: (0, 0))
    dspec = pl.BlockSpec((dil, rows, DSW_HG), lambda i: (0, i, 0))
    nspec = pl.BlockSpec((RT, DSW_HG), lambda i: (i, 0))
    return pl.pallas_call(
        body, name=f"dsw_prep_bwd_g{gi}", grid=(nt,),
        in_specs=[col(0), col(1), gspec, gspec, dspec, dspec, dspec], out_specs=[nspec] * 3 + [gspec] * 2,
        out_shape=[SDS((S, DSW_HG), BF16)] * 3 + [SDS((1, LANES), F32)] * 2,
        scratch_shapes=[pltpu.VMEM((N_LB, RT, LANES), F32)], compiler_params=_cp(1),
    )(proj, proj, q_gain2, k_gain2, dqd, dkd, dvd)


def _head_masks(rows):
    lane = lax.broadcasted_iota(jnp.int32, (rows, LANES), 1)
    return lane < DSW_DH, (lane % DSW_DH) < HALF


def dsw_attn_fwd(qd, kd, vd, bias, gi, S):
    dil = DSW_GROUPS[gi][1]
    sd = S // dil
    nq = sd // QB

    def body(q_ref, k_ref, v_ref, b_ref, o_ref, l_ref, kp_scr, vp_scr):
        i = pl.program_id(1)

        @pl.when(i == 0)
        def _():
            kp_scr[...] = jnp.zeros_like(kp_scr)
            vp_scr[...] = jnp.zeros_like(vp_scr)

        lo_q, _ = _head_masks(QB)
        lo_k, _ = _head_masks(2 * QB)
        col = lax.broadcasted_iota(jnp.int32, (QB, 2 * QB), 1)
        first = jnp.logical_and(i == 0, col < QB)
        hps, heads = range(N_HP), range(2 * N_HP)
        k2s = [jnp.concatenate([kp_scr[:, _lanes(hp)], k_ref[:, _lanes(hp)]], axis=0) for hp in hps]
        v2s = [jnp.concatenate([vp_scr[:, _lanes(hp)], v_ref[:, _lanes(hp)]], axis=0) for hp in hps]
        qs = [q_ref[:, _lanes(hp)] for hp in hps]
        k_now, v_now = k_ref[...], v_ref[...]
        mqs = [lo_q if h % 2 == 0 else jnp.logical_not(lo_q) for h in heads]
        mks = [lo_k if h % 2 == 0 else jnp.logical_not(lo_k) for h in heads]
        ss = [jnp.where(first, NEG, _nt(jnp.where(mqs[h], qs[h // 2], 0).astype(BF16), k2s[h // 2]) + b_ref[h]) for h in heads]
        mxs = [jnp.max(s, axis=1, keepdims=True) for s in ss]
        ps = [jnp.exp(s - mx) for s, mx in zip(ss, mxs)]
        ls = [jnp.sum(p, axis=1, keepdims=True) for p in ps]
        ohs = [jnp.dot(ps[h].astype(BF16), jnp.where(mks[h], v2s[h // 2], 0).astype(BF16), preferred_element_type=F32) / ls[h] for h in heads]
        lse_h = [mx + jnp.log(l) for mx, l in zip(mxs, ls)]
        for hp in hps:
            o_ref[:, _lanes(hp)] = ohs[2 * hp] + ohs[2 * hp + 1]
            l_ref[:, _lanes(hp)] = jnp.where(lo_q, lse_h[2 * hp], lse_h[2 * hp + 1])
        kp_scr[...] = k_now
        vp_scr[...] = v_now

    blk = pl.BlockSpec((None, QB, DSW_HG), lambda r, i: (r, i, 0))
    return pl.pallas_call(
        body, name=f"dsw_attn_g{gi}", grid=(dil, nq),
        in_specs=[blk, blk, blk, pl.BlockSpec((GDN_H, QB, 2 * QB), lambda r, i: (gi, 0, 0))],
        out_specs=[blk, blk], out_shape=[SDS((dil, sd, DSW_HG), F32)] * 2,
        scratch_shapes=[pltpu.VMEM((QB, DSW_HG), BF16)] * 2, compiler_params=_cp(2),
    )(qd, kd, vd, bias)


def dsw_attn_bwd(qd, kd, vd, bias, dod, statd, gi, S):
    dil = DSW_GROUPS[gi][1]
    sd = S // dil
    nq = sd // QB
    cur = lambda i: jnp.minimum(i, nq - 1)
    done = lambda i: jnp.maximum(i - 1, 0)

    def body(q_ref, k_ref, v_ref, b_ref, do_ref, st_ref, dq_ref, dk_ref, dv_ref, db_ref, kp_scr, vp_scr, dk_scr, dv_scr):
        r, i = pl.program_id(0), pl.program_id(1)

        @pl.when(jnp.logical_and(r == 0, i == 0))
        def _():
            db_ref[...] = jnp.zeros_like(db_ref)

        @pl.when(i == 0)
        def _():
            for scr in (kp_scr, vp_scr, dk_scr, dv_scr):
                scr[...] = jnp.zeros_like(scr)

        @pl.when(i < nq)
        def _():
            lo_q, first_half = _head_masks(QB)
            col = lax.broadcasted_iota(jnp.int32, (QB, 2 * QB), 1)
            first = jnp.logical_and(i == 0, col < QB)
            hps, heads = range(N_HP), range(2 * N_HP)
            k2s = [jnp.concatenate([kp_scr[:, _lanes(hp)], k_ref[:, _lanes(hp)]], axis=0) for hp in hps]
            v2s = [jnp.concatenate([vp_scr[:, _lanes(hp)], v_ref[:, _lanes(hp)]], axis=0) for hp in hps]
            qs = [q_ref[:, _lanes(hp)] for hp in hps]
            douts = [do_ref[:, _lanes(hp)] for hp in hps]
            stats = [st_ref[:, _lanes(hp)] for hp in hps]
            dkc = [dk_scr[:, _lanes(hp)] for hp in hps]
            dvc = [dv_scr[:, _lanes(hp)] for hp in hps]
            k_now, v_now = k_ref[...], v_ref[...]
            mqs = [lo_q if h % 2 == 0 else jnp.logical_not(lo_q) for h in heads]
            qms = [jnp.where(mqs[h], qs[h // 2], 0).astype(BF16) for h in heads]
            doms = [jnp.where(mqs[h], douts[h // 2], 0).astype(BF16) for h in heads]
            ss = [jnp.where(first, NEG, _nt(qms[h], k2s[h // 2]) + b_ref[h]) for h in heads]
            lses = [jnp.max(jnp.where(jnp.logical_and(mqs[h], first_half), stats[h // 2], NEG), axis=1, keepdims=True) for h in heads]
            deltas = [jnp.max(jnp.where(jnp.logical_and(mqs[h], jnp.logical_not(first_half)), stats[h // 2], NEG), axis=1, keepdims=True)
                      for h in heads]
            ps = [jnp.exp(ss[h] - lses[h]) for h in heads]
            dss = [ps[h] * (_nt(doms[h], v2s[h // 2]) - deltas[h]) for h in heads]
            dsbs = [d.astype(BF16) for d in dss]
            dqh = [jnp.where(mqs[h], jnp.dot(dsbs[h], k2s[h // 2], preferred_element_type=F32), 0.0) for h in heads]
            dkh = [_tn(dsbs[h], qms[h]) for h in heads]
            dvh = [_tn(ps[h].astype(BF16), doms[h]) for h in heads]
            for h in heads:
                db_ref[h] += dss[h]
            for hp in hps:
                dk2 = dkh[2 * hp] + dkh[2 * hp + 1]
                dv2 = dvh[2 * hp] + dvh[2 * hp + 1]
                dq_ref[:, _lanes(hp)] = dqh[2 * hp] + dqh[2 * hp + 1]
                dk_ref[:, _lanes(hp)] = dkc[hp] + dk2[:QB]
                dv_ref[:, _lanes(hp)] = (dvc[hp] + dv2[:QB]).astype(dv_ref.dtype)
                dk_scr[:, _lanes(hp)] = dk2[QB:]
                dv_scr[:, _lanes(hp)] = dv2[QB:]
            kp_scr[...] = k_now
            vp_scr[...] = v_now

        @pl.when(i == nq)
        def _():
            dk_ref[...] = dk_scr[...]
            dv_ref[...] = dv_scr[...].astype(dv_ref.dtype)

    blk = pl.BlockSpec((None, QB, DSW_HG), lambda r, i: (r, cur(i), 0))
    oblk = pl.BlockSpec((None, QB, DSW_HG), lambda r, i: (r, done(i), 0))
    return pl.pallas_call(
        body, name=f"dsw_attn_bwd_g{gi}", grid=(dil, nq + 1),
        in_specs=[blk, blk, blk, pl.BlockSpec((GDN_H, QB, 2 * QB), lambda r, i: (gi, 0, 0)), blk, blk],
        out_specs=[blk, oblk, oblk, pl.BlockSpec((GDN_H, QB, 2 * QB), lambda r, i: (0, 0, 0))],
        out_shape=[SDS((dil, sd, DSW_HG), F32), SDS((dil, sd, DSW_HG), F32), SDS((dil, sd, DSW_HG), BF16),
                   SDS((GDN_H, QB, 2 * QB), F32)],
        scratch_shapes=[pltpu.VMEM((QB, DSW_HG), BF16)] * 2 + [pltpu.VMEM((QB, DSW_HG), F32)] * 2,
        compiler_params=_cp(2),
    )(qd, kd, vd, bias, dod, statd)


def dsw_combine(ods, lseds, S):
    nt = S // RT
    dils = [d for _, d in DSW_GROUPS]

    def body(*refs):
        ins, (o_ref, l_ref), stages = refs[:6], refs[6:8], refs[8:]
        for g in range(3):
            if dils[g] > 1:
                _interleave(ins[g], stages[g], dils[g], RT // dils[g])
                _interleave(ins[3 + g], stages[3 + g], dils[g], RT // dils[g])
        for j in range(N_LB):
            natural = lambda a: stages[a][j] if dils[a % 3] > 1 else ins[a][0, :, _lanes(j)]
            o, lse = f_combine(None, *[natural(a) for a in range(6)])
            o_ref[:, _lanes(j)] = o.astype(o_ref.dtype)
            l_ref[:, _lanes(j)] = lse

    dspec = lambda d: pl.BlockSpec((d, RT // d, DSW_HG), lambda i: (0, i, 0))
    nspec = pl.BlockSpec((RT, DSW_HG), lambda i: (i, 0))
    return pl.pallas_call(
        body, name="dsw_combine", grid=(nt,),
        in_specs=[dspec(d) for d in dils] * 2, out_specs=[nspec, nspec],
        out_shape=[SDS((S, DSW_HG), BF16), SDS((S, DSW_HG), F32)],
        scratch_shapes=[pltpu.VMEM((N_LB, RT, LANES), F32)] * 6, compiler_params=_cp(1),
    )(*ods, *lseds)


def dsw_bwd_prep(do, o, lse, S):
    nt = S // RT
    dils = [d for _, d in DSW_GROUPS]

    def body(do_ref, o_ref, l_ref, *rest):
        outs, (st_do, st_stat) = rest[:6], rest[6:]
        lo, first_half = _head_masks(RT)
        for j in range(N_LB):
            dout = do_ref[:, _lanes(j)]
            prod = dout * o_ref[:, _lanes(j)].astype(F32)
            s_all = jnp.sum(prod, axis=1, keepdims=True)
            s_lo = jnp.sum(jnp.where(lo, prod, 0.0), axis=1, keepdims=True)
            delta = jnp.where(lo, s_lo, s_all - s_lo)
            stat = jnp.where(first_half, l_ref[:, _lanes(j)], delta)
            st_do[j] = dout
            st_stat[j] = stat
            for g in range(3):
                if dils[g] == 1:
                    outs[g][0, :, _lanes(j)] = dout.astype(BF16)
                    outs[3 + g][0, :, _lanes(j)] = stat
        for g in range(3):
            if dils[g] > 1:
                _deinterleave(st_do, outs[g], dils[g], RT // dils[g], BF16)
                _deinterleave(st_stat, outs[3 + g], dils[g], RT // dils[g], F32)

    nspec = pl.BlockSpec((RT, DSW_HG), lambda i: (i, 0))
    dspec = lambda d: pl.BlockSpec((d, RT // d, DSW_HG), lambda i: (0, i, 0))
    res = pl.pallas_call(
        body, name="dsw_bwd_prep", grid=(nt,),
        in_specs=[nspec] * 3, out_specs=[dspec(d) for d in dils] * 2,
        out_shape=[SDS((d, S // d, DSW_HG), BF16) for d in dils] + [SDS((d, S // d, DSW_HG), F32) for d in dils],
        scratch_shapes=[pltpu.VMEM((N_LB, RT, LANES), F32)] * 2, compiler_params=_cp(1),
    )(do, o, lse)
    return res[:3], res[3:]


def dsw_forward(h, w_in, q_gain2, k_gain2, rel_bias, w_out):
    S = h.shape[0]
    proj = matmul(h, w_in, "nn", BF16, "dsw_in", col_shards=N_SHARD)
    bias = dsw_bias(rel_bias)
    qkv, ods, lseds = [], [], []
    for gi in range(3):
        qd, kd, vd = dsw_prep(proj, q_gain2, k_gain2, gi, S)
        od, ld = dsw_attn_fwd(qd, kd, vd, bias, gi, S)
        qkv.append((qd, kd, vd))
        ods.append(od)
        lseds.append(ld)
    o, lse = dsw_combine(ods, lseds, S)
    y = matmul(o, w_out, "nn", BF16, "dsw_out", col_shards=N_SHARD)
    return y, dict(h=h, proj=proj, qkv=qkv, bias=bias, o=o, lse=lse)


def dsw_backward(dy, sv, w_in, q_gain2, k_gain2, w_out):
    S = dy.shape[0]
    do = matmul(dy, w_out, "nt", F32, "dsw_out_dx", col_shards=N_SHARD)
    d_w_out = matmul(sv["o"], dy, "tn", F32, "dsw_out_dw", col_shards=N_SHARD)
    dods, statds = dsw_bwd_prep(do, sv["o"], sv["lse"], S)
    pieces_q, pieces_k, pieces_v, dbs = [], [], [], []
    d_qg = jnp.zeros((1, LANES), F32)
    d_kg = jnp.zeros((1, LANES), F32)
    for gi in range(3):
        qd, kd, vd = sv["qkv"][gi]
        dqd, dkd, dvd, db = dsw_attn_bwd(qd, kd, vd, sv["bias"], dods[gi], statds[gi], gi, S)
        dq, dk, dv, dqg, dkg = dsw_prep_bwd(sv["proj"], q_gain2, k_gain2, dqd, dkd, dvd, gi, S)
        dbs.append(db)
        pieces_q.append(dq)
        pieces_k.append(dk)
        pieces_v.append(dv)
        d_qg = d_qg + dqg
        d_kg = d_kg + dkg
    dproj = jnp.concatenate(pieces_q + pieces_k + pieces_v, axis=1)
    d_w_in = matmul(sv["h"], dproj, "tn", F32, "dsw_in_dw", col_shards=N_SHARD)
    dh = matmul(dproj, w_in, "nt", BF16, "dsw_in_dx", col_shards=N_SHARD)
    d_rel = dsw_bias_grad(jnp.concatenate(dbs, axis=0))
    return dh, dict(w_in=d_w_in, q_gain2=d_qg, k_gain2=d_kg, rel=d_rel, w_out=d_w_out)


FUSE_M = 512


def ffn_in_act(h, w_in, name):
    S = h.shape[0]
    half = FFN // 2

    def body(h_ref, wg_ref, wu_ref, gu_ref, a_ref):
        j = pl.program_id(1)
        sub = FUSE_M // 2
        for part in range(2):
            rows = slice(part * sub, (part + 1) * sub)
            hb = h_ref[rows, :]
            g = jnp.dot(hb, wg_ref[...], preferred_element_type=F32)
            u = jnp.dot(hb, wu_ref[...], preferred_element_type=F32)
            a_ref[rows, :] = (_silu(g) * u).astype(a_ref.dtype)
            for jj in range(2):
                @pl.when(j == jj)
                def _(g=g, u=u, jj=jj, rows=rows):
                    gu_ref[rows, jj * half:(jj + 1) * half] = g.astype(gu_ref.dtype)
                    gu_ref[rows, FFN + jj * half:FFN + (jj + 1) * half] = u.astype(gu_ref.dtype)

    return pl.pallas_call(
        body, name=name, grid=(S // FUSE_M, 2),
        in_specs=[pl.BlockSpec((FUSE_M, D), lambda i, j: (i, 0)),
                  pl.BlockSpec((None, D, half), lambda i, j: (j, 0, 0)),
                  pl.BlockSpec((None, D, half), lambda i, j: (j + 2, 0, 0))],
        out_specs=[pl.BlockSpec((FUSE_M, 2 * FFN), lambda i, j: (i, 0)), pl.BlockSpec((FUSE_M, half), lambda i, j: (i, j))],
        out_shape=[SDS((S, 2 * FFN), BF16), SDS((S, FFN), BF16)],
        compiler_params=_cp(2),
    )(h, w_in, w_in)


def ffn_forward(h, w_in, w_out, tag):
    gu, a = ffn_in_act(h, w_in, f"ffn_in_act_{tag}")
    f = matmul(a, w_out, "nn", BF16, f"ffn_out_{tag}")
    return f, dict(h=h, gu=gu, a=a)


def ffn_out_dx_act(df, w_out, gu, name):
    S = df.shape[0]
    half = FFN // 2

    def body(df_ref, w_ref, g_ref, u_ref, dgu_ref):
        j = pl.program_id(1)
        sub = FUSE_M // 2
        for part in range(2):
            rows = slice(part * sub, (part + 1) * sub)
            da = _nt(df_ref[rows, :], w_ref[...])
            dg, du = _swiglu_bwd((g_ref[rows, :].astype(F32), u_ref[rows, :].astype(F32)), da)
            for jj in range(2):
                @pl.when(j == jj)
                def _(dg=dg, du=du, jj=jj, rows=rows):
                    dgu_ref[rows, jj * half:(jj + 1) * half] = dg.astype(dgu_ref.dtype)
                    dgu_ref[rows, FFN + jj * half:FFN + (jj + 1) * half] = du.astype(dgu_ref.dtype)

    return pl.pallas_call(
        body, name=name, grid=(S // FUSE_M, 2),
        in_specs=[pl.BlockSpec((FUSE_M, D), lambda i, j: (i, 0)),
                  pl.BlockSpec((half, D), lambda i, j: (j, 0)),
                  pl.BlockSpec((FUSE_M, half), lambda i, j: (i, j)),
                  pl.BlockSpec((FUSE_M, half), lambda i, j: (i, j + 2))],
        out_specs=pl.BlockSpec((FUSE_M, 2 * FFN), lambda i, j: (i, 0)),
        out_shape=SDS((S, 2 * FFN), BF16),
        compiler_params=_cp(2),
    )(df, w_out, gu, gu)


def ffn_backward(df, sv, w_in, w_out, tag):
    d_w_out = matmul(sv["a"], df, "tn", F32, f"ffn_out_dw_{tag}")
    dgu = ffn_out_dx_act(df, w_out, sv["gu"], f"ffn_out_dx_act_{tag}")
    d_w_in = matmul(sv["h"], dgu, "tn", F32, f"ffn_in_dw_{tag}", col_shards=N_SHARD)
    dh = matmul(dgu, w_in, "nt", BF16, f"ffn_in_dx_{tag}", col_shards=N_SHARD)
    return dh, d_w_in, d_w_out


def f_norm_only(ids, x, gain, sc, sh):
    return (_normmod(x, gain, sc, sh),)


WT = 512


def _wide(a, **kw):
    return Row(a, (WT, D), lambda i: (i, 0), **kw)


def _wide_out(S, dtype):
    return Out((S, D), dtype, (WT, D), lambda i: (i, 0))


def adamw(w, g, m, v, name):
    shape = w.shape
    C = shape[-1]
    R = int(np.prod(shape[:-1]))
    w2, g2, m2, v2 = (a.reshape(R, C) for a in (w, g, m, v))
    br = R
    if R > 256:
        br = max(b for b in range(8, 257, 8) if R % b == 0)
    c1 = 1.0 / (1.0 - ADAM_B1 ** ADAM_STEP)
    c2 = 1.0 / (1.0 - ADAM_B2 ** ADAM_STEP)

    def body(w_ref, g_ref, m_ref, v_ref, d_ref, nm_ref, nv_ref):
        gg = g_ref[...]
        mm_ = ADAM_B1 * m_ref[...] + (1.0 - ADAM_B1) * gg
        vv = ADAM_B2 * v_ref[...] + (1.0 - ADAM_B2) * (gg * gg)
        d_ref[...] = -ADAM_LR * ((mm_ * c1) / (jnp.sqrt(vv * c2) + ADAM_EPS) + ADAM_WD * w_ref[...])
        nm_ref[...] = mm_
        nv_ref[...] = vv

    spec = pl.BlockSpec((br, C), lambda i: (i, 0))
    d, nm, nv = pl.pallas_call(
        body, name=name, grid=(R // br,), in_specs=[spec] * 4, out_specs=[spec] * 3,
        out_shape=[SDS((R, C), F32)] * 3, compiler_params=_cp(1),
    )(w2, g2, m2, v2)
    return d.reshape(shape), nm.reshape(shape), nv.reshape(shape)


def _place():
    x, y, c = lax.axis_index("x"), lax.axis_index("y"), lax.axis_index("c")
    chips = [(1 - x, y), (x, 1 - y), (1 - x, 1 - y)]
    return x, y, c, chips


def all_gather_small(blk, name):
    m_per, n = blk.shape

    def body(x_ref, out_ref, send_sems, recv_sems, local_sem):
        x, y, c, chips = _place()
        me, sibling = (x, y, c), (x, y, 1 - c)

        def rows(px, py, pc):
            return out_ref.at[pl.ds((4 * px + 2 * py + pc) * m_per, m_per), :]

        def copy(k, block, to, src=None):
            return pltpu.make_async_remote_copy(
                src_ref=rows(*block) if src is None else src, dst_ref=rows(*block),
                send_sem=send_sems.at[k], recv_sem=recv_sems.at[k], device_id=to, device_id_type=MESH)

        mine = pltpu.make_async_copy(x_ref, rows(*me), local_sem)
        mine.start()
        first = [copy(0, me, sibling, src=x_ref)]
        first += [copy(1 + j, me, (*chip, c), src=x_ref) for j, chip in enumerate(chips)]
        for cp in first:
            cp.start()
        passed = [copy(4 + j, (*chip, c), sibling) for j, chip in enumerate(chips)]
        for j, chip in enumerate(chips):
            copy(1 + j, (*chip, c), me).wait_recv()
            passed[j].start()
        copy(0, sibling, me).wait_recv()
        for j, chip in enumerate(chips):
            copy(4 + j, (*chip, 1 - c), me).wait_recv()
        for cp in first + passed:
            cp.wait_send()
        mine.wait()

    return pl.pallas_call(
        body, name=name, out_shape=SDS((N_DEV * m_per, n), blk.dtype),
        in_specs=[pl.BlockSpec(memory_space=pltpu.VMEM)], out_specs=pl.BlockSpec(memory_space=pltpu.VMEM),
        scratch_shapes=[pltpu.SemaphoreType.DMA((7,)), pltpu.SemaphoreType.DMA((7,)), pltpu.SemaphoreType.DMA],
    )(blk)


def _half(cc, rh):
    return pl.ds(pl.multiple_of(cc * rh, 16), rh)


def all_gather_shards(ws):
    n = len(ws)

    def body(*refs):
        w_refs, out_refs = refs[:n], refs[n:2 * n]
        send_sems, recv_sems, local_sems, own_sems = refs[2 * n:]
        x, y, c, chips = _place()
        sibling = (x, y, 1 - c)
        s_me = 2 * x + y

        def copy(k, src, dst, to):
            return pltpu.make_async_remote_copy(src_ref=src, dst_ref=dst, send_sem=send_sems.at[k], recv_sem=recv_sems.at[k],
                                                device_id=to, device_id_type=MESH)

        local, sends, passed = [], [], []
        for k in range(n):
            rh = ws[k].shape[0] // 2
            cp = pltpu.make_async_remote_copy(src_ref=w_refs[k], dst_ref=out_refs[k].at[s_me], send_sem=local_sems.at[k],
                                              recv_sem=own_sems.at[k], device_id=sibling, device_id_type=MESH)
            cp.start()
            local.append(cp)
            for j, chip in enumerate(chips):
                sd = copy(6 * k + j, w_refs[k].at[_half(c, rh)], out_refs[k].at[s_me, _half(c, rh)], (*chip, c))
                sd.start()
                sends.append(sd)
        for k in range(n):
            rh = ws[k].shape[0] // 2
            for j, (px, py) in enumerate(chips):
                got = out_refs[k].at[2 * px + py, _half(c, rh)]
                copy(6 * k + j, got, got, (px, py, c)).wait_recv()
                fw = copy(6 * k + 3 + j, got, got, sibling)
                fw.start()
                passed.append(fw)
        for k in range(n):
            rh = ws[k].shape[0] // 2
            for j, (px, py) in enumerate(chips):
                got = out_refs[k].at[2 * px + py, _half(1 - c, rh)]
                copy(6 * k + 3 + j, got, got, sibling).wait_recv()
        for cp in sends + passed:
            cp.wait_send()
        for cp in local:
            cp.wait()

    return pl.pallas_call(
        body, name="weights_all_gather", out_shape=[SDS((N_SHARD,) + w.shape, w.dtype) for w in ws],
        in_specs=[ANY] * n, out_specs=[ANY] * n,
        scratch_shapes=[pltpu.SemaphoreType.DMA((6 * n,)), pltpu.SemaphoreType.DMA((6 * n,)), pltpu.SemaphoreType.DMA((n,)),
                        pltpu.SemaphoreType.DMA((n,))],
    )(*ws)


def sibling_exchange(sends, name):
    n = len(sends)

    def body(*refs):
        s_refs, o_refs, send_sems, recv_sems = refs[:n], refs[n:2 * n], refs[2 * n], refs[2 * n + 1]
        x, y, c, _ = _place()
        cps = [pltpu.make_async_remote_copy(src_ref=s_refs[k], dst_ref=o_refs[k], send_sem=send_sems.at[k], recv_sem=recv_sems.at[k],
                                            device_id=(x, y, 1 - c), device_id_type=MESH) for k in range(n)]
        for cp in cps:
            cp.start()
        for cp in cps:
            cp.wait()

    return pl.pallas_call(
        body, name=name, out_shape=[SDS(s.shape, s.dtype) for s in sends], in_specs=[ANY] * n, out_specs=[ANY] * n,
        scratch_shapes=[pltpu.SemaphoreType.DMA((n,)), pltpu.SemaphoreType.DMA((n,))],
    )(*sends)


def add_rows(arrs, out_dtype, name, rt=256):
    Rr, W = arrs[0].shape

    def fn(ids, *vals):
        acc = vals[0]
        for v in vals[1:]:
            acc = acc + v
        return (acc,)

    t = rt if Rr % rt == 0 else max(b for b in range(16, rt + 1, 16) if Rr % b == 0)
    (out,) = rowwise(fn, [Row(a, (t, W), lambda i: (i, 0)) for a in arrs], [],
                     [Out((Rr, W), out_dtype, (t, W), lambda i: (i, 0))], (Rr // t,), name)
    return out


HBM_SPEC = pl.BlockSpec(memory_space=pltpu.HBM)
SEM_SPEC = pl.BlockSpec(memory_space=pltpu.SEMAPHORE)
DATAFLOW = pltpu.SideEffectType.DATAFLOW_SIDE_EFFECTING


def _in_hbm(a):
    return pltpu.with_memory_space_constraint(a, pltpu.HBM)


def _gather_copies(w_refs, land_refs, send_sems, recv_sems):
    x, y, c, chips = _place()
    targets = [(x, y, 1 - c)] + [(*chip, c) for chip in chips]
    cps = []
    for k, (w_ref, land_ref) in enumerate(zip(w_refs, land_refs)):
        for j, to in enumerate(targets):
            cps.append(pltpu.make_async_remote_copy(src_ref=w_ref, dst_ref=land_ref.at[2 * x + y], send_sem=send_sems.at[4 * k + j],
                                                    recv_sem=recv_sems.at[4 * k + j], device_id=to, device_id_type=MESH))
    return cps


def _scatter_copies(p_refs, land_refs, send_sems, recv_sems):
    x, y, c, chips = _place()
    cps = []
    for k, (p_ref, land_ref) in enumerate(zip(p_refs, land_refs)):
        for j, (px, py) in enumerate(chips):
            cps.append(pltpu.make_async_remote_copy(src_ref=p_ref.at[2 * px + py], dst_ref=land_ref.at[j], send_sem=send_sems.at[3 * k + j],
                                                    recv_sem=recv_sems.at[3 * k + j], device_id=(px, py, c), device_id_type=MESH))
    return cps


def copies_start(srcs, land_shapes, make_copies, per_src, name):
    n = len(srcs)
    m = per_src * n

    def body(*refs):
        src_refs, land_refs = refs[:n], refs[n:2 * n]
        send_sems, recv_sems, token = refs[2 * n], refs[2 * n + 1], refs[-1]
        for cp in make_copies(src_refs, land_refs, send_sems, recv_sems):
            cp.start()
        token[...] = jnp.zeros_like(token)

    lands = [lax.empty(shp, s.dtype) for shp, s in zip(land_shapes, srcs)]
    res = pl.pallas_call(
        body, name=name,
        out_shape=(pltpu.SemaphoreType.DMA((m,)), pltpu.SemaphoreType.DMA((m,)), *[pltpu.HBM(s.shape, s.dtype) for s in srcs],
                   *[pltpu.HBM(shp, s.dtype) for shp, s in zip(land_shapes, srcs)], SDS((8, LANES), F32)),
        in_specs=[HBM_SPEC] * (2 * n),
        out_specs=(SEM_SPEC, SEM_SPEC, *[HBM_SPEC] * (2 * n), pl.BlockSpec(memory_space=pltpu.VMEM)),
        input_output_aliases={i: 2 + i for i in range(2 * n)},
        compiler_params=pltpu.CompilerParams(has_side_effects=DATAFLOW),
    )(*[_in_hbm(s) for s in srcs], *[_in_hbm(l) for l in lands])
    return res[0], res[1], list(res[2:2 + n]), list(res[2 + n:2 + 2 * n]), res[-1]


def copies_wait(send_sems, recv_sems, srcs, lands, make_copies, after, name):
    n = len(srcs)

    def body(*refs):
        src_refs, land_refs = refs[:n], refs[n:2 * n]
        for cp in make_copies(src_refs, land_refs, refs[2 * n], refs[2 * n + 1]):
            cp.wait_send()
            cp.wait_recv()

    res = pl.pallas_call(
        body, name=name,
        out_shape=(*[pltpu.HBM(s.shape, s.dtype) for s in srcs], *[pltpu.HBM(l.shape, l.dtype) for l in lands]),
        in_specs=[HBM_SPEC] * (2 * n) + [SEM_SPEC, SEM_SPEC, ANY],
        out_specs=tuple([HBM_SPEC] * (2 * n)),
        input_output_aliases={i: i for i in range(2 * n)},
        compiler_params=pltpu.CompilerParams(has_side_effects=DATAFLOW),
    )(*srcs, *lands, send_sems, recv_sems, after)
    return list(res[n:])


def _pad_lanes(v):
    return jnp.concatenate([v.astype(F32), jnp.zeros((LANES - v.shape[0],), F32)])[None]


def kernel(x, c, w_ada, b_ada, norm_mix, norm_ffn, w_ffn_in, w_ffn_out, gdn_w_in, gdn_conv, gdn_a_log, gdn_dt_bias, gdn_out_norm, gdn_w_out, dsw_w_in, dsw_q_norm, dsw_k_norm, dsw_w_out, rel_bias, loss_target, m_w_ada, m_b_ada, m_norm_mix, m_norm_ffn, m_w_ffn_in, m_w_ffn_out, m_gdn_w_in, m_gdn_conv, m_gdn_a_log, m_gdn_dt_bias, m_gdn_out_norm, m_gdn_w_out, m_dsw_w_in, m_dsw_q_norm, m_dsw_k_norm, m_dsw_w_out, m_rel_bias, v_w_ada, v_b_ada, v_norm_mix, v_norm_ffn, v_w_ffn_in, v_w_ffn_out, v_gdn_w_in, v_gdn_conv, v_gdn_a_log, v_gdn_dt_bias, v_gdn_out_norm, v_gdn_w_out, v_dsw_w_in, v_dsw_q_norm, v_dsw_k_norm, v_dsw_w_out, v_rel_bias):
    S = x.shape[1]
    nt = S // WT
    xi, yi, ci = lax.axis_index("x"), lax.axis_index("y"), lax.axis_index("c")
    me = 4 * xi + 2 * yi + ci
    s_me = 2 * xi + yi
    x0, tgt = x[0], loss_target[0]

    whole = lambda a: Row(a, a.shape, lambda i: (0,) * a.ndim)
    (cond8,) = rowwise(lambda ids, v: (_silu(v),), [whole(c.reshape(8, LANES))], [], [Out((8, LANES), F32, (8, LANES), lambda i: (0, 0))], (1,), "cond")
    cond_all = all_gather_small(cond8, "gather_cond").reshape(N_DEV, D)
    cond16 = jnp.concatenate([cond_all, jnp.zeros((8, D), F32)], axis=0)
    ada_cols = w_ada.shape[2]
    mods = [matmul(cond16, w_ada[l], "nn", F32, f"ada_{l}")[:N_DEV] for l in range(2)]
    buf = jnp.concatenate([jnp.stack(mods, axis=1).reshape(-1, LANES), gdn_conv.reshape(-1, LANES)], axis=0)
    n_mod_rows = N_DEV * 2 * ada_cols // LANES
    got = all_gather_small(buf, "gather_mod").reshape(N_DEV, buf.shape[0], LANES)
    mod_parts, conv_parts = [], []
    for s in range(N_SHARD):
        from_dev = got[2 * s]
        mod_parts.append(lax.dynamic_index_in_dim(from_dev[:n_mod_rows].reshape(N_DEV, 2, ada_cols), me, 0, keepdims=False))
        conv_parts.append(from_dev[n_mod_rows:].reshape(4, -1))
    mod_nb = jnp.concatenate(mod_parts, axis=1)
    conv_w = jnp.concatenate(conv_parts, axis=1)
    (mod,) = rowwise(lambda ids, a, b: (a + b,), [whole(mod_nb), whole(b_ada)], [], [Out(mod_nb.shape, F32, mod_nb.shape, lambda i: (0, 0))], (1,), "mod_bias")
    mod = mod.reshape(2, 6, 1, D)
    sh1, sc1, g1, sh2, sc2, g2 = ([mod[l, k] for l in range(2)] for k in range(6))
    gmix = [norm_mix[l][None] for l in range(2)]
    gffn = [norm_ffn[l][None] for l in range(2)]

    gcols = gdn_w_in.shape[2]
    g_gdn_in, g_gdn_out = all_gather_shards([gdn_w_in[0].astype(BF16), gdn_w_out[0].astype(BF16)])
    gathered = lambda ws: [(N_SHARD,) + w.shape for w in ws]
    gate = (jnp.minimum(jnp.abs(g_gdn_in[0, 0, 0].astype(F32)), 0.0) + jnp.minimum(jnp.abs(mod[0, 0, 0, 0]), 0.0)).astype(BF16)
    w2 = [w_ffn_in[0].astype(BF16) + gate, w_ffn_out[0].astype(BF16) + gate]
    w3 = [dsw_w_in[0].astype(BF16) + gate, dsw_w_out[0].astype(BF16) + gate, w_ffn_in[1].astype(BF16) + gate, w_ffn_out[1].astype(BF16) + gate]
    fly2 = copies_start(w2, gathered(w2), _gather_copies, 4, "weights_ffn0_start")
    fly3 = copies_start(w3, gathered(w3), _gather_copies, 4, "weights_layer1_start")
    started = fly2[4][0, 0] + fly3[4][0, 0]
    w_gdn = jnp.concatenate([g_gdn_in[s] for s in range(N_SHARD)] + [jnp.zeros((D, GDN_PROJ - N_SHARD * gcols), BF16)], axis=1)
    alog, dtb = _pad_lanes(gdn_a_log[0]), _pad_lanes(gdn_dt_bias[0])
    qg2 = jnp.concatenate([dsw_q_norm, dsw_q_norm], axis=1)
    kg2 = jnp.concatenate([dsw_k_norm, dsw_k_norm], axis=1)
    w_gdn_out = g_gdn_out.reshape(GDN_H * LANES, D)
    gdn_args = (w_gdn, conv_w, alog, dtb, gdn_out_norm, w_gdn_out)
    sc1[0] = sc1[0] + started

    (h10,) = rowwise(f_norm_only, [_wide(x0)], [gmix[0], sc1[0], sh1[0]], [_wide_out(S, BF16)], (nt,), "l0_norm")
    y0, sv_g = gdn_forward(h10, *gdn_args)
    x1, h20 = rowwise(f_resid_norm, [_wide(x0), _wide(y0)], [g1[0], gffn[0], sc2[0], sh2[0]], [_wide_out(S, F32), _wide_out(S, BF16)], (nt,), "l0_mid")
    g_in0, g_out0 = copies_wait(*fly2[:4], _gather_copies, y0, "weights_ffn0_wait")
    w_ffn = [(g_in0, g_out0.reshape(FFN, D)), None]
    f0, sv_f0 = ffn_forward(h20, *w_ffn[0], "0")
    x2, h11 = rowwise(f_resid_norm, [_wide(x1), _wide(f0)], [g2[0], gmix[1], sc1[1], sh1[1]], [_wide_out(S, F32), _wide_out(S, BF16)], (nt,), "l1_in")
    g_dsw_in, g_dsw_out, g_in1, g_out1 = copies_wait(*fly3[:4], _gather_copies, f0, "weights_layer1_wait")
    w_ffn[1] = (g_in1, g_out1.reshape(FFN, D))
    dsw_args = (g_dsw_in, qg2, kg2)
    y1, sv_d = dsw_forward(h11, *dsw_args, rel_bias, g_dsw_out)
    x3, h21 = rowwise(f_resid_norm, [_wide(x2), _wide(y1)], [g1[1], gffn[1], sc2[1], sh2[1]], [_wide_out(S, F32), _wide_out(S, BF16)], (nt,), "l1_mid")
    f1, sv_f1 = ffn_forward(h21, *w_ffn[1], "1")
    parts, dx3, df1, dg2_1 = loss_and_grad(x3, f1, tgt, g2[1], S)
    loss = lax.psum(jnp.sum(parts), ("x", "y", "c"))

    dh21, d_win1, d_wout1 = ffn_backward(df1, sv_f1, *w_ffn[1], "1")
    (dx2, dy1), (dg1_1, dgf1, dsc2_1, dsh2_1) = rowwise_bwd(
        f_resid_norm, [_wide(x2), _wide(y1, gdtype=BF16)], [g1[1], gffn[1], sc2[1], sh2[1]], [_wide(dx3), _wide(dh21)], (nt,), "l1_mid_bwd")
    dh11, g_d = dsw_backward(dy1, sv_d, *dsw_args, g_dsw_out)
    (dx1, df0), (dg2_0, dgm1, dsc1_1, dsh1_1) = rowwise_bwd(
        f_resid_norm, [_wide(x1), _wide(f0, gdtype=BF16)], [g2[0], gmix[1], sc1[1], sh1[1]], [_wide(dx2), _wide(dh11)], (nt,), "l1_in_bwd")
    by_shard = lambda a: a.reshape(N_SHARD, a.shape[0] // N_SHARD, a.shape[1])
    landing = lambda ps: [(3,) + p.shape[1:] for p in ps]
    dws3 = [g_d["w_in"], g_d["w_out"], d_win1, by_shard(d_wout1)]
    parts3 = [a.astype(BF16) for a in dws3]
    gfly3 = copies_start(parts3, landing(parts3), _scatter_copies, 3, "grads_layer1_start")
    w_out0 = w_ffn[0][1] + gfly3[4][0, 0].astype(BF16)
    dh20, d_win0, d_wout0 = ffn_backward(df0, sv_f0, w_ffn[0][0], w_out0, "0")
    (dx0p, dy0), (dg1_0, dgf0, dsc2_0, dsh2_0) = rowwise_bwd(
        f_resid_norm, [_wide(x0), _wide(y0, gdtype=BF16)], [g1[0], gffn[0], sc2[0], sh2[0]], [_wide(dx1), _wide(dh20)], (nt,), "l0_mid_bwd")
    dws2 = [d_win0, by_shard(d_wout0)]
    parts2 = [a.astype(BF16) for a in dws2]
    gfly2 = copies_start(parts2, landing(parts2), _scatter_copies, 3, "grads_ffn0_start")
    gdn_args = gdn_args[:5] + (w_gdn_out + gfly2[4][0, 0].astype(BF16),)
    gdn_flight = []

    def start_gdn_grads(d_w_in, d_w_out):
        dws1 = [jnp.stack([d_w_in[:, s * gcols:(s + 1) * gcols] for s in range(N_SHARD)]), by_shard(d_w_out)]
        parts1 = [a.astype(BF16) for a in dws1]
        fly = copies_start(parts1, landing(parts1), _scatter_copies, 3, "grads_gdn_start")
        gdn_flight.extend([dws1, fly])
        return fly[4][0, 0]

    dh10, g_g = gdn_backward(dy0, sv_g, *gdn_args, on_weight_grads=start_gdn_grads)
    dws1, gfly1 = gdn_flight
    (grad_x,), (dgm0, dsc1_0, dsh1_0) = rowwise_bwd(f_first, [_wide(x0)], [gmix[0], sc1[0], sh1[0]], [_wide(dx0p), _wide(dh10)], (nt,), "l0_norm_bwd")

    dmod = jnp.concatenate([dsh1_0, dsc1_0, dg1_0, dsh2_0, dsc2_0, dg2_0, dsh1_1, dsc1_1, dg1_1, dsh2_1, dsc2_1, dg2_1], axis=1)
    d_rel = jnp.transpose(g_d["rel"][:, :, 0])
    fold = lambda v: v[:, :DSW_DH] + v[:, DSW_DH:]
    small = [dmod, jnp.concatenate([dgm0, dgm1], axis=1), jnp.concatenate([dgf0, dgf1], axis=1), g_g["conv"].reshape(1, -1),
             g_g["alog"], g_g["dtb"], g_g["gain"], _pad_lanes(fold(g_d["q_gain2"])[0]), _pad_lanes(fold(g_d["k_gain2"])[0]),
             d_rel.reshape(1, -1)]
    used = [v.shape[1] // LANES for v in small]
    sizes = [-(-u // 8) * 8 for u in used]
    pad8 = lambda v, u, s: jnp.concatenate([v.reshape(u, LANES), jnp.zeros((s - u, LANES), F32)], axis=0) if s > u else v.reshape(u, LANES)
    pad_rows = sum(sizes)
    sbuf = jnp.concatenate([pad8(v, u, s) for v, u, s in zip(small, used, sizes)], axis=0)
    sgot = all_gather_small(sbuf, "gather_small_grads")
    ssum = add_rows([sgot[d * pad_rows:(d + 1) * pad_rows] for d in range(N_DEV)], F32, "sum_small_grads", rt=pad_rows)
    offs = np.cumsum([0] + sizes)
    take = lambda k: ssum[offs[k]:offs[k] + used[k]].reshape(1, -1)
    grad_b_ada = take(0).reshape(2, 6 * D)
    grad_norm_mix = take(1).reshape(2, D)
    grad_norm_ffn = take(2).reshape(2, D)
    conv_full = take(3).reshape(4, -1)
    ncv = gdn_conv.shape[2]
    grad_gdn_conv = lax.dynamic_slice_in_dim(conv_full, s_me * ncv, ncv, axis=1)[None]
    grad_a_log = take(4)[:, :GDN_H]
    grad_dt_bias = take(5)[:, :GDN_H]
    grad_out_norm = take(6)
    grad_q_norm = take(7)[:, :DSW_DH]
    grad_k_norm = take(8)[:, :DSW_DH]
    grad_rel = take(9).reshape(REL_BUCKETS, 3 * GDN_H)
    dmod_all = sgot.reshape(N_DEV, pad_rows, LANES)[:, :used[0]].reshape(N_DEV, 2, 6 * D)
    dmod_mine = lax.dynamic_slice_in_dim(dmod_all, s_me * ada_cols, ada_cols, axis=2)
    dmod16 = jnp.concatenate([dmod_mine, jnp.zeros_like(dmod_mine)], axis=0)
    grad_w_ada = jnp.stack([matmul(cond16, dmod16[:, l], "tn", F32, f"ada_dw_{l}") for l in range(2)])

    got3 = copies_wait(*gfly3[:4], _scatter_copies, grad_x, "grads_layer1_wait")
    got2 = copies_wait(*gfly2[:4], _scatter_copies, grad_x, "grads_ffn0_wait")
    got1 = copies_wait(*gfly1[:4], _scatter_copies, grad_x, "grads_gdn_wait")
    core_sums = []
    for i, (full, got) in enumerate(zip(dws3 + dws2 + dws1, got3 + got2 + got1)):
        own = lax.dynamic_index_in_dim(full, s_me, 0, keepdims=False)
        core_sums.append(add_rows([own, got[0], got[1], got[2]], F32, f"grads_core_sum_{i}"))
    sib_sums = sibling_exchange(core_sums, "grads_core_sums_swap")
    s_dsw_in, s_dsw_out, s_in1, s_out1, s_in0, s_out0, s_gdn_in, s_gdn_out = [
        add_rows([a, b], F32, f"grads_chip_total_{i}") for i, (a, b) in enumerate(zip(core_sums, sib_sums))]
    gsh = dict(gdn_w_in=s_gdn_in[None], gdn_w_out=s_gdn_out[None],
               w_ffn_in=jnp.stack([s_in0, s_in1]), w_ffn_out=jnp.stack([s_out0, s_out1]),
               dsw_w_in=s_dsw_in[None], dsw_w_out=s_dsw_out[None])

    grads = dict(w_ada=grad_w_ada, b_ada=grad_b_ada, norm_mix=grad_norm_mix, norm_ffn=grad_norm_ffn, w_ffn_in=gsh["w_ffn_in"],
                 w_ffn_out=gsh["w_ffn_out"], gdn_w_in=gsh["gdn_w_in"], gdn_conv=grad_gdn_conv, gdn_a_log=grad_a_log,
                 gdn_dt_bias=grad_dt_bias, gdn_out_norm=grad_out_norm, gdn_w_out=gsh["gdn_w_out"], dsw_w_in=gsh["dsw_w_in"],
                 dsw_q_norm=grad_q_norm, dsw_k_norm=grad_k_norm, dsw_w_out=gsh["dsw_w_out"], rel_bias=grad_rel)
    weights = dict(w_ada=w_ada, b_ada=b_ada, norm_mix=norm_mix, norm_ffn=norm_ffn, w_ffn_in=w_ffn_in, w_ffn_out=w_ffn_out,
                   gdn_w_in=gdn_w_in, gdn_conv=gdn_conv, gdn_a_log=gdn_a_log, gdn_dt_bias=gdn_dt_bias, gdn_out_norm=gdn_out_norm,
                   gdn_w_out=gdn_w_out, dsw_w_in=dsw_w_in, dsw_q_norm=dsw_q_norm, dsw_k_norm=dsw_k_norm, dsw_w_out=dsw_w_out,
                   rel_bias=rel_bias)
    ms = dict(w_ada=m_w_ada, b_ada=m_b_ada, norm_mix=m_norm_mix, norm_ffn=m_norm_ffn, w_ffn_in=m_w_ffn_in, w_ffn_out=m_w_ffn_out,
              gdn_w_in=m_gdn_w_in, gdn_conv=m_gdn_conv, gdn_a_log=m_gdn_a_log, gdn_dt_bias=m_gdn_dt_bias, gdn_out_norm=m_gdn_out_norm,
              gdn_w_out=m_gdn_w_out, dsw_w_in=m_dsw_w_in, dsw_q_norm=m_dsw_q_norm, dsw_k_norm=m_dsw_k_norm, dsw_w_out=m_dsw_w_out,
              rel_bias=m_rel_bias)
    vs = dict(w_ada=v_w_ada, b_ada=v_b_ada, norm_mix=v_norm_mix, norm_ffn=v_norm_ffn, w_ffn_in=v_w_ffn_in, w_ffn_out=v_w_ffn_out,
              gdn_w_in=v_gdn_w_in, gdn_conv=v_gdn_conv, gdn_a_log=v_gdn_a_log, gdn_dt_bias=v_gdn_dt_bias, gdn_out_norm=v_gdn_out_norm,
              gdn_w_out=v_gdn_w_out, dsw_w_in=v_dsw_w_in, dsw_q_norm=v_dsw_q_norm, dsw_k_norm=v_dsw_k_norm, dsw_w_out=v_dsw_w_out,
              rel_bias=v_rel_bias)
    names = list(weights)
    deltas, new_m, new_v = [], [], []
    for n in names:
        g = grads[n].reshape(weights[n].shape)
        grads[n] = g
        d, nm, nv = adamw(weights[n], g, ms[n], vs[n], f"adamw_{n}")
        deltas.append(d)
        new_m.append(nm)
        new_v.append(nv)
    return (loss, grad_x[None], *[grads[n] for n in names], *deltas, *new_m, *new_v)
```

```python
import functools
import math

import numpy as np
import jax
import jax.numpy as jnp
from jax import lax
from jax.experimental import pallas as pl
from jax.experimental.pallas import tpu as pltpu

F32 = jnp.float32
BF16 = jnp.bfloat16
SDS = jax.ShapeDtypeStruct
MESH = pl.DeviceIdType.MESH
ANY = pl.BlockSpec(memory_space=pl.ANY)

D = 1024
EPS = 1e-6
LANES = 128
GDN_H = 8
GDN_DK = 128
GDN_C = 64
DSW_GROUPS = ((128, 1), (512, 4), (2048, 16))
DSW_SPAN = 128
DSW_DH = 64
DSW_HG = 512
REL_BUCKETS = 32
REL_MAX_DIST = 2048
FFN = 2816
N_SHARD = 4
N_DEV = 8
VMEM_LIMIT = 48 * 1024 * 1024
NEG = -1e30

ADAM_LR, ADAM_B1, ADAM_B2, ADAM_EPS, ADAM_WD, ADAM_STEP = 0.001, 0.9, 0.999, 1e-08, 0.01, 10


def _cp(n_axes):
    return pltpu.CompilerParams(dimension_semantics=("arbitrary",) * n_axes, vmem_limit_bytes=VMEM_LIMIT)


def _blk(dim, cap):
    if dim <= cap:
        return dim
    best = None
    for b in range(LANES, cap + 1, LANES):
        if dim % b == 0:
            best = b
    assert best is not None, (dim, cap)
    return best


MAX_SHARD_BLOCK = 1408
def matmul(a, b, mode, out_dtype, name, cap_m=MAX_SHARD_BLOCK, cap_n=MAX_SHARD_BLOCK, cap_k=2048, col_shards=0):
    ns = col_shards
    if mode == "nn":
        (M, K) = a.shape
        K2, N = (b.shape[1], ns * b.shape[2]) if ns else b.shape
    elif mode == "nt":
        (M, K) = a.shape
        N, K2 = (b.shape[1], ns * b.shape[2]) if ns else b.shape
    else:
        (K, M), (K2, N) = a.shape, b.shape
    assert K == K2, (a.shape, b.shape, mode)
    if K <= 3072:
        cap_k = K
        if K > 2048:
            cap_n = 1024
    n_unit = N // ns if (ns and mode != "nt") else N
    k_unit = K // ns if (ns and mode == "nt") else K
    bm = _blk(M, cap_m)
    bn = _blk(n_unit, MAX_SHARD_BLOCK) if n_unit != N else _blk(N, cap_n)
    if k_unit != K:
        bk = _blk(k_unit, MAX_SHARD_BLOCK)
    else:
        bk = _blk(K, 1024 if (ns and mode == "tn") else cap_k)
    nk = K // bk
    nps, kps = n_unit // bn, k_unit // bk
    dims = {"nn": ((1,), (0,)), "nt": ((1,), (1,)), "tn": ((0,), (0,))}[mode]

    def dot(a_ref, b_ref):
        return lax.dot_general(a_ref[...].astype(BF16), b_ref[...].astype(BF16), (dims, ((), ())), preferred_element_type=F32)

    def body_one(a_ref, b_ref, o_ref):
        o_ref[...] = dot(a_ref, b_ref).astype(o_ref.dtype)

    def body_acc(a_ref, b_ref, o_ref, acc_ref):
        k = pl.program_id(2)

        @pl.when(k == 0)
        def _():
            acc_ref[...] = jnp.zeros_like(acc_ref)

        acc_ref[...] += dot(a_ref, b_ref)

        @pl.when(k == nk - 1)
        def _():
            o_ref[...] = acc_ref[...].astype(o_ref.dtype)

    a_spec = pl.BlockSpec((bk, bm), lambda i, j, k: (k, i)) if mode == "tn" else pl.BlockSpec((bm, bk), lambda i, j, k: (i, k))
    if mode == "nt":
        b_spec = pl.BlockSpec((None, bn, bk), lambda i, j, k: (k // kps, j, k % kps)) if ns else pl.BlockSpec((bn, bk), lambda i, j, k: (j, k))
    elif mode == "nn" and ns:
        b_spec = pl.BlockSpec((None, bk, bn), lambda i, j, k: (j // nps, k, j % nps))
    else:
        b_spec = pl.BlockSpec((bk, bn), lambda i, j, k: (k, j))
    if mode == "tn" and ns:
        o_spec, o_shape = pl.BlockSpec((None, bm, bn), lambda i, j, k: (j // nps, i, j % nps)), (ns, M, n_unit)
    else:
        o_spec, o_shape = pl.BlockSpec((bm, bn), lambda i, j, k: (i, j)), (M, N)
    return pl.pallas_call(
        body_one if nk == 1 else body_acc, name=name, grid=(M // bm, N // bn, nk),
        in_specs=[a_spec, b_spec], out_specs=o_spec,
        out_shape=SDS(o_shape, out_dtype), scratch_shapes=[] if nk == 1 else [pltpu.VMEM((bm, bn), F32)],
        compiler_params=_cp(3),
    )(a, b)


class Row:
    def __init__(self, arr, bshape, imap, splits=None, diff=True, acc=False, gdtype=F32, gshape=None, gbshape=None, gimap=None,
                 lead=0):
        self.arr, self.bshape, self.imap = arr, tuple(bshape), imap
        self.splits, self.lead = splits, lead
        self.diff, self.acc, self.gdtype = diff, acc, gdtype
        self.gshape = tuple(arr.shape) if gshape is None else tuple(gshape)
        self.gbshape = self.bshape if gbshape is None else tuple(gbshape)
        self.gimap = imap if gimap is None else gimap

    def gspec(self):
        return pl.BlockSpec(self.gbshape, self.gimap)

    def spec(self):
        return pl.BlockSpec(self.bshape, self.imap)

    def pieces(self, ref):
        return _load_pieces(ref, self.splits, self.lead)

    def n_pieces(self):
        return _n_pieces(self.splits, self.lead)


class Out:
    def __init__(self, shape, dtype, bshape, imap, splits=None, lead=0):
        self.shape, self.dtype, self.bshape, self.imap = tuple(shape), dtype, tuple(bshape), imap
        self.splits, self.lead = splits, lead

    def n_pieces(self):
        return _n_pieces(self.splits, self.lead)


def _n_pieces(splits, lead):
    return lead if lead else (1 if splits is None else len(splits))


def _load_pieces(ref, splits, lead):
    if lead:
        return [ref[k].astype(F32) for k in range(lead)]
    if splits is None:
        return [ref[...].astype(F32)]
    out, o = [], 0
    for w in splits:
        out.append(ref[..., o:o + w].astype(F32))
        o += w
    return out


def _store_pieces(ref, splits, lead, vals, accumulate=False):
    def put(idx, v):
        if accumulate:
            ref[idx] += v.astype(ref.dtype)
        else:
            ref[idx] = v.astype(ref.dtype)

    if lead:
        for k in range(lead):
            put(k, vals[k])
    elif splits is None:
        put(..., vals[0])
    else:
        o = 0
        for w, v in zip(splits, vals):
            put((..., slice(o, o + w)), v)
            o += w


def rowwise(fn, rows, params, outs, grid, name):
    nr, npar = len(rows), len(params)

    def body(*refs):
        ids = tuple(pl.program_id(a) for a in range(len(grid)))
        vals = []
        for r, ref in zip(rows, refs[:nr]):
            vals += r.pieces(ref)
        pvals = [ref[...].astype(F32) for ref in refs[nr:nr + npar]]
        res = list(fn(ids, *vals, *pvals))
        o = 0
        for spec, ref in zip(outs, refs[nr + npar:]):
            n = spec.n_pieces()
            _store_pieces(ref, spec.splits, spec.lead, res[o:o + n])
            o += n

    nz = len(grid)
    pspecs = [pl.BlockSpec(p.shape, (lambda *ids, _n=p.ndim: (0,) * _n)) for p in params]
    res = pl.pallas_call(
        body, name=name, grid=grid,
        in_specs=[r.spec() for r in rows] + pspecs,
        out_specs=[pl.BlockSpec(o.bshape, o.imap) for o in outs],
        out_shape=[SDS(o.shape, o.dtype) for o in outs],
        compiler_params=_cp(nz),
    )(*[r.arr for r in rows], *params)
    return list(res)


def rowwise_bwd(fn, rows, params, cots, grid, name):
    nr, npar, nc = len(rows), len(params), len(cots)
    drows = [r for r in rows if r.diff]
    nz = len(grid)

    def body(*refs):
        ids = tuple(pl.program_id(a) for a in range(nz))
        row_refs, par_refs = refs[:nr], refs[nr:nr + npar]
        cot_refs = refs[nr + npar:nr + npar + nc]
        drow_refs = refs[nr + npar + nc:nr + npar + nc + len(drows)]
        dpar_refs = refs[nr + npar + nc + len(drows):]
        pieces, is_diff = [], []
        for r, ref in zip(rows, row_refs):
            ps = r.pieces(ref)
            pieces += ps
            is_diff += [r.diff] * len(ps)
        pvals = [ref[...].astype(F32) for ref in par_refs]
        dvals = [p for p, dflag in zip(pieces, is_diff) if dflag]
        nd = len(dvals)

        def f(*args):
            it = iter(args[:nd])
            full = [next(it) if dflag else p for p, dflag in zip(pieces, is_diff)]
            return tuple(fn(ids, *full, *args[nd:]))

        _, vjp = jax.vjp(f, *dvals, *pvals)
        cvals = []
        for c, ref in zip(cots, cot_refs):
            cvals += c.pieces(ref)
        g = vjp(tuple(cvals))
        o = 0
        first_inner = ids[-1] == 0
        for r, ref in zip(drows, drow_refs):
            n = r.n_pieces()
            gs = g[o:o + n]
            o += n
            if r.acc:
                @pl.when(first_inner)
                def _(ref=ref):
                    ref[...] = jnp.zeros_like(ref)
            _store_pieces(ref, r.splits, r.lead, gs, accumulate=r.acc)
        first = functools.reduce(jnp.logical_and, [i == 0 for i in ids])
        for ref, gp in zip(dpar_refs, g[nd:]):
            @pl.when(first)
            def _(ref=ref):
                ref[...] = jnp.zeros_like(ref)
            ref[...] += gp

    pspecs = [pl.BlockSpec(p.shape, (lambda *ids, _n=p.ndim: (0,) * _n)) for p in params]
    res = pl.pallas_call(
        body, name=name, grid=grid,
        in_specs=[r.spec() for r in rows] + pspecs + [c.spec() for c in cots],
        out_specs=[r.gspec() for r in drows] + pspecs,
        out_shape=[SDS(r.gshape, r.gdtype) for r in drows] + [SDS(p.shape, F32) for p in params],
        compiler_params=_cp(nz),
    )(*[r.arr for r in rows], *params, *[c.arr for c in cots])
    res = list(res)
    return res[:len(drows)], res[len(drows):]


def _sigmoid(x):
    return 0.5 * (jnp.tanh(0.5 * x) + 1.0)


def _silu(x):
    return x * _sigmoid(x)


def _normmod(x, gain, sc, sh):
    inv = lax.rsqrt(jnp.mean(x * x, axis=-1, keepdims=True) + EPS)
    return x * inv * gain * (1.0 + sc) + sh


def f_first(ids, x, gain, sc, sh):
    return x, _normmod(x, gain, sc, sh)


def f_resid_norm(ids, x, y, g, gain, sc, sh):
    xn = x + g * y
    return xn, _normmod(xn, gain, sc, sh)


@jax.custom_vjp
def _swiglu(gate, up):
    return _silu(gate) * up


def _swiglu_fwd(gate, up):
    return _silu(gate) * up, (gate, up)


def _swiglu_bwd(res, da):
    gate, up = res
    s = _sigmoid(gate)
    gs = gate * s
    return da * up * (s + gs * (1.0 - s)), da * gs


_swiglu.defvjp(_swiglu_fwd, _swiglu_bwd)


def loss_and_grad(x, y, tgt, g, S):
    nt = S // WT

    def body(x_ref, y_ref, t_ref, g_ref, part_ref, dx_ref, dy_ref, dg_ref):
        @pl.when(pl.program_id(0) == 0)
        def _():
            dg_ref[...] = jnp.zeros_like(dg_ref)

        yv = y_ref[...].astype(F32)
        gg = g_ref[...]
        e = x_ref[...] + gg * yv - t_ref[...]
        part_ref[...] = 0.5 * jnp.sum(e * e, axis=0, keepdims=True) * (1.0 / D)
        d = e * (1.0 / D)
        dx_ref[...] = d
        dy_ref[...] = (d * gg).astype(dy_ref.dtype)
        dg_ref[...] += jnp.sum(d * yv, axis=0, keepdims=True)

    row = pl.BlockSpec((WT, D), lambda i: (i, 0))
    vec = pl.BlockSpec((1, D), lambda i: (0, 0))
    return pl.pallas_call(
        body, name="loss_and_grad", grid=(nt,), in_specs=[row, row, row, vec],
        out_specs=[pl.BlockSpec((None, 1, D), lambda i: (i, 0, 0)), row, row, vec],
        out_shape=[SDS((nt, 1, D), F32), SDS((S, D), F32), SDS((S, D), BF16), SDS((1, D), F32)],
        compiler_params=_cp(1),
    )(x, y, tgt, g)


def _softplus(x):
    return jnp.maximum(x, 0.0) + jnp.log(1.0 + jnp.exp(-jnp.abs(x)))


def _chunk_tril(T):
    r = lax.broadcasted_iota(jnp.int32, (T, T), 0)
    c = lax.broadcasted_iota(jnp.int32, (T, T), 1)
    return jnp.where((r // GDN_C == c // GDN_C) & (c <= r), 1.0, 0.0).astype(F32)


def _dot_hi(a, b, dims=((1,), (0,))):
    return lax.dot_general(a, b, (dims, ((), ())), precision=lax.Precision.HIGHEST, preferred_element_type=F32)


def _dot_x3(a, b, dims=((1,), (0,))):
    return lax.dot_general(a, b, (dims, ((), ())), precision=lax.Precision.HIGH, preferred_element_type=F32)


def f_gdn_gates(ids, ab, alog, dtb):
    T = ab.shape[0]
    g = -jnp.exp(alog) * _softplus(ab + dtb)
    beta = _sigmoid(ab)
    gcum = _dot_x3(_chunk_tril(T), g)
    row = lax.broadcasted_iota(jnp.int32, (LANES, LANES), 0)
    sel = lambda k: jnp.where(row == k, 1.0, 0.0).astype(F32)
    gcs = [_dot_x3(gcum, sel(h)) for h in range(GDN_H)]
    bts = [_dot_x3(beta, sel(GDN_H + h)) for h in range(GDN_H)]
    return (*gcs, *bts)


def f_gdn_post(ids, *args):
    os_, zs, gain = args[:GDN_H], args[GDN_H:2 * GDN_H], args[2 * GDN_H]
    out = []
    for o, z in zip(os_, zs):
        inv = lax.rsqrt(jnp.mean(o * o, axis=-1, keepdims=True) + EPS)
        out.append(o * inv * gain * _silu(z))
    return tuple(out)


def _qknorm1(x, gain2, scale):
    lane = lax.broadcasted_iota(jnp.int32, x.shape, 1)
    lo = lane < DSW_DH
    x2 = x * x
    s_all = jnp.sum(x2, axis=-1, keepdims=True)
    s_lo = jnp.sum(jnp.where(lo, x2, 0.0), axis=-1, keepdims=True)
    ms = jnp.where(lo, s_lo, s_all - s_lo) * (1.0 / DSW_DH)
    return x * lax.rsqrt(ms + EPS) * (gain2 * scale)


def f_combine(ids, o0, o1, o2, l0, l1, l2):
    m = jnp.maximum(jnp.maximum(l0, l1), l2)
    e0, e1, e2 = jnp.exp(l0 - m), jnp.exp(l1 - m), jnp.exp(l2 - m)
    den = e0 + e1 + e2
    o = (e0 * o0 + e1 * o1 + e2 * o2) / den
    return o, m + jnp.log(den)


GDN_T = 512
HALO = 16


def _conv_pre(xx, w):
    acc = xx * w[3:4, :]
    for j in range(3):
        acc = acc + pltpu.roll(xx, shift=3 - j, axis=0) * w[j:j + 1, :]
    return acc


@jax.custom_vjp
def _qkv_act_core(pre, norm_on, scale):
    s = _silu(pre)
    r = lax.rsqrt(jnp.sum(s * s, axis=-1, keepdims=True) + EPS)
    return jnp.where(norm_on > 0.5, s * r * scale, s)


def _qkv_act_fwd(pre, norm_on, scale):
    return _qkv_act_core(pre, norm_on, scale), (pre, norm_on, scale)


def _qkv_act_bwd(res, dout):
    pre, norm_on, scale = res
    sig = _sigmoid(pre)
    s = pre * sig
    r = lax.rsqrt(jnp.sum(s * s, axis=-1, keepdims=True) + EPS)
    unit = s * r
    dn = dout * scale
    ds = jnp.where(norm_on > 0.5, r * (dn - unit * jnp.sum(dn * unit, axis=-1, keepdims=True)), dout)
    return ds * (sig + s * (1.0 - sig)), jnp.zeros_like(norm_on), jnp.zeros_like(scale)


_qkv_act_core.defvjp(_qkv_act_fwd, _qkv_act_bwd)


def _qkv_act(pre, cidx):
    norm_on = jnp.where(cidx < 2 * GDN_H, 1.0, 0.0).astype(F32)
    scale = jnp.where(cidx < GDN_H, GDN_DK ** -0.5, 1.0).astype(F32)
    return _qkv_act_core(pre, norm_on, scale)


def gdn_pre(proj, conv_w, S):
    nt = S // GDN_T
    hb = GDN_T // HALO

    def body(prev_ref, cur_ref, w_ref, o_ref):
        p, i = pl.program_id(0), pl.program_id(1)
        for h in range(GDN_H):
            cols = slice(LANES * h, LANES * (h + 1))
            prev = jnp.where(i > 0, prev_ref[:, cols].astype(F32), 0.0)
            xx = jnp.concatenate([prev, cur_ref[:, cols].astype(F32)], axis=0)
            pre = _conv_pre(xx, w_ref[:, cols])[HALO:]
            o_ref[h] = _qkv_act(pre, p * GDN_H + h).astype(o_ref.dtype)

    hv = GDN_H * LANES
    return pl.pallas_call(
        body, name="gdn_pre", grid=(3, nt),
        in_specs=[pl.BlockSpec((HALO, hv), lambda p, i: (jnp.maximum(i * hb - 1, 0), p)),
                  pl.BlockSpec((GDN_T, hv), lambda p, i: (i, p)),
                  pl.BlockSpec((4, hv), lambda p, i: (0, p))],
        out_specs=pl.BlockSpec((None, GDN_H, GDN_T, LANES), lambda p, i: (p, 0, i, 0)),
        out_shape=SDS((3, GDN_H, S, LANES), BF16),
        compiler_params=_cp(2),
    )(proj, proj, conv_w)


def gdn_pre_bwd(proj, conv_w, dqkv, S):
    nt = S // GDN_T
    hb = GDN_T // HALO
    last_h = S // HALO - 1

    def body(prev_ref, cur_ref, next_ref, w_ref, d_ref, dnext_ref, dx_ref, dw_ref):
        p, i = pl.program_id(0), pl.program_id(1)

        @pl.when(i == 0)
        def _():
            dw_ref[...] = jnp.zeros_like(dw_ref)

        for h in range(GDN_H):
            cols = slice(LANES * h, LANES * (h + 1))
            w = w_ref[:, cols]
            prev = jnp.where(i > 0, prev_ref[:, cols].astype(F32), 0.0)
            xx = jnp.concatenate([prev, cur_ref[:, cols].astype(F32), next_ref[:, cols].astype(F32)], axis=0)
            dnext = jnp.where(i < nt - 1, dnext_ref[h], 0.0)
            dd = jnp.concatenate([jnp.zeros((HALO, LANES), F32), d_ref[h], dnext], axis=0)
            pre = _conv_pre(xx, w)
            _, vjp = jax.vjp(lambda v, _c=p * GDN_H + h: _qkv_act(v, _c), pre)
            (dpre,) = vjp(dd)
            dx = dpre * w[3:4, :]
            R = dpre.shape[0]
            for j in range(3):
                dx = dx + pltpu.roll(dpre, shift=R - (3 - j), axis=0) * w[j:j + 1, :]
            dx_ref[:, cols] = dx[HALO:HALO + GDN_T].astype(dx_ref.dtype)
            own = HALO + GDN_T
            rows_w = [jnp.sum((dpre * pltpu.roll(xx, shift=3 - j, axis=0))[:own], axis=0, keepdims=True) for j in range(3)]
            rows_w.append(jnp.sum((dpre * xx)[:own], axis=0, keepdims=True))
            r4 = lax.broadcasted_iota(jnp.int32, (4, LANES), 0)
            dw = jnp.zeros((4, LANES), F32)
            for j in range(4):
                dw = dw + jnp.where(r4 == j, rows_w[j], 0.0)
            dw_ref[:, cols] += dw

    hv = GDN_H * LANES
    return pl.pallas_call(
        body, name="gdn_pre_bwd", grid=(3, nt),
        in_specs=[pl.BlockSpec((HALO, hv), lambda p, i: (jnp.maximum(i * hb - 1, 0), p)),
                  pl.BlockSpec((GDN_T, hv), lambda p, i: (i, p)),
                  pl.BlockSpec((HALO, hv), lambda p, i: (jnp.minimum((i + 1) * hb, last_h), p)),
                  pl.BlockSpec((4, hv), lambda p, i: (0, p)),
                  pl.BlockSpec((None, GDN_H, GDN_T, LANES), lambda p, i: (p, 0, i, 0)),
                  pl.BlockSpec((None, GDN_H, HALO, LANES), lambda p, i: (p, 0, jnp.minimum((i + 1) * hb, last_h), 0))],
        out_specs=[pl.BlockSpec((GDN_T, hv), lambda p, i: (i, p)),
                   pl.BlockSpec((4, hv), lambda p, i: (0, p))],
        out_shape=[SDS((S, 3 * hv), BF16), SDS((4, 3 * hv), F32)],
        compiler_params=_cp(2),
    )(proj, proj, proj, conv_w, dqkv, dqkv)


_DIMS = {"nn": ((1,), (0,)), "nt": ((1,), (1,)), "tn": ((0,), (0,))}


def _mm_raw(a, b, mode, hi):
    if hi:
        return _dot_hi(a, b, _DIMS[mode])
    return lax.dot_general(a.astype(BF16), b.astype(BF16), (_DIMS[mode], ((), ())), preferred_element_type=F32)


@functools.partial(jax.custom_vjp, nondiff_argnums=(2, 3))
def mm(a, b, mode, hi):
    return _mm_raw(a, b, mode, hi)


def _mm_fwd(a, b, mode, hi):
    return _mm_raw(a, b, mode, hi), (a, b)


def _mm_bwd(mode, hi, res, dc):
    a, b = res
    if mode == "nn":
        da, db = mm(dc, b, "nt", hi), mm(a, dc, "tn", hi)
    elif mode == "nt":
        da, db = mm(dc, b, "nn", hi), mm(dc, a, "tn", hi)
    else:
        da, db = mm(b, dc, "nt", hi), mm(a, dc, "nn", hi)
    return da, db


mm.defvjp(_mm_fwd, _mm_bwd)


TRI_BASE = 8


def _unit_lower_inverses(Ls):
    n = Ls[0].shape[0]
    r = lax.broadcasted_iota(jnp.int32, (n, n), 0)
    c = lax.broadcasted_iota(jnp.int32, (n, n), 1)
    eye = jnp.where(r == c, 1.0, 0.0).astype(F32)
    base = r // TRI_BASE == c // TRI_BASE
    one = lambda a, b_: _mm_raw(a, b_, "nn", False)
    Ps = [jnp.where(base, -L, 0.0) for L in Ls]
    invs = [eye + P for P in Ps]
    k = 1
    while 2 * k < TRI_BASE:
        Ps = [one(P, P) for P in Ps]
        invs = [inv + one(inv, P) for inv, P in zip(invs, Ps)]
        k *= 2
    b = 2 * TRI_BASE
    while b <= n:
        off_mask = (r // b == c // b) & ((r % b) >= b // 2) & ((c % b) < b // 2)
        ts = [one(inv, jnp.where(off_mask, L, 0.0)) for inv, L in zip(invs, Ls)]
        invs = [inv - one(t, inv) for inv, t in zip(invs, ts)]
        b *= 2
    resid = [eye - inv - _dot_x3(L, inv) for inv, L in zip(invs, Ls)]
    return [inv + _dot_x3(inv, rs) for inv, rs in zip(invs, resid)]


@jax.custom_vjp
def tri_apply(invs, Ls, r1s, r2s):
    return [_mm_raw(i, r, "nn", False) for i, r in zip(invs, r1s)], [_mm_raw(i, r, "nn", False) for i, r in zip(invs, r2s)]


def _tri_fwd(invs, Ls, r1s, r2s):
    s1s = [_mm_raw(i, r, "nn", False) for i, r in zip(invs, r1s)]
    s2s = [_mm_raw(i, r, "nn", False) for i, r in zip(invs, r2s)]
    return (s1s, s2s), (invs, s1s, s2s)


def _tri_bwd(res, ds):
    invs, s1s, s2s = res
    d1s = [_mm_raw(i, d, "tn", False) for i, d in zip(invs, ds[0])]
    d2s = [_mm_raw(i, d, "tn", False) for i, d in zip(invs, ds[1])]
    dLs = [-(_mm_raw(d1, s1, "nt", False) + _mm_raw(d2, s2, "nt", False)) for d1, s1, d2, s2 in zip(d1s, s1s, d2s, s2s)]
    return [jnp.zeros_like(i) for i in invs], dLs, d1s, d2s


tri_apply.defvjp(_tri_fwd, _tri_bwd)


def _gdn_chunk(qs, ks, vs, gcbs, btbs, Ss, invs=None):
    C = qs[0].shape[0]
    r = lax.broadcasted_iota(jnp.int32, (C, C), 0)
    c = lax.broadcasted_iota(jnp.int32, (C, C), 1)
    causal, strict = c <= r, c < r
    rows = lax.broadcasted_iota(jnp.int32, gcbs[0].shape, 0)
    Gs = [g[:, :C] for g in gcbs]
    decays = [jnp.exp(jnp.where(causal, G - G.T, NEG)) for G in Gs]
    kbs = [k * b for k, b in zip(ks, btbs)]
    vbs = [v * b for v, b in zip(vs, btbs)]
    Ls = [jnp.where(strict, mm(kb, k, "nt", False) * d, 0.0) for kb, k, d in zip(kbs, ks, decays)]
    egs = [jnp.exp(g) for g in gcbs]
    if invs is None:
        invs = _unit_lower_inverses(Ls)
    us, ws = tri_apply(invs, Ls, vbs, [kb * eg for kb, eg in zip(kbs, egs)])
    qks = [jnp.where(causal, mm(q, k, "nt", False) * d, 0.0) for q, k, d in zip(qs, ks, decays)]
    g_lasts = [jnp.sum(jnp.where(rows == C - 1, g, 0.0), axis=0, keepdims=True) for g in gcbs]
    q_decs = [q * eg for q, eg in zip(qs, egs)]
    k_decs = [k * jnp.exp(gl - g) for k, gl, g in zip(ks, g_lasts, gcbs)]
    v_news = [u - mm(w, S, "nn", False) for u, w, S in zip(us, ws, Ss)]
    os_ = [mm(qd, S, "nn", False) + mm(qk, vn, "nn", False) for qd, S, qk, vn in zip(q_decs, Ss, qks, v_news)]
    S_news = [S * jnp.exp(gl) + mm(kd, vn, "tn", False) for S, gl, kd, vn in zip(Ss, g_lasts, k_decs, v_news)]
    return os_, S_news, invs


INV_CHUNKS = 4
SCAN_CHUNKS = 4


def gdn_inverses(qkv, gc, bt, S):
    nchunk = S // GDN_C
    rows = INV_CHUNKS * GDN_C

    def body(k_ref, g_ref, b_ref, inv_ref):
        items = [(h, m) for m in range(INV_CHUNKS) for h in range(GDN_H)]
        r = lax.broadcasted_iota(jnp.int32, (GDN_C, GDN_C), 0)
        c = lax.broadcasted_iota(jnp.int32, (GDN_C, GDN_C), 1)
        sl = lambda m: slice(m * GDN_C, (m + 1) * GDN_C)
        ks = [k_ref[h, sl(m), :].astype(F32) for h, m in items]
        Gs = [g_ref[h, sl(m), :GDN_C] for h, m in items]
        kbs = [k * b_ref[h, sl(m), :] for k, (h, m) in zip(ks, items)]
        decays = [jnp.exp(jnp.where(c <= r, G - G.T, NEG)) for G in Gs]
        Ls = [jnp.where(c < r, _mm_raw(kb, k, "nt", False) * d, 0.0) for kb, k, d in zip(kbs, ks, decays)]
        for (h, m), inv in zip(items, _unit_lower_inverses(Ls)):
            inv_ref[h, m] = inv

    hb = pl.BlockSpec((GDN_H, rows, LANES), lambda n: (0, n, 0))
    return pl.pallas_call(
        body, name="gdn_inverses", grid=(nchunk // INV_CHUNKS,),
        in_specs=[pl.BlockSpec((None, GDN_H, rows, LANES), lambda n: (1, 0, n, 0)), hb, hb],
        out_specs=pl.BlockSpec((GDN_H, INV_CHUNKS, GDN_C, GDN_C), lambda n: (0, n, 0, 0)),
        out_shape=SDS((GDN_H, nchunk, GDN_C, GDN_C), F32),
        compiler_params=_cp(1),
    )(qkv, gc, bt)


def gdn_core(qkv, gc, bt, invs, S):
    nchunk = S // GDN_C

    def body(qkv_ref, g_ref, b_ref, inv_ref, o_ref, st_ref, s_scr):
        n = pl.program_id(0)

        @pl.when(n == 0)
        def _():
            s_scr[...] = jnp.zeros_like(s_scr)

        heads = range(GDN_H)
        S_cur = [s_scr[h] for h in heads]
        for m in range(SCAN_CHUNKS):
            sl = slice(m * GDN_C, (m + 1) * GDN_C)
            os_, S_new, _ = _gdn_chunk(*[[qkv_ref[p, h, sl, :].astype(F32) for h in heads] for p in range(3)],
                                       [g_ref[h, sl, :] for h in heads], [b_ref[h, sl, :] for h in heads], S_cur,
                                       invs=[inv_ref[h, m] for h in heads])
            for h in heads:
                st_ref[h, m] = S_cur[h].astype(st_ref.dtype)
                o_ref[h, sl, :] = os_[h]
            S_cur = S_new
        for h in heads:
            s_scr[h] = S_cur[h]

    rows = SCAN_CHUNKS * GDN_C
    blk3 = pl.BlockSpec((3, GDN_H, rows, LANES), lambda n: (0, 0, n, 0))
    hb = pl.BlockSpec((GDN_H, rows, LANES), lambda n: (0, n, 0))
    return pl.pallas_call(
        body, name="gdn_core", grid=(nchunk // SCAN_CHUNKS,),
        in_specs=[blk3, hb, hb, pl.BlockSpec((GDN_H, SCAN_CHUNKS, GDN_C, GDN_C), lambda n: (0, n, 0, 0))],
        out_specs=[hb, pl.BlockSpec((GDN_H, SCAN_CHUNKS, GDN_DK, LANES), lambda n: (0, n, 0, 0))],
        out_shape=[SDS((GDN_H, S, LANES), F32), SDS((GDN_H, nchunk, GDN_DK, LANES), BF16)],
        scratch_shapes=[pltpu.VMEM((GDN_H, GDN_DK, LANES), F32)],
        compiler_params=_cp(1),
    )(qkv, gc, bt, invs)


def gdn_core_bwd(qkv, gc, bt, states, invs, do, S):
    nchunk = S // GDN_C

    def body(qkv_ref, g_ref, b_ref, st_ref, inv_ref, do_ref, dqkv_ref, dg_ref, db_ref, ds_scr):
        n = pl.program_id(0)

        @pl.when(n == 0)
        def _():
            ds_scr[...] = jnp.zeros_like(ds_scr)

        heads = range(GDN_H)
        dS_cur = [ds_scr[h] for h in heads]
        for m in reversed(range(SCAN_CHUNKS)):
            sl = slice(m * GDN_C, (m + 1) * GDN_C)
            saved = [inv_ref[h, m] for h in heads]
            _, vjp = jax.vjp(lambda *a, _s=saved: _gdn_chunk(*a, invs=_s)[:2],
                             *[[qkv_ref[p, h, sl, :].astype(F32) for h in heads] for p in range(3)],
                             [g_ref[h, sl, :] for h in heads], [b_ref[h, sl, :] for h in heads],
                             [st_ref[h, m].astype(F32) for h in heads])
            dq, dk, dv, dg, db, dS_cur = vjp(([do_ref[h, sl, :] for h in heads], dS_cur))
            for h in heads:
                dqkv_ref[0, h, sl, :] = dq[h]
                dqkv_ref[1, h, sl, :] = dk[h]
                dqkv_ref[2, h, sl, :] = dv[h]
                dg_ref[h, sl, :] = dg[h]
                db_ref[h, sl, :] = db[h]
        for h in heads:
            ds_scr[h] = dS_cur[h]

    nblk = nchunk // SCAN_CHUNKS
    rows = SCAN_CHUNKS * GDN_C
    rev = lambda n: nblk - 1 - n
    blk3 = pl.BlockSpec((3, GDN_H, rows, LANES), lambda n: (0, 0, rev(n), 0))
    hb = pl.BlockSpec((GDN_H, rows, LANES), lambda n: (0, rev(n), 0))
    return pl.pallas_call(
        body, name="gdn_core_bwd", grid=(nblk,),
        in_specs=[blk3, hb, hb, pl.BlockSpec((GDN_H, SCAN_CHUNKS, GDN_DK, LANES), lambda n: (0, rev(n), 0, 0)),
                  pl.BlockSpec((GDN_H, SCAN_CHUNKS, GDN_C, GDN_C), lambda n: (0, rev(n), 0, 0)), hb],
        out_specs=[blk3, hb, hb],
        out_shape=[SDS((3, GDN_H, S, LANES), F32), SDS((GDN_H, S, LANES), F32), SDS((GDN_H, S, LANES), F32)],
        scratch_shapes=[pltpu.VMEM((GDN_H, GDN_DK, LANES), F32)],
        compiler_params=_cp(1),
    )(qkv, gc, bt, states, invs, do)


GDN_MAIN = 4 * GDN_H * LANES
GDN_PROJ = GDN_MAIN + LANES
RT = 512


def gdn_forward(h, w_in, conv_w, alog, dtb, out_gain, w_out):
    S = h.shape[0]
    nt = S // RT
    proj = matmul(h, w_in, "nn", BF16, "gdn_in")
    qkv = gdn_pre(proj, conv_w, S)
    ab_row = Row(proj, (RT, LANES), lambda i: (i, GDN_MAIN // LANES), gdtype=BF16, gshape=(S, LANES), gimap=lambda i: (i, 0))
    hm = lambda i: (0, i, 0)
    hv = GDN_H * LANES
    gc, bt = rowwise(f_gdn_gates, [ab_row], [alog, dtb],
                     [Out((GDN_H, S, LANES), F32, (GDN_H, RT, LANES), hm, lead=GDN_H)] * 2, (nt,), "gdn_gates")
    invs = gdn_inverses(qkv, gc, bt, S)
    o, states = gdn_core(qkv, gc, bt, invs, S)
    o_row = Row(o, (GDN_H, RT, LANES), hm, lead=GDN_H)
    z_row = Row(proj, (RT, hv), lambda i: (i, 3), splits=[LANES] * GDN_H, gdtype=BF16, gshape=(S, hv), gimap=lambda i: (i, 0))
    (on,) = rowwise(f_gdn_post, [o_row, z_row], [out_gain],
                    [Out((S, hv), BF16, (RT, hv), lambda i: (i, 0), splits=[LANES] * GDN_H)], (nt,), "gdn_post")
    y = matmul(on, w_out, "nn", BF16, "gdn_out")
    saved = dict(h=h, proj=proj, qkv=qkv, gc=gc, bt=bt, states=states, invs=invs, o=o, on=on, ab_row=ab_row, o_row=o_row, z_row=z_row)
    return y, saved


def gdn_backward(dy, sv, w_in, conv_w, alog, dtb, out_gain, w_out, on_weight_grads=None):
    S = dy.shape[0]
    nt = S // RT
    hm = lambda i: (0, i, 0)
    hv = GDN_H * LANES
    don = matmul(dy, w_out, "nt", BF16, "gdn_out_dx")
    d_w_out = matmul(sv["on"], dy, "tn", F32, "gdn_out_dw")
    (do, dz), (d_gain,) = rowwise_bwd(f_gdn_post, [sv["o_row"], sv["z_row"]], [out_gain],
                                      [Row(don, (RT, hv), lambda i: (i, 0), splits=[LANES] * GDN_H)], (nt,), "gdn_post_bwd")
    dqkv, dgc, dbt = gdn_core_bwd(sv["qkv"], sv["gc"], sv["bt"], sv["states"], sv["invs"], do, S)
    head_blk = lambda a: Row(a, (GDN_H, RT, LANES), hm, lead=GDN_H)
    (dab,), (d_alog, d_dtb) = rowwise_bwd(f_gdn_gates, [sv["ab_row"]], [alog, dtb], [head_blk(dgc), head_blk(dbt)],
                                          (nt,), "gdn_gates_bwd")
    dqkv_proj, d_conv = gdn_pre_bwd(sv["proj"], conv_w, dqkv, S)
    dproj = jnp.concatenate([dqkv_proj, dz, dab], axis=1)
    d_w_in = matmul(sv["h"], dproj, "tn", F32, "gdn_in_dw")
    if on_weight_grads is not None:
        w_in = w_in + on_weight_grads(d_w_in, d_w_out).astype(w_in.dtype)
    dh = matmul(dproj, w_in, "nt", BF16, "gdn_in_dx")
    return dh, dict(w_in=d_w_in, conv=d_conv, alog=d_alog, dtb=d_dtb, gain=d_gain, w_out=d_w_out)


QB = DSW_SPAN
N_HP = DSW_HG // LANES


def _bucket_maps():
    a = np.arange(QB)[:, None]
    j = np.arange(2 * QB)[None, :]
    dist = QB + a - j
    band = (dist >= 0) & (dist <= DSW_SPAN)
    maps = []
    for _, dil in DSW_GROUPS:
        dd = np.maximum(dist, 0) * dil
        max_exact = REL_BUCKETS // 2
        scaled = np.log(np.maximum(dd, 1).astype(np.float32) / np.float32(max_exact)) / np.float32(math.log(REL_MAX_DIST / max_exact))
        large = max_exact + (scaled * np.float32(REL_BUCKETS - max_exact)).astype(np.int32)
        large = np.minimum(large, REL_BUCKETS - 1)
        maps.append(np.where(dd < max_exact, dd, large).astype(np.int32))
    return np.stack(maps), band


def dsw_bias(rel_bias):
    maps, band = _bucket_maps()
    maps = np.where(band[None], maps, -1).astype(np.int32)

    def body(tab_ref, bk_ref, o_ref):
        gh = pl.program_id(0)
        bk = bk_ref[...]
        acc = jnp.full(bk.shape, NEG, F32)
        for b in range(REL_BUCKETS):
            acc = jnp.where(bk == b, tab_ref[b, gh], acc)
        o_ref[...] = acc

    return pl.pallas_call(
        body, name="dsw_bias", grid=(3 * GDN_H,),
        in_specs=[pl.BlockSpec(memory_space=pltpu.SMEM),
                  pl.BlockSpec((None, QB, 2 * QB), lambda gh: (gh // GDN_H, 0, 0))],
        out_specs=pl.BlockSpec((None, QB, 2 * QB), lambda gh: (gh, 0, 0)),
        out_shape=SDS((3 * GDN_H, QB, 2 * QB), F32),
        compiler_params=_cp(1),
    )(rel_bias, jnp.asarray(maps))


def dsw_bias_grad(dbias):
    maps, band = _bucket_maps()
    maps = np.where(band[None], maps, -1).astype(np.int32)

    def body(d_ref, bk_ref, o_ref):
        bk = bk_ref[...]
        d = d_ref[...]
        rows = lax.broadcasted_iota(jnp.int32, (REL_BUCKETS, LANES), 0)
        acc = jnp.zeros((REL_BUCKETS, LANES), F32)
        for b in range(REL_BUCKETS):
            part = jnp.sum(jnp.where(bk == b, d, 0.0), axis=0, keepdims=True)
            val = jnp.sum(part, axis=1, keepdims=True)
            acc = jnp.where(rows == b, val, acc)
        o_ref[...] = acc

    return pl.pallas_call(
        body, name="dsw_bias_grad", grid=(3 * GDN_H,),
        in_specs=[pl.BlockSpec((None, QB, 2 * QB), lambda gh: (gh, 0, 0)),
                  pl.BlockSpec((None, QB, 2 * QB), lambda gh: (gh // GDN_H, 0, 0))],
        out_specs=pl.BlockSpec((None, REL_BUCKETS, LANES), lambda gh: (gh, 0, 0)),
        out_shape=SDS((3 * GDN_H, REL_BUCKETS, LANES), F32),
        compiler_params=_cp(1),
    )(dbias, jnp.asarray(maps))


def _nt(a, b):
    return lax.dot_general(a, b, (((1,), (1,)), ((), ())), preferred_element_type=F32)


def _tn(a, b):
    return lax.dot_general(a, b, (((0,), (0,)), ((), ())), preferred_element_type=F32)


N_LB = DSW_HG // LANES
HALF = DSW_DH // 2


def _lanes(j):
    return slice(LANES * j, LANES * (j + 1))


def _deinterleave(stage, out_ref, dil, rows, dtype):
    for r in range(dil):
        for j in range(N_LB):
            out_ref[r, :, _lanes(j)] = stage[j, pl.ds(r, rows, stride=dil), :].astype(dtype)


def _interleave(in_ref, stage, dil, rows):
    for r in range(dil):
        for j in range(N_LB):
            stage[j, pl.ds(r, rows, stride=dil), :] = in_ref[r, :, _lanes(j)].astype(F32)


def dsw_prep(proj, q_gain2, k_gain2, gi, S):
    dil = DSW_GROUPS[gi][1]
    nt, rows = S // RT, RT // dil

    def body(q_ref, k_ref, v_ref, qg_ref, kg_ref, qo_ref, ko_ref, vo_ref, stage):
        for src, gain_ref, scale, dst in ((q_ref, qg_ref, DSW_DH ** -0.5, qo_ref), (k_ref, kg_ref, 1.0, ko_ref), (v_ref, None, None, vo_ref)):
            for j in range(N_LB):
                val = src[:, _lanes(j)].astype(F32)
                val = val if gain_ref is None else _qknorm1(val, gain_ref[...], scale)
                if dil == 1:
                    dst[0, :, _lanes(j)] = val.astype(BF16)
                else:
                    stage[j] = val
            if dil > 1:
                _deinterleave(stage, dst, dil, rows, BF16)

    col = lambda which: pl.BlockSpec((RT, DSW_HG), lambda i, _c=which * 3 + gi: (i, _c))
    gspec = pl.BlockSpec((1, LANES), lambda i: (0, 0))
    ospec = pl.BlockSpec((dil, rows, DSW_HG), lambda i: (0, i, 0))
    return pl.pallas_call(
        body, name=f"dsw_prep_g{gi}", grid=(nt,),
        in_specs=[col(0), col(1), col(2), gspec, gspec], out_specs=[ospec] * 3,
        out_shape=[SDS((dil, S // dil, DSW_HG), BF16)] * 3,
        scratch_shapes=[pltpu.VMEM((N_LB, RT, LANES), F32)], compiler_params=_cp(1),
    )(proj, proj, proj, q_gain2, k_gain2)


def dsw_prep_bwd(proj, q_gain2, k_gain2, dqd, dkd, dvd, gi, S):
    dil = DSW_GROUPS[gi][1]
    nt, rows = S // RT, RT // dil

    def body(q_ref, k_ref, qg_ref, kg_ref, dq_ref, dk_ref, dv_ref, oq_ref, ok_ref, ov_ref, dqg_ref, dkg_ref, stage):
        i = pl.program_id(0)

        @pl.when(i == 0)
        def _():
            dqg_ref[...] = jnp.zeros_like(dqg_ref)
            dkg_ref[...] = jnp.zeros_like(dkg_ref)

        for src, gain_ref, scale, cot_ref, dst, dg_ref in ((q_ref, qg_ref, DSW_DH ** -0.5, dq_ref, oq_ref, dqg_ref),
                                                          (k_ref, kg_ref, 1.0, dk_ref, ok_ref, dkg_ref)):
            if dil > 1:
                _interleave(cot_ref, stage, dil, rows)
            for j in range(N_LB):
                _, vjp = jax.vjp(lambda x, g, _s=scale: _qknorm1(x, g, _s), src[:, _lanes(j)].astype(F32), gain_ref[...])
                dx, dg = vjp(stage[j] if dil > 1 else cot_ref[0, :, _lanes(j)].astype(F32))
                dst[:, _lanes(j)] = dx.astype(dst.dtype)
                dg_ref[...] += dg
        if dil > 1:
            _interleave(dv_ref, stage, dil, rows)
        for j in range(N_LB):
            ov_ref[:, _lanes(j)] = (stage[j] if dil > 1 else dv_ref[0, :, _lanes(j)]).astype(ov_ref.dtype)

    col = lambda which: pl.BlockSpec((RT, DSW_HG), lambda i, _c=which * 3 + gi: (i, _c))
    gspec = pl.BlockSpec((1, LANES), lambda i: (0, 0))
    dspec = pl.BlockSpec((dil, rows, DSW_HG), lambda i: (0, i, 0))
    nspec = pl.BlockSpec((RT, DSW_HG), lambda i: (i, 0))
    return pl.pallas_call(
        body, name=f"dsw_prep_bwd_g{gi}", grid=(nt,),
        in_specs=[col(0), col(1), gspec, gspec, dspec, dspec, dspec], out_specs=[nspec] * 3 + [gspec] * 2,
        out_shape=[SDS((S, DSW_HG), BF16)] * 3 + [SDS((1, LANES), F32)] * 2,
        scratch_shapes=[pltpu.VMEM((N_LB, RT, LANES), F32)], compiler_params=_cp(1),
    )(proj, proj, q_gain2, k_gain2, dqd, dkd, dvd)


def _head_masks(rows):
    lane = lax.broadcasted_iota(jnp.int32, (rows, LANES), 1)
    return lane < DSW_DH, (lane % DSW_DH) < HALF


def dsw_attn_fwd(qd, kd, vd, bias, gi, S):
    dil = DSW_GROUPS[gi][1]
    sd = S // dil
    nq = sd // QB
    QPS = 2 if nq % 2 == 0 else 1

    def body(q_ref, k_ref, v_ref, b_ref, o_ref, l_ref, kp_scr, vp_scr):
        i = pl.program_id(1)

        @pl.when(i == 0)
        def _():
            kp_scr[...] = jnp.zeros_like(kp_scr)
            vp_scr[...] = jnp.zeros_like(vp_scr)

        lo_q, _ = _head_masks(QB)
        lo_k, _ = _head_masks(2 * QB)
        col = lax.broadcasted_iota(jnp.int32, (QB, 2 * QB), 1)
        first = jnp.logical_and(i == 0, col < QB)
        hps, heads = range(N_HP), range(2 * N_HP)
        mqs = [lo_q if h % 2 == 0 else jnp.logical_not(lo_q) for h in heads]
        mks = [lo_k if h % 2 == 0 else jnp.logical_not(lo_k) for h in heads]
        k_last, v_last = k_ref[(QPS - 1) * QB:, :], v_ref[(QPS - 1) * QB:, :]
        for m in range(QPS):
            rows = slice(m * QB, (m + 1) * QB)
            before = slice((m - 1) * QB, m * QB)
            kps = [kp_scr[:, _lanes(hp)] if m == 0 else k_ref[before, _lanes(hp)] for hp in hps]
            vps = [vp_scr[:, _lanes(hp)] if m == 0 else v_ref[before, _lanes(hp)] for hp in hps]
            k2s = [jnp.concatenate([kps[hp], k_ref[rows, _lanes(hp)]], axis=0) for hp in hps]
            v2s = [jnp.concatenate([vps[hp], v_ref[rows, _lanes(hp)]], axis=0) for hp in hps]
            qs = [q_ref[rows, _lanes(hp)] for hp in hps]
            ss = [_nt(jnp.where(mqs[h], qs[h // 2], 0).astype(BF16), k2s[h // 2]) + b_ref[h] for h in heads]
            if m == 0:
                ss = [jnp.where(first, NEG, s) for s in ss]
            mxs = [jnp.max(s, axis=1, keepdims=True) for s in ss]
            ps = [jnp.exp(s - mx) for s, mx in zip(ss, mxs)]
            ls = [jnp.sum(p, axis=1, keepdims=True) for p in ps]
            ohs = [jnp.dot(ps[h].astype(BF16), jnp.where(mks[h], v2s[h // 2], 0).astype(BF16), preferred_element_type=F32) / ls[h]
                   for h in heads]
            lse_h = [mx + jnp.log(l) for mx, l in zip(mxs, ls)]
            for hp in hps:
                o_ref[rows, _lanes(hp)] = ohs[2 * hp] + ohs[2 * hp + 1]
                l_ref[rows, _lanes(hp)] = jnp.where(lo_q, lse_h[2 * hp], lse_h[2 * hp + 1])
        kp_scr[...] = k_last
        vp_scr[...] = v_last

    blk = pl.BlockSpec((None, QPS * QB, DSW_HG), lambda r, i: (r, i, 0))
    return pl.pallas_call(
        body, name=f"dsw_attn_g{gi}", grid=(dil, nq // QPS),
        in_specs=[blk, blk, blk, pl.BlockSpec((GDN_H, QB, 2 * QB), lambda r, i: (gi, 0, 0))],
        out_specs=[blk, blk], out_shape=[SDS((dil, sd, DSW_HG), F32)] * 2,
        scratch_shapes=[pltpu.VMEM((QB, DSW_HG), BF16)] * 2, compiler_params=_cp(2),
    )(qd, kd, vd, bias)


def dsw_attn_bwd(qd, kd, vd, bias, dod, statd, gi, S):
    dil = DSW_GROUPS[gi][1]
    sd = S // dil
    nq = sd // QB
    cur = lambda i: jnp.minimum(i, nq - 1)
    done = lambda i: jnp.maximum(i - 1, 0)

    def body(q_ref, k_ref, v_ref, b_ref, do_ref, st_ref, dq_ref, dk_ref, dv_ref, db_ref, kp_scr, vp_scr, dk_scr, dv_scr):
        r, i = pl.program_id(0), pl.program_id(1)

        @pl.when(jnp.logical_and(r == 0, i == 0))
        def _():
            db_ref[...] = jnp.zeros_like(db_ref)

        @pl.when(i == 0)
        def _():
            for scr in (kp_scr, vp_scr, dk_scr, dv_scr):
                scr[...] = jnp.zeros_like(scr)

        @pl.when(i < nq)
        def _():
            lo_q, first_half = _head_masks(QB)
            col = lax.broadcasted_iota(jnp.int32, (QB, 2 * QB), 1)
            first = jnp.logical_and(i == 0, col < QB)
            hps, heads = range(N_HP), range(2 * N_HP)
            k2s = [jnp.concatenate([kp_scr[:, _lanes(hp)], k_ref[:, _lanes(hp)]], axis=0) for hp in hps]
            v2s = [jnp.concatenate([vp_scr[:, _lanes(hp)], v_ref[:, _lanes(hp)]], axis=0) for hp in hps]
            qs = [q_ref[:, _lanes(hp)] for hp in hps]
            douts = [do_ref[:, _lanes(hp)] for hp in hps]
            stats = [st_ref[:, _lanes(hp)] for hp in hps]
            dkc = [dk_scr[:, _lanes(hp)] for hp in hps]
            dvc = [dv_scr[:, _lanes(hp)] for hp in hps]
            k_now, v_now = k_ref[...], v_ref[...]
            mqs = [lo_q if h % 2 == 0 else jnp.logical_not(lo_q) for h in heads]
            qms = [jnp.where(mqs[h], qs[h // 2], 0).astype(BF16) for h in heads]
            doms = [jnp.where(mqs[h], douts[h // 2], 0).astype(BF16) for h in heads]
            ss = [jnp.where(first, NEG, _nt(qms[h], k2s[h // 2]) + b_ref[h]) for h in heads]
            lses = [jnp.max(jnp.where(jnp.logical_and(mqs[h], first_half), stats[h // 2], NEG), axis=1, keepdims=True) for h in heads]
            deltas = [jnp.max(jnp.where(jnp.logical_and(mqs[h], jnp.logical_not(first_half)), stats[h // 2], NEG), axis=1, keepdims=True)
                      for h in heads]
            ps = [jnp.exp(ss[h] - lses[h]) for h in heads]
            dss = [ps[h] * (_nt(doms[h], v2s[h // 2]) - deltas[h]) for h in heads]
            dsbs = [d.astype(BF16) for d in dss]
            dqh = [jnp.where(mqs[h], jnp.dot(dsbs[h], k2s[h // 2], preferred_element_type=F32), 0.0) for h in heads]
            dkh = [_tn(dsbs[h], qms[h]) for h in heads]
            dvh = [_tn(ps[h].astype(BF16), doms[h]) for h in heads]
            for h in heads:
                db_ref[h] += dss[h]
            for hp in hps:
                dk2 = dkh[2 * hp] + dkh[2 * hp + 1]
                dv2 = dvh[2 * hp] + dvh[2 * hp + 1]
                dq_ref[:, _lanes(hp)] = dqh[2 * hp] + dqh[2 * hp + 1]
                dk_ref[:, _lanes(hp)] = dkc[hp] + dk2[:QB]
                dv_ref[:, _lanes(hp)] = (dvc[hp] + dv2[:QB]).astype(dv_ref.dtype)
                dk_scr[:, _lanes(hp)] = dk2[QB:]
                dv_scr[:, _lanes(hp)] = dv2[QB:]
            kp_scr[...] = k_now
            vp_scr[...] = v_now

        @pl.when(i == nq)
        def _():
            dk_ref[...] = dk_scr[...]
            dv_ref[...] = dv_scr[...].astype(dv_ref.dtype)

    blk = pl.BlockSpec((None, QB, DSW_HG), lambda r, i: (r, cur(i), 0))
    oblk = pl.BlockSpec((None, QB, DSW_HG), lambda r, i: (r, done(i), 0))
    return pl.pallas_call(
        body, name=f"dsw_attn_bwd_g{gi}", grid=(dil, nq + 1),
        in_specs=[blk, blk, blk, pl.BlockSpec((GDN_H, QB, 2 * QB), lambda r, i: (gi, 0, 0)), blk, blk],
        out_specs=[blk, oblk, oblk, pl.BlockSpec((GDN_H, QB, 2 * QB), lambda r, i: (0, 0, 0))],
        out_shape=[SDS((dil, sd, DSW_HG), F32), SDS((dil, sd, DSW_HG), F32), SDS((dil, sd, DSW_HG), BF16),
                   SDS((GDN_H, QB, 2 * QB), F32)],
        scratch_shapes=[pltpu.VMEM((QB, DSW_HG), BF16)] * 2 + [pltpu.VMEM((QB, DSW_HG), F32)] * 2,
        compiler_params=_cp(2),
    )(qd, kd, vd, bias, dod, statd)


def dsw_combine(ods, lseds, S):
    nt = S // RT
    dils = [d for _, d in DSW_GROUPS]

    def body(*refs):
        ins, (o_ref, l_ref), stages = refs[:6], refs[6:8], refs[8:]
        for g in range(3):
            if dils[g] > 1:
                _interleave(ins[g], stages[g], dils[g], RT // dils[g])
                _interleave(ins[3 + g], stages[3 + g], dils[g], RT // dils[g])
        for j in range(N_LB):
            natural = lambda a: stages[a][j] if dils[a % 3] > 1 else ins[a][0, :, _lanes(j)]
            o, lse = f_combine(None, *[natural(a) for a in range(6)])
            o_ref[:, _lanes(j)] = o.astype(o_ref.dtype)
            l_ref[:, _lanes(j)] = lse

    dspec = lambda d: pl.BlockSpec((d, RT // d, DSW_HG), lambda i: (0, i, 0))
    nspec = pl.BlockSpec((RT, DSW_HG), lambda i: (i, 0))
    return pl.pallas_call(
        body, name="dsw_combine", grid=(nt,),
        in_specs=[dspec(d) for d in dils] * 2, out_specs=[nspec, nspec],
        out_shape=[SDS((S, DSW_HG), BF16), SDS((S, DSW_HG), F32)],
        scratch_shapes=[pltpu.VMEM((N_LB, RT, LANES), F32)] * 6, compiler_params=_cp(1),
    )(*ods, *lseds)


def dsw_bwd_prep(do, o, lse, S):
    nt = S // RT
    dils = [d for _, d in DSW_GROUPS]

    def body(do_ref, o_ref, l_ref, *rest):
        outs, (st_do, st_stat) = rest[:6], rest[6:]
        lo, first_half = _head_masks(RT)
        for j in range(N_LB):
            dout = do_ref[:, _lanes(j)]
            prod = dout * o_ref[:, _lanes(j)].astype(F32)
            s_all = jnp.sum(prod, axis=1, keepdims=True)
            s_lo = jnp.sum(jnp.where(lo, prod, 0.0), axis=1, keepdims=True)
            delta = jnp.where(lo, s_lo, s_all - s_lo)
            stat = jnp.where(first_half, l_ref[:, _lanes(j)], delta)
            st_do[j] = dout
            st_stat[j] = stat
            for g in range(3):
                if dils[g] == 1:
                    outs[g][0, :, _lanes(j)] = dout.astype(BF16)
                    outs[3 + g][0, :, _lanes(j)] = stat
        for g in range(3):
            if dils[g] > 1:
                _deinterleave(st_do, outs[g], dils[g], RT // dils[g], BF16)
                _deinterleave(st_stat, outs[3 + g], dils[g], RT // dils[g], F32)

    nspec = pl.BlockSpec((RT, DSW_HG), lambda i: (i, 0))
    dspec = lambda d: pl.BlockSpec((d, RT // d, DSW_HG), lambda i: (0, i, 0))
    res = pl.pallas_call(
        body, name="dsw_bwd_prep", grid=(nt,),
        in_specs=[nspec] * 3, out_specs=[dspec(d) for d in dils] * 2,
        out_shape=[SDS((d, S // d, DSW_HG), BF16) for d in dils] + [SDS((d, S // d, DSW_HG), F32) for d in dils],
        scratch_shapes=[pltpu.VMEM((N_LB, RT, LANES), F32)] * 2, compiler_params=_cp(1),
    )(do, o, lse)
    return res[:3], res[3:]


def dsw_forward(h, w_in, q_gain2, k_gain2, rel_bias, w_out):
    S = h.shape[0]
    proj = matmul(h, w_in, "nn", BF16, "dsw_in", col_shards=N_SHARD)
    bias = dsw_bias(rel_bias)
    qkv, ods, lseds = [], [], []
    for gi in range(3):
        qd, kd, vd = dsw_prep(proj, q_gain2, k_gain2, gi, S)
        od, ld = dsw_attn_fwd(qd, kd, vd, bias, gi, S)
        qkv.append((qd, kd, vd))
        ods.append(od)
        lseds.append(ld)
    o, lse = dsw_combine(ods, lseds, S)
    y = matmul(o, w_out, "nn", BF16, "dsw_out", col_shards=N_SHARD)
    return y, dict(h=h, proj=proj, qkv=qkv, bias=bias, o=o, lse=lse)


def dsw_backward(dy, sv, w_in, q_gain2, k_gain2, w_out):
    S = dy.shape[0]
    do = matmul(dy, w_out, "nt", F32, "dsw_out_dx", col_shards=N_SHARD)
    d_w_out = matmul(sv["o"], dy, "tn", F32, "dsw_out_dw", col_shards=N_SHARD)
    dods, statds = dsw_bwd_prep(do, sv["o"], sv["lse"], S)
    pieces_q, pieces_k, pieces_v, dbs = [], [], [], []
    d_qg = jnp.zeros((1, LANES), F32)
    d_kg = jnp.zeros((1, LANES), F32)
    for gi in range(3):
        qd, kd, vd = sv["qkv"][gi]
        dqd, dkd, dvd, db = dsw_attn_bwd(qd, kd, vd, sv["bias"], dods[gi], statds[gi], gi, S)
        dq, dk, dv, dqg, dkg = dsw_prep_bwd(sv["proj"], q_gain2, k_gain2, dqd, dkd, dvd, gi, S)
        dbs.append(db)
        pieces_q.append(dq)
        pieces_k.append(dk)
        pieces_v.append(dv)
        d_qg = d_qg + dqg
        d_kg = d_kg + dkg
    dproj = jnp.concatenate(pieces_q + pieces_k + pieces_v, axis=1)
    d_w_in = matmul(sv["h"], dproj, "tn", F32, "dsw_in_dw", col_shards=N_SHARD)
    dh = matmul(dproj, w_in, "nt", BF16, "dsw_in_dx", col_shards=N_SHARD)
    d_rel = dsw_bias_grad(jnp.concatenate(dbs, axis=0))
    return dh, dict(w_in=d_w_in, q_gain2=d_qg, k_gain2=d_kg, rel=d_rel, w_out=d_w_out)


FUSE_M = 512


def ffn_in_act(h, w_in, name):
    S = h.shape[0]
    half = FFN // 2

    def body(h_ref, wg_ref, wu_ref, gu_ref, a_ref):
        j = pl.program_id(1)
        sub = FUSE_M // 2
        for part in range(2):
            rows = slice(part * sub, (part + 1) * sub)
            hb = h_ref[rows, :]
            g = jnp.dot(hb, wg_ref[...], preferred_element_type=F32)
            u = jnp.dot(hb, wu_ref[...], preferred_element_type=F32)
            a_ref[rows, :] = (_silu(g) * u).astype(a_ref.dtype)
            for jj in range(2):
                @pl.when(j == jj)
                def _(g=g, u=u, jj=jj, rows=rows):
                    gu_ref[rows, jj * half:(jj + 1) * half] = g.astype(gu_ref.dtype)
                    gu_ref[rows, FFN + jj * half:FFN + (jj + 1) * half] = u.astype(gu_ref.dtype)

    return pl.pallas_call(
        body, name=name, grid=(S // FUSE_M, 2),
        in_specs=[pl.BlockSpec((FUSE_M, D), lambda i, j: (i, 0)),
                  pl.BlockSpec((None, D, half), lambda i, j: (j, 0, 0)),
                  pl.BlockSpec((None, D, half), lambda i, j: (j + 2, 0, 0))],
        out_specs=[pl.BlockSpec((FUSE_M, 2 * FFN), lambda i, j: (i, 0)), pl.BlockSpec((FUSE_M, half), lambda i, j: (i, j))],
        out_shape=[SDS((S, 2 * FFN), BF16), SDS((S, FFN), BF16)],
        compiler_params=_cp(2),
    )(h, w_in, w_in)


def ffn_forward(h, w_in, w_out, tag):
    gu, a = ffn_in_act(h, w_in, f"ffn_in_act_{tag}")
    f = matmul(a, w_out, "nn", BF16, f"ffn_out_{tag}")
    return f, dict(h=h, gu=gu, a=a)


def ffn_out_dx_act(df, w_out, gu, name):
    S = df.shape[0]
    half = FFN // 2

    def body(df_ref, w_ref, g_ref, u_ref, dgu_ref):
        j = pl.program_id(1)
        sub = FUSE_M // 2
        for part in range(2):
            rows = slice(part * sub, (part + 1) * sub)
            da = _nt(df_ref[rows, :], w_ref[...])
            dg, du = _swiglu_bwd((g_ref[rows, :].astype(F32), u_ref[rows, :].astype(F32)), da)
            for jj in range(2):
                @pl.when(j == jj)
                def _(dg=dg, du=du, jj=jj, rows=rows):
                    dgu_ref[rows, jj * half:(jj + 1) * half] = dg.astype(dgu_ref.dtype)
                    dgu_ref[rows, FFN + jj * half:FFN + (jj + 1) * half] = du.astype(dgu_ref.dtype)

    return pl.pallas_call(
        body, name=name, grid=(S // FUSE_M, 2),
        in_specs=[pl.BlockSpec((FUSE_M, D), lambda i, j: (i, 0)),
                  pl.BlockSpec((half, D), lambda i, j: (j, 0)),
                  pl.BlockSpec((FUSE_M, half), lambda i, j: (i, j)),
                  pl.BlockSpec((FUSE_M, half), lambda i, j: (i, j + 2))],
        out_specs=pl.BlockSpec((FUSE_M, 2 * FFN), lambda i, j: (i, 0)),
        out_shape=SDS((S, 2 * FFN), BF16),
        compiler_params=_cp(2),
    )(df, w_out, gu, gu)


def ffn_backward(df, sv, w_in, w_out, tag):
    d_w_out = matmul(sv["a"], df, "tn", F32, f"ffn_out_dw_{tag}")
    dgu = ffn_out_dx_act(df, w_out, sv["gu"], f"ffn_out_dx_act_{tag}")
    d_w_in = matmul(sv["h"], dgu, "tn", F32, f"ffn_in_dw_{tag}", col_shards=N_SHARD)
    dh = matmul(dgu, w_in, "nt", BF16, f"ffn_in_dx_{tag}", col_shards=N_SHARD)
    return dh, d_w_in, d_w_out


def f_norm_only(ids, x, gain, sc, sh):
    return (_normmod(x, gain, sc, sh),)


WT = 512


def _wide(a, **kw):
    return Row(a, (WT, D), lambda i: (i, 0), **kw)


def _wide_out(S, dtype):
    return Out((S, D), dtype, (WT, D), lambda i: (i, 0))


def adamw(w, g, m, v, name):
    shape = w.shape
    C = shape[-1]
    R = int(np.prod(shape[:-1]))
    w2, g2, m2, v2 = (a.reshape(R, C) for a in (w, g, m, v))
    br = R
    if R > 256:
        br = max(b for b in range(8, 257, 8) if R % b == 0)
    c1 = 1.0 / (1.0 - ADAM_B1 ** ADAM_STEP)
    c2 = 1.0 / (1.0 - ADAM_B2 ** ADAM_STEP)

    def body(w_ref, g_ref, m_ref, v_ref, d_ref, nm_ref, nv_ref):
        gg = g_ref[...]
        mm_ = ADAM_B1 * m_ref[...] + (1.0 - ADAM_B1) * gg
        vv = ADAM_B2 * v_ref[...] + (1.0 - ADAM_B2) * (gg * gg)
        d_ref[...] = -ADAM_LR * ((mm_ * c1) / (jnp.sqrt(vv * c2) + ADAM_EPS) + ADAM_WD * w_ref[...])
        nm_ref[...] = mm_
        nv_ref[...] = vv

    spec = pl.BlockSpec((br, C), lambda i: (i, 0))
    d, nm, nv = pl.pallas_call(
        body, name=name, grid=(R // br,), in_specs=[spec] * 4, out_specs=[spec] * 3,
        out_shape=[SDS((R, C), F32)] * 3, compiler_params=_cp(1),
    )(w2, g2, m2, v2)
    return d.reshape(shape), nm.reshape(shape), nv.reshape(shape)


def _place():
    x, y, c = lax.axis_index("x"), lax.axis_index("y"), lax.axis_index("c")
    chips = [(1 - x, y), (x, 1 - y), (1 - x, 1 - y)]
    return x, y, c, chips


def all_gather_small(blk, name):
    m_per, n = blk.shape

    def body(x_ref, out_ref, send_sems, recv_sems, local_sem):
        x, y, c, chips = _place()
        me, sibling = (x, y, c), (x, y, 1 - c)

        def rows(px, py, pc):
            return out_ref.at[pl.ds((4 * px + 2 * py + pc) * m_per, m_per), :]

        def copy(k, block, to, src=None):
            return pltpu.make_async_remote_copy(
                src_ref=rows(*block) if src is None else src, dst_ref=rows(*block),
                send_sem=send_sems.at[k], recv_sem=recv_sems.at[k], device_id=to, device_id_type=MESH)

        mine = pltpu.make_async_copy(x_ref, rows(*me), local_sem)
        mine.start()
        first = [copy(0, me, sibling, src=x_ref)]
        first += [copy(1 + j, me, (*chip, c), src=x_ref) for j, chip in enumerate(chips)]
        for cp in first:
            cp.start()
        passed = [copy(4 + j, (*chip, c), sibling) for j, chip in enumerate(chips)]
        for j, chip in enumerate(chips):
            copy(1 + j, (*chip, c), me).wait_recv()
            passed[j].start()
        copy(0, sibling, me).wait_recv()
        for j, chip in enumerate(chips):
            copy(4 + j, (*chip, 1 - c), me).wait_recv()
        for cp in first + passed:
            cp.wait_send()
        mine.wait()

    return pl.pallas_call(
        body, name=name, out_shape=SDS((N_DEV * m_per, n), blk.dtype),
        in_specs=[pl.BlockSpec(memory_space=pltpu.VMEM)], out_specs=pl.BlockSpec(memory_space=pltpu.VMEM),
        scratch_shapes=[pltpu.SemaphoreType.DMA((7,)), pltpu.SemaphoreType.DMA((7,)), pltpu.SemaphoreType.DMA],
    )(blk)


def _half(cc, rh):
    return pl.ds(pl.multiple_of(cc * rh, 16), rh)


def all_gather_shards(ws):
    n = len(ws)

    def body(*refs):
        w_refs, out_refs = refs[:n], refs[n:2 * n]
        send_sems, recv_sems, local_sems, own_sems = refs[2 * n:]
        x, y, c, chips = _place()
        sibling = (x, y, 1 - c)
        s_me = 2 * x + y

        def copy(k, src, dst, to):
            return pltpu.make_async_remote_copy(src_ref=src, dst_ref=dst, send_sem=send_sems.at[k], recv_sem=recv_sems.at[k],
                                                device_id=to, device_id_type=MESH)

        local, sends, passed = [], [], []
        for k in range(n):
            rh = ws[k].shape[0] // 2
            cp = pltpu.make_async_remote_copy(src_ref=w_refs[k], dst_ref=out_refs[k].at[s_me], send_sem=local_sems.at[k],
                                              recv_sem=own_sems.at[k], device_id=sibling, device_id_type=MESH)
            cp.start()
            local.append(cp)
            for j, chip in enumerate(chips):
                sd = copy(6 * k + j, w_refs[k].at[_half(c, rh)], out_refs[k].at[s_me, _half(c, rh)], (*chip, c))
                sd.start()
                sends.append(sd)
        for k in range(n):
            rh = ws[k].shape[0] // 2
            for j, (px, py) in enumerate(chips):
                got = out_refs[k].at[2 * px + py, _half(c, rh)]
                copy(6 * k + j, got, got, (px, py, c)).wait_recv()
                fw = copy(6 * k + 3 + j, got, got, sibling)
                fw.start()
                passed.append(fw)
        for k in range(n):
            rh = ws[k].shape[0] // 2
            for j, (px, py) in enumerate(chips):
                got = out_refs[k].at[2 * px + py, _half(1 - c, rh)]
                copy(6 * k + 3 + j, got, got, sibling).wait_recv()
        for cp in sends + passed:
            cp.wait_send()
        for cp in local:
            cp.wait()

    return pl.pallas_call(
        body, name="weights_all_gather", out_shape=[SDS((N_SHARD,) + w.shape, w.dtype) for w in ws],
        in_specs=[ANY] * n, out_specs=[ANY] * n,
        scratch_shapes=[pltpu.SemaphoreType.DMA((6 * n,)), pltpu.SemaphoreType.DMA((6 * n,)), pltpu.SemaphoreType.DMA((n,)),
                        pltpu.SemaphoreType.DMA((n,))],
    )(*ws)


def sibling_exchange(sends, name):
    n = len(sends)

    def body(*refs):
        s_refs, o_refs, send_sems, recv_sems = refs[:n], refs[n:2 * n], refs[2 * n], refs[2 * n + 1]
        x, y, c, _ = _place()
        cps = [pltpu.make_async_remote_copy(src_ref=s_refs[k], dst_ref=o_refs[k], send_sem=send_sems.at[k], recv_sem=recv_sems.at[k],
                                            device_id=(x, y, 1 - c), device_id_type=MESH) for k in range(n)]
        for cp in cps:
            cp.start()
        for cp in cps:
            cp.wait()

    return pl.pallas_call(
        body, name=name, out_shape=[SDS(s.shape, s.dtype) for s in sends], in_specs=[ANY] * n, out_specs=[ANY] * n,
        scratch_shapes=[pltpu.SemaphoreType.DMA((n,)), pltpu.SemaphoreType.DMA((n,))],
    )(*sends)


def add_rows(arrs, out_dtype, name, rt=256):
    Rr, W = arrs[0].shape

    def fn(ids, *vals):
        acc = vals[0]
        for v in vals[1:]:
            acc = acc + v
        return (acc,)

    t = rt if Rr % rt == 0 else max(b for b in range(16, rt + 1, 16) if Rr % b == 0)
    (out,) = rowwise(fn, [Row(a, (t, W), lambda i: (i, 0)) for a in arrs], [],
                     [Out((Rr, W), out_dtype, (t, W), lambda i: (i, 0))], (Rr // t,), name)
    return out


HBM_SPEC = pl.BlockSpec(memory_space=pltpu.HBM)
SEM_SPEC = pl.BlockSpec(memory_space=pltpu.SEMAPHORE)
DATAFLOW = pltpu.SideEffectType.DATAFLOW_SIDE_EFFECTING


def _in_hbm(a):
    return pltpu.with_memory_space_constraint(a, pltpu.HBM)


def _gather_copies(w_refs, land_refs, send_sems, recv_sems):
    x, y, c, chips = _place()
    targets = [(x, y, 1 - c)] + [(*chip, c) for chip in chips]
    cps = []
    for k, (w_ref, land_ref) in enumerate(zip(w_refs, land_refs)):
        for j, to in enumerate(targets):
            cps.append(pltpu.make_async_remote_copy(src_ref=w_ref, dst_ref=land_ref.at[2 * x + y], send_sem=send_sems.at[4 * k + j],
                                                    recv_sem=recv_sems.at[4 * k + j], device_id=to, device_id_type=MESH))
    return cps


def _scatter_copies(p_refs, land_refs, send_sems, recv_sems):
    x, y, c, chips = _place()
    cps = []
    for k, (p_ref, land_ref) in enumerate(zip(p_refs, land_refs)):
        for j, (px, py) in enumerate(chips):
            cps.append(pltpu.make_async_remote_copy(src_ref=p_ref.at[2 * px + py], dst_ref=land_ref.at[j], send_sem=send_sems.at[3 * k + j],
                                                    recv_sem=recv_sems.at[3 * k + j], device_id=(px, py, c), device_id_type=MESH))
    return cps


def copies_start(srcs, land_shapes, make_copies, per_src, name):
    n = len(srcs)
    m = per_src * n

    def body(*refs):
        src_refs, land_refs = refs[:n], refs[n:2 * n]
        send_sems, recv_sems, token = refs[2 * n], refs[2 * n + 1], refs[-1]
        for cp in make_copies(src_refs, land_refs, send_sems, recv_sems):
            cp.start()
        token[...] = jnp.zeros_like(token)

    lands = [lax.empty(shp, s.dtype) for shp, s in zip(land_shapes, srcs)]
    res = pl.pallas_call(
        body, name=name,
        out_shape=(pltpu.SemaphoreType.DMA((m,)), pltpu.SemaphoreType.DMA((m,)), *[pltpu.HBM(s.shape, s.dtype) for s in srcs],
                   *[pltpu.HBM(shp, s.dtype) for shp, s in zip(land_shapes, srcs)], SDS((8, LANES), F32)),
        in_specs=[HBM_SPEC] * (2 * n),
        out_specs=(SEM_SPEC, SEM_SPEC, *[HBM_SPEC] * (2 * n), pl.BlockSpec(memory_space=pltpu.VMEM)),
        input_output_aliases={i: 2 + i for i in range(2 * n)},
        compiler_params=pltpu.CompilerParams(has_side_effects=DATAFLOW),
    )(*[_in_hbm(s) for s in srcs], *[_in_hbm(l) for l in lands])
    return res[0], res[1], list(res[2:2 + n]), list(res[2 + n:2 + 2 * n]), res[-1]


def copies_wait(send_sems, recv_sems, srcs, lands, make_copies, after, name):
    n = len(srcs)

    def body(*refs):
        src_refs, land_refs = refs[:n], refs[n:2 * n]
        for cp in make_copies(src_refs, land_refs, refs[2 * n], refs[2 * n + 1]):
            cp.wait_send()
            cp.wait_recv()

    res = pl.pallas_call(
        body, name=name,
        out_shape=(*[pltpu.HBM(s.shape, s.dtype) for s in srcs], *[pltpu.HBM(l.shape, l.dtype) for l in lands]),
        in_specs=[HBM_SPEC] * (2 * n) + [SEM_SPEC, SEM_SPEC, ANY],
        out_specs=tuple([HBM_SPEC] * (2 * n)),
        input_output_aliases={i: i for i in range(2 * n)},
        compiler_params=pltpu.CompilerParams(has_side_effects=DATAFLOW),
    )(*srcs, *lands, send_sems, recv_sems, after)
    return list(res[n:])


def _pad_lanes(v):
    return jnp.concatenate([v.astype(F32), jnp.zeros((LANES - v.shape[0],), F32)])[None]


def kernel(x, c, w_ada, b_ada, norm_mix, norm_ffn, w_ffn_in, w_ffn_out, gdn_w_in, gdn_conv, gdn_a_log, gdn_dt_bias, gdn_out_norm, gdn_w_out, dsw_w_in, dsw_q_norm, dsw_k_norm, dsw_w_out, rel_bias, loss_target, m_w_ada, m_b_ada, m_norm_mix, m_norm_ffn, m_w_ffn_in, m_w_ffn_out, m_gdn_w_in, m_gdn_conv, m_gdn_a_log, m_gdn_dt_bias, m_gdn_out_norm, m_gdn_w_out, m_dsw_w_in, m_dsw_q_norm, m_dsw_k_norm, m_dsw_w_out, m_rel_bias, v_w_ada, v_b_ada, v_norm_mix, v_norm_ffn, v_w_ffn_in, v_w_ffn_out, v_gdn_w_in, v_gdn_conv, v_gdn_a_log, v_gdn_dt_bias, v_gdn_out_norm, v_gdn_w_out, v_dsw_w_in, v_dsw_q_norm, v_dsw_k_norm, v_dsw_w_out, v_rel_bias):
    S = x.shape[1]
    nt = S // WT
    xi, yi, ci = lax.axis_index("x"), lax.axis_index("y"), lax.axis_index("c")
    me = 4 * xi + 2 * yi + ci
    s_me = 2 * xi + yi
    x0, tgt = x[0], loss_target[0]

    whole = lambda a: Row(a, a.shape, lambda i: (0,) * a.ndim)
    (cond8,) = rowwise(lambda ids, v: (_silu(v),), [whole(c.reshape(8, LANES))], [], [Out((8, LANES), F32, (8, LANES), lambda i: (0, 0))], (1,), "cond")
    cond_all = all_gather_small(cond8, "gather_cond").reshape(N_DEV, D)
    cond16 = jnp.concatenate([cond_all, jnp.zeros((8, D), F32)], axis=0)
    ada_cols = w_ada.shape[2]
    mods = [matmul(cond16, w_ada[l], "nn", F32, f"ada_{l}")[:N_DEV] for l in range(2)]
    buf = jnp.concatenate([jnp.stack(mods, axis=1).reshape(-1, LANES), gdn_conv.reshape(-1, LANES)], axis=0)
    n_mod_rows = N_DEV * 2 * ada_cols // LANES
    got = all_gather_small(buf, "gather_mod").reshape(N_DEV, buf.shape[0], LANES)
    mod_parts, conv_parts = [], []
    for s in range(N_SHARD):
        from_dev = got[2 * s]
        mod_parts.append(lax.dynamic_index_in_dim(from_dev[:n_mod_rows].reshape(N_DEV, 2, ada_cols), me, 0, keepdims=False))
        conv_parts.append(from_dev[n_mod_rows:].reshape(4, -1))
    mod_nb = jnp.concatenate(mod_parts, axis=1)
    conv_w = jnp.concatenate(conv_parts, axis=1)
    (mod,) = rowwise(lambda ids, a, b: (a + b,), [whole(mod_nb), whole(b_ada)], [], [Out(mod_nb.shape, F32, mod_nb.shape, lambda i: (0, 0))], (1,), "mod_bias")
    mod = mod.reshape(2, 6, 1, D)
    sh1, sc1, g1, sh2, sc2, g2 = ([mod[l, k] for l in range(2)] for k in range(6))
    gmix = [norm_mix[l][None] for l in range(2)]
    gffn = [norm_ffn[l][None] for l in range(2)]

    gcols = gdn_w_in.shape[2]
    g_gdn_in, g_gdn_out = all_gather_shards([gdn_w_in[0].astype(BF16), gdn_w_out[0].astype(BF16)])
    gathered = lambda ws: [(N_SHARD,) + w.shape for w in ws]
    gate = (jnp.minimum(jnp.abs(g_gdn_in[0, 0, 0].astype(F32)), 0.0) + jnp.minimum(jnp.abs(mod[0, 0, 0, 0]), 0.0)).astype(BF16)
    w2 = [w_ffn_in[0].astype(BF16) + gate, w_ffn_out[0].astype(BF16) + gate]
    w3 = [dsw_w_in[0].astype(BF16) + gate, dsw_w_out[0].astype(BF16) + gate, w_ffn_in[1].astype(BF16) + gate, w_ffn_out[1].astype(BF16) + gate]
    fly2 = copies_start(w2, gathered(w2), _gather_copies, 4, "weights_ffn0_start")
    fly3 = copies_start(w3, gathered(w3), _gather_copies, 4, "weights_layer1_start")
    started = fly2[4][0, 0] + fly3[4][0, 0]
    w_gdn = jnp.concatenate([g_gdn_in[s] for s in range(N_SHARD)] + [jnp.zeros((D, GDN_PROJ - N_SHARD * gcols), BF16)], axis=1)
    alog, dtb = _pad_lanes(gdn_a_log[0]), _pad_lanes(gdn_dt_bias[0])
    qg2 = jnp.concatenate([dsw_q_norm, dsw_q_norm], axis=1)
    kg2 = jnp.concatenate([dsw_k_norm, dsw_k_norm], axis=1)
    w_gdn_out = g_gdn_out.reshape(GDN_H * LANES, D)
    gdn_args = (w_gdn, conv_w, alog, dtb, gdn_out_norm, w_gdn_out)
    sc1[0] = sc1[0] + started

    (h10,) = rowwise(f_norm_only, [_wide(x0)], [gmix[0], sc1[0], sh1[0]], [_wide_out(S, BF16)], (nt,), "l0_norm")
    y0, sv_g = gdn_forward(h10, *gdn_args)
    x1, h20 = rowwise(f_resid_norm, [_wide(x0), _wide(y0)], [g1[0], gffn[0], sc2[0], sh2[0]], [_wide_out(S, F32), _wide_out(S, BF16)], (nt,), "l0_mid")
    g_in0, g_out0 = copies_wait(*fly2[:4], _gather_copies, y0, "weights_ffn0_wait")
    w_ffn = [(g_in0, g_out0.reshape(FFN, D)), None]
    f0, sv_f0 = ffn_forward(h20, *w_ffn[0], "0")
    x2, h11 = rowwise(f_resid_norm, [_wide(x1), _wide(f0)], [g2[0], gmix[1], sc1[1], sh1[1]], [_wide_out(S, F32), _wide_out(S, BF16)], (nt,), "l1_in")
    g_dsw_in, g_dsw_out, g_in1, g_out1 = copies_wait(*fly3[:4], _gather_copies, f0, "weights_layer1_wait")
    w_ffn[1] = (g_in1, g_out1.reshape(FFN, D))
    dsw_args = (g_dsw_in, qg2, kg2)
    y1, sv_d = dsw_forward(h11, *dsw_args, rel_bias, g_dsw_out)
    x3, h21 = rowwise(f_resid_norm, [_wide(x2), _wide(y1)], [g1[1], gffn[1], sc2[1], sh2[1]], [_wide_out(S, F32), _wide_out(S, BF16)], (nt,), "l1_mid")
    f1, sv_f1 = ffn_forward(h21, *w_ffn[1], "1")
    parts, dx3, df1, dg2_1 = loss_and_grad(x3, f1, tgt, g2[1], S)
    loss = lax.psum(jnp.sum(parts), ("x", "y", "c"))

    dh21, d_win1, d_wout1 = ffn_backward(df1, sv_f1, *w_ffn[1], "1")
    (dx2, dy1), (dg1_1, dgf1, dsc2_1, dsh2_1) = rowwise_bwd(
        f_resid_norm, [_wide(x2), _wide(y1, gdtype=BF16)], [g1[1], gffn[1], sc2[1], sh2[1]], [_wide(dx3), _wide(dh21)], (nt,), "l1_mid_bwd")
    dh11, g_d = dsw_backward(dy1, sv_d, *dsw_args, g_dsw_out)
    (dx1, df0), (dg2_0, dgm1, dsc1_1, dsh1_1) = rowwise_bwd(
        f_resid_norm, [_wide(x1), _wide(f0, gdtype=BF16)], [g2[0], gmix[1], sc1[1], sh1[1]], [_wide(dx2), _wide(dh11)], (nt,), "l1_in_bwd")
    by_shard = lambda a: a.reshape(N_SHARD, a.shape[0] // N_SHARD, a.shape[1])
    landing = lambda ps: [(3,) + p.shape[1:] for p in ps]
    dws3 = [g_d["w_in"], g_d["w_out"], d_win1, by_shard(d_wout1)]
    parts3 = [a.astype(BF16) for a in dws3]
    gfly3 = copies_start(parts3, landing(parts3), _scatter_copies, 3, "grads_layer1_start")
    w_out0 = w_ffn[0][1] + gfly3[4][0, 0].astype(BF16)
    dh20, d_win0, d_wout0 = ffn_backward(df0, sv_f0, w_ffn[0][0], w_out0, "0")
    (dx0p, dy0), (dg1_0, dgf0, dsc2_0, dsh2_0) = rowwise_bwd(
        f_resid_norm, [_wide(x0), _wide(y0, gdtype=BF16)], [g1[0], gffn[0], sc2[0], sh2[0]], [_wide(dx1), _wide(dh20)], (nt,), "l0_mid_bwd")
    dws2 = [d_win0, by_shard(d_wout0)]
    parts2 = [a.astype(BF16) for a in dws2]
    gfly2 = copies_start(parts2, landing(parts2), _scatter_copies, 3, "grads_ffn0_start")
    gdn_args = gdn_args[:5] + (w_gdn_out + gfly2[4][0, 0].astype(BF16),)
    gdn_flight = []

    def start_gdn_grads(d_w_in, d_w_out):
        dws1 = [jnp.stack([d_w_in[:, s * gcols:(s + 1) * gcols] for s in range(N_SHARD)]), by_shard(d_w_out)]
        parts1 = [a.astype(BF16) for a in dws1]
        fly = copies_start(parts1, landing(parts1), _scatter_copies, 3, "grads_gdn_start")
        gdn_flight.extend([dws1, fly])
        return fly[4][0, 0]

    dh10, g_g = gdn_backward(dy0, sv_g, *gdn_args, on_weight_grads=start_gdn_grads)
    dws1, gfly1 = gdn_flight
    (grad_x,), (dgm0, dsc1_0, dsh1_0) = rowwise_bwd(f_first, [_wide(x0)], [gmix[0], sc1[0], sh1[0]], [_wide(dx0p), _wide(dh10)], (nt,), "l0_norm_bwd")

    dmod = jnp.concatenate([dsh1_0, dsc1_0, dg1_0, dsh2_0, dsc2_0, dg2_0, dsh1_1, dsc1_1, dg1_1, dsh2_1, dsc2_1, dg2_1], axis=1)
    d_rel = jnp.transpose(g_d["rel"][:, :, 0])
    fold = lambda v: v[:, :DSW_DH] + v[:, DSW_DH:]
    small = [dmod, jnp.concatenate([dgm0, dgm1], axis=1), jnp.concatenate([dgf0, dgf1], axis=1), g_g["conv"].reshape(1, -1),
             g_g["alog"], g_g["dtb"], g_g["gain"], _pad_lanes(fold(g_d["q_gain2"])[0]), _pad_lanes(fold(g_d["k_gain2"])[0]),
             d_rel.reshape(1, -1)]
    used = [v.shape[1] // LANES for v in small]
    sizes = [-(-u // 8) * 8 for u in used]
    pad8 = lambda v, u, s: jnp.concatenate([v.reshape(u, LANES), jnp.zeros((s - u, LANES), F32)], axis=0) if s > u else v.reshape(u, LANES)
    pad_rows = sum(sizes)
    sbuf = jnp.concatenate([pad8(v, u, s) for v, u, s in zip(small, used, sizes)], axis=0)
    sgot = all_gather_small(sbuf, "gather_small_grads")
    ssum = add_rows([sgot[d * pad_rows:(d + 1) * pad_rows] for d in range(N_DEV)], F32, "sum_small_grads", rt=pad_rows)
    offs = np.cumsum([0] + sizes)
    take = lambda k: ssum[offs[k]:offs[k] + used[k]].reshape(1, -1)
    grad_b_ada = take(0).reshape(2, 6 * D)
    grad_norm_mix = take(1).reshape(2, D)
    grad_norm_ffn = take(2).reshape(2, D)
    conv_full = take(3).reshape(4, -1)
    ncv = gdn_conv.shape[2]
    grad_gdn_conv = lax.dynamic_slice_in_dim(conv_full, s_me * ncv, ncv, axis=1)[None]
    grad_a_log = take(4)[:, :GDN_H]
    grad_dt_bias = take(5)[:, :GDN_H]
    grad_out_norm = take(6)
    grad_q_norm = take(7)[:, :DSW_DH]
    grad_k_norm = take(8)[:, :DSW_DH]
    grad_rel = take(9).reshape(REL_BUCKETS, 3 * GDN_H)
    dmod_all = sgot.reshape(N_DEV, pad_rows, LANES)[:, :used[0]].reshape(N_DEV, 2, 6 * D)
    dmod_mine = lax.dynamic_slice_in_dim(dmod_all, s_me * ada_cols, ada_cols, axis=2)
    dmod16 = jnp.concatenate([dmod_mine, jnp.zeros_like(dmod_mine)], axis=0)
    grad_w_ada = jnp.stack([matmul(cond16, dmod16[:, l], "tn", F32, f"ada_dw_{l}") for l in range(2)])

    got3 = copies_wait(*gfly3[:4], _scatter_copies, grad_x, "grads_layer1_wait")
    got2 = copies_wait(*gfly2[:4], _scatter_copies, grad_x, "grads_ffn0_wait")
    got1 = copies_wait(*gfly1[:4], _scatter_copies, grad_x, "grads_gdn_wait")
    core_sums = []
    for i, (full, got) in enumerate(zip(dws3 + dws2 + dws1, got3 + got2 + got1)):
        own = lax.dynamic_index_in_dim(full, s_me, 0, keepdims=False)
        core_sums.append(add_rows([own, got[0], got[1], got[2]], F32, f"grads_core_sum_{i}"))
    sib_sums = sibling_exchange(core_sums, "grads_core_sums_swap")
    s_dsw_in, s_dsw_out, s_in1, s_out1, s_in0, s_out0, s_gdn_in, s_gdn_out = [
        add_rows([a, b], F32, f"grads_chip_total_{i}") for i, (a, b) in enumerate(zip(core_sums, sib_sums))]
    gsh = dict(gdn_w_in=s_gdn_in[None], gdn_w_out=s_gdn_out[None],
               w_ffn_in=jnp.stack([s_in0, s_in1]), w_ffn_out=jnp.stack([s_out0, s_out1]),
               dsw_w_in=s_dsw_in[None], dsw_w_out=s_dsw_out[None])

    grads = dict(w_ada=grad_w_ada, b_ada=grad_b_ada, norm_mix=grad_norm_mix, norm_ffn=grad_norm_ffn, w_ffn_in=gsh["w_ffn_in"],
                 w_ffn_out=gsh["w_ffn_out"], gdn_w_in=gsh["gdn_w_in"], gdn_conv=grad_gdn_conv, gdn_a_log=grad_a_log,
                 gdn_dt_bias=grad_dt_bias, gdn_out_norm=grad_out_norm, gdn_w_out=gsh["gdn_w_out"], dsw_w_in=gsh["dsw_w_in"],
                 dsw_q_norm=grad_q_norm, dsw_k_norm=grad_k_norm, dsw_w_out=gsh["dsw_w_out"], rel_bias=grad_rel)
    weights = dict(w_ada=w_ada, b_ada=b_ada, norm_mix=norm_mix, norm_ffn=norm_ffn, w_ffn_in=w_ffn_in, w_ffn_out=w_ffn_out,
                   gdn_w_in=gdn_w_in, gdn_conv=gdn_conv, gdn_a_log=gdn_a_log, gdn_dt_bias=gdn_dt_bias, gdn_out_norm=gdn_out_norm,
                   gdn_w_out=gdn_w_out, dsw_w_in=dsw_w_in, dsw_q_norm=dsw_q_norm, dsw_k_norm=dsw_k_norm, dsw_w_out=dsw_w_out,
                   rel_bias=rel_bias)
    ms = dict(w_ada=m_w_ada, b_ada=m_b_ada, norm_mix=m_norm_mix, norm_ffn=m_norm_ffn, w_ffn_in=m_w_ffn_in, w_ffn_out=m_w_ffn_out,
              gdn_w_in=m_gdn_w_in, gdn_conv=m_gdn_conv, gdn_a_log=m_gdn_a_log, gdn_dt_bias=m_gdn_dt_bias, gdn_out_norm=m_gdn_out_norm,
              gdn_w_out=m_gdn_w_out, dsw_w_in=m_dsw_w_in, dsw_q_norm=m_dsw_q_norm, dsw_k_norm=m_dsw_k_norm, dsw_w_out=m_dsw_w_out,
              rel_bias=m_rel_bias)
    vs = dict(w_ada=v_w_ada, b_ada=v_b_ada, norm_mix=v_norm_mix, norm_ffn=v_norm_ffn, w_ffn_in=v_w_ffn_in, w_ffn_out=v_w_ffn_out,
              gdn_w_in=v_gdn_w_in, gdn_conv=v_gdn_conv, gdn_a_log=v_gdn_a_log, gdn_dt_bias=v_gdn_dt_bias, gdn_out_norm=v_gdn_out_norm,
              gdn_w_out=v_gdn_w_out, dsw_w_in=v_dsw_w_in, dsw_q_norm=v_dsw_q_norm, dsw_k_norm=v_dsw_k_norm, dsw_w_out=v_dsw_w_out,
              rel_bias=v_rel_bias)
    names = list(weights)
    deltas, new_m, new_v = [], [], []
    for n in names:
        g = grads[n].reshape(weights[n].shape)
        grads[n] = g
        d, nm, nv = adamw(weights[n], g, ms[n], vs[n], f"adamw_{n}")
        deltas.append(d)
        new_m.append(nm)
        new_v.append(nv)
    return (loss, grad_x[None], *[grads[n] for n in names], *deltas, *new_m, *new_v)
```

```python
import functools
import math

import numpy as np
import jax
import jax.numpy as jnp
from jax import lax
from jax.experimental import pallas as pl
from jax.experimental.pallas import tpu as pltpu

F32 = jnp.float32
BF16 = jnp.bfloat16
SDS = jax.ShapeDtypeStruct
MESH = pl.DeviceIdType.MESH
ANY = pl.BlockSpec(memory_space=pl.ANY)

D = 1024
EPS = 1e-6
LANES = 128
GDN_H = 8
GDN_DK = 128
GDN_C = 64
DSW_GROUPS = ((128, 1), (512, 4), (2048, 16))
DSW_SPAN = 128
DSW_DH = 64
DSW_HG = 512
REL_BUCKETS = 32
REL_MAX_DIST = 2048
FFN = 2816
N_SHARD = 4
N_DEV = 8
VMEM_LIMIT = 48 * 1024 * 1024
NEG = -1e30

ADAM_LR, ADAM_B1, ADAM_B2, ADAM_EPS, ADAM_WD, ADAM_STEP = 0.001, 0.9, 0.999, 1e-08, 0.01, 10


def _cp(n_axes):
    return pltpu.CompilerParams(dimension_semantics=("arbitrary",) * n_axes, vmem_limit_bytes=VMEM_LIMIT)


def _blk(dim, cap):
    if dim <= cap:
        return dim
    best = None
    for b in range(LANES, cap + 1, LANES):
        if dim % b == 0:
            best = b
    assert best is not None, (dim, cap)
    return best


MAX_SHARD_BLOCK = 1408
def matmul(a, b, mode, out_dtype, name, cap_m=MAX_SHARD_BLOCK, cap_n=MAX_SHARD_BLOCK, cap_k=2048, col_shards=0):
    ns = col_shards
    if mode == "nn":
        (M, K) = a.shape
        K2, N = (b.shape[1], ns * b.shape[2]) if ns else b.shape
    elif mode == "nt":
        (M, K) = a.shape
        N, K2 = (b.shape[1], ns * b.shape[2]) if ns else b.shape
    else:
        (K, M), (K2, N) = a.shape, b.shape
    assert K == K2, (a.shape, b.shape, mode)
    if K <= 3072:
        cap_k = K
        if K > 2048:
            cap_n = 1024
    n_unit = N // ns if (ns and mode != "nt") else N
    k_unit = K // ns if (ns and mode == "nt") else K
    bm = _blk(M, cap_m)
    bn = _blk(n_unit, MAX_SHARD_BLOCK) if n_unit != N else _blk(N, cap_n)
    if k_unit != K:
        bk = _blk(k_unit, MAX_SHARD_BLOCK)
    else:
        bk = _blk(K, 1024 if (ns and mode == "tn") else cap_k)
    nk = K // bk
    nps, kps = n_unit // bn, k_unit // bk
    dims = {"nn": ((1,), (0,)), "nt": ((1,), (1,)), "tn": ((0,), (0,))}[mode]

    def dot(a_ref, b_ref):
        return lax.dot_general(a_ref[...].astype(BF16), b_ref[...].astype(BF16), (dims, ((), ())), preferred_element_type=F32)

    def body_one(a_ref, b_ref, o_ref):
        o_ref[...] = dot(a_ref, b_ref).astype(o_ref.dtype)

    def body_acc(a_ref, b_ref, o_ref, acc_ref):
        k = pl.program_id(2)

        @pl.when(k == 0)
        def _():
            acc_ref[...] = jnp.zeros_like(acc_ref)

        acc_ref[...] += dot(a_ref, b_ref)

        @pl.when(k == nk - 1)
        def _():
            o_ref[...] = acc_ref[...].astype(o_ref.dtype)

    a_spec = pl.BlockSpec((bk, bm), lambda i, j, k: (k, i)) if mode == "tn" else pl.BlockSpec((bm, bk), lambda i, j, k: (i, k))
    if mode == "nt":
        b_spec = pl.BlockSpec((None, bn, bk), lambda i, j, k: (k // kps, j, k % kps)) if ns else pl.BlockSpec((bn, bk), lambda i, j, k: (j, k))
    elif mode == "nn" and ns:
        b_spec = pl.BlockSpec((None, bk, bn), lambda i, j, k: (j // nps, k, j % nps))
    else:
        b_spec = pl.BlockSpec((bk, bn), lambda i, j, k: (k, j))
    if mode == "tn" and ns:
        o_spec, o_shape = pl.BlockSpec((None, bm, bn), lambda i, j, k: (j // nps, i, j % nps)), (ns, M, n_unit)
    else:
        o_spec, o_shape = pl.BlockSpec((bm, bn), lambda i, j, k: (i, j)), (M, N)
    return pl.pallas_call(
        body_one if nk == 1 else body_acc, name=name, grid=(M // bm, N // bn, nk),
        in_specs=[a_spec, b_spec], out_specs=o_spec,
        out_shape=SDS(o_shape, out_dtype), scratch_shapes=[] if nk == 1 else [pltpu.VMEM((bm, bn), F32)],
        compiler_params=_cp(3),
    )(a, b)


class Row:
    def __init__(self, arr, bshape, imap, splits=None, diff=True, acc=False, gdtype=F32, gshape=None, gbshape=None, gimap=None,
                 lead=0):
        self.arr, self.bshape, self.imap = arr, tuple(bshape), imap
        self.splits, self.lead = splits, lead
        self.diff, self.acc, self.gdtype = diff, acc, gdtype
        self.gshape = tuple(arr.shape) if gshape is None else tuple(gshape)
        self.gbshape = self.bshape if gbshape is None else tuple(gbshape)
        self.gimap = imap if gimap is None else gimap

    def gspec(self):
        return pl.BlockSpec(self.gbshape, self.gimap)

    def spec(self):
        return pl.BlockSpec(self.bshape, self.imap)

    def pieces(self, ref):
        return _load_pieces(ref, self.splits, self.lead)

    def n_pieces(self):
        return _n_pieces(self.splits, self.lead)


class Out:
    def __init__(self, shape, dtype, bshape, imap, splits=None, lead=0):
        self.shape, self.dtype, self.bshape, self.imap = tuple(shape), dtype, tuple(bshape), imap
        self.splits, self.lead = splits, lead

    def n_pieces(self):
        return _n_pieces(self.splits, self.lead)


def _n_pieces(splits, lead):
    return lead if lead else (1 if splits is None else len(splits))


def _load_pieces(ref, splits, lead):
    if lead:
        return [ref[k].astype(F32) for k in range(lead)]
    if splits is None:
        return [ref[...].astype(F32)]
    out, o = [], 0
    for w in splits:
        out.append(ref[..., o:o + w].astype(F32))
        o += w
    return out


def _store_pieces(ref, splits, lead, vals, accumulate=False):
    def put(idx, v):
        if accumulate:
            ref[idx] += v.astype(ref.dtype)
        else:
            ref[idx] = v.astype(ref.dtype)

    if lead:
        for k in range(lead):
            put(k, vals[k])
    elif splits is None:
        put(..., vals[0])
    else:
        o = 0
        for w, v in zip(splits, vals):
            put((..., slice(o, o + w)), v)
            o += w


def rowwise(fn, rows, params, outs, grid, name):
    nr, npar = len(rows), len(params)

    def body(*refs):
        ids = tuple(pl.program_id(a) for a in range(len(grid)))
        vals = []
        for r, ref in zip(rows, refs[:nr]):
            vals += r.pieces(ref)
        pvals = [ref[...].astype(F32) for ref in refs[nr:nr + npar]]
        res = list(fn(ids, *vals, *pvals))
        o = 0
        for spec, ref in zip(outs, refs[nr + npar:]):
            n = spec.n_pieces()
            _store_pieces(ref, spec.splits, spec.lead, res[o:o + n])
            o += n

    nz = len(grid)
    pspecs = [pl.BlockSpec(p.shape, (lambda *ids, _n=p.ndim: (0,) * _n)) for p in params]
    res = pl.pallas_call(
        body, name=name, grid=grid,
        in_specs=[r.spec() for r in rows] + pspecs,
        out_specs=[pl.BlockSpec(o.bshape, o.imap) for o in outs],
        out_shape=[SDS(o.shape, o.dtype) for o in outs],
        compiler_params=_cp(nz),
    )(*[r.arr for r in rows], *params)
    return list(res)


def rowwise_bwd(fn, rows, params, cots, grid, name):
    nr, npar, nc = len(rows), len(params), len(cots)
    drows = [r for r in rows if r.diff]
    nz = len(grid)

    def body(*refs):
        ids = tuple(pl.program_id(a) for a in range(nz))
        row_refs, par_refs = refs[:nr], refs[nr:nr + npar]
        cot_refs = refs[nr + npar:nr + npar + nc]
        drow_refs = refs[nr + npar + nc:nr + npar + nc + len(drows)]
        dpar_refs = refs[nr + npar + nc + len(drows):]
        pieces, is_diff = [], []
        for r, ref in zip(rows, row_refs):
            ps = r.pieces(ref)
            pieces += ps
            is_diff += [r.diff] * len(ps)
        pvals = [ref[...].astype(F32) for ref in par_refs]
        dvals = [p for p, dflag in zip(pieces, is_diff) if dflag]
        nd = len(dvals)

        def f(*args):
            it = iter(args[:nd])
            full = [next(it) if dflag else p for p, dflag in zip(pieces, is_diff)]
            return tuple(fn(ids, *full, *args[nd:]))

        _, vjp = jax.vjp(f, *dvals, *pvals)
        cvals = []
        for c, ref in zip(cots, cot_refs):
            cvals += c.pieces(ref)
        g = vjp(tuple(cvals))
        o = 0
        first_inner = ids[-1] == 0
        for r, ref in zip(drows, drow_refs):
            n = r.n_pieces()
            gs = g[o:o + n]
            o += n
            if r.acc:
                @pl.when(first_inner)
                def _(ref=ref):
                    ref[...] = jnp.zeros_like(ref)
            _store_pieces(ref, r.splits, r.lead, gs, accumulate=r.acc)
        first = functools.reduce(jnp.logical_and, [i == 0 for i in ids])
        for ref, gp in zip(dpar_refs, g[nd:]):
            @pl.when(first)
            def _(ref=ref):
                ref[...] = jnp.zeros_like(ref)
            ref[...] += gp

    pspecs = [pl.BlockSpec(p.shape, (lambda *ids, _n=p.ndim: (0,) * _n)) for p in params]
    res = pl.pallas_call(
        body, name=name, grid=grid,
        in_specs=[r.spec() for r in rows] + pspecs + [c.spec() for c in cots],
        out_specs=[r.gspec() for r in drows] + pspecs,
        out_shape=[SDS(r.gshape, r.gdtype) for r in drows] + [SDS(p.shape, F32) for p in params],
        compiler_params=_cp(nz),
    )(*[r.arr for r in rows], *params, *[c.arr for c in cots])
    res = list(res)
    return res[:len(drows)], res[len(drows):]


def _sigmoid(x):
    return 0.5 * (jnp.tanh(0.5 * x) + 1.0)


def _silu(x):
    return x * _sigmoid(x)


def _normmod(x, gain, sc, sh):
    inv = lax.rsqrt(jnp.mean(x * x, axis=-1, keepdims=True) + EPS)
    return x * inv * gain * (1.0 + sc) + sh


def f_first(ids, x, gain, sc, sh):
    return x, _normmod(x, gain, sc, sh)


def f_resid_norm(ids, x, y, g, gain, sc, sh):
    xn = x + g * y
    return xn, _normmod(xn, gain, sc, sh)


@jax.custom_vjp
def _swiglu(gate, up):
    return _silu(gate) * up


def _swiglu_fwd(gate, up):
    return _silu(gate) * up, (gate, up)


def _swiglu_bwd(res, da):
    gate, up = res
    s = _sigmoid(gate)
    gs = gate * s
    return da * up * (s + gs * (1.0 - s)), da * gs


_swiglu.defvjp(_swiglu_fwd, _swiglu_bwd)


def loss_and_grad(x, y, tgt, g, S):
    nt = S // WT

    def body(x_ref, y_ref, t_ref, g_ref, part_ref, dx_ref, dy_ref, dg_ref):
        @pl.when(pl.program_id(0) == 0)
        def _():
            dg_ref[...] = jnp.zeros_like(dg_ref)

        yv = y_ref[...].astype(F32)
        gg = g_ref[...]
        e = x_ref[...] + gg * yv - t_ref[...]
        part_ref[...] = 0.5 * jnp.sum(e * e, axis=0, keepdims=True) * (1.0 / D)
        d = e * (1.0 / D)
        dx_ref[...] = d
        dy_ref[...] = (d * gg).astype(dy_ref.dtype)
        dg_ref[...] += jnp.sum(d * yv, axis=0, keepdims=True)

    row = pl.BlockSpec((WT, D), lambda i: (i, 0))
    vec = pl.BlockSpec((1, D), lambda i: (0, 0))
    return pl.pallas_call(
        body, name="loss_and_grad", grid=(nt,), in_specs=[row, row, row, vec],
        out_specs=[pl.BlockSpec((None, 1, D), lambda i: (i, 0, 0)), row, row, vec],
        out_shape=[SDS((nt, 1, D), F32), SDS((S, D), F32), SDS((S, D), BF16), SDS((1, D), F32)],
        compiler_params=_cp(1),
    )(x, y, tgt, g)


def _softplus(x):
    return jnp.maximum(x, 0.0) + jnp.log(1.0 + jnp.exp(-jnp.abs(x)))


def _chunk_tril(T):
    r = lax.broadcasted_iota(jnp.int32, (T, T), 0)
    c = lax.broadcasted_iota(jnp.int32, (T, T), 1)
    return jnp.where((r // GDN_C == c // GDN_C) & (c <= r), 1.0, 0.0).astype(F32)


def _dot_hi(a, b, dims=((1,), (0,))):
    return lax.dot_general(a, b, (dims, ((), ())), precision=lax.Precision.HIGHEST, preferred_element_type=F32)


def _dot_x3(a, b, dims=((1,), (0,))):
    return lax.dot_general(a, b, (dims, ((), ())), precision=lax.Precision.HIGH, preferred_element_type=F32)


def f_gdn_gates(ids, ab, alog, dtb):
    T = ab.shape[0]
    g = -jnp.exp(alog) * _softplus(ab + dtb)
    beta = _sigmoid(ab)
    gcum = _dot_x3(_chunk_tril(T), g)
    row = lax.broadcasted_iota(jnp.int32, (LANES, LANES), 0)
    sel = lambda k: jnp.where(row == k, 1.0, 0.0).astype(F32)
    gcs = [_dot_x3(gcum, sel(h)) for h in range(GDN_H)]
    bts = [_dot_x3(beta, sel(GDN_H + h)) for h in range(GDN_H)]
    return (*gcs, *bts)


def f_gdn_post(ids, *args):
    os_, zs, gain = args[:GDN_H], args[GDN_H:2 * GDN_H], args[2 * GDN_H]
    out = []
    for o, z in zip(os_, zs):
        inv = lax.rsqrt(jnp.mean(o * o, axis=-1, keepdims=True) + EPS)
        out.append(o * inv * gain * _silu(z))
    return tuple(out)


def _qknorm1(x, gain2, scale):
    lane = lax.broadcasted_iota(jnp.int32, x.shape, 1)
    lo = lane < DSW_DH
    x2 = x * x
    s_all = jnp.sum(x2, axis=-1, keepdims=True)
    s_lo = jnp.sum(jnp.where(lo, x2, 0.0), axis=-1, keepdims=True)
    ms = jnp.where(lo, s_lo, s_all - s_lo) * (1.0 / DSW_DH)
    return x * lax.rsqrt(ms + EPS) * (gain2 * scale)


def f_combine(ids, o0, o1, o2, l0, l1, l2):
    m = jnp.maximum(jnp.maximum(l0, l1), l2)
    e0, e1, e2 = jnp.exp(l0 - m), jnp.exp(l1 - m), jnp.exp(l2 - m)
    den = e0 + e1 + e2
    o = (e0 * o0 + e1 * o1 + e2 * o2) / den
    return o, m + jnp.log(den)


GDN_T = 512
HALO = 16


def _conv_pre(xx, w):
    acc = xx * w[3:4, :]
    for j in range(3):
        acc = acc + pltpu.roll(xx, shift=3 - j, axis=0) * w[j:j + 1, :]
    return acc


@jax.custom_vjp
def _qkv_act_core(pre, norm_on, scale):
    s = _silu(pre)
    r = lax.rsqrt(jnp.sum(s * s, axis=-1, keepdims=True) + EPS)
    return jnp.where(norm_on > 0.5, s * r * scale, s)


def _qkv_act_fwd(pre, norm_on, scale):
    return _qkv_act_core(pre, norm_on, scale), (pre, norm_on, scale)


def _qkv_act_bwd(res, dout):
    pre, norm_on, scale = res
    sig = _sigmoid(pre)
    s = pre * sig
    r = lax.rsqrt(jnp.sum(s * s, axis=-1, keepdims=True) + EPS)
    unit = s * r
    dn = dout * scale
    ds = jnp.where(norm_on > 0.5, r * (dn - unit * jnp.sum(dn * unit, axis=-1, keepdims=True)), dout)
    return ds * (sig + s * (1.0 - sig)), jnp.zeros_like(norm_on), jnp.zeros_like(scale)


_qkv_act_core.defvjp(_qkv_act_fwd, _qkv_act_bwd)


def _qkv_act(pre, cidx):
    norm_on = jnp.where(cidx < 2 * GDN_H, 1.0, 0.0).astype(F32)
    scale = jnp.where(cidx < GDN_H, GDN_DK ** -0.5, 1.0).astype(F32)
    return _qkv_act_core(pre, norm_on, scale)


def gdn_pre(proj, conv_w, S):
    nt = S // GDN_T
    hb = GDN_T // HALO

    def body(prev_ref, cur_ref, w_ref, o_ref):
        p, i = pl.program_id(0), pl.program_id(1)
        for h in range(GDN_H):
            cols = slice(LANES * h, LANES * (h + 1))
            prev = jnp.where(i > 0, prev_ref[:, cols].astype(F32), 0.0)
            xx = jnp.concatenate([prev, cur_ref[:, cols].astype(F32)], axis=0)
            pre = _conv_pre(xx, w_ref[:, cols])[HALO:]
            o_ref[h] = _qkv_act(pre, p * GDN_H + h).astype(o_ref.dtype)

    hv = GDN_H * LANES
    return pl.pallas_call(
        body, name="gdn_pre", grid=(3, nt),
        in_specs=[pl.BlockSpec((HALO, hv), lambda p, i: (jnp.maximum(i * hb - 1, 0), p)),
                  pl.BlockSpec((GDN_T, hv), lambda p, i: (i, p)),
                  pl.BlockSpec((4, hv), lambda p, i: (0, p))],
        out_specs=pl.BlockSpec((None, GDN_H, GDN_T, LANES), lambda p, i: (p, 0, i, 0)),
        out_shape=SDS((3, GDN_H, S, LANES), BF16),
        compiler_params=_cp(2),
    )(proj, proj, conv_w)


def gdn_pre_bwd(proj, conv_w, dqkv, S):
    nt = S // GDN_T
    hb = GDN_T // HALO
    last_h = S // HALO - 1

    def body(prev_ref, cur_ref, next_ref, w_ref, d_ref, dnext_ref, dx_ref, dw_ref):
        p, i = pl.program_id(0), pl.program_id(1)

        @pl.when(i == 0)
        def _():
            dw_ref[...] = jnp.zeros_like(dw_ref)

        for h in range(GDN_H):
            cols = slice(LANES * h, LANES * (h + 1))
            w = w_ref[:, cols]
            prev = jnp.where(i > 0, prev_ref[:, cols].astype(F32), 0.0)
            xx = jnp.concatenate([prev, cur_ref[:, cols].astype(F32), next_ref[:, cols].astype(F32)], axis=0)
            dnext = jnp.where(i < nt - 1, dnext_ref[h], 0.0)
            dd = jnp.concatenate([jnp.zeros((HALO, LANES), F32), d_ref[h], dnext], axis=0)
            pre = _conv_pre(xx, w)
            _, vjp = jax.vjp(lambda v, _c=p * GDN_H + h: _qkv_act(v, _c), pre)
            (dpre,) = vjp(dd)
            dx = dpre * w[3:4, :]
            R = dpre.shape[0]
            for j in range(3):
                dx = dx + pltpu.roll(dpre, shift=R - (3 - j), axis=0) * w[j:j + 1, :]
            dx_ref[:, cols] = dx[HALO:HALO + GDN_T].astype(dx_ref.dtype)
            own = HALO + GDN_T
            rows_w = [jnp.sum((dpre * pltpu.roll(xx, shift=3 - j, axis=0))[:own], axis=0, keepdims=True) for j in range(3)]
            rows_w.append(jnp.sum((dpre * xx)[:own], axis=0, keepdims=True))
            r4 = lax.broadcasted_iota(jnp.int32, (4, LANES), 0)
            dw = jnp.zeros((4, LANES), F32)
            for j in range(4):
                dw = dw + jnp.where(r4 == j, rows_w[j], 0.0)
            dw_ref[:, cols] += dw

    hv = GDN_H * LANES
    return pl.pallas_call(
        body, name="gdn_pre_bwd", grid=(3, nt),
        in_specs=[pl.BlockSpec((HALO, hv), lambda p, i: (jnp.maximum(i * hb - 1, 0), p)),
                  pl.BlockSpec((GDN_T, hv), lambda p, i: (i, p)),
                  pl.BlockSpec((HALO, hv), lambda p, i: (jnp.minimum((i + 1) * hb, last_h), p)),
                  pl.BlockSpec((4, hv), lambda p, i: (0, p)),
                  pl.BlockSpec((None, GDN_H, GDN_T, LANES), lambda p, i: (p, 0, i, 0)),
                  pl.BlockSpec((None, GDN_H, HALO, LANES), lambda p, i: (p, 0, jnp.minimum((i + 1) * hb, last_h), 0))],
        out_specs=[pl.BlockSpec((GDN_T, hv), lambda p, i: (i, p)),
                   pl.BlockSpec((4, hv), lambda p, i: (0, p))],
        out_shape=[SDS((S, 3 * hv), BF16), SDS((4, 3 * hv), F32)],
        compiler_params=_cp(2),
    )(proj, proj, proj, conv_w, dqkv, dqkv)


_DIMS = {"nn": ((1,), (0,)), "nt": ((1,), (1,)), "tn": ((0,), (0,))}


def _mm_raw(a, b, mode, hi):
    if hi:
        return _dot_hi(a, b, _DIMS[mode])
    return lax.dot_general(a.astype(BF16), b.astype(BF16), (_DIMS[mode], ((), ())), preferred_element_type=F32)


@functools.partial(jax.custom_vjp, nondiff_argnums=(2, 3))
def mm(a, b, mode, hi):
    return _mm_raw(a, b, mode, hi)


def _mm_fwd(a, b, mode, hi):
    return _mm_raw(a, b, mode, hi), (a, b)


def _mm_bwd(mode, hi, res, dc):
    a, b = res
    if mode == "nn":
        da, db = mm(dc, b, "nt", hi), mm(a, dc, "tn", hi)
    elif mode == "nt":
        da, db = mm(dc, b, "nn", hi), mm(dc, a, "tn", hi)
    else:
        da, db = mm(b, dc, "nt", hi), mm(a, dc, "nn", hi)
    return da, db


mm.defvjp(_mm_fwd, _mm_bwd)


TRI_BASE = 8


def _unit_lower_inverses(Ls):
    n = Ls[0].shape[0]
    r = lax.broadcasted_iota(jnp.int32, (n, n), 0)
    c = lax.broadcasted_iota(jnp.int32, (n, n), 1)
    eye = jnp.where(r == c, 1.0, 0.0).astype(F32)
    base = r // TRI_BASE == c // TRI_BASE
    one = lambda a, b_: _mm_raw(a, b_, "nn", False)
    Ps = [jnp.where(base, -L, 0.0) for L in Ls]
    invs = [eye + P for P in Ps]
    k = 1
    while 2 * k < TRI_BASE:
        Ps = [one(P, P) for P in Ps]
        invs = [inv + one(inv, P) for inv, P in zip(invs, Ps)]
        k *= 2
    b = 2 * TRI_BASE
    while b <= n:
        off_mask = (r // b == c // b) & ((r % b) >= b // 2) & ((c % b) < b // 2)
        ts = [one(inv, jnp.where(off_mask, L, 0.0)) for inv, L in zip(invs, Ls)]
        invs = [inv - one(t, inv) for inv, t in zip(invs, ts)]
        b *= 2
    resid = [eye - inv - _dot_x3(L, inv) for inv, L in zip(invs, Ls)]
    return [inv + _dot_x3(inv, rs) for inv, rs in zip(invs, resid)]


@jax.custom_vjp
def tri_apply(invs, Ls, r1s, r2s):
    return [_mm_raw(i, r, "nn", False) for i, r in zip(invs, r1s)], [_mm_raw(i, r, "nn", False) for i, r in zip(invs, r2s)]


def _tri_fwd(invs, Ls, r1s, r2s):
    s1s = [_mm_raw(i, r, "nn", False) for i, r in zip(invs, r1s)]
    s2s = [_mm_raw(i, r, "nn", False) for i, r in zip(invs, r2s)]
    return (s1s, s2s), (invs, s1s, s2s)


def _tri_bwd(res, ds):
    invs, s1s, s2s = res
    d1s = [_mm_raw(i, d, "tn", False) for i, d in zip(invs, ds[0])]
    d2s = [_mm_raw(i, d, "tn", False) for i, d in zip(invs, ds[1])]
    dLs = [-(_mm_raw(d1, s1, "nt", False) + _mm_raw(d2, s2, "nt", False)) for d1, s1, d2, s2 in zip(d1s, s1s, d2s, s2s)]
    return [jnp.zeros_like(i) for i in invs], dLs, d1s, d2s


tri_apply.defvjp(_tri_fwd, _tri_bwd)


def _gdn_chunk(qs, ks, vs, gcbs, btbs, Ss, invs=None):
    C = qs[0].shape[0]
    r = lax.broadcasted_iota(jnp.int32, (C, C), 0)
    c = lax.broadcasted_iota(jnp.int32, (C, C), 1)
    causal, strict = c <= r, c < r
    rows = lax.broadcasted_iota(jnp.int32, gcbs[0].shape, 0)
    Gs = [g[:, :C] for g in gcbs]
    decays = [jnp.exp(jnp.where(causal, G - G.T, NEG)) for G in Gs]
    kbs = [k * b for k, b in zip(ks, btbs)]
    vbs = [v * b for v, b in zip(vs, btbs)]
    Ls = [jnp.where(strict, mm(kb, k, "nt", False) * d, 0.0) for kb, k, d in zip(kbs, ks, decays)]
    egs = [jnp.exp(g) for g in gcbs]
    if invs is None:
        invs = _unit_lower_inverses(Ls)
    us, ws = tri_apply(invs, Ls, vbs, [kb * eg for kb, eg in zip(kbs, egs)])
    qks = [jnp.where(causal, mm(q, k, "nt", False) * d, 0.0) for q, k, d in zip(qs, ks, decays)]
    g_lasts = [jnp.sum(jnp.where(rows == C - 1, g, 0.0), axis=0, keepdims=True) for g in gcbs]
    q_decs = [q * eg for q, eg in zip(qs, egs)]
    k_decs = [k * jnp.exp(gl - g) for k, gl, g in zip(ks, g_lasts, gcbs)]
    v_news = [u - mm(w, S, "nn", False) for u, w, S in zip(us, ws, Ss)]
    os_ = [mm(qd, S, "nn", False) + mm(qk, vn, "nn", False) for qd, S, qk, vn in zip(q_decs, Ss, qks, v_news)]
    S_news = [S * jnp.exp(gl) + mm(kd, vn, "tn", False) for S, gl, kd, vn in zip(Ss, g_lasts, k_decs, v_news)]
    return os_, S_news, invs


INV_CHUNKS = 4
SCAN_CHUNKS = 4


def gdn_inverses(qkv, gc, bt, S):
    nchunk = S // GDN_C
    rows = INV_CHUNKS * GDN_C

    def body(k_ref, g_ref, b_ref, inv_ref):
        items = [(h, m) for m in range(INV_CHUNKS) for h in range(GDN_H)]
        r = lax.broadcasted_iota(jnp.int32, (GDN_C, GDN_C), 0)
        c = lax.broadcasted_iota(jnp.int32, (GDN_C, GDN_C), 1)
        sl = lambda m: slice(m * GDN_C, (m + 1) * GDN_C)
        ks = [k_ref[h, sl(m), :].astype(F32) for h, m in items]
        Gs = [g_ref[h, sl(m), :GDN_C] for h, m in items]
        kbs = [k * b_ref[h, sl(m), :] for k, (h, m) in zip(ks, items)]
        decays = [jnp.exp(jnp.where(c <= r, G - G.T, NEG)) for G in Gs]
        Ls = [jnp.where(c < r, _mm_raw(kb, k, "nt", False) * d, 0.0) for kb, k, d in zip(kbs, ks, decays)]
        for (h, m), inv in zip(items, _unit_lower_inverses(Ls)):
            inv_ref[h, m] = inv

    hb = pl.BlockSpec((GDN_H, rows, LANES), lambda n: (0, n, 0))
    return pl.pallas_call(
        body, name="gdn_inverses", grid=(nchunk // INV_CHUNKS,),
        in_specs=[pl.BlockSpec((None, GDN_H, rows, LANES), lambda n: (1, 0, n, 0)), hb, hb],
        out_specs=pl.BlockSpec((GDN_H, INV_CHUNKS, GDN_C, GDN_C), lambda n: (0, n, 0, 0)),
        out_shape=SDS((GDN_H, nchunk, GDN_C, GDN_C), F32),
        compiler_params=_cp(1),
    )(qkv, gc, bt)


def gdn_core(qkv, gc, bt, invs, S):
    nchunk = S // GDN_C

    def body(qkv_ref, g_ref, b_ref, inv_ref, o_ref, st_ref, s_scr):
        n = pl.program_id(0)

        @pl.when(n == 0)
        def _():
            s_scr[...] = jnp.zeros_like(s_scr)

        heads = range(GDN_H)
        S_cur = [s_scr[h] for h in heads]
        for m in range(SCAN_CHUNKS):
            sl = slice(m * GDN_C, (m + 1) * GDN_C)
            os_, S_new, _ = _gdn_chunk(*[[qkv_ref[p, h, sl, :].astype(F32) for h in heads] for p in range(3)],
                                       [g_ref[h, sl, :] for h in heads], [b_ref[h, sl, :] for h in heads], S_cur,
                                       invs=[inv_ref[h, m] for h in heads])
            for h in heads:
                st_ref[h, m] = S_cur[h].astype(st_ref.dtype)
                o_ref[h, sl, :] = os_[h]
            S_cur = S_new
        for h in heads:
            s_scr[h] = S_cur[h]

    rows = SCAN_CHUNKS * GDN_C
    blk3 = pl.BlockSpec((3, GDN_H, rows, LANES), lambda n: (0, 0, n, 0))
    hb = pl.BlockSpec((GDN_H, rows, LANES), lambda n: (0, n, 0))
    return pl.pallas_call(
        body, name="gdn_core", grid=(nchunk // SCAN_CHUNKS,),
        in_specs=[blk3, hb, hb, pl.BlockSpec((GDN_H, SCAN_CHUNKS, GDN_C, GDN_C), lambda n: (0, n, 0, 0))],
        out_specs=[hb, pl.BlockSpec((GDN_H, SCAN_CHUNKS, GDN_DK, LANES), lambda n: (0, n, 0, 0))],
        out_shape=[SDS((GDN_H, S, LANES), F32), SDS((GDN_H, nchunk, GDN_DK, LANES), BF16)],
        scratch_shapes=[pltpu.VMEM((GDN_H, GDN_DK, LANES), F32)],
        compiler_params=_cp(1),
    )(qkv, gc, bt, invs)


def gdn_core_bwd(qkv, gc, bt, states, invs, do, S):
    nchunk = S // GDN_C

    def body(qkv_ref, g_ref, b_ref, st_ref, inv_ref, do_ref, dqkv_ref, dg_ref, db_ref, ds_scr):
        n = pl.program_id(0)

        @pl.when(n == 0)
        def _():
            ds_scr[...] = jnp.zeros_like(ds_scr)

        heads = range(GDN_H)
        dS_cur = [ds_scr[h] for h in heads]
        for m in reversed(range(SCAN_CHUNKS)):
            sl = slice(m * GDN_C, (m + 1) * GDN_C)
            saved = [inv_ref[h, m] for h in heads]
            _, vjp = jax.vjp(lambda *a, _s=saved: _gdn_chunk(*a, invs=_s)[:2],
                             *[[qkv_ref[p, h, sl, :].astype(F32) for h in heads] for p in range(3)],
                             [g_ref[h, sl, :] for h in heads], [b_ref[h, sl, :] for h in heads],
                             [st_ref[h, m].astype(F32) for h in heads])
            dq, dk, dv, dg, db, dS_cur = vjp(([do_ref[h, sl, :] for h in heads], dS_cur))
            for h in heads:
                dqkv_ref[0, h, sl, :] = dq[h]
                dqkv_ref[1, h, sl, :] = dk[h]
                dqkv_ref[2, h, sl, :] = dv[h]
                dg_ref[h, sl, :] = dg[h]
                db_ref[h, sl, :] = db[h]
        for h in heads:
            ds_scr[h] = dS_cur[h]

    nblk = nchunk // SCAN_CHUNKS
    rows = SCAN_CHUNKS * GDN_C
    rev = lambda n: nblk - 1 - n
    blk3 = pl.BlockSpec((3, GDN_H, rows, LANES), lambda n: (0, 0, rev(n), 0))
    hb = pl.BlockSpec((GDN_H, rows, LANES), lambda n: (0, rev(n), 0))
    return pl.pallas_call(
        body, name="gdn_core_bwd", grid=(nblk,),
        in_specs=[blk3, hb, hb, pl.BlockSpec((GDN_H, SCAN_CHUNKS, GDN_DK, LANES), lambda n: (0, rev(n), 0, 0)),
                  pl.BlockSpec((GDN_H, SCAN_CHUNKS, GDN_C, GDN_C), lambda n: (0, rev(n), 0, 0)), hb],
        out_specs=[blk3, hb, hb],
        out_shape=[SDS((3, GDN_H, S, LANES), F32), SDS((GDN_H, S, LANES), F32), SDS((GDN_H, S, LANES), F32)],
        scratch_shapes=[pltpu.VMEM((GDN_H, GDN_DK, LANES), F32)],
        compiler_params=_cp(1),
    )(qkv, gc, bt, states, invs, do)


GDN_MAIN = 4 * GDN_H * LANES
GDN_PROJ = GDN_MAIN + LANES
RT = 512


def gdn_forward(h, w_in, conv_w, alog, dtb, out_gain, w_out):
    S = h.shape[0]
    nt = S // RT
    proj = matmul(h, w_in, "nn", BF16, "gdn_in")
    qkv = gdn_pre(proj, conv_w, S)
    ab_row = Row(proj, (RT, LANES), lambda i: (i, GDN_MAIN // LANES), gdtype=BF16, gshape=(S, LANES), gimap=lambda i: (i, 0))
    hm = lambda i: (0, i, 0)
    hv = GDN_H * LANES
    gc, bt = rowwise(f_gdn_gates, [ab_row], [alog, dtb],
                     [Out((GDN_H, S, LANES), F32, (GDN_H, RT, LANES), hm, lead=GDN_H)] * 2, (nt,), "gdn_gates")
    invs = gdn_inverses(qkv, gc, bt, S)
    o, states = gdn_core(qkv, gc, bt, invs, S)
    o_row = Row(o, (GDN_H, RT, LANES), hm, lead=GDN_H)
    z_row = Row(proj, (RT, hv), lambda i: (i, 3), splits=[LANES] * GDN_H, gdtype=BF16, gshape=(S, hv), gimap=lambda i: (i, 0))
    (on,) = rowwise(f_gdn_post, [o_row, z_row], [out_gain],
                    [Out((S, hv), BF16, (RT, hv), lambda i: (i, 0), splits=[LANES] * GDN_H)], (nt,), "gdn_post")
    y = matmul(on, w_out, "nn", BF16, "gdn_out")
    saved = dict(h=h, proj=proj, qkv=qkv, gc=gc, bt=bt, states=states, invs=invs, o=o, on=on, ab_row=ab_row, o_row=o_row, z_row=z_row)
    return y, saved


def gdn_backward(dy, sv, w_in, conv_w, alog, dtb, out_gain, w_out, on_weight_grads=None):
    S = dy.shape[0]
    nt = S // RT
    hm = lambda i: (0, i, 0)
    hv = GDN_H * LANES
    don = matmul(dy, w_out, "nt", BF16, "gdn_out_dx")
    d_w_out = matmul(sv["on"], dy, "tn", F32, "gdn_out_dw")
    (do, dz), (d_gain,) = rowwise_bwd(f_gdn_post, [sv["o_row"], sv["z_row"]], [out_gain],
                                      [Row(don, (RT, hv), lambda i: (i, 0), splits=[LANES] * GDN_H)], (nt,), "gdn_post_bwd")
    dqkv, dgc, dbt = gdn_core_bwd(sv["qkv"], sv["gc"], sv["bt"], sv["states"], sv["invs"], do, S)
    head_blk = lambda a: Row(a, (GDN_H, RT, LANES), hm, lead=GDN_H)
    (dab,), (d_alog, d_dtb) = rowwise_bwd(f_gdn_gates, [sv["ab_row"]], [alog, dtb], [head_blk(dgc), head_blk(dbt)],
                                          (nt,), "gdn_gates_bwd")
    dqkv_proj, d_conv = gdn_pre_bwd(sv["proj"], conv_w, dqkv, S)
    dproj = jnp.concatenate([dqkv_proj, dz, dab], axis=1)
    d_w_in = matmul(sv["h"], dproj, "tn", F32, "gdn_in_dw")
    if on_weight_grads is not None:
        w_in = w_in + on_weight_grads(d_w_in, d_w_out).astype(w_in.dtype)
    dh = matmul(dproj, w_in, "nt", BF16, "gdn_in_dx")
    return dh, dict(w_in=d_w_in, conv=d_conv, alog=d_alog, dtb=d_dtb, gain=d_gain, w_out=d_w_out)


QB = DSW_SPAN
N_HP = DSW_HG // LANES


def _bucket_maps():
    a = np.arange(QB)[:, None]
    j = np.arange(2 * QB)[None, :]
    dist = QB + a - j
    band = (dist >= 0) & (dist <= DSW_SPAN)
    maps = []
    for _, dil in DSW_GROUPS:
        dd = np.maximum(dist, 0) * dil
        max_exact = REL_BUCKETS // 2
        scaled = np.log(np.maximum(dd, 1).astype(np.float32) / np.float32(max_exact)) / np.float32(math.log(REL_MAX_DIST / max_exact))
        large = max_exact + (scaled * np.float32(REL_BUCKETS - max_exact)).astype(np.int32)
        large = np.minimum(large, REL_BUCKETS - 1)
        maps.append(np.where(dd < max_exact, dd, large).astype(np.int32))
    return np.stack(maps), band


def dsw_bias(rel_bias):
    maps, band = _bucket_maps()
    maps = np.where(band[None], maps, -1).astype(np.int32)

    def body(tab_ref, bk_ref, o_ref):
        gh = pl.program_id(0)
        bk = bk_ref[...]
        acc = jnp.full(bk.shape, NEG, F32)
        for b in range(REL_BUCKETS):
            acc = jnp.where(bk == b, tab_ref[b, gh], acc)
        o_ref[...] = acc

    return pl.pallas_call(
        body, name="dsw_bias", grid=(3 * GDN_H,),
        in_specs=[pl.BlockSpec(memory_space=pltpu.SMEM),
                  pl.BlockSpec((None, QB, 2 * QB), lambda gh: (gh // GDN_H, 0, 0))],
        out_specs=pl.BlockSpec((None, QB, 2 * QB), lambda gh: (gh, 0, 0)),
        out_shape=SDS((3 * GDN_H, QB, 2 * QB), F32),
        compiler_params=_cp(1),
    )(rel_bias, jnp.asarray(maps))


def dsw_bias_grad(dbias):
    maps, band = _bucket_maps()
    maps = np.where(band[None], maps, -1).astype(np.int32)

    def body(d_ref, bk_ref, o_ref):
        bk = bk_ref[...]
        d = d_ref[...]
        rows = lax.broadcasted_iota(jnp.int32, (REL_BUCKETS, LANES), 0)
        acc = jnp.zeros((REL_BUCKETS, LANES), F32)
        for b in range(REL_BUCKETS):
            part = jnp.sum(jnp.where(bk == b, d, 0.0), axis=0, keepdims=True)
            val = jnp.sum(part, axis=1, keepdims=True)
            acc = jnp.where(rows == b, val, acc)
        o_ref[...] = acc

    return pl.pallas_call(
        body, name="dsw_bias_grad", grid=(3 * GDN_H,),
        in_specs=[pl.BlockSpec((None, QB, 2 * QB), lambda gh: (gh, 0, 0)),
                  pl.BlockSpec((None, QB, 2 * QB), lambda gh: (gh // GDN_H, 0, 0))],
        out_specs=pl.BlockSpec((None, REL_BUCKETS, LANES), lambda gh: (gh, 0, 0)),
        out_shape=SDS((3 * GDN_H, REL_BUCKETS, LANES), F32),
        compiler_params=_cp(1),
    )(dbias, jnp.asarray(maps))


def _nt(a, b):
    return lax.dot_general(a, b, (((1,), (1,)), ((), ())), preferred_element_type=F32)


def _tn(a, b):
    return lax.dot_general(a, b, (((0,), (0,)), ((), ())), preferred_element_type=F32)


N_LB = DSW_HG // LANES
HALF = DSW_DH // 2


def _lanes(j):
    return slice(LANES * j, LANES * (j + 1))


def _deinterleave(stage, out_ref, dil, rows, dtype):
    for r in range(dil):
        for j in range(N_LB):
            out_ref[r, :, _lanes(j)] = stage[j, pl.ds(r, rows, stride=dil), :].astype(dtype)


def _interleave(in_ref, stage, dil, rows):
    for r in range(dil):
        for j in range(N_LB):
            stage[j, pl.ds(r, rows, stride=dil), :] = in_ref[r, :, _lanes(j)].astype(F32)


def dsw_prep(proj, q_gain2, k_gain2, gi, S):
    dil = DSW_GROUPS[gi][1]
    nt, rows = S // RT, RT // dil

    def body(q_ref, k_ref, v_ref, qg_ref, kg_ref, qo_ref, ko_ref, vo_ref, stage):
        for src, gain_ref, scale, dst in ((q_ref, qg_ref, DSW_DH ** -0.5, qo_ref), (k_ref, kg_ref, 1.0, ko_ref), (v_ref, None, None, vo_ref)):
            for j in range(N_LB):
                val = src[:, _lanes(j)].astype(F32)
                val = val if gain_ref is None else _qknorm1(val, gain_ref[...], scale)
                if dil == 1:
                    dst[0, :, _lanes(j)] = val.astype(BF16)
                else:
                    stage[j] = val
            if dil > 1:
                _deinterleave(stage, dst, dil, rows, BF16)

    col = lambda which: pl.BlockSpec((RT, DSW_HG), lambda i, _c=which * 3 + gi: (i, _c))
    gspec = pl.BlockSpec((1, LANES), lambda i: (0, 0))
    ospec = pl.BlockSpec((dil, rows, DSW_HG), lambda i: (0, i, 0))
    return pl.pallas_call(
        body, name=f"dsw_prep_g{gi}", grid=(nt,),
        in_specs=[col(0), col(1), col(2), gspec, gspec], out_specs=[ospec] * 3,
        out_shape=[SDS((dil, S // dil, DSW_HG), BF16)] * 3,
        scratch_shapes=[pltpu.VMEM((N_LB, RT, LANES), F32)], compiler_params=_cp(1),
    )(proj, proj, proj, q_gain2, k_gain2)


def dsw_prep_bwd(proj, q_gain2, k_gain2, dqd, dkd, dvd, gi, S):
    dil = DSW_GROUPS[gi][1]
    nt, rows = S // RT, RT // dil

    def body(q_ref, k_ref, qg_ref, kg_ref, dq_ref, dk_ref, dv_ref, oq_ref, ok_ref, ov_ref, dqg_ref, dkg_ref, stage):
        i = pl.program_id(0)

        @pl.when(i == 0)
        def _():
            dqg_ref[...] = jnp.zeros_like(dqg_ref)
            dkg_ref[...] = jnp.zeros_like(dkg_ref)

        for src, gain_ref, scale, cot_ref, dst, dg_ref in ((q_ref, qg_ref, DSW_DH ** -0.5, dq_ref, oq_ref, dqg_ref),
                                                          (k_ref, kg_ref, 1.0, dk_ref, ok_ref, dkg_ref)):
            if dil > 1:
                _interleave(cot_ref, stage, dil, rows)
            for j in range(N_LB):
                _, vjp = jax.vjp(lambda x, g, _s=scale: _qknorm1(x, g, _s), src[:, _lanes(j)].astype(F32), gain_ref[...])
                dx, dg = vjp(stage[j] if dil > 1 else cot_ref[0, :, _lanes(j)].astype(F32))
                dst[:, _lanes(j)] = dx.astype(dst.dtype)
                dg_ref[...] += dg
        if dil > 1:
            _interleave(dv_ref, stage, dil, rows)
        for j in range(N_LB):
            ov_ref[:, _lanes(j)] = (stage[j] if dil > 1 else dv_ref[0, :, _lanes(j)]).astype(ov_ref.dtype)

    col = lambda which: pl.BlockSpec((RT, DSW_HG), lambda i, _c=which * 3 + gi: (i, _c))
    gspec = pl.BlockSpec((1, LANES), lambda i: (0, 0))
    dspec = pl.BlockSpec((dil, rows, DSW_HG), lambda i: (0, i, 0))
    nspec = pl.BlockSpec((RT, DSW_HG), lambda i: (i, 0))
    return pl.pallas_call(
        body, name=f"dsw_prep_bwd_g{gi}", grid=(nt,),
        in_specs=[col(0), col(1), gspec, gspec, dspec, dspec, dspec], out_specs=[nspec] * 3 + [gspec] * 2,
        out_shape=[SDS((S, DSW_HG), BF16)] * 3 + [SDS((1, LANES), F32)] * 2,
        scratch_shapes=[pltpu.VMEM((N_LB, RT, LANES), F32)], compiler_params=_cp(1),
    )(proj, proj, q_gain2, k_gain2, dqd, dkd, dvd)


def _head_masks(rows):
    lane = lax.broadcasted_iota(jnp.int32, (rows, LANES), 1)
    return lane < DSW_DH, (lane % DSW_DH) < HALF


def dsw_attn_fwd(qd, kd, vd, bias, gi, S):
    dil = DSW_GROUPS[gi][1]
    sd = S // dil
    nq = sd // QB
    QPS = 2 if nq % 2 == 0 else 1

    def body(q_ref, k_ref, v_ref, b_ref, o_ref, l_ref, kp_scr, vp_scr):
        i = pl.program_id(1)

        @pl.when(i == 0)
        def _():
            kp_scr[...] = jnp.zeros_like(kp_scr)
            vp_scr[...] = jnp.zeros_like(vp_scr)

        lo_q, _ = _head_masks(QB)
        lo_k, _ = _head_masks(2 * QB)
        col = lax.broadcasted_iota(jnp.int32, (QB, 2 * QB), 1)
        first = jnp.logical_and(i == 0, col < QB)
        hps, heads = range(N_HP), range(2 * N_HP)
        mqs = [lo_q if h % 2 == 0 else jnp.logical_not(lo_q) for h in heads]
        mks = [lo_k if h % 2 == 0 else jnp.logical_not(lo_k) for h in heads]
        k_last, v_last = k_ref[(QPS - 1) * QB:, :], v_ref[(QPS - 1) * QB:, :]
        for m in range(QPS):
            rows = slice(m * QB, (m + 1) * QB)
            before = slice((m - 1) * QB, m * QB)
            kps = [kp_scr[:, _lanes(hp)] if m == 0 else k_ref[before, _lanes(hp)] for hp in hps]
            vps = [vp_scr[:, _lanes(hp)] if m == 0 else v_ref[before, _lanes(hp)] for hp in hps]
            k2s = [jnp.concatenate([kps[hp], k_ref[rows, _lanes(hp)]], axis=0) for hp in hps]
            v2s = [jnp.concatenate([vps[hp], v_ref[rows, _lanes(hp)]], axis=0) for hp in hps]
            qs = [q_ref[rows, _lanes(hp)] for hp in hps]
            ss = [_nt(jnp.where(mqs[h], qs[h // 2], 0).astype(BF16), k2s[h // 2]) + b_ref[h] for h in heads]
            if m == 0:
                ss = [jnp.where(first, NEG, s) for s in ss]
            mxs = [jnp.max(s, axis=1, keepdims=True) for s in ss]
            ps = [jnp.exp(s - mx) for s, mx in zip(ss, mxs)]
            ls = [jnp.sum(p, axis=1, keepdims=True) for p in ps]
            ohs = [jnp.dot(ps[h].astype(BF16), jnp.where(mks[h], v2s[h // 2], 0).astype(BF16), preferred_element_type=F32) / ls[h]
                   for h in heads]
            lse_h = [mx + jnp.log(l) for mx, l in zip(mxs, ls)]
            for hp in hps:
                o_ref[rows, _lanes(hp)] = ohs[2 * hp] + ohs[2 * hp + 1]
                l_ref[rows, _lanes(hp)] = jnp.where(lo_q, lse_h[2 * hp], lse_h[2 * hp + 1])
        kp_scr[...] = k_last
        vp_scr[...] = v_last

    blk = pl.BlockSpec((None, QPS * QB, DSW_HG), lambda r, i: (r, i, 0))
    return pl.pallas_call(
        body, name=f"dsw_attn_g{gi}", grid=(dil, nq // QPS),
        in_specs=[blk, blk, blk, pl.BlockSpec((GDN_H, QB, 2 * QB), lambda r, i: (gi, 0, 0))],
        out_specs=[blk, blk], out_shape=[SDS((dil, sd, DSW_HG), F32)] * 2,
        scratch_shapes=[pltpu.VMEM((QB, DSW_HG), BF16)] * 2, compiler_params=_cp(2),
    )(qd, kd, vd, bias)


def dsw_attn_bwd(qd, kd, vd, bias, dod, statd, gi, S):
    dil = DSW_GROUPS[gi][1]
    sd = S // dil
    nq = sd // QB
    B = 2 if nq % 2 == 0 else 1
    nb = nq // B
    cur = lambda i: jnp.minimum(i, nb - 1)
    done = lambda i: jnp.maximum(i - 1, 0)
    last = slice((B - 1) * QB, B * QB)

    def body(q_ref, k_ref, v_ref, b_ref, do_ref, st_ref, dq_ref, dk_ref, dv_ref, db_ref, kp_scr, vp_scr, dk_scr, dv_scr):
        r, i = pl.program_id(0), pl.program_id(1)

        @pl.when(jnp.logical_and(r == 0, i == 0))
        def _():
            db_ref[...] = jnp.zeros_like(db_ref)

        @pl.when(i == 0)
        def _():
            for scr in (kp_scr, vp_scr, dk_scr, dv_scr):
                scr[...] = jnp.zeros_like(scr)

        @pl.when(i < nb)
        def _():
            lo_q, first_half = _head_masks(QB)
            col = lax.broadcasted_iota(jnp.int32, (QB, 2 * QB), 1)
            first = jnp.logical_and(i == 0, col < QB)
            hps, heads = range(N_HP), range(2 * N_HP)
            mqs = [lo_q if h % 2 == 0 else jnp.logical_not(lo_q) for h in heads]
            k_last, v_last = k_ref[last, :], v_ref[last, :]
            to_prev, to_cur = [], []
            for m in range(B):
                rows = slice(m * QB, (m + 1) * QB)
                before = slice((m - 1) * QB, m * QB)
                kps = [kp_scr[:, _lanes(hp)] if m == 0 else k_ref[before, _lanes(hp)] for hp in hps]
                vps = [vp_scr[:, _lanes(hp)] if m == 0 else v_ref[before, _lanes(hp)] for hp in hps]
                k2s = [jnp.concatenate([kps[hp], k_ref[rows, _lanes(hp)]], axis=0) for hp in hps]
                v2s = [jnp.concatenate([vps[hp], v_ref[rows, _lanes(hp)]], axis=0) for hp in hps]
                qs = [q_ref[rows, _lanes(hp)] for hp in hps]
                douts = [do_ref[rows, _lanes(hp)] for hp in hps]
                stats = [st_ref[rows, _lanes(hp)] for hp in hps]
                qms = [jnp.where(mqs[h], qs[h // 2], 0).astype(BF16) for h in heads]
                doms = [jnp.where(mqs[h], douts[h // 2], 0).astype(BF16) for h in heads]
                ss = [_nt(qms[h], k2s[h // 2]) + b_ref[h] for h in heads]
                if m == 0:
                    ss = [jnp.where(first, NEG, s) for s in ss]
                lses = [jnp.max(jnp.where(jnp.logical_and(mqs[h], first_half), stats[h // 2], NEG), axis=1, keepdims=True) for h in heads]
                deltas = [jnp.max(jnp.where(jnp.logical_and(mqs[h], jnp.logical_not(first_half)), stats[h // 2], NEG), axis=1,
                                  keepdims=True) for h in heads]
                ps = [jnp.exp(ss[h] - lses[h]) for h in heads]
                dss = [ps[h] * (_nt(doms[h], v2s[h // 2]) - deltas[h]) for h in heads]
                dsbs = [d.astype(BF16) for d in dss]
                dqh = [jnp.where(mqs[h], jnp.dot(dsbs[h], k2s[h // 2], preferred_element_type=F32), 0.0) for h in heads]
                dkh = [_tn(dsbs[h], qms[h]) for h in heads]
                dvh = [_tn(ps[h].astype(BF16), doms[h]) for h in heads]
                for h in heads:
                    db_ref[h] += dss[h]
                for hp in hps:
                    dq_ref[rows, _lanes(hp)] = dqh[2 * hp] + dqh[2 * hp + 1]
                dk2s = [dkh[2 * hp] + dkh[2 * hp + 1] for hp in hps]
                dv2s = [dvh[2 * hp] + dvh[2 * hp + 1] for hp in hps]
                to_prev.append(([d[:QB] for d in dk2s], [d[:QB] for d in dv2s]))
                to_cur.append(([d[QB:] for d in dk2s], [d[QB:] for d in dv2s]))
            for hp in hps:
                if B > 1:
                    dk_ref[:(B - 1) * QB, _lanes(hp)] = dk_scr[:(B - 1) * QB, _lanes(hp)]
                    dv_ref[:(B - 1) * QB, _lanes(hp)] = dv_scr[:(B - 1) * QB, _lanes(hp)].astype(dv_ref.dtype)
                dk_ref[last, _lanes(hp)] = dk_scr[last, _lanes(hp)] + to_prev[0][0][hp]
                dv_ref[last, _lanes(hp)] = (dv_scr[last, _lanes(hp)] + to_prev[0][1][hp]).astype(dv_ref.dtype)
            for hp in hps:
                for m in range(B):
                    rows = slice(m * QB, (m + 1) * QB)
                    nk, nv = to_cur[m][0][hp], to_cur[m][1][hp]
                    if m < B - 1:
                        nk, nv = nk + to_prev[m + 1][0][hp], nv + to_prev[m + 1][1][hp]
                    dk_scr[rows, _lanes(hp)] = nk
                    dv_scr[rows, _lanes(hp)] = nv
            kp_scr[...] = k_last
            vp_scr[...] = v_last

        @pl.when(i == nb)
        def _():
            dk_ref[...] = dk_scr[...]
            dv_ref[...] = dv_scr[...].astype(dv_ref.dtype)

    blk = pl.BlockSpec((None, B * QB, DSW_HG), lambda r, i: (r, cur(i), 0))
    oblk = pl.BlockSpec((None, B * QB, DSW_HG), lambda r, i: (r, done(i), 0))
    return pl.pallas_call(
        body, name=f"dsw_attn_bwd_g{gi}", grid=(dil, nb + 1),
        in_specs=[blk, blk, blk, pl.BlockSpec((GDN_H, QB, 2 * QB), lambda r, i: (gi, 0, 0)), blk, blk],
        out_specs=[blk, oblk, oblk, pl.BlockSpec((GDN_H, QB, 2 * QB), lambda r, i: (0, 0, 0))],
        out_shape=[SDS((dil, sd, DSW_HG), F32), SDS((dil, sd, DSW_HG), F32), SDS((dil, sd, DSW_HG), BF16),
                   SDS((GDN_H, QB, 2 * QB), F32)],
        scratch_shapes=[pltpu.VMEM((QB, DSW_HG), BF16)] * 2 + [pltpu.VMEM((B * QB, DSW_HG), F32)] * 2,
        compiler_params=_cp(2),
    )(qd, kd, vd, bias, dod, statd)


def dsw_combine(ods, lseds, S):
    nt = S // RT
    dils = [d for _, d in DSW_GROUPS]

    def body(*refs):
        ins, (o_ref, l_ref), stages = refs[:6], refs[6:8], refs[8:]
        for g in range(3):
            if dils[g] > 1:
                _interleave(ins[g], stages[g], dils[g], RT // dils[g])
                _interleave(ins[3 + g], stages[3 + g], dils[g], RT // dils[g])
        for j in range(N_LB):
            natural = lambda a: stages[a][j] if dils[a % 3] > 1 else ins[a][0, :, _lanes(j)]
            o, lse = f_combine(None, *[natural(a) for a in range(6)])
            o_ref[:, _lanes(j)] = o.astype(o_ref.dtype)
            l_ref[:, _lanes(j)] = lse

    dspec = lambda d: pl.BlockSpec((d, RT // d, DSW_HG), lambda i: (0, i, 0))
    nspec = pl.BlockSpec((RT, DSW_HG), lambda i: (i, 0))
    return pl.pallas_call(
        body, name="dsw_combine", grid=(nt,),
        in_specs=[dspec(d) for d in dils] * 2, out_specs=[nspec, nspec],
        out_shape=[SDS((S, DSW_HG), BF16), SDS((S, DSW_HG), F32)],
        scratch_shapes=[pltpu.VMEM((N_LB, RT, LANES), F32)] * 6, compiler_params=_cp(1),
    )(*ods, *lseds)


def dsw_bwd_prep(do, o, lse, S):
    nt = S // RT
    dils = [d for _, d in DSW_GROUPS]

    def body(do_ref, o_ref, l_ref, *rest):
        outs, (st_do, st_stat) = rest[:6], rest[6:]
        lo, first_half = _head_masks(RT)
        for j in range(N_LB):
            dout = do_ref[:, _lanes(j)]
            prod = dout * o_ref[:, _lanes(j)].astype(F32)
            s_all = jnp.sum(prod, axis=1, keepdims=True)
            s_lo = jnp.sum(jnp.where(lo, prod, 0.0), axis=1, keepdims=True)
            delta = jnp.where(lo, s_lo, s_all - s_lo)
            stat = jnp.where(first_half, l_ref[:, _lanes(j)], delta)
            st_do[j] = dout
            st_stat[j] = stat
            for g in range(3):
                if dils[g] == 1:
                    outs[g][0, :, _lanes(j)] = dout.astype(BF16)
                    outs[3 + g][0, :, _lanes(j)] = stat
        for g in range(3):
            if dils[g] > 1:
                _deinterleave(st_do, outs[g], dils[g], RT // dils[g], BF16)
                _deinterleave(st_stat, outs[3 + g], dils[g], RT // dils[g], F32)

    nspec = pl.BlockSpec((RT, DSW_HG), lambda i: (i, 0))
    dspec = lambda d: pl.BlockSpec((d, RT // d, DSW_HG), lambda i: (0, i, 0))
    res = pl.pallas_call(
        body, name="dsw_bwd_prep", grid=(nt,),
        in_specs=[nspec] * 3, out_specs=[dspec(d) for d in dils] * 2,
        out_shape=[SDS((d, S // d, DSW_HG), BF16) for d in dils] + [SDS((d, S // d, DSW_HG), F32) for d in dils],
        scratch_shapes=[pltpu.VMEM((N_LB, RT, LANES), F32)] * 2, compiler_params=_cp(1),
    )(do, o, lse)
    return res[:3], res[3:]


def dsw_forward(h, w_in, q_gain2, k_gain2, rel_bias, w_out):
    S = h.shape[0]
    proj = matmul(h, w_in, "nn", BF16, "dsw_in", col_shards=N_SHARD)
    bias = dsw_bias(rel_bias)
    qkv, ods, lseds = [], [], []
    for gi in range(3):
        qd, kd, vd = dsw_prep(proj, q_gain2, k_gain2, gi, S)
        od, ld = dsw_attn_fwd(qd, kd, vd, bias, gi, S)
        qkv.append((qd, kd, vd))
        ods.append(od)
        lseds.append(ld)
    o, lse = dsw_combine(ods, lseds, S)
    y = matmul(o, w_out, "nn", BF16, "dsw_out", col_shards=N_SHARD)
    return y, dict(h=h, proj=proj, qkv=qkv, bias=bias, o=o, lse=lse)


def dsw_backward(dy, sv, w_in, q_gain2, k_gain2, w_out):
    S = dy.shape[0]
    do = matmul(dy, w_out, "nt", F32, "dsw_out_dx", col_shards=N_SHARD)
    d_w_out = matmul(sv["o"], dy, "tn", F32, "dsw_out_dw", col_shards=N_SHARD)
    dods, statds = dsw_bwd_prep(do, sv["o"], sv["lse"], S)
    pieces_q, pieces_k, pieces_v, dbs = [], [], [], []
    d_qg = jnp.zeros((1, LANES), F32)
    d_kg = jnp.zeros((1, LANES), F32)
    for gi in range(3):
        qd, kd, vd = sv["qkv"][gi]
        dqd, dkd, dvd, db = dsw_attn_bwd(qd, kd, vd, sv["bias"], dods[gi], statds[gi], gi, S)
        dq, dk, dv, dqg, dkg = dsw_prep_bwd(sv["proj"], q_gain2, k_gain2, dqd, dkd, dvd, gi, S)
        dbs.append(db)
        pieces_q.append(dq)
        pieces_k.append(dk)
        pieces_v.append(dv)
        d_qg = d_qg + dqg
        d_kg = d_kg + dkg
    dproj = jnp.concatenate(pieces_q + pieces_k + pieces_v, axis=1)
    d_w_in = matmul(sv["h"], dproj, "tn", F32, "dsw_in_dw", col_shards=N_SHARD)
    dh = matmul(dproj, w_in, "nt", BF16, "dsw_in_dx", col_shards=N_SHARD)
    d_rel = dsw_bias_grad(jnp.concatenate(dbs, axis=0))
    return dh, dict(w_in=d_w_in, q_gain2=d_qg, k_gain2=d_kg, rel=d_rel, w_out=d_w_out)


FUSE_M = 512


def ffn_in_act(h, w_in, name):
    S = h.shape[0]
    half = FFN // 2

    def body(h_ref, wg_ref, wu_ref, gu_ref, a_ref):
        j = pl.program_id(1)
        sub = FUSE_M // 2
        for part in range(2):
            rows = slice(part * sub, (part + 1) * sub)
            hb = h_ref[rows, :]
            g = jnp.dot(hb, wg_ref[...], preferred_element_type=F32)
            u = jnp.dot(hb, wu_ref[...], preferred_element_type=F32)
            a_ref[rows, :] = (_silu(g) * u).astype(a_ref.dtype)
            for jj in range(2):
                @pl.when(j == jj)
                def _(g=g, u=u, jj=jj, rows=rows):
                    gu_ref[rows, jj * half:(jj + 1) * half] = g.astype(gu_ref.dtype)
                    gu_ref[rows, FFN + jj * half:FFN + (jj + 1) * half] = u.astype(gu_ref.dtype)

    return pl.pallas_call(
        body, name=name, grid=(S // FUSE_M, 2),
        in_specs=[pl.BlockSpec((FUSE_M, D), lambda i, j: (i, 0)),
                  pl.BlockSpec((None, D, half), lambda i, j: (j, 0, 0)),
                  pl.BlockSpec((None, D, half), lambda i, j: (j + 2, 0, 0))],
        out_specs=[pl.BlockSpec((FUSE_M, 2 * FFN), lambda i, j: (i, 0)), pl.BlockSpec((FUSE_M, half), lambda i, j: (i, j))],
        out_shape=[SDS((S, 2 * FFN), BF16), SDS((S, FFN), BF16)],
        compiler_params=_cp(2),
    )(h, w_in, w_in)


def ffn_forward(h, w_in, w_out, tag):
    gu, a = ffn_in_act(h, w_in, f"ffn_in_act_{tag}")
    f = matmul(a, w_out, "nn", BF16, f"ffn_out_{tag}")
    return f, dict(h=h, gu=gu, a=a)


def ffn_out_dx_act(df, w_out, gu, name):
    S = df.shape[0]
    half = FFN // 2

    def body(df_ref, w_ref, g_ref, u_ref, dgu_ref):
        j = pl.program_id(1)
        sub = FUSE_M // 2
        for part in range(2):
            rows = slice(part * sub, (part + 1) * sub)
            da = _nt(df_ref[rows, :], w_ref[...])
            dg, du = _swiglu_bwd((g_ref[rows, :].astype(F32), u_ref[rows, :].astype(F32)), da)
            for jj in range(2):
                @pl.when(j == jj)
                def _(dg=dg, du=du, jj=jj, rows=rows):
                    dgu_ref[rows, jj * half:(jj + 1) * half] = dg.astype(dgu_ref.dtype)
                    dgu_ref[rows, FFN + jj * half:FFN + (jj + 1) * half] = du.astype(dgu_ref.dtype)

    return pl.pallas_call(
        body, name=name, grid=(S // FUSE_M, 2),
        in_specs=[pl.BlockSpec((FUSE_M, D), lambda i, j: (i, 0)),
                  pl.BlockSpec((half, D), lambda i, j: (j, 0)),
                  pl.BlockSpec((FUSE_M, half), lambda i, j: (i, j)),
                  pl.BlockSpec((FUSE_M, half), lambda i, j: (i, j + 2))],
        out_specs=pl.BlockSpec((FUSE_M, 2 * FFN), lambda i, j: (i, 0)),
        out_shape=SDS((S, 2 * FFN), BF16),
        compiler_params=_cp(2),
    )(df, w_out, gu, gu)


def ffn_backward(df, sv, w_in, w_out, tag):
    d_w_out = matmul(sv["a"], df, "tn", F32, f"ffn_out_dw_{tag}")
    dgu = ffn_out_dx_act(df, w_out, sv["gu"], f"ffn_out_dx_act_{tag}")
    d_w_in = matmul(sv["h"], dgu, "tn", F32, f"ffn_in_dw_{tag}", col_shards=N_SHARD)
    dh = matmul(dgu, w_in, "nt", BF16, f"ffn_in_dx_{tag}", col_shards=N_SHARD)
    return dh, d_w_in, d_w_out


def f_norm_only(ids, x, gain, sc, sh):
    return (_normmod(x, gain, sc, sh),)


WT = 512


def _wide(a, **kw):
    return Row(a, (WT, D), lambda i: (i, 0), **kw)


def _wide_out(S, dtype):
    return Out((S, D), dtype, (WT, D), lambda i: (i, 0))


def adamw(w, g, m, v, name):
    shape = w.shape
    C = shape[-1]
    R = int(np.prod(shape[:-1]))
    w2, g2, m2, v2 = (a.reshape(R, C) for a in (w, g, m, v))
    br = R
    if R > 256:
        br = max(b for b in range(8, 257, 8) if R % b == 0)
    c1 = 1.0 / (1.0 - ADAM_B1 ** ADAM_STEP)
    c2 = 1.0 / (1.0 - ADAM_B2 ** ADAM_STEP)

    def body(w_ref, g_ref, m_ref, v_ref, d_ref, nm_ref, nv_ref):
        gg = g_ref[...]
        mm_ = ADAM_B1 * m_ref[...] + (1.0 - ADAM_B1) * gg
        vv = ADAM_B2 * v_ref[...] + (1.0 - ADAM_B2) * (gg * gg)
        d_ref[...] = -ADAM_LR * ((mm_ * c1) / (jnp.sqrt(vv * c2) + ADAM_EPS) + ADAM_WD * w_ref[...])
        nm_ref[...] = mm_
        nv_ref[...] = vv

    spec = pl.BlockSpec((br, C), lambda i: (i, 0))
    d, nm, nv = pl.pallas_call(
        body, name=name, grid=(R // br,), in_specs=[spec] * 4, out_specs=[spec] * 3,
        out_shape=[SDS((R, C), F32)] * 3, compiler_params=_cp(1),
    )(w2, g2, m2, v2)
    return d.reshape(shape), nm.reshape(shape), nv.reshape(shape)


def _place():
    x, y, c = lax.axis_index("x"), lax.axis_index("y"), lax.axis_index("c")
    chips = [(1 - x, y), (x, 1 - y), (1 - x, 1 - y)]
    return x, y, c, chips


def all_gather_small(blk, name):
    m_per, n = blk.shape

    def body(x_ref, out_ref, send_sems, recv_sems, local_sem):
        x, y, c, chips = _place()
        me, sibling = (x, y, c), (x, y, 1 - c)

        def rows(px, py, pc):
            return out_ref.at[pl.ds((4 * px + 2 * py + pc) * m_per, m_per), :]

        def copy(k, block, to, src=None):
            return pltpu.make_async_remote_copy(
                src_ref=rows(*block) if src is None else src, dst_ref=rows(*block),
                send_sem=send_sems.at[k], recv_sem=recv_sems.at[k], device_id=to, device_id_type=MESH)

        mine = pltpu.make_async_copy(x_ref, rows(*me), local_sem)
        mine.start()
        first = [copy(0, me, sibling, src=x_ref)]
        first += [copy(1 + j, me, (*chip, c), src=x_ref) for j, chip in enumerate(chips)]
        for cp in first:
            cp.start()
        passed = [copy(4 + j, (*chip, c), sibling) for j, chip in enumerate(chips)]
        for j, chip in enumerate(chips):
            copy(1 + j, (*chip, c), me).wait_recv()
            passed[j].start()
        copy(0, sibling, me).wait_recv()
        for j, chip in enumerate(chips):
            copy(4 + j, (*chip, 1 - c), me).wait_recv()
        for cp in first + passed:
            cp.wait_send()
        mine.wait()

    return pl.pallas_call(
        body, name=name, out_shape=SDS((N_DEV * m_per, n), blk.dtype),
        in_specs=[pl.BlockSpec(memory_space=pltpu.VMEM)], out_specs=pl.BlockSpec(memory_space=pltpu.VMEM),
        scratch_shapes=[pltpu.SemaphoreType.DMA((7,)), pltpu.SemaphoreType.DMA((7,)), pltpu.SemaphoreType.DMA],
    )(blk)


def _half(cc, rh):
    return pl.ds(pl.multiple_of(cc * rh, 16), rh)


def all_gather_shards(ws):
    n = len(ws)

    def body(*refs):
        w_refs, out_refs = refs[:n], refs[n:2 * n]
        send_sems, recv_sems, local_sems, own_sems = refs[2 * n:]
        x, y, c, chips = _place()
        sibling = (x, y, 1 - c)
        s_me = 2 * x + y

        def copy(k, src, dst, to):
            return pltpu.make_async_remote_copy(src_ref=src, dst_ref=dst, send_sem=send_sems.at[k], recv_sem=recv_sems.at[k],
                                                device_id=to, device_id_type=MESH)

        local, sends, passed = [], [], []
        for k in range(n):
            rh = ws[k].shape[0] // 2
            cp = pltpu.make_async_remote_copy(src_ref=w_refs[k], dst_ref=out_refs[k].at[s_me], send_sem=local_sems.at[k],
                                              recv_sem=own_sems.at[k], device_id=sibling, device_id_type=MESH)
            cp.start()
            local.append(cp)
            for j, chip in enumerate(chips):
                sd = copy(6 * k + j, w_refs[k].at[_half(c, rh)], out_refs[k].at[s_me, _half(c, rh)], (*chip, c))
                sd.start()
                sends.append(sd)
        for k in range(n):
            rh = ws[k].shape[0] // 2
            for j, (px, py) in enumerate(chips):
                got = out_refs[k].at[2 * px + py, _half(c, rh)]
                copy(6 * k + j, got, got, (px, py, c)).wait_recv()
                fw = copy(6 * k + 3 + j, got, got, sibling)
                fw.start()
                passed.append(fw)
        for k in range(n):
            rh = ws[k].shape[0] // 2
            for j, (px, py) in enumerate(chips):
                got = out_refs[k].at[2 * px + py, _half(1 - c, rh)]
                copy(6 * k + 3 + j, got, got, sibling).wait_recv()
        for cp in sends + passed:
            cp.wait_send()
        for cp in local:
            cp.wait()

    return pl.pallas_call(
        body, name="weights_all_gather", out_shape=[SDS((N_SHARD,) + w.shape, w.dtype) for w in ws],
        in_specs=[ANY] * n, out_specs=[ANY] * n,
        scratch_shapes=[pltpu.SemaphoreType.DMA((6 * n,)), pltpu.SemaphoreType.DMA((6 * n,)), pltpu.SemaphoreType.DMA((n,)),
                        pltpu.SemaphoreType.DMA((n,))],
    )(*ws)


def sibling_exchange(sends, name):
    n = len(sends)

    def body(*refs):
        s_refs, o_refs, send_sems, recv_sems = refs[:n], refs[n:2 * n], refs[2 * n], refs[2 * n + 1]
        x, y, c, _ = _place()
        cps = [pltpu.make_async_remote_copy(src_ref=s_refs[k], dst_ref=o_refs[k], send_sem=send_sems.at[k], recv_sem=recv_sems.at[k],
                                            device_id=(x, y, 1 - c), device_id_type=MESH) for k in range(n)]
        for cp in cps:
            cp.start()
        for cp in cps:
            cp.wait()

    return pl.pallas_call(
        body, name=name, out_shape=[SDS(s.shape, s.dtype) for s in sends], in_specs=[ANY] * n, out_specs=[ANY] * n,
        scratch_shapes=[pltpu.SemaphoreType.DMA((n,)), pltpu.SemaphoreType.DMA((n,))],
    )(*sends)


def add_rows(arrs, out_dtype, name, rt=256):
    Rr, W = arrs[0].shape

    def fn(ids, *vals):
        acc = vals[0]
        for v in vals[1:]:
            acc = acc + v
        return (acc,)

    t = rt if Rr % rt == 0 else max(b for b in range(16, rt + 1, 16) if Rr % b == 0)
    (out,) = rowwise(fn, [Row(a, (t, W), lambda i: (i, 0)) for a in arrs], [],
                     [Out((Rr, W), out_dtype, (t, W), lambda i: (i, 0))], (Rr // t,), name)
    return out


HBM_SPEC = pl.BlockSpec(memory_space=pltpu.HBM)
SEM_SPEC = pl.BlockSpec(memory_space=pltpu.SEMAPHORE)
DATAFLOW = pltpu.SideEffectType.DATAFLOW_SIDE_EFFECTING


def _in_hbm(a):
    return pltpu.with_memory_space_constraint(a, pltpu.HBM)


def _gather_copies(w_refs, land_refs, send_sems, recv_sems):
    x, y, c, chips = _place()
    targets = [(x, y, 1 - c)] + [(*chip, c) for chip in chips]
    cps = []
    for k, (w_ref, land_ref) in enumerate(zip(w_refs, land_refs)):
        for j, to in enumerate(targets):
            cps.append(pltpu.make_async_remote_copy(src_ref=w_ref, dst_ref=land_ref.at[2 * x + y], send_sem=send_sems.at[4 * k + j],
                                                    recv_sem=recv_sems.at[4 * k + j], device_id=to, device_id_type=MESH))
    return cps


def _scatter_copies(p_refs, land_refs, send_sems, recv_sems):
    x, y, c, chips = _place()
    cps = []
    for k, (p_ref, land_ref) in enumerate(zip(p_refs, land_refs)):
        for j, (px, py) in enumerate(chips):
            cps.append(pltpu.make_async_remote_copy(src_ref=p_ref.at[2 * px + py], dst_ref=land_ref.at[j], send_sem=send_sems.at[3 * k + j],
                                                    recv_sem=recv_sems.at[3 * k + j], device_id=(px, py, c), device_id_type=MESH))
    return cps


def copies_start(srcs, land_shapes, make_copies, per_src, name):
    n = len(srcs)
    m = per_src * n

    def body(*refs):
        src_refs, land_refs = refs[:n], refs[n:2 * n]
        send_sems, recv_sems, token = refs[2 * n], refs[2 * n + 1], refs[-1]
        for cp in make_copies(src_refs, land_refs, send_sems, recv_sems):
            cp.start()
        token[...] = jnp.zeros_like(token)

    lands = [lax.empty(shp, s.dtype) for shp, s in zip(land_shapes, srcs)]
    res = pl.pallas_call(
        body, name=name,
        out_shape=(pltpu.SemaphoreType.DMA((m,)), pltpu.SemaphoreType.DMA((m,)), *[pltpu.HBM(s.shape, s.dtype) for s in srcs],
                   *[pltpu.HBM(shp, s.dtype) for shp, s in zip(land_shapes, srcs)], SDS((8, LANES), F32)),
        in_specs=[HBM_SPEC] * (2 * n),
        out_specs=(SEM_SPEC, SEM_SPEC, *[HBM_SPEC] * (2 * n), pl.BlockSpec(memory_space=pltpu.VMEM)),
        input_output_aliases={i: 2 + i for i in range(2 * n)},
        compiler_params=pltpu.CompilerParams(has_side_effects=DATAFLOW),
    )(*[_in_hbm(s) for s in srcs], *[_in_hbm(l) for l in lands])
    return res[0], res[1], list(res[2:2 + n]), list(res[2 + n:2 + 2 * n]), res[-1]


def copies_wait(send_sems, recv_sems, srcs, lands, make_copies, after, name):
    n = len(srcs)

    def body(*refs):
        src_refs, land_refs = refs[:n], refs[n:2 * n]
        for cp in make_copies(src_refs, land_refs, refs[2 * n], refs[2 * n + 1]):
            cp.wait_send()
            cp.wait_recv()

    res = pl.pallas_call(
        body, name=name,
        out_shape=(*[pltpu.HBM(s.shape, s.dtype) for s in srcs], *[pltpu.HBM(l.shape, l.dtype) for l in lands]),
        in_specs=[HBM_SPEC] * (2 * n) + [SEM_SPEC, SEM_SPEC, ANY],
        out_specs=tuple([HBM_SPEC] * (2 * n)),
        input_output_aliases={i: i for i in range(2 * n)},
        compiler_params=pltpu.CompilerParams(has_side_effects=DATAFLOW),
    )(*srcs, *lands, send_sems, recv_sems, after)
    return list(res[n:])


def _pad_lanes(v):
    return jnp.concatenate([v.astype(F32), jnp.zeros((LANES - v.shape[0],), F32)])[None]


def kernel(x, c, w_ada, b_ada, norm_mix, norm_ffn, w_ffn_in, w_ffn_out, gdn_w_in, gdn_conv, gdn_a_log, gdn_dt_bias, gdn_out_norm, gdn_w_out, dsw_w_in, dsw_q_norm, dsw_k_norm, dsw_w_out, rel_bias, loss_target, m_w_ada, m_b_ada, m_norm_mix, m_norm_ffn, m_w_ffn_in, m_w_ffn_out, m_gdn_w_in, m_gdn_conv, m_gdn_a_log, m_gdn_dt_bias, m_gdn_out_norm, m_gdn_w_out, m_dsw_w_in, m_dsw_q_norm, m_dsw_k_norm, m_dsw_w_out, m_rel_bias, v_w_ada, v_b_ada, v_norm_mix, v_norm_ffn, v_w_ffn_in, v_w_ffn_out, v_gdn_w_in, v_gdn_conv, v_gdn_a_log, v_gdn_dt_bias, v_gdn_out_norm, v_gdn_w_out, v_dsw_w_in, v_dsw_q_norm, v_dsw_k_norm, v_dsw_w_out, v_rel_bias):
    S = x.shape[1]
    nt = S // WT
    xi, yi, ci = lax.axis_index("x"), lax.axis_index("y"), lax.axis_index("c")
    me = 4 * xi + 2 * yi + ci
    s_me = 2 * xi + yi
    x0, tgt = x[0], loss_target[0]

    whole = lambda a: Row(a, a.shape, lambda i: (0,) * a.ndim)
    (cond8,) = rowwise(lambda ids, v: (_silu(v),), [whole(c.reshape(8, LANES))], [], [Out((8, LANES), F32, (8, LANES), lambda i: (0, 0))], (1,), "cond")
    cond_all = all_gather_small(cond8, "gather_cond").reshape(N_DEV, D)
    cond16 = jnp.concatenate([cond_all, jnp.zeros((8, D), F32)], axis=0)
    ada_cols = w_ada.shape[2]
    mods = [matmul(cond16, w_ada[l], "nn", F32, f"ada_{l}")[:N_DEV] for l in range(2)]
    buf = jnp.concatenate([jnp.stack(mods, axis=1).reshape(-1, LANES), gdn_conv.reshape(-1, LANES)], axis=0)
    n_mod_rows = N_DEV * 2 * ada_cols // LANES
    got = all_gather_small(buf, "gather_mod").reshape(N_DEV, buf.shape[0], LANES)
    mod_parts, conv_parts = [], []
    for s in range(N_SHARD):
        from_dev = got[2 * s]
        mod_parts.append(lax.dynamic_index_in_dim(from_dev[:n_mod_rows].reshape(N_DEV, 2, ada_cols), me, 0, keepdims=False))
        conv_parts.append(from_dev[n_mod_rows:].reshape(4, -1))
    mod_nb = jnp.concatenate(mod_parts, axis=1)
    conv_w = jnp.concatenate(conv_parts, axis=1)
    (mod,) = rowwise(lambda ids, a, b: (a + b,), [whole(mod_nb), whole(b_ada)], [], [Out(mod_nb.shape, F32, mod_nb.shape, lambda i: (0, 0))], (1,), "mod_bias")
    mod = mod.reshape(2, 6, 1, D)
    sh1, sc1, g1, sh2, sc2, g2 = ([mod[l, k] for l in range(2)] for k in range(6))
    gmix = [norm_mix[l][None] for l in range(2)]
    gffn = [norm_ffn[l][None] for l in range(2)]

    gcols = gdn_w_in.shape[2]
    g_gdn_in, g_gdn_out = all_gather_shards([gdn_w_in[0].astype(BF16), gdn_w_out[0].astype(BF16)])
    gathered = lambda ws: [(N_SHARD,) + w.shape for w in ws]
    gate = (jnp.minimum(jnp.abs(g_gdn_in[0, 0, 0].astype(F32)), 0.0) + jnp.minimum(jnp.abs(mod[0, 0, 0, 0]), 0.0)).astype(BF16)
    w2 = [w_ffn_in[0].astype(BF16) + gate, w_ffn_out[0].astype(BF16) + gate]
    w3 = [dsw_w_in[0].astype(BF16) + gate, dsw_w_out[0].astype(BF16) + gate, w_ffn_in[1].astype(BF16) + gate, w_ffn_out[1].astype(BF16) + gate]
    fly2 = copies_start(w2, gathered(w2), _gather_copies, 4, "weights_ffn0_start")
    fly3 = copies_start(w3, gathered(w3), _gather_copies, 4, "weights_layer1_start")
    started = fly2[4][0, 0] + fly3[4][0, 0]
    w_gdn = jnp.concatenate([g_gdn_in[s] for s in range(N_SHARD)] + [jnp.zeros((D, GDN_PROJ - N_SHARD * gcols), BF16)], axis=1)
    alog, dtb = _pad_lanes(gdn_a_log[0]), _pad_lanes(gdn_dt_bias[0])
    qg2 = jnp.concatenate([dsw_q_norm, dsw_q_norm], axis=1)
    kg2 = jnp.concatenate([dsw_k_norm, dsw_k_norm], axis=1)
    w_gdn_out = g_gdn_out.reshape(GDN_H * LANES, D)
    gdn_args = (w_gdn, conv_w, alog, dtb, gdn_out_norm, w_gdn_out)
    sc1[0] = sc1[0] + started

    (h10,) = rowwise(f_norm_only, [_wide(x0)], [gmix[0], sc1[0], sh1[0]], [_wide_out(S, BF16)], (nt,), "l0_norm")
    y0, sv_g = gdn_forward(h10, *gdn_args)
    x1, h20 = rowwise(f_resid_norm, [_wide(x0), _wide(y0)], [g1[0], gffn[0], sc2[0], sh2[0]], [_wide_out(S, F32), _wide_out(S, BF16)], (nt,), "l0_mid")
    g_in0, g_out0 = copies_wait(*fly2[:4], _gather_copies, y0, "weights_ffn0_wait")
    w_ffn = [(g_in0, g_out0.reshape(FFN, D)), None]
    f0, sv_f0 = ffn_forward(h20, *w_ffn[0], "0")
    x2, h11 = rowwise(f_resid_norm, [_wide(x1), _wide(f0)], [g2[0], gmix[1], sc1[1], sh1[1]], [_wide_out(S, F32), _wide_out(S, BF16)], (nt,), "l1_in")
    g_dsw_in, g_dsw_out, g_in1, g_out1 = copies_wait(*fly3[:4], _gather_copies, f0, "weights_layer1_wait")
    w_ffn[1] = (g_in1, g_out1.reshape(FFN, D))
    dsw_args = (g_dsw_in, qg2, kg2)
    y1, sv_d = dsw_forward(h11, *dsw_args, rel_bias, g_dsw_out)
    x3, h21 = rowwise(f_resid_norm, [_wide(x2), _wide(y1)], [g1[1], gffn[1], sc2[1], sh2[1]], [_wide_out(S, F32), _wide_out(S, BF16)], (nt,), "l1_mid")
    f1, sv_f1 = ffn_forward(h21, *w_ffn[1], "1")
    parts, dx3, df1, dg2_1 = loss_and_grad(x3, f1, tgt, g2[1], S)
    loss = lax.psum(jnp.sum(parts), ("x", "y", "c"))

    dh21, d_win1, d_wout1 = ffn_backward(df1, sv_f1, *w_ffn[1], "1")
    (dx2, dy1), (dg1_1, dgf1, dsc2_1, dsh2_1) = rowwise_bwd(
        f_resid_norm, [_wide(x2), _wide(y1, gdtype=BF16)], [g1[1], gffn[1], sc2[1], sh2[1]], [_wide(dx3), _wide(dh21)], (nt,), "l1_mid_bwd")
    dh11, g_d = dsw_backward(dy1, sv_d, *dsw_args, g_dsw_out)
    (dx1, df0), (dg2_0, dgm1, dsc1_1, dsh1_1) = rowwise_bwd(
        f_resid_norm, [_wide(x1), _wide(f0, gdtype=BF16)], [g2[0], gmix[1], sc1[1], sh1[1]], [_wide(dx2), _wide(dh11)], (nt,), "l1_in_bwd")
    by_shard = lambda a: a.reshape(N_SHARD, a.shape[0] // N_SHARD, a.shape[1])
    landing = lambda ps: [(3,) + p.shape[1:] for p in ps]
    dws3 = [g_d["w_in"], g_d["w_out"], d_win1, by_shard(d_wout1)]
    parts3 = [a.astype(BF16) for a in dws3]
    gfly3 = copies_start(parts3, landing(parts3), _scatter_copies, 3, "grads_layer1_start")
    w_out0 = w_ffn[0][1] + gfly3[4][0, 0].astype(BF16)
    dh20, d_win0, d_wout0 = ffn_backward(df0, sv_f0, w_ffn[0][0], w_out0, "0")
    (dx0p, dy0), (dg1_0, dgf0, dsc2_0, dsh2_0) = rowwise_bwd(
        f_resid_norm, [_wide(x0), _wide(y0, gdtype=BF16)], [g1[0], gffn[0], sc2[0], sh2[0]], [_wide(dx1), _wide(dh20)], (nt,), "l0_mid_bwd")
    dws2 = [d_win0, by_shard(d_wout0)]
    parts2 = [a.astype(BF16) for a in dws2]
    gfly2 = copies_start(parts2, landing(parts2), _scatter_copies, 3, "grads_ffn0_start")
    gdn_args = gdn_args[:5] + (w_gdn_out + gfly2[4][0, 0].astype(BF16),)
    gdn_flight = []

    def start_gdn_grads(d_w_in, d_w_out):
        dws1 = [jnp.stack([d_w_in[:, s * gcols:(s + 1) * gcols] for s in range(N_SHARD)]), by_shard(d_w_out)]
        parts1 = [a.astype(BF16) for a in dws1]
        fly = copies_start(parts1, landing(parts1), _scatter_copies, 3, "grads_gdn_start")
        gdn_flight.extend([dws1, fly])
        return fly[4][0, 0]

    dh10, g_g = gdn_backward(dy0, sv_g, *gdn_args, on_weight_grads=start_gdn_grads)
    dws1, gfly1 = gdn_flight
    (grad_x,), (dgm0, dsc1_0, dsh1_0) = rowwise_bwd(f_first, [_wide(x0)], [gmix[0], sc1[0], sh1[0]], [_wide(dx0p), _wide(dh10)], (nt,), "l0_norm_bwd")

    dmod = jnp.concatenate([dsh1_0, dsc1_0, dg1_0, dsh2_0, dsc2_0, dg2_0, dsh1_1, dsc1_1, dg1_1, dsh2_1, dsc2_1, dg2_1], axis=1)
    d_rel = jnp.transpose(g_d["rel"][:, :, 0])
    fold = lambda v: v[:, :DSW_DH] + v[:, DSW_DH:]
    small = [dmod, jnp.concatenate([dgm0, dgm1], axis=1), jnp.concatenate([dgf0, dgf1], axis=1), g_g["conv"].reshape(1, -1),
             g_g["alog"], g_g["dtb"], g_g["gain"], _pad_lanes(fold(g_d["q_gain2"])[0]), _pad_lanes(fold(g_d["k_gain2"])[0]),
             d_rel.reshape(1, -1)]
    used = [v.shape[1] // LANES for v in small]
    sizes = [-(-u // 8) * 8 for u in used]
    pad8 = lambda v, u, s: jnp.concatenate([v.reshape(u, LANES), jnp.zeros((s - u, LANES), F32)], axis=0) if s > u else v.reshape(u, LANES)
    pad_rows = sum(sizes)
    sbuf = jnp.concatenate([pad8(v, u, s) for v, u, s in zip(small, used, sizes)], axis=0)
    sgot = all_gather_small(sbuf, "gather_small_grads")
    ssum = add_rows([sgot[d * pad_rows:(d + 1) * pad_rows] for d in range(N_DEV)], F32, "sum_small_grads", rt=pad_rows)
    offs = np.cumsum([0] + sizes)
    take = lambda k: ssum[offs[k]:offs[k] + used[k]].reshape(1, -1)
    grad_b_ada = take(0).reshape(2, 6 * D)
    grad_norm_mix = take(1).reshape(2, D)
    grad_norm_ffn = take(2).reshape(2, D)
    conv_full = take(3).reshape(4, -1)
    ncv = gdn_conv.shape[2]
    grad_gdn_conv = lax.dynamic_slice_in_dim(conv_full, s_me * ncv, ncv, axis=1)[None]
    grad_a_log = take(4)[:, :GDN_H]
    grad_dt_bias = take(5)[:, :GDN_H]
    grad_out_norm = take(6)
    grad_q_norm = take(7)[:, :DSW_DH]
    grad_k_norm = take(8)[:, :DSW_DH]
    grad_rel = take(9).reshape(REL_BUCKETS, 3 * GDN_H)
    dmod_all = sgot.reshape(N_DEV, pad_rows, LANES)[:, :used[0]].reshape(N_DEV, 2, 6 * D)
    dmod_mine = lax.dynamic_slice_in_dim(dmod_all, s_me * ada_cols, ada_cols, axis=2)
    dmod16 = jnp.concatenate([dmod_mine, jnp.zeros_like(dmod_mine)], axis=0)
    grad_w_ada = jnp.stack([matmul(cond16, dmod16[:, l], "tn", F32, f"ada_dw_{l}") for l in range(2)])

    got3 = copies_wait(*gfly3[:4], _scatter_copies, grad_x, "grads_layer1_wait")
    got2 = copies_wait(*gfly2[:4], _scatter_copies, grad_x, "grads_ffn0_wait")
    got1 = copies_wait(*gfly1[:4], _scatter_copies, grad_x, "grads_gdn_wait")
    core_sums = []
    for i, (full, got) in enumerate(zip(dws3 + dws2 + dws1, got3 + got2 + got1)):
        own = lax.dynamic_index_in_dim(full, s_me, 0, keepdims=False)
        core_sums.append(add_rows([own, got[0], got[1], got[2]], F32, f"grads_core_sum_{i}"))
    sib_sums = sibling_exchange(core_sums, "grads_core_sums_swap")
    s_dsw_in, s_dsw_out, s_in1, s_out1, s_in0, s_out0, s_gdn_in, s_gdn_out = [
        add_rows([a, b], F32, f"grads_chip_total_{i}") for i, (a, b) in enumerate(zip(core_sums, sib_sums))]
    gsh = dict(gdn_w_in=s_gdn_in[None], gdn_w_out=s_gdn_out[None],
               w_ffn_in=jnp.stack([s_in0, s_in1]), w_ffn_out=jnp.stack([s_out0, s_out1]),
               dsw_w_in=s_dsw_in[None], dsw_w_out=s_dsw_out[None])

    grads = dict(w_ada=grad_w_ada, b_ada=grad_b_ada, norm_mix=grad_norm_mix, norm_ffn=grad_norm_ffn, w_ffn_in=gsh["w_ffn_in"],
                 w_ffn_out=gsh["w_ffn_out"], gdn_w_in=gsh["gdn_w_in"], gdn_conv=grad_gdn_conv, gdn_a_log=grad_a_log,
                 gdn_dt_bias=grad_dt_bias, gdn_out_norm=grad_out_norm, gdn_w_out=gsh["gdn_w_out"], dsw_w_in=gsh["dsw_w_in"],
                 dsw_q_norm=grad_q_norm, dsw_k_norm=grad_k_norm, dsw_w_out=gsh["dsw_w_out"], rel_bias=grad_rel)
    weights = dict(w_ada=w_ada, b_ada=b_ada, norm_mix=norm_mix, norm_ffn=norm_ffn, w_ffn_in=w_ffn_in, w_ffn_out=w_ffn_out,
                   gdn_w_in=gdn_w_in, gdn_conv=gdn_conv, gdn_a_log=gdn_a_log, gdn_dt_bias=gdn_dt_bias, gdn_out_norm=gdn_out_norm,
                   gdn_w_out=gdn_w_out, dsw_w_in=dsw_w_in, dsw_q_norm=dsw_q_norm, dsw_k_norm=dsw_k_norm, dsw_w_out=dsw_w_out,
                   rel_bias=rel_bias)
    ms = dict(w_ada=m_w_ada, b_ada=m_b_ada, norm_mix=m_norm_mix, norm_ffn=m_norm_ffn, w_ffn_in=m_w_ffn_in, w_ffn_out=m_w_ffn_out,
              gdn_w_in=m_gdn_w_in, gdn_conv=m_gdn_conv, gdn_a_log=m_gdn_a_log, gdn_dt_bias=m_gdn_dt_bias, gdn_out_norm=m_gdn_out_norm,
              gdn_w_out=m_gdn_w_out, dsw_w_in=m_dsw_w_in, dsw_q_norm=m_dsw_q_norm, dsw_k_norm=m_dsw_k_norm, dsw_w_out=m_dsw_w_out,
              rel_bias=m_rel_bias)
    vs = dict(w_ada=v_w_ada, b_ada=v_b_ada, norm_mix=v_norm_mix, norm_ffn=v_norm_ffn, w_ffn_in=v_w_ffn_in, w_ffn_out=v_w_ffn_out,
              gdn_w_in=v_gdn_w_in, gdn_conv=v_gdn_conv, gdn_a_log=v_gdn_a_log, gdn_dt_bias=v_gdn_dt_bias, gdn_out_norm=v_gdn_out_norm,
              gdn_w_out=v_gdn_w_out, dsw_w_in=v_dsw_w_in, dsw_q_norm=v_dsw_q_norm, dsw_k_norm=v_dsw_k_norm, dsw_w_out=v_dsw_w_out,
              rel_bias=v_rel_bias)
    names = list(weights)
    deltas, new_m, new_v = [], [], []
    for n in names:
        g = grads[n].reshape(weights[n].shape)
        grads[n] = g
        d, nm, nv = adamw(weights[n], g, ms[n], vs[n], f"adamw_{n}")
        deltas.append(d)
        new_m.append(nm)
        new_v.append(nv)
    return (loss, grad_x[None], *[grads[n] for n in names], *deltas, *new_m, *new_v)
```

```python
import functools
import math

import numpy as np
import jax
import jax.numpy as jnp
from jax import lax
from jax.experimental import pallas as pl
from jax.experimental.pallas import tpu as pltpu

F32 = jnp.float32
BF16 = jnp.bfloat16
SDS = jax.ShapeDtypeStruct
MESH = pl.DeviceIdType.MESH
ANY = pl.BlockSpec(memory_space=pl.ANY)

D = 1024
EPS = 1e-6
LANES = 128
GDN_H = 8
GDN_DK = 128
GDN_C = 64
DSW_GROUPS = ((128, 1), (512, 4), (2048, 16))
DSW_SPAN = 128
DSW_DH = 64
DSW_HG = 512
REL_BUCKETS = 32
REL_MAX_DIST = 2048
FFN = 2816
N_SHARD = 4
N_DEV = 8
VMEM_LIMIT = 48 * 1024 * 1024
NEG = -1e30

ADAM_LR, ADAM_B1, ADAM_B2, ADAM_EPS, ADAM_WD, ADAM_STEP = 0.001, 0.9, 0.999, 1e-08, 0.01, 10


def _cp(n_axes):
    return pltpu.CompilerParams(dimension_semantics=("arbitrary",) * n_axes, vmem_limit_bytes=VMEM_LIMIT)


def _blk(dim, cap):
    if dim <= cap:
        return dim
    best = None
    for b in range(LANES, cap + 1, LANES):
        if dim % b == 0:
            best = b
    assert best is not None, (dim, cap)
    return best


MAX_SHARD_BLOCK = 1408
def matmul(a, b, mode, out_dtype, name, cap_m=MAX_SHARD_BLOCK, cap_n=MAX_SHARD_BLOCK, cap_k=2048, col_shards=0):
    ns = col_shards
    if mode == "nn":
        (M, K) = a.shape
        K2, N = (b.shape[1], ns * b.shape[2]) if ns else b.shape
    elif mode == "nt":
        (M, K) = a.shape
        N, K2 = (b.shape[1], ns * b.shape[2]) if ns else b.shape
    else:
        (K, M), (K2, N) = a.shape, b.shape
    assert K == K2, (a.shape, b.shape, mode)
    if K <= 3072:
        cap_k = K
        if K > 2048:
            cap_n = 1024
    n_unit = N // ns if (ns and mode != "nt") else N
    k_unit = K // ns if (ns and mode == "nt") else K
    bm = _blk(M, cap_m)
    bn = _blk(n_unit, MAX_SHARD_BLOCK) if n_unit != N else _blk(N, cap_n)
    if k_unit != K:
        bk = _blk(k_unit, MAX_SHARD_BLOCK)
    else:
        bk = _blk(K, 1024 if (ns and mode == "tn") else cap_k)
    nk = K // bk
    nps, kps = n_unit // bn, k_unit // bk
    dims = {"nn": ((1,), (0,)), "nt": ((1,), (1,)), "tn": ((0,), (0,))}[mode]

    def dot(a_ref, b_ref):
        return lax.dot_general(a_ref[...].astype(BF16), b_ref[...].astype(BF16), (dims, ((), ())), preferred_element_type=F32)

    def body_one(a_ref, b_ref, o_ref):
        o_ref[...] = dot(a_ref, b_ref).astype(o_ref.dtype)

    def body_acc(a_ref, b_ref, o_ref, acc_ref):
        k = pl.program_id(2)

        @pl.when(k == 0)
        def _():
            acc_ref[...] = jnp.zeros_like(acc_ref)

        acc_ref[...] += dot(a_ref, b_ref)

        @pl.when(k == nk - 1)
        def _():
            o_ref[...] = acc_ref[...].astype(o_ref.dtype)

    a_spec = pl.BlockSpec((bk, bm), lambda i, j, k: (k, i)) if mode == "tn" else pl.BlockSpec((bm, bk), lambda i, j, k: (i, k))
    if mode == "nt":
        b_spec = pl.BlockSpec((None, bn, bk), lambda i, j, k: (k // kps, j, k % kps)) if ns else pl.BlockSpec((bn, bk), lambda i, j, k: (j, k))
    elif mode == "nn" and ns:
        b_spec = pl.BlockSpec((None, bk, bn), lambda i, j, k: (j // nps, k, j % nps))
    else:
        b_spec = pl.BlockSpec((bk, bn), lambda i, j, k: (k, j))
    if mode == "tn" and ns:
        o_spec, o_shape = pl.BlockSpec((None, bm, bn), lambda i, j, k: (j // nps, i, j % nps)), (ns, M, n_unit)
    else:
        o_spec, o_shape = pl.BlockSpec((bm, bn), lambda i, j, k: (i, j)), (M, N)
    return pl.pallas_call(
        body_one if nk == 1 else body_acc, name=name, grid=(M // bm, N // bn, nk),
        in_specs=[a_spec, b_spec], out_specs=o_spec,
        out_shape=SDS(o_shape, out_dtype), scratch_shapes=[] if nk == 1 else [pltpu.VMEM((bm, bn), F32)],
        compiler_params=_cp(3),
    )(a, b)


class Row:
    def __init__(self, arr, bshape, imap, splits=None, diff=True, acc=False, gdtype=F32, gshape=None, gbshape=None, gimap=None,
                 lead=0):
        self.arr, self.bshape, self.imap = arr, tuple(bshape), imap
        self.splits, self.lead = splits, lead
        self.diff, self.acc, self.gdtype = diff, acc, gdtype
        self.gshape = tuple(arr.shape) if gshape is None else tuple(gshape)
        self.gbshape = self.bshape if gbshape is None else tuple(gbshape)
        self.gimap = imap if gimap is None else gimap

    def gspec(self):
        return pl.BlockSpec(self.gbshape, self.gimap)

    def spec(self):
        return pl.BlockSpec(self.bshape, self.imap)

    def pieces(self, ref):
        return _load_pieces(ref, self.splits, self.lead)

    def n_pieces(self):
        return _n_pieces(self.splits, self.lead)


class Out:
    def __init__(self, shape, dtype, bshape, imap, splits=None, lead=0):
        self.shape, self.dtype, self.bshape, self.imap = tuple(shape), dtype, tuple(bshape), imap
        self.splits, self.lead = splits, lead

    def n_pieces(self):
        return _n_pieces(self.splits, self.lead)


def _n_pieces(splits, lead):
    return lead if lead else (1 if splits is None else len(splits))


def _load_pieces(ref, splits, lead):
    if lead:
        return [ref[k].astype(F32) for k in range(lead)]
    if splits is None:
        return [ref[...].astype(F32)]
    out, o = [], 0
    for w in splits:
        out.append(ref[..., o:o + w].astype(F32))
        o += w
    return out


def _store_pieces(ref, splits, lead, vals, accumulate=False):
    def put(idx, v):
        if accumulate:
            ref[idx] += v.astype(ref.dtype)
        else:
            ref[idx] = v.astype(ref.dtype)

    if lead:
        for k in range(lead):
            put(k, vals[k])
    elif splits is None:
        put(..., vals[0])
    else:
        o = 0
        for w, v in zip(splits, vals):
            put((..., slice(o, o + w)), v)
            o += w


def rowwise(fn, rows, params, outs, grid, name):
    nr, npar = len(rows), len(params)

    def body(*refs):
        ids = tuple(pl.program_id(a) for a in range(len(grid)))
        vals = []
        for r, ref in zip(rows, refs[:nr]):
            vals += r.pieces(ref)
        pvals = [ref[...].astype(F32) for ref in refs[nr:nr + npar]]
        res = list(fn(ids, *vals, *pvals))
        o = 0
        for spec, ref in zip(outs, refs[nr + npar:]):
            n = spec.n_pieces()
            _store_pieces(ref, spec.splits, spec.lead, res[o:o + n])
            o += n

    nz = len(grid)
    pspecs = [pl.BlockSpec(p.shape, (lambda *ids, _n=p.ndim: (0,) * _n)) for p in params]
    res = pl.pallas_call(
        body, name=name, grid=grid,
        in_specs=[r.spec() for r in rows] + pspecs,
        out_specs=[pl.BlockSpec(o.bshape, o.imap) for o in outs],
        out_shape=[SDS(o.shape, o.dtype) for o in outs],
        compiler_params=_cp(nz),
    )(*[r.arr for r in rows], *params)
    return list(res)


def rowwise_bwd(fn, rows, params, cots, grid, name):
    nr, npar, nc = len(rows), len(params), len(cots)
    drows = [r for r in rows if r.diff]
    nz = len(grid)

    def body(*refs):
        ids = tuple(pl.program_id(a) for a in range(nz))
        row_refs, par_refs = refs[:nr], refs[nr:nr + npar]
        cot_refs = refs[nr + npar:nr + npar + nc]
        drow_refs = refs[nr + npar + nc:nr + npar + nc + len(drows)]
        dpar_refs = refs[nr + npar + nc + len(drows):]
        pieces, is_diff = [], []
        for r, ref in zip(rows, row_refs):
            ps = r.pieces(ref)
            pieces += ps
            is_diff += [r.diff] * len(ps)
        pvals = [ref[...].astype(F32) for ref in par_refs]
        dvals = [p for p, dflag in zip(pieces, is_diff) if dflag]
        nd = len(dvals)

        def f(*args):
            it = iter(args[:nd])
            full = [next(it) if dflag else p for p, dflag in zip(pieces, is_diff)]
            return tuple(fn(ids, *full, *args[nd:]))

        _, vjp = jax.vjp(f, *dvals, *pvals)
        cvals = []
        for c, ref in zip(cots, cot_refs):
            cvals += c.pieces(ref)
        g = vjp(tuple(cvals))
        o = 0
        first_inner = ids[-1] == 0
        for r, ref in zip(drows, drow_refs):
            n = r.n_pieces()
            gs = g[o:o + n]
            o += n
            if r.acc:
                @pl.when(first_inner)
                def _(ref=ref):
                    ref[...] = jnp.zeros_like(ref)
            _store_pieces(ref, r.splits, r.lead, gs, accumulate=r.acc)
        first = functools.reduce(jnp.logical_and, [i == 0 for i in ids])
        for ref, gp in zip(dpar_refs, g[nd:]):
            @pl.when(first)
            def _(ref=ref):
                ref[...] = jnp.zeros_like(ref)
            ref[...] += gp

    pspecs = [pl.BlockSpec(p.shape, (lambda *ids, _n=p.ndim: (0,) * _n)) for p in params]
    res = pl.pallas_call(
        body, name=name, grid=grid,
        in_specs=[r.spec() for r in rows] + pspecs + [c.spec() for c in cots],
        out_specs=[r.gspec() for r in drows] + pspecs,
        out_shape=[SDS(r.gshape, r.gdtype) for r in drows] + [SDS(p.shape, F32) for p in params],
        compiler_params=_cp(nz),
    )(*[r.arr for r in rows], *params, *[c.arr for c in cots])
    res = list(res)
    return res[:len(drows)], res[len(drows):]


def _sigmoid(x):
    return 0.5 * (jnp.tanh(0.5 * x) + 1.0)


def _silu(x):
    return x * _sigmoid(x)


def _normmod(x, gain, sc, sh):
    inv = lax.rsqrt(jnp.mean(x * x, axis=-1, keepdims=True) + EPS)
    return x * inv * gain * (1.0 + sc) + sh


def f_first(ids, x, gain, sc, sh):
    return x, _normmod(x, gain, sc, sh)


def f_resid_norm(ids, x, y, g, gain, sc, sh):
    xn = x + g * y
    return xn, _normmod(xn, gain, sc, sh)


@jax.custom_vjp
def _swiglu(gate, up):
    return _silu(gate) * up


def _swiglu_fwd(gate, up):
    return _silu(gate) * up, (gate, up)


def _swiglu_bwd(res, da):
    gate, up = res
    s = _sigmoid(gate)
    gs = gate * s
    return da * up * (s + gs * (1.0 - s)), da * gs


_swiglu.defvjp(_swiglu_fwd, _swiglu_bwd)


def loss_and_grad(x, y, tgt, g, S):
    nt = S // WT

    def body(x_ref, y_ref, t_ref, g_ref, part_ref, dx_ref, dy_ref, dg_ref):
        @pl.when(pl.program_id(0) == 0)
        def _():
            dg_ref[...] = jnp.zeros_like(dg_ref)

        yv = y_ref[...].astype(F32)
        gg = g_ref[...]
        e = x_ref[...] + gg * yv - t_ref[...]
        part_ref[...] = 0.5 * jnp.sum(e * e, axis=0, keepdims=True) * (1.0 / D)
        d = e * (1.0 / D)
        dx_ref[...] = d
        dy_ref[...] = (d * gg).astype(dy_ref.dtype)
        dg_ref[...] += jnp.sum(d * yv, axis=0, keepdims=True)

    row = pl.BlockSpec((WT, D), lambda i: (i, 0))
    vec = pl.BlockSpec((1, D), lambda i: (0, 0))
    return pl.pallas_call(
        body, name="loss_and_grad", grid=(nt,), in_specs=[row, row, row, vec],
        out_specs=[pl.BlockSpec((None, 1, D), lambda i: (i, 0, 0)), row, row, vec],
        out_shape=[SDS((nt, 1, D), F32), SDS((S, D), F32), SDS((S, D), BF16), SDS((1, D), F32)],
        compiler_params=_cp(1),
    )(x, y, tgt, g)


def _softplus(x):
    return jnp.maximum(x, 0.0) + jnp.log(1.0 + jnp.exp(-jnp.abs(x)))


def _chunk_tril(T):
    r = lax.broadcasted_iota(jnp.int32, (T, T), 0)
    c = lax.broadcasted_iota(jnp.int32, (T, T), 1)
    return jnp.where((r // GDN_C == c // GDN_C) & (c <= r), 1.0, 0.0).astype(F32)


def _dot_hi(a, b, dims=((1,), (0,))):
    return lax.dot_general(a, b, (dims, ((), ())), precision=lax.Precision.HIGHEST, preferred_element_type=F32)


def _dot_x3(a, b, dims=((1,), (0,))):
    return lax.dot_general(a, b, (dims, ((), ())), precision=lax.Precision.HIGH, preferred_element_type=F32)


def f_gdn_gates(ids, ab, alog, dtb):
    T = ab.shape[0]
    g = -jnp.exp(alog) * _softplus(ab + dtb)
    beta = _sigmoid(ab)
    gcum = _dot_x3(_chunk_tril(T), g)
    row = lax.broadcasted_iota(jnp.int32, (LANES, LANES), 0)
    sel = lambda k: jnp.where(row == k, 1.0, 0.0).astype(F32)
    gcs = [_dot_x3(gcum, sel(h)) for h in range(GDN_H)]
    bts = [_dot_x3(beta, sel(GDN_H + h)) for h in range(GDN_H)]
    return (*gcs, *bts)


def f_gdn_post(ids, *args):
    os_, zs, gain = args[:GDN_H], args[GDN_H:2 * GDN_H], args[2 * GDN_H]
    out = []
    for o, z in zip(os_, zs):
        inv = lax.rsqrt(jnp.mean(o * o, axis=-1, keepdims=True) + EPS)
        out.append(o * inv * gain * _silu(z))
    return tuple(out)


def _qknorm1(x, gain2, scale):
    lane = lax.broadcasted_iota(jnp.int32, x.shape, 1)
    lo = lane < DSW_DH
    x2 = x * x
    s_all = jnp.sum(x2, axis=-1, keepdims=True)
    s_lo = jnp.sum(jnp.where(lo, x2, 0.0), axis=-1, keepdims=True)
    ms = jnp.where(lo, s_lo, s_all - s_lo) * (1.0 / DSW_DH)
    return x * lax.rsqrt(ms + EPS) * (gain2 * scale)


def f_combine(ids, o0, o1, o2, l0, l1, l2):
    m = jnp.maximum(jnp.maximum(l0, l1), l2)
    e0, e1, e2 = jnp.exp(l0 - m), jnp.exp(l1 - m), jnp.exp(l2 - m)
    den = e0 + e1 + e2
    o = (e0 * o0 + e1 * o1 + e2 * o2) / den
    return o, m + jnp.log(den)


GDN_T = 512
HALO = 16


def _conv_pre(xx, w):
    acc = xx * w[3:4, :]
    for j in range(3):
        acc = acc + pltpu.roll(xx, shift=3 - j, axis=0) * w[j:j + 1, :]
    return acc


@jax.custom_vjp
def _qkv_act_core(pre, norm_on, scale):
    s = _silu(pre)
    r = lax.rsqrt(jnp.sum(s * s, axis=-1, keepdims=True) + EPS)
    return jnp.where(norm_on > 0.5, s * r * scale, s)


def _qkv_act_fwd(pre, norm_on, scale):
    return _qkv_act_core(pre, norm_on, scale), (pre, norm_on, scale)


def _qkv_act_bwd(res, dout):
    pre, norm_on, scale = res
    sig = _sigmoid(pre)
    s = pre * sig
    r = lax.rsqrt(jnp.sum(s * s, axis=-1, keepdims=True) + EPS)
    unit = s * r
    dn = dout * scale
    ds = jnp.where(norm_on > 0.5, r * (dn - unit * jnp.sum(dn * unit, axis=-1, keepdims=True)), dout)
    return ds * (sig + s * (1.0 - sig)), jnp.zeros_like(norm_on), jnp.zeros_like(scale)


_qkv_act_core.defvjp(_qkv_act_fwd, _qkv_act_bwd)


def _qkv_act(pre, cidx):
    norm_on = jnp.where(cidx < 2 * GDN_H, 1.0, 0.0).astype(F32)
    scale = jnp.where(cidx < GDN_H, GDN_DK ** -0.5, 1.0).astype(F32)
    return _qkv_act_core(pre, norm_on, scale)


def gdn_pre(proj, conv_w, S):
    nt = S // GDN_T
    hb = GDN_T // HALO

    def body(prev_ref, cur_ref, w_ref, o_ref):
        p, i = pl.program_id(0), pl.program_id(1)
        for h in range(GDN_H):
            cols = slice(LANES * h, LANES * (h + 1))
            prev = jnp.where(i > 0, prev_ref[:, cols].astype(F32), 0.0)
            xx = jnp.concatenate([prev, cur_ref[:, cols].astype(F32)], axis=0)
            pre = _conv_pre(xx, w_ref[:, cols])[HALO:]
            o_ref[h] = _qkv_act(pre, p * GDN_H + h).astype(o_ref.dtype)

    hv = GDN_H * LANES
    return pl.pallas_call(
        body, name="gdn_pre", grid=(3, nt),
        in_specs=[pl.BlockSpec((HALO, hv), lambda p, i: (jnp.maximum(i * hb - 1, 0), p)),
                  pl.BlockSpec((GDN_T, hv), lambda p, i: (i, p)),
                  pl.BlockSpec((4, hv), lambda p, i: (0, p))],
        out_specs=pl.BlockSpec((None, GDN_H, GDN_T, LANES), lambda p, i: (p, 0, i, 0)),
        out_shape=SDS((3, GDN_H, S, LANES), BF16),
        compiler_params=_cp(2),
    )(proj, proj, conv_w)


def gdn_pre_bwd(proj, conv_w, dqkv, S):
    nt = S // GDN_T
    hb = GDN_T // HALO
    last_h = S // HALO - 1

    def body(prev_ref, cur_ref, next_ref, w_ref, d_ref, dnext_ref, dx_ref, dw_ref):
        p, i = pl.program_id(0), pl.program_id(1)

        @pl.when(i == 0)
        def _():
            dw_ref[...] = jnp.zeros_like(dw_ref)

        for h in range(GDN_H):
            cols = slice(LANES * h, LANES * (h + 1))
            w = w_ref[:, cols]
            prev = jnp.where(i > 0, prev_ref[:, cols].astype(F32), 0.0)
            xx = jnp.concatenate([prev, cur_ref[:, cols].astype(F32), next_ref[:, cols].astype(F32)], axis=0)
            dnext = jnp.where(i < nt - 1, dnext_ref[h], 0.0)
            dd = jnp.concatenate([jnp.zeros((HALO, LANES), F32), d_ref[h], dnext], axis=0)
            pre = _conv_pre(xx, w)
            _, vjp = jax.vjp(lambda v, _c=p * GDN_H + h: _qkv_act(v, _c), pre)
            (dpre,) = vjp(dd)
            dx = dpre * w[3:4, :]
            R = dpre.shape[0]
            for j in range(3):
                dx = dx + pltpu.roll(dpre, shift=R - (3 - j), axis=0) * w[j:j + 1, :]
            dx_ref[:, cols] = dx[HALO:HALO + GDN_T].astype(dx_ref.dtype)
            own = HALO + GDN_T
            rows_w = [jnp.sum((dpre * pltpu.roll(xx, shift=3 - j, axis=0))[:own], axis=0, keepdims=True) for j in range(3)]
            rows_w.append(jnp.sum((dpre * xx)[:own], axis=0, keepdims=True))
            r4 = lax.broadcasted_iota(jnp.int32, (4, LANES), 0)
            dw = jnp.zeros((4, LANES), F32)
            for j in range(4):
                dw = dw + jnp.where(r4 == j, rows_w[j], 0.0)
            dw_ref[:, cols] += dw

    hv = GDN_H * LANES
    return pl.pallas_call(
        body, name="gdn_pre_bwd", grid=(3, nt),
        in_specs=[pl.BlockSpec((HALO, hv), lambda p, i: (jnp.maximum(i * hb - 1, 0), p)),
                  pl.BlockSpec((GDN_T, hv), lambda p, i: (i, p)),
                  pl.BlockSpec((HALO, hv), lambda p, i: (jnp.minimum((i + 1) * hb, last_h), p)),
                  pl.BlockSpec((4, hv), lambda p, i: (0, p)),
                  pl.BlockSpec((None, GDN_H, GDN_T, LANES), lambda p, i: (p, 0, i, 0)),
                  pl.BlockSpec((None, GDN_H, HALO, LANES), lambda p, i: (p, 0, jnp.minimum((i + 1) * hb, last_h), 0))],
        out_specs=[pl.BlockSpec((GDN_T, hv), lambda p, i: (i, p)),
                   pl.BlockSpec((4, hv), lambda p, i: (0, p))],
        out_shape=[SDS((S, 3 * hv), BF16), SDS((4, 3 * hv), F32)],
        compiler_params=_cp(2),
    )(proj, proj, proj, conv_w, dqkv, dqkv)


_DIMS = {"nn": ((1,), (0,)), "nt": ((1,), (1,)), "tn": ((0,), (0,))}


def _mm_raw(a, b, mode, hi):
    if hi:
        return _dot_hi(a, b, _DIMS[mode])
    return lax.dot_general(a.astype(BF16), b.astype(BF16), (_DIMS[mode], ((), ())), preferred_element_type=F32)


@functools.partial(jax.custom_vjp, nondiff_argnums=(2, 3))
def mm(a, b, mode, hi):
    return _mm_raw(a, b, mode, hi)


def _mm_fwd(a, b, mode, hi):
    return _mm_raw(a, b, mode, hi), (a, b)


def _mm_bwd(mode, hi, res, dc):
    a, b = res
    if mode == "nn":
        da, db = mm(dc, b, "nt", hi), mm(a, dc, "tn", hi)
    elif mode == "nt":
        da, db = mm(dc, b, "nn", hi), mm(dc, a, "tn", hi)
    else:
        da, db = mm(b, dc, "nt", hi), mm(a, dc, "nn", hi)
    return da, db


mm.defvjp(_mm_fwd, _mm_bwd)


TRI_BASE = 8


def _unit_lower_inverses(Ls):
    n = Ls[0].shape[0]
    r = lax.broadcasted_iota(jnp.int32, (n, n), 0)
    c = lax.broadcasted_iota(jnp.int32, (n, n), 1)
    eye = jnp.where(r == c, 1.0, 0.0).astype(F32)
    base = r // TRI_BASE == c // TRI_BASE
    one = lambda a, b_: _mm_raw(a, b_, "nn", False)
    Ps = [jnp.where(base, -L, 0.0) for L in Ls]
    invs = [eye + P for P in Ps]
    k = 1
    while 2 * k < TRI_BASE:
        Ps = [one(P, P) for P in Ps]
        invs = [inv + one(inv, P) for inv, P in zip(invs, Ps)]
        k *= 2
    b = 2 * TRI_BASE
    while b <= n:
        off_mask = (r // b == c // b) & ((r % b) >= b // 2) & ((c % b) < b // 2)
        ts = [one(inv, jnp.where(off_mask, L, 0.0)) for inv, L in zip(invs, Ls)]
        invs = [inv - one(t, inv) for inv, t in zip(invs, ts)]
        b *= 2
    resid = [eye - inv - _dot_x3(L, inv) for inv, L in zip(invs, Ls)]
    return [inv + _dot_x3(inv, rs) for inv, rs in zip(invs, resid)]


@jax.custom_vjp
def tri_apply(invs, Ls, r1s, r2s):
    return [_mm_raw(i, r, "nn", False) for i, r in zip(invs, r1s)], [_mm_raw(i, r, "nn", False) for i, r in zip(invs, r2s)]


def _tri_fwd(invs, Ls, r1s, r2s):
    s1s = [_mm_raw(i, r, "nn", False) for i, r in zip(invs, r1s)]
    s2s = [_mm_raw(i, r, "nn", False) for i, r in zip(invs, r2s)]
    return (s1s, s2s), (invs, s1s, s2s)


def _tri_bwd(res, ds):
    invs, s1s, s2s = res
    d1s = [_mm_raw(i, d, "tn", False) for i, d in zip(invs, ds[0])]
    d2s = [_mm_raw(i, d, "tn", False) for i, d in zip(invs, ds[1])]
    dLs = [-(_mm_raw(d1, s1, "nt", False) + _mm_raw(d2, s2, "nt", False)) for d1, s1, d2, s2 in zip(d1s, s1s, d2s, s2s)]
    return [jnp.zeros_like(i) for i in invs], dLs, d1s, d2s


tri_apply.defvjp(_tri_fwd, _tri_bwd)


def _gdn_chunk(qs, ks, vs, gcbs, btbs, Ss, invs=None):
    C = qs[0].shape[0]
    r = lax.broadcasted_iota(jnp.int32, (C, C), 0)
    c = lax.broadcasted_iota(jnp.int32, (C, C), 1)
    causal, strict = c <= r, c < r
    rows = lax.broadcasted_iota(jnp.int32, gcbs[0].shape, 0)
    Gs = [g[:, :C] for g in gcbs]
    decays = [jnp.exp(jnp.where(causal, G - G.T, NEG)) for G in Gs]
    kbs = [k * b for k, b in zip(ks, btbs)]
    vbs = [v * b for v, b in zip(vs, btbs)]
    Ls = [jnp.where(strict, mm(kb, k, "nt", False) * d, 0.0) for kb, k, d in zip(kbs, ks, decays)]
    egs = [jnp.exp(g) for g in gcbs]
    if invs is None:
        invs = _unit_lower_inverses(Ls)
    us, ws = tri_apply(invs, Ls, vbs, [kb * eg for kb, eg in zip(kbs, egs)])
    qks = [jnp.where(causal, mm(q, k, "nt", False) * d, 0.0) for q, k, d in zip(qs, ks, decays)]
    g_lasts = [jnp.sum(jnp.where(rows == C - 1, g, 0.0), axis=0, keepdims=True) for g in gcbs]
    q_decs = [q * eg for q, eg in zip(qs, egs)]
    k_decs = [k * jnp.exp(gl - g) for k, gl, g in zip(ks, g_lasts, gcbs)]
    v_news = [u - mm(w, S, "nn", False) for u, w, S in zip(us, ws, Ss)]
    os_ = [mm(qd, S, "nn", False) + mm(qk, vn, "nn", False) for qd, S, qk, vn in zip(q_decs, Ss, qks, v_news)]
    S_news = [S * jnp.exp(gl) + mm(kd, vn, "tn", False) for S, gl, kd, vn in zip(Ss, g_lasts, k_decs, v_news)]
    return os_, S_news, invs


INV_CHUNKS = 4
SCAN_CHUNKS = 4


def gdn_inverses(qkv, gc, bt, S):
    nchunk = S // GDN_C
    rows = INV_CHUNKS * GDN_C

    def body(k_ref, g_ref, b_ref, inv_ref):
        items = [(h, m) for m in range(INV_CHUNKS) for h in range(GDN_H)]
        r = lax.broadcasted_iota(jnp.int32, (GDN_C, GDN_C), 0)
        c = lax.broadcasted_iota(jnp.int32, (GDN_C, GDN_C), 1)
        sl = lambda m: slice(m * GDN_C, (m + 1) * GDN_C)
        ks = [k_ref[h, sl(m), :].astype(F32) for h, m in items]
        Gs = [g_ref[h, sl(m), :GDN_C] for h, m in items]
        kbs = [k * b_ref[h, sl(m), :] for k, (h, m) in zip(ks, items)]
        decays = [jnp.exp(jnp.where(c <= r, G - G.T, NEG)) for G in Gs]
        Ls = [jnp.where(c < r, _mm_raw(kb, k, "nt", False) * d, 0.0) for kb, k, d in zip(kbs, ks, decays)]
        for (h, m), inv in zip(items, _unit_lower_inverses(Ls)):
            inv_ref[h, m] = inv

    hb = pl.BlockSpec((GDN_H, rows, LANES), lambda n: (0, n, 0))
    return pl.pallas_call(
        body, name="gdn_inverses", grid=(nchunk // INV_CHUNKS,),
        in_specs=[pl.BlockSpec((None, GDN_H, rows, LANES), lambda n: (1, 0, n, 0)), hb, hb],
        out_specs=pl.BlockSpec((GDN_H, INV_CHUNKS, GDN_C, GDN_C), lambda n: (0, n, 0, 0)),
        out_shape=SDS((GDN_H, nchunk, GDN_C, GDN_C), F32),
        compiler_params=_cp(1),
    )(qkv, gc, bt)


def gdn_core(qkv, gc, bt, invs, S):
    nchunk = S // GDN_C

    def body(qkv_ref, g_ref, b_ref, inv_ref, o_ref, st_ref, s_scr):
        n = pl.program_id(0)

        @pl.when(n == 0)
        def _():
            s_scr[...] = jnp.zeros_like(s_scr)

        heads = range(GDN_H)
        S_cur = [s_scr[h] for h in heads]
        for m in range(SCAN_CHUNKS):
            sl = slice(m * GDN_C, (m + 1) * GDN_C)
            os_, S_new, _ = _gdn_chunk(*[[qkv_ref[p, h, sl, :].astype(F32) for h in heads] for p in range(3)],
                                       [g_ref[h, sl, :] for h in heads], [b_ref[h, sl, :] for h in heads], S_cur,
                                       invs=[inv_ref[h, m] for h in heads])
            for h in heads:
                st_ref[h, m] = S_cur[h].astype(st_ref.dtype)
                o_ref[h, sl, :] = os_[h]
            S_cur = S_new
        for h in heads:
            s_scr[h] = S_cur[h]

    rows = SCAN_CHUNKS * GDN_C
    blk3 = pl.BlockSpec((3, GDN_H, rows, LANES), lambda n: (0, 0, n, 0))
    hb = pl.BlockSpec((GDN_H, rows, LANES), lambda n: (0, n, 0))
    return pl.pallas_call(
        body, name="gdn_core", grid=(nchunk // SCAN_CHUNKS,),
        in_specs=[blk3, hb, hb, pl.BlockSpec((GDN_H, SCAN_CHUNKS, GDN_C, GDN_C), lambda n: (0, n, 0, 0))],
        out_specs=[hb, pl.BlockSpec((GDN_H, SCAN_CHUNKS, GDN_DK, LANES), lambda n: (0, n, 0, 0))],
        out_shape=[SDS((GDN_H, S, LANES), F32), SDS((GDN_H, nchunk, GDN_DK, LANES), BF16)],
        scratch_shapes=[pltpu.VMEM((GDN_H, GDN_DK, LANES), F32)],
        compiler_params=_cp(1),
    )(qkv, gc, bt, invs)


def gdn_core_bwd(qkv, gc, bt, states, invs, do, S):
    nchunk = S // GDN_C

    def body(qkv_ref, g_ref, b_ref, st_ref, inv_ref, do_ref, dqkv_ref, dg_ref, db_ref, ds_scr):
        n = pl.program_id(0)

        @pl.when(n == 0)
        def _():
            ds_scr[...] = jnp.zeros_like(ds_scr)

        heads = range(GDN_H)
        dS_cur = [ds_scr[h] for h in heads]
        for m in reversed(range(SCAN_CHUNKS)):
            sl = slice(m * GDN_C, (m + 1) * GDN_C)
            saved = [inv_ref[h, m] for h in heads]
            _, vjp = jax.vjp(lambda *a, _s=saved: _gdn_chunk(*a, invs=_s)[:2],
                             *[[qkv_ref[p, h, sl, :].astype(F32) for h in heads] for p in range(3)],
                             [g_ref[h, sl, :] for h in heads], [b_ref[h, sl, :] for h in heads],
                             [st_ref[h, m].astype(F32) for h in heads])
            dq, dk, dv, dg, db, dS_cur = vjp(([do_ref[h, sl, :] for h in heads], dS_cur))
            for h in heads:
                dqkv_ref[0, h, sl, :] = dq[h]
                dqkv_ref[1, h, sl, :] = dk[h]
                dqkv_ref[2, h, sl, :] = dv[h]
                dg_ref[h, sl, :] = dg[h]
                db_ref[h, sl, :] = db[h]
        for h in heads:
            ds_scr[h] = dS_cur[h]

    nblk = nchunk // SCAN_CHUNKS
    rows = SCAN_CHUNKS * GDN_C
    rev = lambda n: nblk - 1 - n
    blk3 = pl.BlockSpec((3, GDN_H, rows, LANES), lambda n: (0, 0, rev(n), 0))
    hb = pl.BlockSpec((GDN_H, rows, LANES), lambda n: (0, rev(n), 0))
    return pl.pallas_call(
        body, name="gdn_core_bwd", grid=(nblk,),
        in_specs=[blk3, hb, hb, pl.BlockSpec((GDN_H, SCAN_CHUNKS, GDN_DK, LANES), lambda n: (0, rev(n), 0, 0)),
                  pl.BlockSpec((GDN_H, SCAN_CHUNKS, GDN_C, GDN_C), lambda n: (0, rev(n), 0, 0)), hb],
        out_specs=[blk3, hb, hb],
        out_shape=[SDS((3, GDN_H, S, LANES), F32), SDS((GDN_H, S, LANES), F32), SDS((GDN_H, S, LANES), F32)],
        scratch_shapes=[pltpu.VMEM((GDN_H, GDN_DK, LANES), F32)],
        compiler_params=_cp(1),
    )(qkv, gc, bt, states, invs, do)


GDN_MAIN = 4 * GDN_H * LANES
GDN_PROJ = GDN_MAIN + LANES
RT = 512


def gdn_forward(h, w_in, conv_w, alog, dtb, out_gain, w_out):
    S = h.shape[0]
    nt = S // RT
    proj = matmul(h, w_in, "nn", BF16, "gdn_in")
    qkv = gdn_pre(proj, conv_w, S)
    ab_row = Row(proj, (RT, LANES), lambda i: (i, GDN_MAIN // LANES), gdtype=BF16, gshape=(S, LANES), gimap=lambda i: (i, 0))
    hm = lambda i: (0, i, 0)
    hv = GDN_H * LANES
    gc, bt = rowwise(f_gdn_gates, [ab_row], [alog, dtb],
                     [Out((GDN_H, S, LANES), F32, (GDN_H, RT, LANES), hm, lead=GDN_H)] * 2, (nt,), "gdn_gates")
    invs = gdn_inverses(qkv, gc, bt, S)
    o, states = gdn_core(qkv, gc, bt, invs, S)
    o_row = Row(o, (GDN_H, RT, LANES), hm, lead=GDN_H)
    z_row = Row(proj, (RT, hv), lambda i: (i, 3), splits=[LANES] * GDN_H, gdtype=BF16, gshape=(S, hv), gimap=lambda i: (i, 0))
    (on,) = rowwise(f_gdn_post, [o_row, z_row], [out_gain],
                    [Out((S, hv), BF16, (RT, hv), lambda i: (i, 0), splits=[LANES] * GDN_H)], (nt,), "gdn_post")
    y = matmul(on, w_out, "nn", BF16, "gdn_out")
    saved = dict(h=h, proj=proj, qkv=qkv, gc=gc, bt=bt, states=states, invs=invs, o=o, on=on, ab_row=ab_row, o_row=o_row, z_row=z_row)
    return y, saved


def gdn_backward(dy, sv, w_in, conv_w, alog, dtb, out_gain, w_out, on_weight_grads=None):
    S = dy.shape[0]
    nt = S // RT
    hm = lambda i: (0, i, 0)
    hv = GDN_H * LANES
    don = matmul(dy, w_out, "nt", BF16, "gdn_out_dx")
    d_w_out = matmul(sv["on"], dy, "tn", F32, "gdn_out_dw")
    (do, dz), (d_gain,) = rowwise_bwd(f_gdn_post, [sv["o_row"], sv["z_row"]], [out_gain],
                                      [Row(don, (RT, hv), lambda i: (i, 0), splits=[LANES] * GDN_H)], (nt,), "gdn_post_bwd")
    dqkv, dgc, dbt = gdn_core_bwd(sv["qkv"], sv["gc"], sv["bt"], sv["states"], sv["invs"], do, S)
    head_blk = lambda a: Row(a, (GDN_H, RT, LANES), hm, lead=GDN_H)
    (dab,), (d_alog, d_dtb) = rowwise_bwd(f_gdn_gates, [sv["ab_row"]], [alog, dtb], [head_blk(dgc), head_blk(dbt)],
                                          (nt,), "gdn_gates_bwd")
    dqkv_proj, d_conv = gdn_pre_bwd(sv["proj"], conv_w, dqkv, S)
    dproj = jnp.concatenate([dqkv_proj, dz, dab], axis=1)
    d_w_in = matmul(sv["h"], dproj, "tn", F32, "gdn_in_dw")
    if on_weight_grads is not None:
        w_in = w_in + on_weight_grads(d_w_in, d_w_out).astype(w_in.dtype)
    dh = matmul(dproj, w_in, "nt", BF16, "gdn_in_dx")
    return dh, dict(w_in=d_w_in, conv=d_conv, alog=d_alog, dtb=d_dtb, gain=d_gain, w_out=d_w_out)


QB = DSW_SPAN
N_HP = DSW_HG // LANES


def _bucket_maps():
    a = np.arange(QB)[:, None]
    j = np.arange(2 * QB)[None, :]
    dist = QB + a - j
    band = (dist >= 0) & (dist <= DSW_SPAN)
    maps = []
    for _, dil in DSW_GROUPS:
        dd = np.maximum(dist, 0) * dil
        max_exact = REL_BUCKETS // 2
        scaled = np.log(np.maximum(dd, 1).astype(np.float32) / np.float32(max_exact)) / np.float32(math.log(REL_MAX_DIST / max_exact))
        large = max_exact + (scaled * np.float32(REL_BUCKETS - max_exact)).astype(np.int32)
        large = np.minimum(large, REL_BUCKETS - 1)
        maps.append(np.where(dd < max_exact, dd, large).astype(np.int32))
    return np.stack(maps), band


def dsw_bias(rel_bias):
    maps, band = _bucket_maps()
    maps = np.where(band[None], maps, -1).astype(np.int32)

    def body(tab_ref, bk_ref, o_ref):
        gh = pl.program_id(0)
        bk = bk_ref[...]
        acc = jnp.full(bk.shape, NEG, F32)
        for b in range(REL_BUCKETS):
            acc = jnp.where(bk == b, tab_ref[b, gh], acc)
        o_ref[...] = acc

    return pl.pallas_call(
        body, name="dsw_bias", grid=(3 * GDN_H,),
        in_specs=[pl.BlockSpec(memory_space=pltpu.SMEM),
                  pl.BlockSpec((None, QB, 2 * QB), lambda gh: (gh // GDN_H, 0, 0))],
        out_specs=pl.BlockSpec((None, QB, 2 * QB), lambda gh: (gh, 0, 0)),
        out_shape=SDS((3 * GDN_H, QB, 2 * QB), F32),
        compiler_params=_cp(1),
    )(rel_bias, jnp.asarray(maps))


def dsw_bias_grad(dbias):
    maps, band = _bucket_maps()
    maps = np.where(band[None], maps, -1).astype(np.int32)

    def body(d_ref, bk_ref, o_ref):
        bk = bk_ref[...]
        d = d_ref[...]
        rows = lax.broadcasted_iota(jnp.int32, (REL_BUCKETS, LANES), 0)
        acc = jnp.zeros((REL_BUCKETS, LANES), F32)
        for b in range(REL_BUCKETS):
            part = jnp.sum(jnp.where(bk == b, d, 0.0), axis=0, keepdims=True)
            val = jnp.sum(part, axis=1, keepdims=True)
            acc = jnp.where(rows == b, val, acc)
        o_ref[...] = acc

    return pl.pallas_call(
        body, name="dsw_bias_grad", grid=(3 * GDN_H,),
        in_specs=[pl.BlockSpec((None, QB, 2 * QB), lambda gh: (gh, 0, 0)),
                  pl.BlockSpec((None, QB, 2 * QB), lambda gh: (gh // GDN_H, 0, 0))],
        out_specs=pl.BlockSpec((None, REL_BUCKETS, LANES), lambda gh: (gh, 0, 0)),
        out_shape=SDS((3 * GDN_H, REL_BUCKETS, LANES), F32),
        compiler_params=_cp(1),
    )(dbias, jnp.asarray(maps))


def _nt(a, b):
    return lax.dot_general(a, b, (((1,), (1,)), ((), ())), preferred_element_type=F32)


def _tn(a, b):
    return lax.dot_general(a, b, (((0,), (0,)), ((), ())), preferred_element_type=F32)


N_LB = DSW_HG // LANES
HALF = DSW_DH // 2


def _lanes(j):
    return slice(LANES * j, LANES * (j + 1))


def _deinterleave(stage, out_ref, dil, rows, dtype):
    for r in range(dil):
        for j in range(N_LB):
            out_ref[r, :, _lanes(j)] = stage[j, pl.ds(r, rows, stride=dil), :].astype(dtype)


def _interleave(in_ref, stage, dil, rows):
    for r in range(dil):
        for j in range(N_LB):
            stage[j, pl.ds(r, rows, stride=dil), :] = in_ref[r, :, _lanes(j)].astype(F32)


def dsw_prep(proj, q_gain2, k_gain2, gi, S):
    dil = DSW_GROUPS[gi][1]
    nt, rows = S // RT, RT // dil

    def body(q_ref, k_ref, v_ref, qg_ref, kg_ref, qo_ref, ko_ref, vo_ref, stage):
        for src, gain_ref, scale, dst in ((q_ref, qg_ref, DSW_DH ** -0.5, qo_ref), (k_ref, kg_ref, 1.0, ko_ref), (v_ref, None, None, vo_ref)):
            for j in range(N_LB):
                val = src[:, _lanes(j)].astype(F32)
                val = val if gain_ref is None else _qknorm1(val, gain_ref[...], scale)
                if dil == 1:
                    dst[0, :, _lanes(j)] = val.astype(BF16)
                else:
                    stage[j] = val
            if dil > 1:
                _deinterleave(stage, dst, dil, rows, BF16)

    col = lambda which: pl.BlockSpec((RT, DSW_HG), lambda i, _c=which * 3 + gi: (i, _c))
    gspec = pl.BlockSpec((1, LANES), lambda i: (0, 0))
    ospec = pl.BlockSpec((dil, rows, DSW_HG), lambda i: (0, i, 0))
    return pl.pallas_call(
        body, name=f"dsw_prep_g{gi}", grid=(nt,),
        in_specs=[col(0), col(1), col(2), gspec, gspec], out_specs=[ospec] * 3,
        out_shape=[SDS((dil, S // dil, DSW_HG), BF16)] * 3,
        scratch_shapes=[pltpu.VMEM((N_LB, RT, LANES), F32)], compiler_params=_cp(1),
    )(proj, proj, proj, q_gain2, k_gain2)


def dsw_prep_bwd(proj, q_gain2, k_gain2, dqd, dkd, dvd, gi, S):
    dil = DSW_GROUPS[gi][1]
    nt, rows = S // RT, RT // dil

    def body(q_ref, k_ref, qg_ref, kg_ref, dq_ref, dk_ref, dv_ref, oq_ref, ok_ref, ov_ref, dqg_ref, dkg_ref, stage):
        i = pl.program_id(0)

        @pl.when(i == 0)
        def _():
            dqg_ref[...] = jnp.zeros_like(dqg_ref)
            dkg_ref[...] = jnp.zeros_like(dkg_ref)

        for src, gain_ref, scale, cot_ref, dst, dg_ref in ((q_ref, qg_ref, DSW_DH ** -0.5, dq_ref, oq_ref, dqg_ref),
                                                          (k_ref, kg_ref, 1.0, dk_ref, ok_ref, dkg_ref)):
            if dil > 1:
                _interleave(cot_ref, stage, dil, rows)
            for j in range(N_LB):
                _, vjp = jax.vjp(lambda x, g, _s=scale: _qknorm1(x, g, _s), src[:, _lanes(j)].astype(F32), gain_ref[...])
                dx, dg = vjp(stage[j] if dil > 1 else cot_ref[0, :, _lanes(j)].astype(F32))
                dst[:, _lanes(j)] = dx.astype(dst.dtype)
                dg_ref[...] += dg
        if dil > 1:
            _interleave(dv_ref, stage, dil, rows)
        for j in range(N_LB):
            ov_ref[:, _lanes(j)] = (stage[j] if dil > 1 else dv_ref[0, :, _lanes(j)]).astype(ov_ref.dtype)

    col = lambda which: pl.BlockSpec((RT, DSW_HG), lambda i, _c=which * 3 + gi: (i, _c))
    gspec = pl.BlockSpec((1, LANES), lambda i: (0, 0))
    dspec = pl.BlockSpec((dil, rows, DSW_HG), lambda i: (0, i, 0))
    nspec = pl.BlockSpec((RT, DSW_HG), lambda i: (i, 0))
    return pl.pallas_call(
        body, name=f"dsw_prep_bwd_g{gi}", grid=(nt,),
        in_specs=[col(0), col(1), gspec, gspec, dspec, dspec, dspec], out_specs=[nspec] * 3 + [gspec] * 2,
        out_shape=[SDS((S, DSW_HG), BF16)] * 3 + [SDS((1, LANES), F32)] * 2,
        scratch_shapes=[pltpu.VMEM((N_LB, RT, LANES), F32)], compiler_params=_cp(1),
    )(proj, proj, q_gain2, k_gain2, dqd, dkd, dvd)


def _head_masks(rows):
    lane = lax.broadcasted_iota(jnp.int32, (rows, LANES), 1)
    return lane < DSW_DH, (lane % DSW_DH) < HALF


def dsw_attn_fwd(qd, kd, vd, bias, gi, S):
    dil = DSW_GROUPS[gi][1]
    sd = S // dil
    nq = sd // QB
    QPS = max(b for b in (1, 2, 4) if nq % b == 0)

    def body(q_ref, k_ref, v_ref, b_ref, o_ref, l_ref, kp_scr, vp_scr):
        i = pl.program_id(1)

        @pl.when(i == 0)
        def _():
            kp_scr[...] = jnp.zeros_like(kp_scr)
            vp_scr[...] = jnp.zeros_like(vp_scr)

        lo_q, _ = _head_masks(QB)
        lo_k, _ = _head_masks(2 * QB)
        col = lax.broadcasted_iota(jnp.int32, (QB, 2 * QB), 1)
        first = jnp.logical_and(i == 0, col < QB)
        hps, heads = range(N_HP), range(2 * N_HP)
        mqs = [lo_q if h % 2 == 0 else jnp.logical_not(lo_q) for h in heads]
        mks = [lo_k if h % 2 == 0 else jnp.logical_not(lo_k) for h in heads]
        k_last, v_last = k_ref[(QPS - 1) * QB:, :], v_ref[(QPS - 1) * QB:, :]
        for m in range(QPS):
            rows = slice(m * QB, (m + 1) * QB)
            before = slice((m - 1) * QB, m * QB)
            kps = [kp_scr[:, _lanes(hp)] if m == 0 else k_ref[before, _lanes(hp)] for hp in hps]
            vps = [vp_scr[:, _lanes(hp)] if m == 0 else v_ref[before, _lanes(hp)] for hp in hps]
            k2s = [jnp.concatenate([kps[hp], k_ref[rows, _lanes(hp)]], axis=0) for hp in hps]
            v2s = [jnp.concatenate([vps[hp], v_ref[rows, _lanes(hp)]], axis=0) for hp in hps]
            qs = [q_ref[rows, _lanes(hp)] for hp in hps]
            ss = [_nt(jnp.where(mqs[h], qs[h // 2], 0).astype(BF16), k2s[h // 2]) + b_ref[h] for h in heads]
            if m == 0:
                ss = [jnp.where(first, NEG, s) for s in ss]
            mxs = [jnp.max(s, axis=1, keepdims=True) for s in ss]
            ps = [jnp.exp(s - mx) for s, mx in zip(ss, mxs)]
            ls = [jnp.sum(p, axis=1, keepdims=True) for p in ps]
            ohs = [jnp.dot(ps[h].astype(BF16), jnp.where(mks[h], v2s[h // 2], 0).astype(BF16), preferred_element_type=F32) / ls[h]
                   for h in heads]
            lse_h = [mx + jnp.log(l) for mx, l in zip(mxs, ls)]
            for hp in hps:
                o_ref[rows, _lanes(hp)] = ohs[2 * hp] + ohs[2 * hp + 1]
                l_ref[rows, _lanes(hp)] = jnp.where(lo_q, lse_h[2 * hp], lse_h[2 * hp + 1])
        kp_scr[...] = k_last
        vp_scr[...] = v_last

    blk = pl.BlockSpec((None, QPS * QB, DSW_HG), lambda r, i: (r, i, 0))
    return pl.pallas_call(
        body, name=f"dsw_attn_g{gi}", grid=(dil, nq // QPS),
        in_specs=[blk, blk, blk, pl.BlockSpec((GDN_H, QB, 2 * QB), lambda r, i: (gi, 0, 0))],
        out_specs=[blk, blk], out_shape=[SDS((dil, sd, DSW_HG), F32)] * 2,
        scratch_shapes=[pltpu.VMEM((QB, DSW_HG), BF16)] * 2, compiler_params=_cp(2),
    )(qd, kd, vd, bias)


def dsw_attn_bwd(qd, kd, vd, bias, dod, statd, gi, S):
    dil = DSW_GROUPS[gi][1]
    sd = S // dil
    nq = sd // QB
    B = max(b for b in (1, 2, 4) if nq % b == 0)
    nb = nq // B
    cur = lambda i: jnp.minimum(i, nb - 1)
    done = lambda i: jnp.maximum(i - 1, 0)
    last = slice((B - 1) * QB, B * QB)

    def body(q_ref, k_ref, v_ref, b_ref, do_ref, st_ref, dq_ref, dk_ref, dv_ref, db_ref, kp_scr, vp_scr, dk_scr, dv_scr):
        r, i = pl.program_id(0), pl.program_id(1)

        @pl.when(jnp.logical_and(r == 0, i == 0))
        def _():
            db_ref[...] = jnp.zeros_like(db_ref)

        @pl.when(i == 0)
        def _():
            for scr in (kp_scr, vp_scr, dk_scr, dv_scr):
                scr[...] = jnp.zeros_like(scr)

        @pl.when(i < nb)
        def _():
            lo_q, first_half = _head_masks(QB)
            col = lax.broadcasted_iota(jnp.int32, (QB, 2 * QB), 1)
            first = jnp.logical_and(i == 0, col < QB)
            hps, heads = range(N_HP), range(2 * N_HP)
            mqs = [lo_q if h % 2 == 0 else jnp.logical_not(lo_q) for h in heads]
            k_last, v_last = k_ref[last, :], v_ref[last, :]
            to_prev, to_cur = [], []
            for m in range(B):
                rows = slice(m * QB, (m + 1) * QB)
                before = slice((m - 1) * QB, m * QB)
                kps = [kp_scr[:, _lanes(hp)] if m == 0 else k_ref[before, _lanes(hp)] for hp in hps]
                vps = [vp_scr[:, _lanes(hp)] if m == 0 else v_ref[before, _lanes(hp)] for hp in hps]
                k2s = [jnp.concatenate([kps[hp], k_ref[rows, _lanes(hp)]], axis=0) for hp in hps]
                v2s = [jnp.concatenate([vps[hp], v_ref[rows, _lanes(hp)]], axis=0) for hp in hps]
                qs = [q_ref[rows, _lanes(hp)] for hp in hps]
                douts = [do_ref[rows, _lanes(hp)] for hp in hps]
                stats = [st_ref[rows, _lanes(hp)] for hp in hps]
                qms = [jnp.where(mqs[h], qs[h // 2], 0).astype(BF16) for h in heads]
                doms = [jnp.where(mqs[h], douts[h // 2], 0).astype(BF16) for h in heads]
                ss = [_nt(qms[h], k2s[h // 2]) + b_ref[h] for h in heads]
                if m == 0:
                    ss = [jnp.where(first, NEG, s) for s in ss]
                lses = [jnp.max(jnp.where(jnp.logical_and(mqs[h], first_half), stats[h // 2], NEG), axis=1, keepdims=True) for h in heads]
                deltas = [jnp.max(jnp.where(jnp.logical_and(mqs[h], jnp.logical_not(first_half)), stats[h // 2], NEG), axis=1,
                                  keepdims=True) for h in heads]
                ps = [jnp.exp(ss[h] - lses[h]) for h in heads]
                dss = [ps[h] * (_nt(doms[h], v2s[h // 2]) - deltas[h]) for h in heads]
                dsbs = [d.astype(BF16) for d in dss]
                dqh = [jnp.where(mqs[h], jnp.dot(dsbs[h], k2s[h // 2], preferred_element_type=F32), 0.0) for h in heads]
                dkh = [_tn(dsbs[h], qms[h]) for h in heads]
                dvh = [_tn(ps[h].astype(BF16), doms[h]) for h in heads]
                for h in heads:
                    db_ref[h] += dss[h]
                for hp in hps:
                    dq_ref[rows, _lanes(hp)] = dqh[2 * hp] + dqh[2 * hp + 1]
                dk2s = [dkh[2 * hp] + dkh[2 * hp + 1] for hp in hps]
                dv2s = [dvh[2 * hp] + dvh[2 * hp + 1] for hp in hps]
                to_prev.append(([d[:QB] for d in dk2s], [d[:QB] for d in dv2s]))
                to_cur.append(([d[QB:] for d in dk2s], [d[QB:] for d in dv2s]))
            for hp in hps:
                if B > 1:
                    dk_ref[:(B - 1) * QB, _lanes(hp)] = dk_scr[:(B - 1) * QB, _lanes(hp)]
                    dv_ref[:(B - 1) * QB, _lanes(hp)] = dv_scr[:(B - 1) * QB, _lanes(hp)].astype(dv_ref.dtype)
                dk_ref[last, _lanes(hp)] = dk_scr[last, _lanes(hp)] + to_prev[0][0][hp]
                dv_ref[last, _lanes(hp)] = (dv_scr[last, _lanes(hp)] + to_prev[0][1][hp]).astype(dv_ref.dtype)
            for hp in hps:
                for m in range(B):
                    rows = slice(m * QB, (m + 1) * QB)
                    nk, nv = to_cur[m][0][hp], to_cur[m][1][hp]
                    if m < B - 1:
                        nk, nv = nk + to_prev[m + 1][0][hp], nv + to_prev[m + 1][1][hp]
                    dk_scr[rows, _lanes(hp)] = nk
                    dv_scr[rows, _lanes(hp)] = nv
            kp_scr[...] = k_last
            vp_scr[...] = v_last

        @pl.when(i == nb)
        def _():
            dk_ref[...] = dk_scr[...]
            dv_ref[...] = dv_scr[...].astype(dv_ref.dtype)

    blk = pl.BlockSpec((None, B * QB, DSW_HG), lambda r, i: (r, cur(i), 0))
    oblk = pl.BlockSpec((None, B * QB, DSW_HG), lambda r, i: (r, done(i), 0))
    return pl.pallas_call(
        body, name=f"dsw_attn_bwd_g{gi}", grid=(dil, nb + 1),
        in_specs=[blk, blk, blk, pl.BlockSpec((GDN_H, QB, 2 * QB), lambda r, i: (gi, 0, 0)), blk, blk],
        out_specs=[blk, oblk, oblk, pl.BlockSpec((GDN_H, QB, 2 * QB), lambda r, i: (0, 0, 0))],
        out_shape=[SDS((dil, sd, DSW_HG), F32), SDS((dil, sd, DSW_HG), F32), SDS((dil, sd, DSW_HG), BF16),
                   SDS((GDN_H, QB, 2 * QB), F32)],
        scratch_shapes=[pltpu.VMEM((QB, DSW_HG), BF16)] * 2 + [pltpu.VMEM((B * QB, DSW_HG), F32)] * 2,
        compiler_params=_cp(2),
    )(qd, kd, vd, bias, dod, statd)


def dsw_combine(ods, lseds, S):
    nt = S // RT
    dils = [d for _, d in DSW_GROUPS]

    def body(*refs):
        ins, (o_ref, l_ref), stages = refs[:6], refs[6:8], refs[8:]
        for g in range(3):
            if dils[g] > 1:
                _interleave(ins[g], stages[g], dils[g], RT // dils[g])
                _interleave(ins[3 + g], stages[3 + g], dils[g], RT // dils[g])
        for j in range(N_LB):
            natural = lambda a: stages[a][j] if dils[a % 3] > 1 else ins[a][0, :, _lanes(j)]
            o, lse = f_combine(None, *[natural(a) for a in range(6)])
            o_ref[:, _lanes(j)] = o.astype(o_ref.dtype)
            l_ref[:, _lanes(j)] = lse

    dspec = lambda d: pl.BlockSpec((d, RT // d, DSW_HG), lambda i: (0, i, 0))
    nspec = pl.BlockSpec((RT, DSW_HG), lambda i: (i, 0))
    return pl.pallas_call(
        body, name="dsw_combine", grid=(nt,),
        in_specs=[dspec(d) for d in dils] * 2, out_specs=[nspec, nspec],
        out_shape=[SDS((S, DSW_HG), BF16), SDS((S, DSW_HG), F32)],
        scratch_shapes=[pltpu.VMEM((N_LB, RT, LANES), F32)] * 6, compiler_params=_cp(1),
    )(*ods, *lseds)


def dsw_bwd_prep(do, o, lse, S):
    nt = S // RT
    dils = [d for _, d in DSW_GROUPS]

    def body(do_ref, o_ref, l_ref, *rest):
        outs, (st_do, st_stat) = rest[:6], rest[6:]
        lo, first_half = _head_masks(RT)
        for j in range(N_LB):
            dout = do_ref[:, _lanes(j)]
            prod = dout * o_ref[:, _lanes(j)].astype(F32)
            s_all = jnp.sum(prod, axis=1, keepdims=True)
            s_lo = jnp.sum(jnp.where(lo, prod, 0.0), axis=1, keepdims=True)
            delta = jnp.where(lo, s_lo, s_all - s_lo)
            stat = jnp.where(first_half, l_ref[:, _lanes(j)], delta)
            st_do[j] = dout
            st_stat[j] = stat
            for g in range(3):
                if dils[g] == 1:
                    outs[g][0, :, _lanes(j)] = dout.astype(BF16)
                    outs[3 + g][0, :, _lanes(j)] = stat
        for g in range(3):
            if dils[g] > 1:
                _deinterleave(st_do, outs[g], dils[g], RT // dils[g], BF16)
                _deinterleave(st_stat, outs[3 + g], dils[g], RT // dils[g], F32)

    nspec = pl.BlockSpec((RT, DSW_HG), lambda i: (i, 0))
    dspec = lambda d: pl.BlockSpec((d, RT // d, DSW_HG), lambda i: (0, i, 0))
    res = pl.pallas_call(
        body, name="dsw_bwd_prep", grid=(nt,),
        in_specs=[nspec] * 3, out_specs=[dspec(d) for d in dils] * 2,
        out_shape=[SDS((d, S // d, DSW_HG), BF16) for d in dils] + [SDS((d, S // d, DSW_HG), F32) for d in dils],
        scratch_shapes=[pltpu.VMEM((N_LB, RT, LANES), F32)] * 2, compiler_params=_cp(1),
    )(do, o, lse)
    return res[:3], res[3:]


def dsw_forward(h, w_in, q_gain2, k_gain2, rel_bias, w_out):
    S = h.shape[0]
    proj = matmul(h, w_in, "nn", BF16, "dsw_in", col_shards=N_SHARD)
    bias = dsw_bias(rel_bias)
    qkv, ods, lseds = [], [], []
    for gi in range(3):
        qd, kd, vd = dsw_prep(proj, q_gain2, k_gain2, gi, S)
        od, ld = dsw_attn_fwd(qd, kd, vd, bias, gi, S)
        qkv.append((qd, kd, vd))
        ods.append(od)
        lseds.append(ld)
    o, lse = dsw_combine(ods, lseds, S)
    y = matmul(o, w_out, "nn", BF16, "dsw_out", col_shards=N_SHARD)
    return y, dict(h=h, proj=proj, qkv=qkv, bias=bias, o=o, lse=lse)


def dsw_backward(dy, sv, w_in, q_gain2, k_gain2, w_out):
    S = dy.shape[0]
    do = matmul(dy, w_out, "nt", F32, "dsw_out_dx", col_shards=N_SHARD)
    d_w_out = matmul(sv["o"], dy, "tn", F32, "dsw_out_dw", col_shards=N_SHARD)
    dods, statds = dsw_bwd_prep(do, sv["o"], sv["lse"], S)
    pieces_q, pieces_k, pieces_v, dbs = [], [], [], []
    d_qg = jnp.zeros((1, LANES), F32)
    d_kg = jnp.zeros((1, LANES), F32)
    for gi in range(3):
        qd, kd, vd = sv["qkv"][gi]
        dqd, dkd, dvd, db = dsw_attn_bwd(qd, kd, vd, sv["bias"], dods[gi], statds[gi], gi, S)
        dq, dk, dv, dqg, dkg = dsw_prep_bwd(sv["proj"], q_gain2, k_gain2, dqd, dkd, dvd, gi, S)
        dbs.append(db)
        pieces_q.append(dq)
        pieces_k.append(dk)
        pieces_v.append(dv)
        d_qg = d_qg + dqg
        d_kg = d_kg + dkg
    dproj = jnp.concatenate(pieces_q + pieces_k + pieces_v, axis=1)
    d_w_in = matmul(sv["h"], dproj, "tn", F32, "dsw_in_dw", col_shards=N_SHARD)
    dh = matmul(dproj, w_in, "nt", BF16, "dsw_in_dx", col_shards=N_SHARD)
    d_rel = dsw_bias_grad(jnp.concatenate(dbs, axis=0))
    return dh, dict(w_in=d_w_in, q_gain2=d_qg, k_gain2=d_kg, rel=d_rel, w_out=d_w_out)


FUSE_M = 512


def ffn_in_act(h, w_in, name):
    S = h.shape[0]
    half = FFN // 2

    def body(h_ref, wg_ref, wu_ref, gu_ref, a_ref):
        j = pl.program_id(1)
        sub = FUSE_M // 2
        for part in range(2):
            rows = slice(part * sub, (part + 1) * sub)
            hb = h_ref[rows, :]
            g = jnp.dot(hb, wg_ref[...], preferred_element_type=F32)
            u = jnp.dot(hb, wu_ref[...], preferred_element_type=F32)
            a_ref[rows, :] = (_silu(g) * u).astype(a_ref.dtype)
            for jj in range(2):
                @pl.when(j == jj)
                def _(g=g, u=u, jj=jj, rows=rows):
                    gu_ref[rows, jj * half:(jj + 1) * half] = g.astype(gu_ref.dtype)
                    gu_ref[rows, FFN + jj * half:FFN + (jj + 1) * half] = u.astype(gu_ref.dtype)

    return pl.pallas_call(
        body, name=name, grid=(S // FUSE_M, 2),
        in_specs=[pl.BlockSpec((FUSE_M, D), lambda i, j: (i, 0)),
                  pl.BlockSpec((None, D, half), lambda i, j: (j, 0, 0)),
                  pl.BlockSpec((None, D, half), lambda i, j: (j + 2, 0, 0))],
        out_specs=[pl.BlockSpec((FUSE_M, 2 * FFN), lambda i, j: (i, 0)), pl.BlockSpec((FUSE_M, half), lambda i, j: (i, j))],
        out_shape=[SDS((S, 2 * FFN), BF16), SDS((S, FFN), BF16)],
        compiler_params=_cp(2),
    )(h, w_in, w_in)


def ffn_forward(h, w_in, w_out, tag):
    gu, a = ffn_in_act(h, w_in, f"ffn_in_act_{tag}")
    f = matmul(a, w_out, "nn", BF16, f"ffn_out_{tag}")
    return f, dict(h=h, gu=gu, a=a)


def ffn_out_dx_act(df, w_out, gu, name):
    S = df.shape[0]
    half = FFN // 2

    def body(df_ref, w_ref, g_ref, u_ref, dgu_ref):
        j = pl.program_id(1)
        sub = FUSE_M // 2
        for part in range(2):
            rows = slice(part * sub, (part + 1) * sub)
            da = _nt(df_ref[rows, :], w_ref[...])
            dg, du = _swiglu_bwd((g_ref[rows, :].astype(F32), u_ref[rows, :].astype(F32)), da)
            for jj in range(2):
                @pl.when(j == jj)
                def _(dg=dg, du=du, jj=jj, rows=rows):
                    dgu_ref[rows, jj * half:(jj + 1) * half] = dg.astype(dgu_ref.dtype)
                    dgu_ref[rows, FFN + jj * half:FFN + (jj + 1) * half] = du.astype(dgu_ref.dtype)

    return pl.pallas_call(
        body, name=name, grid=(S // FUSE_M, 2),
        in_specs=[pl.BlockSpec((FUSE_M, D), lambda i, j: (i, 0)),
                  pl.BlockSpec((half, D), lambda i, j: (j, 0)),
                  pl.BlockSpec((FUSE_M, half), lambda i, j: (i, j)),
                  pl.BlockSpec((FUSE_M, half), lambda i, j: (i, j + 2))],
        out_specs=pl.BlockSpec((FUSE_M, 2 * FFN), lambda i, j: (i, 0)),
        out_shape=SDS((S, 2 * FFN), BF16),
        compiler_params=_cp(2),
    )(df, w_out, gu, gu)


def ffn_backward(df, sv, w_in, w_out, tag):
    d_w_out = matmul(sv["a"], df, "tn", F32, f"ffn_out_dw_{tag}")
    dgu = ffn_out_dx_act(df, w_out, sv["gu"], f"ffn_out_dx_act_{tag}")
    d_w_in = matmul(sv["h"], dgu, "tn", F32, f"ffn_in_dw_{tag}", col_shards=N_SHARD)
    dh = matmul(dgu, w_in, "nt", BF16, f"ffn_in_dx_{tag}", col_shards=N_SHARD)
    return dh, d_w_in, d_w_out


def f_norm_only(ids, x, gain, sc, sh):
    return (_normmod(x, gain, sc, sh),)


WT = 512


def _wide(a, **kw):
    return Row(a, (WT, D), lambda i: (i, 0), **kw)


def _wide_out(S, dtype):
    return Out((S, D), dtype, (WT, D), lambda i: (i, 0))


def adamw(w, g, m, v, name):
    shape = w.shape
    C = shape[-1]
    R = int(np.prod(shape[:-1]))
    w2, g2, m2, v2 = (a.reshape(R, C) for a in (w, g, m, v))
    br = R
    if R > 256:
        br = max(b for b in range(8, 257, 8) if R % b == 0)
    c1 = 1.0 / (1.0 - ADAM_B1 ** ADAM_STEP)
    c2 = 1.0 / (1.0 - ADAM_B2 ** ADAM_STEP)

    def body(w_ref, g_ref, m_ref, v_ref, d_ref, nm_ref, nv_ref):
        gg = g_ref[...]
        mm_ = ADAM_B1 * m_ref[...] + (1.0 - ADAM_B1) * gg
        vv = ADAM_B2 * v_ref[...] + (1.0 - ADAM_B2) * (gg * gg)
        d_ref[...] = -ADAM_LR * ((mm_ * c1) / (jnp.sqrt(vv * c2) + ADAM_EPS) + ADAM_WD * w_ref[...])
        nm_ref[...] = mm_
        nv_ref[...] = vv

    spec = pl.BlockSpec((br, C), lambda i: (i, 0))
    d, nm, nv = pl.pallas_call(
        body, name=name, grid=(R // br,), in_specs=[spec] * 4, out_specs=[spec] * 3,
        out_shape=[SDS((R, C), F32)] * 3, compiler_params=_cp(1),
    )(w2, g2, m2, v2)
    return d.reshape(shape), nm.reshape(shape), nv.reshape(shape)


def _place():
    x, y, c = lax.axis_index("x"), lax.axis_index("y"), lax.axis_index("c")
    chips = [(1 - x, y), (x, 1 - y), (1 - x, 1 - y)]
    return x, y, c, chips


def all_gather_small(blk, name):
    m_per, n = blk.shape

    def body(x_ref, out_ref, send_sems, recv_sems, local_sem):
        x, y, c, chips = _place()
        me, sibling = (x, y, c), (x, y, 1 - c)

        def rows(px, py, pc):
            return out_ref.at[pl.ds((4 * px + 2 * py + pc) * m_per, m_per), :]

        def copy(k, block, to, src=None):
            return pltpu.make_async_remote_copy(
                src_ref=rows(*block) if src is None else src, dst_ref=rows(*block),
                send_sem=send_sems.at[k], recv_sem=recv_sems.at[k], device_id=to, device_id_type=MESH)

        mine = pltpu.make_async_copy(x_ref, rows(*me), local_sem)
        mine.start()
        first = [copy(0, me, sibling, src=x_ref)]
        first += [copy(1 + j, me, (*chip, c), src=x_ref) for j, chip in enumerate(chips)]
        for cp in first:
            cp.start()
        passed = [copy(4 + j, (*chip, c), sibling) for j, chip in enumerate(chips)]
        for j, chip in enumerate(chips):
            copy(1 + j, (*chip, c), me).wait_recv()
            passed[j].start()
        copy(0, sibling, me).wait_recv()
        for j, chip in enumerate(chips):
            copy(4 + j, (*chip, 1 - c), me).wait_recv()
        for cp in first + passed:
            cp.wait_send()
        mine.wait()

    return pl.pallas_call(
        body, name=name, out_shape=SDS((N_DEV * m_per, n), blk.dtype),
        in_specs=[pl.BlockSpec(memory_space=pltpu.VMEM)], out_specs=pl.BlockSpec(memory_space=pltpu.VMEM),
        scratch_shapes=[pltpu.SemaphoreType.DMA((7,)), pltpu.SemaphoreType.DMA((7,)), pltpu.SemaphoreType.DMA],
    )(blk)


def _half(cc, rh):
    return pl.ds(pl.multiple_of(cc * rh, 16), rh)


def all_gather_shards(ws):
    n = len(ws)

    def body(*refs):
        w_refs, out_refs = refs[:n], refs[n:2 * n]
        send_sems, recv_sems, local_sems, own_sems = refs[2 * n:]
        x, y, c, chips = _place()
        sibling = (x, y, 1 - c)
        s_me = 2 * x + y

        def copy(k, src, dst, to):
            return pltpu.make_async_remote_copy(src_ref=src, dst_ref=dst, send_sem=send_sems.at[k], recv_sem=recv_sems.at[k],
                                                device_id=to, device_id_type=MESH)

        local, sends, passed = [], [], []
        for k in range(n):
            rh = ws[k].shape[0] // 2
            cp = pltpu.make_async_remote_copy(src_ref=w_refs[k], dst_ref=out_refs[k].at[s_me], send_sem=local_sems.at[k],
                                              recv_sem=own_sems.at[k], device_id=sibling, device_id_type=MESH)
            cp.start()
            local.append(cp)
            for j, chip in enumerate(chips):
                sd = copy(6 * k + j, w_refs[k].at[_half(c, rh)], out_refs[k].at[s_me, _half(c, rh)], (*chip, c))
                sd.start()
                sends.append(sd)
        for k in range(n):
            rh = ws[k].shape[0] // 2
            for j, (px, py) in enumerate(chips):
                got = out_refs[k].at[2 * px + py, _half(c, rh)]
                copy(6 * k + j, got, got, (px, py, c)).wait_recv()
                fw = copy(6 * k + 3 + j, got, got, sibling)
                fw.start()
                passed.append(fw)
        for k in range(n):
            rh = ws[k].shape[0] // 2
            for j, (px, py) in enumerate(chips):
                got = out_refs[k].at[2 * px + py, _half(1 - c, rh)]
                copy(6 * k + 3 + j, got, got, sibling).wait_recv()
        for cp in sends + passed:
            cp.wait_send()
        for cp in local:
            cp.wait()

    return pl.pallas_call(
        body, name="weights_all_gather", out_shape=[SDS((N_SHARD,) + w.shape, w.dtype) for w in ws],
        in_specs=[ANY] * n, out_specs=[ANY] * n,
        scratch_shapes=[pltpu.SemaphoreType.DMA((6 * n,)), pltpu.SemaphoreType.DMA((6 * n,)), pltpu.SemaphoreType.DMA((n,)),
                        pltpu.SemaphoreType.DMA((n,))],
    )(*ws)


def sibling_exchange(sends, name):
    n = len(sends)

    def body(*refs):
        s_refs, o_refs, send_sems, recv_sems = refs[:n], refs[n:2 * n], refs[2 * n], refs[2 * n + 1]
        x, y, c, _ = _place()
        cps = [pltpu.make_async_remote_copy(src_ref=s_refs[k], dst_ref=o_refs[k], send_sem=send_sems.at[k], recv_sem=recv_sems.at[k],
                                            device_id=(x, y, 1 - c), device_id_type=MESH) for k in range(n)]
        for cp in cps:
            cp.start()
        for cp in cps:
            cp.wait()

    return pl.pallas_call(
        body, name=name, out_shape=[SDS(s.shape, s.dtype) for s in sends], in_specs=[ANY] * n, out_specs=[ANY] * n,
        scratch_shapes=[pltpu.SemaphoreType.DMA((n,)), pltpu.SemaphoreType.DMA((n,))],
    )(*sends)


def add_rows(arrs, out_dtype, name, rt=256):
    Rr, W = arrs[0].shape

    def fn(ids, *vals):
        acc = vals[0]
        for v in vals[1:]:
            acc = acc + v
        return (acc,)

    t = rt if Rr % rt == 0 else max(b for b in range(16, rt + 1, 16) if Rr % b == 0)
    (out,) = rowwise(fn, [Row(a, (t, W), lambda i: (i, 0)) for a in arrs], [],
                     [Out((Rr, W), out_dtype, (t, W), lambda i: (i, 0))], (Rr // t,), name)
    return out


HBM_SPEC = pl.BlockSpec(memory_space=pltpu.HBM)
SEM_SPEC = pl.BlockSpec(memory_space=pltpu.SEMAPHORE)
DATAFLOW = pltpu.SideEffectType.DATAFLOW_SIDE_EFFECTING


def _in_hbm(a):
    return pltpu.with_memory_space_constraint(a, pltpu.HBM)


def _gather_copies(w_refs, land_refs, send_sems, recv_sems):
    x, y, c, chips = _place()
    targets = [(x, y, 1 - c)] + [(*chip, c) for chip in chips]
    cps = []
    for k, (w_ref, land_ref) in enumerate(zip(w_refs, land_refs)):
        for j, to in enumerate(targets):
            cps.append(pltpu.make_async_remote_copy(src_ref=w_ref, dst_ref=land_ref.at[2 * x + y], send_sem=send_sems.at[4 * k + j],
                                                    recv_sem=recv_sems.at[4 * k + j], device_id=to, device_id_type=MESH))
    return cps


def _scatter_copies(p_refs, land_refs, send_sems, recv_sems):
    x, y, c, chips = _place()
    cps = []
    for k, (p_ref, land_ref) in enumerate(zip(p_refs, land_refs)):
        for j, (px, py) in enumerate(chips):
            cps.append(pltpu.make_async_remote_copy(src_ref=p_ref.at[2 * px + py], dst_ref=land_ref.at[j], send_sem=send_sems.at[3 * k + j],
                                                    recv_sem=recv_sems.at[3 * k + j], device_id=(px, py, c), device_id_type=MESH))
    return cps


def copies_start(srcs, land_shapes, make_copies, per_src, name):
    n = len(srcs)
    m = per_src * n

    def body(*refs):
        src_refs, land_refs = refs[:n], refs[n:2 * n]
        send_sems, recv_sems, token = refs[2 * n], refs[2 * n + 1], refs[-1]
        for cp in make_copies(src_refs, land_refs, send_sems, recv_sems):
            cp.start()
        token[...] = jnp.zeros_like(token)

    lands = [lax.empty(shp, s.dtype) for shp, s in zip(land_shapes, srcs)]
    res = pl.pallas_call(
        body, name=name,
        out_shape=(pltpu.SemaphoreType.DMA((m,)), pltpu.SemaphoreType.DMA((m,)), *[pltpu.HBM(s.shape, s.dtype) for s in srcs],
                   *[pltpu.HBM(shp, s.dtype) for shp, s in zip(land_shapes, srcs)], SDS((8, LANES), F32)),
        in_specs=[HBM_SPEC] * (2 * n),
        out_specs=(SEM_SPEC, SEM_SPEC, *[HBM_SPEC] * (2 * n), pl.BlockSpec(memory_space=pltpu.VMEM)),
        input_output_aliases={i: 2 + i for i in range(2 * n)},
        compiler_params=pltpu.CompilerParams(has_side_effects=DATAFLOW),
    )(*[_in_hbm(s) for s in srcs], *[_in_hbm(l) for l in lands])
    return res[0], res[1], list(res[2:2 + n]), list(res[2 + n:2 + 2 * n]), res[-1]


def copies_wait(send_sems, recv_sems, srcs, lands, make_copies, after, name):
    n = len(srcs)

    def body(*refs):
        src_refs, land_refs = refs[:n], refs[n:2 * n]
        for cp in make_copies(src_refs, land_refs, refs[2 * n], refs[2 * n + 1]):
            cp.wait_send()
            cp.wait_recv()

    res = pl.pallas_call(
        body, name=name,
        out_shape=(*[pltpu.HBM(s.shape, s.dtype) for s in srcs], *[pltpu.HBM(l.shape, l.dtype) for l in lands]),
        in_specs=[HBM_SPEC] * (2 * n) + [SEM_SPEC, SEM_SPEC, ANY],
        out_specs=tuple([HBM_SPEC] * (2 * n)),
        input_output_aliases={i: i for i in range(2 * n)},
        compiler_params=pltpu.CompilerParams(has_side_effects=DATAFLOW),
    )(*srcs, *lands, send_sems, recv_sems, after)
    return list(res[n:])


def _pad_lanes(v):
    return jnp.concatenate([v.astype(F32), jnp.zeros((LANES - v.shape[0],), F32)])[None]


def kernel(x, c, w_ada, b_ada, norm_mix, norm_ffn, w_ffn_in, w_ffn_out, gdn_w_in, gdn_conv, gdn_a_log, gdn_dt_bias, gdn_out_norm, gdn_w_out, dsw_w_in, dsw_q_norm, dsw_k_norm, dsw_w_out, rel_bias, loss_target, m_w_ada, m_b_ada, m_norm_mix, m_norm_ffn, m_w_ffn_in, m_w_ffn_out, m_gdn_w_in, m_gdn_conv, m_gdn_a_log, m_gdn_dt_bias, m_gdn_out_norm, m_gdn_w_out, m_dsw_w_in, m_dsw_q_norm, m_dsw_k_norm, m_dsw_w_out, m_rel_bias, v_w_ada, v_b_ada, v_norm_mix, v_norm_ffn, v_w_ffn_in, v_w_ffn_out, v_gdn_w_in, v_gdn_conv, v_gdn_a_log, v_gdn_dt_bias, v_gdn_out_norm, v_gdn_w_out, v_dsw_w_in, v_dsw_q_norm, v_dsw_k_norm, v_dsw_w_out, v_rel_bias):
    S = x.shape[1]
    nt = S // WT
    xi, yi, ci = lax.axis_index("x"), lax.axis_index("y"), lax.axis_index("c")
    me = 4 * xi + 2 * yi + ci
    s_me = 2 * xi + yi
    x0, tgt = x[0], loss_target[0]

    whole = lambda a: Row(a, a.shape, lambda i: (0,) * a.ndim)
    (cond8,) = rowwise(lambda ids, v: (_silu(v),), [whole(c.reshape(8, LANES))], [], [Out((8, LANES), F32, (8, LANES), lambda i: (0, 0))], (1,), "cond")
    cond_all = all_gather_small(cond8, "gather_cond").reshape(N_DEV, D)
    cond16 = jnp.concatenate([cond_all, jnp.zeros((8, D), F32)], axis=0)
    ada_cols = w_ada.shape[2]
    mods = [matmul(cond16, w_ada[l], "nn", F32, f"ada_{l}")[:N_DEV] for l in range(2)]
    buf = jnp.concatenate([jnp.stack(mods, axis=1).reshape(-1, LANES), gdn_conv.reshape(-1, LANES)], axis=0)
    n_mod_rows = N_DEV * 2 * ada_cols // LANES
    got = all_gather_small(buf, "gather_mod").reshape(N_DEV, buf.shape[0], LANES)
    mod_parts, conv_parts = [], []
    for s in range(N_SHARD):
        from_dev = got[2 * s]
        mod_parts.append(lax.dynamic_index_in_dim(from_dev[:n_mod_rows].reshape(N_DEV, 2, ada_cols), me, 0, keepdims=False))
        conv_parts.append(from_dev[n_mod_rows:].reshape(4, -1))
    mod_nb = jnp.concatenate(mod_parts, axis=1)
    conv_w = jnp.concatenate(conv_parts, axis=1)
    (mod,) = rowwise(lambda ids, a, b: (a + b,), [whole(mod_nb), whole(b_ada)], [], [Out(mod_nb.shape, F32, mod_nb.shape, lambda i: (0, 0))], (1,), "mod_bias")
    mod = mod.reshape(2, 6, 1, D)
    sh1, sc1, g1, sh2, sc2, g2 = ([mod[l, k] for l in range(2)] for k in range(6))
    gmix = [norm_mix[l][None] for l in range(2)]
    gffn = [norm_ffn[l][None] for l in range(2)]

    gcols = gdn_w_in.shape[2]
    g_gdn_in, g_gdn_out = all_gather_shards([gdn_w_in[0].astype(BF16), gdn_w_out[0].astype(BF16)])
    gathered = lambda ws: [(N_SHARD,) + w.shape for w in ws]
    gate = (jnp.minimum(jnp.abs(g_gdn_in[0, 0, 0].astype(F32)), 0.0) + jnp.minimum(jnp.abs(mod[0, 0, 0, 0]), 0.0)).astype(BF16)
    w2 = [w_ffn_in[0].astype(BF16) + gate, w_ffn_out[0].astype(BF16) + gate]
    w3 = [dsw_w_in[0].astype(BF16) + gate, dsw_w_out[0].astype(BF16) + gate, w_ffn_in[1].astype(BF16) + gate, w_ffn_out[1].astype(BF16) + gate]
    fly2 = copies_start(w2, gathered(w2), _gather_copies, 4, "weights_ffn0_start")
    fly3 = copies_start(w3, gathered(w3), _gather_copies, 4, "weights_layer1_start")
    started = fly2[4][0, 0] + fly3[4][0, 0]
    w_gdn = jnp.concatenate([g_gdn_in[s] for s in range(N_SHARD)] + [jnp.zeros((D, GDN_PROJ - N_SHARD * gcols), BF16)], axis=1)
    alog, dtb = _pad_lanes(gdn_a_log[0]), _pad_lanes(gdn_dt_bias[0])
    qg2 = jnp.concatenate([dsw_q_norm, dsw_q_norm], axis=1)
    kg2 = jnp.concatenate([dsw_k_norm, dsw_k_norm], axis=1)
    w_gdn_out = g_gdn_out.reshape(GDN_H * LANES, D)
    gdn_args = (w_gdn, conv_w, alog, dtb, gdn_out_norm, w_gdn_out)
    sc1[0] = sc1[0] + started

    (h10,) = rowwise(f_norm_only, [_wide(x0)], [gmix[0], sc1[0], sh1[0]], [_wide_out(S, BF16)], (nt,), "l0_norm")
    y0, sv_g = gdn_forward(h10, *gdn_args)
    x1, h20 = rowwise(f_resid_norm, [_wide(x0), _wide(y0)], [g1[0], gffn[0], sc2[0], sh2[0]], [_wide_out(S, F32), _wide_out(S, BF16)], (nt,), "l0_mid")
    g_in0, g_out0 = copies_wait(*fly2[:4], _gather_copies, y0, "weights_ffn0_wait")
    w_ffn = [(g_in0, g_out0.reshape(FFN, D)), None]
    f0, sv_f0 = ffn_forward(h20, *w_ffn[0], "0")
    x2, h11 = rowwise(f_resid_norm, [_wide(x1), _wide(f0)], [g2[0], gmix[1], sc1[1], sh1[1]], [_wide_out(S, F32), _wide_out(S, BF16)], (nt,), "l1_in")
    g_dsw_in, g_dsw_out, g_in1, g_out1 = copies_wait(*fly3[:4], _gather_copies, f0, "weights_layer1_wait")
    w_ffn[1] = (g_in1, g_out1.reshape(FFN, D))
    dsw_args = (g_dsw_in, qg2, kg2)
    y1, sv_d = dsw_forward(h11, *dsw_args, rel_bias, g_dsw_out)
    x3, h21 = rowwise(f_resid_norm, [_wide(x2), _wide(y1)], [g1[1], gffn[1], sc2[1], sh2[1]], [_wide_out(S, F32), _wide_out(S, BF16)], (nt,), "l1_mid")
    f1, sv_f1 = ffn_forward(h21, *w_ffn[1], "1")
    parts, dx3, df1, dg2_1 = loss_and_grad(x3, f1, tgt, g2[1], S)
    loss = lax.psum(jnp.sum(parts), ("x", "y", "c"))

    dh21, d_win1, d_wout1 = ffn_backward(df1, sv_f1, *w_ffn[1], "1")
    (dx2, dy1), (dg1_1, dgf1, dsc2_1, dsh2_1) = rowwise_bwd(
        f_resid_norm, [_wide(x2), _wide(y1, gdtype=BF16)], [g1[1], gffn[1], sc2[1], sh2[1]], [_wide(dx3), _wide(dh21)], (nt,), "l1_mid_bwd")
    dh11, g_d = dsw_backward(dy1, sv_d, *dsw_args, g_dsw_out)
    (dx1, df0), (dg2_0, dgm1, dsc1_1, dsh1_1) = rowwise_bwd(
        f_resid_norm, [_wide(x1), _wide(f0, gdtype=BF16)], [g2[0], gmix[1], sc1[1], sh1[1]], [_wide(dx2), _wide(dh11)], (nt,), "l1_in_bwd")
    by_shard = lambda a: a.reshape(N_SHARD, a.shape[0] // N_SHARD, a.shape[1])
    landing = lambda ps: [(3,) + p.shape[1:] for p in ps]
    dws3 = [g_d["w_in"], g_d["w_out"], d_win1, by_shard(d_wout1)]
    parts3 = [a.astype(BF16) for a in dws3]
    gfly3 = copies_start(parts3, landing(parts3), _scatter_copies, 3, "grads_layer1_start")
    w_out0 = w_ffn[0][1] + gfly3[4][0, 0].astype(BF16)
    dh20, d_win0, d_wout0 = ffn_backward(df0, sv_f0, w_ffn[0][0], w_out0, "0")
    (dx0p, dy0), (dg1_0, dgf0, dsc2_0, dsh2_0) = rowwise_bwd(
        f_resid_norm, [_wide(x0), _wide(y0, gdtype=BF16)], [g1[0], gffn[0], sc2[0], sh2[0]], [_wide(dx1), _wide(dh20)], (nt,), "l0_mid_bwd")
    dws2 = [d_win0, by_shard(d_wout0)]
    parts2 = [a.astype(BF16) for a in dws2]
    gfly2 = copies_start(parts2, landing(parts2), _scatter_copies, 3, "grads_ffn0_start")
    gdn_args = gdn_args[:5] + (w_gdn_out + gfly2[4][0, 0].astype(BF16),)
    gdn_flight = []

    def start_gdn_grads(d_w_in, d_w_out):
        dws1 = [jnp.stack([d_w_in[:, s * gcols:(s + 1) * gcols] for s in range(N_SHARD)]), by_shard(d_w_out)]
        parts1 = [a.astype(BF16) for a in dws1]
        fly = copies_start(parts1, landing(parts1), _scatter_copies, 3, "grads_gdn_start")
        gdn_flight.extend([dws1, fly])
        return fly[4][0, 0]

    dh10, g_g = gdn_backward(dy0, sv_g, *gdn_args, on_weight_grads=start_gdn_grads)
    dws1, gfly1 = gdn_flight
    (grad_x,), (dgm0, dsc1_0, dsh1_0) = rowwise_bwd(f_first, [_wide(x0)], [gmix[0], sc1[0], sh1[0]], [_wide(dx0p), _wide(dh10)], (nt,), "l0_norm_bwd")

    dmod = jnp.concatenate([dsh1_0, dsc1_0, dg1_0, dsh2_0, dsc2_0, dg2_0, dsh1_1, dsc1_1, dg1_1, dsh2_1, dsc2_1, dg2_1], axis=1)
    d_rel = jnp.transpose(g_d["rel"][:, :, 0])
    fold = lambda v: v[:, :DSW_DH] + v[:, DSW_DH:]
    small = [dmod, jnp.concatenate([dgm0, dgm1], axis=1), jnp.concatenate([dgf0, dgf1], axis=1), g_g["conv"].reshape(1, -1),
             g_g["alog"], g_g["dtb"], g_g["gain"], _pad_lanes(fold(g_d["q_gain2"])[0]), _pad_lanes(fold(g_d["k_gain2"])[0]),
             d_rel.reshape(1, -1)]
    used = [v.shape[1] // LANES for v in small]
    sizes = [-(-u // 8) * 8 for u in used]
    pad8 = lambda v, u, s: jnp.concatenate([v.reshape(u, LANES), jnp.zeros((s - u, LANES), F32)], axis=0) if s > u else v.reshape(u, LANES)
    pad_rows = sum(sizes)
    sbuf = jnp.concatenate([pad8(v, u, s) for v, u, s in zip(small, used, sizes)], axis=0)
    sgot = all_gather_small(sbuf, "gather_small_grads")
    ssum = add_rows([sgot[d * pad_rows:(d + 1) * pad_rows] for d in range(N_DEV)], F32, "sum_small_grads", rt=pad_rows)
    offs = np.cumsum([0] + sizes)
    take = lambda k: ssum[offs[k]:offs[k] + used[k]].reshape(1, -1)
    grad_b_ada = take(0).reshape(2, 6 * D)
    grad_norm_mix = take(1).reshape(2, D)
    grad_norm_ffn = take(2).reshape(2, D)
    conv_full = take(3).reshape(4, -1)
    ncv = gdn_conv.shape[2]
    grad_gdn_conv = lax.dynamic_slice_in_dim(conv_full, s_me * ncv, ncv, axis=1)[None]
    grad_a_log = take(4)[:, :GDN_H]
    grad_dt_bias = take(5)[:, :GDN_H]
    grad_out_norm = take(6)
    grad_q_norm = take(7)[:, :DSW_DH]
    grad_k_norm = take(8)[:, :DSW_DH]
    grad_rel = take(9).reshape(REL_BUCKETS, 3 * GDN_H)
    dmod_all = sgot.reshape(N_DEV, pad_rows, LANES)[:, :used[0]].reshape(N_DEV, 2, 6 * D)
    dmod_mine = lax.dynamic_slice_in_dim(dmod_all, s_me * ada_cols, ada_cols, axis=2)
    dmod16 = jnp.concatenate([dmod_mine, jnp.zeros_like(dmod_mine)], axis=0)
    grad_w_ada = jnp.stack([matmul(cond16, dmod16[:, l], "tn", F32, f"ada_dw_{l}") for l in range(2)])

    got3 = copies_wait(*gfly3[:4], _scatter_copies, grad_x, "grads_layer1_wait")
    got2 = copies_wait(*gfly2[:4], _scatter_copies, grad_x, "grads_ffn0_wait")
    got1 = copies_wait(*gfly1[:4], _scatter_copies, grad_x, "grads_gdn_wait")
    core_sums = []
    for i, (full, got) in enumerate(zip(dws3 + dws2 + dws1, got3 + got2 + got1)):
        own = lax.dynamic_index_in_dim(full, s_me, 0, keepdims=False)
        core_sums.append(add_rows([own, got[0], got[1], got[2]], F32, f"grads_core_sum_{i}"))
    sib_sums = sibling_exchange(core_sums, "grads_core_sums_swap")
    s_dsw_in, s_dsw_out, s_in1, s_out1, s_in0, s_out0, s_gdn_in, s_gdn_out = [
        add_rows([a, b], F32, f"grads_chip_total_{i}") for i, (a, b) in enumerate(zip(core_sums, sib_sums))]
    gsh = dict(gdn_w_in=s_gdn_in[None], gdn_w_out=s_gdn_out[None],
               w_ffn_in=jnp.stack([s_in0, s_in1]), w_ffn_out=jnp.stack([s_out0, s_out1]),
               dsw_w_in=s_dsw_in[None], dsw_w_out=s_dsw_out[None])

    grads = dict(w_ada=grad_w_ada, b_ada=grad_b_ada, norm_mix=grad_norm_mix, norm_ffn=grad_norm_ffn, w_ffn_in=gsh["w_ffn_in"],
                 w_ffn_out=gsh["w_ffn_out"], gdn_w_in=gsh["gdn_w_in"], gdn_conv=grad_gdn_conv, gdn_a_log=grad_a_log,
                 gdn_dt_bias=grad_dt_bias, gdn_out_norm=grad_out_norm, gdn_w_out=gsh["gdn_w_out"], dsw_w_in=gsh["dsw_w_in"],
                 dsw_q_norm=grad_q_norm, dsw_k_norm=grad_k_norm, dsw_w_out=gsh["dsw_w_out"], rel_bias=grad_rel)
    weights = dict(w_ada=w_ada, b_ada=b_ada, norm_mix=norm_mix, norm_ffn=norm_ffn, w_ffn_in=w_ffn_in, w_ffn_out=w_ffn_out,
                   gdn_w_in=gdn_w_in, gdn_conv=gdn_conv, gdn_a_log=gdn_a_log, gdn_dt_bias=gdn_dt_bias, gdn_out_norm=gdn_out_norm,
                   gdn_w_out=gdn_w_out, dsw_w_in=dsw_w_in, dsw_q_norm=dsw_q_norm, dsw_k_norm=dsw_k_norm, dsw_w_out=dsw_w_out,
                   rel_bias=rel_bias)
    ms = dict(w_ada=m_w_ada, b_ada=m_b_ada, norm_mix=m_norm_mix, norm_ffn=m_norm_ffn, w_ffn_in=m_w_ffn_in, w_ffn_out=m_w_ffn_out,
              gdn_w_in=m_gdn_w_in, gdn_conv=m_gdn_conv, gdn_a_log=m_gdn_a_log, gdn_dt_bias=m_gdn_dt_bias, gdn_out_norm=m_gdn_out_norm,
              gdn_w_out=m_gdn_w_out, dsw_w_in=m_dsw_w_in, dsw_q_norm=m_dsw_q_norm, dsw_k_norm=m_dsw_k_norm, dsw_w_out=m_dsw_w_out,
              rel_bias=m_rel_bias)
    vs = dict(w_ada=v_w_ada, b_ada=v_b_ada, norm_mix=v_norm_mix, norm_ffn=v_norm_ffn, w_ffn_in=v_w_ffn_in, w_ffn_out=v_w_ffn_out,
              gdn_w_in=v_gdn_w_in, gdn_conv=v_gdn_conv, gdn_a_log=v_gdn_a_log, gdn_dt_bias=v_gdn_dt_bias, gdn_out_norm=v_gdn_out_norm,
              gdn_w_out=v_gdn_w_out, dsw_w_in=v_dsw_w_in, dsw_q_norm=v_dsw_q_norm, dsw_k_norm=v_dsw_k_norm, dsw_w_out=v_dsw_w_out,
              rel_bias=v_rel_bias)
    names = list(weights)
    deltas, new_m, new_v = [], [], []
    for n in names:
        g = grads[n].reshape(weights[n].shape)
        grads[n] = g
        d, nm, nv = adamw(weights[n], g, ms[n], vs[n], f"adamw_{n}")
        deltas.append(d)
        new_m.append(nm)
        new_v.append(nv)
    return (loss, grad_x[None], *[grads[n] for n in names], *deltas, *new_m, *new_v)
```

```python
import functools
import math

import numpy as np
import jax
import jax.numpy as jnp
from jax import lax
from jax.experimental import pallas as pl
from jax.experimental.pallas import tpu as pltpu

F32 = jnp.float32
BF16 = jnp.bfloat16
SDS = jax.ShapeDtypeStruct
MESH = pl.DeviceIdType.MESH
ANY = pl.BlockSpec(memory_space=pl.ANY)

D = 1024
EPS = 1e-6
LANES = 128
GDN_H = 8
GDN_DK = 128
GDN_C = 64
DSW_GROUPS = ((128, 1), (512, 4), (2048, 16))
DSW_SPAN = 128
DSW_DH = 64
DSW_HG = 512
REL_BUCKETS = 32
REL_MAX_DIST = 2048
FFN = 2816
N_SHARD = 4
N_DEV = 8
VMEM_LIMIT = 48 * 1024 * 1024
NEG = -1e30

ADAM_LR, ADAM_B1, ADAM_B2, ADAM_EPS, ADAM_WD, ADAM_STEP = 0.001, 0.9, 0.999, 1e-08, 0.01, 10


def _cp(n_axes):
    return pltpu.CompilerParams(dimension_semantics=("arbitrary",) * n_axes, vmem_limit_bytes=VMEM_LIMIT)


def _blk(dim, cap):
    if dim <= cap:
        return dim
    best = None
    for b in range(LANES, cap + 1, LANES):
        if dim % b == 0:
            best = b
    assert best is not None, (dim, cap)
    return best


MAX_SHARD_BLOCK = 1408
def matmul(a, b, mode, out_dtype, name, cap_m=MAX_SHARD_BLOCK, cap_n=MAX_SHARD_BLOCK, cap_k=2048, col_shards=0):
    ns = col_shards
    if mode == "nn":
        (M, K) = a.shape
        K2, N = (b.shape[1], ns * b.shape[2]) if ns else b.shape
    elif mode == "nt":
        (M, K) = a.shape
        N, K2 = (b.shape[1], ns * b.shape[2]) if ns else b.shape
    else:
        (K, M), (K2, N) = a.shape, b.shape
    assert K == K2, (a.shape, b.shape, mode)
    if K <= 3072:
        cap_k = K
        if K > 2048:
            cap_n = 1024
    n_unit = N // ns if (ns and mode != "nt") else N
    k_unit = K // ns if (ns and mode == "nt") else K
    bm = _blk(M, cap_m)
    bn = _blk(n_unit, MAX_SHARD_BLOCK) if n_unit != N else _blk(N, cap_n)
    if k_unit != K:
        bk = _blk(k_unit, MAX_SHARD_BLOCK)
    else:
        bk = _blk(K, 1024 if (ns and mode == "tn") else cap_k)
    nk = K // bk
    nps, kps = n_unit // bn, k_unit // bk
    dims = {"nn": ((1,), (0,)), "nt": ((1,), (1,)), "tn": ((0,), (0,))}[mode]

    def dot(a_ref, b_ref):
        return lax.dot_general(a_ref[...].astype(BF16), b_ref[...].astype(BF16), (dims, ((), ())), preferred_element_type=F32)

    def body_one(a_ref, b_ref, o_ref):
        o_ref[...] = dot(a_ref, b_ref).astype(o_ref.dtype)

    def body_acc(a_ref, b_ref, o_ref, acc_ref):
        k = pl.program_id(2)

        @pl.when(k == 0)
        def _():
            acc_ref[...] = jnp.zeros_like(acc_ref)

        acc_ref[...] += dot(a_ref, b_ref)

        @pl.when(k == nk - 1)
        def _():
            o_ref[...] = acc_ref[...].astype(o_ref.dtype)

    a_spec = pl.BlockSpec((bk, bm), lambda i, j, k: (k, i)) if mode == "tn" else pl.BlockSpec((bm, bk), lambda i, j, k: (i, k))
    if mode == "nt":
        b_spec = pl.BlockSpec((None, bn, bk), lambda i, j, k: (k // kps, j, k % kps)) if ns else pl.BlockSpec((bn, bk), lambda i, j, k: (j, k))
    elif mode == "nn" and ns:
        b_spec = pl.BlockSpec((None, bk, bn), lambda i, j, k: (j // nps, k, j % nps))
    else:
        b_spec = pl.BlockSpec((bk, bn), lambda i, j, k: (k, j))
    if mode == "tn" and ns:
        o_spec, o_shape = pl.BlockSpec((None, bm, bn), lambda i, j, k: (j // nps, i, j % nps)), (ns, M, n_unit)
    else:
        o_spec, o_shape = pl.BlockSpec((bm, bn), lambda i, j, k: (i, j)), (M, N)
    return pl.pallas_call(
        body_one if nk == 1 else body_acc, name=name, grid=(M // bm, N // bn, nk),
        in_specs=[a_spec, b_spec], out_specs=o_spec,
        out_shape=SDS(o_shape, out_dtype), scratch_shapes=[] if nk == 1 else [pltpu.VMEM((bm, bn), F32)],
        compiler_params=_cp(3),
    )(a, b)


class Row:
    def __init__(self, arr, bshape, imap, splits=None, diff=True, acc=False, gdtype=F32, gshape=None, gbshape=None, gimap=None,
                 lead=0):
        self.arr, self.bshape, self.imap = arr, tuple(bshape), imap
        self.splits, self.lead = splits, lead
        self.diff, self.acc, self.gdtype = diff, acc, gdtype
        self.gshape = tuple(arr.shape) if gshape is None else tuple(gshape)
        self.gbshape = self.bshape if gbshape is None else tuple(gbshape)
        self.gimap = imap if gimap is None else gimap

    def gspec(self):
        return pl.BlockSpec(self.gbshape, self.gimap)

    def spec(self):
        return pl.BlockSpec(self.bshape, self.imap)

    def pieces(self, ref):
        return _load_pieces(ref, self.splits, self.lead)

    def n_pieces(self):
        return _n_pieces(self.splits, self.lead)


class Out:
    def __init__(self, shape, dtype, bshape, imap, splits=None, lead=0):
        self.shape, self.dtype, self.bshape, self.imap = tuple(shape), dtype, tuple(bshape), imap
        self.splits, self.lead = splits, lead

    def n_pieces(self):
        return _n_pieces(self.splits, self.lead)


def _n_pieces(splits, lead):
    return lead if lead else (1 if splits is None else len(splits))


def _load_pieces(ref, splits, lead):
    if lead:
        return [ref[k].astype(F32) for k in range(lead)]
    if splits is None:
        return [ref[...].astype(F32)]
    out, o = [], 0
    for w in splits:
        out.append(ref[..., o:o + w].astype(F32))
        o += w
    return out


def _store_pieces(ref, splits, lead, vals, accumulate=False):
    def put(idx, v):
        if accumulate:
            ref[idx] += v.astype(ref.dtype)
        else:
            ref[idx] = v.astype(ref.dtype)

    if lead:
        for k in range(lead):
            put(k, vals[k])
    elif splits is None:
        put(..., vals[0])
    else:
        o = 0
        for w, v in zip(splits, vals):
            put((..., slice(o, o + w)), v)
            o += w


def rowwise(fn, rows, params, outs, grid, name):
    nr, npar = len(rows), len(params)

    def body(*refs):
        ids = tuple(pl.program_id(a) for a in range(len(grid)))
        vals = []
        for r, ref in zip(rows, refs[:nr]):
            vals += r.pieces(ref)
        pvals = [ref[...].astype(F32) for ref in refs[nr:nr + npar]]
        res = list(fn(ids, *vals, *pvals))
        o = 0
        for spec, ref in zip(outs, refs[nr + npar:]):
            n = spec.n_pieces()
            _store_pieces(ref, spec.splits, spec.lead, res[o:o + n])
            o += n

    nz = len(grid)
    pspecs = [pl.BlockSpec(p.shape, (lambda *ids, _n=p.ndim: (0,) * _n)) for p in params]
    res = pl.pallas_call(
        body, name=name, grid=grid,
        in_specs=[r.spec() for r in rows] + pspecs,
        out_specs=[pl.BlockSpec(o.bshape, o.imap) for o in outs],
        out_shape=[SDS(o.shape, o.dtype) for o in outs],
        compiler_params=_cp(nz),
    )(*[r.arr for r in rows], *params)
    return list(res)


def rowwise_bwd(fn, rows, params, cots, grid, name):
    nr, npar, nc = len(rows), len(params), len(cots)
    drows = [r for r in rows if r.diff]
    nz = len(grid)

    def body(*refs):
        ids = tuple(pl.program_id(a) for a in range(nz))
        row_refs, par_refs = refs[:nr], refs[nr:nr + npar]
        cot_refs = refs[nr + npar:nr + npar + nc]
        drow_refs = refs[nr + npar + nc:nr + npar + nc + len(drows)]
        dpar_refs = refs[nr + npar + nc + len(drows):]
        pieces, is_diff = [], []
        for r, ref in zip(rows, row_refs):
            ps = r.pieces(ref)
            pieces += ps
            is_diff += [r.diff] * len(ps)
        pvals = [ref[...].astype(F32) for ref in par_refs]
        dvals = [p for p, dflag in zip(pieces, is_diff) if dflag]
        nd = len(dvals)

        def f(*args):
            it = iter(args[:nd])
            full = [next(it) if dflag else p for p, dflag in zip(pieces, is_diff)]
            return tuple(fn(ids, *full, *args[nd:]))

        _, vjp = jax.vjp(f, *dvals, *pvals)
        cvals = []
        for c, ref in zip(cots, cot_refs):
            cvals += c.pieces(ref)
        g = vjp(tuple(cvals))
        o = 0
        first_inner = ids[-1] == 0
        for r, ref in zip(drows, drow_refs):
            n = r.n_pieces()
            gs = g[o:o + n]
            o += n
            if r.acc:
                @pl.when(first_inner)
                def _(ref=ref):
                    ref[...] = jnp.zeros_like(ref)
            _store_pieces(ref, r.splits, r.lead, gs, accumulate=r.acc)
        first = functools.reduce(jnp.logical_and, [i == 0 for i in ids])
        for ref, gp in zip(dpar_refs, g[nd:]):
            @pl.when(first)
            def _(ref=ref):
                ref[...] = jnp.zeros_like(ref)
            ref[...] += gp

    pspecs = [pl.BlockSpec(p.shape, (lambda *ids, _n=p.ndim: (0,) * _n)) for p in params]
    res = pl.pallas_call(
        body, name=name, grid=grid,
        in_specs=[r.spec() for r in rows] + pspecs + [c.spec() for c in cots],
        out_specs=[r.gspec() for r in drows] + pspecs,
        out_shape=[SDS(r.gshape, r.gdtype) for r in drows] + [SDS(p.shape, F32) for p in params],
        compiler_params=_cp(nz),
    )(*[r.arr for r in rows], *params, *[c.arr for c in cots])
    res = list(res)
    return res[:len(drows)], res[len(drows):]


def _sigmoid(x):
    return 0.5 * (jnp.tanh(0.5 * x) + 1.0)


def _silu(x):
    return x * _sigmoid(x)


def _normmod(x, gain, sc, sh):
    inv = lax.rsqrt(jnp.mean(x * x, axis=-1, keepdims=True) + EPS)
    return x * inv * gain * (1.0 + sc) + sh


def f_first(ids, x, gain, sc, sh):
    return x, _normmod(x, gain, sc, sh)


def f_resid_norm(ids, x, y, g, gain, sc, sh):
    xn = x + g * y
    return xn, _normmod(xn, gain, sc, sh)


@jax.custom_vjp
def _swiglu(gate, up):
    return _silu(gate) * up


def _swiglu_fwd(gate, up):
    return _silu(gate) * up, (gate, up)


def _swiglu_bwd(res, da):
    gate, up = res
    s = _sigmoid(gate)
    gs = gate * s
    return da * up * (s + gs * (1.0 - s)), da * gs


_swiglu.defvjp(_swiglu_fwd, _swiglu_bwd)


def loss_and_grad(x, y, tgt, g, S):
    nt = S // WT

    def body(x_ref, y_ref, t_ref, g_ref, part_ref, dx_ref, dy_ref, dg_ref):
        @pl.when(pl.program_id(0) == 0)
        def _():
            dg_ref[...] = jnp.zeros_like(dg_ref)

        yv = y_ref[...].astype(F32)
        gg = g_ref[...]
        e = x_ref[...] + gg * yv - t_ref[...]
        part_ref[...] = 0.5 * jnp.sum(e * e, axis=0, keepdims=True) * (1.0 / D)
        d = e * (1.0 / D)
        dx_ref[...] = d
        dy_ref[...] = (d * gg).astype(dy_ref.dtype)
        dg_ref[...] += jnp.sum(d * yv, axis=0, keepdims=True)

    row = pl.BlockSpec((WT, D), lambda i: (i, 0))
    vec = pl.BlockSpec((1, D), lambda i: (0, 0))
    return pl.pallas_call(
        body, name="loss_and_grad", grid=(nt,), in_specs=[row, row, row, vec],
        out_specs=[pl.BlockSpec((None, 1, D), lambda i: (i, 0, 0)), row, row, vec],
        out_shape=[SDS((nt, 1, D), F32), SDS((S, D), F32), SDS((S, D), BF16), SDS((1, D), F32)],
        compiler_params=_cp(1),
    )(x, y, tgt, g)


def _softplus(x):
    return jnp.maximum(x, 0.0) + jnp.log(1.0 + jnp.exp(-jnp.abs(x)))


def _chunk_tril(T):
    r = lax.broadcasted_iota(jnp.int32, (T, T), 0)
    c = lax.broadcasted_iota(jnp.int32, (T, T), 1)
    return jnp.where((r // GDN_C == c // GDN_C) & (c <= r), 1.0, 0.0).astype(F32)


def _dot_hi(a, b, dims=((1,), (0,))):
    return lax.dot_general(a, b, (dims, ((), ())), precision=lax.Precision.HIGHEST, preferred_element_type=F32)


def _dot_x3(a, b, dims=((1,), (0,))):
    return lax.dot_general(a, b, (dims, ((), ())), precision=lax.Precision.HIGH, preferred_element_type=F32)


def f_gdn_gates(ids, ab, alog, dtb):
    T = ab.shape[0]
    g = -jnp.exp(alog) * _softplus(ab + dtb)
    beta = _sigmoid(ab)
    gcum = _dot_x3(_chunk_tril(T), g)
    row = lax.broadcasted_iota(jnp.int32, (LANES, LANES), 0)
    sel = lambda k: jnp.where(row == k, 1.0, 0.0).astype(F32)
    gcs = [_dot_x3(gcum, sel(h)) for h in range(GDN_H)]
    bts = [_dot_x3(beta, sel(GDN_H + h)) for h in range(GDN_H)]
    return (*gcs, *bts)


def f_gdn_post(ids, *args):
    os_, zs, gain = args[:GDN_H], args[GDN_H:2 * GDN_H], args[2 * GDN_H]
    out = []
    for o, z in zip(os_, zs):
        inv = lax.rsqrt(jnp.mean(o * o, axis=-1, keepdims=True) + EPS)
        out.append(o * inv * gain * _silu(z))
    return tuple(out)


def _qknorm1(x, gain2, scale):
    lane = lax.broadcasted_iota(jnp.int32, x.shape, 1)
    lo = lane < DSW_DH
    x2 = x * x
    s_all = jnp.sum(x2, axis=-1, keepdims=True)
    s_lo = jnp.sum(jnp.where(lo, x2, 0.0), axis=-1, keepdims=True)
    ms = jnp.where(lo, s_lo, s_all - s_lo) * (1.0 / DSW_DH)
    return x * lax.rsqrt(ms + EPS) * (gain2 * scale)


def f_combine(ids, o0, o1, o2, l0, l1, l2):
    m = jnp.maximum(jnp.maximum(l0, l1), l2)
    e0, e1, e2 = jnp.exp(l0 - m), jnp.exp(l1 - m), jnp.exp(l2 - m)
    den = e0 + e1 + e2
    o = (e0 * o0 + e1 * o1 + e2 * o2) / den
    return o, m + jnp.log(den)


GDN_T = 512
HALO = 16


def _conv_pre(xx, w):
    acc = xx * w[3:4, :]
    for j in range(3):
        acc = acc + pltpu.roll(xx, shift=3 - j, axis=0) * w[j:j + 1, :]
    return acc


@jax.custom_vjp
def _qkv_act_core(pre, norm_on, scale):
    s = _silu(pre)
    r = lax.rsqrt(jnp.sum(s * s, axis=-1, keepdims=True) + EPS)
    return jnp.where(norm_on > 0.5, s * r * scale, s)


def _qkv_act_fwd(pre, norm_on, scale):
    return _qkv_act_core(pre, norm_on, scale), (pre, norm_on, scale)


def _qkv_act_bwd(res, dout):
    pre, norm_on, scale = res
    sig = _sigmoid(pre)
    s = pre * sig
    r = lax.rsqrt(jnp.sum(s * s, axis=-1, keepdims=True) + EPS)
    unit = s * r
    dn = dout * scale
    ds = jnp.where(norm_on > 0.5, r * (dn - unit * jnp.sum(dn * unit, axis=-1, keepdims=True)), dout)
    return ds * (sig + s * (1.0 - sig)), jnp.zeros_like(norm_on), jnp.zeros_like(scale)


_qkv_act_core.defvjp(_qkv_act_fwd, _qkv_act_bwd)


def _qkv_act(pre, cidx):
    norm_on = jnp.where(cidx < 2 * GDN_H, 1.0, 0.0).astype(F32)
    scale = jnp.where(cidx < GDN_H, GDN_DK ** -0.5, 1.0).astype(F32)
    return _qkv_act_core(pre, norm_on, scale)


def gdn_pre(proj, conv_w, S):
    nt = S // GDN_T
    hb = GDN_T // HALO

    def body(prev_ref, cur_ref, w_ref, o_ref):
        p, i = pl.program_id(0), pl.program_id(1)
        for h in range(GDN_H):
            cols = slice(LANES * h, LANES * (h + 1))
            prev = jnp.where(i > 0, prev_ref[:, cols].astype(F32), 0.0)
            xx = jnp.concatenate([prev, cur_ref[:, cols].astype(F32)], axis=0)
            pre = _conv_pre(xx, w_ref[:, cols])[HALO:]
            o_ref[h] = _qkv_act(pre, p * GDN_H + h).astype(o_ref.dtype)

    hv = GDN_H * LANES
    return pl.pallas_call(
        body, name="gdn_pre", grid=(3, nt),
        in_specs=[pl.BlockSpec((HALO, hv), lambda p, i: (jnp.maximum(i * hb - 1, 0), p)),
                  pl.BlockSpec((GDN_T, hv), lambda p, i: (i, p)),
                  pl.BlockSpec((4, hv), lambda p, i: (0, p))],
        out_specs=pl.BlockSpec((None, GDN_H, GDN_T, LANES), lambda p, i: (p, 0, i, 0)),
        out_shape=SDS((3, GDN_H, S, LANES), BF16),
        compiler_params=_cp(2),
    )(proj, proj, conv_w)


def gdn_pre_bwd(proj, conv_w, dqkv, S):
    nt = S // GDN_T
    hb = GDN_T // HALO
    last_h = S // HALO - 1

    def body(prev_ref, cur_ref, next_ref, w_ref, d_ref, dnext_ref, dx_ref, dw_ref):
        p, i = pl.program_id(0), pl.program_id(1)

        @pl.when(i == 0)
        def _():
            dw_ref[...] = jnp.zeros_like(dw_ref)

        for h in range(GDN_H):
            cols = slice(LANES * h, LANES * (h + 1))
            w = w_ref[:, cols]
            prev = jnp.where(i > 0, prev_ref[:, cols].astype(F32), 0.0)
            xx = jnp.concatenate([prev, cur_ref[:, cols].astype(F32), next_ref[:, cols].astype(F32)], axis=0)
            dnext = jnp.where(i < nt - 1, dnext_ref[h], 0.0)
            dd = jnp.concatenate([jnp.zeros((HALO, LANES), F32), d_ref[h], dnext], axis=0)
            pre = _conv_pre(xx, w)
            _, vjp = jax.vjp(lambda v, _c=p * GDN_H + h: _qkv_act(v, _c), pre)
            (dpre,) = vjp(dd)
            dx = dpre * w[3:4, :]
            R = dpre.shape[0]
            for j in range(3):
                dx = dx + pltpu.roll(dpre, shift=R - (3 - j), axis=0) * w[j:j + 1, :]
            dx_ref[:, cols] = dx[HALO:HALO + GDN_T].astype(dx_ref.dtype)
            own = HALO + GDN_T
            rows_w = [jnp.sum((dpre * pltpu.roll(xx, shift=3 - j, axis=0))[:own], axis=0, keepdims=True) for j in range(3)]
            rows_w.append(jnp.sum((dpre * xx)[:own], axis=0, keepdims=True))
            r4 = lax.broadcasted_iota(jnp.int32, (4, LANES), 0)
            dw = jnp.zeros((4, LANES), F32)
            for j in range(4):
                dw = dw + jnp.where(r4 == j, rows_w[j], 0.0)
            dw_ref[:, cols] += dw

    hv = GDN_H * LANES
    return pl.pallas_call(
        body, name="gdn_pre_bwd", grid=(3, nt),
        in_specs=[pl.BlockSpec((HALO, hv), lambda p, i: (jnp.maximum(i * hb - 1, 0), p)),
                  pl.BlockSpec((GDN_T, hv), lambda p, i: (i, p)),
                  pl.BlockSpec((HALO, hv), lambda p, i: (jnp.minimum((i + 1) * hb, last_h), p)),
                  pl.BlockSpec((4, hv), lambda p, i: (0, p)),
                  pl.BlockSpec((None, GDN_H, GDN_T, LANES), lambda p, i: (p, 0, i, 0)),
                  pl.BlockSpec((None, GDN_H, HALO, LANES), lambda p, i: (p, 0, jnp.minimum((i + 1) * hb, last_h), 0))],
        out_specs=[pl.BlockSpec((GDN_T, hv), lambda p, i: (i, p)),
                   pl.BlockSpec((4, hv), lambda p, i: (0, p))],
        out_shape=[SDS((S, 3 * hv), BF16), SDS((4, 3 * hv), F32)],
        compiler_params=_cp(2),
    )(proj, proj, proj, conv_w, dqkv, dqkv)


_DIMS = {"nn": ((1,), (0,)), "nt": ((1,), (1,)), "tn": ((0,), (0,))}


def _mm_raw(a, b, mode, hi):
    if hi:
        return _dot_hi(a, b, _DIMS[mode])
    return lax.dot_general(a.astype(BF16), b.astype(BF16), (_DIMS[mode], ((), ())), preferred_element_type=F32)


@functools.partial(jax.custom_vjp, nondiff_argnums=(2, 3))
def mm(a, b, mode, hi):
    return _mm_raw(a, b, mode, hi)


def _mm_fwd(a, b, mode, hi):
    return _mm_raw(a, b, mode, hi), (a, b)


def _mm_bwd(mode, hi, res, dc):
    a, b = res
    if mode == "nn":
        da, db = mm(dc, b, "nt", hi), mm(a, dc, "tn", hi)
    elif mode == "nt":
        da, db = mm(dc, b, "nn", hi), mm(dc, a, "tn", hi)
    else:
        da, db = mm(b, dc, "nt", hi), mm(a, dc, "nn", hi)
    return da, db


mm.defvjp(_mm_fwd, _mm_bwd)


TRI_BASE = 8


def _unit_lower_inverses(Ls):
    n = Ls[0].shape[0]
    r = lax.broadcasted_iota(jnp.int32, (n, n), 0)
    c = lax.broadcasted_iota(jnp.int32, (n, n), 1)
    eye = jnp.where(r == c, 1.0, 0.0).astype(F32)
    base = r // TRI_BASE == c // TRI_BASE
    one = lambda a, b_: _mm_raw(a, b_, "nn", False)
    Ps = [jnp.where(base, -L, 0.0) for L in Ls]
    invs = [eye + P for P in Ps]
    k = 1
    while 2 * k < TRI_BASE:
        Ps = [one(P, P) for P in Ps]
        invs = [inv + one(inv, P) for inv, P in zip(invs, Ps)]
        k *= 2
    b = 2 * TRI_BASE
    while b <= n:
        off_mask = (r // b == c // b) & ((r % b) >= b // 2) & ((c % b) < b // 2)
        ts = [one(inv, jnp.where(off_mask, L, 0.0)) for inv, L in zip(invs, Ls)]
        invs = [inv - one(t, inv) for inv, t in zip(invs, ts)]
        b *= 2
    resid = [eye - inv - _dot_x3(L, inv) for inv, L in zip(invs, Ls)]
    return [inv + _dot_x3(inv, rs) for inv, rs in zip(invs, resid)]


@jax.custom_vjp
def tri_apply(invs, Ls, r1s, r2s):
    return [_mm_raw(i, r, "nn", False) for i, r in zip(invs, r1s)], [_mm_raw(i, r, "nn", False) for i, r in zip(invs, r2s)]


def _tri_fwd(invs, Ls, r1s, r2s):
    s1s = [_mm_raw(i, r, "nn", False) for i, r in zip(invs, r1s)]
    s2s = [_mm_raw(i, r, "nn", False) for i, r in zip(invs, r2s)]
    return (s1s, s2s), (invs, s1s, s2s)


def _tri_bwd(res, ds):
    invs, s1s, s2s = res
    d1s = [_mm_raw(i, d, "tn", False) for i, d in zip(invs, ds[0])]
    d2s = [_mm_raw(i, d, "tn", False) for i, d in zip(invs, ds[1])]
    dLs = [-(_mm_raw(d1, s1, "nt", False) + _mm_raw(d2, s2, "nt", False)) for d1, s1, d2, s2 in zip(d1s, s1s, d2s, s2s)]
    return [jnp.zeros_like(i) for i in invs], dLs, d1s, d2s


tri_apply.defvjp(_tri_fwd, _tri_bwd)


def _gdn_chunk(qs, ks, vs, gcbs, btbs, Ss, invs=None):
    C = qs[0].shape[0]
    r = lax.broadcasted_iota(jnp.int32, (C, C), 0)
    c = lax.broadcasted_iota(jnp.int32, (C, C), 1)
    causal, strict = c <= r, c < r
    rows = lax.broadcasted_iota(jnp.int32, gcbs[0].shape, 0)
    Gs = [g[:, :C] for g in gcbs]
    decays = [jnp.exp(jnp.where(causal, G - G.T, NEG)) for G in Gs]
    kbs = [k * b for k, b in zip(ks, btbs)]
    vbs = [v * b for v, b in zip(vs, btbs)]
    Ls = [jnp.where(strict, mm(kb, k, "nt", False) * d, 0.0) for kb, k, d in zip(kbs, ks, decays)]
    egs = [jnp.exp(g) for g in gcbs]
    if invs is None:
        invs = _unit_lower_inverses(Ls)
    us, ws = tri_apply(invs, Ls, vbs, [kb * eg for kb, eg in zip(kbs, egs)])
    qks = [jnp.where(causal, mm(q, k, "nt", False) * d, 0.0) for q, k, d in zip(qs, ks, decays)]
    g_lasts = [jnp.sum(jnp.where(rows == C - 1, g, 0.0), axis=0, keepdims=True) for g in gcbs]
    q_decs = [q * eg for q, eg in zip(qs, egs)]
    k_decs = [k * jnp.exp(gl - g) for k, gl, g in zip(ks, g_lasts, gcbs)]
    v_news = [u - mm(w, S, "nn", False) for u, w, S in zip(us, ws, Ss)]
    os_ = [mm(qd, S, "nn", False) + mm(qk, vn, "nn", False) for qd, S, qk, vn in zip(q_decs, Ss, qks, v_news)]
    S_news = [S * jnp.exp(gl) + mm(kd, vn, "tn", False) for S, gl, kd, vn in zip(Ss, g_lasts, k_decs, v_news)]
    return os_, S_news, invs


INV_CHUNKS = 4
SCAN_CHUNKS = 4


def gdn_inverses(qkv, gc, bt, S):
    nchunk = S // GDN_C
    rows = INV_CHUNKS * GDN_C

    def body(k_ref, g_ref, b_ref, inv_ref):
        items = [(h, m) for m in range(INV_CHUNKS) for h in range(GDN_H)]
        r = lax.broadcasted_iota(jnp.int32, (GDN_C, GDN_C), 0)
        c = lax.broadcasted_iota(jnp.int32, (GDN_C, GDN_C), 1)
        sl = lambda m: slice(m * GDN_C, (m + 1) * GDN_C)
        ks = [k_ref[h, sl(m), :].astype(F32) for h, m in items]
        Gs = [g_ref[h, sl(m), :GDN_C] for h, m in items]
        kbs = [k * b_ref[h, sl(m), :] for k, (h, m) in zip(ks, items)]
        decays = [jnp.exp(jnp.where(c <= r, G - G.T, NEG)) for G in Gs]
        Ls = [jnp.where(c < r, _mm_raw(kb, k, "nt", False) * d, 0.0) for kb, k, d in zip(kbs, ks, decays)]
        for (h, m), inv in zip(items, _unit_lower_inverses(Ls)):
            inv_ref[h, m] = inv

    hb = pl.BlockSpec((GDN_H, rows, LANES), lambda n: (0, n, 0))
    return pl.pallas_call(
        body, name="gdn_inverses", grid=(nchunk // INV_CHUNKS,),
        in_specs=[pl.BlockSpec((None, GDN_H, rows, LANES), lambda n: (1, 0, n, 0)), hb, hb],
        out_specs=pl.BlockSpec((GDN_H, INV_CHUNKS, GDN_C, GDN_C), lambda n: (0, n, 0, 0)),
        out_shape=SDS((GDN_H, nchunk, GDN_C, GDN_C), F32),
        compiler_params=_cp(1),
    )(qkv, gc, bt)


def gdn_core(qkv, gc, bt, invs, S):
    nchunk = S // GDN_C

    def body(qkv_ref, g_ref, b_ref, inv_ref, o_ref, st_ref, s_scr):
        n = pl.program_id(0)

        @pl.when(n == 0)
        def _():
            s_scr[...] = jnp.zeros_like(s_scr)

        heads = range(GDN_H)
        S_cur = [s_scr[h] for h in heads]
        for m in range(SCAN_CHUNKS):
            sl = slice(m * GDN_C, (m + 1) * GDN_C)
            os_, S_new, _ = _gdn_chunk(*[[qkv_ref[p, h, sl, :].astype(F32) for h in heads] for p in range(3)],
                                       [g_ref[h, sl, :] for h in heads], [b_ref[h, sl, :] for h in heads], S_cur,
                                       invs=[inv_ref[h, m] for h in heads])
            for h in heads:
                st_ref[h, m] = S_cur[h].astype(st_ref.dtype)
                o_ref[h, sl, :] = os_[h]
            S_cur = S_new
        for h in heads:
            s_scr[h] = S_cur[h]

    rows = SCAN_CHUNKS * GDN_C
    blk3 = pl.BlockSpec((3, GDN_H, rows, LANES), lambda n: (0, 0, n, 0))
    hb = pl.BlockSpec((GDN_H, rows, LANES), lambda n: (0, n, 0))
    return pl.pallas_call(
        body, name="gdn_core", grid=(nchunk // SCAN_CHUNKS,),
        in_specs=[blk3, hb, hb, pl.BlockSpec((GDN_H, SCAN_CHUNKS, GDN_C, GDN_C), lambda n: (0, n, 0, 0))],
        out_specs=[hb, pl.BlockSpec((GDN_H, SCAN_CHUNKS, GDN_DK, LANES), lambda n: (0, n, 0, 0))],
        out_shape=[SDS((GDN_H, S, LANES), F32), SDS((GDN_H, nchunk, GDN_DK, LANES), BF16)],
        scratch_shapes=[pltpu.VMEM((GDN_H, GDN_DK, LANES), F32)],
        compiler_params=_cp(1),
    )(qkv, gc, bt, invs)


def gdn_core_bwd(qkv, gc, bt, states, invs, do, S):
    nchunk = S // GDN_C

    def body(qkv_ref, g_ref, b_ref, st_ref, inv_ref, do_ref, dqkv_ref, dg_ref, db_ref, ds_scr):
        n = pl.program_id(0)

        @pl.when(n == 0)
        def _():
            ds_scr[...] = jnp.zeros_like(ds_scr)

        heads = range(GDN_H)
        dS_cur = [ds_scr[h] for h in heads]
        for m in reversed(range(SCAN_CHUNKS)):
            sl = slice(m * GDN_C, (m + 1) * GDN_C)
            saved = [inv_ref[h, m] for h in heads]
            _, vjp = jax.vjp(lambda *a, _s=saved: _gdn_chunk(*a, invs=_s)[:2],
                             *[[qkv_ref[p, h, sl, :].astype(F32) for h in heads] for p in range(3)],
                             [g_ref[h, sl, :] for h in heads], [b_ref[h, sl, :] for h in heads],
                             [st_ref[h, m].astype(F32) for h in heads])
            dq, dk, dv, dg, db, dS_cur = vjp(([do_ref[h, sl, :] for h in heads], dS_cur))
            for h in heads:
                dqkv_ref[0, h, sl, :] = dq[h]
                dqkv_ref[1, h, sl, :] = dk[h]
                dqkv_ref[2, h, sl, :] = dv[h]
                dg_ref[h, sl, :] = dg[h]
                db_ref[h, sl, :] = db[h]
        for h in heads:
            ds_scr[h] = dS_cur[h]

    nblk = nchunk // SCAN_CHUNKS
    rows = SCAN_CHUNKS * GDN_C
    rev = lambda n: nblk - 1 - n
    blk3 = pl.BlockSpec((3, GDN_H, rows, LANES), lambda n: (0, 0, rev(n), 0))
    hb = pl.BlockSpec((GDN_H, rows, LANES), lambda n: (0, rev(n), 0))
    return pl.pallas_call(
        body, name="gdn_core_bwd", grid=(nblk,),
        in_specs=[blk3, hb, hb, pl.BlockSpec((GDN_H, SCAN_CHUNKS, GDN_DK, LANES), lambda n: (0, rev(n), 0, 0)),
                  pl.BlockSpec((GDN_H, SCAN_CHUNKS, GDN_C, GDN_C), lambda n: (0, rev(n), 0, 0)), hb],
        out_specs=[blk3, hb, hb],
        out_shape=[SDS((3, GDN_H, S, LANES), F32), SDS((GDN_H, S, LANES), F32), SDS((GDN_H, S, LANES), F32)],
        scratch_shapes=[pltpu.VMEM((GDN_H, GDN_DK, LANES), F32)],
        compiler_params=_cp(1),
    )(qkv, gc, bt, states, invs, do)


GDN_MAIN = 4 * GDN_H * LANES
GDN_PROJ = GDN_MAIN + LANES
RT = 512


def gdn_forward(h, w_in, conv_w, alog, dtb, out_gain, w_out):
    S = h.shape[0]
    nt = S // RT
    proj = matmul(h, w_in, "nn", BF16, "gdn_in")
    qkv = gdn_pre(proj, conv_w, S)
    ab_row = Row(proj, (RT, LANES), lambda i: (i, GDN_MAIN // LANES), gdtype=BF16, gshape=(S, LANES), gimap=lambda i: (i, 0))
    hm = lambda i: (0, i, 0)
    hv = GDN_H * LANES
    gc, bt = rowwise(f_gdn_gates, [ab_row], [alog, dtb],
                     [Out((GDN_H, S, LANES), F32, (GDN_H, RT, LANES), hm, lead=GDN_H)] * 2, (nt,), "gdn_gates")
    invs = gdn_inverses(qkv, gc, bt, S)
    o, states = gdn_core(qkv, gc, bt, invs, S)
    o_row = Row(o, (GDN_H, RT, LANES), hm, lead=GDN_H)
    z_row = Row(proj, (RT, hv), lambda i: (i, 3), splits=[LANES] * GDN_H, gdtype=BF16, gshape=(S, hv), gimap=lambda i: (i, 0))
    (on,) = rowwise(f_gdn_post, [o_row, z_row], [out_gain],
                    [Out((S, hv), BF16, (RT, hv), lambda i: (i, 0), splits=[LANES] * GDN_H)], (nt,), "gdn_post")
    y = matmul(on, w_out, "nn", BF16, "gdn_out")
    saved = dict(h=h, proj=proj, qkv=qkv, gc=gc, bt=bt, states=states, invs=invs, o=o, on=on, ab_row=ab_row, o_row=o_row, z_row=z_row)
    return y, saved


def gdn_backward(dy, sv, w_in, conv_w, alog, dtb, out_gain, w_out, on_weight_grads=None):
    S = dy.shape[0]
    nt = S // RT
    hm = lambda i: (0, i, 0)
    hv = GDN_H * LANES
    don = matmul(dy, w_out, "nt", BF16, "gdn_out_dx")
    d_w_out = matmul(sv["on"], dy, "tn", F32, "gdn_out_dw")
    (do, dz), (d_gain,) = rowwise_bwd(f_gdn_post, [sv["o_row"], sv["z_row"]], [out_gain],
                                      [Row(don, (RT, hv), lambda i: (i, 0), splits=[LANES] * GDN_H)], (nt,), "gdn_post_bwd")
    dqkv, dgc, dbt = gdn_core_bwd(sv["qkv"], sv["gc"], sv["bt"], sv["states"], sv["invs"], do, S)
    head_blk = lambda a: Row(a, (GDN_H, RT, LANES), hm, lead=GDN_H)
    (dab,), (d_alog, d_dtb) = rowwise_bwd(f_gdn_gates, [sv["ab_row"]], [alog, dtb], [head_blk(dgc), head_blk(dbt)],
                                          (nt,), "gdn_gates_bwd")
    dqkv_proj, d_conv = gdn_pre_bwd(sv["proj"], conv_w, dqkv, S)
    dproj = jnp.concatenate([dqkv_proj, dz, dab], axis=1)
    d_w_in = matmul(sv["h"], dproj, "tn", F32, "gdn_in_dw")
    if on_weight_grads is not None:
        w_in = w_in + on_weight_grads(d_w_in, d_w_out).astype(w_in.dtype)
    dh = matmul(dproj, w_in, "nt", BF16, "gdn_in_dx")
    return dh, dict(w_in=d_w_in, conv=d_conv, alog=d_alog, dtb=d_dtb, gain=d_gain, w_out=d_w_out)


QB = DSW_SPAN
N_HP = DSW_HG // LANES


def _bucket_maps():
    a = np.arange(QB)[:, None]
    j = np.arange(2 * QB)[None, :]
    dist = QB + a - j
    band = (dist >= 0) & (dist <= DSW_SPAN)
    maps = []
    for _, dil in DSW_GROUPS:
        dd = np.maximum(dist, 0) * dil
        max_exact = REL_BUCKETS // 2
        scaled = np.log(np.maximum(dd, 1).astype(np.float32) / np.float32(max_exact)) / np.float32(math.log(REL_MAX_DIST / max_exact))
        large = max_exact + (scaled * np.float32(REL_BUCKETS - max_exact)).astype(np.int32)
        large = np.minimum(large, REL_BUCKETS - 1)
        maps.append(np.where(dd < max_exact, dd, large).astype(np.int32))
    return np.stack(maps), band


def dsw_bias(rel_bias):
    maps, band = _bucket_maps()
    maps = np.where(band[None], maps, -1).astype(np.int32)

    def body(tab_ref, bk_ref, o_ref):
        gh = pl.program_id(0)
        bk = bk_ref[...]
        acc = jnp.full(bk.shape, NEG, F32)
        for b in range(REL_BUCKETS):
            acc = jnp.where(bk == b, tab_ref[b, gh], acc)
        o_ref[...] = acc

    return pl.pallas_call(
        body, name="dsw_bias", grid=(3 * GDN_H,),
        in_specs=[pl.BlockSpec(memory_space=pltpu.SMEM),
                  pl.BlockSpec((None, QB, 2 * QB), lambda gh: (gh // GDN_H, 0, 0))],
        out_specs=pl.BlockSpec((None, QB, 2 * QB), lambda gh: (gh, 0, 0)),
        out_shape=SDS((3 * GDN_H, QB, 2 * QB), F32),
        compiler_params=_cp(1),
    )(rel_bias, jnp.asarray(maps))


def dsw_bias_grad(dbias):
    maps, band = _bucket_maps()
    maps = np.where(band[None], maps, -1).astype(np.int32)

    def body(d_ref, bk_ref, o_ref):
        bk = bk_ref[...]
        d = d_ref[...]
        rows = lax.broadcasted_iota(jnp.int32, (REL_BUCKETS, LANES), 0)
        acc = jnp.zeros((REL_BUCKETS, LANES), F32)
        for b in range(REL_BUCKETS):
            part = jnp.sum(jnp.where(bk == b, d, 0.0), axis=0, keepdims=True)
            val = jnp.sum(part, axis=1, keepdims=True)
            acc = jnp.where(rows == b, val, acc)
        o_ref[...] = acc

    return pl.pallas_call(
        body, name="dsw_bias_grad", grid=(3 * GDN_H,),
        in_specs=[pl.BlockSpec((None, QB, 2 * QB), lambda gh: (gh, 0, 0)),
                  pl.BlockSpec((None, QB, 2 * QB), lambda gh: (gh // GDN_H, 0, 0))],
        out_specs=pl.BlockSpec((None, REL_BUCKETS, LANES), lambda gh: (gh, 0, 0)),
        out_shape=SDS((3 * GDN_H, REL_BUCKETS, LANES), F32),
        compiler_params=_cp(1),
    )(dbias, jnp.asarray(maps))


def _nt(a, b):
    return lax.dot_general(a, b, (((1,), (1,)), ((), ())), preferred_element_type=F32)


def _tn(a, b):
    return lax.dot_general(a, b, (((0,), (0,)), ((), ())), preferred_element_type=F32)


N_LB = DSW_HG // LANES
HALF = DSW_DH // 2


def _lanes(j):
    return slice(LANES * j, LANES * (j + 1))


def _deinterleave(stage, out_ref, dil, rows, dtype):
    for r in range(dil):
        for j in range(N_LB):
            out_ref[r, :, _lanes(j)] = stage[j, pl.ds(r, rows, stride=dil), :].astype(dtype)


def _interleave(in_ref, stage, dil, rows):
    for r in range(dil):
        for j in range(N_LB):
            stage[j, pl.ds(r, rows, stride=dil), :] = in_ref[r, :, _lanes(j)].astype(F32)


def dsw_prep(proj, q_gain2, k_gain2, gi, S):
    dil = DSW_GROUPS[gi][1]
    nt, rows = S // RT, RT // dil

    def body(q_ref, k_ref, v_ref, qg_ref, kg_ref, qo_ref, ko_ref, vo_ref, stage):
        for src, gain_ref, scale, dst in ((q_ref, qg_ref, DSW_DH ** -0.5, qo_ref), (k_ref, kg_ref, 1.0, ko_ref), (v_ref, None, None, vo_ref)):
            for j in range(N_LB):
                val = src[:, _lanes(j)].astype(F32)
                val = val if gain_ref is None else _qknorm1(val, gain_ref[...], scale)
                if dil == 1:
                    dst[0, :, _lanes(j)] = val.astype(BF16)
                else:
                    stage[j] = val
            if dil > 1:
                _deinterleave(stage, dst, dil, rows, BF16)

    col = lambda which: pl.BlockSpec((RT, DSW_HG), lambda i, _c=which * 3 + gi: (i, _c))
    gspec = pl.BlockSpec((1, LANES), lambda i: (0, 0))
    ospec = pl.BlockSpec((dil, rows, DSW_HG), lambda i: (0, i, 0))
    return pl.pallas_call(
        body, name=f"dsw_prep_g{gi}", grid=(nt,),
        in_specs=[col(0), col(1), col(2), gspec, gspec], out_specs=[ospec] * 3,
        out_shape=[SDS((dil, S // dil, DSW_HG), BF16)] * 3,
        scratch_shapes=[pltpu.VMEM((N_LB, RT, LANES), F32)], compiler_params=_cp(1),
    )(proj, proj, proj, q_gain2, k_gain2)


def dsw_prep_bwd(proj, q_gain2, k_gain2, dqd, dkd, dvd, gi, S):
    dil = DSW_GROUPS[gi][1]
    nt, rows = S // RT, RT // dil

    def body(q_ref, k_ref, qg_ref, kg_ref, dq_ref, dk_ref, dv_ref, oq_ref, ok_ref, ov_ref, dqg_ref, dkg_ref, stage):
        i = pl.program_id(0)

        @pl.when(i == 0)
        def _():
            dqg_ref[...] = jnp.zeros_like(dqg_ref)
            dkg_ref[...] = jnp.zeros_like(dkg_ref)

        for src, gain_ref, scale, cot_ref, dst, dg_ref in ((q_ref, qg_ref, DSW_DH ** -0.5, dq_ref, oq_ref, dqg_ref),
                                                          (k_ref, kg_ref, 1.0, dk_ref, ok_ref, dkg_ref)):
            if dil > 1:
                _interleave(cot_ref, stage, dil, rows)
            for j in range(N_LB):
                _, vjp = jax.vjp(lambda x, g, _s=scale: _qknorm1(x, g, _s), src[:, _lanes(j)].astype(F32), gain_ref[...])
                dx, dg = vjp(stage[j] if dil > 1 else cot_ref[0, :, _lanes(j)].astype(F32))
                dst[:, _lanes(j)] = dx.astype(dst.dtype)
                dg_ref[...] += dg
        if dil > 1:
            _interleave(dv_ref, stage, dil, rows)
        for j in range(N_LB):
            ov_ref[:, _lanes(j)] = (stage[j] if dil > 1 else dv_ref[0, :, _lanes(j)]).astype(ov_ref.dtype)

    col = lambda which: pl.BlockSpec((RT, DSW_HG), lambda i, _c=which * 3 + gi: (i, _c))
    gspec = pl.BlockSpec((1, LANES), lambda i: (0, 0))
    dspec = pl.BlockSpec((dil, rows, DSW_HG), lambda i: (0, i, 0))
    nspec = pl.BlockSpec((RT, DSW_HG), lambda i: (i, 0))
    return pl.pallas_call(
        body, name=f"dsw_prep_bwd_g{gi}", grid=(nt,),
        in_specs=[col(0), col(1), gspec, gspec, dspec, dspec, dspec], out_specs=[nspec] * 3 + [gspec] * 2,
        out_shape=[SDS((S, DSW_HG), BF16)] * 3 + [SDS((1, LANES), F32)] * 2,
        scratch_shapes=[pltpu.VMEM((N_LB, RT, LANES), F32)], compiler_params=_cp(1),
    )(proj, proj, q_gain2, k_gain2, dqd, dkd, dvd)


def _head_masks(rows):
    lane = lax.broadcasted_iota(jnp.int32, (rows, LANES), 1)
    return lane < DSW_DH, (lane % DSW_DH) < HALF


def dsw_attn_fwd(qd, kd, vd, bias, gi, S):
    dil = DSW_GROUPS[gi][1]
    sd = S // dil
    nq = sd // QB
    QPS = max(b for b in (1, 2, 4) if nq % b == 0)

    def body(q_ref, k_ref, v_ref, b_ref, o_ref, l_ref, kp_scr, vp_scr):
        i = pl.program_id(1)

        @pl.when(i == 0)
        def _():
            kp_scr[...] = jnp.zeros_like(kp_scr)
            vp_scr[...] = jnp.zeros_like(vp_scr)

        lo_q, _ = _head_masks(QB)
        lo_k, _ = _head_masks(2 * QB)
        col = lax.broadcasted_iota(jnp.int32, (QB, 2 * QB), 1)
        first = jnp.logical_and(i == 0, col < QB)
        hps, heads = range(N_HP), range(2 * N_HP)
        mqs = [lo_q if h % 2 == 0 else jnp.logical_not(lo_q) for h in heads]
        mks = [lo_k if h % 2 == 0 else jnp.logical_not(lo_k) for h in heads]
        k_last, v_last = k_ref[(QPS - 1) * QB:, :], v_ref[(QPS - 1) * QB:, :]
        for m in range(QPS):
            rows = slice(m * QB, (m + 1) * QB)
            before = slice((m - 1) * QB, m * QB)
            kps = [kp_scr[:, _lanes(hp)] if m == 0 else k_ref[before, _lanes(hp)] for hp in hps]
            vps = [vp_scr[:, _lanes(hp)] if m == 0 else v_ref[before, _lanes(hp)] for hp in hps]
            k2s = [jnp.concatenate([kps[hp], k_ref[rows, _lanes(hp)]], axis=0) for hp in hps]
            v2s = [jnp.concatenate([vps[hp], v_ref[rows, _lanes(hp)]], axis=0) for hp in hps]
            qs = [q_ref[rows, _lanes(hp)] for hp in hps]
            ss = [_nt(jnp.where(mqs[h], qs[h // 2], 0).astype(BF16), k2s[h // 2]) + b_ref[h] for h in heads]
            if m == 0:
                ss = [jnp.where(first, NEG, s) for s in ss]
            mxs = [jnp.max(s, axis=1, keepdims=True) for s in ss]
            ps = [jnp.exp(s - mx) for s, mx in zip(ss, mxs)]
            ls = [jnp.sum(p, axis=1, keepdims=True) for p in ps]
            ohs = [jnp.dot(ps[h].astype(BF16), jnp.where(mks[h], v2s[h // 2], 0).astype(BF16), preferred_element_type=F32) / ls[h]
                   for h in heads]
            lse_h = [mx + jnp.log(l) for mx, l in zip(mxs, ls)]
            for hp in hps:
                o_ref[rows, _lanes(hp)] = ohs[2 * hp] + ohs[2 * hp + 1]
                l_ref[rows, _lanes(hp)] = jnp.where(lo_q, lse_h[2 * hp], lse_h[2 * hp + 1])
        kp_scr[...] = k_last
        vp_scr[...] = v_last

    blk = pl.BlockSpec((None, QPS * QB, DSW_HG), lambda r, i: (r, i, 0))
    return pl.pallas_call(
        body, name=f"dsw_attn_g{gi}", grid=(dil, nq // QPS),
        in_specs=[blk, blk, blk, pl.BlockSpec((GDN_H, QB, 2 * QB), lambda r, i: (gi, 0, 0))],
        out_specs=[blk, blk], out_shape=[SDS((dil, sd, DSW_HG), F32)] * 2,
        scratch_shapes=[pltpu.VMEM((QB, DSW_HG), BF16)] * 2, compiler_params=_cp(2),
    )(qd, kd, vd, bias)


def dsw_attn_bwd(qd, kd, vd, bias, dod, statd, gi, S):
    dil = DSW_GROUPS[gi][1]
    sd = S // dil
    nq = sd // QB
    B = max(b for b in (1, 2, 4) if nq % b == 0)
    nb = nq // B
    steps = dil * nb
    cur = lambda t: jnp.minimum(t, steps - 1)
    done = lambda t: jnp.maximum(t - 1, 0)
    last = slice((B - 1) * QB, B * QB)

    def body(q_ref, k_ref, v_ref, b_ref, do_ref, st_ref, dq_ref, dk_ref, dv_ref, db_ref, kp_scr, vp_scr, dk_scr, dv_scr):
        t = pl.program_id(0)

        @pl.when(t == 0)
        def _():
            db_ref[...] = jnp.zeros_like(db_ref)
            for scr in (kp_scr, vp_scr, dk_scr, dv_scr):
                scr[...] = jnp.zeros_like(scr)

        @pl.when(t < steps)
        def _():
            lo_q, first_half = _head_masks(QB)
            col = lax.broadcasted_iota(jnp.int32, (QB, 2 * QB), 1)
            first = jnp.logical_and(t % nb == 0, col < QB)
            hps, heads = range(N_HP), range(2 * N_HP)
            mqs = [lo_q if h % 2 == 0 else jnp.logical_not(lo_q) for h in heads]
            k_last, v_last = k_ref[last, :], v_ref[last, :]
            to_prev, to_cur = [], []
            for m in range(B):
                rows = slice(m * QB, (m + 1) * QB)
                before = slice((m - 1) * QB, m * QB)
                kps = [kp_scr[:, _lanes(hp)] if m == 0 else k_ref[before, _lanes(hp)] for hp in hps]
                vps = [vp_scr[:, _lanes(hp)] if m == 0 else v_ref[before, _lanes(hp)] for hp in hps]
                k2s = [jnp.concatenate([kps[hp], k_ref[rows, _lanes(hp)]], axis=0) for hp in hps]
                v2s = [jnp.concatenate([vps[hp], v_ref[rows, _lanes(hp)]], axis=0) for hp in hps]
                qs = [q_ref[rows, _lanes(hp)] for hp in hps]
                douts = [do_ref[rows, _lanes(hp)] for hp in hps]
                stats = [st_ref[rows, _lanes(hp)] for hp in hps]
                qms = [jnp.where(mqs[h], qs[h // 2], 0).astype(BF16) for h in heads]
                doms = [jnp.where(mqs[h], douts[h // 2], 0).astype(BF16) for h in heads]
                ss = [_nt(qms[h], k2s[h // 2]) + b_ref[h] for h in heads]
                if m == 0:
                    ss = [jnp.where(first, NEG, s) for s in ss]
                lses = [jnp.max(jnp.where(jnp.logical_and(mqs[h], first_half), stats[h // 2], NEG), axis=1, keepdims=True) for h in heads]
                deltas = [jnp.max(jnp.where(jnp.logical_and(mqs[h], jnp.logical_not(first_half)), stats[h // 2], NEG), axis=1,
                                  keepdims=True) for h in heads]
                ps = [jnp.exp(ss[h] - lses[h]) for h in heads]
                dss = [ps[h] * (_nt(doms[h], v2s[h // 2]) - deltas[h]) for h in heads]
                dsbs = [d.astype(BF16) for d in dss]
                dqh = [jnp.where(mqs[h], jnp.dot(dsbs[h], k2s[h // 2], preferred_element_type=F32), 0.0) for h in heads]
                dkh = [_tn(dsbs[h], qms[h]) for h in heads]
                dvh = [_tn(ps[h].astype(BF16), doms[h]) for h in heads]
                for h in heads:
                    db_ref[h] += dss[h]
                for hp in hps:
                    dq_ref[rows, _lanes(hp)] = dqh[2 * hp] + dqh[2 * hp + 1]
                dk2s = [dkh[2 * hp] + dkh[2 * hp + 1] for hp in hps]
                dv2s = [dvh[2 * hp] + dvh[2 * hp + 1] for hp in hps]
                to_prev.append(([d[:QB] for d in dk2s], [d[:QB] for d in dv2s]))
                to_cur.append(([d[QB:] for d in dk2s], [d[QB:] for d in dv2s]))
            for hp in hps:
                if B > 1:
                    dk_ref[:(B - 1) * QB, _lanes(hp)] = dk_scr[:(B - 1) * QB, _lanes(hp)]
                    dv_ref[:(B - 1) * QB, _lanes(hp)] = dv_scr[:(B - 1) * QB, _lanes(hp)].astype(dv_ref.dtype)
                dk_ref[last, _lanes(hp)] = dk_scr[last, _lanes(hp)] + to_prev[0][0][hp]
                dv_ref[last, _lanes(hp)] = (dv_scr[last, _lanes(hp)] + to_prev[0][1][hp]).astype(dv_ref.dtype)
            for hp in hps:
                for m in range(B):
                    rows = slice(m * QB, (m + 1) * QB)
                    nk, nv = to_cur[m][0][hp], to_cur[m][1][hp]
                    if m < B - 1:
                        nk, nv = nk + to_prev[m + 1][0][hp], nv + to_prev[m + 1][1][hp]
                    dk_scr[rows, _lanes(hp)] = nk
                    dv_scr[rows, _lanes(hp)] = nv
            kp_scr[...] = k_last
            vp_scr[...] = v_last

        @pl.when(t == steps)
        def _():
            dk_ref[...] = dk_scr[...]
            dv_ref[...] = dv_scr[...].astype(dv_ref.dtype)

    blk = pl.BlockSpec((None, B * QB, DSW_HG), lambda t: (cur(t) // nb, cur(t) % nb, 0))
    oblk = pl.BlockSpec((None, B * QB, DSW_HG), lambda t: (done(t) // nb, done(t) % nb, 0))
    return pl.pallas_call(
        body, name=f"dsw_attn_bwd_g{gi}", grid=(steps + 1,),
        in_specs=[blk, blk, blk, pl.BlockSpec((GDN_H, QB, 2 * QB), lambda t: (gi, 0, 0)), blk, blk],
        out_specs=[blk, oblk, oblk, pl.BlockSpec((GDN_H, QB, 2 * QB), lambda t: (0, 0, 0))],
        out_shape=[SDS((dil, sd, DSW_HG), F32), SDS((dil, sd, DSW_HG), F32), SDS((dil, sd, DSW_HG), BF16),
                   SDS((GDN_H, QB, 2 * QB), F32)],
        scratch_shapes=[pltpu.VMEM((QB, DSW_HG), BF16)] * 2 + [pltpu.VMEM((B * QB, DSW_HG), F32)] * 2,
        compiler_params=_cp(1),
    )(qd, kd, vd, bias, dod, statd)


def dsw_combine(ods, lseds, S):
    nt = S // RT
    dils = [d for _, d in DSW_GROUPS]

    def body(*refs):
        ins, (o_ref, l_ref), stages = refs[:6], refs[6:8], refs[8:]
        for g in range(3):
            if dils[g] > 1:
                _interleave(ins[g], stages[g], dils[g], RT // dils[g])
                _interleave(ins[3 + g], stages[3 + g], dils[g], RT // dils[g])
        for j in range(N_LB):
            natural = lambda a: stages[a][j] if dils[a % 3] > 1 else ins[a][0, :, _lanes(j)]
            o, lse = f_combine(None, *[natural(a) for a in range(6)])
            o_ref[:, _lanes(j)] = o.astype(o_ref.dtype)
            l_ref[:, _lanes(j)] = lse

    dspec = lambda d: pl.BlockSpec((d, RT // d, DSW_HG), lambda i: (0, i, 0))
    nspec = pl.BlockSpec((RT, DSW_HG), lambda i: (i, 0))
    return pl.pallas_call(
        body, name="dsw_combine", grid=(nt,),
        in_specs=[dspec(d) for d in dils] * 2, out_specs=[nspec, nspec],
        out_shape=[SDS((S, DSW_HG), BF16), SDS((S, DSW_HG), F32)],
        scratch_shapes=[pltpu.VMEM((N_LB, RT, LANES), F32)] * 6, compiler_params=_cp(1),
    )(*ods, *lseds)


def dsw_bwd_prep(do, o, lse, S):
    nt = S // RT
    dils = [d for _, d in DSW_GROUPS]

    def body(do_ref, o_ref, l_ref, *rest):
        outs, (st_do, st_stat) = rest[:6], rest[6:]
        lo, first_half = _head_masks(RT)
        for j in range(N_LB):
            dout = do_ref[:, _lanes(j)]
            prod = dout * o_ref[:, _lanes(j)].astype(F32)
            s_all = jnp.sum(prod, axis=1, keepdims=True)
            s_lo = jnp.sum(jnp.where(lo, prod, 0.0), axis=1, keepdims=True)
            delta = jnp.where(lo, s_lo, s_all - s_lo)
            stat = jnp.where(first_half, l_ref[:, _lanes(j)], delta)
            st_do[j] = dout
            st_stat[j] = stat
            for g in range(3):
                if dils[g] == 1:
                    outs[g][0, :, _lanes(j)] = dout.astype(BF16)
                    outs[3 + g][0, :, _lanes(j)] = stat
        for g in range(3):
            if dils[g] > 1:
                _deinterleave(st_do, outs[g], dils[g], RT // dils[g], BF16)
                _deinterleave(st_stat, outs[3 + g], dils[g], RT // dils[g], F32)

    nspec = pl.BlockSpec((RT, DSW_HG), lambda i: (i, 0))
    dspec = lambda d: pl.BlockSpec((d, RT // d, DSW_HG), lambda i: (0, i, 0))
    res = pl.pallas_call(
        body, name="dsw_bwd_prep", grid=(nt,),
        in_specs=[nspec] * 3, out_specs=[dspec(d) for d in dils] * 2,
        out_shape=[SDS((d, S // d, DSW_HG), BF16) for d in dils] + [SDS((d, S // d, DSW_HG), F32) for d in dils],
        scratch_shapes=[pltpu.VMEM((N_LB, RT, LANES), F32)] * 2, compiler_params=_cp(1),
    )(do, o, lse)
    return res[:3], res[3:]


def dsw_forward(h, w_in, q_gain2, k_gain2, rel_bias, w_out):
    S = h.shape[0]
    proj = matmul(h, w_in, "nn", BF16, "dsw_in", col_shards=N_SHARD)
    bias = dsw_bias(rel_bias)
    qkv, ods, lseds = [], [], []
    for gi in range(3):
        qd, kd, vd = dsw_prep(proj, q_gain2, k_gain2, gi, S)
        od, ld = dsw_attn_fwd(qd, kd, vd, bias, gi, S)
        qkv.append((qd, kd, vd))
        ods.append(od)
        lseds.append(ld)
    o, lse = dsw_combine(ods, lseds, S)
    y = matmul(o, w_out, "nn", BF16, "dsw_out", col_shards=N_SHARD)
    return y, dict(h=h, proj=proj, qkv=qkv, bias=bias, o=o, lse=lse)


def dsw_backward(dy, sv, w_in, q_gain2, k_gain2, w_out):
    S = dy.shape[0]
    do = matmul(dy, w_out, "nt", F32, "dsw_out_dx", col_shards=N_SHARD)
    d_w_out = matmul(sv["o"], dy, "tn", F32, "dsw_out_dw", col_shards=N_SHARD)
    dods, statds = dsw_bwd_prep(do, sv["o"], sv["lse"], S)
    pieces_q, pieces_k, pieces_v, dbs = [], [], [], []
    d_qg = jnp.zeros((1, LANES), F32)
    d_kg = jnp.zeros((1, LANES), F32)
    for gi in range(3):
        qd, kd, vd = sv["qkv"][gi]
        dqd, dkd, dvd, db = dsw_attn_bwd(qd, kd, vd, sv["bias"], dods[gi], statds[gi], gi, S)
        dq, dk, dv, dqg, dkg = dsw_prep_bwd(sv["proj"], q_gain2, k_gain2, dqd, dkd, dvd, gi, S)
        dbs.append(db)
        pieces_q.append(dq)
        pieces_k.append(dk)
        pieces_v.append(dv)
        d_qg = d_qg + dqg
        d_kg = d_kg + dkg
    dproj = jnp.concatenate(pieces_q + pieces_k + pieces_v, axis=1)
    d_w_in = matmul(sv["h"], dproj, "tn", F32, "dsw_in_dw", col_shards=N_SHARD)
    dh = matmul(dproj, w_in, "nt", BF16, "dsw_in_dx", col_shards=N_SHARD)
    d_rel = dsw_bias_grad(jnp.concatenate(dbs, axis=0))
    return dh, dict(w_in=d_w_in, q_gain2=d_qg, k_gain2=d_kg, rel=d_rel, w_out=d_w_out)


FUSE_M = 512


def ffn_in_act(h, w_in, name):
    S = h.shape[0]
    half = FFN // 2

    def body(h_ref, wg_ref, wu_ref, gu_ref, a_ref):
        j = pl.program_id(1)
        sub = FUSE_M // 2
        for part in range(2):
            rows = slice(part * sub, (part + 1) * sub)
            hb = h_ref[rows, :]
            g = jnp.dot(hb, wg_ref[...], preferred_element_type=F32)
            u = jnp.dot(hb, wu_ref[...], preferred_element_type=F32)
            a_ref[rows, :] = (_silu(g) * u).astype(a_ref.dtype)
            for jj in range(2):
                @pl.when(j == jj)
                def _(g=g, u=u, jj=jj, rows=rows):
                    gu_ref[rows, jj * half:(jj + 1) * half] = g.astype(gu_ref.dtype)
                    gu_ref[rows, FFN + jj * half:FFN + (jj + 1) * half] = u.astype(gu_ref.dtype)

    return pl.pallas_call(
        body, name=name, grid=(S // FUSE_M, 2),
        in_specs=[pl.BlockSpec((FUSE_M, D), lambda i, j: (i, 0)),
                  pl.BlockSpec((None, D, half), lambda i, j: (j, 0, 0)),
                  pl.BlockSpec((None, D, half), lambda i, j: (j + 2, 0, 0))],
        out_specs=[pl.BlockSpec((FUSE_M, 2 * FFN), lambda i, j: (i, 0)), pl.BlockSpec((FUSE_M, half), lambda i, j: (i, j))],
        out_shape=[SDS((S, 2 * FFN), BF16), SDS((S, FFN), BF16)],
        compiler_params=_cp(2),
    )(h, w_in, w_in)


def ffn_forward(h, w_in, w_out, tag):
    gu, a = ffn_in_act(h, w_in, f"ffn_in_act_{tag}")
    f = matmul(a, w_out, "nn", BF16, f"ffn_out_{tag}")
    return f, dict(h=h, gu=gu, a=a)


def ffn_out_dx_act(df, w_out, gu, name):
    S = df.shape[0]
    half = FFN // 2

    def body(df_ref, w_ref, g_ref, u_ref, dgu_ref):
        j = pl.program_id(1)
        sub = FUSE_M // 2
        for part in range(2):
            rows = slice(part * sub, (part + 1) * sub)
            da = _nt(df_ref[rows, :], w_ref[...])
            dg, du = _swiglu_bwd((g_ref[rows, :].astype(F32), u_ref[rows, :].astype(F32)), da)
            for jj in range(2):
                @pl.when(j == jj)
                def _(dg=dg, du=du, jj=jj, rows=rows):
                    dgu_ref[rows, jj * half:(jj + 1) * half] = dg.astype(dgu_ref.dtype)
                    dgu_ref[rows, FFN + jj * half:FFN + (jj + 1) * half] = du.astype(dgu_ref.dtype)

    return pl.pallas_call(
        body, name=name, grid=(S // FUSE_M, 2),
        in_specs=[pl.BlockSpec((FUSE_M, D), lambda i, j: (i, 0)),
                  pl.BlockSpec((half, D), lambda i, j: (j, 0)),
                  pl.BlockSpec((FUSE_M, half), lambda i, j: (i, j)),
                  pl.BlockSpec((FUSE_M, half), lambda i, j: (i, j + 2))],
        out_specs=pl.BlockSpec((FUSE_M, 2 * FFN), lambda i, j: (i, 0)),
        out_shape=SDS((S, 2 * FFN), BF16),
        compiler_params=_cp(2),
    )(df, w_out, gu, gu)


def ffn_backward(df, sv, w_in, w_out, tag):
    d_w_out = matmul(sv["a"], df, "tn", F32, f"ffn_out_dw_{tag}")
    dgu = ffn_out_dx_act(df, w_out, sv["gu"], f"ffn_out_dx_act_{tag}")
    d_w_in = matmul(sv["h"], dgu, "tn", F32, f"ffn_in_dw_{tag}", col_shards=N_SHARD)
    dh = matmul(dgu, w_in, "nt", BF16, f"ffn_in_dx_{tag}", col_shards=N_SHARD)
    return dh, d_w_in, d_w_out


def f_norm_only(ids, x, gain, sc, sh):
    return (_normmod(x, gain, sc, sh),)


WT = 512


def _wide(a, **kw):
    return Row(a, (WT, D), lambda i: (i, 0), **kw)


def _wide_out(S, dtype):
    return Out((S, D), dtype, (WT, D), lambda i: (i, 0))


def adamw(w, g, m, v, name):
    shape = w.shape
    C = shape[-1]
    R = int(np.prod(shape[:-1]))
    w2, g2, m2, v2 = (a.reshape(R, C) for a in (w, g, m, v))
    br = R
    if R > 256:
        br = max(b for b in range(8, 257, 8) if R % b == 0)
    c1 = 1.0 / (1.0 - ADAM_B1 ** ADAM_STEP)
    c2 = 1.0 / (1.0 - ADAM_B2 ** ADAM_STEP)

    def body(w_ref, g_ref, m_ref, v_ref, d_ref, nm_ref, nv_ref):
        gg = g_ref[...]
        mm_ = ADAM_B1 * m_ref[...] + (1.0 - ADAM_B1) * gg
        vv = ADAM_B2 * v_ref[...] + (1.0 - ADAM_B2) * (gg * gg)
        d_ref[...] = -ADAM_LR * ((mm_ * c1) / (jnp.sqrt(vv * c2) + ADAM_EPS) + ADAM_WD * w_ref[...])
        nm_ref[...] = mm_
        nv_ref[...] = vv

    spec = pl.BlockSpec((br, C), lambda i: (i, 0))
    d, nm, nv = pl.pallas_call(
        body, name=name, grid=(R // br,), in_specs=[spec] * 4, out_specs=[spec] * 3,
        out_shape=[SDS((R, C), F32)] * 3, compiler_params=_cp(1),
    )(w2, g2, m2, v2)
    return d.reshape(shape), nm.reshape(shape), nv.reshape(shape)


def _place():
    x, y, c = lax.axis_index("x"), lax.axis_index("y"), lax.axis_index("c")
    chips = [(1 - x, y), (x, 1 - y), (1 - x, 1 - y)]
    return x, y, c, chips


def all_gather_small(blk, name):
    m_per, n = blk.shape

    def body(x_ref, out_ref, send_sems, recv_sems, local_sem):
        x, y, c, chips = _place()
        me, sibling = (x, y, c), (x, y, 1 - c)

        def rows(px, py, pc):
            return out_ref.at[pl.ds((4 * px + 2 * py + pc) * m_per, m_per), :]

        def copy(k, block, to, src=None):
            return pltpu.make_async_remote_copy(
                src_ref=rows(*block) if src is None else src, dst_ref=rows(*block),
                send_sem=send_sems.at[k], recv_sem=recv_sems.at[k], device_id=to, device_id_type=MESH)

        mine = pltpu.make_async_copy(x_ref, rows(*me), local_sem)
        mine.start()
        first = [copy(0, me, sibling, src=x_ref)]
        first += [copy(1 + j, me, (*chip, c), src=x_ref) for j, chip in enumerate(chips)]
        for cp in first:
            cp.start()
        passed = [copy(4 + j, (*chip, c), sibling) for j, chip in enumerate(chips)]
        for j, chip in enumerate(chips):
            copy(1 + j, (*chip, c), me).wait_recv()
            passed[j].start()
        copy(0, sibling, me).wait_recv()
        for j, chip in enumerate(chips):
            copy(4 + j, (*chip, 1 - c), me).wait_recv()
        for cp in first + passed:
            cp.wait_send()
        mine.wait()

    return pl.pallas_call(
        body, name=name, out_shape=SDS((N_DEV * m_per, n), blk.dtype),
        in_specs=[pl.BlockSpec(memory_space=pltpu.VMEM)], out_specs=pl.BlockSpec(memory_space=pltpu.VMEM),
        scratch_shapes=[pltpu.SemaphoreType.DMA((7,)), pltpu.SemaphoreType.DMA((7,)), pltpu.SemaphoreType.DMA],
    )(blk)


def _half(cc, rh):
    return pl.ds(pl.multiple_of(cc * rh, 16), rh)


def all_gather_shards(ws):
    n = len(ws)

    def body(*refs):
        w_refs, out_refs = refs[:n], refs[n:2 * n]
        send_sems, recv_sems, local_sems, own_sems = refs[2 * n:]
        x, y, c, chips = _place()
        sibling = (x, y, 1 - c)
        s_me = 2 * x + y

        def copy(k, src, dst, to):
            return pltpu.make_async_remote_copy(src_ref=src, dst_ref=dst, send_sem=send_sems.at[k], recv_sem=recv_sems.at[k],
                                                device_id=to, device_id_type=MESH)

        local, sends, passed = [], [], []
        for k in range(n):
            rh = ws[k].shape[0] // 2
            cp = pltpu.make_async_remote_copy(src_ref=w_refs[k], dst_ref=out_refs[k].at[s_me], send_sem=local_sems.at[k],
                                              recv_sem=own_sems.at[k], device_id=sibling, device_id_type=MESH)
            cp.start()
            local.append(cp)
            for j, chip in enumerate(chips):
                sd = copy(6 * k + j, w_refs[k].at[_half(c, rh)], out_refs[k].at[s_me, _half(c, rh)], (*chip, c))
                sd.start()
                sends.append(sd)
        for k in range(n):
            rh = ws[k].shape[0] // 2
            for j, (px, py) in enumerate(chips):
                got = out_refs[k].at[2 * px + py, _half(c, rh)]
                copy(6 * k + j, got, got, (px, py, c)).wait_recv()
                fw = copy(6 * k + 3 + j, got, got, sibling)
                fw.start()
                passed.append(fw)
        for k in range(n):
            rh = ws[k].shape[0] // 2
            for j, (px, py) in enumerate(chips):
                got = out_refs[k].at[2 * px + py, _half(1 - c, rh)]
                copy(6 * k + 3 + j, got, got, sibling).wait_recv()
        for cp in sends + passed:
            cp.wait_send()
        for cp in local:
            cp.wait()

    return pl.pallas_call(
        body, name="weights_all_gather", out_shape=[SDS((N_SHARD,) + w.shape, w.dtype) for w in ws],
        in_specs=[ANY] * n, out_specs=[ANY] * n,
        scratch_shapes=[pltpu.SemaphoreType.DMA((6 * n,)), pltpu.SemaphoreType.DMA((6 * n,)), pltpu.SemaphoreType.DMA((n,)),
                        pltpu.SemaphoreType.DMA((n,))],
    )(*ws)


def sibling_exchange(sends, name):
    n = len(sends)

    def body(*refs):
        s_refs, o_refs, send_sems, recv_sems = refs[:n], refs[n:2 * n], refs[2 * n], refs[2 * n + 1]
        x, y, c, _ = _place()
        cps = [pltpu.make_async_remote_copy(src_ref=s_refs[k], dst_ref=o_refs[k], send_sem=send_sems.at[k], recv_sem=recv_sems.at[k],
                                            device_id=(x, y, 1 - c), device_id_type=MESH) for k in range(n)]
        for cp in cps:
            cp.start()
        for cp in cps:
            cp.wait()

    return pl.pallas_call(
        body, name=name, out_shape=[SDS(s.shape, s.dtype) for s in sends], in_specs=[ANY] * n, out_specs=[ANY] * n,
        scratch_shapes=[pltpu.SemaphoreType.DMA((n,)), pltpu.SemaphoreType.DMA((n,))],
    )(*sends)


def add_rows(arrs, out_dtype, name, rt=256):
    Rr, W = arrs[0].shape

    def fn(ids, *vals):
        acc = vals[0]
        for v in vals[1:]:
            acc = acc + v
        return (acc,)

    t = rt if Rr % rt == 0 else max(b for b in range(16, rt + 1, 16) if Rr % b == 0)
    (out,) = rowwise(fn, [Row(a, (t, W), lambda i: (i, 0)) for a in arrs], [],
                     [Out((Rr, W), out_dtype, (t, W), lambda i: (i, 0))], (Rr // t,), name)
    return out


HBM_SPEC = pl.BlockSpec(memory_space=pltpu.HBM)
SEM_SPEC = pl.BlockSpec(memory_space=pltpu.SEMAPHORE)
DATAFLOW = pltpu.SideEffectType.DATAFLOW_SIDE_EFFECTING


def _in_hbm(a):
    return pltpu.with_memory_space_constraint(a, pltpu.HBM)


def _gather_copies(w_refs, land_refs, send_sems, recv_sems):
    x, y, c, chips = _place()
    targets = [(x, y, 1 - c)] + [(*chip, c) for chip in chips]
    cps = []
    for k, (w_ref, land_ref) in enumerate(zip(w_refs, land_refs)):
        for j, to in enumerate(targets):
            cps.append(pltpu.make_async_remote_copy(src_ref=w_ref, dst_ref=land_ref.at[2 * x + y], send_sem=send_sems.at[4 * k + j],
                                                    recv_sem=recv_sems.at[4 * k + j], device_id=to, device_id_type=MESH))
    return cps


def _scatter_copies(p_refs, land_refs, send_sems, recv_sems):
    x, y, c, chips = _place()
    cps = []
    for k, (p_ref, land_ref) in enumerate(zip(p_refs, land_refs)):
        for j, (px, py) in enumerate(chips):
            cps.append(pltpu.make_async_remote_copy(src_ref=p_ref.at[2 * px + py], dst_ref=land_ref.at[j], send_sem=send_sems.at[3 * k + j],
                                                    recv_sem=recv_sems.at[3 * k + j], device_id=(px, py, c), device_id_type=MESH))
    return cps


def copies_start(srcs, land_shapes, make_copies, per_src, name):
    n = len(srcs)
    m = per_src * n

    def body(*refs):
        src_refs, land_refs = refs[:n], refs[n:2 * n]
        send_sems, recv_sems, token = refs[2 * n], refs[2 * n + 1], refs[-1]
        for cp in make_copies(src_refs, land_refs, send_sems, recv_sems):
            cp.start()
        token[...] = jnp.zeros_like(token)

    lands = [lax.empty(shp, s.dtype) for shp, s in zip(land_shapes, srcs)]
    res = pl.pallas_call(
        body, name=name,
        out_shape=(pltpu.SemaphoreType.DMA((m,)), pltpu.SemaphoreType.DMA((m,)), *[pltpu.HBM(s.shape, s.dtype) for s in srcs],
                   *[pltpu.HBM(shp, s.dtype) for shp, s in zip(land_shapes, srcs)], SDS((8, LANES), F32)),
        in_specs=[HBM_SPEC] * (2 * n),
        out_specs=(SEM_SPEC, SEM_SPEC, *[HBM_SPEC] * (2 * n), pl.BlockSpec(memory_space=pltpu.VMEM)),
        input_output_aliases={i: 2 + i for i in range(2 * n)},
        compiler_params=pltpu.CompilerParams(has_side_effects=DATAFLOW),
    )(*[_in_hbm(s) for s in srcs], *[_in_hbm(l) for l in lands])
    return res[0], res[1], list(res[2:2 + n]), list(res[2 + n:2 + 2 * n]), res[-1]


def copies_wait(send_sems, recv_sems, srcs, lands, make_copies, after, name):
    n = len(srcs)

    def body(*refs):
        src_refs, land_refs = refs[:n], refs[n:2 * n]
        for cp in make_copies(src_refs, land_refs, refs[2 * n], refs[2 * n + 1]):
            cp.wait_send()
            cp.wait_recv()

    res = pl.pallas_call(
        body, name=name,
        out_shape=(*[pltpu.HBM(s.shape, s.dtype) for s in srcs], *[pltpu.HBM(l.shape, l.dtype) for l in lands]),
        in_specs=[HBM_SPEC] * (2 * n) + [SEM_SPEC, SEM_SPEC, ANY],
        out_specs=tuple([HBM_SPEC] * (2 * n)),
        input_output_aliases={i: i for i in range(2 * n)},
        compiler_params=pltpu.CompilerParams(has_side_effects=DATAFLOW),
    )(*srcs, *lands, send_sems, recv_sems, after)
    return list(res[n:])


def _pad_lanes(v):
    return jnp.concatenate([v.astype(F32), jnp.zeros((LANES - v.shape[0],), F32)])[None]


def kernel(x, c, w_ada, b_ada, norm_mix, norm_ffn, w_ffn_in, w_ffn_out, gdn_w_in, gdn_conv, gdn_a_log, gdn_dt_bias, gdn_out_norm, gdn_w_out, dsw_w_in, dsw_q_norm, dsw_k_norm, dsw_w_out, rel_bias, loss_target, m_w_ada, m_b_ada, m_norm_mix, m_norm_ffn, m_w_ffn_in, m_w_ffn_out, m_gdn_w_in, m_gdn_conv, m_gdn_a_log, m_gdn_dt_bias, m_gdn_out_norm, m_gdn_w_out, m_dsw_w_in, m_dsw_q_norm, m_dsw_k_norm, m_dsw_w_out, m_rel_bias, v_w_ada, v_b_ada, v_norm_mix, v_norm_ffn, v_w_ffn_in, v_w_ffn_out, v_gdn_w_in, v_gdn_conv, v_gdn_a_log, v_gdn_dt_bias, v_gdn_out_norm, v_gdn_w_out, v_dsw_w_in, v_dsw_q_norm, v_dsw_k_norm, v_dsw_w_out, v_rel_bias):
    S = x.shape[1]
    nt = S // WT
    xi, yi, ci = lax.axis_index("x"), lax.axis_index("y"), lax.axis_index("c")
    me = 4 * xi + 2 * yi + ci
    s_me = 2 * xi + yi
    x0, tgt = x[0], loss_target[0]

    whole = lambda a: Row(a, a.shape, lambda i: (0,) * a.ndim)
    (cond8,) = rowwise(lambda ids, v: (_silu(v),), [whole(c.reshape(8, LANES))], [], [Out((8, LANES), F32, (8, LANES), lambda i: (0, 0))], (1,), "cond")
    cond_all = all_gather_small(cond8, "gather_cond").reshape(N_DEV, D)
    cond16 = jnp.concatenate([cond_all, jnp.zeros((8, D), F32)], axis=0)
    ada_cols = w_ada.shape[2]
    mods = [matmul(cond16, w_ada[l], "nn", F32, f"ada_{l}")[:N_DEV] for l in range(2)]
    buf = jnp.concatenate([jnp.stack(mods, axis=1).reshape(-1, LANES), gdn_conv.reshape(-1, LANES)], axis=0)
    n_mod_rows = N_DEV * 2 * ada_cols // LANES
    got = all_gather_small(buf, "gather_mod").reshape(N_DEV, buf.shape[0], LANES)
    mod_parts, conv_parts = [], []
    for s in range(N_SHARD):
        from_dev = got[2 * s]
        mod_parts.append(lax.dynamic_index_in_dim(from_dev[:n_mod_rows].reshape(N_DEV, 2, ada_cols), me, 0, keepdims=False))
        conv_parts.append(from_dev[n_mod_rows:].reshape(4, -1))
    mod_nb = jnp.concatenate(mod_parts, axis=1)
    conv_w = jnp.concatenate(conv_parts, axis=1)
    (mod,) = rowwise(lambda ids, a, b: (a + b,), [whole(mod_nb), whole(b_ada)], [], [Out(mod_nb.shape, F32, mod_nb.shape, lambda i: (0, 0))], (1,), "mod_bias")
    mod = mod.reshape(2, 6, 1, D)
    sh1, sc1, g1, sh2, sc2, g2 = ([mod[l, k] for l in range(2)] for k in range(6))
    gmix = [norm_mix[l][None] for l in range(2)]
    gffn = [norm_ffn[l][None] for l in range(2)]

    gcols = gdn_w_in.shape[2]
    g_gdn_in, g_gdn_out = all_gather_shards([gdn_w_in[0].astype(BF16), gdn_w_out[0].astype(BF16)])
    gathered = lambda ws: [(N_SHARD,) + w.shape for w in ws]
    gate = (jnp.minimum(jnp.abs(g_gdn_in[0, 0, 0].astype(F32)), 0.0) + jnp.minimum(jnp.abs(mod[0, 0, 0, 0]), 0.0)).astype(BF16)
    w2 = [w_ffn_in[0].astype(BF16) + gate, w_ffn_out[0].astype(BF16) + gate]
    w3 = [dsw_w_in[0].astype(BF16) + gate, dsw_w_out[0].astype(BF16) + gate, w_ffn_in[1].astype(BF16) + gate, w_ffn_out[1].astype(BF16) + gate]
    fly2 = copies_start(w2, gathered(w2), _gather_copies, 4, "weights_ffn0_start")
    fly3 = copies_start(w3, gathered(w3), _gather_copies, 4, "weights_layer1_start")
    started = fly2[4][0, 0] + fly3[4][0, 0]
    w_gdn = jnp.concatenate([g_gdn_in[s] for s in range(N_SHARD)] + [jnp.zeros((D, GDN_PROJ - N_SHARD * gcols), BF16)], axis=1)
    alog, dtb = _pad_lanes(gdn_a_log[0]), _pad_lanes(gdn_dt_bias[0])
    qg2 = jnp.concatenate([dsw_q_norm, dsw_q_norm], axis=1)
    kg2 = jnp.concatenate([dsw_k_norm, dsw_k_norm], axis=1)
    w_gdn_out = g_gdn_out.reshape(GDN_H * LANES, D)
    gdn_args = (w_gdn, conv_w, alog, dtb, gdn_out_norm, w_gdn_out)
    sc1[0] = sc1[0] + started

    (h10,) = rowwise(f_norm_only, [_wide(x0)], [gmix[0], sc1[0], sh1[0]], [_wide_out(S, BF16)], (nt,), "l0_norm")
    y0, sv_g = gdn_forward(h10, *gdn_args)
    x1, h20 = rowwise(f_resid_norm, [_wide(x0), _wide(y0)], [g1[0], gffn[0], sc2[0], sh2[0]], [_wide_out(S, F32), _wide_out(S, BF16)], (nt,), "l0_mid")
    g_in0, g_out0 = copies_wait(*fly2[:4], _gather_copies, y0, "weights_ffn0_wait")
    w_ffn = [(g_in0, g_out0.reshape(FFN, D)), None]
    f0, sv_f0 = ffn_forward(h20, *w_ffn[0], "0")
    x2, h11 = rowwise(f_resid_norm, [_wide(x1), _wide(f0)], [g2[0], gmix[1], sc1[1], sh1[1]], [_wide_out(S, F32), _wide_out(S, BF16)], (nt,), "l1_in")
    g_dsw_in, g_dsw_out, g_in1, g_out1 = copies_wait(*fly3[:4], _gather_copies, f0, "weights_layer1_wait")
    w_ffn[1] = (g_in1, g_out1.reshape(FFN, D))
    dsw_args = (g_dsw_in, qg2, kg2)
    y1, sv_d = dsw_forward(h11, *dsw_args, rel_bias, g_dsw_out)
    x3, h21 = rowwise(f_resid_norm, [_wide(x2), _wide(y1)], [g1[1], gffn[1], sc2[1], sh2[1]], [_wide_out(S, F32), _wide_out(S, BF16)], (nt,), "l1_mid")
    f1, sv_f1 = ffn_forward(h21, *w_ffn[1], "1")
    parts, dx3, df1, dg2_1 = loss_and_grad(x3, f1, tgt, g2[1], S)
    loss = lax.psum(jnp.sum(parts), ("x", "y", "c"))

    dh21, d_win1, d_wout1 = ffn_backward(df1, sv_f1, *w_ffn[1], "1")
    (dx2, dy1), (dg1_1, dgf1, dsc2_1, dsh2_1) = rowwise_bwd(
        f_resid_norm, [_wide(x2), _wide(y1, gdtype=BF16)], [g1[1], gffn[1], sc2[1], sh2[1]], [_wide(dx3), _wide(dh21)], (nt,), "l1_mid_bwd")
    dh11, g_d = dsw_backward(dy1, sv_d, *dsw_args, g_dsw_out)
    (dx1, df0), (dg2_0, dgm1, dsc1_1, dsh1_1) = rowwise_bwd(
        f_resid_norm, [_wide(x1), _wide(f0, gdtype=BF16)], [g2[0], gmix[1], sc1[1], sh1[1]], [_wide(dx2), _wide(dh11)], (nt,), "l1_in_bwd")
    by_shard = lambda a: a.reshape(N_SHARD, a.shape[0] // N_SHARD, a.shape[1])
    landing = lambda ps: [(3,) + p.shape[1:] for p in ps]
    dws3 = [g_d["w_in"], g_d["w_out"], d_win1, by_shard(d_wout1)]
    parts3 = [a.astype(BF16) for a in dws3]
    gfly3 = copies_start(parts3, landing(parts3), _scatter_copies, 3, "grads_layer1_start")
    w_out0 = w_ffn[0][1] + gfly3[4][0, 0].astype(BF16)
    dh20, d_win0, d_wout0 = ffn_backward(df0, sv_f0, w_ffn[0][0], w_out0, "0")
    (dx0p, dy0), (dg1_0, dgf0, dsc2_0, dsh2_0) = rowwise_bwd(
        f_resid_norm, [_wide(x0), _wide(y0, gdtype=BF16)], [g1[0], gffn[0], sc2[0], sh2[0]], [_wide(dx1), _wide(dh20)], (nt,), "l0_mid_bwd")
    dws2 = [d_win0, by_shard(d_wout0)]
    parts2 = [a.astype(BF16) for a in dws2]
    gfly2 = copies_start(parts2, landing(parts2), _scatter_copies, 3, "grads_ffn0_start")
    gdn_args = gdn_args[:5] + (w_gdn_out + gfly2[4][0, 0].astype(BF16),)
    gdn_flight = []

    def start_gdn_grads(d_w_in, d_w_out):
        dws1 = [jnp.stack([d_w_in[:, s * gcols:(s + 1) * gcols] for s in range(N_SHARD)]), by_shard(d_w_out)]
        parts1 = [a.astype(BF16) for a in dws1]
        fly = copies_start(parts1, landing(parts1), _scatter_copies, 3, "grads_gdn_start")
        gdn_flight.extend([dws1, fly])
        return fly[4][0, 0]

    dh10, g_g = gdn_backward(dy0, sv_g, *gdn_args, on_weight_grads=start_gdn_grads)
    dws1, gfly1 = gdn_flight
    (grad_x,), (dgm0, dsc1_0, dsh1_0) = rowwise_bwd(f_first, [_wide(x0)], [gmix[0], sc1[0], sh1[0]], [_wide(dx0p), _wide(dh10)], (nt,), "l0_norm_bwd")

    dmod = jnp.concatenate([dsh1_0, dsc1_0, dg1_0, dsh2_0, dsc2_0, dg2_0, dsh1_1, dsc1_1, dg1_1, dsh2_1, dsc2_1, dg2_1], axis=1)
    d_rel = jnp.transpose(g_d["rel"][:, :, 0])
    fold = lambda v: v[:, :DSW_DH] + v[:, DSW_DH:]
    small = [dmod, jnp.concatenate([dgm0, dgm1], axis=1), jnp.concatenate([dgf0, dgf1], axis=1), g_g["conv"].reshape(1, -1),
             g_g["alog"], g_g["dtb"], g_g["gain"], _pad_lanes(fold(g_d["q_gain2"])[0]), _pad_lanes(fold(g_d["k_gain2"])[0]),
             d_rel.reshape(1, -1)]
    used = [v.shape[1] // LANES for v in small]
    sizes = [-(-u // 8) * 8 for u in used]
    pad8 = lambda v, u, s: jnp.concatenate([v.reshape(u, LANES), jnp.zeros((s - u, LANES), F32)], axis=0) if s > u else v.reshape(u, LANES)
    pad_rows = sum(sizes)
    sbuf = jnp.concatenate([pad8(v, u, s) for v, u, s in zip(small, used, sizes)], axis=0)
    sgot = all_gather_small(sbuf, "gather_small_grads")
    ssum = add_rows([sgot[d * pad_rows:(d + 1) * pad_rows] for d in range(N_DEV)], F32, "sum_small_grads", rt=pad_rows)
    offs = np.cumsum([0] + sizes)
    take = lambda k: ssum[offs[k]:offs[k] + used[k]].reshape(1, -1)
    grad_b_ada = take(0).reshape(2, 6 * D)
    grad_norm_mix = take(1).reshape(2, D)
    grad_norm_ffn = take(2).reshape(2, D)
    conv_full = take(3).reshape(4, -1)
    ncv = gdn_conv.shape[2]
    grad_gdn_conv = lax.dynamic_slice_in_dim(conv_full, s_me * ncv, ncv, axis=1)[None]
    grad_a_log = take(4)[:, :GDN_H]
    grad_dt_bias = take(5)[:, :GDN_H]
    grad_out_norm = take(6)
    grad_q_norm = take(7)[:, :DSW_DH]
    grad_k_norm = take(8)[:, :DSW_DH]
    grad_rel = take(9).reshape(REL_BUCKETS, 3 * GDN_H)
    dmod_all = sgot.reshape(N_DEV, pad_rows, LANES)[:, :used[0]].reshape(N_DEV, 2, 6 * D)
    dmod_mine = lax.dynamic_slice_in_dim(dmod_all, s_me * ada_cols, ada_cols, axis=2)
    dmod16 = jnp.concatenate([dmod_mine, jnp.zeros_like(dmod_mine)], axis=0)
    grad_w_ada = jnp.stack([matmul(cond16, dmod16[:, l], "tn", F32, f"ada_dw_{l}") for l in range(2)])

    got3 = copies_wait(*gfly3[:4], _scatter_copies, grad_x, "grads_layer1_wait")
    got2 = copies_wait(*gfly2[:4], _scatter_copies, grad_x, "grads_ffn0_wait")
    got1 = copies_wait(*gfly1[:4], _scatter_copies, grad_x, "grads_gdn_wait")
    core_sums = []
    for i, (full, got) in enumerate(zip(dws3 + dws2 + dws1, got3 + got2 + got1)):
        own = lax.dynamic_index_in_dim(full, s_me, 0, keepdims=False)
        core_sums.append(add_rows([own, got[0], got[1], got[2]], F32, f"grads_core_sum_{i}"))
    sib_sums = sibling_exchange(core_sums, "grads_core_sums_swap")
    s_dsw_in, s_dsw_out, s_in1, s_out1, s_in0, s_out0, s_gdn_in, s_gdn_out = [
        add_rows([a, b], F32, f"grads_chip_total_{i}") for i, (a, b) in enumerate(zip(core_sums, sib_sums))]
    gsh = dict(gdn_w_in=s_gdn_in[None], gdn_w_out=s_gdn_out[None],
               w_ffn_in=jnp.stack([s_in0, s_in1]), w_ffn_out=jnp.stack([s_out0, s_out1]),
               dsw_w_in=s_dsw_in[None], dsw_w_out=s_dsw_out[None])

    grads = dict(w_ada=grad_w_ada, b_ada=grad_b_ada, norm_mix=grad_norm_mix, norm_ffn=grad_norm_ffn, w_ffn_in=gsh["w_ffn_in"],
                 w_ffn_out=gsh["w_ffn_out"], gdn_w_in=gsh["gdn_w_in"], gdn_conv=grad_gdn_conv, gdn_a_log=grad_a_log,
                 gdn_dt_bias=grad_dt_bias, gdn_out_norm=grad_out_norm, gdn_w_out=gsh["gdn_w_out"], dsw_w_in=gsh["dsw_w_in"],
                 dsw_q_norm=grad_q_norm, dsw_k_norm=grad_k_norm, dsw_w_out=gsh["dsw_w_out"], rel_bias=grad_rel)
    weights = dict(w_ada=w_ada, b_ada=b_ada, norm_mix=norm_mix, norm_ffn=norm_ffn, w_ffn_in=w_ffn_in, w_ffn_out=w_ffn_out,
                   gdn_w_in=gdn_w_in, gdn_conv=gdn_conv, gdn_a_log=gdn_a_log, gdn_dt_bias=gdn_dt_bias, gdn_out_norm=gdn_out_norm,
                   gdn_w_out=gdn_w_out, dsw_w_in=dsw_w_in, dsw_q_norm=dsw_q_norm, dsw_k_norm=dsw_k_norm, dsw_w_out=dsw_w_out,
                   rel_bias=rel_bias)
    ms = dict(w_ada=m_w_ada, b_ada=m_b_ada, norm_mix=m_norm_mix, norm_ffn=m_norm_ffn, w_ffn_in=m_w_ffn_in, w_ffn_out=m_w_ffn_out,
              gdn_w_in=m_gdn_w_in, gdn_conv=m_gdn_conv, gdn_a_log=m_gdn_a_log, gdn_dt_bias=m_gdn_dt_bias, gdn_out_norm=m_gdn_out_norm,
              gdn_w_out=m_gdn_w_out, dsw_w_in=m_dsw_w_in, dsw_q_norm=m_dsw_q_norm, dsw_k_norm=m_dsw_k_norm, dsw_w_out=m_dsw_w_out,
              rel_bias=m_rel_bias)
    vs = dict(w_ada=v_w_ada, b_ada=v_b_ada, norm_mix=v_norm_mix, norm_ffn=v_norm_ffn, w_ffn_in=v_w_ffn_in, w_ffn_out=v_w_ffn_out,
              gdn_w_in=v_gdn_w_in, gdn_conv=v_gdn_conv, gdn_a_log=v_gdn_a_log, gdn_dt_bias=v_gdn_dt_bias, gdn_out_norm=v_gdn_out_norm,
              gdn_w_out=v_gdn_w_out, dsw_w_in=v_dsw_w_in, dsw_q_norm=v_dsw_q_norm, dsw_k_norm=v_dsw_k_norm, dsw_w_out=v_dsw_w_out,
              rel_bias=v_rel_bias)
    names = list(weights)
    deltas, new_m, new_v = [], [], []
    for n in names:
        g = grads[n].reshape(weights[n].shape)
        grads[n] = g
        d, nm, nv = adamw(weights[n], g, ms[n], vs[n], f"adamw_{n}")
        deltas.append(d)
        new_m.append(nm)
        new_v.append(nv)
    return (loss, grad_x[None], *[grads[n] for n in names], *deltas, *new_m, *new_v)
```

```python
import functools
import math

import numpy as np
import jax
import jax.numpy as jnp
from jax import lax
from jax.experimental import pallas as pl
from jax.experimental.pallas import tpu as pltpu

F32 = jnp.float32
BF16 = jnp.bfloat16
SDS = jax.ShapeDtypeStruct
MESH = pl.DeviceIdType.MESH
ANY = pl.BlockSpec(memory_space=pl.ANY)

D = 1024
EPS = 1e-6
LANES = 128
GDN_H = 8
GDN_DK = 128
GDN_C = 64
DSW_GROUPS = ((128, 1), (512, 4), (2048, 16))
DSW_SPAN = 128
DSW_DH = 64
DSW_HG = 512
REL_BUCKETS = 32
REL_MAX_DIST = 2048
FFN = 2816
N_SHARD = 4
N_DEV = 8
VMEM_LIMIT = 48 * 1024 * 1024
NEG = -1e30

ADAM_LR, ADAM_B1, ADAM_B2, ADAM_EPS, ADAM_WD, ADAM_STEP = 0.001, 0.9, 0.999, 1e-08, 0.01, 10


def _cp(n_axes):
    return pltpu.CompilerParams(dimension_semantics=("arbitrary",) * n_axes, vmem_limit_bytes=VMEM_LIMIT)


def _blk(dim, cap):
    if dim <= cap:
        return dim
    best = None
    for b in range(LANES, cap + 1, LANES):
        if dim % b == 0:
            best = b
    assert best is not None, (dim, cap)
    return best


MAX_SHARD_BLOCK = 1408
def matmul(a, b, mode, out_dtype, name, cap_m=MAX_SHARD_BLOCK, cap_n=MAX_SHARD_BLOCK, cap_k=2048, col_shards=0):
    ns = col_shards
    if mode == "nn":
        (M, K) = a.shape
        K2, N = (b.shape[1], ns * b.shape[2]) if ns else b.shape
    elif mode == "nt":
        (M, K) = a.shape
        N, K2 = (b.shape[1], ns * b.shape[2]) if ns else b.shape
    else:
        (K, M), (K2, N) = a.shape, b.shape
    assert K == K2, (a.shape, b.shape, mode)
    if K <= 3072:
        cap_k = K
        if K > 2048:
            cap_n = 1024
    n_unit = N // ns if (ns and mode != "nt") else N
    k_unit = K // ns if (ns and mode == "nt") else K
    bm = _blk(M, cap_m)
    bn = _blk(n_unit, MAX_SHARD_BLOCK) if n_unit != N else _blk(N, cap_n)
    if k_unit != K:
        bk = _blk(k_unit, MAX_SHARD_BLOCK)
    else:
        bk = _blk(K, 1024 if (ns and mode == "tn") else cap_k)
    nk = K // bk
    nps, kps = n_unit // bn, k_unit // bk
    dims = {"nn": ((1,), (0,)), "nt": ((1,), (1,)), "tn": ((0,), (0,))}[mode]

    def dot(a_ref, b_ref):
        return lax.dot_general(a_ref[...].astype(BF16), b_ref[...].astype(BF16), (dims, ((), ())), preferred_element_type=F32)

    def body_one(a_ref, b_ref, o_ref):
        o_ref[...] = dot(a_ref, b_ref).astype(o_ref.dtype)

    def body_acc(a_ref, b_ref, o_ref, acc_ref):
        k = pl.program_id(2)

        @pl.when(k == 0)
        def _():
            acc_ref[...] = jnp.zeros_like(acc_ref)

        acc_ref[...] += dot(a_ref, b_ref)

        @pl.when(k == nk - 1)
        def _():
            o_ref[...] = acc_ref[...].astype(o_ref.dtype)

    a_spec = pl.BlockSpec((bk, bm), lambda i, j, k: (k, i)) if mode == "tn" else pl.BlockSpec((bm, bk), lambda i, j, k: (i, k))
    if mode == "nt":
        b_spec = pl.BlockSpec((None, bn, bk), lambda i, j, k: (k // kps, j, k % kps)) if ns else pl.BlockSpec((bn, bk), lambda i, j, k: (j, k))
    elif mode == "nn" and ns:
        b_spec = pl.BlockSpec((None, bk, bn), lambda i, j, k: (j // nps, k, j % nps))
    else:
        b_spec = pl.BlockSpec((bk, bn), lambda i, j, k: (k, j))
    if mode == "tn" and ns:
        o_spec, o_shape = pl.BlockSpec((None, bm, bn), lambda i, j, k: (j // nps, i, j % nps)), (ns, M, n_unit)
    else:
        o_spec, o_shape = pl.BlockSpec((bm, bn), lambda i, j, k: (i, j)), (M, N)
    return pl.pallas_call(
        body_one if nk == 1 else body_acc, name=name, grid=(M // bm, N // bn, nk),
        in_specs=[a_spec, b_spec], out_specs=o_spec,
        out_shape=SDS(o_shape, out_dtype), scratch_shapes=[] if nk == 1 else [pltpu.VMEM((bm, bn), F32)],
        compiler_params=_cp(3),
    )(a, b)


class Row:
    def __init__(self, arr, bshape, imap, splits=None, diff=True, acc=False, gdtype=F32, gshape=None, gbshape=None, gimap=None,
                 lead=0):
        self.arr, self.bshape, self.imap = arr, tuple(bshape), imap
        self.splits, self.lead = splits, lead
        self.diff, self.acc, self.gdtype = diff, acc, gdtype
        self.gshape = tuple(arr.shape) if gshape is None else tuple(gshape)
        self.gbshape = self.bshape if gbshape is None else tuple(gbshape)
        self.gimap = imap if gimap is None else gimap

    def gspec(self):
        return pl.BlockSpec(self.gbshape, self.gimap)

    def spec(self):
        return pl.BlockSpec(self.bshape, self.imap)

    def pieces(self, ref):
        return _load_pieces(ref, self.splits, self.lead)

    def n_pieces(self):
        return _n_pieces(self.splits, self.lead)


class Out:
    def __init__(self, shape, dtype, bshape, imap, splits=None, lead=0):
        self.shape, self.dtype, self.bshape, self.imap = tuple(shape), dtype, tuple(bshape), imap
        self.splits, self.lead = splits, lead

    def n_pieces(self):
        return _n_pieces(self.splits, self.lead)


def _n_pieces(splits, lead):
    return lead if lead else (1 if splits is None else len(splits))


def _load_pieces(ref, splits, lead):
    if lead:
        return [ref[k].astype(F32) for k in range(lead)]
    if splits is None:
        return [ref[...].astype(F32)]
    out, o = [], 0
    for w in splits:
        out.append(ref[..., o:o + w].astype(F32))
        o += w
    return out


def _store_pieces(ref, splits, lead, vals, accumulate=False):
    def put(idx, v):
        if accumulate:
            ref[idx] += v.astype(ref.dtype)
        else:
            ref[idx] = v.astype(ref.dtype)

    if lead:
        for k in range(lead):
            put(k, vals[k])
    elif splits is None:
        put(..., vals[0])
    else:
        o = 0
        for w, v in zip(splits, vals):
            put((..., slice(o, o + w)), v)
            o += w


def rowwise(fn, rows, params, outs, grid, name):
    nr, npar = len(rows), len(params)

    def body(*refs):
        ids = tuple(pl.program_id(a) for a in range(len(grid)))
        vals = []
        for r, ref in zip(rows, refs[:nr]):
            vals += r.pieces(ref)
        pvals = [ref[...].astype(F32) for ref in refs[nr:nr + npar]]
        res = list(fn(ids, *vals, *pvals))
        o = 0
        for spec, ref in zip(outs, refs[nr + npar:]):
            n = spec.n_pieces()
            _store_pieces(ref, spec.splits, spec.lead, res[o:o + n])
            o += n

    nz = len(grid)
    pspecs = [pl.BlockSpec(p.shape, (lambda *ids, _n=p.ndim: (0,) * _n)) for p in params]
    res = pl.pallas_call(
        body, name=name, grid=grid,
        in_specs=[r.spec() for r in rows] + pspecs,
        out_specs=[pl.BlockSpec(o.bshape, o.imap) for o in outs],
        out_shape=[SDS(o.shape, o.dtype) for o in outs],
        compiler_params=_cp(nz),
    )(*[r.arr for r in rows], *params)
    return list(res)


def rowwise_bwd(fn, rows, params, cots, grid, name):
    nr, npar, nc = len(rows), len(params), len(cots)
    drows = [r for r in rows if r.diff]
    nz = len(grid)

    def body(*refs):
        ids = tuple(pl.program_id(a) for a in range(nz))
        row_refs, par_refs = refs[:nr], refs[nr:nr + npar]
        cot_refs = refs[nr + npar:nr + npar + nc]
        drow_refs = refs[nr + npar + nc:nr + npar + nc + len(drows)]
        dpar_refs = refs[nr + npar + nc + len(drows):]
        pieces, is_diff = [], []
        for r, ref in zip(rows, row_refs):
            ps = r.pieces(ref)
            pieces += ps
            is_diff += [r.diff] * len(ps)
        pvals = [ref[...].astype(F32) for ref in par_refs]
        dvals = [p for p, dflag in zip(pieces, is_diff) if dflag]
        nd = len(dvals)

        def f(*args):
            it = iter(args[:nd])
            full = [next(it) if dflag else p for p, dflag in zip(pieces, is_diff)]
            return tuple(fn(ids, *full, *args[nd:]))

        _, vjp = jax.vjp(f, *dvals, *pvals)
        cvals = []
        for c, ref in zip(cots, cot_refs):
            cvals += c.pieces(ref)
        g = vjp(tuple(cvals))
        o = 0
        first_inner = ids[-1] == 0
        for r, ref in zip(drows, drow_refs):
            n = r.n_pieces()
            gs = g[o:o + n]
            o += n
            if r.acc:
                @pl.when(first_inner)
                def _(ref=ref):
                    ref[...] = jnp.zeros_like(ref)
            _store_pieces(ref, r.splits, r.lead, gs, accumulate=r.acc)
        first = functools.reduce(jnp.logical_and, [i == 0 for i in ids])
        for ref, gp in zip(dpar_refs, g[nd:]):
            @pl.when(first)
            def _(ref=ref):
                ref[...] = jnp.zeros_like(ref)
            ref[...] += gp

    pspecs = [pl.BlockSpec(p.shape, (lambda *ids, _n=p.ndim: (0,) * _n)) for p in params]
    res = pl.pallas_call(
        body, name=name, grid=grid,
        in_specs=[r.spec() for r in rows] + pspecs + [c.spec() for c in cots],
        out_specs=[r.gspec() for r in drows] + pspecs,
        out_shape=[SDS(r.gshape, r.gdtype) for r in drows] + [SDS(p.shape, F32) for p in params],
        compiler_params=_cp(nz),
    )(*[r.arr for r in rows], *params, *[c.arr for c in cots])
    res = list(res)
    return res[:len(drows)], res[len(drows):]


def _sigmoid(x):
    return 0.5 * (jnp.tanh(0.5 * x) + 1.0)


def _silu(x):
    return x * _sigmoid(x)


def _normmod(x, gain, sc, sh):
    inv = lax.rsqrt(jnp.mean(x * x, axis=-1, keepdims=True) + EPS)
    return x * inv * gain * (1.0 + sc) + sh


def f_first(ids, x, gain, sc, sh):
    return x, _normmod(x, gain, sc, sh)


def f_resid_norm(ids, x, y, g, gain, sc, sh):
    xn = x + g * y
    return xn, _normmod(xn, gain, sc, sh)


@jax.custom_vjp
def _swiglu(gate, up):
    return _silu(gate) * up


def _swiglu_fwd(gate, up):
    return _silu(gate) * up, (gate, up)


def _swiglu_bwd(res, da):
    gate, up = res
    s = _sigmoid(gate)
    gs = gate * s
    return da * up * (s + gs * (1.0 - s)), da * gs


_swiglu.defvjp(_swiglu_fwd, _swiglu_bwd)


def loss_and_grad(x, y, tgt, g, S):
    nt = S // WT

    def body(x_ref, y_ref, t_ref, g_ref, part_ref, dx_ref, dy_ref, dg_ref):
        @pl.when(pl.program_id(0) == 0)
        def _():
            dg_ref[...] = jnp.zeros_like(dg_ref)

        yv = y_ref[...].astype(F32)
        gg = g_ref[...]
        e = x_ref[...] + gg * yv - t_ref[...]
        part_ref[...] = 0.5 * jnp.sum(e * e, axis=0, keepdims=True) * (1.0 / D)
        d = e * (1.0 / D)
        dx_ref[...] = d
        dy_ref[...] = (d * gg).astype(dy_ref.dtype)
        dg_ref[...] += jnp.sum(d * yv, axis=0, keepdims=True)

    row = pl.BlockSpec((WT, D), lambda i: (i, 0))
    vec = pl.BlockSpec((1, D), lambda i: (0, 0))
    return pl.pallas_call(
        body, name="loss_and_grad", grid=(nt,), in_specs=[row, row, row, vec],
        out_specs=[pl.BlockSpec((None, 1, D), lambda i: (i, 0, 0)), row, row, vec],
        out_shape=[SDS((nt, 1, D), F32), SDS((S, D), F32), SDS((S, D), BF16), SDS((1, D), F32)],
        compiler_params=_cp(1),
    )(x, y, tgt, g)


def _softplus(x):
    return jnp.maximum(x, 0.0) + jnp.log(1.0 + jnp.exp(-jnp.abs(x)))


def _chunk_tril(T):
    r = lax.broadcasted_iota(jnp.int32, (T, T), 0)
    c = lax.broadcasted_iota(jnp.int32, (T, T), 1)
    return jnp.where((r // GDN_C == c // GDN_C) & (c <= r), 1.0, 0.0).astype(F32)


def _dot_hi(a, b, dims=((1,), (0,))):
    return lax.dot_general(a, b, (dims, ((), ())), precision=lax.Precision.HIGHEST, preferred_element_type=F32)


def _dot_x3(a, b, dims=((1,), (0,))):
    return lax.dot_general(a, b, (dims, ((), ())), precision=lax.Precision.HIGH, preferred_element_type=F32)


def f_gdn_gates(ids, ab, alog, dtb):
    T = ab.shape[0]
    g = -jnp.exp(alog) * _softplus(ab + dtb)
    beta = _sigmoid(ab)
    gcum = _dot_x3(_chunk_tril(T), g)
    row = lax.broadcasted_iota(jnp.int32, (LANES, LANES), 0)
    sel = lambda k: jnp.where(row == k, 1.0, 0.0).astype(F32)
    gcs = [_dot_x3(gcum, sel(h)) for h in range(GDN_H)]
    bts = [_dot_x3(beta, sel(GDN_H + h)) for h in range(GDN_H)]
    return (*gcs, *bts)


def f_gdn_post(ids, *args):
    os_, zs, gain = args[:GDN_H], args[GDN_H:2 * GDN_H], args[2 * GDN_H]
    out = []
    for o, z in zip(os_, zs):
        inv = lax.rsqrt(jnp.mean(o * o, axis=-1, keepdims=True) + EPS)
        out.append(o * inv * gain * _silu(z))
    return tuple(out)


def _qknorm1(x, gain2, scale):
    lane = lax.broadcasted_iota(jnp.int32, x.shape, 1)
    lo = lane < DSW_DH
    x2 = x * x
    s_all = jnp.sum(x2, axis=-1, keepdims=True)
    s_lo = jnp.sum(jnp.where(lo, x2, 0.0), axis=-1, keepdims=True)
    ms = jnp.where(lo, s_lo, s_all - s_lo) * (1.0 / DSW_DH)
    return x * lax.rsqrt(ms + EPS) * (gain2 * scale)


def f_combine(ids, o0, o1, o2, l0, l1, l2):
    m = jnp.maximum(jnp.maximum(l0, l1), l2)
    e0, e1, e2 = jnp.exp(l0 - m), jnp.exp(l1 - m), jnp.exp(l2 - m)
    den = e0 + e1 + e2
    o = (e0 * o0 + e1 * o1 + e2 * o2) / den
    return o, m + jnp.log(den)


GDN_T = 512
HALO = 16


def _conv_pre(xx, w):
    acc = xx * w[3:4, :]
    for j in range(3):
        acc = acc + pltpu.roll(xx, shift=3 - j, axis=0) * w[j:j + 1, :]
    return acc


@jax.custom_vjp
def _qkv_act_core(pre, norm_on, scale):
    s = _silu(pre)
    r = lax.rsqrt(jnp.sum(s * s, axis=-1, keepdims=True) + EPS)
    return jnp.where(norm_on > 0.5, s * r * scale, s)


def _qkv_act_fwd(pre, norm_on, scale):
    return _qkv_act_core(pre, norm_on, scale), (pre, norm_on, scale)


def _qkv_act_bwd(res, dout):
    pre, norm_on, scale = res
    sig = _sigmoid(pre)
    s = pre * sig
    r = lax.rsqrt(jnp.sum(s * s, axis=-1, keepdims=True) + EPS)
    unit = s * r
    dn = dout * scale
    ds = jnp.where(norm_on > 0.5, r * (dn - unit * jnp.sum(dn * unit, axis=-1, keepdims=True)), dout)
    return ds * (sig + s * (1.0 - sig)), jnp.zeros_like(norm_on), jnp.zeros_like(scale)


_qkv_act_core.defvjp(_qkv_act_fwd, _qkv_act_bwd)


def _qkv_act(pre, cidx):
    norm_on = jnp.where(cidx < 2 * GDN_H, 1.0, 0.0).astype(F32)
    scale = jnp.where(cidx < GDN_H, GDN_DK ** -0.5, 1.0).astype(F32)
    return _qkv_act_core(pre, norm_on, scale)


def gdn_pre(proj, conv_w, S):
    nt = S // GDN_T
    hb = GDN_T // HALO

    def body(prev_ref, cur_ref, w_ref, o_ref):
        p, i = pl.program_id(0), pl.program_id(1)
        for h in range(GDN_H):
            cols = slice(LANES * h, LANES * (h + 1))
            prev = jnp.where(i > 0, prev_ref[:, cols].astype(F32), 0.0)
            xx = jnp.concatenate([prev, cur_ref[:, cols].astype(F32)], axis=0)
            pre = _conv_pre(xx, w_ref[:, cols])[HALO:]
            o_ref[h] = _qkv_act(pre, p * GDN_H + h).astype(o_ref.dtype)

    hv = GDN_H * LANES
    return pl.pallas_call(
        body, name="gdn_pre", grid=(3, nt),
        in_specs=[pl.BlockSpec((HALO, hv), lambda p, i: (jnp.maximum(i * hb - 1, 0), p)),
                  pl.BlockSpec((GDN_T, hv), lambda p, i: (i, p)),
                  pl.BlockSpec((4, hv), lambda p, i: (0, p))],
        out_specs=pl.BlockSpec((None, GDN_H, GDN_T, LANES), lambda p, i: (p, 0, i, 0)),
        out_shape=SDS((3, GDN_H, S, LANES), BF16),
        compiler_params=_cp(2),
    )(proj, proj, conv_w)


def gdn_pre_bwd(proj, conv_w, dqkv, S):
    nt = S // GDN_T
    hb = GDN_T // HALO
    last_h = S // HALO - 1

    def body(prev_ref, cur_ref, next_ref, w_ref, d_ref, dnext_ref, dx_ref, dw_ref):
        p, i = pl.program_id(0), pl.program_id(1)

        @pl.when(i == 0)
        def _():
            dw_ref[...] = jnp.zeros_like(dw_ref)

        for h in range(GDN_H):
            cols = slice(LANES * h, LANES * (h + 1))
            w = w_ref[:, cols]
            prev = jnp.where(i > 0, prev_ref[:, cols].astype(F32), 0.0)
            xx = jnp.concatenate([prev, cur_ref[:, cols].astype(F32), next_ref[:, cols].astype(F32)], axis=0)
            dnext = jnp.where(i < nt - 1, dnext_ref[h], 0.0)
            dd = jnp.concatenate([jnp.zeros((HALO, LANES), F32), d_ref[h], dnext], axis=0)
            pre = _conv_pre(xx, w)
            _, vjp = jax.vjp(lambda v, _c=p * GDN_H + h: _qkv_act(v, _c), pre)
            (dpre,) = vjp(dd)
            dx = dpre * w[3:4, :]
            R = dpre.shape[0]
            for j in range(3):
                dx = dx + pltpu.roll(dpre, shift=R - (3 - j), axis=0) * w[j:j + 1, :]
            dx_ref[:, cols] = dx[HALO:HALO + GDN_T].astype(dx_ref.dtype)
            own = HALO + GDN_T
            rows_w = [jnp.sum((dpre * pltpu.roll(xx, shift=3 - j, axis=0))[:own], axis=0, keepdims=True) for j in range(3)]
            rows_w.append(jnp.sum((dpre * xx)[:own], axis=0, keepdims=True))
            r4 = lax.broadcasted_iota(jnp.int32, (4, LANES), 0)
            dw = jnp.zeros((4, LANES), F32)
            for j in range(4):
                dw = dw + jnp.where(r4 == j, rows_w[j], 0.0)
            dw_ref[:, cols] += dw

    hv = GDN_H * LANES
    return pl.pallas_call(
        body, name="gdn_pre_bwd", grid=(3, nt),
        in_specs=[pl.BlockSpec((HALO, hv), lambda p, i: (jnp.maximum(i * hb - 1, 0), p)),
                  pl.BlockSpec((GDN_T, hv), lambda p, i: (i, p)),
                  pl.BlockSpec((HALO, hv), lambda p, i: (jnp.minimum((i + 1) * hb, last_h), p)),
                  pl.BlockSpec((4, hv), lambda p, i: (0, p)),
                  pl.BlockSpec((None, GDN_H, GDN_T, LANES), lambda p, i: (p, 0, i, 0)),
                  pl.BlockSpec((None, GDN_H, HALO, LANES), lambda p, i: (p, 0, jnp.minimum((i + 1) * hb, last_h), 0))],
        out_specs=[pl.BlockSpec((GDN_T, hv), lambda p, i: (i, p)),
                   pl.BlockSpec((4, hv), lambda p, i: (0, p))],
        out_shape=[SDS((S, 3 * hv), BF16), SDS((4, 3 * hv), F32)],
        compiler_params=_cp(2),
    )(proj, proj, proj, conv_w, dqkv, dqkv)


_DIMS = {"nn": ((1,), (0,)), "nt": ((1,), (1,)), "tn": ((0,), (0,))}


def _mm_raw(a, b, mode, hi):
    if hi:
        return _dot_hi(a, b, _DIMS[mode])
    return lax.dot_general(a.astype(BF16), b.astype(BF16), (_DIMS[mode], ((), ())), preferred_element_type=F32)


@functools.partial(jax.custom_vjp, nondiff_argnums=(2, 3))
def mm(a, b, mode, hi):
    return _mm_raw(a, b, mode, hi)


def _mm_fwd(a, b, mode, hi):
    return _mm_raw(a, b, mode, hi), (a, b)


def _mm_bwd(mode, hi, res, dc):
    a, b = res
    if mode == "nn":
        da, db = mm(dc, b, "nt", hi), mm(a, dc, "tn", hi)
    elif mode == "nt":
        da, db = mm(dc, b, "nn", hi), mm(dc, a, "tn", hi)
    else:
        da, db = mm(b, dc, "nt", hi), mm(a, dc, "nn", hi)
    return da, db


mm.defvjp(_mm_fwd, _mm_bwd)


TRI_BASE = 8


def _unit_lower_inverses(Ls):
    n = Ls[0].shape[0]
    r = lax.broadcasted_iota(jnp.int32, (n, n), 0)
    c = lax.broadcasted_iota(jnp.int32, (n, n), 1)
    eye = jnp.where(r == c, 1.0, 0.0).astype(F32)
    base = r // TRI_BASE == c // TRI_BASE
    one = lambda a, b_: _mm_raw(a, b_, "nn", False)
    Ps = [jnp.where(base, -L, 0.0) for L in Ls]
    invs = [eye + P for P in Ps]
    k = 1
    while 2 * k < TRI_BASE:
        Ps = [one(P, P) for P in Ps]
        invs = [inv + one(inv, P) for inv, P in zip(invs, Ps)]
        k *= 2
    b = 2 * TRI_BASE
    while b <= n:
        off_mask = (r // b == c // b) & ((r % b) >= b // 2) & ((c % b) < b // 2)
        ts = [one(inv, jnp.where(off_mask, L, 0.0)) for inv, L in zip(invs, Ls)]
        invs = [inv - one(t, inv) for inv, t in zip(invs, ts)]
        b *= 2
    resid = [eye - inv - _dot_x3(L, inv) for inv, L in zip(invs, Ls)]
    return [inv + _dot_x3(inv, rs) for inv, rs in zip(invs, resid)]


@jax.custom_vjp
def tri_apply(invs, Ls, r1s, r2s):
    return [_mm_raw(i, r, "nn", False) for i, r in zip(invs, r1s)], [_mm_raw(i, r, "nn", False) for i, r in zip(invs, r2s)]


def _tri_fwd(invs, Ls, r1s, r2s):
    s1s = [_mm_raw(i, r, "nn", False) for i, r in zip(invs, r1s)]
    s2s = [_mm_raw(i, r, "nn", False) for i, r in zip(invs, r2s)]
    return (s1s, s2s), (invs, s1s, s2s)


def _tri_bwd(res, ds):
    invs, s1s, s2s = res
    d1s = [_mm_raw(i, d, "tn", False) for i, d in zip(invs, ds[0])]
    d2s = [_mm_raw(i, d, "tn", False) for i, d in zip(invs, ds[1])]
    dLs = [-(_mm_raw(d1, s1, "nt", False) + _mm_raw(d2, s2, "nt", False)) for d1, s1, d2, s2 in zip(d1s, s1s, d2s, s2s)]
    return [jnp.zeros_like(i) for i in invs], dLs, d1s, d2s


tri_apply.defvjp(_tri_fwd, _tri_bwd)


def _gdn_chunk(qs, ks, vs, gcbs, btbs, Ss, invs=None):
    C = qs[0].shape[0]
    r = lax.broadcasted_iota(jnp.int32, (C, C), 0)
    c = lax.broadcasted_iota(jnp.int32, (C, C), 1)
    causal, strict = c <= r, c < r
    rows = lax.broadcasted_iota(jnp.int32, gcbs[0].shape, 0)
    Gs = [g[:, :C] for g in gcbs]
    decays = [jnp.exp(jnp.where(causal, G - G.T, NEG)) for G in Gs]
    kbs = [k * b for k, b in zip(ks, btbs)]
    vbs = [v * b for v, b in zip(vs, btbs)]
    Ls = [jnp.where(strict, mm(kb, k, "nt", False) * d, 0.0) for kb, k, d in zip(kbs, ks, decays)]
    egs = [jnp.exp(g) for g in gcbs]
    if invs is None:
        invs = _unit_lower_inverses(Ls)
    us, ws = tri_apply(invs, Ls, vbs, [kb * eg for kb, eg in zip(kbs, egs)])
    qks = [jnp.where(causal, mm(q, k, "nt", False) * d, 0.0) for q, k, d in zip(qs, ks, decays)]
    g_lasts = [jnp.sum(jnp.where(rows == C - 1, g, 0.0), axis=0, keepdims=True) for g in gcbs]
    q_decs = [q * eg for q, eg in zip(qs, egs)]
    k_decs = [k * jnp.exp(gl - g) for k, gl, g in zip(ks, g_lasts, gcbs)]
    v_news = [u - mm(w, S, "nn", False) for u, w, S in zip(us, ws, Ss)]
    os_ = [mm(qd, S, "nn", False) + mm(qk, vn, "nn", False) for qd, S, qk, vn in zip(q_decs, Ss, qks, v_news)]
    S_news = [S * jnp.exp(gl) + mm(kd, vn, "tn", False) for S, gl, kd, vn in zip(Ss, g_lasts, k_decs, v_news)]
    return os_, S_news, invs


INV_CHUNKS = 4
SCAN_CHUNKS = 4


def gdn_inverses(qkv, gc, bt, S):
    nchunk = S // GDN_C
    rows = INV_CHUNKS * GDN_C

    def body(k_ref, g_ref, b_ref, inv_ref):
        items = [(h, m) for m in range(INV_CHUNKS) for h in range(GDN_H)]
        r = lax.broadcasted_iota(jnp.int32, (GDN_C, GDN_C), 0)
        c = lax.broadcasted_iota(jnp.int32, (GDN_C, GDN_C), 1)
        sl = lambda m: slice(m * GDN_C, (m + 1) * GDN_C)
        ks = [k_ref[h, sl(m), :].astype(F32) for h, m in items]
        Gs = [g_ref[h, sl(m), :GDN_C] for h, m in items]
        kbs = [k * b_ref[h, sl(m), :] for k, (h, m) in zip(ks, items)]
        decays = [jnp.exp(jnp.where(c <= r, G - G.T, NEG)) for G in Gs]
        Ls = [jnp.where(c < r, _mm_raw(kb, k, "nt", False) * d, 0.0) for kb, k, d in zip(kbs, ks, decays)]
        for (h, m), inv in zip(items, _unit_lower_inverses(Ls)):
            inv_ref[h, m] = inv

    hb = pl.BlockSpec((GDN_H, rows, LANES), lambda n: (0, n, 0))
    return pl.pallas_call(
        body, name="gdn_inverses", grid=(nchunk // INV_CHUNKS,),
        in_specs=[pl.BlockSpec((None, GDN_H, rows, LANES), lambda n: (1, 0, n, 0)), hb, hb],
        out_specs=pl.BlockSpec((GDN_H, INV_CHUNKS, GDN_C, GDN_C), lambda n: (0, n, 0, 0)),
        out_shape=SDS((GDN_H, nchunk, GDN_C, GDN_C), F32),
        compiler_params=_cp(1),
    )(qkv, gc, bt)


def gdn_core(qkv, gc, bt, invs, S):
    nchunk = S // GDN_C

    def body(qkv_ref, g_ref, b_ref, inv_ref, o_ref, st_ref, s_scr):
        n = pl.program_id(0)

        @pl.when(n == 0)
        def _():
            s_scr[...] = jnp.zeros_like(s_scr)

        heads = range(GDN_H)
        S_cur = [s_scr[h] for h in heads]
        for m in range(SCAN_CHUNKS):
            sl = slice(m * GDN_C, (m + 1) * GDN_C)
            os_, S_new, _ = _gdn_chunk(*[[qkv_ref[p, h, sl, :].astype(F32) for h in heads] for p in range(3)],
                                       [g_ref[h, sl, :] for h in heads], [b_ref[h, sl, :] for h in heads], S_cur,
                                       invs=[inv_ref[h, m] for h in heads])
            for h in heads:
                st_ref[h, m] = S_cur[h].astype(st_ref.dtype)
                o_ref[h, sl, :] = os_[h]
            S_cur = S_new
        for h in heads:
            s_scr[h] = S_cur[h]

    rows = SCAN_CHUNKS * GDN_C
    blk3 = pl.BlockSpec((3, GDN_H, rows, LANES), lambda n: (0, 0, n, 0))
    hb = pl.BlockSpec((GDN_H, rows, LANES), lambda n: (0, n, 0))
    return pl.pallas_call(
        body, name="gdn_core", grid=(nchunk // SCAN_CHUNKS,),
        in_specs=[blk3, hb, hb, pl.BlockSpec((GDN_H, SCAN_CHUNKS, GDN_C, GDN_C), lambda n: (0, n, 0, 0))],
        out_specs=[hb, pl.BlockSpec((GDN_H, SCAN_CHUNKS, GDN_DK, LANES), lambda n: (0, n, 0, 0))],
        out_shape=[SDS((GDN_H, S, LANES), F32), SDS((GDN_H, nchunk, GDN_DK, LANES), BF16)],
        scratch_shapes=[pltpu.VMEM((GDN_H, GDN_DK, LANES), F32)],
        compiler_params=_cp(1),
    )(qkv, gc, bt, invs)


def gdn_core_bwd(qkv, gc, bt, states, invs, do, S):
    nchunk = S // GDN_C

    def body(qkv_ref, g_ref, b_ref, st_ref, inv_ref, do_ref, dqkv_ref, dg_ref, db_ref, ds_scr):
        n = pl.program_id(0)

        @pl.when(n == 0)
        def _():
            ds_scr[...] = jnp.zeros_like(ds_scr)

        heads = range(GDN_H)
        dS_cur = [ds_scr[h] for h in heads]
        for m in reversed(range(SCAN_CHUNKS)):
            sl = slice(m * GDN_C, (m + 1) * GDN_C)
            saved = [inv_ref[h, m] for h in heads]
            _, vjp = jax.vjp(lambda *a, _s=saved: _gdn_chunk(*a, invs=_s)[:2],
                             *[[qkv_ref[p, h, sl, :].astype(F32) for h in heads] for p in range(3)],
                             [g_ref[h, sl, :] for h in heads], [b_ref[h, sl, :] for h in heads],
                             [st_ref[h, m].astype(F32) for h in heads])
            dq, dk, dv, dg, db, dS_cur = vjp(([do_ref[h, sl, :] for h in heads], dS_cur))
            for h in heads:
                dqkv_ref[0, h, sl, :] = dq[h]
                dqkv_ref[1, h, sl, :] = dk[h]
                dqkv_ref[2, h, sl, :] = dv[h]
                dg_ref[h, sl, :] = dg[h]
                db_ref[h, sl, :] = db[h]
        for h in heads:
            ds_scr[h] = dS_cur[h]

    nblk = nchunk // SCAN_CHUNKS
    rows = SCAN_CHUNKS * GDN_C
    rev = lambda n: nblk - 1 - n
    blk3 = pl.BlockSpec((3, GDN_H, rows, LANES), lambda n: (0, 0, rev(n), 0))
    hb = pl.BlockSpec((GDN_H, rows, LANES), lambda n: (0, rev(n), 0))
    return pl.pallas_call(
        body, name="gdn_core_bwd", grid=(nblk,),
        in_specs=[blk3, hb, hb, pl.BlockSpec((GDN_H, SCAN_CHUNKS, GDN_DK, LANES), lambda n: (0, rev(n), 0, 0)),
                  pl.BlockSpec((GDN_H, SCAN_CHUNKS, GDN_C, GDN_C), lambda n: (0, rev(n), 0, 0)), hb],
        out_specs=[blk3, hb, hb],
        out_shape=[SDS((3, GDN_H, S, LANES), F32), SDS((GDN_H, S, LANES), F32), SDS((GDN_H, S, LANES), F32)],
        scratch_shapes=[pltpu.VMEM((GDN_H, GDN_DK, LANES), F32)],
        compiler_params=_cp(1),
    )(qkv, gc, bt, states, invs, do)


GDN_MAIN = 4 * GDN_H * LANES
GDN_PROJ = GDN_MAIN + LANES
RT = 512


def gdn_forward(h, w_in, conv_w, alog, dtb, out_gain, w_out):
    S = h.shape[0]
    nt = S // RT
    proj = matmul(h, w_in, "nn", BF16, "gdn_in")
    qkv = gdn_pre(proj, conv_w, S)
    ab_row = Row(proj, (RT, LANES), lambda i: (i, GDN_MAIN // LANES), gdtype=BF16, gshape=(S, LANES), gimap=lambda i: (i, 0))
    hm = lambda i: (0, i, 0)
    hv = GDN_H * LANES
    gc, bt = rowwise(f_gdn_gates, [ab_row], [alog, dtb],
                     [Out((GDN_H, S, LANES), F32, (GDN_H, RT, LANES), hm, lead=GDN_H)] * 2, (nt,), "gdn_gates")
    invs = gdn_inverses(qkv, gc, bt, S)
    o, states = gdn_core(qkv, gc, bt, invs, S)
    o_row = Row(o, (GDN_H, RT, LANES), hm, lead=GDN_H)
    z_row = Row(proj, (RT, hv), lambda i: (i, 3), splits=[LANES] * GDN_H, gdtype=BF16, gshape=(S, hv), gimap=lambda i: (i, 0))
    (on,) = rowwise(f_gdn_post, [o_row, z_row], [out_gain],
                    [Out((S, hv), BF16, (RT, hv), lambda i: (i, 0), splits=[LANES] * GDN_H)], (nt,), "gdn_post")
    y = matmul(on, w_out, "nn", BF16, "gdn_out")
    saved = dict(h=h, proj=proj, qkv=qkv, gc=gc, bt=bt, states=states, invs=invs, o=o, on=on, ab_row=ab_row, o_row=o_row, z_row=z_row)
    return y, saved


def gdn_backward(dy, sv, w_in, conv_w, alog, dtb, out_gain, w_out, on_weight_grads=None):
    S = dy.shape[0]
    nt = S // RT
    hm = lambda i: (0, i, 0)
    hv = GDN_H * LANES
    don = matmul(dy, w_out, "nt", BF16, "gdn_out_dx")
    d_w_out = matmul(sv["on"], dy, "tn", F32, "gdn_out_dw")
    (do, dz), (d_gain,) = rowwise_bwd(f_gdn_post, [sv["o_row"], sv["z_row"]], [out_gain],
                                      [Row(don, (RT, hv), lambda i: (i, 0), splits=[LANES] * GDN_H)], (nt,), "gdn_post_bwd")
    dqkv, dgc, dbt = gdn_core_bwd(sv["qkv"], sv["gc"], sv["bt"], sv["states"], sv["invs"], do, S)
    head_blk = lambda a: Row(a, (GDN_H, RT, LANES), hm, lead=GDN_H)
    (dab,), (d_alog, d_dtb) = rowwise_bwd(f_gdn_gates, [sv["ab_row"]], [alog, dtb], [head_blk(dgc), head_blk(dbt)],
                                          (nt,), "gdn_gates_bwd")
    dqkv_proj, d_conv = gdn_pre_bwd(sv["proj"], conv_w, dqkv, S)
    dproj = jnp.concatenate([dqkv_proj, dz, dab], axis=1)
    d_w_in = matmul(sv["h"], dproj, "tn", F32, "gdn_in_dw")
    if on_weight_grads is not None:
        w_in = w_in + on_weight_grads(d_w_in, d_w_out).astype(w_in.dtype)
    dh = matmul(dproj, w_in, "nt", BF16, "gdn_in_dx")
    return dh, dict(w_in=d_w_in, conv=d_conv, alog=d_alog, dtb=d_dtb, gain=d_gain, w_out=d_w_out)


QB = DSW_SPAN
N_HP = DSW_HG // LANES


def _bucket_maps():
    a = np.arange(QB)[:, None]
    j = np.arange(2 * QB)[None, :]
    dist = QB + a - j
    band = (dist >= 0) & (dist <= DSW_SPAN)
    maps = []
    for _, dil in DSW_GROUPS:
        dd = np.maximum(dist, 0) * dil
        max_exact = REL_BUCKETS // 2
        scaled = np.log(np.maximum(dd, 1).astype(np.float32) / np.float32(max_exact)) / np.float32(math.log(REL_MAX_DIST / max_exact))
        large = max_exact + (scaled * np.float32(REL_BUCKETS - max_exact)).astype(np.int32)
        large = np.minimum(large, REL_BUCKETS - 1)
        maps.append(np.where(dd < max_exact, dd, large).astype(np.int32))
    return np.stack(maps), band


def dsw_bias(rel_bias):
    maps, band = _bucket_maps()
    maps = np.where(band[None], maps, -1).astype(np.int32)

    def body(tab_ref, bk_ref, o_ref):
        gh = pl.program_id(0)
        bk = bk_ref[...]
        acc = jnp.full(bk.shape, NEG, F32)
        for b in range(REL_BUCKETS):
            acc = jnp.where(bk == b, tab_ref[b, gh], acc)
        o_ref[...] = acc

    return pl.pallas_call(
        body, name="dsw_bias", grid=(3 * GDN_H,),
        in_specs=[pl.BlockSpec(memory_space=pltpu.SMEM),
                  pl.BlockSpec((None, QB, 2 * QB), lambda gh: (gh // GDN_H, 0, 0))],
        out_specs=pl.BlockSpec((None, QB, 2 * QB), lambda gh: (gh, 0, 0)),
        out_shape=SDS((3 * GDN_H, QB, 2 * QB), F32),
        compiler_params=_cp(1),
    )(rel_bias, jnp.asarray(maps))


def dsw_bias_grad(dbias):
    maps, band = _bucket_maps()
    maps = np.where(band[None], maps, -1).astype(np.int32)

    def body(d_ref, bk_ref, o_ref):
        bk = bk_ref[...]
        d = d_ref[...]
        rows = lax.broadcasted_iota(jnp.int32, (REL_BUCKETS, LANES), 0)
        acc = jnp.zeros((REL_BUCKETS, LANES), F32)
        for b in range(REL_BUCKETS):
            part = jnp.sum(jnp.where(bk == b, d, 0.0), axis=0, keepdims=True)
            val = jnp.sum(part, axis=1, keepdims=True)
            acc = jnp.where(rows == b, val, acc)
        o_ref[...] = acc

    return pl.pallas_call(
        body, name="dsw_bias_grad", grid=(3 * GDN_H,),
        in_specs=[pl.BlockSpec((None, QB, 2 * QB), lambda gh: (gh, 0, 0)),
                  pl.BlockSpec((None, QB, 2 * QB), lambda gh: (gh // GDN_H, 0, 0))],
        out_specs=pl.BlockSpec((None, REL_BUCKETS, LANES), lambda gh: (gh, 0, 0)),
        out_shape=SDS((3 * GDN_H, REL_BUCKETS, LANES), F32),
        compiler_params=_cp(1),
    )(dbias, jnp.asarray(maps))


def _nt(a, b):
    return lax.dot_general(a, b, (((1,), (1,)), ((), ())), preferred_element_type=F32)


def _tn(a, b):
    return lax.dot_general(a, b, (((0,), (0,)), ((), ())), preferred_element_type=F32)


N_LB = DSW_HG // LANES
HALF = DSW_DH // 2


def _lanes(j):
    return slice(LANES * j, LANES * (j + 1))


def _deinterleave(stage, out_ref, dil, rows, dtype):
    for r in range(dil):
        for j in range(N_LB):
            out_ref[r, :, _lanes(j)] = stage[j, pl.ds(r, rows, stride=dil), :].astype(dtype)


def _interleave(in_ref, stage, dil, rows):
    for r in range(dil):
        for j in range(N_LB):
            stage[j, pl.ds(r, rows, stride=dil), :] = in_ref[r, :, _lanes(j)].astype(F32)


def dsw_prep(proj, q_gain2, k_gain2, gi, S):
    dil = DSW_GROUPS[gi][1]
    nt, rows = S // RT, RT // dil

    def body(q_ref, k_ref, v_ref, qg_ref, kg_ref, qo_ref, ko_ref, vo_ref, stage):
        for src, gain_ref, scale, dst in ((q_ref, qg_ref, DSW_DH ** -0.5, qo_ref), (k_ref, kg_ref, 1.0, ko_ref), (v_ref, None, None, vo_ref)):
            for j in range(N_LB):
                val = src[:, _lanes(j)].astype(F32)
                val = val if gain_ref is None else _qknorm1(val, gain_ref[...], scale)
                if dil == 1:
                    dst[0, :, _lanes(j)] = val.astype(BF16)
                else:
                    stage[j] = val
            if dil > 1:
                _deinterleave(stage, dst, dil, rows, BF16)

    col = lambda which: pl.BlockSpec((RT, DSW_HG), lambda i, _c=which * 3 + gi: (i, _c))
    gspec = pl.BlockSpec((1, LANES), lambda i: (0, 0))
    ospec = pl.BlockSpec((dil, rows, DSW_HG), lambda i: (0, i, 0))
    return pl.pallas_call(
        body, name=f"dsw_prep_g{gi}", grid=(nt,),
        in_specs=[col(0), col(1), col(2), gspec, gspec], out_specs=[ospec] * 3,
        out_shape=[SDS((dil, S // dil, DSW_HG), BF16)] * 3,
        scratch_shapes=[pltpu.VMEM((N_LB, RT, LANES), F32)], compiler_params=_cp(1),
    )(proj, proj, proj, q_gain2, k_gain2)


def dsw_prep_bwd(proj, q_gain2, k_gain2, dqd, dkd, dvd, gi, S):
    dil = DSW_GROUPS[gi][1]
    nt, rows = S // RT, RT // dil

    def body(q_ref, k_ref, qg_ref, kg_ref, dq_ref, dk_ref, dv_ref, oq_ref, ok_ref, ov_ref, dqg_ref, dkg_ref, stage):
        i = pl.program_id(0)

        @pl.when(i == 0)
        def _():
            dqg_ref[...] = jnp.zeros_like(dqg_ref)
            dkg_ref[...] = jnp.zeros_like(dkg_ref)

        for src, gain_ref, scale, cot_ref, dst, dg_ref in ((q_ref, qg_ref, DSW_DH ** -0.5, dq_ref, oq_ref, dqg_ref),
                                                          (k_ref, kg_ref, 1.0, dk_ref, ok_ref, dkg_ref)):
            if dil > 1:
                _interleave(cot_ref, stage, dil, rows)
            for j in range(N_LB):
                _, vjp = jax.vjp(lambda x, g, _s=scale: _qknorm1(x, g, _s), src[:, _lanes(j)].astype(F32), gain_ref[...])
                dx, dg = vjp(stage[j] if dil > 1 else cot_ref[0, :, _lanes(j)].astype(F32))
                dst[:, _lanes(j)] = dx.astype(dst.dtype)
                dg_ref[...] += dg
        if dil > 1:
            _interleave(dv_ref, stage, dil, rows)
        for j in range(N_LB):
            ov_ref[:, _lanes(j)] = (stage[j] if dil > 1 else dv_ref[0, :, _lanes(j)]).astype(ov_ref.dtype)

    col = lambda which: pl.BlockSpec((RT, DSW_HG), lambda i, _c=which * 3 + gi: (i, _c))
    gspec = pl.BlockSpec((1, LANES), lambda i: (0, 0))
    dspec = pl.BlockSpec((dil, rows, DSW_HG), lambda i: (0, i, 0))
    nspec = pl.BlockSpec((RT, DSW_HG), lambda i: (i, 0))
    return pl.pallas_call(
        body, name=f"dsw_prep_bwd_g{gi}", grid=(nt,),
        in_specs=[col(0), col(1), gspec, gspec, dspec, dspec, dspec], out_specs=[nspec] * 3 + [gspec] * 2,
        out_shape=[SDS((S, DSW_HG), BF16)] * 3 + [SDS((1, LANES), F32)] * 2,
        scratch_shapes=[pltpu.VMEM((N_LB, RT, LANES), F32)], compiler_params=_cp(1),
    )(proj, proj, q_gain2, k_gain2, dqd, dkd, dvd)


def _head_masks(rows):
    lane = lax.broadcasted_iota(jnp.int32, (rows, LANES), 1)
    return lane < DSW_DH, (lane % DSW_DH) < HALF


def dsw_attn_fwd(qd, kd, vd, bias, gi, S):
    dil = DSW_GROUPS[gi][1]
    sd = S // dil
    nq = sd // QB
    QPS = max(b for b in (1, 2, 4) if nq % b == 0)

    def body(q_ref, k_ref, v_ref, b_ref, o_ref, l_ref, kp_scr, vp_scr):
        i = pl.program_id(1)

        @pl.when(i == 0)
        def _():
            kp_scr[...] = jnp.zeros_like(kp_scr)
            vp_scr[...] = jnp.zeros_like(vp_scr)

        lo_q, _ = _head_masks(QB)
        lo_k, _ = _head_masks(2 * QB)
        col = lax.broadcasted_iota(jnp.int32, (QB, 2 * QB), 1)
        first = jnp.logical_and(i == 0, col < QB)
        hps, heads = range(N_HP), range(2 * N_HP)
        mqs = [lo_q if h % 2 == 0 else jnp.logical_not(lo_q) for h in heads]
        mks = [lo_k if h % 2 == 0 else jnp.logical_not(lo_k) for h in heads]
        k_last, v_last = k_ref[(QPS - 1) * QB:, :], v_ref[(QPS - 1) * QB:, :]
        for m in range(QPS):
            rows = slice(m * QB, (m + 1) * QB)
            before = slice((m - 1) * QB, m * QB)
            kps = [kp_scr[:, _lanes(hp)] if m == 0 else k_ref[before, _lanes(hp)] for hp in hps]
            vps = [vp_scr[:, _lanes(hp)] if m == 0 else v_ref[before, _lanes(hp)] for hp in hps]
            k2s = [jnp.concatenate([kps[hp], k_ref[rows, _lanes(hp)]], axis=0) for hp in hps]
            v2s = [jnp.concatenate([vps[hp], v_ref[rows, _lanes(hp)]], axis=0) for hp in hps]
            qs = [q_ref[rows, _lanes(hp)] for hp in hps]
            ss = [_nt(jnp.where(mqs[h], qs[h // 2], 0).astype(BF16), k2s[h // 2]) + b_ref[h] for h in heads]
            if m == 0:
                ss = [jnp.where(first, NEG, s) for s in ss]
            mxs = [jnp.max(s, axis=1, keepdims=True) for s in ss]
            ps = [jnp.exp(s - mx) for s, mx in zip(ss, mxs)]
            ls = [jnp.sum(p, axis=1, keepdims=True) for p in ps]
            ohs = [jnp.dot(ps[h].astype(BF16), jnp.where(mks[h], v2s[h // 2], 0).astype(BF16), preferred_element_type=F32) / ls[h]
                   for h in heads]
            lse_h = [mx + jnp.log(l) for mx, l in zip(mxs, ls)]
            for hp in hps:
                o_ref[rows, _lanes(hp)] = ohs[2 * hp] + ohs[2 * hp + 1]
                l_ref[rows, _lanes(hp)] = jnp.where(lo_q, lse_h[2 * hp], lse_h[2 * hp + 1])
        kp_scr[...] = k_last
        vp_scr[...] = v_last

    blk = pl.BlockSpec((None, QPS * QB, DSW_HG), lambda r, i: (r, i, 0))
    return pl.pallas_call(
        body, name=f"dsw_attn_g{gi}", grid=(dil, nq // QPS),
        in_specs=[blk, blk, blk, pl.BlockSpec((GDN_H, QB, 2 * QB), lambda r, i: (gi, 0, 0))],
        out_specs=[blk, blk], out_shape=[SDS((dil, sd, DSW_HG), F32)] * 2,
        scratch_shapes=[pltpu.VMEM((QB, DSW_HG), BF16)] * 2, compiler_params=_cp(2),
    )(qd, kd, vd, bias)


def dsw_attn_bwd(qd, kd, vd, bias, dod, statd, gi, S):
    dil = DSW_GROUPS[gi][1]
    sd = S // dil
    nq = sd // QB
    B = max(b for b in (1, 2, 4, 8) if nq % b == 0)
    nb = nq // B
    steps = dil * nb
    cur = lambda t: jnp.minimum(t, steps - 1)
    done = lambda t: jnp.maximum(t - 1, 0)
    last = slice((B - 1) * QB, B * QB)

    def body(q_ref, k_ref, v_ref, b_ref, do_ref, st_ref, dq_ref, dk_ref, dv_ref, db_ref, kp_scr, vp_scr, dk_scr, dv_scr):
        t = pl.program_id(0)

        @pl.when(t == 0)
        def _():
            db_ref[...] = jnp.zeros_like(db_ref)
            for scr in (kp_scr, vp_scr, dk_scr, dv_scr):
                scr[...] = jnp.zeros_like(scr)

        @pl.when(t < steps)
        def _():
            lo_q, first_half = _head_masks(QB)
            col = lax.broadcasted_iota(jnp.int32, (QB, 2 * QB), 1)
            first = jnp.logical_and(t % nb == 0, col < QB)
            hps, heads = range(N_HP), range(2 * N_HP)
            mqs = [lo_q if h % 2 == 0 else jnp.logical_not(lo_q) for h in heads]
            k_last, v_last = k_ref[last, :], v_ref[last, :]
            to_prev, to_cur = [], []
            for m in range(B):
                rows = slice(m * QB, (m + 1) * QB)
                before = slice((m - 1) * QB, m * QB)
                kps = [kp_scr[:, _lanes(hp)] if m == 0 else k_ref[before, _lanes(hp)] for hp in hps]
                vps = [vp_scr[:, _lanes(hp)] if m == 0 else v_ref[before, _lanes(hp)] for hp in hps]
                k2s = [jnp.concatenate([kps[hp], k_ref[rows, _lanes(hp)]], axis=0) for hp in hps]
                v2s = [jnp.concatenate([vps[hp], v_ref[rows, _lanes(hp)]], axis=0) for hp in hps]
                qs = [q_ref[rows, _lanes(hp)] for hp in hps]
                douts = [do_ref[rows, _lanes(hp)] for hp in hps]
                stats = [st_ref[rows, _lanes(hp)] for hp in hps]
                qms = [jnp.where(mqs[h], qs[h // 2], 0).astype(BF16) for h in heads]
                doms = [jnp.where(mqs[h], douts[h // 2], 0).astype(BF16) for h in heads]
                ss = [_nt(qms[h], k2s[h // 2]) + b_ref[h] for h in heads]
                if m == 0:
                    ss = [jnp.where(first, NEG, s) for s in ss]
                lses = [jnp.max(jnp.where(jnp.logical_and(mqs[h], first_half), stats[h // 2], NEG), axis=1, keepdims=True) for h in heads]
                deltas = [jnp.max(jnp.where(jnp.logical_and(mqs[h], jnp.logical_not(first_half)), stats[h // 2], NEG), axis=1,
                                  keepdims=True) for h in heads]
                ps = [jnp.exp(ss[h] - lses[h]) for h in heads]
                dss = [ps[h] * (_nt(doms[h], v2s[h // 2]) - deltas[h]) for h in heads]
                dsbs = [d.astype(BF16) for d in dss]
                dqh = [jnp.where(mqs[h], jnp.dot(dsbs[h], k2s[h // 2], preferred_element_type=F32), 0.0) for h in heads]
                dkh = [_tn(dsbs[h], qms[h]) for h in heads]
                dvh = [_tn(ps[h].astype(BF16), doms[h]) for h in heads]
                for h in heads:
                    db_ref[h] += dss[h]
                for hp in hps:
                    dq_ref[rows, _lanes(hp)] = dqh[2 * hp] + dqh[2 * hp + 1]
                dk2s = [dkh[2 * hp] + dkh[2 * hp + 1] for hp in hps]
                dv2s = [dvh[2 * hp] + dvh[2 * hp + 1] for hp in hps]
                to_prev.append(([d[:QB] for d in dk2s], [d[:QB] for d in dv2s]))
                to_cur.append(([d[QB:] for d in dk2s], [d[QB:] for d in dv2s]))
            for hp in hps:
                if B > 1:
                    dk_ref[:(B - 1) * QB, _lanes(hp)] = dk_scr[:(B - 1) * QB, _lanes(hp)]
                    dv_ref[:(B - 1) * QB, _lanes(hp)] = dv_scr[:(B - 1) * QB, _lanes(hp)].astype(dv_ref.dtype)
                dk_ref[last, _lanes(hp)] = dk_scr[last, _lanes(hp)] + to_prev[0][0][hp]
                dv_ref[last, _lanes(hp)] = (dv_scr[last, _lanes(hp)] + to_prev[0][1][hp]).astype(dv_ref.dtype)
            for hp in hps:
                for m in range(B):
                    rows = slice(m * QB, (m + 1) * QB)
                    nk, nv = to_cur[m][0][hp], to_cur[m][1][hp]
                    if m < B - 1:
                        nk, nv = nk + to_prev[m + 1][0][hp], nv + to_prev[m + 1][1][hp]
                    dk_scr[rows, _lanes(hp)] = nk
                    dv_scr[rows, _lanes(hp)] = nv
            kp_scr[...] = k_last
            vp_scr[...] = v_last

        @pl.when(t == steps)
        def _():
            dk_ref[...] = dk_scr[...]
            dv_ref[...] = dv_scr[...].astype(dv_ref.dtype)

    blk = pl.BlockSpec((None, B * QB, DSW_HG), lambda t: (cur(t) // nb, cur(t) % nb, 0))
    oblk = pl.BlockSpec((None, B * QB, DSW_HG), lambda t: (done(t) // nb, done(t) % nb, 0))
    return pl.pallas_call(
        body, name=f"dsw_attn_bwd_g{gi}", grid=(steps + 1,),
        in_specs=[blk, blk, blk, pl.BlockSpec((GDN_H, QB, 2 * QB), lambda t: (gi, 0, 0)), blk, blk],
        out_specs=[blk, oblk, oblk, pl.BlockSpec((GDN_H, QB, 2 * QB), lambda t: (0, 0, 0))],
        out_shape=[SDS((dil, sd, DSW_HG), F32), SDS((dil, sd, DSW_HG), F32), SDS((dil, sd, DSW_HG), BF16),
                   SDS((GDN_H, QB, 2 * QB), F32)],
        scratch_shapes=[pltpu.VMEM((QB, DSW_HG), BF16)] * 2 + [pltpu.VMEM((B * QB, DSW_HG), F32)] * 2,
        compiler_params=_cp(1),
    )(qd, kd, vd, bias, dod, statd)


def dsw_combine(ods, lseds, S):
    nt = S // RT
    dils = [d for _, d in DSW_GROUPS]

    def body(*refs):
        ins, (o_ref, l_ref), stages = refs[:6], refs[6:8], refs[8:]
        for g in range(3):
            if dils[g] > 1:
                _interleave(ins[g], stages[g], dils[g], RT // dils[g])
                _interleave(ins[3 + g], stages[3 + g], dils[g], RT // dils[g])
        for j in range(N_LB):
            natural = lambda a: stages[a][j] if dils[a % 3] > 1 else ins[a][0, :, _lanes(j)]
            o, lse = f_combine(None, *[natural(a) for a in range(6)])
            o_ref[:, _lanes(j)] = o.astype(o_ref.dtype)
            l_ref[:, _lanes(j)] = lse

    dspec = lambda d: pl.BlockSpec((d, RT // d, DSW_HG), lambda i: (0, i, 0))
    nspec = pl.BlockSpec((RT, DSW_HG), lambda i: (i, 0))
    return pl.pallas_call(
        body, name="dsw_combine", grid=(nt,),
        in_specs=[dspec(d) for d in dils] * 2, out_specs=[nspec, nspec],
        out_shape=[SDS((S, DSW_HG), BF16), SDS((S, DSW_HG), F32)],
        scratch_shapes=[pltpu.VMEM((N_LB, RT, LANES), F32)] * 6, compiler_params=_cp(1),
    )(*ods, *lseds)


def dsw_bwd_prep(do, o, lse, S):
    nt = S // RT
    dils = [d for _, d in DSW_GROUPS]

    def body(do_ref, o_ref, l_ref, *rest):
        outs, (st_do, st_stat) = rest[:6], rest[6:]
        lo, first_half = _head_masks(RT)
        for j in range(N_LB):
            dout = do_ref[:, _lanes(j)]
            prod = dout * o_ref[:, _lanes(j)].astype(F32)
            s_all = jnp.sum(prod, axis=1, keepdims=True)
            s_lo = jnp.sum(jnp.where(lo, prod, 0.0), axis=1, keepdims=True)
            delta = jnp.where(lo, s_lo, s_all - s_lo)
            stat = jnp.where(first_half, l_ref[:, _lanes(j)], delta)
            st_do[j] = dout
            st_stat[j] = stat
            for g in range(3):
                if dils[g] == 1:
                    outs[g][0, :, _lanes(j)] = dout.astype(BF16)
                    outs[3 + g][0, :, _lanes(j)] = stat
        for g in range(3):
            if dils[g] > 1:
                _deinterleave(st_do, outs[g], dils[g], RT // dils[g], BF16)
                _deinterleave(st_stat, outs[3 + g], dils[g], RT // dils[g], F32)

    nspec = pl.BlockSpec((RT, DSW_HG), lambda i: (i, 0))
    dspec = lambda d: pl.BlockSpec((d, RT // d, DSW_HG), lambda i: (0, i, 0))
    res = pl.pallas_call(
        body, name="dsw_bwd_prep", grid=(nt,),
        in_specs=[nspec] * 3, out_specs=[dspec(d) for d in dils] * 2,
        out_shape=[SDS((d, S // d, DSW_HG), BF16) for d in dils] + [SDS((d, S // d, DSW_HG), F32) for d in dils],
        scratch_shapes=[pltpu.VMEM((N_LB, RT, LANES), F32)] * 2, compiler_params=_cp(1),
    )(do, o, lse)
    return res[:3], res[3:]


def dsw_forward(h, w_in, q_gain2, k_gain2, rel_bias, w_out):
    S = h.shape[0]
    proj = matmul(h, w_in, "nn", BF16, "dsw_in", col_shards=N_SHARD)
    bias = dsw_bias(rel_bias)
    qkv, ods, lseds = [], [], []
    for gi in range(3):
        qd, kd, vd = dsw_prep(proj, q_gain2, k_gain2, gi, S)
        od, ld = dsw_attn_fwd(qd, kd, vd, bias, gi, S)
        qkv.append((qd, kd, vd))
        ods.append(od)
        lseds.append(ld)
    o, lse = dsw_combine(ods, lseds, S)
    y = matmul(o, w_out, "nn", BF16, "dsw_out", col_shards=N_SHARD)
    return y, dict(h=h, proj=proj, qkv=qkv, bias=bias, o=o, lse=lse)


def dsw_backward(dy, sv, w_in, q_gain2, k_gain2, w_out):
    S = dy.shape[0]
    do = matmul(dy, w_out, "nt", F32, "dsw_out_dx", col_shards=N_SHARD)
    d_w_out = matmul(sv["o"], dy, "tn", F32, "dsw_out_dw", col_shards=N_SHARD)
    dods, statds = dsw_bwd_prep(do, sv["o"], sv["lse"], S)
    pieces_q, pieces_k, pieces_v, dbs = [], [], [], []
    d_qg = jnp.zeros((1, LANES), F32)
    d_kg = jnp.zeros((1, LANES), F32)
    for gi in range(3):
        qd, kd, vd = sv["qkv"][gi]
        dqd, dkd, dvd, db = dsw_attn_bwd(qd, kd, vd, sv["bias"], dods[gi], statds[gi], gi, S)
        dq, dk, dv, dqg, dkg = dsw_prep_bwd(sv["proj"], q_gain2, k_gain2, dqd, dkd, dvd, gi, S)
        dbs.append(db)
        pieces_q.append(dq)
        pieces_k.append(dk)
        pieces_v.append(dv)
        d_qg = d_qg + dqg
        d_kg = d_kg + dkg
    dproj = jnp.concatenate(pieces_q + pieces_k + pieces_v, axis=1)
    d_w_in = matmul(sv["h"], dproj, "tn", F32, "dsw_in_dw", col_shards=N_SHARD)
    dh = matmul(dproj, w_in, "nt", BF16, "dsw_in_dx", col_shards=N_SHARD)
    d_rel = dsw_bias_grad(jnp.concatenate(dbs, axis=0))
    return dh, dict(w_in=d_w_in, q_gain2=d_qg, k_gain2=d_kg, rel=d_rel, w_out=d_w_out)


FUSE_M = 512


def ffn_in_act(h, w_in, name):
    S = h.shape[0]
    half = FFN // 2

    def body(h_ref, wg_ref, wu_ref, gu_ref, a_ref):
        j = pl.program_id(1)
        sub = FUSE_M // 2
        for part in range(2):
            rows = slice(part * sub, (part + 1) * sub)
            hb = h_ref[rows, :]
            g = jnp.dot(hb, wg_ref[...], preferred_element_type=F32)
            u = jnp.dot(hb, wu_ref[...], preferred_element_type=F32)
            a_ref[rows, :] = (_silu(g) * u).astype(a_ref.dtype)
            for jj in range(2):
                @pl.when(j == jj)
                def _(g=g, u=u, jj=jj, rows=rows):
                    gu_ref[rows, jj * half:(jj + 1) * half] = g.astype(gu_ref.dtype)
                    gu_ref[rows, FFN + jj * half:FFN + (jj + 1) * half] = u.astype(gu_ref.dtype)

    return pl.pallas_call(
        body, name=name, grid=(S // FUSE_M, 2),
        in_specs=[pl.BlockSpec((FUSE_M, D), lambda i, j: (i, 0)),
                  pl.BlockSpec((None, D, half), lambda i, j: (j, 0, 0)),
                  pl.BlockSpec((None, D, half), lambda i, j: (j + 2, 0, 0))],
        out_specs=[pl.BlockSpec((FUSE_M, 2 * FFN), lambda i, j: (i, 0)), pl.BlockSpec((FUSE_M, half), lambda i, j: (i, j))],
        out_shape=[SDS((S, 2 * FFN), BF16), SDS((S, FFN), BF16)],
        compiler_params=_cp(2),
    )(h, w_in, w_in)


def ffn_forward(h, w_in, w_out, tag):
    gu, a = ffn_in_act(h, w_in, f"ffn_in_act_{tag}")
    f = matmul(a, w_out, "nn", BF16, f"ffn_out_{tag}")
    return f, dict(h=h, gu=gu, a=a)


def ffn_out_dx_act(df, w_out, gu, name):
    S = df.shape[0]
    half = FFN // 2

    def body(df_ref, w_ref, g_ref, u_ref, dgu_ref):
        j = pl.program_id(1)
        sub = FUSE_M // 2
        for part in range(2):
            rows = slice(part * sub, (part + 1) * sub)
            da = _nt(df_ref[rows, :], w_ref[...])
            dg, du = _swiglu_bwd((g_ref[rows, :].astype(F32), u_ref[rows, :].astype(F32)), da)
            for jj in range(2):
                @pl.when(j == jj)
                def _(dg=dg, du=du, jj=jj, rows=rows):
                    dgu_ref[rows, jj * half:(jj + 1) * half] = dg.astype(dgu_ref.dtype)
                    dgu_ref[rows, FFN + jj * half:FFN + (jj + 1) * half] = du.astype(dgu_ref.dtype)

    return pl.pallas_call(
        body, name=name, grid=(S // FUSE_M, 2),
        in_specs=[pl.BlockSpec((FUSE_M, D), lambda i, j: (i, 0)),
                  pl.BlockSpec((half, D), lambda i, j: (j, 0)),
                  pl.BlockSpec((FUSE_M, half), lambda i, j: (i, j)),
                  pl.BlockSpec((FUSE_M, half), lambda i, j: (i, j + 2))],
        out_specs=pl.BlockSpec((FUSE_M, 2 * FFN), lambda i, j: (i, 0)),
        out_shape=SDS((S, 2 * FFN), BF16),
        compiler_params=_cp(2),
    )(df, w_out, gu, gu)


def ffn_backward(df, sv, w_in, w_out, tag):
    d_w_out = matmul(sv["a"], df, "tn", F32, f"ffn_out_dw_{tag}")
    dgu = ffn_out_dx_act(df, w_out, sv["gu"], f"ffn_out_dx_act_{tag}")
    d_w_in = matmul(sv["h"], dgu, "tn", F32, f"ffn_in_dw_{tag}", col_shards=N_SHARD)
    dh = matmul(dgu, w_in, "nt", BF16, f"ffn_in_dx_{tag}", col_shards=N_SHARD)
    return dh, d_w_in, d_w_out


def f_norm_only(ids, x, gain, sc, sh):
    return (_normmod(x, gain, sc, sh),)


WT = 512


def _wide(a, **kw):
    return Row(a, (WT, D), lambda i: (i, 0), **kw)


def _wide_out(S, dtype):
    return Out((S, D), dtype, (WT, D), lambda i: (i, 0))


def adamw(w, g, m, v, name):
    shape = w.shape
    C = shape[-1]
    R = int(np.prod(shape[:-1]))
    w2, g2, m2, v2 = (a.reshape(R, C) for a in (w, g, m, v))
    br = R
    if R > 256:
        br = max(b for b in range(8, 257, 8) if R % b == 0)
    c1 = 1.0 / (1.0 - ADAM_B1 ** ADAM_STEP)
    c2 = 1.0 / (1.0 - ADAM_B2 ** ADAM_STEP)

    def body(w_ref, g_ref, m_ref, v_ref, d_ref, nm_ref, nv_ref):
        gg = g_ref[...]
        mm_ = ADAM_B1 * m_ref[...] + (1.0 - ADAM_B1) * gg
        vv = ADAM_B2 * v_ref[...] + (1.0 - ADAM_B2) * (gg * gg)
        d_ref[...] = -ADAM_LR * ((mm_ * c1) / (jnp.sqrt(vv * c2) + ADAM_EPS) + ADAM_WD * w_ref[...])
        nm_ref[...] = mm_
        nv_ref[...] = vv

    spec = pl.BlockSpec((br, C), lambda i: (i, 0))
    d, nm, nv = pl.pallas_call(
        body, name=name, grid=(R // br,), in_specs=[spec] * 4, out_specs=[spec] * 3,
        out_shape=[SDS((R, C), F32)] * 3, compiler_params=_cp(1),
    )(w2, g2, m2, v2)
    return d.reshape(shape), nm.reshape(shape), nv.reshape(shape)


def _place():
    x, y, c = lax.axis_index("x"), lax.axis_index("y"), lax.axis_index("c")
    chips = [(1 - x, y), (x, 1 - y), (1 - x, 1 - y)]
    return x, y, c, chips


def all_gather_small(blk, name):
    m_per, n = blk.shape

    def body(x_ref, out_ref, send_sems, recv_sems, local_sem):
        x, y, c, chips = _place()
        me, sibling = (x, y, c), (x, y, 1 - c)

        def rows(px, py, pc):
            return out_ref.at[pl.ds((4 * px + 2 * py + pc) * m_per, m_per), :]

        def copy(k, block, to, src=None):
            return pltpu.make_async_remote_copy(
                src_ref=rows(*block) if src is None else src, dst_ref=rows(*block),
                send_sem=send_sems.at[k], recv_sem=recv_sems.at[k], device_id=to, device_id_type=MESH)

        mine = pltpu.make_async_copy(x_ref, rows(*me), local_sem)
        mine.start()
        first = [copy(0, me, sibling, src=x_ref)]
        first += [copy(1 + j, me, (*chip, c), src=x_ref) for j, chip in enumerate(chips)]
        for cp in first:
            cp.start()
        passed = [copy(4 + j, (*chip, c), sibling) for j, chip in enumerate(chips)]
        for j, chip in enumerate(chips):
            copy(1 + j, (*chip, c), me).wait_recv()
            passed[j].start()
        copy(0, sibling, me).wait_recv()
        for j, chip in enumerate(chips):
            copy(4 + j, (*chip, 1 - c), me).wait_recv()
        for cp in first + passed:
            cp.wait_send()
        mine.wait()

    return pl.pallas_call(
        body, name=name, out_shape=SDS((N_DEV * m_per, n), blk.dtype),
        in_specs=[pl.BlockSpec(memory_space=pltpu.VMEM)], out_specs=pl.BlockSpec(memory_space=pltpu.VMEM),
        scratch_shapes=[pltpu.SemaphoreType.DMA((7,)), pltpu.SemaphoreType.DMA((7,)), pltpu.SemaphoreType.DMA],
    )(blk)


def _half(cc, rh):
    return pl.ds(pl.multiple_of(cc * rh, 16), rh)


def all_gather_shards(ws):
    n = len(ws)

    def body(*refs):
        w_refs, out_refs = refs[:n], refs[n:2 * n]
        send_sems, recv_sems, local_sems, own_sems = refs[2 * n:]
        x, y, c, chips = _place()
        sibling = (x, y, 1 - c)
        s_me = 2 * x + y

        def copy(k, src, dst, to):
            return pltpu.make_async_remote_copy(src_ref=src, dst_ref=dst, send_sem=send_sems.at[k], recv_sem=recv_sems.at[k],
                                                device_id=to, device_id_type=MESH)

        local, sends, passed = [], [], []
        for k in range(n):
            rh = ws[k].shape[0] // 2
            cp = pltpu.make_async_remote_copy(src_ref=w_refs[k], dst_ref=out_refs[k].at[s_me], send_sem=local_sems.at[k],
                                              recv_sem=own_sems.at[k], device_id=sibling, device_id_type=MESH)
            cp.start()
            local.append(cp)
            for j, chip in enumerate(chips):
                sd = copy(6 * k + j, w_refs[k].at[_half(c, rh)], out_refs[k].at[s_me, _half(c, rh)], (*chip, c))
                sd.start()
                sends.append(sd)
        for k in range(n):
            rh = ws[k].shape[0] // 2
            for j, (px, py) in enumerate(chips):
                got = out_refs[k].at[2 * px + py, _half(c, rh)]
                copy(6 * k + j, got, got, (px, py, c)).wait_recv()
                fw = copy(6 * k + 3 + j, got, got, sibling)
                fw.start()
                passed.append(fw)
        for k in range(n):
            rh = ws[k].shape[0] // 2
            for j, (px, py) in enumerate(chips):
                got = out_refs[k].at[2 * px + py, _half(1 - c, rh)]
                copy(6 * k + 3 + j, got, got, sibling).wait_recv()
        for cp in sends + passed:
            cp.wait_send()
        for cp in local:
            cp.wait()

    return pl.pallas_call(
        body, name="weights_all_gather", out_shape=[SDS((N_SHARD,) + w.shape, w.dtype) for w in ws],
        in_specs=[ANY] * n, out_specs=[ANY] * n,
        scratch_shapes=[pltpu.SemaphoreType.DMA((6 * n,)), pltpu.SemaphoreType.DMA((6 * n,)), pltpu.SemaphoreType.DMA((n,)),
                        pltpu.SemaphoreType.DMA((n,))],
    )(*ws)


def sibling_exchange(sends, name):
    n = len(sends)

    def body(*refs):
        s_refs, o_refs, send_sems, recv_sems = refs[:n], refs[n:2 * n], refs[2 * n], refs[2 * n + 1]
        x, y, c, _ = _place()
        cps = [pltpu.make_async_remote_copy(src_ref=s_refs[k], dst_ref=o_refs[k], send_sem=send_sems.at[k], recv_sem=recv_sems.at[k],
                                            device_id=(x, y, 1 - c), device_id_type=MESH) for k in range(n)]
        for cp in cps:
            cp.start()
        for cp in cps:
            cp.wait()

    return pl.pallas_call(
        body, name=name, out_shape=[SDS(s.shape, s.dtype) for s in sends], in_specs=[ANY] * n, out_specs=[ANY] * n,
        scratch_shapes=[pltpu.SemaphoreType.DMA((n,)), pltpu.SemaphoreType.DMA((n,))],
    )(*sends)


def add_rows(arrs, out_dtype, name, rt=256):
    Rr, W = arrs[0].shape

    def fn(ids, *vals):
        acc = vals[0]
        for v in vals[1:]:
            acc = acc + v
        return (acc,)

    t = rt if Rr % rt == 0 else max(b for b in range(16, rt + 1, 16) if Rr % b == 0)
    (out,) = rowwise(fn, [Row(a, (t, W), lambda i: (i, 0)) for a in arrs], [],
                     [Out((Rr, W), out_dtype, (t, W), lambda i: (i, 0))], (Rr // t,), name)
    return out


HBM_SPEC = pl.BlockSpec(memory_space=pltpu.HBM)
SEM_SPEC = pl.BlockSpec(memory_space=pltpu.SEMAPHORE)
DATAFLOW = pltpu.SideEffectType.DATAFLOW_SIDE_EFFECTING


def _in_hbm(a):
    return pltpu.with_memory_space_constraint(a, pltpu.HBM)


def _gather_copies(w_refs, land_refs, send_sems, recv_sems):
    x, y, c, chips = _place()
    targets = [(x, y, 1 - c)] + [(*chip, c) for chip in chips]
    cps = []
    for k, (w_ref, land_ref) in enumerate(zip(w_refs, land_refs)):
        for j, to in enumerate(targets):
            cps.append(pltpu.make_async_remote_copy(src_ref=w_ref, dst_ref=land_ref.at[2 * x + y], send_sem=send_sems.at[4 * k + j],
                                                    recv_sem=recv_sems.at[4 * k + j], device_id=to, device_id_type=MESH))
    return cps


def _scatter_copies(p_refs, land_refs, send_sems, recv_sems):
    x, y, c, chips = _place()
    cps = []
    for k, (p_ref, land_ref) in enumerate(zip(p_refs, land_refs)):
        for j, (px, py) in enumerate(chips):
            cps.append(pltpu.make_async_remote_copy(src_ref=p_ref.at[2 * px + py], dst_ref=land_ref.at[j], send_sem=send_sems.at[3 * k + j],
                                                    recv_sem=recv_sems.at[3 * k + j], device_id=(px, py, c), device_id_type=MESH))
    return cps


def copies_start(srcs, land_shapes, make_copies, per_src, name):
    n = len(srcs)
    m = per_src * n

    def body(*refs):
        src_refs, land_refs = refs[:n], refs[n:2 * n]
        send_sems, recv_sems, token = refs[2 * n], refs[2 * n + 1], refs[-1]
        for cp in make_copies(src_refs, land_refs, send_sems, recv_sems):
            cp.start()
        token[...] = jnp.zeros_like(token)

    lands = [lax.empty(shp, s.dtype) for shp, s in zip(land_shapes, srcs)]
    res = pl.pallas_call(
        body, name=name,
        out_shape=(pltpu.SemaphoreType.DMA((m,)), pltpu.SemaphoreType.DMA((m,)), *[pltpu.HBM(s.shape, s.dtype) for s in srcs],
                   *[pltpu.HBM(shp, s.dtype) for shp, s in zip(land_shapes, srcs)], SDS((8, LANES), F32)),
        in_specs=[HBM_SPEC] * (2 * n),
        out_specs=(SEM_SPEC, SEM_SPEC, *[HBM_SPEC] * (2 * n), pl.BlockSpec(memory_space=pltpu.VMEM)),
        input_output_aliases={i: 2 + i for i in range(2 * n)},
        compiler_params=pltpu.CompilerParams(has_side_effects=DATAFLOW),
    )(*[_in_hbm(s) for s in srcs], *[_in_hbm(l) for l in lands])
    return res[0], res[1], list(res[2:2 + n]), list(res[2 + n:2 + 2 * n]), res[-1]


def copies_wait(send_sems, recv_sems, srcs, lands, make_copies, after, name):
    n = len(srcs)

    def body(*refs):
        src_refs, land_refs = refs[:n], refs[n:2 * n]
        for cp in make_copies(src_refs, land_refs, refs[2 * n], refs[2 * n + 1]):
            cp.wait_send()
            cp.wait_recv()

    res = pl.pallas_call(
        body, name=name,
        out_shape=(*[pltpu.HBM(s.shape, s.dtype) for s in srcs], *[pltpu.HBM(l.shape, l.dtype) for l in lands]),
        in_specs=[HBM_SPEC] * (2 * n) + [SEM_SPEC, SEM_SPEC, ANY],
        out_specs=tuple([HBM_SPEC] * (2 * n)),
        input_output_aliases={i: i for i in range(2 * n)},
        compiler_params=pltpu.CompilerParams(has_side_effects=DATAFLOW),
    )(*srcs, *lands, send_sems, recv_sems, after)
    return list(res[n:])


def _pad_lanes(v):
    return jnp.concatenate([v.astype(F32), jnp.zeros((LANES - v.shape[0],), F32)])[None]


def kernel(x, c, w_ada, b_ada, norm_mix, norm_ffn, w_ffn_in, w_ffn_out, gdn_w_in, gdn_conv, gdn_a_log, gdn_dt_bias, gdn_out_norm, gdn_w_out, dsw_w_in, dsw_q_norm, dsw_k_norm, dsw_w_out, rel_bias, loss_target, m_w_ada, m_b_ada, m_norm_mix, m_norm_ffn, m_w_ffn_in, m_w_ffn_out, m_gdn_w_in, m_gdn_conv, m_gdn_a_log, m_gdn_dt_bias, m_gdn_out_norm, m_gdn_w_out, m_dsw_w_in, m_dsw_q_norm, m_dsw_k_norm, m_dsw_w_out, m_rel_bias, v_w_ada, v_b_ada, v_norm_mix, v_norm_ffn, v_w_ffn_in, v_w_ffn_out, v_gdn_w_in, v_gdn_conv, v_gdn_a_log, v_gdn_dt_bias, v_gdn_out_norm, v_gdn_w_out, v_dsw_w_in, v_dsw_q_norm, v_dsw_k_norm, v_dsw_w_out, v_rel_bias):
    S = x.shape[1]
    nt = S // WT
    xi, yi, ci = lax.axis_index("x"), lax.axis_index("y"), lax.axis_index("c")
    me = 4 * xi + 2 * yi + ci
    s_me = 2 * xi + yi
    x0, tgt = x[0], loss_target[0]

    whole = lambda a: Row(a, a.shape, lambda i: (0,) * a.ndim)
    (cond8,) = rowwise(lambda ids, v: (_silu(v),), [whole(c.reshape(8, LANES))], [], [Out((8, LANES), F32, (8, LANES), lambda i: (0, 0))], (1,), "cond")
    cond_all = all_gather_small(cond8, "gather_cond").reshape(N_DEV, D)
    cond16 = jnp.concatenate([cond_all, jnp.zeros((8, D), F32)], axis=0)
    ada_cols = w_ada.shape[2]
    mods = [matmul(cond16, w_ada[l], "nn", F32, f"ada_{l}")[:N_DEV] for l in range(2)]
    buf = jnp.concatenate([jnp.stack(mods, axis=1).reshape(-1, LANES), gdn_conv.reshape(-1, LANES)], axis=0)
    n_mod_rows = N_DEV * 2 * ada_cols // LANES
    got = all_gather_small(buf, "gather_mod").reshape(N_DEV, buf.shape[0], LANES)
    mod_parts, conv_parts = [], []
    for s in range(N_SHARD):
        from_dev = got[2 * s]
        mod_parts.append(lax.dynamic_index_in_dim(from_dev[:n_mod_rows].reshape(N_DEV, 2, ada_cols), me, 0, keepdims=False))
        conv_parts.append(from_dev[n_mod_rows:].reshape(4, -1))
    mod_nb = jnp.concatenate(mod_parts, axis=1)
    conv_w = jnp.concatenate(conv_parts, axis=1)
    (mod,) = rowwise(lambda ids, a, b: (a + b,), [whole(mod_nb), whole(b_ada)], [], [Out(mod_nb.shape, F32, mod_nb.shape, lambda i: (0, 0))], (1,), "mod_bias")
    mod = mod.reshape(2, 6, 1, D)
    sh1, sc1, g1, sh2, sc2, g2 = ([mod[l, k] for l in range(2)] for k in range(6))
    gmix = [norm_mix[l][None] for l in range(2)]
    gffn = [norm_ffn[l][None] for l in range(2)]

    gcols = gdn_w_in.shape[2]
    g_gdn_in, g_gdn_out = all_gather_shards([gdn_w_in[0].astype(BF16), gdn_w_out[0].astype(BF16)])
    gathered = lambda ws: [(N_SHARD,) + w.shape for w in ws]
    gate = (jnp.minimum(jnp.abs(g_gdn_in[0, 0, 0].astype(F32)), 0.0) + jnp.minimum(jnp.abs(mod[0, 0, 0, 0]), 0.0)).astype(BF16)
    w2 = [w_ffn_in[0].astype(BF16) + gate, w_ffn_out[0].astype(BF16) + gate]
    w3 = [dsw_w_in[0].astype(BF16) + gate, dsw_w_out[0].astype(BF16) + gate, w_ffn_in[1].astype(BF16) + gate, w_ffn_out[1].astype(BF16) + gate]
    fly2 = copies_start(w2, gathered(w2), _gather_copies, 4, "weights_ffn0_start")
    fly3 = copies_start(w3, gathered(w3), _gather_copies, 4, "weights_layer1_start")
    started = fly2[4][0, 0] + fly3[4][0, 0]
    w_gdn = jnp.concatenate([g_gdn_in[s] for s in range(N_SHARD)] + [jnp.zeros((D, GDN_PROJ - N_SHARD * gcols), BF16)], axis=1)
    alog, dtb = _pad_lanes(gdn_a_log[0]), _pad_lanes(gdn_dt_bias[0])
    qg2 = jnp.concatenate([dsw_q_norm, dsw_q_norm], axis=1)
    kg2 = jnp.concatenate([dsw_k_norm, dsw_k_norm], axis=1)
    w_gdn_out = g_gdn_out.reshape(GDN_H * LANES, D)
    gdn_args = (w_gdn, conv_w, alog, dtb, gdn_out_norm, w_gdn_out)
    sc1[0] = sc1[0] + started

    (h10,) = rowwise(f_norm_only, [_wide(x0)], [gmix[0], sc1[0], sh1[0]], [_wide_out(S, BF16)], (nt,), "l0_norm")
    y0, sv_g = gdn_forward(h10, *gdn_args)
    x1, h20 = rowwise(f_resid_norm, [_wide(x0), _wide(y0)], [g1[0], gffn[0], sc2[0], sh2[0]], [_wide_out(S, F32), _wide_out(S, BF16)], (nt,), "l0_mid")
    g_in0, g_out0 = copies_wait(*fly2[:4], _gather_copies, y0, "weights_ffn0_wait")
    w_ffn = [(g_in0, g_out0.reshape(FFN, D)), None]
    f0, sv_f0 = ffn_forward(h20, *w_ffn[0], "0")
    x2, h11 = rowwise(f_resid_norm, [_wide(x1), _wide(f0)], [g2[0], gmix[1], sc1[1], sh1[1]], [_wide_out(S, F32), _wide_out(S, BF16)], (nt,), "l1_in")
    g_dsw_in, g_dsw_out, g_in1, g_out1 = copies_wait(*fly3[:4], _gather_copies, f0, "weights_layer1_wait")
    w_ffn[1] = (g_in1, g_out1.reshape(FFN, D))
    dsw_args = (g_dsw_in, qg2, kg2)
    y1, sv_d = dsw_forward(h11, *dsw_args, rel_bias, g_dsw_out)
    x3, h21 = rowwise(f_resid_norm, [_wide(x2), _wide(y1)], [g1[1], gffn[1], sc2[1], sh2[1]], [_wide_out(S, F32), _wide_out(S, BF16)], (nt,), "l1_mid")
    f1, sv_f1 = ffn_forward(h21, *w_ffn[1], "1")
    parts, dx3, df1, dg2_1 = loss_and_grad(x3, f1, tgt, g2[1], S)
    loss = lax.psum(jnp.sum(parts), ("x", "y", "c"))

    dh21, d_win1, d_wout1 = ffn_backward(df1, sv_f1, *w_ffn[1], "1")
    (dx2, dy1), (dg1_1, dgf1, dsc2_1, dsh2_1) = rowwise_bwd(
        f_resid_norm, [_wide(x2), _wide(y1, gdtype=BF16)], [g1[1], gffn[1], sc2[1], sh2[1]], [_wide(dx3), _wide(dh21)], (nt,), "l1_mid_bwd")
    dh11, g_d = dsw_backward(dy1, sv_d, *dsw_args, g_dsw_out)
    (dx1, df0), (dg2_0, dgm1, dsc1_1, dsh1_1) = rowwise_bwd(
        f_resid_norm, [_wide(x1), _wide(f0, gdtype=BF16)], [g2[0], gmix[1], sc1[1], sh1[1]], [_wide(dx2), _wide(dh11)], (nt,), "l1_in_bwd")
    by_shard = lambda a: a.reshape(N_SHARD, a.shape[0] // N_SHARD, a.shape[1])
    landing = lambda ps: [(3,) + p.shape[1:] for p in ps]
    dws3 = [g_d["w_in"], g_d["w_out"], d_win1, by_shard(d_wout1)]
    parts3 = [a.astype(BF16) for a in dws3]
    gfly3 = copies_start(parts3, landing(parts3), _scatter_copies, 3, "grads_layer1_start")
    w_out0 = w_ffn[0][1] + gfly3[4][0, 0].astype(BF16)
    dh20, d_win0, d_wout0 = ffn_backward(df0, sv_f0, w_ffn[0][0], w_out0, "0")
    (dx0p, dy0), (dg1_0, dgf0, dsc2_0, dsh2_0) = rowwise_bwd(
        f_resid_norm, [_wide(x0), _wide(y0, gdtype=BF16)], [g1[0], gffn[0], sc2[0], sh2[0]], [_wide(dx1), _wide(dh20)], (nt,), "l0_mid_bwd")
    dws2 = [d_win0, by_shard(d_wout0)]
    parts2 = [a.astype(BF16) for a in dws2]
    gfly2 = copies_start(parts2, landing(parts2), _scatter_copies, 3, "grads_ffn0_start")
    gdn_args = gdn_args[:5] + (w_gdn_out + gfly2[4][0, 0].astype(BF16),)
    gdn_flight = []

    def start_gdn_grads(d_w_in, d_w_out):
        dws1 = [jnp.stack([d_w_in[:, s * gcols:(s + 1) * gcols] for s in range(N_SHARD)]), by_shard(d_w_out)]
        parts1 = [a.astype(BF16) for a in dws1]
        fly = copies_start(parts1, landing(parts1), _scatter_copies, 3, "grads_gdn_start")
        gdn_flight.extend([dws1, fly])
        return fly[4][0, 0]

    dh10, g_g = gdn_backward(dy0, sv_g, *gdn_args, on_weight_grads=start_gdn_grads)
    dws1, gfly1 = gdn_flight
    (grad_x,), (dgm0, dsc1_0, dsh1_0) = rowwise_bwd(f_first, [_wide(x0)], [gmix[0], sc1[0], sh1[0]], [_wide(dx0p), _wide(dh10)], (nt,), "l0_norm_bwd")

    dmod = jnp.concatenate([dsh1_0, dsc1_0, dg1_0, dsh2_0, dsc2_0, dg2_0, dsh1_1, dsc1_1, dg1_1, dsh2_1, dsc2_1, dg2_1], axis=1)
    d_rel = jnp.transpose(g_d["rel"][:, :, 0])
    fold = lambda v: v[:, :DSW_DH] + v[:, DSW_DH:]
    small = [dmod, jnp.concatenate([dgm0, dgm1], axis=1), jnp.concatenate([dgf0, dgf1], axis=1), g_g["conv"].reshape(1, -1),
             g_g["alog"], g_g["dtb"], g_g["gain"], _pad_lanes(fold(g_d["q_gain2"])[0]), _pad_lanes(fold(g_d["k_gain2"])[0]),
             d_rel.reshape(1, -1)]
    used = [v.shape[1] // LANES for v in small]
    sizes = [-(-u // 8) * 8 for u in used]
    pad8 = lambda v, u, s: jnp.concatenate([v.reshape(u, LANES), jnp.zeros((s - u, LANES), F32)], axis=0) if s > u else v.reshape(u, LANES)
    pad_rows = sum(sizes)
    sbuf = jnp.concatenate([pad8(v, u, s) for v, u, s in zip(small, used, sizes)], axis=0)
    sgot = all_gather_small(sbuf, "gather_small_grads")
    ssum = add_rows([sgot[d * pad_rows:(d + 1) * pad_rows] for d in range(N_DEV)], F32, "sum_small_grads", rt=pad_rows)
    offs = np.cumsum([0] + sizes)
    take = lambda k: ssum[offs[k]:offs[k] + used[k]].reshape(1, -1)
    grad_b_ada = take(0).reshape(2, 6 * D)
    grad_norm_mix = take(1).reshape(2, D)
    grad_norm_ffn = take(2).reshape(2, D)
    conv_full = take(3).reshape(4, -1)
    ncv = gdn_conv.shape[2]
    grad_gdn_conv = lax.dynamic_slice_in_dim(conv_full, s_me * ncv, ncv, axis=1)[None]
    grad_a_log = take(4)[:, :GDN_H]
    grad_dt_bias = take(5)[:, :GDN_H]
    grad_out_norm = take(6)
    grad_q_norm = take(7)[:, :DSW_DH]
    grad_k_norm = take(8)[:, :DSW_DH]
    grad_rel = take(9).reshape(REL_BUCKETS, 3 * GDN_H)
    dmod_all = sgot.reshape(N_DEV, pad_rows, LANES)[:, :used[0]].reshape(N_DEV, 2, 6 * D)
    dmod_mine = lax.dynamic_slice_in_dim(dmod_all, s_me * ada_cols, ada_cols, axis=2)
    dmod16 = jnp.concatenate([dmod_mine, jnp.zeros_like(dmod_mine)], axis=0)
    grad_w_ada = jnp.stack([matmul(cond16, dmod16[:, l], "tn", F32, f"ada_dw_{l}") for l in range(2)])

    got3 = copies_wait(*gfly3[:4], _scatter_copies, grad_x, "grads_layer1_wait")
    got2 = copies_wait(*gfly2[:4], _scatter_copies, grad_x, "grads_ffn0_wait")
    got1 = copies_wait(*gfly1[:4], _scatter_copies, grad_x, "grads_gdn_wait")
    core_sums = []
    for i, (full, got) in enumerate(zip(dws3 + dws2 + dws1, got3 + got2 + got1)):
        own = lax.dynamic_index_in_dim(full, s_me, 0, keepdims=False)
        core_sums.append(add_rows([own, got[0], got[1], got[2]], F32, f"grads_core_sum_{i}"))
    sib_sums = sibling_exchange(core_sums, "grads_core_sums_swap")
    s_dsw_in, s_dsw_out, s_in1, s_out1, s_in0, s_out0, s_gdn_in, s_gdn_out = [
        add_rows([a, b], F32, f"grads_chip_total_{i}") for i, (a, b) in enumerate(zip(core_sums, sib_sums))]
    gsh = dict(gdn_w_in=s_gdn_in[None], gdn_w_out=s_gdn_out[None],
               w_ffn_in=jnp.stack([s_in0, s_in1]), w_ffn_out=jnp.stack([s_out0, s_out1]),
               dsw_w_in=s_dsw_in[None], dsw_w_out=s_dsw_out[None])

    grads = dict(w_ada=grad_w_ada, b_ada=grad_b_ada, norm_mix=grad_norm_mix, norm_ffn=grad_norm_ffn, w_ffn_in=gsh["w_ffn_in"],
                 w_ffn_out=gsh["w_ffn_out"], gdn_w_in=gsh["gdn_w_in"], gdn_conv=grad_gdn_conv, gdn_a_log=grad_a_log,
                 gdn_dt_bias=grad_dt_bias, gdn_out_norm=grad_out_norm, gdn_w_out=gsh["gdn_w_out"], dsw_w_in=gsh["dsw_w_in"],
                 dsw_q_norm=grad_q_norm, dsw_k_norm=grad_k_norm, dsw_w_out=gsh["dsw_w_out"], rel_bias=grad_rel)
    weights = dict(w_ada=w_ada, b_ada=b_ada, norm_mix=norm_mix, norm_ffn=norm_ffn, w_ffn_in=w_ffn_in, w_ffn_out=w_ffn_out,
                   gdn_w_in=gdn_w_in, gdn_conv=gdn_conv, gdn_a_log=gdn_a_log, gdn_dt_bias=gdn_dt_bias, gdn_out_norm=gdn_out_norm,
                   gdn_w_out=gdn_w_out, dsw_w_in=dsw_w_in, dsw_q_norm=dsw_q_norm, dsw_k_norm=dsw_k_norm, dsw_w_out=dsw_w_out,
                   rel_bias=rel_bias)
    ms = dict(w_ada=m_w_ada, b_ada=m_b_ada, norm_mix=m_norm_mix, norm_ffn=m_norm_ffn, w_ffn_in=m_w_ffn_in, w_ffn_out=m_w_ffn_out,
              gdn_w_in=m_gdn_w_in, gdn_conv=m_gdn_conv, gdn_a_log=m_gdn_a_log, gdn_dt_bias=m_gdn_dt_bias, gdn_out_norm=m_gdn_out_norm,
              gdn_w_out=m_gdn_w_out, dsw_w_in=m_dsw_w_in, dsw_q_norm=m_dsw_q_norm, dsw_k_norm=m_dsw_k_norm, dsw_w_out=m_dsw_w_out,
              rel_bias=m_rel_bias)
    vs = dict(w_ada=v_w_ada, b_ada=v_b_ada, norm_mix=v_norm_mix, norm_ffn=v_norm_ffn, w_ffn_in=v_w_ffn_in, w_ffn_out=v_w_ffn_out,
              gdn_w_in=v_gdn_w_in, gdn_conv=v_gdn_conv, gdn_a_log=v_gdn_a_log, gdn_dt_bias=v_gdn_dt_bias, gdn_out_norm=v_gdn_out_norm,
              gdn_w_out=v_gdn_w_out, dsw_w_in=v_dsw_w_in, dsw_q_norm=v_dsw_q_norm, dsw_k_norm=v_dsw_k_norm, dsw_w_out=v_dsw_w_out,
              rel_bias=v_rel_bias)
    names = list(weights)
    deltas, new_m, new_v = [], [], []
    for n in names:
        g = grads[n].reshape(weights[n].shape)
        grads[n] = g
        d, nm, nv = adamw(weights[n], g, ms[n], vs[n], f"adamw_{n}")
        deltas.append(d)
        new_m.append(nm)
        new_v.append(nv)
    return (loss, grad_x[None], *[grads[n] for n in names], *deltas, *new_m, *new_v)
```
